```python
import math
import jax, jax.numpy as jnp
from jax import lax
import numpy as np

D_MODEL = 1024
BATCH = 8
SEQ = 8192
DEPTH = 1

N_Q_HEADS = 8
N_KV_HEADS = 2
HEAD_DIM = 64
WINDOW = 128
ATTN_BLOCK = WINDOW
ROPE_THETA = 10000.0
MLSTM_HEADS = 4
MLSTM_HEAD_DIM = 128
MLSTM_CHUNK = 64
CONV_WIDTH = 4
D_FF = -(-8 * D_MODEL // (3 * 256)) * 256
NORM_EPS = 1e-6

ATTN_Q_WIDTH = N_Q_HEADS * HEAD_DIM
ATTN_KV_WIDTH = N_KV_HEADS * HEAD_DIM
MLSTM_WIDTH = MLSTM_HEADS * MLSTM_HEAD_DIM
IN_SPLIT_SIZES = (ATTN_Q_WIDTH, ATTN_KV_WIDTH, ATTN_KV_WIDTH,
                  MLSTM_WIDTH, MLSTM_WIDTH, MLSTM_WIDTH, MLSTM_WIDTH,
                  MLSTM_HEADS, MLSTM_HEADS, D_MODEL, D_MODEL)
IN_WIDTH = sum(IN_SPLIT_SIZES)

kernel_name = 'hybrid_swa_sink_mlstm_gated_block'


def rms_norm(x, g):
    xf = x.astype(jnp.float32)
    y = xf * lax.rsqrt(jnp.mean(xf * xf, axis=-1, keepdims=True) + NORM_EPS)
    return (y * g.astype(jnp.float32)).astype(x.dtype)


def modulate(h, shift, scale):
    return h * (1 + scale[:, None, :]) + shift[:, None, :]


def rope(t, positions):
    half = HEAD_DIM // 2
    inv_freq = ROPE_THETA ** (-2.0 * jnp.arange(half, dtype=jnp.float32) / HEAD_DIM)
    ang = positions.astype(jnp.float32)[..., None] * inv_freq
    cos = jnp.cos(ang)[:, :, None, :]
    sin = jnp.sin(ang)[:, :, None, :]
    tf = t.astype(jnp.float32)
    t1, t2 = tf[..., :half], tf[..., half:]
    return jnp.concatenate([t1 * cos - t2 * sin, t2 * cos + t1 * sin], axis=-1).astype(t.dtype)


def causal_depthwise_conv(u, w, b):
    S = u.shape[1]
    up = jnp.pad(u, ((0, 0), (CONV_WIDTH - 1, 0), (0, 0)))
    out = b
    for j in range(CONV_WIDTH):
        out = out + up[:, j:j + S] * w[j]
    return out


def sliding_window_attention(q, k, v, sinks):
    B, S = q.shape[0], q.shape[1]
    nb = S // ATTN_BLOCK
    g = N_Q_HEADS // N_KV_HEADS
    qb = q.astype(jnp.float32).reshape(B, nb, ATTN_BLOCK, N_KV_HEADS, g, HEAD_DIM) * (HEAD_DIM ** -0.5)

    def band(t):
        t = t.astype(jnp.float32).reshape(B, nb, ATTN_BLOCK, N_KV_HEADS, HEAD_DIM)
        prev = jnp.pad(t, ((0, 0), (1, 0), (0, 0), (0, 0), (0, 0)))[:, :-1]
        return jnp.concatenate([prev, t], axis=2)

    kb, vb = band(k), band(v)
    s = jnp.einsum('bnqhgd,bnkhd->bnhgqk', qb, kb)
    qi = jnp.arange(ATTN_BLOCK)[:, None]
    kj = jnp.arange(2 * ATTN_BLOCK)[None, :]
    rel = kj - ATTN_BLOCK
    in_band = (rel <= qi) & (qi - rel < WINDOW)
    blk = jnp.arange(nb)[:, None, None]
    mask = in_band[None] & ((blk > 0) | (kj[None] >= ATTN_BLOCK))
    s = jnp.where(mask[None, :, None, None], s, -jnp.inf)
    sink = sinks.astype(jnp.float32).reshape(N_KV_HEADS, g)[None, None, :, :, None, None]
    m = jnp.maximum(jnp.max(s, axis=-1, keepdims=True), sink)
    p = jnp.exp(s - m)
    p = p / (jnp.sum(p, axis=-1, keepdims=True) + jnp.exp(sink - m))
    o = jnp.einsum('bnhgqk,bnkhd->bnqhgd', p, vb)
    return o.reshape(B, S, N_Q_HEADS * HEAD_DIM)


def mlstm_chunkwise(q, k, v, i_pre, f_pre):
    B, S, H, D = q.shape
    L = MLSTM_CHUNK
    nc = S // L

    def chunks(t):
        return t.astype(jnp.float32).reshape(B, nc, L, H, D).transpose(0, 3, 1, 2, 4)

    qc = chunks(q)
    kc = chunks(k) * (D ** -0.5)
    vc = chunks(v)
    ig = i_pre.astype(jnp.float32).reshape(B, nc, L, H).transpose(0, 3, 1, 2)
    logf = jax.nn.log_sigmoid(f_pre.astype(jnp.float32)).reshape(B, nc, L, H).transpose(0, 3, 1, 2)
    b = jnp.cumsum(logf, axis=-1)
    b_last = b[..., -1]
    a = b_last[..., None] - b + ig

    def step(carry, inp):
        C, n, m = carry
        k_c, v_c, a_c, bl_c = inp
        m_new = jnp.maximum(bl_c + m, jnp.max(a_c, axis=-1))
        decay = jnp.exp(bl_c + m - m_new)
        kw = k_c * jnp.exp(a_c - m_new[..., None])[..., None]
        C_new = decay[..., None, None] * C + jnp.einsum('bhld,bhle->bhde', kw, v_c)
        n_new = decay[..., None] * n + jnp.sum(kw, axis=-2)
        return (C_new, n_new, m_new), (C, n, m)

    init = (jnp.zeros((B, H, D, D), jnp.float32), jnp.zeros((B, H, D), jnp.float32),
            jnp.zeros((B, H), jnp.float32))
    xs = (kc.transpose(2, 0, 1, 3, 4), vc.transpose(2, 0, 1, 3, 4),
          a.transpose(2, 0, 1, 3), b_last.transpose(2, 0, 1))
    _, (C_prev, n_prev, m_prev) = lax.scan(step, init, xs)
    C_prev = C_prev.transpose(1, 2, 0, 3, 4)
    n_prev = n_prev.transpose(1, 2, 0, 3)
    m_prev = m_prev.transpose(1, 2, 0)

    causal = jnp.tril(jnp.ones((L, L), dtype=bool))
    d_mat = jnp.where(causal, b[..., :, None] - b[..., None, :] + ig[..., None, :], -jnp.inf)
    inter = b + m_prev[..., None]
    m_t = jnp.maximum(inter, jnp.max(d_mat, axis=-1))
    w_intra = jnp.exp(d_mat - m_t[..., None])
    w_inter = jnp.exp(inter - m_t)
    scores = jnp.einsum('bhctd,bhcsd->bhcts', qc, kc) * w_intra
    num = jnp.einsum('bhcts,bhcsd->bhctd', scores, vc) + \
        w_inter[..., None] * jnp.einsum('bhctd,bhcde->bhcte', qc, C_prev)
    den = jnp.sum(scores, axis=-1) + w_inter * jnp.einsum('bhctd,bhcd->bhct', qc, n_prev)
    h = num / jnp.maximum(jnp.abs(den), jnp.exp(-m_t))[..., None]
    return h.transpose(0, 2, 3, 1, 4).reshape(B, S, H, D)


def head_layer_norm(h, w):
    mu = jnp.mean(h, axis=-1, keepdims=True)
    var = jnp.mean(jnp.square(h - mu), axis=-1, keepdims=True)
    y = (h - mu) * lax.rsqrt(var + NORM_EPS)
    return y.reshape(h.shape[0], h.shape[1], -1) * w.astype(jnp.float32)


def _fwd_setup_inputs(seed: int = 0) -> dict:
    key = jax.random.key(seed)
    ks = jax.random.split(key, 24)
    f32 = jnp.float32

    def dense(k, shape, fan_in, scale=1.0):
        return jax.random.normal(k, shape, f32) * (scale * fan_in ** -0.5)

    def gain(k, width):
        return 1.0 + 0.05 * jax.random.normal(k, (DEPTH, width), f32)

    x = jax.random.normal(ks[0], (BATCH, SEQ, D_MODEL), f32)
    c = jax.random.normal(ks[1], (BATCH, D_MODEL), f32)
    offsets = jax.random.randint(ks[2], (BATCH, 1), 0, 4096, dtype=jnp.int32)
    positions = (offsets + jnp.arange(SEQ, dtype=jnp.int32)[None, :]).astype(jnp.int32)
    w_ada = dense(ks[3], (DEPTH, D_MODEL, 6 * D_MODEL), D_MODEL, 0.5)
    b_ada = 0.02 * jax.random.normal(ks[4], (DEPTH, 6 * D_MODEL), f32)
    g_pre_mix = gain(ks[5], D_MODEL)
    g_post_mix = gain(ks[6], D_MODEL)
    w_in = dense(ks[7], (DEPTH, D_MODEL, IN_WIDTH), D_MODEL)
    b_i = 0.1 * jax.random.normal(ks[8], (DEPTH, MLSTM_HEADS), f32)
    b_f = jnp.linspace(3.0, 6.0, MLSTM_HEADS, dtype=f32)[None, :] + \
        0.1 * jax.random.normal(ks[9], (DEPTH, MLSTM_HEADS), f32)
    b_if = jnp.concatenate([b_i, b_f], axis=-1)
    conv_w = dense(ks[10], (DEPTH, CONV_WIDTH, 2 * MLSTM_WIDTH), CONV_WIDTH)
    conv_b = 0.02 * jax.random.normal(ks[11], (DEPTH, 2 * MLSTM_WIDTH), f32)
    attn_sinks = 0.5 * jax.random.normal(ks[12], (DEPTH, N_Q_HEADS), f32)
    mlstm_norm_w = gain(ks[13], MLSTM_WIDTH)
    w_branch_attn = dense(ks[14], (DEPTH, ATTN_Q_WIDTH, D_MODEL), ATTN_Q_WIDTH)
    w_branch_mlstm = dense(ks[15], (DEPTH, MLSTM_WIDTH, D_MODEL), MLSTM_WIDTH)
    w_out = dense(ks[16], (DEPTH, D_MODEL, D_MODEL), D_MODEL)
    g_pre_ffn = gain(ks[17], D_MODEL)
    g_post_ffn = gain(ks[18], D_MODEL)
    w_ffn_gate = dense(ks[19], (DEPTH, D_MODEL, D_FF), D_MODEL)
    w_ffn_up = dense(ks[20], (DEPTH, D_MODEL, D_FF), D_MODEL)
    w_ffn_down = dense(ks[21], (DEPTH, D_FF, D_MODEL), D_FF)
    return {'x': x, 'c': c, 'positions': positions, 'w_ada': w_ada, 'b_ada': b_ada,
            'g_pre_mix': g_pre_mix, 'g_post_mix': g_post_mix, 'w_in': w_in, 'b_if': b_if,
            'conv_w': conv_w, 'conv_b': conv_b, 'attn_sinks': attn_sinks,
            'mlstm_norm_w': mlstm_norm_w, 'w_branch_attn': w_branch_attn,
            'w_branch_mlstm': w_branch_mlstm, 'w_out': w_out, 'g_pre_ffn': g_pre_ffn,
            'g_post_ffn': g_post_ffn, 'w_ffn_gate': w_ffn_gate, 'w_ffn_up': w_ffn_up,
            'w_ffn_down': w_ffn_down}


def _fwd_reference(x, c, positions, w_ada, b_ada, g_pre_mix, g_post_mix, w_in, b_if, conv_w, conv_b,
              attn_sinks, mlstm_norm_w, w_branch_attn, w_branch_mlstm, w_out, g_pre_ffn,
              g_post_ffn, w_ffn_gate, w_ffn_up, w_ffn_down):
    B, S, _ = x.shape
    split_points = np.cumsum(IN_SPLIT_SIZES)[:-1].tolist()
    for l in range(DEPTH):
        mod = c @ w_ada[l] + b_ada[l]
        shift_m, scale_m, gate_m, shift_f, scale_f, gate_f = jnp.split(mod, 6, axis=-1)

        h = modulate(rms_norm(x, g_pre_mix[l]), shift_m, scale_m)
        proj = h @ w_in[l]
        q_a, k_a, v_a, q_m, k_m, v_m, o_m, i_m, f_m, g_a, g_m = jnp.split(proj, split_points, axis=-1)

        q_a = rope(q_a.reshape(B, S, N_Q_HEADS, HEAD_DIM), positions)
        k_a = rope(k_a.reshape(B, S, N_KV_HEADS, HEAD_DIM), positions)
        v_a = v_a.reshape(B, S, N_KV_HEADS, HEAD_DIM)
        y_a = sliding_window_attention(q_a, k_a, v_a, attn_sinks[l]).astype(x.dtype)

        qk_m = jax.nn.silu(causal_depthwise_conv(jnp.concatenate([q_m, k_m], axis=-1), conv_w[l], conv_b[l]))
        q_m, k_m = jnp.split(qk_m, 2, axis=-1)
        i_pre = i_m + b_if[l][:MLSTM_HEADS]
        f_pre = f_m + b_if[l][MLSTM_HEADS:]
        h_m = mlstm_chunkwise(q_m.reshape(B, S, MLSTM_HEADS, MLSTM_HEAD_DIM),
                              k_m.reshape(B, S, MLSTM_HEADS, MLSTM_HEAD_DIM),
                              v_m.reshape(B, S, MLSTM_HEADS, MLSTM_HEAD_DIM), i_pre, f_pre)
        y_m = (jax.nn.sigmoid(o_m.astype(jnp.float32)) * head_layer_norm(h_m, mlstm_norm_w[l])).astype(x.dtype)

        merged = jax.nn.sigmoid(g_a) * (y_a @ w_branch_attn[l]) + jax.nn.sigmoid(g_m) * (y_m @ w_branch_mlstm[l])
        mix = merged @ w_out[l]
        x = x + gate_m[:, None, :] * rms_norm(mix, g_post_mix[l])

        h2 = modulate(rms_norm(x, g_pre_ffn[l]), shift_f, scale_f)
        ff = (jax.nn.silu(h2 @ w_ffn_gate[l]) * (h2 @ w_ffn_up[l])) @ w_ffn_down[l]
        x = x + gate_f[:, None, :] * rms_norm(ff, g_post_ffn[l])
    return x


import jax as _jax
import jax.numpy as _jnp

TWIN_FORMAT = 'train_step'
FWD_PARAMS = ['x', 'c', 'positions', 'w_ada', 'b_ada', 'g_pre_mix', 'g_post_mix', 'w_in', 'b_if', 'conv_w', 'conv_b', 'attn_sinks', 'mlstm_norm_w', 'w_branch_attn', 'w_branch_mlstm', 'w_out', 'g_pre_ffn', 'g_post_ffn', 'w_ffn_gate', 'w_ffn_up', 'w_ffn_down']
TWIN_WEIGHTS = ['w_ada', 'b_ada', 'g_pre_mix', 'g_post_mix', 'w_in', 'b_if', 'conv_w', 'conv_b', 'attn_sinks', 'mlstm_norm_w', 'w_branch_attn', 'w_branch_mlstm', 'w_out', 'g_pre_ffn', 'g_post_ffn', 'w_ffn_gate', 'w_ffn_up', 'w_ffn_down']
TWIN_DIFF_INPUT = 'x'
TWIN_INPUTS = ['x', 'c', 'positions', 'w_ada', 'b_ada', 'g_pre_mix', 'g_post_mix', 'w_in', 'b_if', 'conv_w', 'conv_b', 'attn_sinks', 'mlstm_norm_w', 'w_branch_attn', 'w_branch_mlstm', 'w_out', 'g_pre_ffn', 'g_post_ffn', 'w_ffn_gate', 'w_ffn_up', 'w_ffn_down', 'loss_target', 'm_w_ada', 'm_b_ada', 'm_g_pre_mix', 'm_g_post_mix', 'm_w_in', 'm_b_if', 'm_conv_w', 'm_conv_b', 'm_attn_sinks', 'm_mlstm_norm_w', 'm_w_branch_attn', 'm_w_branch_mlstm', 'm_w_out', 'm_g_pre_ffn', 'm_g_post_ffn', 'm_w_ffn_gate', 'm_w_ffn_up', 'm_w_ffn_down', 'v_w_ada', 'v_b_ada', 'v_g_pre_mix', 'v_g_post_mix', 'v_w_in', 'v_b_if', 'v_conv_w', 'v_conv_b', 'v_attn_sinks', 'v_mlstm_norm_w', 'v_w_branch_attn', 'v_w_branch_mlstm', 'v_w_out', 'v_g_pre_ffn', 'v_g_post_ffn', 'v_w_ffn_gate', 'v_w_ffn_up', 'v_w_ffn_down']
TWIN_OUTPUTS = ['loss', 'grad_x', 'grad_w_ada', 'grad_b_ada', 'grad_g_pre_mix', 'grad_g_post_mix', 'grad_w_in', 'grad_b_if', 'grad_conv_w', 'grad_conv_b', 'grad_attn_sinks', 'grad_mlstm_norm_w', 'grad_w_branch_attn', 'grad_w_branch_mlstm', 'grad_w_out', 'grad_g_pre_ffn', 'grad_g_post_ffn', 'grad_w_ffn_gate', 'grad_w_ffn_up', 'grad_w_ffn_down', 'delta_w_ada', 'delta_b_ada', 'delta_g_pre_mix', 'delta_g_post_mix', 'delta_w_in', 'delta_b_if', 'delta_conv_w', 'delta_conv_b', 'delta_attn_sinks', 'delta_mlstm_norm_w', 'delta_w_branch_attn', 'delta_w_branch_mlstm', 'delta_w_out', 'delta_g_pre_ffn', 'delta_g_post_ffn', 'delta_w_ffn_gate', 'delta_w_ffn_up', 'delta_w_ffn_down', 'new_m_w_ada', 'new_m_b_ada', 'new_m_g_pre_mix', 'new_m_g_post_mix', 'new_m_w_in', 'new_m_b_if', 'new_m_conv_w', 'new_m_conv_b', 'new_m_attn_sinks', 'new_m_mlstm_norm_w', 'new_m_w_branch_attn', 'new_m_w_branch_mlstm', 'new_m_w_out', 'new_m_g_pre_ffn', 'new_m_g_post_ffn', 'new_m_w_ffn_gate', 'new_m_w_ffn_up', 'new_m_w_ffn_down', 'new_v_w_ada', 'new_v_b_ada', 'new_v_g_pre_mix', 'new_v_g_post_mix', 'new_v_w_in', 'new_v_b_if', 'new_v_conv_w', 'new_v_conv_b', 'new_v_attn_sinks', 'new_v_mlstm_norm_w', 'new_v_w_branch_attn', 'new_v_w_branch_mlstm', 'new_v_w_out', 'new_v_g_pre_ffn', 'new_v_g_post_ffn', 'new_v_w_ffn_gate', 'new_v_w_ffn_up', 'new_v_w_ffn_down']
TWIN_LEAF_KINDS = {'loss': 'loss', 'grad_x': 'grad_x', 'grad_w_ada': 'grad_w', 'grad_b_ada': 'grad_w', 'grad_g_pre_mix': 'grad_w', 'grad_g_post_mix': 'grad_w', 'grad_w_in': 'grad_w', 'grad_b_if': 'grad_w', 'grad_conv_w': 'grad_w', 'grad_conv_b': 'grad_w', 'grad_attn_sinks': 'grad_w', 'grad_mlstm_norm_w': 'grad_w', 'grad_w_branch_attn': 'grad_w', 'grad_w_branch_mlstm': 'grad_w', 'grad_w_out': 'grad_w', 'grad_g_pre_ffn': 'grad_w', 'grad_g_post_ffn': 'grad_w', 'grad_w_ffn_gate': 'grad_w', 'grad_w_ffn_up': 'grad_w', 'grad_w_ffn_down': 'grad_w', 'delta_w_ada': 'delta_w', 'delta_b_ada': 'delta_w', 'delta_g_pre_mix': 'delta_w', 'delta_g_post_mix': 'delta_w', 'delta_w_in': 'delta_w', 'delta_b_if': 'delta_w', 'delta_conv_w': 'delta_w', 'delta_conv_b': 'delta_w', 'delta_attn_sinks': 'delta_w', 'delta_mlstm_norm_w': 'delta_w', 'delta_w_branch_attn': 'delta_w', 'delta_w_branch_mlstm': 'delta_w', 'delta_w_out': 'delta_w', 'delta_g_pre_ffn': 'delta_w', 'delta_g_post_ffn': 'delta_w', 'delta_w_ffn_gate': 'delta_w', 'delta_w_ffn_up': 'delta_w', 'delta_w_ffn_down': 'delta_w', 'new_m_w_ada': 'new_m', 'new_m_b_ada': 'new_m', 'new_m_g_pre_mix': 'new_m', 'new_m_g_post_mix': 'new_m', 'new_m_w_in': 'new_m', 'new_m_b_if': 'new_m', 'new_m_conv_w': 'new_m', 'new_m_conv_b': 'new_m', 'new_m_attn_sinks': 'new_m', 'new_m_mlstm_norm_w': 'new_m', 'new_m_w_branch_attn': 'new_m', 'new_m_w_branch_mlstm': 'new_m', 'new_m_w_out': 'new_m', 'new_m_g_pre_ffn': 'new_m', 'new_m_g_post_ffn': 'new_m', 'new_m_w_ffn_gate': 'new_m', 'new_m_w_ffn_up': 'new_m', 'new_m_w_ffn_down': 'new_m', 'new_v_w_ada': 'new_v', 'new_v_b_ada': 'new_v', 'new_v_g_pre_mix': 'new_v', 'new_v_g_post_mix': 'new_v', 'new_v_w_in': 'new_v', 'new_v_b_if': 'new_v', 'new_v_conv_w': 'new_v', 'new_v_conv_b': 'new_v', 'new_v_attn_sinks': 'new_v', 'new_v_mlstm_norm_w': 'new_v', 'new_v_w_branch_attn': 'new_v', 'new_v_w_branch_mlstm': 'new_v', 'new_v_w_out': 'new_v', 'new_v_g_pre_ffn': 'new_v', 'new_v_g_post_ffn': 'new_v', 'new_v_w_ffn_gate': 'new_v', 'new_v_w_ffn_up': 'new_v', 'new_v_w_ffn_down': 'new_v'}


def _forward(args):
    return _fwd_reference(*[args[k] for k in FWD_PARAMS])


def _output_shape():
    def fwd():
        inp = _fwd_setup_inputs(0)
        return _fwd_reference(*[inp[k] for k in FWD_PARAMS])
    out = _jax.eval_shape(fwd)
    return out.shape, out.dtype

N_MICROBATCH = 1
ADAM_LR = 0.001
ADAM_B1 = 0.9
ADAM_B2 = 0.999
ADAM_EPS = 1e-08
ADAM_WD = 0.01
ADAM_STEP = 10
PER_EXAMPLE_BATCH_AXIS = {'x': 0, 'c': 0, 'positions': 0, 'loss_target': 0}
SHARED_INPUTS = []
_WEIGHT_DTYPES = {'w_ada': _jnp.float32, 'b_ada': _jnp.float32, 'g_pre_mix': _jnp.float32, 'g_post_mix': _jnp.float32, 'w_in': _jnp.float32, 'b_if': _jnp.float32, 'conv_w': _jnp.float32, 'conv_b': _jnp.float32, 'attn_sinks': _jnp.float32, 'mlstm_norm_w': _jnp.float32, 'w_branch_attn': _jnp.float32, 'w_branch_mlstm': _jnp.float32, 'w_out': _jnp.float32, 'g_pre_ffn': _jnp.float32, 'g_post_ffn': _jnp.float32, 'w_ffn_gate': _jnp.float32, 'w_ffn_up': _jnp.float32, 'w_ffn_down': _jnp.float32}
MOMENT_SCALE = {'w_ada': 9.373172e+00, 'b_ada': 9.200925e+00, 'g_pre_mix': 5.956527e-01, 'g_post_mix': 2.043120e+01, 'w_in': 2.487690e+00, 'b_if': 4.498281e+00, 'conv_w': 1.221603e-01, 'conv_b': 1.265877e-01, 'attn_sinks': 1.364482e-01, 'mlstm_norm_w': 4.684173e+00, 'w_branch_attn': 3.921988e+00, 'w_branch_mlstm': 3.439380e+00, 'w_out': 5.117246e+00, 'g_pre_ffn': 1.018654e+00, 'g_post_ffn': 1.784223e+01, 'w_ffn_gate': 6.113751e-01, 'w_ffn_up': 8.262339e-01, 'w_ffn_down': 1.401896e+00}


def _to_microbatches(a, axis):
    t = _jnp.moveaxis(a, axis, 0)
    t = t.reshape((N_MICROBATCH, t.shape[0] // N_MICROBATCH) + t.shape[1:])
    return _jnp.moveaxis(t, 1, axis + 1)


def setup_inputs(seed: int = 0) -> dict:
    inp = _fwd_setup_inputs(seed)
    key = _jax.random.fold_in(_jax.random.key(seed), 7919)
    shape, _ = _output_shape()
    out = dict(inp)
    out["loss_target"] = _jax.random.normal(_jax.random.fold_in(key, 0), shape, _jnp.float32)
    for i, name in enumerate(TWIN_WEIGHTS):
        w = inp[name].astype(_jnp.float32)
        if MOMENT_SCALE is None:
            s = _jnp.sqrt(_jnp.mean(_jnp.square(w)) + 1e-30)
        else:
            s = MOMENT_SCALE[name]
        km, kv = _jax.random.split(_jax.random.fold_in(key, i + 1))
        out[name] = w
        out["m_" + name] = s * _jax.random.normal(km, w.shape, _jnp.float32)
        out["v_" + name] = (s * s) * _jax.random.uniform(kv, w.shape, _jnp.float32, 0.5, 1.5)
    if N_MICROBATCH > 1:
        for name, axis in PER_EXAMPLE_BATCH_AXIS.items():
            out[name] = _to_microbatches(out[name], axis)
    return {'x': out['x'], 'c': out['c'], 'positions': out['positions'], 'w_ada': out['w_ada'], 'b_ada': out['b_ada'], 'g_pre_mix': out['g_pre_mix'], 'g_post_mix': out['g_post_mix'], 'w_in': out['w_in'], 'b_if': out['b_if'], 'conv_w': out['conv_w'], 'conv_b': out['conv_b'], 'attn_sinks': out['attn_sinks'], 'mlstm_norm_w': out['mlstm_norm_w'], 'w_branch_attn': out['w_branch_attn'], 'w_branch_mlstm': out['w_branch_mlstm'], 'w_out': out['w_out'], 'g_pre_ffn': out['g_pre_ffn'], 'g_post_ffn': out['g_post_ffn'], 'w_ffn_gate': out['w_ffn_gate'], 'w_ffn_up': out['w_ffn_up'], 'w_ffn_down': out['w_ffn_down'], 'loss_target': out['loss_target'], 'm_w_ada': out['m_w_ada'], 'm_b_ada': out['m_b_ada'], 'm_g_pre_mix': out['m_g_pre_mix'], 'm_g_post_mix': out['m_g_post_mix'], 'm_w_in': out['m_w_in'], 'm_b_if': out['m_b_if'], 'm_conv_w': out['m_conv_w'], 'm_conv_b': out['m_conv_b'], 'm_attn_sinks': out['m_attn_sinks'], 'm_mlstm_norm_w': out['m_mlstm_norm_w'], 'm_w_branch_attn': out['m_w_branch_attn'], 'm_w_branch_mlstm': out['m_w_branch_mlstm'], 'm_w_out': out['m_w_out'], 'm_g_pre_ffn': out['m_g_pre_ffn'], 'm_g_post_ffn': out['m_g_post_ffn'], 'm_w_ffn_gate': out['m_w_ffn_gate'], 'm_w_ffn_up': out['m_w_ffn_up'], 'm_w_ffn_down': out['m_w_ffn_down'], 'v_w_ada': out['v_w_ada'], 'v_b_ada': out['v_b_ada'], 'v_g_pre_mix': out['v_g_pre_mix'], 'v_g_post_mix': out['v_g_post_mix'], 'v_w_in': out['v_w_in'], 'v_b_if': out['v_b_if'], 'v_conv_w': out['v_conv_w'], 'v_conv_b': out['v_conv_b'], 'v_attn_sinks': out['v_attn_sinks'], 'v_mlstm_norm_w': out['v_mlstm_norm_w'], 'v_w_branch_attn': out['v_w_branch_attn'], 'v_w_branch_mlstm': out['v_w_branch_mlstm'], 'v_w_out': out['v_w_out'], 'v_g_pre_ffn': out['v_g_pre_ffn'], 'v_g_post_ffn': out['v_g_post_ffn'], 'v_w_ffn_gate': out['v_w_ffn_gate'], 'v_w_ffn_up': out['v_w_ffn_up'], 'v_w_ffn_down': out['v_w_ffn_down']}


def _loss(weights, diff, rest, loss_target):
    with _jax.named_scope("forward"):
        args = {**rest, TWIN_DIFF_INPUT: diff, **{k: w.astype(_WEIGHT_DTYPES[k]) for k, w in weights.items()}}
        y = _forward(args)
    with _jax.named_scope("loss_head"):
        err = _jnp.square(y.astype(_jnp.float32) - loss_target)
        return 0.5 * _jnp.sum(_jnp.mean(err, axis=-1)) if err.ndim else 0.5 * err


def _adamw(w, g, m, v):
    m = ADAM_B1 * m + (1.0 - ADAM_B1) * g
    v = ADAM_B2 * v + (1.0 - ADAM_B2) * _jnp.square(g)
    m_hat = m / (1.0 - ADAM_B1 ** ADAM_STEP)
    v_hat = v / (1.0 - ADAM_B2 ** ADAM_STEP)
    delta = -ADAM_LR * (m_hat / (_jnp.sqrt(v_hat) + ADAM_EPS) + ADAM_WD * w)
    return delta, m, v


def reference(x, c, positions, w_ada, b_ada, g_pre_mix, g_post_mix, w_in, b_if, conv_w, conv_b, attn_sinks, mlstm_norm_w, w_branch_attn, w_branch_mlstm, w_out, g_pre_ffn, g_post_ffn, w_ffn_gate, w_ffn_up, w_ffn_down, loss_target, m_w_ada, m_b_ada, m_g_pre_mix, m_g_post_mix, m_w_in, m_b_if, m_conv_w, m_conv_b, m_attn_sinks, m_mlstm_norm_w, m_w_branch_attn, m_w_branch_mlstm, m_w_out, m_g_pre_ffn, m_g_post_ffn, m_w_ffn_gate, m_w_ffn_up, m_w_ffn_down, v_w_ada, v_b_ada, v_g_pre_mix, v_g_post_mix, v_w_in, v_b_if, v_conv_w, v_conv_b, v_attn_sinks, v_mlstm_norm_w, v_w_branch_attn, v_w_branch_mlstm, v_w_out, v_g_pre_ffn, v_g_post_ffn, v_w_ffn_gate, v_w_ffn_up, v_w_ffn_down):
    given = dict(x=x, c=c, positions=positions, w_ada=w_ada, b_ada=b_ada, g_pre_mix=g_pre_mix, g_post_mix=g_post_mix, w_in=w_in, b_if=b_if, conv_w=conv_w, conv_b=conv_b, attn_sinks=attn_sinks, mlstm_norm_w=mlstm_norm_w, w_branch_attn=w_branch_attn, w_branch_mlstm=w_branch_mlstm, w_out=w_out, g_pre_ffn=g_pre_ffn, g_post_ffn=g_post_ffn, w_ffn_gate=w_ffn_gate, w_ffn_up=w_ffn_up, w_ffn_down=w_ffn_down, loss_target=loss_target, m_w_ada=m_w_ada, m_b_ada=m_b_ada, m_g_pre_mix=m_g_pre_mix, m_g_post_mix=m_g_post_mix, m_w_in=m_w_in, m_b_if=m_b_if, m_conv_w=m_conv_w, m_conv_b=m_conv_b, m_attn_sinks=m_attn_sinks, m_mlstm_norm_w=m_mlstm_norm_w, m_w_branch_attn=m_w_branch_attn, m_w_branch_mlstm=m_w_branch_mlstm, m_w_out=m_w_out, m_g_pre_ffn=m_g_pre_ffn, m_g_post_ffn=m_g_post_ffn, m_w_ffn_gate=m_w_ffn_gate, m_w_ffn_up=m_w_ffn_up, m_w_ffn_down=m_w_ffn_down, v_w_ada=v_w_ada, v_b_ada=v_b_ada, v_g_pre_mix=v_g_pre_mix, v_g_post_mix=v_g_post_mix, v_w_in=v_w_in, v_b_if=v_b_if, v_conv_w=v_conv_w, v_conv_b=v_conv_b, v_attn_sinks=v_attn_sinks, v_mlstm_norm_w=v_mlstm_norm_w, v_w_branch_attn=v_w_branch_attn, v_w_branch_mlstm=v_w_branch_mlstm, v_w_out=v_w_out, v_g_pre_ffn=v_g_pre_ffn, v_g_post_ffn=v_g_post_ffn, v_w_ffn_gate=v_w_ffn_gate, v_w_ffn_up=v_w_ffn_up, v_w_ffn_down=v_w_ffn_down)
    weights = {n: given[n] for n in TWIN_WEIGHTS}
    shared = {n: given[n] for n in SHARED_INPUTS}
    per_example = {n: given[n] for n in ['x', 'c', 'positions']}
    grad_fn = _jax.value_and_grad(_loss, argnums=(0, 1))

    def one_microbatch(ex, loss_target):
        ex = dict(ex)
        diff = ex.pop(TWIN_DIFF_INPUT)
        return grad_fn(weights, diff, {**shared, **ex}, loss_target)

    if N_MICROBATCH == 1:
        loss, (grad_w, grad_x) = one_microbatch(per_example, given["loss_target"])
    else:
        def body(carry, xs):
            loss_sum, grad_sum = carry
            l_k, (gw_k, gx_k) = one_microbatch(xs[0], xs[1])
            with _jax.named_scope("update"):
                return (loss_sum + l_k, _jax.tree.map(_jnp.add, grad_sum, gw_k)), gx_k

        init = (_jnp.zeros((), _jnp.float32), _jax.tree.map(_jnp.zeros_like, weights))
        (loss, grad_w), grad_x = _jax.lax.scan(body, init, (per_example, given["loss_target"]))
    with _jax.named_scope("update"):
        delta_w, new_m, new_v = {}, {}, {}
        for n in TWIN_WEIGHTS:
            delta_w[n], new_m[n], new_v[n] = _adamw(weights[n], grad_w[n], given["m_" + n], given["v_" + n])
    return (loss, grad_x, *[grad_w[n] for n in TWIN_WEIGHTS], *[delta_w[n] for n in TWIN_WEIGHTS],
            *[new_m[n] for n in TWIN_WEIGHTS], *[new_v[n] for n in TWIN_WEIGHTS])
```

```python
import functools

import jax
import jax.numpy as jnp
import numpy as np
from jax import lax
from jax.experimental import pallas as pl
from jax.experimental.pallas import tpu as pltpu

F32 = jnp.float32
BF16 = jnp.bfloat16

N_DEV = 8
D_MODEL = 1024
D_FF = 2816
N_Q_HEADS = 8
HEAD_DIM = 64
ATTN_BLOCK = 128
ROPE_THETA = 10000.0
MLSTM_HEADS = 4
MLSTM_HEAD_DIM = 128
MLSTM_CHUNK = 128
NORM_EPS = 1e-6
ADAM_LR = 0.001
ADAM_B1 = 0.9
ADAM_B2 = 0.999
ADAM_EPS = 1e-08
ADAM_WD = 0.01
ADAM_STEP = 10

ROW_TILE = 256
LANES = 128
NEG = -1e30
VMEM_LIMIT = 56 * 1024 * 1024

A_W = 768
M_W = 2048
IF_W = 128
G_W = 2048
CAT_W = A_W + M_W + IF_W + G_W

R_SHIFT_M, R_SCALE_M, R_GATE_M, R_SHIFT_F, R_SCALE_F, R_GATE_F = 0, 1, 2, 3, 4, 5
R_G_PRE_MIX, R_G_POST_MIX, R_G_PRE_FFN, R_G_POST_FFN = 6, 7, 8, 9


def _dot(a, b):
    return jnp.dot(a, b, preferred_element_type=F32)


def _dot_nt(a, b):
    return lax.dot_general(a, b, (((1,), (1,)), ((), ())), preferred_element_type=F32)


def _dot_tn(a, b):
    return lax.dot_general(a, b, (((0,), (0,)), ((), ())), preferred_element_type=F32)


def _sigmoid(x):
    return 1.0 / (1.0 + jnp.exp(-x))


def _colsum(x):
    return jnp.sum(x, axis=0, keepdims=True)


def _rowmean(x):
    return jnp.mean(x, axis=-1, keepdims=True)


def _params(sem=None, vmem=VMEM_LIMIT):
    kw = dict(vmem_limit_bytes=vmem)
    if sem is not None:
        kw["dimension_semantics"] = sem
    return pltpu.CompilerParams(**kw)


def _full(shape):
    nd = len(shape)
    return pl.BlockSpec(shape, lambda *_: (0,) * nd)


def _pre_proj(x, vecs, w_cat):
    S = x.shape[0]
    tm = ROW_TILE

    def body(x_ref, v_ref, w_ref, h_ref, pa_ref, pm_ref, pif_ref, pg_ref):
        xv = x_ref[...]
        r = lax.rsqrt(_rowmean(xv * xv) + NORM_EPS)
        h = (xv * r * v_ref[R_G_PRE_MIX:R_G_PRE_MIX + 1, :]) * (1.0 + v_ref[R_SCALE_M:R_SCALE_M + 1, :]) \
            + v_ref[R_SHIFT_M:R_SHIFT_M + 1, :]
        hb = h.astype(BF16)
        h_ref[...] = hb
        pa_ref[...] = _dot(hb, w_ref[:, 0:A_W])
        pm_ref[...] = _dot(hb, w_ref[:, A_W:A_W + M_W])
        pif_ref[...] = _dot(hb, w_ref[:, A_W + M_W:A_W + M_W + IF_W])
        pg_ref[...] = _dot(hb, w_ref[:, A_W + M_W + IF_W:CAT_W])

    row = lambda w: pl.BlockSpec((tm, w), lambda i: (i, 0))
    return pl.pallas_call(
        body, name="pre_proj", grid=(S // tm,),
        in_specs=[row(D_MODEL), _full(vecs.shape), _full(w_cat.shape)],
        out_specs=[row(D_MODEL), row(A_W), row(M_W), row(IF_W), row(G_W)],
        out_shape=[jax.ShapeDtypeStruct((S, D_MODEL), BF16), jax.ShapeDtypeStruct((S, A_W), F32),
                   jax.ShapeDtypeStruct((S, M_W), F32), jax.ShapeDtypeStruct((S, IF_W), F32),
                   jax.ShapeDtypeStruct((S, G_W), F32)],
        compiler_params=_params(("parallel",)),
    )(x, vecs, w_cat)


def _mix_fwd(x, ya, ym, pg, vecs, w_ba, w_bm, w_out):
    S = x.shape[0]
    tm = ROW_TILE

    def body(x_ref, ya_ref, ym_ref, pg_ref, v_ref, wba_ref, wbm_ref, wout_ref,
             x1_ref, merged_ref, mix_ref, pa_ref, pb_ref):
        pa = _dot(ya_ref[...], wba_ref[...])
        pb = _dot(ym_ref[...], wbm_ref[...])
        merged = _sigmoid(pg_ref[:, 0:D_MODEL]) * pa + _sigmoid(pg_ref[:, D_MODEL:G_W]) * pb
        mb = merged.astype(BF16)
        mix = _dot(mb, wout_ref[...])
        r = lax.rsqrt(_rowmean(mix * mix) + NORM_EPS)
        x1_ref[...] = x_ref[...] + v_ref[R_GATE_M:R_GATE_M + 1, :] * (mix * r * v_ref[R_G_POST_MIX:R_G_POST_MIX + 1, :])
        merged_ref[...] = mb
        mix_ref[...] = mix
        pa_ref[...] = pa.astype(BF16)
        pb_ref[...] = pb.astype(BF16)

    row = lambda w: pl.BlockSpec((tm, w), lambda i: (i, 0))
    sd = lambda w, dt: jax.ShapeDtypeStruct((S, w), dt)
    return pl.pallas_call(
        body, name="mix_fwd", grid=(S // tm,),
        in_specs=[row(D_MODEL), row(512), row(512), row(G_W), _full(vecs.shape), _full(w_ba.shape),
                  _full(w_bm.shape), _full(w_out.shape)],
        out_specs=[row(D_MODEL)] * 5,
        out_shape=[sd(D_MODEL, F32), sd(D_MODEL, BF16), sd(D_MODEL, F32), sd(D_MODEL, BF16), sd(D_MODEL, BF16)],
        compiler_params=_params(("parallel",)),
    )(x, ya, ym, pg, vecs, w_ba, w_bm, w_out)


def _ffn_fwd_bwd(x1, tgt, vecs, w_gate, w_up, w_down):
    S = x1.shape[0]
    tm = ROW_TILE

    def body(x1_ref, tgt_ref, v_ref, wg_hbm, wu_hbm, wd_hbm,
             dx1_ref, h2_ref, hid_ref, da_ref, du_ref, dff_ref, acc_ref, loss_ref,
             wg, wu, wd, sem):
        i = pl.program_id(0)

        @pl.when(i == 0)
        def _():
            cps = [pltpu.make_async_copy(wg_hbm, wg, sem.at[0]), pltpu.make_async_copy(wu_hbm, wu, sem.at[1]),
                   pltpu.make_async_copy(wd_hbm, wd, sem.at[2])]
            for cp in cps:
                cp.start()
            for cp in cps:
                cp.wait()
            acc_ref[...] = jnp.zeros_like(acc_ref)
            loss_ref[...] = jnp.zeros_like(loss_ref)

        vrow = lambda r: v_ref[r:r + 1, :]
        x1v = x1_ref[...]
        r3 = lax.rsqrt(_rowmean(x1v * x1v) + NORM_EPS)
        x1hat = x1v * r3
        xn3 = x1hat * vrow(R_G_PRE_FFN)
        h2b = (xn3 * (1.0 + vrow(R_SCALE_F)) + vrow(R_SHIFT_F)).astype(BF16)
        h2_ref[...] = h2b
        a = _dot(h2b, wg[...])
        u = _dot(h2b, wu[...])
        sg = _sigmoid(a)
        sil = a * sg
        hidb = (sil * u).astype(BF16)
        hid_ref[...] = hidb
        ff = _dot(hidb, wd[...])
        r4 = lax.rsqrt(_rowmean(ff * ff) + NORM_EPS)
        ffhat = ff * r4
        n4 = ffhat * vrow(R_G_POST_FFN)
        err = x1v + vrow(R_GATE_F) * n4 - tgt_ref[...]
        loss_ref[...] += jnp.sum(err * err) * (0.5 / D_MODEL)
        dy = err * (1.0 / D_MODEL)
        acc_ref[0:1, :] += _colsum(dy * n4)
        dn4 = dy * vrow(R_GATE_F)
        acc_ref[1:2, :] += _colsum(dn4 * ffhat)
        dffhat = dn4 * vrow(R_G_POST_FFN)
        dffb = (r4 * (dffhat - ffhat * _rowmean(dffhat * ffhat))).astype(BF16)
        dff_ref[...] = dffb
        dhid = _dot_nt(dffb, wd[...])
        dub = (dhid * sil).astype(BF16)
        dab = (dhid * u * (sg * (1.0 + a * (1.0 - sg)))).astype(BF16)
        da_ref[...] = dab
        du_ref[...] = dub
        dh2 = _dot_nt(dab, wg[...]) + _dot_nt(dub, wu[...])
        acc_ref[2:3, :] += _colsum(dh2 * xn3)
        acc_ref[3:4, :] += _colsum(dh2)
        dxn3 = dh2 * (1.0 + vrow(R_SCALE_F))
        acc_ref[4:5, :] += _colsum(dxn3 * x1hat)
        dx1hat = dxn3 * vrow(R_G_PRE_FFN)
        dx1_ref[...] = dy + r3 * (dx1hat - x1hat * _rowmean(dx1hat * x1hat))

    row = lambda w: pl.BlockSpec((tm, w), lambda i: (i, 0))
    sd = lambda w, dt: jax.ShapeDtypeStruct((S, w), dt)
    anyspec = pl.BlockSpec(memory_space=pl.ANY)
    return pl.pallas_call(
        body, name="ffn_fwd_bwd", grid=(S // tm,),
        in_specs=[row(D_MODEL), row(D_MODEL), _full(vecs.shape), anyspec, anyspec, anyspec],
        out_specs=[row(D_MODEL), row(D_MODEL), row(D_FF), row(D_FF), row(D_FF), row(D_MODEL),
                   _full((8, D_MODEL)), _full((8, LANES))],
        out_shape=[sd(D_MODEL, F32), sd(D_MODEL, BF16), sd(D_FF, BF16), sd(D_FF, BF16), sd(D_FF, BF16),
                   sd(D_MODEL, BF16), jax.ShapeDtypeStruct((8, D_MODEL), F32), jax.ShapeDtypeStruct((8, LANES), F32)],
        scratch_shapes=[pltpu.VMEM(w_gate.shape, BF16), pltpu.VMEM(w_up.shape, BF16), pltpu.VMEM(w_down.shape, BF16),
                        pltpu.SemaphoreType.DMA((3,))],
        compiler_params=_params(("arbitrary",)),
    )(x1, tgt, vecs, w_gate, w_up, w_down)


def _mix_bwd(dx1, mix, pa, pb, pg, vecs, w_ba, w_bm, w_out):
    S = dx1.shape[0]
    tm = ROW_TILE

    def body(dx1_ref, mix_ref, pa_ref, pb_ref, pg_ref, v_ref, wba_ref, wbm_ref, wout_ref,
             dmix_ref, dpa_ref, dpb_ref, dg_ref, dya_ref, dym_ref, acc_ref):
        i = pl.program_id(0)

        @pl.when(i == 0)
        def _():
            acc_ref[...] = jnp.zeros_like(acc_ref)

        vrow = lambda r: v_ref[r:r + 1, :]
        dx1v = dx1_ref[...]
        mix = mix_ref[...]
        r2 = lax.rsqrt(_rowmean(mix * mix) + NORM_EPS)
        mixhat = mix * r2
        acc_ref[0:1, :] += _colsum(dx1v * (mixhat * vrow(R_G_POST_MIX)))
        dn2 = dx1v * vrow(R_GATE_M)
        acc_ref[1:2, :] += _colsum(dn2 * mixhat)
        dmixhat = dn2 * vrow(R_G_POST_MIX)
        dmixb = (r2 * (dmixhat - mixhat * _rowmean(dmixhat * mixhat))).astype(BF16)
        dmix_ref[...] = dmixb
        dmerged = _dot_nt(dmixb, wout_ref[...])
        sa = _sigmoid(pg_ref[:, 0:D_MODEL])
        sm = _sigmoid(pg_ref[:, D_MODEL:G_W])
        dpab = (dmerged * sa).astype(BF16)
        dpbb = (dmerged * sm).astype(BF16)
        dpa_ref[...] = dpab
        dpb_ref[...] = dpbb
        dg_ref[:, 0:D_MODEL] = (dmerged * pa_ref[...].astype(F32) * (sa * (1.0 - sa))).astype(BF16)
        dg_ref[:, D_MODEL:G_W] = (dmerged * pb_ref[...].astype(F32) * (sm * (1.0 - sm))).astype(BF16)
        dya_ref[...] = _dot_nt(dpab, wba_ref[...])
        dym_ref[...] = _dot_nt(dpbb, wbm_ref[...])

    row = lambda w: pl.BlockSpec((tm, w), lambda i: (i, 0))
    sd = lambda w, dt: jax.ShapeDtypeStruct((S, w), dt)
    return pl.pallas_call(
        body, name="mix_bwd", grid=(S // tm,),
        in_specs=[row(D_MODEL), row(D_MODEL), row(D_MODEL), row(D_MODEL), row(G_W), _full(vecs.shape),
                  _full(w_ba.shape), _full(w_bm.shape), _full(w_out.shape)],
        out_specs=[row(D_MODEL), row(D_MODEL), row(D_MODEL), row(G_W), row(512), row(512), _full((8, D_MODEL))],
        out_shape=[sd(D_MODEL, BF16), sd(D_MODEL, BF16), sd(D_MODEL, BF16), sd(G_W, BF16), sd(512, F32), sd(512, F32),
                   jax.ShapeDtypeStruct((8, D_MODEL), F32)],
        compiler_params=_params(("arbitrary",)),
    )(dx1, mix, pa, pb, pg, vecs, w_ba, w_bm, w_out)


def _pre_bwd(dproj, x, dx1, vecs, w_cat):
    S = x.shape[0]
    tm = ROW_TILE

    def body(dp_ref, x_ref, dx1_ref, v_ref, w_ref, dx_ref, acc_ref):
        i = pl.program_id(0)

        @pl.when(i == 0)
        def _():
            acc_ref[...] = jnp.zeros_like(acc_ref)

        vrow = lambda r: v_ref[r:r + 1, :]
        dh = _dot_nt(dp_ref[...], w_ref[...])
        xv = x_ref[...]
        r1 = lax.rsqrt(_rowmean(xv * xv) + NORM_EPS)
        xhat = xv * r1
        acc_ref[0:1, :] += _colsum(dh * (xhat * vrow(R_G_PRE_MIX)))
        acc_ref[1:2, :] += _colsum(dh)
        dxn = dh * (1.0 + vrow(R_SCALE_M))
        acc_ref[2:3, :] += _colsum(dxn * xhat)
        dxhat = dxn * vrow(R_G_PRE_MIX)
        dx_ref[...] = dx1_ref[...] + r1 * (dxhat - xhat * _rowmean(dxhat * xhat))

    row = lambda w: pl.BlockSpec((tm, w), lambda i: (i, 0))
    return pl.pallas_call(
        body, name="pre_bwd", grid=(S // tm,),
        in_specs=[row(CAT_W), row(D_MODEL), row(D_MODEL), _full(vecs.shape), _full(w_cat.shape)],
        out_specs=[row(D_MODEL), _full((8, D_MODEL))],
        out_shape=[jax.ShapeDtypeStruct((S, D_MODEL), F32), jax.ShapeDtypeStruct((8, D_MODEL), F32)],
        compiler_params=_params(("arbitrary",)),
    )(dproj, x, dx1, vecs, w_cat)


def _matmul_tn(a, b, tn, name, ts=512):
    S, K = a.shape
    N = b.shape[1]
    n_s = S // ts

    def body(a_ref, b_ref, o_ref, acc_ref):
        s = pl.program_id(1)

        @pl.when(s == 0)
        def _():
            acc_ref[...] = jnp.zeros_like(acc_ref)

        acc_ref[...] += _dot_tn(a_ref[...], b_ref[...])

        @pl.when(s == n_s - 1)
        def _():
            o_ref[...] = acc_ref[...].astype(BF16)

    return pl.pallas_call(
        body, name=name, grid=(N // tn, n_s),
        in_specs=[pl.BlockSpec((ts, K), lambda j, s: (s, 0)), pl.BlockSpec((ts, tn), lambda j, s: (s, j))],
        out_specs=pl.BlockSpec((K, tn), lambda j, s: (0, j)),
        out_shape=jax.ShapeDtypeStruct((K, N), BF16),
        scratch_shapes=[pltpu.VMEM((K, tn), F32)],
        compiler_params=_params(("parallel", "arbitrary")),
    )(a, b)


def _rope_swap(t):
    lane = lax.broadcasted_iota(jnp.int32, t.shape, 1)
    first = (lane & (HEAD_DIM - 1)) < (HEAD_DIM // 2)
    return jnp.where(first, pltpu.roll(t, LANES - HEAD_DIM // 2, 1), pltpu.roll(t, HEAD_DIM // 2, 1))


def _rope(t, cos, sin_signed):
    return t * cos + _rope_swap(t) * sin_signed


def _rope_t(d, cos, sin_signed):
    return d * cos + _rope_swap(d * sin_signed)


def _to_kv_lanes(chunk, p, h):
    lane = lax.broadcasted_iota(jnp.int32, chunk.shape, 1)
    src = chunk if p == h else pltpu.roll(chunk, HEAD_DIM, 1)
    return jnp.where((lane >> 6) == h, src, jnp.zeros_like(src))


def _from_kv_lanes(o_a, o_b, h):
    lane = lax.broadcasted_iota(jnp.int32, o_a.shape, 1)
    a = o_a if h == 0 else pltpu.roll(o_a, HEAD_DIM, 1)
    b = o_b if h == 1 else pltpu.roll(o_b, HEAD_DIM, 1)
    return jnp.where(lane < HEAD_DIM, a, b)


def _band_mask(n):
    blk = ATTN_BLOCK
    qi = lax.broadcasted_iota(jnp.int32, (4 * blk, 2 * blk), 0) & (blk - 1)
    kj = lax.broadcasted_iota(jnp.int32, (4 * blk, 2 * blk), 1)
    return (kj > qi) & (kj <= qi + blk) & ((n > 0) | (kj >= blk))


def _stack_heads(chunks, h, dtype):
    parts = []
    for g in range(4):
        j = 4 * h + g
        parts.append(_to_kv_lanes(chunks[j // 2], j % 2, h))
    return jnp.concatenate(parts, axis=0).astype(dtype)


def _attn_fwd(pa, cos, sin, sinks):
    S = pa.shape[0]
    blk = ATTN_BLOCK
    nb = S // blk

    def body(sink_ref, cur_ref, prev_ref, cos_ref, sin_ref, cosp_ref, sinp_ref,
             ya_ref, qr_ref, kr_ref, vb_ref, lse_ref):
        n = pl.program_id(0)
        cos_c, sin_c = cos_ref[...], sin_ref[...]
        qch = [_rope(cur_ref[:, c * LANES:(c + 1) * LANES], cos_c, sin_c) * (HEAD_DIM ** -0.5) for c in range(4)]
        for c in range(4):
            qr_ref[:, c * LANES:(c + 1) * LANES] = qch[c].astype(BF16)
        k_cur = _rope(cur_ref[:, 512:640], cos_c, sin_c).astype(BF16)
        k_prev = _rope(prev_ref[:, 0:LANES], cosp_ref[...], sinp_ref[...]).astype(BF16)
        v_cur = cur_ref[:, 640:768].astype(BF16)
        v_prev = prev_ref[:, LANES:2 * LANES].astype(BF16)
        kr_ref[...] = k_cur
        vb_ref[...] = v_cur
        K = jnp.concatenate([k_prev, k_cur], axis=0)
        V = jnp.concatenate([v_prev, v_cur], axis=0)
        mask = _band_mask(n)
        rowg = lax.broadcasted_iota(jnp.int32, (4 * blk, 1), 0) >> 7
        lane = lax.broadcasted_iota(jnp.int32, (blk, LANES), 1)
        lse_tile = jnp.zeros((blk, LANES), F32)
        outs = []
        for h in range(2):
            qs = _stack_heads(qch, h, BF16)
            s = jnp.where(mask, _dot_nt(qs, K), NEG)
            sink = jnp.zeros((4 * blk, 1), F32)
            for g in range(4):
                sink = jnp.where(rowg == g, sink_ref[4 * h + g], sink)
            m = jnp.maximum(jnp.max(s, axis=1, keepdims=True), sink)
            p = jnp.exp(s - m)
            den = jnp.sum(p, axis=1, keepdims=True) + jnp.exp(sink - m)
            o = _dot((p / den).astype(BF16), V)
            lse = m + jnp.log(den)
            for g in range(4):
                outs.append(o[g * blk:(g + 1) * blk, :])
                lse_tile = jnp.where(lane == 4 * h + g, lse[g * blk:(g + 1) * blk, :], lse_tile)
        for c in range(4):
            ya_ref[:, c * LANES:(c + 1) * LANES] = _from_kv_lanes(outs[2 * c], outs[2 * c + 1], c // 2).astype(BF16)
        lse_ref[...] = lse_tile

    prev = lambda n: jnp.maximum(n - 1, 0)
    sd = lambda w, dt: jax.ShapeDtypeStruct((S, w), dt)
    return pl.pallas_call(
        body, name="attn_fwd", grid=(nb,),
        in_specs=[pl.BlockSpec(memory_space=pltpu.SMEM),
                  pl.BlockSpec((blk, A_W), lambda n: (n, 0)),
                  pl.BlockSpec((blk, 256), lambda n: (prev(n), 2)),
                  pl.BlockSpec((blk, LANES), lambda n: (n, 0)), pl.BlockSpec((blk, LANES), lambda n: (n, 0)),
                  pl.BlockSpec((blk, LANES), lambda n: (prev(n), 0)), pl.BlockSpec((blk, LANES), lambda n: (prev(n), 0))],
        out_specs=[pl.BlockSpec((blk, 512), lambda n: (n, 0)), pl.BlockSpec((blk, 512), lambda n: (n, 0)),
                   pl.BlockSpec((blk, LANES), lambda n: (n, 0)), pl.BlockSpec((blk, LANES), lambda n: (n, 0)),
                   pl.BlockSpec((blk, LANES), lambda n: (n, 0))],
        out_shape=[sd(512, BF16), sd(512, BF16), sd(LANES, BF16), sd(LANES, BF16), sd(LANES, F32)],
        compiler_params=_params(("parallel",)),
    )(sinks, pa, pa, cos, sin, cos, sin)


def _attn_bwd(dya, qr, kr, vb, lse, cos, sin, sinks):
    S = dya.shape[0]
    blk = ATTN_BLOCK
    nb = S // blk

    def body(sink_ref, dya_ref, qr_ref, kc_ref, kp_ref, vc_ref, vp_ref, lse_ref, cos_ref, sin_ref, cosp_ref, sinp_ref,
             dq_ref, dkv_ref, dsink_ref, ck, cv):
        n = pl.program_id(0)

        @pl.when(n == 0)
        def _():
            ck[...] = jnp.zeros_like(ck)
            cv[...] = jnp.zeros_like(cv)
            dsink_ref[...] = jnp.zeros_like(dsink_ref)

        @pl.when(n < nb)
        def _():
            K = jnp.concatenate([kp_ref[...], kc_ref[...]], axis=0)
            V = jnp.concatenate([vp_ref[...], vc_ref[...]], axis=0)
            qch = [qr_ref[:, c * LANES:(c + 1) * LANES] for c in range(4)]
            dch = [dya_ref[:, c * LANES:(c + 1) * LANES] for c in range(4)]
            lse_tile = lse_ref[...]
            mask = _band_mask(n)
            rowg = lax.broadcasted_iota(jnp.int32, (4 * blk, 1), 0) >> 7
            lane8 = lax.broadcasted_iota(jnp.int32, (8, LANES), 1)
            dk_acc = jnp.zeros((2 * blk, LANES), F32)
            dv_acc = jnp.zeros((2 * blk, LANES), F32)
            dsink = jnp.zeros((8, LANES), F32)
            dqs = []
            for h in range(2):
                qs = _stack_heads(qch, h, BF16)
                dos = _stack_heads(dch, h, BF16)
                lse_col = jnp.concatenate([lse_tile[:, 4 * h + g:4 * h + g + 1] for g in range(4)], axis=0)
                p = jnp.where(mask, jnp.exp(_dot_nt(qs, K) - lse_col), 0.0)
                dp = _dot_nt(dos, V)
                delta = jnp.sum(p * dp, axis=1, keepdims=True)
                dsb = (p * (dp - delta)).astype(BF16)
                dq = _dot(dsb, K)
                dk_acc = dk_acc + _dot_tn(dsb, qs)
                dv_acc = dv_acc + _dot_tn(p.astype(BF16), dos)
                sink = jnp.zeros((4 * blk, 1), F32)
                for g in range(4):
                    sink = jnp.where(rowg == g, sink_ref[4 * h + g], sink)
                ps_delta = jnp.exp(sink - lse_col) * delta
                for g in range(4):
                    dqs.append(dq[g * blk:(g + 1) * blk, :])
                    dsink = jnp.where(lane8 == 4 * h + g, dsink - jnp.sum(ps_delta[g * blk:(g + 1) * blk, :]), dsink)
            dsink_ref[...] += dsink
            cos_c, sin_c = cos_ref[...], sin_ref[...]
            for c in range(4):
                dqc = _from_kv_lanes(dqs[2 * c], dqs[2 * c + 1], c // 2) * (HEAD_DIM ** -0.5)
                dq_ref[:, c * LANES:(c + 1) * LANES] = _rope_t(dqc, cos_c, sin_c).astype(BF16)
            dkv_ref[:, 0:LANES] = _rope_t(dk_acc[0:blk, :] + ck[...], cosp_ref[...], sinp_ref[...]).astype(BF16)
            dkv_ref[:, LANES:2 * LANES] = (dv_acc[0:blk, :] + cv[...]).astype(BF16)
            ck[...] = dk_acc[blk:2 * blk, :]
            cv[...] = dv_acc[blk:2 * blk, :]

        @pl.when(n == nb)
        def _():
            dkv_ref[:, 0:LANES] = _rope_t(ck[...], cosp_ref[...], sinp_ref[...]).astype(BF16)
            dkv_ref[:, LANES:2 * LANES] = cv[...].astype(BF16)

    cur = lambda n: jnp.minimum(n, nb - 1)
    prev = lambda n: jnp.maximum(n - 1, 0)
    bs = lambda w, f: pl.BlockSpec((blk, w), lambda n: (f(n), 0))
    return pl.pallas_call(
        body, name="attn_bwd", grid=(nb + 1,),
        in_specs=[pl.BlockSpec(memory_space=pltpu.SMEM),
                  bs(512, cur), bs(512, cur), bs(LANES, cur), bs(LANES, prev), bs(LANES, cur), bs(LANES, prev),
                  bs(LANES, cur), bs(LANES, cur), bs(LANES, cur), bs(LANES, prev), bs(LANES, prev)],
        out_specs=[bs(512, cur), bs(256, prev), _full((8, LANES))],
        out_shape=[jax.ShapeDtypeStruct((S, 512), BF16), jax.ShapeDtypeStruct((S, 256), BF16),
                   jax.ShapeDtypeStruct((8, LANES), F32)],
        scratch_shapes=[pltpu.VMEM((blk, LANES), F32), pltpu.VMEM((blk, LANES), F32)],
        compiler_params=_params(("arbitrary",)),
    )(sinks, dya, qr, kr, kr, vb, vb, lse, cos, sin, cos, sin)


def _split3(x):
    hi = x.astype(BF16)
    r1 = x - hi.astype(F32)
    mid = r1.astype(BF16)
    lo = (r1 - mid.astype(F32)).astype(BF16)
    return hi, mid, lo


def _tri_matmul(tri_b, x):
    hi, mid, lo = _split3(x)
    return _dot(tri_b, hi) + _dot(tri_b, mid) + _dot(tri_b, lo)


def _log_sigmoid(x):
    return jnp.minimum(x, 0.0) - jnp.log(1.0 + jnp.exp(-jnp.abs(x)))


def _conv_silu_fwd(cur, prev, cw_ref, first):
    L = cur.shape[0]
    row = lax.broadcasted_iota(jnp.int32, cur.shape, 0)
    prev = jnp.where(first, jnp.zeros_like(prev), prev)
    shifted = [cur]
    for k in range(1, 4):
        shifted.append(jnp.where(row < k, pltpu.roll(prev, k, 0), pltpu.roll(cur, k, 0)))
    z = cw_ref[4:5, :]
    for k in range(3, -1, -1):
        z = z + shifted[k] * cw_ref[3 - k:4 - k, :]
    return z, shifted


def _mlstm_head_fwd(qh, kh, vh, i_col, b_col, C_prev, n_prev, m_prev, tri, eye):
    L = qh.shape[0]
    col2row = lambda x: jnp.sum(jnp.where(eye, x, 0.0), axis=0, keepdims=True)
    b_row = col2row(b_col)
    i_row = col2row(i_col)
    bl = b_col[L - 1:L, :]
    Dm = jnp.where(tri, b_col - b_row + i_row, NEG)
    inter = b_col + m_prev
    m_t = jnp.maximum(inter, jnp.max(Dm, axis=1, keepdims=True))
    W = jnp.exp(Dm - m_t)
    e_t = jnp.exp(inter - m_t)
    qb, kb, vb = qh.astype(BF16), kh.astype(BF16), vh.astype(BF16)
    Sc = _dot_nt(qb, kb) * W
    P1 = _dot(qb, C_prev.astype(BF16))
    num = _dot(Sc.astype(BF16), vb) + e_t * P1
    qn = jnp.sum(qh * n_prev, axis=1, keepdims=True)
    den = jnp.sum(Sc, axis=1, keepdims=True) + e_t * qn
    floor = jnp.exp(-m_t)
    g = jnp.maximum(jnp.abs(den), floor)
    hv = num / g
    a_col = bl - b_col + i_col
    m_new = jnp.maximum(bl + m_prev, jnp.max(a_col, axis=0, keepdims=True))
    dec = jnp.exp(bl + m_prev - m_new)
    u_col = jnp.exp(a_col - m_new)
    return dict(W=W, e_t=e_t, qb=qb, kb=kb, vb=vb, Sc=Sc, P1=P1, qn=qn, den=den, floor=floor, g=g, hv=hv,
                m_new=m_new, dec=dec, u_col=u_col)


def _mlstm_fwd(pm, pif, cw, sv):
    S = pm.shape[0]
    L = MLSTM_CHUNK
    nc = S // L
    HD = MLSTM_HEAD_DIM
    W4 = MLSTM_HEADS * HD

    def body(cur_ref, prev_ref, pif_ref, cw_ref, sv_ref, ym_ref, cst_ref, nst_ref, C, nm):
        c = pl.program_id(0)

        @pl.when(c == 0)
        def _():
            C[...] = jnp.zeros_like(C)
            nm[...] = jnp.zeros_like(nm)

        z, _ = _conv_silu_fwd(cur_ref[:, 0:2 * W4], prev_ref[...], cw_ref, c == 0)
        qk = z * _sigmoid(z)
        gt = pif_ref[...] + sv_ref[1:2, 0:LANES]
        r_i = lax.broadcasted_iota(jnp.int32, (L, L), 0)
        c_i = lax.broadcasted_iota(jnp.int32, (L, L), 1)
        tri = c_i <= r_i
        eye = c_i == r_i
        b_all = _tri_matmul(tri.astype(BF16), _log_sigmoid(gt))
        nst_ref[0] = nm[...]
        for h in range(MLSTM_HEADS):
            sl = slice(h * HD, (h + 1) * HD)
            qh = qk[:, sl]
            kh = qk[:, W4 + h * HD:W4 + (h + 1) * HD] * (HD ** -0.5)
            vh = cur_ref[:, 2 * W4 + h * HD:2 * W4 + (h + 1) * HD]
            n_prev = nm[h:h + 1, :]
            m_prev = nm[4 + h:5 + h, 0:1]
            C_prev = C[h]
            cst_ref[0, h] = C_prev
            f = _mlstm_head_fwd(qh, kh, vh, gt[:, h:h + 1], b_all[:, 4 + h:5 + h], C_prev, n_prev, m_prev, tri, eye)
            hv = f["hv"]
            xc = hv - _rowmean(hv)
            hhat = xc * lax.rsqrt(_rowmean(xc * xc) + NORM_EPS)
            so = _sigmoid(cur_ref[:, 3 * W4 + h * HD:3 * W4 + (h + 1) * HD])
            ym_ref[:, sl] = (so * hhat * sv_ref[0:1, sl]).astype(BF16)
            kw = kh * f["u_col"]
            C[h] = f["dec"] * C_prev + _dot_tn(kw.astype(BF16), f["vb"])
            nm[h:h + 1, :] = f["dec"] * n_prev + _colsum(kw)
            nm[4 + h:5 + h, :] = jnp.broadcast_to(f["m_new"], (1, LANES))

    prev = lambda c: jnp.maximum(c - 1, 0)
    return pl.pallas_call(
        body, name="mlstm_fwd", grid=(nc,),
        in_specs=[pl.BlockSpec((L, M_W), lambda c: (c, 0)), pl.BlockSpec((L, 2 * W4), lambda c: (prev(c), 0)),
                  pl.BlockSpec((L, IF_W), lambda c: (c, 0)), _full(cw.shape), _full(sv.shape)],
        out_specs=[pl.BlockSpec((L, W4), lambda c: (c, 0)),
                   pl.BlockSpec((1, MLSTM_HEADS, HD, HD), lambda c: (c, 0, 0, 0)),
                   pl.BlockSpec((1, 8, LANES), lambda c: (c, 0, 0))],
        out_shape=[jax.ShapeDtypeStruct((S, W4), BF16), jax.ShapeDtypeStruct((nc, MLSTM_HEADS, HD, HD), F32),
                   jax.ShapeDtypeStruct((nc, 8, LANES), F32)],
        scratch_shapes=[pltpu.VMEM((MLSTM_HEADS, HD, HD), F32), pltpu.VMEM((8, LANES), F32)],
        compiler_params=_params(("arbitrary",)),
    )(pm, pm, pif, cw, sv)


def _mlstm_bwd(pm, pif, cw, sv, dym, cst, nst):
    S = pm.shape[0]
    L = MLSTM_CHUNK
    nc = S // L
    HD = MLSTM_HEAD_DIM
    W4 = MLSTM_HEADS * HD

    def body(cur_ref, prev_ref, pif_ref, cw_ref, sv_ref, dym_ref, cst_ref, nst_ref,
             dm_ref, dif_ref, dcw_ref, dsv_ref, dC, dn, dz_next, dqk):
        r = pl.program_id(0)
        c = nc - 1 - r

        @pl.when(r == 0)
        def _():
            dC[...] = jnp.zeros_like(dC)
            dn[...] = jnp.zeros_like(dn)
            dz_next[...] = jnp.zeros_like(dz_next)
            dcw_ref[...] = jnp.zeros_like(dcw_ref)
            dsv_ref[...] = jnp.zeros_like(dsv_ref)

        z, shifted = _conv_silu_fwd(cur_ref[:, 0:2 * W4], prev_ref[...], cw_ref, c == 0)
        sgz = _sigmoid(z)
        qk = z * sgz
        gt = pif_ref[...] + sv_ref[1:2, 0:LANES]
        r_i = lax.broadcasted_iota(jnp.int32, (L, L), 0)
        c_i = lax.broadcasted_iota(jnp.int32, (L, L), 1)
        tri = c_i <= r_i
        eye = c_i == r_i
        b_all = _tri_matmul(tri.astype(BF16), _log_sigmoid(gt))
        lane = lax.broadcasted_iota(jnp.int32, (L, LANES), 1)
        rowl = lax.broadcasted_iota(jnp.int32, (L, 1), 0)
        nmv = nst_ref[0]
        di_tile = jnp.zeros((L, LANES), F32)
        db_tile = jnp.zeros((L, LANES), F32)
        for h in range(MLSTM_HEADS):
            sl = slice(h * HD, (h + 1) * HD)
            qh = qk[:, sl]
            kh = qk[:, W4 + h * HD:W4 + (h + 1) * HD] * (HD ** -0.5)
            vh = cur_ref[:, 2 * W4 + h * HD:2 * W4 + (h + 1) * HD]
            n_prev = nmv[h:h + 1, :]
            m_prev = nmv[4 + h:5 + h, 0:1]
            C_prev = cst_ref[0, h]
            f = _mlstm_head_fwd(qh, kh, vh, gt[:, h:h + 1], b_all[:, 4 + h:5 + h], C_prev, n_prev, m_prev, tri, eye)
            hv, g, den, e_t, u_col, dec = f["hv"], f["g"], f["den"], f["e_t"], f["u_col"], f["dec"]
            qb, kb, vb, Sc, W = f["qb"], f["kb"], f["vb"], f["Sc"], f["W"]
            xc = hv - _rowmean(hv)
            rstd = lax.rsqrt(_rowmean(xc * xc) + NORM_EPS)
            hhat = xc * rstd
            wn = sv_ref[0:1, sl]
            so = _sigmoid(cur_ref[:, 3 * W4 + h * HD:3 * W4 + (h + 1) * HD])
            dy = dym_ref[:, sl]
            dm_ref[:, 3 * W4 + h * HD:3 * W4 + (h + 1) * HD] = (dy * hhat * wn * (so * (1.0 - so))).astype(BF16)
            dln = dy * so
            dsv_ref[0:1, sl] += _colsum(dln * hhat)
            dhhat = dln * wn
            dh = rstd * (dhhat - _rowmean(dhhat) - hhat * _rowmean(dhhat * hhat))
            dnum = dh / g
            active = jnp.abs(den) > f["floor"]
            dden = jnp.where(active, -jnp.sum(dh * hv, axis=1, keepdims=True) / g * jnp.where(den >= 0.0, 1.0, -1.0), 0.0)
            dnumb = dnum.astype(BF16)
            dSc = _dot_nt(dnumb, vb) + dden
            dA = (dSc * W).astype(BF16)
            G = dSc * Sc
            Gr = jnp.sum(G, axis=1, keepdims=True)
            Gc = jnp.sum(jnp.where(eye, _colsum(G), 0.0), axis=1, keepdims=True)
            dCn = dC[h]
            dCnb = dCn.astype(BF16)
            dn_new = dn[h:h + 1, :]
            kdC = _dot(kb, dCnb)
            vdC = _dot_nt(vb, dCnb)
            dv = _dot_tn(Sc.astype(BF16), dnumb) + u_col * kdC
            Cb = C_prev.astype(BF16)
            dq = _dot(dA, kb) + e_t * _dot_nt(dnumb, Cb) + (e_t * dden) * n_prev
            dk = _dot_tn(dA, qb) + u_col * (vdC + dn_new)
            de = jnp.sum(f["P1"] * dnum, axis=1, keepdims=True) + dden * f["qn"]
            E = de * e_t
            du = jnp.sum(kdC * vh, axis=1, keepdims=True) + jnp.sum(kh * dn_new, axis=1, keepdims=True)
            U = du * u_col
            ddec = jnp.sum(dCn * C_prev) + jnp.sum(dn_new * n_prev)
            dbl = ddec * dec + jnp.sum(U, axis=0, keepdims=True)
            db = Gr + E - Gc - U + jnp.where(rowl == L - 1, dbl, 0.0)
            di_tile = jnp.where(lane == h, Gc + U, di_tile)
            db_tile = jnp.where(lane == 4 + h, db, db_tile)
            dC[h] = dec * dCn + _dot_tn((qh * e_t).astype(BF16), dnumb)
            dn[h:h + 1, :] = dec * dn_new + _colsum((e_t * dden) * qh)
            dqk[:, sl] = dq
            dqk[:, W4 + h * HD:W4 + (h + 1) * HD] = dk * (HD ** -0.5)
            dm_ref[:, 2 * W4 + h * HD:2 * W4 + (h + 1) * HD] = dv.astype(BF16)
        dlf = _tri_matmul((r_i <= c_i).astype(BF16), db_tile)
        dif = jnp.where(lane < 4, di_tile, jnp.where(lane < 8, dlf * (1.0 - _sigmoid(gt)), 0.0))
        dif_ref[...] = dif.astype(BF16)
        dsv_ref[1:2, 0:LANES] += _colsum(dif)
        dz = dqk[...] * (sgz * (1.0 + z * (1.0 - sgz)))
        dcw_ref[4:5, :] += _colsum(dz)
        row = lax.broadcasted_iota(jnp.int32, dz.shape, 0)
        dzn = dz_next[...]
        du_in = dz * cw_ref[3:4, :]
        dcw_ref[3:4, :] += _colsum(dz * shifted[0])
        for k in range(1, 4):
            dcw_ref[3 - k:4 - k, :] += _colsum(dz * shifted[k])
            up = jnp.where(row >= L - k, pltpu.roll(dzn, L - k, 0), pltpu.roll(dz, L - k, 0))
            du_in = du_in + up * cw_ref[3 - k:4 - k, :]
        dz_next[...] = dz
        dm_ref[:, 0:2 * W4] = du_in.astype(BF16)

    cidx = lambda r: nc - 1 - r
    prev = lambda r: jnp.maximum(nc - 2 - r, 0)
    return pl.pallas_call(
        body, name="mlstm_bwd", grid=(nc,),
        in_specs=[pl.BlockSpec((L, M_W), lambda r: (cidx(r), 0)), pl.BlockSpec((L, 2 * W4), lambda r: (prev(r), 0)),
                  pl.BlockSpec((L, IF_W), lambda r: (cidx(r), 0)), _full(cw.shape), _full(sv.shape),
                  pl.BlockSpec((L, W4), lambda r: (cidx(r), 0)),
                  pl.BlockSpec((1, MLSTM_HEADS, HD, HD), lambda r: (cidx(r), 0, 0, 0)),
                  pl.BlockSpec((1, 8, LANES), lambda r: (cidx(r), 0, 0))],
        out_specs=[pl.BlockSpec((L, M_W), lambda r: (cidx(r), 0)), pl.BlockSpec((L, IF_W), lambda r: (cidx(r), 0)),
                   _full((8, 2 * W4)), _full((8, W4))],
        out_shape=[jax.ShapeDtypeStruct((S, M_W), BF16), jax.ShapeDtypeStruct((S, IF_W), BF16),
                   jax.ShapeDtypeStruct((8, 2 * W4), F32), jax.ShapeDtypeStruct((8, W4), F32)],
        scratch_shapes=[pltpu.VMEM((MLSTM_HEADS, HD, HD), F32), pltpu.VMEM((8, LANES), F32),
                        pltpu.VMEM((L, 2 * W4), F32), pltpu.VMEM((L, 2 * W4), F32)],
        compiler_params=_params(("arbitrary",)),
    )(pm, pm, pif, cw, sv, dym, cst, nst)


def _rope_tables(positions):
    half = HEAD_DIM // 2
    inv_freq = ROPE_THETA ** (-2.0 * jnp.arange(half, dtype=F32) / HEAD_DIM)
    ang = positions.astype(F32)[:, None] * inv_freq
    cos = jnp.tile(jnp.cos(ang), (1, LANES // half))
    sign = jnp.tile(jnp.concatenate([-jnp.ones((half,), F32), jnp.ones((half,), F32)]), LANES // HEAD_DIM)
    sin = jnp.tile(jnp.sin(ang), (1, LANES // half)) * sign
    return cos, sin


def _local_step(x, tgt, positions, mod, gains, w_cat, w_ba, w_bm, w_out, w_gate, w_up, w_down,
                conv_w, conv_b, b_if, sinks, norm_w):
    S = x.shape[0]
    vecs = jnp.concatenate([mod, gains, jnp.zeros((6, D_MODEL), F32)], axis=0)
    cw = jnp.concatenate([conv_w, conv_b.reshape(1, -1), jnp.zeros((3, 2 * 512), F32)], axis=0)
    sv = jnp.zeros((8, 512), F32).at[0].set(norm_w).at[1, 0:8].set(b_if)
    cos, sin = _rope_tables(positions)

    h, pa, pm, pif, pg = _pre_proj(x, vecs, w_cat)
    ya, qr, kr, vb, lse = _attn_fwd(pa, cos, sin, sinks)
    ym, cst, nst = _mlstm_fwd(pm, pif, cw, sv)
    x1, merged, mix, pba, pbm = _mix_fwd(x, ya, ym, pg, vecs, w_ba, w_bm, w_out)
    dx1, h2, hid, da, du, dff, acc_f, loss = _ffn_fwd_bwd(x1, tgt, vecs, w_gate, w_up, w_down)
    dmix, dpa, dpb, dg, dya, dym, acc_m = _mix_bwd(dx1, mix, pba, pbm, pg, vecs, w_ba, w_bm, w_out)
    dq, dkv, dsink = _attn_bwd(dya, qr, kr, vb, lse, cos, sin, sinks)
    dm, dif, dcw, dsv = _mlstm_bwd(pm, pif, cw, sv, dym, cst, nst)
    dproj = jnp.concatenate([dq, dkv, dm, dif, dg], axis=1)
    grad_x, acc_p = _pre_bwd(dproj, x, dx1, vecs, w_cat)

    g_w_cat = _matmul_tn(h, dproj, 1664, "dw_in")
    g_w_ba = _matmul_tn(ya, dpa, 1024, "dw_branch_attn")
    g_w_bm = _matmul_tn(ym, dpb, 1024, "dw_branch_mlstm")
    g_w_out = _matmul_tn(merged, dmix, 1024, "dw_out")
    g_w_gate = _matmul_tn(h2, da, 1408, "dw_ffn_gate")
    g_w_up = _matmul_tn(h2, du, 1408, "dw_ffn_up")
    g_w_down = _matmul_tn(hid, dff, 1024, "dw_ffn_down")

    dmod = jnp.stack([acc_p[1], acc_p[0], acc_m[0], acc_f[3], acc_f[2], acc_f[0]])
    dgains = jnp.stack([acc_p[2], acc_m[1], acc_f[4], acc_f[1]])
    small = dict(dmod=dmod, dgains=dgains, dconv_w=dcw[0:4], dconv_b=dcw[4], db_if=dsv[1, 0:8],
                 dsinks=dsink[0, 0:8], dnorm_w=dsv[0])
    big = dict(w_cat=g_w_cat, w_ba=g_w_ba, w_bm=g_w_bm, w_out=g_w_out, w_gate=g_w_gate, w_up=g_w_up, w_down=g_w_down)
    return loss[0, 0], grad_x, big, small


MESH_ID = pl.DeviceIdType.MESH


def _mesh_pos():
    return lax.axis_index("x"), lax.axis_index("y"), lax.axis_index("c")


def _flip(v, bit):
    return 1 - v if bit else v


def _relations():
    return [((r >> 2) & 1, (r >> 1) & 1, r & 1) for r in range(1, N_DEV)]


def _small_exchange(p, gather, name):
    V = p.shape[-1]

    def body(p_ref, out_ref, send_sems, recv_sems):
        x, y, c = _mesh_pos()
        me = 4 * x + 2 * y + c
        out_ref[me] = p_ref[...] if gather else p_ref[me]
        peers = []
        for dx, dy, dc in _relations():
            px, py, pc = _flip(x, dx), _flip(y, dy), _flip(c, dc)
            peers.append(((px, py, pc), 4 * px + 2 * py + pc))

        def copy(k, landing):
            peer, pid = peers[k]
            return pltpu.make_async_remote_copy(
                src_ref=p_ref if gather else p_ref.at[pid], dst_ref=out_ref.at[landing],
                send_sem=send_sems.at[k], recv_sem=recv_sems.at[k], device_id=peer, device_id_type=MESH_ID)

        sends = [copy(k, me) for k in range(N_DEV - 1)]
        for cp in sends:
            cp.start()
        for k in range(N_DEV - 1):
            copy(k, peers[k][1]).wait_recv()
        for cp in sends:
            cp.wait_send()

    vm = pl.BlockSpec(memory_space=pltpu.VMEM)
    return pl.pallas_call(
        body, name=name, in_specs=[vm], out_specs=vm,
        out_shape=jax.ShapeDtypeStruct((N_DEV, 8, V), F32),
        scratch_shapes=[pltpu.SemaphoreType.DMA((N_DEV - 1,)), pltpu.SemaphoreType.DMA((N_DEV - 1,))],
        compiler_params=pltpu.CompilerParams(vmem_limit_bytes=VMEM_LIMIT),
    )(p)


def _all_gather_hbm(packed):
    R = packed.shape[0]

    def body(p_ref, out_ref, send_sems, recv_sems, local_sem):
        x, y, c = _mesh_pos()
        me, sibling = (x, y, c), (x, y, 1 - c)
        chips = [(1 - x, y), (x, 1 - y), (1 - x, 1 - y)]
        slot = lambda px, py, pc: out_ref.at[4 * px + 2 * py + pc]

        def copy(k, block, to, src=None):
            return pltpu.make_async_remote_copy(
                src_ref=slot(*block) if src is None else src, dst_ref=slot(*block),
                send_sem=send_sems.at[k], recv_sem=recv_sems.at[k], device_id=to, device_id_type=MESH_ID)

        mine = pltpu.make_async_copy(p_ref, slot(*me), local_sem)
        mine.start()
        first = [copy(0, me, sibling, src=p_ref)]
        first += [copy(1 + j, me, (*chip, c), src=p_ref) for j, chip in enumerate(chips)]
        for cp in first:
            cp.start()
        passed = [copy(4 + j, (*chip, c), sibling) for j, chip in enumerate(chips)]
        for j, chip in enumerate(chips):
            copy(1 + j, (*chip, c), me).wait_recv()
            passed[j].start()
        copy(0, sibling, me).wait_recv()
        for j, chip in enumerate(chips):
            copy(4 + j, (*chip, 1 - c), me).wait_recv()
        for cp in first + passed:
            cp.wait_send()
        mine.wait()

    hbm = pl.BlockSpec(memory_space=pl.ANY)
    return pl.pallas_call(
        body, name="gather_weights", in_specs=[hbm], out_specs=hbm,
        out_shape=jax.ShapeDtypeStruct((N_DEV, R, LANES), packed.dtype),
        scratch_shapes=[pltpu.SemaphoreType.DMA((N_DEV - 1,)), pltpu.SemaphoreType.DMA((N_DEV - 1,)),
                        pltpu.SemaphoreType.DMA],
    )(packed)


def _scatter_hbm(pieces):
    R = pieces.shape[1]

    def body(p_ref, out_ref, send_sems, recv_sems, local_sem):
        x, y, c = _mesh_pos()
        me = 4 * x + 2 * y + c
        mine = pltpu.make_async_copy(p_ref.at[me], out_ref.at[me], local_sem)
        mine.start()
        peers = []
        for dx, dy, dc in _relations():
            px, py, pc = _flip(x, dx), _flip(y, dy), _flip(c, dc)
            peers.append(((px, py, pc), 4 * px + 2 * py + pc))

        def copy(k, landing):
            peer, pid = peers[k]
            return pltpu.make_async_remote_copy(
                src_ref=p_ref.at[pid], dst_ref=out_ref.at[landing],
                send_sem=send_sems.at[k], recv_sem=recv_sems.at[k], device_id=peer, device_id_type=MESH_ID)

        sends = [copy(k, me) for k in range(N_DEV - 1)]
        for cp in sends:
            cp.start()
        for k in range(N_DEV - 1):
            copy(k, peers[k][1]).wait_recv()
        for cp in sends:
            cp.wait_send()
        mine.wait()

    hbm = pl.BlockSpec(memory_space=pl.ANY)
    return pl.pallas_call(
        body, name="scatter_grads", in_specs=[hbm], out_specs=hbm,
        out_shape=jax.ShapeDtypeStruct((N_DEV, R, LANES), pieces.dtype),
        scratch_shapes=[pltpu.SemaphoreType.DMA((N_DEV - 1,)), pltpu.SemaphoreType.DMA((N_DEV - 1,)),
                        pltpu.SemaphoreType.DMA],
    )(pieces)


def _adamw(w, g, m, v):
    m2 = ADAM_B1 * m + (1.0 - ADAM_B1) * g
    v2 = ADAM_B2 * v + (1.0 - ADAM_B2) * (g * g)
    m_hat = m2 / (1.0 - ADAM_B1 ** ADAM_STEP)
    v_hat = v2 / (1.0 - ADAM_B2 ** ADAM_STEP)
    delta = -ADAM_LR * (m_hat / (jnp.sqrt(v_hat) + ADAM_EPS) + ADAM_WD * w)
    return delta, m2, v2


def _mod_partial(cmat, w_shard, b_shard):
    def body(c_ref, w_ref, b_ref, o_ref):
        o_ref[...] = _dot(c_ref[...].astype(BF16), w_ref[...].astype(BF16)) + b_ref[...]

    return pl.pallas_call(
        body, name="mod_partial", out_shape=jax.ShapeDtypeStruct((N_DEV, w_shard.shape[1]), F32),
        compiler_params=_params(),
    )(cmat, w_shard, b_shard)


def _adamw_w_ada(cmat, dmod_cols, w, m, v):
    def body(c_ref, d_ref, w_ref, m_ref, v_ref, g_ref, dl_ref, m2_ref, v2_ref):
        g = _dot_tn(c_ref[...].astype(BF16), d_ref[...].astype(BF16))
        g_ref[...] = g
        dl_ref[...], m2_ref[...], v2_ref[...] = _adamw(w_ref[...], g, m_ref[...], v_ref[...])

    return pl.pallas_call(
        body, name="adamw_w_ada", out_shape=[jax.ShapeDtypeStruct(w.shape, F32)] * 4,
        compiler_params=_params(),
    )(cmat, dmod_cols, w, m, v)


def _adamw_small(gathered, w, m, v):
    def body(g_ref, w_ref, m_ref, v_ref, go_ref, dl_ref, m2_ref, v2_ref):
        g = g_ref[0]
        for k in range(1, N_DEV):
            g = g + g_ref[k]
        go_ref[...] = g
        dl_ref[...], m2_ref[...], v2_ref[...] = _adamw(w_ref[...], g, m_ref[...], v_ref[...])

    return pl.pallas_call(
        body, name="adamw_small", out_shape=[jax.ShapeDtypeStruct(w.shape, F32)] * 4,
        compiler_params=_params(),
    )(gathered, w, m, v)


def _adamw_packed(recv, w, m, v, tr):
    R = w.shape[0]

    def body(r_ref, w_ref, m_ref, v_ref, g_ref, dl_ref, m2_ref, v2_ref):
        g = r_ref[0].astype(F32)
        for k in range(1, N_DEV):
            g = g + r_ref[k].astype(F32)
        g_ref[...] = g
        dl_ref[...], m2_ref[...], v2_ref[...] = _adamw(w_ref[...], g, m_ref[...], v_ref[...])

    row = pl.BlockSpec((tr, LANES), lambda i: (i, 0))
    return pl.pallas_call(
        body, name="adamw_packed", grid=(R // tr,),
        in_specs=[pl.BlockSpec((N_DEV, tr, LANES), lambda i: (0, i, 0)), row, row, row],
        out_specs=[row] * 4, out_shape=[jax.ShapeDtypeStruct((R, LANES), F32)] * 4,
        compiler_params=_params(("parallel",)),
    )(recv, w, m, v)


BIG = (("w_in", (1024, 4872), 1), ("w_branch_attn", (512, 1024), 1), ("w_branch_mlstm", (512, 1024), 1),
       ("w_out", (1024, 1024), 0), ("w_ffn_gate", (1024, D_FF), 1), ("w_ffn_up", (1024, D_FF), 1),
       ("w_ffn_down", (D_FF, 1024), 0), ("conv_w", (4, 1024), 1))
PACK_TILE = 496
PACK_ROWS = 31 * PACK_TILE
SMALL = (("b_ada", 6144), ("g_pre_mix", 1024), ("g_post_mix", 1024), ("g_pre_ffn", 1024), ("g_post_ffn", 1024),
         ("conv_b", 1024), ("mlstm_norm_w", 512), ("b_if", 8), ("attn_sinks", 8))
SMALL_W = 12032


def _shard_elems(shape):
    return shape[0] * shape[1] // N_DEV


def _pack_shards(shards, dtype):
    flat = jnp.concatenate([shards[n].reshape(-1).astype(dtype) for n, _, _ in BIG])
    return jnp.pad(flat, (0, PACK_ROWS * LANES - flat.shape[0])).reshape(PACK_ROWS, LANES)


def _unpack_shards(packed):
    flat = packed.reshape(-1)
    out, off = {}, 0
    for n, shape, ax in BIG:
        sshape = (shape[0], shape[1] // N_DEV) if ax == 1 else (shape[0] // N_DEV, shape[1])
        out[n] = flat[off:off + _shard_elems(shape)].reshape((1,) + sshape)
        off += _shard_elems(shape)
    return out


def _unpack_gathered(gathered):
    flat = gathered.reshape(N_DEV, -1)
    out, off = {}, 0
    for n, shape, ax in BIG:
        ne = _shard_elems(shape)
        blk = flat[:, off:off + ne]
        if ax == 1:
            out[n] = blk.reshape(N_DEV, shape[0], shape[1] // N_DEV).transpose(1, 0, 2).reshape(shape)
        else:
            out[n] = blk.reshape(shape)
        off += ne
    return out


def _pack_full_grads(grads):
    parts = []
    for n, shape, ax in BIG:
        g = grads[n].astype(BF16)
        if ax == 1:
            g = g.reshape(shape[0], N_DEV, shape[1] // N_DEV).transpose(1, 0, 2)
        parts.append(g.reshape(N_DEV, -1))
    flat = jnp.concatenate(parts, axis=1)
    return jnp.pad(flat, ((0, 0), (0, PACK_ROWS * LANES - flat.shape[1]))).reshape(N_DEV, PACK_ROWS, LANES)


def _pack_small(vals):
    parts = []
    for n, width in SMALL:
        a = vals[n].reshape(-1)
        parts.append(jnp.pad(a, (0, LANES - width)) if width < LANES else a)
    return jnp.concatenate(parts)


def _unpack_small(vec):
    out, off = {}, 0
    for n, width in SMALL:
        out[n] = vec[off:off + width].reshape(1, width)
        off += max(width, LANES)
    return out


IF_AT = A_W + M_W


def _regroup_w_in(w_in):
    return jnp.concatenate([w_in[:, :IF_AT + 8], jnp.zeros((w_in.shape[0], IF_W - 8), w_in.dtype), w_in[:, IF_AT + 8:]], axis=1)


def _ungroup_w_in(g_cat):
    return jnp.concatenate([g_cat[:, :IF_AT + 8], g_cat[:, IF_AT + IF_W:]], axis=1)


WEIGHT_NAMES = ("w_ada", "b_ada", "g_pre_mix", "g_post_mix", "w_in", "b_if", "conv_w", "conv_b", "attn_sinks",
                "mlstm_norm_w", "w_branch_attn", "w_branch_mlstm", "w_out", "g_pre_ffn", "g_post_ffn",
                "w_ffn_gate", "w_ffn_up", "w_ffn_down")


def kernel(x, c, positions, w_ada, b_ada, g_pre_mix, g_post_mix, w_in, b_if, conv_w, conv_b, attn_sinks, mlstm_norm_w, w_branch_attn, w_branch_mlstm, w_out, g_pre_ffn, g_post_ffn, w_ffn_gate, w_ffn_up, w_ffn_down, loss_target, m_w_ada, m_b_ada, m_g_pre_mix, m_g_post_mix, m_w_in, m_b_if, m_conv_w, m_conv_b, m_attn_sinks, m_mlstm_norm_w, m_w_branch_attn, m_w_branch_mlstm, m_w_out, m_g_pre_ffn, m_g_post_ffn, m_w_ffn_gate, m_w_ffn_up, m_w_ffn_down, v_w_ada, v_b_ada, v_g_pre_mix, v_g_post_mix, v_w_in, v_b_if, v_conv_w, v_conv_b, v_attn_sinks, v_mlstm_norm_w, v_w_branch_attn, v_w_branch_mlstm, v_w_out, v_g_pre_ffn, v_g_post_ffn, v_w_ffn_gate, v_w_ffn_up, v_w_ffn_down):
    given = dict(locals())
    W = {n: given[n][0] for n in WEIGHT_NAMES}
    M = {n: given["m_" + n][0] for n in WEIGHT_NAMES}
    V = {n: given["v_" + n][0] for n in WEIGHT_NAMES}
    me = 4 * lax.axis_index("x") + 2 * lax.axis_index("y") + lax.axis_index("c")

    cg = _small_exchange(jnp.broadcast_to(c, (8, D_MODEL)), True, "gather_c")
    cmat = cg[:, 0, :]
    ada_w = D_MODEL * 6 // N_DEV
    b_cols = lax.dynamic_slice(W["b_ada"], (me * ada_w,), (ada_w,)).reshape(1, ada_w)
    mod_part = _mod_partial(cmat, W["w_ada"], b_cols)
    mod_recv = _small_exchange(jnp.broadcast_to(mod_part[:, None, :], (N_DEV, 8, ada_w)), False, "scatter_mod")
    mod = mod_recv[:, 0, :].reshape(6, D_MODEL)

    gathered = _unpack_gathered(_all_gather_hbm(_pack_shards({n: W[n] for n, _, _ in BIG}, BF16)))
    gains = jnp.stack([W["g_pre_mix"], W["g_post_mix"], W["g_pre_ffn"], W["g_post_ffn"]])
    loss, grad_x, big, small = _local_step(
        x[0], loss_target[0], positions[0], mod, gains, _regroup_w_in(gathered["w_in"]), gathered["w_branch_attn"],
        gathered["w_branch_mlstm"], gathered["w_out"], gathered["w_ffn_gate"], gathered["w_ffn_up"],
        gathered["w_ffn_down"], gathered["conv_w"].astype(F32), W["conv_b"], W["b_if"], W["attn_sinks"],
        W["mlstm_norm_w"])

    full_grads = {"w_in": _ungroup_w_in(big["w_cat"]), "w_branch_attn": big["w_ba"], "w_branch_mlstm": big["w_bm"],
                  "w_out": big["w_out"], "w_ffn_gate": big["w_gate"], "w_ffn_up": big["w_up"],
                  "w_ffn_down": big["w_down"], "conv_w": small["dconv_w"]}
    recv = _scatter_hbm(_pack_full_grads(full_grads))
    packed = _adamw_packed(recv, _pack_shards({n: W[n] for n, _, _ in BIG}, F32),
                           _pack_shards({n: M[n] for n, _, _ in BIG}, F32),
                           _pack_shards({n: V[n] for n, _, _ in BIG}, F32), PACK_TILE)
    big_out = [_unpack_shards(p) for p in packed]

    part = {"b_ada": small["dmod"], "g_pre_mix": small["dgains"][0], "g_post_mix": small["dgains"][1],
            "g_pre_ffn": small["dgains"][2], "g_post_ffn": small["dgains"][3], "conv_b": small["dconv_b"],
            "mlstm_norm_w": small["dnorm_w"], "b_if": small["db_if"], "attn_sinks": small["dsinks"]}
    sg = _small_exchange(jnp.broadcast_to(_pack_small(part)[None, :], (8, SMALL_W)), True, "gather_small")
    rows8 = lambda d: jnp.broadcast_to(_pack_small(d)[None, :], (8, SMALL_W))
    small_out = [_unpack_small(o[0]) for o in _adamw_small(sg, rows8(W), rows8(M), rows8(V))]
    dmod_cols = lax.dynamic_slice(sg[:, 0, :], (0, me * ada_w), (N_DEV, ada_w))
    ada_out = _adamw_w_ada(cmat, dmod_cols, W["w_ada"], M["w_ada"], V["w_ada"])

    total = lax.psum(loss, ("x", "y", "c"))
    outs = [total, grad_x[None]]
    for k in range(4):
        for n in WEIGHT_NAMES:
            if n == "w_ada":
                outs.append(ada_out[k][None])
            elif n in big_out[k]:
                outs.append(big_out[k][n])
            else:
                outs.append(small_out[k][n])
    return tuple(outs)
```

```python
import functools

import jax
import jax.numpy as jnp
import numpy as np
from jax import lax
from jax.experimental import pallas as pl
from jax.experimental.pallas import tpu as pltpu

F32 = jnp.float32
BF16 = jnp.bfloat16

N_DEV = 8
D_MODEL = 1024
D_FF = 2816
N_Q_HEADS = 8
HEAD_DIM = 64
ATTN_BLOCK = 128
ROPE_THETA = 10000.0
MLSTM_HEADS = 4
MLSTM_HEAD_DIM = 128
MLSTM_CHUNK = 128
NORM_EPS = 1e-6
ADAM_LR = 0.001
ADAM_B1 = 0.9
ADAM_B2 = 0.999
ADAM_EPS = 1e-08
ADAM_WD = 0.01
ADAM_STEP = 10

ROW_TILE = 256
LANES = 128
NEG = -1e30
VMEM_LIMIT = 56 * 1024 * 1024

A_W = 768
M_W = 2048
IF_W = 128
G_W = 2048
CAT_W = A_W + M_W + IF_W + G_W

R_SHIFT_M, R_SCALE_M, R_GATE_M, R_SHIFT_F, R_SCALE_F, R_GATE_F = 0, 1, 2, 3, 4, 5
R_G_PRE_MIX, R_G_POST_MIX, R_G_PRE_FFN, R_G_POST_FFN = 6, 7, 8, 9


def _dot(a, b):
    return jnp.dot(a, b, preferred_element_type=F32)


def _dot_nt(a, b):
    return lax.dot_general(a, b, (((1,), (1,)), ((), ())), preferred_element_type=F32)


def _dot_tn(a, b):
    return lax.dot_general(a, b, (((0,), (0,)), ((), ())), preferred_element_type=F32)


def _sigmoid(x):
    return 1.0 / (1.0 + jnp.exp(-x))


def _colsum(x):
    return jnp.sum(x, axis=0, keepdims=True)


def _rowmean(x):
    return jnp.mean(x, axis=-1, keepdims=True)


def _params(sem=None, vmem=VMEM_LIMIT):
    kw = dict(vmem_limit_bytes=vmem)
    if sem is not None:
        kw["dimension_semantics"] = sem
    return pltpu.CompilerParams(**kw)


def _full(shape):
    nd = len(shape)
    return pl.BlockSpec(shape, lambda *_: (0,) * nd)


def _pre_proj(x, vecs, w_cat):
    S = x.shape[0]
    tm = ROW_TILE

    def body(x_ref, v_ref, w_ref, h_ref, pa_ref, pm_ref, pif_ref, pg_ref):
        xv = x_ref[...]
        r = lax.rsqrt(_rowmean(xv * xv) + NORM_EPS)
        h = (xv * r * v_ref[R_G_PRE_MIX:R_G_PRE_MIX + 1, :]) * (1.0 + v_ref[R_SCALE_M:R_SCALE_M + 1, :]) \
            + v_ref[R_SHIFT_M:R_SHIFT_M + 1, :]
        hb = h.astype(BF16)
        h_ref[...] = hb
        pa_ref[...] = _dot(hb, w_ref[:, 0:A_W])
        pm_ref[...] = _dot(hb, w_ref[:, A_W:A_W + M_W])
        pif_ref[...] = _dot(hb, w_ref[:, A_W + M_W:A_W + M_W + IF_W])
        pg_ref[...] = _dot(hb, w_ref[:, A_W + M_W + IF_W:CAT_W])

    row = lambda w: pl.BlockSpec((tm, w), lambda i: (i, 0))
    return pl.pallas_call(
        body, name="pre_proj", grid=(S // tm,),
        in_specs=[row(D_MODEL), _full(vecs.shape), _full(w_cat.shape)],
        out_specs=[row(D_MODEL), row(A_W), row(M_W), row(IF_W), row(G_W)],
        out_shape=[jax.ShapeDtypeStruct((S, D_MODEL), BF16), jax.ShapeDtypeStruct((S, A_W), F32),
                   jax.ShapeDtypeStruct((S, M_W), F32), jax.ShapeDtypeStruct((S, IF_W), F32),
                   jax.ShapeDtypeStruct((S, G_W), F32)],
        compiler_params=_params(("parallel",)),
    )(x, vecs, w_cat)


def _mix_fwd(x, ya, ym, pg, vecs, w_ba, w_bm, w_out):
    S = x.shape[0]
    tm = ROW_TILE

    def body(x_ref, ya_ref, ym_ref, pg_ref, v_ref, wba_ref, wbm_ref, wout_ref,
             x1_ref, merged_ref, mix_ref, pa_ref, pb_ref):
        pa = _dot(ya_ref[...], wba_ref[...])
        pb = _dot(ym_ref[...], wbm_ref[...])
        merged = _sigmoid(pg_ref[:, 0:D_MODEL]) * pa + _sigmoid(pg_ref[:, D_MODEL:G_W]) * pb
        mb = merged.astype(BF16)
        mix = _dot(mb, wout_ref[...])
        r = lax.rsqrt(_rowmean(mix * mix) + NORM_EPS)
        x1_ref[...] = x_ref[...] + v_ref[R_GATE_M:R_GATE_M + 1, :] * (mix * r * v_ref[R_G_POST_MIX:R_G_POST_MIX + 1, :])
        merged_ref[...] = mb
        mix_ref[...] = mix
        pa_ref[...] = pa.astype(BF16)
        pb_ref[...] = pb.astype(BF16)

    row = lambda w: pl.BlockSpec((tm, w), lambda i: (i, 0))
    sd = lambda w, dt: jax.ShapeDtypeStruct((S, w), dt)
    return pl.pallas_call(
        body, name="mix_fwd", grid=(S // tm,),
        in_specs=[row(D_MODEL), row(512), row(512), row(G_W), _full(vecs.shape), _full(w_ba.shape),
                  _full(w_bm.shape), _full(w_out.shape)],
        out_specs=[row(D_MODEL)] * 5,
        out_shape=[sd(D_MODEL, F32), sd(D_MODEL, BF16), sd(D_MODEL, F32), sd(D_MODEL, BF16), sd(D_MODEL, BF16)],
        compiler_params=_params(("parallel",)),
    )(x, ya, ym, pg, vecs, w_ba, w_bm, w_out)


def _ffn_fwd_bwd(x1, tgt, vecs, w_gate, w_up, w_down):
    S = x1.shape[0]
    tm = ROW_TILE

    def body(x1_ref, tgt_ref, v_ref, wg_hbm, wu_hbm, wd_hbm,
             dx1_ref, h2_ref, hid_ref, da_ref, du_ref, dff_ref, acc_ref, loss_ref,
             wg, wu, wd, sem):
        i = pl.program_id(0)

        @pl.when(i == 0)
        def _():
            cps = [pltpu.make_async_copy(wg_hbm, wg, sem.at[0]), pltpu.make_async_copy(wu_hbm, wu, sem.at[1]),
                   pltpu.make_async_copy(wd_hbm, wd, sem.at[2])]
            for cp in cps:
                cp.start()
            for cp in cps:
                cp.wait()
            acc_ref[...] = jnp.zeros_like(acc_ref)
            loss_ref[...] = jnp.zeros_like(loss_ref)

        vrow = lambda r: v_ref[r:r + 1, :]
        x1v = x1_ref[...]
        r3 = lax.rsqrt(_rowmean(x1v * x1v) + NORM_EPS)
        x1hat = x1v * r3
        xn3 = x1hat * vrow(R_G_PRE_FFN)
        h2b = (xn3 * (1.0 + vrow(R_SCALE_F)) + vrow(R_SHIFT_F)).astype(BF16)
        h2_ref[...] = h2b
        a = _dot_nt(h2b, wg[...])
        u = _dot_nt(h2b, wu[...])
        sg = _sigmoid(a)
        sil = a * sg
        hidb = (sil * u).astype(BF16)
        hid_ref[...] = hidb
        ff = _dot(hidb, wd[...])
        r4 = lax.rsqrt(_rowmean(ff * ff) + NORM_EPS)
        ffhat = ff * r4
        n4 = ffhat * vrow(R_G_POST_FFN)
        err = x1v + vrow(R_GATE_F) * n4 - tgt_ref[...]
        loss_ref[...] += jnp.sum(err * err) * (0.5 / D_MODEL)
        dy = err * (1.0 / D_MODEL)
        acc_ref[0:1, :] += _colsum(dy * n4)
        dn4 = dy * vrow(R_GATE_F)
        acc_ref[1:2, :] += _colsum(dn4 * ffhat)
        dffhat = dn4 * vrow(R_G_POST_FFN)
        dffb = (r4 * (dffhat - ffhat * _rowmean(dffhat * ffhat))).astype(BF16)
        dff_ref[...] = dffb
        dhid = _dot_nt(dffb, wd[...])
        dub = (dhid * sil).astype(BF16)
        dab = (dhid * u * (sg * (1.0 + a * (1.0 - sg)))).astype(BF16)
        da_ref[...] = dab
        du_ref[...] = dub
        dh2 = _dot(dab, wg[...]) + _dot(dub, wu[...])
        acc_ref[2:3, :] += _colsum(dh2 * xn3)
        acc_ref[3:4, :] += _colsum(dh2)
        dxn3 = dh2 * (1.0 + vrow(R_SCALE_F))
        acc_ref[4:5, :] += _colsum(dxn3 * x1hat)
        dx1hat = dxn3 * vrow(R_G_PRE_FFN)
        dx1_ref[...] = dy + r3 * (dx1hat - x1hat * _rowmean(dx1hat * x1hat))

    row = lambda w: pl.BlockSpec((tm, w), lambda i: (i, 0))
    sd = lambda w, dt: jax.ShapeDtypeStruct((S, w), dt)
    anyspec = pl.BlockSpec(memory_space=pl.ANY)
    return pl.pallas_call(
        body, name="ffn_fwd_bwd", grid=(S // tm,),
        in_specs=[row(D_MODEL), row(D_MODEL), _full(vecs.shape), anyspec, anyspec, anyspec],
        out_specs=[row(D_MODEL), row(D_MODEL), row(D_FF), row(D_FF), row(D_FF), row(D_MODEL),
                   _full((8, D_MODEL)), _full((8, LANES))],
        out_shape=[sd(D_MODEL, F32), sd(D_MODEL, BF16), sd(D_FF, BF16), sd(D_FF, BF16), sd(D_FF, BF16),
                   sd(D_MODEL, BF16), jax.ShapeDtypeStruct((8, D_MODEL), F32), jax.ShapeDtypeStruct((8, LANES), F32)],
        scratch_shapes=[pltpu.VMEM(w_gate.shape, BF16), pltpu.VMEM(w_up.shape, BF16), pltpu.VMEM(w_down.shape, BF16),
                        pltpu.SemaphoreType.DMA((3,))],
        compiler_params=_params(("arbitrary",)),
    )(x1, tgt, vecs, w_gate, w_up, w_down)


def _mix_bwd(dx1, mix, pa, pb, pg, vecs, w_ba, w_bm, w_out):
    S = dx1.shape[0]
    tm = ROW_TILE

    def body(dx1_ref, mix_ref, pa_ref, pb_ref, pg_ref, v_ref, wba_ref, wbm_ref, wout_ref,
             dmix_ref, dpa_ref, dpb_ref, dg_ref, dya_ref, dym_ref, acc_ref):
        i = pl.program_id(0)

        @pl.when(i == 0)
        def _():
            acc_ref[...] = jnp.zeros_like(acc_ref)

        vrow = lambda r: v_ref[r:r + 1, :]
        dx1v = dx1_ref[...]
        mix = mix_ref[...]
        r2 = lax.rsqrt(_rowmean(mix * mix) + NORM_EPS)
        mixhat = mix * r2
        acc_ref[0:1, :] += _colsum(dx1v * (mixhat * vrow(R_G_POST_MIX)))
        dn2 = dx1v * vrow(R_GATE_M)
        acc_ref[1:2, :] += _colsum(dn2 * mixhat)
        dmixhat = dn2 * vrow(R_G_POST_MIX)
        dmixb = (r2 * (dmixhat - mixhat * _rowmean(dmixhat * mixhat))).astype(BF16)
        dmix_ref[...] = dmixb
        dmerged = _dot_nt(dmixb, wout_ref[...])
        sa = _sigmoid(pg_ref[:, 0:D_MODEL])
        sm = _sigmoid(pg_ref[:, D_MODEL:G_W])
        dpab = (dmerged * sa).astype(BF16)
        dpbb = (dmerged * sm).astype(BF16)
        dpa_ref[...] = dpab
        dpb_ref[...] = dpbb
        dg_ref[:, 0:D_MODEL] = (dmerged * pa_ref[...].astype(F32) * (sa * (1.0 - sa))).astype(BF16)
        dg_ref[:, D_MODEL:G_W] = (dmerged * pb_ref[...].astype(F32) * (sm * (1.0 - sm))).astype(BF16)
        dya_ref[...] = _dot_nt(dpab, wba_ref[...])
        dym_ref[...] = _dot_nt(dpbb, wbm_ref[...])

    row = lambda w: pl.BlockSpec((tm, w), lambda i: (i, 0))
    sd = lambda w, dt: jax.ShapeDtypeStruct((S, w), dt)
    return pl.pallas_call(
        body, name="mix_bwd", grid=(S // tm,),
        in_specs=[row(D_MODEL), row(D_MODEL), row(D_MODEL), row(D_MODEL), row(G_W), _full(vecs.shape),
                  _full(w_ba.shape), _full(w_bm.shape), _full(w_out.shape)],
        out_specs=[row(D_MODEL), row(D_MODEL), row(D_MODEL), row(G_W), row(512), row(512), _full((8, D_MODEL))],
        out_shape=[sd(D_MODEL, BF16), sd(D_MODEL, BF16), sd(D_MODEL, BF16), sd(G_W, BF16), sd(512, F32), sd(512, F32),
                   jax.ShapeDtypeStruct((8, D_MODEL), F32)],
        compiler_params=_params(("arbitrary",)),
    )(dx1, mix, pa, pb, pg, vecs, w_ba, w_bm, w_out)


def _pre_bwd(dproj, x, dx1, vecs, w_cat):
    S = x.shape[0]
    tm = ROW_TILE

    def body(dp_ref, x_ref, dx1_ref, v_ref, w_ref, dx_ref, acc_ref):
        i = pl.program_id(0)

        @pl.when(i == 0)
        def _():
            acc_ref[...] = jnp.zeros_like(acc_ref)

        vrow = lambda r: v_ref[r:r + 1, :]
        dh = _dot_nt(dp_ref[...], w_ref[...])
        xv = x_ref[...]
        r1 = lax.rsqrt(_rowmean(xv * xv) + NORM_EPS)
        xhat = xv * r1
        acc_ref[0:1, :] += _colsum(dh * (xhat * vrow(R_G_PRE_MIX)))
        acc_ref[1:2, :] += _colsum(dh)
        dxn = dh * (1.0 + vrow(R_SCALE_M))
        acc_ref[2:3, :] += _colsum(dxn * xhat)
        dxhat = dxn * vrow(R_G_PRE_MIX)
        dx_ref[...] = dx1_ref[...] + r1 * (dxhat - xhat * _rowmean(dxhat * xhat))

    row = lambda w: pl.BlockSpec((tm, w), lambda i: (i, 0))
    return pl.pallas_call(
        body, name="pre_bwd", grid=(S // tm,),
        in_specs=[row(CAT_W), row(D_MODEL), row(D_MODEL), _full(vecs.shape), _full(w_cat.shape)],
        out_specs=[row(D_MODEL), _full((8, D_MODEL))],
        out_shape=[jax.ShapeDtypeStruct((S, D_MODEL), F32), jax.ShapeDtypeStruct((8, D_MODEL), F32)],
        compiler_params=_params(("arbitrary",)),
    )(dproj, x, dx1, vecs, w_cat)


def _matmul_tn(a, b, tn, name, ts=512):
    S, K = a.shape
    N = b.shape[1]
    n_s = S // ts

    def body(a_ref, b_ref, o_ref, acc_ref):
        s = pl.program_id(1)

        @pl.when(s == 0)
        def _():
            acc_ref[...] = jnp.zeros_like(acc_ref)

        acc_ref[...] += _dot_tn(a_ref[...], b_ref[...])

        @pl.when(s == n_s - 1)
        def _():
            o_ref[...] = acc_ref[...].astype(BF16)

    return pl.pallas_call(
        body, name=name, grid=(N // tn, n_s),
        in_specs=[pl.BlockSpec((ts, K), lambda j, s: (s, 0)), pl.BlockSpec((ts, tn), lambda j, s: (s, j))],
        out_specs=pl.BlockSpec((K, tn), lambda j, s: (0, j)),
        out_shape=jax.ShapeDtypeStruct((K, N), BF16),
        scratch_shapes=[pltpu.VMEM((K, tn), F32)],
        compiler_params=_params(("parallel", "arbitrary")),
    )(a, b)


def _rope_swap(t):
    lane = lax.broadcasted_iota(jnp.int32, t.shape, 1)
    first = (lane & (HEAD_DIM - 1)) < (HEAD_DIM // 2)
    return jnp.where(first, pltpu.roll(t, LANES - HEAD_DIM // 2, 1), pltpu.roll(t, HEAD_DIM // 2, 1))


def _rope(t, cos, sin_signed):
    return t * cos + _rope_swap(t) * sin_signed


def _rope_t(d, cos, sin_signed):
    return d * cos + _rope_swap(d * sin_signed)


def _to_kv_lanes(chunk, p, h):
    lane = lax.broadcasted_iota(jnp.int32, chunk.shape, 1)
    src = chunk if p == h else pltpu.roll(chunk, HEAD_DIM, 1)
    return jnp.where((lane >> 6) == h, src, jnp.zeros_like(src))


def _from_kv_lanes(o_a, o_b, h):
    lane = lax.broadcasted_iota(jnp.int32, o_a.shape, 1)
    a = o_a if h == 0 else pltpu.roll(o_a, HEAD_DIM, 1)
    b = o_b if h == 1 else pltpu.roll(o_b, HEAD_DIM, 1)
    return jnp.where(lane < HEAD_DIM, a, b)


def _band_mask(n):
    blk = ATTN_BLOCK
    qi = lax.broadcasted_iota(jnp.int32, (4 * blk, 2 * blk), 0) & (blk - 1)
    kj = lax.broadcasted_iota(jnp.int32, (4 * blk, 2 * blk), 1)
    return (kj > qi) & (kj <= qi + blk) & ((n > 0) | (kj >= blk))


def _stack_heads(chunks, h, dtype):
    parts = []
    for g in range(4):
        j = 4 * h + g
        parts.append(_to_kv_lanes(chunks[j // 2], j % 2, h))
    return jnp.concatenate(parts, axis=0).astype(dtype)


def _attn_fwd(pa, cos, sin, sinks):
    S = pa.shape[0]
    blk = ATTN_BLOCK
    nb = S // blk

    def body(sink_ref, cur_ref, prev_ref, cos_ref, sin_ref, cosp_ref, sinp_ref,
             ya_ref, qr_ref, kr_ref, vb_ref, lse_ref):
        n = pl.program_id(0)
        cos_c, sin_c = cos_ref[...], sin_ref[...]
        qch = [_rope(cur_ref[:, c * LANES:(c + 1) * LANES], cos_c, sin_c) * (HEAD_DIM ** -0.5) for c in range(4)]
        for c in range(4):
            qr_ref[:, c * LANES:(c + 1) * LANES] = qch[c].astype(BF16)
        k_cur = _rope(cur_ref[:, 512:640], cos_c, sin_c).astype(BF16)
        k_prev = _rope(prev_ref[:, 0:LANES], cosp_ref[...], sinp_ref[...]).astype(BF16)
        v_cur = cur_ref[:, 640:768].astype(BF16)
        v_prev = prev_ref[:, LANES:2 * LANES].astype(BF16)
        kr_ref[...] = k_cur
        vb_ref[...] = v_cur
        K = jnp.concatenate([k_prev, k_cur], axis=0)
        V = jnp.concatenate([v_prev, v_cur], axis=0)
        mask = _band_mask(n)
        rowg = lax.broadcasted_iota(jnp.int32, (4 * blk, 1), 0) >> 7
        lane = lax.broadcasted_iota(jnp.int32, (blk, LANES), 1)
        lse_tile = jnp.zeros((blk, LANES), F32)
        outs = []
        for h in range(2):
            qs = _stack_heads(qch, h, BF16)
            s = jnp.where(mask, _dot_nt(qs, K), NEG)
            sink = jnp.zeros((4 * blk, 1), F32)
            for g in range(4):
                sink = jnp.where(rowg == g, sink_ref[4 * h + g], sink)
            m = jnp.maximum(jnp.max(s, axis=1, keepdims=True), sink)
            p = jnp.exp(s - m)
            den = jnp.sum(p, axis=1, keepdims=True) + jnp.exp(sink - m)
            o = _dot((p / den).astype(BF16), V)
            lse = m + jnp.log(den)
            for g in range(4):
                outs.append(o[g * blk:(g + 1) * blk, :])
                lse_tile = jnp.where(lane == 4 * h + g, lse[g * blk:(g + 1) * blk, :], lse_tile)
        for c in range(4):
            ya_ref[:, c * LANES:(c + 1) * LANES] = _from_kv_lanes(outs[2 * c], outs[2 * c + 1], c // 2).astype(BF16)
        lse_ref[...] = lse_tile

    prev = lambda n: jnp.maximum(n - 1, 0)
    sd = lambda w, dt: jax.ShapeDtypeStruct((S, w), dt)
    return pl.pallas_call(
        body, name="attn_fwd", grid=(nb,),
        in_specs=[pl.BlockSpec(memory_space=pltpu.SMEM),
                  pl.BlockSpec((blk, A_W), lambda n: (n, 0)),
                  pl.BlockSpec((blk, 256), lambda n: (prev(n), 2)),
                  pl.BlockSpec((blk, LANES), lambda n: (n, 0)), pl.BlockSpec((blk, LANES), lambda n: (n, 0)),
                  pl.BlockSpec((blk, LANES), lambda n: (prev(n), 0)), pl.BlockSpec((blk, LANES), lambda n: (prev(n), 0))],
        out_specs=[pl.BlockSpec((blk, 512), lambda n: (n, 0)), pl.BlockSpec((blk, 512), lambda n: (n, 0)),
                   pl.BlockSpec((blk, LANES), lambda n: (n, 0)), pl.BlockSpec((blk, LANES), lambda n: (n, 0)),
                   pl.BlockSpec((blk, LANES), lambda n: (n, 0))],
        out_shape=[sd(512, BF16), sd(512, BF16), sd(LANES, BF16), sd(LANES, BF16), sd(LANES, F32)],
        compiler_params=_params(("parallel",)),
    )(sinks, pa, pa, cos, sin, cos, sin)


def _attn_bwd(dya, qr, kr, vb, lse, cos, sin, sinks):
    S = dya.shape[0]
    blk = ATTN_BLOCK
    nb = S // blk

    def body(sink_ref, dya_ref, qr_ref, kc_ref, kp_ref, vc_ref, vp_ref, lse_ref, cos_ref, sin_ref, cosp_ref, sinp_ref,
             dq_ref, dkv_ref, dsink_ref, ck, cv):
        n = pl.program_id(0)

        @pl.when(n == 0)
        def _():
            ck[...] = jnp.zeros_like(ck)
            cv[...] = jnp.zeros_like(cv)
            dsink_ref[...] = jnp.zeros_like(dsink_ref)

        @pl.when(n < nb)
        def _():
            K = jnp.concatenate([kp_ref[...], kc_ref[...]], axis=0)
            V = jnp.concatenate([vp_ref[...], vc_ref[...]], axis=0)
            qch = [qr_ref[:, c * LANES:(c + 1) * LANES] for c in range(4)]
            dch = [dya_ref[:, c * LANES:(c + 1) * LANES] for c in range(4)]
            lse_tile = lse_ref[...]
            mask = _band_mask(n)
            rowg = lax.broadcasted_iota(jnp.int32, (4 * blk, 1), 0) >> 7
            lane8 = lax.broadcasted_iota(jnp.int32, (8, LANES), 1)
            dk_acc = jnp.zeros((2 * blk, LANES), F32)
            dv_acc = jnp.zeros((2 * blk, LANES), F32)
            dsink = jnp.zeros((8, LANES), F32)
            dqs = []
            for h in range(2):
                qs = _stack_heads(qch, h, BF16)
                dos = _stack_heads(dch, h, BF16)
                lse_col = jnp.concatenate([lse_tile[:, 4 * h + g:4 * h + g + 1] for g in range(4)], axis=0)
                p = jnp.where(mask, jnp.exp(_dot_nt(qs, K) - lse_col), 0.0)
                dp = _dot_nt(dos, V)
                delta = jnp.sum(p * dp, axis=1, keepdims=True)
                dsb = (p * (dp - delta)).astype(BF16)
                dq = _dot(dsb, K)
                dk_acc = dk_acc + _dot_tn(dsb, qs)
                dv_acc = dv_acc + _dot_tn(p.astype(BF16), dos)
                sink = jnp.zeros((4 * blk, 1), F32)
                for g in range(4):
                    sink = jnp.where(rowg == g, sink_ref[4 * h + g], sink)
                ps_delta = jnp.exp(sink - lse_col) * delta
                for g in range(4):
                    dqs.append(dq[g * blk:(g + 1) * blk, :])
                    dsink = jnp.where(lane8 == 4 * h + g, dsink - jnp.sum(ps_delta[g * blk:(g + 1) * blk, :]), dsink)
            dsink_ref[...] += dsink
            cos_c, sin_c = cos_ref[...], sin_ref[...]
            for c in range(4):
                dqc = _from_kv_lanes(dqs[2 * c], dqs[2 * c + 1], c // 2) * (HEAD_DIM ** -0.5)
                dq_ref[:, c * LANES:(c + 1) * LANES] = _rope_t(dqc, cos_c, sin_c).astype(BF16)
            dkv_ref[:, 0:LANES] = _rope_t(dk_acc[0:blk, :] + ck[...], cosp_ref[...], sinp_ref[...]).astype(BF16)
            dkv_ref[:, LANES:2 * LANES] = (dv_acc[0:blk, :] + cv[...]).astype(BF16)
            ck[...] = dk_acc[blk:2 * blk, :]
            cv[...] = dv_acc[blk:2 * blk, :]

        @pl.when(n == nb)
        def _():
            dkv_ref[:, 0:LANES] = _rope_t(ck[...], cosp_ref[...], sinp_ref[...]).astype(BF16)
            dkv_ref[:, LANES:2 * LANES] = cv[...].astype(BF16)

    cur = lambda n: jnp.minimum(n, nb - 1)
    prev = lambda n: jnp.maximum(n - 1, 0)
    bs = lambda w, f: pl.BlockSpec((blk, w), lambda n: (f(n), 0))
    return pl.pallas_call(
        body, name="attn_bwd", grid=(nb + 1,),
        in_specs=[pl.BlockSpec(memory_space=pltpu.SMEM),
                  bs(512, cur), bs(512, cur), bs(LANES, cur), bs(LANES, prev), bs(LANES, cur), bs(LANES, prev),
                  bs(LANES, cur), bs(LANES, cur), bs(LANES, cur), bs(LANES, prev), bs(LANES, prev)],
        out_specs=[bs(512, cur), bs(256, prev), _full((8, LANES))],
        out_shape=[jax.ShapeDtypeStruct((S, 512), BF16), jax.ShapeDtypeStruct((S, 256), BF16),
                   jax.ShapeDtypeStruct((8, LANES), F32)],
        scratch_shapes=[pltpu.VMEM((blk, LANES), F32), pltpu.VMEM((blk, LANES), F32)],
        compiler_params=_params(("arbitrary",)),
    )(sinks, dya, qr, kr, kr, vb, vb, lse, cos, sin, cos, sin)


def _split3(x):
    hi = x.astype(BF16)
    r1 = x - hi.astype(F32)
    mid = r1.astype(BF16)
    lo = (r1 - mid.astype(F32)).astype(BF16)
    return hi, mid, lo


def _tri_matmul(tri_b, x):
    hi, mid, lo = _split3(x)
    return _dot(tri_b, hi) + _dot(tri_b, mid) + _dot(tri_b, lo)


def _log_sigmoid(x):
    return jnp.minimum(x, 0.0) - jnp.log(1.0 + jnp.exp(-jnp.abs(x)))


def _conv_silu_fwd(cur, prev, cw_ref, first):
    L = cur.shape[0]
    row = lax.broadcasted_iota(jnp.int32, cur.shape, 0)
    prev = jnp.where(first, jnp.zeros_like(prev), prev)
    shifted = [cur]
    for k in range(1, 4):
        shifted.append(jnp.where(row < k, pltpu.roll(prev, k, 0), pltpu.roll(cur, k, 0)))
    z = cw_ref[4:5, :]
    for k in range(3, -1, -1):
        z = z + shifted[k] * cw_ref[3 - k:4 - k, :]
    return z, shifted


def _mlstm_head_fwd(qh, kh, vh, i_col, b_col, C_prev, n_prev, m_prev, tri, eye):
    L = qh.shape[0]
    col2row = lambda x: jnp.sum(jnp.where(eye, x, 0.0), axis=0, keepdims=True)
    b_row = col2row(b_col)
    i_row = col2row(i_col)
    bl = b_col[L - 1:L, :]
    Dm = jnp.where(tri, b_col - b_row + i_row, NEG)
    inter = b_col + m_prev
    m_t = jnp.maximum(inter, jnp.max(Dm, axis=1, keepdims=True))
    W = jnp.exp(Dm - m_t)
    e_t = jnp.exp(inter - m_t)
    qb, kb, vb = qh.astype(BF16), kh.astype(BF16), vh.astype(BF16)
    Sc = _dot_nt(qb, kb) * W
    P1 = _dot(qb, C_prev.astype(BF16))
    num = _dot(Sc.astype(BF16), vb) + e_t * P1
    qn = jnp.sum(qh * n_prev, axis=1, keepdims=True)
    den = jnp.sum(Sc, axis=1, keepdims=True) + e_t * qn
    floor = jnp.exp(-m_t)
    g = jnp.maximum(jnp.abs(den), floor)
    hv = num / g
    a_col = bl - b_col + i_col
    m_new = jnp.maximum(bl + m_prev, jnp.max(a_col, axis=0, keepdims=True))
    dec = jnp.exp(bl + m_prev - m_new)
    u_col = jnp.exp(a_col - m_new)
    return dict(W=W, e_t=e_t, qb=qb, kb=kb, vb=vb, Sc=Sc, P1=P1, qn=qn, den=den, floor=floor, g=g, hv=hv,
                m_new=m_new, dec=dec, u_col=u_col)


def _mlstm_fwd(pm, pif, cw, sv):
    S = pm.shape[0]
    L = MLSTM_CHUNK
    nc = S // L
    HD = MLSTM_HEAD_DIM
    W4 = MLSTM_HEADS * HD

    def body(cur_ref, prev_ref, pif_ref, cw_ref, sv_ref, ym_ref, cst_ref, nst_ref, C, nm):
        c = pl.program_id(0)

        @pl.when(c == 0)
        def _():
            C[...] = jnp.zeros_like(C)
            nm[...] = jnp.zeros_like(nm)

        z, _ = _conv_silu_fwd(cur_ref[:, 0:2 * W4], prev_ref[...], cw_ref, c == 0)
        qk = z * _sigmoid(z)
        gt = pif_ref[...] + sv_ref[1:2, 0:LANES]
        r_i = lax.broadcasted_iota(jnp.int32, (L, L), 0)
        c_i = lax.broadcasted_iota(jnp.int32, (L, L), 1)
        tri = c_i <= r_i
        eye = c_i == r_i
        b_all = _tri_matmul(tri.astype(BF16), _log_sigmoid(gt))
        nst_ref[0] = nm[...]
        for h in range(MLSTM_HEADS):
            sl = slice(h * HD, (h + 1) * HD)
            qh = qk[:, sl]
            kh = qk[:, W4 + h * HD:W4 + (h + 1) * HD] * (HD ** -0.5)
            vh = cur_ref[:, 2 * W4 + h * HD:2 * W4 + (h + 1) * HD]
            n_prev = nm[h:h + 1, :]
            m_prev = nm[4 + h:5 + h, 0:1]
            C_prev = C[h]
            cst_ref[0, h] = C_prev
            f = _mlstm_head_fwd(qh, kh, vh, gt[:, h:h + 1], b_all[:, 4 + h:5 + h], C_prev, n_prev, m_prev, tri, eye)
            hv = f["hv"]
            xc = hv - _rowmean(hv)
            hhat = xc * lax.rsqrt(_rowmean(xc * xc) + NORM_EPS)
            so = _sigmoid(cur_ref[:, 3 * W4 + h * HD:3 * W4 + (h + 1) * HD])
            ym_ref[:, sl] = (so * hhat * sv_ref[0:1, sl]).astype(BF16)
            kw = kh * f["u_col"]
            C[h] = f["dec"] * C_prev + _dot_tn(kw.astype(BF16), f["vb"])
            nm[h:h + 1, :] = f["dec"] * n_prev + _colsum(kw)
            nm[4 + h:5 + h, :] = jnp.broadcast_to(f["m_new"], (1, LANES))

    prev = lambda c: jnp.maximum(c - 1, 0)
    return pl.pallas_call(
        body, name="mlstm_fwd", grid=(nc,),
        in_specs=[pl.BlockSpec((L, M_W), lambda c: (c, 0)), pl.BlockSpec((L, 2 * W4), lambda c: (prev(c), 0)),
                  pl.BlockSpec((L, IF_W), lambda c: (c, 0)), _full(cw.shape), _full(sv.shape)],
        out_specs=[pl.BlockSpec((L, W4), lambda c: (c, 0)),
                   pl.BlockSpec((1, MLSTM_HEADS, HD, HD), lambda c: (c, 0, 0, 0)),
                   pl.BlockSpec((1, 8, LANES), lambda c: (c, 0, 0))],
        out_shape=[jax.ShapeDtypeStruct((S, W4), BF16), jax.ShapeDtypeStruct((nc, MLSTM_HEADS, HD, HD), F32),
                   jax.ShapeDtypeStruct((nc, 8, LANES), F32)],
        scratch_shapes=[pltpu.VMEM((MLSTM_HEADS, HD, HD), F32), pltpu.VMEM((8, LANES), F32)],
        compiler_params=_params(("arbitrary",)),
    )(pm, pm, pif, cw, sv)


def _mlstm_bwd(pm, pif, cw, sv, dym, cst, nst):
    S = pm.shape[0]
    L = MLSTM_CHUNK
    nc = S // L
    HD = MLSTM_HEAD_DIM
    W4 = MLSTM_HEADS * HD

    def body(cur_ref, prev_ref, pif_ref, cw_ref, sv_ref, dym_ref, cst_ref, nst_ref,
             dm_ref, dif_ref, dcw_ref, dsv_ref, dC, dn, dz_next, dqk):
        r = pl.program_id(0)
        c = nc - 1 - r

        @pl.when(r == 0)
        def _():
            dC[...] = jnp.zeros_like(dC)
            dn[...] = jnp.zeros_like(dn)
            dz_next[...] = jnp.zeros_like(dz_next)
            dcw_ref[...] = jnp.zeros_like(dcw_ref)
            dsv_ref[...] = jnp.zeros_like(dsv_ref)

        z, shifted = _conv_silu_fwd(cur_ref[:, 0:2 * W4], prev_ref[...], cw_ref, c == 0)
        sgz = _sigmoid(z)
        qk = z * sgz
        gt = pif_ref[...] + sv_ref[1:2, 0:LANES]
        r_i = lax.broadcasted_iota(jnp.int32, (L, L), 0)
        c_i = lax.broadcasted_iota(jnp.int32, (L, L), 1)
        tri = c_i <= r_i
        eye = c_i == r_i
        b_all = _tri_matmul(tri.astype(BF16), _log_sigmoid(gt))
        lane = lax.broadcasted_iota(jnp.int32, (L, LANES), 1)
        rowl = lax.broadcasted_iota(jnp.int32, (L, 1), 0)
        nmv = nst_ref[0]
        di_tile = jnp.zeros((L, LANES), F32)
        db_tile = jnp.zeros((L, LANES), F32)
        for h in range(MLSTM_HEADS):
            sl = slice(h * HD, (h + 1) * HD)
            qh = qk[:, sl]
            kh = qk[:, W4 + h * HD:W4 + (h + 1) * HD] * (HD ** -0.5)
            vh = cur_ref[:, 2 * W4 + h * HD:2 * W4 + (h + 1) * HD]
            n_prev = nmv[h:h + 1, :]
            m_prev = nmv[4 + h:5 + h, 0:1]
            C_prev = cst_ref[0, h]
            f = _mlstm_head_fwd(qh, kh, vh, gt[:, h:h + 1], b_all[:, 4 + h:5 + h], C_prev, n_prev, m_prev, tri, eye)
            hv, g, den, e_t, u_col, dec = f["hv"], f["g"], f["den"], f["e_t"], f["u_col"], f["dec"]
            qb, kb, vb, Sc, W = f["qb"], f["kb"], f["vb"], f["Sc"], f["W"]
            xc = hv - _rowmean(hv)
            rstd = lax.rsqrt(_rowmean(xc * xc) + NORM_EPS)
            hhat = xc * rstd
            wn = sv_ref[0:1, sl]
            so = _sigmoid(cur_ref[:, 3 * W4 + h * HD:3 * W4 + (h + 1) * HD])
            dy = dym_ref[:, sl]
            dm_ref[:, 3 * W4 + h * HD:3 * W4 + (h + 1) * HD] = (dy * hhat * wn * (so * (1.0 - so))).astype(BF16)
            dln = dy * so
            dsv_ref[0:1, sl] += _colsum(dln * hhat)
            dhhat = dln * wn
            dh = rstd * (dhhat - _rowmean(dhhat) - hhat * _rowmean(dhhat * hhat))
            dnum = dh / g
            active = jnp.abs(den) > f["floor"]
            dden = jnp.where(active, -jnp.sum(dh * hv, axis=1, keepdims=True) / g * jnp.where(den >= 0.0, 1.0, -1.0), 0.0)
            dnumb = dnum.astype(BF16)
            dSc = _dot_nt(dnumb, vb) + dden
            dA = (dSc * W).astype(BF16)
            G = dSc * Sc
            Gr = jnp.sum(G, axis=1, keepdims=True)
            Gc = jnp.sum(jnp.where(eye, _colsum(G), 0.0), axis=1, keepdims=True)
            dCn = dC[h]
            dCnb = dCn.astype(BF16)
            dn_new = dn[h:h + 1, :]
            kdC = _dot(kb, dCnb)
            vdC = _dot_nt(vb, dCnb)
            dv = _dot_tn(Sc.astype(BF16), dnumb) + u_col * kdC
            Cb = C_prev.astype(BF16)
            dq = _dot(dA, kb) + e_t * _dot_nt(dnumb, Cb) + (e_t * dden) * n_prev
            dk = _dot_tn(dA, qb) + u_col * (vdC + dn_new)
            de = jnp.sum(f["P1"] * dnum, axis=1, keepdims=True) + dden * f["qn"]
            E = de * e_t
            du = jnp.sum(kdC * vh, axis=1, keepdims=True) + jnp.sum(kh * dn_new, axis=1, keepdims=True)
            U = du * u_col
            ddec = jnp.sum(dCn * C_prev) + jnp.sum(dn_new * n_prev)
            dbl = ddec * dec + jnp.sum(U, axis=0, keepdims=True)
            db = Gr + E - Gc - U + jnp.where(rowl == L - 1, dbl, 0.0)
            di_tile = jnp.where(lane == h, Gc + U, di_tile)
            db_tile = jnp.where(lane == 4 + h, db, db_tile)
            dC[h] = dec * dCn + _dot_tn((qh * e_t).astype(BF16), dnumb)
            dn[h:h + 1, :] = dec * dn_new + _colsum((e_t * dden) * qh)
            dqk[:, sl] = dq
            dqk[:, W4 + h * HD:W4 + (h + 1) * HD] = dk * (HD ** -0.5)
            dm_ref[:, 2 * W4 + h * HD:2 * W4 + (h + 1) * HD] = dv.astype(BF16)
        dlf = _tri_matmul((r_i <= c_i).astype(BF16), db_tile)
        dif = jnp.where(lane < 4, di_tile, jnp.where(lane < 8, dlf * (1.0 - _sigmoid(gt)), 0.0))
        dif_ref[...] = dif.astype(BF16)
        dsv_ref[1:2, 0:LANES] += _colsum(dif)
        dz = dqk[...] * (sgz * (1.0 + z * (1.0 - sgz)))
        dcw_ref[4:5, :] += _colsum(dz)
        row = lax.broadcasted_iota(jnp.int32, dz.shape, 0)
        dzn = dz_next[...]
        du_in = dz * cw_ref[3:4, :]
        dcw_ref[3:4, :] += _colsum(dz * shifted[0])
        for k in range(1, 4):
            dcw_ref[3 - k:4 - k, :] += _colsum(dz * shifted[k])
            up = jnp.where(row >= L - k, pltpu.roll(dzn, L - k, 0), pltpu.roll(dz, L - k, 0))
            du_in = du_in + up * cw_ref[3 - k:4 - k, :]
        dz_next[...] = dz
        dm_ref[:, 0:2 * W4] = du_in.astype(BF16)

    cidx = lambda r: nc - 1 - r
    prev = lambda r: jnp.maximum(nc - 2 - r, 0)
    return pl.pallas_call(
        body, name="mlstm_bwd", grid=(nc,),
        in_specs=[pl.BlockSpec((L, M_W), lambda r: (cidx(r), 0)), pl.BlockSpec((L, 2 * W4), lambda r: (prev(r), 0)),
                  pl.BlockSpec((L, IF_W), lambda r: (cidx(r), 0)), _full(cw.shape), _full(sv.shape),
                  pl.BlockSpec((L, W4), lambda r: (cidx(r), 0)),
                  pl.BlockSpec((1, MLSTM_HEADS, HD, HD), lambda r: (cidx(r), 0, 0, 0)),
                  pl.BlockSpec((1, 8, LANES), lambda r: (cidx(r), 0, 0))],
        out_specs=[pl.BlockSpec((L, M_W), lambda r: (cidx(r), 0)), pl.BlockSpec((L, IF_W), lambda r: (cidx(r), 0)),
                   _full((8, 2 * W4)), _full((8, W4))],
        out_shape=[jax.ShapeDtypeStruct((S, M_W), BF16), jax.ShapeDtypeStruct((S, IF_W), BF16),
                   jax.ShapeDtypeStruct((8, 2 * W4), F32), jax.ShapeDtypeStruct((8, W4), F32)],
        scratch_shapes=[pltpu.VMEM((MLSTM_HEADS, HD, HD), F32), pltpu.VMEM((8, LANES), F32),
                        pltpu.VMEM((L, 2 * W4), F32), pltpu.VMEM((L, 2 * W4), F32)],
        compiler_params=_params(("arbitrary",)),
    )(pm, pm, pif, cw, sv, dym, cst, nst)


def _rope_tables(positions):
    half = HEAD_DIM // 2
    inv_freq = ROPE_THETA ** (-2.0 * jnp.arange(half, dtype=F32) / HEAD_DIM)
    ang = positions.astype(F32)[:, None] * inv_freq
    cos = jnp.tile(jnp.cos(ang), (1, LANES // half))
    sign = jnp.tile(jnp.concatenate([-jnp.ones((half,), F32), jnp.ones((half,), F32)]), LANES // HEAD_DIM)
    sin = jnp.tile(jnp.sin(ang), (1, LANES // half)) * sign
    return cos, sin


def _local_step(x, tgt, positions, mod, gains, w_cat, w_ba, w_bm, w_out, w_gate, w_up, w_down,
                conv_w, conv_b, b_if, sinks, norm_w):
    S = x.shape[0]
    vecs = jnp.concatenate([mod, gains, jnp.zeros((6, D_MODEL), F32)], axis=0)
    cw = jnp.concatenate([conv_w, conv_b.reshape(1, -1), jnp.zeros((3, 2 * 512), F32)], axis=0)
    sv = jnp.zeros((8, 512), F32).at[0].set(norm_w).at[1, 0:8].set(b_if)
    cos, sin = _rope_tables(positions)

    h, pa, pm, pif, pg = _pre_proj(x, vecs, w_cat)
    ya, qr, kr, vb, lse = _attn_fwd(pa, cos, sin, sinks)
    ym, cst, nst = _mlstm_fwd(pm, pif, cw, sv)
    x1, merged, mix, pba, pbm = _mix_fwd(x, ya, ym, pg, vecs, w_ba, w_bm, w_out)
    dx1, h2, hid, da, du, dff, acc_f, loss = _ffn_fwd_bwd(x1, tgt, vecs, w_gate, w_up, w_down)
    dmix, dpa, dpb, dg, dya, dym, acc_m = _mix_bwd(dx1, mix, pba, pbm, pg, vecs, w_ba, w_bm, w_out)
    dq, dkv, dsink = _attn_bwd(dya, qr, kr, vb, lse, cos, sin, sinks)
    dm, dif, dcw, dsv = _mlstm_bwd(pm, pif, cw, sv, dym, cst, nst)
    dproj = jnp.concatenate([dq, dkv, dm, dif, dg], axis=1)
    grad_x, acc_p = _pre_bwd(dproj, x, dx1, vecs, w_cat)

    g_w_cat = _matmul_tn(h, dproj, 1664, "dw_in")
    g_w_ba = _matmul_tn(ya, dpa, 1024, "dw_branch_attn")
    g_w_bm = _matmul_tn(ym, dpb, 1024, "dw_branch_mlstm")
    g_w_out = _matmul_tn(merged, dmix, 1024, "dw_out")
    g_w_gate = _matmul_tn(da, h2, 1024, "dw_ffn_gate")
    g_w_up = _matmul_tn(du, h2, 1024, "dw_ffn_up")
    g_w_down = _matmul_tn(hid, dff, 1024, "dw_ffn_down")

    dmod = jnp.stack([acc_p[1], acc_p[0], acc_m[0], acc_f[3], acc_f[2], acc_f[0]])
    dgains = jnp.stack([acc_p[2], acc_m[1], acc_f[4], acc_f[1]])
    small = dict(dmod=dmod, dgains=dgains, dconv_w=dcw[0:4], dconv_b=dcw[4], db_if=dsv[1, 0:8],
                 dsinks=dsink[0, 0:8], dnorm_w=dsv[0])
    big = dict(w_cat=g_w_cat, w_ba=g_w_ba, w_bm=g_w_bm, w_out=g_w_out, w_gate=g_w_gate, w_up=g_w_up, w_down=g_w_down)
    return loss[0, 0], grad_x, big, small


MESH_ID = pl.DeviceIdType.MESH


def _mesh_pos():
    return lax.axis_index("x"), lax.axis_index("y"), lax.axis_index("c")


def _flip(v, bit):
    return 1 - v if bit else v


def _relations():
    return [((r >> 2) & 1, (r >> 1) & 1, r & 1) for r in range(1, N_DEV)]


def _small_exchange(p, gather, name):
    V = p.shape[-1]

    def body(p_ref, out_ref, send_sems, recv_sems):
        x, y, c = _mesh_pos()
        me = 4 * x + 2 * y + c
        out_ref[me] = p_ref[...] if gather else p_ref[me]
        peers = []
        for dx, dy, dc in _relations():
            px, py, pc = _flip(x, dx), _flip(y, dy), _flip(c, dc)
            peers.append(((px, py, pc), 4 * px + 2 * py + pc))

        def copy(k, landing):
            peer, pid = peers[k]
            return pltpu.make_async_remote_copy(
                src_ref=p_ref if gather else p_ref.at[pid], dst_ref=out_ref.at[landing],
                send_sem=send_sems.at[k], recv_sem=recv_sems.at[k], device_id=peer, device_id_type=MESH_ID)

        sends = [copy(k, me) for k in range(N_DEV - 1)]
        for cp in sends:
            cp.start()
        for k in range(N_DEV - 1):
            copy(k, peers[k][1]).wait_recv()
        for cp in sends:
            cp.wait_send()

    vm = pl.BlockSpec(memory_space=pltpu.VMEM)
    return pl.pallas_call(
        body, name=name, in_specs=[vm], out_specs=vm,
        out_shape=jax.ShapeDtypeStruct((N_DEV, 8, V), F32),
        scratch_shapes=[pltpu.SemaphoreType.DMA((N_DEV - 1,)), pltpu.SemaphoreType.DMA((N_DEV - 1,))],
        compiler_params=pltpu.CompilerParams(vmem_limit_bytes=VMEM_LIMIT),
    )(p)


def _all_gather_hbm(shards):
    n = len(shards)

    def body(*refs):
        p_refs, out_refs = refs[:n], refs[n:2 * n]
        send_sems, recv_sems, local_sems = refs[2 * n:]
        x, y, c = _mesh_pos()
        me, sibling = (x, y, c), (x, y, 1 - c)
        chips = [(1 - x, y), (x, 1 - y), (1 - x, 1 - y)]

        def copy(a, k, block, to, own=False):
            slot = out_refs[a].at[4 * block[0] + 2 * block[1] + block[2]]
            return pltpu.make_async_remote_copy(
                src_ref=p_refs[a] if own else slot, dst_ref=slot,
                send_sem=send_sems.at[a, k], recv_sem=recv_sems.at[a, k], device_id=to, device_id_type=MESH_ID)

        mine = [pltpu.make_async_copy(p_refs[a], out_refs[a].at[4 * x + 2 * y + c], local_sems.at[a]) for a in range(n)]
        for cp in mine:
            cp.start()
        first = []
        for a in range(n):
            first.append(copy(a, 0, me, sibling, own=True))
            first += [copy(a, 1 + j, me, (*chip, c), own=True) for j, chip in enumerate(chips)]
        for cp in first:
            cp.start()
        passed = []
        for j, chip in enumerate(chips):
            for a in range(n):
                copy(a, 1 + j, (*chip, c), me).wait_recv()
                passed.append(copy(a, 4 + j, (*chip, c), sibling))
                passed[-1].start()
        for a in range(n):
            copy(a, 0, sibling, me).wait_recv()
            for j, chip in enumerate(chips):
                copy(a, 4 + j, (*chip, 1 - c), me).wait_recv()
        for cp in first + passed:
            cp.wait_send()
        for cp in mine:
            cp.wait()

    hbm = pl.BlockSpec(memory_space=pl.ANY)
    return pl.pallas_call(
        body, name="gather_weights", in_specs=[hbm] * n, out_specs=[hbm] * n,
        out_shape=[jax.ShapeDtypeStruct((N_DEV,) + s.shape, s.dtype) for s in shards],
        scratch_shapes=[pltpu.SemaphoreType.DMA((n, N_DEV - 1)), pltpu.SemaphoreType.DMA((n, N_DEV - 1)),
                        pltpu.SemaphoreType.DMA((n,))],
    )(*shards)


def _scatter_hbm(pieces):
    n = len(pieces)

    def body(*refs):
        p_refs, out_refs = refs[:n], refs[n:2 * n]
        send_sems, recv_sems, local_sems = refs[2 * n:]
        x, y, c = _mesh_pos()
        me = 4 * x + 2 * y + c
        mine = [pltpu.make_async_copy(p_refs[a].at[me], out_refs[a].at[me], local_sems.at[a]) for a in range(n)]
        for cp in mine:
            cp.start()
        peers = []
        for dx, dy, dc in _relations():
            px, py, pc = _flip(x, dx), _flip(y, dy), _flip(c, dc)
            peers.append(((px, py, pc), 4 * px + 2 * py + pc))

        def copy(a, k, landing):
            peer, pid = peers[k]
            return pltpu.make_async_remote_copy(
                src_ref=p_refs[a].at[pid], dst_ref=out_refs[a].at[landing],
                send_sem=send_sems.at[a, k], recv_sem=recv_sems.at[a, k], device_id=peer, device_id_type=MESH_ID)

        sends = [copy(a, k, me) for a in range(n) for k in range(N_DEV - 1)]
        for cp in sends:
            cp.start()
        for a in range(n):
            for k in range(N_DEV - 1):
                copy(a, k, peers[k][1]).wait_recv()
        for cp in sends:
            cp.wait_send()
        for cp in mine:
            cp.wait()

    hbm = pl.BlockSpec(memory_space=pl.ANY)
    return pl.pallas_call(
        body, name="scatter_grads", in_specs=[hbm] * n, out_specs=[hbm] * n,
        out_shape=[jax.ShapeDtypeStruct(p.shape, p.dtype) for p in pieces],
        scratch_shapes=[pltpu.SemaphoreType.DMA((n, N_DEV - 1)), pltpu.SemaphoreType.DMA((n, N_DEV - 1)),
                        pltpu.SemaphoreType.DMA((n,))],
    )(*pieces)


def _adamw(w, g, m, v):
    m2 = ADAM_B1 * m + (1.0 - ADAM_B1) * g
    v2 = ADAM_B2 * v + (1.0 - ADAM_B2) * (g * g)
    m_hat = m2 / (1.0 - ADAM_B1 ** ADAM_STEP)
    v_hat = v2 / (1.0 - ADAM_B2 ** ADAM_STEP)
    delta = -ADAM_LR * (m_hat / (jnp.sqrt(v_hat) + ADAM_EPS) + ADAM_WD * w)
    return delta, m2, v2


def _mod_partial(cmat, w_shard, b_shard):
    def body(c_ref, w_ref, b_ref, o_ref):
        o_ref[...] = _dot(c_ref[...].astype(BF16), w_ref[...].astype(BF16)) + b_ref[...]

    return pl.pallas_call(
        body, name="mod_partial", out_shape=jax.ShapeDtypeStruct((N_DEV, w_shard.shape[1]), F32),
        compiler_params=_params(),
    )(cmat, w_shard, b_shard)


def _adamw_w_ada(cmat, dmod_cols, w, m, v):
    def body(c_ref, d_ref, w_ref, m_ref, v_ref, g_ref, dl_ref, m2_ref, v2_ref):
        g = _dot_tn(c_ref[...].astype(BF16), d_ref[...].astype(BF16))
        g_ref[...] = g
        dl_ref[...], m2_ref[...], v2_ref[...] = _adamw(w_ref[...], g, m_ref[...], v_ref[...])

    return pl.pallas_call(
        body, name="adamw_w_ada", out_shape=[jax.ShapeDtypeStruct(w.shape, F32)] * 4,
        compiler_params=_params(),
    )(cmat, dmod_cols, w, m, v)


def _adamw_small(gathered, w, m, v):
    def body(g_ref, w_ref, m_ref, v_ref, go_ref, dl_ref, m2_ref, v2_ref):
        g = g_ref[0]
        for k in range(1, N_DEV):
            g = g + g_ref[k]
        go_ref[...] = g
        dl_ref[...], m2_ref[...], v2_ref[...] = _adamw(w_ref[...], g, m_ref[...], v_ref[...])

    return pl.pallas_call(
        body, name="adamw_small", out_shape=[jax.ShapeDtypeStruct(w.shape, F32)] * 4,
        compiler_params=_params(),
    )(gathered, w, m, v)


def _row_tile(rows):
    return rows // 4 if rows >= 512 else rows


def _sum_partials(r_ref):
    g = r_ref[0].astype(F32)
    for k in range(1, N_DEV):
        g = g + r_ref[k].astype(F32)
    return g


def _adamw_sum(recv, w, m, v, name):
    r, cdim = w.shape
    tr = _row_tile(r)

    def body(r_ref, w_ref, m_ref, v_ref, g_ref, dl_ref, m2_ref, v2_ref):
        g = _sum_partials(r_ref)
        g_ref[...] = g
        dl_ref[...], m2_ref[...], v2_ref[...] = _adamw(w_ref[...], g, m_ref[...], v_ref[...])

    row = pl.BlockSpec((tr, cdim), lambda i: (i, 0))
    return pl.pallas_call(
        body, name=name, grid=(r // tr,),
        in_specs=[pl.BlockSpec((N_DEV, tr, cdim), lambda i: (0, i, 0)), row, row, row],
        out_specs=[row] * 4, out_shape=[jax.ShapeDtypeStruct((r, cdim), F32)] * 4,
        compiler_params=_params(("parallel",)),
    )(recv, w, m, v)


def _sum8(recv, name):
    _, r, cdim = recv.shape
    tr = _row_tile(r)

    def body(r_ref, g_ref):
        g_ref[...] = _sum_partials(r_ref)

    return pl.pallas_call(
        body, name=name, grid=(r // tr,),
        in_specs=[pl.BlockSpec((N_DEV, tr, cdim), lambda i: (0, i, 0))],
        out_specs=pl.BlockSpec((tr, cdim), lambda i: (i, 0)), out_shape=jax.ShapeDtypeStruct((r, cdim), F32),
        compiler_params=_params(("parallel",)),
    )(recv)


def _adamw_plain(g, w, m, v, name):
    r, cdim = w.shape
    tr = _row_tile(r)

    def body(g_ref, w_ref, m_ref, v_ref, dl_ref, m2_ref, v2_ref):
        dl_ref[...], m2_ref[...], v2_ref[...] = _adamw(w_ref[...], g_ref[...], m_ref[...], v_ref[...])

    row = pl.BlockSpec((tr, cdim), lambda i: (i, 0))
    return pl.pallas_call(
        body, name=name, grid=(r // tr,), in_specs=[row] * 4, out_specs=[row] * 3,
        out_shape=[jax.ShapeDtypeStruct((r, cdim), F32)] * 3,
        compiler_params=_params(("parallel",)),
    )(g, w, m, v)


IN_SHARD = 609
IN_SHARD_PAD = 640
IF_AT = A_W + M_W


def _regrouped(u):
    return u if u < IF_AT + 8 else u + (IF_W - 8)


def _selection(k, rows, row0, transpose):
    shape = (rows, IN_SHARD_PAD) if transpose else (IN_SHARD_PAD, rows)
    l = lax.broadcasted_iota(jnp.int32, shape, 1 if transpose else 0)
    r = lax.broadcasted_iota(jnp.int32, shape, 0 if transpose else 1) + row0
    u = l + IN_SHARD * k
    ru = u + jnp.where(u >= IF_AT + 8, IF_W - 8, 0)
    return ((ru == r) & (l < IN_SHARD)).astype(BF16)


def _regroup_w_in(g):
    def body(g_ref, o_ref):
        for cb in range(CAT_W // LANES):
            r0 = cb * LANES
            acc = jnp.zeros((D_MODEL, LANES), F32)
            for k in range(N_DEV):
                lo, hi = _regrouped(IN_SHARD * k), _regrouped(IN_SHARD * k + IN_SHARD - 1)
                if hi >= r0 and lo < r0 + LANES:
                    acc = acc + _dot(g_ref[k], _selection(k, LANES, r0, False))
            o_ref[:, r0:r0 + LANES] = acc.astype(BF16)

    return pl.pallas_call(
        body, name="regroup_w_in", out_shape=jax.ShapeDtypeStruct((D_MODEL, CAT_W), BF16),
        compiler_params=_params(),
    )(g)


def _ungroup_w_in(g_cat):
    def body(g_ref, o_ref):
        for k in range(N_DEV):
            lo, hi = _regrouped(IN_SHARD * k), _regrouped(IN_SHARD * k + IN_SHARD - 1)
            w0, w1 = lo // LANES * LANES, (hi // LANES + 1) * LANES
            o_ref[k] = _dot(g_ref[:, w0:w1], _selection(k, w1 - w0, w0, True)).astype(BF16)

    return pl.pallas_call(
        body, name="ungroup_w_in", out_shape=jax.ShapeDtypeStruct((N_DEV, D_MODEL, IN_SHARD_PAD), BF16),
        compiler_params=_params(),
    )(g_cat)


SMALL = (("b_ada", 6144), ("g_pre_mix", 1024), ("g_post_mix", 1024), ("g_pre_ffn", 1024), ("g_post_ffn", 1024),
         ("conv_b", 1024), ("mlstm_norm_w", 512), ("b_if", 8), ("attn_sinks", 8))
SMALL_W = 12032


def _pack_small(vals):
    parts = []
    for n, width in SMALL:
        a = vals[n].reshape(-1)
        parts.append(jnp.pad(a, (0, LANES - width)) if width < LANES else a)
    return jnp.concatenate(parts)


def _unpack_small(vec):
    out, off = {}, 0
    for n, width in SMALL:
        out[n] = vec[off:off + width].reshape(1, width)
        off += max(width, LANES)
    return out


WEIGHT_NAMES = ("w_ada", "b_ada", "g_pre_mix", "g_post_mix", "w_in", "b_if", "conv_w", "conv_b", "attn_sinks",
                "mlstm_norm_w", "w_branch_attn", "w_branch_mlstm", "w_out", "g_pre_ffn", "g_post_ffn",
                "w_ffn_gate", "w_ffn_up", "w_ffn_down")


def kernel(x, c, positions, w_ada, b_ada, g_pre_mix, g_post_mix, w_in, b_if, conv_w, conv_b, attn_sinks, mlstm_norm_w, w_branch_attn, w_branch_mlstm, w_out, g_pre_ffn, g_post_ffn, w_ffn_gate, w_ffn_up, w_ffn_down, loss_target, m_w_ada, m_b_ada, m_g_pre_mix, m_g_post_mix, m_w_in, m_b_if, m_conv_w, m_conv_b, m_attn_sinks, m_mlstm_norm_w, m_w_branch_attn, m_w_branch_mlstm, m_w_out, m_g_pre_ffn, m_g_post_ffn, m_w_ffn_gate, m_w_ffn_up, m_w_ffn_down, v_w_ada, v_b_ada, v_g_pre_mix, v_g_post_mix, v_w_in, v_b_if, v_conv_w, v_conv_b, v_attn_sinks, v_mlstm_norm_w, v_w_branch_attn, v_w_branch_mlstm, v_w_out, v_g_pre_ffn, v_g_post_ffn, v_w_ffn_gate, v_w_ffn_up, v_w_ffn_down):
    given = dict(locals())
    W = {n: given[n][0] for n in WEIGHT_NAMES}
    M = {n: given["m_" + n][0] for n in WEIGHT_NAMES}
    V = {n: given["v_" + n][0] for n in WEIGHT_NAMES}
    me = 4 * lax.axis_index("x") + 2 * lax.axis_index("y") + lax.axis_index("c")

    cg = _small_exchange(jnp.broadcast_to(c, (8, D_MODEL)), True, "gather_c")
    cmat = cg[:, 0, :]
    ada_w = D_MODEL * 6 // N_DEV
    b_cols = lax.dynamic_slice(W["b_ada"], (me * ada_w,), (ada_w,)).reshape(1, ada_w)
    mod_part = _mod_partial(cmat, W["w_ada"], b_cols)
    mod_recv = _small_exchange(jnp.broadcast_to(mod_part[:, None, :], (N_DEV, 8, ada_w)), False, "scatter_mod")
    mod = mod_recv[:, 0, :].reshape(6, D_MODEL)

    ff_sh = D_FF // N_DEV
    g_in, g_ba, g_bm, g_out, g_gate, g_up, g_down, g_conv = _all_gather_hbm([
        jnp.pad(W["w_in"], ((0, 0), (0, IN_SHARD_PAD - IN_SHARD))).astype(BF16),
        W["w_branch_attn"].astype(BF16), W["w_branch_mlstm"].astype(BF16), W["w_out"].astype(BF16),
        W["w_ffn_gate"].T.astype(BF16), W["w_ffn_up"].T.astype(BF16), W["w_ffn_down"].astype(BF16),
        jnp.pad(W["conv_w"], ((0, 4), (0, 0)))])
    cols = lambda g: g.transpose(1, 0, 2).reshape(g.shape[1], N_DEV * g.shape[2])
    gains = jnp.stack([W["g_pre_mix"], W["g_post_mix"], W["g_pre_ffn"], W["g_post_ffn"]])
    loss, grad_x, big, small = _local_step(
        x[0], loss_target[0], positions[0], mod, gains, _regroup_w_in(g_in), cols(g_ba), cols(g_bm),
        g_out.reshape(D_MODEL, D_MODEL), g_gate.reshape(D_FF, D_MODEL), g_up.reshape(D_FF, D_MODEL),
        g_down.reshape(D_FF, D_MODEL), cols(g_conv)[0:4], W["conv_b"], W["b_if"], W["attn_sinks"], W["mlstm_norm_w"])

    pieces = lambda g, n: g.reshape(g.shape[0], N_DEV, n).transpose(1, 0, 2)
    r_in, r_ba, r_bm, r_out, r_gate, r_up, r_down, r_conv = _scatter_hbm([
        _ungroup_w_in(big["w_cat"]), pieces(big["w_ba"], 128), pieces(big["w_bm"], 128),
        big["w_out"].reshape(N_DEV, D_MODEL // N_DEV, D_MODEL), big["w_gate"].reshape(N_DEV, ff_sh, D_MODEL),
        big["w_up"].reshape(N_DEV, ff_sh, D_MODEL), big["w_down"].reshape(N_DEV, ff_sh, D_MODEL),
        jnp.pad(pieces(small["dconv_w"], 128), ((0, 0), (0, 4), (0, 0)))])
    big_out = [{} for _ in range(4)]

    def put(n, res):
        for k in range(4):
            big_out[k][n] = res[k][None]

    for n, r in (("w_branch_attn", r_ba), ("w_branch_mlstm", r_bm), ("w_out", r_out), ("w_ffn_down", r_down)):
        put(n, _adamw_sum(r, W[n], M[n], V[n], "adamw_" + n))
    pad4 = lambda a: jnp.pad(a, ((0, 4), (0, 0)))
    put("conv_w", [o[0:4] for o in _adamw_sum(r_conv, pad4(W["conv_w"]), pad4(M["conv_w"]), pad4(V["conv_w"]),
                                                 "adamw_conv_w")])
    for n, g in (("w_in", _sum8(r_in, "sum_w_in")[:, 0:IN_SHARD]), ("w_ffn_gate", _sum8(r_gate, "sum_w_ffn_gate").T),
                 ("w_ffn_up", _sum8(r_up, "sum_w_ffn_up").T)):
        put(n, [g] + list(_adamw_plain(g, W[n], M[n], V[n], "adamw_" + n)))

    part = {"b_ada": small["dmod"], "g_pre_mix": small["dgains"][0], "g_post_mix": small["dgains"][1],
            "g_pre_ffn": small["dgains"][2], "g_post_ffn": small["dgains"][3], "conv_b": small["dconv_b"],
            "mlstm_norm_w": small["dnorm_w"], "b_if": small["db_if"], "attn_sinks": small["dsinks"]}
    sg = _small_exchange(jnp.broadcast_to(_pack_small(part)[None, :], (8, SMALL_W)), True, "gather_small")
    rows8 = lambda d: jnp.broadcast_to(_pack_small(d)[None, :], (8, SMALL_W))
    small_out = [_unpack_small(o[0]) for o in _adamw_small(sg, rows8(W), rows8(M), rows8(V))]
    dmod_cols = lax.dynamic_slice(sg[:, 0, :], (0, me * ada_w), (N_DEV, ada_w))
    ada_out = _adamw_w_ada(cmat, dmod_cols, W["w_ada"], M["w_ada"], V["w_ada"])

    total = lax.psum(loss, ("x", "y", "c"))
    outs = [total, grad_x[None]]
    for k in range(4):
        for n in WEIGHT_NAMES:
            if n == "w_ada":
                outs.append(ada_out[k][None])
            elif n in big_out[k]:
                outs.append(big_out[k][n])
            else:
                outs.append(small_out[k][n])
    return tuple(outs)
```

```python
import functools

import jax
import jax.numpy as jnp
import numpy as np
from jax import lax
from jax.experimental import pallas as pl
from jax.experimental.pallas import tpu as pltpu

F32 = jnp.float32
BF16 = jnp.bfloat16

N_DEV = 8
D_MODEL = 1024
D_FF = 2816
N_Q_HEADS = 8
HEAD_DIM = 64
ATTN_BLOCK = 128
ROPE_THETA = 10000.0
MLSTM_HEADS = 4
MLSTM_HEAD_DIM = 128
MLSTM_CHUNK = 128
NORM_EPS = 1e-6
ADAM_LR = 0.001
ADAM_B1 = 0.9
ADAM_B2 = 0.999
ADAM_EPS = 1e-08
ADAM_WD = 0.01
ADAM_STEP = 10

ROW_TILE = 256
LANES = 128
NEG = -1e30
VMEM_LIMIT = 56 * 1024 * 1024

A_W = 768
M_W = 2048
IF_W = 128
G_W = 2048
CAT_W = A_W + M_W + IF_W + G_W

R_SHIFT_M, R_SCALE_M, R_GATE_M, R_SHIFT_F, R_SCALE_F, R_GATE_F = 0, 1, 2, 3, 4, 5
R_G_PRE_MIX, R_G_POST_MIX, R_G_PRE_FFN, R_G_POST_FFN = 6, 7, 8, 9


def _dot(a, b):
    return jnp.dot(a, b, preferred_element_type=F32)


def _dot_nt(a, b):
    return lax.dot_general(a, b, (((1,), (1,)), ((), ())), preferred_element_type=F32)


def _dot_tn(a, b):
    return lax.dot_general(a, b, (((0,), (0,)), ((), ())), preferred_element_type=F32)


def _sigmoid(x):
    return 1.0 / (1.0 + jnp.exp(-x))


def _colsum(x):
    return jnp.sum(x, axis=0, keepdims=True)


def _rowmean(x):
    return jnp.mean(x, axis=-1, keepdims=True)


def _params(sem=None, vmem=VMEM_LIMIT):
    kw = dict(vmem_limit_bytes=vmem)
    if sem is not None:
        kw["dimension_semantics"] = sem
    return pltpu.CompilerParams(**kw)


def _full(shape):
    nd = len(shape)
    return pl.BlockSpec(shape, lambda *_: (0,) * nd)


def _pre_proj(x, vecs, w_cat):
    S = x.shape[0]
    tm = ROW_TILE

    def body(x_ref, v_ref, w_ref, h_ref, pa_ref, pm_ref, pif_ref, pg_ref):
        xv = x_ref[...]
        r = lax.rsqrt(_rowmean(xv * xv) + NORM_EPS)
        h = (xv * r * v_ref[R_G_PRE_MIX:R_G_PRE_MIX + 1, :]) * (1.0 + v_ref[R_SCALE_M:R_SCALE_M + 1, :]) \
            + v_ref[R_SHIFT_M:R_SHIFT_M + 1, :]
        hb = h.astype(BF16)
        h_ref[...] = hb
        pa_ref[...] = _dot(hb, w_ref[:, 0:A_W])
        pm_ref[...] = _dot(hb, w_ref[:, A_W:A_W + M_W])
        pif_ref[...] = _dot(hb, w_ref[:, A_W + M_W:A_W + M_W + IF_W])
        pg_ref[...] = _dot(hb, w_ref[:, A_W + M_W + IF_W:CAT_W])

    row = lambda w: pl.BlockSpec((tm, w), lambda i: (i, 0))
    return pl.pallas_call(
        body, name="pre_proj", grid=(S // tm,),
        in_specs=[row(D_MODEL), _full(vecs.shape), _full(w_cat.shape)],
        out_specs=[row(D_MODEL), row(A_W), row(M_W), row(IF_W), row(G_W)],
        out_shape=[jax.ShapeDtypeStruct((S, D_MODEL), BF16), jax.ShapeDtypeStruct((S, A_W), F32),
                   jax.ShapeDtypeStruct((S, M_W), F32), jax.ShapeDtypeStruct((S, IF_W), F32),
                   jax.ShapeDtypeStruct((S, G_W), F32)],
        compiler_params=_params(("parallel",)),
    )(x, vecs, w_cat)


def _mix_fwd(x, ya, ym, pg, vecs, w_ba, w_bm, w_out):
    S = x.shape[0]
    tm = ROW_TILE

    def body(x_ref, ya_ref, ym_ref, pg_ref, v_ref, wba_ref, wbm_ref, wout_ref,
             x1_ref, merged_ref, mix_ref, pa_ref, pb_ref):
        pa = _dot(ya_ref[...], wba_ref[...])
        pb = _dot(ym_ref[...], wbm_ref[...])
        merged = _sigmoid(pg_ref[:, 0:D_MODEL]) * pa + _sigmoid(pg_ref[:, D_MODEL:G_W]) * pb
        mb = merged.astype(BF16)
        mix = _dot(mb, wout_ref[...])
        r = lax.rsqrt(_rowmean(mix * mix) + NORM_EPS)
        x1_ref[...] = x_ref[...] + v_ref[R_GATE_M:R_GATE_M + 1, :] * (mix * r * v_ref[R_G_POST_MIX:R_G_POST_MIX + 1, :])
        merged_ref[...] = mb
        mix_ref[...] = mix
        pa_ref[...] = pa.astype(BF16)
        pb_ref[...] = pb.astype(BF16)

    row = lambda w: pl.BlockSpec((tm, w), lambda i: (i, 0))
    sd = lambda w, dt: jax.ShapeDtypeStruct((S, w), dt)
    return pl.pallas_call(
        body, name="mix_fwd", grid=(S // tm,),
        in_specs=[row(D_MODEL), row(512), row(512), row(G_W), _full(vecs.shape), _full(w_ba.shape),
                  _full(w_bm.shape), _full(w_out.shape)],
        out_specs=[row(D_MODEL)] * 5,
        out_shape=[sd(D_MODEL, F32), sd(D_MODEL, BF16), sd(D_MODEL, F32), sd(D_MODEL, BF16), sd(D_MODEL, BF16)],
        compiler_params=_params(("parallel",)),
    )(x, ya, ym, pg, vecs, w_ba, w_bm, w_out)


def _ffn_fwd_bwd(x1, tgt, vecs, w_gate, w_up, w_down):
    S = x1.shape[0]
    tm = ROW_TILE

    def body(x1_ref, tgt_ref, v_ref, wg_hbm, wu_hbm, wd_hbm,
             dx1_ref, h2_ref, hid_ref, da_ref, du_ref, dff_ref, acc_ref, loss_ref,
             wg, wu, wd, sem):
        i = pl.program_id(0)

        @pl.when(i == 0)
        def _():
            cps = [pltpu.make_async_copy(wg_hbm, wg, sem.at[0]), pltpu.make_async_copy(wu_hbm, wu, sem.at[1]),
                   pltpu.make_async_copy(wd_hbm, wd, sem.at[2])]
            for cp in cps:
                cp.start()
            for cp in cps:
                cp.wait()
            acc_ref[...] = jnp.zeros_like(acc_ref)
            loss_ref[...] = jnp.zeros_like(loss_ref)

        vrow = lambda r: v_ref[r:r + 1, :]
        x1v = x1_ref[...]
        r3 = lax.rsqrt(_rowmean(x1v * x1v) + NORM_EPS)
        x1hat = x1v * r3
        xn3 = x1hat * vrow(R_G_PRE_FFN)
        h2b = (xn3 * (1.0 + vrow(R_SCALE_F)) + vrow(R_SHIFT_F)).astype(BF16)
        h2_ref[...] = h2b
        a = _dot_nt(h2b, wg[...])
        u = _dot_nt(h2b, wu[...])
        sg = _sigmoid(a)
        sil = a * sg
        hidb = (sil * u).astype(BF16)
        hid_ref[...] = hidb
        ff = _dot(hidb, wd[...])
        r4 = lax.rsqrt(_rowmean(ff * ff) + NORM_EPS)
        ffhat = ff * r4
        n4 = ffhat * vrow(R_G_POST_FFN)
        err = x1v + vrow(R_GATE_F) * n4 - tgt_ref[...]
        loss_ref[...] += jnp.sum(err * err) * (0.5 / D_MODEL)
        dy = err * (1.0 / D_MODEL)
        acc_ref[0:1, :] += _colsum(dy * n4)
        dn4 = dy * vrow(R_GATE_F)
        acc_ref[1:2, :] += _colsum(dn4 * ffhat)
        dffhat = dn4 * vrow(R_G_POST_FFN)
        dffb = (r4 * (dffhat - ffhat * _rowmean(dffhat * ffhat))).astype(BF16)
        dff_ref[...] = dffb
        dhid = _dot_nt(dffb, wd[...])
        dub = (dhid * sil).astype(BF16)
        dab = (dhid * u * (sg * (1.0 + a * (1.0 - sg)))).astype(BF16)
        da_ref[...] = dab
        du_ref[...] = dub
        dh2 = _dot(dab, wg[...]) + _dot(dub, wu[...])
        acc_ref[2:3, :] += _colsum(dh2 * xn3)
        acc_ref[3:4, :] += _colsum(dh2)
        dxn3 = dh2 * (1.0 + vrow(R_SCALE_F))
        acc_ref[4:5, :] += _colsum(dxn3 * x1hat)
        dx1hat = dxn3 * vrow(R_G_PRE_FFN)
        dx1_ref[...] = dy + r3 * (dx1hat - x1hat * _rowmean(dx1hat * x1hat))

    row = lambda w: pl.BlockSpec((tm, w), lambda i: (i, 0))
    sd = lambda w, dt: jax.ShapeDtypeStruct((S, w), dt)
    anyspec = pl.BlockSpec(memory_space=pl.ANY)
    return pl.pallas_call(
        body, name="ffn_fwd_bwd", grid=(S // tm,),
        in_specs=[row(D_MODEL), row(D_MODEL), _full(vecs.shape), anyspec, anyspec, anyspec],
        out_specs=[row(D_MODEL), row(D_MODEL), row(D_FF), row(D_FF), row(D_FF), row(D_MODEL),
                   _full((8, D_MODEL)), _full((8, LANES))],
        out_shape=[sd(D_MODEL, F32), sd(D_MODEL, BF16), sd(D_FF, BF16), sd(D_FF, BF16), sd(D_FF, BF16),
                   sd(D_MODEL, BF16), jax.ShapeDtypeStruct((8, D_MODEL), F32), jax.ShapeDtypeStruct((8, LANES), F32)],
        scratch_shapes=[pltpu.VMEM(w_gate.shape, BF16), pltpu.VMEM(w_up.shape, BF16), pltpu.VMEM(w_down.shape, BF16),
                        pltpu.SemaphoreType.DMA((3,))],
        compiler_params=_params(("arbitrary",)),
    )(x1, tgt, vecs, w_gate, w_up, w_down)


def _mix_bwd(dx1, mix, pa, pb, pg, vecs, w_ba, w_bm, w_out):
    S = dx1.shape[0]
    tm = ROW_TILE

    def body(dx1_ref, mix_ref, pa_ref, pb_ref, pg_ref, v_ref, wba_ref, wbm_ref, wout_ref,
             dmix_ref, dpa_ref, dpb_ref, dg_ref, dya_ref, dym_ref, acc_ref):
        i = pl.program_id(0)

        @pl.when(i == 0)
        def _():
            acc_ref[...] = jnp.zeros_like(acc_ref)

        vrow = lambda r: v_ref[r:r + 1, :]
        dx1v = dx1_ref[...]
        mix = mix_ref[...]
        r2 = lax.rsqrt(_rowmean(mix * mix) + NORM_EPS)
        mixhat = mix * r2
        acc_ref[0:1, :] += _colsum(dx1v * (mixhat * vrow(R_G_POST_MIX)))
        dn2 = dx1v * vrow(R_GATE_M)
        acc_ref[1:2, :] += _colsum(dn2 * mixhat)
        dmixhat = dn2 * vrow(R_G_POST_MIX)
        dmixb = (r2 * (dmixhat - mixhat * _rowmean(dmixhat * mixhat))).astype(BF16)
        dmix_ref[...] = dmixb
        dmerged = _dot_nt(dmixb, wout_ref[...])
        sa = _sigmoid(pg_ref[:, 0:D_MODEL])
        sm = _sigmoid(pg_ref[:, D_MODEL:G_W])
        dpab = (dmerged * sa).astype(BF16)
        dpbb = (dmerged * sm).astype(BF16)
        dpa_ref[...] = dpab
        dpb_ref[...] = dpbb
        dg_ref[:, 0:D_MODEL] = (dmerged * pa_ref[...].astype(F32) * (sa * (1.0 - sa))).astype(BF16)
        dg_ref[:, D_MODEL:G_W] = (dmerged * pb_ref[...].astype(F32) * (sm * (1.0 - sm))).astype(BF16)
        dya_ref[...] = _dot_nt(dpab, wba_ref[...])
        dym_ref[...] = _dot_nt(dpbb, wbm_ref[...])

    row = lambda w: pl.BlockSpec((tm, w), lambda i: (i, 0))
    sd = lambda w, dt: jax.ShapeDtypeStruct((S, w), dt)
    return pl.pallas_call(
        body, name="mix_bwd", grid=(S // tm,),
        in_specs=[row(D_MODEL), row(D_MODEL), row(D_MODEL), row(D_MODEL), row(G_W), _full(vecs.shape),
                  _full(w_ba.shape), _full(w_bm.shape), _full(w_out.shape)],
        out_specs=[row(D_MODEL), row(D_MODEL), row(D_MODEL), row(G_W), row(512), row(512), _full((8, D_MODEL))],
        out_shape=[sd(D_MODEL, BF16), sd(D_MODEL, BF16), sd(D_MODEL, BF16), sd(G_W, BF16), sd(512, F32), sd(512, F32),
                   jax.ShapeDtypeStruct((8, D_MODEL), F32)],
        compiler_params=_params(("arbitrary",)),
    )(dx1, mix, pa, pb, pg, vecs, w_ba, w_bm, w_out)


def _pre_bwd(dproj, x, dx1, vecs, w_cat):
    S = x.shape[0]
    tm = ROW_TILE

    def body(dp_ref, x_ref, dx1_ref, v_ref, w_ref, dx_ref, acc_ref):
        i = pl.program_id(0)

        @pl.when(i == 0)
        def _():
            acc_ref[...] = jnp.zeros_like(acc_ref)

        vrow = lambda r: v_ref[r:r + 1, :]
        dh = _dot_nt(dp_ref[...], w_ref[...])
        xv = x_ref[...]
        r1 = lax.rsqrt(_rowmean(xv * xv) + NORM_EPS)
        xhat = xv * r1
        acc_ref[0:1, :] += _colsum(dh * (xhat * vrow(R_G_PRE_MIX)))
        acc_ref[1:2, :] += _colsum(dh)
        dxn = dh * (1.0 + vrow(R_SCALE_M))
        acc_ref[2:3, :] += _colsum(dxn * xhat)
        dxhat = dxn * vrow(R_G_PRE_MIX)
        dx_ref[...] = dx1_ref[...] + r1 * (dxhat - xhat * _rowmean(dxhat * xhat))

    row = lambda w: pl.BlockSpec((tm, w), lambda i: (i, 0))
    return pl.pallas_call(
        body, name="pre_bwd", grid=(S // tm,),
        in_specs=[row(CAT_W), row(D_MODEL), row(D_MODEL), _full(vecs.shape), _full(w_cat.shape)],
        out_specs=[row(D_MODEL), _full((8, D_MODEL))],
        out_shape=[jax.ShapeDtypeStruct((S, D_MODEL), F32), jax.ShapeDtypeStruct((8, D_MODEL), F32)],
        compiler_params=_params(("arbitrary",)),
    )(dproj, x, dx1, vecs, w_cat)


def _matmul_tn(a, b, tn, name, ts=512):
    S, K = a.shape
    N = b.shape[1]
    n_s = S // ts

    def body(a_ref, b_ref, o_ref, acc_ref):
        s = pl.program_id(1)

        @pl.when(s == 0)
        def _():
            acc_ref[...] = jnp.zeros_like(acc_ref)

        acc_ref[...] += _dot_tn(a_ref[...], b_ref[...])

        @pl.when(s == n_s - 1)
        def _():
            o_ref[...] = acc_ref[...].astype(BF16)

    return pl.pallas_call(
        body, name=name, grid=(N // tn, n_s),
        in_specs=[pl.BlockSpec((ts, K), lambda j, s: (s, 0)), pl.BlockSpec((ts, tn), lambda j, s: (s, j))],
        out_specs=pl.BlockSpec((K, tn), lambda j, s: (0, j)),
        out_shape=jax.ShapeDtypeStruct((K, N), BF16),
        scratch_shapes=[pltpu.VMEM((K, tn), F32)],
        compiler_params=_params(("parallel", "arbitrary")),
    )(a, b)


def _rope_swap(t):
    lane = lax.broadcasted_iota(jnp.int32, t.shape, 1)
    first = (lane & (HEAD_DIM - 1)) < (HEAD_DIM // 2)
    return jnp.where(first, pltpu.roll(t, LANES - HEAD_DIM // 2, 1), pltpu.roll(t, HEAD_DIM // 2, 1))


def _rope(t, cos, sin_signed):
    return t * cos + _rope_swap(t) * sin_signed


def _rope_t(d, cos, sin_signed):
    return d * cos + _rope_swap(d * sin_signed)


def _to_kv_lanes(chunk, p, h):
    lane = lax.broadcasted_iota(jnp.int32, chunk.shape, 1)
    src = chunk if p == h else pltpu.roll(chunk, HEAD_DIM, 1)
    return jnp.where((lane >> 6) == h, src, jnp.zeros_like(src))


def _from_kv_lanes(o_a, o_b, h):
    lane = lax.broadcasted_iota(jnp.int32, o_a.shape, 1)
    a = o_a if h == 0 else pltpu.roll(o_a, HEAD_DIM, 1)
    b = o_b if h == 1 else pltpu.roll(o_b, HEAD_DIM, 1)
    return jnp.where(lane < HEAD_DIM, a, b)


def _band_mask(n):
    blk = ATTN_BLOCK
    qi = lax.broadcasted_iota(jnp.int32, (4 * blk, 2 * blk), 0) & (blk - 1)
    kj = lax.broadcasted_iota(jnp.int32, (4 * blk, 2 * blk), 1)
    return (kj > qi) & (kj <= qi + blk) & ((n > 0) | (kj >= blk))


def _stack_heads(chunks, h, dtype):
    parts = []
    for g in range(4):
        j = 4 * h + g
        parts.append(_to_kv_lanes(chunks[j // 2], j % 2, h))
    return jnp.concatenate(parts, axis=0).astype(dtype)


def _attn_fwd(pa, cos, sin, sinks):
    S = pa.shape[0]
    blk = ATTN_BLOCK
    nb = S // blk

    def body(sink_ref, cur_ref, prev_ref, cos_ref, sin_ref, cosp_ref, sinp_ref,
             ya_ref, qr_ref, kr_ref, vb_ref, lse_ref):
        n = pl.program_id(0)
        cos_c, sin_c = cos_ref[...], sin_ref[...]
        qch = [_rope(cur_ref[:, c * LANES:(c + 1) * LANES], cos_c, sin_c) * (HEAD_DIM ** -0.5) for c in range(4)]
        for c in range(4):
            qr_ref[:, c * LANES:(c + 1) * LANES] = qch[c].astype(BF16)
        k_cur = _rope(cur_ref[:, 512:640], cos_c, sin_c).astype(BF16)
        k_prev = _rope(prev_ref[:, 0:LANES], cosp_ref[...], sinp_ref[...]).astype(BF16)
        v_cur = cur_ref[:, 640:768].astype(BF16)
        v_prev = prev_ref[:, LANES:2 * LANES].astype(BF16)
        kr_ref[...] = k_cur
        vb_ref[...] = v_cur
        K = jnp.concatenate([k_prev, k_cur], axis=0)
        V = jnp.concatenate([v_prev, v_cur], axis=0)
        mask = _band_mask(n)
        rowg = lax.broadcasted_iota(jnp.int32, (4 * blk, 1), 0) >> 7
        lane = lax.broadcasted_iota(jnp.int32, (blk, LANES), 1)
        lse_tile = jnp.zeros((blk, LANES), F32)
        outs = []
        for h in range(2):
            qs = _stack_heads(qch, h, BF16)
            s = jnp.where(mask, _dot_nt(qs, K), NEG)
            sink = jnp.zeros((4 * blk, 1), F32)
            for g in range(4):
                sink = jnp.where(rowg == g, sink_ref[4 * h + g], sink)
            m = jnp.maximum(jnp.max(s, axis=1, keepdims=True), sink)
            p = jnp.exp(s - m)
            den = jnp.sum(p, axis=1, keepdims=True) + jnp.exp(sink - m)
            o = _dot((p / den).astype(BF16), V)
            lse = m + jnp.log(den)
            for g in range(4):
                outs.append(o[g * blk:(g + 1) * blk, :])
                lse_tile = jnp.where(lane == 4 * h + g, lse[g * blk:(g + 1) * blk, :], lse_tile)
        for c in range(4):
            ya_ref[:, c * LANES:(c + 1) * LANES] = _from_kv_lanes(outs[2 * c], outs[2 * c + 1], c // 2).astype(BF16)
        lse_ref[...] = lse_tile

    prev = lambda n: jnp.maximum(n - 1, 0)
    sd = lambda w, dt: jax.ShapeDtypeStruct((S, w), dt)
    return pl.pallas_call(
        body, name="attn_fwd", grid=(nb,),
        in_specs=[pl.BlockSpec(memory_space=pltpu.SMEM),
                  pl.BlockSpec((blk, A_W), lambda n: (n, 0)),
                  pl.BlockSpec((blk, 256), lambda n: (prev(n), 2)),
                  pl.BlockSpec((blk, LANES), lambda n: (n, 0)), pl.BlockSpec((blk, LANES), lambda n: (n, 0)),
                  pl.BlockSpec((blk, LANES), lambda n: (prev(n), 0)), pl.BlockSpec((blk, LANES), lambda n: (prev(n), 0))],
        out_specs=[pl.BlockSpec((blk, 512), lambda n: (n, 0)), pl.BlockSpec((blk, 512), lambda n: (n, 0)),
                   pl.BlockSpec((blk, LANES), lambda n: (n, 0)), pl.BlockSpec((blk, LANES), lambda n: (n, 0)),
                   pl.BlockSpec((blk, LANES), lambda n: (n, 0))],
        out_shape=[sd(512, BF16), sd(512, BF16), sd(LANES, BF16), sd(LANES, BF16), sd(LANES, F32)],
        compiler_params=_params(("parallel",)),
    )(sinks, pa, pa, cos, sin, cos, sin)


def _attn_bwd(dya, qr, kr, vb, lse, cos, sin, sinks):
    S = dya.shape[0]
    blk = ATTN_BLOCK
    nb = S // blk

    def body(sink_ref, dya_ref, qr_ref, kc_ref, kp_ref, vc_ref, vp_ref, lse_ref, cos_ref, sin_ref, cosp_ref, sinp_ref,
             dq_ref, dkv_ref, dsink_ref, ck, cv):
        n = pl.program_id(0)

        @pl.when(n == 0)
        def _():
            ck[...] = jnp.zeros_like(ck)
            cv[...] = jnp.zeros_like(cv)
            dsink_ref[...] = jnp.zeros_like(dsink_ref)

        @pl.when(n < nb)
        def _():
            K = jnp.concatenate([kp_ref[...], kc_ref[...]], axis=0)
            V = jnp.concatenate([vp_ref[...], vc_ref[...]], axis=0)
            qch = [qr_ref[:, c * LANES:(c + 1) * LANES] for c in range(4)]
            dch = [dya_ref[:, c * LANES:(c + 1) * LANES] for c in range(4)]
            lse_tile = lse_ref[...]
            mask = _band_mask(n)
            rowg = lax.broadcasted_iota(jnp.int32, (4 * blk, 1), 0) >> 7
            lane8 = lax.broadcasted_iota(jnp.int32, (8, LANES), 1)
            dk_acc = jnp.zeros((2 * blk, LANES), F32)
            dv_acc = jnp.zeros((2 * blk, LANES), F32)
            dsink = jnp.zeros((8, LANES), F32)
            dqs = []
            for h in range(2):
                qs = _stack_heads(qch, h, BF16)
                dos = _stack_heads(dch, h, BF16)
                lse_col = jnp.concatenate([lse_tile[:, 4 * h + g:4 * h + g + 1] for g in range(4)], axis=0)
                p = jnp.where(mask, jnp.exp(_dot_nt(qs, K) - lse_col), 0.0)
                dp = _dot_nt(dos, V)
                delta = jnp.sum(p * dp, axis=1, keepdims=True)
                dsb = (p * (dp - delta)).astype(BF16)
                dq = _dot(dsb, K)
                dk_acc = dk_acc + _dot_tn(dsb, qs)
                dv_acc = dv_acc + _dot_tn(p.astype(BF16), dos)
                sink = jnp.zeros((4 * blk, 1), F32)
                for g in range(4):
                    sink = jnp.where(rowg == g, sink_ref[4 * h + g], sink)
                ps_delta = jnp.exp(sink - lse_col) * delta
                for g in range(4):
                    dqs.append(dq[g * blk:(g + 1) * blk, :])
                    dsink = jnp.where(lane8 == 4 * h + g, dsink - jnp.sum(ps_delta[g * blk:(g + 1) * blk, :]), dsink)
            dsink_ref[...] += dsink
            cos_c, sin_c = cos_ref[...], sin_ref[...]
            for c in range(4):
                dqc = _from_kv_lanes(dqs[2 * c], dqs[2 * c + 1], c // 2) * (HEAD_DIM ** -0.5)
                dq_ref[:, c * LANES:(c + 1) * LANES] = _rope_t(dqc, cos_c, sin_c).astype(BF16)
            dkv_ref[:, 0:LANES] = _rope_t(dk_acc[0:blk, :] + ck[...], cosp_ref[...], sinp_ref[...]).astype(BF16)
            dkv_ref[:, LANES:2 * LANES] = (dv_acc[0:blk, :] + cv[...]).astype(BF16)
            ck[...] = dk_acc[blk:2 * blk, :]
            cv[...] = dv_acc[blk:2 * blk, :]

        @pl.when(n == nb)
        def _():
            dkv_ref[:, 0:LANES] = _rope_t(ck[...], cosp_ref[...], sinp_ref[...]).astype(BF16)
            dkv_ref[:, LANES:2 * LANES] = cv[...].astype(BF16)

    cur = lambda n: jnp.minimum(n, nb - 1)
    prev = lambda n: jnp.maximum(n - 1, 0)
    bs = lambda w, f: pl.BlockSpec((blk, w), lambda n: (f(n), 0))
    return pl.pallas_call(
        body, name="attn_bwd", grid=(nb + 1,),
        in_specs=[pl.BlockSpec(memory_space=pltpu.SMEM),
                  bs(512, cur), bs(512, cur), bs(LANES, cur), bs(LANES, prev), bs(LANES, cur), bs(LANES, prev),
                  bs(LANES, cur), bs(LANES, cur), bs(LANES, cur), bs(LANES, prev), bs(LANES, prev)],
        out_specs=[bs(512, cur), bs(256, prev), _full((8, LANES))],
        out_shape=[jax.ShapeDtypeStruct((S, 512), BF16), jax.ShapeDtypeStruct((S, 256), BF16),
                   jax.ShapeDtypeStruct((8, LANES), F32)],
        scratch_shapes=[pltpu.VMEM((blk, LANES), F32), pltpu.VMEM((blk, LANES), F32)],
        compiler_params=_params(("arbitrary",)),
    )(sinks, dya, qr, kr, kr, vb, vb, lse, cos, sin, cos, sin)


def _split3(x):
    hi = x.astype(BF16)
    r1 = x - hi.astype(F32)
    mid = r1.astype(BF16)
    lo = (r1 - mid.astype(F32)).astype(BF16)
    return hi, mid, lo


def _tri_matmul(tri_b, x):
    hi, mid, lo = _split3(x)
    return _dot(tri_b, hi) + _dot(tri_b, mid) + _dot(tri_b, lo)


def _log_sigmoid(x):
    return jnp.minimum(x, 0.0) - jnp.log(1.0 + jnp.exp(-jnp.abs(x)))


def _conv_silu_fwd(cur, prev, cw_ref, first):
    L = cur.shape[0]
    row = lax.broadcasted_iota(jnp.int32, cur.shape, 0)
    prev = jnp.where(first, jnp.zeros_like(prev), prev)
    shifted = [cur]
    for k in range(1, 4):
        shifted.append(jnp.where(row < k, pltpu.roll(prev, k, 0), pltpu.roll(cur, k, 0)))
    z = cw_ref[4:5, :]
    for k in range(3, -1, -1):
        z = z + shifted[k] * cw_ref[3 - k:4 - k, :]
    return z, shifted


def _mlstm_head_fwd(qh, kh, vh, i_col, b_col, C_prev, n_prev, m_prev, tri, eye):
    L = qh.shape[0]
    col2row = lambda x: jnp.sum(jnp.where(eye, x, 0.0), axis=0, keepdims=True)
    b_row = col2row(b_col)
    i_row = col2row(i_col)
    bl = b_col[L - 1:L, :]
    Dm = jnp.where(tri, b_col - b_row + i_row, NEG)
    inter = b_col + m_prev
    m_t = jnp.maximum(inter, jnp.max(Dm, axis=1, keepdims=True))
    W = jnp.exp(Dm - m_t)
    e_t = jnp.exp(inter - m_t)
    qb, kb, vb = qh.astype(BF16), kh.astype(BF16), vh.astype(BF16)
    Sc = _dot_nt(qb, kb) * W
    P1 = _dot(qb, C_prev.astype(BF16))
    num = _dot(Sc.astype(BF16), vb) + e_t * P1
    qn = jnp.sum(qh * n_prev, axis=1, keepdims=True)
    den = jnp.sum(Sc, axis=1, keepdims=True) + e_t * qn
    floor = jnp.exp(-m_t)
    g = jnp.maximum(jnp.abs(den), floor)
    hv = num / g
    a_col = bl - b_col + i_col
    m_new = jnp.maximum(bl + m_prev, jnp.max(a_col, axis=0, keepdims=True))
    dec = jnp.exp(bl + m_prev - m_new)
    u_col = jnp.exp(a_col - m_new)
    return dict(W=W, e_t=e_t, qb=qb, kb=kb, vb=vb, Sc=Sc, P1=P1, qn=qn, den=den, floor=floor, g=g, hv=hv,
                m_new=m_new, dec=dec, u_col=u_col)


def _mlstm_fwd(pm, pif, cw, sv):
    S = pm.shape[0]
    L = MLSTM_CHUNK
    nc = S // L
    HD = MLSTM_HEAD_DIM
    W4 = MLSTM_HEADS * HD

    def body(cur_ref, prev_ref, pif_ref, cw_ref, sv_ref, ym_ref, cst_ref, nst_ref, C, nm):
        c = pl.program_id(0)

        @pl.when(c == 0)
        def _():
            C[...] = jnp.zeros_like(C)
            nm[...] = jnp.zeros_like(nm)

        z, _ = _conv_silu_fwd(cur_ref[:, 0:2 * W4], prev_ref[...], cw_ref, c == 0)
        qk = z * _sigmoid(z)
        gt = pif_ref[...] + sv_ref[1:2, 0:LANES]
        r_i = lax.broadcasted_iota(jnp.int32, (L, L), 0)
        c_i = lax.broadcasted_iota(jnp.int32, (L, L), 1)
        tri = c_i <= r_i
        eye = c_i == r_i
        b_all = _tri_matmul(tri.astype(BF16), _log_sigmoid(gt))
        nst_ref[0] = nm[...]
        for h in range(MLSTM_HEADS):
            sl = slice(h * HD, (h + 1) * HD)
            qh = qk[:, sl]
            kh = qk[:, W4 + h * HD:W4 + (h + 1) * HD] * (HD ** -0.5)
            vh = cur_ref[:, 2 * W4 + h * HD:2 * W4 + (h + 1) * HD]
            n_prev = nm[h:h + 1, :]
            m_prev = nm[4 + h:5 + h, 0:1]
            C_prev = C[h]
            cst_ref[0, h] = C_prev
            f = _mlstm_head_fwd(qh, kh, vh, gt[:, h:h + 1], b_all[:, 4 + h:5 + h], C_prev, n_prev, m_prev, tri, eye)
            hv = f["hv"]
            xc = hv - _rowmean(hv)
            hhat = xc * lax.rsqrt(_rowmean(xc * xc) + NORM_EPS)
            so = _sigmoid(cur_ref[:, 3 * W4 + h * HD:3 * W4 + (h + 1) * HD])
            ym_ref[:, sl] = (so * hhat * sv_ref[0:1, sl]).astype(BF16)
            kw = kh * f["u_col"]
            C[h] = f["dec"] * C_prev + _dot_tn(kw.astype(BF16), f["vb"])
            nm[h:h + 1, :] = f["dec"] * n_prev + _colsum(kw)
            nm[4 + h:5 + h, :] = jnp.broadcast_to(f["m_new"], (1, LANES))

    prev = lambda c: jnp.maximum(c - 1, 0)
    return pl.pallas_call(
        body, name="mlstm_fwd", grid=(nc,),
        in_specs=[pl.BlockSpec((L, M_W), lambda c: (c, 0)), pl.BlockSpec((L, 2 * W4), lambda c: (prev(c), 0)),
                  pl.BlockSpec((L, IF_W), lambda c: (c, 0)), _full(cw.shape), _full(sv.shape)],
        out_specs=[pl.BlockSpec((L, W4), lambda c: (c, 0)),
                   pl.BlockSpec((1, MLSTM_HEADS, HD, HD), lambda c: (c, 0, 0, 0)),
                   pl.BlockSpec((1, 8, LANES), lambda c: (c, 0, 0))],
        out_shape=[jax.ShapeDtypeStruct((S, W4), BF16), jax.ShapeDtypeStruct((nc, MLSTM_HEADS, HD, HD), F32),
                   jax.ShapeDtypeStruct((nc, 8, LANES), F32)],
        scratch_shapes=[pltpu.VMEM((MLSTM_HEADS, HD, HD), F32), pltpu.VMEM((8, LANES), F32)],
        compiler_params=_params(("arbitrary",)),
    )(pm, pm, pif, cw, sv)


def _mlstm_bwd(pm, pif, cw, sv, dym, cst, nst):
    S = pm.shape[0]
    L = MLSTM_CHUNK
    nc = S // L
    HD = MLSTM_HEAD_DIM
    W4 = MLSTM_HEADS * HD

    def body(cur_ref, prev_ref, pif_ref, cw_ref, sv_ref, dym_ref, cst_ref, nst_ref,
             dm_ref, dif_ref, dcw_ref, dsv_ref, dC, dn, dz_next, dqk):
        r = pl.program_id(0)
        c = nc - 1 - r

        @pl.when(r == 0)
        def _():
            dC[...] = jnp.zeros_like(dC)
            dn[...] = jnp.zeros_like(dn)
            dz_next[...] = jnp.zeros_like(dz_next)
            dcw_ref[...] = jnp.zeros_like(dcw_ref)
            dsv_ref[...] = jnp.zeros_like(dsv_ref)

        z, shifted = _conv_silu_fwd(cur_ref[:, 0:2 * W4], prev_ref[...], cw_ref, c == 0)
        sgz = _sigmoid(z)
        qk = z * sgz
        gt = pif_ref[...] + sv_ref[1:2, 0:LANES]
        r_i = lax.broadcasted_iota(jnp.int32, (L, L), 0)
        c_i = lax.broadcasted_iota(jnp.int32, (L, L), 1)
        tri = c_i <= r_i
        eye = c_i == r_i
        b_all = _tri_matmul(tri.astype(BF16), _log_sigmoid(gt))
        lane = lax.broadcasted_iota(jnp.int32, (L, LANES), 1)
        rowl = lax.broadcasted_iota(jnp.int32, (L, 1), 0)
        nmv = nst_ref[0]
        di_tile = jnp.zeros((L, LANES), F32)
        db_tile = jnp.zeros((L, LANES), F32)
        for h in range(MLSTM_HEADS):
            sl = slice(h * HD, (h + 1) * HD)
            qh = qk[:, sl]
            kh = qk[:, W4 + h * HD:W4 + (h + 1) * HD] * (HD ** -0.5)
            vh = cur_ref[:, 2 * W4 + h * HD:2 * W4 + (h + 1) * HD]
            n_prev = nmv[h:h + 1, :]
            m_prev = nmv[4 + h:5 + h, 0:1]
            C_prev = cst_ref[0, h]
            f = _mlstm_head_fwd(qh, kh, vh, gt[:, h:h + 1], b_all[:, 4 + h:5 + h], C_prev, n_prev, m_prev, tri, eye)
            hv, g, den, e_t, u_col, dec = f["hv"], f["g"], f["den"], f["e_t"], f["u_col"], f["dec"]
            qb, kb, vb, Sc, W = f["qb"], f["kb"], f["vb"], f["Sc"], f["W"]
            xc = hv - _rowmean(hv)
            rstd = lax.rsqrt(_rowmean(xc * xc) + NORM_EPS)
            hhat = xc * rstd
            wn = sv_ref[0:1, sl]
            so = _sigmoid(cur_ref[:, 3 * W4 + h * HD:3 * W4 + (h + 1) * HD])
            dy = dym_ref[:, sl]
            dm_ref[:, 3 * W4 + h * HD:3 * W4 + (h + 1) * HD] = (dy * hhat * wn * (so * (1.0 - so))).astype(BF16)
            dln = dy * so
            dsv_ref[0:1, sl] += _colsum(dln * hhat)
            dhhat = dln * wn
            dh = rstd * (dhhat - _rowmean(dhhat) - hhat * _rowmean(dhhat * hhat))
            dnum = dh / g
            active = jnp.abs(den) > f["floor"]
            dden = jnp.where(active, -jnp.sum(dh * hv, axis=1, keepdims=True) / g * jnp.where(den >= 0.0, 1.0, -1.0), 0.0)
            dnumb = dnum.astype(BF16)
            dSc = _dot_nt(dnumb, vb) + dden
            dA = (dSc * W).astype(BF16)
            G = dSc * Sc
            Gr = jnp.sum(G, axis=1, keepdims=True)
            Gc = jnp.sum(jnp.where(eye, _colsum(G), 0.0), axis=1, keepdims=True)
            dCn = dC[h]
            dCnb = dCn.astype(BF16)
            dn_new = dn[h:h + 1, :]
            kdC = _dot(kb, dCnb)
            vdC = _dot_nt(vb, dCnb)
            dv = _dot_tn(Sc.astype(BF16), dnumb) + u_col * kdC
            Cb = C_prev.astype(BF16)
            dq = _dot(dA, kb) + e_t * _dot_nt(dnumb, Cb) + (e_t * dden) * n_prev
            dk = _dot_tn(dA, qb) + u_col * (vdC + dn_new)
            de = jnp.sum(f["P1"] * dnum, axis=1, keepdims=True) + dden * f["qn"]
            E = de * e_t
            du = jnp.sum(kdC * vh, axis=1, keepdims=True) + jnp.sum(kh * dn_new, axis=1, keepdims=True)
            U = du * u_col
            ddec = jnp.sum(dCn * C_prev) + jnp.sum(dn_new * n_prev)
            dbl = ddec * dec + jnp.sum(U, axis=0, keepdims=True)
            db = Gr + E - Gc - U + jnp.where(rowl == L - 1, dbl, 0.0)
            di_tile = jnp.where(lane == h, Gc + U, di_tile)
            db_tile = jnp.where(lane == 4 + h, db, db_tile)
            dC[h] = dec * dCn + _dot_tn((qh * e_t).astype(BF16), dnumb)
            dn[h:h + 1, :] = dec * dn_new + _colsum((e_t * dden) * qh)
            dqk[:, sl] = dq
            dqk[:, W4 + h * HD:W4 + (h + 1) * HD] = dk * (HD ** -0.5)
            dm_ref[:, 2 * W4 + h * HD:2 * W4 + (h + 1) * HD] = dv.astype(BF16)
        dlf = _tri_matmul((r_i <= c_i).astype(BF16), db_tile)
        dif = jnp.where(lane < 4, di_tile, jnp.where(lane < 8, dlf * (1.0 - _sigmoid(gt)), 0.0))
        dif_ref[...] = dif.astype(BF16)
        dsv_ref[1:2, 0:LANES] += _colsum(dif)
        dz = dqk[...] * (sgz * (1.0 + z * (1.0 - sgz)))
        dcw_ref[4:5, :] += _colsum(dz)
        row = lax.broadcasted_iota(jnp.int32, dz.shape, 0)
        dzn = dz_next[...]
        du_in = dz * cw_ref[3:4, :]
        dcw_ref[3:4, :] += _colsum(dz * shifted[0])
        for k in range(1, 4):
            dcw_ref[3 - k:4 - k, :] += _colsum(dz * shifted[k])
            up = jnp.where(row >= L - k, pltpu.roll(dzn, L - k, 0), pltpu.roll(dz, L - k, 0))
            du_in = du_in + up * cw_ref[3 - k:4 - k, :]
        dz_next[...] = dz
        dm_ref[:, 0:2 * W4] = du_in.astype(BF16)

    cidx = lambda r: nc - 1 - r
    prev = lambda r: jnp.maximum(nc - 2 - r, 0)
    return pl.pallas_call(
        body, name="mlstm_bwd", grid=(nc,),
        in_specs=[pl.BlockSpec((L, M_W), lambda r: (cidx(r), 0)), pl.BlockSpec((L, 2 * W4), lambda r: (prev(r), 0)),
                  pl.BlockSpec((L, IF_W), lambda r: (cidx(r), 0)), _full(cw.shape), _full(sv.shape),
                  pl.BlockSpec((L, W4), lambda r: (cidx(r), 0)),
                  pl.BlockSpec((1, MLSTM_HEADS, HD, HD), lambda r: (cidx(r), 0, 0, 0)),
                  pl.BlockSpec((1, 8, LANES), lambda r: (cidx(r), 0, 0))],
        out_specs=[pl.BlockSpec((L, M_W), lambda r: (cidx(r), 0)), pl.BlockSpec((L, IF_W), lambda r: (cidx(r), 0)),
                   _full((8, 2 * W4)), _full((8, W4))],
        out_shape=[jax.ShapeDtypeStruct((S, M_W), BF16), jax.ShapeDtypeStruct((S, IF_W), BF16),
                   jax.ShapeDtypeStruct((8, 2 * W4), F32), jax.ShapeDtypeStruct((8, W4), F32)],
        scratch_shapes=[pltpu.VMEM((MLSTM_HEADS, HD, HD), F32), pltpu.VMEM((8, LANES), F32),
                        pltpu.VMEM((L, 2 * W4), F32), pltpu.VMEM((L, 2 * W4), F32)],
        compiler_params=_params(("arbitrary",)),
    )(pm, pm, pif, cw, sv, dym, cst, nst)


def _rope_tables(positions):
    half = HEAD_DIM // 2
    inv_freq = ROPE_THETA ** (-2.0 * jnp.arange(half, dtype=F32) / HEAD_DIM)
    ang = positions.astype(F32)[:, None] * inv_freq
    cos = jnp.tile(jnp.cos(ang), (1, LANES // half))
    sign = jnp.tile(jnp.concatenate([-jnp.ones((half,), F32), jnp.ones((half,), F32)]), LANES // HEAD_DIM)
    sin = jnp.tile(jnp.sin(ang), (1, LANES // half)) * sign
    return cos, sin


def _local_step(x, tgt, positions, mod, gains, w_cat, w_ba, w_bm, w_out, w_gate, w_up, w_down,
                conv_w, conv_b, b_if, sinks, norm_w):
    t = _tables(mod, gains, conv_w, conv_b, b_if, norm_w, positions)
    a = _mixer_fwd(x, t, sinks, w_cat)
    b = _ffn_part(x, tgt, t, a, w_ba, w_bm, w_out, w_gate, w_up, w_down)
    c = _mixer_bwd(b["dx1"], t, a, b, sinks, w_ba, w_bm, w_out)
    grad_x, acc_p = _pre_bwd(c["dproj"], x, b["dx1"], t["vecs"], w_cat)
    big = dict(w_cat=c["g_w_cat"], w_ba=c["g_w_ba"], w_bm=c["g_w_bm"], w_out=c["g_w_out"], w_gate=b["g_w_gate"],
               w_up=b["g_w_up"], w_down=b["g_w_down"])
    return b["loss"], grad_x, big, _small_grads(acc_p, b, c)


def _tables(mod, gains, conv_w, conv_b, b_if, norm_w, positions):
    cos, sin = _rope_tables(positions)
    return dict(
        vecs=jnp.concatenate([mod, gains, jnp.zeros((6, D_MODEL), F32)], axis=0),
        cw=jnp.concatenate([conv_w, conv_b.reshape(1, -1), jnp.zeros((3, 2 * 512), F32)], axis=0),
        sv=jnp.zeros((8, 512), F32).at[0].set(norm_w).at[1, 0:8].set(b_if), cos=cos, sin=sin)


def _mixer_fwd(x, t, sinks, w_cat):
    h, pa, pm, pif, pg = _pre_proj(x, t["vecs"], w_cat)
    ya, qr, kr, vb, lse = _attn_fwd(pa, t["cos"], t["sin"], sinks)
    ym, cst, nst = _mlstm_fwd(pm, pif, t["cw"], t["sv"])
    return dict(h=h, pm=pm, pif=pif, pg=pg, ya=ya, qr=qr, kr=kr, vb=vb, lse=lse, ym=ym, cst=cst, nst=nst)


def _ffn_part(x, tgt, t, a, w_ba, w_bm, w_out, w_gate, w_up, w_down):
    x1, merged, mix, pba, pbm = _mix_fwd(x, a["ya"], a["ym"], a["pg"], t["vecs"], w_ba, w_bm, w_out)
    dx1, h2, hid, da, du, dff, acc_f, loss = _ffn_fwd_bwd(x1, tgt, t["vecs"], w_gate, w_up, w_down)
    return dict(merged=merged, mix=mix, pba=pba, pbm=pbm, dx1=dx1, acc_f=acc_f, loss=loss[0, 0],
                g_w_gate=_matmul_tn(da, h2, 1024, "dw_ffn_gate"),
                g_w_up=_matmul_tn(du, h2, 1024, "dw_ffn_up"),
                g_w_down=_matmul_tn(hid, dff, 1024, "dw_ffn_down"))


def _mixer_bwd(dx1, t, a, b, sinks, w_ba, w_bm, w_out):
    dmix, dpa, dpb, dg, dya, dym, acc_m = _mix_bwd(dx1, b["mix"], b["pba"], b["pbm"], a["pg"], t["vecs"], w_ba, w_bm, w_out)
    g_w_out = _matmul_tn(b["merged"], dmix, 1024, "dw_out")
    g_w_ba = _matmul_tn(a["ya"], dpa, 1024, "dw_branch_attn")
    g_w_bm = _matmul_tn(a["ym"], dpb, 1024, "dw_branch_mlstm")
    dq, dkv, dsink = _attn_bwd(dya, a["qr"], a["kr"], a["vb"], a["lse"], t["cos"], t["sin"], sinks)
    dm, dif, dcw, dsv = _mlstm_bwd(a["pm"], a["pif"], t["cw"], t["sv"], dym, a["cst"], a["nst"])
    dproj = jnp.concatenate([dq, dkv, dm, dif, dg], axis=1)
    return dict(dproj=dproj, g_w_cat=_matmul_tn(a["h"], dproj, 1664, "dw_in"), g_w_out=g_w_out, g_w_ba=g_w_ba,
                g_w_bm=g_w_bm, acc_m=acc_m, dsink=dsink, dcw=dcw, dsv=dsv)


def _small_grads(acc_p, b, c):
    acc_f, acc_m = b["acc_f"], c["acc_m"]
    dmod = jnp.stack([acc_p[1], acc_p[0], acc_m[0], acc_f[3], acc_f[2], acc_f[0]])
    dgains = jnp.stack([acc_p[2], acc_m[1], acc_f[4], acc_f[1]])
    return dict(dmod=dmod, dgains=dgains, dconv_w=c["dcw"][0:4], dconv_b=c["dcw"][4], db_if=c["dsv"][1, 0:8],
                dsinks=c["dsink"][0, 0:8], dnorm_w=c["dsv"][0])


MESH_ID = pl.DeviceIdType.MESH


def _mesh_pos():
    return lax.axis_index("x"), lax.axis_index("y"), lax.axis_index("c")


def _flip(v, bit):
    return 1 - v if bit else v


def _relations():
    return [((r >> 2) & 1, (r >> 1) & 1, r & 1) for r in range(1, N_DEV)]


def _small_exchange(p, gather, name):
    V = p.shape[-1]

    def body(p_ref, out_ref, send_sems, recv_sems):
        x, y, c = _mesh_pos()
        me = 4 * x + 2 * y + c
        out_ref[me] = p_ref[...] if gather else p_ref[me]
        peers = []
        for dx, dy, dc in _relations():
            px, py, pc = _flip(x, dx), _flip(y, dy), _flip(c, dc)
            peers.append(((px, py, pc), 4 * px + 2 * py + pc))

        def copy(k, landing):
            peer, pid = peers[k]
            return pltpu.make_async_remote_copy(
                src_ref=p_ref if gather else p_ref.at[pid], dst_ref=out_ref.at[landing],
                send_sem=send_sems.at[k], recv_sem=recv_sems.at[k], device_id=peer, device_id_type=MESH_ID)

        sends = [copy(k, me) for k in range(N_DEV - 1)]
        for cp in sends:
            cp.start()
        for k in range(N_DEV - 1):
            copy(k, peers[k][1]).wait_recv()
        for cp in sends:
            cp.wait_send()

    vm = pl.BlockSpec(memory_space=pltpu.VMEM)
    return pl.pallas_call(
        body, name=name, in_specs=[vm], out_specs=vm,
        out_shape=jax.ShapeDtypeStruct((N_DEV, 8, V), F32),
        scratch_shapes=[pltpu.SemaphoreType.DMA((N_DEV - 1,)), pltpu.SemaphoreType.DMA((N_DEV - 1,))],
        compiler_params=pltpu.CompilerParams(vmem_limit_bytes=VMEM_LIMIT),
    )(p)


HBM_SPEC = pl.BlockSpec(memory_space=pltpu.HBM)
SEM_SPEC = pl.BlockSpec(memory_space=pltpu.SEMAPHORE)


def _peers(x, y, c):
    out = []
    for dx, dy, dc in _relations():
        px, py, pc = _flip(x, dx), _flip(y, dy), _flip(c, dc)
        out.append(((px, py, pc), 4 * px + 2 * py + pc))
    return out


def _exchange_start(arrs, gather, after, name):
    n = len(arrs)
    me_out = 4 * lax.axis_index("x") + 2 * lax.axis_index("y") + lax.axis_index("c")
    lands = []
    for a in arrs:
        own = a[None] if gather else lax.dynamic_index_in_dim(a, me_out, 0, keepdims=True)
        empty = lax.empty(((N_DEV,) + a.shape) if gather else a.shape, a.dtype)
        lands.append(lax.dynamic_update_index_in_dim(empty, own, me_out, 0))

    def body(*refs):
        a_refs, l_refs = refs[:n], refs[n:2 * n]
        send_sems, recv_sems = refs[2 * n + 1], refs[2 * n + 2]
        token = refs[4 * n + 3]
        x, y, c = _mesh_pos()
        me = 4 * x + 2 * y + c
        for a in range(n):
            for k, (peer, pid) in enumerate(_peers(x, y, c)):
                pltpu.make_async_remote_copy(
                    src_ref=a_refs[a] if gather else a_refs[a].at[pid], dst_ref=l_refs[a].at[me],
                    send_sem=send_sems.at[a * (N_DEV - 1) + k], recv_sem=recv_sems.at[a * (N_DEV - 1) + k],
                    device_id=peer, device_id_type=MESH_ID).start()
        token[...] = jnp.zeros_like(token)

    sem = pltpu.SemaphoreType.DMA((n * (N_DEV - 1),))
    hbm = lambda a: pltpu.with_memory_space_constraint(a, pltpu.HBM)
    res = pl.pallas_call(
        body, name=name,
        out_shape=(sem, sem, *[pltpu.HBM(a.shape, a.dtype) for a in arrs], *[pltpu.HBM(l.shape, l.dtype) for l in lands],
                   jax.ShapeDtypeStruct((8, LANES), F32)),
        in_specs=[HBM_SPEC] * (2 * n) + [pl.BlockSpec(memory_space=pl.ANY)],
        out_specs=(SEM_SPEC, SEM_SPEC, *[HBM_SPEC] * (2 * n), pl.BlockSpec(memory_space=pltpu.VMEM)),
        input_output_aliases={i: 2 + i for i in range(2 * n)},
        compiler_params=pltpu.CompilerParams(has_side_effects=pltpu.SideEffectType.DATAFLOW_SIDE_EFFECTING),
    )(*[hbm(a) for a in arrs], *[hbm(l) for l in lands], after)
    return dict(sems=res[0:2], arrs=res[2:2 + n], lands=res[2 + n:2 + 2 * n], token=res[2 + 2 * n], gather=gather)


def _exchange_wait(st, after, name):
    n = len(st["arrs"])
    gather = st["gather"]

    def body(*refs):
        a_refs, l_refs = refs[:n], refs[n:2 * n]
        send_sems, recv_sems = refs[2 * n], refs[2 * n + 1]
        x, y, c = _mesh_pos()
        for a in range(n):
            for k, (peer, pid) in enumerate(_peers(x, y, c)):
                cp = pltpu.make_async_remote_copy(
                    src_ref=a_refs[a] if gather else a_refs[a].at[pid], dst_ref=l_refs[a].at[pid],
                    send_sem=send_sems.at[a * (N_DEV - 1) + k], recv_sem=recv_sems.at[a * (N_DEV - 1) + k],
                    device_id=peer, device_id_type=MESH_ID)
                cp.wait_send()
                cp.wait_recv()

    both = list(st["arrs"]) + list(st["lands"])
    res = pl.pallas_call(
        body, name=name, out_shape=[pltpu.HBM(a.shape, a.dtype) for a in both],
        in_specs=[HBM_SPEC] * (2 * n) + [SEM_SPEC, SEM_SPEC, pl.BlockSpec(memory_space=pl.ANY)],
        out_specs=[HBM_SPEC] * (2 * n), input_output_aliases={i: i for i in range(2 * n)},
        compiler_params=pltpu.CompilerParams(has_side_effects=pltpu.SideEffectType.DATAFLOW_SIDE_EFFECTING),
    )(*both, *st["sems"], after)
    return res[n:2 * n]


def _after(x, token):
    return lax.optimization_barrier((x, token))[0]


def _all_gather_hbm(shards):
    n = len(shards)

    def body(*refs):
        p_refs, out_refs = refs[:n], refs[n:2 * n]
        send_sems, recv_sems, local_sems = refs[2 * n:]
        x, y, c = _mesh_pos()
        me, sibling = (x, y, c), (x, y, 1 - c)
        chips = [(1 - x, y), (x, 1 - y), (1 - x, 1 - y)]

        def copy(a, k, block, to, own=False):
            slot = out_refs[a].at[4 * block[0] + 2 * block[1] + block[2]]
            return pltpu.make_async_remote_copy(
                src_ref=p_refs[a] if own else slot, dst_ref=slot,
                send_sem=send_sems.at[a, k], recv_sem=recv_sems.at[a, k], device_id=to, device_id_type=MESH_ID)

        mine = [pltpu.make_async_copy(p_refs[a], out_refs[a].at[4 * x + 2 * y + c], local_sems.at[a]) for a in range(n)]
        for cp in mine:
            cp.start()
        first = []
        for a in range(n):
            first.append(copy(a, 0, me, sibling, own=True))
            first += [copy(a, 1 + j, me, (*chip, c), own=True) for j, chip in enumerate(chips)]
        for cp in first:
            cp.start()
        passed = []
        for j, chip in enumerate(chips):
            for a in range(n):
                copy(a, 1 + j, (*chip, c), me).wait_recv()
                passed.append(copy(a, 4 + j, (*chip, c), sibling))
                passed[-1].start()
        for a in range(n):
            copy(a, 0, sibling, me).wait_recv()
            for j, chip in enumerate(chips):
                copy(a, 4 + j, (*chip, 1 - c), me).wait_recv()
        for cp in first + passed:
            cp.wait_send()
        for cp in mine:
            cp.wait()

    hbm = pl.BlockSpec(memory_space=pl.ANY)
    return pl.pallas_call(
        body, name="gather_weights", in_specs=[hbm] * n, out_specs=[hbm] * n,
        out_shape=[jax.ShapeDtypeStruct((N_DEV,) + s.shape, s.dtype) for s in shards],
        scratch_shapes=[pltpu.SemaphoreType.DMA((n, N_DEV - 1)), pltpu.SemaphoreType.DMA((n, N_DEV - 1)),
                        pltpu.SemaphoreType.DMA((n,))],
    )(*shards)


def _scatter_hbm(pieces):
    n = len(pieces)

    def body(*refs):
        p_refs, out_refs = refs[:n], refs[n:2 * n]
        send_sems, recv_sems, local_sems = refs[2 * n:]
        x, y, c = _mesh_pos()
        me = 4 * x + 2 * y + c
        mine = [pltpu.make_async_copy(p_refs[a].at[me], out_refs[a].at[me], local_sems.at[a]) for a in range(n)]
        for cp in mine:
            cp.start()
        peers = []
        for dx, dy, dc in _relations():
            px, py, pc = _flip(x, dx), _flip(y, dy), _flip(c, dc)
            peers.append(((px, py, pc), 4 * px + 2 * py + pc))

        def copy(a, k, landing):
            peer, pid = peers[k]
            return pltpu.make_async_remote_copy(
                src_ref=p_refs[a].at[pid], dst_ref=out_refs[a].at[landing],
                send_sem=send_sems.at[a, k], recv_sem=recv_sems.at[a, k], device_id=peer, device_id_type=MESH_ID)

        sends = [copy(a, k, me) for a in range(n) for k in range(N_DEV - 1)]
        for cp in sends:
            cp.start()
        for a in range(n):
            for k in range(N_DEV - 1):
                copy(a, k, peers[k][1]).wait_recv()
        for cp in sends:
            cp.wait_send()
        for cp in mine:
            cp.wait()

    hbm = pl.BlockSpec(memory_space=pl.ANY)
    return pl.pallas_call(
        body, name="scatter_grads", in_specs=[hbm] * n, out_specs=[hbm] * n,
        out_shape=[jax.ShapeDtypeStruct(p.shape, p.dtype) for p in pieces],
        scratch_shapes=[pltpu.SemaphoreType.DMA((n, N_DEV - 1)), pltpu.SemaphoreType.DMA((n, N_DEV - 1)),
                        pltpu.SemaphoreType.DMA((n,))],
    )(*pieces)


def _adamw(w, g, m, v):
    m2 = ADAM_B1 * m + (1.0 - ADAM_B1) * g
    v2 = ADAM_B2 * v + (1.0 - ADAM_B2) * (g * g)
    m_hat = m2 / (1.0 - ADAM_B1 ** ADAM_STEP)
    v_hat = v2 / (1.0 - ADAM_B2 ** ADAM_STEP)
    delta = -ADAM_LR * (m_hat / (jnp.sqrt(v_hat) + ADAM_EPS) + ADAM_WD * w)
    return delta, m2, v2


def _mod_partial(cmat, w_shard, b_shard):
    def body(c_ref, w_ref, b_ref, o_ref):
        o_ref[...] = _dot(c_ref[...].astype(BF16), w_ref[...].astype(BF16)) + b_ref[...]

    return pl.pallas_call(
        body, name="mod_partial", out_shape=jax.ShapeDtypeStruct((N_DEV, w_shard.shape[1]), F32),
        compiler_params=_params(),
    )(cmat, w_shard, b_shard)


def _adamw_w_ada(cmat, dmod_cols, w, m, v):
    def body(c_ref, d_ref, w_ref, m_ref, v_ref, g_ref, dl_ref, m2_ref, v2_ref):
        g = _dot_tn(c_ref[...].astype(BF16), d_ref[...].astype(BF16))
        g_ref[...] = g
        dl_ref[...], m2_ref[...], v2_ref[...] = _adamw(w_ref[...], g, m_ref[...], v_ref[...])

    return pl.pallas_call(
        body, name="adamw_w_ada", out_shape=[jax.ShapeDtypeStruct(w.shape, F32)] * 4,
        compiler_params=_params(),
    )(cmat, dmod_cols, w, m, v)


def _adamw_small(gathered, w, m, v):
    def body(g_ref, w_ref, m_ref, v_ref, go_ref, dl_ref, m2_ref, v2_ref):
        g = g_ref[0]
        for k in range(1, N_DEV):
            g = g + g_ref[k]
        go_ref[...] = g
        dl_ref[...], m2_ref[...], v2_ref[...] = _adamw(w_ref[...], g, m_ref[...], v_ref[...])

    return pl.pallas_call(
        body, name="adamw_small", out_shape=[jax.ShapeDtypeStruct(w.shape, F32)] * 4,
        compiler_params=_params(),
    )(gathered, w, m, v)


def _row_tile(rows):
    return rows // 4 if rows >= 512 else rows


def _sum_partials(r_ref):
    g = r_ref[0].astype(F32)
    for k in range(1, N_DEV):
        g = g + r_ref[k].astype(F32)
    return g


def _adamw_sum(recv, w, m, v, name):
    r, cdim = w.shape
    tr = _row_tile(r)

    def body(r_ref, w_ref, m_ref, v_ref, g_ref, dl_ref, m2_ref, v2_ref):
        g = _sum_partials(r_ref)
        g_ref[...] = g
        dl_ref[...], m2_ref[...], v2_ref[...] = _adamw(w_ref[...], g, m_ref[...], v_ref[...])

    row = pl.BlockSpec((tr, cdim), lambda i: (i, 0))
    return pl.pallas_call(
        body, name=name, grid=(r // tr,),
        in_specs=[pl.BlockSpec((N_DEV, tr, cdim), lambda i: (0, i, 0)), row, row, row],
        out_specs=[row] * 4, out_shape=[jax.ShapeDtypeStruct((r, cdim), F32)] * 4,
        compiler_params=_params(("parallel",)),
    )(recv, w, m, v)


def _sum8(recv, name):
    _, r, cdim = recv.shape
    tr = _row_tile(r)

    def body(r_ref, g_ref):
        g_ref[...] = _sum_partials(r_ref)

    return pl.pallas_call(
        body, name=name, grid=(r // tr,),
        in_specs=[pl.BlockSpec((N_DEV, tr, cdim), lambda i: (0, i, 0))],
        out_specs=pl.BlockSpec((tr, cdim), lambda i: (i, 0)), out_shape=jax.ShapeDtypeStruct((r, cdim), F32),
        compiler_params=_params(("parallel",)),
    )(recv)


def _adamw_plain(g, w, m, v, name):
    r, cdim = w.shape
    tr = _row_tile(r)

    def body(g_ref, w_ref, m_ref, v_ref, dl_ref, m2_ref, v2_ref):
        dl_ref[...], m2_ref[...], v2_ref[...] = _adamw(w_ref[...], g_ref[...], m_ref[...], v_ref[...])

    row = pl.BlockSpec((tr, cdim), lambda i: (i, 0))
    return pl.pallas_call(
        body, name=name, grid=(r // tr,), in_specs=[row] * 4, out_specs=[row] * 3,
        out_shape=[jax.ShapeDtypeStruct((r, cdim), F32)] * 3,
        compiler_params=_params(("parallel",)),
    )(g, w, m, v)


IN_SHARD = 609
IN_SHARD_PAD = 640
IF_AT = A_W + M_W


def _regrouped(u):
    return u if u < IF_AT + 8 else u + (IF_W - 8)


def _selection(k, rows, row0, transpose):
    shape = (rows, IN_SHARD_PAD) if transpose else (IN_SHARD_PAD, rows)
    l = lax.broadcasted_iota(jnp.int32, shape, 1 if transpose else 0)
    r = lax.broadcasted_iota(jnp.int32, shape, 0 if transpose else 1) + row0
    u = l + IN_SHARD * k
    ru = u + jnp.where(u >= IF_AT + 8, IF_W - 8, 0)
    return ((ru == r) & (l < IN_SHARD)).astype(BF16)


def _regroup_w_in(g):
    def body(g_ref, o_ref):
        for cb in range(CAT_W // LANES):
            r0 = cb * LANES
            acc = jnp.zeros((D_MODEL, LANES), F32)
            for k in range(N_DEV):
                lo, hi = _regrouped(IN_SHARD * k), _regrouped(IN_SHARD * k + IN_SHARD - 1)
                if hi >= r0 and lo < r0 + LANES:
                    acc = acc + _dot(g_ref[k], _selection(k, LANES, r0, False))
            o_ref[:, r0:r0 + LANES] = acc.astype(BF16)

    return pl.pallas_call(
        body, name="regroup_w_in", out_shape=jax.ShapeDtypeStruct((D_MODEL, CAT_W), BF16),
        compiler_params=_params(),
    )(g)


def _ungroup_w_in(g_cat):
    def body(g_ref, o_ref):
        for k in range(N_DEV):
            lo, hi = _regrouped(IN_SHARD * k), _regrouped(IN_SHARD * k + IN_SHARD - 1)
            w0, w1 = lo // LANES * LANES, (hi // LANES + 1) * LANES
            o_ref[k] = _dot(g_ref[:, w0:w1], _selection(k, w1 - w0, w0, True)).astype(BF16)

    return pl.pallas_call(
        body, name="ungroup_w_in", out_shape=jax.ShapeDtypeStruct((N_DEV, D_MODEL, IN_SHARD_PAD), BF16),
        compiler_params=_params(),
    )(g_cat)


SMALL = (("b_ada", 6144), ("g_pre_mix", 1024), ("g_post_mix", 1024), ("g_pre_ffn", 1024), ("g_post_ffn", 1024),
         ("conv_b", 1024), ("mlstm_norm_w", 512), ("b_if", 8), ("attn_sinks", 8))
SMALL_W = 12032
SMALL_PAD = 12288


def _pack_small(vals):
    parts = []
    for n, width in SMALL:
        a = vals[n].reshape(-1)
        parts.append(jnp.pad(a, (0, LANES - width)) if width < LANES else a)
    return jnp.concatenate(parts)


def _unpack_small(vec):
    out, off = {}, 0
    for n, width in SMALL:
        out[n] = vec[off:off + width].reshape(1, width)
        off += max(width, LANES)
    return out


WEIGHT_NAMES = ("w_ada", "b_ada", "g_pre_mix", "g_post_mix", "w_in", "b_if", "conv_w", "conv_b", "attn_sinks",
                "mlstm_norm_w", "w_branch_attn", "w_branch_mlstm", "w_out", "g_pre_ffn", "g_post_ffn",
                "w_ffn_gate", "w_ffn_up", "w_ffn_down")


def kernel(x, c, positions, w_ada, b_ada, g_pre_mix, g_post_mix, w_in, b_if, conv_w, conv_b, attn_sinks, mlstm_norm_w, w_branch_attn, w_branch_mlstm, w_out, g_pre_ffn, g_post_ffn, w_ffn_gate, w_ffn_up, w_ffn_down, loss_target, m_w_ada, m_b_ada, m_g_pre_mix, m_g_post_mix, m_w_in, m_b_if, m_conv_w, m_conv_b, m_attn_sinks, m_mlstm_norm_w, m_w_branch_attn, m_w_branch_mlstm, m_w_out, m_g_pre_ffn, m_g_post_ffn, m_w_ffn_gate, m_w_ffn_up, m_w_ffn_down, v_w_ada, v_b_ada, v_g_pre_mix, v_g_post_mix, v_w_in, v_b_if, v_conv_w, v_conv_b, v_attn_sinks, v_mlstm_norm_w, v_w_branch_attn, v_w_branch_mlstm, v_w_out, v_g_pre_ffn, v_g_post_ffn, v_w_ffn_gate, v_w_ffn_up, v_w_ffn_down):
    given = dict(locals())
    W = {n: given[n][0] for n in WEIGHT_NAMES}
    M = {n: given["m_" + n][0] for n in WEIGHT_NAMES}
    V = {n: given["v_" + n][0] for n in WEIGHT_NAMES}
    me = 4 * lax.axis_index("x") + 2 * lax.axis_index("y") + lax.axis_index("c")

    ff_sh = D_FF // N_DEV
    st_a = _exchange_start([jnp.pad(W["w_in"], ((0, 0), (0, IN_SHARD_PAD - IN_SHARD))).astype(BF16),
                            jnp.pad(W["conv_w"], ((0, 4), (0, 0)))], True, c, "gather_in_start")

    cg = _small_exchange(_after(c, st_a["token"]).reshape(8, D_MODEL // 8), True, "gather_c")
    cmat = cg.reshape(N_DEV, D_MODEL)
    ada_w = D_MODEL * 6 // N_DEV
    b_cols = lax.dynamic_slice(W["b_ada"], (me * ada_w,), (ada_w,)).reshape(1, ada_w)
    mod_part = _mod_partial(cmat, W["w_ada"], b_cols)
    mod_recv = _small_exchange(jnp.broadcast_to(mod_part[:, None, :], (N_DEV, 8, ada_w)), False, "scatter_mod")
    mod = mod_recv[:, 0, :].reshape(6, D_MODEL)

    st_b = _exchange_start([W["w_branch_attn"].astype(BF16), W["w_branch_mlstm"].astype(BF16), W["w_out"].astype(BF16),
                            W["w_ffn_gate"].T.astype(BF16), W["w_ffn_up"].T.astype(BF16), W["w_ffn_down"].astype(BF16)],
                           True, mod_recv, "gather_rest_start")
    g_in, g_conv = _exchange_wait(st_a, st_b["token"], "gather_in_wait")
    cols = lambda g: g.transpose(1, 0, 2).reshape(g.shape[1], N_DEV * g.shape[2])
    gains = jnp.stack([W["g_pre_mix"], W["g_post_mix"], W["g_pre_ffn"], W["g_post_ffn"]])
    xs, tgt = x[0], loss_target[0]
    t = _tables(mod, gains, cols(g_conv)[0:4], W["conv_b"], W["b_if"], W["mlstm_norm_w"], positions[0])
    w_cat = _regroup_w_in(g_in)
    a = _mixer_fwd(xs, t, W["attn_sinks"], w_cat)
    g_ba, g_bm, g_out, g_gate, g_up, g_down = _exchange_wait(st_b, a["ym"], "gather_rest_wait")
    w_ba, w_bm, w_out = cols(g_ba), cols(g_bm), g_out.reshape(D_MODEL, D_MODEL)
    b = _ffn_part(xs, tgt, t, a, w_ba, w_bm, w_out, g_gate.reshape(D_FF, D_MODEL), g_up.reshape(D_FF, D_MODEL),
                  g_down.reshape(D_FF, D_MODEL))

    st_f = _exchange_start([b["g_w_gate"].reshape(N_DEV, ff_sh, D_MODEL), b["g_w_up"].reshape(N_DEV, ff_sh, D_MODEL),
                            b["g_w_down"].reshape(N_DEV, ff_sh, D_MODEL)], False, b["dx1"], "scatter_ffn_start")
    cm = _mixer_bwd(_after(b["dx1"], st_f["token"]), t, a, b, W["attn_sinks"], w_ba, w_bm, w_out)
    pieces = lambda g, n: g.reshape(g.shape[0], N_DEV, n).transpose(1, 0, 2)
    st_m = _exchange_start([_ungroup_w_in(cm["g_w_cat"]), pieces(cm["g_w_ba"], 128), pieces(cm["g_w_bm"], 128),
                            cm["g_w_out"].reshape(N_DEV, D_MODEL // N_DEV, D_MODEL),
                            jnp.pad(pieces(cm["dcw"][0:4], 128), ((0, 0), (0, 4), (0, 0)))], False, cm["dproj"],
                           "scatter_mixer_start")
    r_gate, r_up, r_down = _exchange_wait(st_f, st_m["token"], "scatter_ffn_wait")
    grad_x, acc_p = _pre_bwd(_after(cm["dproj"], st_m["token"]), xs, b["dx1"], t["vecs"], w_cat)
    small = _small_grads(acc_p, b, cm)
    loss = b["loss"]

    big_out = [{} for _ in range(4)]

    def put(n, res):
        for k in range(4):
            big_out[k][n] = res[k][None]

    put("w_ffn_down", _adamw_sum(r_down, W["w_ffn_down"], M["w_ffn_down"], V["w_ffn_down"], "adamw_w_ffn_down"))
    for n, g in (("w_ffn_gate", _sum8(r_gate, "sum_w_ffn_gate").T), ("w_ffn_up", _sum8(r_up, "sum_w_ffn_up").T)):
        put(n, [g] + list(_adamw_plain(g, W[n], M[n], V[n], "adamw_" + n)))

    part = {"b_ada": small["dmod"], "g_pre_mix": small["dgains"][0], "g_post_mix": small["dgains"][1],
            "g_pre_ffn": small["dgains"][2], "g_post_ffn": small["dgains"][3], "conv_b": small["dconv_b"],
            "mlstm_norm_w": small["dnorm_w"], "b_if": small["db_if"], "attn_sinks": small["dsinks"]}
    rows8 = lambda d: jnp.pad(_pack_small(d), (0, SMALL_PAD - SMALL_W)).reshape(8, SMALL_PAD // 8)
    sg = _small_exchange(rows8(part), True, "gather_small")
    small_out = [_unpack_small(o.reshape(-1)) for o in _adamw_small(sg, rows8(W), rows8(M), rows8(V))]
    dmod_cols = lax.dynamic_slice(sg.reshape(N_DEV, SMALL_PAD), (0, me * ada_w), (N_DEV, ada_w))
    ada_out = _adamw_w_ada(cmat, dmod_cols, W["w_ada"], M["w_ada"], V["w_ada"])

    r_in, r_ba, r_bm, r_out, r_conv = _exchange_wait(st_m, grad_x, "scatter_mixer_wait")
    for n, r in (("w_branch_attn", r_ba), ("w_branch_mlstm", r_bm), ("w_out", r_out)):
        put(n, _adamw_sum(r, W[n], M[n], V[n], "adamw_" + n))
    pad4 = lambda v: jnp.pad(v, ((0, 4), (0, 0)))
    put("conv_w", [o[0:4] for o in _adamw_sum(r_conv, pad4(W["conv_w"]), pad4(M["conv_w"]), pad4(V["conv_w"]),
                                                 "adamw_conv_w")])
    g = _sum8(r_in, "sum_w_in")[:, 0:IN_SHARD]
    put("w_in", [g] + list(_adamw_plain(g, W["w_in"], M["w_in"], V["w_in"], "adamw_w_in")))

    total = lax.psum(loss, ("x", "y", "c"))
    outs = [total, grad_x[None]]
    for k in range(4):
        for n in WEIGHT_NAMES:
            if n == "w_ada":
                outs.append(ada_out[k][None])
            elif n in big_out[k]:
                outs.append(big_out[k][n])
            else:
                outs.append(small_out[k][n])
    return tuple(outs)
```

```python
import functools

import jax
import jax.numpy as jnp
import numpy as np
from jax import lax
from jax.experimental import pallas as pl
from jax.experimental.pallas import tpu as pltpu

F32 = jnp.float32
BF16 = jnp.bfloat16

N_DEV = 8
D_MODEL = 1024
D_FF = 2816
N_Q_HEADS = 8
HEAD_DIM = 64
ATTN_BLOCK = 128
ROPE_THETA = 10000.0
MLSTM_HEADS = 4
MLSTM_HEAD_DIM = 128
MLSTM_CHUNK = 128
NORM_EPS = 1e-6
ADAM_LR = 0.001
ADAM_B1 = 0.9
ADAM_B2 = 0.999
ADAM_EPS = 1e-08
ADAM_WD = 0.01
ADAM_STEP = 10

ROW_TILE = 256
LANES = 128
NEG = -1e30
VMEM_LIMIT = 56 * 1024 * 1024

A_W = 768
M_W = 2048
IF_W = 128
G_W = 2048
CAT_W = A_W + M_W + IF_W + G_W

R_SHIFT_M, R_SCALE_M, R_GATE_M, R_SHIFT_F, R_SCALE_F, R_GATE_F = 0, 1, 2, 3, 4, 5
R_G_PRE_MIX, R_G_POST_MIX, R_G_PRE_FFN, R_G_POST_FFN = 6, 7, 8, 9


def _dot(a, b):
    return jnp.dot(a, b, preferred_element_type=F32)


def _dot_nt(a, b):
    return lax.dot_general(a, b, (((1,), (1,)), ((), ())), preferred_element_type=F32)


def _dot_tn(a, b):
    return lax.dot_general(a, b, (((0,), (0,)), ((), ())), preferred_element_type=F32)


def _sigmoid(x):
    return 1.0 / (1.0 + jnp.exp(-x))


def _colsum(x):
    return jnp.sum(x, axis=0, keepdims=True)


def _rowmean(x):
    return jnp.mean(x, axis=-1, keepdims=True)


def _params(sem=None, vmem=VMEM_LIMIT):
    kw = dict(vmem_limit_bytes=vmem)
    if sem is not None:
        kw["dimension_semantics"] = sem
    return pltpu.CompilerParams(**kw)


def _full(shape):
    nd = len(shape)
    return pl.BlockSpec(shape, lambda *_: (0,) * nd)


def _pre_proj(x, vecs, w_cat):
    S = x.shape[0]
    tm = ROW_TILE

    def body(x_ref, v_ref, w_ref, h_ref, pa_ref, pm_ref, pif_ref, pg_ref):
        xv = x_ref[...]
        r = lax.rsqrt(_rowmean(xv * xv) + NORM_EPS)
        h = (xv * r * v_ref[R_G_PRE_MIX:R_G_PRE_MIX + 1, :]) * (1.0 + v_ref[R_SCALE_M:R_SCALE_M + 1, :]) \
            + v_ref[R_SHIFT_M:R_SHIFT_M + 1, :]
        hb = h.astype(BF16)
        h_ref[...] = hb
        pa_ref[...] = _dot(hb, w_ref[:, 0:A_W])
        pm_ref[...] = _dot(hb, w_ref[:, A_W:A_W + M_W])
        pif_ref[...] = _dot(hb, w_ref[:, A_W + M_W:A_W + M_W + IF_W])
        pg_ref[...] = _dot(hb, w_ref[:, A_W + M_W + IF_W:CAT_W])

    row = lambda w: pl.BlockSpec((tm, w), lambda i: (i, 0))
    return pl.pallas_call(
        body, name="pre_proj", grid=(S // tm,),
        in_specs=[row(D_MODEL), _full(vecs.shape), _full(w_cat.shape)],
        out_specs=[row(D_MODEL), row(A_W), row(M_W), row(IF_W), row(G_W)],
        out_shape=[jax.ShapeDtypeStruct((S, D_MODEL), BF16), jax.ShapeDtypeStruct((S, A_W), F32),
                   jax.ShapeDtypeStruct((S, M_W), F32), jax.ShapeDtypeStruct((S, IF_W), F32),
                   jax.ShapeDtypeStruct((S, G_W), F32)],
        compiler_params=_params(("parallel",)),
    )(x, vecs, w_cat)


def _mix_fwd(x, ya, ym, pg, vecs, w_ba, w_bm, w_out):
    S = x.shape[0]
    tm = ROW_TILE

    def body(x_ref, ya_ref, ym_ref, pg_ref, v_ref, wba_ref, wbm_ref, wout_ref,
             x1_ref, merged_ref, mix_ref, pa_ref, pb_ref):
        pa = _dot(ya_ref[...], wba_ref[...])
        pb = _dot(ym_ref[...], wbm_ref[...])
        merged = _sigmoid(pg_ref[:, 0:D_MODEL]) * pa + _sigmoid(pg_ref[:, D_MODEL:G_W]) * pb
        mb = merged.astype(BF16)
        mix = _dot(mb, wout_ref[...])
        r = lax.rsqrt(_rowmean(mix * mix) + NORM_EPS)
        x1_ref[...] = x_ref[...] + v_ref[R_GATE_M:R_GATE_M + 1, :] * (mix * r * v_ref[R_G_POST_MIX:R_G_POST_MIX + 1, :])
        merged_ref[...] = mb
        mix_ref[...] = mix
        pa_ref[...] = pa.astype(BF16)
        pb_ref[...] = pb.astype(BF16)

    row = lambda w: pl.BlockSpec((tm, w), lambda i: (i, 0))
    sd = lambda w, dt: jax.ShapeDtypeStruct((S, w), dt)
    return pl.pallas_call(
        body, name="mix_fwd", grid=(S // tm,),
        in_specs=[row(D_MODEL), row(512), row(512), row(G_W), _full(vecs.shape), _full(w_ba.shape),
                  _full(w_bm.shape), _full(w_out.shape)],
        out_specs=[row(D_MODEL)] * 5,
        out_shape=[sd(D_MODEL, F32), sd(D_MODEL, BF16), sd(D_MODEL, F32), sd(D_MODEL, BF16), sd(D_MODEL, BF16)],
        compiler_params=_params(("parallel",)),
    )(x, ya, ym, pg, vecs, w_ba, w_bm, w_out)


def _ffn_fwd_bwd(x1, tgt, vecs, w_gate, w_up, w_down):
    S = x1.shape[0]
    tm = ROW_TILE

    def body(x1_ref, tgt_ref, v_ref, wg_hbm, wu_hbm, wd_hbm,
             dx1_ref, h2_ref, hid_ref, da_ref, du_ref, dff_ref, acc_ref, loss_ref,
             wg, wu, wd, sem):
        i = pl.program_id(0)

        @pl.when(i == 0)
        def _():
            cps = [pltpu.make_async_copy(wg_hbm, wg, sem.at[0]), pltpu.make_async_copy(wu_hbm, wu, sem.at[1]),
                   pltpu.make_async_copy(wd_hbm, wd, sem.at[2])]
            for cp in cps:
                cp.start()
            for cp in cps:
                cp.wait()
            acc_ref[...] = jnp.zeros_like(acc_ref)
            loss_ref[...] = jnp.zeros_like(loss_ref)

        vrow = lambda r: v_ref[r:r + 1, :]
        x1v = x1_ref[...]
        r3 = lax.rsqrt(_rowmean(x1v * x1v) + NORM_EPS)
        x1hat = x1v * r3
        xn3 = x1hat * vrow(R_G_PRE_FFN)
        h2b = (xn3 * (1.0 + vrow(R_SCALE_F)) + vrow(R_SHIFT_F)).astype(BF16)
        h2_ref[...] = h2b
        a = _dot_nt(h2b, wg[...])
        u = _dot_nt(h2b, wu[...])
        sg = _sigmoid(a)
        sil = a * sg
        hidb = (sil * u).astype(BF16)
        hid_ref[...] = hidb
        ff = _dot(hidb, wd[...])
        r4 = lax.rsqrt(_rowmean(ff * ff) + NORM_EPS)
        ffhat = ff * r4
        n4 = ffhat * vrow(R_G_POST_FFN)
        err = x1v + vrow(R_GATE_F) * n4 - tgt_ref[...]
        loss_ref[...] += jnp.sum(err * err) * (0.5 / D_MODEL)
        dy = err * (1.0 / D_MODEL)
        acc_ref[0:1, :] += _colsum(dy * n4)
        dn4 = dy * vrow(R_GATE_F)
        acc_ref[1:2, :] += _colsum(dn4 * ffhat)
        dffhat = dn4 * vrow(R_G_POST_FFN)
        dffb = (r4 * (dffhat - ffhat * _rowmean(dffhat * ffhat))).astype(BF16)
        dff_ref[...] = dffb
        dhid = _dot_nt(dffb, wd[...])
        dub = (dhid * sil).astype(BF16)
        dab = (dhid * u * (sg * (1.0 + a * (1.0 - sg)))).astype(BF16)
        da_ref[...] = dab
        du_ref[...] = dub
        dh2 = _dot(dab, wg[...]) + _dot(dub, wu[...])
        acc_ref[2:3, :] += _colsum(dh2 * xn3)
        acc_ref[3:4, :] += _colsum(dh2)
        dxn3 = dh2 * (1.0 + vrow(R_SCALE_F))
        acc_ref[4:5, :] += _colsum(dxn3 * x1hat)
        dx1hat = dxn3 * vrow(R_G_PRE_FFN)
        dx1_ref[...] = dy + r3 * (dx1hat - x1hat * _rowmean(dx1hat * x1hat))

    row = lambda w: pl.BlockSpec((tm, w), lambda i: (i, 0))
    sd = lambda w, dt: jax.ShapeDtypeStruct((S, w), dt)
    anyspec = pl.BlockSpec(memory_space=pl.ANY)
    return pl.pallas_call(
        body, name="ffn_fwd_bwd", grid=(S // tm,),
        in_specs=[row(D_MODEL), row(D_MODEL), _full(vecs.shape), anyspec, anyspec, anyspec],
        out_specs=[row(D_MODEL), row(D_MODEL), row(D_FF), row(D_FF), row(D_FF), row(D_MODEL),
                   _full((8, D_MODEL)), _full((8, LANES))],
        out_shape=[sd(D_MODEL, F32), sd(D_MODEL, BF16), sd(D_FF, BF16), sd(D_FF, BF16), sd(D_FF, BF16),
                   sd(D_MODEL, BF16), jax.ShapeDtypeStruct((8, D_MODEL), F32), jax.ShapeDtypeStruct((8, LANES), F32)],
        scratch_shapes=[pltpu.VMEM(w_gate.shape, BF16), pltpu.VMEM(w_up.shape, BF16), pltpu.VMEM(w_down.shape, BF16),
                        pltpu.SemaphoreType.DMA((3,))],
        compiler_params=_params(("arbitrary",)),
    )(x1, tgt, vecs, w_gate, w_up, w_down)


def _mix_bwd(dx1, mix, pa, pb, pg, vecs, w_ba, w_bm, w_out):
    S = dx1.shape[0]
    tm = ROW_TILE

    def body(dx1_ref, mix_ref, pa_ref, pb_ref, pg_ref, v_ref, wba_ref, wbm_ref, wout_ref,
             dmix_ref, dpa_ref, dpb_ref, dg_ref, dya_ref, dym_ref, acc_ref):
        i = pl.program_id(0)

        @pl.when(i == 0)
        def _():
            acc_ref[...] = jnp.zeros_like(acc_ref)

        vrow = lambda r: v_ref[r:r + 1, :]
        dx1v = dx1_ref[...]
        mix = mix_ref[...]
        r2 = lax.rsqrt(_rowmean(mix * mix) + NORM_EPS)
        mixhat = mix * r2
        acc_ref[0:1, :] += _colsum(dx1v * (mixhat * vrow(R_G_POST_MIX)))
        dn2 = dx1v * vrow(R_GATE_M)
        acc_ref[1:2, :] += _colsum(dn2 * mixhat)
        dmixhat = dn2 * vrow(R_G_POST_MIX)
        dmixb = (r2 * (dmixhat - mixhat * _rowmean(dmixhat * mixhat))).astype(BF16)
        dmix_ref[...] = dmixb
        dmerged = _dot_nt(dmixb, wout_ref[...])
        sa = _sigmoid(pg_ref[:, 0:D_MODEL])
        sm = _sigmoid(pg_ref[:, D_MODEL:G_W])
        dpab = (dmerged * sa).astype(BF16)
        dpbb = (dmerged * sm).astype(BF16)
        dpa_ref[...] = dpab
        dpb_ref[...] = dpbb
        dg_ref[:, 0:D_MODEL] = (dmerged * pa_ref[...].astype(F32) * (sa * (1.0 - sa))).astype(BF16)
        dg_ref[:, D_MODEL:G_W] = (dmerged * pb_ref[...].astype(F32) * (sm * (1.0 - sm))).astype(BF16)
        dya_ref[...] = _dot_nt(dpab, wba_ref[...])
        dym_ref[...] = _dot_nt(dpbb, wbm_ref[...])

    row = lambda w: pl.BlockSpec((tm, w), lambda i: (i, 0))
    sd = lambda w, dt: jax.ShapeDtypeStruct((S, w), dt)
    return pl.pallas_call(
        body, name="mix_bwd", grid=(S // tm,),
        in_specs=[row(D_MODEL), row(D_MODEL), row(D_MODEL), row(D_MODEL), row(G_W), _full(vecs.shape),
                  _full(w_ba.shape), _full(w_bm.shape), _full(w_out.shape)],
        out_specs=[row(D_MODEL), row(D_MODEL), row(D_MODEL), row(G_W), row(512), row(512), _full((8, D_MODEL))],
        out_shape=[sd(D_MODEL, BF16), sd(D_MODEL, BF16), sd(D_MODEL, BF16), sd(G_W, BF16), sd(512, F32), sd(512, F32),
                   jax.ShapeDtypeStruct((8, D_MODEL), F32)],
        compiler_params=_params(("arbitrary",)),
    )(dx1, mix, pa, pb, pg, vecs, w_ba, w_bm, w_out)


def _pre_bwd(dproj, x, dx1, vecs, w_cat):
    S = x.shape[0]
    tm = ROW_TILE

    def body(dp_ref, x_ref, dx1_ref, v_ref, w_ref, dx_ref, acc_ref):
        i = pl.program_id(0)

        @pl.when(i == 0)
        def _():
            acc_ref[...] = jnp.zeros_like(acc_ref)

        vrow = lambda r: v_ref[r:r + 1, :]
        dh = _dot_nt(dp_ref[...], w_ref[...])
        xv = x_ref[...]
        r1 = lax.rsqrt(_rowmean(xv * xv) + NORM_EPS)
        xhat = xv * r1
        acc_ref[0:1, :] += _colsum(dh * (xhat * vrow(R_G_PRE_MIX)))
        acc_ref[1:2, :] += _colsum(dh)
        dxn = dh * (1.0 + vrow(R_SCALE_M))
        acc_ref[2:3, :] += _colsum(dxn * xhat)
        dxhat = dxn * vrow(R_G_PRE_MIX)
        dx_ref[...] = dx1_ref[...] + r1 * (dxhat - xhat * _rowmean(dxhat * xhat))

    row = lambda w: pl.BlockSpec((tm, w), lambda i: (i, 0))
    return pl.pallas_call(
        body, name="pre_bwd", grid=(S // tm,),
        in_specs=[row(CAT_W), row(D_MODEL), row(D_MODEL), _full(vecs.shape), _full(w_cat.shape)],
        out_specs=[row(D_MODEL), _full((8, D_MODEL))],
        out_shape=[jax.ShapeDtypeStruct((S, D_MODEL), F32), jax.ShapeDtypeStruct((8, D_MODEL), F32)],
        compiler_params=_params(("arbitrary",)),
    )(dproj, x, dx1, vecs, w_cat)


def _matmul_tn(a, b, tn, name, ts=512):
    S, K = a.shape
    N = b.shape[1]
    n_s = S // ts

    def body(a_ref, b_ref, o_ref, acc_ref):
        s = pl.program_id(1)

        @pl.when(s == 0)
        def _():
            acc_ref[...] = jnp.zeros_like(acc_ref)

        acc_ref[...] += _dot_tn(a_ref[...], b_ref[...])

        @pl.when(s == n_s - 1)
        def _():
            o_ref[...] = acc_ref[...].astype(BF16)

    return pl.pallas_call(
        body, name=name, grid=(N // tn, n_s),
        in_specs=[pl.BlockSpec((ts, K), lambda j, s: (s, 0)), pl.BlockSpec((ts, tn), lambda j, s: (s, j))],
        out_specs=pl.BlockSpec((K, tn), lambda j, s: (0, j)),
        out_shape=jax.ShapeDtypeStruct((K, N), BF16),
        scratch_shapes=[pltpu.VMEM((K, tn), F32)],
        compiler_params=_params(("parallel", "arbitrary")),
    )(a, b)


def _rope_swap(t):
    lane = lax.broadcasted_iota(jnp.int32, t.shape, 1)
    first = (lane & (HEAD_DIM - 1)) < (HEAD_DIM // 2)
    return jnp.where(first, pltpu.roll(t, LANES - HEAD_DIM // 2, 1), pltpu.roll(t, HEAD_DIM // 2, 1))


def _rope(t, cos, sin_signed):
    return t * cos + _rope_swap(t) * sin_signed


def _rope_t(d, cos, sin_signed):
    return d * cos + _rope_swap(d * sin_signed)


def _to_kv_lanes(chunk, p, h):
    lane = lax.broadcasted_iota(jnp.int32, chunk.shape, 1)
    src = chunk if p == h else pltpu.roll(chunk, HEAD_DIM, 1)
    return jnp.where((lane >> 6) == h, src, jnp.zeros_like(src))


def _from_kv_lanes(o_a, o_b, h):
    lane = lax.broadcasted_iota(jnp.int32, o_a.shape, 1)
    a = o_a if h == 0 else pltpu.roll(o_a, HEAD_DIM, 1)
    b = o_b if h == 1 else pltpu.roll(o_b, HEAD_DIM, 1)
    return jnp.where(lane < HEAD_DIM, a, b)


def _band_mask(n):
    blk = ATTN_BLOCK
    qi = lax.broadcasted_iota(jnp.int32, (4 * blk, 2 * blk), 0) & (blk - 1)
    kj = lax.broadcasted_iota(jnp.int32, (4 * blk, 2 * blk), 1)
    return (kj > qi) & (kj <= qi + blk) & ((n > 0) | (kj >= blk))


def _stack_heads(chunks, h, dtype):
    parts = []
    for g in range(4):
        j = 4 * h + g
        parts.append(_to_kv_lanes(chunks[j // 2], j % 2, h))
    return jnp.concatenate(parts, axis=0).astype(dtype)


def _attn_fwd(pa, cos, sin, sinks):
    S = pa.shape[0]
    blk = ATTN_BLOCK
    nb = S // blk

    def body(sink_ref, cur_ref, prev_ref, cos_ref, sin_ref, cosp_ref, sinp_ref,
             ya_ref, qr_ref, kr_ref, vb_ref, lse_ref):
        n = pl.program_id(0)
        cos_c, sin_c = cos_ref[...], sin_ref[...]
        qch = [_rope(cur_ref[:, c * LANES:(c + 1) * LANES], cos_c, sin_c) * (HEAD_DIM ** -0.5) for c in range(4)]
        for c in range(4):
            qr_ref[:, c * LANES:(c + 1) * LANES] = qch[c].astype(BF16)
        k_cur = _rope(cur_ref[:, 512:640], cos_c, sin_c).astype(BF16)
        k_prev = _rope(prev_ref[:, 0:LANES], cosp_ref[...], sinp_ref[...]).astype(BF16)
        v_cur = cur_ref[:, 640:768].astype(BF16)
        v_prev = prev_ref[:, LANES:2 * LANES].astype(BF16)
        kr_ref[...] = k_cur
        vb_ref[...] = v_cur
        K = jnp.concatenate([k_prev, k_cur], axis=0)
        V = jnp.concatenate([v_prev, v_cur], axis=0)
        mask = _band_mask(n)
        rowg = lax.broadcasted_iota(jnp.int32, (4 * blk, 1), 0) >> 7
        lane = lax.broadcasted_iota(jnp.int32, (blk, LANES), 1)
        lse_tile = jnp.zeros((blk, LANES), F32)
        outs = []
        for h in range(2):
            qs = _stack_heads(qch, h, BF16)
            s = jnp.where(mask, _dot_nt(qs, K), NEG)
            sink = jnp.zeros((4 * blk, 1), F32)
            for g in range(4):
                sink = jnp.where(rowg == g, sink_ref[4 * h + g], sink)
            m = jnp.maximum(jnp.max(s, axis=1, keepdims=True), sink)
            p = jnp.exp(s - m)
            den = jnp.sum(p, axis=1, keepdims=True) + jnp.exp(sink - m)
            o = _dot((p / den).astype(BF16), V)
            lse = m + jnp.log(den)
            for g in range(4):
                outs.append(o[g * blk:(g + 1) * blk, :])
                lse_tile = jnp.where(lane == 4 * h + g, lse[g * blk:(g + 1) * blk, :], lse_tile)
        for c in range(4):
            ya_ref[:, c * LANES:(c + 1) * LANES] = _from_kv_lanes(outs[2 * c], outs[2 * c + 1], c // 2).astype(BF16)
        lse_ref[...] = lse_tile

    prev = lambda n: jnp.maximum(n - 1, 0)
    sd = lambda w, dt: jax.ShapeDtypeStruct((S, w), dt)
    return pl.pallas_call(
        body, name="attn_fwd", grid=(nb,),
        in_specs=[pl.BlockSpec(memory_space=pltpu.SMEM),
                  pl.BlockSpec((blk, A_W), lambda n: (n, 0)),
                  pl.BlockSpec((blk, 256), lambda n: (prev(n), 2)),
                  pl.BlockSpec((blk, LANES), lambda n: (n, 0)), pl.BlockSpec((blk, LANES), lambda n: (n, 0)),
                  pl.BlockSpec((blk, LANES), lambda n: (prev(n), 0)), pl.BlockSpec((blk, LANES), lambda n: (prev(n), 0))],
        out_specs=[pl.BlockSpec((blk, 512), lambda n: (n, 0)), pl.BlockSpec((blk, 512), lambda n: (n, 0)),
                   pl.BlockSpec((blk, LANES), lambda n: (n, 0)), pl.BlockSpec((blk, LANES), lambda n: (n, 0)),
                   pl.BlockSpec((blk, LANES), lambda n: (n, 0))],
        out_shape=[sd(512, BF16), sd(512, BF16), sd(LANES, BF16), sd(LANES, BF16), sd(LANES, F32)],
        compiler_params=_params(("parallel",)),
    )(sinks, pa, pa, cos, sin, cos, sin)


def _attn_bwd(dya, qr, kr, vb, lse, cos, sin, sinks):
    S = dya.shape[0]
    blk = ATTN_BLOCK
    nb = S // blk

    def body(sink_ref, dya_ref, qr_ref, kc_ref, kp_ref, vc_ref, vp_ref, lse_ref, cos_ref, sin_ref, cosp_ref, sinp_ref,
             dq_ref, dkv_ref, dsink_ref, ck, cv):
        n = pl.program_id(0)

        @pl.when(n == 0)
        def _():
            ck[...] = jnp.zeros_like(ck)
            cv[...] = jnp.zeros_like(cv)
            dsink_ref[...] = jnp.zeros_like(dsink_ref)

        @pl.when(n < nb)
        def _():
            K = jnp.concatenate([kp_ref[...], kc_ref[...]], axis=0)
            V = jnp.concatenate([vp_ref[...], vc_ref[...]], axis=0)
            qch = [qr_ref[:, c * LANES:(c + 1) * LANES] for c in range(4)]
            dch = [dya_ref[:, c * LANES:(c + 1) * LANES] for c in range(4)]
            lse_tile = lse_ref[...]
            mask = _band_mask(n)
            rowg = lax.broadcasted_iota(jnp.int32, (4 * blk, 1), 0) >> 7
            lane8 = lax.broadcasted_iota(jnp.int32, (8, LANES), 1)
            dk_acc = jnp.zeros((2 * blk, LANES), F32)
            dv_acc = jnp.zeros((2 * blk, LANES), F32)
            dsink = jnp.zeros((8, LANES), F32)
            dqs = []
            for h in range(2):
                qs = _stack_heads(qch, h, BF16)
                dos = _stack_heads(dch, h, BF16)
                lse_col = jnp.concatenate([lse_tile[:, 4 * h + g:4 * h + g + 1] for g in range(4)], axis=0)
                p = jnp.where(mask, jnp.exp(_dot_nt(qs, K) - lse_col), 0.0)
                dp = _dot_nt(dos, V)
                delta = jnp.sum(p * dp, axis=1, keepdims=True)
                dsb = (p * (dp - delta)).astype(BF16)
                dq = _dot(dsb, K)
                dk_acc = dk_acc + _dot_tn(dsb, qs)
                dv_acc = dv_acc + _dot_tn(p.astype(BF16), dos)
                sink = jnp.zeros((4 * blk, 1), F32)
                for g in range(4):
                    sink = jnp.where(rowg == g, sink_ref[4 * h + g], sink)
                ps_delta = jnp.exp(sink - lse_col) * delta
                for g in range(4):
                    dqs.append(dq[g * blk:(g + 1) * blk, :])
                    dsink = jnp.where(lane8 == 4 * h + g, dsink - jnp.sum(ps_delta[g * blk:(g + 1) * blk, :]), dsink)
            dsink_ref[...] += dsink
            cos_c, sin_c = cos_ref[...], sin_ref[...]
            for c in range(4):
                dqc = _from_kv_lanes(dqs[2 * c], dqs[2 * c + 1], c // 2) * (HEAD_DIM ** -0.5)
                dq_ref[:, c * LANES:(c + 1) * LANES] = _rope_t(dqc, cos_c, sin_c).astype(BF16)
            dkv_ref[:, 0:LANES] = _rope_t(dk_acc[0:blk, :] + ck[...], cosp_ref[...], sinp_ref[...]).astype(BF16)
            dkv_ref[:, LANES:2 * LANES] = (dv_acc[0:blk, :] + cv[...]).astype(BF16)
            ck[...] = dk_acc[blk:2 * blk, :]
            cv[...] = dv_acc[blk:2 * blk, :]

        @pl.when(n == nb)
        def _():
            dkv_ref[:, 0:LANES] = _rope_t(ck[...], cosp_ref[...], sinp_ref[...]).astype(BF16)
            dkv_ref[:, LANES:2 * LANES] = cv[...].astype(BF16)

    cur = lambda n: jnp.minimum(n, nb - 1)
    prev = lambda n: jnp.maximum(n - 1, 0)
    bs = lambda w, f: pl.BlockSpec((blk, w), lambda n: (f(n), 0))
    return pl.pallas_call(
        body, name="attn_bwd", grid=(nb + 1,),
        in_specs=[pl.BlockSpec(memory_space=pltpu.SMEM),
                  bs(512, cur), bs(512, cur), bs(LANES, cur), bs(LANES, prev), bs(LANES, cur), bs(LANES, prev),
                  bs(LANES, cur), bs(LANES, cur), bs(LANES, cur), bs(LANES, prev), bs(LANES, prev)],
        out_specs=[bs(512, cur), bs(256, prev), _full((8, LANES))],
        out_shape=[jax.ShapeDtypeStruct((S, 512), BF16), jax.ShapeDtypeStruct((S, 256), BF16),
                   jax.ShapeDtypeStruct((8, LANES), F32)],
        scratch_shapes=[pltpu.VMEM((blk, LANES), F32), pltpu.VMEM((blk, LANES), F32)],
        compiler_params=_params(("arbitrary",)),
    )(sinks, dya, qr, kr, kr, vb, vb, lse, cos, sin, cos, sin)


def _split3(x):
    hi = x.astype(BF16)
    r1 = x - hi.astype(F32)
    mid = r1.astype(BF16)
    lo = (r1 - mid.astype(F32)).astype(BF16)
    return hi, mid, lo


def _tri_matmul(tri_b, x):
    hi, mid, lo = _split3(x)
    return _dot(tri_b, hi) + _dot(tri_b, mid) + _dot(tri_b, lo)


def _log_sigmoid(x):
    return jnp.minimum(x, 0.0) - jnp.log(1.0 + jnp.exp(-jnp.abs(x)))


def _conv_silu_fwd(cur, prev, cw_ref, first):
    L = cur.shape[0]
    row = lax.broadcasted_iota(jnp.int32, cur.shape, 0)
    prev = jnp.where(first, jnp.zeros_like(prev), prev)
    shifted = [cur]
    for k in range(1, 4):
        shifted.append(jnp.where(row < k, pltpu.roll(prev, k, 0), pltpu.roll(cur, k, 0)))
    z = cw_ref[4:5, :]
    for k in range(3, -1, -1):
        z = z + shifted[k] * cw_ref[3 - k:4 - k, :]
    return z, shifted


def _mlstm_head_fwd(qh, kh, vh, i_col, b_col, C_prev, n_prev, m_prev, tri, eye):
    L = qh.shape[0]
    col2row = lambda x: jnp.sum(jnp.where(eye, x, 0.0), axis=0, keepdims=True)
    b_row = col2row(b_col)
    i_row = col2row(i_col)
    bl = b_col[L - 1:L, :]
    Dm = jnp.where(tri, b_col - b_row + i_row, NEG)
    inter = b_col + m_prev
    m_t = jnp.maximum(inter, jnp.max(Dm, axis=1, keepdims=True))
    W = jnp.exp(Dm - m_t)
    e_t = jnp.exp(inter - m_t)
    qb, kb, vb = qh.astype(BF16), kh.astype(BF16), vh.astype(BF16)
    Sc = _dot_nt(qb, kb) * W
    P1 = _dot(qb, C_prev.astype(BF16))
    num = _dot(Sc.astype(BF16), vb) + e_t * P1
    qn = jnp.sum(qh * n_prev, axis=1, keepdims=True)
    den = jnp.sum(Sc, axis=1, keepdims=True) + e_t * qn
    floor = jnp.exp(-m_t)
    g = jnp.maximum(jnp.abs(den), floor)
    hv = num / g
    a_col = bl - b_col + i_col
    m_new = jnp.maximum(bl + m_prev, jnp.max(a_col, axis=0, keepdims=True))
    dec = jnp.exp(bl + m_prev - m_new)
    u_col = jnp.exp(a_col - m_new)
    return dict(W=W, e_t=e_t, qb=qb, kb=kb, vb=vb, Sc=Sc, P1=P1, qn=qn, den=den, floor=floor, g=g, hv=hv,
                m_new=m_new, dec=dec, u_col=u_col)


def _mlstm_fwd(pm, pif, cw, sv):
    S = pm.shape[0]
    L = MLSTM_CHUNK
    nc = S // L
    HD = MLSTM_HEAD_DIM
    W4 = MLSTM_HEADS * HD

    def body(cur_ref, prev_ref, pif_ref, cw_ref, sv_ref, ym_ref, cst_ref, nst_ref, C, nm):
        c = pl.program_id(0)

        @pl.when(c == 0)
        def _():
            C[...] = jnp.zeros_like(C)
            nm[...] = jnp.zeros_like(nm)

        z, _ = _conv_silu_fwd(cur_ref[:, 0:2 * W4], prev_ref[...], cw_ref, c == 0)
        qk = z * _sigmoid(z)
        gt = pif_ref[...] + sv_ref[1:2, 0:LANES]
        r_i = lax.broadcasted_iota(jnp.int32, (L, L), 0)
        c_i = lax.broadcasted_iota(jnp.int32, (L, L), 1)
        tri = c_i <= r_i
        eye = c_i == r_i
        b_all = _tri_matmul(tri.astype(BF16), _log_sigmoid(gt))
        nst_ref[0] = nm[...]
        for h in range(MLSTM_HEADS):
            sl = slice(h * HD, (h + 1) * HD)
            qh = qk[:, sl]
            kh = qk[:, W4 + h * HD:W4 + (h + 1) * HD] * (HD ** -0.5)
            vh = cur_ref[:, 2 * W4 + h * HD:2 * W4 + (h + 1) * HD]
            n_prev = nm[h:h + 1, :]
            m_prev = nm[4 + h:5 + h, 0:1]
            C_prev = C[h]
            cst_ref[0, h] = C_prev
            f = _mlstm_head_fwd(qh, kh, vh, gt[:, h:h + 1], b_all[:, 4 + h:5 + h], C_prev, n_prev, m_prev, tri, eye)
            hv = f["hv"]
            xc = hv - _rowmean(hv)
            hhat = xc * lax.rsqrt(_rowmean(xc * xc) + NORM_EPS)
            so = _sigmoid(cur_ref[:, 3 * W4 + h * HD:3 * W4 + (h + 1) * HD])
            ym_ref[:, sl] = (so * hhat * sv_ref[0:1, sl]).astype(BF16)
            kw = kh * f["u_col"]
            C[h] = f["dec"] * C_prev + _dot_tn(kw.astype(BF16), f["vb"])
            nm[h:h + 1, :] = f["dec"] * n_prev + _colsum(kw)
            nm[4 + h:5 + h, :] = jnp.broadcast_to(f["m_new"], (1, LANES))

    prev = lambda c: jnp.maximum(c - 1, 0)
    return pl.pallas_call(
        body, name="mlstm_fwd", grid=(nc,),
        in_specs=[pl.BlockSpec((L, M_W), lambda c: (c, 0)), pl.BlockSpec((L, 2 * W4), lambda c: (prev(c), 0)),
                  pl.BlockSpec((L, IF_W), lambda c: (c, 0)), _full(cw.shape), _full(sv.shape)],
        out_specs=[pl.BlockSpec((L, W4), lambda c: (c, 0)),
                   pl.BlockSpec((1, MLSTM_HEADS, HD, HD), lambda c: (c, 0, 0, 0)),
                   pl.BlockSpec((1, 8, LANES), lambda c: (c, 0, 0))],
        out_shape=[jax.ShapeDtypeStruct((S, W4), BF16), jax.ShapeDtypeStruct((nc, MLSTM_HEADS, HD, HD), F32),
                   jax.ShapeDtypeStruct((nc, 8, LANES), F32)],
        scratch_shapes=[pltpu.VMEM((MLSTM_HEADS, HD, HD), F32), pltpu.VMEM((8, LANES), F32)],
        compiler_params=_params(("arbitrary",)),
    )(pm, pm, pif, cw, sv)


def _mlstm_bwd(pm, pif, cw, sv, dym, cst, nst):
    S = pm.shape[0]
    L = MLSTM_CHUNK
    nc = S // L
    HD = MLSTM_HEAD_DIM
    W4 = MLSTM_HEADS * HD

    def body(cur_ref, prev_ref, pif_ref, cw_ref, sv_ref, dym_ref, cst_ref, nst_ref,
             dm_ref, dif_ref, dcw_ref, dsv_ref, dC, dn, dz_next, dqk):
        r = pl.program_id(0)
        c = nc - 1 - r

        @pl.when(r == 0)
        def _():
            dC[...] = jnp.zeros_like(dC)
            dn[...] = jnp.zeros_like(dn)
            dz_next[...] = jnp.zeros_like(dz_next)
            dcw_ref[...] = jnp.zeros_like(dcw_ref)
            dsv_ref[...] = jnp.zeros_like(dsv_ref)

        z, shifted = _conv_silu_fwd(cur_ref[:, 0:2 * W4], prev_ref[...], cw_ref, c == 0)
        sgz = _sigmoid(z)
        qk = z * sgz
        gt = pif_ref[...] + sv_ref[1:2, 0:LANES]
        r_i = lax.broadcasted_iota(jnp.int32, (L, L), 0)
        c_i = lax.broadcasted_iota(jnp.int32, (L, L), 1)
        tri = c_i <= r_i
        eye = c_i == r_i
        b_all = _tri_matmul(tri.astype(BF16), _log_sigmoid(gt))
        lane = lax.broadcasted_iota(jnp.int32, (L, LANES), 1)
        rowl = lax.broadcasted_iota(jnp.int32, (L, 1), 0)
        nmv = nst_ref[0]
        di_tile = jnp.zeros((L, LANES), F32)
        db_tile = jnp.zeros((L, LANES), F32)
        for h in range(MLSTM_HEADS):
            sl = slice(h * HD, (h + 1) * HD)
            qh = qk[:, sl]
            kh = qk[:, W4 + h * HD:W4 + (h + 1) * HD] * (HD ** -0.5)
            vh = cur_ref[:, 2 * W4 + h * HD:2 * W4 + (h + 1) * HD]
            n_prev = nmv[h:h + 1, :]
            m_prev = nmv[4 + h:5 + h, 0:1]
            C_prev = cst_ref[0, h]
            f = _mlstm_head_fwd(qh, kh, vh, gt[:, h:h + 1], b_all[:, 4 + h:5 + h], C_prev, n_prev, m_prev, tri, eye)
            hv, g, den, e_t, u_col, dec = f["hv"], f["g"], f["den"], f["e_t"], f["u_col"], f["dec"]
            qb, kb, vb, Sc, W = f["qb"], f["kb"], f["vb"], f["Sc"], f["W"]
            xc = hv - _rowmean(hv)
            rstd = lax.rsqrt(_rowmean(xc * xc) + NORM_EPS)
            hhat = xc * rstd
            wn = sv_ref[0:1, sl]
            so = _sigmoid(cur_ref[:, 3 * W4 + h * HD:3 * W4 + (h + 1) * HD])
            dy = dym_ref[:, sl]
            dm_ref[:, 3 * W4 + h * HD:3 * W4 + (h + 1) * HD] = (dy * hhat * wn * (so * (1.0 - so))).astype(BF16)
            dln = dy * so
            dsv_ref[0:1, sl] += _colsum(dln * hhat)
            dhhat = dln * wn
            dh = rstd * (dhhat - _rowmean(dhhat) - hhat * _rowmean(dhhat * hhat))
            dnum = dh / g
            active = jnp.abs(den) > f["floor"]
            dden = jnp.where(active, -jnp.sum(dh * hv, axis=1, keepdims=True) / g * jnp.where(den >= 0.0, 1.0, -1.0), 0.0)
            dnumb = dnum.astype(BF16)
            dSc = _dot_nt(dnumb, vb) + dden
            dA = (dSc * W).astype(BF16)
            G = dSc * Sc
            Gr = jnp.sum(G, axis=1, keepdims=True)
            Gc = jnp.sum(jnp.where(eye, _colsum(G), 0.0), axis=1, keepdims=True)
            dCn = dC[h]
            dCnb = dCn.astype(BF16)
            dn_new = dn[h:h + 1, :]
            kdC = _dot(kb, dCnb)
            vdC = _dot_nt(vb, dCnb)
            dv = _dot_tn(Sc.astype(BF16), dnumb) + u_col * kdC
            Cb = C_prev.astype(BF16)
            dq = _dot(dA, kb) + e_t * _dot_nt(dnumb, Cb) + (e_t * dden) * n_prev
            dk = _dot_tn(dA, qb) + u_col * (vdC + dn_new)
            de = jnp.sum(f["P1"] * dnum, axis=1, keepdims=True) + dden * f["qn"]
            E = de * e_t
            du = jnp.sum(kdC * vh, axis=1, keepdims=True) + jnp.sum(kh * dn_new, axis=1, keepdims=True)
            U = du * u_col
            ddec = jnp.sum(dCn * C_prev) + jnp.sum(dn_new * n_prev)
            dbl = ddec * dec + jnp.sum(U, axis=0, keepdims=True)
            db = Gr + E - Gc - U + jnp.where(rowl == L - 1, dbl, 0.0)
            di_tile = jnp.where(lane == h, Gc + U, di_tile)
            db_tile = jnp.where(lane == 4 + h, db, db_tile)
            dC[h] = dec * dCn + _dot_tn((qh * e_t).astype(BF16), dnumb)
            dn[h:h + 1, :] = dec * dn_new + _colsum((e_t * dden) * qh)
            dqk[:, sl] = dq
            dqk[:, W4 + h * HD:W4 + (h + 1) * HD] = dk * (HD ** -0.5)
            dm_ref[:, 2 * W4 + h * HD:2 * W4 + (h + 1) * HD] = dv.astype(BF16)
        dlf = _tri_matmul((r_i <= c_i).astype(BF16), db_tile)
        dif = jnp.where(lane < 4, di_tile, jnp.where(lane < 8, dlf * (1.0 - _sigmoid(gt)), 0.0))
        dif_ref[...] = dif.astype(BF16)
        dsv_ref[1:2, 0:LANES] += _colsum(dif)
        dz = dqk[...] * (sgz * (1.0 + z * (1.0 - sgz)))
        dcw_ref[4:5, :] += _colsum(dz)
        row = lax.broadcasted_iota(jnp.int32, dz.shape, 0)
        dzn = dz_next[...]
        du_in = dz * cw_ref[3:4, :]
        dcw_ref[3:4, :] += _colsum(dz * shifted[0])
        for k in range(1, 4):
            dcw_ref[3 - k:4 - k, :] += _colsum(dz * shifted[k])
            up = jnp.where(row >= L - k, pltpu.roll(dzn, L - k, 0), pltpu.roll(dz, L - k, 0))
            du_in = du_in + up * cw_ref[3 - k:4 - k, :]
        dz_next[...] = dz
        dm_ref[:, 0:2 * W4] = du_in.astype(BF16)

    cidx = lambda r: nc - 1 - r
    prev = lambda r: jnp.maximum(nc - 2 - r, 0)
    return pl.pallas_call(
        body, name="mlstm_bwd", grid=(nc,),
        in_specs=[pl.BlockSpec((L, M_W), lambda r: (cidx(r), 0)), pl.BlockSpec((L, 2 * W4), lambda r: (prev(r), 0)),
                  pl.BlockSpec((L, IF_W), lambda r: (cidx(r), 0)), _full(cw.shape), _full(sv.shape),
                  pl.BlockSpec((L, W4), lambda r: (cidx(r), 0)),
                  pl.BlockSpec((1, MLSTM_HEADS, HD, HD), lambda r: (cidx(r), 0, 0, 0)),
                  pl.BlockSpec((1, 8, LANES), lambda r: (cidx(r), 0, 0))],
        out_specs=[pl.BlockSpec((L, M_W), lambda r: (cidx(r), 0)), pl.BlockSpec((L, IF_W), lambda r: (cidx(r), 0)),
                   _full((8, 2 * W4)), _full((8, W4))],
        out_shape=[jax.ShapeDtypeStruct((S, M_W), BF16), jax.ShapeDtypeStruct((S, IF_W), BF16),
                   jax.ShapeDtypeStruct((8, 2 * W4), F32), jax.ShapeDtypeStruct((8, W4), F32)],
        scratch_shapes=[pltpu.VMEM((MLSTM_HEADS, HD, HD), F32), pltpu.VMEM((8, LANES), F32),
                        pltpu.VMEM((L, 2 * W4), F32), pltpu.VMEM((L, 2 * W4), F32)],
        compiler_params=_params(("arbitrary",)),
    )(pm, pm, pif, cw, sv, dym, cst, nst)


def _rope_tables(positions):
    half = HEAD_DIM // 2
    inv_freq = ROPE_THETA ** (-2.0 * jnp.arange(half, dtype=F32) / HEAD_DIM)
    ang = positions.astype(F32)[:, None] * inv_freq
    cos = jnp.tile(jnp.cos(ang), (1, LANES // half))
    sign = jnp.tile(jnp.concatenate([-jnp.ones((half,), F32), jnp.ones((half,), F32)]), LANES // HEAD_DIM)
    sin = jnp.tile(jnp.sin(ang), (1, LANES // half)) * sign
    return cos, sin


def _local_step(x, tgt, positions, mod, gains, w_cat, w_ba, w_bm, w_out, w_gate, w_up, w_down,
                conv_w, conv_b, b_if, sinks, norm_w):
    t = _tables(mod, gains, conv_w, conv_b, b_if, norm_w, positions)
    a = _mixer_fwd(x, t, sinks, w_cat)
    b = _ffn_part(x, tgt, t, a, w_ba, w_bm, w_out, w_gate, w_up, w_down)
    c = _mixer_bwd(b["dx1"], t, a, b, sinks, w_ba, w_bm, w_out)
    grad_x, acc_p = _pre_bwd(c["dproj"], x, b["dx1"], t["vecs"], w_cat)
    big = dict(w_cat=c["g_w_cat"], w_ba=c["g_w_ba"], w_bm=c["g_w_bm"], w_out=c["g_w_out"], w_gate=b["g_w_gate"],
               w_up=b["g_w_up"], w_down=b["g_w_down"])
    return b["loss"], grad_x, big, _small_grads(acc_p, b, c)


def _tables(mod, gains, conv_w, conv_b, b_if, norm_w, positions):
    cos, sin = _rope_tables(positions)
    return dict(
        vecs=jnp.concatenate([mod, gains, jnp.zeros((6, D_MODEL), F32)], axis=0),
        cw=jnp.concatenate([conv_w, conv_b.reshape(1, -1), jnp.zeros((3, 2 * 512), F32)], axis=0),
        sv=jnp.zeros((8, 512), F32).at[0].set(norm_w).at[1, 0:8].set(b_if), cos=cos, sin=sin)


def _mixer_fwd(x, t, sinks, w_cat):
    h, pa, pm, pif, pg = _pre_proj(x, t["vecs"], w_cat)
    ya, qr, kr, vb, lse = _attn_fwd(pa, t["cos"], t["sin"], sinks)
    ym, cst, nst = _mlstm_fwd(pm, pif, t["cw"], t["sv"])
    return dict(h=h, pm=pm, pif=pif, pg=pg, ya=ya, qr=qr, kr=kr, vb=vb, lse=lse, ym=ym, cst=cst, nst=nst)


def _ffn_part(x, tgt, t, a, w_ba, w_bm, w_out, w_gate, w_up, w_down):
    x1, merged, mix, pba, pbm = _mix_fwd(x, a["ya"], a["ym"], a["pg"], t["vecs"], w_ba, w_bm, w_out)
    dx1, h2, hid, da, du, dff, acc_f, loss = _ffn_fwd_bwd(x1, tgt, t["vecs"], w_gate, w_up, w_down)
    return dict(merged=merged, mix=mix, pba=pba, pbm=pbm, dx1=dx1, acc_f=acc_f, loss=loss[0, 0],
                g_w_gate=_matmul_tn(da, h2, 1024, "dw_ffn_gate"),
                g_w_up=_matmul_tn(du, h2, 1024, "dw_ffn_up"),
                g_w_down=_matmul_tn(hid, dff, 1024, "dw_ffn_down"))


def _mixer_bwd(dx1, t, a, b, sinks, w_ba, w_bm, w_out):
    dmix, dpa, dpb, dg, dya, dym, acc_m = _mix_bwd(dx1, b["mix"], b["pba"], b["pbm"], a["pg"], t["vecs"], w_ba, w_bm, w_out)
    g_w_out = _matmul_tn(b["merged"], dmix, 1024, "dw_out")
    g_w_ba = _matmul_tn(a["ya"], dpa, 1024, "dw_branch_attn")
    g_w_bm = _matmul_tn(a["ym"], dpb, 1024, "dw_branch_mlstm")
    dq, dkv, dsink = _attn_bwd(dya, a["qr"], a["kr"], a["vb"], a["lse"], t["cos"], t["sin"], sinks)
    dm, dif, dcw, dsv = _mlstm_bwd(a["pm"], a["pif"], t["cw"], t["sv"], dym, a["cst"], a["nst"])
    dproj = jnp.concatenate([dq, dkv, dm, dif, dg], axis=1)
    return dict(dproj=dproj, g_w_cat=_matmul_tn(a["h"], dproj, 1664, "dw_in"), g_w_out=g_w_out, g_w_ba=g_w_ba,
                g_w_bm=g_w_bm, acc_m=acc_m, dsink=dsink, dcw=dcw, dsv=dsv)


def _small_grads(acc_p, b, c):
    acc_f, acc_m = b["acc_f"], c["acc_m"]
    dmod = jnp.stack([acc_p[1], acc_p[0], acc_m[0], acc_f[3], acc_f[2], acc_f[0]])
    dgains = jnp.stack([acc_p[2], acc_m[1], acc_f[4], acc_f[1]])
    return dict(dmod=dmod, dgains=dgains, dconv_w=c["dcw"][0:4], dconv_b=c["dcw"][4], db_if=c["dsv"][1, 0:8],
                dsinks=c["dsink"][0, 0:8], dnorm_w=c["dsv"][0])


MESH_ID = pl.DeviceIdType.MESH


def _mesh_pos():
    return lax.axis_index("x"), lax.axis_index("y"), lax.axis_index("c")


def _flip(v, bit):
    return 1 - v if bit else v


def _relations():
    return [((r >> 2) & 1, (r >> 1) & 1, r & 1) for r in range(1, N_DEV)]


def _small_exchange(p, gather, name):
    V = p.shape[-1]

    def body(p_ref, out_ref, send_sems, recv_sems):
        x, y, c = _mesh_pos()
        me = 4 * x + 2 * y + c
        out_ref[me] = p_ref[...] if gather else p_ref[me]
        peers = []
        for dx, dy, dc in _relations():
            px, py, pc = _flip(x, dx), _flip(y, dy), _flip(c, dc)
            peers.append(((px, py, pc), 4 * px + 2 * py + pc))

        def copy(k, landing):
            peer, pid = peers[k]
            return pltpu.make_async_remote_copy(
                src_ref=p_ref if gather else p_ref.at[pid], dst_ref=out_ref.at[landing],
                send_sem=send_sems.at[k], recv_sem=recv_sems.at[k], device_id=peer, device_id_type=MESH_ID)

        sends = [copy(k, me) for k in range(N_DEV - 1)]
        for cp in sends:
            cp.start()
        for k in range(N_DEV - 1):
            copy(k, peers[k][1]).wait_recv()
        for cp in sends:
            cp.wait_send()

    vm = pl.BlockSpec(memory_space=pltpu.VMEM)
    return pl.pallas_call(
        body, name=name, in_specs=[vm], out_specs=vm,
        out_shape=jax.ShapeDtypeStruct((N_DEV, 8, V), F32),
        scratch_shapes=[pltpu.SemaphoreType.DMA((N_DEV - 1,)), pltpu.SemaphoreType.DMA((N_DEV - 1,))],
        compiler_params=pltpu.CompilerParams(vmem_limit_bytes=VMEM_LIMIT),
    )(p)


HBM_SPEC = pl.BlockSpec(memory_space=pltpu.HBM)
SEM_SPEC = pl.BlockSpec(memory_space=pltpu.SEMAPHORE)


def _peers(x, y, c):
    out = []
    for dx, dy, dc in _relations():
        px, py, pc = _flip(x, dx), _flip(y, dy), _flip(c, dc)
        out.append(((px, py, pc), 4 * px + 2 * py + pc))
    return out


def _exchange_start(arrs, gather, after, name):
    n = len(arrs)
    me_out = 4 * lax.axis_index("x") + 2 * lax.axis_index("y") + lax.axis_index("c")
    lands = []
    for a in arrs:
        own = a[None] if gather else lax.dynamic_index_in_dim(a, me_out, 0, keepdims=True)
        empty = lax.empty(((N_DEV,) + a.shape) if gather else a.shape, a.dtype)
        lands.append(lax.dynamic_update_index_in_dim(empty, own, me_out, 0))

    def body(*refs):
        a_refs, l_refs = refs[:n], refs[n:2 * n]
        send_sems, recv_sems = refs[2 * n + 1], refs[2 * n + 2]
        token = refs[4 * n + 3]
        x, y, c = _mesh_pos()
        me = 4 * x + 2 * y + c
        for a in range(n):
            for k, (peer, pid) in enumerate(_peers(x, y, c)):
                pltpu.make_async_remote_copy(
                    src_ref=a_refs[a] if gather else a_refs[a].at[pid], dst_ref=l_refs[a].at[me],
                    send_sem=send_sems.at[a * (N_DEV - 1) + k], recv_sem=recv_sems.at[a * (N_DEV - 1) + k],
                    device_id=peer, device_id_type=MESH_ID).start()
        token[...] = jnp.zeros_like(token)

    sem = pltpu.SemaphoreType.DMA((n * (N_DEV - 1),))
    hbm = lambda a: pltpu.with_memory_space_constraint(a, pltpu.HBM)
    res = pl.pallas_call(
        body, name=name,
        out_shape=(sem, sem, *[pltpu.HBM(a.shape, a.dtype) for a in arrs], *[pltpu.HBM(l.shape, l.dtype) for l in lands],
                   jax.ShapeDtypeStruct((8, LANES), F32)),
        in_specs=[HBM_SPEC] * (2 * n) + [pl.BlockSpec(memory_space=pl.ANY)],
        out_specs=(SEM_SPEC, SEM_SPEC, *[HBM_SPEC] * (2 * n), pl.BlockSpec(memory_space=pltpu.VMEM)),
        input_output_aliases={i: 2 + i for i in range(2 * n)},
        compiler_params=pltpu.CompilerParams(has_side_effects=pltpu.SideEffectType.DATAFLOW_SIDE_EFFECTING),
    )(*[hbm(a) for a in arrs], *[hbm(l) for l in lands], after)
    return dict(sems=res[0:2], arrs=res[2:2 + n], lands=res[2 + n:2 + 2 * n], token=res[2 + 2 * n], gather=gather)


def _exchange_wait(st, after, name):
    n = len(st["arrs"])
    gather = st["gather"]

    def body(*refs):
        a_refs, l_refs = refs[:n], refs[n:2 * n]
        send_sems, recv_sems = refs[2 * n], refs[2 * n + 1]
        x, y, c = _mesh_pos()
        for a in range(n):
            for k, (peer, pid) in enumerate(_peers(x, y, c)):
                cp = pltpu.make_async_remote_copy(
                    src_ref=a_refs[a] if gather else a_refs[a].at[pid], dst_ref=l_refs[a].at[pid],
                    send_sem=send_sems.at[a * (N_DEV - 1) + k], recv_sem=recv_sems.at[a * (N_DEV - 1) + k],
                    device_id=peer, device_id_type=MESH_ID)
                cp.wait_send()
                cp.wait_recv()

    both = list(st["arrs"]) + list(st["lands"])
    res = pl.pallas_call(
        body, name=name, out_shape=[pltpu.HBM(a.shape, a.dtype) for a in both],
        in_specs=[HBM_SPEC] * (2 * n) + [SEM_SPEC, SEM_SPEC, pl.BlockSpec(memory_space=pl.ANY)],
        out_specs=[HBM_SPEC] * (2 * n), input_output_aliases={i: i for i in range(2 * n)},
        compiler_params=pltpu.CompilerParams(has_side_effects=pltpu.SideEffectType.DATAFLOW_SIDE_EFFECTING),
    )(*both, *st["sems"], after)
    return res[n:2 * n]


def _after(x, token):
    return lax.optimization_barrier((x, token))[0]


def _all_gather_hbm(shards):
    n = len(shards)

    def body(*refs):
        p_refs, out_refs = refs[:n], refs[n:2 * n]
        send_sems, recv_sems, local_sems = refs[2 * n:]
        x, y, c = _mesh_pos()
        me, sibling = (x, y, c), (x, y, 1 - c)
        chips = [(1 - x, y), (x, 1 - y), (1 - x, 1 - y)]

        def copy(a, k, block, to, own=False):
            slot = out_refs[a].at[4 * block[0] + 2 * block[1] + block[2]]
            return pltpu.make_async_remote_copy(
                src_ref=p_refs[a] if own else slot, dst_ref=slot,
                send_sem=send_sems.at[a, k], recv_sem=recv_sems.at[a, k], device_id=to, device_id_type=MESH_ID)

        mine = [pltpu.make_async_copy(p_refs[a], out_refs[a].at[4 * x + 2 * y + c], local_sems.at[a]) for a in range(n)]
        for cp in mine:
            cp.start()
        first = []
        for a in range(n):
            first.append(copy(a, 0, me, sibling, own=True))
            first += [copy(a, 1 + j, me, (*chip, c), own=True) for j, chip in enumerate(chips)]
        for cp in first:
            cp.start()
        passed = []
        for j, chip in enumerate(chips):
            for a in range(n):
                copy(a, 1 + j, (*chip, c), me).wait_recv()
                passed.append(copy(a, 4 + j, (*chip, c), sibling))
                passed[-1].start()
        for a in range(n):
            copy(a, 0, sibling, me).wait_recv()
            for j, chip in enumerate(chips):
                copy(a, 4 + j, (*chip, 1 - c), me).wait_recv()
        for cp in first + passed:
            cp.wait_send()
        for cp in mine:
            cp.wait()

    hbm = pl.BlockSpec(memory_space=pl.ANY)
    return pl.pallas_call(
        body, name="gather_weights", in_specs=[hbm] * n, out_specs=[hbm] * n,
        out_shape=[jax.ShapeDtypeStruct((N_DEV,) + s.shape, s.dtype) for s in shards],
        scratch_shapes=[pltpu.SemaphoreType.DMA((n, N_DEV - 1)), pltpu.SemaphoreType.DMA((n, N_DEV - 1)),
                        pltpu.SemaphoreType.DMA((n,))],
    )(*shards)


def _adamw(w, g, m, v):
    m2 = ADAM_B1 * m + (1.0 - ADAM_B1) * g
    v2 = ADAM_B2 * v + (1.0 - ADAM_B2) * (g * g)
    m_hat = m2 / (1.0 - ADAM_B1 ** ADAM_STEP)
    v_hat = v2 / (1.0 - ADAM_B2 ** ADAM_STEP)
    delta = -ADAM_LR * (m_hat / (jnp.sqrt(v_hat) + ADAM_EPS) + ADAM_WD * w)
    return delta, m2, v2


def _mod_partial(cmat, w_shard, b_shard):
    def body(c_ref, w_ref, b_ref, o_ref):
        o_ref[...] = _dot(c_ref[...].astype(BF16), w_ref[...].astype(BF16)) + b_ref[...]

    return pl.pallas_call(
        body, name="mod_partial", out_shape=jax.ShapeDtypeStruct((N_DEV, w_shard.shape[1]), F32),
        compiler_params=_params(),
    )(cmat, w_shard, b_shard)


def _adamw_w_ada(cmat, dmod_cols, w, m, v):
    def body(c_ref, d_ref, w_ref, m_ref, v_ref, g_ref, dl_ref, m2_ref, v2_ref):
        g = _dot_tn(c_ref[...].astype(BF16), d_ref[...].astype(BF16))
        g_ref[...] = g
        dl_ref[...], m2_ref[...], v2_ref[...] = _adamw(w_ref[...], g, m_ref[...], v_ref[...])

    return pl.pallas_call(
        body, name="adamw_w_ada", out_shape=[jax.ShapeDtypeStruct(w.shape, F32)] * 4,
        compiler_params=_params(),
    )(cmat, dmod_cols, w, m, v)


def _adamw_small(gathered, w, m, v):
    def body(g_ref, w_ref, m_ref, v_ref, go_ref, dl_ref, m2_ref, v2_ref):
        g = g_ref[0]
        for k in range(1, N_DEV):
            g = g + g_ref[k]
        go_ref[...] = g
        dl_ref[...], m2_ref[...], v2_ref[...] = _adamw(w_ref[...], g, m_ref[...], v_ref[...])

    return pl.pallas_call(
        body, name="adamw_small", out_shape=[jax.ShapeDtypeStruct(w.shape, F32)] * 4,
        compiler_params=_params(),
    )(gathered, w, m, v)


def _row_tile(rows):
    return rows // 4 if rows >= 512 else rows


def _sum_partials(r_ref):
    g = r_ref[0].astype(F32)
    for k in range(1, N_DEV):
        g = g + r_ref[k].astype(F32)
    return g


def _adamw_sum(recv, w, m, v, name):
    r, cdim = w.shape
    tr = _row_tile(r)

    def body(r_ref, w_ref, m_ref, v_ref, g_ref, dl_ref, m2_ref, v2_ref):
        g = _sum_partials(r_ref)
        g_ref[...] = g
        dl_ref[...], m2_ref[...], v2_ref[...] = _adamw(w_ref[...], g, m_ref[...], v_ref[...])

    row = pl.BlockSpec((tr, cdim), lambda i: (i, 0))
    return pl.pallas_call(
        body, name=name, grid=(r // tr,),
        in_specs=[pl.BlockSpec((N_DEV, tr, cdim), lambda i: (0, i, 0)), row, row, row],
        out_specs=[row] * 4, out_shape=[jax.ShapeDtypeStruct((r, cdim), F32)] * 4,
        compiler_params=_params(("parallel",)),
    )(recv, w, m, v)


def _sum8(recv, name):
    _, r, cdim = recv.shape
    tr = _row_tile(r)

    def body(r_ref, g_ref):
        g_ref[...] = _sum_partials(r_ref)

    return pl.pallas_call(
        body, name=name, grid=(r // tr,),
        in_specs=[pl.BlockSpec((N_DEV, tr, cdim), lambda i: (0, i, 0))],
        out_specs=pl.BlockSpec((tr, cdim), lambda i: (i, 0)), out_shape=jax.ShapeDtypeStruct((r, cdim), F32),
        compiler_params=_params(("parallel",)),
    )(recv)


def _adamw_plain(g, w, m, v, name):
    r, cdim = w.shape
    tr = _row_tile(r)

    def body(g_ref, w_ref, m_ref, v_ref, dl_ref, m2_ref, v2_ref):
        dl_ref[...], m2_ref[...], v2_ref[...] = _adamw(w_ref[...], g_ref[...], m_ref[...], v_ref[...])

    row = pl.BlockSpec((tr, cdim), lambda i: (i, 0))
    return pl.pallas_call(
        body, name=name, grid=(r // tr,), in_specs=[row] * 4, out_specs=[row] * 3,
        out_shape=[jax.ShapeDtypeStruct((r, cdim), F32)] * 3,
        compiler_params=_params(("parallel",)),
    )(g, w, m, v)


IN_SHARD = 609
IN_SHARD_PAD = 640
IF_AT = A_W + M_W


def _regrouped(u):
    return u if u < IF_AT + 8 else u + (IF_W - 8)


def _selection(k, rows, row0, transpose):
    shape = (rows, IN_SHARD_PAD) if transpose else (IN_SHARD_PAD, rows)
    l = lax.broadcasted_iota(jnp.int32, shape, 1 if transpose else 0)
    r = lax.broadcasted_iota(jnp.int32, shape, 0 if transpose else 1) + row0
    u = l + IN_SHARD * k
    ru = u + jnp.where(u >= IF_AT + 8, IF_W - 8, 0)
    return ((ru == r) & (l < IN_SHARD)).astype(BF16)


def _regroup_w_in(g):
    def body(g_ref, o_ref):
        for cb in range(CAT_W // LANES):
            r0 = cb * LANES
            acc = jnp.zeros((D_MODEL, LANES), F32)
            for k in range(N_DEV):
                lo, hi = _regrouped(IN_SHARD * k), _regrouped(IN_SHARD * k + IN_SHARD - 1)
                if hi >= r0 and lo < r0 + LANES:
                    acc = acc + _dot(g_ref[k], _selection(k, LANES, r0, False))
            o_ref[:, r0:r0 + LANES] = acc.astype(BF16)

    return pl.pallas_call(
        body, name="regroup_w_in", out_shape=jax.ShapeDtypeStruct((D_MODEL, CAT_W), BF16),
        compiler_params=_params(),
    )(g)


def _ungroup_w_in(g_cat):
    def body(g_ref, o_ref):
        for k in range(N_DEV):
            lo, hi = _regrouped(IN_SHARD * k), _regrouped(IN_SHARD * k + IN_SHARD - 1)
            w0, w1 = lo // LANES * LANES, (hi // LANES + 1) * LANES
            o_ref[k] = _dot(g_ref[:, w0:w1], _selection(k, w1 - w0, w0, True)).astype(BF16)

    return pl.pallas_call(
        body, name="ungroup_w_in", out_shape=jax.ShapeDtypeStruct((N_DEV, D_MODEL, IN_SHARD_PAD), BF16),
        compiler_params=_params(),
    )(g_cat)


SMALL = (("b_ada", 6144), ("g_pre_mix", 1024), ("g_post_mix", 1024), ("g_pre_ffn", 1024), ("g_post_ffn", 1024),
         ("conv_b", 1024), ("mlstm_norm_w", 512), ("b_if", 8), ("attn_sinks", 8))
SMALL_W = 12032
SMALL_PAD = 12288


def _pack_small(vals):
    parts = []
    for n, width in SMALL:
        a = vals[n].reshape(-1)
        parts.append(jnp.pad(a, (0, LANES - width)) if width < LANES else a)
    return jnp.concatenate(parts)


def _unpack_small(vec):
    out, off = {}, 0
    for n, width in SMALL:
        out[n] = vec[off:off + width].reshape(1, width)
        off += max(width, LANES)
    return out


WEIGHT_NAMES = ("w_ada", "b_ada", "g_pre_mix", "g_post_mix", "w_in", "b_if", "conv_w", "conv_b", "attn_sinks",
                "mlstm_norm_w", "w_branch_attn", "w_branch_mlstm", "w_out", "g_pre_ffn", "g_post_ffn",
                "w_ffn_gate", "w_ffn_up", "w_ffn_down")


def kernel(x, c, positions, w_ada, b_ada, g_pre_mix, g_post_mix, w_in, b_if, conv_w, conv_b, attn_sinks, mlstm_norm_w, w_branch_attn, w_branch_mlstm, w_out, g_pre_ffn, g_post_ffn, w_ffn_gate, w_ffn_up, w_ffn_down, loss_target, m_w_ada, m_b_ada, m_g_pre_mix, m_g_post_mix, m_w_in, m_b_if, m_conv_w, m_conv_b, m_attn_sinks, m_mlstm_norm_w, m_w_branch_attn, m_w_branch_mlstm, m_w_out, m_g_pre_ffn, m_g_post_ffn, m_w_ffn_gate, m_w_ffn_up, m_w_ffn_down, v_w_ada, v_b_ada, v_g_pre_mix, v_g_post_mix, v_w_in, v_b_if, v_conv_w, v_conv_b, v_attn_sinks, v_mlstm_norm_w, v_w_branch_attn, v_w_branch_mlstm, v_w_out, v_g_pre_ffn, v_g_post_ffn, v_w_ffn_gate, v_w_ffn_up, v_w_ffn_down):
    given = dict(locals())
    W = {n: given[n][0] for n in WEIGHT_NAMES}
    M = {n: given["m_" + n][0] for n in WEIGHT_NAMES}
    V = {n: given["v_" + n][0] for n in WEIGHT_NAMES}
    me = 4 * lax.axis_index("x") + 2 * lax.axis_index("y") + lax.axis_index("c")

    ff_sh = D_FF // N_DEV
    g_in, g_conv, cg = _all_gather_hbm([jnp.pad(W["w_in"], ((0, 0), (0, IN_SHARD_PAD - IN_SHARD))).astype(BF16),
                                        jnp.pad(W["conv_w"], ((0, 4), (0, 0))), c.reshape(8, D_MODEL // 8)])

    cmat = cg.reshape(N_DEV, D_MODEL)
    ada_w = D_MODEL * 6 // N_DEV
    b_cols = lax.dynamic_slice(W["b_ada"], (me * ada_w,), (ada_w,)).reshape(1, ada_w)
    mod_part = _mod_partial(cmat, W["w_ada"], b_cols)
    mod_recv = _small_exchange(jnp.broadcast_to(mod_part[:, None, :], (N_DEV, 8, ada_w)), False, "scatter_mod")
    mod = mod_recv[:, 0, :].reshape(6, D_MODEL)

    st_b = _exchange_start([W["w_branch_attn"].astype(BF16), W["w_branch_mlstm"].astype(BF16), W["w_out"].astype(BF16),
                            W["w_ffn_gate"].T.astype(BF16), W["w_ffn_up"].T.astype(BF16), W["w_ffn_down"].astype(BF16)],
                           True, mod_recv, "gather_rest_start")
    g_in = _after(g_in, st_b["token"])
    cols = lambda g: g.transpose(1, 0, 2).reshape(g.shape[1], N_DEV * g.shape[2])
    gains = jnp.stack([W["g_pre_mix"], W["g_post_mix"], W["g_pre_ffn"], W["g_post_ffn"]])
    xs, tgt = x[0], loss_target[0]
    t = _tables(mod, gains, cols(g_conv)[0:4], W["conv_b"], W["b_if"], W["mlstm_norm_w"], positions[0])
    w_cat = _regroup_w_in(g_in)
    a = _mixer_fwd(xs, t, W["attn_sinks"], w_cat)
    g_ba, g_bm, g_out, g_gate, g_up, g_down = _exchange_wait(st_b, a["ym"], "gather_rest_wait")
    w_ba, w_bm, w_out = cols(g_ba), cols(g_bm), g_out.reshape(D_MODEL, D_MODEL)
    b = _ffn_part(xs, tgt, t, a, w_ba, w_bm, w_out, g_gate.reshape(D_FF, D_MODEL), g_up.reshape(D_FF, D_MODEL),
                  g_down.reshape(D_FF, D_MODEL))

    st_f = _exchange_start([b["g_w_gate"].reshape(N_DEV, ff_sh, D_MODEL), b["g_w_up"].reshape(N_DEV, ff_sh, D_MODEL),
                            b["g_w_down"].reshape(N_DEV, ff_sh, D_MODEL)], False, b["dx1"], "scatter_ffn_start")
    cm = _mixer_bwd(_after(b["dx1"], st_f["token"]), t, a, b, W["attn_sinks"], w_ba, w_bm, w_out)
    pieces = lambda g, n: g.reshape(g.shape[0], N_DEV, n).transpose(1, 0, 2)
    st_m = _exchange_start([_ungroup_w_in(cm["g_w_cat"]), pieces(cm["g_w_ba"], 128), pieces(cm["g_w_bm"], 128),
                            cm["g_w_out"].reshape(N_DEV, D_MODEL // N_DEV, D_MODEL),
                            jnp.pad(pieces(cm["dcw"][0:4], 128), ((0, 0), (0, 4), (0, 0)))], False, cm["dproj"],
                           "scatter_mixer_start")
    r_gate, r_up, r_down = _exchange_wait(st_f, st_m["token"], "scatter_ffn_wait")
    grad_x, acc_p = _pre_bwd(_after(cm["dproj"], st_m["token"]), xs, b["dx1"], t["vecs"], w_cat)
    small = _small_grads(acc_p, b, cm)
    loss = b["loss"]

    big_out = [{} for _ in range(4)]

    def put(n, res):
        for k in range(4):
            big_out[k][n] = res[k][None]

    put("w_ffn_down", _adamw_sum(r_down, W["w_ffn_down"], M["w_ffn_down"], V["w_ffn_down"], "adamw_w_ffn_down"))
    for n, g in (("w_ffn_gate", _sum8(r_gate, "sum_w_ffn_gate").T), ("w_ffn_up", _sum8(r_up, "sum_w_ffn_up").T)):
        put(n, [g] + list(_adamw_plain(g, W[n], M[n], V[n], "adamw_" + n)))

    part = {"b_ada": small["dmod"], "g_pre_mix": small["dgains"][0], "g_post_mix": small["dgains"][1],
            "g_pre_ffn": small["dgains"][2], "g_post_ffn": small["dgains"][3], "conv_b": small["dconv_b"],
            "mlstm_norm_w": small["dnorm_w"], "b_if": small["db_if"], "attn_sinks": small["dsinks"]}
    rows8 = lambda d: jnp.pad(_pack_small(d), (0, SMALL_PAD - SMALL_W)).reshape(8, SMALL_PAD // 8)
    sg = _small_exchange(rows8(part), True, "gather_small")
    small_out = [_unpack_small(o.reshape(-1)) for o in _adamw_small(sg, rows8(W), rows8(M), rows8(V))]
    dmod_cols = lax.dynamic_slice(sg.reshape(N_DEV, SMALL_PAD), (0, me * ada_w), (N_DEV, ada_w))
    ada_out = _adamw_w_ada(cmat, dmod_cols, W["w_ada"], M["w_ada"], V["w_ada"])

    r_in, r_ba, r_bm, r_out, r_conv = _exchange_wait(st_m, grad_x, "scatter_mixer_wait")
    for n, r in (("w_branch_attn", r_ba), ("w_branch_mlstm", r_bm), ("w_out", r_out)):
        put(n, _adamw_sum(r, W[n], M[n], V[n], "adamw_" + n))
    pad4 = lambda v: jnp.pad(v, ((0, 4), (0, 0)))
    put("conv_w", [o[0:4] for o in _adamw_sum(r_conv, pad4(W["conv_w"]), pad4(M["conv_w"]), pad4(V["conv_w"]),
                                                 "adamw_conv_w")])
    g = _sum8(r_in, "sum_w_in")[:, 0:IN_SHARD]
    put("w_in", [g] + list(_adamw_plain(g, W["w_in"], M["w_in"], V["w_in"], "adamw_w_in")))

    total = lax.psum(loss, ("x", "y", "c"))
    outs = [total, grad_x[None]]
    for k in range(4):
        for n in WEIGHT_NAMES:
            if n == "w_ada":
                outs.append(ada_out[k][None])
            elif n in big_out[k]:
                outs.append(big_out[k][n])
            else:
                outs.append(small_out[k][n])
    return tuple(outs)
```

```python
import functools

import jax
import jax.numpy as jnp
import numpy as np
from jax import lax
from jax.experimental import pallas as pl
from jax.experimental.pallas import tpu as pltpu

F32 = jnp.float32
BF16 = jnp.bfloat16

N_DEV = 8
D_MODEL = 1024
D_FF = 2816
N_Q_HEADS = 8
HEAD_DIM = 64
ATTN_BLOCK = 128
ROPE_THETA = 10000.0
MLSTM_HEADS = 4
MLSTM_HEAD_DIM = 128
MLSTM_CHUNK = 128
NORM_EPS = 1e-6
ADAM_LR = 0.001
ADAM_B1 = 0.9
ADAM_B2 = 0.999
ADAM_EPS = 1e-08
ADAM_WD = 0.01
ADAM_STEP = 10

ROW_TILE = 256
WIDE_TILE = 512
LANES = 128
NEG = -1e30
VMEM_LIMIT = 56 * 1024 * 1024

A_W = 768
M_W = 2048
IF_W = 128
G_W = 2048
CAT_W = A_W + M_W + IF_W + G_W

R_SHIFT_M, R_SCALE_M, R_GATE_M, R_SHIFT_F, R_SCALE_F, R_GATE_F = 0, 1, 2, 3, 4, 5
R_G_PRE_MIX, R_G_POST_MIX, R_G_PRE_FFN, R_G_POST_FFN = 6, 7, 8, 9


def _dot(a, b):
    return jnp.dot(a, b, preferred_element_type=F32)


def _dot_nt(a, b):
    return lax.dot_general(a, b, (((1,), (1,)), ((), ())), preferred_element_type=F32)


def _dot_tn(a, b):
    return lax.dot_general(a, b, (((0,), (0,)), ((), ())), preferred_element_type=F32)


def _sigmoid(x):
    return 1.0 / (1.0 + jnp.exp(-x))


def _colsum(x):
    return jnp.sum(x, axis=0, keepdims=True)


def _rowmean(x):
    return jnp.mean(x, axis=-1, keepdims=True)


def _params(sem=None, vmem=VMEM_LIMIT):
    kw = dict(vmem_limit_bytes=vmem)
    if sem is not None:
        kw["dimension_semantics"] = sem
    return pltpu.CompilerParams(**kw)


def _full(shape):
    nd = len(shape)
    return pl.BlockSpec(shape, lambda *_: (0,) * nd)


def _pre_proj(x, vecs, w_cat):
    S = x.shape[0]
    tm = WIDE_TILE

    def body(x_ref, v_ref, w_ref, h_ref, pa_ref, pm_ref, pif_ref, pg_ref):
        xv = x_ref[...]
        r = lax.rsqrt(_rowmean(xv * xv) + NORM_EPS)
        h = (xv * r * v_ref[R_G_PRE_MIX:R_G_PRE_MIX + 1, :]) * (1.0 + v_ref[R_SCALE_M:R_SCALE_M + 1, :]) \
            + v_ref[R_SHIFT_M:R_SHIFT_M + 1, :]
        hb = h.astype(BF16)
        h_ref[...] = hb
        pa_ref[...] = _dot(hb, w_ref[:, 0:A_W])
        pm_ref[...] = _dot(hb, w_ref[:, A_W:A_W + M_W])
        pif_ref[...] = _dot(hb, w_ref[:, A_W + M_W:A_W + M_W + IF_W])
        pg_ref[...] = _dot(hb, w_ref[:, A_W + M_W + IF_W:CAT_W]).astype(BF16)

    row = lambda w: pl.BlockSpec((tm, w), lambda i: (i, 0))
    return pl.pallas_call(
        body, name="pre_proj", grid=(S // tm,),
        in_specs=[row(D_MODEL), _full(vecs.shape), _full(w_cat.shape)],
        out_specs=[row(D_MODEL), row(A_W), row(M_W), row(IF_W), row(G_W)],
        out_shape=[jax.ShapeDtypeStruct((S, D_MODEL), BF16), jax.ShapeDtypeStruct((S, A_W), F32),
                   jax.ShapeDtypeStruct((S, M_W), F32), jax.ShapeDtypeStruct((S, IF_W), F32),
                   jax.ShapeDtypeStruct((S, G_W), BF16)],
        compiler_params=_params(("parallel",)),
    )(x, vecs, w_cat)


def _mix_fwd(x, ya, ym, pg, vecs, w_ba, w_bm, w_out):
    S = x.shape[0]
    tm = WIDE_TILE

    def body(x_ref, ya_ref, ym_ref, pg_ref, v_ref, wba_ref, wbm_ref, wout_ref,
             x1_ref, merged_ref, mix_ref, pa_ref, pb_ref):
        pa = _dot(ya_ref[...], wba_ref[...])
        pb = _dot(ym_ref[...], wbm_ref[...])
        merged = _sigmoid(pg_ref[:, 0:D_MODEL].astype(F32)) * pa + _sigmoid(pg_ref[:, D_MODEL:G_W].astype(F32)) * pb
        mb = merged.astype(BF16)
        mix = _dot(mb, wout_ref[...])
        r = lax.rsqrt(_rowmean(mix * mix) + NORM_EPS)
        x1_ref[...] = x_ref[...] + v_ref[R_GATE_M:R_GATE_M + 1, :] * (mix * r * v_ref[R_G_POST_MIX:R_G_POST_MIX + 1, :])
        merged_ref[...] = mb
        mix_ref[...] = mix
        pa_ref[...] = pa.astype(BF16)
        pb_ref[...] = pb.astype(BF16)

    row = lambda w: pl.BlockSpec((tm, w), lambda i: (i, 0))
    sd = lambda w, dt: jax.ShapeDtypeStruct((S, w), dt)
    return pl.pallas_call(
        body, name="mix_fwd", grid=(S // tm,),
        in_specs=[row(D_MODEL), row(512), row(512), row(G_W), _full(vecs.shape), _full(w_ba.shape),
                  _full(w_bm.shape), _full(w_out.shape)],
        out_specs=[row(D_MODEL)] * 5,
        out_shape=[sd(D_MODEL, F32), sd(D_MODEL, BF16), sd(D_MODEL, F32), sd(D_MODEL, BF16), sd(D_MODEL, BF16)],
        compiler_params=_params(("parallel",)),
    )(x, ya, ym, pg, vecs, w_ba, w_bm, w_out)


def _ffn_fwd_bwd(x1, tgt, vecs, w_gate, w_up, w_down):
    S = x1.shape[0]
    tm = ROW_TILE

    def body(x1_ref, tgt_ref, v_ref, wg_hbm, wu_hbm, wd_hbm,
             dx1_ref, h2_ref, hid_ref, da_ref, du_ref, dff_ref, acc_ref, loss_ref,
             wg, wu, wd, sem):
        i = pl.program_id(0)

        @pl.when(i == 0)
        def _():
            cps = [pltpu.make_async_copy(wg_hbm, wg, sem.at[0]), pltpu.make_async_copy(wu_hbm, wu, sem.at[1]),
                   pltpu.make_async_copy(wd_hbm, wd, sem.at[2])]
            for cp in cps:
                cp.start()
            for cp in cps:
                cp.wait()
            acc_ref[...] = jnp.zeros_like(acc_ref)
            loss_ref[...] = jnp.zeros_like(loss_ref)

        vrow = lambda r: v_ref[r:r + 1, :]
        x1v = x1_ref[...]
        r3 = lax.rsqrt(_rowmean(x1v * x1v) + NORM_EPS)
        x1hat = x1v * r3
        xn3 = x1hat * vrow(R_G_PRE_FFN)
        h2b = (xn3 * (1.0 + vrow(R_SCALE_F)) + vrow(R_SHIFT_F)).astype(BF16)
        h2_ref[...] = h2b
        a = _dot_nt(h2b, wg[...])
        u = _dot_nt(h2b, wu[...])
        sg = _sigmoid(a)
        sil = a * sg
        hidb = (sil * u).astype(BF16)
        hid_ref[...] = hidb
        ff = _dot(hidb, wd[...])
        r4 = lax.rsqrt(_rowmean(ff * ff) + NORM_EPS)
        ffhat = ff * r4
        n4 = ffhat * vrow(R_G_POST_FFN)
        err = x1v + vrow(R_GATE_F) * n4 - tgt_ref[...]
        loss_ref[...] += jnp.sum(err * err) * (0.5 / D_MODEL)
        dy = err * (1.0 / D_MODEL)
        acc_ref[0:1, :] += _colsum(dy * n4)
        dn4 = dy * vrow(R_GATE_F)
        acc_ref[1:2, :] += _colsum(dn4 * ffhat)
        dffhat = dn4 * vrow(R_G_POST_FFN)
        dffb = (r4 * (dffhat - ffhat * _rowmean(dffhat * ffhat))).astype(BF16)
        dff_ref[...] = dffb
        dhid = _dot_nt(dffb, wd[...])
        dub = (dhid * sil).astype(BF16)
        dab = (dhid * u * (sg * (1.0 + a * (1.0 - sg)))).astype(BF16)
        da_ref[...] = dab
        du_ref[...] = dub
        dh2 = _dot(dab, wg[...]) + _dot(dub, wu[...])
        acc_ref[2:3, :] += _colsum(dh2 * xn3)
        acc_ref[3:4, :] += _colsum(dh2)
        dxn3 = dh2 * (1.0 + vrow(R_SCALE_F))
        acc_ref[4:5, :] += _colsum(dxn3 * x1hat)
        dx1hat = dxn3 * vrow(R_G_PRE_FFN)
        dx1_ref[...] = dy + r3 * (dx1hat - x1hat * _rowmean(dx1hat * x1hat))

    row = lambda w: pl.BlockSpec((tm, w), lambda i: (i, 0))
    sd = lambda w, dt: jax.ShapeDtypeStruct((S, w), dt)
    anyspec = pl.BlockSpec(memory_space=pl.ANY)
    return pl.pallas_call(
        body, name="ffn_fwd_bwd", grid=(S // tm,),
        in_specs=[row(D_MODEL), row(D_MODEL), _full(vecs.shape), anyspec, anyspec, anyspec],
        out_specs=[row(D_MODEL), row(D_MODEL), row(D_FF), row(D_FF), row(D_FF), row(D_MODEL),
                   _full((8, D_MODEL)), _full((8, LANES))],
        out_shape=[sd(D_MODEL, F32), sd(D_MODEL, BF16), sd(D_FF, BF16), sd(D_FF, BF16), sd(D_FF, BF16),
                   sd(D_MODEL, BF16), jax.ShapeDtypeStruct((8, D_MODEL), F32), jax.ShapeDtypeStruct((8, LANES), F32)],
        scratch_shapes=[pltpu.VMEM(w_gate.shape, BF16), pltpu.VMEM(w_up.shape, BF16), pltpu.VMEM(w_down.shape, BF16),
                        pltpu.SemaphoreType.DMA((3,))],
        compiler_params=_params(("arbitrary",)),
    )(x1, tgt, vecs, w_gate, w_up, w_down)


def _mix_bwd(dx1, mix, pa, pb, pg, vecs, w_ba, w_bm, w_out):
    S = dx1.shape[0]
    tm = WIDE_TILE

    def body(dx1_ref, mix_ref, pa_ref, pb_ref, pg_ref, v_ref, wba_ref, wbm_ref, wout_ref,
             dmix_ref, dpa_ref, dpb_ref, dg_ref, dya_ref, dym_ref, acc_ref):
        i = pl.program_id(0)

        @pl.when(i == 0)
        def _():
            acc_ref[...] = jnp.zeros_like(acc_ref)

        vrow = lambda r: v_ref[r:r + 1, :]
        dx1v = dx1_ref[...]
        mix = mix_ref[...]
        r2 = lax.rsqrt(_rowmean(mix * mix) + NORM_EPS)
        mixhat = mix * r2
        acc_ref[0:1, :] += _colsum(dx1v * (mixhat * vrow(R_G_POST_MIX)))
        dn2 = dx1v * vrow(R_GATE_M)
        acc_ref[1:2, :] += _colsum(dn2 * mixhat)
        dmixhat = dn2 * vrow(R_G_POST_MIX)
        dmixb = (r2 * (dmixhat - mixhat * _rowmean(dmixhat * mixhat))).astype(BF16)
        dmix_ref[...] = dmixb
        dmerged = _dot_nt(dmixb, wout_ref[...])
        sa = _sigmoid(pg_ref[:, 0:D_MODEL].astype(F32))
        sm = _sigmoid(pg_ref[:, D_MODEL:G_W].astype(F32))
        dpab = (dmerged * sa).astype(BF16)
        dpbb = (dmerged * sm).astype(BF16)
        dpa_ref[...] = dpab
        dpb_ref[...] = dpbb
        dg_ref[:, 0:D_MODEL] = (dmerged * pa_ref[...].astype(F32) * (sa * (1.0 - sa))).astype(BF16)
        dg_ref[:, D_MODEL:G_W] = (dmerged * pb_ref[...].astype(F32) * (sm * (1.0 - sm))).astype(BF16)
        dya_ref[...] = _dot_nt(dpab, wba_ref[...])
        dym_ref[...] = _dot_nt(dpbb, wbm_ref[...])

    row = lambda w: pl.BlockSpec((tm, w), lambda i: (i, 0))
    sd = lambda w, dt: jax.ShapeDtypeStruct((S, w), dt)
    return pl.pallas_call(
        body, name="mix_bwd", grid=(S // tm,),
        in_specs=[row(D_MODEL), row(D_MODEL), row(D_MODEL), row(D_MODEL), row(G_W), _full(vecs.shape),
                  _full(w_ba.shape), _full(w_bm.shape), _full(w_out.shape)],
        out_specs=[row(D_MODEL), row(D_MODEL), row(D_MODEL), row(G_W), row(512), row(512), _full((8, D_MODEL))],
        out_shape=[sd(D_MODEL, BF16), sd(D_MODEL, BF16), sd(D_MODEL, BF16), sd(G_W, BF16), sd(512, F32), sd(512, F32),
                   jax.ShapeDtypeStruct((8, D_MODEL), F32)],
        compiler_params=_params(("arbitrary",)),
    )(dx1, mix, pa, pb, pg, vecs, w_ba, w_bm, w_out)


def _pre_bwd(dproj, x, dx1, vecs, w_cat):
    S = x.shape[0]
    tm = WIDE_TILE

    def body(dp_ref, x_ref, dx1_ref, v_ref, w_ref, dx_ref, acc_ref):
        i = pl.program_id(0)

        @pl.when(i == 0)
        def _():
            acc_ref[...] = jnp.zeros_like(acc_ref)

        vrow = lambda r: v_ref[r:r + 1, :]
        dh = _dot_nt(dp_ref[...], w_ref[...])
        xv = x_ref[...]
        r1 = lax.rsqrt(_rowmean(xv * xv) + NORM_EPS)
        xhat = xv * r1
        acc_ref[0:1, :] += _colsum(dh * (xhat * vrow(R_G_PRE_MIX)))
        acc_ref[1:2, :] += _colsum(dh)
        dxn = dh * (1.0 + vrow(R_SCALE_M))
        acc_ref[2:3, :] += _colsum(dxn * xhat)
        dxhat = dxn * vrow(R_G_PRE_MIX)
        dx_ref[...] = dx1_ref[...] + r1 * (dxhat - xhat * _rowmean(dxhat * xhat))

    row = lambda w: pl.BlockSpec((tm, w), lambda i: (i, 0))
    return pl.pallas_call(
        body, name="pre_bwd", grid=(S // tm,),
        in_specs=[row(CAT_W), row(D_MODEL), row(D_MODEL), _full(vecs.shape), _full(w_cat.shape)],
        out_specs=[row(D_MODEL), _full((8, D_MODEL))],
        out_shape=[jax.ShapeDtypeStruct((S, D_MODEL), F32), jax.ShapeDtypeStruct((8, D_MODEL), F32)],
        compiler_params=_params(("arbitrary",)),
    )(dproj, x, dx1, vecs, w_cat)


def _matmul_tn(a, b, tn, name, ts=1024):
    S, K = a.shape
    N = b.shape[1]
    n_s = S // ts

    def body(a_ref, b_ref, o_ref, acc_ref):
        s = pl.program_id(1)

        @pl.when(s == 0)
        def _():
            acc_ref[...] = jnp.zeros_like(acc_ref)

        acc_ref[...] += _dot_tn(a_ref[...], b_ref[...])

        @pl.when(s == n_s - 1)
        def _():
            o_ref[...] = acc_ref[...].astype(BF16)

    return pl.pallas_call(
        body, name=name, grid=(N // tn, n_s),
        in_specs=[pl.BlockSpec((ts, K), lambda j, s: (s, 0)), pl.BlockSpec((ts, tn), lambda j, s: (s, j))],
        out_specs=pl.BlockSpec((K, tn), lambda j, s: (0, j)),
        out_shape=jax.ShapeDtypeStruct((K, N), BF16),
        scratch_shapes=[pltpu.VMEM((K, tn), F32)],
        compiler_params=_params(("parallel", "arbitrary")),
    )(a, b)


def _rope_swap(t):
    lane = lax.broadcasted_iota(jnp.int32, t.shape, 1)
    first = (lane & (HEAD_DIM - 1)) < (HEAD_DIM // 2)
    return jnp.where(first, pltpu.roll(t, LANES - HEAD_DIM // 2, 1), pltpu.roll(t, HEAD_DIM // 2, 1))


def _rope(t, cos, sin_signed):
    return t * cos + _rope_swap(t) * sin_signed


def _rope_t(d, cos, sin_signed):
    return d * cos + _rope_swap(d * sin_signed)


def _to_kv_lanes(chunk, p, h):
    lane = lax.broadcasted_iota(jnp.int32, chunk.shape, 1)
    src = chunk if p == h else pltpu.roll(chunk, HEAD_DIM, 1)
    return jnp.where((lane >> 6) == h, src, jnp.zeros_like(src))


def _from_kv_lanes(o_a, o_b, h):
    lane = lax.broadcasted_iota(jnp.int32, o_a.shape, 1)
    a = o_a if h == 0 else pltpu.roll(o_a, HEAD_DIM, 1)
    b = o_b if h == 1 else pltpu.roll(o_b, HEAD_DIM, 1)
    return jnp.where(lane < HEAD_DIM, a, b)


def _band_mask(n):
    blk = ATTN_BLOCK
    qi = lax.broadcasted_iota(jnp.int32, (4 * blk, 2 * blk), 0) & (blk - 1)
    kj = lax.broadcasted_iota(jnp.int32, (4 * blk, 2 * blk), 1)
    return (kj > qi) & (kj <= qi + blk) & ((n > 0) | (kj >= blk))


def _stack_heads(chunks, h, dtype):
    parts = []
    for g in range(4):
        j = 4 * h + g
        parts.append(_to_kv_lanes(chunks[j // 2], j % 2, h))
    return jnp.concatenate(parts, axis=0).astype(dtype)


def _attn_fwd(pa, cos, sin, sinks):
    S = pa.shape[0]
    blk = ATTN_BLOCK
    nb = S // blk

    def body(sink_ref, cur_ref, prev_ref, cos_ref, sin_ref, cosp_ref, sinp_ref,
             ya_ref, qr_ref, kr_ref, vb_ref, lse_ref):
        n = pl.program_id(0)
        cos_c, sin_c = cos_ref[...], sin_ref[...]
        qch = [_rope(cur_ref[:, c * LANES:(c + 1) * LANES], cos_c, sin_c) * (HEAD_DIM ** -0.5) for c in range(4)]
        for c in range(4):
            qr_ref[:, c * LANES:(c + 1) * LANES] = qch[c].astype(BF16)
        k_cur = _rope(cur_ref[:, 512:640], cos_c, sin_c).astype(BF16)
        k_prev = _rope(prev_ref[:, 0:LANES], cosp_ref[...], sinp_ref[...]).astype(BF16)
        v_cur = cur_ref[:, 640:768].astype(BF16)
        v_prev = prev_ref[:, LANES:2 * LANES].astype(BF16)
        kr_ref[...] = k_cur
        vb_ref[...] = v_cur
        K = jnp.concatenate([k_prev, k_cur], axis=0)
        V = jnp.concatenate([v_prev, v_cur], axis=0)
        mask = _band_mask(n)
        rowg = lax.broadcasted_iota(jnp.int32, (4 * blk, 1), 0) >> 7
        lane = lax.broadcasted_iota(jnp.int32, (blk, LANES), 1)
        lse_tile = jnp.zeros((blk, LANES), F32)
        outs = []
        for h in range(2):
            qs = _stack_heads(qch, h, BF16)
            s = jnp.where(mask, _dot_nt(qs, K), NEG)
            sink = jnp.zeros((4 * blk, 1), F32)
            for g in range(4):
                sink = jnp.where(rowg == g, sink_ref[4 * h + g], sink)
            m = jnp.maximum(jnp.max(s, axis=1, keepdims=True), sink)
            p = jnp.exp(s - m)
            den = jnp.sum(p, axis=1, keepdims=True) + jnp.exp(sink - m)
            o = _dot((p / den).astype(BF16), V)
            lse = m + jnp.log(den)
            for g in range(4):
                outs.append(o[g * blk:(g + 1) * blk, :])
                lse_tile = jnp.where(lane == 4 * h + g, lse[g * blk:(g + 1) * blk, :], lse_tile)
        for c in range(4):
            ya_ref[:, c * LANES:(c + 1) * LANES] = _from_kv_lanes(outs[2 * c], outs[2 * c + 1], c // 2).astype(BF16)
        lse_ref[...] = lse_tile

    prev = lambda n: jnp.maximum(n - 1, 0)
    sd = lambda w, dt: jax.ShapeDtypeStruct((S, w), dt)
    return pl.pallas_call(
        body, name="attn_fwd", grid=(nb,),
        in_specs=[pl.BlockSpec(memory_space=pltpu.SMEM),
                  pl.BlockSpec((blk, A_W), lambda n: (n, 0)),
                  pl.BlockSpec((blk, 256), lambda n: (prev(n), 2)),
                  pl.BlockSpec((blk, LANES), lambda n: (n, 0)), pl.BlockSpec((blk, LANES), lambda n: (n, 0)),
                  pl.BlockSpec((blk, LANES), lambda n: (prev(n), 0)), pl.BlockSpec((blk, LANES), lambda n: (prev(n), 0))],
        out_specs=[pl.BlockSpec((blk, 512), lambda n: (n, 0)), pl.BlockSpec((blk, 512), lambda n: (n, 0)),
                   pl.BlockSpec((blk, LANES), lambda n: (n, 0)), pl.BlockSpec((blk, LANES), lambda n: (n, 0)),
                   pl.BlockSpec((blk, LANES), lambda n: (n, 0))],
        out_shape=[sd(512, BF16), sd(512, BF16), sd(LANES, BF16), sd(LANES, BF16), sd(LANES, F32)],
        compiler_params=_params(("parallel",)),
    )(sinks, pa, pa, cos, sin, cos, sin)


def _attn_bwd(dya, qr, kr, vb, lse, cos, sin, sinks):
    S = dya.shape[0]
    blk = ATTN_BLOCK
    nb = S // blk

    def body(sink_ref, dya_ref, qr_ref, kc_ref, kp_ref, vc_ref, vp_ref, lse_ref, cos_ref, sin_ref, cosp_ref, sinp_ref,
             dq_ref, dkv_ref, dsink_ref, ck, cv):
        n = pl.program_id(0)

        @pl.when(n == 0)
        def _():
            ck[...] = jnp.zeros_like(ck)
            cv[...] = jnp.zeros_like(cv)
            dsink_ref[...] = jnp.zeros_like(dsink_ref)

        @pl.when(n < nb)
        def _():
            K = jnp.concatenate([kp_ref[...], kc_ref[...]], axis=0)
            V = jnp.concatenate([vp_ref[...], vc_ref[...]], axis=0)
            qch = [qr_ref[:, c * LANES:(c + 1) * LANES] for c in range(4)]
            dch = [dya_ref[:, c * LANES:(c + 1) * LANES] for c in range(4)]
            lse_tile = lse_ref[...]
            mask = _band_mask(n)
            rowg = lax.broadcasted_iota(jnp.int32, (4 * blk, 1), 0) >> 7
            lane8 = lax.broadcasted_iota(jnp.int32, (8, LANES), 1)
            dk_acc = jnp.zeros((2 * blk, LANES), F32)
            dv_acc = jnp.zeros((2 * blk, LANES), F32)
            dsink = jnp.zeros((8, LANES), F32)
            dqs = []
            for h in range(2):
                qs = _stack_heads(qch, h, BF16)
                dos = _stack_heads(dch, h, BF16)
                lse_col = jnp.concatenate([lse_tile[:, 4 * h + g:4 * h + g + 1] for g in range(4)], axis=0)
                p = jnp.where(mask, jnp.exp(_dot_nt(qs, K) - lse_col), 0.0)
                dp = _dot_nt(dos, V)
                delta = jnp.sum(p * dp, axis=1, keepdims=True)
                dsb = (p * (dp - delta)).astype(BF16)
                dq = _dot(dsb, K)
                dk_acc = dk_acc + _dot_tn(dsb, qs)
                dv_acc = dv_acc + _dot_tn(p.astype(BF16), dos)
                sink = jnp.zeros((4 * blk, 1), F32)
                for g in range(4):
                    sink = jnp.where(rowg == g, sink_ref[4 * h + g], sink)
                ps_delta = jnp.exp(sink - lse_col) * delta
                for g in range(4):
                    dqs.append(dq[g * blk:(g + 1) * blk, :])
                    dsink = jnp.where(lane8 == 4 * h + g, dsink - jnp.sum(ps_delta[g * blk:(g + 1) * blk, :]), dsink)
            dsink_ref[...] += dsink
            cos_c, sin_c = cos_ref[...], sin_ref[...]
            for c in range(4):
                dqc = _from_kv_lanes(dqs[2 * c], dqs[2 * c + 1], c // 2) * (HEAD_DIM ** -0.5)
                dq_ref[:, c * LANES:(c + 1) * LANES] = _rope_t(dqc, cos_c, sin_c).astype(BF16)
            dkv_ref[:, 0:LANES] = _rope_t(dk_acc[0:blk, :] + ck[...], cosp_ref[...], sinp_ref[...]).astype(BF16)
            dkv_ref[:, LANES:2 * LANES] = (dv_acc[0:blk, :] + cv[...]).astype(BF16)
            ck[...] = dk_acc[blk:2 * blk, :]
            cv[...] = dv_acc[blk:2 * blk, :]

        @pl.when(n == nb)
        def _():
            dkv_ref[:, 0:LANES] = _rope_t(ck[...], cosp_ref[...], sinp_ref[...]).astype(BF16)
            dkv_ref[:, LANES:2 * LANES] = cv[...].astype(BF16)

    cur = lambda n: jnp.minimum(n, nb - 1)
    prev = lambda n: jnp.maximum(n - 1, 0)
    bs = lambda w, f: pl.BlockSpec((blk, w), lambda n: (f(n), 0))
    return pl.pallas_call(
        body, name="attn_bwd", grid=(nb + 1,),
        in_specs=[pl.BlockSpec(memory_space=pltpu.SMEM),
                  bs(512, cur), bs(512, cur), bs(LANES, cur), bs(LANES, prev), bs(LANES, cur), bs(LANES, prev),
                  bs(LANES, cur), bs(LANES, cur), bs(LANES, cur), bs(LANES, prev), bs(LANES, prev)],
        out_specs=[bs(512, cur), bs(256, prev), _full((8, LANES))],
        out_shape=[jax.ShapeDtypeStruct((S, 512), BF16), jax.ShapeDtypeStruct((S, 256), BF16),
                   jax.ShapeDtypeStruct((8, LANES), F32)],
        scratch_shapes=[pltpu.VMEM((blk, LANES), F32), pltpu.VMEM((blk, LANES), F32)],
        compiler_params=_params(("arbitrary",)),
    )(sinks, dya, qr, kr, kr, vb, vb, lse, cos, sin, cos, sin)


def _split3(x):
    hi = x.astype(BF16)
    r1 = x - hi.astype(F32)
    mid = r1.astype(BF16)
    lo = (r1 - mid.astype(F32)).astype(BF16)
    return hi, mid, lo


def _tri_matmul(tri_b, x):
    hi, mid, lo = _split3(x)
    return _dot(tri_b, hi) + _dot(tri_b, mid) + _dot(tri_b, lo)


def _log_sigmoid(x):
    return jnp.minimum(x, 0.0) - jnp.log(1.0 + jnp.exp(-jnp.abs(x)))


def _shift_rows(cur, seam, k, down):
    L = cur.shape[0]
    row8 = lax.broadcasted_iota(jnp.int32, seam.shape, 0)
    if down:
        mixed = jnp.concatenate([cur[:L - 8], jnp.where(row8 >= 8 - k, seam, cur[L - 8:])], axis=0)
        return pltpu.roll(mixed, k, 0)
    mixed = jnp.concatenate([jnp.where(row8 < k, seam, cur[:8]), cur[8:]], axis=0)
    return pltpu.roll(mixed, L - k, 0)


def _conv_fwd(cur, tail, cw_ref):
    z = cw_ref[4:5, :]
    for k in range(3, 0, -1):
        z = z + _shift_rows(cur, tail, k, True) * cw_ref[3 - k:4 - k, :]
    return z + cur * cw_ref[3:4, :]


def _mlstm_head_fwd(qh, kh, vh, i_col, b_col, C_prev, n_prev, m_prev, tri, eye):
    L = qh.shape[0]
    col2row = lambda x: jnp.sum(jnp.where(eye, x, 0.0), axis=0, keepdims=True)
    b_row = col2row(b_col)
    i_row = col2row(i_col)
    bl = b_col[L - 1:L, :]
    Dm = jnp.where(tri, b_col - b_row + i_row, NEG)
    inter = b_col + m_prev
    m_t = jnp.maximum(inter, jnp.max(Dm, axis=1, keepdims=True))
    W = jnp.exp(Dm - m_t)
    e_t = jnp.exp(inter - m_t)
    qb, kb, vb = qh.astype(BF16), kh.astype(BF16), vh.astype(BF16)
    Sc = _dot_nt(qb, kb) * W
    P1 = _dot(qb, C_prev.astype(BF16))
    num = _dot(Sc.astype(BF16), vb) + e_t * P1
    qn = jnp.sum(qh * n_prev, axis=1, keepdims=True)
    den = jnp.sum(Sc, axis=1, keepdims=True) + e_t * qn
    floor = jnp.exp(-m_t)
    g = jnp.maximum(jnp.abs(den), floor)
    hv = num / g
    a_col = bl - b_col + i_col
    m_new = jnp.maximum(bl + m_prev, jnp.max(a_col, axis=0, keepdims=True))
    dec = jnp.exp(bl + m_prev - m_new)
    u_col = jnp.exp(a_col - m_new)
    return dict(W=W, e_t=e_t, qb=qb, kb=kb, vb=vb, Sc=Sc, P1=P1, qn=qn, den=den, floor=floor, g=g, hv=hv,
                m_new=m_new, dec=dec, u_col=u_col)


def _mlstm_fwd(pm, pif, cw, sv):
    S = pm.shape[0]
    L = MLSTM_CHUNK
    nc = S // L
    HD = MLSTM_HEAD_DIM
    W4 = MLSTM_HEADS * HD

    def body(cur_ref, pif_ref, cw_ref, sv_ref, ym_ref, z_ref, cst_ref, nst_ref, C, nm, tail):
        c = pl.program_id(0)

        @pl.when(c == 0)
        def _():
            C[...] = jnp.zeros_like(C)
            nm[...] = jnp.zeros_like(nm)
            tail[...] = jnp.zeros_like(tail)

        z = _conv_fwd(cur_ref[:, 0:2 * W4], tail[...], cw_ref)
        tail[...] = cur_ref[L - 8:L, 0:2 * W4]
        z_ref[...] = z
        qk = z * _sigmoid(z)
        gt = pif_ref[...] + sv_ref[1:2, 0:LANES]
        r_i = lax.broadcasted_iota(jnp.int32, (L, L), 0)
        c_i = lax.broadcasted_iota(jnp.int32, (L, L), 1)
        tri = c_i <= r_i
        eye = c_i == r_i
        b_all = _tri_matmul(tri.astype(BF16), _log_sigmoid(gt))
        nst_ref[0] = nm[...]
        for h in range(MLSTM_HEADS):
            sl = slice(h * HD, (h + 1) * HD)
            qh = qk[:, sl]
            kh = qk[:, W4 + h * HD:W4 + (h + 1) * HD] * (HD ** -0.5)
            vh = cur_ref[:, 2 * W4 + h * HD:2 * W4 + (h + 1) * HD]
            n_prev = nm[h:h + 1, :]
            m_prev = nm[4 + h:5 + h, 0:1]
            C_prev = C[h]
            cst_ref[0, h] = C_prev
            f = _mlstm_head_fwd(qh, kh, vh, gt[:, h:h + 1], b_all[:, 4 + h:5 + h], C_prev, n_prev, m_prev, tri, eye)
            hv = f["hv"]
            xc = hv - _rowmean(hv)
            hhat = xc * lax.rsqrt(_rowmean(xc * xc) + NORM_EPS)
            so = _sigmoid(cur_ref[:, 3 * W4 + h * HD:3 * W4 + (h + 1) * HD])
            ym_ref[:, sl] = (so * hhat * sv_ref[0:1, sl]).astype(BF16)
            kw = kh * f["u_col"]
            C[h] = f["dec"] * C_prev + _dot_tn(kw.astype(BF16), f["vb"])
            nm[h:h + 1, :] = f["dec"] * n_prev + _colsum(kw)
            nm[4 + h:5 + h, :] = jnp.broadcast_to(f["m_new"], (1, LANES))

    return pl.pallas_call(
        body, name="mlstm_fwd", grid=(nc,),
        in_specs=[pl.BlockSpec((L, M_W), lambda c: (c, 0)),
                  pl.BlockSpec((L, IF_W), lambda c: (c, 0)), _full(cw.shape), _full(sv.shape)],
        out_specs=[pl.BlockSpec((L, W4), lambda c: (c, 0)), pl.BlockSpec((L, 2 * W4), lambda c: (c, 0)),
                   pl.BlockSpec((1, MLSTM_HEADS, HD, HD), lambda c: (c, 0, 0, 0)),
                   pl.BlockSpec((1, 8, LANES), lambda c: (c, 0, 0))],
        out_shape=[jax.ShapeDtypeStruct((S, W4), BF16), jax.ShapeDtypeStruct((S, 2 * W4), F32),
                   jax.ShapeDtypeStruct((nc, MLSTM_HEADS, HD, HD), F32), jax.ShapeDtypeStruct((nc, 8, LANES), F32)],
        scratch_shapes=[pltpu.VMEM((MLSTM_HEADS, HD, HD), F32), pltpu.VMEM((8, LANES), F32),
                        pltpu.VMEM((8, 2 * W4), F32)],
        compiler_params=_params(("arbitrary",)),
    )(pm, pif, cw, sv)


def _mlstm_bwd(pm, zc, pif, cw, sv, dym, cst, nst):
    S = pm.shape[0]
    L = MLSTM_CHUNK
    nc = S // L
    HD = MLSTM_HEAD_DIM
    W4 = MLSTM_HEADS * HD

    def body(cur_ref, z_ref, pif_ref, cw_ref, sv_ref, dym_ref, cst_ref, nst_ref,
             dm_ref, dif_ref, dcw_ref, dsv_ref, dC, dn, dz_next, dqk):
        r = pl.program_id(0)
        c = nc - 1 - r

        @pl.when(r == 0)
        def _():
            dC[...] = jnp.zeros_like(dC)
            dn[...] = jnp.zeros_like(dn)
            dz_next[...] = jnp.zeros_like(dz_next)
            dcw_ref[...] = jnp.zeros_like(dcw_ref)
            dsv_ref[...] = jnp.zeros_like(dsv_ref)

        z = z_ref[...]
        sgz = _sigmoid(z)
        qk = z * sgz
        gt = pif_ref[...] + sv_ref[1:2, 0:LANES]
        r_i = lax.broadcasted_iota(jnp.int32, (L, L), 0)
        c_i = lax.broadcasted_iota(jnp.int32, (L, L), 1)
        tri = c_i <= r_i
        eye = c_i == r_i
        b_all = _tri_matmul(tri.astype(BF16), _log_sigmoid(gt))
        lane = lax.broadcasted_iota(jnp.int32, (L, LANES), 1)
        rowl = lax.broadcasted_iota(jnp.int32, (L, 1), 0)
        nmv = nst_ref[0]
        di_tile = jnp.zeros((L, LANES), F32)
        db_tile = jnp.zeros((L, LANES), F32)
        for h in range(MLSTM_HEADS):
            sl = slice(h * HD, (h + 1) * HD)
            qh = qk[:, sl]
            kh = qk[:, W4 + h * HD:W4 + (h + 1) * HD] * (HD ** -0.5)
            vh = cur_ref[:, 2 * W4 + h * HD:2 * W4 + (h + 1) * HD]
            n_prev = nmv[h:h + 1, :]
            m_prev = nmv[4 + h:5 + h, 0:1]
            C_prev = cst_ref[0, h]
            f = _mlstm_head_fwd(qh, kh, vh, gt[:, h:h + 1], b_all[:, 4 + h:5 + h], C_prev, n_prev, m_prev, tri, eye)
            hv, g, den, e_t, u_col, dec = f["hv"], f["g"], f["den"], f["e_t"], f["u_col"], f["dec"]
            qb, kb, vb, Sc, W = f["qb"], f["kb"], f["vb"], f["Sc"], f["W"]
            xc = hv - _rowmean(hv)
            rstd = lax.rsqrt(_rowmean(xc * xc) + NORM_EPS)
            hhat = xc * rstd
            wn = sv_ref[0:1, sl]
            so = _sigmoid(cur_ref[:, 3 * W4 + h * HD:3 * W4 + (h + 1) * HD])
            dy = dym_ref[:, sl]
            dm_ref[:, 3 * W4 + h * HD:3 * W4 + (h + 1) * HD] = (dy * hhat * wn * (so * (1.0 - so))).astype(BF16)
            dln = dy * so
            dsv_ref[0:1, sl] += _colsum(dln * hhat)
            dhhat = dln * wn
            dh = rstd * (dhhat - _rowmean(dhhat) - hhat * _rowmean(dhhat * hhat))
            dnum = dh / g
            active = jnp.abs(den) > f["floor"]
            dden = jnp.where(active, -jnp.sum(dh * hv, axis=1, keepdims=True) / g * jnp.where(den >= 0.0, 1.0, -1.0), 0.0)
            dnumb = dnum.astype(BF16)
            dSc = _dot_nt(dnumb, vb) + dden
            dA = (dSc * W).astype(BF16)
            G = dSc * Sc
            Gr = jnp.sum(G, axis=1, keepdims=True)
            Gc = jnp.sum(jnp.where(eye, _colsum(G), 0.0), axis=1, keepdims=True)
            dCn = dC[h]
            dCnb = dCn.astype(BF16)
            dn_new = dn[h:h + 1, :]
            kdC = _dot(kb, dCnb)
            vdC = _dot_nt(vb, dCnb)
            dv = _dot_tn(Sc.astype(BF16), dnumb) + u_col * kdC
            Cb = C_prev.astype(BF16)
            dq = _dot(dA, kb) + e_t * _dot_nt(dnumb, Cb) + (e_t * dden) * n_prev
            dk = _dot_tn(dA, qb) + u_col * (vdC + dn_new)
            de = jnp.sum(f["P1"] * dnum, axis=1, keepdims=True) + dden * f["qn"]
            E = de * e_t
            du = jnp.sum(kdC * vh, axis=1, keepdims=True) + jnp.sum(kh * dn_new, axis=1, keepdims=True)
            U = du * u_col
            ddec = jnp.sum(dCn * C_prev) + jnp.sum(dn_new * n_prev)
            dbl = ddec * dec + jnp.sum(U, axis=0, keepdims=True)
            db = Gr + E - Gc - U + jnp.where(rowl == L - 1, dbl, 0.0)
            di_tile = jnp.where(lane == h, Gc + U, di_tile)
            db_tile = jnp.where(lane == 4 + h, db, db_tile)
            dC[h] = dec * dCn + _dot_tn((qh * e_t).astype(BF16), dnumb)
            dn[h:h + 1, :] = dec * dn_new + _colsum((e_t * dden) * qh)
            dqk[:, sl] = dq
            dqk[:, W4 + h * HD:W4 + (h + 1) * HD] = dk * (HD ** -0.5)
            dm_ref[:, 2 * W4 + h * HD:2 * W4 + (h + 1) * HD] = dv.astype(BF16)
        dlf = _tri_matmul((r_i <= c_i).astype(BF16), db_tile)
        dif = jnp.where(lane < 4, di_tile, jnp.where(lane < 8, dlf * (1.0 - _sigmoid(gt)), 0.0))
        dif_ref[...] = dif.astype(BF16)
        dsv_ref[1:2, 0:LANES] += _colsum(dif)
        dz = dqk[...] * (sgz * (1.0 + z * (1.0 - sgz)))
        dcw_ref[4:5, :] += _colsum(dz)
        u = cur_ref[:, 0:2 * W4]
        du_in = dz * cw_ref[3:4, :]
        dcw_ref[3:4, :] += _colsum(dz * u)
        for k in range(1, 4):
            up = _shift_rows(dz, dz_next[...], k, False)
            dcw_ref[3 - k:4 - k, :] += _colsum(up * u)
            du_in = du_in + up * cw_ref[3 - k:4 - k, :]
        dz_next[...] = dz[0:8, :]
        dm_ref[:, 0:2 * W4] = du_in.astype(BF16)

    cidx = lambda r: nc - 1 - r
    return pl.pallas_call(
        body, name="mlstm_bwd", grid=(nc,),
        in_specs=[pl.BlockSpec((L, M_W), lambda r: (cidx(r), 0)), pl.BlockSpec((L, 2 * W4), lambda r: (cidx(r), 0)),
                  pl.BlockSpec((L, IF_W), lambda r: (cidx(r), 0)), _full(cw.shape), _full(sv.shape),
                  pl.BlockSpec((L, W4), lambda r: (cidx(r), 0)),
                  pl.BlockSpec((1, MLSTM_HEADS, HD, HD), lambda r: (cidx(r), 0, 0, 0)),
                  pl.BlockSpec((1, 8, LANES), lambda r: (cidx(r), 0, 0))],
        out_specs=[pl.BlockSpec((L, M_W), lambda r: (cidx(r), 0)), pl.BlockSpec((L, IF_W), lambda r: (cidx(r), 0)),
                   _full((8, 2 * W4)), _full((8, W4))],
        out_shape=[jax.ShapeDtypeStruct((S, M_W), BF16), jax.ShapeDtypeStruct((S, IF_W), BF16),
                   jax.ShapeDtypeStruct((8, 2 * W4), F32), jax.ShapeDtypeStruct((8, W4), F32)],
        scratch_shapes=[pltpu.VMEM((MLSTM_HEADS, HD, HD), F32), pltpu.VMEM((8, LANES), F32),
                        pltpu.VMEM((8, 2 * W4), F32), pltpu.VMEM((L, 2 * W4), F32)],
        compiler_params=_params(("arbitrary",)),
    )(pm, zc, pif, cw, sv, dym, cst, nst)


def _rope_tables(positions):
    half = HEAD_DIM // 2
    inv_freq = ROPE_THETA ** (-2.0 * jnp.arange(half, dtype=F32) / HEAD_DIM)
    ang = positions.astype(F32)[:, None] * inv_freq
    cos = jnp.tile(jnp.cos(ang), (1, LANES // half))
    sign = jnp.tile(jnp.concatenate([-jnp.ones((half,), F32), jnp.ones((half,), F32)]), LANES // HEAD_DIM)
    sin = jnp.tile(jnp.sin(ang), (1, LANES // half)) * sign
    return cos, sin


def _local_step(x, tgt, positions, mod, gains, w_cat, w_ba, w_bm, w_out, w_gate, w_up, w_down,
                conv_w, conv_b, b_if, sinks, norm_w):
    t = _tables(mod, gains, conv_w, conv_b, b_if, norm_w, positions)
    a = _mixer_fwd(x, t, sinks, w_cat)
    b = _ffn_part(x, tgt, t, a, w_ba, w_bm, w_out, w_gate, w_up, w_down)
    c = _mixer_bwd(b["dx1"], t, a, b, sinks, w_ba, w_bm, w_out)
    grad_x, acc_p = _pre_bwd(c["dproj"], x, b["dx1"], t["vecs"], w_cat)
    big = dict(w_cat=c["g_w_cat"], w_ba=c["g_w_ba"], w_bm=c["g_w_bm"], w_out=c["g_w_out"], w_gate=b["g_w_gate"],
               w_up=b["g_w_up"], w_down=b["g_w_down"])
    return b["loss"], grad_x, big, _small_grads(acc_p, b, c)


def _tables(mod, gains, conv_w, conv_b, b_if, norm_w, positions):
    cos, sin = _rope_tables(positions)
    return dict(
        vecs=jnp.concatenate([mod, gains, jnp.zeros((6, D_MODEL), F32)], axis=0),
        cw=jnp.concatenate([conv_w, conv_b.reshape(1, -1), jnp.zeros((3, 2 * 512), F32)], axis=0),
        sv=jnp.zeros((8, 512), F32).at[0].set(norm_w).at[1, 0:8].set(b_if), cos=cos, sin=sin)


def _mixer_fwd(x, t, sinks, w_cat):
    h, pa, pm, pif, pg = _pre_proj(x, t["vecs"], w_cat)
    ya, qr, kr, vb, lse = _attn_fwd(pa, t["cos"], t["sin"], sinks)
    ym, zc, cst, nst = _mlstm_fwd(pm, pif, t["cw"], t["sv"])
    return dict(h=h, pm=pm, pif=pif, pg=pg, ya=ya, qr=qr, kr=kr, vb=vb, lse=lse, ym=ym, zc=zc, cst=cst, nst=nst)


def _ffn_part(x, tgt, t, a, w_ba, w_bm, w_out, w_gate, w_up, w_down):
    x1, merged, mix, pba, pbm = _mix_fwd(x, a["ya"], a["ym"], a["pg"], t["vecs"], w_ba, w_bm, w_out)
    dx1, h2, hid, da, du, dff, acc_f, loss = _ffn_fwd_bwd(x1, tgt, t["vecs"], w_gate, w_up, w_down)
    return dict(merged=merged, mix=mix, pba=pba, pbm=pbm, dx1=dx1, acc_f=acc_f, loss=loss[0, 0],
                g_w_gate=_matmul_tn(da, h2, 1024, "dw_ffn_gate"),
                g_w_up=_matmul_tn(du, h2, 1024, "dw_ffn_up"),
                g_w_down=_matmul_tn(hid, dff, 1024, "dw_ffn_down"))


def _mixer_bwd(dx1, t, a, b, sinks, w_ba, w_bm, w_out):
    dmix, dpa, dpb, dg, dya, dym, acc_m = _mix_bwd(dx1, b["mix"], b["pba"], b["pbm"], a["pg"], t["vecs"], w_ba, w_bm, w_out)
    g_w_out = _matmul_tn(b["merged"], dmix, 1024, "dw_out")
    g_w_ba = _matmul_tn(a["ya"], dpa, 1024, "dw_branch_attn")
    g_w_bm = _matmul_tn(a["ym"], dpb, 1024, "dw_branch_mlstm")
    dq, dkv, dsink = _attn_bwd(dya, a["qr"], a["kr"], a["vb"], a["lse"], t["cos"], t["sin"], sinks)
    dm, dif, dcw, dsv = _mlstm_bwd(a["pm"], a["zc"], a["pif"], t["cw"], t["sv"], dym, a["cst"], a["nst"])
    dproj = jnp.concatenate([dq, dkv, dm, dif, dg], axis=1)
    return dict(dproj=dproj, g_w_cat=_matmul_tn(a["h"], dproj, 1664, "dw_in"), g_w_out=g_w_out, g_w_ba=g_w_ba,
                g_w_bm=g_w_bm, acc_m=acc_m, dsink=dsink, dcw=dcw, dsv=dsv)


def _small_grads(acc_p, b, c):
    acc_f, acc_m = b["acc_f"], c["acc_m"]
    dmod = jnp.stack([acc_p[1], acc_p[0], acc_m[0], acc_f[3], acc_f[2], acc_f[0]])
    dgains = jnp.stack([acc_p[2], acc_m[1], acc_f[4], acc_f[1]])
    return dict(dmod=dmod, dgains=dgains, dconv_w=c["dcw"][0:4], dconv_b=c["dcw"][4], db_if=c["dsv"][1, 0:8],
                dsinks=c["dsink"][0, 0:8], dnorm_w=c["dsv"][0])


MESH_ID = pl.DeviceIdType.MESH


def _mesh_pos():
    return lax.axis_index("x"), lax.axis_index("y"), lax.axis_index("c")


def _flip(v, bit):
    return 1 - v if bit else v


def _relations():
    return [((r >> 2) & 1, (r >> 1) & 1, r & 1) for r in range(1, N_DEV)]


def _small_exchange(p, gather, name):
    V = p.shape[-1]

    def body(p_ref, out_ref, send_sems, recv_sems):
        x, y, c = _mesh_pos()
        me = 4 * x + 2 * y + c
        out_ref[me] = p_ref[...] if gather else p_ref[me]
        peers = []
        for dx, dy, dc in _relations():
            px, py, pc = _flip(x, dx), _flip(y, dy), _flip(c, dc)
            peers.append(((px, py, pc), 4 * px + 2 * py + pc))

        def copy(k, landing):
            peer, pid = peers[k]
            return pltpu.make_async_remote_copy(
                src_ref=p_ref if gather else p_ref.at[pid], dst_ref=out_ref.at[landing],
                send_sem=send_sems.at[k], recv_sem=recv_sems.at[k], device_id=peer, device_id_type=MESH_ID)

        sends = [copy(k, me) for k in range(N_DEV - 1)]
        for cp in sends:
            cp.start()
        for k in range(N_DEV - 1):
            copy(k, peers[k][1]).wait_recv()
        for cp in sends:
            cp.wait_send()

    vm = pl.BlockSpec(memory_space=pltpu.VMEM)
    return pl.pallas_call(
        body, name=name, in_specs=[vm], out_specs=vm,
        out_shape=jax.ShapeDtypeStruct((N_DEV, 8, V), F32),
        scratch_shapes=[pltpu.SemaphoreType.DMA((N_DEV - 1,)), pltpu.SemaphoreType.DMA((N_DEV - 1,))],
        compiler_params=pltpu.CompilerParams(vmem_limit_bytes=VMEM_LIMIT),
    )(p)


HBM_SPEC = pl.BlockSpec(memory_space=pltpu.HBM)
SEM_SPEC = pl.BlockSpec(memory_space=pltpu.SEMAPHORE)


def _peers(x, y, c):
    out = []
    for dx, dy, dc in _relations():
        px, py, pc = _flip(x, dx), _flip(y, dy), _flip(c, dc)
        out.append(((px, py, pc), 4 * px + 2 * py + pc))
    return out


def _exchange_start(arrs, gather, after, name):
    n = len(arrs)
    me_out = 4 * lax.axis_index("x") + 2 * lax.axis_index("y") + lax.axis_index("c")
    lands = []
    for a in arrs:
        own = a[None] if gather else lax.dynamic_index_in_dim(a, me_out, 0, keepdims=True)
        empty = lax.empty(((N_DEV,) + a.shape) if gather else a.shape, a.dtype)
        lands.append(lax.dynamic_update_index_in_dim(empty, own, me_out, 0))

    def body(*refs):
        a_refs, l_refs = refs[:n], refs[n:2 * n]
        send_sems, recv_sems = refs[2 * n + 1], refs[2 * n + 2]
        token = refs[4 * n + 3]
        x, y, c = _mesh_pos()
        me = 4 * x + 2 * y + c
        for a in range(n):
            for k, (peer, pid) in enumerate(_peers(x, y, c)):
                pltpu.make_async_remote_copy(
                    src_ref=a_refs[a] if gather else a_refs[a].at[pid], dst_ref=l_refs[a].at[me],
                    send_sem=send_sems.at[a * (N_DEV - 1) + k], recv_sem=recv_sems.at[a * (N_DEV - 1) + k],
                    device_id=peer, device_id_type=MESH_ID).start()
        token[...] = jnp.zeros_like(token)

    sem = pltpu.SemaphoreType.DMA((n * (N_DEV - 1),))
    hbm = lambda a: pltpu.with_memory_space_constraint(a, pltpu.HBM)
    res = pl.pallas_call(
        body, name=name,
        out_shape=(sem, sem, *[pltpu.HBM(a.shape, a.dtype) for a in arrs], *[pltpu.HBM(l.shape, l.dtype) for l in lands],
                   jax.ShapeDtypeStruct((8, LANES), F32)),
        in_specs=[HBM_SPEC] * (2 * n) + [pl.BlockSpec(memory_space=pl.ANY)],
        out_specs=(SEM_SPEC, SEM_SPEC, *[HBM_SPEC] * (2 * n), pl.BlockSpec(memory_space=pltpu.VMEM)),
        input_output_aliases={i: 2 + i for i in range(2 * n)},
        compiler_params=pltpu.CompilerParams(has_side_effects=pltpu.SideEffectType.DATAFLOW_SIDE_EFFECTING),
    )(*[hbm(a) for a in arrs], *[hbm(l) for l in lands], after)
    return dict(sems=res[0:2], arrs=res[2:2 + n], lands=res[2 + n:2 + 2 * n], token=res[2 + 2 * n], gather=gather)


def _exchange_wait(st, after, name):
    n = len(st["arrs"])
    gather = st["gather"]

    def body(*refs):
        a_refs, l_refs = refs[:n], refs[n:2 * n]
        send_sems, recv_sems = refs[2 * n], refs[2 * n + 1]
        x, y, c = _mesh_pos()
        for a in range(n):
            for k, (peer, pid) in enumerate(_peers(x, y, c)):
                cp = pltpu.make_async_remote_copy(
                    src_ref=a_refs[a] if gather else a_refs[a].at[pid], dst_ref=l_refs[a].at[pid],
                    send_sem=send_sems.at[a * (N_DEV - 1) + k], recv_sem=recv_sems.at[a * (N_DEV - 1) + k],
                    device_id=peer, device_id_type=MESH_ID)
                cp.wait_send()
                cp.wait_recv()

    both = list(st["arrs"]) + list(st["lands"])
    res = pl.pallas_call(
        body, name=name, out_shape=[pltpu.HBM(a.shape, a.dtype) for a in both],
        in_specs=[HBM_SPEC] * (2 * n) + [SEM_SPEC, SEM_SPEC, pl.BlockSpec(memory_space=pl.ANY)],
        out_specs=[HBM_SPEC] * (2 * n), input_output_aliases={i: i for i in range(2 * n)},
        compiler_params=pltpu.CompilerParams(has_side_effects=pltpu.SideEffectType.DATAFLOW_SIDE_EFFECTING),
    )(*both, *st["sems"], after)
    return res[n:2 * n]


def _after(x, token):
    return lax.optimization_barrier((x, token))[0]


def _all_gather_hbm(shards):
    n = len(shards)

    def body(*refs):
        p_refs, out_refs = refs[:n], refs[n:2 * n]
        send_sems, recv_sems, local_sems = refs[2 * n:]
        x, y, c = _mesh_pos()
        me, sibling = (x, y, c), (x, y, 1 - c)
        chips = [(1 - x, y), (x, 1 - y), (1 - x, 1 - y)]

        def copy(a, k, block, to, own=False):
            slot = out_refs[a].at[4 * block[0] + 2 * block[1] + block[2]]
            return pltpu.make_async_remote_copy(
                src_ref=p_refs[a] if own else slot, dst_ref=slot,
                send_sem=send_sems.at[a, k], recv_sem=recv_sems.at[a, k], device_id=to, device_id_type=MESH_ID)

        mine = [pltpu.make_async_copy(p_refs[a], out_refs[a].at[4 * x + 2 * y + c], local_sems.at[a]) for a in range(n)]
        for cp in mine:
            cp.start()
        first = []
        for a in range(n):
            first.append(copy(a, 0, me, sibling, own=True))
            first += [copy(a, 1 + j, me, (*chip, c), own=True) for j, chip in enumerate(chips)]
        for cp in first:
            cp.start()
        passed = []
        for j, chip in enumerate(chips):
            for a in range(n):
                copy(a, 1 + j, (*chip, c), me).wait_recv()
                passed.append(copy(a, 4 + j, (*chip, c), sibling))
                passed[-1].start()
        for a in range(n):
            copy(a, 0, sibling, me).wait_recv()
            for j, chip in enumerate(chips):
                copy(a, 4 + j, (*chip, 1 - c), me).wait_recv()
        for cp in first + passed:
            cp.wait_send()
        for cp in mine:
            cp.wait()

    hbm = pl.BlockSpec(memory_space=pl.ANY)
    return pl.pallas_call(
        body, name="gather_weights", in_specs=[hbm] * n, out_specs=[hbm] * n,
        out_shape=[jax.ShapeDtypeStruct((N_DEV,) + s.shape, s.dtype) for s in shards],
        scratch_shapes=[pltpu.SemaphoreType.DMA((n, N_DEV - 1)), pltpu.SemaphoreType.DMA((n, N_DEV - 1)),
                        pltpu.SemaphoreType.DMA((n,))],
    )(*shards)


def _adamw(w, g, m, v):
    m2 = ADAM_B1 * m + (1.0 - ADAM_B1) * g
    v2 = ADAM_B2 * v + (1.0 - ADAM_B2) * (g * g)
    m_hat = m2 / (1.0 - ADAM_B1 ** ADAM_STEP)
    v_hat = v2 / (1.0 - ADAM_B2 ** ADAM_STEP)
    delta = -ADAM_LR * (m_hat / (jnp.sqrt(v_hat) + ADAM_EPS) + ADAM_WD * w)
    return delta, m2, v2


def _mod_partial(cmat, w_shard, b_shard):
    def body(c_ref, w_ref, b_ref, o_ref):
        o_ref[...] = _dot(c_ref[...].astype(BF16), w_ref[...].astype(BF16)) + b_ref[...]

    return pl.pallas_call(
        body, name="mod_partial", out_shape=jax.ShapeDtypeStruct((N_DEV, w_shard.shape[1]), F32),
        compiler_params=_params(),
    )(cmat, w_shard, b_shard)


def _adamw_w_ada(cmat, dmod_cols, w, m, v):
    def body(c_ref, d_ref, w_ref, m_ref, v_ref, g_ref, dl_ref, m2_ref, v2_ref):
        g = _dot_tn(c_ref[...].astype(BF16), d_ref[...].astype(BF16))
        g_ref[...] = g
        dl_ref[...], m2_ref[...], v2_ref[...] = _adamw(w_ref[...], g, m_ref[...], v_ref[...])

    return pl.pallas_call(
        body, name="adamw_w_ada", out_shape=[jax.ShapeDtypeStruct(w.shape, F32)] * 4,
        compiler_params=_params(),
    )(cmat, dmod_cols, w, m, v)


def _adamw_small(gathered, w, m, v):
    def body(g_ref, w_ref, m_ref, v_ref, go_ref, dl_ref, m2_ref, v2_ref):
        g = g_ref[0]
        for k in range(1, N_DEV):
            g = g + g_ref[k]
        go_ref[...] = g
        dl_ref[...], m2_ref[...], v2_ref[...] = _adamw(w_ref[...], g, m_ref[...], v_ref[...])

    return pl.pallas_call(
        body, name="adamw_small", out_shape=[jax.ShapeDtypeStruct(w.shape, F32)] * 4,
        compiler_params=_params(),
    )(gathered, w, m, v)


def _row_tile(rows):
    return rows // 4 if rows >= 512 else rows


def _sum_partials(r_ref):
    g = r_ref[0].astype(F32)
    for k in range(1, N_DEV):
        g = g + r_ref[k].astype(F32)
    return g


def _adamw_sum(recv, w, m, v, name):
    r, cdim = w.shape
    tr = _row_tile(r)

    def body(r_ref, w_ref, m_ref, v_ref, g_ref, dl_ref, m2_ref, v2_ref):
        g = _sum_partials(r_ref)
        g_ref[...] = g
        dl_ref[...], m2_ref[...], v2_ref[...] = _adamw(w_ref[...], g, m_ref[...], v_ref[...])

    row = pl.BlockSpec((tr, cdim), lambda i: (i, 0))
    return pl.pallas_call(
        body, name=name, grid=(r // tr,),
        in_specs=[pl.BlockSpec((N_DEV, tr, cdim), lambda i: (0, i, 0)), row, row, row],
        out_specs=[row] * 4, out_shape=[jax.ShapeDtypeStruct((r, cdim), F32)] * 4,
        compiler_params=_params(("parallel",)),
    )(recv, w, m, v)


def _sum8(recv, name):
    _, r, cdim = recv.shape
    tr = _row_tile(r)

    def body(r_ref, g_ref):
        g_ref[...] = _sum_partials(r_ref)

    return pl.pallas_call(
        body, name=name, grid=(r // tr,),
        in_specs=[pl.BlockSpec((N_DEV, tr, cdim), lambda i: (0, i, 0))],
        out_specs=pl.BlockSpec((tr, cdim), lambda i: (i, 0)), out_shape=jax.ShapeDtypeStruct((r, cdim), F32),
        compiler_params=_params(("parallel",)),
    )(recv)


def _adamw_plain(g, w, m, v, name):
    r, cdim = w.shape
    tr = _row_tile(r)

    def body(g_ref, w_ref, m_ref, v_ref, dl_ref, m2_ref, v2_ref):
        dl_ref[...], m2_ref[...], v2_ref[...] = _adamw(w_ref[...], g_ref[...], m_ref[...], v_ref[...])

    row = pl.BlockSpec((tr, cdim), lambda i: (i, 0))
    return pl.pallas_call(
        body, name=name, grid=(r // tr,), in_specs=[row] * 4, out_specs=[row] * 3,
        out_shape=[jax.ShapeDtypeStruct((r, cdim), F32)] * 3,
        compiler_params=_params(("parallel",)),
    )(g, w, m, v)


IN_SHARD = 609
IN_SHARD_PAD = 640
IF_AT = A_W + M_W


def _regrouped(u):
    return u if u < IF_AT + 8 else u + (IF_W - 8)


def _selection(k, rows, row0, transpose):
    shape = (rows, IN_SHARD_PAD) if transpose else (IN_SHARD_PAD, rows)
    l = lax.broadcasted_iota(jnp.int32, shape, 1 if transpose else 0)
    r = lax.broadcasted_iota(jnp.int32, shape, 0 if transpose else 1) + row0
    u = l + IN_SHARD * k
    ru = u + jnp.where(u >= IF_AT + 8, IF_W - 8, 0)
    return ((ru == r) & (l < IN_SHARD)).astype(BF16)


def _regroup_w_in(g):
    def body(g_ref, o_ref):
        for cb in range(CAT_W // LANES):
            r0 = cb * LANES
            acc = jnp.zeros((D_MODEL, LANES), F32)
            for k in range(N_DEV):
                lo, hi = _regrouped(IN_SHARD * k), _regrouped(IN_SHARD * k + IN_SHARD - 1)
                if hi >= r0 and lo < r0 + LANES:
                    acc = acc + _dot(g_ref[k], _selection(k, LANES, r0, False))
            o_ref[:, r0:r0 + LANES] = acc.astype(BF16)

    return pl.pallas_call(
        body, name="regroup_w_in", out_shape=jax.ShapeDtypeStruct((D_MODEL, CAT_W), BF16),
        compiler_params=_params(),
    )(g)


def _ungroup_w_in(g_cat):
    def body(g_ref, o_ref):
        for k in range(N_DEV):
            lo, hi = _regrouped(IN_SHARD * k), _regrouped(IN_SHARD * k + IN_SHARD - 1)
            w0, w1 = lo // LANES * LANES, (hi // LANES + 1) * LANES
            o_ref[k] = _dot(g_ref[:, w0:w1], _selection(k, w1 - w0, w0, True)).astype(BF16)

    return pl.pallas_call(
        body, name="ungroup_w_in", out_shape=jax.ShapeDtypeStruct((N_DEV, D_MODEL, IN_SHARD_PAD), BF16),
        compiler_params=_params(),
    )(g_cat)


SMALL = (("b_ada", 6144), ("g_pre_mix", 1024), ("g_post_mix", 1024), ("g_pre_ffn", 1024), ("g_post_ffn", 1024),
         ("conv_b", 1024), ("mlstm_norm_w", 512), ("b_if", 8), ("attn_sinks", 8))
SMALL_W = 12032
SMALL_PAD = 12288


def _pack_small(vals):
    parts = []
    for n, width in SMALL:
        a = vals[n].reshape(-1)
        parts.append(jnp.pad(a, (0, LANES - width)) if width < LANES else a)
    return jnp.concatenate(parts)


def _unpack_small(vec):
    out, off = {}, 0
    for n, width in SMALL:
        out[n] = vec[off:off + width].reshape(1, width)
        off += max(width, LANES)
    return out


WEIGHT_NAMES = ("w_ada", "b_ada", "g_pre_mix", "g_post_mix", "w_in", "b_if", "conv_w", "conv_b", "attn_sinks",
                "mlstm_norm_w", "w_branch_attn", "w_branch_mlstm", "w_out", "g_pre_ffn", "g_post_ffn",
                "w_ffn_gate", "w_ffn_up", "w_ffn_down")


def kernel(x, c, positions, w_ada, b_ada, g_pre_mix, g_post_mix, w_in, b_if, conv_w, conv_b, attn_sinks, mlstm_norm_w, w_branch_attn, w_branch_mlstm, w_out, g_pre_ffn, g_post_ffn, w_ffn_gate, w_ffn_up, w_ffn_down, loss_target, m_w_ada, m_b_ada, m_g_pre_mix, m_g_post_mix, m_w_in, m_b_if, m_conv_w, m_conv_b, m_attn_sinks, m_mlstm_norm_w, m_w_branch_attn, m_w_branch_mlstm, m_w_out, m_g_pre_ffn, m_g_post_ffn, m_w_ffn_gate, m_w_ffn_up, m_w_ffn_down, v_w_ada, v_b_ada, v_g_pre_mix, v_g_post_mix, v_w_in, v_b_if, v_conv_w, v_conv_b, v_attn_sinks, v_mlstm_norm_w, v_w_branch_attn, v_w_branch_mlstm, v_w_out, v_g_pre_ffn, v_g_post_ffn, v_w_ffn_gate, v_w_ffn_up, v_w_ffn_down):
    given = dict(locals())
    W = {n: given[n][0] for n in WEIGHT_NAMES}
    M = {n: given["m_" + n][0] for n in WEIGHT_NAMES}
    V = {n: given["v_" + n][0] for n in WEIGHT_NAMES}
    me = 4 * lax.axis_index("x") + 2 * lax.axis_index("y") + lax.axis_index("c")

    ff_sh = D_FF // N_DEV
    g_in, g_conv, cg = _all_gather_hbm([jnp.pad(W["w_in"], ((0, 0), (0, IN_SHARD_PAD - IN_SHARD))).astype(BF16),
                                        jnp.pad(W["conv_w"], ((0, 4), (0, 0))), c.reshape(8, D_MODEL // 8)])

    cmat = cg.reshape(N_DEV, D_MODEL)
    ada_w = D_MODEL * 6 // N_DEV
    b_cols = lax.dynamic_slice(W["b_ada"], (me * ada_w,), (ada_w,)).reshape(1, ada_w)
    mod_part = _mod_partial(cmat, W["w_ada"], b_cols)
    mod_recv = _small_exchange(jnp.broadcast_to(mod_part[:, None, :], (N_DEV, 8, ada_w)), False, "scatter_mod")
    mod = mod_recv[:, 0, :].reshape(6, D_MODEL)

    st_b = _exchange_start([W["w_branch_attn"].astype(BF16), W["w_branch_mlstm"].astype(BF16), W["w_out"].astype(BF16),
                            W["w_ffn_gate"].T.astype(BF16), W["w_ffn_up"].T.astype(BF16), W["w_ffn_down"].astype(BF16)],
                           True, mod_recv, "gather_rest_start")
    g_in = _after(g_in, st_b["token"])
    cols = lambda g: g.transpose(1, 0, 2).reshape(g.shape[1], N_DEV * g.shape[2])
    gains = jnp.stack([W["g_pre_mix"], W["g_post_mix"], W["g_pre_ffn"], W["g_post_ffn"]])
    xs, tgt = x[0], loss_target[0]
    t = _tables(mod, gains, cols(g_conv)[0:4], W["conv_b"], W["b_if"], W["mlstm_norm_w"], positions[0])
    w_cat = _regroup_w_in(g_in)
    a = _mixer_fwd(xs, t, W["attn_sinks"], w_cat)
    g_ba, g_bm, g_out, g_gate, g_up, g_down = _exchange_wait(st_b, a["ym"], "gather_rest_wait")
    w_ba, w_bm, w_out = cols(g_ba), cols(g_bm), g_out.reshape(D_MODEL, D_MODEL)
    b = _ffn_part(xs, tgt, t, a, w_ba, w_bm, w_out, g_gate.reshape(D_FF, D_MODEL), g_up.reshape(D_FF, D_MODEL),
                  g_down.reshape(D_FF, D_MODEL))

    st_f = _exchange_start([b["g_w_gate"].reshape(N_DEV, ff_sh, D_MODEL), b["g_w_up"].reshape(N_DEV, ff_sh, D_MODEL),
                            b["g_w_down"].reshape(N_DEV, ff_sh, D_MODEL)], False, b["dx1"], "scatter_ffn_start")
    cm = _mixer_bwd(_after(b["dx1"], st_f["token"]), t, a, b, W["attn_sinks"], w_ba, w_bm, w_out)
    pieces = lambda g, n: g.reshape(g.shape[0], N_DEV, n).transpose(1, 0, 2)
    st_m = _exchange_start([_ungroup_w_in(cm["g_w_cat"]), pieces(cm["g_w_ba"], 128), pieces(cm["g_w_bm"], 128),
                            cm["g_w_out"].reshape(N_DEV, D_MODEL // N_DEV, D_MODEL),
                            jnp.pad(pieces(cm["dcw"][0:4], 128), ((0, 0), (0, 4), (0, 0)))], False, cm["dproj"],
                           "scatter_mixer_start")
    r_gate, r_up, r_down = _exchange_wait(st_f, st_m["token"], "scatter_ffn_wait")
    grad_x, acc_p = _pre_bwd(_after(cm["dproj"], st_m["token"]), xs, b["dx1"], t["vecs"], w_cat)
    small = _small_grads(acc_p, b, cm)
    loss = b["loss"]

    big_out = [{} for _ in range(4)]

    def put(n, res):
        for k in range(4):
            big_out[k][n] = res[k][None]

    put("w_ffn_down", _adamw_sum(r_down, W["w_ffn_down"], M["w_ffn_down"], V["w_ffn_down"], "adamw_w_ffn_down"))
    for n, g in (("w_ffn_gate", _sum8(r_gate, "sum_w_ffn_gate").T), ("w_ffn_up", _sum8(r_up, "sum_w_ffn_up").T)):
        put(n, [g] + list(_adamw_plain(g, W[n], M[n], V[n], "adamw_" + n)))

    part = {"b_ada": small["dmod"], "g_pre_mix": small["dgains"][0], "g_post_mix": small["dgains"][1],
            "g_pre_ffn": small["dgains"][2], "g_post_ffn": small["dgains"][3], "conv_b": small["dconv_b"],
            "mlstm_norm_w": small["dnorm_w"], "b_if": small["db_if"], "attn_sinks": small["dsinks"]}
    rows8 = lambda d: jnp.pad(_pack_small(d), (0, SMALL_PAD - SMALL_W)).reshape(8, SMALL_PAD // 8)
    sg = _small_exchange(rows8(part), True, "gather_small")
    small_out = [_unpack_small(o.reshape(-1)) for o in _adamw_small(sg, rows8(W), rows8(M), rows8(V))]
    dmod_cols = lax.dynamic_slice(sg.reshape(N_DEV, SMALL_PAD), (0, me * ada_w), (N_DEV, ada_w))
    ada_out = _adamw_w_ada(cmat, dmod_cols, W["w_ada"], M["w_ada"], V["w_ada"])

    r_in, r_ba, r_bm, r_out, r_conv = _exchange_wait(st_m, grad_x, "scatter_mixer_wait")
    for n, r in (("w_branch_attn", r_ba), ("w_branch_mlstm", r_bm), ("w_out", r_out)):
        put(n, _adamw_sum(r, W[n], M[n], V[n], "adamw_" + n))
    pad4 = lambda v: jnp.pad(v, ((0, 4), (0, 0)))
    put("conv_w", [o[0:4] for o in _adamw_sum(r_conv, pad4(W["conv_w"]), pad4(M["conv_w"]), pad4(V["conv_w"]),
                                                 "adamw_conv_w")])
    g = _sum8(r_in, "sum_w_in")[:, 0:IN_SHARD]
    put("w_in", [g] + list(_adamw_plain(g, W["w_in"], M["w_in"], V["w_in"], "adamw_w_in")))

    total = lax.psum(loss, ("x", "y", "c"))
    outs = [total, grad_x[None]]
    for k in range(4):
        for n in WEIGHT_NAMES:
            if n == "w_ada":
                outs.append(ada_out[k][None])
            elif n in big_out[k]:
                outs.append(big_out[k][n])
            else:
                outs.append(small_out[k][n])
    return tuple(outs)
```

```python
import functools

import jax
import jax.numpy as jnp
import numpy as np
from jax import lax
from jax.experimental import pallas as pl
from jax.experimental.pallas import tpu as pltpu

F32 = jnp.float32
BF16 = jnp.bfloat16

N_DEV = 8
D_MODEL = 1024
D_FF = 2816
N_Q_HEADS = 8
HEAD_DIM = 64
ATTN_BLOCK = 128
ROPE_THETA = 10000.0
MLSTM_HEADS = 4
MLSTM_HEAD_DIM = 128
MLSTM_CHUNK = 128
NORM_EPS = 1e-6
ADAM_LR = 0.001
ADAM_B1 = 0.9
ADAM_B2 = 0.999
ADAM_EPS = 1e-08
ADAM_WD = 0.01
ADAM_STEP = 10

ROW_TILE = 256
WIDE_TILE = 512
LANES = 128
NEG = -1e30
VMEM_LIMIT = 56 * 1024 * 1024

A_W = 768
M_W = 2048
IF_W = 128
G_W = 2048
CAT_W = A_W + M_W + IF_W + G_W

R_SHIFT_M, R_SCALE_M, R_GATE_M, R_SHIFT_F, R_SCALE_F, R_GATE_F = 0, 1, 2, 3, 4, 5
R_G_PRE_MIX, R_G_POST_MIX, R_G_PRE_FFN, R_G_POST_FFN = 6, 7, 8, 9


def _dot(a, b):
    return jnp.dot(a, b, preferred_element_type=F32)


def _dot_nt(a, b):
    return lax.dot_general(a, b, (((1,), (1,)), ((), ())), preferred_element_type=F32)


def _dot_tn(a, b):
    return lax.dot_general(a, b, (((0,), (0,)), ((), ())), preferred_element_type=F32)


def _recip(x):
    return 1.0 / x


def _sigmoid(x):
    return _recip(1.0 + jnp.exp(-x))


def _colsum(x):
    return jnp.sum(x, axis=0, keepdims=True)


def _rowmean(x):
    return jnp.mean(x, axis=-1, keepdims=True)


def _params(sem=None, vmem=VMEM_LIMIT):
    kw = dict(vmem_limit_bytes=vmem)
    if sem is not None:
        kw["dimension_semantics"] = sem
    return pltpu.CompilerParams(**kw)


def _full(shape):
    nd = len(shape)
    return pl.BlockSpec(shape, lambda *_: (0,) * nd)


def _pre_proj(x, vecs, w_cat):
    S = x.shape[0]
    tm = WIDE_TILE

    def body(x_ref, v_ref, w_ref, h_ref, pa_ref, pm_ref, pif_ref, pg_ref):
        xv = x_ref[...]
        r = lax.rsqrt(_rowmean(xv * xv) + NORM_EPS)
        h = (xv * r * v_ref[R_G_PRE_MIX:R_G_PRE_MIX + 1, :]) * (1.0 + v_ref[R_SCALE_M:R_SCALE_M + 1, :]) \
            + v_ref[R_SHIFT_M:R_SHIFT_M + 1, :]
        hb = h.astype(BF16)
        h_ref[...] = hb
        pa_ref[...] = _dot(hb, w_ref[:, 0:A_W])
        pm_ref[...] = _dot(hb, w_ref[:, A_W:A_W + M_W])
        pif_ref[...] = _dot(hb, w_ref[:, A_W + M_W:A_W + M_W + IF_W])
        pg_ref[...] = _dot(hb, w_ref[:, A_W + M_W + IF_W:CAT_W]).astype(BF16)

    row = lambda w: pl.BlockSpec((tm, w), lambda i: (i, 0))
    return pl.pallas_call(
        body, name="pre_proj", grid=(S // tm,),
        in_specs=[row(D_MODEL), _full(vecs.shape), _full(w_cat.shape)],
        out_specs=[row(D_MODEL), row(A_W), row(M_W), row(IF_W), row(G_W)],
        out_shape=[jax.ShapeDtypeStruct((S, D_MODEL), BF16), jax.ShapeDtypeStruct((S, A_W), F32),
                   jax.ShapeDtypeStruct((S, M_W), F32), jax.ShapeDtypeStruct((S, IF_W), F32),
                   jax.ShapeDtypeStruct((S, G_W), BF16)],
        compiler_params=_params(("parallel",)),
    )(x, vecs, w_cat)


def _mix_fwd(x, ya, ym, pg, vecs, w_ba, w_bm, w_out):
    S = x.shape[0]
    tm = WIDE_TILE

    def body(x_ref, ya_ref, ym_ref, pg_ref, v_ref, wba_ref, wbm_ref, wout_ref,
             x1_ref, merged_ref, mix_ref, pa_ref, pb_ref):
        pa = _dot(ya_ref[...], wba_ref[...])
        pb = _dot(ym_ref[...], wbm_ref[...])
        merged = _sigmoid(pg_ref[:, 0:D_MODEL].astype(F32)) * pa + _sigmoid(pg_ref[:, D_MODEL:G_W].astype(F32)) * pb
        mb = merged.astype(BF16)
        mix = _dot(mb, wout_ref[...])
        r = lax.rsqrt(_rowmean(mix * mix) + NORM_EPS)
        x1_ref[...] = x_ref[...] + v_ref[R_GATE_M:R_GATE_M + 1, :] * (mix * r * v_ref[R_G_POST_MIX:R_G_POST_MIX + 1, :])
        merged_ref[...] = mb
        mix_ref[...] = mix
        pa_ref[...] = pa.astype(BF16)
        pb_ref[...] = pb.astype(BF16)

    row = lambda w: pl.BlockSpec((tm, w), lambda i: (i, 0))
    sd = lambda w, dt: jax.ShapeDtypeStruct((S, w), dt)
    return pl.pallas_call(
        body, name="mix_fwd", grid=(S // tm,),
        in_specs=[row(D_MODEL), row(512), row(512), row(G_W), _full(vecs.shape), _full(w_ba.shape),
                  _full(w_bm.shape), _full(w_out.shape)],
        out_specs=[row(D_MODEL)] * 5,
        out_shape=[sd(D_MODEL, F32), sd(D_MODEL, BF16), sd(D_MODEL, F32), sd(D_MODEL, BF16), sd(D_MODEL, BF16)],
        compiler_params=_params(("parallel",)),
    )(x, ya, ym, pg, vecs, w_ba, w_bm, w_out)


def _ffn_fwd_bwd(x1, tgt, vecs, w_gate, w_up, w_down):
    S = x1.shape[0]
    tm = ROW_TILE

    def body(x1_ref, tgt_ref, v_ref, wg_hbm, wu_hbm, wd_hbm,
             dx1_ref, h2_ref, hid_ref, da_ref, du_ref, dff_ref, acc_ref, loss_ref,
             wg, wu, wd, sem):
        i = pl.program_id(0)

        @pl.when(i == 0)
        def _():
            cps = [pltpu.make_async_copy(wg_hbm, wg, sem.at[0]), pltpu.make_async_copy(wu_hbm, wu, sem.at[1]),
                   pltpu.make_async_copy(wd_hbm, wd, sem.at[2])]
            for cp in cps:
                cp.start()
            for cp in cps:
                cp.wait()
            acc_ref[...] = jnp.zeros_like(acc_ref)
            loss_ref[...] = jnp.zeros_like(loss_ref)

        vrow = lambda r: v_ref[r:r + 1, :]
        x1v = x1_ref[...]
        r3 = lax.rsqrt(_rowmean(x1v * x1v) + NORM_EPS)
        x1hat = x1v * r3
        xn3 = x1hat * vrow(R_G_PRE_FFN)
        h2b = (xn3 * (1.0 + vrow(R_SCALE_F)) + vrow(R_SHIFT_F)).astype(BF16)
        h2_ref[...] = h2b
        a = _dot_nt(h2b, wg[...])
        u = _dot_nt(h2b, wu[...])
        sg = _sigmoid(a)
        sil = a * sg
        hidb = (sil * u).astype(BF16)
        hid_ref[...] = hidb
        ff = _dot(hidb, wd[...])
        r4 = lax.rsqrt(_rowmean(ff * ff) + NORM_EPS)
        ffhat = ff * r4
        n4 = ffhat * vrow(R_G_POST_FFN)
        err = x1v + vrow(R_GATE_F) * n4 - tgt_ref[...]
        loss_ref[...] += jnp.sum(err * err) * (0.5 / D_MODEL)
        dy = err * (1.0 / D_MODEL)
        acc_ref[0:1, :] += _colsum(dy * n4)
        dn4 = dy * vrow(R_GATE_F)
        acc_ref[1:2, :] += _colsum(dn4 * ffhat)
        dffhat = dn4 * vrow(R_G_POST_FFN)
        dffb = (r4 * (dffhat - ffhat * _rowmean(dffhat * ffhat))).astype(BF16)
        dff_ref[...] = dffb
        dhid = _dot_nt(dffb, wd[...])
        dub = (dhid * sil).astype(BF16)
        dab = (dhid * u * (sg * (1.0 + a * (1.0 - sg)))).astype(BF16)
        da_ref[...] = dab
        du_ref[...] = dub
        dh2 = _dot(dab, wg[...]) + _dot(dub, wu[...])
        acc_ref[2:3, :] += _colsum(dh2 * xn3)
        acc_ref[3:4, :] += _colsum(dh2)
        dxn3 = dh2 * (1.0 + vrow(R_SCALE_F))
        acc_ref[4:5, :] += _colsum(dxn3 * x1hat)
        dx1hat = dxn3 * vrow(R_G_PRE_FFN)
        dx1_ref[...] = dy + r3 * (dx1hat - x1hat * _rowmean(dx1hat * x1hat))

    row = lambda w: pl.BlockSpec((tm, w), lambda i: (i, 0))
    sd = lambda w, dt: jax.ShapeDtypeStruct((S, w), dt)
    anyspec = pl.BlockSpec(memory_space=pl.ANY)
    return pl.pallas_call(
        body, name="ffn_fwd_bwd", grid=(S // tm,),
        in_specs=[row(D_MODEL), row(D_MODEL), _full(vecs.shape), anyspec, anyspec, anyspec],
        out_specs=[row(D_MODEL), row(D_MODEL), row(D_FF), row(D_FF), row(D_FF), row(D_MODEL),
                   _full((8, D_MODEL)), _full((8, LANES))],
        out_shape=[sd(D_MODEL, F32), sd(D_MODEL, BF16), sd(D_FF, BF16), sd(D_FF, BF16), sd(D_FF, BF16),
                   sd(D_MODEL, BF16), jax.ShapeDtypeStruct((8, D_MODEL), F32), jax.ShapeDtypeStruct((8, LANES), F32)],
        scratch_shapes=[pltpu.VMEM(w_gate.shape, BF16), pltpu.VMEM(w_up.shape, BF16), pltpu.VMEM(w_down.shape, BF16),
                        pltpu.SemaphoreType.DMA((3,))],
        compiler_params=_params(("arbitrary",)),
    )(x1, tgt, vecs, w_gate, w_up, w_down)


def _mix_bwd(dx1, mix, pa, pb, pg, vecs, w_ba, w_bm, w_out):
    S = dx1.shape[0]
    tm = WIDE_TILE

    def body(dx1_ref, mix_ref, pa_ref, pb_ref, pg_ref, v_ref, wba_ref, wbm_ref, wout_ref,
             dmix_ref, dpa_ref, dpb_ref, dg_ref, dya_ref, dym_ref, acc_ref):
        i = pl.program_id(0)

        @pl.when(i == 0)
        def _():
            acc_ref[...] = jnp.zeros_like(acc_ref)

        vrow = lambda r: v_ref[r:r + 1, :]
        dx1v = dx1_ref[...]
        mix = mix_ref[...]
        r2 = lax.rsqrt(_rowmean(mix * mix) + NORM_EPS)
        mixhat = mix * r2
        acc_ref[0:1, :] += _colsum(dx1v * (mixhat * vrow(R_G_POST_MIX)))
        dn2 = dx1v * vrow(R_GATE_M)
        acc_ref[1:2, :] += _colsum(dn2 * mixhat)
        dmixhat = dn2 * vrow(R_G_POST_MIX)
        dmixb = (r2 * (dmixhat - mixhat * _rowmean(dmixhat * mixhat))).astype(BF16)
        dmix_ref[...] = dmixb
        dmerged = _dot_nt(dmixb, wout_ref[...])
        sa = _sigmoid(pg_ref[:, 0:D_MODEL].astype(F32))
        sm = _sigmoid(pg_ref[:, D_MODEL:G_W].astype(F32))
        dpab = (dmerged * sa).astype(BF16)
        dpbb = (dmerged * sm).astype(BF16)
        dpa_ref[...] = dpab
        dpb_ref[...] = dpbb
        dg_ref[:, 0:D_MODEL] = (dmerged * pa_ref[...].astype(F32) * (sa * (1.0 - sa))).astype(BF16)
        dg_ref[:, D_MODEL:G_W] = (dmerged * pb_ref[...].astype(F32) * (sm * (1.0 - sm))).astype(BF16)
        dya_ref[...] = _dot_nt(dpab, wba_ref[...])
        dym_ref[...] = _dot_nt(dpbb, wbm_ref[...])

    row = lambda w: pl.BlockSpec((tm, w), lambda i: (i, 0))
    sd = lambda w, dt: jax.ShapeDtypeStruct((S, w), dt)
    return pl.pallas_call(
        body, name="mix_bwd", grid=(S // tm,),
        in_specs=[row(D_MODEL), row(D_MODEL), row(D_MODEL), row(D_MODEL), row(G_W), _full(vecs.shape),
                  _full(w_ba.shape), _full(w_bm.shape), _full(w_out.shape)],
        out_specs=[row(D_MODEL), row(D_MODEL), row(D_MODEL), row(G_W), row(512), row(512), _full((8, D_MODEL))],
        out_shape=[sd(D_MODEL, BF16), sd(D_MODEL, BF16), sd(D_MODEL, BF16), sd(G_W, BF16), sd(512, F32), sd(512, F32),
                   jax.ShapeDtypeStruct((8, D_MODEL), F32)],
        compiler_params=_params(("arbitrary",)),
    )(dx1, mix, pa, pb, pg, vecs, w_ba, w_bm, w_out)


def _pre_bwd(dproj, x, dx1, vecs, w_cat):
    S = x.shape[0]
    tm = WIDE_TILE

    def body(dp_ref, x_ref, dx1_ref, v_ref, w_ref, dx_ref, acc_ref):
        i = pl.program_id(0)

        @pl.when(i == 0)
        def _():
            acc_ref[...] = jnp.zeros_like(acc_ref)

        vrow = lambda r: v_ref[r:r + 1, :]
        dh = _dot_nt(dp_ref[...], w_ref[...])
        xv = x_ref[...]
        r1 = lax.rsqrt(_rowmean(xv * xv) + NORM_EPS)
        xhat = xv * r1
        acc_ref[0:1, :] += _colsum(dh * (xhat * vrow(R_G_PRE_MIX)))
        acc_ref[1:2, :] += _colsum(dh)
        dxn = dh * (1.0 + vrow(R_SCALE_M))
        acc_ref[2:3, :] += _colsum(dxn * xhat)
        dxhat = dxn * vrow(R_G_PRE_MIX)
        dx_ref[...] = dx1_ref[...] + r1 * (dxhat - xhat * _rowmean(dxhat * xhat))

    row = lambda w: pl.BlockSpec((tm, w), lambda i: (i, 0))
    return pl.pallas_call(
        body, name="pre_bwd", grid=(S // tm,),
        in_specs=[row(CAT_W), row(D_MODEL), row(D_MODEL), _full(vecs.shape), _full(w_cat.shape)],
        out_specs=[row(D_MODEL), _full((8, D_MODEL))],
        out_shape=[jax.ShapeDtypeStruct((S, D_MODEL), F32), jax.ShapeDtypeStruct((8, D_MODEL), F32)],
        compiler_params=_params(("arbitrary",)),
    )(dproj, x, dx1, vecs, w_cat)


def _matmul_tn(a, b, tn, name, ts=1024):
    S, K = a.shape
    N = b.shape[1]
    n_s = S // ts

    def body(a_ref, b_ref, o_ref, acc_ref):
        s = pl.program_id(1)

        @pl.when(s == 0)
        def _():
            acc_ref[...] = jnp.zeros_like(acc_ref)

        acc_ref[...] += _dot_tn(a_ref[...], b_ref[...])

        @pl.when(s == n_s - 1)
        def _():
            o_ref[...] = acc_ref[...].astype(BF16)

    return pl.pallas_call(
        body, name=name, grid=(N // tn, n_s),
        in_specs=[pl.BlockSpec((ts, K), lambda j, s: (s, 0)), pl.BlockSpec((ts, tn), lambda j, s: (s, j))],
        out_specs=pl.BlockSpec((K, tn), lambda j, s: (0, j)),
        out_shape=jax.ShapeDtypeStruct((K, N), BF16),
        scratch_shapes=[pltpu.VMEM((K, tn), F32)],
        compiler_params=_params(("parallel", "arbitrary")),
    )(a, b)


def _rope_swap(t):
    lane = lax.broadcasted_iota(jnp.int32, t.shape, 1)
    first = (lane & (HEAD_DIM - 1)) < (HEAD_DIM // 2)
    return jnp.where(first, pltpu.roll(t, LANES - HEAD_DIM // 2, 1), pltpu.roll(t, HEAD_DIM // 2, 1))


def _rope(t, cos, sin_signed):
    return t * cos + _rope_swap(t) * sin_signed


def _rope_t(d, cos, sin_signed):
    return d * cos + _rope_swap(d * sin_signed)


def _to_kv_lanes(chunk, p, h):
    lane = lax.broadcasted_iota(jnp.int32, chunk.shape, 1)
    src = chunk if p == h else pltpu.roll(chunk, HEAD_DIM, 1)
    return jnp.where((lane >> 6) == h, src, jnp.zeros_like(src))


def _from_kv_lanes(o_a, o_b, h):
    lane = lax.broadcasted_iota(jnp.int32, o_a.shape, 1)
    a = o_a if h == 0 else pltpu.roll(o_a, HEAD_DIM, 1)
    b = o_b if h == 1 else pltpu.roll(o_b, HEAD_DIM, 1)
    return jnp.where(lane < HEAD_DIM, a, b)


def _band_mask(n):
    blk = ATTN_BLOCK
    qi = lax.broadcasted_iota(jnp.int32, (4 * blk, 2 * blk), 0) & (blk - 1)
    kj = lax.broadcasted_iota(jnp.int32, (4 * blk, 2 * blk), 1)
    return (kj > qi) & (kj <= qi + blk) & ((n > 0) | (kj >= blk))


def _stack_heads(chunks, h, dtype):
    parts = []
    for g in range(4):
        j = 4 * h + g
        parts.append(_to_kv_lanes(chunks[j // 2], j % 2, h))
    return jnp.concatenate(parts, axis=0).astype(dtype)


def _fused_call(parts, name, n_steps):
    counts = [(len(p["in_specs"]), len(p["out_specs"]), len(p["scratch"])) for p in parts]
    n_in, n_out = sum(c[0] for c in counts), sum(c[1] for c in counts)

    def kernel_fn(*refs):
        i = pl.program_id(0)
        groups, a, b, c = [], 0, n_in, n_in + n_out
        for ci, co, cs in counts:
            groups.append(refs[a:a + ci] + refs[b:b + co] + refs[c:c + cs])
            a, b, c = a + ci, b + co, c + cs
        for p, g in zip(parts, groups):
            p["init"](i, *g)
        for p, g in zip(parts, groups):
            p["body"](i, *g)

    flat = lambda key: [v for p in parts for v in p[key]]
    res = pl.pallas_call(
        kernel_fn, name=name, grid=(n_steps,), in_specs=flat("in_specs"), out_specs=flat("out_specs"),
        out_shape=flat("out_shape"), scratch_shapes=flat("scratch"), compiler_params=_params(("arbitrary",)),
    )(*flat("operands"))
    out, pos = [], 0
    for _, co, _ in counts:
        out.append(res[pos:pos + co])
        pos += co
    return out


def _attn_fwd_part(pa, cos, sin, sinks):
    S = pa.shape[0]
    blk = ATTN_BLOCK
    nb = S // blk

    def body(n, sink_ref, cur_ref, prev_ref, cos_ref, sin_ref, cosp_ref, sinp_ref,
             ya_ref, qr_ref, kr_ref, vb_ref, lse_ref):
        cos_c, sin_c = cos_ref[...], sin_ref[...]
        qch = [_rope(cur_ref[:, c * LANES:(c + 1) * LANES], cos_c, sin_c) * (HEAD_DIM ** -0.5) for c in range(4)]
        for c in range(4):
            qr_ref[:, c * LANES:(c + 1) * LANES] = qch[c].astype(BF16)
        k_cur = _rope(cur_ref[:, 512:640], cos_c, sin_c).astype(BF16)
        k_prev = _rope(prev_ref[:, 0:LANES], cosp_ref[...], sinp_ref[...]).astype(BF16)
        v_cur = cur_ref[:, 640:768].astype(BF16)
        v_prev = prev_ref[:, LANES:2 * LANES].astype(BF16)
        kr_ref[...] = k_cur
        vb_ref[...] = v_cur
        K = jnp.concatenate([k_prev, k_cur], axis=0)
        V = jnp.concatenate([v_prev, v_cur], axis=0)
        mask = _band_mask(n)
        rowg = lax.broadcasted_iota(jnp.int32, (4 * blk, 1), 0) >> 7
        lane = lax.broadcasted_iota(jnp.int32, (blk, LANES), 1)
        lse_tile = jnp.zeros((blk, LANES), F32)
        outs = []
        for h in range(2):
            qs = _stack_heads(qch, h, BF16)
            s = jnp.where(mask, _dot_nt(qs, K), NEG)
            sink = jnp.zeros((4 * blk, 1), F32)
            for g in range(4):
                sink = jnp.where(rowg == g, sink_ref[4 * h + g], sink)
            m = jnp.maximum(jnp.max(s, axis=1, keepdims=True), sink)
            p = jnp.exp(s - m)
            den = jnp.sum(p, axis=1, keepdims=True) + jnp.exp(sink - m)
            o = _dot((p * _recip(den)).astype(BF16), V)
            lse = m + jnp.log(den)
            for g in range(4):
                outs.append(o[g * blk:(g + 1) * blk, :])
                lse_tile = jnp.where(lane == 4 * h + g, lse[g * blk:(g + 1) * blk, :], lse_tile)
        for c in range(4):
            ya_ref[:, c * LANES:(c + 1) * LANES] = _from_kv_lanes(outs[2 * c], outs[2 * c + 1], c // 2).astype(BF16)
        lse_ref[...] = lse_tile

    prev = lambda n: jnp.maximum(n - 1, 0)
    sd = lambda w, dt: jax.ShapeDtypeStruct((S, w), dt)
    return dict(
        init=lambda n, *refs: None, body=body, scratch=[], operands=[sinks, pa, pa, cos, sin, cos, sin],
        in_specs=[pl.BlockSpec(memory_space=pltpu.SMEM),
                  pl.BlockSpec((blk, A_W), lambda n: (n, 0)),
                  pl.BlockSpec((blk, 256), lambda n: (prev(n), 2)),
                  pl.BlockSpec((blk, LANES), lambda n: (n, 0)), pl.BlockSpec((blk, LANES), lambda n: (n, 0)),
                  pl.BlockSpec((blk, LANES), lambda n: (prev(n), 0)), pl.BlockSpec((blk, LANES), lambda n: (prev(n), 0))],
        out_specs=[pl.BlockSpec((blk, 512), lambda n: (n, 0)), pl.BlockSpec((blk, 512), lambda n: (n, 0)),
                   pl.BlockSpec((blk, LANES), lambda n: (n, 0)), pl.BlockSpec((blk, LANES), lambda n: (n, 0)),
                   pl.BlockSpec((blk, LANES), lambda n: (n, 0))],
        out_shape=[sd(512, BF16), sd(512, BF16), sd(LANES, BF16), sd(LANES, BF16), sd(LANES, F32)])


def _attn_bwd_part(dya, qr, kr, vb, lse, cos, sin, sinks):
    S = dya.shape[0]
    blk = ATTN_BLOCK
    nb = S // blk

    def init(n, sink_ref, dya_ref, qr_ref, kc_ref, kp_ref, vc_ref, vp_ref, lse_ref, cos_ref, sin_ref, cosp_ref, sinp_ref,
             dq_ref, dkv_ref, last_ref, dsink_ref, ck, cv):
        @pl.when(n == 0)
        def _():
            ck[...] = jnp.zeros_like(ck)
            cv[...] = jnp.zeros_like(cv)
            dsink_ref[...] = jnp.zeros_like(dsink_ref)

    def body(n, sink_ref, dya_ref, qr_ref, kc_ref, kp_ref, vc_ref, vp_ref, lse_ref, cos_ref, sin_ref, cosp_ref, sinp_ref,
             dq_ref, dkv_ref, last_ref, dsink_ref, ck, cv):
        K = jnp.concatenate([kp_ref[...], kc_ref[...]], axis=0)
        V = jnp.concatenate([vp_ref[...], vc_ref[...]], axis=0)
        qch = [qr_ref[:, c * LANES:(c + 1) * LANES] for c in range(4)]
        dch = [dya_ref[:, c * LANES:(c + 1) * LANES] for c in range(4)]
        lse_tile = lse_ref[...]
        mask = _band_mask(n)
        rowg = lax.broadcasted_iota(jnp.int32, (4 * blk, 1), 0) >> 7
        lane8 = lax.broadcasted_iota(jnp.int32, (8, LANES), 1)
        dk_acc = jnp.zeros((2 * blk, LANES), F32)
        dv_acc = jnp.zeros((2 * blk, LANES), F32)
        dsink = jnp.zeros((8, LANES), F32)
        dqs = []
        for h in range(2):
            qs = _stack_heads(qch, h, BF16)
            dos = _stack_heads(dch, h, BF16)
            lse_col = jnp.concatenate([lse_tile[:, 4 * h + g:4 * h + g + 1] for g in range(4)], axis=0)
            p = jnp.where(mask, jnp.exp(_dot_nt(qs, K) - lse_col), 0.0)
            dp = _dot_nt(dos, V)
            delta = jnp.sum(p * dp, axis=1, keepdims=True)
            dsb = (p * (dp - delta)).astype(BF16)
            dq = _dot(dsb, K)
            dk_acc = dk_acc + _dot_tn(dsb, qs)
            dv_acc = dv_acc + _dot_tn(p.astype(BF16), dos)
            sink = jnp.zeros((4 * blk, 1), F32)
            for g in range(4):
                sink = jnp.where(rowg == g, sink_ref[4 * h + g], sink)
            ps_delta = jnp.exp(sink - lse_col) * delta
            for g in range(4):
                dqs.append(dq[g * blk:(g + 1) * blk, :])
                dsink = jnp.where(lane8 == 4 * h + g, dsink - jnp.sum(ps_delta[g * blk:(g + 1) * blk, :]), dsink)
        dsink_ref[...] += dsink
        cos_c, sin_c = cos_ref[...], sin_ref[...]
        for c in range(4):
            dqc = _from_kv_lanes(dqs[2 * c], dqs[2 * c + 1], c // 2) * (HEAD_DIM ** -0.5)
            dq_ref[:, c * LANES:(c + 1) * LANES] = _rope_t(dqc, cos_c, sin_c).astype(BF16)
        dkv_ref[:, 0:LANES] = _rope_t(dk_acc[0:blk, :] + ck[...], cosp_ref[...], sinp_ref[...]).astype(BF16)
        dkv_ref[:, LANES:2 * LANES] = (dv_acc[0:blk, :] + cv[...]).astype(BF16)
        ck[...] = dk_acc[blk:2 * blk, :]
        cv[...] = dv_acc[blk:2 * blk, :]
        last_ref[:, 0:LANES] = _rope_t(dk_acc[blk:2 * blk, :], cos_c, sin_c).astype(BF16)
        last_ref[:, LANES:2 * LANES] = dv_acc[blk:2 * blk, :].astype(BF16)

    prev = lambda n: jnp.maximum(n - 1, 0)
    same = lambda n: n
    bs = lambda w, f: pl.BlockSpec((blk, w), lambda n: (f(n), 0))
    return dict(
        init=init, body=body, operands=[sinks, dya, qr, kr, kr, vb, vb, lse, cos, sin, cos, sin],
        in_specs=[pl.BlockSpec(memory_space=pltpu.SMEM),
                  bs(512, same), bs(512, same), bs(LANES, same), bs(LANES, prev), bs(LANES, same), bs(LANES, prev),
                  bs(LANES, same), bs(LANES, same), bs(LANES, same), bs(LANES, prev), bs(LANES, prev)],
        out_specs=[bs(512, same), bs(256, prev), _full((blk, 256)), _full((8, LANES))],
        out_shape=[jax.ShapeDtypeStruct((S, 512), BF16), jax.ShapeDtypeStruct((S, 256), BF16),
                   jax.ShapeDtypeStruct((blk, 256), BF16), jax.ShapeDtypeStruct((8, LANES), F32)],
        scratch=[pltpu.VMEM((blk, LANES), F32), pltpu.VMEM((blk, LANES), F32)])


def _split3(x):
    hi = x.astype(BF16)
    r1 = x - hi.astype(F32)
    mid = r1.astype(BF16)
    lo = (r1 - mid.astype(F32)).astype(BF16)
    return hi, mid, lo


def _tri_matmul(tri_b, x):
    hi, mid, lo = _split3(x)
    return _dot(tri_b, hi) + _dot(tri_b, mid) + _dot(tri_b, lo)


def _log_sigmoid(x):
    return jnp.minimum(x, 0.0) - jnp.log(1.0 + jnp.exp(-jnp.abs(x)))


def _shift_rows(cur, seam, k, down):
    L = cur.shape[0]
    row8 = lax.broadcasted_iota(jnp.int32, seam.shape, 0)
    if down:
        mixed = jnp.concatenate([cur[:L - 8], jnp.where(row8 >= 8 - k, seam, cur[L - 8:])], axis=0)
        return pltpu.roll(mixed, k, 0)
    mixed = jnp.concatenate([jnp.where(row8 < k, seam, cur[:8]), cur[8:]], axis=0)
    return pltpu.roll(mixed, L - k, 0)


def _conv_fwd(cur, tail, cw_ref):
    z = cw_ref[4:5, :]
    for k in range(3, 0, -1):
        z = z + _shift_rows(cur, tail, k, True) * cw_ref[3 - k:4 - k, :]
    return z + cur * cw_ref[3:4, :]


def _stack(f):
    return jnp.concatenate([f(h) for h in range(MLSTM_HEADS)], axis=0)


def _head(x, h):
    L = x.shape[0] // MLSTM_HEADS
    return x[h * L:(h + 1) * L]


def _mlstm_heads_fwd(qk, cur_ref, gt, b_all, c_prev, nmv, tri, eye):
    L = qk.shape[0]
    HD = MLSTM_HEAD_DIM
    W4 = MLSTM_HEADS * HD
    col2row = lambda x: jnp.sum(jnp.where(eye, x, 0.0), axis=0, keepdims=True)
    b_col = _stack(lambda h: b_all[:, 4 + h:5 + h])
    i_col = _stack(lambda h: gt[:, h:h + 1])
    b_row = _stack(lambda h: jnp.broadcast_to(col2row(b_all[:, 4 + h:5 + h]), (L, L)))
    i_row = _stack(lambda h: jnp.broadcast_to(col2row(gt[:, h:h + 1]), (L, L)))
    bl = _stack(lambda h: jnp.broadcast_to(b_all[L - 1:L, 4 + h:5 + h], (L, 1)))
    m_prev = _stack(lambda h: jnp.broadcast_to(nmv[4 + h:5 + h, 0:1], (L, 1)))
    n_prev = _stack(lambda h: jnp.broadcast_to(nmv[h:h + 1, :], (L, HD)))
    tri4 = jnp.concatenate([tri] * MLSTM_HEADS, axis=0)
    Dm = jnp.where(tri4, b_col - b_row + i_row, NEG)
    inter = b_col + m_prev
    m_t = jnp.maximum(inter, jnp.max(Dm, axis=1, keepdims=True))
    W = jnp.exp(Dm - m_t)
    e_t = jnp.exp(inter - m_t)
    q = _stack(lambda h: qk[:, h * HD:(h + 1) * HD])
    k = _stack(lambda h: qk[:, W4 + h * HD:W4 + (h + 1) * HD]) * (HD ** -0.5)
    v = _stack(lambda h: cur_ref[:, 2 * W4 + h * HD:2 * W4 + (h + 1) * HD])
    qb, kb, vb = q.astype(BF16), k.astype(BF16), v.astype(BF16)
    Sc = _stack(lambda h: _dot_nt(_head(qb, h), _head(kb, h))) * W
    Scb = Sc.astype(BF16)
    cb = [c.astype(BF16) for c in c_prev]
    P1 = _stack(lambda h: _dot(_head(qb, h), cb[h]))
    num = _stack(lambda h: _dot(_head(Scb, h), _head(vb, h))) + e_t * P1
    qn = jnp.sum(q * n_prev, axis=1, keepdims=True)
    den = jnp.sum(Sc, axis=1, keepdims=True) + e_t * qn
    floor = jnp.exp(-m_t)
    inv_g = _recip(jnp.maximum(jnp.abs(den), floor))
    hv = num * inv_g
    a_col = bl - b_col + i_col
    a_max = _stack(lambda h: jnp.broadcast_to(jnp.max(_head(a_col, h), axis=0, keepdims=True), (L, 1)))
    m_new = jnp.maximum(bl + m_prev, a_max)
    dec = jnp.exp(bl + m_prev - m_new)
    u_col = jnp.exp(a_col - m_new)
    return dict(W=W, e_t=e_t, q=q, k=k, v=v, qb=qb, kb=kb, vb=vb, cb=cb, Sc=Sc, Scb=Scb, P1=P1, qn=qn, den=den,
                floor=floor, inv_g=inv_g, hv=hv, n_prev=n_prev, m_new=m_new, dec=dec, u_col=u_col)


def _mlstm_fwd_part(pm, pif, cw, sv):
    S = pm.shape[0]
    L = MLSTM_CHUNK
    nc = S // L
    HD = MLSTM_HEAD_DIM
    W4 = MLSTM_HEADS * HD

    def init(c, cur_ref, pif_ref, cw_ref, sv_ref, ym_ref, z_ref, cst_ref, nst_ref, C, nm, tail):
        @pl.when(c == 0)
        def _():
            C[...] = jnp.zeros_like(C)
            nm[...] = jnp.zeros_like(nm)
            tail[...] = jnp.zeros_like(tail)

    def body(c, cur_ref, pif_ref, cw_ref, sv_ref, ym_ref, z_ref, cst_ref, nst_ref, C, nm, tail):
        z = _conv_fwd(cur_ref[:, 0:2 * W4], tail[...], cw_ref)
        tail[...] = cur_ref[L - 8:L, 0:2 * W4]
        z_ref[...] = z
        qk = z * _sigmoid(z)
        gt = pif_ref[...] + sv_ref[1:2, 0:LANES]
        r_i = lax.broadcasted_iota(jnp.int32, (L, L), 0)
        c_i = lax.broadcasted_iota(jnp.int32, (L, L), 1)
        tri = c_i <= r_i
        eye = c_i == r_i
        b_all = _tri_matmul(tri.astype(BF16), _log_sigmoid(gt))
        nmv = nm[...]
        nst_ref[0] = nmv
        c_prev = [C[h] for h in range(MLSTM_HEADS)]
        f = _mlstm_heads_fwd(qk, cur_ref, gt, b_all, c_prev, nmv, tri, eye)
        hv = f["hv"]
        xc = hv - _rowmean(hv)
        hhat = xc * lax.rsqrt(_rowmean(xc * xc) + NORM_EPS)
        so = _sigmoid(_stack(lambda h: cur_ref[:, 3 * W4 + h * HD:3 * W4 + (h + 1) * HD]))
        wn = _stack(lambda h: jnp.broadcast_to(sv_ref[0:1, h * HD:(h + 1) * HD], (L, HD)))
        y = (so * hhat * wn).astype(BF16)
        kw = f["k"] * f["u_col"]
        kwb = kw.astype(BF16)
        n_new, m_new = [], []
        for h in range(MLSTM_HEADS):
            cst_ref[0, h] = c_prev[h]
            ym_ref[:, h * HD:(h + 1) * HD] = _head(y, h)
            dec = f["dec"][h * L:h * L + 1, :]
            C[h] = dec * c_prev[h] + _dot_tn(_head(kwb, h), _head(f["vb"], h))
            n_new.append(dec * nmv[h:h + 1, :] + _colsum(_head(kw, h)))
            m_new.append(jnp.broadcast_to(f["m_new"][h * L:h * L + 1, :], (1, LANES)))
        nm[...] = jnp.concatenate(n_new + m_new, axis=0)

    return dict(
        init=init, body=body, operands=[pm, pif, cw, sv],
        in_specs=[pl.BlockSpec((L, M_W), lambda c: (c, 0)),
                  pl.BlockSpec((L, IF_W), lambda c: (c, 0)), _full(cw.shape), _full(sv.shape)],
        out_specs=[pl.BlockSpec((L, W4), lambda c: (c, 0)), pl.BlockSpec((L, 2 * W4), lambda c: (c, 0)),
                   pl.BlockSpec((1, MLSTM_HEADS, HD, HD), lambda c: (c, 0, 0, 0)),
                   pl.BlockSpec((1, 8, LANES), lambda c: (c, 0, 0))],
        out_shape=[jax.ShapeDtypeStruct((S, W4), BF16), jax.ShapeDtypeStruct((S, 2 * W4), F32),
                   jax.ShapeDtypeStruct((nc, MLSTM_HEADS, HD, HD), F32), jax.ShapeDtypeStruct((nc, 8, LANES), F32)],
        scratch=[pltpu.VMEM((MLSTM_HEADS, HD, HD), F32), pltpu.VMEM((8, LANES), F32), pltpu.VMEM((8, 2 * W4), F32)])


def _mlstm_bwd_part(pm, zc, pif, cw, sv, dym, cst, nst):
    S = pm.shape[0]
    L = MLSTM_CHUNK
    nc = S // L
    HD = MLSTM_HEAD_DIM
    W4 = MLSTM_HEADS * HD

    def init(r, cur_ref, z_ref, pif_ref, cw_ref, sv_ref, dym_ref, cst_ref, nst_ref,
             dm_ref, dif_ref, dcw_ref, dsv_ref, dC, dn, dz_next, dqk):
        @pl.when(r == 0)
        def _():
            dC[...] = jnp.zeros_like(dC)
            dn[...] = jnp.zeros_like(dn)
            dz_next[...] = jnp.zeros_like(dz_next)
            dcw_ref[...] = jnp.zeros_like(dcw_ref)
            dsv_ref[...] = jnp.zeros_like(dsv_ref)

    def body(r, cur_ref, z_ref, pif_ref, cw_ref, sv_ref, dym_ref, cst_ref, nst_ref,
             dm_ref, dif_ref, dcw_ref, dsv_ref, dC, dn, dz_next, dqk):
        z = z_ref[...]
        sgz = _sigmoid(z)
        qk = z * sgz
        gt = pif_ref[...] + sv_ref[1:2, 0:LANES]
        r_i = lax.broadcasted_iota(jnp.int32, (L, L), 0)
        c_i = lax.broadcasted_iota(jnp.int32, (L, L), 1)
        tri = c_i <= r_i
        eye = c_i == r_i
        b_all = _tri_matmul(tri.astype(BF16), _log_sigmoid(gt))
        lane = lax.broadcasted_iota(jnp.int32, (L, LANES), 1)
        rowl = lax.broadcasted_iota(jnp.int32, (L, 1), 0)
        nmv = nst_ref[0]
        heads = range(MLSTM_HEADS)
        c_prev = [cst_ref[0, h] for h in heads]
        f = _mlstm_heads_fwd(qk, cur_ref, gt, b_all, c_prev, nmv, tri, eye)
        hv, inv_g, den, e_t, u_col, n_prev = f["hv"], f["inv_g"], f["den"], f["e_t"], f["u_col"], f["n_prev"]
        q, k, v, qb, kb, vb, Sc, Scb, W = f["q"], f["k"], f["v"], f["qb"], f["kb"], f["vb"], f["Sc"], f["Scb"], f["W"]
        xc = hv - _rowmean(hv)
        rstd = lax.rsqrt(_rowmean(xc * xc) + NORM_EPS)
        hhat = xc * rstd
        wn = _stack(lambda h: jnp.broadcast_to(sv_ref[0:1, h * HD:(h + 1) * HD], (L, HD)))
        so = _sigmoid(_stack(lambda h: cur_ref[:, 3 * W4 + h * HD:3 * W4 + (h + 1) * HD]))
        dy = _stack(lambda h: dym_ref[:, h * HD:(h + 1) * HD])
        d_o = (dy * hhat * wn * (so * (1.0 - so))).astype(BF16)
        dln = dy * so
        dwn = dln * hhat
        dhhat = dln * wn
        dh = rstd * (dhhat - _rowmean(dhhat) - hhat * _rowmean(dhhat * hhat))
        dnum = dh * inv_g
        active = jnp.abs(den) > f["floor"]
        dden = jnp.where(active, -jnp.sum(dh * hv, axis=1, keepdims=True) * inv_g * jnp.where(den >= 0.0, 1.0, -1.0), 0.0)
        dnumb = dnum.astype(BF16)
        dSc = _stack(lambda h: _dot_nt(_head(dnumb, h), _head(vb, h))) + dden
        dA = (dSc * W).astype(BF16)
        G = dSc * Sc
        Gr = jnp.sum(G, axis=1, keepdims=True)
        Gc = _stack(lambda h: jnp.sum(jnp.where(eye, _colsum(_head(G, h)), 0.0), axis=1, keepdims=True))
        dCn = [dC[h] for h in heads]
        dCnb = [d.astype(BF16) for d in dCn]
        dnv = dn[...]
        dn_new = _stack(lambda h: jnp.broadcast_to(dnv[h:h + 1, :], (L, HD)))
        kdC = _stack(lambda h: _dot(_head(kb, h), dCnb[h]))
        vdC = _stack(lambda h: _dot_nt(_head(vb, h), dCnb[h]))
        dv = (_stack(lambda h: _dot_tn(_head(Scb, h), _head(dnumb, h))) + u_col * kdC).astype(BF16)
        dq = _stack(lambda h: _dot(_head(dA, h), _head(kb, h))) \
            + e_t * _stack(lambda h: _dot_nt(_head(dnumb, h), f["cb"][h])) + (e_t * dden) * n_prev
        dk = (_stack(lambda h: _dot_tn(_head(dA, h), _head(qb, h))) + u_col * (vdC + dn_new)) * (HD ** -0.5)
        E = (jnp.sum(f["P1"] * dnum, axis=1, keepdims=True) + dden * f["qn"]) * e_t
        U = (jnp.sum(kdC * v, axis=1, keepdims=True) + jnp.sum(k * dn_new, axis=1, keepdims=True)) * u_col
        qe = (q * e_t).astype(BF16)
        qd = (e_t * dden) * q
        di = Gc + U
        db = Gr + E - Gc - U
        di_tile = jnp.zeros((L, LANES), F32)
        db_tile = jnp.zeros((L, LANES), F32)
        dn_rows = []
        for h in heads:
            dec = f["dec"][h * L:h * L + 1, :]
            ddec = jnp.sum(dCn[h] * c_prev[h]) + jnp.sum(dnv[h:h + 1, :] * nmv[h:h + 1, :])
            dbl = ddec * dec + jnp.sum(_head(U, h), axis=0, keepdims=True)
            di_tile = jnp.where(lane == h, _head(di, h), di_tile)
            db_tile = jnp.where(lane == 4 + h, _head(db, h) + jnp.where(rowl == L - 1, dbl, 0.0), db_tile)
            dC[h] = dec * dCn[h] + _dot_tn(_head(qe, h), _head(dnumb, h))
            dn_rows.append(dec * dnv[h:h + 1, :] + _colsum(_head(qd, h)))
            dsv_ref[0:1, h * HD:(h + 1) * HD] += _colsum(_head(dwn, h))
            dqk[:, h * HD:(h + 1) * HD] = _head(dq, h)
            dqk[:, W4 + h * HD:W4 + (h + 1) * HD] = _head(dk, h)
            dm_ref[:, 2 * W4 + h * HD:2 * W4 + (h + 1) * HD] = _head(dv, h)
            dm_ref[:, 3 * W4 + h * HD:3 * W4 + (h + 1) * HD] = _head(d_o, h)
        dn[...] = jnp.concatenate(dn_rows + [jnp.zeros((8 - MLSTM_HEADS, LANES), F32)], axis=0)
        dlf = _tri_matmul((r_i <= c_i).astype(BF16), db_tile)
        dif = jnp.where(lane < 4, di_tile, jnp.where(lane < 8, dlf * (1.0 - _sigmoid(gt)), 0.0))
        dif_ref[...] = dif.astype(BF16)
        dsv_ref[1:2, 0:LANES] += _colsum(dif)
        dz = dqk[...] * (sgz * (1.0 + z * (1.0 - sgz)))
        dcw_ref[4:5, :] += _colsum(dz)
        u = cur_ref[:, 0:2 * W4]
        du_in = dz * cw_ref[3:4, :]
        dcw_ref[3:4, :] += _colsum(dz * u)
        for k in range(1, 4):
            up = _shift_rows(dz, dz_next[...], k, False)
            dcw_ref[3 - k:4 - k, :] += _colsum(up * u)
            du_in = du_in + up * cw_ref[3 - k:4 - k, :]
        dz_next[...] = dz[0:8, :]
        dm_ref[:, 0:2 * W4] = du_in.astype(BF16)

    cidx = lambda r: nc - 1 - r
    return dict(
        init=init, body=body, operands=[pm, zc, pif, cw, sv, dym, cst, nst],
        in_specs=[pl.BlockSpec((L, M_W), lambda r: (cidx(r), 0)), pl.BlockSpec((L, 2 * W4), lambda r: (cidx(r), 0)),
                  pl.BlockSpec((L, IF_W), lambda r: (cidx(r), 0)), _full(cw.shape), _full(sv.shape),
                  pl.BlockSpec((L, W4), lambda r: (cidx(r), 0)),
                  pl.BlockSpec((1, MLSTM_HEADS, HD, HD), lambda r: (cidx(r), 0, 0, 0)),
                  pl.BlockSpec((1, 8, LANES), lambda r: (cidx(r), 0, 0))],
        out_specs=[pl.BlockSpec((L, M_W), lambda r: (cidx(r), 0)), pl.BlockSpec((L, IF_W), lambda r: (cidx(r), 0)),
                   _full((8, 2 * W4)), _full((8, W4))],
        out_shape=[jax.ShapeDtypeStruct((S, M_W), BF16), jax.ShapeDtypeStruct((S, IF_W), BF16),
                   jax.ShapeDtypeStruct((8, 2 * W4), F32), jax.ShapeDtypeStruct((8, W4), F32)],
        scratch=[pltpu.VMEM((MLSTM_HEADS, HD, HD), F32), pltpu.VMEM((8, LANES), F32),
                 pltpu.VMEM((8, 2 * W4), F32), pltpu.VMEM((L, 2 * W4), F32)])


def _rope_tables(positions):
    half = HEAD_DIM // 2
    inv_freq = ROPE_THETA ** (-2.0 * jnp.arange(half, dtype=F32) / HEAD_DIM)
    ang = positions.astype(F32)[:, None] * inv_freq
    cos = jnp.tile(jnp.cos(ang), (1, LANES // half))
    sign = jnp.tile(jnp.concatenate([-jnp.ones((half,), F32), jnp.ones((half,), F32)]), LANES // HEAD_DIM)
    sin = jnp.tile(jnp.sin(ang), (1, LANES // half)) * sign
    return cos, sin


def _local_step(x, tgt, positions, mod, gains, w_cat, w_ba, w_bm, w_out, w_gate, w_up, w_down,
                conv_w, conv_b, b_if, sinks, norm_w):
    t = _tables(mod, gains, conv_w, conv_b, b_if, norm_w, positions)
    a = _mixer_fwd(x, t, sinks, w_cat)
    b = _ffn_part(x, tgt, t, a, w_ba, w_bm, w_out, w_gate, w_up, w_down)
    c = _mixer_bwd(b["dx1"], t, a, b, sinks, w_ba, w_bm, w_out)
    grad_x, acc_p = _pre_bwd(c["dproj"], x, b["dx1"], t["vecs"], w_cat)
    big = dict(w_cat=c["g_w_cat"], w_ba=c["g_w_ba"], w_bm=c["g_w_bm"], w_out=c["g_w_out"], w_gate=b["g_w_gate"],
               w_up=b["g_w_up"], w_down=b["g_w_down"])
    return b["loss"], grad_x, big, _small_grads(acc_p, b, c)


def _tables(mod, gains, conv_w, conv_b, b_if, norm_w, positions):
    cos, sin = _rope_tables(positions)
    return dict(
        vecs=jnp.concatenate([mod, gains, jnp.zeros((6, D_MODEL), F32)], axis=0),
        cw=jnp.concatenate([conv_w, conv_b.reshape(1, -1), jnp.zeros((3, 2 * 512), F32)], axis=0),
        sv=jnp.zeros((8, 512), F32).at[0].set(norm_w).at[1, 0:8].set(b_if), cos=cos, sin=sin)


def _mixer_fwd(x, t, sinks, w_cat):
    h, pa, pm, pif, pg = _pre_proj(x, t["vecs"], w_cat)
    n_blk = x.shape[0] // ATTN_BLOCK
    (ya, qr, kr, vb, lse), = _fused_call([_attn_fwd_part(pa, t["cos"], t["sin"], sinks)], "attn_fwd", n_blk)
    (ym, zc, cst, nst), = _fused_call([_mlstm_fwd_part(pm, pif, t["cw"], t["sv"])], "mlstm_fwd", n_blk)
    return dict(h=h, pm=pm, pif=pif, pg=pg, ya=ya, qr=qr, kr=kr, vb=vb, lse=lse, ym=ym, zc=zc, cst=cst, nst=nst)


def _ffn_part(x, tgt, t, a, w_ba, w_bm, w_out, w_gate, w_up, w_down):
    x1, merged, mix, pba, pbm = _mix_fwd(x, a["ya"], a["ym"], a["pg"], t["vecs"], w_ba, w_bm, w_out)
    dx1, h2, hid, da, du, dff, acc_f, loss = _ffn_fwd_bwd(x1, tgt, t["vecs"], w_gate, w_up, w_down)
    return dict(merged=merged, mix=mix, pba=pba, pbm=pbm, dx1=dx1, acc_f=acc_f, loss=loss[0, 0],
                g_w_gate=_matmul_tn(da, h2, 1024, "dw_ffn_gate"),
                g_w_up=_matmul_tn(du, h2, 1024, "dw_ffn_up"),
                g_w_down=_matmul_tn(hid, dff, 1024, "dw_ffn_down"))


def _mixer_bwd(dx1, t, a, b, sinks, w_ba, w_bm, w_out):
    dmix, dpa, dpb, dg, dya, dym, acc_m = _mix_bwd(dx1, b["mix"], b["pba"], b["pbm"], a["pg"], t["vecs"], w_ba, w_bm, w_out)
    g_w_out = _matmul_tn(b["merged"], dmix, 1024, "dw_out")
    g_w_ba = _matmul_tn(a["ya"], dpa, 1024, "dw_branch_attn")
    g_w_bm = _matmul_tn(a["ym"], dpb, 1024, "dw_branch_mlstm")
    n_blk = dx1.shape[0] // ATTN_BLOCK
    (dq, dkv, dkv_last, dsink), = _fused_call(
        [_attn_bwd_part(dya, a["qr"], a["kr"], a["vb"], a["lse"], t["cos"], t["sin"], sinks)], "attn_bwd", n_blk)
    (dm, dif, dcw, dsv), = _fused_call(
        [_mlstm_bwd_part(a["pm"], a["zc"], a["pif"], t["cw"], t["sv"], dym, a["cst"], a["nst"])], "mlstm_bwd", n_blk)
    dkv = lax.dynamic_update_slice(dkv, dkv_last, (dkv.shape[0] - ATTN_BLOCK, 0))
    dproj = jnp.concatenate([dq, dkv, dm, dif, dg], axis=1)
    return dict(dproj=dproj, g_w_cat=_matmul_tn(a["h"], dproj, 1664, "dw_in"), g_w_out=g_w_out, g_w_ba=g_w_ba,
                g_w_bm=g_w_bm, acc_m=acc_m, dsink=dsink, dcw=dcw, dsv=dsv)


def _small_grads(acc_p, b, c):
    acc_f, acc_m = b["acc_f"], c["acc_m"]
    dmod = jnp.stack([acc_p[1], acc_p[0], acc_m[0], acc_f[3], acc_f[2], acc_f[0]])
    dgains = jnp.stack([acc_p[2], acc_m[1], acc_f[4], acc_f[1]])
    return dict(dmod=dmod, dgains=dgains, dconv_w=c["dcw"][0:4], dconv_b=c["dcw"][4], db_if=c["dsv"][1, 0:8],
                dsinks=c["dsink"][0, 0:8], dnorm_w=c["dsv"][0])


MESH_ID = pl.DeviceIdType.MESH


def _mesh_pos():
    return lax.axis_index("x"), lax.axis_index("y"), lax.axis_index("c")


def _flip(v, bit):
    return 1 - v if bit else v


def _relations():
    return [((r >> 2) & 1, (r >> 1) & 1, r & 1) for r in range(1, N_DEV)]


def _small_exchange(p, gather, name):
    V = p.shape[-1]

    def body(p_ref, out_ref, send_sems, recv_sems):
        x, y, c = _mesh_pos()
        me = 4 * x + 2 * y + c
        out_ref[me] = p_ref[...] if gather else p_ref[me]
        peers = []
        for dx, dy, dc in _relations():
            px, py, pc = _flip(x, dx), _flip(y, dy), _flip(c, dc)
            peers.append(((px, py, pc), 4 * px + 2 * py + pc))

        def copy(k, landing):
            peer, pid = peers[k]
            return pltpu.make_async_remote_copy(
                src_ref=p_ref if gather else p_ref.at[pid], dst_ref=out_ref.at[landing],
                send_sem=send_sems.at[k], recv_sem=recv_sems.at[k], device_id=peer, device_id_type=MESH_ID)

        sends = [copy(k, me) for k in range(N_DEV - 1)]
        for cp in sends:
            cp.start()
        for k in range(N_DEV - 1):
            copy(k, peers[k][1]).wait_recv()
        for cp in sends:
            cp.wait_send()

    vm = pl.BlockSpec(memory_space=pltpu.VMEM)
    return pl.pallas_call(
        body, name=name, in_specs=[vm], out_specs=vm,
        out_shape=jax.ShapeDtypeStruct((N_DEV, 8, V), F32),
        scratch_shapes=[pltpu.SemaphoreType.DMA((N_DEV - 1,)), pltpu.SemaphoreType.DMA((N_DEV - 1,))],
        compiler_params=pltpu.CompilerParams(vmem_limit_bytes=VMEM_LIMIT),
    )(p)


HBM_SPEC = pl.BlockSpec(memory_space=pltpu.HBM)
SEM_SPEC = pl.BlockSpec(memory_space=pltpu.SEMAPHORE)


def _peers(x, y, c):
    out = []
    for dx, dy, dc in _relations():
        px, py, pc = _flip(x, dx), _flip(y, dy), _flip(c, dc)
        out.append(((px, py, pc), 4 * px + 2 * py + pc))
    return out


def _exchange_start(arrs, gather, after, name):
    n = len(arrs)
    me_out = 4 * lax.axis_index("x") + 2 * lax.axis_index("y") + lax.axis_index("c")
    lands = []
    for a in arrs:
        own = a[None] if gather else lax.dynamic_index_in_dim(a, me_out, 0, keepdims=True)
        empty = lax.empty(((N_DEV,) + a.shape) if gather else a.shape, a.dtype)
        lands.append(lax.dynamic_update_index_in_dim(empty, own, me_out, 0))

    def body(*refs):
        a_refs, l_refs = refs[:n], refs[n:2 * n]
        send_sems, recv_sems = refs[2 * n + 1], refs[2 * n + 2]
        token = refs[4 * n + 3]
        x, y, c = _mesh_pos()
        me = 4 * x + 2 * y + c
        for a in range(n):
            for k, (peer, pid) in enumerate(_peers(x, y, c)):
                pltpu.make_async_remote_copy(
                    src_ref=a_refs[a] if gather else a_refs[a].at[pid], dst_ref=l_refs[a].at[me],
                    send_sem=send_sems.at[a * (N_DEV - 1) + k], recv_sem=recv_sems.at[a * (N_DEV - 1) + k],
                    device_id=peer, device_id_type=MESH_ID).start()
        token[...] = jnp.zeros_like(token)

    sem = pltpu.SemaphoreType.DMA((n * (N_DEV - 1),))
    hbm = lambda a: pltpu.with_memory_space_constraint(a, pltpu.HBM)
    res = pl.pallas_call(
        body, name=name,
        out_shape=(sem, sem, *[pltpu.HBM(a.shape, a.dtype) for a in arrs], *[pltpu.HBM(l.shape, l.dtype) for l in lands],
                   jax.ShapeDtypeStruct((8, LANES), F32)),
        in_specs=[HBM_SPEC] * (2 * n) + [pl.BlockSpec(memory_space=pl.ANY)],
        out_specs=(SEM_SPEC, SEM_SPEC, *[HBM_SPEC] * (2 * n), pl.BlockSpec(memory_space=pltpu.VMEM)),
        input_output_aliases={i: 2 + i for i in range(2 * n)},
        compiler_params=pltpu.CompilerParams(has_side_effects=pltpu.SideEffectType.DATAFLOW_SIDE_EFFECTING),
    )(*[hbm(a) for a in arrs], *[hbm(l) for l in lands], after)
    return dict(sems=res[0:2], arrs=res[2:2 + n], lands=res[2 + n:2 + 2 * n], token=res[2 + 2 * n], gather=gather)


def _exchange_wait(st, after, name):
    n = len(st["arrs"])
    gather = st["gather"]

    def body(*refs):
        a_refs, l_refs = refs[:n], refs[n:2 * n]
        send_sems, recv_sems = refs[2 * n], refs[2 * n + 1]
        x, y, c = _mesh_pos()
        for a in range(n):
            for k, (peer, pid) in enumerate(_peers(x, y, c)):
                cp = pltpu.make_async_remote_copy(
                    src_ref=a_refs[a] if gather else a_refs[a].at[pid], dst_ref=l_refs[a].at[pid],
                    send_sem=send_sems.at[a * (N_DEV - 1) + k], recv_sem=recv_sems.at[a * (N_DEV - 1) + k],
                    device_id=peer, device_id_type=MESH_ID)
                cp.wait_send()
                cp.wait_recv()

    both = list(st["arrs"]) + list(st["lands"])
    res = pl.pallas_call(
        body, name=name, out_shape=[pltpu.HBM(a.shape, a.dtype) for a in both],
        in_specs=[HBM_SPEC] * (2 * n) + [SEM_SPEC, SEM_SPEC, pl.BlockSpec(memory_space=pl.ANY)],
        out_specs=[HBM_SPEC] * (2 * n), input_output_aliases={i: i for i in range(2 * n)},
        compiler_params=pltpu.CompilerParams(has_side_effects=pltpu.SideEffectType.DATAFLOW_SIDE_EFFECTING),
    )(*both, *st["sems"], after)
    return res[n:2 * n]


def _after(x, token):
    return lax.optimization_barrier((x, token))[0]


def _all_gather_hbm(shards):
    n = len(shards)

    def body(*refs):
        p_refs, out_refs = refs[:n], refs[n:2 * n]
        send_sems, recv_sems, local_sems = refs[2 * n:]
        x, y, c = _mesh_pos()
        me, sibling = (x, y, c), (x, y, 1 - c)
        chips = [(1 - x, y), (x, 1 - y), (1 - x, 1 - y)]

        def copy(a, k, block, to, own=False):
            slot = out_refs[a].at[4 * block[0] + 2 * block[1] + block[2]]
            return pltpu.make_async_remote_copy(
                src_ref=p_refs[a] if own else slot, dst_ref=slot,
                send_sem=send_sems.at[a, k], recv_sem=recv_sems.at[a, k], device_id=to, device_id_type=MESH_ID)

        mine = [pltpu.make_async_copy(p_refs[a], out_refs[a].at[4 * x + 2 * y + c], local_sems.at[a]) for a in range(n)]
        for cp in mine:
            cp.start()
        first = []
        for a in range(n):
            first.append(copy(a, 0, me, sibling, own=True))
            first += [copy(a, 1 + j, me, (*chip, c), own=True) for j, chip in enumerate(chips)]
        for cp in first:
            cp.start()
        passed = []
        for j, chip in enumerate(chips):
            for a in range(n):
                copy(a, 1 + j, (*chip, c), me).wait_recv()
                passed.append(copy(a, 4 + j, (*chip, c), sibling))
                passed[-1].start()
        for a in range(n):
            copy(a, 0, sibling, me).wait_recv()
            for j, chip in enumerate(chips):
                copy(a, 4 + j, (*chip, 1 - c), me).wait_recv()
        for cp in first + passed:
            cp.wait_send()
        for cp in mine:
            cp.wait()

    hbm = pl.BlockSpec(memory_space=pl.ANY)
    return pl.pallas_call(
        body, name="gather_weights", in_specs=[hbm] * n, out_specs=[hbm] * n,
        out_shape=[jax.ShapeDtypeStruct((N_DEV,) + s.shape, s.dtype) for s in shards],
        scratch_shapes=[pltpu.SemaphoreType.DMA((n, N_DEV - 1)), pltpu.SemaphoreType.DMA((n, N_DEV - 1)),
                        pltpu.SemaphoreType.DMA((n,))],
    )(*shards)


def _adamw(w, g, m, v):
    m2 = ADAM_B1 * m + (1.0 - ADAM_B1) * g
    v2 = ADAM_B2 * v + (1.0 - ADAM_B2) * (g * g)
    m_hat = m2 / (1.0 - ADAM_B1 ** ADAM_STEP)
    v_hat = v2 / (1.0 - ADAM_B2 ** ADAM_STEP)
    delta = -ADAM_LR * (m_hat / (jnp.sqrt(v_hat) + ADAM_EPS) + ADAM_WD * w)
    return delta, m2, v2


def _mod_partial(cmat, w_shard, b_shard):
    def body(c_ref, w_ref, b_ref, o_ref):
        o_ref[...] = _dot(c_ref[...].astype(BF16), w_ref[...].astype(BF16)) + b_ref[...]

    return pl.pallas_call(
        body, name="mod_partial", out_shape=jax.ShapeDtypeStruct((N_DEV, w_shard.shape[1]), F32),
        compiler_params=_params(),
    )(cmat, w_shard, b_shard)


def _adamw_w_ada(cmat, dmod_cols, w, m, v):
    def body(c_ref, d_ref, w_ref, m_ref, v_ref, g_ref, dl_ref, m2_ref, v2_ref):
        g = _dot_tn(c_ref[...].astype(BF16), d_ref[...].astype(BF16))
        g_ref[...] = g
        dl_ref[...], m2_ref[...], v2_ref[...] = _adamw(w_ref[...], g, m_ref[...], v_ref[...])

    return pl.pallas_call(
        body, name="adamw_w_ada", out_shape=[jax.ShapeDtypeStruct(w.shape, F32)] * 4,
        compiler_params=_params(),
    )(cmat, dmod_cols, w, m, v)


def _adamw_small(gathered, w, m, v):
    def body(g_ref, w_ref, m_ref, v_ref, go_ref, dl_ref, m2_ref, v2_ref):
        g = g_ref[0]
        for k in range(1, N_DEV):
            g = g + g_ref[k]
        go_ref[...] = g
        dl_ref[...], m2_ref[...], v2_ref[...] = _adamw(w_ref[...], g, m_ref[...], v_ref[...])

    return pl.pallas_call(
        body, name="adamw_small", out_shape=[jax.ShapeDtypeStruct(w.shape, F32)] * 4,
        compiler_params=_params(),
    )(gathered, w, m, v)


def _row_tile(rows):
    return rows // 4 if rows >= 512 else rows


def _sum_partials(r_ref):
    g = r_ref[0].astype(F32)
    for k in range(1, N_DEV):
        g = g + r_ref[k].astype(F32)
    return g


def _adamw_sum(recv, w, m, v, name):
    r, cdim = w.shape
    tr = _row_tile(r)

    def body(r_ref, w_ref, m_ref, v_ref, g_ref, dl_ref, m2_ref, v2_ref):
        g = _sum_partials(r_ref)
        g_ref[...] = g
        dl_ref[...], m2_ref[...], v2_ref[...] = _adamw(w_ref[...], g, m_ref[...], v_ref[...])

    row = pl.BlockSpec((tr, cdim), lambda i: (i, 0))
    return pl.pallas_call(
        body, name=name, grid=(r // tr,),
        in_specs=[pl.BlockSpec((N_DEV, tr, cdim), lambda i: (0, i, 0)), row, row, row],
        out_specs=[row] * 4, out_shape=[jax.ShapeDtypeStruct((r, cdim), F32)] * 4,
        compiler_params=_params(("parallel",)),
    )(recv, w, m, v)


def _sum8(recv, name):
    _, r, cdim = recv.shape
    tr = _row_tile(r)

    def body(r_ref, g_ref):
        g_ref[...] = _sum_partials(r_ref)

    return pl.pallas_call(
        body, name=name, grid=(r // tr,),
        in_specs=[pl.BlockSpec((N_DEV, tr, cdim), lambda i: (0, i, 0))],
        out_specs=pl.BlockSpec((tr, cdim), lambda i: (i, 0)), out_shape=jax.ShapeDtypeStruct((r, cdim), F32),
        compiler_params=_params(("parallel",)),
    )(recv)


def _adamw_plain(g, w, m, v, name):
    r, cdim = w.shape
    tr = _row_tile(r)

    def body(g_ref, w_ref, m_ref, v_ref, dl_ref, m2_ref, v2_ref):
        dl_ref[...], m2_ref[...], v2_ref[...] = _adamw(w_ref[...], g_ref[...], m_ref[...], v_ref[...])

    row = pl.BlockSpec((tr, cdim), lambda i: (i, 0))
    return pl.pallas_call(
        body, name=name, grid=(r // tr,), in_specs=[row] * 4, out_specs=[row] * 3,
        out_shape=[jax.ShapeDtypeStruct((r, cdim), F32)] * 3,
        compiler_params=_params(("parallel",)),
    )(g, w, m, v)


IN_SHARD = 609
IN_SHARD_PAD = 640
IF_AT = A_W + M_W


def _regrouped(u):
    return u if u < IF_AT + 8 else u + (IF_W - 8)


def _selection(k, rows, row0, transpose):
    shape = (rows, IN_SHARD_PAD) if transpose else (IN_SHARD_PAD, rows)
    l = lax.broadcasted_iota(jnp.int32, shape, 1 if transpose else 0)
    r = lax.broadcasted_iota(jnp.int32, shape, 0 if transpose else 1) + row0
    u = l + IN_SHARD * k
    ru = u + jnp.where(u >= IF_AT + 8, IF_W - 8, 0)
    return ((ru == r) & (l < IN_SHARD)).astype(BF16)


def _regroup_w_in(g):
    def body(g_ref, o_ref):
        for cb in range(CAT_W // LANES):
            r0 = cb * LANES
            acc = jnp.zeros((D_MODEL, LANES), F32)
            for k in range(N_DEV):
                lo, hi = _regrouped(IN_SHARD * k), _regrouped(IN_SHARD * k + IN_SHARD - 1)
                if hi >= r0 and lo < r0 + LANES:
                    acc = acc + _dot(g_ref[k], _selection(k, LANES, r0, False))
            o_ref[:, r0:r0 + LANES] = acc.astype(BF16)

    return pl.pallas_call(
        body, name="regroup_w_in", out_shape=jax.ShapeDtypeStruct((D_MODEL, CAT_W), BF16),
        compiler_params=_params(),
    )(g)


def _ungroup_w_in(g_cat):
    def body(g_ref, o_ref):
        for k in range(N_DEV):
            lo, hi = _regrouped(IN_SHARD * k), _regrouped(IN_SHARD * k + IN_SHARD - 1)
            w0, w1 = lo // LANES * LANES, (hi // LANES + 1) * LANES
            o_ref[k] = _dot(g_ref[:, w0:w1], _selection(k, w1 - w0, w0, True)).astype(BF16)

    return pl.pallas_call(
        body, name="ungroup_w_in", out_shape=jax.ShapeDtypeStruct((N_DEV, D_MODEL, IN_SHARD_PAD), BF16),
        compiler_params=_params(),
    )(g_cat)


SMALL = (("b_ada", 6144), ("g_pre_mix", 1024), ("g_post_mix", 1024), ("g_pre_ffn", 1024), ("g_post_ffn", 1024),
         ("conv_b", 1024), ("mlstm_norm_w", 512), ("b_if", 8), ("attn_sinks", 8))
SMALL_W = 12032
SMALL_PAD = 12288


def _pack_small(vals):
    parts = []
    for n, width in SMALL:
        a = vals[n].reshape(-1)
        parts.append(jnp.pad(a, (0, LANES - width)) if width < LANES else a)
    return jnp.concatenate(parts)


def _unpack_small(vec):
    out, off = {}, 0
    for n, width in SMALL:
        out[n] = vec[off:off + width].reshape(1, width)
        off += max(width, LANES)
    return out


WEIGHT_NAMES = ("w_ada", "b_ada", "g_pre_mix", "g_post_mix", "w_in", "b_if", "conv_w", "conv_b", "attn_sinks",
                "mlstm_norm_w", "w_branch_attn", "w_branch_mlstm", "w_out", "g_pre_ffn", "g_post_ffn",
                "w_ffn_gate", "w_ffn_up", "w_ffn_down")


def kernel(x, c, positions, w_ada, b_ada, g_pre_mix, g_post_mix, w_in, b_if, conv_w, conv_b, attn_sinks, mlstm_norm_w, w_branch_attn, w_branch_mlstm, w_out, g_pre_ffn, g_post_ffn, w_ffn_gate, w_ffn_up, w_ffn_down, loss_target, m_w_ada, m_b_ada, m_g_pre_mix, m_g_post_mix, m_w_in, m_b_if, m_conv_w, m_conv_b, m_attn_sinks, m_mlstm_norm_w, m_w_branch_attn, m_w_branch_mlstm, m_w_out, m_g_pre_ffn, m_g_post_ffn, m_w_ffn_gate, m_w_ffn_up, m_w_ffn_down, v_w_ada, v_b_ada, v_g_pre_mix, v_g_post_mix, v_w_in, v_b_if, v_conv_w, v_conv_b, v_attn_sinks, v_mlstm_norm_w, v_w_branch_attn, v_w_branch_mlstm, v_w_out, v_g_pre_ffn, v_g_post_ffn, v_w_ffn_gate, v_w_ffn_up, v_w_ffn_down):
    given = dict(locals())
    W = {n: given[n][0] for n in WEIGHT_NAMES}
    M = {n: given["m_" + n][0] for n in WEIGHT_NAMES}
    V = {n: given["v_" + n][0] for n in WEIGHT_NAMES}
    me = 4 * lax.axis_index("x") + 2 * lax.axis_index("y") + lax.axis_index("c")

    ff_sh = D_FF // N_DEV
    g_in, g_conv, cg = _all_gather_hbm([jnp.pad(W["w_in"], ((0, 0), (0, IN_SHARD_PAD - IN_SHARD))).astype(BF16),
                                        jnp.pad(W["conv_w"], ((0, 4), (0, 0))), c.reshape(8, D_MODEL // 8)])

    cmat = cg.reshape(N_DEV, D_MODEL)
    ada_w = D_MODEL * 6 // N_DEV
    b_cols = lax.dynamic_slice(W["b_ada"], (me * ada_w,), (ada_w,)).reshape(1, ada_w)
    mod_part = _mod_partial(cmat, W["w_ada"], b_cols)
    mod_recv = _small_exchange(jnp.broadcast_to(mod_part[:, None, :], (N_DEV, 8, ada_w)), False, "scatter_mod")
    mod = mod_recv[:, 0, :].reshape(6, D_MODEL)

    st_b = _exchange_start([W["w_branch_attn"].astype(BF16), W["w_branch_mlstm"].astype(BF16), W["w_out"].astype(BF16),
                            W["w_ffn_gate"].T.astype(BF16), W["w_ffn_up"].T.astype(BF16), W["w_ffn_down"].astype(BF16)],
                           True, mod_recv, "gather_rest_start")
    g_in = _after(g_in, st_b["token"])
    cols = lambda g: g.transpose(1, 0, 2).reshape(g.shape[1], N_DEV * g.shape[2])
    gains = jnp.stack([W["g_pre_mix"], W["g_post_mix"], W["g_pre_ffn"], W["g_post_ffn"]])
    xs, tgt = x[0], loss_target[0]
    t = _tables(mod, gains, cols(g_conv)[0:4], W["conv_b"], W["b_if"], W["mlstm_norm_w"], positions[0])
    w_cat = _regroup_w_in(g_in)
    a = _mixer_fwd(xs, t, W["attn_sinks"], w_cat)
    g_ba, g_bm, g_out, g_gate, g_up, g_down = _exchange_wait(st_b, a["ym"], "gather_rest_wait")
    w_ba, w_bm, w_out = cols(g_ba), cols(g_bm), g_out.reshape(D_MODEL, D_MODEL)
    b = _ffn_part(xs, tgt, t, a, w_ba, w_bm, w_out, g_gate.reshape(D_FF, D_MODEL), g_up.reshape(D_FF, D_MODEL),
                  g_down.reshape(D_FF, D_MODEL))

    st_f = _exchange_start([b["g_w_gate"].reshape(N_DEV, ff_sh, D_MODEL), b["g_w_up"].reshape(N_DEV, ff_sh, D_MODEL),
                            b["g_w_down"].reshape(N_DEV, ff_sh, D_MODEL)], False, b["dx1"], "scatter_ffn_start")
    cm = _mixer_bwd(_after(b["dx1"], st_f["token"]), t, a, b, W["attn_sinks"], w_ba, w_bm, w_out)
    pieces = lambda g, n: g.reshape(g.shape[0], N_DEV, n).transpose(1, 0, 2)
    st_m = _exchange_start([_ungroup_w_in(cm["g_w_cat"]), pieces(cm["g_w_ba"], 128), pieces(cm["g_w_bm"], 128),
                            cm["g_w_out"].reshape(N_DEV, D_MODEL // N_DEV, D_MODEL),
                            jnp.pad(pieces(cm["dcw"][0:4], 128), ((0, 0), (0, 4), (0, 0)))], False, cm["dproj"],
                           "scatter_mixer_start")
    r_gate, r_up, r_down = _exchange_wait(st_f, st_m["token"], "scatter_ffn_wait")
    grad_x, acc_p = _pre_bwd(_after(cm["dproj"], st_m["token"]), xs, b["dx1"], t["vecs"], w_cat)
    small = _small_grads(acc_p, b, cm)
    loss = b["loss"]

    big_out = [{} for _ in range(4)]

    def put(n, res):
        for k in range(4):
            big_out[k][n] = res[k][None]

    put("w_ffn_down", _adamw_sum(r_down, W["w_ffn_down"], M["w_ffn_down"], V["w_ffn_down"], "adamw_w_ffn_down"))
    for n, g in (("w_ffn_gate", _sum8(r_gate, "sum_w_ffn_gate").T), ("w_ffn_up", _sum8(r_up, "sum_w_ffn_up").T)):
        put(n, [g] + list(_adamw_plain(g, W[n], M[n], V[n], "adamw_" + n)))

    part = {"b_ada": small["dmod"], "g_pre_mix": small["dgains"][0], "g_post_mix": small["dgains"][1],
            "g_pre_ffn": small["dgains"][2], "g_post_ffn": small["dgains"][3], "conv_b": small["dconv_b"],
            "mlstm_norm_w": small["dnorm_w"], "b_if": small["db_if"], "attn_sinks": small["dsinks"]}
    rows8 = lambda d: jnp.pad(_pack_small(d), (0, SMALL_PAD - SMALL_W)).reshape(8, SMALL_PAD // 8)
    sg = _small_exchange(rows8(part), True, "gather_small")
    small_out = [_unpack_small(o.reshape(-1)) for o in _adamw_small(sg, rows8(W), rows8(M), rows8(V))]
    dmod_cols = lax.dynamic_slice(sg.reshape(N_DEV, SMALL_PAD), (0, me * ada_w), (N_DEV, ada_w))
    ada_out = _adamw_w_ada(cmat, dmod_cols, W["w_ada"], M["w_ada"], V["w_ada"])

    r_in, r_ba, r_bm, r_out, r_conv = _exchange_wait(st_m, grad_x, "scatter_mixer_wait")
    for n, r in (("w_branch_attn", r_ba), ("w_branch_mlstm", r_bm), ("w_out", r_out)):
        put(n, _adamw_sum(r, W[n], M[n], V[n], "adamw_" + n))
    pad4 = lambda v: jnp.pad(v, ((0, 4), (0, 0)))
    put("conv_w", [o[0:4] for o in _adamw_sum(r_conv, pad4(W["conv_w"]), pad4(M["conv_w"]), pad4(V["conv_w"]),
                                                 "adamw_conv_w")])
    g = _sum8(r_in, "sum_w_in")[:, 0:IN_SHARD]
    put("w_in", [g] + list(_adamw_plain(g, W["w_in"], M["w_in"], V["w_in"], "adamw_w_in")))

    total = lax.psum(loss, ("x", "y", "c"))
    outs = [total, grad_x[None]]
    for k in range(4):
        for n in WEIGHT_NAMES:
            if n == "w_ada":
                outs.append(ada_out[k][None])
            elif n in big_out[k]:
                outs.append(big_out[k][n])
            else:
                outs.append(small_out[k][n])
    return tuple(outs)
```

```python
import functools

import jax
import jax.numpy as jnp
import numpy as np
from jax import lax
from jax.experimental import pallas as pl
from jax.experimental.pallas import tpu as pltpu

F32 = jnp.float32
BF16 = jnp.bfloat16

N_DEV = 8
D_MODEL = 1024
D_FF = 2816
N_Q_HEADS = 8
HEAD_DIM = 64
ATTN_BLOCK = 128
ROPE_THETA = 10000.0
MLSTM_HEADS = 4
MLSTM_HEAD_DIM = 128
MLSTM_CHUNK = 128
NORM_EPS = 1e-6
ADAM_LR = 0.001
ADAM_B1 = 0.9
ADAM_B2 = 0.999
ADAM_EPS = 1e-08
ADAM_WD = 0.01
ADAM_STEP = 10

ROW_TILE = 256
WIDE_TILE = 512
LANES = 128
NEG = -1e30
VMEM_LIMIT = 56 * 1024 * 1024

A_W = 768
M_W = 2048
IF_W = 128
G_W = 2048
CAT_W = A_W + M_W + IF_W + G_W

R_SHIFT_M, R_SCALE_M, R_GATE_M, R_SHIFT_F, R_SCALE_F, R_GATE_F = 0, 1, 2, 3, 4, 5
R_G_PRE_MIX, R_G_POST_MIX, R_G_PRE_FFN, R_G_POST_FFN = 6, 7, 8, 9


def _dot(a, b):
    return jnp.dot(a, b, preferred_element_type=F32)


def _dot_nt(a, b):
    return lax.dot_general(a, b, (((1,), (1,)), ((), ())), preferred_element_type=F32)


def _dot_tn(a, b):
    return lax.dot_general(a, b, (((0,), (0,)), ((), ())), preferred_element_type=F32)


def _recip(x):
    return 1.0 / x


def _sigmoid(x):
    return _recip(1.0 + jnp.exp(-x))


def _colsum(x):
    return jnp.sum(x, axis=0, keepdims=True)


def _rowmean(x):
    return jnp.mean(x, axis=-1, keepdims=True)


def _params(sem=None, vmem=VMEM_LIMIT):
    kw = dict(vmem_limit_bytes=vmem)
    if sem is not None:
        kw["dimension_semantics"] = sem
    return pltpu.CompilerParams(**kw)


def _full(shape):
    nd = len(shape)
    return pl.BlockSpec(shape, lambda *_: (0,) * nd)


def _pre_proj(x, vecs, w_cat):
    S = x.shape[0]
    tm = WIDE_TILE

    def body(x_ref, v_ref, w_ref, h_ref, pa_ref, pm_ref, pif_ref, pg_ref):
        xv = x_ref[...]
        r = lax.rsqrt(_rowmean(xv * xv) + NORM_EPS)
        h = (xv * r * v_ref[R_G_PRE_MIX:R_G_PRE_MIX + 1, :]) * (1.0 + v_ref[R_SCALE_M:R_SCALE_M + 1, :]) \
            + v_ref[R_SHIFT_M:R_SHIFT_M + 1, :]
        hb = h.astype(BF16)
        h_ref[...] = hb
        pa_ref[...] = _dot(hb, w_ref[:, 0:A_W])
        pm_ref[...] = _dot(hb, w_ref[:, A_W:A_W + M_W])
        pif_ref[...] = _dot(hb, w_ref[:, A_W + M_W:A_W + M_W + IF_W])
        pg_ref[...] = _dot(hb, w_ref[:, A_W + M_W + IF_W:CAT_W]).astype(BF16)

    row = lambda w: pl.BlockSpec((tm, w), lambda i: (i, 0))
    return pl.pallas_call(
        body, name="pre_proj", grid=(S // tm,),
        in_specs=[row(D_MODEL), _full(vecs.shape), _full(w_cat.shape)],
        out_specs=[row(D_MODEL), row(A_W), row(M_W), row(IF_W), row(G_W)],
        out_shape=[jax.ShapeDtypeStruct((S, D_MODEL), BF16), jax.ShapeDtypeStruct((S, A_W), F32),
                   jax.ShapeDtypeStruct((S, M_W), F32), jax.ShapeDtypeStruct((S, IF_W), F32),
                   jax.ShapeDtypeStruct((S, G_W), BF16)],
        compiler_params=_params(("parallel",)),
    )(x, vecs, w_cat)


def _mix_fwd(x, ya, ym, pg, vecs, w_ba, w_bm, w_out):
    S = x.shape[0]
    tm = WIDE_TILE

    def body(x_ref, ya_ref, ym_ref, pg_ref, v_ref, wba_ref, wbm_ref, wout_ref,
             x1_ref, merged_ref, mix_ref, pa_ref, pb_ref):
        pa = _dot(ya_ref[...], wba_ref[...])
        pb = _dot(ym_ref[...], wbm_ref[...])
        merged = _sigmoid(pg_ref[:, 0:D_MODEL].astype(F32)) * pa + _sigmoid(pg_ref[:, D_MODEL:G_W].astype(F32)) * pb
        mb = merged.astype(BF16)
        mix = _dot(mb, wout_ref[...])
        r = lax.rsqrt(_rowmean(mix * mix) + NORM_EPS)
        x1_ref[...] = x_ref[...] + v_ref[R_GATE_M:R_GATE_M + 1, :] * (mix * r * v_ref[R_G_POST_MIX:R_G_POST_MIX + 1, :])
        merged_ref[...] = mb
        mix_ref[...] = mix
        pa_ref[...] = pa.astype(BF16)
        pb_ref[...] = pb.astype(BF16)

    row = lambda w: pl.BlockSpec((tm, w), lambda i: (i, 0))
    sd = lambda w, dt: jax.ShapeDtypeStruct((S, w), dt)
    return pl.pallas_call(
        body, name="mix_fwd", grid=(S // tm,),
        in_specs=[row(D_MODEL), row(512), row(512), row(G_W), _full(vecs.shape), _full(w_ba.shape),
                  _full(w_bm.shape), _full(w_out.shape)],
        out_specs=[row(D_MODEL)] * 5,
        out_shape=[sd(D_MODEL, F32), sd(D_MODEL, BF16), sd(D_MODEL, F32), sd(D_MODEL, BF16), sd(D_MODEL, BF16)],
        compiler_params=_params(("parallel",)),
    )(x, ya, ym, pg, vecs, w_ba, w_bm, w_out)


def _ffn_fwd_bwd(x1, tgt, vecs, w_gate, w_up, w_down):
    S = x1.shape[0]
    tm = ROW_TILE

    def body(x1_ref, tgt_ref, v_ref, wg_hbm, wu_hbm, wd_hbm,
             dx1_ref, h2_ref, hid_ref, da_ref, du_ref, dff_ref, acc_ref, loss_ref,
             wg, wu, wd, sem):
        i = pl.program_id(0)

        @pl.when(i == 0)
        def _():
            cps = [pltpu.make_async_copy(wg_hbm, wg, sem.at[0]), pltpu.make_async_copy(wu_hbm, wu, sem.at[1]),
                   pltpu.make_async_copy(wd_hbm, wd, sem.at[2])]
            for cp in cps:
                cp.start()
            for cp in cps:
                cp.wait()
            acc_ref[...] = jnp.zeros_like(acc_ref)
            loss_ref[...] = jnp.zeros_like(loss_ref)

        vrow = lambda r: v_ref[r:r + 1, :]
        x1v = x1_ref[...]
        r3 = lax.rsqrt(_rowmean(x1v * x1v) + NORM_EPS)
        x1hat = x1v * r3
        xn3 = x1hat * vrow(R_G_PRE_FFN)
        h2b = (xn3 * (1.0 + vrow(R_SCALE_F)) + vrow(R_SHIFT_F)).astype(BF16)
        h2_ref[...] = h2b
        a = _dot_nt(h2b, wg[...])
        u = _dot_nt(h2b, wu[...])
        sg = _sigmoid(a)
        sil = a * sg
        hidb = (sil * u).astype(BF16)
        hid_ref[...] = hidb
        ff = _dot(hidb, wd[...])
        r4 = lax.rsqrt(_rowmean(ff * ff) + NORM_EPS)
        ffhat = ff * r4
        n4 = ffhat * vrow(R_G_POST_FFN)
        err = x1v + vrow(R_GATE_F) * n4 - tgt_ref[...]
        loss_ref[...] += jnp.sum(err * err) * (0.5 / D_MODEL)
        dy = err * (1.0 / D_MODEL)
        acc_ref[0:1, :] += _colsum(dy * n4)
        dn4 = dy * vrow(R_GATE_F)
        acc_ref[1:2, :] += _colsum(dn4 * ffhat)
        dffhat = dn4 * vrow(R_G_POST_FFN)
        dffb = (r4 * (dffhat - ffhat * _rowmean(dffhat * ffhat))).astype(BF16)
        dff_ref[...] = dffb
        dhid = _dot_nt(dffb, wd[...])
        dub = (dhid * sil).astype(BF16)
        dab = (dhid * u * (sg * (1.0 + a * (1.0 - sg)))).astype(BF16)
        da_ref[...] = dab
        du_ref[...] = dub
        dh2 = _dot(dab, wg[...]) + _dot(dub, wu[...])
        acc_ref[2:3, :] += _colsum(dh2 * xn3)
        acc_ref[3:4, :] += _colsum(dh2)
        dxn3 = dh2 * (1.0 + vrow(R_SCALE_F))
        acc_ref[4:5, :] += _colsum(dxn3 * x1hat)
        dx1hat = dxn3 * vrow(R_G_PRE_FFN)
        dx1_ref[...] = dy + r3 * (dx1hat - x1hat * _rowmean(dx1hat * x1hat))

    row = lambda w: pl.BlockSpec((tm, w), lambda i: (i, 0))
    sd = lambda w, dt: jax.ShapeDtypeStruct((S, w), dt)
    anyspec = pl.BlockSpec(memory_space=pl.ANY)
    return pl.pallas_call(
        body, name="ffn_fwd_bwd", grid=(S // tm,),
        in_specs=[row(D_MODEL), row(D_MODEL), _full(vecs.shape), anyspec, anyspec, anyspec],
        out_specs=[row(D_MODEL), row(D_MODEL), row(D_FF), row(D_FF), row(D_FF), row(D_MODEL),
                   _full((8, D_MODEL)), _full((8, LANES))],
        out_shape=[sd(D_MODEL, F32), sd(D_MODEL, BF16), sd(D_FF, BF16), sd(D_FF, BF16), sd(D_FF, BF16),
                   sd(D_MODEL, BF16), jax.ShapeDtypeStruct((8, D_MODEL), F32), jax.ShapeDtypeStruct((8, LANES), F32)],
        scratch_shapes=[pltpu.VMEM(w_gate.shape, BF16), pltpu.VMEM(w_up.shape, BF16), pltpu.VMEM(w_down.shape, BF16),
                        pltpu.SemaphoreType.DMA((3,))],
        compiler_params=_params(("arbitrary",)),
    )(x1, tgt, vecs, w_gate, w_up, w_down)


def _mix_bwd(dx1, mix, pa, pb, pg, vecs, w_ba, w_bm, w_out):
    S = dx1.shape[0]
    tm = WIDE_TILE

    def body(dx1_ref, mix_ref, pa_ref, pb_ref, pg_ref, v_ref, wba_ref, wbm_ref, wout_ref,
             dmix_ref, dpa_ref, dpb_ref, dg_ref, dya_ref, dym_ref, acc_ref):
        i = pl.program_id(0)

        @pl.when(i == 0)
        def _():
            acc_ref[...] = jnp.zeros_like(acc_ref)

        vrow = lambda r: v_ref[r:r + 1, :]
        dx1v = dx1_ref[...]
        mix = mix_ref[...]
        r2 = lax.rsqrt(_rowmean(mix * mix) + NORM_EPS)
        mixhat = mix * r2
        acc_ref[0:1, :] += _colsum(dx1v * (mixhat * vrow(R_G_POST_MIX)))
        dn2 = dx1v * vrow(R_GATE_M)
        acc_ref[1:2, :] += _colsum(dn2 * mixhat)
        dmixhat = dn2 * vrow(R_G_POST_MIX)
        dmixb = (r2 * (dmixhat - mixhat * _rowmean(dmixhat * mixhat))).astype(BF16)
        dmix_ref[...] = dmixb
        dmerged = _dot_nt(dmixb, wout_ref[...])
        sa = _sigmoid(pg_ref[:, 0:D_MODEL].astype(F32))
        sm = _sigmoid(pg_ref[:, D_MODEL:G_W].astype(F32))
        dpab = (dmerged * sa).astype(BF16)
        dpbb = (dmerged * sm).astype(BF16)
        dpa_ref[...] = dpab
        dpb_ref[...] = dpbb
        dg_ref[:, 0:D_MODEL] = (dmerged * pa_ref[...].astype(F32) * (sa * (1.0 - sa))).astype(BF16)
        dg_ref[:, D_MODEL:G_W] = (dmerged * pb_ref[...].astype(F32) * (sm * (1.0 - sm))).astype(BF16)
        dya_ref[...] = _dot_nt(dpab, wba_ref[...])
        dym_ref[...] = _dot_nt(dpbb, wbm_ref[...])

    row = lambda w: pl.BlockSpec((tm, w), lambda i: (i, 0))
    sd = lambda w, dt: jax.ShapeDtypeStruct((S, w), dt)
    return pl.pallas_call(
        body, name="mix_bwd", grid=(S // tm,),
        in_specs=[row(D_MODEL), row(D_MODEL), row(D_MODEL), row(D_MODEL), row(G_W), _full(vecs.shape),
                  _full(w_ba.shape), _full(w_bm.shape), _full(w_out.shape)],
        out_specs=[row(D_MODEL), row(D_MODEL), row(D_MODEL), row(G_W), row(512), row(512), _full((8, D_MODEL))],
        out_shape=[sd(D_MODEL, BF16), sd(D_MODEL, BF16), sd(D_MODEL, BF16), sd(G_W, BF16), sd(512, F32), sd(512, F32),
                   jax.ShapeDtypeStruct((8, D_MODEL), F32)],
        compiler_params=_params(("arbitrary",)),
    )(dx1, mix, pa, pb, pg, vecs, w_ba, w_bm, w_out)


def _pre_bwd(dproj, x, dx1, vecs, w_cat):
    S = x.shape[0]
    tm = WIDE_TILE

    def body(dp_ref, x_ref, dx1_ref, v_ref, w_ref, dx_ref, acc_ref):
        i = pl.program_id(0)

        @pl.when(i == 0)
        def _():
            acc_ref[...] = jnp.zeros_like(acc_ref)

        vrow = lambda r: v_ref[r:r + 1, :]
        dh = _dot_nt(dp_ref[...], w_ref[...])
        xv = x_ref[...]
        r1 = lax.rsqrt(_rowmean(xv * xv) + NORM_EPS)
        xhat = xv * r1
        acc_ref[0:1, :] += _colsum(dh * (xhat * vrow(R_G_PRE_MIX)))
        acc_ref[1:2, :] += _colsum(dh)
        dxn = dh * (1.0 + vrow(R_SCALE_M))
        acc_ref[2:3, :] += _colsum(dxn * xhat)
        dxhat = dxn * vrow(R_G_PRE_MIX)
        dx_ref[...] = dx1_ref[...] + r1 * (dxhat - xhat * _rowmean(dxhat * xhat))

    row = lambda w: pl.BlockSpec((tm, w), lambda i: (i, 0))
    return pl.pallas_call(
        body, name="pre_bwd", grid=(S // tm,),
        in_specs=[row(CAT_W), row(D_MODEL), row(D_MODEL), _full(vecs.shape), _full(w_cat.shape)],
        out_specs=[row(D_MODEL), _full((8, D_MODEL))],
        out_shape=[jax.ShapeDtypeStruct((S, D_MODEL), F32), jax.ShapeDtypeStruct((8, D_MODEL), F32)],
        compiler_params=_params(("arbitrary",)),
    )(dproj, x, dx1, vecs, w_cat)


def _matmul_tn(a, b, tn, name, ts=1024):
    S, K = a.shape
    N = b.shape[1]
    n_s = S // ts

    def body(a_ref, b_ref, o_ref, acc_ref):
        s = pl.program_id(1)

        @pl.when(s == 0)
        def _():
            acc_ref[...] = jnp.zeros_like(acc_ref)

        acc_ref[...] += _dot_tn(a_ref[...], b_ref[...])

        @pl.when(s == n_s - 1)
        def _():
            o_ref[...] = acc_ref[...].astype(BF16)

    return pl.pallas_call(
        body, name=name, grid=(N // tn, n_s),
        in_specs=[pl.BlockSpec((ts, K), lambda j, s: (s, 0)), pl.BlockSpec((ts, tn), lambda j, s: (s, j))],
        out_specs=pl.BlockSpec((K, tn), lambda j, s: (0, j)),
        out_shape=jax.ShapeDtypeStruct((K, N), BF16),
        scratch_shapes=[pltpu.VMEM((K, tn), F32)],
        compiler_params=_params(("parallel", "arbitrary")),
    )(a, b)


def _rope_swap(t):
    lane = lax.broadcasted_iota(jnp.int32, t.shape, 1)
    first = (lane & (HEAD_DIM - 1)) < (HEAD_DIM // 2)
    return jnp.where(first, pltpu.roll(t, LANES - HEAD_DIM // 2, 1), pltpu.roll(t, HEAD_DIM // 2, 1))


def _rope(t, cos, sin_signed):
    return t * cos + _rope_swap(t) * sin_signed


def _rope_t(d, cos, sin_signed):
    return d * cos + _rope_swap(d * sin_signed)


def _to_kv_lanes(chunk, p, h):
    lane = lax.broadcasted_iota(jnp.int32, chunk.shape, 1)
    src = chunk if p == h else pltpu.roll(chunk, HEAD_DIM, 1)
    return jnp.where((lane >> 6) == h, src, jnp.zeros_like(src))


def _from_kv_lanes(o_a, o_b, h):
    lane = lax.broadcasted_iota(jnp.int32, o_a.shape, 1)
    a = o_a if h == 0 else pltpu.roll(o_a, HEAD_DIM, 1)
    b = o_b if h == 1 else pltpu.roll(o_b, HEAD_DIM, 1)
    return jnp.where(lane < HEAD_DIM, a, b)


def _band_bias(n):
    blk = ATTN_BLOCK
    qi = lax.broadcasted_iota(jnp.int32, (blk, 2 * blk), 0)
    kj = lax.broadcasted_iota(jnp.int32, (blk, 2 * blk), 1)
    seen = (kj > qi) & (kj <= qi + blk) & ((n > 0) | (kj >= blk))
    return jnp.concatenate([jnp.where(seen, 0.0, NEG)] * N_Q_HEADS, axis=0)


def _stack_heads(chunks, h, dtype):
    parts = []
    for g in range(4):
        j = 4 * h + g
        parts.append(_to_kv_lanes(chunks[j // 2], j % 2, h))
    return jnp.concatenate(parts, axis=0).astype(dtype)


def _fused_call(parts, name, n_steps):
    counts = [(len(p["in_specs"]), len(p["out_specs"]), len(p["scratch"])) for p in parts]
    n_in, n_out = sum(c[0] for c in counts), sum(c[1] for c in counts)

    def kernel_fn(*refs):
        i = pl.program_id(0)
        groups, a, b, c = [], 0, n_in, n_in + n_out
        for ci, co, cs in counts:
            groups.append(refs[a:a + ci] + refs[b:b + co] + refs[c:c + cs])
            a, b, c = a + ci, b + co, c + cs
        for p, g in zip(parts, groups):
            p["init"](i, *g)
        for p, g in zip(parts, groups):
            p["body"](i, *g)

    flat = lambda key: [v for p in parts for v in p[key]]
    res = pl.pallas_call(
        kernel_fn, name=name, grid=(n_steps,), in_specs=flat("in_specs"), out_specs=flat("out_specs"),
        out_shape=flat("out_shape"), scratch_shapes=flat("scratch"), compiler_params=_params(("arbitrary",)),
    )(*flat("operands"))
    out, pos = [], 0
    for _, co, _ in counts:
        out.append(res[pos:pos + co])
        pos += co
    return out


def _attn_fwd_part(pa, cos, sin, sinks):
    S = pa.shape[0]
    blk = ATTN_BLOCK
    nb = S // blk

    def body(n, sink_ref, cur_ref, prev_ref, cos_ref, sin_ref, cosp_ref, sinp_ref,
             ya_ref, qr_ref, kr_ref, vb_ref, lse_ref):
        cos_c, sin_c = cos_ref[...], sin_ref[...]
        qch = [_rope(cur_ref[:, c * LANES:(c + 1) * LANES], cos_c, sin_c) * (HEAD_DIM ** -0.5) for c in range(4)]
        for c in range(4):
            qr_ref[:, c * LANES:(c + 1) * LANES] = qch[c].astype(BF16)
        k_cur = _rope(cur_ref[:, 512:640], cos_c, sin_c).astype(BF16)
        k_prev = _rope(prev_ref[:, 0:LANES], cosp_ref[...], sinp_ref[...]).astype(BF16)
        v_cur = cur_ref[:, 640:768].astype(BF16)
        v_prev = prev_ref[:, LANES:2 * LANES].astype(BF16)
        kr_ref[...] = k_cur
        vb_ref[...] = v_cur
        K = jnp.concatenate([k_prev, k_cur], axis=0)
        V = jnp.concatenate([v_prev, v_cur], axis=0)
        lane = lax.broadcasted_iota(jnp.int32, (blk, LANES), 1)
        s = jnp.concatenate([_dot_nt(_stack_heads(qch, h, BF16), K) for h in range(2)], axis=0)
        s = s + _band_bias(n)
        rowmax = jnp.max(s, axis=1, keepdims=True)
        hd = lambda x, j: x[j * blk:(j + 1) * blk]
        m = jnp.concatenate([jnp.maximum(hd(rowmax, j), sink_ref[j]) for j in range(N_Q_HEADS)], axis=0)
        p = jnp.exp(s - m)
        den = jnp.sum(p, axis=1, keepdims=True) \
            + jnp.concatenate([jnp.exp(sink_ref[j] - hd(m, j)) for j in range(N_Q_HEADS)], axis=0)
        pb = (p * _recip(den)).astype(BF16)
        o = jnp.concatenate([_dot(pb[4 * h * blk:4 * (h + 1) * blk], V) for h in range(2)], axis=0)
        lse = m + jnp.log(den)
        outs = [o[j * blk:(j + 1) * blk, :] for j in range(N_Q_HEADS)]
        lse_tile = jnp.zeros((blk, LANES), F32)
        for j in range(N_Q_HEADS):
            lse_tile = jnp.where(lane == j, lse[j * blk:(j + 1) * blk, :], lse_tile)
        for c in range(4):
            ya_ref[:, c * LANES:(c + 1) * LANES] = _from_kv_lanes(outs[2 * c], outs[2 * c + 1], c // 2).astype(BF16)
        lse_ref[...] = lse_tile

    prev = lambda n: jnp.maximum(n - 1, 0)
    sd = lambda w, dt: jax.ShapeDtypeStruct((S, w), dt)
    return dict(
        init=lambda n, *refs: None, body=body, scratch=[], operands=[sinks, pa, pa, cos, sin, cos, sin],
        in_specs=[pl.BlockSpec(memory_space=pltpu.SMEM),
                  pl.BlockSpec((blk, A_W), lambda n: (n, 0)),
                  pl.BlockSpec((blk, 256), lambda n: (prev(n), 2)),
                  pl.BlockSpec((blk, LANES), lambda n: (n, 0)), pl.BlockSpec((blk, LANES), lambda n: (n, 0)),
                  pl.BlockSpec((blk, LANES), lambda n: (prev(n), 0)), pl.BlockSpec((blk, LANES), lambda n: (prev(n), 0))],
        out_specs=[pl.BlockSpec((blk, 512), lambda n: (n, 0)), pl.BlockSpec((blk, 512), lambda n: (n, 0)),
                   pl.BlockSpec((blk, LANES), lambda n: (n, 0)), pl.BlockSpec((blk, LANES), lambda n: (n, 0)),
                   pl.BlockSpec((blk, LANES), lambda n: (n, 0))],
        out_shape=[sd(512, BF16), sd(512, BF16), sd(LANES, BF16), sd(LANES, BF16), sd(LANES, F32)])


def _attn_bwd_part(dya, qr, kr, vb, lse, cos, sin, sinks):
    S = dya.shape[0]
    blk = ATTN_BLOCK
    nb = S // blk

    def init(n, sink_ref, dya_ref, qr_ref, kc_ref, kp_ref, vc_ref, vp_ref, lse_ref, cos_ref, sin_ref, cosp_ref, sinp_ref,
             dq_ref, dkv_ref, last_ref, dsink_ref, ck, cv):
        @pl.when(n == 0)
        def _():
            ck[...] = jnp.zeros_like(ck)
            cv[...] = jnp.zeros_like(cv)
            dsink_ref[...] = jnp.zeros_like(dsink_ref)

    def body(n, sink_ref, dya_ref, qr_ref, kc_ref, kp_ref, vc_ref, vp_ref, lse_ref, cos_ref, sin_ref, cosp_ref, sinp_ref,
             dq_ref, dkv_ref, last_ref, dsink_ref, ck, cv):
        K = jnp.concatenate([kp_ref[...], kc_ref[...]], axis=0)
        V = jnp.concatenate([vp_ref[...], vc_ref[...]], axis=0)
        qch = [qr_ref[:, c * LANES:(c + 1) * LANES] for c in range(4)]
        dch = [dya_ref[:, c * LANES:(c + 1) * LANES] for c in range(4)]
        lse_tile = lse_ref[...]
        lane8 = lax.broadcasted_iota(jnp.int32, (8, LANES), 1)
        grp = lambda x, h: x[4 * h * blk:4 * (h + 1) * blk]
        qs = jnp.concatenate([_stack_heads(qch, h, BF16) for h in range(2)], axis=0)
        dos = jnp.concatenate([_stack_heads(dch, h, BF16) for h in range(2)], axis=0)
        lse_col = jnp.concatenate([lse_tile[:, j:j + 1] for j in range(N_Q_HEADS)], axis=0)
        s = jnp.concatenate([_dot_nt(grp(qs, h), K) for h in range(2)], axis=0)
        p = jnp.exp(s + _band_bias(n) - lse_col)
        dp = jnp.concatenate([_dot_nt(grp(dos, h), V) for h in range(2)], axis=0)
        delta = jnp.sum(p * dp, axis=1, keepdims=True)
        dsb = (p * (dp - delta)).astype(BF16)
        pb = p.astype(BF16)
        dq = jnp.concatenate([_dot(grp(dsb, h), K) for h in range(2)], axis=0)
        dk_acc = _dot_tn(grp(dsb, 0), grp(qs, 0)) + _dot_tn(grp(dsb, 1), grp(qs, 1))
        dv_acc = _dot_tn(grp(pb, 0), grp(dos, 0)) + _dot_tn(grp(pb, 1), grp(dos, 1))
        dqs = [dq[j * blk:(j + 1) * blk, :] for j in range(N_Q_HEADS)]
        dsink = jnp.zeros((8, LANES), F32)
        for j in range(N_Q_HEADS):
            rows = slice(j * blk, (j + 1) * blk)
            ps_delta = jnp.exp(sink_ref[j] - lse_col[rows]) * delta[rows]
            dsink = jnp.where(lane8 == j, dsink - jnp.sum(ps_delta), dsink)
        dsink_ref[...] += dsink
        cos_c, sin_c = cos_ref[...], sin_ref[...]
        for c in range(4):
            dqc = _from_kv_lanes(dqs[2 * c], dqs[2 * c + 1], c // 2) * (HEAD_DIM ** -0.5)
            dq_ref[:, c * LANES:(c + 1) * LANES] = _rope_t(dqc, cos_c, sin_c).astype(BF16)
        dkv_ref[:, 0:LANES] = _rope_t(dk_acc[0:blk, :] + ck[...], cosp_ref[...], sinp_ref[...]).astype(BF16)
        dkv_ref[:, LANES:2 * LANES] = (dv_acc[0:blk, :] + cv[...]).astype(BF16)
        ck[...] = dk_acc[blk:2 * blk, :]
        cv[...] = dv_acc[blk:2 * blk, :]
        last_ref[:, 0:LANES] = _rope_t(dk_acc[blk:2 * blk, :], cos_c, sin_c).astype(BF16)
        last_ref[:, LANES:2 * LANES] = dv_acc[blk:2 * blk, :].astype(BF16)

    prev = lambda n: jnp.maximum(n - 1, 0)
    same = lambda n: n
    bs = lambda w, f: pl.BlockSpec((blk, w), lambda n: (f(n), 0))
    return dict(
        init=init, body=body, operands=[sinks, dya, qr, kr, kr, vb, vb, lse, cos, sin, cos, sin],
        in_specs=[pl.BlockSpec(memory_space=pltpu.SMEM),
                  bs(512, same), bs(512, same), bs(LANES, same), bs(LANES, prev), bs(LANES, same), bs(LANES, prev),
                  bs(LANES, same), bs(LANES, same), bs(LANES, same), bs(LANES, prev), bs(LANES, prev)],
        out_specs=[bs(512, same), bs(256, prev), _full((blk, 256)), _full((8, LANES))],
        out_shape=[jax.ShapeDtypeStruct((S, 512), BF16), jax.ShapeDtypeStruct((S, 256), BF16),
                   jax.ShapeDtypeStruct((blk, 256), BF16), jax.ShapeDtypeStruct((8, LANES), F32)],
        scratch=[pltpu.VMEM((blk, LANES), F32), pltpu.VMEM((blk, LANES), F32)])


def _split3(x):
    hi = x.astype(BF16)
    r1 = x - hi.astype(F32)
    mid = r1.astype(BF16)
    lo = (r1 - mid.astype(F32)).astype(BF16)
    return hi, mid, lo


def _tri_matmul(tri_b, x):
    hi, mid, lo = _split3(x)
    return _dot(tri_b, hi) + _dot(tri_b, mid) + _dot(tri_b, lo)


def _log_sigmoid(x):
    return jnp.minimum(x, 0.0) - jnp.log(1.0 + jnp.exp(-jnp.abs(x)))


def _shift_rows(cur, seam, k, down):
    L = cur.shape[0]
    row8 = lax.broadcasted_iota(jnp.int32, seam.shape, 0)
    if down:
        mixed = jnp.concatenate([cur[:L - 8], jnp.where(row8 >= 8 - k, seam, cur[L - 8:])], axis=0)
        return pltpu.roll(mixed, k, 0)
    mixed = jnp.concatenate([jnp.where(row8 < k, seam, cur[:8]), cur[8:]], axis=0)
    return pltpu.roll(mixed, L - k, 0)


def _conv_fwd(cur, tail, cw_ref):
    z = cw_ref[4:5, :]
    for k in range(3, 0, -1):
        z = z + _shift_rows(cur, tail, k, True) * cw_ref[3 - k:4 - k, :]
    return z + cur * cw_ref[3:4, :]


def _stack(f):
    return jnp.concatenate([f(h) for h in range(MLSTM_HEADS)], axis=0)


def _head(x, h):
    L = x.shape[0] // MLSTM_HEADS
    return x[h * L:(h + 1) * L]


def _mlstm_heads_fwd(qk, cur_ref, gt, b_all, c_prev, nmv, tri, eye):
    L = qk.shape[0]
    HD = MLSTM_HEAD_DIM
    W4 = MLSTM_HEADS * HD
    col2row = lambda x: jnp.sum(jnp.where(eye, x, 0.0), axis=0, keepdims=True)
    b_col = _stack(lambda h: b_all[:, 4 + h:5 + h])
    i_col = _stack(lambda h: gt[:, h:h + 1])
    b_row = _stack(lambda h: jnp.broadcast_to(col2row(b_all[:, 4 + h:5 + h]), (L, L)))
    i_row = _stack(lambda h: jnp.broadcast_to(col2row(gt[:, h:h + 1]), (L, L)))
    bl = _stack(lambda h: jnp.broadcast_to(b_all[L - 1:L, 4 + h:5 + h], (L, 1)))
    m_prev = _stack(lambda h: jnp.broadcast_to(nmv[4 + h:5 + h, 0:1], (L, 1)))
    n_prev = _stack(lambda h: jnp.broadcast_to(nmv[h:h + 1, :], (L, HD)))
    tri4 = jnp.concatenate([tri] * MLSTM_HEADS, axis=0)
    Dm = jnp.where(tri4, b_col - b_row + i_row, NEG)
    inter = b_col + m_prev
    m_t = jnp.maximum(inter, jnp.max(Dm, axis=1, keepdims=True))
    W = jnp.exp(Dm - m_t)
    e_t = jnp.exp(inter - m_t)
    q = _stack(lambda h: qk[:, h * HD:(h + 1) * HD])
    k = _stack(lambda h: qk[:, W4 + h * HD:W4 + (h + 1) * HD]) * (HD ** -0.5)
    v = _stack(lambda h: cur_ref[:, 2 * W4 + h * HD:2 * W4 + (h + 1) * HD])
    qb, kb, vb = q.astype(BF16), k.astype(BF16), v.astype(BF16)
    Sc = _stack(lambda h: _dot_nt(_head(qb, h), _head(kb, h))) * W
    Scb = Sc.astype(BF16)
    cb = [c.astype(BF16) for c in c_prev]
    P1 = _stack(lambda h: _dot(_head(qb, h), cb[h]))
    num = _stack(lambda h: _dot(_head(Scb, h), _head(vb, h))) + e_t * P1
    qn = jnp.sum(q * n_prev, axis=1, keepdims=True)
    den = jnp.sum(Sc, axis=1, keepdims=True) + e_t * qn
    floor = jnp.exp(-m_t)
    inv_g = _recip(jnp.maximum(jnp.abs(den), floor))
    hv = num * inv_g
    a_col = bl - b_col + i_col
    a_max = _stack(lambda h: jnp.broadcast_to(jnp.max(_head(a_col, h), axis=0, keepdims=True), (L, 1)))
    m_new = jnp.maximum(bl + m_prev, a_max)
    dec = jnp.exp(bl + m_prev - m_new)
    u_col = jnp.exp(a_col - m_new)
    return dict(W=W, e_t=e_t, q=q, k=k, v=v, qb=qb, kb=kb, vb=vb, cb=cb, Sc=Sc, Scb=Scb, P1=P1, qn=qn, den=den,
                floor=floor, inv_g=inv_g, hv=hv, n_prev=n_prev, m_new=m_new, dec=dec, u_col=u_col)


def _mlstm_fwd_part(pm, pif, cw, sv):
    S = pm.shape[0]
    L = MLSTM_CHUNK
    nc = S // L
    HD = MLSTM_HEAD_DIM
    W4 = MLSTM_HEADS * HD

    def init(c, cur_ref, pif_ref, cw_ref, sv_ref, ym_ref, z_ref, cst_ref, nst_ref, C, nm, tail):
        @pl.when(c == 0)
        def _():
            C[...] = jnp.zeros_like(C)
            nm[...] = jnp.zeros_like(nm)
            tail[...] = jnp.zeros_like(tail)

    def body(c, cur_ref, pif_ref, cw_ref, sv_ref, ym_ref, z_ref, cst_ref, nst_ref, C, nm, tail):
        z = _conv_fwd(cur_ref[:, 0:2 * W4], tail[...], cw_ref)
        tail[...] = cur_ref[L - 8:L, 0:2 * W4]
        z_ref[...] = z
        qk = z * _sigmoid(z)
        gt = pif_ref[...] + sv_ref[1:2, 0:LANES]
        r_i = lax.broadcasted_iota(jnp.int32, (L, L), 0)
        c_i = lax.broadcasted_iota(jnp.int32, (L, L), 1)
        tri = c_i <= r_i
        eye = c_i == r_i
        b_all = _tri_matmul(tri.astype(BF16), _log_sigmoid(gt))
        nmv = nm[...]
        nst_ref[0] = nmv
        c_prev = [C[h] for h in range(MLSTM_HEADS)]
        f = _mlstm_heads_fwd(qk, cur_ref, gt, b_all, c_prev, nmv, tri, eye)
        hv = f["hv"]
        xc = hv - _rowmean(hv)
        hhat = xc * lax.rsqrt(_rowmean(xc * xc) + NORM_EPS)
        so = _sigmoid(_stack(lambda h: cur_ref[:, 3 * W4 + h * HD:3 * W4 + (h + 1) * HD]))
        wn = _stack(lambda h: jnp.broadcast_to(sv_ref[0:1, h * HD:(h + 1) * HD], (L, HD)))
        y = (so * hhat * wn).astype(BF16)
        kw = f["k"] * f["u_col"]
        kwb = kw.astype(BF16)
        n_new, m_new = [], []
        for h in range(MLSTM_HEADS):
            cst_ref[0, h] = c_prev[h]
            ym_ref[:, h * HD:(h + 1) * HD] = _head(y, h)
            dec = f["dec"][h * L:h * L + 1, :]
            C[h] = dec * c_prev[h] + _dot_tn(_head(kwb, h), _head(f["vb"], h))
            n_new.append(dec * nmv[h:h + 1, :] + _colsum(_head(kw, h)))
            m_new.append(jnp.broadcast_to(f["m_new"][h * L:h * L + 1, :], (1, LANES)))
        nm[...] = jnp.concatenate(n_new + m_new, axis=0)

    return dict(
        init=init, body=body, operands=[pm, pif, cw, sv],
        in_specs=[pl.BlockSpec((L, M_W), lambda c: (c, 0)),
                  pl.BlockSpec((L, IF_W), lambda c: (c, 0)), _full(cw.shape), _full(sv.shape)],
        out_specs=[pl.BlockSpec((L, W4), lambda c: (c, 0)), pl.BlockSpec((L, 2 * W4), lambda c: (c, 0)),
                   pl.BlockSpec((1, MLSTM_HEADS, HD, HD), lambda c: (c, 0, 0, 0)),
                   pl.BlockSpec((1, 8, LANES), lambda c: (c, 0, 0))],
        out_shape=[jax.ShapeDtypeStruct((S, W4), BF16), jax.ShapeDtypeStruct((S, 2 * W4), F32),
                   jax.ShapeDtypeStruct((nc, MLSTM_HEADS, HD, HD), F32), jax.ShapeDtypeStruct((nc, 8, LANES), F32)],
        scratch=[pltpu.VMEM((MLSTM_HEADS, HD, HD), F32), pltpu.VMEM((8, LANES), F32), pltpu.VMEM((8, 2 * W4), F32)])


def _mlstm_bwd_part(pm, zc, pif, cw, sv, dym, cst, nst):
    S = pm.shape[0]
    L = MLSTM_CHUNK
    nc = S // L
    HD = MLSTM_HEAD_DIM
    W4 = MLSTM_HEADS * HD

    def init(r, cur_ref, z_ref, pif_ref, cw_ref, sv_ref, dym_ref, cst_ref, nst_ref,
             dm_ref, dif_ref, dcw_ref, dsv_ref, dC, dn, dz_next, dqk):
        @pl.when(r == 0)
        def _():
            dC[...] = jnp.zeros_like(dC)
            dn[...] = jnp.zeros_like(dn)
            dz_next[...] = jnp.zeros_like(dz_next)
            dcw_ref[...] = jnp.zeros_like(dcw_ref)
            dsv_ref[...] = jnp.zeros_like(dsv_ref)

    def body(r, cur_ref, z_ref, pif_ref, cw_ref, sv_ref, dym_ref, cst_ref, nst_ref,
             dm_ref, dif_ref, dcw_ref, dsv_ref, dC, dn, dz_next, dqk):
        z = z_ref[...]
        sgz = _sigmoid(z)
        qk = z * sgz
        gt = pif_ref[...] + sv_ref[1:2, 0:LANES]
        r_i = lax.broadcasted_iota(jnp.int32, (L, L), 0)
        c_i = lax.broadcasted_iota(jnp.int32, (L, L), 1)
        tri = c_i <= r_i
        eye = c_i == r_i
        b_all = _tri_matmul(tri.astype(BF16), _log_sigmoid(gt))
        lane = lax.broadcasted_iota(jnp.int32, (L, LANES), 1)
        rowl = lax.broadcasted_iota(jnp.int32, (L, 1), 0)
        nmv = nst_ref[0]
        heads = range(MLSTM_HEADS)
        c_prev = [cst_ref[0, h] for h in heads]
        f = _mlstm_heads_fwd(qk, cur_ref, gt, b_all, c_prev, nmv, tri, eye)
        hv, inv_g, den, e_t, u_col, n_prev = f["hv"], f["inv_g"], f["den"], f["e_t"], f["u_col"], f["n_prev"]
        q, k, v, qb, kb, vb, Sc, Scb, W = f["q"], f["k"], f["v"], f["qb"], f["kb"], f["vb"], f["Sc"], f["Scb"], f["W"]
        xc = hv - _rowmean(hv)
        rstd = lax.rsqrt(_rowmean(xc * xc) + NORM_EPS)
        hhat = xc * rstd
        wn = _stack(lambda h: jnp.broadcast_to(sv_ref[0:1, h * HD:(h + 1) * HD], (L, HD)))
        so = _sigmoid(_stack(lambda h: cur_ref[:, 3 * W4 + h * HD:3 * W4 + (h + 1) * HD]))
        dy = _stack(lambda h: dym_ref[:, h * HD:(h + 1) * HD])
        d_o = (dy * hhat * wn * (so * (1.0 - so))).astype(BF16)
        dln = dy * so
        dwn = dln * hhat
        dhhat = dln * wn
        dh = rstd * (dhhat - _rowmean(dhhat) - hhat * _rowmean(dhhat * hhat))
        dnum = dh * inv_g
        active = jnp.abs(den) > f["floor"]
        dden = jnp.where(active, -jnp.sum(dh * hv, axis=1, keepdims=True) * inv_g * jnp.where(den >= 0.0, 1.0, -1.0), 0.0)
        dnumb = dnum.astype(BF16)
        dSc = _stack(lambda h: _dot_nt(_head(dnumb, h), _head(vb, h))) + dden
        dA = (dSc * W).astype(BF16)
        G = dSc * Sc
        Gr = jnp.sum(G, axis=1, keepdims=True)
        Gc = _stack(lambda h: jnp.sum(jnp.where(eye, _colsum(_head(G, h)), 0.0), axis=1, keepdims=True))
        dCn = [dC[h] for h in heads]
        dCnb = [d.astype(BF16) for d in dCn]
        dnv = dn[...]
        dn_new = _stack(lambda h: jnp.broadcast_to(dnv[h:h + 1, :], (L, HD)))
        kdC = _stack(lambda h: _dot(_head(kb, h), dCnb[h]))
        vdC = _stack(lambda h: _dot_nt(_head(vb, h), dCnb[h]))
        dv = (_stack(lambda h: _dot_tn(_head(Scb, h), _head(dnumb, h))) + u_col * kdC).astype(BF16)
        dq = _stack(lambda h: _dot(_head(dA, h), _head(kb, h))) \
            + e_t * _stack(lambda h: _dot_nt(_head(dnumb, h), f["cb"][h])) + (e_t * dden) * n_prev
        dk = (_stack(lambda h: _dot_tn(_head(dA, h), _head(qb, h))) + u_col * (vdC + dn_new)) * (HD ** -0.5)
        E = (jnp.sum(f["P1"] * dnum, axis=1, keepdims=True) + dden * f["qn"]) * e_t
        U = (jnp.sum(kdC * v, axis=1, keepdims=True) + jnp.sum(k * dn_new, axis=1, keepdims=True)) * u_col
        qe = (q * e_t).astype(BF16)
        qd = (e_t * dden) * q
        di = Gc + U
        db = Gr + E - Gc - U
        di_tile = jnp.zeros((L, LANES), F32)
        db_tile = jnp.zeros((L, LANES), F32)
        dn_rows = []
        for h in heads:
            dec = f["dec"][h * L:h * L + 1, :]
            ddec = jnp.sum(dCn[h] * c_prev[h]) + jnp.sum(dnv[h:h + 1, :] * nmv[h:h + 1, :])
            dbl = ddec * dec + jnp.sum(_head(U, h), axis=0, keepdims=True)
            di_tile = jnp.where(lane == h, _head(di, h), di_tile)
            db_tile = jnp.where(lane == 4 + h, _head(db, h) + jnp.where(rowl == L - 1, dbl, 0.0), db_tile)
            dC[h] = dec * dCn[h] + _dot_tn(_head(qe, h), _head(dnumb, h))
            dn_rows.append(dec * dnv[h:h + 1, :] + _colsum(_head(qd, h)))
            dsv_ref[0:1, h * HD:(h + 1) * HD] += _colsum(_head(dwn, h))
            dqk[:, h * HD:(h + 1) * HD] = _head(dq, h)
            dqk[:, W4 + h * HD:W4 + (h + 1) * HD] = _head(dk, h)
            dm_ref[:, 2 * W4 + h * HD:2 * W4 + (h + 1) * HD] = _head(dv, h)
            dm_ref[:, 3 * W4 + h * HD:3 * W4 + (h + 1) * HD] = _head(d_o, h)
        dn[...] = jnp.concatenate(dn_rows + [jnp.zeros((8 - MLSTM_HEADS, LANES), F32)], axis=0)
        dlf = _tri_matmul((r_i <= c_i).astype(BF16), db_tile)
        dif = jnp.where(lane < 4, di_tile, jnp.where(lane < 8, dlf * (1.0 - _sigmoid(gt)), 0.0))
        dif_ref[...] = dif.astype(BF16)
        dsv_ref[1:2, 0:LANES] += _colsum(dif)
        dz = dqk[...] * (sgz * (1.0 + z * (1.0 - sgz)))
        dcw_ref[4:5, :] += _colsum(dz)
        u = cur_ref[:, 0:2 * W4]
        du_in = dz * cw_ref[3:4, :]
        dcw_ref[3:4, :] += _colsum(dz * u)
        for k in range(1, 4):
            up = _shift_rows(dz, dz_next[...], k, False)
            dcw_ref[3 - k:4 - k, :] += _colsum(up * u)
            du_in = du_in + up * cw_ref[3 - k:4 - k, :]
        dz_next[...] = dz[0:8, :]
        dm_ref[:, 0:2 * W4] = du_in.astype(BF16)

    cidx = lambda r: nc - 1 - r
    return dict(
        init=init, body=body, operands=[pm, zc, pif, cw, sv, dym, cst, nst],
        in_specs=[pl.BlockSpec((L, M_W), lambda r: (cidx(r), 0)), pl.BlockSpec((L, 2 * W4), lambda r: (cidx(r), 0)),
                  pl.BlockSpec((L, IF_W), lambda r: (cidx(r), 0)), _full(cw.shape), _full(sv.shape),
                  pl.BlockSpec((L, W4), lambda r: (cidx(r), 0)),
                  pl.BlockSpec((1, MLSTM_HEADS, HD, HD), lambda r: (cidx(r), 0, 0, 0)),
                  pl.BlockSpec((1, 8, LANES), lambda r: (cidx(r), 0, 0))],
        out_specs=[pl.BlockSpec((L, M_W), lambda r: (cidx(r), 0)), pl.BlockSpec((L, IF_W), lambda r: (cidx(r), 0)),
                   _full((8, 2 * W4)), _full((8, W4))],
        out_shape=[jax.ShapeDtypeStruct((S, M_W), BF16), jax.ShapeDtypeStruct((S, IF_W), BF16),
                   jax.ShapeDtypeStruct((8, 2 * W4), F32), jax.ShapeDtypeStruct((8, W4), F32)],
        scratch=[pltpu.VMEM((MLSTM_HEADS, HD, HD), F32), pltpu.VMEM((8, LANES), F32),
                 pltpu.VMEM((8, 2 * W4), F32), pltpu.VMEM((L, 2 * W4), F32)])


def _rope_tables(positions):
    half = HEAD_DIM // 2
    inv_freq = ROPE_THETA ** (-2.0 * jnp.arange(half, dtype=F32) / HEAD_DIM)
    ang = positions.astype(F32)[:, None] * inv_freq
    cos = jnp.tile(jnp.cos(ang), (1, LANES // half))
    sign = jnp.tile(jnp.concatenate([-jnp.ones((half,), F32), jnp.ones((half,), F32)]), LANES // HEAD_DIM)
    sin = jnp.tile(jnp.sin(ang), (1, LANES // half)) * sign
    return cos, sin


def _local_step(x, tgt, positions, mod, gains, w_cat, w_ba, w_bm, w_out, w_gate, w_up, w_down,
                conv_w, conv_b, b_if, sinks, norm_w):
    t = _tables(mod, gains, conv_w, conv_b, b_if, norm_w, positions)
    a = _mixer_fwd(x, t, sinks, w_cat)
    b = _ffn_part(x, tgt, t, a, w_ba, w_bm, w_out, w_gate, w_up, w_down)
    c = _mixer_bwd(b["dx1"], t, a, b, sinks, w_ba, w_bm, w_out)
    grad_x, acc_p = _pre_bwd(c["dproj"], x, b["dx1"], t["vecs"], w_cat)
    big = dict(w_cat=c["g_w_cat"], w_ba=c["g_w_ba"], w_bm=c["g_w_bm"], w_out=c["g_w_out"], w_gate=b["g_w_gate"],
               w_up=b["g_w_up"], w_down=b["g_w_down"])
    return b["loss"], grad_x, big, _small_grads(acc_p, b, c)


def _tables(mod, gains, conv_w, conv_b, b_if, norm_w, positions):
    cos, sin = _rope_tables(positions)
    return dict(
        vecs=jnp.concatenate([mod, gains, jnp.zeros((6, D_MODEL), F32)], axis=0),
        cw=jnp.concatenate([conv_w, conv_b.reshape(1, -1), jnp.zeros((3, 2 * 512), F32)], axis=0),
        sv=jnp.zeros((8, 512), F32).at[0].set(norm_w).at[1, 0:8].set(b_if), cos=cos, sin=sin)


def _mixer_fwd(x, t, sinks, w_cat):
    h, pa, pm, pif, pg = _pre_proj(x, t["vecs"], w_cat)
    n_blk = x.shape[0] // ATTN_BLOCK
    (ya, qr, kr, vb, lse), = _fused_call([_attn_fwd_part(pa, t["cos"], t["sin"], sinks)], "attn_fwd", n_blk)
    (ym, zc, cst, nst), = _fused_call([_mlstm_fwd_part(pm, pif, t["cw"], t["sv"])], "mlstm_fwd", n_blk)
    return dict(h=h, pm=pm, pif=pif, pg=pg, ya=ya, qr=qr, kr=kr, vb=vb, lse=lse, ym=ym, zc=zc, cst=cst, nst=nst)


def _ffn_part(x, tgt, t, a, w_ba, w_bm, w_out, w_gate, w_up, w_down):
    x1, merged, mix, pba, pbm = _mix_fwd(x, a["ya"], a["ym"], a["pg"], t["vecs"], w_ba, w_bm, w_out)
    dx1, h2, hid, da, du, dff, acc_f, loss = _ffn_fwd_bwd(x1, tgt, t["vecs"], w_gate, w_up, w_down)
    return dict(merged=merged, mix=mix, pba=pba, pbm=pbm, dx1=dx1, acc_f=acc_f, loss=loss[0, 0],
                g_w_gate=_matmul_tn(da, h2, 1024, "dw_ffn_gate"),
                g_w_up=_matmul_tn(du, h2, 1024, "dw_ffn_up"),
                g_w_down=_matmul_tn(hid, dff, 1024, "dw_ffn_down"))


def _mixer_bwd(dx1, t, a, b, sinks, w_ba, w_bm, w_out):
    dmix, dpa, dpb, dg, dya, dym, acc_m = _mix_bwd(dx1, b["mix"], b["pba"], b["pbm"], a["pg"], t["vecs"], w_ba, w_bm, w_out)
    g_w_out = _matmul_tn(b["merged"], dmix, 1024, "dw_out")
    g_w_ba = _matmul_tn(a["ya"], dpa, 1024, "dw_branch_attn")
    g_w_bm = _matmul_tn(a["ym"], dpb, 1024, "dw_branch_mlstm")
    n_blk = dx1.shape[0] // ATTN_BLOCK
    (dq, dkv, dkv_last, dsink), = _fused_call(
        [_attn_bwd_part(dya, a["qr"], a["kr"], a["vb"], a["lse"], t["cos"], t["sin"], sinks)], "attn_bwd", n_blk)
    (dm, dif, dcw, dsv), = _fused_call(
        [_mlstm_bwd_part(a["pm"], a["zc"], a["pif"], t["cw"], t["sv"], dym, a["cst"], a["nst"])], "mlstm_bwd", n_blk)
    dkv = lax.dynamic_update_slice(dkv, dkv_last, (dkv.shape[0] - ATTN_BLOCK, 0))
    dproj = jnp.concatenate([dq, dkv, dm, dif, dg], axis=1)
    return dict(dproj=dproj, g_w_cat=_matmul_tn(a["h"], dproj, 1664, "dw_in"), g_w_out=g_w_out, g_w_ba=g_w_ba,
                g_w_bm=g_w_bm, acc_m=acc_m, dsink=dsink, dcw=dcw, dsv=dsv)


def _small_grads(acc_p, b, c):
    acc_f, acc_m = b["acc_f"], c["acc_m"]
    dmod = jnp.stack([acc_p[1], acc_p[0], acc_m[0], acc_f[3], acc_f[2], acc_f[0]])
    dgains = jnp.stack([acc_p[2], acc_m[1], acc_f[4], acc_f[1]])
    return dict(dmod=dmod, dgains=dgains, dconv_w=c["dcw"][0:4], dconv_b=c["dcw"][4], db_if=c["dsv"][1, 0:8],
                dsinks=c["dsink"][0, 0:8], dnorm_w=c["dsv"][0])


MESH_ID = pl.DeviceIdType.MESH


def _mesh_pos():
    return lax.axis_index("x"), lax.axis_index("y"), lax.axis_index("c")


def _flip(v, bit):
    return 1 - v if bit else v


def _relations():
    return [((r >> 2) & 1, (r >> 1) & 1, r & 1) for r in range(1, N_DEV)]


def _small_exchange(p, gather, name):
    V = p.shape[-1]

    def body(p_ref, out_ref, send_sems, recv_sems):
        x, y, c = _mesh_pos()
        me = 4 * x + 2 * y + c
        out_ref[me] = p_ref[...] if gather else p_ref[me]
        peers = []
        for dx, dy, dc in _relations():
            px, py, pc = _flip(x, dx), _flip(y, dy), _flip(c, dc)
            peers.append(((px, py, pc), 4 * px + 2 * py + pc))

        def copy(k, landing):
            peer, pid = peers[k]
            return pltpu.make_async_remote_copy(
                src_ref=p_ref if gather else p_ref.at[pid], dst_ref=out_ref.at[landing],
                send_sem=send_sems.at[k], recv_sem=recv_sems.at[k], device_id=peer, device_id_type=MESH_ID)

        sends = [copy(k, me) for k in range(N_DEV - 1)]
        for cp in sends:
            cp.start()
        for k in range(N_DEV - 1):
            copy(k, peers[k][1]).wait_recv()
        for cp in sends:
            cp.wait_send()

    vm = pl.BlockSpec(memory_space=pltpu.VMEM)
    return pl.pallas_call(
        body, name=name, in_specs=[vm], out_specs=vm,
        out_shape=jax.ShapeDtypeStruct((N_DEV, 8, V), F32),
        scratch_shapes=[pltpu.SemaphoreType.DMA((N_DEV - 1,)), pltpu.SemaphoreType.DMA((N_DEV - 1,))],
        compiler_params=pltpu.CompilerParams(vmem_limit_bytes=VMEM_LIMIT),
    )(p)


HBM_SPEC = pl.BlockSpec(memory_space=pltpu.HBM)
SEM_SPEC = pl.BlockSpec(memory_space=pltpu.SEMAPHORE)


def _peers(x, y, c):
    out = []
    for dx, dy, dc in _relations():
        px, py, pc = _flip(x, dx), _flip(y, dy), _flip(c, dc)
        out.append(((px, py, pc), 4 * px + 2 * py + pc))
    return out


def _exchange_start(arrs, gather, after, name):
    n = len(arrs)
    me_out = 4 * lax.axis_index("x") + 2 * lax.axis_index("y") + lax.axis_index("c")
    lands = []
    for a in arrs:
        own = a[None] if gather else lax.dynamic_index_in_dim(a, me_out, 0, keepdims=True)
        empty = lax.empty(((N_DEV,) + a.shape) if gather else a.shape, a.dtype)
        lands.append(lax.dynamic_update_index_in_dim(empty, own, me_out, 0))

    def body(*refs):
        a_refs, l_refs = refs[:n], refs[n:2 * n]
        send_sems, recv_sems = refs[2 * n + 1], refs[2 * n + 2]
        token = refs[4 * n + 3]
        x, y, c = _mesh_pos()
        me = 4 * x + 2 * y + c
        for a in range(n):
            for k, (peer, pid) in enumerate(_peers(x, y, c)):
                pltpu.make_async_remote_copy(
                    src_ref=a_refs[a] if gather else a_refs[a].at[pid], dst_ref=l_refs[a].at[me],
                    send_sem=send_sems.at[a * (N_DEV - 1) + k], recv_sem=recv_sems.at[a * (N_DEV - 1) + k],
                    device_id=peer, device_id_type=MESH_ID).start()
        token[...] = jnp.zeros_like(token)

    sem = pltpu.SemaphoreType.DMA((n * (N_DEV - 1),))
    hbm = lambda a: pltpu.with_memory_space_constraint(a, pltpu.HBM)
    res = pl.pallas_call(
        body, name=name,
        out_shape=(sem, sem, *[pltpu.HBM(a.shape, a.dtype) for a in arrs], *[pltpu.HBM(l.shape, l.dtype) for l in lands],
                   jax.ShapeDtypeStruct((8, LANES), F32)),
        in_specs=[HBM_SPEC] * (2 * n) + [pl.BlockSpec(memory_space=pl.ANY)],
        out_specs=(SEM_SPEC, SEM_SPEC, *[HBM_SPEC] * (2 * n), pl.BlockSpec(memory_space=pltpu.VMEM)),
        input_output_aliases={i: 2 + i for i in range(2 * n)},
        compiler_params=pltpu.CompilerParams(has_side_effects=pltpu.SideEffectType.DATAFLOW_SIDE_EFFECTING),
    )(*[hbm(a) for a in arrs], *[hbm(l) for l in lands], after)
    return dict(sems=res[0:2], arrs=res[2:2 + n], lands=res[2 + n:2 + 2 * n], token=res[2 + 2 * n], gather=gather)


def _exchange_wait(st, after, name):
    n = len(st["arrs"])
    gather = st["gather"]

    def body(*refs):
        a_refs, l_refs = refs[:n], refs[n:2 * n]
        send_sems, recv_sems = refs[2 * n], refs[2 * n + 1]
        x, y, c = _mesh_pos()
        for a in range(n):
            for k, (peer, pid) in enumerate(_peers(x, y, c)):
                cp = pltpu.make_async_remote_copy(
                    src_ref=a_refs[a] if gather else a_refs[a].at[pid], dst_ref=l_refs[a].at[pid],
                    send_sem=send_sems.at[a * (N_DEV - 1) + k], recv_sem=recv_sems.at[a * (N_DEV - 1) + k],
                    device_id=peer, device_id_type=MESH_ID)
                cp.wait_send()
                cp.wait_recv()

    both = list(st["arrs"]) + list(st["lands"])
    res = pl.pallas_call(
        body, name=name, out_shape=[pltpu.HBM(a.shape, a.dtype) for a in both],
        in_specs=[HBM_SPEC] * (2 * n) + [SEM_SPEC, SEM_SPEC, pl.BlockSpec(memory_space=pl.ANY)],
        out_specs=[HBM_SPEC] * (2 * n), input_output_aliases={i: i for i in range(2 * n)},
        compiler_params=pltpu.CompilerParams(has_side_effects=pltpu.SideEffectType.DATAFLOW_SIDE_EFFECTING),
    )(*both, *st["sems"], after)
    return res[n:2 * n]


def _after(x, token):
    return lax.optimization_barrier((x, token))[0]


def _all_gather_hbm(shards):
    n = len(shards)

    def body(*refs):
        p_refs, out_refs = refs[:n], refs[n:2 * n]
        send_sems, recv_sems, local_sems = refs[2 * n:]
        x, y, c = _mesh_pos()
        me, sibling = (x, y, c), (x, y, 1 - c)
        chips = [(1 - x, y), (x, 1 - y), (1 - x, 1 - y)]

        def copy(a, k, block, to, own=False):
            slot = out_refs[a].at[4 * block[0] + 2 * block[1] + block[2]]
            return pltpu.make_async_remote_copy(
                src_ref=p_refs[a] if own else slot, dst_ref=slot,
                send_sem=send_sems.at[a, k], recv_sem=recv_sems.at[a, k], device_id=to, device_id_type=MESH_ID)

        mine = [pltpu.make_async_copy(p_refs[a], out_refs[a].at[4 * x + 2 * y + c], local_sems.at[a]) for a in range(n)]
        for cp in mine:
            cp.start()
        first = []
        for a in range(n):
            first.append(copy(a, 0, me, sibling, own=True))
            first += [copy(a, 1 + j, me, (*chip, c), own=True) for j, chip in enumerate(chips)]
        for cp in first:
            cp.start()
        passed = []
        for j, chip in enumerate(chips):
            for a in range(n):
                copy(a, 1 + j, (*chip, c), me).wait_recv()
                passed.append(copy(a, 4 + j, (*chip, c), sibling))
                passed[-1].start()
        for a in range(n):
            copy(a, 0, sibling, me).wait_recv()
            for j, chip in enumerate(chips):
                copy(a, 4 + j, (*chip, 1 - c), me).wait_recv()
        for cp in first + passed:
            cp.wait_send()
        for cp in mine:
            cp.wait()

    hbm = pl.BlockSpec(memory_space=pl.ANY)
    return pl.pallas_call(
        body, name="gather_weights", in_specs=[hbm] * n, out_specs=[hbm] * n,
        out_shape=[jax.ShapeDtypeStruct((N_DEV,) + s.shape, s.dtype) for s in shards],
        scratch_shapes=[pltpu.SemaphoreType.DMA((n, N_DEV - 1)), pltpu.SemaphoreType.DMA((n, N_DEV - 1)),
                        pltpu.SemaphoreType.DMA((n,))],
    )(*shards)


def _adamw(w, g, m, v):
    m2 = ADAM_B1 * m + (1.0 - ADAM_B1) * g
    v2 = ADAM_B2 * v + (1.0 - ADAM_B2) * (g * g)
    m_hat = m2 / (1.0 - ADAM_B1 ** ADAM_STEP)
    v_hat = v2 / (1.0 - ADAM_B2 ** ADAM_STEP)
    delta = -ADAM_LR * (m_hat / (jnp.sqrt(v_hat) + ADAM_EPS) + ADAM_WD * w)
    return delta, m2, v2


def _mod_partial(cmat, w_shard, b_shard):
    def body(c_ref, w_ref, b_ref, o_ref):
        o_ref[...] = _dot(c_ref[...].astype(BF16), w_ref[...].astype(BF16)) + b_ref[...]

    return pl.pallas_call(
        body, name="mod_partial", out_shape=jax.ShapeDtypeStruct((N_DEV, w_shard.shape[1]), F32),
        compiler_params=_params(),
    )(cmat, w_shard, b_shard)


def _adamw_w_ada(cmat, dmod_cols, w, m, v):
    def body(c_ref, d_ref, w_ref, m_ref, v_ref, g_ref, dl_ref, m2_ref, v2_ref):
        g = _dot_tn(c_ref[...].astype(BF16), d_ref[...].astype(BF16))
        g_ref[...] = g
        dl_ref[...], m2_ref[...], v2_ref[...] = _adamw(w_ref[...], g, m_ref[...], v_ref[...])

    return pl.pallas_call(
        body, name="adamw_w_ada", out_shape=[jax.ShapeDtypeStruct(w.shape, F32)] * 4,
        compiler_params=_params(),
    )(cmat, dmod_cols, w, m, v)


def _adamw_small(gathered, w, m, v):
    def body(g_ref, w_ref, m_ref, v_ref, go_ref, dl_ref, m2_ref, v2_ref):
        g = g_ref[0]
        for k in range(1, N_DEV):
            g = g + g_ref[k]
        go_ref[...] = g
        dl_ref[...], m2_ref[...], v2_ref[...] = _adamw(w_ref[...], g, m_ref[...], v_ref[...])

    return pl.pallas_call(
        body, name="adamw_small", out_shape=[jax.ShapeDtypeStruct(w.shape, F32)] * 4,
        compiler_params=_params(),
    )(gathered, w, m, v)


def _row_tile(rows):
    return rows // 4 if rows >= 512 else rows


def _sum_partials(r_ref):
    g = r_ref[0].astype(F32)
    for k in range(1, N_DEV):
        g = g + r_ref[k].astype(F32)
    return g


def _adamw_sum(recv, w, m, v, name):
    r, cdim = w.shape
    tr = _row_tile(r)

    def body(r_ref, w_ref, m_ref, v_ref, g_ref, dl_ref, m2_ref, v2_ref):
        g = _sum_partials(r_ref)
        g_ref[...] = g
        dl_ref[...], m2_ref[...], v2_ref[...] = _adamw(w_ref[...], g, m_ref[...], v_ref[...])

    row = pl.BlockSpec((tr, cdim), lambda i: (i, 0))
    return pl.pallas_call(
        body, name=name, grid=(r // tr,),
        in_specs=[pl.BlockSpec((N_DEV, tr, cdim), lambda i: (0, i, 0)), row, row, row],
        out_specs=[row] * 4, out_shape=[jax.ShapeDtypeStruct((r, cdim), F32)] * 4,
        compiler_params=_params(("parallel",)),
    )(recv, w, m, v)


def _sum8(recv, name):
    _, r, cdim = recv.shape
    tr = _row_tile(r)

    def body(r_ref, g_ref):
        g_ref[...] = _sum_partials(r_ref)

    return pl.pallas_call(
        body, name=name, grid=(r // tr,),
        in_specs=[pl.BlockSpec((N_DEV, tr, cdim), lambda i: (0, i, 0))],
        out_specs=pl.BlockSpec((tr, cdim), lambda i: (i, 0)), out_shape=jax.ShapeDtypeStruct((r, cdim), F32),
        compiler_params=_params(("parallel",)),
    )(recv)


def _adamw_plain(g, w, m, v, name):
    r, cdim = w.shape
    tr = _row_tile(r)

    def body(g_ref, w_ref, m_ref, v_ref, dl_ref, m2_ref, v2_ref):
        dl_ref[...], m2_ref[...], v2_ref[...] = _adamw(w_ref[...], g_ref[...], m_ref[...], v_ref[...])

    row = pl.BlockSpec((tr, cdim), lambda i: (i, 0))
    return pl.pallas_call(
        body, name=name, grid=(r // tr,), in_specs=[row] * 4, out_specs=[row] * 3,
        out_shape=[jax.ShapeDtypeStruct((r, cdim), F32)] * 3,
        compiler_params=_params(("parallel",)),
    )(g, w, m, v)


IN_SHARD = 609
IN_SHARD_PAD = 640
IF_AT = A_W + M_W


def _regrouped(u):
    return u if u < IF_AT + 8 else u + (IF_W - 8)


def _selection(k, rows, row0, transpose):
    shape = (rows, IN_SHARD_PAD) if transpose else (IN_SHARD_PAD, rows)
    l = lax.broadcasted_iota(jnp.int32, shape, 1 if transpose else 0)
    r = lax.broadcasted_iota(jnp.int32, shape, 0 if transpose else 1) + row0
    u = l + IN_SHARD * k
    ru = u + jnp.where(u >= IF_AT + 8, IF_W - 8, 0)
    return ((ru == r) & (l < IN_SHARD)).astype(BF16)


def _regroup_w_in(g):
    def body(g_ref, o_ref):
        for cb in range(CAT_W // LANES):
            r0 = cb * LANES
            acc = jnp.zeros((D_MODEL, LANES), F32)
            for k in range(N_DEV):
                lo, hi = _regrouped(IN_SHARD * k), _regrouped(IN_SHARD * k + IN_SHARD - 1)
                if hi >= r0 and lo < r0 + LANES:
                    acc = acc + _dot(g_ref[k], _selection(k, LANES, r0, False))
            o_ref[:, r0:r0 + LANES] = acc.astype(BF16)

    return pl.pallas_call(
        body, name="regroup_w_in", out_shape=jax.ShapeDtypeStruct((D_MODEL, CAT_W), BF16),
        compiler_params=_params(),
    )(g)


def _ungroup_w_in(g_cat):
    def body(g_ref, o_ref):
        for k in range(N_DEV):
            lo, hi = _regrouped(IN_SHARD * k), _regrouped(IN_SHARD * k + IN_SHARD - 1)
            w0, w1 = lo // LANES * LANES, (hi // LANES + 1) * LANES
            o_ref[k] = _dot(g_ref[:, w0:w1], _selection(k, w1 - w0, w0, True)).astype(BF16)

    return pl.pallas_call(
        body, name="ungroup_w_in", out_shape=jax.ShapeDtypeStruct((N_DEV, D_MODEL, IN_SHARD_PAD), BF16),
        compiler_params=_params(),
    )(g_cat)


SMALL = (("b_ada", 6144), ("g_pre_mix", 1024), ("g_post_mix", 1024), ("g_pre_ffn", 1024), ("g_post_ffn", 1024),
         ("conv_b", 1024), ("mlstm_norm_w", 512), ("b_if", 8), ("attn_sinks", 8))
SMALL_W = 12032
SMALL_PAD = 12288


def _pack_small(vals):
    parts = []
    for n, width in SMALL:
        a = vals[n].reshape(-1)
        parts.append(jnp.pad(a, (0, LANES - width)) if width < LANES else a)
    return jnp.concatenate(parts)


def _unpack_small(vec):
    out, off = {}, 0
    for n, width in SMALL:
        out[n] = vec[off:off + width].reshape(1, width)
        off += max(width, LANES)
    return out


WEIGHT_NAMES = ("w_ada", "b_ada", "g_pre_mix", "g_post_mix", "w_in", "b_if", "conv_w", "conv_b", "attn_sinks",
                "mlstm_norm_w", "w_branch_attn", "w_branch_mlstm", "w_out", "g_pre_ffn", "g_post_ffn",
                "w_ffn_gate", "w_ffn_up", "w_ffn_down")


def kernel(x, c, positions, w_ada, b_ada, g_pre_mix, g_post_mix, w_in, b_if, conv_w, conv_b, attn_sinks, mlstm_norm_w, w_branch_attn, w_branch_mlstm, w_out, g_pre_ffn, g_post_ffn, w_ffn_gate, w_ffn_up, w_ffn_down, loss_target, m_w_ada, m_b_ada, m_g_pre_mix, m_g_post_mix, m_w_in, m_b_if, m_conv_w, m_conv_b, m_attn_sinks, m_mlstm_norm_w, m_w_branch_attn, m_w_branch_mlstm, m_w_out, m_g_pre_ffn, m_g_post_ffn, m_w_ffn_gate, m_w_ffn_up, m_w_ffn_down, v_w_ada, v_b_ada, v_g_pre_mix, v_g_post_mix, v_w_in, v_b_if, v_conv_w, v_conv_b, v_attn_sinks, v_mlstm_norm_w, v_w_branch_attn, v_w_branch_mlstm, v_w_out, v_g_pre_ffn, v_g_post_ffn, v_w_ffn_gate, v_w_ffn_up, v_w_ffn_down):
    given = dict(locals())
    W = {n: given[n][0] for n in WEIGHT_NAMES}
    M = {n: given["m_" + n][0] for n in WEIGHT_NAMES}
    V = {n: given["v_" + n][0] for n in WEIGHT_NAMES}
    me = 4 * lax.axis_index("x") + 2 * lax.axis_index("y") + lax.axis_index("c")

    ff_sh = D_FF // N_DEV
    g_in, g_conv, cg = _all_gather_hbm([jnp.pad(W["w_in"], ((0, 0), (0, IN_SHARD_PAD - IN_SHARD))).astype(BF16),
                                        jnp.pad(W["conv_w"], ((0, 4), (0, 0))), c.reshape(8, D_MODEL // 8)])

    cmat = cg.reshape(N_DEV, D_MODEL)
    ada_w = D_MODEL * 6 // N_DEV
    b_cols = lax.dynamic_slice(W["b_ada"], (me * ada_w,), (ada_w,)).reshape(1, ada_w)
    mod_part = _mod_partial(cmat, W["w_ada"], b_cols)
    mod_recv = _small_exchange(jnp.broadcast_to(mod_part[:, None, :], (N_DEV, 8, ada_w)), False, "scatter_mod")
    mod = mod_recv[:, 0, :].reshape(6, D_MODEL)

    st_b = _exchange_start([W["w_branch_attn"].astype(BF16), W["w_branch_mlstm"].astype(BF16), W["w_out"].astype(BF16),
                            W["w_ffn_gate"].T.astype(BF16), W["w_ffn_up"].T.astype(BF16), W["w_ffn_down"].astype(BF16)],
                           True, mod_recv, "gather_rest_start")
    g_in = _after(g_in, st_b["token"])
    cols = lambda g: g.transpose(1, 0, 2).reshape(g.shape[1], N_DEV * g.shape[2])
    gains = jnp.stack([W["g_pre_mix"], W["g_post_mix"], W["g_pre_ffn"], W["g_post_ffn"]])
    xs, tgt = x[0], loss_target[0]
    t = _tables(mod, gains, cols(g_conv)[0:4], W["conv_b"], W["b_if"], W["mlstm_norm_w"], positions[0])
    w_cat = _regroup_w_in(g_in)
    a = _mixer_fwd(xs, t, W["attn_sinks"], w_cat)
    g_ba, g_bm, g_out, g_gate, g_up, g_down = _exchange_wait(st_b, a["ym"], "gather_rest_wait")
    w_ba, w_bm, w_out = cols(g_ba), cols(g_bm), g_out.reshape(D_MODEL, D_MODEL)
    b = _ffn_part(xs, tgt, t, a, w_ba, w_bm, w_out, g_gate.reshape(D_FF, D_MODEL), g_up.reshape(D_FF, D_MODEL),
                  g_down.reshape(D_FF, D_MODEL))

    st_f = _exchange_start([b["g_w_gate"].reshape(N_DEV, ff_sh, D_MODEL), b["g_w_up"].reshape(N_DEV, ff_sh, D_MODEL),
                            b["g_w_down"].reshape(N_DEV, ff_sh, D_MODEL)], False, b["dx1"], "scatter_ffn_start")
    cm = _mixer_bwd(_after(b["dx1"], st_f["token"]), t, a, b, W["attn_sinks"], w_ba, w_bm, w_out)
    pieces = lambda g, n: g.reshape(g.shape[0], N_DEV, n).transpose(1, 0, 2)
    st_m = _exchange_start([_ungroup_w_in(cm["g_w_cat"]), pieces(cm["g_w_ba"], 128), pieces(cm["g_w_bm"], 128),
                            cm["g_w_out"].reshape(N_DEV, D_MODEL // N_DEV, D_MODEL),
                            jnp.pad(pieces(cm["dcw"][0:4], 128), ((0, 0), (0, 4), (0, 0)))], False, cm["dproj"],
                           "scatter_mixer_start")
    r_gate, r_up, r_down = _exchange_wait(st_f, st_m["token"], "scatter_ffn_wait")
    grad_x, acc_p = _pre_bwd(_after(cm["dproj"], st_m["token"]), xs, b["dx1"], t["vecs"], w_cat)
    small = _small_grads(acc_p, b, cm)
    loss = b["loss"]

    big_out = [{} for _ in range(4)]

    def put(n, res):
        for k in range(4):
            big_out[k][n] = res[k][None]

    put("w_ffn_down", _adamw_sum(r_down, W["w_ffn_down"], M["w_ffn_down"], V["w_ffn_down"], "adamw_w_ffn_down"))
    for n, g in (("w_ffn_gate", _sum8(r_gate, "sum_w_ffn_gate").T), ("w_ffn_up", _sum8(r_up, "sum_w_ffn_up").T)):
        put(n, [g] + list(_adamw_plain(g, W[n], M[n], V[n], "adamw_" + n)))

    part = {"b_ada": small["dmod"], "g_pre_mix": small["dgains"][0], "g_post_mix": small["dgains"][1],
            "g_pre_ffn": small["dgains"][2], "g_post_ffn": small["dgains"][3], "conv_b": small["dconv_b"],
            "mlstm_norm_w": small["dnorm_w"], "b_if": small["db_if"], "attn_sinks": small["dsinks"]}
    rows8 = lambda d: jnp.pad(_pack_small(d), (0, SMALL_PAD - SMALL_W)).reshape(8, SMALL_PAD // 8)
    sg = _small_exchange(rows8(part), True, "gather_small")
    small_out = [_unpack_small(o.reshape(-1)) for o in _adamw_small(sg, rows8(W), rows8(M), rows8(V))]
    dmod_cols = lax.dynamic_slice(sg.reshape(N_DEV, SMALL_PAD), (0, me * ada_w), (N_DEV, ada_w))
    ada_out = _adamw_w_ada(cmat, dmod_cols, W["w_ada"], M["w_ada"], V["w_ada"])

    r_in, r_ba, r_bm, r_out, r_conv = _exchange_wait(st_m, grad_x, "scatter_mixer_wait")
    for n, r in (("w_branch_attn", r_ba), ("w_branch_mlstm", r_bm), ("w_out", r_out)):
        put(n, _adamw_sum(r, W[n], M[n], V[n], "adamw_" + n))
    pad4 = lambda v: jnp.pad(v, ((0, 4), (0, 0)))
    put("conv_w", [o[0:4] for o in _adamw_sum(r_conv, pad4(W["conv_w"]), pad4(M["conv_w"]), pad4(V["conv_w"]),
                                                 "adamw_conv_w")])
    g = _sum8(r_in, "sum_w_in")[:, 0:IN_SHARD]
    put("w_in", [g] + list(_adamw_plain(g, W["w_in"], M["w_in"], V["w_in"], "adamw_w_in")))

    total = lax.psum(loss, ("x", "y", "c"))
    outs = [total, grad_x[None]]
    for k in range(4):
        for n in WEIGHT_NAMES:
            if n == "w_ada":
                outs.append(ada_out[k][None])
            elif n in big_out[k]:
                outs.append(big_out[k][n])
            else:
                outs.append(small_out[k][n])
    return tuple(outs)
```

```python
import functools

import jax
import jax.numpy as jnp
import numpy as np
from jax import lax
from jax.experimental import pallas as pl
from jax.experimental.pallas import tpu as pltpu

F32 = jnp.float32
BF16 = jnp.bfloat16

N_DEV = 8
D_MODEL = 1024
D_FF = 2816
N_Q_HEADS = 8
HEAD_DIM = 64
ATTN_BLOCK = 128
ROPE_THETA = 10000.0
MLSTM_HEADS = 4
MLSTM_HEAD_DIM = 128
MLSTM_CHUNK = 128
NORM_EPS = 1e-6
ADAM_LR = 0.001
ADAM_B1 = 0.9
ADAM_B2 = 0.999
ADAM_EPS = 1e-08
ADAM_WD = 0.01
ADAM_STEP = 10

ROW_TILE = 256
WIDE_TILE = 512
LANES = 128
NEG = -1e30
VMEM_LIMIT = 56 * 1024 * 1024

A_W = 768
M_W = 2048
IF_W = 128
G_W = 2048
CAT_W = A_W + M_W + IF_W + G_W

R_SHIFT_M, R_SCALE_M, R_GATE_M, R_SHIFT_F, R_SCALE_F, R_GATE_F = 0, 1, 2, 3, 4, 5
R_G_PRE_MIX, R_G_POST_MIX, R_G_PRE_FFN, R_G_POST_FFN = 6, 7, 8, 9


def _dot(a, b):
    return jnp.dot(a, b, preferred_element_type=F32)


def _dot_nt(a, b):
    return lax.dot_general(a, b, (((1,), (1,)), ((), ())), preferred_element_type=F32)


def _dot_tn(a, b):
    return lax.dot_general(a, b, (((0,), (0,)), ((), ())), preferred_element_type=F32)


def _recip(x):
    return 1.0 / x


def _sigmoid(x):
    return _recip(1.0 + jnp.exp(-x))


def _colsum(x):
    return jnp.sum(x, axis=0, keepdims=True)


def _rowmean(x):
    return jnp.mean(x, axis=-1, keepdims=True)


def _params(sem=None, vmem=VMEM_LIMIT):
    kw = dict(vmem_limit_bytes=vmem)
    if sem is not None:
        kw["dimension_semantics"] = sem
    return pltpu.CompilerParams(**kw)


def _full(shape):
    nd = len(shape)
    return pl.BlockSpec(shape, lambda *_: (0,) * nd)


def _pre_proj(x, vecs, w_cat):
    S = x.shape[0]
    tm = WIDE_TILE

    def body(x_ref, v_ref, w_ref, h_ref, pa_ref, pm_ref, pif_ref, pg_ref):
        xv = x_ref[...]
        r = lax.rsqrt(_rowmean(xv * xv) + NORM_EPS)
        h = (xv * r * v_ref[R_G_PRE_MIX:R_G_PRE_MIX + 1, :]) * (1.0 + v_ref[R_SCALE_M:R_SCALE_M + 1, :]) \
            + v_ref[R_SHIFT_M:R_SHIFT_M + 1, :]
        hb = h.astype(BF16)
        h_ref[...] = hb
        pa_ref[...] = _dot(hb, w_ref[:, 0:A_W])
        pm_ref[...] = _dot(hb, w_ref[:, A_W:A_W + M_W])
        pif_ref[...] = _dot(hb, w_ref[:, A_W + M_W:A_W + M_W + IF_W])
        pg_ref[...] = _dot(hb, w_ref[:, A_W + M_W + IF_W:CAT_W]).astype(BF16)

    row = lambda w: pl.BlockSpec((tm, w), lambda i: (i, 0))
    return pl.pallas_call(
        body, name="pre_proj", grid=(S // tm,),
        in_specs=[row(D_MODEL), _full(vecs.shape), _full(w_cat.shape)],
        out_specs=[row(D_MODEL), row(A_W), row(M_W), row(IF_W), row(G_W)],
        out_shape=[jax.ShapeDtypeStruct((S, D_MODEL), BF16), jax.ShapeDtypeStruct((S, A_W), F32),
                   jax.ShapeDtypeStruct((S, M_W), F32), jax.ShapeDtypeStruct((S, IF_W), F32),
                   jax.ShapeDtypeStruct((S, G_W), BF16)],
        compiler_params=_params(("parallel",)),
    )(x, vecs, w_cat)


def _mix_fwd(x, ya, ym, pg, vecs, w_ba, w_bm, w_out):
    S = x.shape[0]
    tm = WIDE_TILE

    def body(x_ref, ya_ref, ym_ref, pg_ref, v_ref, wba_ref, wbm_ref, wout_ref,
             x1_ref, merged_ref, mix_ref, pa_ref, pb_ref):
        pa = _dot(ya_ref[...], wba_ref[...])
        pb = _dot(ym_ref[...], wbm_ref[...])
        merged = _sigmoid(pg_ref[:, 0:D_MODEL].astype(F32)) * pa + _sigmoid(pg_ref[:, D_MODEL:G_W].astype(F32)) * pb
        mb = merged.astype(BF16)
        mix = _dot(mb, wout_ref[...])
        r = lax.rsqrt(_rowmean(mix * mix) + NORM_EPS)
        x1_ref[...] = x_ref[...] + v_ref[R_GATE_M:R_GATE_M + 1, :] * (mix * r * v_ref[R_G_POST_MIX:R_G_POST_MIX + 1, :])
        merged_ref[...] = mb
        mix_ref[...] = mix
        pa_ref[...] = pa.astype(BF16)
        pb_ref[...] = pb.astype(BF16)

    row = lambda w: pl.BlockSpec((tm, w), lambda i: (i, 0))
    sd = lambda w, dt: jax.ShapeDtypeStruct((S, w), dt)
    return pl.pallas_call(
        body, name="mix_fwd", grid=(S // tm,),
        in_specs=[row(D_MODEL), row(512), row(512), row(G_W), _full(vecs.shape), _full(w_ba.shape),
                  _full(w_bm.shape), _full(w_out.shape)],
        out_specs=[row(D_MODEL)] * 5,
        out_shape=[sd(D_MODEL, F32), sd(D_MODEL, BF16), sd(D_MODEL, F32), sd(D_MODEL, BF16), sd(D_MODEL, BF16)],
        compiler_params=_params(("parallel",)),
    )(x, ya, ym, pg, vecs, w_ba, w_bm, w_out)


def _ffn_fwd_bwd(x1, tgt, vecs, w_gate, w_up, w_down):
    S = x1.shape[0]
    tm = ROW_TILE

    def body(x1_ref, tgt_ref, v_ref, wg_hbm, wu_hbm, wd_hbm,
             dx1_ref, h2_ref, hid_ref, da_ref, du_ref, dff_ref, acc_ref, loss_ref,
             wg, wu, wd, sem):
        i = pl.program_id(0)

        @pl.when(i == 0)
        def _():
            cps = [pltpu.make_async_copy(wg_hbm, wg, sem.at[0]), pltpu.make_async_copy(wu_hbm, wu, sem.at[1]),
                   pltpu.make_async_copy(wd_hbm, wd, sem.at[2])]
            for cp in cps:
                cp.start()
            for cp in cps:
                cp.wait()
            acc_ref[...] = jnp.zeros_like(acc_ref)
            loss_ref[...] = jnp.zeros_like(loss_ref)

        vrow = lambda r: v_ref[r:r + 1, :]
        x1v = x1_ref[...]
        r3 = lax.rsqrt(_rowmean(x1v * x1v) + NORM_EPS)
        x1hat = x1v * r3
        xn3 = x1hat * vrow(R_G_PRE_FFN)
        h2b = (xn3 * (1.0 + vrow(R_SCALE_F)) + vrow(R_SHIFT_F)).astype(BF16)
        h2_ref[...] = h2b
        a = _dot_nt(h2b, wg[...])
        u = _dot_nt(h2b, wu[...])
        sg = _sigmoid(a)
        sil = a * sg
        hidb = (sil * u).astype(BF16)
        hid_ref[...] = hidb
        ff = _dot(hidb, wd[...])
        r4 = lax.rsqrt(_rowmean(ff * ff) + NORM_EPS)
        ffhat = ff * r4
        n4 = ffhat * vrow(R_G_POST_FFN)
        err = x1v + vrow(R_GATE_F) * n4 - tgt_ref[...]
        loss_ref[...] += jnp.sum(err * err) * (0.5 / D_MODEL)
        dy = err * (1.0 / D_MODEL)
        acc_ref[0:1, :] += _colsum(dy * n4)
        dn4 = dy * vrow(R_GATE_F)
        acc_ref[1:2, :] += _colsum(dn4 * ffhat)
        dffhat = dn4 * vrow(R_G_POST_FFN)
        dffb = (r4 * (dffhat - ffhat * _rowmean(dffhat * ffhat))).astype(BF16)
        dff_ref[...] = dffb
        dhid = _dot_nt(dffb, wd[...])
        dub = (dhid * sil).astype(BF16)
        dab = (dhid * u * (sg * (1.0 + a * (1.0 - sg)))).astype(BF16)
        da_ref[...] = dab
        du_ref[...] = dub
        dh2 = _dot(dab, wg[...]) + _dot(dub, wu[...])
        acc_ref[2:3, :] += _colsum(dh2 * xn3)
        acc_ref[3:4, :] += _colsum(dh2)
        dxn3 = dh2 * (1.0 + vrow(R_SCALE_F))
        acc_ref[4:5, :] += _colsum(dxn3 * x1hat)
        dx1hat = dxn3 * vrow(R_G_PRE_FFN)
        dx1_ref[...] = dy + r3 * (dx1hat - x1hat * _rowmean(dx1hat * x1hat))

    row = lambda w: pl.BlockSpec((tm, w), lambda i: (i, 0))
    sd = lambda w, dt: jax.ShapeDtypeStruct((S, w), dt)
    anyspec = pl.BlockSpec(memory_space=pl.ANY)
    return pl.pallas_call(
        body, name="ffn_fwd_bwd", grid=(S // tm,),
        in_specs=[row(D_MODEL), row(D_MODEL), _full(vecs.shape), anyspec, anyspec, anyspec],
        out_specs=[row(D_MODEL), row(D_MODEL), row(D_FF), row(D_FF), row(D_FF), row(D_MODEL),
                   _full((8, D_MODEL)), _full((8, LANES))],
        out_shape=[sd(D_MODEL, F32), sd(D_MODEL, BF16), sd(D_FF, BF16), sd(D_FF, BF16), sd(D_FF, BF16),
                   sd(D_MODEL, BF16), jax.ShapeDtypeStruct((8, D_MODEL), F32), jax.ShapeDtypeStruct((8, LANES), F32)],
        scratch_shapes=[pltpu.VMEM(w_gate.shape, BF16), pltpu.VMEM(w_up.shape, BF16), pltpu.VMEM(w_down.shape, BF16),
                        pltpu.SemaphoreType.DMA((3,))],
        compiler_params=_params(("arbitrary",)),
    )(x1, tgt, vecs, w_gate, w_up, w_down)


def _mix_bwd(dx1, mix, pa, pb, pg, vecs, w_ba, w_bm, w_out):
    S = dx1.shape[0]
    tm = WIDE_TILE

    def body(dx1_ref, mix_ref, pa_ref, pb_ref, pg_ref, v_ref, wba_ref, wbm_ref, wout_ref,
             dmix_ref, dpa_ref, dpb_ref, dg_ref, dya_ref, dym_ref, acc_ref):
        i = pl.program_id(0)

        @pl.when(i == 0)
        def _():
            acc_ref[...] = jnp.zeros_like(acc_ref)

        vrow = lambda r: v_ref[r:r + 1, :]
        dx1v = dx1_ref[...]
        mix = mix_ref[...]
        r2 = lax.rsqrt(_rowmean(mix * mix) + NORM_EPS)
        mixhat = mix * r2
        acc_ref[0:1, :] += _colsum(dx1v * (mixhat * vrow(R_G_POST_MIX)))
        dn2 = dx1v * vrow(R_GATE_M)
        acc_ref[1:2, :] += _colsum(dn2 * mixhat)
        dmixhat = dn2 * vrow(R_G_POST_MIX)
        dmixb = (r2 * (dmixhat - mixhat * _rowmean(dmixhat * mixhat))).astype(BF16)
        dmix_ref[...] = dmixb
        dmerged = _dot_nt(dmixb, wout_ref[...])
        sa = _sigmoid(pg_ref[:, 0:D_MODEL].astype(F32))
        sm = _sigmoid(pg_ref[:, D_MODEL:G_W].astype(F32))
        dpab = (dmerged * sa).astype(BF16)
        dpbb = (dmerged * sm).astype(BF16)
        dpa_ref[...] = dpab
        dpb_ref[...] = dpbb
        dg_ref[:, 0:D_MODEL] = (dmerged * pa_ref[...].astype(F32) * (sa * (1.0 - sa))).astype(BF16)
        dg_ref[:, D_MODEL:G_W] = (dmerged * pb_ref[...].astype(F32) * (sm * (1.0 - sm))).astype(BF16)
        dya_ref[...] = _dot_nt(dpab, wba_ref[...])
        dym_ref[...] = _dot_nt(dpbb, wbm_ref[...])

    row = lambda w: pl.BlockSpec((tm, w), lambda i: (i, 0))
    sd = lambda w, dt: jax.ShapeDtypeStruct((S, w), dt)
    return pl.pallas_call(
        body, name="mix_bwd", grid=(S // tm,),
        in_specs=[row(D_MODEL), row(D_MODEL), row(D_MODEL), row(D_MODEL), row(G_W), _full(vecs.shape),
                  _full(w_ba.shape), _full(w_bm.shape), _full(w_out.shape)],
        out_specs=[row(D_MODEL), row(D_MODEL), row(D_MODEL), row(G_W), row(512), row(512), _full((8, D_MODEL))],
        out_shape=[sd(D_MODEL, BF16), sd(D_MODEL, BF16), sd(D_MODEL, BF16), sd(G_W, BF16), sd(512, F32), sd(512, F32),
                   jax.ShapeDtypeStruct((8, D_MODEL), F32)],
        compiler_params=_params(("arbitrary",)),
    )(dx1, mix, pa, pb, pg, vecs, w_ba, w_bm, w_out)


def _pre_bwd(dproj, x, dx1, vecs, w_cat):
    S = x.shape[0]
    tm = WIDE_TILE

    def body(dp_ref, x_ref, dx1_ref, v_ref, w_ref, dx_ref, acc_ref):
        i = pl.program_id(0)

        @pl.when(i == 0)
        def _():
            acc_ref[...] = jnp.zeros_like(acc_ref)

        vrow = lambda r: v_ref[r:r + 1, :]
        dh = _dot_nt(dp_ref[...], w_ref[...])
        xv = x_ref[...]
        r1 = lax.rsqrt(_rowmean(xv * xv) + NORM_EPS)
        xhat = xv * r1
        acc_ref[0:1, :] += _colsum(dh * (xhat * vrow(R_G_PRE_MIX)))
        acc_ref[1:2, :] += _colsum(dh)
        dxn = dh * (1.0 + vrow(R_SCALE_M))
        acc_ref[2:3, :] += _colsum(dxn * xhat)
        dxhat = dxn * vrow(R_G_PRE_MIX)
        dx_ref[...] = dx1_ref[...] + r1 * (dxhat - xhat * _rowmean(dxhat * xhat))

    row = lambda w: pl.BlockSpec((tm, w), lambda i: (i, 0))
    return pl.pallas_call(
        body, name="pre_bwd", grid=(S // tm,),
        in_specs=[row(CAT_W), row(D_MODEL), row(D_MODEL), _full(vecs.shape), _full(w_cat.shape)],
        out_specs=[row(D_MODEL), _full((8, D_MODEL))],
        out_shape=[jax.ShapeDtypeStruct((S, D_MODEL), F32), jax.ShapeDtypeStruct((8, D_MODEL), F32)],
        compiler_params=_params(("arbitrary",)),
    )(dproj, x, dx1, vecs, w_cat)


def _matmul_tn(a, b, tn, name, ts=1024):
    S, K = a.shape
    N = b.shape[1]
    n_s = S // ts

    def body(a_ref, b_ref, o_ref, acc_ref):
        s = pl.program_id(1)

        @pl.when(s == 0)
        def _():
            acc_ref[...] = jnp.zeros_like(acc_ref)

        acc_ref[...] += _dot_tn(a_ref[...], b_ref[...])

        @pl.when(s == n_s - 1)
        def _():
            o_ref[...] = acc_ref[...].astype(BF16)

    return pl.pallas_call(
        body, name=name, grid=(N // tn, n_s),
        in_specs=[pl.BlockSpec((ts, K), lambda j, s: (s, 0)), pl.BlockSpec((ts, tn), lambda j, s: (s, j))],
        out_specs=pl.BlockSpec((K, tn), lambda j, s: (0, j)),
        out_shape=jax.ShapeDtypeStruct((K, N), BF16),
        scratch_shapes=[pltpu.VMEM((K, tn), F32)],
        compiler_params=_params(("parallel", "arbitrary")),
    )(a, b)


def _rope_swap(t):
    lane = lax.broadcasted_iota(jnp.int32, t.shape, 1)
    first = (lane & (HEAD_DIM - 1)) < (HEAD_DIM // 2)
    return jnp.where(first, pltpu.roll(t, LANES - HEAD_DIM // 2, 1), pltpu.roll(t, HEAD_DIM // 2, 1))


def _rope(t, cos, sin_signed):
    return t * cos + _rope_swap(t) * sin_signed


def _rope_t(d, cos, sin_signed):
    return d * cos + _rope_swap(d * sin_signed)


def _to_kv_lanes(chunk, p, h):
    lane = lax.broadcasted_iota(jnp.int32, chunk.shape, 1)
    src = chunk if p == h else pltpu.roll(chunk, HEAD_DIM, 1)
    return jnp.where((lane >> 6) == h, src, jnp.zeros_like(src))


def _from_kv_lanes(o_a, o_b, h):
    lane = lax.broadcasted_iota(jnp.int32, o_a.shape, 1)
    a = o_a if h == 0 else pltpu.roll(o_a, HEAD_DIM, 1)
    b = o_b if h == 1 else pltpu.roll(o_b, HEAD_DIM, 1)
    return jnp.where(lane < HEAD_DIM, a, b)


def _band_bias(n):
    blk = ATTN_BLOCK
    qi = lax.broadcasted_iota(jnp.int32, (blk, 2 * blk), 0)
    kj = lax.broadcasted_iota(jnp.int32, (blk, 2 * blk), 1)
    seen = (kj > qi) & (kj <= qi + blk) & ((n > 0) | (kj >= blk))
    return jnp.concatenate([jnp.where(seen, 0.0, NEG)] * N_Q_HEADS, axis=0)


def _stack_heads(chunks, h, dtype):
    parts = []
    for g in range(4):
        j = 4 * h + g
        parts.append(_to_kv_lanes(chunks[j // 2], j % 2, h))
    return jnp.concatenate(parts, axis=0).astype(dtype)


def _fused_call(parts, name, n_steps):
    counts = [(len(p["in_specs"]), len(p["out_specs"]), len(p["scratch"])) for p in parts]
    n_in, n_out = sum(c[0] for c in counts), sum(c[1] for c in counts)

    def kernel_fn(*refs):
        i = pl.program_id(0)
        groups, a, b, c = [], 0, n_in, n_in + n_out
        for ci, co, cs in counts:
            groups.append(refs[a:a + ci] + refs[b:b + co] + refs[c:c + cs])
            a, b, c = a + ci, b + co, c + cs
        for p, g in zip(parts, groups):
            p["init"](i, *g)
        for p, g in zip(parts, groups):
            p["body"](i, *g)

    flat = lambda key: [v for p in parts for v in p[key]]
    res = pl.pallas_call(
        kernel_fn, name=name, grid=(n_steps,), in_specs=flat("in_specs"), out_specs=flat("out_specs"),
        out_shape=flat("out_shape"), scratch_shapes=flat("scratch"), compiler_params=_params(("arbitrary",)),
    )(*flat("operands"))
    out, pos = [], 0
    for _, co, _ in counts:
        out.append(res[pos:pos + co])
        pos += co
    return out


def _attn_fwd_part(pa, cos, sin, sinks):
    S = pa.shape[0]
    blk = ATTN_BLOCK
    nb = S // blk

    def body(n, sink_ref, cur_ref, prev_ref, cos_ref, sin_ref, cosp_ref, sinp_ref,
             ya_ref, qr_ref, kr_ref, vb_ref, lse_ref):
        cos_c, sin_c = cos_ref[...], sin_ref[...]
        qch = [_rope(cur_ref[:, c * LANES:(c + 1) * LANES], cos_c, sin_c) * (HEAD_DIM ** -0.5) for c in range(4)]
        for c in range(4):
            qr_ref[:, c * LANES:(c + 1) * LANES] = qch[c].astype(BF16)
        k_cur = _rope(cur_ref[:, 512:640], cos_c, sin_c).astype(BF16)
        k_prev = _rope(prev_ref[:, 0:LANES], cosp_ref[...], sinp_ref[...]).astype(BF16)
        v_cur = cur_ref[:, 640:768].astype(BF16)
        v_prev = prev_ref[:, LANES:2 * LANES].astype(BF16)
        kr_ref[...] = k_cur
        vb_ref[...] = v_cur
        K = jnp.concatenate([k_prev, k_cur], axis=0)
        V = jnp.concatenate([v_prev, v_cur], axis=0)
        lane = lax.broadcasted_iota(jnp.int32, (blk, LANES), 1)
        s = jnp.concatenate([_dot_nt(_stack_heads(qch, h, BF16), K) for h in range(2)], axis=0)
        s = s + _band_bias(n)
        rowmax = jnp.max(s, axis=1, keepdims=True)
        hd = lambda x, j: x[j * blk:(j + 1) * blk]
        m = jnp.concatenate([jnp.maximum(hd(rowmax, j), sink_ref[j]) for j in range(N_Q_HEADS)], axis=0)
        p = jnp.exp(s - m)
        den = jnp.sum(p, axis=1, keepdims=True) \
            + jnp.concatenate([jnp.exp(sink_ref[j] - hd(m, j)) for j in range(N_Q_HEADS)], axis=0)
        pb = (p * _recip(den)).astype(BF16)
        o = jnp.concatenate([_dot(pb[4 * h * blk:4 * (h + 1) * blk], V) for h in range(2)], axis=0)
        lse = m + jnp.log(den)
        outs = [o[j * blk:(j + 1) * blk, :] for j in range(N_Q_HEADS)]
        lse_tile = jnp.zeros((blk, LANES), F32)
        for j in range(N_Q_HEADS):
            lse_tile = jnp.where(lane == j, lse[j * blk:(j + 1) * blk, :], lse_tile)
        for c in range(4):
            ya_ref[:, c * LANES:(c + 1) * LANES] = _from_kv_lanes(outs[2 * c], outs[2 * c + 1], c // 2).astype(BF16)
        lse_ref[...] = lse_tile

    prev = lambda n: jnp.maximum(n - 1, 0)
    sd = lambda w, dt: jax.ShapeDtypeStruct((S, w), dt)
    return dict(
        init=lambda n, *refs: None, body=body, scratch=[], operands=[sinks, pa, pa, cos, sin, cos, sin],
        in_specs=[pl.BlockSpec(memory_space=pltpu.SMEM),
                  pl.BlockSpec((blk, A_W), lambda n: (n, 0)),
                  pl.BlockSpec((blk, 256), lambda n: (prev(n), 2)),
                  pl.BlockSpec((blk, LANES), lambda n: (n, 0)), pl.BlockSpec((blk, LANES), lambda n: (n, 0)),
                  pl.BlockSpec((blk, LANES), lambda n: (prev(n), 0)), pl.BlockSpec((blk, LANES), lambda n: (prev(n), 0))],
        out_specs=[pl.BlockSpec((blk, 512), lambda n: (n, 0)), pl.BlockSpec((blk, 512), lambda n: (n, 0)),
                   pl.BlockSpec((blk, LANES), lambda n: (n, 0)), pl.BlockSpec((blk, LANES), lambda n: (n, 0)),
                   pl.BlockSpec((blk, LANES), lambda n: (n, 0))],
        out_shape=[sd(512, BF16), sd(512, BF16), sd(LANES, BF16), sd(LANES, BF16), sd(LANES, F32)])


def _attn_bwd_part(dya, qr, kr, vb, lse, cos, sin, sinks):
    S = dya.shape[0]
    blk = ATTN_BLOCK
    nb = S // blk

    def init(n, sink_ref, dya_ref, qr_ref, kc_ref, kp_ref, vc_ref, vp_ref, lse_ref, cos_ref, sin_ref, cosp_ref, sinp_ref,
             dq_ref, dkv_ref, last_ref, dsink_ref, ck, cv):
        @pl.when(n == 0)
        def _():
            ck[...] = jnp.zeros_like(ck)
            cv[...] = jnp.zeros_like(cv)
            dsink_ref[...] = jnp.zeros_like(dsink_ref)

    def body(n, sink_ref, dya_ref, qr_ref, kc_ref, kp_ref, vc_ref, vp_ref, lse_ref, cos_ref, sin_ref, cosp_ref, sinp_ref,
             dq_ref, dkv_ref, last_ref, dsink_ref, ck, cv):
        K = jnp.concatenate([kp_ref[...], kc_ref[...]], axis=0)
        V = jnp.concatenate([vp_ref[...], vc_ref[...]], axis=0)
        qch = [qr_ref[:, c * LANES:(c + 1) * LANES] for c in range(4)]
        dch = [dya_ref[:, c * LANES:(c + 1) * LANES] for c in range(4)]
        lse_tile = lse_ref[...]
        lane8 = lax.broadcasted_iota(jnp.int32, (8, LANES), 1)
        grp = lambda x, h: x[4 * h * blk:4 * (h + 1) * blk]
        qs = jnp.concatenate([_stack_heads(qch, h, BF16) for h in range(2)], axis=0)
        dos = jnp.concatenate([_stack_heads(dch, h, BF16) for h in range(2)], axis=0)
        lse_col = jnp.concatenate([lse_tile[:, j:j + 1] for j in range(N_Q_HEADS)], axis=0)
        s = jnp.concatenate([_dot_nt(grp(qs, h), K) for h in range(2)], axis=0)
        p = jnp.exp(s + _band_bias(n) - lse_col)
        dp = jnp.concatenate([_dot_nt(grp(dos, h), V) for h in range(2)], axis=0)
        delta = jnp.sum(p * dp, axis=1, keepdims=True)
        dsb = (p * (dp - delta)).astype(BF16)
        pb = p.astype(BF16)
        dq = jnp.concatenate([_dot(grp(dsb, h), K) for h in range(2)], axis=0)
        dk_acc = _dot_tn(grp(dsb, 0), grp(qs, 0)) + _dot_tn(grp(dsb, 1), grp(qs, 1))
        dv_acc = _dot_tn(grp(pb, 0), grp(dos, 0)) + _dot_tn(grp(pb, 1), grp(dos, 1))
        dqs = [dq[j * blk:(j + 1) * blk, :] for j in range(N_Q_HEADS)]
        dsink = jnp.zeros((8, LANES), F32)
        for j in range(N_Q_HEADS):
            rows = slice(j * blk, (j + 1) * blk)
            ps_delta = jnp.exp(sink_ref[j] - lse_col[rows]) * delta[rows]
            dsink = jnp.where(lane8 == j, dsink - jnp.sum(ps_delta), dsink)
        dsink_ref[...] += dsink
        cos_c, sin_c = cos_ref[...], sin_ref[...]
        for c in range(4):
            dqc = _from_kv_lanes(dqs[2 * c], dqs[2 * c + 1], c // 2) * (HEAD_DIM ** -0.5)
            dq_ref[:, c * LANES:(c + 1) * LANES] = _rope_t(dqc, cos_c, sin_c).astype(BF16)
        dkv_ref[:, 0:LANES] = _rope_t(dk_acc[0:blk, :] + ck[...], cosp_ref[...], sinp_ref[...]).astype(BF16)
        dkv_ref[:, LANES:2 * LANES] = (dv_acc[0:blk, :] + cv[...]).astype(BF16)
        ck[...] = dk_acc[blk:2 * blk, :]
        cv[...] = dv_acc[blk:2 * blk, :]
        last_ref[:, 0:LANES] = _rope_t(dk_acc[blk:2 * blk, :], cos_c, sin_c).astype(BF16)
        last_ref[:, LANES:2 * LANES] = dv_acc[blk:2 * blk, :].astype(BF16)

    prev = lambda n: jnp.maximum(n - 1, 0)
    same = lambda n: n
    bs = lambda w, f: pl.BlockSpec((blk, w), lambda n: (f(n), 0))
    return dict(
        init=init, body=body, operands=[sinks, dya, qr, kr, kr, vb, vb, lse, cos, sin, cos, sin],
        in_specs=[pl.BlockSpec(memory_space=pltpu.SMEM),
                  bs(512, same), bs(512, same), bs(LANES, same), bs(LANES, prev), bs(LANES, same), bs(LANES, prev),
                  bs(LANES, same), bs(LANES, same), bs(LANES, same), bs(LANES, prev), bs(LANES, prev)],
        out_specs=[bs(512, same), bs(256, prev), _full((blk, 256)), _full((8, LANES))],
        out_shape=[jax.ShapeDtypeStruct((S, 512), BF16), jax.ShapeDtypeStruct((S, 256), BF16),
                   jax.ShapeDtypeStruct((blk, 256), BF16), jax.ShapeDtypeStruct((8, LANES), F32)],
        scratch=[pltpu.VMEM((blk, LANES), F32), pltpu.VMEM((blk, LANES), F32)])


def _split3(x):
    hi = x.astype(BF16)
    r1 = x - hi.astype(F32)
    mid = r1.astype(BF16)
    lo = (r1 - mid.astype(F32)).astype(BF16)
    return hi, mid, lo


def _tri_matmul(tri_b, x):
    hi, mid, lo = _split3(x)
    return _dot(tri_b, hi) + _dot(tri_b, mid) + _dot(tri_b, lo)


def _log_sigmoid(x):
    return jnp.minimum(x, 0.0) - jnp.log(1.0 + jnp.exp(-jnp.abs(x)))


def _shift_rows(cur, seam, k, down):
    L = cur.shape[0]
    row8 = lax.broadcasted_iota(jnp.int32, seam.shape, 0)
    if down:
        mixed = jnp.concatenate([cur[:L - 8], jnp.where(row8 >= 8 - k, seam, cur[L - 8:])], axis=0)
        return pltpu.roll(mixed, k, 0)
    mixed = jnp.concatenate([jnp.where(row8 < k, seam, cur[:8]), cur[8:]], axis=0)
    return pltpu.roll(mixed, L - k, 0)


def _conv_fwd(cur, tail, cw_ref):
    z = cw_ref[4:5, :]
    for k in range(3, 0, -1):
        z = z + _shift_rows(cur, tail, k, True) * cw_ref[3 - k:4 - k, :]
    return z + cur * cw_ref[3:4, :]


def _stack(f):
    return jnp.concatenate([f(h) for h in range(MLSTM_HEADS)], axis=0)


def _head(x, h):
    L = x.shape[0] // MLSTM_HEADS
    return x[h * L:(h + 1) * L]


def _mlstm_heads_fwd(qk, cur_ref, gt, b_all, c_prev, nmv, tri, eye):
    L = qk.shape[0]
    HD = MLSTM_HEAD_DIM
    W4 = MLSTM_HEADS * HD
    col2row = lambda x: jnp.sum(jnp.where(eye, x, 0.0), axis=0, keepdims=True)
    b_col = _stack(lambda h: b_all[:, 4 + h:5 + h])
    i_col = _stack(lambda h: gt[:, h:h + 1])
    b_row = _stack(lambda h: jnp.broadcast_to(col2row(b_all[:, 4 + h:5 + h]), (L, L)))
    i_row = _stack(lambda h: jnp.broadcast_to(col2row(gt[:, h:h + 1]), (L, L)))
    bl = _stack(lambda h: jnp.broadcast_to(b_all[L - 1:L, 4 + h:5 + h], (L, 1)))
    m_prev = _stack(lambda h: jnp.broadcast_to(nmv[4 + h:5 + h, 0:1], (L, 1)))
    n_prev = _stack(lambda h: jnp.broadcast_to(nmv[h:h + 1, :], (L, HD)))
    tri4 = jnp.concatenate([tri] * MLSTM_HEADS, axis=0)
    Dm = jnp.where(tri4, b_col - b_row + i_row, NEG)
    inter = b_col + m_prev
    m_t = jnp.maximum(inter, jnp.max(Dm, axis=1, keepdims=True))
    W = jnp.exp(Dm - m_t)
    e_t = jnp.exp(inter - m_t)
    q = _stack(lambda h: qk[:, h * HD:(h + 1) * HD])
    k = _stack(lambda h: qk[:, W4 + h * HD:W4 + (h + 1) * HD]) * (HD ** -0.5)
    v = _stack(lambda h: cur_ref[:, 2 * W4 + h * HD:2 * W4 + (h + 1) * HD])
    qb, kb, vb = q.astype(BF16), k.astype(BF16), v.astype(BF16)
    Sc = _stack(lambda h: _dot_nt(_head(qb, h), _head(kb, h))) * W
    Scb = Sc.astype(BF16)
    cb = [c.astype(BF16) for c in c_prev]
    P1 = _stack(lambda h: _dot(_head(qb, h), cb[h]))
    num = _stack(lambda h: _dot(_head(Scb, h), _head(vb, h))) + e_t * P1
    qn = jnp.sum(q * n_prev, axis=1, keepdims=True)
    den = jnp.sum(Sc, axis=1, keepdims=True) + e_t * qn
    floor = jnp.exp(-m_t)
    inv_g = _recip(jnp.maximum(jnp.abs(den), floor))
    hv = num * inv_g
    a_col = bl - b_col + i_col
    a_max = _stack(lambda h: jnp.broadcast_to(jnp.max(_head(a_col, h), axis=0, keepdims=True), (L, 1)))
    m_new = jnp.maximum(bl + m_prev, a_max)
    dec = jnp.exp(bl + m_prev - m_new)
    u_col = jnp.exp(a_col - m_new)
    return dict(W=W, e_t=e_t, q=q, k=k, v=v, qb=qb, kb=kb, vb=vb, cb=cb, Sc=Sc, Scb=Scb, P1=P1, qn=qn, den=den,
                floor=floor, inv_g=inv_g, hv=hv, n_prev=n_prev, m_new=m_new, dec=dec, u_col=u_col)


def _mlstm_fwd_part(pm, pif, cw, sv):
    S = pm.shape[0]
    L = MLSTM_CHUNK
    nc = S // L
    HD = MLSTM_HEAD_DIM
    W4 = MLSTM_HEADS * HD

    def init(c, cur_ref, pif_ref, cw_ref, sv_ref, ym_ref, z_ref, cst_ref, nst_ref, C, nm, tail):
        @pl.when(c == 0)
        def _():
            C[...] = jnp.zeros_like(C)
            nm[...] = jnp.zeros_like(nm)
            tail[...] = jnp.zeros_like(tail)

    def body(c, cur_ref, pif_ref, cw_ref, sv_ref, ym_ref, z_ref, cst_ref, nst_ref, C, nm, tail):
        z = _conv_fwd(cur_ref[:, 0:2 * W4], tail[...], cw_ref)
        tail[...] = cur_ref[L - 8:L, 0:2 * W4]
        z_ref[...] = z
        qk = z * _sigmoid(z)
        gt = pif_ref[...] + sv_ref[1:2, 0:LANES]
        r_i = lax.broadcasted_iota(jnp.int32, (L, L), 0)
        c_i = lax.broadcasted_iota(jnp.int32, (L, L), 1)
        tri = c_i <= r_i
        eye = c_i == r_i
        b_all = _tri_matmul(tri.astype(BF16), _log_sigmoid(gt))
        nmv = nm[...]
        nst_ref[0] = nmv
        c_prev = [C[h] for h in range(MLSTM_HEADS)]
        f = _mlstm_heads_fwd(qk, cur_ref, gt, b_all, c_prev, nmv, tri, eye)
        hv = f["hv"]
        xc = hv - _rowmean(hv)
        hhat = xc * lax.rsqrt(_rowmean(xc * xc) + NORM_EPS)
        so = _sigmoid(_stack(lambda h: cur_ref[:, 3 * W4 + h * HD:3 * W4 + (h + 1) * HD]))
        wn = _stack(lambda h: jnp.broadcast_to(sv_ref[0:1, h * HD:(h + 1) * HD], (L, HD)))
        y = (so * hhat * wn).astype(BF16)
        kw = f["k"] * f["u_col"]
        kwb = kw.astype(BF16)
        n_new, m_new = [], []
        for h in range(MLSTM_HEADS):
            cst_ref[0, h] = c_prev[h]
            ym_ref[:, h * HD:(h + 1) * HD] = _head(y, h)
            dec = f["dec"][h * L:h * L + 1, :]
            C[h] = dec * c_prev[h] + _dot_tn(_head(kwb, h), _head(f["vb"], h))
            n_new.append(dec * nmv[h:h + 1, :] + _colsum(_head(kw, h)))
            m_new.append(jnp.broadcast_to(f["m_new"][h * L:h * L + 1, :], (1, LANES)))
        nm[...] = jnp.concatenate(n_new + m_new, axis=0)

    return dict(
        init=init, body=body, operands=[pm, pif, cw, sv],
        in_specs=[pl.BlockSpec((L, M_W), lambda c: (c, 0)),
                  pl.BlockSpec((L, IF_W), lambda c: (c, 0)), _full(cw.shape), _full(sv.shape)],
        out_specs=[pl.BlockSpec((L, W4), lambda c: (c, 0)), pl.BlockSpec((L, 2 * W4), lambda c: (c, 0)),
                   pl.BlockSpec((1, MLSTM_HEADS, HD, HD), lambda c: (c, 0, 0, 0)),
                   pl.BlockSpec((1, 8, LANES), lambda c: (c, 0, 0))],
        out_shape=[jax.ShapeDtypeStruct((S, W4), BF16), jax.ShapeDtypeStruct((S, 2 * W4), F32),
                   jax.ShapeDtypeStruct((nc, MLSTM_HEADS, HD, HD), F32), jax.ShapeDtypeStruct((nc, 8, LANES), F32)],
        scratch=[pltpu.VMEM((MLSTM_HEADS, HD, HD), F32), pltpu.VMEM((8, LANES), F32), pltpu.VMEM((8, 2 * W4), F32)])


def _mlstm_bwd_part(pm, zc, pif, cw, sv, dym, cst, nst):
    S = pm.shape[0]
    L = MLSTM_CHUNK
    nc = S // L
    HD = MLSTM_HEAD_DIM
    W4 = MLSTM_HEADS * HD

    def init(r, cur_ref, z_ref, pif_ref, cw_ref, sv_ref, dym_ref, cst_ref, nst_ref,
             dm_ref, dif_ref, dcw_ref, dsv_ref, dC, dn, dz_next, dqk):
        @pl.when(r == 0)
        def _():
            dC[...] = jnp.zeros_like(dC)
            dn[...] = jnp.zeros_like(dn)
            dz_next[...] = jnp.zeros_like(dz_next)
            dcw_ref[...] = jnp.zeros_like(dcw_ref)
            dsv_ref[...] = jnp.zeros_like(dsv_ref)

    def body(r, cur_ref, z_ref, pif_ref, cw_ref, sv_ref, dym_ref, cst_ref, nst_ref,
             dm_ref, dif_ref, dcw_ref, dsv_ref, dC, dn, dz_next, dqk):
        z = z_ref[...]
        sgz = _sigmoid(z)
        qk = z * sgz
        gt = pif_ref[...] + sv_ref[1:2, 0:LANES]
        r_i = lax.broadcasted_iota(jnp.int32, (L, L), 0)
        c_i = lax.broadcasted_iota(jnp.int32, (L, L), 1)
        tri = c_i <= r_i
        eye = c_i == r_i
        b_all = _tri_matmul(tri.astype(BF16), _log_sigmoid(gt))
        lane = lax.broadcasted_iota(jnp.int32, (L, LANES), 1)
        rowl = lax.broadcasted_iota(jnp.int32, (L, 1), 0)
        nmv = nst_ref[0]
        heads = range(MLSTM_HEADS)
        c_prev = [cst_ref[0, h] for h in heads]
        f = _mlstm_heads_fwd(qk, cur_ref, gt, b_all, c_prev, nmv, tri, eye)
        hv, inv_g, den, e_t, u_col, n_prev = f["hv"], f["inv_g"], f["den"], f["e_t"], f["u_col"], f["n_prev"]
        q, k, v, qb, kb, vb, Sc, Scb, W = f["q"], f["k"], f["v"], f["qb"], f["kb"], f["vb"], f["Sc"], f["Scb"], f["W"]
        xc = hv - _rowmean(hv)
        rstd = lax.rsqrt(_rowmean(xc * xc) + NORM_EPS)
        hhat = xc * rstd
        wn = _stack(lambda h: jnp.broadcast_to(sv_ref[0:1, h * HD:(h + 1) * HD], (L, HD)))
        so = _sigmoid(_stack(lambda h: cur_ref[:, 3 * W4 + h * HD:3 * W4 + (h + 1) * HD]))
        dy = _stack(lambda h: dym_ref[:, h * HD:(h + 1) * HD])
        d_o = (dy * hhat * wn * (so * (1.0 - so))).astype(BF16)
        dln = dy * so
        dwn = dln * hhat
        dhhat = dln * wn
        dh = rstd * (dhhat - _rowmean(dhhat) - hhat * _rowmean(dhhat * hhat))
        dnum = dh * inv_g
        active = jnp.abs(den) > f["floor"]
        dden = jnp.where(active, -jnp.sum(dh * hv, axis=1, keepdims=True) * inv_g * jnp.where(den >= 0.0, 1.0, -1.0), 0.0)
        dnumb = dnum.astype(BF16)
        dSc = _stack(lambda h: _dot_nt(_head(dnumb, h), _head(vb, h))) + dden
        dA = (dSc * W).astype(BF16)
        G = dSc * Sc
        Gr = jnp.sum(G, axis=1, keepdims=True)
        Gc = _stack(lambda h: jnp.sum(jnp.where(eye, _colsum(_head(G, h)), 0.0), axis=1, keepdims=True))
        dCn = [dC[h] for h in heads]
        dCnb = [d.astype(BF16) for d in dCn]
        dnv = dn[...]
        dn_new = _stack(lambda h: jnp.broadcast_to(dnv[h:h + 1, :], (L, HD)))
        kdC = _stack(lambda h: _dot(_head(kb, h), dCnb[h]))
        vdC = _stack(lambda h: _dot_nt(_head(vb, h), dCnb[h]))
        dv = (_stack(lambda h: _dot_tn(_head(Scb, h), _head(dnumb, h))) + u_col * kdC).astype(BF16)
        dq = _stack(lambda h: _dot(_head(dA, h), _head(kb, h))) \
            + e_t * _stack(lambda h: _dot_nt(_head(dnumb, h), f["cb"][h])) + (e_t * dden) * n_prev
        dk = (_stack(lambda h: _dot_tn(_head(dA, h), _head(qb, h))) + u_col * (vdC + dn_new)) * (HD ** -0.5)
        E = (jnp.sum(f["P1"] * dnum, axis=1, keepdims=True) + dden * f["qn"]) * e_t
        U = (jnp.sum(kdC * v, axis=1, keepdims=True) + jnp.sum(k * dn_new, axis=1, keepdims=True)) * u_col
        qe = (q * e_t).astype(BF16)
        qd = (e_t * dden) * q
        di = Gc + U
        db = Gr + E - Gc - U
        di_tile = jnp.zeros((L, LANES), F32)
        db_tile = jnp.zeros((L, LANES), F32)
        dn_rows = []
        for h in heads:
            dec = f["dec"][h * L:h * L + 1, :]
            ddec = jnp.sum(dCn[h] * c_prev[h]) + jnp.sum(dnv[h:h + 1, :] * nmv[h:h + 1, :])
            dbl = ddec * dec + jnp.sum(_head(U, h), axis=0, keepdims=True)
            di_tile = jnp.where(lane == h, _head(di, h), di_tile)
            db_tile = jnp.where(lane == 4 + h, _head(db, h) + jnp.where(rowl == L - 1, dbl, 0.0), db_tile)
            dC[h] = dec * dCn[h] + _dot_tn(_head(qe, h), _head(dnumb, h))
            dn_rows.append(dec * dnv[h:h + 1, :] + _colsum(_head(qd, h)))
            dsv_ref[0:1, h * HD:(h + 1) * HD] += _colsum(_head(dwn, h))
            dqk[:, h * HD:(h + 1) * HD] = _head(dq, h)
            dqk[:, W4 + h * HD:W4 + (h + 1) * HD] = _head(dk, h)
            dm_ref[:, 2 * W4 + h * HD:2 * W4 + (h + 1) * HD] = _head(dv, h)
            dm_ref[:, 3 * W4 + h * HD:3 * W4 + (h + 1) * HD] = _head(d_o, h)
        dn[...] = jnp.concatenate(dn_rows + [jnp.zeros((8 - MLSTM_HEADS, LANES), F32)], axis=0)
        dlf = _tri_matmul((r_i <= c_i).astype(BF16), db_tile)
        dif = jnp.where(lane < 4, di_tile, jnp.where(lane < 8, dlf * (1.0 - _sigmoid(gt)), 0.0))
        dif_ref[...] = dif.astype(BF16)
        dsv_ref[1:2, 0:LANES] += _colsum(dif)
        dz = dqk[...] * (sgz * (1.0 + z * (1.0 - sgz)))
        dcw_ref[4:5, :] += _colsum(dz)
        u = cur_ref[:, 0:2 * W4]
        du_in = dz * cw_ref[3:4, :]
        dcw_ref[3:4, :] += _colsum(dz * u)
        for k in range(1, 4):
            up = _shift_rows(dz, dz_next[...], k, False)
            dcw_ref[3 - k:4 - k, :] += _colsum(up * u)
            du_in = du_in + up * cw_ref[3 - k:4 - k, :]
        dz_next[...] = dz[0:8, :]
        dm_ref[:, 0:2 * W4] = du_in.astype(BF16)

    cidx = lambda r: nc - 1 - r
    return dict(
        init=init, body=body, operands=[pm, zc, pif, cw, sv, dym, cst, nst],
        in_specs=[pl.BlockSpec((L, M_W), lambda r: (cidx(r), 0)), pl.BlockSpec((L, 2 * W4), lambda r: (cidx(r), 0)),
                  pl.BlockSpec((L, IF_W), lambda r: (cidx(r), 0)), _full(cw.shape), _full(sv.shape),
                  pl.BlockSpec((L, W4), lambda r: (cidx(r), 0)),
                  pl.BlockSpec((1, MLSTM_HEADS, HD, HD), lambda r: (cidx(r), 0, 0, 0)),
                  pl.BlockSpec((1, 8, LANES), lambda r: (cidx(r), 0, 0))],
        out_specs=[pl.BlockSpec((L, M_W), lambda r: (cidx(r), 0)), pl.BlockSpec((L, IF_W), lambda r: (cidx(r), 0)),
                   _full((8, 2 * W4)), _full((8, W4))],
        out_shape=[jax.ShapeDtypeStruct((S, M_W), BF16), jax.ShapeDtypeStruct((S, IF_W), BF16),
                   jax.ShapeDtypeStruct((8, 2 * W4), F32), jax.ShapeDtypeStruct((8, W4), F32)],
        scratch=[pltpu.VMEM((MLSTM_HEADS, HD, HD), F32), pltpu.VMEM((8, LANES), F32),
                 pltpu.VMEM((8, 2 * W4), F32), pltpu.VMEM((L, 2 * W4), F32)])


def _rope_tables(positions):
    half = HEAD_DIM // 2
    inv_freq = ROPE_THETA ** (-2.0 * jnp.arange(half, dtype=F32) / HEAD_DIM)
    ang = positions.astype(F32)[:, None] * inv_freq
    cos = jnp.tile(jnp.cos(ang), (1, LANES // half))
    sign = jnp.tile(jnp.concatenate([-jnp.ones((half,), F32), jnp.ones((half,), F32)]), LANES // HEAD_DIM)
    sin = jnp.tile(jnp.sin(ang), (1, LANES // half)) * sign
    return cos, sin


def _local_step(x, tgt, positions, mod, gains, w_cat, w_ba, w_bm, w_out, w_gate, w_up, w_down,
                conv_w, conv_b, b_if, sinks, norm_w):
    t = _tables(mod, gains, conv_w, conv_b, b_if, norm_w, positions)
    a = _mixer_fwd(x, t, sinks, w_cat)
    b = _ffn_part(x, tgt, t, a, w_ba, w_bm, w_out, w_gate, w_up, w_down)
    c = _mixer_bwd(b["dx1"], t, a, b, sinks, w_ba, w_bm, w_out)
    grad_x, acc_p = _pre_bwd(c["dproj"], x, b["dx1"], t["vecs"], w_cat)
    big = dict(w_cat=c["g_w_cat"], w_ba=c["g_w_ba"], w_bm=c["g_w_bm"], w_out=c["g_w_out"], w_gate=b["g_w_gate"],
               w_up=b["g_w_up"], w_down=b["g_w_down"])
    return b["loss"], grad_x, big, _small_grads(acc_p, b, c)


def _tables(mod, gains, conv_w, conv_b, b_if, norm_w, positions):
    cos, sin = _rope_tables(positions)
    return dict(
        vecs=jnp.concatenate([mod, gains, jnp.zeros((6, D_MODEL), F32)], axis=0),
        cw=jnp.concatenate([conv_w, conv_b.reshape(1, -1), jnp.zeros((3, 2 * 512), F32)], axis=0),
        sv=jnp.zeros((8, 512), F32).at[0].set(norm_w).at[1, 0:8].set(b_if), cos=cos, sin=sin)


def _mixer_fwd(x, t, sinks, w_cat):
    h, pa, pm, pif, pg = _pre_proj(x, t["vecs"], w_cat)
    n_blk = x.shape[0] // ATTN_BLOCK
    (ya, qr, kr, vb, lse), = _fused_call([_attn_fwd_part(pa, t["cos"], t["sin"], sinks)], "attn_fwd", n_blk)
    (ym, zc, cst, nst), = _fused_call([_mlstm_fwd_part(pm, pif, t["cw"], t["sv"])], "mlstm_fwd", n_blk)
    return dict(h=h, pm=pm, pif=pif, pg=pg, ya=ya, qr=qr, kr=kr, vb=vb, lse=lse, ym=ym, zc=zc, cst=cst, nst=nst)


def _ffn_part(x, tgt, t, a, w_ba, w_bm, w_out, w_gate, w_up, w_down):
    x1, merged, mix, pba, pbm = _mix_fwd(x, a["ya"], a["ym"], a["pg"], t["vecs"], w_ba, w_bm, w_out)
    dx1, h2, hid, da, du, dff, acc_f, loss = _ffn_fwd_bwd(x1, tgt, t["vecs"], w_gate, w_up, w_down)
    return dict(merged=merged, mix=mix, pba=pba, pbm=pbm, dx1=dx1, acc_f=acc_f, loss=loss[0, 0],
                g_w_gate=_matmul_tn(da, h2, 1024, "dw_ffn_gate"),
                g_w_up=_matmul_tn(du, h2, 1024, "dw_ffn_up"),
                g_w_down=_matmul_tn(hid, dff, 1024, "dw_ffn_down"))


def _mixer_bwd(dx1, t, a, b, sinks, w_ba, w_bm, w_out):
    dmix, dpa, dpb, dg, dya, dym, acc_m = _mix_bwd(dx1, b["mix"], b["pba"], b["pbm"], a["pg"], t["vecs"], w_ba, w_bm, w_out)
    g_w_out = _matmul_tn(b["merged"], dmix, 1024, "dw_out")
    g_w_ba = _matmul_tn(a["ya"], dpa, 1024, "dw_branch_attn")
    g_w_bm = _matmul_tn(a["ym"], dpb, 1024, "dw_branch_mlstm")
    n_blk = dx1.shape[0] // ATTN_BLOCK
    (dq, dkv, dkv_last, dsink), = _fused_call(
        [_attn_bwd_part(dya, a["qr"], a["kr"], a["vb"], a["lse"], t["cos"], t["sin"], sinks)], "attn_bwd", n_blk)
    (dm, dif, dcw, dsv), = _fused_call(
        [_mlstm_bwd_part(a["pm"], a["zc"], a["pif"], t["cw"], t["sv"], dym, a["cst"], a["nst"])], "mlstm_bwd", n_blk)
    dkv = lax.dynamic_update_slice(dkv, dkv_last, (dkv.shape[0] - ATTN_BLOCK, 0))
    dproj = jnp.concatenate([dq, dkv, dm, dif, dg], axis=1)
    return dict(dproj=dproj, g_w_cat=_matmul_tn(a["h"], dproj, 1664, "dw_in"), g_w_out=g_w_out, g_w_ba=g_w_ba,
                g_w_bm=g_w_bm, acc_m=acc_m, dsink=dsink, dcw=dcw, dsv=dsv)


def _small_grads(acc_p, b, c):
    acc_f, acc_m = b["acc_f"], c["acc_m"]
    dmod = jnp.stack([acc_p[1], acc_p[0], acc_m[0], acc_f[3], acc_f[2], acc_f[0]])
    dgains = jnp.stack([acc_p[2], acc_m[1], acc_f[4], acc_f[1]])
    return dict(dmod=dmod, dgains=dgains, dconv_w=c["dcw"][0:4], dconv_b=c["dcw"][4], db_if=c["dsv"][1, 0:8],
                dsinks=c["dsink"][0, 0:8], dnorm_w=c["dsv"][0])


MESH_ID = pl.DeviceIdType.MESH


def _mesh_pos():
    return lax.axis_index("x"), lax.axis_index("y"), lax.axis_index("c")


def _flip(v, bit):
    return 1 - v if bit else v


def _relations():
    return [((r >> 2) & 1, (r >> 1) & 1, r & 1) for r in range(1, N_DEV)]


def _small_exchange(p, gather, name):
    R, V = p.shape[-2:]

    def body(p_ref, out_ref, send_sems, recv_sems):
        x, y, c = _mesh_pos()
        me = 4 * x + 2 * y + c
        out_ref[me] = p_ref[...] if gather else p_ref[me]
        peers = []
        for dx, dy, dc in _relations():
            px, py, pc = _flip(x, dx), _flip(y, dy), _flip(c, dc)
            peers.append(((px, py, pc), 4 * px + 2 * py + pc))

        def copy(k, landing):
            peer, pid = peers[k]
            return pltpu.make_async_remote_copy(
                src_ref=p_ref if gather else p_ref.at[pid], dst_ref=out_ref.at[landing],
                send_sem=send_sems.at[k], recv_sem=recv_sems.at[k], device_id=peer, device_id_type=MESH_ID)

        sends = [copy(k, me) for k in range(N_DEV - 1)]
        for cp in sends:
            cp.start()
        for k in range(N_DEV - 1):
            copy(k, peers[k][1]).wait_recv()
        for cp in sends:
            cp.wait_send()

    vm = pl.BlockSpec(memory_space=pltpu.VMEM)
    return pl.pallas_call(
        body, name=name, in_specs=[vm], out_specs=vm,
        out_shape=jax.ShapeDtypeStruct((N_DEV, R, V), F32),
        scratch_shapes=[pltpu.SemaphoreType.DMA((N_DEV - 1,)), pltpu.SemaphoreType.DMA((N_DEV - 1,))],
        compiler_params=pltpu.CompilerParams(vmem_limit_bytes=VMEM_LIMIT),
    )(p)


HBM_SPEC = pl.BlockSpec(memory_space=pltpu.HBM)
SEM_SPEC = pl.BlockSpec(memory_space=pltpu.SEMAPHORE)


def _peers(x, y, c):
    out = []
    for dx, dy, dc in _relations():
        px, py, pc = _flip(x, dx), _flip(y, dy), _flip(c, dc)
        out.append(((px, py, pc), 4 * px + 2 * py + pc))
    return out


def _exchange_start(arrs, gather, after, name):
    n = len(arrs)
    lands = [lax.empty(((N_DEV,) + a.shape) if gather else a.shape, a.dtype) for a in arrs]

    def body(*refs):
        a_refs, l_refs = refs[:n], refs[n:2 * n]
        send_sems, recv_sems = refs[2 * n + 1], refs[2 * n + 2]
        token = refs[4 * n + 3]
        x, y, c = _mesh_pos()
        me = 4 * x + 2 * y + c
        for a in range(n):
            for k, (peer, pid) in enumerate(_peers(x, y, c)):
                pltpu.make_async_remote_copy(
                    src_ref=a_refs[a] if gather else a_refs[a].at[pid], dst_ref=l_refs[a].at[me],
                    send_sem=send_sems.at[a * (N_DEV - 1) + k], recv_sem=recv_sems.at[a * (N_DEV - 1) + k],
                    device_id=peer, device_id_type=MESH_ID).start()
        token[...] = jnp.zeros_like(token)

    sem = pltpu.SemaphoreType.DMA((n * (N_DEV - 1),))
    hbm = lambda a: pltpu.with_memory_space_constraint(a, pltpu.HBM)
    res = pl.pallas_call(
        body, name=name,
        out_shape=(sem, sem, *[pltpu.HBM(a.shape, a.dtype) for a in arrs], *[pltpu.HBM(l.shape, l.dtype) for l in lands],
                   jax.ShapeDtypeStruct((8, LANES), F32)),
        in_specs=[HBM_SPEC] * (2 * n) + [pl.BlockSpec(memory_space=pl.ANY)],
        out_specs=(SEM_SPEC, SEM_SPEC, *[HBM_SPEC] * (2 * n), pl.BlockSpec(memory_space=pltpu.VMEM)),
        input_output_aliases={i: 2 + i for i in range(2 * n)},
        compiler_params=pltpu.CompilerParams(has_side_effects=pltpu.SideEffectType.DATAFLOW_SIDE_EFFECTING),
    )(*[hbm(a) for a in arrs], *[hbm(l) for l in lands], after)
    return dict(sems=res[0:2], arrs=res[2:2 + n], lands=res[2 + n:2 + 2 * n], token=res[2 + 2 * n], gather=gather)


def _exchange_wait(st, after, name):
    n = len(st["arrs"])
    gather = st["gather"]

    def body(*refs):
        a_refs, l_refs = refs[:n], refs[n:2 * n]
        send_sems, recv_sems = refs[2 * n], refs[2 * n + 1]
        x, y, c = _mesh_pos()
        for a in range(n):
            for k, (peer, pid) in enumerate(_peers(x, y, c)):
                cp = pltpu.make_async_remote_copy(
                    src_ref=a_refs[a] if gather else a_refs[a].at[pid], dst_ref=l_refs[a].at[pid],
                    send_sem=send_sems.at[a * (N_DEV - 1) + k], recv_sem=recv_sems.at[a * (N_DEV - 1) + k],
                    device_id=peer, device_id_type=MESH_ID)
                cp.wait_send()
                cp.wait_recv()

    both = list(st["arrs"]) + list(st["lands"])
    res = pl.pallas_call(
        body, name=name, out_shape=[pltpu.HBM(a.shape, a.dtype) for a in both],
        in_specs=[HBM_SPEC] * (2 * n) + [SEM_SPEC, SEM_SPEC, pl.BlockSpec(memory_space=pl.ANY)],
        out_specs=[HBM_SPEC] * (2 * n), input_output_aliases={i: i for i in range(2 * n)},
        compiler_params=pltpu.CompilerParams(has_side_effects=pltpu.SideEffectType.DATAFLOW_SIDE_EFFECTING),
    )(*both, *st["sems"], after)
    return _fill_own(res[n:2 * n], res[0:n], gather, name + "_own")


def _fill_own(lands, arrs, gather, name):
    n = len(lands)

    def body(*refs):
        a_refs, l_refs, sems = refs[:n], refs[n:2 * n], refs[3 * n]
        x, y, c = _mesh_pos()
        me = 4 * x + 2 * y + c
        cps = [pltpu.make_async_copy(a_refs[a] if gather else a_refs[a].at[me], l_refs[a].at[me], sems.at[a])
               for a in range(n)]
        for cp in cps:
            cp.start()
        for cp in cps:
            cp.wait()

    hbm = pl.BlockSpec(memory_space=pl.ANY)
    return pl.pallas_call(
        body, name=name, in_specs=[hbm] * (2 * n), out_specs=[hbm] * n,
        out_shape=[jax.ShapeDtypeStruct(l.shape, l.dtype) for l in lands],
        input_output_aliases={n + i: i for i in range(n)},
        scratch_shapes=[pltpu.SemaphoreType.DMA((n,))],
    )(*arrs, *lands)


def _after(x, token):
    return lax.optimization_barrier((x, token))[0]


def _all_gather_hbm(shards):
    n = len(shards)

    def body(*refs):
        p_refs, out_refs = refs[:n], refs[n:2 * n]
        send_sems, recv_sems, local_sems = refs[2 * n:]
        x, y, c = _mesh_pos()
        me, sibling = (x, y, c), (x, y, 1 - c)
        chips = [(1 - x, y), (x, 1 - y), (1 - x, 1 - y)]

        def copy(a, k, block, to, own=False):
            slot = out_refs[a].at[4 * block[0] + 2 * block[1] + block[2]]
            return pltpu.make_async_remote_copy(
                src_ref=p_refs[a] if own else slot, dst_ref=slot,
                send_sem=send_sems.at[a, k], recv_sem=recv_sems.at[a, k], device_id=to, device_id_type=MESH_ID)

        mine = [pltpu.make_async_copy(p_refs[a], out_refs[a].at[4 * x + 2 * y + c], local_sems.at[a]) for a in range(n)]
        for cp in mine:
            cp.start()
        first = []
        for a in range(n):
            first.append(copy(a, 0, me, sibling, own=True))
            first += [copy(a, 1 + j, me, (*chip, c), own=True) for j, chip in enumerate(chips)]
        for cp in first:
            cp.start()
        passed = []
        for j, chip in enumerate(chips):
            for a in range(n):
                copy(a, 1 + j, (*chip, c), me).wait_recv()
                passed.append(copy(a, 4 + j, (*chip, c), sibling))
                passed[-1].start()
        for a in range(n):
            copy(a, 0, sibling, me).wait_recv()
            for j, chip in enumerate(chips):
                copy(a, 4 + j, (*chip, 1 - c), me).wait_recv()
        for cp in first + passed:
            cp.wait_send()
        for cp in mine:
            cp.wait()

    hbm = pl.BlockSpec(memory_space=pl.ANY)
    return pl.pallas_call(
        body, name="gather_weights", in_specs=[hbm] * n, out_specs=[hbm] * n,
        out_shape=[jax.ShapeDtypeStruct((N_DEV,) + s.shape, s.dtype) for s in shards],
        scratch_shapes=[pltpu.SemaphoreType.DMA((n, N_DEV - 1)), pltpu.SemaphoreType.DMA((n, N_DEV - 1)),
                        pltpu.SemaphoreType.DMA((n,))],
    )(*shards)


def _adamw(w, g, m, v):
    m2 = ADAM_B1 * m + (1.0 - ADAM_B1) * g
    v2 = ADAM_B2 * v + (1.0 - ADAM_B2) * (g * g)
    m_hat = m2 / (1.0 - ADAM_B1 ** ADAM_STEP)
    v_hat = v2 / (1.0 - ADAM_B2 ** ADAM_STEP)
    delta = -ADAM_LR * (m_hat / (jnp.sqrt(v_hat) + ADAM_EPS) + ADAM_WD * w)
    return delta, m2, v2


def _mod_partial(cmat, w_shard, b_shard):
    def body(c_ref, w_ref, b_ref, o_ref):
        o_ref[...] = _dot(c_ref[...].astype(BF16), w_ref[...].astype(BF16)) + b_ref[...]

    return pl.pallas_call(
        body, name="mod_partial", out_shape=jax.ShapeDtypeStruct((N_DEV, w_shard.shape[1]), F32),
        compiler_params=_params(),
    )(cmat, w_shard, b_shard)


def _adamw_w_ada(cmat, dmod_cols, w, m, v):
    def body(c_ref, d_ref, w_ref, m_ref, v_ref, g_ref, dl_ref, m2_ref, v2_ref):
        g = _dot_tn(c_ref[...].astype(BF16), d_ref[...].astype(BF16))
        g_ref[...] = g
        dl_ref[...], m2_ref[...], v2_ref[...] = _adamw(w_ref[...], g, m_ref[...], v_ref[...])

    return pl.pallas_call(
        body, name="adamw_w_ada", out_shape=[jax.ShapeDtypeStruct(w.shape, F32)] * 4,
        compiler_params=_params(),
    )(cmat, dmod_cols, w, m, v)


SMALL_ROWS = 16
SMALL_AT = {"b_ada": (0, 6, 0, D_MODEL), "g_pre_mix": (6, 1, 0, D_MODEL), "g_post_mix": (7, 1, 0, D_MODEL),
            "g_pre_ffn": (8, 1, 0, D_MODEL), "g_post_ffn": (9, 1, 0, D_MODEL), "conv_b": (10, 1, 0, D_MODEL),
            "mlstm_norm_w": (11, 1, 0, 512), "b_if": (11, 1, 512, LANES), "attn_sinks": (11, 1, 640, LANES)}


def _small_table(part):
    tail = jnp.concatenate([part["mlstm_norm_w"], jnp.pad(part["b_if"], (0, LANES - 8)),
                            jnp.pad(part["attn_sinks"], (0, LANES - 8)), jnp.zeros((256,), F32)])
    return jnp.concatenate([part["b_ada"], part["gains"], part["conv_b"][None], tail[None],
                            jnp.zeros((SMALL_ROWS - 12, D_MODEL), F32)], axis=0)


def _adamw_small(gathered, wmv):
    names = list(SMALL_AT)

    def body(*refs):
        g_ref, ins, outs = refs[0], refs[1:1 + 3 * len(names)], refs[1 + 3 * len(names):]
        g = g_ref[0]
        for k in range(1, N_DEV):
            g = g + g_ref[k]
        for i, n in enumerate(names):
            r0, rows, l0, lanes = SMALL_AT[n]
            gi = jnp.concatenate([g[r:r + 1, l0:l0 + lanes] for r in range(r0, r0 + rows)], axis=1)
            w_ref, m_ref, v_ref = ins[3 * i:3 * i + 3]
            go, dl, m2, v2 = outs[4 * i:4 * i + 4]
            go[...] = gi
            dl[...], m2[...], v2[...] = _adamw(w_ref[...], gi, m_ref[...], v_ref[...])

    flat = [a for n in names for a in wmv[n]]
    res = pl.pallas_call(
        body, name="adamw_small",
        out_shape=[jax.ShapeDtypeStruct(wmv[n][0].shape, F32) for n in names for _ in range(4)],
        compiler_params=_params(),
    )(gathered, *flat)
    return {n: res[4 * i:4 * i + 4] for i, n in enumerate(names)}


def _row_tile(rows):
    return rows // 4 if rows >= 512 else rows


def _sum_partials(r_ref):
    g = r_ref[0].astype(F32)
    for k in range(1, N_DEV):
        g = g + r_ref[k].astype(F32)
    return g


def _adamw_sum(recv, w, m, v, name):
    r, cdim = w.shape
    tr = _row_tile(r)

    def body(r_ref, w_ref, m_ref, v_ref, g_ref, dl_ref, m2_ref, v2_ref):
        g = _sum_partials(r_ref)
        g_ref[...] = g
        dl_ref[...], m2_ref[...], v2_ref[...] = _adamw(w_ref[...], g, m_ref[...], v_ref[...])

    row = pl.BlockSpec((tr, cdim), lambda i: (i, 0))
    return pl.pallas_call(
        body, name=name, grid=(r // tr,),
        in_specs=[pl.BlockSpec((N_DEV, tr, cdim), lambda i: (0, i, 0)), row, row, row],
        out_specs=[row] * 4, out_shape=[jax.ShapeDtypeStruct((r, cdim), F32)] * 4,
        compiler_params=_params(("parallel",)),
    )(recv, w, m, v)


def _sum8(recv, name):
    _, r, cdim = recv.shape
    tr = _row_tile(r)

    def body(r_ref, g_ref):
        g_ref[...] = _sum_partials(r_ref)

    return pl.pallas_call(
        body, name=name, grid=(r // tr,),
        in_specs=[pl.BlockSpec((N_DEV, tr, cdim), lambda i: (0, i, 0))],
        out_specs=pl.BlockSpec((tr, cdim), lambda i: (i, 0)), out_shape=jax.ShapeDtypeStruct((r, cdim), F32),
        compiler_params=_params(("parallel",)),
    )(recv)


def _adamw_plain(g, w, m, v, name):
    r, cdim = w.shape
    tr = _row_tile(r)

    def body(g_ref, w_ref, m_ref, v_ref, dl_ref, m2_ref, v2_ref):
        dl_ref[...], m2_ref[...], v2_ref[...] = _adamw(w_ref[...], g_ref[...], m_ref[...], v_ref[...])

    row = pl.BlockSpec((tr, cdim), lambda i: (i, 0))
    return pl.pallas_call(
        body, name=name, grid=(r // tr,), in_specs=[row] * 4, out_specs=[row] * 3,
        out_shape=[jax.ShapeDtypeStruct((r, cdim), F32)] * 3,
        compiler_params=_params(("parallel",)),
    )(g, w, m, v)


IN_SHARD = 609
IN_SHARD_PAD = 640
IF_AT = A_W + M_W


def _regrouped(u):
    return u if u < IF_AT + 8 else u + (IF_W - 8)


def _selection(k, rows, row0, transpose):
    shape = (rows, IN_SHARD_PAD) if transpose else (IN_SHARD_PAD, rows)
    l = lax.broadcasted_iota(jnp.int32, shape, 1 if transpose else 0)
    r = lax.broadcasted_iota(jnp.int32, shape, 0 if transpose else 1) + row0
    u = l + IN_SHARD * k
    ru = u + jnp.where(u >= IF_AT + 8, IF_W - 8, 0)
    return ((ru == r) & (l < IN_SHARD)).astype(BF16)


def _regroup_w_in(g):
    def body(g_ref, o_ref):
        for cb in range(CAT_W // LANES):
            r0 = cb * LANES
            acc = jnp.zeros((D_MODEL, LANES), F32)
            for k in range(N_DEV):
                lo, hi = _regrouped(IN_SHARD * k), _regrouped(IN_SHARD * k + IN_SHARD - 1)
                if hi >= r0 and lo < r0 + LANES:
                    acc = acc + _dot(g_ref[k], _selection(k, LANES, r0, False))
            o_ref[:, r0:r0 + LANES] = acc.astype(BF16)

    return pl.pallas_call(
        body, name="regroup_w_in", out_shape=jax.ShapeDtypeStruct((D_MODEL, CAT_W), BF16),
        compiler_params=_params(),
    )(g)


def _ungroup_w_in(g_cat):
    def body(g_ref, o_ref):
        for k in range(N_DEV):
            lo, hi = _regrouped(IN_SHARD * k), _regrouped(IN_SHARD * k + IN_SHARD - 1)
            w0, w1 = lo // LANES * LANES, (hi // LANES + 1) * LANES
            o_ref[k] = _dot(g_ref[:, w0:w1], _selection(k, w1 - w0, w0, True)).astype(BF16)

    return pl.pallas_call(
        body, name="ungroup_w_in", out_shape=jax.ShapeDtypeStruct((N_DEV, D_MODEL, IN_SHARD_PAD), BF16),
        compiler_params=_params(),
    )(g_cat)


WEIGHT_NAMES = ("w_ada", "b_ada", "g_pre_mix", "g_post_mix", "w_in", "b_if", "conv_w", "conv_b", "attn_sinks",
                "mlstm_norm_w", "w_branch_attn", "w_branch_mlstm", "w_out", "g_pre_ffn", "g_post_ffn",
                "w_ffn_gate", "w_ffn_up", "w_ffn_down")


def kernel(x, c, positions, w_ada, b_ada, g_pre_mix, g_post_mix, w_in, b_if, conv_w, conv_b, attn_sinks, mlstm_norm_w, w_branch_attn, w_branch_mlstm, w_out, g_pre_ffn, g_post_ffn, w_ffn_gate, w_ffn_up, w_ffn_down, loss_target, m_w_ada, m_b_ada, m_g_pre_mix, m_g_post_mix, m_w_in, m_b_if, m_conv_w, m_conv_b, m_attn_sinks, m_mlstm_norm_w, m_w_branch_attn, m_w_branch_mlstm, m_w_out, m_g_pre_ffn, m_g_post_ffn, m_w_ffn_gate, m_w_ffn_up, m_w_ffn_down, v_w_ada, v_b_ada, v_g_pre_mix, v_g_post_mix, v_w_in, v_b_if, v_conv_w, v_conv_b, v_attn_sinks, v_mlstm_norm_w, v_w_branch_attn, v_w_branch_mlstm, v_w_out, v_g_pre_ffn, v_g_post_ffn, v_w_ffn_gate, v_w_ffn_up, v_w_ffn_down):
    given = dict(locals())
    W = {n: given[n][0] for n in WEIGHT_NAMES}
    M = {n: given["m_" + n][0] for n in WEIGHT_NAMES}
    V = {n: given["v_" + n][0] for n in WEIGHT_NAMES}
    me = 4 * lax.axis_index("x") + 2 * lax.axis_index("y") + lax.axis_index("c")

    ff_sh = D_FF // N_DEV
    g_in, g_conv, cg = _all_gather_hbm([jnp.pad(W["w_in"], ((0, 0), (0, IN_SHARD_PAD - IN_SHARD))).astype(BF16),
                                        jnp.pad(W["conv_w"], ((0, 4), (0, 0))), c.reshape(8, D_MODEL // 8)])

    cmat = cg.reshape(N_DEV, D_MODEL)
    ada_w = D_MODEL * 6 // N_DEV
    b_cols = lax.dynamic_slice(W["b_ada"], (me * ada_w,), (ada_w,)).reshape(1, ada_w)
    mod_part = _mod_partial(cmat, W["w_ada"], b_cols)
    mod_recv = _small_exchange(jnp.broadcast_to(mod_part[:, None, :], (N_DEV, 8, ada_w)), False, "scatter_mod")
    mod = mod_recv[:, 0, :].reshape(6, D_MODEL)

    st_b = _exchange_start([W["w_branch_attn"].astype(BF16), W["w_branch_mlstm"].astype(BF16), W["w_out"].astype(BF16),
                            W["w_ffn_gate"].T.astype(BF16), W["w_ffn_up"].T.astype(BF16), W["w_ffn_down"].astype(BF16)],
                           True, mod_recv, "gather_rest_start")
    g_in = _after(g_in, st_b["token"])
    cols = lambda g: g.transpose(1, 0, 2).reshape(g.shape[1], N_DEV * g.shape[2])
    gains = jnp.stack([W["g_pre_mix"], W["g_post_mix"], W["g_pre_ffn"], W["g_post_ffn"]])
    xs, tgt = x[0], loss_target[0]
    t = _tables(mod, gains, cols(g_conv)[0:4], W["conv_b"], W["b_if"], W["mlstm_norm_w"], positions[0])
    w_cat = _regroup_w_in(g_in)
    a = _mixer_fwd(xs, t, W["attn_sinks"], w_cat)
    g_ba, g_bm, g_out, g_gate, g_up, g_down = _exchange_wait(st_b, a["ym"], "gather_rest_wait")
    w_ba, w_bm, w_out = cols(g_ba), cols(g_bm), g_out.reshape(D_MODEL, D_MODEL)
    b = _ffn_part(xs, tgt, t, a, w_ba, w_bm, w_out, g_gate.reshape(D_FF, D_MODEL), g_up.reshape(D_FF, D_MODEL),
                  g_down.reshape(D_FF, D_MODEL))

    st_f = _exchange_start([b["g_w_gate"].reshape(N_DEV, ff_sh, D_MODEL), b["g_w_up"].reshape(N_DEV, ff_sh, D_MODEL),
                            b["g_w_down"].reshape(N_DEV, ff_sh, D_MODEL)], False, b["dx1"], "scatter_ffn_start")
    cm = _mixer_bwd(_after(b["dx1"], st_f["token"]), t, a, b, W["attn_sinks"], w_ba, w_bm, w_out)
    pieces = lambda g, n: g.reshape(g.shape[0], N_DEV, n).transpose(1, 0, 2)
    st_m = _exchange_start([_ungroup_w_in(cm["g_w_cat"]), pieces(cm["g_w_ba"], 128), pieces(cm["g_w_bm"], 128),
                            cm["g_w_out"].reshape(N_DEV, D_MODEL // N_DEV, D_MODEL),
                            jnp.pad(pieces(cm["dcw"][0:4], 128), ((0, 0), (0, 4), (0, 0)))], False, cm["dproj"],
                           "scatter_mixer_start")
    r_gate, r_up, r_down = _exchange_wait(st_f, st_m["token"], "scatter_ffn_wait")
    grad_x, acc_p = _pre_bwd(_after(cm["dproj"], st_m["token"]), xs, b["dx1"], t["vecs"], w_cat)
    small = _small_grads(acc_p, b, cm)
    loss = b["loss"]

    big_out = [{} for _ in range(4)]

    def put(n, res):
        for k in range(4):
            big_out[k][n] = res[k][None]

    put("w_ffn_down", _adamw_sum(r_down, W["w_ffn_down"], M["w_ffn_down"], V["w_ffn_down"], "adamw_w_ffn_down"))
    for n, g in (("w_ffn_gate", _sum8(r_gate, "sum_w_ffn_gate").T), ("w_ffn_up", _sum8(r_up, "sum_w_ffn_up").T)):
        put(n, [g] + list(_adamw_plain(g, W[n], M[n], V[n], "adamw_" + n)))

    sg = _small_exchange(_small_table({"b_ada": small["dmod"], "gains": small["dgains"], "conv_b": small["dconv_b"],
                                       "mlstm_norm_w": small["dnorm_w"], "b_if": small["db_if"],
                                       "attn_sinks": small["dsinks"]}), True, "gather_small")
    as_row = lambda a, n: jnp.pad(a, (0, SMALL_AT[n][3] * SMALL_AT[n][1] - a.shape[0]))[None]
    small_res = _adamw_small(sg, {n: [as_row(d[n], n) for d in (W, M, V)] for n in SMALL_AT})
    small_out = [{n: small_res[n][k][:, 0:W[n].shape[0]] for n in SMALL_AT} for k in range(4)]
    dmod_cols = lax.dynamic_slice(sg[:, 0:6, :].reshape(N_DEV, 6 * D_MODEL), (0, me * ada_w), (N_DEV, ada_w))
    ada_out = _adamw_w_ada(cmat, dmod_cols, W["w_ada"], M["w_ada"], V["w_ada"])

    r_in, r_ba, r_bm, r_out, r_conv = _exchange_wait(st_m, grad_x, "scatter_mixer_wait")
    for n, r in (("w_branch_attn", r_ba), ("w_branch_mlstm", r_bm), ("w_out", r_out)):
        put(n, _adamw_sum(r, W[n], M[n], V[n], "adamw_" + n))
    pad4 = lambda v: jnp.pad(v, ((0, 4), (0, 0)))
    put("conv_w", [o[0:4] for o in _adamw_sum(r_conv, pad4(W["conv_w"]), pad4(M["conv_w"]), pad4(V["conv_w"]),
                                                 "adamw_conv_w")])
    g = _sum8(r_in, "sum_w_in")[:, 0:IN_SHARD]
    put("w_in", [g] + list(_adamw_plain(g, W["w_in"], M["w_in"], V["w_in"], "adamw_w_in")))

    total = lax.psum(loss, ("x", "y", "c"))
    outs = [total, grad_x[None]]
    for k in range(4):
        for n in WEIGHT_NAMES:
            if n == "w_ada":
                outs.append(ada_out[k][None])
            elif n in big_out[k]:
                outs.append(big_out[k][n])
            else:
                outs.append(small_out[k][n])
    return tuple(outs)
```

```python
import functools

import jax
import jax.numpy as jnp
import numpy as np
from jax import lax
from jax.experimental import pallas as pl
from jax.experimental.pallas import tpu as pltpu

F32 = jnp.float32
BF16 = jnp.bfloat16

N_DEV = 8
D_MODEL = 1024
D_FF = 2816
N_Q_HEADS = 8
HEAD_DIM = 64
ATTN_BLOCK = 128
ROPE_THETA = 10000.0
MLSTM_HEADS = 4
MLSTM_HEAD_DIM = 128
MLSTM_CHUNK = 128
NORM_EPS = 1e-6
ADAM_LR = 0.001
ADAM_B1 = 0.9
ADAM_B2 = 0.999
ADAM_EPS = 1e-08
ADAM_WD = 0.01
ADAM_STEP = 10

ROW_TILE = 256
WIDE_TILE = 512
LANES = 128
NEG = -1e30
VMEM_LIMIT = 56 * 1024 * 1024

A_W = 768
M_W = 2048
IF_W = 128
G_W = 2048
CAT_W = A_W + M_W + IF_W + G_W

R_SHIFT_M, R_SCALE_M, R_GATE_M, R_SHIFT_F, R_SCALE_F, R_GATE_F = 0, 1, 2, 3, 4, 5
R_G_PRE_MIX, R_G_POST_MIX, R_G_PRE_FFN, R_G_POST_FFN = 6, 7, 8, 9


def _dot(a, b):
    return jnp.dot(a, b, preferred_element_type=F32)


def _dot_nt(a, b):
    return lax.dot_general(a, b, (((1,), (1,)), ((), ())), preferred_element_type=F32)


def _dot_tn(a, b):
    return lax.dot_general(a, b, (((0,), (0,)), ((), ())), preferred_element_type=F32)


def _recip(x):
    return 1.0 / x


def _sigmoid(x):
    return _recip(1.0 + jnp.exp(-x))


def _colsum(x):
    return jnp.sum(x, axis=0, keepdims=True)


def _rowmean(x):
    return jnp.mean(x, axis=-1, keepdims=True)


def _params(sem=None, vmem=VMEM_LIMIT):
    kw = dict(vmem_limit_bytes=vmem)
    if sem is not None:
        kw["dimension_semantics"] = sem
    return pltpu.CompilerParams(**kw)


def _full(shape):
    nd = len(shape)
    return pl.BlockSpec(shape, lambda *_: (0,) * nd)


def _pre_proj(x, vecs, w_cat):
    S = x.shape[0]
    tm = WIDE_TILE

    def body(x_ref, v_ref, w_ref, h_ref, pa_ref, pm_ref, pif_ref, pg_ref):
        xv = x_ref[...]
        r = lax.rsqrt(_rowmean(xv * xv) + NORM_EPS)
        h = (xv * r * v_ref[R_G_PRE_MIX:R_G_PRE_MIX + 1, :]) * (1.0 + v_ref[R_SCALE_M:R_SCALE_M + 1, :]) \
            + v_ref[R_SHIFT_M:R_SHIFT_M + 1, :]
        hb = h.astype(BF16)
        h_ref[...] = hb
        pa_ref[...] = _dot(hb, w_ref[:, 0:A_W])
        pm_ref[...] = _dot(hb, w_ref[:, A_W:A_W + M_W])
        pif_ref[...] = _dot(hb, w_ref[:, A_W + M_W:A_W + M_W + IF_W])
        pg_ref[...] = _dot(hb, w_ref[:, A_W + M_W + IF_W:CAT_W]).astype(BF16)

    row = lambda w: pl.BlockSpec((tm, w), lambda i: (i, 0))
    return pl.pallas_call(
        body, name="pre_proj", grid=(S // tm,),
        in_specs=[row(D_MODEL), _full(vecs.shape), _full(w_cat.shape)],
        out_specs=[row(D_MODEL), row(A_W), row(M_W), row(IF_W), row(G_W)],
        out_shape=[jax.ShapeDtypeStruct((S, D_MODEL), BF16), jax.ShapeDtypeStruct((S, A_W), F32),
                   jax.ShapeDtypeStruct((S, M_W), F32), jax.ShapeDtypeStruct((S, IF_W), F32),
                   jax.ShapeDtypeStruct((S, G_W), BF16)],
        compiler_params=_params(("parallel",)),
    )(x, vecs, w_cat)


def _mix_fwd(x, ya, ym, pg, vecs, w_ba, w_bm, w_out):
    S = x.shape[0]
    tm = WIDE_TILE

    def body(x_ref, ya_ref, ym_ref, pg_ref, v_ref, wba_ref, wbm_ref, wout_ref,
             x1_ref, merged_ref, mix_ref, pa_ref, pb_ref):
        pa = _dot(ya_ref[...], wba_ref[...])
        pb = _dot(ym_ref[...], wbm_ref[...])
        merged = _sigmoid(pg_ref[:, 0:D_MODEL].astype(F32)) * pa + _sigmoid(pg_ref[:, D_MODEL:G_W].astype(F32)) * pb
        mb = merged.astype(BF16)
        mix = _dot(mb, wout_ref[...])
        r = lax.rsqrt(_rowmean(mix * mix) + NORM_EPS)
        x1_ref[...] = x_ref[...] + v_ref[R_GATE_M:R_GATE_M + 1, :] * (mix * r * v_ref[R_G_POST_MIX:R_G_POST_MIX + 1, :])
        merged_ref[...] = mb
        mix_ref[...] = mix
        pa_ref[...] = pa.astype(BF16)
        pb_ref[...] = pb.astype(BF16)

    row = lambda w: pl.BlockSpec((tm, w), lambda i: (i, 0))
    sd = lambda w, dt: jax.ShapeDtypeStruct((S, w), dt)
    return pl.pallas_call(
        body, name="mix_fwd", grid=(S // tm,),
        in_specs=[row(D_MODEL), row(512), row(512), row(G_W), _full(vecs.shape), _full(w_ba.shape),
                  _full(w_bm.shape), _full(w_out.shape)],
        out_specs=[row(D_MODEL)] * 5,
        out_shape=[sd(D_MODEL, F32), sd(D_MODEL, BF16), sd(D_MODEL, F32), sd(D_MODEL, BF16), sd(D_MODEL, BF16)],
        compiler_params=_params(("parallel",)),
    )(x, ya, ym, pg, vecs, w_ba, w_bm, w_out)


def _ffn_fwd_bwd(x1, tgt, vecs, w_gate, w_up, w_down):
    S = x1.shape[0]
    tm = ROW_TILE

    def body(x1_ref, tgt_ref, v_ref, wg_hbm, wu_hbm, wd_hbm,
             dx1_ref, h2_ref, hid_ref, da_ref, du_ref, dff_ref, acc_ref, loss_ref,
             wg, wu, wd, sem):
        i = pl.program_id(0)

        @pl.when(i == 0)
        def _():
            cps = [pltpu.make_async_copy(wg_hbm, wg, sem.at[0]), pltpu.make_async_copy(wu_hbm, wu, sem.at[1]),
                   pltpu.make_async_copy(wd_hbm, wd, sem.at[2])]
            for cp in cps:
                cp.start()
            for cp in cps:
                cp.wait()
            acc_ref[...] = jnp.zeros_like(acc_ref)
            loss_ref[...] = jnp.zeros_like(loss_ref)

        vrow = lambda r: v_ref[r:r + 1, :]
        x1v = x1_ref[...]
        r3 = lax.rsqrt(_rowmean(x1v * x1v) + NORM_EPS)
        x1hat = x1v * r3
        xn3 = x1hat * vrow(R_G_PRE_FFN)
        h2b = (xn3 * (1.0 + vrow(R_SCALE_F)) + vrow(R_SHIFT_F)).astype(BF16)
        h2_ref[...] = h2b
        a = _dot_nt(h2b, wg[...])
        u = _dot_nt(h2b, wu[...])
        sg = _sigmoid(a)
        sil = a * sg
        hidb = (sil * u).astype(BF16)
        hid_ref[...] = hidb
        ff = _dot(hidb, wd[...])
        r4 = lax.rsqrt(_rowmean(ff * ff) + NORM_EPS)
        ffhat = ff * r4
        n4 = ffhat * vrow(R_G_POST_FFN)
        err = x1v + vrow(R_GATE_F) * n4 - tgt_ref[...]
        loss_ref[...] += jnp.sum(err * err) * (0.5 / D_MODEL)
        dy = err * (1.0 / D_MODEL)
        acc_ref[0:1, :] += _colsum(dy * n4)
        dn4 = dy * vrow(R_GATE_F)
        acc_ref[1:2, :] += _colsum(dn4 * ffhat)
        dffhat = dn4 * vrow(R_G_POST_FFN)
        dffb = (r4 * (dffhat - ffhat * _rowmean(dffhat * ffhat))).astype(BF16)
        dff_ref[...] = dffb
        dhid = _dot_nt(dffb, wd[...])
        dub = (dhid * sil).astype(BF16)
        dab = (dhid * u * (sg * (1.0 + a * (1.0 - sg)))).astype(BF16)
        da_ref[...] = dab
        du_ref[...] = dub
        dh2 = _dot(dab, wg[...]) + _dot(dub, wu[...])
        acc_ref[2:3, :] += _colsum(dh2 * xn3)
        acc_ref[3:4, :] += _colsum(dh2)
        dxn3 = dh2 * (1.0 + vrow(R_SCALE_F))
        acc_ref[4:5, :] += _colsum(dxn3 * x1hat)
        dx1hat = dxn3 * vrow(R_G_PRE_FFN)
        dx1_ref[...] = dy + r3 * (dx1hat - x1hat * _rowmean(dx1hat * x1hat))

    row = lambda w: pl.BlockSpec((tm, w), lambda i: (i, 0))
    sd = lambda w, dt: jax.ShapeDtypeStruct((S, w), dt)
    anyspec = pl.BlockSpec(memory_space=pl.ANY)
    return pl.pallas_call(
        body, name="ffn_fwd_bwd", grid=(S // tm,),
        in_specs=[row(D_MODEL), row(D_MODEL), _full(vecs.shape), anyspec, anyspec, anyspec],
        out_specs=[row(D_MODEL), row(D_MODEL), row(D_FF), row(D_FF), row(D_FF), row(D_MODEL),
                   _full((8, D_MODEL)), _full((8, LANES))],
        out_shape=[sd(D_MODEL, F32), sd(D_MODEL, BF16), sd(D_FF, BF16), sd(D_FF, BF16), sd(D_FF, BF16),
                   sd(D_MODEL, BF16), jax.ShapeDtypeStruct((8, D_MODEL), F32), jax.ShapeDtypeStruct((8, LANES), F32)],
        scratch_shapes=[pltpu.VMEM(w_gate.shape, BF16), pltpu.VMEM(w_up.shape, BF16), pltpu.VMEM(w_down.shape, BF16),
                        pltpu.SemaphoreType.DMA((3,))],
        compiler_params=_params(("arbitrary",)),
    )(x1, tgt, vecs, w_gate, w_up, w_down)


def _mix_bwd(dx1, mix, pa, pb, pg, vecs, w_ba, w_bm, w_out):
    S = dx1.shape[0]
    tm = WIDE_TILE

    def body(dx1_ref, mix_ref, pa_ref, pb_ref, pg_ref, v_ref, wba_ref, wbm_ref, wout_ref,
             dmix_ref, dpa_ref, dpb_ref, dg_ref, dya_ref, dym_ref, acc_ref):
        i = pl.program_id(0)

        @pl.when(i == 0)
        def _():
            acc_ref[...] = jnp.zeros_like(acc_ref)

        vrow = lambda r: v_ref[r:r + 1, :]
        dx1v = dx1_ref[...]
        mix = mix_ref[...]
        r2 = lax.rsqrt(_rowmean(mix * mix) + NORM_EPS)
        mixhat = mix * r2
        acc_ref[0:1, :] += _colsum(dx1v * (mixhat * vrow(R_G_POST_MIX)))
        dn2 = dx1v * vrow(R_GATE_M)
        acc_ref[1:2, :] += _colsum(dn2 * mixhat)
        dmixhat = dn2 * vrow(R_G_POST_MIX)
        dmixb = (r2 * (dmixhat - mixhat * _rowmean(dmixhat * mixhat))).astype(BF16)
        dmix_ref[...] = dmixb
        dmerged = _dot_nt(dmixb, wout_ref[...])
        sa = _sigmoid(pg_ref[:, 0:D_MODEL].astype(F32))
        sm = _sigmoid(pg_ref[:, D_MODEL:G_W].astype(F32))
        dpab = (dmerged * sa).astype(BF16)
        dpbb = (dmerged * sm).astype(BF16)
        dpa_ref[...] = dpab
        dpb_ref[...] = dpbb
        dg_ref[:, 0:D_MODEL] = (dmerged * pa_ref[...].astype(F32) * (sa * (1.0 - sa))).astype(BF16)
        dg_ref[:, D_MODEL:G_W] = (dmerged * pb_ref[...].astype(F32) * (sm * (1.0 - sm))).astype(BF16)
        dya_ref[...] = _dot_nt(dpab, wba_ref[...])
        dym_ref[...] = _dot_nt(dpbb, wbm_ref[...])

    row = lambda w: pl.BlockSpec((tm, w), lambda i: (i, 0))
    sd = lambda w, dt: jax.ShapeDtypeStruct((S, w), dt)
    return pl.pallas_call(
        body, name="mix_bwd", grid=(S // tm,),
        in_specs=[row(D_MODEL), row(D_MODEL), row(D_MODEL), row(D_MODEL), row(G_W), _full(vecs.shape),
                  _full(w_ba.shape), _full(w_bm.shape), _full(w_out.shape)],
        out_specs=[row(D_MODEL), row(D_MODEL), row(D_MODEL), row(G_W), row(512), row(512), _full((8, D_MODEL))],
        out_shape=[sd(D_MODEL, BF16), sd(D_MODEL, BF16), sd(D_MODEL, BF16), sd(G_W, BF16), sd(512, F32), sd(512, F32),
                   jax.ShapeDtypeStruct((8, D_MODEL), F32)],
        compiler_params=_params(("arbitrary",)),
    )(dx1, mix, pa, pb, pg, vecs, w_ba, w_bm, w_out)


def _pre_bwd(dproj, x, dx1, vecs, w_cat):
    S = x.shape[0]
    tm = WIDE_TILE

    def body(dp_ref, x_ref, dx1_ref, v_ref, w_ref, dx_ref, acc_ref):
        i = pl.program_id(0)

        @pl.when(i == 0)
        def _():
            acc_ref[...] = jnp.zeros_like(acc_ref)

        vrow = lambda r: v_ref[r:r + 1, :]
        dh = _dot_nt(dp_ref[...], w_ref[...])
        xv = x_ref[...]
        r1 = lax.rsqrt(_rowmean(xv * xv) + NORM_EPS)
        xhat = xv * r1
        acc_ref[0:1, :] += _colsum(dh * (xhat * vrow(R_G_PRE_MIX)))
        acc_ref[1:2, :] += _colsum(dh)
        dxn = dh * (1.0 + vrow(R_SCALE_M))
        acc_ref[2:3, :] += _colsum(dxn * xhat)
        dxhat = dxn * vrow(R_G_PRE_MIX)
        dx_ref[...] = dx1_ref[...] + r1 * (dxhat - xhat * _rowmean(dxhat * xhat))

    row = lambda w: pl.BlockSpec((tm, w), lambda i: (i, 0))
    return pl.pallas_call(
        body, name="pre_bwd", grid=(S // tm,),
        in_specs=[row(CAT_W), row(D_MODEL), row(D_MODEL), _full(vecs.shape), _full(w_cat.shape)],
        out_specs=[row(D_MODEL), _full((8, D_MODEL))],
        out_shape=[jax.ShapeDtypeStruct((S, D_MODEL), F32), jax.ShapeDtypeStruct((8, D_MODEL), F32)],
        compiler_params=_params(("arbitrary",)),
    )(dproj, x, dx1, vecs, w_cat)


def _matmul_tn(a, b, tn, name, ts=1024):
    S, K = a.shape
    N = b.shape[1]
    n_s = S // ts

    def body(a_ref, b_ref, o_ref, acc_ref):
        s = pl.program_id(1)

        @pl.when(s == 0)
        def _():
            acc_ref[...] = jnp.zeros_like(acc_ref)

        acc_ref[...] += _dot_tn(a_ref[...], b_ref[...])

        @pl.when(s == n_s - 1)
        def _():
            o_ref[...] = acc_ref[...].astype(BF16)

    return pl.pallas_call(
        body, name=name, grid=(N // tn, n_s),
        in_specs=[pl.BlockSpec((ts, K), lambda j, s: (s, 0)), pl.BlockSpec((ts, tn), lambda j, s: (s, j))],
        out_specs=pl.BlockSpec((K, tn), lambda j, s: (0, j)),
        out_shape=jax.ShapeDtypeStruct((K, N), BF16),
        scratch_shapes=[pltpu.VMEM((K, tn), F32)],
        compiler_params=_params(("parallel", "arbitrary")),
    )(a, b)


def _rope_swap(t):
    lane = lax.broadcasted_iota(jnp.int32, t.shape, 1)
    first = (lane & (HEAD_DIM - 1)) < (HEAD_DIM // 2)
    return jnp.where(first, pltpu.roll(t, LANES - HEAD_DIM // 2, 1), pltpu.roll(t, HEAD_DIM // 2, 1))


def _rope(t, cos, sin_signed):
    return t * cos + _rope_swap(t) * sin_signed


def _rope_t(d, cos, sin_signed):
    return d * cos + _rope_swap(d * sin_signed)


def _to_kv_lanes(chunk, p, h):
    lane = lax.broadcasted_iota(jnp.int32, chunk.shape, 1)
    src = chunk if p == h else pltpu.roll(chunk, HEAD_DIM, 1)
    return jnp.where((lane >> 6) == h, src, jnp.zeros_like(src))


def _from_kv_lanes(o_a, o_b, h):
    lane = lax.broadcasted_iota(jnp.int32, o_a.shape, 1)
    a = o_a if h == 0 else pltpu.roll(o_a, HEAD_DIM, 1)
    b = o_b if h == 1 else pltpu.roll(o_b, HEAD_DIM, 1)
    return jnp.where(lane < HEAD_DIM, a, b)


def _band_bias(n):
    blk = ATTN_BLOCK
    qi = lax.broadcasted_iota(jnp.int32, (blk, 2 * blk), 0)
    kj = lax.broadcasted_iota(jnp.int32, (blk, 2 * blk), 1)
    seen = (kj > qi) & (kj <= qi + blk) & ((n > 0) | (kj >= blk))
    return jnp.concatenate([jnp.where(seen, 0.0, NEG)] * N_Q_HEADS, axis=0)


def _stack_heads(chunks, h, dtype):
    parts = []
    for g in range(4):
        j = 4 * h + g
        parts.append(_to_kv_lanes(chunks[j // 2], j % 2, h))
    return jnp.concatenate(parts, axis=0).astype(dtype)


def _fused_call(parts, name, n_steps):
    counts = [(len(p["in_specs"]), len(p["out_specs"]), len(p["scratch"])) for p in parts]
    n_in, n_out = sum(c[0] for c in counts), sum(c[1] for c in counts)

    def kernel_fn(*refs):
        i = pl.program_id(0)
        groups, a, b, c = [], 0, n_in, n_in + n_out
        for ci, co, cs in counts:
            groups.append(refs[a:a + ci] + refs[b:b + co] + refs[c:c + cs])
            a, b, c = a + ci, b + co, c + cs
        for p, g in zip(parts, groups):
            p["init"](i, *g)
        for p, g in zip(parts, groups):
            p["body"](i, *g)

    flat = lambda key: [v for p in parts for v in p[key]]
    res = pl.pallas_call(
        kernel_fn, name=name, grid=(n_steps,), in_specs=flat("in_specs"), out_specs=flat("out_specs"),
        out_shape=flat("out_shape"), scratch_shapes=flat("scratch"), compiler_params=_params(("arbitrary",)),
    )(*flat("operands"))
    out, pos = [], 0
    for _, co, _ in counts:
        out.append(res[pos:pos + co])
        pos += co
    return out


def _attn_fwd_part(pa, cos, sin, sinks):
    S = pa.shape[0]
    blk = ATTN_BLOCK
    nb = S // blk

    def body(n, sink_ref, cur_ref, prev_ref, cos_ref, sin_ref, cosp_ref, sinp_ref,
             ya_ref, qr_ref, kr_ref, vb_ref, lse_ref):
        cos_c, sin_c = cos_ref[...], sin_ref[...]
        qch = [_rope(cur_ref[:, c * LANES:(c + 1) * LANES], cos_c, sin_c) * (HEAD_DIM ** -0.5) for c in range(4)]
        for c in range(4):
            qr_ref[:, c * LANES:(c + 1) * LANES] = qch[c].astype(BF16)
        k_cur = _rope(cur_ref[:, 512:640], cos_c, sin_c).astype(BF16)
        k_prev = _rope(prev_ref[:, 0:LANES], cosp_ref[...], sinp_ref[...]).astype(BF16)
        v_cur = cur_ref[:, 640:768].astype(BF16)
        v_prev = prev_ref[:, LANES:2 * LANES].astype(BF16)
        kr_ref[...] = k_cur
        vb_ref[...] = v_cur
        K = jnp.concatenate([k_prev, k_cur], axis=0)
        V = jnp.concatenate([v_prev, v_cur], axis=0)
        lane = lax.broadcasted_iota(jnp.int32, (blk, LANES), 1)
        s = jnp.concatenate([_dot_nt(_stack_heads(qch, h, BF16), K) for h in range(2)], axis=0)
        s = s + _band_bias(n)
        rowmax = jnp.max(s, axis=1, keepdims=True)
        hd = lambda x, j: x[j * blk:(j + 1) * blk]
        m = jnp.concatenate([jnp.maximum(hd(rowmax, j), sink_ref[j]) for j in range(N_Q_HEADS)], axis=0)
        p = jnp.exp(s - m)
        den = jnp.sum(p, axis=1, keepdims=True) \
            + jnp.concatenate([jnp.exp(sink_ref[j] - hd(m, j)) for j in range(N_Q_HEADS)], axis=0)
        pb = (p * _recip(den)).astype(BF16)
        o = jnp.concatenate([_dot(pb[4 * h * blk:4 * (h + 1) * blk], V) for h in range(2)], axis=0)
        lse = m + jnp.log(den)
        outs = [o[j * blk:(j + 1) * blk, :] for j in range(N_Q_HEADS)]
        lse_tile = jnp.zeros((blk, LANES), F32)
        for j in range(N_Q_HEADS):
            lse_tile = jnp.where(lane == j, lse[j * blk:(j + 1) * blk, :], lse_tile)
        for c in range(4):
            ya_ref[:, c * LANES:(c + 1) * LANES] = _from_kv_lanes(outs[2 * c], outs[2 * c + 1], c // 2).astype(BF16)
        lse_ref[...] = lse_tile

    prev = lambda n: jnp.maximum(n - 1, 0)
    sd = lambda w, dt: jax.ShapeDtypeStruct((S, w), dt)
    return dict(
        init=lambda n, *refs: None, body=body, scratch=[], operands=[sinks, pa, pa, cos, sin, cos, sin],
        in_specs=[pl.BlockSpec(memory_space=pltpu.SMEM),
                  pl.BlockSpec((blk, A_W), lambda n: (n, 0)),
                  pl.BlockSpec((blk, 256), lambda n: (prev(n), 2)),
                  pl.BlockSpec((blk, LANES), lambda n: (n, 0)), pl.BlockSpec((blk, LANES), lambda n: (n, 0)),
                  pl.BlockSpec((blk, LANES), lambda n: (prev(n), 0)), pl.BlockSpec((blk, LANES), lambda n: (prev(n), 0))],
        out_specs=[pl.BlockSpec((blk, 512), lambda n: (n, 0)), pl.BlockSpec((blk, 512), lambda n: (n, 0)),
                   pl.BlockSpec((blk, LANES), lambda n: (n, 0)), pl.BlockSpec((blk, LANES), lambda n: (n, 0)),
                   pl.BlockSpec((blk, LANES), lambda n: (n, 0))],
        out_shape=[sd(512, BF16), sd(512, BF16), sd(LANES, BF16), sd(LANES, BF16), sd(LANES, F32)])


def _attn_bwd_part(dya, qr, kr, vb, lse, cos, sin, sinks):
    S = dya.shape[0]
    blk = ATTN_BLOCK
    nb = S // blk

    def init(n, sink_ref, dya_ref, qr_ref, kc_ref, kp_ref, vc_ref, vp_ref, lse_ref, cos_ref, sin_ref, cosp_ref, sinp_ref,
             dq_ref, dkv_ref, last_ref, dsink_ref, ck, cv):
        @pl.when(n == 0)
        def _():
            ck[...] = jnp.zeros_like(ck)
            cv[...] = jnp.zeros_like(cv)
            dsink_ref[...] = jnp.zeros_like(dsink_ref)

    def body(n, sink_ref, dya_ref, qr_ref, kc_ref, kp_ref, vc_ref, vp_ref, lse_ref, cos_ref, sin_ref, cosp_ref, sinp_ref,
             dq_ref, dkv_ref, last_ref, dsink_ref, ck, cv):
        K = jnp.concatenate([kp_ref[...], kc_ref[...]], axis=0)
        V = jnp.concatenate([vp_ref[...], vc_ref[...]], axis=0)
        qch = [qr_ref[:, c * LANES:(c + 1) * LANES] for c in range(4)]
        dch = [dya_ref[:, c * LANES:(c + 1) * LANES] for c in range(4)]
        lse_tile = lse_ref[...]
        lane8 = lax.broadcasted_iota(jnp.int32, (8, LANES), 1)
        grp = lambda x, h: x[4 * h * blk:4 * (h + 1) * blk]
        qs = jnp.concatenate([_stack_heads(qch, h, BF16) for h in range(2)], axis=0)
        dos = jnp.concatenate([_stack_heads(dch, h, BF16) for h in range(2)], axis=0)
        lse_col = jnp.concatenate([lse_tile[:, j:j + 1] for j in range(N_Q_HEADS)], axis=0)
        s = jnp.concatenate([_dot_nt(grp(qs, h), K) for h in range(2)], axis=0)
        p = jnp.exp(s + _band_bias(n) - lse_col)
        dp = jnp.concatenate([_dot_nt(grp(dos, h), V) for h in range(2)], axis=0)
        delta = jnp.sum(p * dp, axis=1, keepdims=True)
        dsb = (p * (dp - delta)).astype(BF16)
        pb = p.astype(BF16)
        dq = jnp.concatenate([_dot(grp(dsb, h), K) for h in range(2)], axis=0)
        dk_acc = _dot_tn(grp(dsb, 0), grp(qs, 0)) + _dot_tn(grp(dsb, 1), grp(qs, 1))
        dv_acc = _dot_tn(grp(pb, 0), grp(dos, 0)) + _dot_tn(grp(pb, 1), grp(dos, 1))
        dqs = [dq[j * blk:(j + 1) * blk, :] for j in range(N_Q_HEADS)]
        dsink = jnp.zeros((8, LANES), F32)
        for j in range(N_Q_HEADS):
            rows = slice(j * blk, (j + 1) * blk)
            ps_delta = jnp.exp(sink_ref[j] - lse_col[rows]) * delta[rows]
            dsink = jnp.where(lane8 == j, dsink - jnp.sum(ps_delta), dsink)
        dsink_ref[...] += dsink
        cos_c, sin_c = cos_ref[...], sin_ref[...]
        for c in range(4):
            dqc = _from_kv_lanes(dqs[2 * c], dqs[2 * c + 1], c // 2) * (HEAD_DIM ** -0.5)
            dq_ref[:, c * LANES:(c + 1) * LANES] = _rope_t(dqc, cos_c, sin_c).astype(BF16)
        dkv_ref[:, 0:LANES] = _rope_t(dk_acc[0:blk, :] + ck[...], cosp_ref[...], sinp_ref[...]).astype(BF16)
        dkv_ref[:, LANES:2 * LANES] = (dv_acc[0:blk, :] + cv[...]).astype(BF16)
        ck[...] = dk_acc[blk:2 * blk, :]
        cv[...] = dv_acc[blk:2 * blk, :]
        last_ref[:, 0:LANES] = _rope_t(dk_acc[blk:2 * blk, :], cos_c, sin_c).astype(BF16)
        last_ref[:, LANES:2 * LANES] = dv_acc[blk:2 * blk, :].astype(BF16)

    prev = lambda n: jnp.maximum(n - 1, 0)
    same = lambda n: n
    bs = lambda w, f: pl.BlockSpec((blk, w), lambda n: (f(n), 0))
    return dict(
        init=init, body=body, operands=[sinks, dya, qr, kr, kr, vb, vb, lse, cos, sin, cos, sin],
        in_specs=[pl.BlockSpec(memory_space=pltpu.SMEM),
                  bs(512, same), bs(512, same), bs(LANES, same), bs(LANES, prev), bs(LANES, same), bs(LANES, prev),
                  bs(LANES, same), bs(LANES, same), bs(LANES, same), bs(LANES, prev), bs(LANES, prev)],
        out_specs=[bs(512, same), bs(256, prev), _full((blk, 256)), _full((8, LANES))],
        out_shape=[jax.ShapeDtypeStruct((S, 512), BF16), jax.ShapeDtypeStruct((S, 256), BF16),
                   jax.ShapeDtypeStruct((blk, 256), BF16), jax.ShapeDtypeStruct((8, LANES), F32)],
        scratch=[pltpu.VMEM((blk, LANES), F32), pltpu.VMEM((blk, LANES), F32)])


def _split3(x):
    hi = x.astype(BF16)
    r1 = x - hi.astype(F32)
    mid = r1.astype(BF16)
    lo = (r1 - mid.astype(F32)).astype(BF16)
    return hi, mid, lo


def _tri_matmul(tri_b, x):
    hi, mid, lo = _split3(x)
    return _dot(tri_b, hi) + _dot(tri_b, mid) + _dot(tri_b, lo)


def _log_sigmoid(x):
    return jnp.minimum(x, 0.0) - jnp.log(1.0 + jnp.exp(-jnp.abs(x)))


def _shift_rows(cur, seam, k, down):
    L = cur.shape[0]
    row8 = lax.broadcasted_iota(jnp.int32, seam.shape, 0)
    if down:
        mixed = jnp.concatenate([cur[:L - 8], jnp.where(row8 >= 8 - k, seam, cur[L - 8:])], axis=0)
        return pltpu.roll(mixed, k, 0)
    mixed = jnp.concatenate([jnp.where(row8 < k, seam, cur[:8]), cur[8:]], axis=0)
    return pltpu.roll(mixed, L - k, 0)


def _conv_fwd(cur, tail, cw_ref):
    z = cw_ref[4:5, :]
    for k in range(3, 0, -1):
        z = z + _shift_rows(cur, tail, k, True) * cw_ref[3 - k:4 - k, :]
    return z + cur * cw_ref[3:4, :]


def _stack(f):
    return jnp.concatenate([f(h) for h in range(MLSTM_HEADS)], axis=0)


def _head(x, h):
    L = x.shape[0] // MLSTM_HEADS
    return x[h * L:(h + 1) * L]


def _mlstm_heads_fwd(qk, cur_ref, gt, b_all, c_prev, nmv, tri, eye):
    L = qk.shape[0]
    HD = MLSTM_HEAD_DIM
    W4 = MLSTM_HEADS * HD
    col2row = lambda x: jnp.sum(jnp.where(eye, x, 0.0), axis=0, keepdims=True)
    b_col = _stack(lambda h: b_all[:, 4 + h:5 + h])
    i_col = _stack(lambda h: gt[:, h:h + 1])
    b_row = _stack(lambda h: jnp.broadcast_to(col2row(b_all[:, 4 + h:5 + h]), (L, L)))
    i_row = _stack(lambda h: jnp.broadcast_to(col2row(gt[:, h:h + 1]), (L, L)))
    bl = _stack(lambda h: jnp.broadcast_to(b_all[L - 1:L, 4 + h:5 + h], (L, 1)))
    m_prev = _stack(lambda h: jnp.broadcast_to(nmv[4 + h:5 + h, 0:1], (L, 1)))
    n_prev = _stack(lambda h: jnp.broadcast_to(nmv[h:h + 1, :], (L, HD)))
    tri4 = jnp.concatenate([tri] * MLSTM_HEADS, axis=0)
    Dm = jnp.where(tri4, b_col - b_row + i_row, NEG)
    inter = b_col + m_prev
    m_t = jnp.maximum(inter, jnp.max(Dm, axis=1, keepdims=True))
    W = jnp.exp(Dm - m_t)
    e_t = jnp.exp(inter - m_t)
    q = _stack(lambda h: qk[:, h * HD:(h + 1) * HD])
    k = _stack(lambda h: qk[:, W4 + h * HD:W4 + (h + 1) * HD]) * (HD ** -0.5)
    v = _stack(lambda h: cur_ref[:, 2 * W4 + h * HD:2 * W4 + (h + 1) * HD])
    qb, kb, vb = q.astype(BF16), k.astype(BF16), v.astype(BF16)
    Sc = _stack(lambda h: _dot_nt(_head(qb, h), _head(kb, h))) * W
    Scb = Sc.astype(BF16)
    cb = [c.astype(BF16) for c in c_prev]
    P1 = _stack(lambda h: _dot(_head(qb, h), cb[h]))
    num = _stack(lambda h: _dot(_head(Scb, h), _head(vb, h))) + e_t * P1
    qn = jnp.sum(q * n_prev, axis=1, keepdims=True)
    den = jnp.sum(Sc, axis=1, keepdims=True) + e_t * qn
    floor = jnp.exp(-m_t)
    inv_g = _recip(jnp.maximum(jnp.abs(den), floor))
    hv = num * inv_g
    a_col = bl - b_col + i_col
    a_max = _stack(lambda h: jnp.broadcast_to(jnp.max(_head(a_col, h), axis=0, keepdims=True), (L, 1)))
    m_new = jnp.maximum(bl + m_prev, a_max)
    dec = jnp.exp(bl + m_prev - m_new)
    u_col = jnp.exp(a_col - m_new)
    return dict(W=W, e_t=e_t, q=q, k=k, v=v, qb=qb, kb=kb, vb=vb, cb=cb, Sc=Sc, Scb=Scb, P1=P1, qn=qn, den=den,
                floor=floor, inv_g=inv_g, hv=hv, n_prev=n_prev, m_new=m_new, dec=dec, u_col=u_col)


def _mlstm_fwd_part(pm, pif, cw, sv):
    S = pm.shape[0]
    L = MLSTM_CHUNK
    nc = S // L
    HD = MLSTM_HEAD_DIM
    W4 = MLSTM_HEADS * HD

    def init(c, cur_ref, pif_ref, cw_ref, sv_ref, ym_ref, z_ref, cst_ref, nst_ref, C, nm, tail):
        @pl.when(c == 0)
        def _():
            C[...] = jnp.zeros_like(C)
            nm[...] = jnp.zeros_like(nm)
            tail[...] = jnp.zeros_like(tail)

    def body(c, cur_ref, pif_ref, cw_ref, sv_ref, ym_ref, z_ref, cst_ref, nst_ref, C, nm, tail):
        z = _conv_fwd(cur_ref[:, 0:2 * W4], tail[...], cw_ref)
        tail[...] = cur_ref[L - 8:L, 0:2 * W4]
        z_ref[...] = z
        qk = z * _sigmoid(z)
        gt = pif_ref[...] + sv_ref[1:2, 0:LANES]
        r_i = lax.broadcasted_iota(jnp.int32, (L, L), 0)
        c_i = lax.broadcasted_iota(jnp.int32, (L, L), 1)
        tri = c_i <= r_i
        eye = c_i == r_i
        b_all = _tri_matmul(tri.astype(BF16), _log_sigmoid(gt))
        nmv = nm[...]
        nst_ref[0] = nmv
        c_prev = [C[h] for h in range(MLSTM_HEADS)]
        f = _mlstm_heads_fwd(qk, cur_ref, gt, b_all, c_prev, nmv, tri, eye)
        hv = f["hv"]
        xc = hv - _rowmean(hv)
        hhat = xc * lax.rsqrt(_rowmean(xc * xc) + NORM_EPS)
        so = _sigmoid(_stack(lambda h: cur_ref[:, 3 * W4 + h * HD:3 * W4 + (h + 1) * HD]))
        wn = _stack(lambda h: jnp.broadcast_to(sv_ref[0:1, h * HD:(h + 1) * HD], (L, HD)))
        y = (so * hhat * wn).astype(BF16)
        kw = f["k"] * f["u_col"]
        kwb = kw.astype(BF16)
        n_new, m_new = [], []
        for h in range(MLSTM_HEADS):
            cst_ref[0, h] = c_prev[h]
            ym_ref[:, h * HD:(h + 1) * HD] = _head(y, h)
            dec = f["dec"][h * L:h * L + 1, :]
            C[h] = dec * c_prev[h] + _dot_tn(_head(kwb, h), _head(f["vb"], h))
            n_new.append(dec * nmv[h:h + 1, :] + _colsum(_head(kw, h)))
            m_new.append(jnp.broadcast_to(f["m_new"][h * L:h * L + 1, :], (1, LANES)))
        nm[...] = jnp.concatenate(n_new + m_new, axis=0)

    return dict(
        init=init, body=body, operands=[pm, pif, cw, sv],
        in_specs=[pl.BlockSpec((L, M_W), lambda c: (c, 0)),
                  pl.BlockSpec((L, IF_W), lambda c: (c, 0)), _full(cw.shape), _full(sv.shape)],
        out_specs=[pl.BlockSpec((L, W4), lambda c: (c, 0)), pl.BlockSpec((L, 2 * W4), lambda c: (c, 0)),
                   pl.BlockSpec((1, MLSTM_HEADS, HD, HD), lambda c: (c, 0, 0, 0)),
                   pl.BlockSpec((1, 8, LANES), lambda c: (c, 0, 0))],
        out_shape=[jax.ShapeDtypeStruct((S, W4), BF16), jax.ShapeDtypeStruct((S, 2 * W4), F32),
                   jax.ShapeDtypeStruct((nc, MLSTM_HEADS, HD, HD), F32), jax.ShapeDtypeStruct((nc, 8, LANES), F32)],
        scratch=[pltpu.VMEM((MLSTM_HEADS, HD, HD), F32), pltpu.VMEM((8, LANES), F32), pltpu.VMEM((8, 2 * W4), F32)])


def _mlstm_bwd_part(pm, zc, pif, cw, sv, dym, cst, nst):
    S = pm.shape[0]
    L = MLSTM_CHUNK
    nc = S // L
    HD = MLSTM_HEAD_DIM
    W4 = MLSTM_HEADS * HD

    def init(r, cur_ref, z_ref, pif_ref, cw_ref, sv_ref, dym_ref, cst_ref, nst_ref,
             dm_ref, dif_ref, dcw_ref, dsv_ref, dC, dn, dz_next, dqk):
        @pl.when(r == 0)
        def _():
            dC[...] = jnp.zeros_like(dC)
            dn[...] = jnp.zeros_like(dn)
            dz_next[...] = jnp.zeros_like(dz_next)
            dcw_ref[...] = jnp.zeros_like(dcw_ref)
            dsv_ref[...] = jnp.zeros_like(dsv_ref)

    def body(r, cur_ref, z_ref, pif_ref, cw_ref, sv_ref, dym_ref, cst_ref, nst_ref,
             dm_ref, dif_ref, dcw_ref, dsv_ref, dC, dn, dz_next, dqk):
        z = z_ref[...]
        sgz = _sigmoid(z)
        qk = z * sgz
        gt = pif_ref[...] + sv_ref[1:2, 0:LANES]
        r_i = lax.broadcasted_iota(jnp.int32, (L, L), 0)
        c_i = lax.broadcasted_iota(jnp.int32, (L, L), 1)
        tri = c_i <= r_i
        eye = c_i == r_i
        b_all = _tri_matmul(tri.astype(BF16), _log_sigmoid(gt))
        lane = lax.broadcasted_iota(jnp.int32, (L, LANES), 1)
        rowl = lax.broadcasted_iota(jnp.int32, (L, 1), 0)
        nmv = nst_ref[0]
        heads = range(MLSTM_HEADS)
        c_prev = [cst_ref[0, h] for h in heads]
        f = _mlstm_heads_fwd(qk, cur_ref, gt, b_all, c_prev, nmv, tri, eye)
        hv, inv_g, den, e_t, u_col, n_prev = f["hv"], f["inv_g"], f["den"], f["e_t"], f["u_col"], f["n_prev"]
        q, k, v, qb, kb, vb, Sc, Scb, W = f["q"], f["k"], f["v"], f["qb"], f["kb"], f["vb"], f["Sc"], f["Scb"], f["W"]
        xc = hv - _rowmean(hv)
        rstd = lax.rsqrt(_rowmean(xc * xc) + NORM_EPS)
        hhat = xc * rstd
        wn = _stack(lambda h: jnp.broadcast_to(sv_ref[0:1, h * HD:(h + 1) * HD], (L, HD)))
        so = _sigmoid(_stack(lambda h: cur_ref[:, 3 * W4 + h * HD:3 * W4 + (h + 1) * HD]))
        dy = _stack(lambda h: dym_ref[:, h * HD:(h + 1) * HD])
        d_o = (dy * hhat * wn * (so * (1.0 - so))).astype(BF16)
        dln = dy * so
        dwn = dln * hhat
        dhhat = dln * wn
        dh = rstd * (dhhat - _rowmean(dhhat) - hhat * _rowmean(dhhat * hhat))
        dnum = dh * inv_g
        active = jnp.abs(den) > f["floor"]
        dden = jnp.where(active, -jnp.sum(dh * hv, axis=1, keepdims=True) * inv_g * jnp.where(den >= 0.0, 1.0, -1.0), 0.0)
        dnumb = dnum.astype(BF16)
        dSc = _stack(lambda h: _dot_nt(_head(dnumb, h), _head(vb, h))) + dden
        dA = (dSc * W).astype(BF16)
        G = dSc * Sc
        Gr = jnp.sum(G, axis=1, keepdims=True)
        Gc = _stack(lambda h: jnp.sum(jnp.where(eye, _colsum(_head(G, h)), 0.0), axis=1, keepdims=True))
        dCn = [dC[h] for h in heads]
        dCnb = [d.astype(BF16) for d in dCn]
        dnv = dn[...]
        dn_new = _stack(lambda h: jnp.broadcast_to(dnv[h:h + 1, :], (L, HD)))
        kdC = _stack(lambda h: _dot(_head(kb, h), dCnb[h]))
        vdC = _stack(lambda h: _dot_nt(_head(vb, h), dCnb[h]))
        dv = (_stack(lambda h: _dot_tn(_head(Scb, h), _head(dnumb, h))) + u_col * kdC).astype(BF16)
        dq = _stack(lambda h: _dot(_head(dA, h), _head(kb, h))) \
            + e_t * _stack(lambda h: _dot_nt(_head(dnumb, h), f["cb"][h])) + (e_t * dden) * n_prev
        dk = (_stack(lambda h: _dot_tn(_head(dA, h), _head(qb, h))) + u_col * (vdC + dn_new)) * (HD ** -0.5)
        E = (jnp.sum(f["P1"] * dnum, axis=1, keepdims=True) + dden * f["qn"]) * e_t
        U = (jnp.sum(kdC * v, axis=1, keepdims=True) + jnp.sum(k * dn_new, axis=1, keepdims=True)) * u_col
        qe = (q * e_t).astype(BF16)
        qd = (e_t * dden) * q
        di = Gc + U
        db = Gr + E - Gc - U
        di_tile = jnp.zeros((L, LANES), F32)
        db_tile = jnp.zeros((L, LANES), F32)
        dn_rows = []
        for h in heads:
            dec = f["dec"][h * L:h * L + 1, :]
            ddec = jnp.sum(dCn[h] * c_prev[h]) + jnp.sum(dnv[h:h + 1, :] * nmv[h:h + 1, :])
            dbl = ddec * dec + jnp.sum(_head(U, h), axis=0, keepdims=True)
            di_tile = jnp.where(lane == h, _head(di, h), di_tile)
            db_tile = jnp.where(lane == 4 + h, _head(db, h) + jnp.where(rowl == L - 1, dbl, 0.0), db_tile)
            dC[h] = dec * dCn[h] + _dot_tn(_head(qe, h), _head(dnumb, h))
            dn_rows.append(dec * dnv[h:h + 1, :] + _colsum(_head(qd, h)))
            dsv_ref[0:1, h * HD:(h + 1) * HD] += _colsum(_head(dwn, h))
            dqk[:, h * HD:(h + 1) * HD] = _head(dq, h)
            dqk[:, W4 + h * HD:W4 + (h + 1) * HD] = _head(dk, h)
            dm_ref[:, 2 * W4 + h * HD:2 * W4 + (h + 1) * HD] = _head(dv, h)
            dm_ref[:, 3 * W4 + h * HD:3 * W4 + (h + 1) * HD] = _head(d_o, h)
        dn[...] = jnp.concatenate(dn_rows + [jnp.zeros((8 - MLSTM_HEADS, LANES), F32)], axis=0)
        dlf = _tri_matmul((r_i <= c_i).astype(BF16), db_tile)
        dif = jnp.where(lane < 4, di_tile, jnp.where(lane < 8, dlf * (1.0 - _sigmoid(gt)), 0.0))
        dif_ref[...] = dif.astype(BF16)
        dsv_ref[1:2, 0:LANES] += _colsum(dif)
        dz = dqk[...] * (sgz * (1.0 + z * (1.0 - sgz)))
        dcw_ref[4:5, :] += _colsum(dz)
        u = cur_ref[:, 0:2 * W4]
        du_in = dz * cw_ref[3:4, :]
        dcw_ref[3:4, :] += _colsum(dz * u)
        for k in range(1, 4):
            up = _shift_rows(dz, dz_next[...], k, False)
            dcw_ref[3 - k:4 - k, :] += _colsum(up * u)
            du_in = du_in + up * cw_ref[3 - k:4 - k, :]
        dz_next[...] = dz[0:8, :]
        dm_ref[:, 0:2 * W4] = du_in.astype(BF16)

    cidx = lambda r: nc - 1 - r
    return dict(
        init=init, body=body, operands=[pm, zc, pif, cw, sv, dym, cst, nst],
        in_specs=[pl.BlockSpec((L, M_W), lambda r: (cidx(r), 0)), pl.BlockSpec((L, 2 * W4), lambda r: (cidx(r), 0)),
                  pl.BlockSpec((L, IF_W), lambda r: (cidx(r), 0)), _full(cw.shape), _full(sv.shape),
                  pl.BlockSpec((L, W4), lambda r: (cidx(r), 0)),
                  pl.BlockSpec((1, MLSTM_HEADS, HD, HD), lambda r: (cidx(r), 0, 0, 0)),
                  pl.BlockSpec((1, 8, LANES), lambda r: (cidx(r), 0, 0))],
        out_specs=[pl.BlockSpec((L, M_W), lambda r: (cidx(r), 0)), pl.BlockSpec((L, IF_W), lambda r: (cidx(r), 0)),
                   _full((8, 2 * W4)), _full((8, W4))],
        out_shape=[jax.ShapeDtypeStruct((S, M_W), BF16), jax.ShapeDtypeStruct((S, IF_W), BF16),
                   jax.ShapeDtypeStruct((8, 2 * W4), F32), jax.ShapeDtypeStruct((8, W4), F32)],
        scratch=[pltpu.VMEM((MLSTM_HEADS, HD, HD), F32), pltpu.VMEM((8, LANES), F32),
                 pltpu.VMEM((8, 2 * W4), F32), pltpu.VMEM((L, 2 * W4), F32)])


def _rope_tables(positions):
    half = HEAD_DIM // 2
    inv_freq = ROPE_THETA ** (-2.0 * jnp.arange(half, dtype=F32) / HEAD_DIM)
    ang = positions.astype(F32)[:, None] * inv_freq
    cos = jnp.tile(jnp.cos(ang), (1, LANES // half))
    sign = jnp.tile(jnp.concatenate([-jnp.ones((half,), F32), jnp.ones((half,), F32)]), LANES // HEAD_DIM)
    sin = jnp.tile(jnp.sin(ang), (1, LANES // half)) * sign
    return cos, sin


def _local_step(x, tgt, positions, mod, gains, w_cat, w_ba, w_bm, w_out, w_gate, w_up, w_down,
                conv_w, conv_b, b_if, sinks, norm_w):
    t = _tables(mod, gains, conv_w, conv_b, b_if, norm_w, positions)
    a = _mixer_fwd(x, t, sinks, w_cat)
    b = _ffn_part(x, tgt, t, a, w_ba, w_bm, w_out, w_gate, w_up, w_down)
    c = _mixer_bwd(b["dx1"], t, a, b, sinks, w_ba, w_bm, w_out)
    grad_x, acc_p = _pre_bwd(c["dproj"], x, b["dx1"], t["vecs"], w_cat)
    big = dict(w_cat=c["g_w_cat"], w_ba=c["g_w_ba"], w_bm=c["g_w_bm"], w_out=c["g_w_out"], w_gate=b["g_w_gate"],
               w_up=b["g_w_up"], w_down=b["g_w_down"])
    return b["loss"], grad_x, big, _small_grads(acc_p, b, c)


def _tables(mod, gains, conv_w, conv_b, b_if, norm_w, positions):
    cos, sin = _rope_tables(positions)
    return dict(
        vecs=jnp.concatenate([mod, gains, jnp.zeros((6, D_MODEL), F32)], axis=0),
        cw=jnp.concatenate([conv_w, conv_b.reshape(1, -1), jnp.zeros((3, 2 * 512), F32)], axis=0),
        sv=jnp.zeros((8, 512), F32).at[0].set(norm_w).at[1, 0:8].set(b_if), cos=cos, sin=sin)


def _mixer_fwd(x, t, sinks, w_cat):
    h, pa, pm, pif, pg = _pre_proj(x, t["vecs"], w_cat)
    n_blk = x.shape[0] // ATTN_BLOCK
    (ya, qr, kr, vb, lse), = _fused_call([_attn_fwd_part(pa, t["cos"], t["sin"], sinks)], "attn_fwd", n_blk)
    (ym, zc, cst, nst), = _fused_call([_mlstm_fwd_part(pm, pif, t["cw"], t["sv"])], "mlstm_fwd", n_blk)
    return dict(h=h, pm=pm, pif=pif, pg=pg, ya=ya, qr=qr, kr=kr, vb=vb, lse=lse, ym=ym, zc=zc, cst=cst, nst=nst)


def _ffn_part(x, tgt, t, a, w_ba, w_bm, w_out, w_gate, w_up, w_down):
    x1, merged, mix, pba, pbm = _mix_fwd(x, a["ya"], a["ym"], a["pg"], t["vecs"], w_ba, w_bm, w_out)
    dx1, h2, hid, da, du, dff, acc_f, loss = _ffn_fwd_bwd(x1, tgt, t["vecs"], w_gate, w_up, w_down)
    return dict(merged=merged, mix=mix, pba=pba, pbm=pbm, dx1=dx1, acc_f=acc_f, loss=loss[0, 0],
                g_w_gate=_matmul_tn(da, h2, 1024, "dw_ffn_gate"),
                g_w_up=_matmul_tn(du, h2, 1024, "dw_ffn_up"),
                g_w_down=_matmul_tn(hid, dff, 1024, "dw_ffn_down"))


def _mixer_bwd(dx1, t, a, b, sinks, w_ba, w_bm, w_out):
    dmix, dpa, dpb, dg, dya, dym, acc_m = _mix_bwd(dx1, b["mix"], b["pba"], b["pbm"], a["pg"], t["vecs"], w_ba, w_bm, w_out)
    g_w_out = _matmul_tn(b["merged"], dmix, 1024, "dw_out")
    g_w_ba = _matmul_tn(a["ya"], dpa, 1024, "dw_branch_attn")
    g_w_bm = _matmul_tn(a["ym"], dpb, 1024, "dw_branch_mlstm")
    n_blk = dx1.shape[0] // ATTN_BLOCK
    (dq, dkv, dkv_last, dsink), = _fused_call(
        [_attn_bwd_part(dya, a["qr"], a["kr"], a["vb"], a["lse"], t["cos"], t["sin"], sinks)], "attn_bwd", n_blk)
    (dm, dif, dcw, dsv), = _fused_call(
        [_mlstm_bwd_part(a["pm"], a["zc"], a["pif"], t["cw"], t["sv"], dym, a["cst"], a["nst"])], "mlstm_bwd", n_blk)
    dkv = lax.dynamic_update_slice(dkv, dkv_last, (dkv.shape[0] - ATTN_BLOCK, 0))
    dproj = jnp.concatenate([dq, dkv, dm, dif, dg], axis=1)
    return dict(dproj=dproj, g_w_cat=_matmul_tn(a["h"], dproj, 1664, "dw_in"), g_w_out=g_w_out, g_w_ba=g_w_ba,
                g_w_bm=g_w_bm, acc_m=acc_m, dsink=dsink, dcw=dcw, dsv=dsv)


def _small_grads(acc_p, b, c):
    acc_f, acc_m = b["acc_f"], c["acc_m"]
    dmod = jnp.stack([acc_p[1], acc_p[0], acc_m[0], acc_f[3], acc_f[2], acc_f[0]])
    dgains = jnp.stack([acc_p[2], acc_m[1], acc_f[4], acc_f[1]])
    return dict(dmod=dmod, dgains=dgains, dconv_w=c["dcw"][0:4], dconv_b=c["dcw"][4], db_if=c["dsv"][1, 0:8],
                dsinks=c["dsink"][0, 0:8], dnorm_w=c["dsv"][0])


MESH_ID = pl.DeviceIdType.MESH


def _mesh_pos():
    return lax.axis_index("x"), lax.axis_index("y"), lax.axis_index("c")


def _flip(v, bit):
    return 1 - v if bit else v


def _relations():
    return [((r >> 2) & 1, (r >> 1) & 1, r & 1) for r in range(1, N_DEV)]


def _small_exchange(p, gather, name):
    R, V = p.shape[-2:]

    def body(p_ref, out_ref, send_sems, recv_sems):
        x, y, c = _mesh_pos()
        me = 4 * x + 2 * y + c
        out_ref[me] = p_ref[...] if gather else p_ref[me]
        peers = []
        for dx, dy, dc in _relations():
            px, py, pc = _flip(x, dx), _flip(y, dy), _flip(c, dc)
            peers.append(((px, py, pc), 4 * px + 2 * py + pc))

        def copy(k, landing):
            peer, pid = peers[k]
            return pltpu.make_async_remote_copy(
                src_ref=p_ref if gather else p_ref.at[pid], dst_ref=out_ref.at[landing],
                send_sem=send_sems.at[k], recv_sem=recv_sems.at[k], device_id=peer, device_id_type=MESH_ID)

        sends = [copy(k, me) for k in range(N_DEV - 1)]
        for cp in sends:
            cp.start()
        for k in range(N_DEV - 1):
            copy(k, peers[k][1]).wait_recv()
        for cp in sends:
            cp.wait_send()

    vm = pl.BlockSpec(memory_space=pltpu.VMEM)
    return pl.pallas_call(
        body, name=name, in_specs=[vm], out_specs=vm,
        out_shape=jax.ShapeDtypeStruct((N_DEV, R, V), F32),
        scratch_shapes=[pltpu.SemaphoreType.DMA((N_DEV - 1,)), pltpu.SemaphoreType.DMA((N_DEV - 1,))],
        compiler_params=pltpu.CompilerParams(vmem_limit_bytes=VMEM_LIMIT),
    )(p)


HBM_SPEC = pl.BlockSpec(memory_space=pltpu.HBM)
SEM_SPEC = pl.BlockSpec(memory_space=pltpu.SEMAPHORE)


def _peers(x, y, c):
    out = []
    for dx, dy, dc in _relations():
        px, py, pc = _flip(x, dx), _flip(y, dy), _flip(c, dc)
        out.append(((px, py, pc), 4 * px + 2 * py + pc))
    return out


def _exchange_start(arrs, gather, after, name):
    n = len(arrs)
    me_out = 4 * lax.axis_index("x") + 2 * lax.axis_index("y") + lax.axis_index("c")
    lands = []
    for a in arrs:
        own = a[None] if gather else lax.dynamic_index_in_dim(a, me_out, 0, keepdims=True)
        empty = lax.empty(((N_DEV,) + a.shape) if gather else a.shape, a.dtype)
        lands.append(lax.dynamic_update_index_in_dim(empty, own, me_out, 0))

    def body(*refs):
        a_refs, l_refs = refs[:n], refs[n:2 * n]
        send_sems, recv_sems = refs[2 * n + 1], refs[2 * n + 2]
        token = refs[4 * n + 3]
        x, y, c = _mesh_pos()
        me = 4 * x + 2 * y + c
        for a in range(n):
            for k, (peer, pid) in enumerate(_peers(x, y, c)):
                pltpu.make_async_remote_copy(
                    src_ref=a_refs[a] if gather else a_refs[a].at[pid], dst_ref=l_refs[a].at[me],
                    send_sem=send_sems.at[a * (N_DEV - 1) + k], recv_sem=recv_sems.at[a * (N_DEV - 1) + k],
                    device_id=peer, device_id_type=MESH_ID).start()
        token[...] = jnp.zeros_like(token)

    sem = pltpu.SemaphoreType.DMA((n * (N_DEV - 1),))
    hbm = lambda a: pltpu.with_memory_space_constraint(a, pltpu.HBM)
    res = pl.pallas_call(
        body, name=name,
        out_shape=(sem, sem, *[pltpu.HBM(a.shape, a.dtype) for a in arrs], *[pltpu.HBM(l.shape, l.dtype) for l in lands],
                   jax.ShapeDtypeStruct((8, LANES), F32)),
        in_specs=[HBM_SPEC] * (2 * n) + [pl.BlockSpec(memory_space=pl.ANY)],
        out_specs=(SEM_SPEC, SEM_SPEC, *[HBM_SPEC] * (2 * n), pl.BlockSpec(memory_space=pltpu.VMEM)),
        input_output_aliases={i: 2 + i for i in range(2 * n)},
        compiler_params=pltpu.CompilerParams(has_side_effects=pltpu.SideEffectType.DATAFLOW_SIDE_EFFECTING),
    )(*[hbm(a) for a in arrs], *[hbm(l) for l in lands], after)
    return dict(sems=res[0:2], arrs=res[2:2 + n], lands=res[2 + n:2 + 2 * n], token=res[2 + 2 * n], gather=gather)


def _exchange_wait(st, after, name):
    n = len(st["arrs"])
    gather = st["gather"]

    def body(*refs):
        a_refs, l_refs = refs[:n], refs[n:2 * n]
        send_sems, recv_sems = refs[2 * n], refs[2 * n + 1]
        x, y, c = _mesh_pos()
        for a in range(n):
            for k, (peer, pid) in enumerate(_peers(x, y, c)):
                cp = pltpu.make_async_remote_copy(
                    src_ref=a_refs[a] if gather else a_refs[a].at[pid], dst_ref=l_refs[a].at[pid],
                    send_sem=send_sems.at[a * (N_DEV - 1) + k], recv_sem=recv_sems.at[a * (N_DEV - 1) + k],
                    device_id=peer, device_id_type=MESH_ID)
                cp.wait_send()
                cp.wait_recv()

    both = list(st["arrs"]) + list(st["lands"])
    res = pl.pallas_call(
        body, name=name, out_shape=[pltpu.HBM(a.shape, a.dtype) for a in both],
        in_specs=[HBM_SPEC] * (2 * n) + [SEM_SPEC, SEM_SPEC, pl.BlockSpec(memory_space=pl.ANY)],
        out_specs=[HBM_SPEC] * (2 * n), input_output_aliases={i: i for i in range(2 * n)},
        compiler_params=pltpu.CompilerParams(has_side_effects=pltpu.SideEffectType.DATAFLOW_SIDE_EFFECTING),
    )(*both, *st["sems"], after)
    return res[n:2 * n]


def _tie(x, token, name):
    def body(x_ref, t_ref, o_ref):
        o_ref[...] = x_ref[...]

    vm = pl.BlockSpec(memory_space=pltpu.VMEM)
    return pl.pallas_call(
        body, name=name, in_specs=[vm, pl.BlockSpec(memory_space=pl.ANY)], out_specs=vm,
        out_shape=jax.ShapeDtypeStruct(x.shape, x.dtype),
    )(x, token)


def _all_gather_hbm(shards):
    n = len(shards)

    def body(*refs):
        p_refs, out_refs = refs[:n], refs[n:2 * n]
        send_sems, recv_sems, local_sems = refs[2 * n:]
        x, y, c = _mesh_pos()
        me, sibling = (x, y, c), (x, y, 1 - c)
        chips = [(1 - x, y), (x, 1 - y), (1 - x, 1 - y)]

        def copy(a, k, block, to, own=False):
            slot = out_refs[a].at[4 * block[0] + 2 * block[1] + block[2]]
            return pltpu.make_async_remote_copy(
                src_ref=p_refs[a] if own else slot, dst_ref=slot,
                send_sem=send_sems.at[a, k], recv_sem=recv_sems.at[a, k], device_id=to, device_id_type=MESH_ID)

        mine = [pltpu.make_async_copy(p_refs[a], out_refs[a].at[4 * x + 2 * y + c], local_sems.at[a]) for a in range(n)]
        for cp in mine:
            cp.start()
        first = []
        for a in range(n):
            first.append(copy(a, 0, me, sibling, own=True))
            first += [copy(a, 1 + j, me, (*chip, c), own=True) for j, chip in enumerate(chips)]
        for cp in first:
            cp.start()
        passed = []
        for j, chip in enumerate(chips):
            for a in range(n):
                copy(a, 1 + j, (*chip, c), me).wait_recv()
                passed.append(copy(a, 4 + j, (*chip, c), sibling))
                passed[-1].start()
        for a in range(n):
            copy(a, 0, sibling, me).wait_recv()
            for j, chip in enumerate(chips):
                copy(a, 4 + j, (*chip, 1 - c), me).wait_recv()
        for cp in first + passed:
            cp.wait_send()
        for cp in mine:
            cp.wait()

    hbm = pl.BlockSpec(memory_space=pl.ANY)
    return pl.pallas_call(
        body, name="gather_weights", in_specs=[hbm] * n, out_specs=[hbm] * n,
        out_shape=[jax.ShapeDtypeStruct((N_DEV,) + s.shape, s.dtype) for s in shards],
        scratch_shapes=[pltpu.SemaphoreType.DMA((n, N_DEV - 1)), pltpu.SemaphoreType.DMA((n, N_DEV - 1)),
                        pltpu.SemaphoreType.DMA((n,))],
    )(*shards)


def _adamw(w, g, m, v):
    m2 = ADAM_B1 * m + (1.0 - ADAM_B1) * g
    v2 = ADAM_B2 * v + (1.0 - ADAM_B2) * (g * g)
    m_hat = m2 / (1.0 - ADAM_B1 ** ADAM_STEP)
    v_hat = v2 / (1.0 - ADAM_B2 ** ADAM_STEP)
    delta = -ADAM_LR * (m_hat / (jnp.sqrt(v_hat) + ADAM_EPS) + ADAM_WD * w)
    return delta, m2, v2


def _mod_partial(cmat, w_shard, b_shard):
    def body(c_ref, w_ref, b_ref, o_ref):
        o_ref[...] = _dot(c_ref[...].astype(BF16), w_ref[...].astype(BF16)) + b_ref[...]

    return pl.pallas_call(
        body, name="mod_partial", out_shape=jax.ShapeDtypeStruct((N_DEV, w_shard.shape[1]), F32),
        compiler_params=_params(),
    )(cmat, w_shard, b_shard)


def _adamw_w_ada(cmat, dmod_cols, w, m, v):
    def body(c_ref, d_ref, w_ref, m_ref, v_ref, g_ref, dl_ref, m2_ref, v2_ref):
        g = _dot_tn(c_ref[...].astype(BF16), d_ref[...].astype(BF16))
        g_ref[...] = g
        dl_ref[...], m2_ref[...], v2_ref[...] = _adamw(w_ref[...], g, m_ref[...], v_ref[...])

    return pl.pallas_call(
        body, name="adamw_w_ada", out_shape=[jax.ShapeDtypeStruct(w.shape, F32)] * 4,
        compiler_params=_params(),
    )(cmat, dmod_cols, w, m, v)


SMALL_ROWS = 16
SMALL_AT = {"b_ada": (0, 6, 0, D_MODEL), "g_pre_mix": (6, 1, 0, D_MODEL), "g_post_mix": (7, 1, 0, D_MODEL),
            "g_pre_ffn": (8, 1, 0, D_MODEL), "g_post_ffn": (9, 1, 0, D_MODEL), "conv_b": (10, 1, 0, D_MODEL),
            "mlstm_norm_w": (11, 1, 0, 512), "b_if": (11, 1, 512, LANES), "attn_sinks": (11, 1, 640, LANES)}


def _small_table(part):
    tail = jnp.concatenate([part["mlstm_norm_w"], jnp.pad(part["b_if"], (0, LANES - 8)),
                            jnp.pad(part["attn_sinks"], (0, LANES - 8)), jnp.zeros((256,), F32)])
    return jnp.concatenate([part["b_ada"], part["gains"], part["conv_b"][None], tail[None],
                            jnp.zeros((SMALL_ROWS - 12, D_MODEL), F32)], axis=0)


def _adamw_small(gathered, wmv):
    names = list(SMALL_AT)

    def body(*refs):
        g_ref, ins, outs = refs[0], refs[1:1 + 3 * len(names)], refs[1 + 3 * len(names):]
        g = g_ref[0]
        for k in range(1, N_DEV):
            g = g + g_ref[k]
        for i, n in enumerate(names):
            r0, rows, l0, lanes = SMALL_AT[n]
            gi = jnp.concatenate([g[r:r + 1, l0:l0 + lanes] for r in range(r0, r0 + rows)], axis=1)
            w_ref, m_ref, v_ref = ins[3 * i:3 * i + 3]
            go, dl, m2, v2 = outs[4 * i:4 * i + 4]
            go[...] = gi
            dl[...], m2[...], v2[...] = _adamw(w_ref[...], gi, m_ref[...], v_ref[...])

    flat = [a for n in names for a in wmv[n]]
    res = pl.pallas_call(
        body, name="adamw_small",
        out_shape=[jax.ShapeDtypeStruct(wmv[n][0].shape, F32) for n in names for _ in range(4)],
        compiler_params=_params(),
    )(gathered, *flat)
    return {n: res[4 * i:4 * i + 4] for i, n in enumerate(names)}


def _row_tile(rows):
    return rows // 4 if rows >= 512 else rows


def _sum_partials(r_ref):
    g = r_ref[0].astype(F32)
    for k in range(1, N_DEV):
        g = g + r_ref[k].astype(F32)
    return g


def _adamw_sum(recv, w, m, v, name):
    r, cdim = w.shape
    tr = _row_tile(r)

    def body(r_ref, w_ref, m_ref, v_ref, g_ref, dl_ref, m2_ref, v2_ref):
        g = _sum_partials(r_ref)
        g_ref[...] = g
        dl_ref[...], m2_ref[...], v2_ref[...] = _adamw(w_ref[...], g, m_ref[...], v_ref[...])

    row = pl.BlockSpec((tr, cdim), lambda i: (i, 0))
    return pl.pallas_call(
        body, name=name, grid=(r // tr,),
        in_specs=[pl.BlockSpec((N_DEV, tr, cdim), lambda i: (0, i, 0)), row, row, row],
        out_specs=[row] * 4, out_shape=[jax.ShapeDtypeStruct((r, cdim), F32)] * 4,
        compiler_params=_params(("parallel",)),
    )(recv, w, m, v)


def _sum8(recv, name):
    _, r, cdim = recv.shape
    tr = _row_tile(r)

    def body(r_ref, g_ref):
        g_ref[...] = _sum_partials(r_ref)

    return pl.pallas_call(
        body, name=name, grid=(r // tr,),
        in_specs=[pl.BlockSpec((N_DEV, tr, cdim), lambda i: (0, i, 0))],
        out_specs=pl.BlockSpec((tr, cdim), lambda i: (i, 0)), out_shape=jax.ShapeDtypeStruct((r, cdim), F32),
        compiler_params=_params(("parallel",)),
    )(recv)


def _adamw_plain(g, w, m, v, name):
    r, cdim = w.shape
    tr = _row_tile(r)

    def body(g_ref, w_ref, m_ref, v_ref, dl_ref, m2_ref, v2_ref):
        dl_ref[...], m2_ref[...], v2_ref[...] = _adamw(w_ref[...], g_ref[...], m_ref[...], v_ref[...])

    row = pl.BlockSpec((tr, cdim), lambda i: (i, 0))
    return pl.pallas_call(
        body, name=name, grid=(r // tr,), in_specs=[row] * 4, out_specs=[row] * 3,
        out_shape=[jax.ShapeDtypeStruct((r, cdim), F32)] * 3,
        compiler_params=_params(("parallel",)),
    )(g, w, m, v)


IN_SHARD = 609
IN_SHARD_PAD = 640
IF_AT = A_W + M_W


def _regrouped(u):
    return u if u < IF_AT + 8 else u + (IF_W - 8)


def _selection(k, rows, row0, transpose):
    shape = (rows, IN_SHARD_PAD) if transpose else (IN_SHARD_PAD, rows)
    l = lax.broadcasted_iota(jnp.int32, shape, 1 if transpose else 0)
    r = lax.broadcasted_iota(jnp.int32, shape, 0 if transpose else 1) + row0
    u = l + IN_SHARD * k
    ru = u + jnp.where(u >= IF_AT + 8, IF_W - 8, 0)
    return ((ru == r) & (l < IN_SHARD)).astype(BF16)


def _regroup_w_in(g):
    def body(g_ref, o_ref):
        for cb in range(CAT_W // LANES):
            r0 = cb * LANES
            acc = jnp.zeros((D_MODEL, LANES), F32)
            for k in range(N_DEV):
                lo, hi = _regrouped(IN_SHARD * k), _regrouped(IN_SHARD * k + IN_SHARD - 1)
                if hi >= r0 and lo < r0 + LANES:
                    acc = acc + _dot(g_ref[k], _selection(k, LANES, r0, False))
            o_ref[:, r0:r0 + LANES] = acc.astype(BF16)

    return pl.pallas_call(
        body, name="regroup_w_in", out_shape=jax.ShapeDtypeStruct((D_MODEL, CAT_W), BF16),
        compiler_params=_params(),
    )(g)


def _ungroup_w_in(g_cat):
    def body(g_ref, o_ref):
        for k in range(N_DEV):
            lo, hi = _regrouped(IN_SHARD * k), _regrouped(IN_SHARD * k + IN_SHARD - 1)
            w0, w1 = lo // LANES * LANES, (hi // LANES + 1) * LANES
            o_ref[k] = _dot(g_ref[:, w0:w1], _selection(k, w1 - w0, w0, True)).astype(BF16)

    return pl.pallas_call(
        body, name="ungroup_w_in", out_shape=jax.ShapeDtypeStruct((N_DEV, D_MODEL, IN_SHARD_PAD), BF16),
        compiler_params=_params(),
    )(g_cat)


WEIGHT_NAMES = ("w_ada", "b_ada", "g_pre_mix", "g_post_mix", "w_in", "b_if", "conv_w", "conv_b", "attn_sinks",
                "mlstm_norm_w", "w_branch_attn", "w_branch_mlstm", "w_out", "g_pre_ffn", "g_post_ffn",
                "w_ffn_gate", "w_ffn_up", "w_ffn_down")


def kernel(x, c, positions, w_ada, b_ada, g_pre_mix, g_post_mix, w_in, b_if, conv_w, conv_b, attn_sinks, mlstm_norm_w, w_branch_attn, w_branch_mlstm, w_out, g_pre_ffn, g_post_ffn, w_ffn_gate, w_ffn_up, w_ffn_down, loss_target, m_w_ada, m_b_ada, m_g_pre_mix, m_g_post_mix, m_w_in, m_b_if, m_conv_w, m_conv_b, m_attn_sinks, m_mlstm_norm_w, m_w_branch_attn, m_w_branch_mlstm, m_w_out, m_g_pre_ffn, m_g_post_ffn, m_w_ffn_gate, m_w_ffn_up, m_w_ffn_down, v_w_ada, v_b_ada, v_g_pre_mix, v_g_post_mix, v_w_in, v_b_if, v_conv_w, v_conv_b, v_attn_sinks, v_mlstm_norm_w, v_w_branch_attn, v_w_branch_mlstm, v_w_out, v_g_pre_ffn, v_g_post_ffn, v_w_ffn_gate, v_w_ffn_up, v_w_ffn_down):
    given = dict(locals())
    W = {n: given[n][0] for n in WEIGHT_NAMES}
    M = {n: given["m_" + n][0] for n in WEIGHT_NAMES}
    V = {n: given["v_" + n][0] for n in WEIGHT_NAMES}
    me = 4 * lax.axis_index("x") + 2 * lax.axis_index("y") + lax.axis_index("c")

    ff_sh = D_FF // N_DEV
    g_in, g_conv, cg = _all_gather_hbm([jnp.pad(W["w_in"], ((0, 0), (0, IN_SHARD_PAD - IN_SHARD))).astype(BF16),
                                        jnp.pad(W["conv_w"], ((0, 4), (0, 0))), c.reshape(8, D_MODEL // 8)])

    cmat = cg.reshape(N_DEV, D_MODEL)
    ada_w = D_MODEL * 6 // N_DEV
    b_cols = lax.dynamic_slice(W["b_ada"], (me * ada_w,), (ada_w,)).reshape(1, ada_w)
    mod_part = _mod_partial(cmat, W["w_ada"], b_cols)
    mod_recv = _small_exchange(jnp.broadcast_to(mod_part[:, None, :], (N_DEV, 8, ada_w)), False, "scatter_mod")
    mod = mod_recv[:, 0, :].reshape(6, D_MODEL)

    st_b = _exchange_start([W["w_branch_attn"].astype(BF16), W["w_branch_mlstm"].astype(BF16), W["w_out"].astype(BF16),
                            W["w_ffn_gate"].T.astype(BF16), W["w_ffn_up"].T.astype(BF16), W["w_ffn_down"].astype(BF16)],
                           True, mod_recv, "gather_rest_start")
    cols = lambda g: g.transpose(1, 0, 2).reshape(g.shape[1], N_DEV * g.shape[2])
    gains = jnp.stack([W["g_pre_mix"], W["g_post_mix"], W["g_pre_ffn"], W["g_post_ffn"]])
    xs, tgt = x[0], loss_target[0]
    t = _tables(mod, gains, cols(g_conv)[0:4], W["conv_b"], W["b_if"], W["mlstm_norm_w"], positions[0])
    vecs = t["vecs"]
    t["vecs"] = _tie(vecs, st_b["token"], "tie_fwd")
    w_cat = _regroup_w_in(g_in)
    a = _mixer_fwd(xs, t, W["attn_sinks"], w_cat)
    g_ba, g_bm, g_out, g_gate, g_up, g_down = _exchange_wait(st_b, a["ym"], "gather_rest_wait")
    w_ba, w_bm, w_out = cols(g_ba), cols(g_bm), g_out.reshape(D_MODEL, D_MODEL)
    b = _ffn_part(xs, tgt, t, a, w_ba, w_bm, w_out, g_gate.reshape(D_FF, D_MODEL), g_up.reshape(D_FF, D_MODEL),
                  g_down.reshape(D_FF, D_MODEL))

    st_f = _exchange_start([b["g_w_gate"].reshape(N_DEV, ff_sh, D_MODEL), b["g_w_up"].reshape(N_DEV, ff_sh, D_MODEL),
                            b["g_w_down"].reshape(N_DEV, ff_sh, D_MODEL)], False, b["dx1"], "scatter_ffn_start")
    t["vecs"] = _tie(vecs, st_f["token"], "tie_bwd")
    cm = _mixer_bwd(b["dx1"], t, a, b, W["attn_sinks"], w_ba, w_bm, w_out)
    pieces = lambda g, n: g.reshape(g.shape[0], N_DEV, n).transpose(1, 0, 2)
    st_m = _exchange_start([_ungroup_w_in(cm["g_w_cat"]), pieces(cm["g_w_ba"], 128), pieces(cm["g_w_bm"], 128),
                            cm["g_w_out"].reshape(N_DEV, D_MODEL // N_DEV, D_MODEL),
                            jnp.pad(pieces(cm["dcw"][0:4], 128), ((0, 0), (0, 4), (0, 0)))], False, cm["dproj"],
                           "scatter_mixer_start")
    r_gate, r_up, r_down = _exchange_wait(st_f, st_m["token"], "scatter_ffn_wait")
    grad_x, acc_p = _pre_bwd(cm["dproj"], xs, b["dx1"], _tie(vecs, st_m["token"], "tie_pre_bwd"), w_cat)
    small = _small_grads(acc_p, b, cm)
    loss = b["loss"]

    big_out = [{} for _ in range(4)]

    def put(n, res):
        for k in range(4):
            big_out[k][n] = res[k][None]

    put("w_ffn_down", _adamw_sum(r_down, W["w_ffn_down"], M["w_ffn_down"], V["w_ffn_down"], "adamw_w_ffn_down"))
    for n, g in (("w_ffn_gate", _sum8(r_gate, "sum_w_ffn_gate").T), ("w_ffn_up", _sum8(r_up, "sum_w_ffn_up").T)):
        put(n, [g] + list(_adamw_plain(g, W[n], M[n], V[n], "adamw_" + n)))

    sg = _small_exchange(_small_table({"b_ada": small["dmod"], "gains": small["dgains"], "conv_b": small["dconv_b"],
                                       "mlstm_norm_w": small["dnorm_w"], "b_if": small["db_if"],
                                       "attn_sinks": small["dsinks"]}), True, "gather_small")
    as_row = lambda a, n: jnp.pad(a, (0, SMALL_AT[n][3] * SMALL_AT[n][1] - a.shape[0]))[None]
    small_res = _adamw_small(sg, {n: [as_row(d[n], n) for d in (W, M, V)] for n in SMALL_AT})
    small_out = [{n: small_res[n][k][:, 0:W[n].shape[0]] for n in SMALL_AT} for k in range(4)]
    dmod_cols = lax.dynamic_slice(sg[:, 0:6, :].reshape(N_DEV, 6 * D_MODEL), (0, me * ada_w), (N_DEV, ada_w))
    ada_out = _adamw_w_ada(cmat, dmod_cols, W["w_ada"], M["w_ada"], V["w_ada"])

    r_in, r_ba, r_bm, r_out, r_conv = _exchange_wait(st_m, grad_x, "scatter_mixer_wait")
    for n, r in (("w_branch_attn", r_ba), ("w_branch_mlstm", r_bm), ("w_out", r_out)):
        put(n, _adamw_sum(r, W[n], M[n], V[n], "adamw_" + n))
    pad4 = lambda v: jnp.pad(v, ((0, 4), (0, 0)))
    put("conv_w", [o[0:4] for o in _adamw_sum(r_conv, pad4(W["conv_w"]), pad4(M["conv_w"]), pad4(V["conv_w"]),
                                                 "adamw_conv_w")])
    g = _sum8(r_in, "sum_w_in")[:, 0:IN_SHARD]
    put("w_in", [g] + list(_adamw_plain(g, W["w_in"], M["w_in"], V["w_in"], "adamw_w_in")))

    total = lax.psum(loss, ("x", "y", "c"))
    outs = [total, grad_x[None]]
    for k in range(4):
        for n in WEIGHT_NAMES:
            if n == "w_ada":
                outs.append(ada_out[k][None])
            elif n in big_out[k]:
                outs.append(big_out[k][n])
            else:
                outs.append(small_out[k][n])
    return tuple(outs)
```

```python
import functools

import jax
import jax.numpy as jnp
import numpy as np
from jax import lax
from jax.experimental import pallas as pl
from jax.experimental.pallas import tpu as pltpu

F32 = jnp.float32
BF16 = jnp.bfloat16

N_DEV = 8
D_MODEL = 1024
D_FF = 2816
N_Q_HEADS = 8
HEAD_DIM = 64
ATTN_BLOCK = 128
ROPE_THETA = 10000.0
MLSTM_HEADS = 4
MLSTM_HEAD_DIM = 128
MLSTM_CHUNK = 128
NORM_EPS = 1e-6
ADAM_LR = 0.001
ADAM_B1 = 0.9
ADAM_B2 = 0.999
ADAM_EPS = 1e-08
ADAM_WD = 0.01
ADAM_STEP = 10

ROW_TILE = 256
WIDE_TILE = 512
LANES = 128
NEG = -1e30
VMEM_LIMIT = 56 * 1024 * 1024

A_W = 768
M_W = 2048
IF_W = 128
G_W = 2048
CAT_W = A_W + M_W + IF_W + G_W

R_SHIFT_M, R_SCALE_M, R_GATE_M, R_SHIFT_F, R_SCALE_F, R_GATE_F = 0, 1, 2, 3, 4, 5
R_G_PRE_MIX, R_G_POST_MIX, R_G_PRE_FFN, R_G_POST_FFN = 6, 7, 8, 9


def _dot(a, b):
    return jnp.dot(a, b, preferred_element_type=F32)


def _dot_nt(a, b):
    return lax.dot_general(a, b, (((1,), (1,)), ((), ())), preferred_element_type=F32)


def _dot_tn(a, b):
    return lax.dot_general(a, b, (((0,), (0,)), ((), ())), preferred_element_type=F32)


def _recip(x):
    return 1.0 / x


def _sigmoid(x):
    return _recip(1.0 + jnp.exp(-x))


def _colsum(x):
    return jnp.sum(x, axis=0, keepdims=True)


def _rowmean(x):
    return jnp.mean(x, axis=-1, keepdims=True)


def _params(sem=None, vmem=VMEM_LIMIT):
    kw = dict(vmem_limit_bytes=vmem)
    if sem is not None:
        kw["dimension_semantics"] = sem
    return pltpu.CompilerParams(**kw)


def _full(shape):
    nd = len(shape)
    return pl.BlockSpec(shape, lambda *_: (0,) * nd)


def _pre_proj(x, vecs, w_cat):
    S = x.shape[0]
    tm = WIDE_TILE

    def body(x_ref, v_ref, w_ref, h_ref, pa_ref, pm_ref, pif_ref, pg_ref):
        xv = x_ref[...]
        r = lax.rsqrt(_rowmean(xv * xv) + NORM_EPS)
        h = (xv * r * v_ref[R_G_PRE_MIX:R_G_PRE_MIX + 1, :]) * (1.0 + v_ref[R_SCALE_M:R_SCALE_M + 1, :]) \
            + v_ref[R_SHIFT_M:R_SHIFT_M + 1, :]
        hb = h.astype(BF16)
        h_ref[...] = hb
        pa_ref[...] = _dot(hb, w_ref[:, 0:A_W])
        pm_ref[...] = _dot(hb, w_ref[:, A_W:A_W + M_W])
        pif_ref[...] = _dot(hb, w_ref[:, A_W + M_W:A_W + M_W + IF_W])
        pg_ref[...] = _dot(hb, w_ref[:, A_W + M_W + IF_W:CAT_W]).astype(BF16)

    row = lambda w: pl.BlockSpec((tm, w), lambda i: (i, 0))
    return pl.pallas_call(
        body, name="pre_proj", grid=(S // tm,),
        in_specs=[row(D_MODEL), _full(vecs.shape), _full(w_cat.shape)],
        out_specs=[row(D_MODEL), row(A_W), row(M_W), row(IF_W), row(G_W)],
        out_shape=[jax.ShapeDtypeStruct((S, D_MODEL), BF16), jax.ShapeDtypeStruct((S, A_W), F32),
                   jax.ShapeDtypeStruct((S, M_W), F32), jax.ShapeDtypeStruct((S, IF_W), F32),
                   jax.ShapeDtypeStruct((S, G_W), BF16)],
        compiler_params=_params(("parallel",)),
    )(x, vecs, w_cat)


def _mix_fwd(x, ya, ym, pg, vecs, w_ba, w_bm, w_out):
    S = x.shape[0]
    tm = WIDE_TILE

    def body(x_ref, ya_ref, ym_ref, pg_ref, v_ref, wba_ref, wbm_ref, wout_ref,
             x1_ref, merged_ref, mix_ref, pa_ref, pb_ref):
        pa = _dot(ya_ref[...], wba_ref[...])
        pb = _dot(ym_ref[...], wbm_ref[...])
        merged = _sigmoid(pg_ref[:, 0:D_MODEL].astype(F32)) * pa + _sigmoid(pg_ref[:, D_MODEL:G_W].astype(F32)) * pb
        mb = merged.astype(BF16)
        mix = _dot(mb, wout_ref[...])
        r = lax.rsqrt(_rowmean(mix * mix) + NORM_EPS)
        x1_ref[...] = x_ref[...] + v_ref[R_GATE_M:R_GATE_M + 1, :] * (mix * r * v_ref[R_G_POST_MIX:R_G_POST_MIX + 1, :])
        merged_ref[...] = mb
        mix_ref[...] = mix
        pa_ref[...] = pa.astype(BF16)
        pb_ref[...] = pb.astype(BF16)

    row = lambda w: pl.BlockSpec((tm, w), lambda i: (i, 0))
    sd = lambda w, dt: jax.ShapeDtypeStruct((S, w), dt)
    return pl.pallas_call(
        body, name="mix_fwd", grid=(S // tm,),
        in_specs=[row(D_MODEL), row(512), row(512), row(G_W), _full(vecs.shape), _full(w_ba.shape),
                  _full(w_bm.shape), _full(w_out.shape)],
        out_specs=[row(D_MODEL)] * 5,
        out_shape=[sd(D_MODEL, F32), sd(D_MODEL, BF16), sd(D_MODEL, F32), sd(D_MODEL, BF16), sd(D_MODEL, BF16)],
        compiler_params=_params(("parallel",)),
    )(x, ya, ym, pg, vecs, w_ba, w_bm, w_out)


def _ffn_fwd_bwd(x1, tgt, vecs, w_gate, w_up, w_down):
    S = x1.shape[0]
    tm = ROW_TILE

    def body(x1_ref, tgt_ref, v_ref, wg_hbm, wu_hbm, wd_hbm,
             dx1_ref, h2_ref, hid_ref, da_ref, du_ref, dff_ref, acc_ref, loss_ref,
             wg, wu, wd, sem):
        i = pl.program_id(0)

        @pl.when(i == 0)
        def _():
            cps = [pltpu.make_async_copy(wg_hbm, wg, sem.at[0]), pltpu.make_async_copy(wu_hbm, wu, sem.at[1]),
                   pltpu.make_async_copy(wd_hbm, wd, sem.at[2])]
            for cp in cps:
                cp.start()
            for cp in cps:
                cp.wait()
            acc_ref[...] = jnp.zeros_like(acc_ref)
            loss_ref[...] = jnp.zeros_like(loss_ref)

        vrow = lambda r: v_ref[r:r + 1, :]
        x1v = x1_ref[...]
        r3 = lax.rsqrt(_rowmean(x1v * x1v) + NORM_EPS)
        x1hat = x1v * r3
        xn3 = x1hat * vrow(R_G_PRE_FFN)
        h2b = (xn3 * (1.0 + vrow(R_SCALE_F)) + vrow(R_SHIFT_F)).astype(BF16)
        h2_ref[...] = h2b
        a = _dot_nt(h2b, wg[...])
        u = _dot_nt(h2b, wu[...])
        sg = _sigmoid(a)
        sil = a * sg
        hidb = (sil * u).astype(BF16)
        hid_ref[...] = hidb
        ff = _dot(hidb, wd[...])
        r4 = lax.rsqrt(_rowmean(ff * ff) + NORM_EPS)
        ffhat = ff * r4
        n4 = ffhat * vrow(R_G_POST_FFN)
        err = x1v + vrow(R_GATE_F) * n4 - tgt_ref[...]
        loss_ref[...] += jnp.sum(err * err) * (0.5 / D_MODEL)
        dy = err * (1.0 / D_MODEL)
        acc_ref[0:1, :] += _colsum(dy * n4)
        dn4 = dy * vrow(R_GATE_F)
        acc_ref[1:2, :] += _colsum(dn4 * ffhat)
        dffhat = dn4 * vrow(R_G_POST_FFN)
        dffb = (r4 * (dffhat - ffhat * _rowmean(dffhat * ffhat))).astype(BF16)
        dff_ref[...] = dffb
        dhid = _dot_nt(dffb, wd[...])
        dub = (dhid * sil).astype(BF16)
        dab = (dhid * u * (sg * (1.0 + a * (1.0 - sg)))).astype(BF16)
        da_ref[...] = dab
        du_ref[...] = dub
        dh2 = _dot(dab, wg[...]) + _dot(dub, wu[...])
        acc_ref[2:3, :] += _colsum(dh2 * xn3)
        acc_ref[3:4, :] += _colsum(dh2)
        dxn3 = dh2 * (1.0 + vrow(R_SCALE_F))
        acc_ref[4:5, :] += _colsum(dxn3 * x1hat)
        dx1hat = dxn3 * vrow(R_G_PRE_FFN)
        dx1_ref[...] = dy + r3 * (dx1hat - x1hat * _rowmean(dx1hat * x1hat))

    row = lambda w: pl.BlockSpec((tm, w), lambda i: (i, 0))
    sd = lambda w, dt: jax.ShapeDtypeStruct((S, w), dt)
    anyspec = pl.BlockSpec(memory_space=pl.ANY)
    return pl.pallas_call(
        body, name="ffn_fwd_bwd", grid=(S // tm,),
        in_specs=[row(D_MODEL), row(D_MODEL), _full(vecs.shape), anyspec, anyspec, anyspec],
        out_specs=[row(D_MODEL), row(D_MODEL), row(D_FF), row(D_FF), row(D_FF), row(D_MODEL),
                   _full((8, D_MODEL)), _full((8, LANES))],
        out_shape=[sd(D_MODEL, F32), sd(D_MODEL, BF16), sd(D_FF, BF16), sd(D_FF, BF16), sd(D_FF, BF16),
                   sd(D_MODEL, BF16), jax.ShapeDtypeStruct((8, D_MODEL), F32), jax.ShapeDtypeStruct((8, LANES), F32)],
        scratch_shapes=[pltpu.VMEM(w_gate.shape, BF16), pltpu.VMEM(w_up.shape, BF16), pltpu.VMEM(w_down.shape, BF16),
                        pltpu.SemaphoreType.DMA((3,))],
        compiler_params=_params(("arbitrary",)),
    )(x1, tgt, vecs, w_gate, w_up, w_down)


def _mix_bwd(dx1, mix, pa, pb, pg, vecs, w_ba, w_bm, w_out):
    S = dx1.shape[0]
    tm = WIDE_TILE

    def body(dx1_ref, mix_ref, pa_ref, pb_ref, pg_ref, v_ref, wba_ref, wbm_ref, wout_ref,
             dmix_ref, dpa_ref, dpb_ref, dg_ref, dya_ref, dym_ref, acc_ref):
        i = pl.program_id(0)

        @pl.when(i == 0)
        def _():
            acc_ref[...] = jnp.zeros_like(acc_ref)

        vrow = lambda r: v_ref[r:r + 1, :]
        dx1v = dx1_ref[...]
        mix = mix_ref[...]
        r2 = lax.rsqrt(_rowmean(mix * mix) + NORM_EPS)
        mixhat = mix * r2
        acc_ref[0:1, :] += _colsum(dx1v * (mixhat * vrow(R_G_POST_MIX)))
        dn2 = dx1v * vrow(R_GATE_M)
        acc_ref[1:2, :] += _colsum(dn2 * mixhat)
        dmixhat = dn2 * vrow(R_G_POST_MIX)
        dmixb = (r2 * (dmixhat - mixhat * _rowmean(dmixhat * mixhat))).astype(BF16)
        dmix_ref[...] = dmixb
        dmerged = _dot_nt(dmixb, wout_ref[...])
        sa = _sigmoid(pg_ref[:, 0:D_MODEL].astype(F32))
        sm = _sigmoid(pg_ref[:, D_MODEL:G_W].astype(F32))
        dpab = (dmerged * sa).astype(BF16)
        dpbb = (dmerged * sm).astype(BF16)
        dpa_ref[...] = dpab
        dpb_ref[...] = dpbb
        dg_ref[:, 0:D_MODEL] = (dmerged * pa_ref[...].astype(F32) * (sa * (1.0 - sa))).astype(BF16)
        dg_ref[:, D_MODEL:G_W] = (dmerged * pb_ref[...].astype(F32) * (sm * (1.0 - sm))).astype(BF16)
        dya_ref[...] = _dot_nt(dpab, wba_ref[...])
        dym_ref[...] = _dot_nt(dpbb, wbm_ref[...])

    row = lambda w: pl.BlockSpec((tm, w), lambda i: (i, 0))
    sd = lambda w, dt: jax.ShapeDtypeStruct((S, w), dt)
    return pl.pallas_call(
        body, name="mix_bwd", grid=(S // tm,),
        in_specs=[row(D_MODEL), row(D_MODEL), row(D_MODEL), row(D_MODEL), row(G_W), _full(vecs.shape),
                  _full(w_ba.shape), _full(w_bm.shape), _full(w_out.shape)],
        out_specs=[row(D_MODEL), row(D_MODEL), row(D_MODEL), row(G_W), row(512), row(512), _full((8, D_MODEL))],
        out_shape=[sd(D_MODEL, BF16), sd(D_MODEL, BF16), sd(D_MODEL, BF16), sd(G_W, BF16), sd(512, F32), sd(512, F32),
                   jax.ShapeDtypeStruct((8, D_MODEL), F32)],
        compiler_params=_params(("arbitrary",)),
    )(dx1, mix, pa, pb, pg, vecs, w_ba, w_bm, w_out)


def _pre_bwd(dproj, x, dx1, vecs, w_cat):
    S = x.shape[0]
    tm = WIDE_TILE

    def body(dp_ref, x_ref, dx1_ref, v_ref, w_ref, dx_ref, acc_ref):
        i = pl.program_id(0)

        @pl.when(i == 0)
        def _():
            acc_ref[...] = jnp.zeros_like(acc_ref)

        vrow = lambda r: v_ref[r:r + 1, :]
        dh = _dot_nt(dp_ref[...], w_ref[...])
        xv = x_ref[...]
        r1 = lax.rsqrt(_rowmean(xv * xv) + NORM_EPS)
        xhat = xv * r1
        acc_ref[0:1, :] += _colsum(dh * (xhat * vrow(R_G_PRE_MIX)))
        acc_ref[1:2, :] += _colsum(dh)
        dxn = dh * (1.0 + vrow(R_SCALE_M))
        acc_ref[2:3, :] += _colsum(dxn * xhat)
        dxhat = dxn * vrow(R_G_PRE_MIX)
        dx_ref[...] = dx1_ref[...] + r1 * (dxhat - xhat * _rowmean(dxhat * xhat))

    row = lambda w: pl.BlockSpec((tm, w), lambda i: (i, 0))
    return pl.pallas_call(
        body, name="pre_bwd", grid=(S // tm,),
        in_specs=[row(CAT_W), row(D_MODEL), row(D_MODEL), _full(vecs.shape), _full(w_cat.shape)],
        out_specs=[row(D_MODEL), _full((8, D_MODEL))],
        out_shape=[jax.ShapeDtypeStruct((S, D_MODEL), F32), jax.ShapeDtypeStruct((8, D_MODEL), F32)],
        compiler_params=_params(("arbitrary",)),
    )(dproj, x, dx1, vecs, w_cat)


def _matmul_tn(a, b, tn, name, ts=1024):
    S, K = a.shape
    N = b.shape[1]
    n_s = S // ts

    def body(a_ref, b_ref, o_ref, acc_ref):
        s = pl.program_id(1)

        @pl.when(s == 0)
        def _():
            acc_ref[...] = jnp.zeros_like(acc_ref)

        acc_ref[...] += _dot_tn(a_ref[...], b_ref[...])

        @pl.when(s == n_s - 1)
        def _():
            o_ref[...] = acc_ref[...].astype(BF16)

    return pl.pallas_call(
        body, name=name, grid=(N // tn, n_s),
        in_specs=[pl.BlockSpec((ts, K), lambda j, s: (s, 0)), pl.BlockSpec((ts, tn), lambda j, s: (s, j))],
        out_specs=pl.BlockSpec((K, tn), lambda j, s: (0, j)),
        out_shape=jax.ShapeDtypeStruct((K, N), BF16),
        scratch_shapes=[pltpu.VMEM((K, tn), F32)],
        compiler_params=_params(("parallel", "arbitrary")),
    )(a, b)


def _rope_swap(t):
    lane = lax.broadcasted_iota(jnp.int32, t.shape, 1)
    first = (lane & (HEAD_DIM - 1)) < (HEAD_DIM // 2)
    return jnp.where(first, pltpu.roll(t, LANES - HEAD_DIM // 2, 1), pltpu.roll(t, HEAD_DIM // 2, 1))


def _rope(t, cos, sin_signed):
    return t * cos + _rope_swap(t) * sin_signed


def _rope_t(d, cos, sin_signed):
    return d * cos + _rope_swap(d * sin_signed)


def _to_kv_lanes(chunk, p, h):
    lane = lax.broadcasted_iota(jnp.int32, chunk.shape, 1)
    src = chunk if p == h else pltpu.roll(chunk, HEAD_DIM, 1)
    return jnp.where((lane >> 6) == h, src, jnp.zeros_like(src))


def _from_kv_lanes(o_a, o_b, h):
    lane = lax.broadcasted_iota(jnp.int32, o_a.shape, 1)
    a = o_a if h == 0 else pltpu.roll(o_a, HEAD_DIM, 1)
    b = o_b if h == 1 else pltpu.roll(o_b, HEAD_DIM, 1)
    return jnp.where(lane < HEAD_DIM, a, b)


def _band_bias(n):
    blk = ATTN_BLOCK
    qi = lax.broadcasted_iota(jnp.int32, (blk, 2 * blk), 0)
    kj = lax.broadcasted_iota(jnp.int32, (blk, 2 * blk), 1)
    seen = (kj > qi) & (kj <= qi + blk) & ((n > 0) | (kj >= blk))
    return jnp.concatenate([jnp.where(seen, 0.0, NEG)] * N_Q_HEADS, axis=0)


def _stack_heads(chunks, h, dtype):
    parts = []
    for g in range(4):
        j = 4 * h + g
        parts.append(_to_kv_lanes(chunks[j // 2], j % 2, h))
    return jnp.concatenate(parts, axis=0).astype(dtype)


def _fused_call(parts, name, n_steps):
    counts = [(len(p["in_specs"]), len(p["out_specs"]), len(p["scratch"])) for p in parts]
    n_in, n_out = sum(c[0] for c in counts), sum(c[1] for c in counts)

    def kernel_fn(*refs):
        i = pl.program_id(0)
        groups, a, b, c = [], 0, n_in, n_in + n_out
        for ci, co, cs in counts:
            groups.append(refs[a:a + ci] + refs[b:b + co] + refs[c:c + cs])
            a, b, c = a + ci, b + co, c + cs
        for p, g in zip(parts, groups):
            p["init"](i, *g)
        for p, g in zip(parts, groups):
            p["body"](i, *g)

    flat = lambda key: [v for p in parts for v in p[key]]
    res = pl.pallas_call(
        kernel_fn, name=name, grid=(n_steps,), in_specs=flat("in_specs"), out_specs=flat("out_specs"),
        out_shape=flat("out_shape"), scratch_shapes=flat("scratch"), compiler_params=_params(("arbitrary",)),
    )(*flat("operands"))
    out, pos = [], 0
    for _, co, _ in counts:
        out.append(res[pos:pos + co])
        pos += co
    return out


def _attn_fwd_part(pa, cos, sin, sinks):
    S = pa.shape[0]
    blk = ATTN_BLOCK
    nb = S // blk

    def body(n, sink_ref, cur_ref, prev_ref, cos_ref, sin_ref, cosp_ref, sinp_ref,
             ya_ref, qr_ref, kr_ref, vb_ref, lse_ref):
        cos_c, sin_c = cos_ref[...], sin_ref[...]
        qch = [_rope(cur_ref[:, c * LANES:(c + 1) * LANES], cos_c, sin_c) * (HEAD_DIM ** -0.5) for c in range(4)]
        for c in range(4):
            qr_ref[:, c * LANES:(c + 1) * LANES] = qch[c].astype(BF16)
        k_cur = _rope(cur_ref[:, 512:640], cos_c, sin_c).astype(BF16)
        k_prev = _rope(prev_ref[:, 0:LANES], cosp_ref[...], sinp_ref[...]).astype(BF16)
        v_cur = cur_ref[:, 640:768].astype(BF16)
        v_prev = prev_ref[:, LANES:2 * LANES].astype(BF16)
        kr_ref[...] = k_cur
        vb_ref[...] = v_cur
        K = jnp.concatenate([k_prev, k_cur], axis=0)
        V = jnp.concatenate([v_prev, v_cur], axis=0)
        lane = lax.broadcasted_iota(jnp.int32, (blk, LANES), 1)
        s = jnp.concatenate([_dot_nt(_stack_heads(qch, h, BF16), K) for h in range(2)], axis=0)
        s = s + _band_bias(n)
        rowmax = jnp.max(s, axis=1, keepdims=True)
        hd = lambda x, j: x[j * blk:(j + 1) * blk]
        m = jnp.concatenate([jnp.maximum(hd(rowmax, j), sink_ref[j]) for j in range(N_Q_HEADS)], axis=0)
        p = jnp.exp(s - m)
        den = jnp.sum(p, axis=1, keepdims=True) \
            + jnp.concatenate([jnp.exp(sink_ref[j] - hd(m, j)) for j in range(N_Q_HEADS)], axis=0)
        pb = (p * _recip(den)).astype(BF16)
        o = jnp.concatenate([_dot(pb[4 * h * blk:4 * (h + 1) * blk], V) for h in range(2)], axis=0)
        lse = m + jnp.log(den)
        outs = [o[j * blk:(j + 1) * blk, :] for j in range(N_Q_HEADS)]
        lse_tile = jnp.zeros((blk, LANES), F32)
        for j in range(N_Q_HEADS):
            lse_tile = jnp.where(lane == j, lse[j * blk:(j + 1) * blk, :], lse_tile)
        for c in range(4):
            ya_ref[:, c * LANES:(c + 1) * LANES] = _from_kv_lanes(outs[2 * c], outs[2 * c + 1], c // 2).astype(BF16)
        lse_ref[...] = lse_tile

    prev = lambda n: jnp.maximum(n - 1, 0)
    sd = lambda w, dt: jax.ShapeDtypeStruct((S, w), dt)
    return dict(
        init=lambda n, *refs: None, body=body, scratch=[], operands=[sinks, pa, pa, cos, sin, cos, sin],
        in_specs=[pl.BlockSpec(memory_space=pltpu.SMEM),
                  pl.BlockSpec((blk, A_W), lambda n: (n, 0)),
                  pl.BlockSpec((blk, 256), lambda n: (prev(n), 2)),
                  pl.BlockSpec((blk, LANES), lambda n: (n, 0)), pl.BlockSpec((blk, LANES), lambda n: (n, 0)),
                  pl.BlockSpec((blk, LANES), lambda n: (prev(n), 0)), pl.BlockSpec((blk, LANES), lambda n: (prev(n), 0))],
        out_specs=[pl.BlockSpec((blk, 512), lambda n: (n, 0)), pl.BlockSpec((blk, 512), lambda n: (n, 0)),
                   pl.BlockSpec((blk, LANES), lambda n: (n, 0)), pl.BlockSpec((blk, LANES), lambda n: (n, 0)),
                   pl.BlockSpec((blk, LANES), lambda n: (n, 0))],
        out_shape=[sd(512, BF16), sd(512, BF16), sd(LANES, BF16), sd(LANES, BF16), sd(LANES, F32)])


def _attn_bwd_part(dya, qr, kr, vb, lse, cos, sin, sinks):
    S = dya.shape[0]
    blk = ATTN_BLOCK
    nb = S // blk

    def init(n, sink_ref, dya_ref, qr_ref, kc_ref, kp_ref, vc_ref, vp_ref, lse_ref, cos_ref, sin_ref, cosp_ref, sinp_ref,
             dq_ref, dkv_ref, last_ref, dsink_ref, ck, cv):
        @pl.when(n == 0)
        def _():
            ck[...] = jnp.zeros_like(ck)
            cv[...] = jnp.zeros_like(cv)
            dsink_ref[...] = jnp.zeros_like(dsink_ref)

    def body(n, sink_ref, dya_ref, qr_ref, kc_ref, kp_ref, vc_ref, vp_ref, lse_ref, cos_ref, sin_ref, cosp_ref, sinp_ref,
             dq_ref, dkv_ref, last_ref, dsink_ref, ck, cv):
        K = jnp.concatenate([kp_ref[...], kc_ref[...]], axis=0)
        V = jnp.concatenate([vp_ref[...], vc_ref[...]], axis=0)
        qch = [qr_ref[:, c * LANES:(c + 1) * LANES] for c in range(4)]
        dch = [dya_ref[:, c * LANES:(c + 1) * LANES] for c in range(4)]
        lse_tile = lse_ref[...]
        lane8 = lax.broadcasted_iota(jnp.int32, (8, LANES), 1)
        grp = lambda x, h: x[4 * h * blk:4 * (h + 1) * blk]
        qs = jnp.concatenate([_stack_heads(qch, h, BF16) for h in range(2)], axis=0)
        dos = jnp.concatenate([_stack_heads(dch, h, BF16) for h in range(2)], axis=0)
        lse_col = jnp.concatenate([lse_tile[:, j:j + 1] for j in range(N_Q_HEADS)], axis=0)
        s = jnp.concatenate([_dot_nt(grp(qs, h), K) for h in range(2)], axis=0)
        p = jnp.exp(s + _band_bias(n) - lse_col)
        dp = jnp.concatenate([_dot_nt(grp(dos, h), V) for h in range(2)], axis=0)
        delta = jnp.sum(p * dp, axis=1, keepdims=True)
        dsb = (p * (dp - delta)).astype(BF16)
        pb = p.astype(BF16)
        dq = jnp.concatenate([_dot(grp(dsb, h), K) for h in range(2)], axis=0)
        dk_acc = _dot_tn(grp(dsb, 0), grp(qs, 0)) + _dot_tn(grp(dsb, 1), grp(qs, 1))
        dv_acc = _dot_tn(grp(pb, 0), grp(dos, 0)) + _dot_tn(grp(pb, 1), grp(dos, 1))
        dqs = [dq[j * blk:(j + 1) * blk, :] for j in range(N_Q_HEADS)]
        dsink = jnp.zeros((8, LANES), F32)
        for j in range(N_Q_HEADS):
            rows = slice(j * blk, (j + 1) * blk)
            ps_delta = jnp.exp(sink_ref[j] - lse_col[rows]) * delta[rows]
            dsink = jnp.where(lane8 == j, dsink - jnp.sum(ps_delta), dsink)
        dsink_ref[...] += dsink
        cos_c, sin_c = cos_ref[...], sin_ref[...]
        for c in range(4):
            dqc = _from_kv_lanes(dqs[2 * c], dqs[2 * c + 1], c // 2) * (HEAD_DIM ** -0.5)
            dq_ref[:, c * LANES:(c + 1) * LANES] = _rope_t(dqc, cos_c, sin_c).astype(BF16)
        dkv_ref[:, 0:LANES] = _rope_t(dk_acc[0:blk, :] + ck[...], cosp_ref[...], sinp_ref[...]).astype(BF16)
        dkv_ref[:, LANES:2 * LANES] = (dv_acc[0:blk, :] + cv[...]).astype(BF16)
        ck[...] = dk_acc[blk:2 * blk, :]
        cv[...] = dv_acc[blk:2 * blk, :]
        last_ref[:, 0:LANES] = _rope_t(dk_acc[blk:2 * blk, :], cos_c, sin_c).astype(BF16)
        last_ref[:, LANES:2 * LANES] = dv_acc[blk:2 * blk, :].astype(BF16)

    prev = lambda n: jnp.maximum(n - 1, 0)
    same = lambda n: n
    bs = lambda w, f: pl.BlockSpec((blk, w), lambda n: (f(n), 0))
    return dict(
        init=init, body=body, operands=[sinks, dya, qr, kr, kr, vb, vb, lse, cos, sin, cos, sin],
        in_specs=[pl.BlockSpec(memory_space=pltpu.SMEM),
                  bs(512, same), bs(512, same), bs(LANES, same), bs(LANES, prev), bs(LANES, same), bs(LANES, prev),
                  bs(LANES, same), bs(LANES, same), bs(LANES, same), bs(LANES, prev), bs(LANES, prev)],
        out_specs=[bs(512, same), bs(256, prev), _full((blk, 256)), _full((8, LANES))],
        out_shape=[jax.ShapeDtypeStruct((S, 512), BF16), jax.ShapeDtypeStruct((S, 256), BF16),
                   jax.ShapeDtypeStruct((blk, 256), BF16), jax.ShapeDtypeStruct((8, LANES), F32)],
        scratch=[pltpu.VMEM((blk, LANES), F32), pltpu.VMEM((blk, LANES), F32)])


def _split3(x):
    hi = x.astype(BF16)
    r1 = x - hi.astype(F32)
    mid = r1.astype(BF16)
    lo = (r1 - mid.astype(F32)).astype(BF16)
    return hi, mid, lo


def _tri_matmul(tri_b, x):
    hi, mid, lo = _split3(x)
    return _dot(tri_b, hi) + _dot(tri_b, mid) + _dot(tri_b, lo)


def _log_sigmoid(x):
    return jnp.minimum(x, 0.0) - jnp.log(1.0 + jnp.exp(-jnp.abs(x)))


def _shift_rows(cur, seam, k, down):
    L = cur.shape[0]
    row8 = lax.broadcasted_iota(jnp.int32, seam.shape, 0)
    if down:
        mixed = jnp.concatenate([cur[:L - 8], jnp.where(row8 >= 8 - k, seam, cur[L - 8:])], axis=0)
        return pltpu.roll(mixed, k, 0)
    mixed = jnp.concatenate([jnp.where(row8 < k, seam, cur[:8]), cur[8:]], axis=0)
    return pltpu.roll(mixed, L - k, 0)


def _conv_fwd(cur, tail, cw_ref):
    z = cw_ref[4:5, :]
    for k in range(3, 0, -1):
        z = z + _shift_rows(cur, tail, k, True) * cw_ref[3 - k:4 - k, :]
    return z + cur * cw_ref[3:4, :]


def _stack(f):
    return jnp.concatenate([f(h) for h in range(MLSTM_HEADS)], axis=0)


def _head(x, h):
    L = x.shape[0] // MLSTM_HEADS
    return x[h * L:(h + 1) * L]


def _mlstm_heads_fwd(qk, cur_ref, gt, b_all, c_prev, nmv, tri, eye):
    L = qk.shape[0]
    HD = MLSTM_HEAD_DIM
    W4 = MLSTM_HEADS * HD
    col2row = lambda x: jnp.sum(jnp.where(eye, x, 0.0), axis=0, keepdims=True)
    b_col = _stack(lambda h: b_all[:, 4 + h:5 + h])
    i_col = _stack(lambda h: gt[:, h:h + 1])
    b_row = _stack(lambda h: jnp.broadcast_to(col2row(b_all[:, 4 + h:5 + h]), (L, L)))
    i_row = _stack(lambda h: jnp.broadcast_to(col2row(gt[:, h:h + 1]), (L, L)))
    bl = _stack(lambda h: jnp.broadcast_to(b_all[L - 1:L, 4 + h:5 + h], (L, 1)))
    m_prev = _stack(lambda h: jnp.broadcast_to(nmv[4 + h:5 + h, 0:1], (L, 1)))
    n_prev = _stack(lambda h: jnp.broadcast_to(nmv[h:h + 1, :], (L, HD)))
    tri4 = jnp.concatenate([tri] * MLSTM_HEADS, axis=0)
    Dm = jnp.where(tri4, b_col - b_row + i_row, NEG)
    inter = b_col + m_prev
    m_t = jnp.maximum(inter, jnp.max(Dm, axis=1, keepdims=True))
    W = jnp.exp(Dm - m_t)
    e_t = jnp.exp(inter - m_t)
    q = _stack(lambda h: qk[:, h * HD:(h + 1) * HD])
    k = _stack(lambda h: qk[:, W4 + h * HD:W4 + (h + 1) * HD]) * (HD ** -0.5)
    v = _stack(lambda h: cur_ref[:, 2 * W4 + h * HD:2 * W4 + (h + 1) * HD])
    qb, kb, vb = q.astype(BF16), k.astype(BF16), v.astype(BF16)
    Sc = _stack(lambda h: _dot_nt(_head(qb, h), _head(kb, h))) * W
    Scb = Sc.astype(BF16)
    cb = [c.astype(BF16) for c in c_prev]
    P1 = _stack(lambda h: _dot(_head(qb, h), cb[h]))
    num = _stack(lambda h: _dot(_head(Scb, h), _head(vb, h))) + e_t * P1
    qn = jnp.sum(q * n_prev, axis=1, keepdims=True)
    den = jnp.sum(Sc, axis=1, keepdims=True) + e_t * qn
    floor = jnp.exp(-m_t)
    inv_g = _recip(jnp.maximum(jnp.abs(den), floor))
    hv = num * inv_g
    a_col = bl - b_col + i_col
    a_max = _stack(lambda h: jnp.broadcast_to(jnp.max(_head(a_col, h), axis=0, keepdims=True), (L, 1)))
    m_new = jnp.maximum(bl + m_prev, a_max)
    dec = jnp.exp(bl + m_prev - m_new)
    u_col = jnp.exp(a_col - m_new)
    return dict(W=W, e_t=e_t, q=q, k=k, v=v, qb=qb, kb=kb, vb=vb, cb=cb, Sc=Sc, Scb=Scb, P1=P1, qn=qn, den=den,
                floor=floor, inv_g=inv_g, hv=hv, n_prev=n_prev, m_new=m_new, dec=dec, u_col=u_col)


def _mlstm_fwd_part(pm, pif, cw, sv):
    S = pm.shape[0]
    L = MLSTM_CHUNK
    nc = S // L
    HD = MLSTM_HEAD_DIM
    W4 = MLSTM_HEADS * HD

    def init(c, cur_ref, pif_ref, cw_ref, sv_ref, ym_ref, z_ref, cst_ref, nst_ref, C, nm, tail):
        @pl.when(c == 0)
        def _():
            C[...] = jnp.zeros_like(C)
            nm[...] = jnp.zeros_like(nm)
            tail[...] = jnp.zeros_like(tail)

    def body(c, cur_ref, pif_ref, cw_ref, sv_ref, ym_ref, z_ref, cst_ref, nst_ref, C, nm, tail):
        z = _conv_fwd(cur_ref[:, 0:2 * W4], tail[...], cw_ref)
        tail[...] = cur_ref[L - 8:L, 0:2 * W4]
        z_ref[...] = z
        qk = z * _sigmoid(z)
        gt = pif_ref[...] + sv_ref[1:2, 0:LANES]
        r_i = lax.broadcasted_iota(jnp.int32, (L, L), 0)
        c_i = lax.broadcasted_iota(jnp.int32, (L, L), 1)
        tri = c_i <= r_i
        eye = c_i == r_i
        b_all = _tri_matmul(tri.astype(BF16), _log_sigmoid(gt))
        nmv = nm[...]
        nst_ref[0] = nmv
        c_prev = [C[h] for h in range(MLSTM_HEADS)]
        f = _mlstm_heads_fwd(qk, cur_ref, gt, b_all, c_prev, nmv, tri, eye)
        hv = f["hv"]
        xc = hv - _rowmean(hv)
        hhat = xc * lax.rsqrt(_rowmean(xc * xc) + NORM_EPS)
        so = _sigmoid(_stack(lambda h: cur_ref[:, 3 * W4 + h * HD:3 * W4 + (h + 1) * HD]))
        wn = _stack(lambda h: jnp.broadcast_to(sv_ref[0:1, h * HD:(h + 1) * HD], (L, HD)))
        y = (so * hhat * wn).astype(BF16)
        kw = f["k"] * f["u_col"]
        kwb = kw.astype(BF16)
        n_new, m_new = [], []
        for h in range(MLSTM_HEADS):
            cst_ref[0, h] = c_prev[h]
            ym_ref[:, h * HD:(h + 1) * HD] = _head(y, h)
            dec = f["dec"][h * L:h * L + 1, :]
            C[h] = dec * c_prev[h] + _dot_tn(_head(kwb, h), _head(f["vb"], h))
            n_new.append(dec * nmv[h:h + 1, :] + _colsum(_head(kw, h)))
            m_new.append(jnp.broadcast_to(f["m_new"][h * L:h * L + 1, :], (1, LANES)))
        nm[...] = jnp.concatenate(n_new + m_new, axis=0)

    return dict(
        init=init, body=body, operands=[pm, pif, cw, sv],
        in_specs=[pl.BlockSpec((L, M_W), lambda c: (c, 0)),
                  pl.BlockSpec((L, IF_W), lambda c: (c, 0)), _full(cw.shape), _full(sv.shape)],
        out_specs=[pl.BlockSpec((L, W4), lambda c: (c, 0)), pl.BlockSpec((L, 2 * W4), lambda c: (c, 0)),
                   pl.BlockSpec((1, MLSTM_HEADS, HD, HD), lambda c: (c, 0, 0, 0)),
                   pl.BlockSpec((1, 8, LANES), lambda c: (c, 0, 0))],
        out_shape=[jax.ShapeDtypeStruct((S, W4), BF16), jax.ShapeDtypeStruct((S, 2 * W4), F32),
                   jax.ShapeDtypeStruct((nc, MLSTM_HEADS, HD, HD), F32), jax.ShapeDtypeStruct((nc, 8, LANES), F32)],
        scratch=[pltpu.VMEM((MLSTM_HEADS, HD, HD), F32), pltpu.VMEM((8, LANES), F32), pltpu.VMEM((8, 2 * W4), F32)])


def _mlstm_bwd_part(pm, zc, pif, cw, sv, dym, cst, nst):
    S = pm.shape[0]
    L = MLSTM_CHUNK
    nc = S // L
    HD = MLSTM_HEAD_DIM
    W4 = MLSTM_HEADS * HD

    def init(r, cur_ref, z_ref, pif_ref, cw_ref, sv_ref, dym_ref, cst_ref, nst_ref,
             dm_ref, dif_ref, dcw_ref, dsv_ref, dC, dn, dz_next, dqk):
        @pl.when(r == 0)
        def _():
            dC[...] = jnp.zeros_like(dC)
            dn[...] = jnp.zeros_like(dn)
            dz_next[...] = jnp.zeros_like(dz_next)
            dcw_ref[...] = jnp.zeros_like(dcw_ref)
            dsv_ref[...] = jnp.zeros_like(dsv_ref)

    def body(r, cur_ref, z_ref, pif_ref, cw_ref, sv_ref, dym_ref, cst_ref, nst_ref,
             dm_ref, dif_ref, dcw_ref, dsv_ref, dC, dn, dz_next, dqk):
        z = z_ref[...]
        sgz = _sigmoid(z)
        qk = z * sgz
        gt = pif_ref[...] + sv_ref[1:2, 0:LANES]
        r_i = lax.broadcasted_iota(jnp.int32, (L, L), 0)
        c_i = lax.broadcasted_iota(jnp.int32, (L, L), 1)
        tri = c_i <= r_i
        eye = c_i == r_i
        b_all = _tri_matmul(tri.astype(BF16), _log_sigmoid(gt))
        lane = lax.broadcasted_iota(jnp.int32, (L, LANES), 1)
        rowl = lax.broadcasted_iota(jnp.int32, (L, 1), 0)
        nmv = nst_ref[0]
        heads = range(MLSTM_HEADS)
        c_prev = [cst_ref[0, h] for h in heads]
        f = _mlstm_heads_fwd(qk, cur_ref, gt, b_all, c_prev, nmv, tri, eye)
        hv, inv_g, den, e_t, u_col, n_prev = f["hv"], f["inv_g"], f["den"], f["e_t"], f["u_col"], f["n_prev"]
        q, k, v, qb, kb, vb, Sc, Scb, W = f["q"], f["k"], f["v"], f["qb"], f["kb"], f["vb"], f["Sc"], f["Scb"], f["W"]
        xc = hv - _rowmean(hv)
        rstd = lax.rsqrt(_rowmean(xc * xc) + NORM_EPS)
        hhat = xc * rstd
        wn = _stack(lambda h: jnp.broadcast_to(sv_ref[0:1, h * HD:(h + 1) * HD], (L, HD)))
        so = _sigmoid(_stack(lambda h: cur_ref[:, 3 * W4 + h * HD:3 * W4 + (h + 1) * HD]))
        dy = _stack(lambda h: dym_ref[:, h * HD:(h + 1) * HD])
        d_o = (dy * hhat * wn * (so * (1.0 - so))).astype(BF16)
        dln = dy * so
        dwn = dln * hhat
        dhhat = dln * wn
        dh = rstd * (dhhat - _rowmean(dhhat) - hhat * _rowmean(dhhat * hhat))
        dnum = dh * inv_g
        active = jnp.abs(den) > f["floor"]
        dden = jnp.where(active, -jnp.sum(dh * hv, axis=1, keepdims=True) * inv_g * jnp.where(den >= 0.0, 1.0, -1.0), 0.0)
        dnumb = dnum.astype(BF16)
        dSc = _stack(lambda h: _dot_nt(_head(dnumb, h), _head(vb, h))) + dden
        dA = (dSc * W).astype(BF16)
        G = dSc * Sc
        Gr = jnp.sum(G, axis=1, keepdims=True)
        Gc = _stack(lambda h: jnp.sum(jnp.where(eye, _colsum(_head(G, h)), 0.0), axis=1, keepdims=True))
        dCn = [dC[h] for h in heads]
        dCnb = [d.astype(BF16) for d in dCn]
        dnv = dn[...]
        dn_new = _stack(lambda h: jnp.broadcast_to(dnv[h:h + 1, :], (L, HD)))
        kdC = _stack(lambda h: _dot(_head(kb, h), dCnb[h]))
        vdC = _stack(lambda h: _dot_nt(_head(vb, h), dCnb[h]))
        dv = (_stack(lambda h: _dot_tn(_head(Scb, h), _head(dnumb, h))) + u_col * kdC).astype(BF16)
        dq = _stack(lambda h: _dot(_head(dA, h), _head(kb, h))) \
            + e_t * _stack(lambda h: _dot_nt(_head(dnumb, h), f["cb"][h])) + (e_t * dden) * n_prev
        dk = (_stack(lambda h: _dot_tn(_head(dA, h), _head(qb, h))) + u_col * (vdC + dn_new)) * (HD ** -0.5)
        E = (jnp.sum(f["P1"] * dnum, axis=1, keepdims=True) + dden * f["qn"]) * e_t
        U = (jnp.sum(kdC * v, axis=1, keepdims=True) + jnp.sum(k * dn_new, axis=1, keepdims=True)) * u_col
        qe = (q * e_t).astype(BF16)
        qd = (e_t * dden) * q
        di = Gc + U
        db = Gr + E - Gc - U
        di_tile = jnp.zeros((L, LANES), F32)
        db_tile = jnp.zeros((L, LANES), F32)
        dn_rows = []
        for h in heads:
            dec = f["dec"][h * L:h * L + 1, :]
            ddec = jnp.sum(dCn[h] * c_prev[h]) + jnp.sum(dnv[h:h + 1, :] * nmv[h:h + 1, :])
            dbl = ddec * dec + jnp.sum(_head(U, h), axis=0, keepdims=True)
            di_tile = jnp.where(lane == h, _head(di, h), di_tile)
            db_tile = jnp.where(lane == 4 + h, _head(db, h) + jnp.where(rowl == L - 1, dbl, 0.0), db_tile)
            dC[h] = dec * dCn[h] + _dot_tn(_head(qe, h), _head(dnumb, h))
            dn_rows.append(dec * dnv[h:h + 1, :] + _colsum(_head(qd, h)))
            dsv_ref[0:1, h * HD:(h + 1) * HD] += _colsum(_head(dwn, h))
            dqk[:, h * HD:(h + 1) * HD] = _head(dq, h)
            dqk[:, W4 + h * HD:W4 + (h + 1) * HD] = _head(dk, h)
            dm_ref[:, 2 * W4 + h * HD:2 * W4 + (h + 1) * HD] = _head(dv, h)
            dm_ref[:, 3 * W4 + h * HD:3 * W4 + (h + 1) * HD] = _head(d_o, h)
        dn[...] = jnp.concatenate(dn_rows + [jnp.zeros((8 - MLSTM_HEADS, LANES), F32)], axis=0)
        dlf = _tri_matmul((r_i <= c_i).astype(BF16), db_tile)
        dif = jnp.where(lane < 4, di_tile, jnp.where(lane < 8, dlf * (1.0 - _sigmoid(gt)), 0.0))
        dif_ref[...] = dif.astype(BF16)
        dsv_ref[1:2, 0:LANES] += _colsum(dif)
        dz = dqk[...] * (sgz * (1.0 + z * (1.0 - sgz)))
        dcw_ref[4:5, :] += _colsum(dz)
        u = cur_ref[:, 0:2 * W4]
        du_in = dz * cw_ref[3:4, :]
        dcw_ref[3:4, :] += _colsum(dz * u)
        for k in range(1, 4):
            up = _shift_rows(dz, dz_next[...], k, False)
            dcw_ref[3 - k:4 - k, :] += _colsum(up * u)
            du_in = du_in + up * cw_ref[3 - k:4 - k, :]
        dz_next[...] = dz[0:8, :]
        dm_ref[:, 0:2 * W4] = du_in.astype(BF16)

    cidx = lambda r: nc - 1 - r
    return dict(
        init=init, body=body, operands=[pm, zc, pif, cw, sv, dym, cst, nst],
        in_specs=[pl.BlockSpec((L, M_W), lambda r: (cidx(r), 0)), pl.BlockSpec((L, 2 * W4), lambda r: (cidx(r), 0)),
                  pl.BlockSpec((L, IF_W), lambda r: (cidx(r), 0)), _full(cw.shape), _full(sv.shape),
                  pl.BlockSpec((L, W4), lambda r: (cidx(r), 0)),
                  pl.BlockSpec((1, MLSTM_HEADS, HD, HD), lambda r: (cidx(r), 0, 0, 0)),
                  pl.BlockSpec((1, 8, LANES), lambda r: (cidx(r), 0, 0))],
        out_specs=[pl.BlockSpec((L, M_W), lambda r: (cidx(r), 0)), pl.BlockSpec((L, IF_W), lambda r: (cidx(r), 0)),
                   _full((8, 2 * W4)), _full((8, W4))],
        out_shape=[jax.ShapeDtypeStruct((S, M_W), BF16), jax.ShapeDtypeStruct((S, IF_W), BF16),
                   jax.ShapeDtypeStruct((8, 2 * W4), F32), jax.ShapeDtypeStruct((8, W4), F32)],
        scratch=[pltpu.VMEM((MLSTM_HEADS, HD, HD), F32), pltpu.VMEM((8, LANES), F32),
                 pltpu.VMEM((8, 2 * W4), F32), pltpu.VMEM((L, 2 * W4), F32)])


def _rope_tables(positions):
    half = HEAD_DIM // 2
    inv_freq = ROPE_THETA ** (-2.0 * jnp.arange(half, dtype=F32) / HEAD_DIM)
    ang = positions.astype(F32)[:, None] * inv_freq
    cos = jnp.tile(jnp.cos(ang), (1, LANES // half))
    sign = jnp.tile(jnp.concatenate([-jnp.ones((half,), F32), jnp.ones((half,), F32)]), LANES // HEAD_DIM)
    sin = jnp.tile(jnp.sin(ang), (1, LANES // half)) * sign
    return cos, sin


def _local_step(x, tgt, positions, mod, gains, w_cat, w_ba, w_bm, w_out, w_gate, w_up, w_down,
                conv_w, conv_b, b_if, sinks, norm_w):
    t = _tables(mod, gains, conv_w, conv_b, b_if, norm_w, positions)
    a = _mixer_fwd(x, t, sinks, w_cat)
    b = _ffn_part(x, tgt, t, a, w_ba, w_bm, w_out, w_gate, w_up, w_down)
    c = _mixer_bwd(b["dx1"], t, a, b, sinks, w_ba, w_bm, w_out)
    grad_x, acc_p = _pre_bwd(c["dproj"], x, b["dx1"], t["vecs"], w_cat)
    big = dict(w_cat=c["g_w_cat"], w_ba=c["g_w_ba"], w_bm=c["g_w_bm"], w_out=c["g_w_out"], w_gate=b["g_w_gate"],
               w_up=b["g_w_up"], w_down=b["g_w_down"])
    return b["loss"], grad_x, big, _small_grads(acc_p, b, c)


def _tables(mod, gains, conv_w, conv_b, b_if, norm_w, positions):
    cos, sin = _rope_tables(positions)
    return dict(
        vecs=jnp.concatenate([mod, gains, jnp.zeros((6, D_MODEL), F32)], axis=0),
        cw=jnp.concatenate([conv_w, conv_b.reshape(1, -1), jnp.zeros((3, 2 * 512), F32)], axis=0),
        sv=jnp.zeros((8, 512), F32).at[0].set(norm_w).at[1, 0:8].set(b_if), cos=cos, sin=sin)


def _mixer_fwd(x, t, sinks, w_cat):
    h, pa, pm, pif, pg = _pre_proj(x, t["vecs"], w_cat)
    n_blk = x.shape[0] // ATTN_BLOCK
    (ya, qr, kr, vb, lse), = _fused_call([_attn_fwd_part(pa, t["cos"], t["sin"], sinks)], "attn_fwd", n_blk)
    (ym, zc, cst, nst), = _fused_call([_mlstm_fwd_part(pm, pif, t["cw"], t["sv"])], "mlstm_fwd", n_blk)
    return dict(h=h, pm=pm, pif=pif, pg=pg, ya=ya, qr=qr, kr=kr, vb=vb, lse=lse, ym=ym, zc=zc, cst=cst, nst=nst)


def _ffn_part(x, tgt, t, a, w_ba, w_bm, w_out, w_gate, w_up, w_down):
    x1, merged, mix, pba, pbm = _mix_fwd(x, a["ya"], a["ym"], a["pg"], t["vecs"], w_ba, w_bm, w_out)
    dx1, h2, hid, da, du, dff, acc_f, loss = _ffn_fwd_bwd(x1, tgt, t["vecs"], w_gate, w_up, w_down)
    return dict(merged=merged, mix=mix, pba=pba, pbm=pbm, dx1=dx1, acc_f=acc_f, loss=loss[0, 0],
                g_w_gate=_matmul_tn(da, h2, 1024, "dw_ffn_gate"),
                g_w_up=_matmul_tn(du, h2, 1024, "dw_ffn_up"),
                g_w_down=_matmul_tn(hid, dff, 1024, "dw_ffn_down"))


def _mixer_bwd(dx1, t, a, b, sinks, w_ba, w_bm, w_out):
    dmix, dpa, dpb, dg, dya, dym, acc_m = _mix_bwd(dx1, b["mix"], b["pba"], b["pbm"], a["pg"], t["vecs"], w_ba, w_bm, w_out)
    g_w_out = _matmul_tn(b["merged"], dmix, 1024, "dw_out")
    g_w_ba = _matmul_tn(a["ya"], dpa, 1024, "dw_branch_attn")
    g_w_bm = _matmul_tn(a["ym"], dpb, 1024, "dw_branch_mlstm")
    n_blk = dx1.shape[0] // ATTN_BLOCK
    (dq, dkv, dkv_last, dsink), = _fused_call(
        [_attn_bwd_part(dya, a["qr"], a["kr"], a["vb"], a["lse"], t["cos"], t["sin"], sinks)], "attn_bwd", n_blk)
    (dm, dif, dcw, dsv), = _fused_call(
        [_mlstm_bwd_part(a["pm"], a["zc"], a["pif"], t["cw"], t["sv"], dym, a["cst"], a["nst"])], "mlstm_bwd", n_blk)
    dkv = lax.dynamic_update_slice(dkv, dkv_last, (dkv.shape[0] - ATTN_BLOCK, 0))
    dproj = jnp.concatenate([dq, dkv, dm, dif, dg], axis=1)
    return dict(dproj=dproj, g_w_cat=_matmul_tn(a["h"], dproj, 1664, "dw_in"), g_w_out=g_w_out, g_w_ba=g_w_ba,
                g_w_bm=g_w_bm, acc_m=acc_m, dsink=dsink, dcw=dcw, dsv=dsv)


def _small_grads(acc_p, b, c):
    acc_f, acc_m = b["acc_f"], c["acc_m"]
    dmod = jnp.stack([acc_p[1], acc_p[0], acc_m[0], acc_f[3], acc_f[2], acc_f[0]])
    dgains = jnp.stack([acc_p[2], acc_m[1], acc_f[4], acc_f[1]])
    return dict(dmod=dmod, dgains=dgains, dconv_w=c["dcw"][0:4], dconv_b=c["dcw"][4], db_if=c["dsv"][1, 0:8],
                dsinks=c["dsink"][0, 0:8], dnorm_w=c["dsv"][0])


MESH_ID = pl.DeviceIdType.MESH


def _mesh_pos():
    return lax.axis_index("x"), lax.axis_index("y"), lax.axis_index("c")


def _flip(v, bit):
    return 1 - v if bit else v


def _relations():
    return [((r >> 2) & 1, (r >> 1) & 1, r & 1) for r in range(1, N_DEV)]


def _small_exchange(p, gather, name):
    R, V = p.shape[-2:]

    def body(p_ref, out_ref, send_sems, recv_sems):
        x, y, c = _mesh_pos()
        me = 4 * x + 2 * y + c
        out_ref[me] = p_ref[...] if gather else p_ref[me]
        peers = []
        for dx, dy, dc in _relations():
            px, py, pc = _flip(x, dx), _flip(y, dy), _flip(c, dc)
            peers.append(((px, py, pc), 4 * px + 2 * py + pc))

        def copy(k, landing):
            peer, pid = peers[k]
            return pltpu.make_async_remote_copy(
                src_ref=p_ref if gather else p_ref.at[pid], dst_ref=out_ref.at[landing],
                send_sem=send_sems.at[k], recv_sem=recv_sems.at[k], device_id=peer, device_id_type=MESH_ID)

        sends = [copy(k, me) for k in range(N_DEV - 1)]
        for cp in sends:
            cp.start()
        for k in range(N_DEV - 1):
            copy(k, peers[k][1]).wait_recv()
        for cp in sends:
            cp.wait_send()

    vm = pl.BlockSpec(memory_space=pltpu.VMEM)
    return pl.pallas_call(
        body, name=name, in_specs=[vm], out_specs=vm,
        out_shape=jax.ShapeDtypeStruct((N_DEV, R, V), F32),
        scratch_shapes=[pltpu.SemaphoreType.DMA((N_DEV - 1,)), pltpu.SemaphoreType.DMA((N_DEV - 1,))],
        compiler_params=pltpu.CompilerParams(vmem_limit_bytes=VMEM_LIMIT),
    )(p)


HBM_SPEC = pl.BlockSpec(memory_space=pltpu.HBM)
SEM_SPEC = pl.BlockSpec(memory_space=pltpu.SEMAPHORE)


def _peers(x, y, c):
    out = []
    for dx, dy, dc in _relations():
        px, py, pc = _flip(x, dx), _flip(y, dy), _flip(c, dc)
        out.append(((px, py, pc), 4 * px + 2 * py + pc))
    return out


def _exchange_start(arrs, gather, after, name):
    n = len(arrs)
    me_out = 4 * lax.axis_index("x") + 2 * lax.axis_index("y") + lax.axis_index("c")
    lands = []
    for a in arrs:
        own = a[None] if gather else lax.dynamic_index_in_dim(a, me_out, 0, keepdims=True)
        empty = lax.empty(((N_DEV,) + a.shape) if gather else a.shape, a.dtype)
        lands.append(lax.dynamic_update_index_in_dim(empty, own, me_out, 0))

    def body(*refs):
        a_refs, l_refs = refs[:n], refs[n:2 * n]
        send_sems, recv_sems = refs[2 * n + 1], refs[2 * n + 2]
        token = refs[4 * n + 3]
        x, y, c = _mesh_pos()
        me = 4 * x + 2 * y + c
        for a in range(n):
            for k, (peer, pid) in enumerate(_peers(x, y, c)):
                pltpu.make_async_remote_copy(
                    src_ref=a_refs[a] if gather else a_refs[a].at[pid], dst_ref=l_refs[a].at[me],
                    send_sem=send_sems.at[a * (N_DEV - 1) + k], recv_sem=recv_sems.at[a * (N_DEV - 1) + k],
                    device_id=peer, device_id_type=MESH_ID).start()
        token[...] = jnp.zeros_like(token)

    sem = pltpu.SemaphoreType.DMA((n * (N_DEV - 1),))
    hbm = lambda a: pltpu.with_memory_space_constraint(a, pltpu.HBM)
    res = pl.pallas_call(
        body, name=name,
        out_shape=(sem, sem, *[pltpu.HBM(a.shape, a.dtype) for a in arrs], *[pltpu.HBM(l.shape, l.dtype) for l in lands],
                   jax.ShapeDtypeStruct((8, LANES), F32)),
        in_specs=[HBM_SPEC] * (2 * n) + [pl.BlockSpec(memory_space=pl.ANY)],
        out_specs=(SEM_SPEC, SEM_SPEC, *[HBM_SPEC] * (2 * n), pl.BlockSpec(memory_space=pltpu.VMEM)),
        input_output_aliases={i: 2 + i for i in range(2 * n)},
        compiler_params=pltpu.CompilerParams(has_side_effects=pltpu.SideEffectType.DATAFLOW_SIDE_EFFECTING),
    )(*[hbm(a) for a in arrs], *[hbm(l) for l in lands], after)
    return dict(sems=res[0:2], arrs=res[2:2 + n], lands=res[2 + n:2 + 2 * n], token=res[2 + 2 * n], gather=gather)


def _exchange_wait(st, after, name):
    n = len(st["arrs"])
    gather = st["gather"]

    def body(*refs):
        a_refs, l_refs = refs[:n], refs[n:2 * n]
        send_sems, recv_sems = refs[2 * n], refs[2 * n + 1]
        x, y, c = _mesh_pos()
        for a in range(n):
            for k, (peer, pid) in enumerate(_peers(x, y, c)):
                cp = pltpu.make_async_remote_copy(
                    src_ref=a_refs[a] if gather else a_refs[a].at[pid], dst_ref=l_refs[a].at[pid],
                    send_sem=send_sems.at[a * (N_DEV - 1) + k], recv_sem=recv_sems.at[a * (N_DEV - 1) + k],
                    device_id=peer, device_id_type=MESH_ID)
                cp.wait_send()
                cp.wait_recv()

    both = list(st["arrs"]) + list(st["lands"])
    res = pl.pallas_call(
        body, name=name, out_shape=[pltpu.HBM(a.shape, a.dtype) for a in both],
        in_specs=[HBM_SPEC] * (2 * n) + [SEM_SPEC, SEM_SPEC, pl.BlockSpec(memory_space=pl.ANY)],
        out_specs=[HBM_SPEC] * (2 * n), input_output_aliases={i: i for i in range(2 * n)},
        compiler_params=pltpu.CompilerParams(has_side_effects=pltpu.SideEffectType.DATAFLOW_SIDE_EFFECTING),
    )(*both, *st["sems"], after)
    return res[n:2 * n]


def _tie(x, token, name):
    def body(x_ref, t_ref, o_ref):
        o_ref[...] = x_ref[...]

    vm = pl.BlockSpec(memory_space=pltpu.VMEM)
    return pl.pallas_call(
        body, name=name, in_specs=[vm, pl.BlockSpec(memory_space=pl.ANY)], out_specs=vm,
        out_shape=jax.ShapeDtypeStruct(x.shape, x.dtype),
    )(x, token)


def _all_gather_hbm(shards):
    n = len(shards)

    def body(*refs):
        p_refs, out_refs = refs[:n], refs[n:2 * n]
        send_sems, recv_sems, local_sems = refs[2 * n:]
        x, y, c = _mesh_pos()
        me, sibling = (x, y, c), (x, y, 1 - c)
        chips = [(1 - x, y), (x, 1 - y), (1 - x, 1 - y)]

        def copy(a, k, block, to, own=False):
            slot = out_refs[a].at[4 * block[0] + 2 * block[1] + block[2]]
            return pltpu.make_async_remote_copy(
                src_ref=p_refs[a] if own else slot, dst_ref=slot,
                send_sem=send_sems.at[a, k], recv_sem=recv_sems.at[a, k], device_id=to, device_id_type=MESH_ID)

        mine = [pltpu.make_async_copy(p_refs[a], out_refs[a].at[4 * x + 2 * y + c], local_sems.at[a]) for a in range(n)]
        for cp in mine:
            cp.start()
        first = []
        for a in range(n):
            first.append(copy(a, 0, me, sibling, own=True))
            first += [copy(a, 1 + j, me, (*chip, c), own=True) for j, chip in enumerate(chips)]
        for cp in first:
            cp.start()
        passed = []
        for j, chip in enumerate(chips):
            for a in range(n):
                copy(a, 1 + j, (*chip, c), me).wait_recv()
                passed.append(copy(a, 4 + j, (*chip, c), sibling))
                passed[-1].start()
        for a in range(n):
            copy(a, 0, sibling, me).wait_recv()
            for j, chip in enumerate(chips):
                copy(a, 4 + j, (*chip, 1 - c), me).wait_recv()
        for cp in first + passed:
            cp.wait_send()
        for cp in mine:
            cp.wait()

    hbm = pl.BlockSpec(memory_space=pl.ANY)
    return pl.pallas_call(
        body, name="gather_weights", in_specs=[hbm] * n, out_specs=[hbm] * n,
        out_shape=[jax.ShapeDtypeStruct((N_DEV,) + s.shape, s.dtype) for s in shards],
        scratch_shapes=[pltpu.SemaphoreType.DMA((n, N_DEV - 1)), pltpu.SemaphoreType.DMA((n, N_DEV - 1)),
                        pltpu.SemaphoreType.DMA((n,))],
    )(*shards)


def _adamw(w, g, m, v):
    m2 = ADAM_B1 * m + (1.0 - ADAM_B1) * g
    v2 = ADAM_B2 * v + (1.0 - ADAM_B2) * (g * g)
    m_hat = m2 / (1.0 - ADAM_B1 ** ADAM_STEP)
    v_hat = v2 / (1.0 - ADAM_B2 ** ADAM_STEP)
    delta = -ADAM_LR * (m_hat / (jnp.sqrt(v_hat) + ADAM_EPS) + ADAM_WD * w)
    return delta, m2, v2


def _mod_partial(cmat, w_shard, b_shard):
    def body(c_ref, w_ref, b_ref, o_ref):
        o_ref[...] = _dot(c_ref[...].astype(BF16), w_ref[...].astype(BF16)) + b_ref[...]

    return pl.pallas_call(
        body, name="mod_partial", out_shape=jax.ShapeDtypeStruct((N_DEV, w_shard.shape[1]), F32),
        compiler_params=_params(),
    )(cmat, w_shard, b_shard)


def _adamw_w_ada(cmat, dmod_cols, w, m, v):
    def body(c_ref, d_ref, w_ref, m_ref, v_ref, g_ref, dl_ref, m2_ref, v2_ref):
        g = _dot_tn(c_ref[...].astype(BF16), d_ref[...].astype(BF16))
        g_ref[...] = g
        dl_ref[...], m2_ref[...], v2_ref[...] = _adamw(w_ref[...], g, m_ref[...], v_ref[...])

    return pl.pallas_call(
        body, name="adamw_w_ada", out_shape=[jax.ShapeDtypeStruct(w.shape, F32)] * 4,
        compiler_params=_params(),
    )(cmat, dmod_cols, w, m, v)


SMALL_ROWS = 16
SMALL_AT = {"b_ada": (0, 6, 0, D_MODEL), "g_pre_mix": (6, 1, 0, D_MODEL), "g_post_mix": (7, 1, 0, D_MODEL),
            "g_pre_ffn": (8, 1, 0, D_MODEL), "g_post_ffn": (9, 1, 0, D_MODEL), "conv_b": (10, 1, 0, D_MODEL),
            "mlstm_norm_w": (11, 1, 0, 512), "b_if": (11, 1, 512, LANES), "attn_sinks": (11, 1, 640, LANES)}


def _small_table(part):
    tail = jnp.concatenate([part["mlstm_norm_w"], jnp.pad(part["b_if"], (0, LANES - 8)),
                            jnp.pad(part["attn_sinks"], (0, LANES - 8)), jnp.zeros((256,), F32)])
    return jnp.concatenate([part["b_ada"], part["gains"], part["conv_b"][None], tail[None],
                            jnp.zeros((SMALL_ROWS - 12, D_MODEL), F32)], axis=0)


def _adamw_small(gathered, wmv):
    names = list(SMALL_AT)

    def body(*refs):
        g_ref, ins, outs = refs[0], refs[1:1 + 3 * len(names)], refs[1 + 3 * len(names):]
        g = g_ref[0]
        for k in range(1, N_DEV):
            g = g + g_ref[k]
        for i, n in enumerate(names):
            r0, rows, l0, lanes = SMALL_AT[n]
            gi = jnp.concatenate([g[r:r + 1, l0:l0 + lanes] for r in range(r0, r0 + rows)], axis=1)
            w_ref, m_ref, v_ref = ins[3 * i:3 * i + 3]
            go, dl, m2, v2 = outs[4 * i:4 * i + 4]
            go[...] = gi
            dl[...], m2[...], v2[...] = _adamw(w_ref[...], gi, m_ref[...], v_ref[...])

    flat = [a for n in names for a in wmv[n]]
    res = pl.pallas_call(
        body, name="adamw_small",
        out_shape=[jax.ShapeDtypeStruct(wmv[n][0].shape, F32) for n in names for _ in range(4)],
        compiler_params=_params(),
    )(gathered, *flat)
    return {n: res[4 * i:4 * i + 4] for i, n in enumerate(names)}


def _row_tile(rows):
    return rows // 4 if rows >= 512 else rows


def _sum_partials(r_ref):
    g = r_ref[0].astype(F32)
    for k in range(1, N_DEV):
        g = g + r_ref[k].astype(F32)
    return g


def _adamw_sum(recv, w, m, v, name):
    r, cdim = w.shape
    tr = _row_tile(r)

    def body(r_ref, w_ref, m_ref, v_ref, g_ref, dl_ref, m2_ref, v2_ref):
        g = _sum_partials(r_ref)
        g_ref[...] = g
        dl_ref[...], m2_ref[...], v2_ref[...] = _adamw(w_ref[...], g, m_ref[...], v_ref[...])

    row = pl.BlockSpec((tr, cdim), lambda i: (i, 0))
    return pl.pallas_call(
        body, name=name, grid=(r // tr,),
        in_specs=[pl.BlockSpec((N_DEV, tr, cdim), lambda i: (0, i, 0)), row, row, row],
        out_specs=[row] * 4, out_shape=[jax.ShapeDtypeStruct((r, cdim), F32)] * 4,
        compiler_params=_params(("parallel",)),
    )(recv, w, m, v)


def _sum8(recv, name):
    _, r, cdim = recv.shape
    tr = _row_tile(r)

    def body(r_ref, g_ref):
        g_ref[...] = _sum_partials(r_ref)

    return pl.pallas_call(
        body, name=name, grid=(r // tr,),
        in_specs=[pl.BlockSpec((N_DEV, tr, cdim), lambda i: (0, i, 0))],
        out_specs=pl.BlockSpec((tr, cdim), lambda i: (i, 0)), out_shape=jax.ShapeDtypeStruct((r, cdim), F32),
        compiler_params=_params(("parallel",)),
    )(recv)


def _eye(n):
    return (lax.broadcasted_iota(jnp.int32, (n, n), 0) == lax.broadcasted_iota(jnp.int32, (n, n), 1)).astype(BF16)


def _transpose_to_bf16(w, name):
    r, c = w.shape

    def body(w_ref, o_ref):
        o_ref[...] = _dot_tn(w_ref[...].astype(BF16), _eye(r)).astype(BF16)

    return pl.pallas_call(body, name=name, out_shape=jax.ShapeDtypeStruct((c, r), BF16), compiler_params=_params())(w)


def _sum8_transposed(recv, name):
    _, r, c = recv.shape

    def body(r_ref, g_ref):
        eye = _eye(r)
        hi, mid, lo = _split3(_sum_partials(r_ref))
        g_ref[...] = _dot_tn(hi, eye) + _dot_tn(mid, eye) + _dot_tn(lo, eye)

    return pl.pallas_call(body, name=name, out_shape=jax.ShapeDtypeStruct((c, r), F32), compiler_params=_params())(recv)


def _adamw_plain(g, w, m, v, name):
    r, cdim = w.shape
    tr = _row_tile(r)

    def body(g_ref, w_ref, m_ref, v_ref, dl_ref, m2_ref, v2_ref):
        dl_ref[...], m2_ref[...], v2_ref[...] = _adamw(w_ref[...], g_ref[...], m_ref[...], v_ref[...])

    row = pl.BlockSpec((tr, cdim), lambda i: (i, 0))
    return pl.pallas_call(
        body, name=name, grid=(r // tr,), in_specs=[row] * 4, out_specs=[row] * 3,
        out_shape=[jax.ShapeDtypeStruct((r, cdim), F32)] * 3,
        compiler_params=_params(("parallel",)),
    )(g, w, m, v)


IN_SHARD = 609
IN_SHARD_PAD = 640
IF_AT = A_W + M_W


def _regrouped(u):
    return u if u < IF_AT + 8 else u + (IF_W - 8)


def _selection(k, rows, row0, transpose):
    shape = (rows, IN_SHARD_PAD) if transpose else (IN_SHARD_PAD, rows)
    l = lax.broadcasted_iota(jnp.int32, shape, 1 if transpose else 0)
    r = lax.broadcasted_iota(jnp.int32, shape, 0 if transpose else 1) + row0
    u = l + IN_SHARD * k
    ru = u + jnp.where(u >= IF_AT + 8, IF_W - 8, 0)
    return ((ru == r) & (l < IN_SHARD)).astype(BF16)


def _regroup_w_in(g):
    def body(g_ref, o_ref):
        for cb in range(CAT_W // LANES):
            r0 = cb * LANES
            acc = jnp.zeros((D_MODEL, LANES), F32)
            for k in range(N_DEV):
                lo, hi = _regrouped(IN_SHARD * k), _regrouped(IN_SHARD * k + IN_SHARD - 1)
                if hi >= r0 and lo < r0 + LANES:
                    acc = acc + _dot(g_ref[k], _selection(k, LANES, r0, False))
            o_ref[:, r0:r0 + LANES] = acc.astype(BF16)

    return pl.pallas_call(
        body, name="regroup_w_in", out_shape=jax.ShapeDtypeStruct((D_MODEL, CAT_W), BF16),
        compiler_params=_params(),
    )(g)


def _ungroup_w_in(g_cat):
    def body(g_ref, o_ref):
        for k in range(N_DEV):
            lo, hi = _regrouped(IN_SHARD * k), _regrouped(IN_SHARD * k + IN_SHARD - 1)
            w0, w1 = lo // LANES * LANES, (hi // LANES + 1) * LANES
            o_ref[k] = _dot(g_ref[:, w0:w1], _selection(k, w1 - w0, w0, True)).astype(BF16)

    return pl.pallas_call(
        body, name="ungroup_w_in", out_shape=jax.ShapeDtypeStruct((N_DEV, D_MODEL, IN_SHARD_PAD), BF16),
        compiler_params=_params(),
    )(g_cat)


WEIGHT_NAMES = ("w_ada", "b_ada", "g_pre_mix", "g_post_mix", "w_in", "b_if", "conv_w", "conv_b", "attn_sinks",
                "mlstm_norm_w", "w_branch_attn", "w_branch_mlstm", "w_out", "g_pre_ffn", "g_post_ffn",
                "w_ffn_gate", "w_ffn_up", "w_ffn_down")


def kernel(x, c, positions, w_ada, b_ada, g_pre_mix, g_post_mix, w_in, b_if, conv_w, conv_b, attn_sinks, mlstm_norm_w, w_branch_attn, w_branch_mlstm, w_out, g_pre_ffn, g_post_ffn, w_ffn_gate, w_ffn_up, w_ffn_down, loss_target, m_w_ada, m_b_ada, m_g_pre_mix, m_g_post_mix, m_w_in, m_b_if, m_conv_w, m_conv_b, m_attn_sinks, m_mlstm_norm_w, m_w_branch_attn, m_w_branch_mlstm, m_w_out, m_g_pre_ffn, m_g_post_ffn, m_w_ffn_gate, m_w_ffn_up, m_w_ffn_down, v_w_ada, v_b_ada, v_g_pre_mix, v_g_post_mix, v_w_in, v_b_if, v_conv_w, v_conv_b, v_attn_sinks, v_mlstm_norm_w, v_w_branch_attn, v_w_branch_mlstm, v_w_out, v_g_pre_ffn, v_g_post_ffn, v_w_ffn_gate, v_w_ffn_up, v_w_ffn_down):
    given = dict(locals())
    W = {n: given[n][0] for n in WEIGHT_NAMES}
    M = {n: given["m_" + n][0] for n in WEIGHT_NAMES}
    V = {n: given["v_" + n][0] for n in WEIGHT_NAMES}
    me = 4 * lax.axis_index("x") + 2 * lax.axis_index("y") + lax.axis_index("c")

    ff_sh = D_FF // N_DEV
    g_in, g_conv, cg = _all_gather_hbm([jnp.pad(W["w_in"], ((0, 0), (0, IN_SHARD_PAD - IN_SHARD))).astype(BF16),
                                        jnp.pad(W["conv_w"], ((0, 4), (0, 0))), c.reshape(8, D_MODEL // 8)])

    cmat = cg.reshape(N_DEV, D_MODEL)
    ada_w = D_MODEL * 6 // N_DEV
    b_cols = lax.dynamic_slice(W["b_ada"], (me * ada_w,), (ada_w,)).reshape(1, ada_w)
    mod_part = _mod_partial(cmat, W["w_ada"], b_cols)
    mod_recv = _small_exchange(jnp.broadcast_to(mod_part[:, None, :], (N_DEV, 8, ada_w)), False, "scatter_mod")
    mod = mod_recv[:, 0, :].reshape(6, D_MODEL)

    st_b = _exchange_start([W["w_branch_attn"].astype(BF16), W["w_branch_mlstm"].astype(BF16), W["w_out"].astype(BF16),
                            _transpose_to_bf16(W["w_ffn_gate"], "transpose_w_ffn_gate"),
                            _transpose_to_bf16(W["w_ffn_up"], "transpose_w_ffn_up"), W["w_ffn_down"].astype(BF16)],
                           True, mod_recv, "gather_rest_start")
    cols = lambda g: g.transpose(1, 0, 2).reshape(g.shape[1], N_DEV * g.shape[2])
    gains = jnp.stack([W["g_pre_mix"], W["g_post_mix"], W["g_pre_ffn"], W["g_post_ffn"]])
    xs, tgt = x[0], loss_target[0]
    t = _tables(mod, gains, cols(g_conv)[0:4], W["conv_b"], W["b_if"], W["mlstm_norm_w"], positions[0])
    vecs = t["vecs"]
    t["vecs"] = _tie(vecs, st_b["token"], "tie_fwd")
    w_cat = _regroup_w_in(g_in)
    a = _mixer_fwd(xs, t, W["attn_sinks"], w_cat)
    g_ba, g_bm, g_out, g_gate, g_up, g_down = _exchange_wait(st_b, a["ym"], "gather_rest_wait")
    w_ba, w_bm, w_out = cols(g_ba), cols(g_bm), g_out.reshape(D_MODEL, D_MODEL)
    b = _ffn_part(xs, tgt, t, a, w_ba, w_bm, w_out, g_gate.reshape(D_FF, D_MODEL), g_up.reshape(D_FF, D_MODEL),
                  g_down.reshape(D_FF, D_MODEL))

    st_f = _exchange_start([b["g_w_gate"].reshape(N_DEV, ff_sh, D_MODEL), b["g_w_up"].reshape(N_DEV, ff_sh, D_MODEL),
                            b["g_w_down"].reshape(N_DEV, ff_sh, D_MODEL)], False, b["dx1"], "scatter_ffn_start")
    t["vecs"] = _tie(vecs, st_f["token"], "tie_bwd")
    cm = _mixer_bwd(b["dx1"], t, a, b, W["attn_sinks"], w_ba, w_bm, w_out)
    pieces = lambda g, n: g.reshape(g.shape[0], N_DEV, n).transpose(1, 0, 2)
    st_m = _exchange_start([_ungroup_w_in(cm["g_w_cat"]), pieces(cm["g_w_ba"], 128), pieces(cm["g_w_bm"], 128),
                            cm["g_w_out"].reshape(N_DEV, D_MODEL // N_DEV, D_MODEL),
                            jnp.pad(pieces(cm["dcw"][0:4], 128), ((0, 0), (0, 4), (0, 0)))], False, cm["dproj"],
                           "scatter_mixer_start")
    r_gate, r_up, r_down = _exchange_wait(st_f, st_m["token"], "scatter_ffn_wait")
    grad_x, acc_p = _pre_bwd(cm["dproj"], xs, b["dx1"], _tie(vecs, st_m["token"], "tie_pre_bwd"), w_cat)
    small = _small_grads(acc_p, b, cm)
    loss = b["loss"]

    big_out = [{} for _ in range(4)]

    def put(n, res):
        for k in range(4):
            big_out[k][n] = res[k][None]

    put("w_ffn_down", _adamw_sum(r_down, W["w_ffn_down"], M["w_ffn_down"], V["w_ffn_down"], "adamw_w_ffn_down"))
    for n, g in (("w_ffn_gate", _sum8_transposed(r_gate, "sum_w_ffn_gate")),
                 ("w_ffn_up", _sum8_transposed(r_up, "sum_w_ffn_up"))):
        put(n, [g] + list(_adamw_plain(g, W[n], M[n], V[n], "adamw_" + n)))

    sg = _small_exchange(_small_table({"b_ada": small["dmod"], "gains": small["dgains"], "conv_b": small["dconv_b"],
                                       "mlstm_norm_w": small["dnorm_w"], "b_if": small["db_if"],
                                       "attn_sinks": small["dsinks"]}), True, "gather_small")
    as_row = lambda a, n: jnp.pad(a, (0, SMALL_AT[n][3] * SMALL_AT[n][1] - a.shape[0]))[None]
    small_res = _adamw_small(sg, {n: [as_row(d[n], n) for d in (W, M, V)] for n in SMALL_AT})
    small_out = [{n: small_res[n][k][:, 0:W[n].shape[0]] for n in SMALL_AT} for k in range(4)]
    dmod_cols = lax.dynamic_slice(sg[:, 0:6, :].reshape(N_DEV, 6 * D_MODEL), (0, me * ada_w), (N_DEV, ada_w))
    ada_out = _adamw_w_ada(cmat, dmod_cols, W["w_ada"], M["w_ada"], V["w_ada"])

    r_in, r_ba, r_bm, r_out, r_conv = _exchange_wait(st_m, grad_x, "scatter_mixer_wait")
    for n, r in (("w_branch_attn", r_ba), ("w_branch_mlstm", r_bm), ("w_out", r_out)):
        put(n, _adamw_sum(r, W[n], M[n], V[n], "adamw_" + n))
    pad4 = lambda v: jnp.pad(v, ((0, 4), (0, 0)))
    put("conv_w", [o[0:4] for o in _adamw_sum(r_conv, pad4(W["conv_w"]), pad4(M["conv_w"]), pad4(V["conv_w"]),
                                                 "adamw_conv_w")])
    g = _sum8(r_in, "sum_w_in")[:, 0:IN_SHARD]
    put("w_in", [g] + list(_adamw_plain(g, W["w_in"], M["w_in"], V["w_in"], "adamw_w_in")))

    total = lax.psum(loss, ("x", "y", "c"))
    outs = [total, grad_x[None]]
    for k in range(4):
        for n in WEIGHT_NAMES:
            if n == "w_ada":
                outs.append(ada_out[k][None])
            elif n in big_out[k]:
                outs.append(big_out[k][n])
            else:
                outs.append(small_out[k][n])
    return tuple(outs)
```

```python
import functools

import jax
import jax.numpy as jnp
import numpy as np
from jax import lax
from jax.experimental import pallas as pl
from jax.experimental.pallas import tpu as pltpu

F32 = jnp.float32
BF16 = jnp.bfloat16

N_DEV = 8
D_MODEL = 1024
D_FF = 2816
N_Q_HEADS = 8
HEAD_DIM = 64
ATTN_BLOCK = 128
ROPE_THETA = 10000.0
MLSTM_HEADS = 4
MLSTM_HEAD_DIM = 128
MLSTM_CHUNK = 128
NORM_EPS = 1e-6
ADAM_LR = 0.001
ADAM_B1 = 0.9
ADAM_B2 = 0.999
ADAM_EPS = 1e-08
ADAM_WD = 0.01
ADAM_STEP = 10

ROW_TILE = 256
WIDE_TILE = 512
LANES = 128
NEG = -1e30
VMEM_LIMIT = 56 * 1024 * 1024

A_W = 768
M_W = 2048
IF_W = 128
G_W = 2048
CAT_W = A_W + M_W + IF_W + G_W

R_SHIFT_M, R_SCALE_M, R_GATE_M, R_SHIFT_F, R_SCALE_F, R_GATE_F = 0, 1, 2, 3, 4, 5
R_G_PRE_MIX, R_G_POST_MIX, R_G_PRE_FFN, R_G_POST_FFN = 6, 7, 8, 9


def _dot(a, b):
    return jnp.dot(a, b, preferred_element_type=F32)


def _dot_nt(a, b):
    return lax.dot_general(a, b, (((1,), (1,)), ((), ())), preferred_element_type=F32)


def _dot_tn(a, b):
    return lax.dot_general(a, b, (((0,), (0,)), ((), ())), preferred_element_type=F32)


def _recip(x):
    return 1.0 / x


def _sigmoid(x):
    return _recip(1.0 + jnp.exp(-x))


def _colsum(x):
    return jnp.sum(x, axis=0, keepdims=True)


def _rowmean(x):
    return jnp.mean(x, axis=-1, keepdims=True)


def _params(sem=None, vmem=VMEM_LIMIT):
    kw = dict(vmem_limit_bytes=vmem)
    if sem is not None:
        kw["dimension_semantics"] = sem
    return pltpu.CompilerParams(**kw)


def _full(shape):
    nd = len(shape)
    return pl.BlockSpec(shape, lambda *_: (0,) * nd)


def _pre_proj(x, vecs, w_cat):
    S = x.shape[0]
    tm = WIDE_TILE

    def body(x_ref, v_ref, w_ref, h_ref, pa_ref, pm_ref, pif_ref, pg_ref):
        xv = x_ref[...]
        r = lax.rsqrt(_rowmean(xv * xv) + NORM_EPS)
        h = (xv * r * v_ref[R_G_PRE_MIX:R_G_PRE_MIX + 1, :]) * (1.0 + v_ref[R_SCALE_M:R_SCALE_M + 1, :]) \
            + v_ref[R_SHIFT_M:R_SHIFT_M + 1, :]
        hb = h.astype(BF16)
        h_ref[...] = hb
        pa_ref[...] = _dot(hb, w_ref[:, 0:A_W])
        pm_ref[...] = _dot(hb, w_ref[:, A_W:A_W + M_W])
        pif_ref[...] = _dot(hb, w_ref[:, A_W + M_W:A_W + M_W + IF_W])
        pg_ref[...] = _dot(hb, w_ref[:, A_W + M_W + IF_W:CAT_W]).astype(BF16)

    row = lambda w: pl.BlockSpec((tm, w), lambda i: (i, 0))
    return pl.pallas_call(
        body, name="pre_proj", grid=(S // tm,),
        in_specs=[row(D_MODEL), _full(vecs.shape), _full(w_cat.shape)],
        out_specs=[row(D_MODEL), row(A_W), row(M_W), row(IF_W), row(G_W)],
        out_shape=[jax.ShapeDtypeStruct((S, D_MODEL), BF16), jax.ShapeDtypeStruct((S, A_W), F32),
                   jax.ShapeDtypeStruct((S, M_W), F32), jax.ShapeDtypeStruct((S, IF_W), F32),
                   jax.ShapeDtypeStruct((S, G_W), BF16)],
        compiler_params=_params(("parallel",)),
    )(x, vecs, w_cat)


def _mix_fwd(x, ya, ym, pg, vecs, w_ba, w_bm, w_out):
    S = x.shape[0]
    tm = WIDE_TILE

    def body(x_ref, ya_ref, ym_ref, pg_ref, v_ref, wba_ref, wbm_ref, wout_ref,
             x1_ref, merged_ref, mix_ref, pa_ref, pb_ref):
        pa = _dot(ya_ref[...], wba_ref[...])
        pb = _dot(ym_ref[...], wbm_ref[...])
        merged = _sigmoid(pg_ref[:, 0:D_MODEL].astype(F32)) * pa + _sigmoid(pg_ref[:, D_MODEL:G_W].astype(F32)) * pb
        mb = merged.astype(BF16)
        mix = _dot(mb, wout_ref[...])
        r = lax.rsqrt(_rowmean(mix * mix) + NORM_EPS)
        x1_ref[...] = x_ref[...] + v_ref[R_GATE_M:R_GATE_M + 1, :] * (mix * r * v_ref[R_G_POST_MIX:R_G_POST_MIX + 1, :])
        merged_ref[...] = mb
        mix_ref[...] = mix
        pa_ref[...] = pa.astype(BF16)
        pb_ref[...] = pb.astype(BF16)

    row = lambda w: pl.BlockSpec((tm, w), lambda i: (i, 0))
    sd = lambda w, dt: jax.ShapeDtypeStruct((S, w), dt)
    return pl.pallas_call(
        body, name="mix_fwd", grid=(S // tm,),
        in_specs=[row(D_MODEL), row(512), row(512), row(G_W), _full(vecs.shape), _full(w_ba.shape),
                  _full(w_bm.shape), _full(w_out.shape)],
        out_specs=[row(D_MODEL)] * 5,
        out_shape=[sd(D_MODEL, F32), sd(D_MODEL, BF16), sd(D_MODEL, F32), sd(D_MODEL, BF16), sd(D_MODEL, BF16)],
        compiler_params=_params(("parallel",)),
    )(x, ya, ym, pg, vecs, w_ba, w_bm, w_out)


def _ffn_fwd_bwd(x1, tgt, vecs, w_gate, w_up, w_down):
    S = x1.shape[0]
    tm = ROW_TILE

    def body(x1_ref, tgt_ref, v_ref, wg_hbm, wu_hbm, wd_hbm,
             dx1_ref, h2_ref, hid_ref, da_ref, du_ref, dff_ref, acc_ref, loss_ref,
             wg, wu, wd, sem):
        i = pl.program_id(0)

        @pl.when(i == 0)
        def _():
            cps = [pltpu.make_async_copy(wg_hbm, wg, sem.at[0]), pltpu.make_async_copy(wu_hbm, wu, sem.at[1]),
                   pltpu.make_async_copy(wd_hbm, wd, sem.at[2])]
            for cp in cps:
                cp.start()
            for cp in cps:
                cp.wait()
            acc_ref[...] = jnp.zeros_like(acc_ref)
            loss_ref[...] = jnp.zeros_like(loss_ref)

        vrow = lambda r: v_ref[r:r + 1, :]
        x1v = x1_ref[...]
        r3 = lax.rsqrt(_rowmean(x1v * x1v) + NORM_EPS)
        x1hat = x1v * r3
        xn3 = x1hat * vrow(R_G_PRE_FFN)
        h2b = (xn3 * (1.0 + vrow(R_SCALE_F)) + vrow(R_SHIFT_F)).astype(BF16)
        h2_ref[...] = h2b
        a = _dot_nt(h2b, wg[...])
        u = _dot_nt(h2b, wu[...])
        sg = _sigmoid(a)
        sil = a * sg
        hidb = (sil * u).astype(BF16)
        hid_ref[...] = hidb
        ff = _dot(hidb, wd[...])
        r4 = lax.rsqrt(_rowmean(ff * ff) + NORM_EPS)
        ffhat = ff * r4
        n4 = ffhat * vrow(R_G_POST_FFN)
        err = x1v + vrow(R_GATE_F) * n4 - tgt_ref[...]
        loss_ref[...] += jnp.sum(err * err) * (0.5 / D_MODEL)
        dy = err * (1.0 / D_MODEL)
        acc_ref[0:1, :] += _colsum(dy * n4)
        dn4 = dy * vrow(R_GATE_F)
        acc_ref[1:2, :] += _colsum(dn4 * ffhat)
        dffhat = dn4 * vrow(R_G_POST_FFN)
        dffb = (r4 * (dffhat - ffhat * _rowmean(dffhat * ffhat))).astype(BF16)
        dff_ref[...] = dffb
        dhid = _dot_nt(dffb, wd[...])
        dub = (dhid * sil).astype(BF16)
        dab = (dhid * u * (sg * (1.0 + a * (1.0 - sg)))).astype(BF16)
        da_ref[...] = dab
        du_ref[...] = dub
        dh2 = _dot(dab, wg[...]) + _dot(dub, wu[...])
        acc_ref[2:3, :] += _colsum(dh2 * xn3)
        acc_ref[3:4, :] += _colsum(dh2)
        dxn3 = dh2 * (1.0 + vrow(R_SCALE_F))
        acc_ref[4:5, :] += _colsum(dxn3 * x1hat)
        dx1hat = dxn3 * vrow(R_G_PRE_FFN)
        dx1_ref[...] = dy + r3 * (dx1hat - x1hat * _rowmean(dx1hat * x1hat))

    row = lambda w: pl.BlockSpec((tm, w), lambda i: (i, 0))
    sd = lambda w, dt: jax.ShapeDtypeStruct((S, w), dt)
    anyspec = pl.BlockSpec(memory_space=pl.ANY)
    return pl.pallas_call(
        body, name="ffn_fwd_bwd", grid=(S // tm,),
        in_specs=[row(D_MODEL), row(D_MODEL), _full(vecs.shape), anyspec, anyspec, anyspec],
        out_specs=[row(D_MODEL), row(D_MODEL), row(D_FF), row(D_FF), row(D_FF), row(D_MODEL),
                   _full((8, D_MODEL)), _full((8, LANES))],
        out_shape=[sd(D_MODEL, F32), sd(D_MODEL, BF16), sd(D_FF, BF16), sd(D_FF, BF16), sd(D_FF, BF16),
                   sd(D_MODEL, BF16), jax.ShapeDtypeStruct((8, D_MODEL), F32), jax.ShapeDtypeStruct((8, LANES), F32)],
        scratch_shapes=[pltpu.VMEM(w_gate.shape, BF16), pltpu.VMEM(w_up.shape, BF16), pltpu.VMEM(w_down.shape, BF16),
                        pltpu.SemaphoreType.DMA((3,))],
        compiler_params=_params(("arbitrary",)),
    )(x1, tgt, vecs, w_gate, w_up, w_down)


def _mix_bwd(dx1, mix, pa, pb, pg, vecs, w_ba, w_bm, w_out):
    S = dx1.shape[0]
    tm = WIDE_TILE

    def body(dx1_ref, mix_ref, pa_ref, pb_ref, pg_ref, v_ref, wba_ref, wbm_ref, wout_ref,
             dmix_ref, dpa_ref, dpb_ref, dg_ref, dya_ref, dym_ref, acc_ref):
        i = pl.program_id(0)

        @pl.when(i == 0)
        def _():
            acc_ref[...] = jnp.zeros_like(acc_ref)

        vrow = lambda r: v_ref[r:r + 1, :]
        dx1v = dx1_ref[...]
        mix = mix_ref[...]
        r2 = lax.rsqrt(_rowmean(mix * mix) + NORM_EPS)
        mixhat = mix * r2
        acc_ref[0:1, :] += _colsum(dx1v * (mixhat * vrow(R_G_POST_MIX)))
        dn2 = dx1v * vrow(R_GATE_M)
        acc_ref[1:2, :] += _colsum(dn2 * mixhat)
        dmixhat = dn2 * vrow(R_G_POST_MIX)
        dmixb = (r2 * (dmixhat - mixhat * _rowmean(dmixhat * mixhat))).astype(BF16)
        dmix_ref[...] = dmixb
        dmerged = _dot_nt(dmixb, wout_ref[...])
        sa = _sigmoid(pg_ref[:, 0:D_MODEL].astype(F32))
        sm = _sigmoid(pg_ref[:, D_MODEL:G_W].astype(F32))
        dpab = (dmerged * sa).astype(BF16)
        dpbb = (dmerged * sm).astype(BF16)
        dpa_ref[...] = dpab
        dpb_ref[...] = dpbb
        dg_ref[:, 0:D_MODEL] = (dmerged * pa_ref[...].astype(F32) * (sa * (1.0 - sa))).astype(BF16)
        dg_ref[:, D_MODEL:G_W] = (dmerged * pb_ref[...].astype(F32) * (sm * (1.0 - sm))).astype(BF16)
        dya_ref[...] = _dot_nt(dpab, wba_ref[...])
        dym_ref[...] = _dot_nt(dpbb, wbm_ref[...])

    row = lambda w: pl.BlockSpec((tm, w), lambda i: (i, 0))
    sd = lambda w, dt: jax.ShapeDtypeStruct((S, w), dt)
    return pl.pallas_call(
        body, name="mix_bwd", grid=(S // tm,),
        in_specs=[row(D_MODEL), row(D_MODEL), row(D_MODEL), row(D_MODEL), row(G_W), _full(vecs.shape),
                  _full(w_ba.shape), _full(w_bm.shape), _full(w_out.shape)],
        out_specs=[row(D_MODEL), row(D_MODEL), row(D_MODEL), row(G_W), row(512), row(512), _full((8, D_MODEL))],
        out_shape=[sd(D_MODEL, BF16), sd(D_MODEL, BF16), sd(D_MODEL, BF16), sd(G_W, BF16), sd(512, F32), sd(512, F32),
                   jax.ShapeDtypeStruct((8, D_MODEL), F32)],
        compiler_params=_params(("arbitrary",)),
    )(dx1, mix, pa, pb, pg, vecs, w_ba, w_bm, w_out)


def _pre_bwd(pieces, x, dx1, vecs, w_cat):
    S = x.shape[0]
    tm = WIDE_TILE
    n = len(pieces)
    starts = [sum(p.shape[1] for p in pieces[:k]) for k in range(n + 1)]

    def body(*refs):
        p_refs = refs[:n]
        x_ref, dx1_ref, v_ref, w_ref, dx_ref, acc_ref = refs[n:]
        i = pl.program_id(0)

        @pl.when(i == 0)
        def _():
            acc_ref[...] = jnp.zeros_like(acc_ref)

        vrow = lambda r: v_ref[r:r + 1, :]
        dh = _dot_nt(p_refs[0][...], w_ref[:, starts[0]:starts[1]])
        for k in range(1, n):
            dh = dh + _dot_nt(p_refs[k][...], w_ref[:, starts[k]:starts[k + 1]])
        xv = x_ref[...]
        r1 = lax.rsqrt(_rowmean(xv * xv) + NORM_EPS)
        xhat = xv * r1
        acc_ref[0:1, :] += _colsum(dh * (xhat * vrow(R_G_PRE_MIX)))
        acc_ref[1:2, :] += _colsum(dh)
        dxn = dh * (1.0 + vrow(R_SCALE_M))
        acc_ref[2:3, :] += _colsum(dxn * xhat)
        dxhat = dxn * vrow(R_G_PRE_MIX)
        dx_ref[...] = dx1_ref[...] + r1 * (dxhat - xhat * _rowmean(dxhat * xhat))

    row = lambda w: pl.BlockSpec((tm, w), lambda i: (i, 0))
    return pl.pallas_call(
        body, name="pre_bwd", grid=(S // tm,),
        in_specs=[row(p.shape[1]) for p in pieces] + [row(D_MODEL), row(D_MODEL), _full(vecs.shape), _full(w_cat.shape)],
        out_specs=[row(D_MODEL), _full((8, D_MODEL))],
        out_shape=[jax.ShapeDtypeStruct((S, D_MODEL), F32), jax.ShapeDtypeStruct((8, D_MODEL), F32)],
        compiler_params=_params(("arbitrary",)),
    )(*pieces, x, dx1, vecs, w_cat)


def _matmul_tn(a, b, tn, name, ts=1024):
    S, K = a.shape
    N = b.shape[1]
    n_s = S // ts

    def body(a_ref, b_ref, o_ref, acc_ref):
        s = pl.program_id(1)

        @pl.when(s == 0)
        def _():
            acc_ref[...] = jnp.zeros_like(acc_ref)

        acc_ref[...] += _dot_tn(a_ref[...], b_ref[...])

        @pl.when(s == n_s - 1)
        def _():
            o_ref[...] = acc_ref[...].astype(BF16)

    return pl.pallas_call(
        body, name=name, grid=(N // tn, n_s),
        in_specs=[pl.BlockSpec((ts, K), lambda j, s: (s, 0)), pl.BlockSpec((ts, tn), lambda j, s: (s, j))],
        out_specs=pl.BlockSpec((K, tn), lambda j, s: (0, j)),
        out_shape=jax.ShapeDtypeStruct((K, N), BF16),
        scratch_shapes=[pltpu.VMEM((K, tn), F32)],
        compiler_params=_params(("parallel", "arbitrary")),
    )(a, b)


def _rope_swap(t):
    lane = lax.broadcasted_iota(jnp.int32, t.shape, 1)
    first = (lane & (HEAD_DIM - 1)) < (HEAD_DIM // 2)
    return jnp.where(first, pltpu.roll(t, LANES - HEAD_DIM // 2, 1), pltpu.roll(t, HEAD_DIM // 2, 1))


def _rope(t, cos, sin_signed):
    return t * cos + _rope_swap(t) * sin_signed


def _rope_t(d, cos, sin_signed):
    return d * cos + _rope_swap(d * sin_signed)


def _to_kv_lanes(chunk, p, h):
    lane = lax.broadcasted_iota(jnp.int32, chunk.shape, 1)
    src = chunk if p == h else pltpu.roll(chunk, HEAD_DIM, 1)
    return jnp.where((lane >> 6) == h, src, jnp.zeros_like(src))


def _from_kv_lanes(o_a, o_b, h):
    lane = lax.broadcasted_iota(jnp.int32, o_a.shape, 1)
    a = o_a if h == 0 else pltpu.roll(o_a, HEAD_DIM, 1)
    b = o_b if h == 1 else pltpu.roll(o_b, HEAD_DIM, 1)
    return jnp.where(lane < HEAD_DIM, a, b)


def _band_bias(n):
    blk = ATTN_BLOCK
    qi = lax.broadcasted_iota(jnp.int32, (blk, 2 * blk), 0)
    kj = lax.broadcasted_iota(jnp.int32, (blk, 2 * blk), 1)
    seen = (kj > qi) & (kj <= qi + blk) & ((n > 0) | (kj >= blk))
    return jnp.concatenate([jnp.where(seen, 0.0, NEG)] * N_Q_HEADS, axis=0)


def _stack_heads(chunks, h, dtype):
    parts = []
    for g in range(4):
        j = 4 * h + g
        parts.append(_to_kv_lanes(chunks[j // 2], j % 2, h))
    return jnp.concatenate(parts, axis=0).astype(dtype)


def _fused_call(parts, name, n_steps):
    counts = [(len(p["in_specs"]), len(p["out_specs"]), len(p["scratch"])) for p in parts]
    n_in, n_out = sum(c[0] for c in counts), sum(c[1] for c in counts)

    def kernel_fn(*refs):
        i = pl.program_id(0)
        groups, a, b, c = [], 0, n_in, n_in + n_out
        for ci, co, cs in counts:
            groups.append(refs[a:a + ci] + refs[b:b + co] + refs[c:c + cs])
            a, b, c = a + ci, b + co, c + cs
        for p, g in zip(parts, groups):
            p["init"](i, *g)
        for p, g in zip(parts, groups):
            p["body"](i, *g)

    flat = lambda key: [v for p in parts for v in p[key]]
    res = pl.pallas_call(
        kernel_fn, name=name, grid=(n_steps,), in_specs=flat("in_specs"), out_specs=flat("out_specs"),
        out_shape=flat("out_shape"), scratch_shapes=flat("scratch"), compiler_params=_params(("arbitrary",)),
    )(*flat("operands"))
    out, pos = [], 0
    for _, co, _ in counts:
        out.append(res[pos:pos + co])
        pos += co
    return out


def _attn_fwd_part(pa, cos, sin, sinks):
    S = pa.shape[0]
    blk = ATTN_BLOCK
    nb = S // blk

    def body(n, sink_ref, cur_ref, prev_ref, cos_ref, sin_ref, cosp_ref, sinp_ref,
             ya_ref, qr_ref, kr_ref, vb_ref, lse_ref):
        cos_c, sin_c = cos_ref[...], sin_ref[...]
        qch = [_rope(cur_ref[:, c * LANES:(c + 1) * LANES], cos_c, sin_c) * (HEAD_DIM ** -0.5) for c in range(4)]
        for c in range(4):
            qr_ref[:, c * LANES:(c + 1) * LANES] = qch[c].astype(BF16)
        k_cur = _rope(cur_ref[:, 512:640], cos_c, sin_c).astype(BF16)
        k_prev = _rope(prev_ref[:, 0:LANES], cosp_ref[...], sinp_ref[...]).astype(BF16)
        v_cur = cur_ref[:, 640:768].astype(BF16)
        v_prev = prev_ref[:, LANES:2 * LANES].astype(BF16)
        kr_ref[...] = k_cur
        vb_ref[...] = v_cur
        K = jnp.concatenate([k_prev, k_cur], axis=0)
        V = jnp.concatenate([v_prev, v_cur], axis=0)
        lane = lax.broadcasted_iota(jnp.int32, (blk, LANES), 1)
        s = jnp.concatenate([_dot_nt(_stack_heads(qch, h, BF16), K) for h in range(2)], axis=0)
        s = s + _band_bias(n)
        rowmax = jnp.max(s, axis=1, keepdims=True)
        hd = lambda x, j: x[j * blk:(j + 1) * blk]
        m = jnp.concatenate([jnp.maximum(hd(rowmax, j), sink_ref[j]) for j in range(N_Q_HEADS)], axis=0)
        p = jnp.exp(s - m)
        den = jnp.sum(p, axis=1, keepdims=True) \
            + jnp.concatenate([jnp.exp(sink_ref[j] - hd(m, j)) for j in range(N_Q_HEADS)], axis=0)
        pb = (p * _recip(den)).astype(BF16)
        o = jnp.concatenate([_dot(pb[4 * h * blk:4 * (h + 1) * blk], V) for h in range(2)], axis=0)
        lse = m + jnp.log(den)
        outs = [o[j * blk:(j + 1) * blk, :] for j in range(N_Q_HEADS)]
        lse_tile = jnp.zeros((blk, LANES), F32)
        for j in range(N_Q_HEADS):
            lse_tile = jnp.where(lane == j, lse[j * blk:(j + 1) * blk, :], lse_tile)
        for c in range(4):
            ya_ref[:, c * LANES:(c + 1) * LANES] = _from_kv_lanes(outs[2 * c], outs[2 * c + 1], c // 2).astype(BF16)
        lse_ref[...] = lse_tile

    prev = lambda n: jnp.maximum(n - 1, 0)
    sd = lambda w, dt: jax.ShapeDtypeStruct((S, w), dt)
    return dict(
        init=lambda n, *refs: None, body=body, scratch=[], operands=[sinks, pa, pa, cos, sin, cos, sin],
        in_specs=[pl.BlockSpec(memory_space=pltpu.SMEM),
                  pl.BlockSpec((blk, A_W), lambda n: (n, 0)),
                  pl.BlockSpec((blk, 256), lambda n: (prev(n), 2)),
                  pl.BlockSpec((blk, LANES), lambda n: (n, 0)), pl.BlockSpec((blk, LANES), lambda n: (n, 0)),
                  pl.BlockSpec((blk, LANES), lambda n: (prev(n), 0)), pl.BlockSpec((blk, LANES), lambda n: (prev(n), 0))],
        out_specs=[pl.BlockSpec((blk, 512), lambda n: (n, 0)), pl.BlockSpec((blk, 512), lambda n: (n, 0)),
                   pl.BlockSpec((blk, LANES), lambda n: (n, 0)), pl.BlockSpec((blk, LANES), lambda n: (n, 0)),
                   pl.BlockSpec((blk, LANES), lambda n: (n, 0))],
        out_shape=[sd(512, BF16), sd(512, BF16), sd(LANES, BF16), sd(LANES, BF16), sd(LANES, F32)])


def _attn_bwd_part(dya, qr, kr, vb, lse, cos, sin, sinks):
    S = dya.shape[0]
    blk = ATTN_BLOCK
    nb = S // blk

    def init(n, sink_ref, dya_ref, qr_ref, kc_ref, kp_ref, vc_ref, vp_ref, lse_ref, cos_ref, sin_ref, cosp_ref, sinp_ref,
             dq_ref, dkv_ref, last_ref, dsink_ref, ck, cv):
        @pl.when(n == 0)
        def _():
            ck[...] = jnp.zeros_like(ck)
            cv[...] = jnp.zeros_like(cv)
            dsink_ref[...] = jnp.zeros_like(dsink_ref)

    def body(n, sink_ref, dya_ref, qr_ref, kc_ref, kp_ref, vc_ref, vp_ref, lse_ref, cos_ref, sin_ref, cosp_ref, sinp_ref,
             dq_ref, dkv_ref, last_ref, dsink_ref, ck, cv):
        K = jnp.concatenate([kp_ref[...], kc_ref[...]], axis=0)
        V = jnp.concatenate([vp_ref[...], vc_ref[...]], axis=0)
        qch = [qr_ref[:, c * LANES:(c + 1) * LANES] for c in range(4)]
        dch = [dya_ref[:, c * LANES:(c + 1) * LANES] for c in range(4)]
        lse_tile = lse_ref[...]
        lane8 = lax.broadcasted_iota(jnp.int32, (8, LANES), 1)
        grp = lambda x, h: x[4 * h * blk:4 * (h + 1) * blk]
        qs = jnp.concatenate([_stack_heads(qch, h, BF16) for h in range(2)], axis=0)
        dos = jnp.concatenate([_stack_heads(dch, h, BF16) for h in range(2)], axis=0)
        lse_col = jnp.concatenate([lse_tile[:, j:j + 1] for j in range(N_Q_HEADS)], axis=0)
        s = jnp.concatenate([_dot_nt(grp(qs, h), K) for h in range(2)], axis=0)
        p = jnp.exp(s + _band_bias(n) - lse_col)
        dp = jnp.concatenate([_dot_nt(grp(dos, h), V) for h in range(2)], axis=0)
        delta = jnp.sum(p * dp, axis=1, keepdims=True)
        dsb = (p * (dp - delta)).astype(BF16)
        pb = p.astype(BF16)
        dq = jnp.concatenate([_dot(grp(dsb, h), K) for h in range(2)], axis=0)
        dk_acc = _dot_tn(grp(dsb, 0), grp(qs, 0)) + _dot_tn(grp(dsb, 1), grp(qs, 1))
        dv_acc = _dot_tn(grp(pb, 0), grp(dos, 0)) + _dot_tn(grp(pb, 1), grp(dos, 1))
        dqs = [dq[j * blk:(j + 1) * blk, :] for j in range(N_Q_HEADS)]
        dsink = jnp.zeros((8, LANES), F32)
        for j in range(N_Q_HEADS):
            rows = slice(j * blk, (j + 1) * blk)
            ps_delta = jnp.exp(sink_ref[j] - lse_col[rows]) * delta[rows]
            dsink = jnp.where(lane8 == j, dsink - jnp.sum(ps_delta), dsink)
        dsink_ref[...] += dsink
        cos_c, sin_c = cos_ref[...], sin_ref[...]
        for c in range(4):
            dqc = _from_kv_lanes(dqs[2 * c], dqs[2 * c + 1], c // 2) * (HEAD_DIM ** -0.5)
            dq_ref[:, c * LANES:(c + 1) * LANES] = _rope_t(dqc, cos_c, sin_c).astype(BF16)
        dkv_ref[:, 0:LANES] = _rope_t(dk_acc[0:blk, :] + ck[...], cosp_ref[...], sinp_ref[...]).astype(BF16)
        dkv_ref[:, LANES:2 * LANES] = (dv_acc[0:blk, :] + cv[...]).astype(BF16)
        ck[...] = dk_acc[blk:2 * blk, :]
        cv[...] = dv_acc[blk:2 * blk, :]
        last_ref[:, 0:LANES] = _rope_t(dk_acc[blk:2 * blk, :], cos_c, sin_c).astype(BF16)
        last_ref[:, LANES:2 * LANES] = dv_acc[blk:2 * blk, :].astype(BF16)

    prev = lambda n: jnp.maximum(n - 1, 0)
    same = lambda n: n
    bs = lambda w, f: pl.BlockSpec((blk, w), lambda n: (f(n), 0))
    return dict(
        init=init, body=body, operands=[sinks, dya, qr, kr, kr, vb, vb, lse, cos, sin, cos, sin],
        in_specs=[pl.BlockSpec(memory_space=pltpu.SMEM),
                  bs(512, same), bs(512, same), bs(LANES, same), bs(LANES, prev), bs(LANES, same), bs(LANES, prev),
                  bs(LANES, same), bs(LANES, same), bs(LANES, same), bs(LANES, prev), bs(LANES, prev)],
        out_specs=[bs(512, same), bs(256, prev), _full((blk, 256)), _full((8, LANES))],
        out_shape=[jax.ShapeDtypeStruct((S, 512), BF16), jax.ShapeDtypeStruct((S, 256), BF16),
                   jax.ShapeDtypeStruct((blk, 256), BF16), jax.ShapeDtypeStruct((8, LANES), F32)],
        scratch=[pltpu.VMEM((blk, LANES), F32), pltpu.VMEM((blk, LANES), F32)])


def _split3(x):
    hi = x.astype(BF16)
    r1 = x - hi.astype(F32)
    mid = r1.astype(BF16)
    lo = (r1 - mid.astype(F32)).astype(BF16)
    return hi, mid, lo


def _tri_matmul(tri_b, x):
    hi, mid, lo = _split3(x)
    return _dot(tri_b, hi) + _dot(tri_b, mid) + _dot(tri_b, lo)


def _log_sigmoid(x):
    return jnp.minimum(x, 0.0) - jnp.log(1.0 + jnp.exp(-jnp.abs(x)))


def _shift_rows(cur, seam, k, down):
    L = cur.shape[0]
    row8 = lax.broadcasted_iota(jnp.int32, seam.shape, 0)
    if down:
        mixed = jnp.concatenate([cur[:L - 8], jnp.where(row8 >= 8 - k, seam, cur[L - 8:])], axis=0)
        return pltpu.roll(mixed, k, 0)
    mixed = jnp.concatenate([jnp.where(row8 < k, seam, cur[:8]), cur[8:]], axis=0)
    return pltpu.roll(mixed, L - k, 0)


def _conv_fwd(cur, tail, cw_ref):
    z = cw_ref[4:5, :]
    for k in range(3, 0, -1):
        z = z + _shift_rows(cur, tail, k, True) * cw_ref[3 - k:4 - k, :]
    return z + cur * cw_ref[3:4, :]


def _stack(f):
    return jnp.concatenate([f(h) for h in range(MLSTM_HEADS)], axis=0)


def _head(x, h):
    L = x.shape[0] // MLSTM_HEADS
    return x[h * L:(h + 1) * L]


def _mlstm_heads_fwd(qk, cur_ref, gt, b_all, c_prev, nmv, tri, eye):
    L = qk.shape[0]
    HD = MLSTM_HEAD_DIM
    W4 = MLSTM_HEADS * HD
    col2row = lambda x: jnp.sum(jnp.where(eye, x, 0.0), axis=0, keepdims=True)
    b_col = _stack(lambda h: b_all[:, 4 + h:5 + h])
    i_col = _stack(lambda h: gt[:, h:h + 1])
    b_row = _stack(lambda h: jnp.broadcast_to(col2row(b_all[:, 4 + h:5 + h]), (L, L)))
    i_row = _stack(lambda h: jnp.broadcast_to(col2row(gt[:, h:h + 1]), (L, L)))
    bl = _stack(lambda h: jnp.broadcast_to(b_all[L - 1:L, 4 + h:5 + h], (L, 1)))
    m_prev = _stack(lambda h: jnp.broadcast_to(nmv[4 + h:5 + h, 0:1], (L, 1)))
    n_prev = _stack(lambda h: jnp.broadcast_to(nmv[h:h + 1, :], (L, HD)))
    tri4 = jnp.concatenate([tri] * MLSTM_HEADS, axis=0)
    Dm = jnp.where(tri4, b_col - b_row + i_row, NEG)
    inter = b_col + m_prev
    m_t = jnp.maximum(inter, jnp.max(Dm, axis=1, keepdims=True))
    W = jnp.exp(Dm - m_t)
    e_t = jnp.exp(inter - m_t)
    q = _stack(lambda h: qk[:, h * HD:(h + 1) * HD])
    k = _stack(lambda h: qk[:, W4 + h * HD:W4 + (h + 1) * HD]) * (HD ** -0.5)
    v = _stack(lambda h: cur_ref[:, 2 * W4 + h * HD:2 * W4 + (h + 1) * HD])
    qb, kb, vb = q.astype(BF16), k.astype(BF16), v.astype(BF16)
    Sc = _stack(lambda h: _dot_nt(_head(qb, h), _head(kb, h))) * W
    Scb = Sc.astype(BF16)
    cb = [c.astype(BF16) for c in c_prev]
    P1 = _stack(lambda h: _dot(_head(qb, h), cb[h]))
    num = _stack(lambda h: _dot(_head(Scb, h), _head(vb, h))) + e_t * P1
    qn = jnp.sum(q * n_prev, axis=1, keepdims=True)
    den = jnp.sum(Sc, axis=1, keepdims=True) + e_t * qn
    floor = jnp.exp(-m_t)
    inv_g = _recip(jnp.maximum(jnp.abs(den), floor))
    hv = num * inv_g
    a_col = bl - b_col + i_col
    a_max = _stack(lambda h: jnp.broadcast_to(jnp.max(_head(a_col, h), axis=0, keepdims=True), (L, 1)))
    m_new = jnp.maximum(bl + m_prev, a_max)
    dec = jnp.exp(bl + m_prev - m_new)
    u_col = jnp.exp(a_col - m_new)
    return dict(W=W, e_t=e_t, q=q, k=k, v=v, qb=qb, kb=kb, vb=vb, cb=cb, Sc=Sc, Scb=Scb, P1=P1, qn=qn, den=den,
                floor=floor, inv_g=inv_g, hv=hv, n_prev=n_prev, m_new=m_new, dec=dec, u_col=u_col)


def _mlstm_fwd_part(pm, pif, cw, sv):
    S = pm.shape[0]
    L = MLSTM_CHUNK
    nc = S // L
    HD = MLSTM_HEAD_DIM
    W4 = MLSTM_HEADS * HD

    def init(c, cur_ref, pif_ref, cw_ref, sv_ref, ym_ref, z_ref, cst_ref, nst_ref, C, nm, tail):
        @pl.when(c == 0)
        def _():
            C[...] = jnp.zeros_like(C)
            nm[...] = jnp.zeros_like(nm)
            tail[...] = jnp.zeros_like(tail)

    def body(c, cur_ref, pif_ref, cw_ref, sv_ref, ym_ref, z_ref, cst_ref, nst_ref, C, nm, tail):
        z = _conv_fwd(cur_ref[:, 0:2 * W4], tail[...], cw_ref)
        tail[...] = cur_ref[L - 8:L, 0:2 * W4]
        z_ref[...] = z
        qk = z * _sigmoid(z)
        gt = pif_ref[...] + sv_ref[1:2, 0:LANES]
        r_i = lax.broadcasted_iota(jnp.int32, (L, L), 0)
        c_i = lax.broadcasted_iota(jnp.int32, (L, L), 1)
        tri = c_i <= r_i
        eye = c_i == r_i
        b_all = _tri_matmul(tri.astype(BF16), _log_sigmoid(gt))
        nmv = nm[...]
        nst_ref[0] = nmv
        c_prev = [C[h] for h in range(MLSTM_HEADS)]
        f = _mlstm_heads_fwd(qk, cur_ref, gt, b_all, c_prev, nmv, tri, eye)
        hv = f["hv"]
        xc = hv - _rowmean(hv)
        hhat = xc * lax.rsqrt(_rowmean(xc * xc) + NORM_EPS)
        so = _sigmoid(_stack(lambda h: cur_ref[:, 3 * W4 + h * HD:3 * W4 + (h + 1) * HD]))
        wn = _stack(lambda h: jnp.broadcast_to(sv_ref[0:1, h * HD:(h + 1) * HD], (L, HD)))
        y = (so * hhat * wn).astype(BF16)
        kw = f["k"] * f["u_col"]
        kwb = kw.astype(BF16)
        n_new, m_new = [], []
        for h in range(MLSTM_HEADS):
            cst_ref[0, h] = c_prev[h]
            ym_ref[:, h * HD:(h + 1) * HD] = _head(y, h)
            dec = f["dec"][h * L:h * L + 1, :]
            C[h] = dec * c_prev[h] + _dot_tn(_head(kwb, h), _head(f["vb"], h))
            n_new.append(dec * nmv[h:h + 1, :] + _colsum(_head(kw, h)))
            m_new.append(jnp.broadcast_to(f["m_new"][h * L:h * L + 1, :], (1, LANES)))
        nm[...] = jnp.concatenate(n_new + m_new, axis=0)

    return dict(
        init=init, body=body, operands=[pm, pif, cw, sv],
        in_specs=[pl.BlockSpec((L, M_W), lambda c: (c, 0)),
                  pl.BlockSpec((L, IF_W), lambda c: (c, 0)), _full(cw.shape), _full(sv.shape)],
        out_specs=[pl.BlockSpec((L, W4), lambda c: (c, 0)), pl.BlockSpec((L, 2 * W4), lambda c: (c, 0)),
                   pl.BlockSpec((1, MLSTM_HEADS, HD, HD), lambda c: (c, 0, 0, 0)),
                   pl.BlockSpec((1, 8, LANES), lambda c: (c, 0, 0))],
        out_shape=[jax.ShapeDtypeStruct((S, W4), BF16), jax.ShapeDtypeStruct((S, 2 * W4), F32),
                   jax.ShapeDtypeStruct((nc, MLSTM_HEADS, HD, HD), F32), jax.ShapeDtypeStruct((nc, 8, LANES), F32)],
        scratch=[pltpu.VMEM((MLSTM_HEADS, HD, HD), F32), pltpu.VMEM((8, LANES), F32), pltpu.VMEM((8, 2 * W4), F32)])


def _mlstm_bwd_part(pm, zc, pif, cw, sv, dym, cst, nst):
    S = pm.shape[0]
    L = MLSTM_CHUNK
    nc = S // L
    HD = MLSTM_HEAD_DIM
    W4 = MLSTM_HEADS * HD

    def init(r, cur_ref, z_ref, pif_ref, cw_ref, sv_ref, dym_ref, cst_ref, nst_ref,
             dm_ref, dif_ref, dcw_ref, dsv_ref, dC, dn, dz_next, dqk):
        @pl.when(r == 0)
        def _():
            dC[...] = jnp.zeros_like(dC)
            dn[...] = jnp.zeros_like(dn)
            dz_next[...] = jnp.zeros_like(dz_next)
            dcw_ref[...] = jnp.zeros_like(dcw_ref)
            dsv_ref[...] = jnp.zeros_like(dsv_ref)

    def body(r, cur_ref, z_ref, pif_ref, cw_ref, sv_ref, dym_ref, cst_ref, nst_ref,
             dm_ref, dif_ref, dcw_ref, dsv_ref, dC, dn, dz_next, dqk):
        z = z_ref[...]
        sgz = _sigmoid(z)
        qk = z * sgz
        gt = pif_ref[...] + sv_ref[1:2, 0:LANES]
        r_i = lax.broadcasted_iota(jnp.int32, (L, L), 0)
        c_i = lax.broadcasted_iota(jnp.int32, (L, L), 1)
        tri = c_i <= r_i
        eye = c_i == r_i
        b_all = _tri_matmul(tri.astype(BF16), _log_sigmoid(gt))
        lane = lax.broadcasted_iota(jnp.int32, (L, LANES), 1)
        rowl = lax.broadcasted_iota(jnp.int32, (L, 1), 0)
        nmv = nst_ref[0]
        heads = range(MLSTM_HEADS)
        c_prev = [cst_ref[0, h] for h in heads]
        f = _mlstm_heads_fwd(qk, cur_ref, gt, b_all, c_prev, nmv, tri, eye)
        hv, inv_g, den, e_t, u_col, n_prev = f["hv"], f["inv_g"], f["den"], f["e_t"], f["u_col"], f["n_prev"]
        q, k, v, qb, kb, vb, Sc, Scb, W = f["q"], f["k"], f["v"], f["qb"], f["kb"], f["vb"], f["Sc"], f["Scb"], f["W"]
        xc = hv - _rowmean(hv)
        rstd = lax.rsqrt(_rowmean(xc * xc) + NORM_EPS)
        hhat = xc * rstd
        wn = _stack(lambda h: jnp.broadcast_to(sv_ref[0:1, h * HD:(h + 1) * HD], (L, HD)))
        so = _sigmoid(_stack(lambda h: cur_ref[:, 3 * W4 + h * HD:3 * W4 + (h + 1) * HD]))
        dy = _stack(lambda h: dym_ref[:, h * HD:(h + 1) * HD])
        d_o = (dy * hhat * wn * (so * (1.0 - so))).astype(BF16)
        dln = dy * so
        dwn = dln * hhat
        dhhat = dln * wn
        dh = rstd * (dhhat - _rowmean(dhhat) - hhat * _rowmean(dhhat * hhat))
        dnum = dh * inv_g
        active = jnp.abs(den) > f["floor"]
        dden = jnp.where(active, -jnp.sum(dh * hv, axis=1, keepdims=True) * inv_g * jnp.where(den >= 0.0, 1.0, -1.0), 0.0)
        dnumb = dnum.astype(BF16)
        dSc = _stack(lambda h: _dot_nt(_head(dnumb, h), _head(vb, h))) + dden
        dA = (dSc * W).astype(BF16)
        G = dSc * Sc
        Gr = jnp.sum(G, axis=1, keepdims=True)
        Gc = _stack(lambda h: jnp.sum(jnp.where(eye, _colsum(_head(G, h)), 0.0), axis=1, keepdims=True))
        dCn = [dC[h] for h in heads]
        dCnb = [d.astype(BF16) for d in dCn]
        dnv = dn[...]
        dn_new = _stack(lambda h: jnp.broadcast_to(dnv[h:h + 1, :], (L, HD)))
        kdC = _stack(lambda h: _dot(_head(kb, h), dCnb[h]))
        vdC = _stack(lambda h: _dot_nt(_head(vb, h), dCnb[h]))
        dv = (_stack(lambda h: _dot_tn(_head(Scb, h), _head(dnumb, h))) + u_col * kdC).astype(BF16)
        dq = _stack(lambda h: _dot(_head(dA, h), _head(kb, h))) \
            + e_t * _stack(lambda h: _dot_nt(_head(dnumb, h), f["cb"][h])) + (e_t * dden) * n_prev
        dk = (_stack(lambda h: _dot_tn(_head(dA, h), _head(qb, h))) + u_col * (vdC + dn_new)) * (HD ** -0.5)
        E = (jnp.sum(f["P1"] * dnum, axis=1, keepdims=True) + dden * f["qn"]) * e_t
        U = (jnp.sum(kdC * v, axis=1, keepdims=True) + jnp.sum(k * dn_new, axis=1, keepdims=True)) * u_col
        qe = (q * e_t).astype(BF16)
        qd = (e_t * dden) * q
        di = Gc + U
        db = Gr + E - Gc - U
        di_tile = jnp.zeros((L, LANES), F32)
        db_tile = jnp.zeros((L, LANES), F32)
        dn_rows = []
        for h in heads:
            dec = f["dec"][h * L:h * L + 1, :]
            ddec = jnp.sum(dCn[h] * c_prev[h]) + jnp.sum(dnv[h:h + 1, :] * nmv[h:h + 1, :])
            dbl = ddec * dec + jnp.sum(_head(U, h), axis=0, keepdims=True)
            di_tile = jnp.where(lane == h, _head(di, h), di_tile)
            db_tile = jnp.where(lane == 4 + h, _head(db, h) + jnp.where(rowl == L - 1, dbl, 0.0), db_tile)
            dC[h] = dec * dCn[h] + _dot_tn(_head(qe, h), _head(dnumb, h))
            dn_rows.append(dec * dnv[h:h + 1, :] + _colsum(_head(qd, h)))
            dsv_ref[0:1, h * HD:(h + 1) * HD] += _colsum(_head(dwn, h))
            dqk[:, h * HD:(h + 1) * HD] = _head(dq, h)
            dqk[:, W4 + h * HD:W4 + (h + 1) * HD] = _head(dk, h)
            dm_ref[:, 2 * W4 + h * HD:2 * W4 + (h + 1) * HD] = _head(dv, h)
            dm_ref[:, 3 * W4 + h * HD:3 * W4 + (h + 1) * HD] = _head(d_o, h)
        dn[...] = jnp.concatenate(dn_rows + [jnp.zeros((8 - MLSTM_HEADS, LANES), F32)], axis=0)
        dlf = _tri_matmul((r_i <= c_i).astype(BF16), db_tile)
        dif = jnp.where(lane < 4, di_tile, jnp.where(lane < 8, dlf * (1.0 - _sigmoid(gt)), 0.0))
        dif_ref[...] = dif.astype(BF16)
        dsv_ref[1:2, 0:LANES] += _colsum(dif)
        dz = dqk[...] * (sgz * (1.0 + z * (1.0 - sgz)))
        dcw_ref[4:5, :] += _colsum(dz)
        u = cur_ref[:, 0:2 * W4]
        du_in = dz * cw_ref[3:4, :]
        dcw_ref[3:4, :] += _colsum(dz * u)
        for k in range(1, 4):
            up = _shift_rows(dz, dz_next[...], k, False)
            dcw_ref[3 - k:4 - k, :] += _colsum(up * u)
            du_in = du_in + up * cw_ref[3 - k:4 - k, :]
        dz_next[...] = dz[0:8, :]
        dm_ref[:, 0:2 * W4] = du_in.astype(BF16)

    cidx = lambda r: nc - 1 - r
    return dict(
        init=init, body=body, operands=[pm, zc, pif, cw, sv, dym, cst, nst],
        in_specs=[pl.BlockSpec((L, M_W), lambda r: (cidx(r), 0)), pl.BlockSpec((L, 2 * W4), lambda r: (cidx(r), 0)),
                  pl.BlockSpec((L, IF_W), lambda r: (cidx(r), 0)), _full(cw.shape), _full(sv.shape),
                  pl.BlockSpec((L, W4), lambda r: (cidx(r), 0)),
                  pl.BlockSpec((1, MLSTM_HEADS, HD, HD), lambda r: (cidx(r), 0, 0, 0)),
                  pl.BlockSpec((1, 8, LANES), lambda r: (cidx(r), 0, 0))],
        out_specs=[pl.BlockSpec((L, M_W), lambda r: (cidx(r), 0)), pl.BlockSpec((L, IF_W), lambda r: (cidx(r), 0)),
                   _full((8, 2 * W4)), _full((8, W4))],
        out_shape=[jax.ShapeDtypeStruct((S, M_W), BF16), jax.ShapeDtypeStruct((S, IF_W), BF16),
                   jax.ShapeDtypeStruct((8, 2 * W4), F32), jax.ShapeDtypeStruct((8, W4), F32)],
        scratch=[pltpu.VMEM((MLSTM_HEADS, HD, HD), F32), pltpu.VMEM((8, LANES), F32),
                 pltpu.VMEM((8, 2 * W4), F32), pltpu.VMEM((L, 2 * W4), F32)])


def _rope_tables(positions):
    half = HEAD_DIM // 2
    inv_freq = ROPE_THETA ** (-2.0 * jnp.arange(half, dtype=F32) / HEAD_DIM)
    ang = positions.astype(F32)[:, None] * inv_freq
    cos = jnp.tile(jnp.cos(ang), (1, LANES // half))
    sign = jnp.tile(jnp.concatenate([-jnp.ones((half,), F32), jnp.ones((half,), F32)]), LANES // HEAD_DIM)
    sin = jnp.tile(jnp.sin(ang), (1, LANES // half)) * sign
    return cos, sin


def _local_step(x, tgt, positions, mod, gains, w_cat, w_ba, w_bm, w_out, w_gate, w_up, w_down,
                conv_w, conv_b, b_if, sinks, norm_w):
    t = _tables(mod, gains, conv_w, conv_b, b_if, norm_w, positions)
    a = _mixer_fwd(x, t, sinks, w_cat)
    b = _ffn_part(x, tgt, t, a, w_ba, w_bm, w_out, w_gate, w_up, w_down)
    c = _mixer_bwd(b["dx1"], t, a, b, sinks, w_ba, w_bm, w_out)
    grad_x, acc_p = _pre_bwd(c["dproj"], x, b["dx1"], t["vecs"], w_cat)
    big = dict(w_cat=jnp.concatenate(c["g_w_cat"], axis=1), w_ba=c["g_w_ba"], w_bm=c["g_w_bm"], w_out=c["g_w_out"], w_gate=b["g_w_gate"],
               w_up=b["g_w_up"], w_down=b["g_w_down"])
    return b["loss"], grad_x, big, _small_grads(acc_p, b, c)


def _tables(mod, gains, conv_w, conv_b, b_if, norm_w, positions):
    cos, sin = _rope_tables(positions)
    return dict(
        vecs=jnp.concatenate([mod, gains, jnp.zeros((6, D_MODEL), F32)], axis=0),
        cw=jnp.concatenate([conv_w, conv_b.reshape(1, -1), jnp.zeros((3, 2 * 512), F32)], axis=0),
        sv=jnp.zeros((8, 512), F32).at[0].set(norm_w).at[1, 0:8].set(b_if), cos=cos, sin=sin)


def _mixer_fwd(x, t, sinks, w_cat):
    h, pa, pm, pif, pg = _pre_proj(x, t["vecs"], w_cat)
    n_blk = x.shape[0] // ATTN_BLOCK
    (ya, qr, kr, vb, lse), = _fused_call([_attn_fwd_part(pa, t["cos"], t["sin"], sinks)], "attn_fwd", n_blk)
    (ym, zc, cst, nst), = _fused_call([_mlstm_fwd_part(pm, pif, t["cw"], t["sv"])], "mlstm_fwd", n_blk)
    return dict(h=h, pm=pm, pif=pif, pg=pg, ya=ya, qr=qr, kr=kr, vb=vb, lse=lse, ym=ym, zc=zc, cst=cst, nst=nst)


def _ffn_part(x, tgt, t, a, w_ba, w_bm, w_out, w_gate, w_up, w_down):
    x1, merged, mix, pba, pbm = _mix_fwd(x, a["ya"], a["ym"], a["pg"], t["vecs"], w_ba, w_bm, w_out)
    dx1, h2, hid, da, du, dff, acc_f, loss = _ffn_fwd_bwd(x1, tgt, t["vecs"], w_gate, w_up, w_down)
    return dict(merged=merged, mix=mix, pba=pba, pbm=pbm, dx1=dx1, acc_f=acc_f, loss=loss[0, 0],
                g_w_gate=_matmul_tn(da, h2, 1024, "dw_ffn_gate"),
                g_w_up=_matmul_tn(du, h2, 1024, "dw_ffn_up"),
                g_w_down=_matmul_tn(hid, dff, 1024, "dw_ffn_down"))


def _mixer_bwd(dx1, t, a, b, sinks, w_ba, w_bm, w_out):
    dmix, dpa, dpb, dg, dya, dym, acc_m = _mix_bwd(dx1, b["mix"], b["pba"], b["pbm"], a["pg"], t["vecs"], w_ba, w_bm, w_out)
    g_w_out = _matmul_tn(b["merged"], dmix, 1024, "dw_out")
    g_w_ba = _matmul_tn(a["ya"], dpa, 1024, "dw_branch_attn")
    g_w_bm = _matmul_tn(a["ym"], dpb, 1024, "dw_branch_mlstm")
    n_blk = dx1.shape[0] // ATTN_BLOCK
    (dq, dkv, dkv_last, dsink), = _fused_call(
        [_attn_bwd_part(dya, a["qr"], a["kr"], a["vb"], a["lse"], t["cos"], t["sin"], sinks)], "attn_bwd", n_blk)
    (dm, dif, dcw, dsv), = _fused_call(
        [_mlstm_bwd_part(a["pm"], a["zc"], a["pif"], t["cw"], t["sv"], dym, a["cst"], a["nst"])], "mlstm_bwd", n_blk)
    dkv = lax.dynamic_update_slice(dkv, dkv_last, (dkv.shape[0] - ATTN_BLOCK, 0))
    dproj = [dq, dkv, dm, dif, dg]
    g_w_cat = [_matmul_tn(a["h"], p, min(p.shape[1], 1024), "dw_in_" + n)
               for p, n in zip(dproj, ("q", "kv", "mlstm", "gates", "branch"))]
    return dict(dproj=dproj, g_w_cat=g_w_cat, g_w_out=g_w_out, g_w_ba=g_w_ba,
                g_w_bm=g_w_bm, acc_m=acc_m, dsink=dsink, dcw=dcw, dsv=dsv)


def _small_grads(acc_p, b, c):
    acc_f, acc_m = b["acc_f"], c["acc_m"]
    dmod = jnp.stack([acc_p[1], acc_p[0], acc_m[0], acc_f[3], acc_f[2], acc_f[0]])
    dgains = jnp.stack([acc_p[2], acc_m[1], acc_f[4], acc_f[1]])
    return dict(dmod=dmod, dgains=dgains, dconv_w=c["dcw"][0:4], dconv_b=c["dcw"][4], db_if=c["dsv"][1, 0:8],
                dsinks=c["dsink"][0, 0:8], dnorm_w=c["dsv"][0])


MESH_ID = pl.DeviceIdType.MESH


def _mesh_pos():
    return lax.axis_index("x"), lax.axis_index("y"), lax.axis_index("c")


def _flip(v, bit):
    return 1 - v if bit else v


def _relations():
    return [((r >> 2) & 1, (r >> 1) & 1, r & 1) for r in range(1, N_DEV)]


def _small_exchange(p, gather, name):
    R, V = p.shape[-2:]

    def body(p_ref, out_ref, send_sems, recv_sems):
        x, y, c = _mesh_pos()
        me = 4 * x + 2 * y + c
        out_ref[me] = p_ref[...] if gather else p_ref[me]
        peers = []
        for dx, dy, dc in _relations():
            px, py, pc = _flip(x, dx), _flip(y, dy), _flip(c, dc)
            peers.append(((px, py, pc), 4 * px + 2 * py + pc))

        def copy(k, landing):
            peer, pid = peers[k]
            return pltpu.make_async_remote_copy(
                src_ref=p_ref if gather else p_ref.at[pid], dst_ref=out_ref.at[landing],
                send_sem=send_sems.at[k], recv_sem=recv_sems.at[k], device_id=peer, device_id_type=MESH_ID)

        sends = [copy(k, me) for k in range(N_DEV - 1)]
        for cp in sends:
            cp.start()
        for k in range(N_DEV - 1):
            copy(k, peers[k][1]).wait_recv()
        for cp in sends:
            cp.wait_send()

    vm = pl.BlockSpec(memory_space=pltpu.VMEM)
    return pl.pallas_call(
        body, name=name, in_specs=[vm], out_specs=vm,
        out_shape=jax.ShapeDtypeStruct((N_DEV, R, V), F32),
        scratch_shapes=[pltpu.SemaphoreType.DMA((N_DEV - 1,)), pltpu.SemaphoreType.DMA((N_DEV - 1,))],
        compiler_params=pltpu.CompilerParams(vmem_limit_bytes=VMEM_LIMIT),
    )(p)


HBM_SPEC = pl.BlockSpec(memory_space=pltpu.HBM)
SEM_SPEC = pl.BlockSpec(memory_space=pltpu.SEMAPHORE)


def _peers(x, y, c):
    out = []
    for dx, dy, dc in _relations():
        px, py, pc = _flip(x, dx), _flip(y, dy), _flip(c, dc)
        out.append(((px, py, pc), 4 * px + 2 * py + pc))
    return out


def _exchange_start(arrs, gather, after, name):
    n = len(arrs)
    me_out = 4 * lax.axis_index("x") + 2 * lax.axis_index("y") + lax.axis_index("c")
    lands = []
    for a in arrs:
        own = a[None] if gather else lax.dynamic_index_in_dim(a, me_out, 0, keepdims=True)
        empty = lax.empty(((N_DEV,) + a.shape) if gather else a.shape, a.dtype)
        lands.append(lax.dynamic_update_index_in_dim(empty, own, me_out, 0))

    def body(*refs):
        a_refs, l_refs = refs[:n], refs[n:2 * n]
        send_sems, recv_sems = refs[2 * n + 1], refs[2 * n + 2]
        token = refs[4 * n + 3]
        x, y, c = _mesh_pos()
        me = 4 * x + 2 * y + c
        for a in range(n):
            for k, (peer, pid) in enumerate(_peers(x, y, c)):
                pltpu.make_async_remote_copy(
                    src_ref=a_refs[a] if gather else a_refs[a].at[pid], dst_ref=l_refs[a].at[me],
                    send_sem=send_sems.at[a * (N_DEV - 1) + k], recv_sem=recv_sems.at[a * (N_DEV - 1) + k],
                    device_id=peer, device_id_type=MESH_ID).start()
        token[...] = jnp.zeros_like(token)

    sem = pltpu.SemaphoreType.DMA((n * (N_DEV - 1),))
    hbm = lambda a: pltpu.with_memory_space_constraint(a, pltpu.HBM)
    res = pl.pallas_call(
        body, name=name,
        out_shape=(sem, sem, *[pltpu.HBM(a.shape, a.dtype) for a in arrs], *[pltpu.HBM(l.shape, l.dtype) for l in lands],
                   jax.ShapeDtypeStruct((8, LANES), F32)),
        in_specs=[HBM_SPEC] * (2 * n) + [pl.BlockSpec(memory_space=pl.ANY)],
        out_specs=(SEM_SPEC, SEM_SPEC, *[HBM_SPEC] * (2 * n), pl.BlockSpec(memory_space=pltpu.VMEM)),
        input_output_aliases={i: 2 + i for i in range(2 * n)},
        compiler_params=pltpu.CompilerParams(has_side_effects=pltpu.SideEffectType.DATAFLOW_SIDE_EFFECTING),
    )(*[hbm(a) for a in arrs], *[hbm(l) for l in lands], after)
    return dict(sems=res[0:2], arrs=res[2:2 + n], lands=res[2 + n:2 + 2 * n], token=res[2 + 2 * n], gather=gather)


def _exchange_wait(st, after, name):
    n = len(st["arrs"])
    gather = st["gather"]

    def body(*refs):
        a_refs, l_refs = refs[:n], refs[n:2 * n]
        send_sems, recv_sems = refs[2 * n], refs[2 * n + 1]
        x, y, c = _mesh_pos()
        for a in range(n):
            for k, (peer, pid) in enumerate(_peers(x, y, c)):
                cp = pltpu.make_async_remote_copy(
                    src_ref=a_refs[a] if gather else a_refs[a].at[pid], dst_ref=l_refs[a].at[pid],
                    send_sem=send_sems.at[a * (N_DEV - 1) + k], recv_sem=recv_sems.at[a * (N_DEV - 1) + k],
                    device_id=peer, device_id_type=MESH_ID)
                cp.wait_send()
                cp.wait_recv()

    both = list(st["arrs"]) + list(st["lands"])
    res = pl.pallas_call(
        body, name=name, out_shape=[pltpu.HBM(a.shape, a.dtype) for a in both],
        in_specs=[HBM_SPEC] * (2 * n) + [SEM_SPEC, SEM_SPEC, pl.BlockSpec(memory_space=pl.ANY)],
        out_specs=[HBM_SPEC] * (2 * n), input_output_aliases={i: i for i in range(2 * n)},
        compiler_params=pltpu.CompilerParams(has_side_effects=pltpu.SideEffectType.DATAFLOW_SIDE_EFFECTING),
    )(*both, *st["sems"], after)
    return res[n:2 * n]


def _tie(x, token, name):
    def body(x_ref, t_ref, o_ref):
        o_ref[...] = x_ref[...]

    vm = pl.BlockSpec(memory_space=pltpu.VMEM)
    return pl.pallas_call(
        body, name=name, in_specs=[vm, pl.BlockSpec(memory_space=pl.ANY)], out_specs=vm,
        out_shape=jax.ShapeDtypeStruct(x.shape, x.dtype),
    )(x, token)


def _all_gather_hbm(shards):
    n = len(shards)

    def body(*refs):
        p_refs, out_refs = refs[:n], refs[n:2 * n]
        send_sems, recv_sems, local_sems = refs[2 * n:]
        x, y, c = _mesh_pos()
        me, sibling = (x, y, c), (x, y, 1 - c)
        chips = [(1 - x, y), (x, 1 - y), (1 - x, 1 - y)]

        def copy(a, k, block, to, own=False):
            slot = out_refs[a].at[4 * block[0] + 2 * block[1] + block[2]]
            return pltpu.make_async_remote_copy(
                src_ref=p_refs[a] if own else slot, dst_ref=slot,
                send_sem=send_sems.at[a, k], recv_sem=recv_sems.at[a, k], device_id=to, device_id_type=MESH_ID)

        mine = [pltpu.make_async_copy(p_refs[a], out_refs[a].at[4 * x + 2 * y + c], local_sems.at[a]) for a in range(n)]
        for cp in mine:
            cp.start()
        first = []
        for a in range(n):
            first.append(copy(a, 0, me, sibling, own=True))
            first += [copy(a, 1 + j, me, (*chip, c), own=True) for j, chip in enumerate(chips)]
        for cp in first:
            cp.start()
        passed = []
        for j, chip in enumerate(chips):
            for a in range(n):
                copy(a, 1 + j, (*chip, c), me).wait_recv()
                passed.append(copy(a, 4 + j, (*chip, c), sibling))
                passed[-1].start()
        for a in range(n):
            copy(a, 0, sibling, me).wait_recv()
            for j, chip in enumerate(chips):
                copy(a, 4 + j, (*chip, 1 - c), me).wait_recv()
        for cp in first + passed:
            cp.wait_send()
        for cp in mine:
            cp.wait()

    hbm = pl.BlockSpec(memory_space=pl.ANY)
    return pl.pallas_call(
        body, name="gather_weights", in_specs=[hbm] * n, out_specs=[hbm] * n,
        out_shape=[jax.ShapeDtypeStruct((N_DEV,) + s.shape, s.dtype) for s in shards],
        scratch_shapes=[pltpu.SemaphoreType.DMA((n, N_DEV - 1)), pltpu.SemaphoreType.DMA((n, N_DEV - 1)),
                        pltpu.SemaphoreType.DMA((n,))],
    )(*shards)


def _adamw(w, g, m, v):
    m2 = ADAM_B1 * m + (1.0 - ADAM_B1) * g
    v2 = ADAM_B2 * v + (1.0 - ADAM_B2) * (g * g)
    m_hat = m2 / (1.0 - ADAM_B1 ** ADAM_STEP)
    v_hat = v2 / (1.0 - ADAM_B2 ** ADAM_STEP)
    delta = -ADAM_LR * (m_hat / (jnp.sqrt(v_hat) + ADAM_EPS) + ADAM_WD * w)
    return delta, m2, v2


def _mod_partial(cmat, w_shard, b_shard):
    def body(c_ref, w_ref, b_ref, o_ref):
        o_ref[...] = _dot(c_ref[...].astype(BF16), w_ref[...].astype(BF16)) + b_ref[...]

    return pl.pallas_call(
        body, name="mod_partial", out_shape=jax.ShapeDtypeStruct((N_DEV, w_shard.shape[1]), F32),
        compiler_params=_params(),
    )(cmat, w_shard, b_shard)


def _adamw_w_ada(cmat, dmod_cols, w, m, v):
    def body(c_ref, d_ref, w_ref, m_ref, v_ref, g_ref, dl_ref, m2_ref, v2_ref):
        g = _dot_tn(c_ref[...].astype(BF16), d_ref[...].astype(BF16))
        g_ref[...] = g
        dl_ref[...], m2_ref[...], v2_ref[...] = _adamw(w_ref[...], g, m_ref[...], v_ref[...])

    return pl.pallas_call(
        body, name="adamw_w_ada", out_shape=[jax.ShapeDtypeStruct(w.shape, F32)] * 4,
        compiler_params=_params(),
    )(cmat, dmod_cols, w, m, v)


SMALL_ROWS = 16
SMALL_AT = {"b_ada": (0, 6, 0, D_MODEL), "g_pre_mix": (6, 1, 0, D_MODEL), "g_post_mix": (7, 1, 0, D_MODEL),
            "g_pre_ffn": (8, 1, 0, D_MODEL), "g_post_ffn": (9, 1, 0, D_MODEL), "conv_b": (10, 1, 0, D_MODEL),
            "mlstm_norm_w": (11, 1, 0, 512), "b_if": (11, 1, 512, LANES), "attn_sinks": (11, 1, 640, LANES)}


def _small_table(part):
    tail = jnp.concatenate([part["mlstm_norm_w"], jnp.pad(part["b_if"], (0, LANES - 8)),
                            jnp.pad(part["attn_sinks"], (0, LANES - 8)), jnp.zeros((256,), F32)])
    return jnp.concatenate([part["b_ada"], part["gains"], part["conv_b"][None], tail[None],
                            jnp.zeros((SMALL_ROWS - 12, D_MODEL), F32)], axis=0)


def _adamw_small(gathered, wmv):
    names = list(SMALL_AT)

    def body(*refs):
        g_ref, ins, outs = refs[0], refs[1:1 + 3 * len(names)], refs[1 + 3 * len(names):]
        g = g_ref[0]
        for k in range(1, N_DEV):
            g = g + g_ref[k]
        for i, n in enumerate(names):
            r0, rows, l0, lanes = SMALL_AT[n]
            gi = jnp.concatenate([g[r:r + 1, l0:l0 + lanes] for r in range(r0, r0 + rows)], axis=1)
            w_ref, m_ref, v_ref = ins[3 * i:3 * i + 3]
            go, dl, m2, v2 = outs[4 * i:4 * i + 4]
            go[...] = gi
            dl[...], m2[...], v2[...] = _adamw(w_ref[...], gi, m_ref[...], v_ref[...])

    flat = [a for n in names for a in wmv[n]]
    res = pl.pallas_call(
        body, name="adamw_small",
        out_shape=[jax.ShapeDtypeStruct(wmv[n][0].shape, F32) for n in names for _ in range(4)],
        compiler_params=_params(),
    )(gathered, *flat)
    return {n: res[4 * i:4 * i + 4] for i, n in enumerate(names)}


def _row_tile(rows):
    return rows // 4 if rows >= 512 else rows


def _sum_partials(r_ref):
    g = r_ref[0].astype(F32)
    for k in range(1, N_DEV):
        g = g + r_ref[k].astype(F32)
    return g


def _adamw_sum(recv, w, m, v, name):
    r, cdim = w.shape
    tr = _row_tile(r)

    def body(r_ref, w_ref, m_ref, v_ref, g_ref, dl_ref, m2_ref, v2_ref):
        g = _sum_partials(r_ref)
        g_ref[...] = g
        dl_ref[...], m2_ref[...], v2_ref[...] = _adamw(w_ref[...], g, m_ref[...], v_ref[...])

    row = pl.BlockSpec((tr, cdim), lambda i: (i, 0))
    return pl.pallas_call(
        body, name=name, grid=(r // tr,),
        in_specs=[pl.BlockSpec((N_DEV, tr, cdim), lambda i: (0, i, 0)), row, row, row],
        out_specs=[row] * 4, out_shape=[jax.ShapeDtypeStruct((r, cdim), F32)] * 4,
        compiler_params=_params(("parallel",)),
    )(recv, w, m, v)


def _sum8(recv, name):
    _, r, cdim = recv.shape
    tr = _row_tile(r)

    def body(r_ref, g_ref):
        g_ref[...] = _sum_partials(r_ref)

    return pl.pallas_call(
        body, name=name, grid=(r // tr,),
        in_specs=[pl.BlockSpec((N_DEV, tr, cdim), lambda i: (0, i, 0))],
        out_specs=pl.BlockSpec((tr, cdim), lambda i: (i, 0)), out_shape=jax.ShapeDtypeStruct((r, cdim), F32),
        compiler_params=_params(("parallel",)),
    )(recv)


def _eye(n):
    return (lax.broadcasted_iota(jnp.int32, (n, n), 0) == lax.broadcasted_iota(jnp.int32, (n, n), 1)).astype(BF16)


def _transpose_to_bf16(w, name):
    r, c = w.shape

    def body(w_ref, o_ref):
        o_ref[...] = _dot_tn(w_ref[...].astype(BF16), _eye(r)).astype(BF16)

    return pl.pallas_call(body, name=name, out_shape=jax.ShapeDtypeStruct((c, r), BF16), compiler_params=_params())(w)


def _sum8_transposed(recv, name):
    _, r, c = recv.shape

    def body(r_ref, g_ref):
        eye = _eye(r)
        hi, mid, lo = _split3(_sum_partials(r_ref))
        g_ref[...] = _dot_tn(hi, eye) + _dot_tn(mid, eye) + _dot_tn(lo, eye)

    return pl.pallas_call(body, name=name, out_shape=jax.ShapeDtypeStruct((c, r), F32), compiler_params=_params())(recv)


def _adamw_plain(g, w, m, v, name):
    r, cdim = w.shape
    tr = _row_tile(r)

    def body(g_ref, w_ref, m_ref, v_ref, dl_ref, m2_ref, v2_ref):
        dl_ref[...], m2_ref[...], v2_ref[...] = _adamw(w_ref[...], g_ref[...], m_ref[...], v_ref[...])

    row = pl.BlockSpec((tr, cdim), lambda i: (i, 0))
    return pl.pallas_call(
        body, name=name, grid=(r // tr,), in_specs=[row] * 4, out_specs=[row] * 3,
        out_shape=[jax.ShapeDtypeStruct((r, cdim), F32)] * 3,
        compiler_params=_params(("parallel",)),
    )(g, w, m, v)


IN_SHARD = 609
IN_SHARD_PAD = 640
IF_AT = A_W + M_W


def _regrouped(u):
    return u if u < IF_AT + 8 else u + (IF_W - 8)


def _selection(k, rows, row0, transpose):
    shape = (rows, IN_SHARD_PAD) if transpose else (IN_SHARD_PAD, rows)
    l = lax.broadcasted_iota(jnp.int32, shape, 1 if transpose else 0)
    r = lax.broadcasted_iota(jnp.int32, shape, 0 if transpose else 1) + row0
    u = l + IN_SHARD * k
    ru = u + jnp.where(u >= IF_AT + 8, IF_W - 8, 0)
    return ((ru == r) & (l < IN_SHARD)).astype(BF16)


def _regroup_w_in(g):
    def body(g_ref, o_ref):
        for cb in range(CAT_W // LANES):
            r0 = cb * LANES
            acc = jnp.zeros((D_MODEL, LANES), F32)
            for k in range(N_DEV):
                lo, hi = _regrouped(IN_SHARD * k), _regrouped(IN_SHARD * k + IN_SHARD - 1)
                if hi >= r0 and lo < r0 + LANES:
                    acc = acc + _dot(g_ref[k], _selection(k, LANES, r0, False))
            o_ref[:, r0:r0 + LANES] = acc.astype(BF16)

    return pl.pallas_call(
        body, name="regroup_w_in", out_shape=jax.ShapeDtypeStruct((D_MODEL, CAT_W), BF16),
        compiler_params=_params(),
    )(g)


def _ungroup_w_in(g_parts):
    n = len(g_parts)

    def body(*refs):
        o_ref, g_ref = refs[n], refs[n + 1]
        at = 0
        for p in refs[:n]:
            g_ref[:, at:at + p.shape[1]] = p[...]
            at += p.shape[1]
        for k in range(N_DEV):
            lo, hi = _regrouped(IN_SHARD * k), _regrouped(IN_SHARD * k + IN_SHARD - 1)
            w0, w1 = lo // LANES * LANES, (hi // LANES + 1) * LANES
            o_ref[k] = _dot(g_ref[:, w0:w1], _selection(k, w1 - w0, w0, True)).astype(BF16)

    return pl.pallas_call(
        body, name="ungroup_w_in", out_shape=jax.ShapeDtypeStruct((N_DEV, D_MODEL, IN_SHARD_PAD), BF16),
        scratch_shapes=[pltpu.VMEM((D_MODEL, CAT_W), BF16)], compiler_params=_params(),
    )(*g_parts)


WEIGHT_NAMES = ("w_ada", "b_ada", "g_pre_mix", "g_post_mix", "w_in", "b_if", "conv_w", "conv_b", "attn_sinks",
                "mlstm_norm_w", "w_branch_attn", "w_branch_mlstm", "w_out", "g_pre_ffn", "g_post_ffn",
                "w_ffn_gate", "w_ffn_up", "w_ffn_down")


def kernel(x, c, positions, w_ada, b_ada, g_pre_mix, g_post_mix, w_in, b_if, conv_w, conv_b, attn_sinks, mlstm_norm_w, w_branch_attn, w_branch_mlstm, w_out, g_pre_ffn, g_post_ffn, w_ffn_gate, w_ffn_up, w_ffn_down, loss_target, m_w_ada, m_b_ada, m_g_pre_mix, m_g_post_mix, m_w_in, m_b_if, m_conv_w, m_conv_b, m_attn_sinks, m_mlstm_norm_w, m_w_branch_attn, m_w_branch_mlstm, m_w_out, m_g_pre_ffn, m_g_post_ffn, m_w_ffn_gate, m_w_ffn_up, m_w_ffn_down, v_w_ada, v_b_ada, v_g_pre_mix, v_g_post_mix, v_w_in, v_b_if, v_conv_w, v_conv_b, v_attn_sinks, v_mlstm_norm_w, v_w_branch_attn, v_w_branch_mlstm, v_w_out, v_g_pre_ffn, v_g_post_ffn, v_w_ffn_gate, v_w_ffn_up, v_w_ffn_down):
    given = dict(locals())
    W = {n: given[n][0] for n in WEIGHT_NAMES}
    M = {n: given["m_" + n][0] for n in WEIGHT_NAMES}
    V = {n: given["v_" + n][0] for n in WEIGHT_NAMES}
    me = 4 * lax.axis_index("x") + 2 * lax.axis_index("y") + lax.axis_index("c")

    ff_sh = D_FF // N_DEV
    g_in, g_conv, cg = _all_gather_hbm([jnp.pad(W["w_in"], ((0, 0), (0, IN_SHARD_PAD - IN_SHARD))).astype(BF16),
                                        jnp.pad(W["conv_w"], ((0, 4), (0, 0))), c.reshape(8, D_MODEL // 8)])

    cmat = cg.reshape(N_DEV, D_MODEL)
    ada_w = D_MODEL * 6 // N_DEV
    b_cols = lax.dynamic_slice(W["b_ada"], (me * ada_w,), (ada_w,)).reshape(1, ada_w)
    mod_part = _mod_partial(cmat, W["w_ada"], b_cols)
    mod_recv = _small_exchange(jnp.broadcast_to(mod_part[:, None, :], (N_DEV, 8, ada_w)), False, "scatter_mod")
    mod = mod_recv[:, 0, :].reshape(6, D_MODEL)

    st_b = _exchange_start([W["w_branch_attn"].astype(BF16), W["w_branch_mlstm"].astype(BF16), W["w_out"].astype(BF16),
                            _transpose_to_bf16(W["w_ffn_gate"], "transpose_w_ffn_gate"),
                            _transpose_to_bf16(W["w_ffn_up"], "transpose_w_ffn_up"), W["w_ffn_down"].astype(BF16)],
                           True, mod_recv, "gather_rest_start")
    cols = lambda g: g.transpose(1, 0, 2).reshape(g.shape[1], N_DEV * g.shape[2])
    gains = jnp.stack([W["g_pre_mix"], W["g_post_mix"], W["g_pre_ffn"], W["g_post_ffn"]])
    xs, tgt = x[0], loss_target[0]
    t = _tables(mod, gains, cols(g_conv)[0:4], W["conv_b"], W["b_if"], W["mlstm_norm_w"], positions[0])
    vecs = t["vecs"]
    t["vecs"] = _tie(vecs, st_b["token"], "tie_fwd")
    w_cat = _regroup_w_in(g_in)
    a = _mixer_fwd(xs, t, W["attn_sinks"], w_cat)
    g_ba, g_bm, g_out, g_gate, g_up, g_down = _exchange_wait(st_b, a["ym"], "gather_rest_wait")
    w_ba, w_bm, w_out = cols(g_ba), cols(g_bm), g_out.reshape(D_MODEL, D_MODEL)
    b = _ffn_part(xs, tgt, t, a, w_ba, w_bm, w_out, g_gate.reshape(D_FF, D_MODEL), g_up.reshape(D_FF, D_MODEL),
                  g_down.reshape(D_FF, D_MODEL))

    st_f = _exchange_start([b["g_w_gate"].reshape(N_DEV, ff_sh, D_MODEL), b["g_w_up"].reshape(N_DEV, ff_sh, D_MODEL),
                            b["g_w_down"].reshape(N_DEV, ff_sh, D_MODEL)], False, b["dx1"], "scatter_ffn_start")
    t["vecs"] = _tie(vecs, st_f["token"], "tie_bwd")
    cm = _mixer_bwd(b["dx1"], t, a, b, W["attn_sinks"], w_ba, w_bm, w_out)
    pieces = lambda g, n: g.reshape(g.shape[0], N_DEV, n).transpose(1, 0, 2)
    st_m = _exchange_start([_ungroup_w_in(cm["g_w_cat"]), pieces(cm["g_w_ba"], 128), pieces(cm["g_w_bm"], 128),
                            cm["g_w_out"].reshape(N_DEV, D_MODEL // N_DEV, D_MODEL),
                            jnp.pad(pieces(cm["dcw"][0:4], 128), ((0, 0), (0, 4), (0, 0)))], False, cm["dcw"],
                           "scatter_mixer_start")
    r_gate, r_up, r_down = _exchange_wait(st_f, st_m["token"], "scatter_ffn_wait")
    grad_x, acc_p = _pre_bwd(cm["dproj"], xs, b["dx1"], _tie(vecs, st_m["token"], "tie_pre_bwd"), w_cat)
    small = _small_grads(acc_p, b, cm)
    loss = b["loss"]

    big_out = [{} for _ in range(4)]

    def put(n, res):
        for k in range(4):
            big_out[k][n] = res[k][None]

    put("w_ffn_down", _adamw_sum(r_down, W["w_ffn_down"], M["w_ffn_down"], V["w_ffn_down"], "adamw_w_ffn_down"))
    for n, g in (("w_ffn_gate", _sum8_transposed(r_gate, "sum_w_ffn_gate")),
                 ("w_ffn_up", _sum8_transposed(r_up, "sum_w_ffn_up"))):
        put(n, [g] + list(_adamw_plain(g, W[n], M[n], V[n], "adamw_" + n)))

    sg = _small_exchange(_small_table({"b_ada": small["dmod"], "gains": small["dgains"], "conv_b": small["dconv_b"],
                                       "mlstm_norm_w": small["dnorm_w"], "b_if": small["db_if"],
                                       "attn_sinks": small["dsinks"]}), True, "gather_small")
    as_row = lambda a, n: jnp.pad(a, (0, SMALL_AT[n][3] * SMALL_AT[n][1] - a.shape[0]))[None]
    small_res = _adamw_small(sg, {n: [as_row(d[n], n) for d in (W, M, V)] for n in SMALL_AT})
    small_out = [{n: small_res[n][k][:, 0:W[n].shape[0]] for n in SMALL_AT} for k in range(4)]
    dmod_cols = lax.dynamic_slice(sg[:, 0:6, :].reshape(N_DEV, 6 * D_MODEL), (0, me * ada_w), (N_DEV, ada_w))
    ada_out = _adamw_w_ada(cmat, dmod_cols, W["w_ada"], M["w_ada"], V["w_ada"])

    r_in, r_ba, r_bm, r_out, r_conv = _exchange_wait(st_m, grad_x, "scatter_mixer_wait")
    for n, r in (("w_branch_attn", r_ba), ("w_branch_mlstm", r_bm), ("w_out", r_out)):
        put(n, _adamw_sum(r, W[n], M[n], V[n], "adamw_" + n))
    pad4 = lambda v: jnp.pad(v, ((0, 4), (0, 0)))
    put("conv_w", [o[0:4] for o in _adamw_sum(r_conv, pad4(W["conv_w"]), pad4(M["conv_w"]), pad4(V["conv_w"]),
                                                 "adamw_conv_w")])
    g = _sum8(r_in, "sum_w_in")[:, 0:IN_SHARD]
    put("w_in", [g] + list(_adamw_plain(g, W["w_in"], M["w_in"], V["w_in"], "adamw_w_in")))

    total = lax.psum(loss, ("x", "y", "c"))
    outs = [total, grad_x[None]]
    for k in range(4):
        for n in WEIGHT_NAMES:
            if n == "w_ada":
                outs.append(ada_out[k][None])
            elif n in big_out[k]:
                outs.append(big_out[k][n])
            else:
                outs.append(small_out[k][n])
    return tuple(outs)
```

```python
import functools

import jax
import jax.numpy as jnp
import numpy as np
from jax import lax
from jax.experimental import pallas as pl
from jax.experimental.pallas import tpu as pltpu

F32 = jnp.float32
BF16 = jnp.bfloat16

N_DEV = 8
D_MODEL = 1024
D_FF = 2816
N_Q_HEADS = 8
HEAD_DIM = 64
ATTN_BLOCK = 128
ROPE_THETA = 10000.0
MLSTM_HEADS = 4
MLSTM_HEAD_DIM = 128
MLSTM_CHUNK = 128
NORM_EPS = 1e-6
ADAM_LR = 0.001
ADAM_B1 = 0.9
ADAM_B2 = 0.999
ADAM_EPS = 1e-08
ADAM_WD = 0.01
ADAM_STEP = 10

ROW_TILE = 256
WIDE_TILE = 512
LANES = 128
NEG = -1e30
VMEM_LIMIT = 56 * 1024 * 1024

A_W = 768
M_W = 2048
IF_W = 128
G_W = 2048
CAT_W = A_W + M_W + IF_W + G_W

R_SHIFT_M, R_SCALE_M, R_GATE_M, R_SHIFT_F, R_SCALE_F, R_GATE_F = 0, 1, 2, 3, 4, 5
R_G_PRE_MIX, R_G_POST_MIX, R_G_PRE_FFN, R_G_POST_FFN = 6, 7, 8, 9


def _dot(a, b):
    return jnp.dot(a, b, preferred_element_type=F32)


def _dot_nt(a, b):
    return lax.dot_general(a, b, (((1,), (1,)), ((), ())), preferred_element_type=F32)


def _dot_tn(a, b):
    return lax.dot_general(a, b, (((0,), (0,)), ((), ())), preferred_element_type=F32)


def _recip(x):
    return 1.0 / x


def _sigmoid(x):
    return _recip(1.0 + jnp.exp(-x))


def _colsum(x):
    return jnp.sum(x, axis=0, keepdims=True)


def _rowmean(x):
    return jnp.mean(x, axis=-1, keepdims=True)


def _params(sem=None, vmem=VMEM_LIMIT):
    kw = dict(vmem_limit_bytes=vmem)
    if sem is not None:
        kw["dimension_semantics"] = sem
    return pltpu.CompilerParams(**kw)


def _full(shape):
    nd = len(shape)
    return pl.BlockSpec(shape, lambda *_: (0,) * nd)


def _pre_proj(x, vecs, w_cat):
    S = x.shape[0]
    tm = WIDE_TILE

    def body(x_ref, v_ref, w_ref, h_ref, pa_ref, pm_ref, pif_ref, pg_ref):
        xv = x_ref[...]
        r = lax.rsqrt(_rowmean(xv * xv) + NORM_EPS)
        h = (xv * r * v_ref[R_G_PRE_MIX:R_G_PRE_MIX + 1, :]) * (1.0 + v_ref[R_SCALE_M:R_SCALE_M + 1, :]) \
            + v_ref[R_SHIFT_M:R_SHIFT_M + 1, :]
        hb = h.astype(BF16)
        h_ref[...] = hb
        pa_ref[...] = _dot(hb, w_ref[:, 0:A_W])
        pm_ref[...] = _dot(hb, w_ref[:, A_W:A_W + M_W])
        pif_ref[...] = _dot(hb, w_ref[:, A_W + M_W:A_W + M_W + IF_W])
        pg_ref[...] = _dot(hb, w_ref[:, A_W + M_W + IF_W:CAT_W]).astype(BF16)

    row = lambda w: pl.BlockSpec((tm, w), lambda i: (i, 0))
    return pl.pallas_call(
        body, name="pre_proj", grid=(S // tm,),
        in_specs=[row(D_MODEL), _full(vecs.shape), _full(w_cat.shape)],
        out_specs=[row(D_MODEL), row(A_W), row(M_W), row(IF_W), row(G_W)],
        out_shape=[jax.ShapeDtypeStruct((S, D_MODEL), BF16), jax.ShapeDtypeStruct((S, A_W), F32),
                   jax.ShapeDtypeStruct((S, M_W), F32), jax.ShapeDtypeStruct((S, IF_W), F32),
                   jax.ShapeDtypeStruct((S, G_W), BF16)],
        compiler_params=_params(("parallel",)),
    )(x, vecs, w_cat)


def _mix_fwd(x, ya, ym, pg, vecs, w_ba, w_bm, w_out):
    S = x.shape[0]
    tm = WIDE_TILE

    def body(x_ref, ya_ref, ym_ref, pg_ref, v_ref, wba_ref, wbm_ref, wout_ref,
             x1_ref, merged_ref, mix_ref, pa_ref, pb_ref):
        pa = _dot(ya_ref[...], wba_ref[...])
        pb = _dot(ym_ref[...], wbm_ref[...])
        merged = _sigmoid(pg_ref[:, 0:D_MODEL].astype(F32)) * pa + _sigmoid(pg_ref[:, D_MODEL:G_W].astype(F32)) * pb
        mb = merged.astype(BF16)
        mix = _dot(mb, wout_ref[...])
        r = lax.rsqrt(_rowmean(mix * mix) + NORM_EPS)
        x1_ref[...] = x_ref[...] + v_ref[R_GATE_M:R_GATE_M + 1, :] * (mix * r * v_ref[R_G_POST_MIX:R_G_POST_MIX + 1, :])
        merged_ref[...] = mb
        mix_ref[...] = mix
        pa_ref[...] = pa.astype(BF16)
        pb_ref[...] = pb.astype(BF16)

    row = lambda w: pl.BlockSpec((tm, w), lambda i: (i, 0))
    sd = lambda w, dt: jax.ShapeDtypeStruct((S, w), dt)
    return pl.pallas_call(
        body, name="mix_fwd", grid=(S // tm,),
        in_specs=[row(D_MODEL), row(512), row(512), row(G_W), _full(vecs.shape), _full(w_ba.shape),
                  _full(w_bm.shape), _full(w_out.shape)],
        out_specs=[row(D_MODEL)] * 5,
        out_shape=[sd(D_MODEL, F32), sd(D_MODEL, BF16), sd(D_MODEL, F32), sd(D_MODEL, BF16), sd(D_MODEL, BF16)],
        compiler_params=_params(("parallel",)),
    )(x, ya, ym, pg, vecs, w_ba, w_bm, w_out)


def _ffn_fwd_bwd(x1, tgt, vecs, w_gate, w_up, w_down):
    S = x1.shape[0]
    tm = ROW_TILE

    def body(x1_ref, tgt_ref, v_ref, wg_hbm, wu_hbm, wd_hbm,
             dx1_ref, h2_ref, hid_ref, da_ref, du_ref, dff_ref, acc_ref, loss_ref,
             wg, wu, wd, sem):
        i = pl.program_id(0)

        @pl.when(i == 0)
        def _():
            cps = [pltpu.make_async_copy(wg_hbm, wg, sem.at[0]), pltpu.make_async_copy(wu_hbm, wu, sem.at[1]),
                   pltpu.make_async_copy(wd_hbm, wd, sem.at[2])]
            for cp in cps:
                cp.start()
            for cp in cps:
                cp.wait()
            acc_ref[...] = jnp.zeros_like(acc_ref)
            loss_ref[...] = jnp.zeros_like(loss_ref)

        vrow = lambda r: v_ref[r:r + 1, :]
        x1v = x1_ref[...]
        r3 = lax.rsqrt(_rowmean(x1v * x1v) + NORM_EPS)
        x1hat = x1v * r3
        xn3 = x1hat * vrow(R_G_PRE_FFN)
        h2b = (xn3 * (1.0 + vrow(R_SCALE_F)) + vrow(R_SHIFT_F)).astype(BF16)
        h2_ref[...] = h2b
        a = _dot_nt(h2b, wg[...])
        u = _dot_nt(h2b, wu[...])
        sg = _sigmoid(a)
        sil = a * sg
        hidb = (sil * u).astype(BF16)
        hid_ref[...] = hidb
        ff = _dot(hidb, wd[...])
        r4 = lax.rsqrt(_rowmean(ff * ff) + NORM_EPS)
        ffhat = ff * r4
        n4 = ffhat * vrow(R_G_POST_FFN)
        err = x1v + vrow(R_GATE_F) * n4 - tgt_ref[...]
        loss_ref[...] += jnp.sum(err * err) * (0.5 / D_MODEL)
        dy = err * (1.0 / D_MODEL)
        acc_ref[0:1, :] += _colsum(dy * n4)
        dn4 = dy * vrow(R_GATE_F)
        acc_ref[1:2, :] += _colsum(dn4 * ffhat)
        dffhat = dn4 * vrow(R_G_POST_FFN)
        dffb = (r4 * (dffhat - ffhat * _rowmean(dffhat * ffhat))).astype(BF16)
        dff_ref[...] = dffb
        dhid = _dot_nt(dffb, wd[...])
        dub = (dhid * sil).astype(BF16)
        dab = (dhid * u * (sg * (1.0 + a * (1.0 - sg)))).astype(BF16)
        da_ref[...] = dab
        du_ref[...] = dub
        dh2 = _dot(dab, wg[...]) + _dot(dub, wu[...])
        acc_ref[2:3, :] += _colsum(dh2 * xn3)
        acc_ref[3:4, :] += _colsum(dh2)
        dxn3 = dh2 * (1.0 + vrow(R_SCALE_F))
        acc_ref[4:5, :] += _colsum(dxn3 * x1hat)
        dx1hat = dxn3 * vrow(R_G_PRE_FFN)
        dx1_ref[...] = dy + r3 * (dx1hat - x1hat * _rowmean(dx1hat * x1hat))

    row = lambda w: pl.BlockSpec((tm, w), lambda i: (i, 0))
    sd = lambda w, dt: jax.ShapeDtypeStruct((S, w), dt)
    anyspec = pl.BlockSpec(memory_space=pl.ANY)
    return pl.pallas_call(
        body, name="ffn_fwd_bwd", grid=(S // tm,),
        in_specs=[row(D_MODEL), row(D_MODEL), _full(vecs.shape), anyspec, anyspec, anyspec],
        out_specs=[row(D_MODEL), row(D_MODEL), row(D_FF), row(D_FF), row(D_FF), row(D_MODEL),
                   _full((8, D_MODEL)), _full((8, LANES))],
        out_shape=[sd(D_MODEL, F32), sd(D_MODEL, BF16), sd(D_FF, BF16), sd(D_FF, BF16), sd(D_FF, BF16),
                   sd(D_MODEL, BF16), jax.ShapeDtypeStruct((8, D_MODEL), F32), jax.ShapeDtypeStruct((8, LANES), F32)],
        scratch_shapes=[pltpu.VMEM(w_gate.shape, BF16), pltpu.VMEM(w_up.shape, BF16), pltpu.VMEM(w_down.shape, BF16),
                        pltpu.SemaphoreType.DMA((3,))],
        compiler_params=_params(("arbitrary",)),
    )(x1, tgt, vecs, w_gate, w_up, w_down)


def _mix_bwd(dx1, mix, pa, pb, pg, vecs, w_ba, w_bm, w_out):
    S = dx1.shape[0]
    tm = WIDE_TILE

    def body(dx1_ref, mix_ref, pa_ref, pb_ref, pg_ref, v_ref, wba_ref, wbm_ref, wout_ref,
             dmix_ref, dpa_ref, dpb_ref, dg_ref, dya_ref, dym_ref, acc_ref):
        i = pl.program_id(0)

        @pl.when(i == 0)
        def _():
            acc_ref[...] = jnp.zeros_like(acc_ref)

        vrow = lambda r: v_ref[r:r + 1, :]
        dx1v = dx1_ref[...]
        mix = mix_ref[...]
        r2 = lax.rsqrt(_rowmean(mix * mix) + NORM_EPS)
        mixhat = mix * r2
        acc_ref[0:1, :] += _colsum(dx1v * (mixhat * vrow(R_G_POST_MIX)))
        dn2 = dx1v * vrow(R_GATE_M)
        acc_ref[1:2, :] += _colsum(dn2 * mixhat)
        dmixhat = dn2 * vrow(R_G_POST_MIX)
        dmixb = (r2 * (dmixhat - mixhat * _rowmean(dmixhat * mixhat))).astype(BF16)
        dmix_ref[...] = dmixb
        dmerged = _dot_nt(dmixb, wout_ref[...])
        sa = _sigmoid(pg_ref[:, 0:D_MODEL].astype(F32))
        sm = _sigmoid(pg_ref[:, D_MODEL:G_W].astype(F32))
        dpab = (dmerged * sa).astype(BF16)
        dpbb = (dmerged * sm).astype(BF16)
        dpa_ref[...] = dpab
        dpb_ref[...] = dpbb
        dg_ref[:, 0:D_MODEL] = (dmerged * pa_ref[...].astype(F32) * (sa * (1.0 - sa))).astype(BF16)
        dg_ref[:, D_MODEL:G_W] = (dmerged * pb_ref[...].astype(F32) * (sm * (1.0 - sm))).astype(BF16)
        dya_ref[...] = _dot_nt(dpab, wba_ref[...])
        dym_ref[...] = _dot_nt(dpbb, wbm_ref[...])

    row = lambda w: pl.BlockSpec((tm, w), lambda i: (i, 0))
    sd = lambda w, dt: jax.ShapeDtypeStruct((S, w), dt)
    return pl.pallas_call(
        body, name="mix_bwd", grid=(S // tm,),
        in_specs=[row(D_MODEL), row(D_MODEL), row(D_MODEL), row(D_MODEL), row(G_W), _full(vecs.shape),
                  _full(w_ba.shape), _full(w_bm.shape), _full(w_out.shape)],
        out_specs=[row(D_MODEL), row(D_MODEL), row(D_MODEL), row(G_W), row(512), row(512), _full((8, D_MODEL))],
        out_shape=[sd(D_MODEL, BF16), sd(D_MODEL, BF16), sd(D_MODEL, BF16), sd(G_W, BF16), sd(512, F32), sd(512, F32),
                   jax.ShapeDtypeStruct((8, D_MODEL), F32)],
        compiler_params=_params(("arbitrary",)),
    )(dx1, mix, pa, pb, pg, vecs, w_ba, w_bm, w_out)


def _pre_bwd(pieces, x, dx1, vecs, w_cat):
    S = x.shape[0]
    tm = WIDE_TILE
    n = len(pieces)
    starts = [sum(p.shape[1] for p in pieces[:k]) for k in range(n + 1)]

    def body(*refs):
        p_refs = refs[:n]
        x_ref, dx1_ref, v_ref, w_ref, dx_ref, acc_ref = refs[n:]
        i = pl.program_id(0)

        @pl.when(i == 0)
        def _():
            acc_ref[...] = jnp.zeros_like(acc_ref)

        vrow = lambda r: v_ref[r:r + 1, :]
        dh = _dot_nt(p_refs[0][...], w_ref[:, starts[0]:starts[1]])
        for k in range(1, n):
            dh = dh + _dot_nt(p_refs[k][...], w_ref[:, starts[k]:starts[k + 1]])
        xv = x_ref[...]
        r1 = lax.rsqrt(_rowmean(xv * xv) + NORM_EPS)
        xhat = xv * r1
        acc_ref[0:1, :] += _colsum(dh * (xhat * vrow(R_G_PRE_MIX)))
        acc_ref[1:2, :] += _colsum(dh)
        dxn = dh * (1.0 + vrow(R_SCALE_M))
        acc_ref[2:3, :] += _colsum(dxn * xhat)
        dxhat = dxn * vrow(R_G_PRE_MIX)
        dx_ref[...] = dx1_ref[...] + r1 * (dxhat - xhat * _rowmean(dxhat * xhat))

    row = lambda w: pl.BlockSpec((tm, w), lambda i: (i, 0))
    return pl.pallas_call(
        body, name="pre_bwd", grid=(S // tm,),
        in_specs=[row(p.shape[1]) for p in pieces] + [row(D_MODEL), row(D_MODEL), _full(vecs.shape), _full(w_cat.shape)],
        out_specs=[row(D_MODEL), _full((8, D_MODEL))],
        out_shape=[jax.ShapeDtypeStruct((S, D_MODEL), F32), jax.ShapeDtypeStruct((8, D_MODEL), F32)],
        compiler_params=_params(("arbitrary",)),
    )(*pieces, x, dx1, vecs, w_cat)


def _matmul_tn(a, b, tn, name, ts=1024):
    S, K = a.shape
    N = b.shape[1]
    n_s = S // ts

    def body(a_ref, b_ref, o_ref, acc_ref):
        s = pl.program_id(1)

        @pl.when(s == 0)
        def _():
            acc_ref[...] = jnp.zeros_like(acc_ref)

        acc_ref[...] += _dot_tn(a_ref[...], b_ref[...])

        @pl.when(s == n_s - 1)
        def _():
            o_ref[...] = acc_ref[...].astype(BF16)

    return pl.pallas_call(
        body, name=name, grid=(N // tn, n_s),
        in_specs=[pl.BlockSpec((ts, K), lambda j, s: (s, 0)), pl.BlockSpec((ts, tn), lambda j, s: (s, j))],
        out_specs=pl.BlockSpec((K, tn), lambda j, s: (0, j)),
        out_shape=jax.ShapeDtypeStruct((K, N), BF16),
        scratch_shapes=[pltpu.VMEM((K, tn), F32)],
        compiler_params=_params(("parallel", "arbitrary")),
    )(a, b)


def _rope_swap(t):
    lane = lax.broadcasted_iota(jnp.int32, t.shape, 1)
    first = (lane & (HEAD_DIM - 1)) < (HEAD_DIM // 2)
    return jnp.where(first, pltpu.roll(t, LANES - HEAD_DIM // 2, 1), pltpu.roll(t, HEAD_DIM // 2, 1))


def _rope(t, cos, sin_signed):
    return t * cos + _rope_swap(t) * sin_signed


def _rope_t(d, cos, sin_signed):
    return d * cos + _rope_swap(d * sin_signed)


def _to_kv_lanes(chunk, p, h):
    lane = lax.broadcasted_iota(jnp.int32, chunk.shape, 1)
    src = chunk if p == h else pltpu.roll(chunk, HEAD_DIM, 1)
    return jnp.where((lane >> 6) == h, src, jnp.zeros_like(src))


def _from_kv_lanes(o_a, o_b, h):
    lane = lax.broadcasted_iota(jnp.int32, o_a.shape, 1)
    a = o_a if h == 0 else pltpu.roll(o_a, HEAD_DIM, 1)
    b = o_b if h == 1 else pltpu.roll(o_b, HEAD_DIM, 1)
    return jnp.where(lane < HEAD_DIM, a, b)


def _band_bias(n):
    blk = ATTN_BLOCK
    qi = lax.broadcasted_iota(jnp.int32, (blk, 2 * blk), 0)
    kj = lax.broadcasted_iota(jnp.int32, (blk, 2 * blk), 1)
    seen = (kj > qi) & (kj <= qi + blk) & ((n > 0) | (kj >= blk))
    return jnp.concatenate([jnp.where(seen, 0.0, NEG)] * N_Q_HEADS, axis=0)


def _stack_heads(chunks, h, dtype):
    parts = []
    for g in range(4):
        j = 4 * h + g
        parts.append(_to_kv_lanes(chunks[j // 2], j % 2, h))
    return jnp.concatenate(parts, axis=0).astype(dtype)


def _fused_call(parts, name, n_steps):
    counts = [(len(p["in_specs"]), len(p["out_specs"]), len(p["scratch"])) for p in parts]
    n_in, n_out = sum(c[0] for c in counts), sum(c[1] for c in counts)

    def kernel_fn(*refs):
        i = pl.program_id(0)
        groups, a, b, c = [], 0, n_in, n_in + n_out
        for ci, co, cs in counts:
            groups.append(refs[a:a + ci] + refs[b:b + co] + refs[c:c + cs])
            a, b, c = a + ci, b + co, c + cs
        for p, g in zip(parts, groups):
            p["init"](i, *g)
        for p, g in zip(parts, groups):
            p["body"](i, *g)

    flat = lambda key: [v for p in parts for v in p[key]]
    res = pl.pallas_call(
        kernel_fn, name=name, grid=(n_steps,), in_specs=flat("in_specs"), out_specs=flat("out_specs"),
        out_shape=flat("out_shape"), scratch_shapes=flat("scratch"), compiler_params=_params(("arbitrary",)),
    )(*flat("operands"))
    out, pos = [], 0
    for _, co, _ in counts:
        out.append(res[pos:pos + co])
        pos += co
    return out


def _attn_fwd_part(pa, cos, sin, sinks):
    S = pa.shape[0]
    blk = ATTN_BLOCK
    nb = S // blk

    def body(n, sink_ref, cur_ref, prev_ref, cos_ref, sin_ref, cosp_ref, sinp_ref,
             ya_ref, qr_ref, kr_ref, vb_ref, lse_ref):
        cos_c, sin_c = cos_ref[...], sin_ref[...]
        qch = [_rope(cur_ref[:, c * LANES:(c + 1) * LANES], cos_c, sin_c) * (HEAD_DIM ** -0.5) for c in range(4)]
        for c in range(4):
            qr_ref[:, c * LANES:(c + 1) * LANES] = qch[c].astype(BF16)
        k_cur = _rope(cur_ref[:, 512:640], cos_c, sin_c).astype(BF16)
        k_prev = _rope(prev_ref[:, 0:LANES], cosp_ref[...], sinp_ref[...]).astype(BF16)
        v_cur = cur_ref[:, 640:768].astype(BF16)
        v_prev = prev_ref[:, LANES:2 * LANES].astype(BF16)
        kr_ref[...] = k_cur
        vb_ref[...] = v_cur
        K = jnp.concatenate([k_prev, k_cur], axis=0)
        V = jnp.concatenate([v_prev, v_cur], axis=0)
        lane = lax.broadcasted_iota(jnp.int32, (blk, LANES), 1)
        s = jnp.concatenate([_dot_nt(_stack_heads(qch, h, BF16), K) for h in range(2)], axis=0)
        s = s + _band_bias(n)
        rowmax = jnp.max(s, axis=1, keepdims=True)
        hd = lambda x, j: x[j * blk:(j + 1) * blk]
        m = jnp.concatenate([jnp.maximum(hd(rowmax, j), sink_ref[j]) for j in range(N_Q_HEADS)], axis=0)
        p = jnp.exp(s - m)
        pb = p.astype(BF16)
        den = _rowsum_mxu(pb) + jnp.concatenate([jnp.exp(sink_ref[j] - hd(m, j)) for j in range(N_Q_HEADS)], axis=0)
        o = jnp.concatenate([_dot(pb[4 * h * blk:4 * (h + 1) * blk], V) for h in range(2)], axis=0) * _recip(den)
        lse = m + jnp.log(den)
        outs = [o[j * blk:(j + 1) * blk, :] for j in range(N_Q_HEADS)]
        lse_tile = jnp.zeros((blk, LANES), F32)
        for j in range(N_Q_HEADS):
            lse_tile = jnp.where(lane == j, lse[j * blk:(j + 1) * blk, :], lse_tile)
        for c in range(4):
            ya_ref[:, c * LANES:(c + 1) * LANES] = _from_kv_lanes(outs[2 * c], outs[2 * c + 1], c // 2).astype(BF16)
        lse_ref[...] = lse_tile

    prev = lambda n: jnp.maximum(n - 1, 0)
    sd = lambda w, dt: jax.ShapeDtypeStruct((S, w), dt)
    return dict(
        init=lambda n, *refs: None, body=body, scratch=[], operands=[sinks, pa, pa, cos, sin, cos, sin],
        in_specs=[pl.BlockSpec(memory_space=pltpu.SMEM),
                  pl.BlockSpec((blk, A_W), lambda n: (n, 0)),
                  pl.BlockSpec((blk, 256), lambda n: (prev(n), 2)),
                  pl.BlockSpec((blk, LANES), lambda n: (n, 0)), pl.BlockSpec((blk, LANES), lambda n: (n, 0)),
                  pl.BlockSpec((blk, LANES), lambda n: (prev(n), 0)), pl.BlockSpec((blk, LANES), lambda n: (prev(n), 0))],
        out_specs=[pl.BlockSpec((blk, 512), lambda n: (n, 0)), pl.BlockSpec((blk, 512), lambda n: (n, 0)),
                   pl.BlockSpec((blk, LANES), lambda n: (n, 0)), pl.BlockSpec((blk, LANES), lambda n: (n, 0)),
                   pl.BlockSpec((blk, LANES), lambda n: (n, 0))],
        out_shape=[sd(512, BF16), sd(512, BF16), sd(LANES, BF16), sd(LANES, BF16), sd(LANES, F32)])


def _attn_bwd_part(dya, qr, kr, vb, lse, cos, sin, sinks):
    S = dya.shape[0]
    blk = ATTN_BLOCK
    nb = S // blk

    def init(n, sink_ref, dya_ref, qr_ref, kc_ref, kp_ref, vc_ref, vp_ref, lse_ref, cos_ref, sin_ref, cosp_ref, sinp_ref,
             dq_ref, dkv_ref, last_ref, dsink_ref, ck, cv):
        @pl.when(n == 0)
        def _():
            ck[...] = jnp.zeros_like(ck)
            cv[...] = jnp.zeros_like(cv)
            dsink_ref[...] = jnp.zeros_like(dsink_ref)

    def body(n, sink_ref, dya_ref, qr_ref, kc_ref, kp_ref, vc_ref, vp_ref, lse_ref, cos_ref, sin_ref, cosp_ref, sinp_ref,
             dq_ref, dkv_ref, last_ref, dsink_ref, ck, cv):
        K = jnp.concatenate([kp_ref[...], kc_ref[...]], axis=0)
        V = jnp.concatenate([vp_ref[...], vc_ref[...]], axis=0)
        qch = [qr_ref[:, c * LANES:(c + 1) * LANES] for c in range(4)]
        dch = [dya_ref[:, c * LANES:(c + 1) * LANES] for c in range(4)]
        lse_tile = lse_ref[...]
        lane8 = lax.broadcasted_iota(jnp.int32, (8, LANES), 1)
        grp = lambda x, h: x[4 * h * blk:4 * (h + 1) * blk]
        qs = jnp.concatenate([_stack_heads(qch, h, BF16) for h in range(2)], axis=0)
        dos = jnp.concatenate([_stack_heads(dch, h, BF16) for h in range(2)], axis=0)
        lse_col = jnp.concatenate([lse_tile[:, j:j + 1] for j in range(N_Q_HEADS)], axis=0)
        s = jnp.concatenate([_dot_nt(grp(qs, h), K) for h in range(2)], axis=0)
        p = jnp.exp(s + _band_bias(n) - lse_col)
        dp = jnp.concatenate([_dot_nt(grp(dos, h), V) for h in range(2)], axis=0)
        delta = jnp.sum(p * dp, axis=1, keepdims=True)
        dsb = (p * (dp - delta)).astype(BF16)
        pb = p.astype(BF16)
        dq = jnp.concatenate([_dot(grp(dsb, h), K) for h in range(2)], axis=0)
        dk_acc = _dot_tn(grp(dsb, 0), grp(qs, 0)) + _dot_tn(grp(dsb, 1), grp(qs, 1))
        dv_acc = _dot_tn(grp(pb, 0), grp(dos, 0)) + _dot_tn(grp(pb, 1), grp(dos, 1))
        dqs = [dq[j * blk:(j + 1) * blk, :] for j in range(N_Q_HEADS)]
        dsink = jnp.zeros((8, LANES), F32)
        for j in range(N_Q_HEADS):
            rows = slice(j * blk, (j + 1) * blk)
            ps_delta = jnp.exp(sink_ref[j] - lse_col[rows]) * delta[rows]
            dsink = jnp.where(lane8 == j, dsink - jnp.sum(ps_delta), dsink)
        dsink_ref[...] += dsink
        cos_c, sin_c = cos_ref[...], sin_ref[...]
        for c in range(4):
            dqc = _from_kv_lanes(dqs[2 * c], dqs[2 * c + 1], c // 2) * (HEAD_DIM ** -0.5)
            dq_ref[:, c * LANES:(c + 1) * LANES] = _rope_t(dqc, cos_c, sin_c).astype(BF16)
        dkv_ref[:, 0:LANES] = _rope_t(dk_acc[0:blk, :] + ck[...], cosp_ref[...], sinp_ref[...]).astype(BF16)
        dkv_ref[:, LANES:2 * LANES] = (dv_acc[0:blk, :] + cv[...]).astype(BF16)
        ck[...] = dk_acc[blk:2 * blk, :]
        cv[...] = dv_acc[blk:2 * blk, :]
        last_ref[:, 0:LANES] = _rope_t(dk_acc[blk:2 * blk, :], cos_c, sin_c).astype(BF16)
        last_ref[:, LANES:2 * LANES] = dv_acc[blk:2 * blk, :].astype(BF16)

    prev = lambda n: jnp.maximum(n - 1, 0)
    same = lambda n: n
    bs = lambda w, f: pl.BlockSpec((blk, w), lambda n: (f(n), 0))
    return dict(
        init=init, body=body, operands=[sinks, dya, qr, kr, kr, vb, vb, lse, cos, sin, cos, sin],
        in_specs=[pl.BlockSpec(memory_space=pltpu.SMEM),
                  bs(512, same), bs(512, same), bs(LANES, same), bs(LANES, prev), bs(LANES, same), bs(LANES, prev),
                  bs(LANES, same), bs(LANES, same), bs(LANES, same), bs(LANES, prev), bs(LANES, prev)],
        out_specs=[bs(512, same), bs(256, prev), _full((blk, 256)), _full((8, LANES))],
        out_shape=[jax.ShapeDtypeStruct((S, 512), BF16), jax.ShapeDtypeStruct((S, 256), BF16),
                   jax.ShapeDtypeStruct((blk, 256), BF16), jax.ShapeDtypeStruct((8, LANES), F32)],
        scratch=[pltpu.VMEM((blk, LANES), F32), pltpu.VMEM((blk, LANES), F32)])


def _split3(x):
    hi = x.astype(BF16)
    r1 = x - hi.astype(F32)
    mid = r1.astype(BF16)
    lo = (r1 - mid.astype(F32)).astype(BF16)
    return hi, mid, lo


def _tri_matmul(tri_b, x):
    hi, mid, lo = _split3(x)
    return _dot(tri_b, hi) + _dot(tri_b, mid) + _dot(tri_b, lo)


def _log_sigmoid(x):
    return jnp.minimum(x, 0.0) - jnp.log(1.0 + jnp.exp(-jnp.abs(x)))


def _shift_rows(cur, seam, k, down):
    L = cur.shape[0]
    row8 = lax.broadcasted_iota(jnp.int32, seam.shape, 0)
    if down:
        mixed = jnp.concatenate([cur[:L - 8], jnp.where(row8 >= 8 - k, seam, cur[L - 8:])], axis=0)
        return pltpu.roll(mixed, k, 0)
    mixed = jnp.concatenate([jnp.where(row8 < k, seam, cur[:8]), cur[8:]], axis=0)
    return pltpu.roll(mixed, L - k, 0)


def _conv_fwd(cur, tail, cw_ref):
    z = cw_ref[4:5, :]
    for k in range(3, 0, -1):
        z = z + _shift_rows(cur, tail, k, True) * cw_ref[3 - k:4 - k, :]
    return z + cur * cw_ref[3:4, :]


def _rowsum_mxu(x, two_pass=False):
    ones = jnp.ones((x.shape[1], LANES), BF16)
    hi = x.astype(BF16)
    s = _dot(hi, ones)
    if two_pass:
        s = s + _dot((x - hi.astype(F32)).astype(BF16), ones)
    return s


def _stack(f):
    return jnp.concatenate([f(h) for h in range(MLSTM_HEADS)], axis=0)


def _head(x, h):
    L = x.shape[0] // MLSTM_HEADS
    return x[h * L:(h + 1) * L]


def _mlstm_heads_fwd(qk, cur_ref, gt, b_all, c_prev, nmv, tri, eye):
    L = qk.shape[0]
    HD = MLSTM_HEAD_DIM
    W4 = MLSTM_HEADS * HD
    col2row = lambda x: jnp.sum(jnp.where(eye, x, 0.0), axis=0, keepdims=True)
    b_col = _stack(lambda h: b_all[:, 4 + h:5 + h])
    i_col = _stack(lambda h: gt[:, h:h + 1])
    b_row = _stack(lambda h: jnp.broadcast_to(col2row(b_all[:, 4 + h:5 + h]), (L, L)))
    i_row = _stack(lambda h: jnp.broadcast_to(col2row(gt[:, h:h + 1]), (L, L)))
    bl = _stack(lambda h: jnp.broadcast_to(b_all[L - 1:L, 4 + h:5 + h], (L, 1)))
    m_prev = _stack(lambda h: jnp.broadcast_to(nmv[4 + h:5 + h, 0:1], (L, 1)))
    n_prev = _stack(lambda h: jnp.broadcast_to(nmv[h:h + 1, :], (L, HD)))
    tri4 = jnp.concatenate([tri] * MLSTM_HEADS, axis=0)
    Dm = jnp.where(tri4, b_col - b_row + i_row, NEG)
    inter = b_col + m_prev
    m_t = jnp.maximum(inter, jnp.max(Dm, axis=1, keepdims=True))
    W = jnp.exp(Dm - m_t)
    e_t = jnp.exp(inter - m_t)
    q = _stack(lambda h: qk[:, h * HD:(h + 1) * HD])
    k = _stack(lambda h: qk[:, W4 + h * HD:W4 + (h + 1) * HD]) * (HD ** -0.5)
    v = _stack(lambda h: cur_ref[:, 2 * W4 + h * HD:2 * W4 + (h + 1) * HD])
    qb, kb, vb = q.astype(BF16), k.astype(BF16), v.astype(BF16)
    Sc = _stack(lambda h: _dot_nt(_head(qb, h), _head(kb, h))) * W
    Scb = Sc.astype(BF16)
    cb = [c.astype(BF16) for c in c_prev]
    P1 = _stack(lambda h: _dot(_head(qb, h), cb[h]))
    num = _stack(lambda h: _dot(_head(Scb, h), _head(vb, h))) + e_t * P1
    qn = _rowsum_mxu(q * n_prev)
    den = _rowsum_mxu(Scb) + e_t * qn
    floor = jnp.broadcast_to(jnp.exp(-m_t), den.shape)
    inv_g = _recip(jnp.maximum(jnp.abs(den), floor))
    hv = num * inv_g
    a_col = bl - b_col + i_col
    a_max = _stack(lambda h: jnp.broadcast_to(jnp.max(_head(a_col, h), axis=0, keepdims=True), (L, 1)))
    m_new = jnp.maximum(bl + m_prev, a_max)
    dec = jnp.exp(bl + m_prev - m_new)
    u_col = jnp.exp(a_col - m_new)
    return dict(W=W, e_t=e_t, q=q, k=k, v=v, qb=qb, kb=kb, vb=vb, cb=cb, Sc=Sc, Scb=Scb, P1=P1, qn=qn, den=den,
                floor=floor, inv_g=inv_g, hv=hv, n_prev=n_prev, m_new=m_new, dec=dec, u_col=u_col)


def _mlstm_fwd_part(pm, pif, cw, sv):
    S = pm.shape[0]
    L = MLSTM_CHUNK
    nc = S // L
    HD = MLSTM_HEAD_DIM
    W4 = MLSTM_HEADS * HD

    def init(c, cur_ref, pif_ref, cw_ref, sv_ref, ym_ref, z_ref, cst_ref, nst_ref, C, nm, tail):
        @pl.when(c == 0)
        def _():
            C[...] = jnp.zeros_like(C)
            nm[...] = jnp.zeros_like(nm)
            tail[...] = jnp.zeros_like(tail)

    def body(c, cur_ref, pif_ref, cw_ref, sv_ref, ym_ref, z_ref, cst_ref, nst_ref, C, nm, tail):
        z = _conv_fwd(cur_ref[:, 0:2 * W4], tail[...], cw_ref)
        tail[...] = cur_ref[L - 8:L, 0:2 * W4]
        z_ref[...] = z
        qk = z * _sigmoid(z)
        gt = pif_ref[...] + sv_ref[1:2, 0:LANES]
        r_i = lax.broadcasted_iota(jnp.int32, (L, L), 0)
        c_i = lax.broadcasted_iota(jnp.int32, (L, L), 1)
        tri = c_i <= r_i
        eye = c_i == r_i
        b_all = _tri_matmul(tri.astype(BF16), _log_sigmoid(gt))
        nmv = nm[...]
        nst_ref[0] = nmv
        c_prev = [C[h] for h in range(MLSTM_HEADS)]
        f = _mlstm_heads_fwd(qk, cur_ref, gt, b_all, c_prev, nmv, tri, eye)
        hv = f["hv"]
        xc = hv - _rowsum_mxu(hv, True) * (1.0 / HD)
        hhat = xc * lax.rsqrt(_rowsum_mxu(xc * xc) * (1.0 / HD) + NORM_EPS)
        so = _sigmoid(_stack(lambda h: cur_ref[:, 3 * W4 + h * HD:3 * W4 + (h + 1) * HD]))
        wn = _stack(lambda h: jnp.broadcast_to(sv_ref[0:1, h * HD:(h + 1) * HD], (L, HD)))
        y = (so * hhat * wn).astype(BF16)
        kw = f["k"] * f["u_col"]
        kwb = kw.astype(BF16)
        n_new, m_new = [], []
        for h in range(MLSTM_HEADS):
            cst_ref[0, h] = c_prev[h]
            ym_ref[:, h * HD:(h + 1) * HD] = _head(y, h)
            dec = f["dec"][h * L:h * L + 1, :]
            C[h] = dec * c_prev[h] + _dot_tn(_head(kwb, h), _head(f["vb"], h))
            n_new.append(dec * nmv[h:h + 1, :] + _colsum(_head(kw, h)))
            m_new.append(jnp.broadcast_to(f["m_new"][h * L:h * L + 1, :], (1, LANES)))
        nm[...] = jnp.concatenate(n_new + m_new, axis=0)

    return dict(
        init=init, body=body, operands=[pm, pif, cw, sv],
        in_specs=[pl.BlockSpec((L, M_W), lambda c: (c, 0)),
                  pl.BlockSpec((L, IF_W), lambda c: (c, 0)), _full(cw.shape), _full(sv.shape)],
        out_specs=[pl.BlockSpec((L, W4), lambda c: (c, 0)), pl.BlockSpec((L, 2 * W4), lambda c: (c, 0)),
                   pl.BlockSpec((1, MLSTM_HEADS, HD, HD), lambda c: (c, 0, 0, 0)),
                   pl.BlockSpec((1, 8, LANES), lambda c: (c, 0, 0))],
        out_shape=[jax.ShapeDtypeStruct((S, W4), BF16), jax.ShapeDtypeStruct((S, 2 * W4), F32),
                   jax.ShapeDtypeStruct((nc, MLSTM_HEADS, HD, HD), F32), jax.ShapeDtypeStruct((nc, 8, LANES), F32)],
        scratch=[pltpu.VMEM((MLSTM_HEADS, HD, HD), F32), pltpu.VMEM((8, LANES), F32), pltpu.VMEM((8, 2 * W4), F32)])


def _mlstm_bwd_part(pm, zc, pif, cw, sv, dym, cst, nst):
    S = pm.shape[0]
    L = MLSTM_CHUNK
    nc = S // L
    HD = MLSTM_HEAD_DIM
    W4 = MLSTM_HEADS * HD

    def init(r, cur_ref, z_ref, pif_ref, cw_ref, sv_ref, dym_ref, cst_ref, nst_ref,
             dm_ref, dif_ref, dcw_ref, dsv_ref, dC, dn, dz_next, dqk):
        @pl.when(r == 0)
        def _():
            dC[...] = jnp.zeros_like(dC)
            dn[...] = jnp.zeros_like(dn)
            dz_next[...] = jnp.zeros_like(dz_next)
            dcw_ref[...] = jnp.zeros_like(dcw_ref)
            dsv_ref[...] = jnp.zeros_like(dsv_ref)

    def body(r, cur_ref, z_ref, pif_ref, cw_ref, sv_ref, dym_ref, cst_ref, nst_ref,
             dm_ref, dif_ref, dcw_ref, dsv_ref, dC, dn, dz_next, dqk):
        z = z_ref[...]
        sgz = _sigmoid(z)
        qk = z * sgz
        gt = pif_ref[...] + sv_ref[1:2, 0:LANES]
        r_i = lax.broadcasted_iota(jnp.int32, (L, L), 0)
        c_i = lax.broadcasted_iota(jnp.int32, (L, L), 1)
        tri = c_i <= r_i
        eye = c_i == r_i
        b_all = _tri_matmul(tri.astype(BF16), _log_sigmoid(gt))
        lane = lax.broadcasted_iota(jnp.int32, (L, LANES), 1)
        rowl = lax.broadcasted_iota(jnp.int32, (L, 1), 0)
        nmv = nst_ref[0]
        heads = range(MLSTM_HEADS)
        c_prev = [cst_ref[0, h] for h in heads]
        f = _mlstm_heads_fwd(qk, cur_ref, gt, b_all, c_prev, nmv, tri, eye)
        hv, inv_g, den, e_t, u_col, n_prev = f["hv"], f["inv_g"], f["den"], f["e_t"], f["u_col"], f["n_prev"]
        q, k, v, qb, kb, vb, Sc, Scb, W = f["q"], f["k"], f["v"], f["qb"], f["kb"], f["vb"], f["Sc"], f["Scb"], f["W"]
        xc = hv - _rowsum_mxu(hv, True) * (1.0 / HD)
        rstd = lax.rsqrt(_rowsum_mxu(xc * xc) * (1.0 / HD) + NORM_EPS)
        hhat = xc * rstd
        wn = _stack(lambda h: jnp.broadcast_to(sv_ref[0:1, h * HD:(h + 1) * HD], (L, HD)))
        so = _sigmoid(_stack(lambda h: cur_ref[:, 3 * W4 + h * HD:3 * W4 + (h + 1) * HD]))
        dy = _stack(lambda h: dym_ref[:, h * HD:(h + 1) * HD])
        d_o = (dy * hhat * wn * (so * (1.0 - so))).astype(BF16)
        dln = dy * so
        dwn = dln * hhat
        dhhat = dln * wn
        m2 = _rowsum_mxu(dhhat * hhat) * (1.0 / HD)
        dh = rstd * (dhhat - _rowsum_mxu(dhhat) * (1.0 / HD) - hhat * m2)
        dnum = dh * inv_g
        active = jnp.abs(den) > f["floor"]
        dden = jnp.where(active, -(HD * NORM_EPS) * m2 * rstd * rstd * inv_g * jnp.where(den >= 0.0, 1.0, -1.0), 0.0)
        dnumb = dnum.astype(BF16)
        dSc = _stack(lambda h: _dot_nt(_head(dnumb, h), _head(vb, h))) + dden
        dA = (dSc * W).astype(BF16)
        G = dSc * Sc
        Gb = G.astype(BF16)
        ones = jnp.ones((L, LANES), BF16)
        Gr = _dot(Gb, ones)
        Gc = _stack(lambda h: _dot_tn(_head(Gb, h), ones))
        dCn = [dC[h] for h in heads]
        dCnb = [d.astype(BF16) for d in dCn]
        dnv = dn[...]
        dn_new = _stack(lambda h: jnp.broadcast_to(dnv[h:h + 1, :], (L, HD)))
        kdC = _stack(lambda h: _dot(_head(kb, h), dCnb[h]))
        vdC = _stack(lambda h: _dot_nt(_head(vb, h), dCnb[h]))
        dv = (_stack(lambda h: _dot_tn(_head(Scb, h), _head(dnumb, h))) + u_col * kdC).astype(BF16)
        dq = _stack(lambda h: _dot(_head(dA, h), _head(kb, h))) \
            + e_t * _stack(lambda h: _dot_nt(_head(dnumb, h), f["cb"][h])) + (e_t * dden) * n_prev
        dk = (_stack(lambda h: _dot_tn(_head(dA, h), _head(qb, h))) + u_col * (vdC + dn_new)) * (HD ** -0.5)
        E = (_rowsum_mxu(f["P1"] * dnum) + dden * f["qn"]) * e_t
        U = _rowsum_mxu(kdC * v + k * dn_new) * u_col
        qe = (q * e_t).astype(BF16)
        qd = (e_t * dden) * q
        di = Gc + U
        db = Gr + E - Gc - U
        di_tile = jnp.zeros((L, LANES), F32)
        db_tile = jnp.zeros((L, LANES), F32)
        dn_rows = []
        for h in heads:
            dec = f["dec"][h * L:h * L + 1, :]
            ddec = jnp.sum(dCn[h] * c_prev[h]) + jnp.sum(dnv[h:h + 1, :] * nmv[h:h + 1, :])
            dbl = ddec * dec + jnp.sum(_head(U, h), axis=0, keepdims=True)
            di_tile = jnp.where(lane == h, _head(di, h), di_tile)
            db_tile = jnp.where(lane == 4 + h, _head(db, h) + jnp.where(rowl == L - 1, dbl, 0.0), db_tile)
            dC[h] = dec * dCn[h] + _dot_tn(_head(qe, h), _head(dnumb, h))
            dn_rows.append(dec * dnv[h:h + 1, :] + _colsum(_head(qd, h)))
            dsv_ref[0:1, h * HD:(h + 1) * HD] += _colsum(_head(dwn, h))
            dqk[:, h * HD:(h + 1) * HD] = _head(dq, h)
            dqk[:, W4 + h * HD:W4 + (h + 1) * HD] = _head(dk, h)
            dm_ref[:, 2 * W4 + h * HD:2 * W4 + (h + 1) * HD] = _head(dv, h)
            dm_ref[:, 3 * W4 + h * HD:3 * W4 + (h + 1) * HD] = _head(d_o, h)
        dn[...] = jnp.concatenate(dn_rows + [jnp.zeros((8 - MLSTM_HEADS, LANES), F32)], axis=0)
        dlf = _tri_matmul((r_i <= c_i).astype(BF16), db_tile)
        dif = jnp.where(lane < 4, di_tile, jnp.where(lane < 8, dlf * (1.0 - _sigmoid(gt)), 0.0))
        dif_ref[...] = dif.astype(BF16)
        dsv_ref[1:2, 0:LANES] += _colsum(dif)
        dz = dqk[...] * (sgz * (1.0 + z * (1.0 - sgz)))
        dcw_ref[4:5, :] += _colsum(dz)
        u = cur_ref[:, 0:2 * W4]
        du_in = dz * cw_ref[3:4, :]
        dcw_ref[3:4, :] += _colsum(dz * u)
        for k in range(1, 4):
            up = _shift_rows(dz, dz_next[...], k, False)
            dcw_ref[3 - k:4 - k, :] += _colsum(up * u)
            du_in = du_in + up * cw_ref[3 - k:4 - k, :]
        dz_next[...] = dz[0:8, :]
        dm_ref[:, 0:2 * W4] = du_in.astype(BF16)

    cidx = lambda r: nc - 1 - r
    return dict(
        init=init, body=body, operands=[pm, zc, pif, cw, sv, dym, cst, nst],
        in_specs=[pl.BlockSpec((L, M_W), lambda r: (cidx(r), 0)), pl.BlockSpec((L, 2 * W4), lambda r: (cidx(r), 0)),
                  pl.BlockSpec((L, IF_W), lambda r: (cidx(r), 0)), _full(cw.shape), _full(sv.shape),
                  pl.BlockSpec((L, W4), lambda r: (cidx(r), 0)),
                  pl.BlockSpec((1, MLSTM_HEADS, HD, HD), lambda r: (cidx(r), 0, 0, 0)),
                  pl.BlockSpec((1, 8, LANES), lambda r: (cidx(r), 0, 0))],
        out_specs=[pl.BlockSpec((L, M_W), lambda r: (cidx(r), 0)), pl.BlockSpec((L, IF_W), lambda r: (cidx(r), 0)),
                   _full((8, 2 * W4)), _full((8, W4))],
        out_shape=[jax.ShapeDtypeStruct((S, M_W), BF16), jax.ShapeDtypeStruct((S, IF_W), BF16),
                   jax.ShapeDtypeStruct((8, 2 * W4), F32), jax.ShapeDtypeStruct((8, W4), F32)],
        scratch=[pltpu.VMEM((MLSTM_HEADS, HD, HD), F32), pltpu.VMEM((8, LANES), F32),
                 pltpu.VMEM((8, 2 * W4), F32), pltpu.VMEM((L, 2 * W4), F32)])


def _rope_tables(positions):
    half = HEAD_DIM // 2
    inv_freq = ROPE_THETA ** (-2.0 * jnp.arange(half, dtype=F32) / HEAD_DIM)
    ang = positions.astype(F32)[:, None] * inv_freq
    cos = jnp.tile(jnp.cos(ang), (1, LANES // half))
    sign = jnp.tile(jnp.concatenate([-jnp.ones((half,), F32), jnp.ones((half,), F32)]), LANES // HEAD_DIM)
    sin = jnp.tile(jnp.sin(ang), (1, LANES // half)) * sign
    return cos, sin


def _local_step(x, tgt, positions, mod, gains, w_cat, w_ba, w_bm, w_out, w_gate, w_up, w_down,
                conv_w, conv_b, b_if, sinks, norm_w):
    t = _tables(mod, gains, conv_w, conv_b, b_if, norm_w, positions)
    a = _mixer_fwd(x, t, sinks, w_cat)
    b = _ffn_part(x, tgt, t, a, w_ba, w_bm, w_out, w_gate, w_up, w_down)
    c = _mixer_bwd(b["dx1"], t, a, b, sinks, w_ba, w_bm, w_out)
    grad_x, acc_p = _pre_bwd(c["dproj"], x, b["dx1"], t["vecs"], w_cat)
    big = dict(w_cat=jnp.concatenate(c["g_w_cat"], axis=1), w_ba=c["g_w_ba"], w_bm=c["g_w_bm"], w_out=c["g_w_out"], w_gate=b["g_w_gate"],
               w_up=b["g_w_up"], w_down=b["g_w_down"])
    return b["loss"], grad_x, big, _small_grads(acc_p, b, c)


def _tables(mod, gains, conv_w, conv_b, b_if, norm_w, positions):
    cos, sin = _rope_tables(positions)
    return dict(
        vecs=jnp.concatenate([mod, gains, jnp.zeros((6, D_MODEL), F32)], axis=0),
        cw=jnp.concatenate([conv_w, conv_b.reshape(1, -1), jnp.zeros((3, 2 * 512), F32)], axis=0),
        sv=jnp.zeros((8, 512), F32).at[0].set(norm_w).at[1, 0:8].set(b_if), cos=cos, sin=sin)


def _mixer_fwd(x, t, sinks, w_cat):
    h, pa, pm, pif, pg = _pre_proj(x, t["vecs"], w_cat)
    n_blk = x.shape[0] // ATTN_BLOCK
    (ya, qr, kr, vb, lse), = _fused_call([_attn_fwd_part(pa, t["cos"], t["sin"], sinks)], "attn_fwd", n_blk)
    (ym, zc, cst, nst), = _fused_call([_mlstm_fwd_part(pm, pif, t["cw"], t["sv"])], "mlstm_fwd", n_blk)
    return dict(h=h, pm=pm, pif=pif, pg=pg, ya=ya, qr=qr, kr=kr, vb=vb, lse=lse, ym=ym, zc=zc, cst=cst, nst=nst)


def _ffn_part(x, tgt, t, a, w_ba, w_bm, w_out, w_gate, w_up, w_down):
    x1, merged, mix, pba, pbm = _mix_fwd(x, a["ya"], a["ym"], a["pg"], t["vecs"], w_ba, w_bm, w_out)
    dx1, h2, hid, da, du, dff, acc_f, loss = _ffn_fwd_bwd(x1, tgt, t["vecs"], w_gate, w_up, w_down)
    return dict(merged=merged, mix=mix, pba=pba, pbm=pbm, dx1=dx1, acc_f=acc_f, loss=loss[0, 0],
                g_w_gate=_matmul_tn(da, h2, 1024, "dw_ffn_gate"),
                g_w_up=_matmul_tn(du, h2, 1024, "dw_ffn_up"),
                g_w_down=_matmul_tn(hid, dff, 1024, "dw_ffn_down"))


def _mixer_bwd(dx1, t, a, b, sinks, w_ba, w_bm, w_out):
    dmix, dpa, dpb, dg, dya, dym, acc_m = _mix_bwd(dx1, b["mix"], b["pba"], b["pbm"], a["pg"], t["vecs"], w_ba, w_bm, w_out)
    g_w_out = _matmul_tn(b["merged"], dmix, 1024, "dw_out")
    g_w_ba = _matmul_tn(a["ya"], dpa, 1024, "dw_branch_attn")
    g_w_bm = _matmul_tn(a["ym"], dpb, 1024, "dw_branch_mlstm")
    n_blk = dx1.shape[0] // ATTN_BLOCK
    (dq, dkv, dkv_last, dsink), = _fused_call(
        [_attn_bwd_part(dya, a["qr"], a["kr"], a["vb"], a["lse"], t["cos"], t["sin"], sinks)], "attn_bwd", n_blk)
    (dm, dif, dcw, dsv), = _fused_call(
        [_mlstm_bwd_part(a["pm"], a["zc"], a["pif"], t["cw"], t["sv"], dym, a["cst"], a["nst"])], "mlstm_bwd", n_blk)
    dkv = lax.dynamic_update_slice(dkv, dkv_last, (dkv.shape[0] - ATTN_BLOCK, 0))
    dproj = [dq, dkv, dm, dif, dg]
    g_w_cat = [_matmul_tn(a["h"], p, min(p.shape[1], 1024), "dw_in_" + n)
               for p, n in zip(dproj, ("q", "kv", "mlstm", "gates", "branch"))]
    return dict(dproj=dproj, g_w_cat=g_w_cat, g_w_out=g_w_out, g_w_ba=g_w_ba,
                g_w_bm=g_w_bm, acc_m=acc_m, dsink=dsink, dcw=dcw, dsv=dsv)


def _small_grads(acc_p, b, c):
    acc_f, acc_m = b["acc_f"], c["acc_m"]
    dmod = jnp.stack([acc_p[1], acc_p[0], acc_m[0], acc_f[3], acc_f[2], acc_f[0]])
    dgains = jnp.stack([acc_p[2], acc_m[1], acc_f[4], acc_f[1]])
    return dict(dmod=dmod, dgains=dgains, dconv_w=c["dcw"][0:4], dconv_b=c["dcw"][4], db_if=c["dsv"][1, 0:8],
                dsinks=c["dsink"][0, 0:8], dnorm_w=c["dsv"][0])


MESH_ID = pl.DeviceIdType.MESH


def _mesh_pos():
    return lax.axis_index("x"), lax.axis_index("y"), lax.axis_index("c")


def _flip(v, bit):
    return 1 - v if bit else v


def _relations():
    return [((r >> 2) & 1, (r >> 1) & 1, r & 1) for r in range(1, N_DEV)]


def _small_exchange(p, gather, name):
    R, V = p.shape[-2:]

    def body(p_ref, out_ref, send_sems, recv_sems):
        x, y, c = _mesh_pos()
        me = 4 * x + 2 * y + c
        out_ref[me] = p_ref[...] if gather else p_ref[me]
        peers = []
        for dx, dy, dc in _relations():
            px, py, pc = _flip(x, dx), _flip(y, dy), _flip(c, dc)
            peers.append(((px, py, pc), 4 * px + 2 * py + pc))

        def copy(k, landing):
            peer, pid = peers[k]
            return pltpu.make_async_remote_copy(
                src_ref=p_ref if gather else p_ref.at[pid], dst_ref=out_ref.at[landing],
                send_sem=send_sems.at[k], recv_sem=recv_sems.at[k], device_id=peer, device_id_type=MESH_ID)

        sends = [copy(k, me) for k in range(N_DEV - 1)]
        for cp in sends:
            cp.start()
        for k in range(N_DEV - 1):
            copy(k, peers[k][1]).wait_recv()
        for cp in sends:
            cp.wait_send()

    vm = pl.BlockSpec(memory_space=pltpu.VMEM)
    return pl.pallas_call(
        body, name=name, in_specs=[vm], out_specs=vm,
        out_shape=jax.ShapeDtypeStruct((N_DEV, R, V), F32),
        scratch_shapes=[pltpu.SemaphoreType.DMA((N_DEV - 1,)), pltpu.SemaphoreType.DMA((N_DEV - 1,))],
        compiler_params=pltpu.CompilerParams(vmem_limit_bytes=VMEM_LIMIT),
    )(p)


HBM_SPEC = pl.BlockSpec(memory_space=pltpu.HBM)
SEM_SPEC = pl.BlockSpec(memory_space=pltpu.SEMAPHORE)


def _peers(x, y, c):
    out = []
    for dx, dy, dc in _relations():
        px, py, pc = _flip(x, dx), _flip(y, dy), _flip(c, dc)
        out.append(((px, py, pc), 4 * px + 2 * py + pc))
    return out


def _exchange_start(arrs, gather, after, name):
    n = len(arrs)
    me_out = 4 * lax.axis_index("x") + 2 * lax.axis_index("y") + lax.axis_index("c")
    lands = []
    for a in arrs:
        own = a[None] if gather else lax.dynamic_index_in_dim(a, me_out, 0, keepdims=True)
        empty = lax.empty(((N_DEV,) + a.shape) if gather else a.shape, a.dtype)
        lands.append(lax.dynamic_update_index_in_dim(empty, own, me_out, 0))

    def body(*refs):
        a_refs, l_refs = refs[:n], refs[n:2 * n]
        send_sems, recv_sems = refs[2 * n + 1], refs[2 * n + 2]
        token = refs[4 * n + 3]
        x, y, c = _mesh_pos()
        me = 4 * x + 2 * y + c
        for a in range(n):
            for k, (peer, pid) in enumerate(_peers(x, y, c)):
                pltpu.make_async_remote_copy(
                    src_ref=a_refs[a] if gather else a_refs[a].at[pid], dst_ref=l_refs[a].at[me],
                    send_sem=send_sems.at[a * (N_DEV - 1) + k], recv_sem=recv_sems.at[a * (N_DEV - 1) + k],
                    device_id=peer, device_id_type=MESH_ID).start()
        token[...] = jnp.zeros_like(token)

    sem = pltpu.SemaphoreType.DMA((n * (N_DEV - 1),))
    hbm = lambda a: pltpu.with_memory_space_constraint(a, pltpu.HBM)
    res = pl.pallas_call(
        body, name=name,
        out_shape=(sem, sem, *[pltpu.HBM(a.shape, a.dtype) for a in arrs], *[pltpu.HBM(l.shape, l.dtype) for l in lands],
                   jax.ShapeDtypeStruct((8, LANES), F32)),
        in_specs=[HBM_SPEC] * (2 * n) + [pl.BlockSpec(memory_space=pl.ANY)],
        out_specs=(SEM_SPEC, SEM_SPEC, *[HBM_SPEC] * (2 * n), pl.BlockSpec(memory_space=pltpu.VMEM)),
        input_output_aliases={i: 2 + i for i in range(2 * n)},
        compiler_params=pltpu.CompilerParams(has_side_effects=pltpu.SideEffectType.DATAFLOW_SIDE_EFFECTING),
    )(*[hbm(a) for a in arrs], *[hbm(l) for l in lands], after)
    return dict(sems=res[0:2], arrs=res[2:2 + n], lands=res[2 + n:2 + 2 * n], token=res[2 + 2 * n], gather=gather)


def _exchange_wait(st, after, name):
    n = len(st["arrs"])
    gather = st["gather"]

    def body(*refs):
        a_refs, l_refs = refs[:n], refs[n:2 * n]
        send_sems, recv_sems = refs[2 * n], refs[2 * n + 1]
        x, y, c = _mesh_pos()
        for a in range(n):
            for k, (peer, pid) in enumerate(_peers(x, y, c)):
                cp = pltpu.make_async_remote_copy(
                    src_ref=a_refs[a] if gather else a_refs[a].at[pid], dst_ref=l_refs[a].at[pid],
                    send_sem=send_sems.at[a * (N_DEV - 1) + k], recv_sem=recv_sems.at[a * (N_DEV - 1) + k],
                    device_id=peer, device_id_type=MESH_ID)
                cp.wait_send()
                cp.wait_recv()

    both = list(st["arrs"]) + list(st["lands"])
    res = pl.pallas_call(
        body, name=name, out_shape=[pltpu.HBM(a.shape, a.dtype) for a in both],
        in_specs=[HBM_SPEC] * (2 * n) + [SEM_SPEC, SEM_SPEC, pl.BlockSpec(memory_space=pl.ANY)],
        out_specs=[HBM_SPEC] * (2 * n), input_output_aliases={i: i for i in range(2 * n)},
        compiler_params=pltpu.CompilerParams(has_side_effects=pltpu.SideEffectType.DATAFLOW_SIDE_EFFECTING),
    )(*both, *st["sems"], after)
    return res[n:2 * n]


def _tie(x, token, name):
    def body(x_ref, t_ref, o_ref):
        o_ref[...] = x_ref[...]

    vm = pl.BlockSpec(memory_space=pltpu.VMEM)
    return pl.pallas_call(
        body, name=name, in_specs=[vm, pl.BlockSpec(memory_space=pl.ANY)], out_specs=vm,
        out_shape=jax.ShapeDtypeStruct(x.shape, x.dtype),
    )(x, token)


def _all_gather_hbm(shards):
    n = len(shards)

    def body(*refs):
        p_refs, out_refs = refs[:n], refs[n:2 * n]
        send_sems, recv_sems, local_sems = refs[2 * n:]
        x, y, c = _mesh_pos()
        me, sibling = (x, y, c), (x, y, 1 - c)
        chips = [(1 - x, y), (x, 1 - y), (1 - x, 1 - y)]

        def copy(a, k, block, to, own=False):
            slot = out_refs[a].at[4 * block[0] + 2 * block[1] + block[2]]
            return pltpu.make_async_remote_copy(
                src_ref=p_refs[a] if own else slot, dst_ref=slot,
                send_sem=send_sems.at[a, k], recv_sem=recv_sems.at[a, k], device_id=to, device_id_type=MESH_ID)

        mine = [pltpu.make_async_copy(p_refs[a], out_refs[a].at[4 * x + 2 * y + c], local_sems.at[a]) for a in range(n)]
        for cp in mine:
            cp.start()
        first = []
        for a in range(n):
            first.append(copy(a, 0, me, sibling, own=True))
            first += [copy(a, 1 + j, me, (*chip, c), own=True) for j, chip in enumerate(chips)]
        for cp in first:
            cp.start()
        passed = []
        for j, chip in enumerate(chips):
            for a in range(n):
                copy(a, 1 + j, (*chip, c), me).wait_recv()
                passed.append(copy(a, 4 + j, (*chip, c), sibling))
                passed[-1].start()
        for a in range(n):
            copy(a, 0, sibling, me).wait_recv()
            for j, chip in enumerate(chips):
                copy(a, 4 + j, (*chip, 1 - c), me).wait_recv()
        for cp in first + passed:
            cp.wait_send()
        for cp in mine:
            cp.wait()

    hbm = pl.BlockSpec(memory_space=pl.ANY)
    return pl.pallas_call(
        body, name="gather_weights", in_specs=[hbm] * n, out_specs=[hbm] * n,
        out_shape=[jax.ShapeDtypeStruct((N_DEV,) + s.shape, s.dtype) for s in shards],
        scratch_shapes=[pltpu.SemaphoreType.DMA((n, N_DEV - 1)), pltpu.SemaphoreType.DMA((n, N_DEV - 1)),
                        pltpu.SemaphoreType.DMA((n,))],
    )(*shards)


def _adamw(w, g, m, v):
    m2 = ADAM_B1 * m + (1.0 - ADAM_B1) * g
    v2 = ADAM_B2 * v + (1.0 - ADAM_B2) * (g * g)
    m_hat = m2 / (1.0 - ADAM_B1 ** ADAM_STEP)
    v_hat = v2 / (1.0 - ADAM_B2 ** ADAM_STEP)
    delta = -ADAM_LR * (m_hat / (jnp.sqrt(v_hat) + ADAM_EPS) + ADAM_WD * w)
    return delta, m2, v2


def _mod_partial(cmat, w_shard, b_shard):
    def body(c_ref, w_ref, b_ref, o_ref):
        o_ref[...] = _dot(c_ref[...].astype(BF16), w_ref[...].astype(BF16)) + b_ref[...]

    return pl.pallas_call(
        body, name="mod_partial", out_shape=jax.ShapeDtypeStruct((N_DEV, w_shard.shape[1]), F32),
        compiler_params=_params(),
    )(cmat, w_shard, b_shard)


def _adamw_w_ada(cmat, dmod_cols, w, m, v):
    def body(c_ref, d_ref, w_ref, m_ref, v_ref, g_ref, dl_ref, m2_ref, v2_ref):
        g = _dot_tn(c_ref[...].astype(BF16), d_ref[...].astype(BF16))
        g_ref[...] = g
        dl_ref[...], m2_ref[...], v2_ref[...] = _adamw(w_ref[...], g, m_ref[...], v_ref[...])

    return pl.pallas_call(
        body, name="adamw_w_ada", out_shape=[jax.ShapeDtypeStruct(w.shape, F32)] * 4,
        compiler_params=_params(),
    )(cmat, dmod_cols, w, m, v)


SMALL_ROWS = 16
SMALL_AT = {"b_ada": (0, 6, 0, D_MODEL), "g_pre_mix": (6, 1, 0, D_MODEL), "g_post_mix": (7, 1, 0, D_MODEL),
            "g_pre_ffn": (8, 1, 0, D_MODEL), "g_post_ffn": (9, 1, 0, D_MODEL), "conv_b": (10, 1, 0, D_MODEL),
            "mlstm_norm_w": (11, 1, 0, 512), "b_if": (11, 1, 512, LANES), "attn_sinks": (11, 1, 640, LANES)}


def _small_table(part):
    tail = jnp.concatenate([part["mlstm_norm_w"], jnp.pad(part["b_if"], (0, LANES - 8)),
                            jnp.pad(part["attn_sinks"], (0, LANES - 8)), jnp.zeros((256,), F32)])
    return jnp.concatenate([part["b_ada"], part["gains"], part["conv_b"][None], tail[None],
                            jnp.zeros((SMALL_ROWS - 12, D_MODEL), F32)], axis=0)


def _adamw_small(gathered, wmv):
    names = list(SMALL_AT)

    def body(*refs):
        g_ref, ins, outs = refs[0], refs[1:1 + 3 * len(names)], refs[1 + 3 * len(names):]
        g = g_ref[0]
        for k in range(1, N_DEV):
            g = g + g_ref[k]
        for i, n in enumerate(names):
            r0, rows, l0, lanes = SMALL_AT[n]
            gi = jnp.concatenate([g[r:r + 1, l0:l0 + lanes] for r in range(r0, r0 + rows)], axis=1)
            w_ref, m_ref, v_ref = ins[3 * i:3 * i + 3]
            go, dl, m2, v2 = outs[4 * i:4 * i + 4]
            go[...] = gi
            dl[...], m2[...], v2[...] = _adamw(w_ref[...], gi, m_ref[...], v_ref[...])

    flat = [a for n in names for a in wmv[n]]
    res = pl.pallas_call(
        body, name="adamw_small",
        out_shape=[jax.ShapeDtypeStruct(wmv[n][0].shape, F32) for n in names for _ in range(4)],
        compiler_params=_params(),
    )(gathered, *flat)
    return {n: res[4 * i:4 * i + 4] for i, n in enumerate(names)}


def _row_tile(rows):
    return rows // 4 if rows >= 512 else rows


def _sum_partials(r_ref):
    g = r_ref[0].astype(F32)
    for k in range(1, N_DEV):
        g = g + r_ref[k].astype(F32)
    return g


def _adamw_sum(recv, w, m, v, name):
    r, cdim = w.shape
    tr = _row_tile(r)

    def body(r_ref, w_ref, m_ref, v_ref, g_ref, dl_ref, m2_ref, v2_ref):
        g = _sum_partials(r_ref)
        g_ref[...] = g
        dl_ref[...], m2_ref[...], v2_ref[...] = _adamw(w_ref[...], g, m_ref[...], v_ref[...])

    row = pl.BlockSpec((tr, cdim), lambda i: (i, 0))
    return pl.pallas_call(
        body, name=name, grid=(r // tr,),
        in_specs=[pl.BlockSpec((N_DEV, tr, cdim), lambda i: (0, i, 0)), row, row, row],
        out_specs=[row] * 4, out_shape=[jax.ShapeDtypeStruct((r, cdim), F32)] * 4,
        compiler_params=_params(("parallel",)),
    )(recv, w, m, v)


def _sum8(recv, name):
    _, r, cdim = recv.shape
    tr = _row_tile(r)

    def body(r_ref, g_ref):
        g_ref[...] = _sum_partials(r_ref)

    return pl.pallas_call(
        body, name=name, grid=(r // tr,),
        in_specs=[pl.BlockSpec((N_DEV, tr, cdim), lambda i: (0, i, 0))],
        out_specs=pl.BlockSpec((tr, cdim), lambda i: (i, 0)), out_shape=jax.ShapeDtypeStruct((r, cdim), F32),
        compiler_params=_params(("parallel",)),
    )(recv)


def _eye(n):
    return (lax.broadcasted_iota(jnp.int32, (n, n), 0) == lax.broadcasted_iota(jnp.int32, (n, n), 1)).astype(BF16)


def _transpose_to_bf16(w, name):
    r, c = w.shape

    def body(w_ref, o_ref):
        o_ref[...] = _dot_tn(w_ref[...].astype(BF16), _eye(r)).astype(BF16)

    return pl.pallas_call(body, name=name, out_shape=jax.ShapeDtypeStruct((c, r), BF16), compiler_params=_params())(w)


def _sum8_transposed(recv, name):
    _, r, c = recv.shape

    def body(r_ref, g_ref):
        eye = _eye(r)
        hi, mid, lo = _split3(_sum_partials(r_ref))
        g_ref[...] = _dot_tn(hi, eye) + _dot_tn(mid, eye) + _dot_tn(lo, eye)

    return pl.pallas_call(body, name=name, out_shape=jax.ShapeDtypeStruct((c, r), F32), compiler_params=_params())(recv)


def _adamw_plain(g, w, m, v, name):
    r, cdim = w.shape
    tr = _row_tile(r)

    def body(g_ref, w_ref, m_ref, v_ref, dl_ref, m2_ref, v2_ref):
        dl_ref[...], m2_ref[...], v2_ref[...] = _adamw(w_ref[...], g_ref[...], m_ref[...], v_ref[...])

    row = pl.BlockSpec((tr, cdim), lambda i: (i, 0))
    return pl.pallas_call(
        body, name=name, grid=(r // tr,), in_specs=[row] * 4, out_specs=[row] * 3,
        out_shape=[jax.ShapeDtypeStruct((r, cdim), F32)] * 3,
        compiler_params=_params(("parallel",)),
    )(g, w, m, v)


IN_SHARD = 609
IN_SHARD_PAD = 640
IF_AT = A_W + M_W


def _regrouped(u):
    return u if u < IF_AT + 8 else u + (IF_W - 8)


def _selection(k, rows, row0, transpose):
    shape = (rows, IN_SHARD_PAD) if transpose else (IN_SHARD_PAD, rows)
    l = lax.broadcasted_iota(jnp.int32, shape, 1 if transpose else 0)
    r = lax.broadcasted_iota(jnp.int32, shape, 0 if transpose else 1) + row0
    u = l + IN_SHARD * k
    ru = u + jnp.where(u >= IF_AT + 8, IF_W - 8, 0)
    return ((ru == r) & (l < IN_SHARD)).astype(BF16)


def _regroup_w_in(g):
    def body(g_ref, o_ref):
        for cb in range(CAT_W // LANES):
            r0 = cb * LANES
            acc = jnp.zeros((D_MODEL, LANES), F32)
            for k in range(N_DEV):
                lo, hi = _regrouped(IN_SHARD * k), _regrouped(IN_SHARD * k + IN_SHARD - 1)
                if hi >= r0 and lo < r0 + LANES:
                    acc = acc + _dot(g_ref[k], _selection(k, LANES, r0, False))
            o_ref[:, r0:r0 + LANES] = acc.astype(BF16)

    return pl.pallas_call(
        body, name="regroup_w_in", out_shape=jax.ShapeDtypeStruct((D_MODEL, CAT_W), BF16),
        compiler_params=_params(),
    )(g)


def _ungroup_w_in(g_parts):
    n = len(g_parts)

    def body(*refs):
        o_ref, g_ref = refs[n], refs[n + 1]
        at = 0
        for p in refs[:n]:
            g_ref[:, at:at + p.shape[1]] = p[...]
            at += p.shape[1]
        for k in range(N_DEV):
            lo, hi = _regrouped(IN_SHARD * k), _regrouped(IN_SHARD * k + IN_SHARD - 1)
            w0, w1 = lo // LANES * LANES, (hi // LANES + 1) * LANES
            o_ref[k] = _dot(g_ref[:, w0:w1], _selection(k, w1 - w0, w0, True)).astype(BF16)

    return pl.pallas_call(
        body, name="ungroup_w_in", out_shape=jax.ShapeDtypeStruct((N_DEV, D_MODEL, IN_SHARD_PAD), BF16),
        scratch_shapes=[pltpu.VMEM((D_MODEL, CAT_W), BF16)], compiler_params=_params(),
    )(*g_parts)


WEIGHT_NAMES = ("w_ada", "b_ada", "g_pre_mix", "g_post_mix", "w_in", "b_if", "conv_w", "conv_b", "attn_sinks",
                "mlstm_norm_w", "w_branch_attn", "w_branch_mlstm", "w_out", "g_pre_ffn", "g_post_ffn",
                "w_ffn_gate", "w_ffn_up", "w_ffn_down")


def kernel(x, c, positions, w_ada, b_ada, g_pre_mix, g_post_mix, w_in, b_if, conv_w, conv_b, attn_sinks, mlstm_norm_w, w_branch_attn, w_branch_mlstm, w_out, g_pre_ffn, g_post_ffn, w_ffn_gate, w_ffn_up, w_ffn_down, loss_target, m_w_ada, m_b_ada, m_g_pre_mix, m_g_post_mix, m_w_in, m_b_if, m_conv_w, m_conv_b, m_attn_sinks, m_mlstm_norm_w, m_w_branch_attn, m_w_branch_mlstm, m_w_out, m_g_pre_ffn, m_g_post_ffn, m_w_ffn_gate, m_w_ffn_up, m_w_ffn_down, v_w_ada, v_b_ada, v_g_pre_mix, v_g_post_mix, v_w_in, v_b_if, v_conv_w, v_conv_b, v_attn_sinks, v_mlstm_norm_w, v_w_branch_attn, v_w_branch_mlstm, v_w_out, v_g_pre_ffn, v_g_post_ffn, v_w_ffn_gate, v_w_ffn_up, v_w_ffn_down):
    given = dict(locals())
    W = {n: given[n][0] for n in WEIGHT_NAMES}
    M = {n: given["m_" + n][0] for n in WEIGHT_NAMES}
    V = {n: given["v_" + n][0] for n in WEIGHT_NAMES}
    me = 4 * lax.axis_index("x") + 2 * lax.axis_index("y") + lax.axis_index("c")

    ff_sh = D_FF // N_DEV
    g_in, g_conv, cg = _all_gather_hbm([jnp.pad(W["w_in"], ((0, 0), (0, IN_SHARD_PAD - IN_SHARD))).astype(BF16),
                                        jnp.pad(W["conv_w"], ((0, 4), (0, 0))), c.reshape(8, D_MODEL // 8)])

    cmat = cg.reshape(N_DEV, D_MODEL)
    ada_w = D_MODEL * 6 // N_DEV
    b_cols = lax.dynamic_slice(W["b_ada"], (me * ada_w,), (ada_w,)).reshape(1, ada_w)
    mod_part = _mod_partial(cmat, W["w_ada"], b_cols)
    mod_recv = _small_exchange(jnp.broadcast_to(mod_part[:, None, :], (N_DEV, 8, ada_w)), False, "scatter_mod")
    mod = mod_recv[:, 0, :].reshape(6, D_MODEL)

    st_b = _exchange_start([W["w_branch_attn"].astype(BF16), W["w_branch_mlstm"].astype(BF16), W["w_out"].astype(BF16),
                            _transpose_to_bf16(W["w_ffn_gate"], "transpose_w_ffn_gate"),
                            _transpose_to_bf16(W["w_ffn_up"], "transpose_w_ffn_up"), W["w_ffn_down"].astype(BF16)],
                           True, mod_recv, "gather_rest_start")
    cols = lambda g: g.transpose(1, 0, 2).reshape(g.shape[1], N_DEV * g.shape[2])
    gains = jnp.stack([W["g_pre_mix"], W["g_post_mix"], W["g_pre_ffn"], W["g_post_ffn"]])
    xs, tgt = x[0], loss_target[0]
    t = _tables(mod, gains, cols(g_conv)[0:4], W["conv_b"], W["b_if"], W["mlstm_norm_w"], positions[0])
    vecs = t["vecs"]
    t["vecs"] = _tie(vecs, st_b["token"], "tie_fwd")
    w_cat = _regroup_w_in(g_in)
    a = _mixer_fwd(xs, t, W["attn_sinks"], w_cat)
    g_ba, g_bm, g_out, g_gate, g_up, g_down = _exchange_wait(st_b, a["ym"], "gather_rest_wait")
    w_ba, w_bm, w_out = cols(g_ba), cols(g_bm), g_out.reshape(D_MODEL, D_MODEL)
    b = _ffn_part(xs, tgt, t, a, w_ba, w_bm, w_out, g_gate.reshape(D_FF, D_MODEL), g_up.reshape(D_FF, D_MODEL),
                  g_down.reshape(D_FF, D_MODEL))

    st_f = _exchange_start([b["g_w_gate"].reshape(N_DEV, ff_sh, D_MODEL), b["g_w_up"].reshape(N_DEV, ff_sh, D_MODEL),
                            b["g_w_down"].reshape(N_DEV, ff_sh, D_MODEL)], False, b["dx1"], "scatter_ffn_start")
    t["vecs"] = _tie(vecs, st_f["token"], "tie_bwd")
    cm = _mixer_bwd(b["dx1"], t, a, b, W["attn_sinks"], w_ba, w_bm, w_out)
    pieces = lambda g, n: g.reshape(g.shape[0], N_DEV, n).transpose(1, 0, 2)
    st_m = _exchange_start([_ungroup_w_in(cm["g_w_cat"]), pieces(cm["g_w_ba"], 128), pieces(cm["g_w_bm"], 128),
                            cm["g_w_out"].reshape(N_DEV, D_MODEL // N_DEV, D_MODEL),
                            jnp.pad(pieces(cm["dcw"][0:4], 128), ((0, 0), (0, 4), (0, 0)))], False, cm["dcw"],
                           "scatter_mixer_start")
    r_gate, r_up, r_down = _exchange_wait(st_f, st_m["token"], "scatter_ffn_wait")
    grad_x, acc_p = _pre_bwd(cm["dproj"], xs, b["dx1"], _tie(vecs, st_m["token"], "tie_pre_bwd"), w_cat)
    small = _small_grads(acc_p, b, cm)
    loss = b["loss"]

    big_out = [{} for _ in range(4)]

    def put(n, res):
        for k in range(4):
            big_out[k][n] = res[k][None]

    put("w_ffn_down", _adamw_sum(r_down, W["w_ffn_down"], M["w_ffn_down"], V["w_ffn_down"], "adamw_w_ffn_down"))
    for n, g in (("w_ffn_gate", _sum8_transposed(r_gate, "sum_w_ffn_gate")),
                 ("w_ffn_up", _sum8_transposed(r_up, "sum_w_ffn_up"))):
        put(n, [g] + list(_adamw_plain(g, W[n], M[n], V[n], "adamw_" + n)))

    sg = _small_exchange(_small_table({"b_ada": small["dmod"], "gains": small["dgains"], "conv_b": small["dconv_b"],
                                       "mlstm_norm_w": small["dnorm_w"], "b_if": small["db_if"],
                                       "attn_sinks": small["dsinks"]}), True, "gather_small")
    as_row = lambda a, n: jnp.pad(a, (0, SMALL_AT[n][3] * SMALL_AT[n][1] - a.shape[0]))[None]
    small_res = _adamw_small(sg, {n: [as_row(d[n], n) for d in (W, M, V)] for n in SMALL_AT})
    small_out = [{n: small_res[n][k][:, 0:W[n].shape[0]] for n in SMALL_AT} for k in range(4)]
    dmod_cols = lax.dynamic_slice(sg[:, 0:6, :].reshape(N_DEV, 6 * D_MODEL), (0, me * ada_w), (N_DEV, ada_w))
    ada_out = _adamw_w_ada(cmat, dmod_cols, W["w_ada"], M["w_ada"], V["w_ada"])

    r_in, r_ba, r_bm, r_out, r_conv = _exchange_wait(st_m, grad_x, "scatter_mixer_wait")
    for n, r in (("w_branch_attn", r_ba), ("w_branch_mlstm", r_bm), ("w_out", r_out)):
        put(n, _adamw_sum(r, W[n], M[n], V[n], "adamw_" + n))
    pad4 = lambda v: jnp.pad(v, ((0, 4), (0, 0)))
    put("conv_w", [o[0:4] for o in _adamw_sum(r_conv, pad4(W["conv_w"]), pad4(M["conv_w"]), pad4(V["conv_w"]),
                                                 "adamw_conv_w")])
    g = _sum8(r_in, "sum_w_in")[:, 0:IN_SHARD]
    put("w_in", [g] + list(_adamw_plain(g, W["w_in"], M["w_in"], V["w_in"], "adamw_w_in")))

    total = lax.psum(loss, ("x", "y", "c"))
    outs = [total, grad_x[None]]
    for k in range(4):
        for n in WEIGHT_NAMES:
            if n == "w_ada":
                outs.append(ada_out[k][None])
            elif n in big_out[k]:
                outs.append(big_out[k][n])
            else:
                outs.append(small_out[k][n])
    return tuple(outs)
```

```python
import functools

import jax
import jax.numpy as jnp
import numpy as np
from jax import lax
from jax.experimental import pallas as pl
from jax.experimental.pallas import tpu as pltpu

F32 = jnp.float32
BF16 = jnp.bfloat16

N_DEV = 8
D_MODEL = 1024
D_FF = 2816
N_Q_HEADS = 8
HEAD_DIM = 64
ATTN_BLOCK = 128
ROPE_THETA = 10000.0
MLSTM_HEADS = 4
MLSTM_HEAD_DIM = 128
MLSTM_CHUNK = 128
NORM_EPS = 1e-6
ADAM_LR = 0.001
ADAM_B1 = 0.9
ADAM_B2 = 0.999
ADAM_EPS = 1e-08
ADAM_WD = 0.01
ADAM_STEP = 10

ROW_TILE = 256
WIDE_TILE = 512
LANES = 128
NEG = -1e30
VMEM_LIMIT = 56 * 1024 * 1024

A_W = 768
M_W = 2048
IF_W = 128
G_W = 2048
CAT_W = A_W + M_W + IF_W + G_W

R_SHIFT_M, R_SCALE_M, R_GATE_M, R_SHIFT_F, R_SCALE_F, R_GATE_F = 0, 1, 2, 3, 4, 5
R_G_PRE_MIX, R_G_POST_MIX, R_G_PRE_FFN, R_G_POST_FFN = 6, 7, 8, 9


def _dot(a, b):
    return jnp.dot(a, b, preferred_element_type=F32)


def _dot_nt(a, b):
    return lax.dot_general(a, b, (((1,), (1,)), ((), ())), preferred_element_type=F32)


def _dot_tn(a, b):
    return lax.dot_general(a, b, (((0,), (0,)), ((), ())), preferred_element_type=F32)


def _recip(x):
    return 1.0 / x


def _sigmoid(x):
    return _recip(1.0 + jnp.exp(-x))


def _colsum(x):
    return jnp.sum(x, axis=0, keepdims=True)


def _rowmean(x):
    return jnp.mean(x, axis=-1, keepdims=True)


def _params(sem=None, vmem=VMEM_LIMIT):
    kw = dict(vmem_limit_bytes=vmem)
    if sem is not None:
        kw["dimension_semantics"] = sem
    return pltpu.CompilerParams(**kw)


def _full(shape):
    nd = len(shape)
    return pl.BlockSpec(shape, lambda *_: (0,) * nd)


def _pre_proj(x, vecs, w_cat):
    S = x.shape[0]
    tm = WIDE_TILE

    def body(x_ref, v_ref, w_ref, h_ref, pa_ref, pm_ref, pif_ref, pg_ref):
        xv = x_ref[...]
        r = lax.rsqrt(_rowmean(xv * xv) + NORM_EPS)
        h = (xv * r * v_ref[R_G_PRE_MIX:R_G_PRE_MIX + 1, :]) * (1.0 + v_ref[R_SCALE_M:R_SCALE_M + 1, :]) \
            + v_ref[R_SHIFT_M:R_SHIFT_M + 1, :]
        hb = h.astype(BF16)
        h_ref[...] = hb
        pa_ref[...] = _dot(hb, w_ref[:, 0:A_W])
        pm_ref[...] = _dot(hb, w_ref[:, A_W:A_W + M_W])
        pif_ref[...] = _dot(hb, w_ref[:, A_W + M_W:A_W + M_W + IF_W])
        pg_ref[...] = _dot(hb, w_ref[:, A_W + M_W + IF_W:CAT_W]).astype(BF16)

    row = lambda w: pl.BlockSpec((tm, w), lambda i: (i, 0))
    return pl.pallas_call(
        body, name="pre_proj", grid=(S // tm,),
        in_specs=[row(D_MODEL), _full(vecs.shape), _full(w_cat.shape)],
        out_specs=[row(D_MODEL), row(A_W), row(M_W), row(IF_W), row(G_W)],
        out_shape=[jax.ShapeDtypeStruct((S, D_MODEL), BF16), jax.ShapeDtypeStruct((S, A_W), F32),
                   jax.ShapeDtypeStruct((S, M_W), F32), jax.ShapeDtypeStruct((S, IF_W), F32),
                   jax.ShapeDtypeStruct((S, G_W), BF16)],
        compiler_params=_params(("parallel",)),
    )(x, vecs, w_cat)


def _mix_fwd(x, ya, ym, pg, vecs, w_ba, w_bm, w_out):
    S = x.shape[0]
    tm = WIDE_TILE

    def body(x_ref, ya_ref, ym_ref, pg_ref, v_ref, wba_ref, wbm_ref, wout_ref,
             x1_ref, merged_ref, mix_ref, pa_ref, pb_ref):
        pa = _dot(ya_ref[...], wba_ref[...])
        pb = _dot(ym_ref[...], wbm_ref[...])
        merged = _sigmoid(pg_ref[:, 0:D_MODEL].astype(F32)) * pa + _sigmoid(pg_ref[:, D_MODEL:G_W].astype(F32)) * pb
        mb = merged.astype(BF16)
        mix = _dot(mb, wout_ref[...])
        r = lax.rsqrt(_rowmean(mix * mix) + NORM_EPS)
        x1_ref[...] = x_ref[...] + v_ref[R_GATE_M:R_GATE_M + 1, :] * (mix * r * v_ref[R_G_POST_MIX:R_G_POST_MIX + 1, :])
        merged_ref[...] = mb
        mix_ref[...] = mix
        pa_ref[...] = pa.astype(BF16)
        pb_ref[...] = pb.astype(BF16)

    row = lambda w: pl.BlockSpec((tm, w), lambda i: (i, 0))
    sd = lambda w, dt: jax.ShapeDtypeStruct((S, w), dt)
    return pl.pallas_call(
        body, name="mix_fwd", grid=(S // tm,),
        in_specs=[row(D_MODEL), row(512), row(512), row(G_W), _full(vecs.shape), _full(w_ba.shape),
                  _full(w_bm.shape), _full(w_out.shape)],
        out_specs=[row(D_MODEL)] * 5,
        out_shape=[sd(D_MODEL, F32), sd(D_MODEL, BF16), sd(D_MODEL, F32), sd(D_MODEL, BF16), sd(D_MODEL, BF16)],
        compiler_params=_params(("parallel",)),
    )(x, ya, ym, pg, vecs, w_ba, w_bm, w_out)


def _ffn_fwd_bwd(x1, tgt, vecs, w_gate, w_up, w_down):
    S = x1.shape[0]
    tm = ROW_TILE

    def body(x1_ref, tgt_ref, v_ref, wg_hbm, wu_hbm, wd_hbm,
             dx1_ref, h2_ref, hid_ref, da_ref, du_ref, dff_ref, acc_ref, loss_ref,
             wg, wu, wd, sem):
        i = pl.program_id(0)

        @pl.when(i == 0)
        def _():
            cps = [pltpu.make_async_copy(wg_hbm, wg, sem.at[0]), pltpu.make_async_copy(wu_hbm, wu, sem.at[1]),
                   pltpu.make_async_copy(wd_hbm, wd, sem.at[2])]
            for cp in cps:
                cp.start()
            for cp in cps:
                cp.wait()
            acc_ref[...] = jnp.zeros_like(acc_ref)
            loss_ref[...] = jnp.zeros_like(loss_ref)

        vrow = lambda r: v_ref[r:r + 1, :]
        x1v = x1_ref[...]
        r3 = lax.rsqrt(_rowmean(x1v * x1v) + NORM_EPS)
        x1hat = x1v * r3
        xn3 = x1hat * vrow(R_G_PRE_FFN)
        h2b = (xn3 * (1.0 + vrow(R_SCALE_F)) + vrow(R_SHIFT_F)).astype(BF16)
        h2_ref[...] = h2b
        a = _dot_nt(h2b, wg[...])
        u = _dot_nt(h2b, wu[...])
        sg = _sigmoid(a)
        sil = a * sg
        hidb = (sil * u).astype(BF16)
        hid_ref[...] = hidb
        ff = _dot(hidb, wd[...])
        r4 = lax.rsqrt(_rowmean(ff * ff) + NORM_EPS)
        ffhat = ff * r4
        n4 = ffhat * vrow(R_G_POST_FFN)
        err = x1v + vrow(R_GATE_F) * n4 - tgt_ref[...]
        loss_ref[...] += jnp.sum(err * err) * (0.5 / D_MODEL)
        dy = err * (1.0 / D_MODEL)
        acc_ref[0:1, :] += _colsum(dy * n4)
        dn4 = dy * vrow(R_GATE_F)
        acc_ref[1:2, :] += _colsum(dn4 * ffhat)
        dffhat = dn4 * vrow(R_G_POST_FFN)
        dffb = (r4 * (dffhat - ffhat * _rowmean(dffhat * ffhat))).astype(BF16)
        dff_ref[...] = dffb
        dhid = _dot_nt(dffb, wd[...])
        dub = (dhid * sil).astype(BF16)
        dab = (dhid * u * (sg * (1.0 + a * (1.0 - sg)))).astype(BF16)
        da_ref[...] = dab
        du_ref[...] = dub
        dh2 = _dot(dab, wg[...]) + _dot(dub, wu[...])
        acc_ref[2:3, :] += _colsum(dh2 * xn3)
        acc_ref[3:4, :] += _colsum(dh2)
        dxn3 = dh2 * (1.0 + vrow(R_SCALE_F))
        acc_ref[4:5, :] += _colsum(dxn3 * x1hat)
        dx1hat = dxn3 * vrow(R_G_PRE_FFN)
        dx1_ref[...] = dy + r3 * (dx1hat - x1hat * _rowmean(dx1hat * x1hat))

    row = lambda w: pl.BlockSpec((tm, w), lambda i: (i, 0))
    sd = lambda w, dt: jax.ShapeDtypeStruct((S, w), dt)
    anyspec = pl.BlockSpec(memory_space=pl.ANY)
    return pl.pallas_call(
        body, name="ffn_fwd_bwd", grid=(S // tm,),
        in_specs=[row(D_MODEL), row(D_MODEL), _full(vecs.shape), anyspec, anyspec, anyspec],
        out_specs=[row(D_MODEL), row(D_MODEL), row(D_FF), row(D_FF), row(D_FF), row(D_MODEL),
                   _full((8, D_MODEL)), _full((8, LANES))],
        out_shape=[sd(D_MODEL, F32), sd(D_MODEL, BF16), sd(D_FF, BF16), sd(D_FF, BF16), sd(D_FF, BF16),
                   sd(D_MODEL, BF16), jax.ShapeDtypeStruct((8, D_MODEL), F32), jax.ShapeDtypeStruct((8, LANES), F32)],
        scratch_shapes=[pltpu.VMEM(w_gate.shape, BF16), pltpu.VMEM(w_up.shape, BF16), pltpu.VMEM(w_down.shape, BF16),
                        pltpu.SemaphoreType.DMA((3,))],
        compiler_params=_params(("arbitrary",)),
    )(x1, tgt, vecs, w_gate, w_up, w_down)


def _mix_bwd(dx1, mix, pa, pb, pg, vecs, w_ba, w_bm, w_out):
    S = dx1.shape[0]
    tm = WIDE_TILE

    def body(dx1_ref, mix_ref, pa_ref, pb_ref, pg_ref, v_ref, wba_ref, wbm_ref, wout_ref,
             dmix_ref, dpa_ref, dpb_ref, dg_ref, dya_ref, dym_ref, acc_ref):
        i = pl.program_id(0)

        @pl.when(i == 0)
        def _():
            acc_ref[...] = jnp.zeros_like(acc_ref)

        vrow = lambda r: v_ref[r:r + 1, :]
        dx1v = dx1_ref[...]
        mix = mix_ref[...]
        r2 = lax.rsqrt(_rowmean(mix * mix) + NORM_EPS)
        mixhat = mix * r2
        acc_ref[0:1, :] += _colsum(dx1v * (mixhat * vrow(R_G_POST_MIX)))
        dn2 = dx1v * vrow(R_GATE_M)
        acc_ref[1:2, :] += _colsum(dn2 * mixhat)
        dmixhat = dn2 * vrow(R_G_POST_MIX)
        dmixb = (r2 * (dmixhat - mixhat * _rowmean(dmixhat * mixhat))).astype(BF16)
        dmix_ref[...] = dmixb
        dmerged = _dot_nt(dmixb, wout_ref[...])
        sa = _sigmoid(pg_ref[:, 0:D_MODEL].astype(F32))
        sm = _sigmoid(pg_ref[:, D_MODEL:G_W].astype(F32))
        dpab = (dmerged * sa).astype(BF16)
        dpbb = (dmerged * sm).astype(BF16)
        dpa_ref[...] = dpab
        dpb_ref[...] = dpbb
        dg_ref[:, 0:D_MODEL] = (dmerged * pa_ref[...].astype(F32) * (sa * (1.0 - sa))).astype(BF16)
        dg_ref[:, D_MODEL:G_W] = (dmerged * pb_ref[...].astype(F32) * (sm * (1.0 - sm))).astype(BF16)
        dya_ref[...] = _dot_nt(dpab, wba_ref[...])
        dym_ref[...] = _dot_nt(dpbb, wbm_ref[...])

    row = lambda w: pl.BlockSpec((tm, w), lambda i: (i, 0))
    sd = lambda w, dt: jax.ShapeDtypeStruct((S, w), dt)
    return pl.pallas_call(
        body, name="mix_bwd", grid=(S // tm,),
        in_specs=[row(D_MODEL), row(D_MODEL), row(D_MODEL), row(D_MODEL), row(G_W), _full(vecs.shape),
                  _full(w_ba.shape), _full(w_bm.shape), _full(w_out.shape)],
        out_specs=[row(D_MODEL), row(D_MODEL), row(D_MODEL), row(G_W), row(512), row(512), _full((8, D_MODEL))],
        out_shape=[sd(D_MODEL, BF16), sd(D_MODEL, BF16), sd(D_MODEL, BF16), sd(G_W, BF16), sd(512, F32), sd(512, F32),
                   jax.ShapeDtypeStruct((8, D_MODEL), F32)],
        compiler_params=_params(("arbitrary",)),
    )(dx1, mix, pa, pb, pg, vecs, w_ba, w_bm, w_out)


def _pre_bwd(pieces, x, dx1, vecs, w_cat):
    S = x.shape[0]
    tm = WIDE_TILE
    n = len(pieces)
    starts = [sum(p.shape[1] for p in pieces[:k]) for k in range(n + 1)]

    def body(*refs):
        p_refs = refs[:n]
        x_ref, dx1_ref, v_ref, w_ref, dx_ref, acc_ref = refs[n:]
        i = pl.program_id(0)

        @pl.when(i == 0)
        def _():
            acc_ref[...] = jnp.zeros_like(acc_ref)

        vrow = lambda r: v_ref[r:r + 1, :]
        dh = _dot_nt(p_refs[0][...], w_ref[:, starts[0]:starts[1]])
        for k in range(1, n):
            dh = dh + _dot_nt(p_refs[k][...], w_ref[:, starts[k]:starts[k + 1]])
        xv = x_ref[...]
        r1 = lax.rsqrt(_rowmean(xv * xv) + NORM_EPS)
        xhat = xv * r1
        acc_ref[0:1, :] += _colsum(dh * (xhat * vrow(R_G_PRE_MIX)))
        acc_ref[1:2, :] += _colsum(dh)
        dxn = dh * (1.0 + vrow(R_SCALE_M))
        acc_ref[2:3, :] += _colsum(dxn * xhat)
        dxhat = dxn * vrow(R_G_PRE_MIX)
        dx_ref[...] = dx1_ref[...] + r1 * (dxhat - xhat * _rowmean(dxhat * xhat))

    row = lambda w: pl.BlockSpec((tm, w), lambda i: (i, 0))
    return pl.pallas_call(
        body, name="pre_bwd", grid=(S // tm,),
        in_specs=[row(p.shape[1]) for p in pieces] + [row(D_MODEL), row(D_MODEL), _full(vecs.shape), _full(w_cat.shape)],
        out_specs=[row(D_MODEL), _full((8, D_MODEL))],
        out_shape=[jax.ShapeDtypeStruct((S, D_MODEL), F32), jax.ShapeDtypeStruct((8, D_MODEL), F32)],
        compiler_params=_params(("arbitrary",)),
    )(*pieces, x, dx1, vecs, w_cat)


def _matmul_tn(a, b, tn, name, ts=1024):
    S, K = a.shape
    N = b.shape[1]
    n_s = S // ts

    def body(a_ref, b_ref, o_ref, acc_ref):
        s = pl.program_id(1)

        @pl.when(s == 0)
        def _():
            acc_ref[...] = jnp.zeros_like(acc_ref)

        acc_ref[...] += _dot_tn(a_ref[...], b_ref[...])

        @pl.when(s == n_s - 1)
        def _():
            o_ref[...] = acc_ref[...].astype(BF16)

    return pl.pallas_call(
        body, name=name, grid=(N // tn, n_s),
        in_specs=[pl.BlockSpec((ts, K), lambda j, s: (s, 0)), pl.BlockSpec((ts, tn), lambda j, s: (s, j))],
        out_specs=pl.BlockSpec((K, tn), lambda j, s: (0, j)),
        out_shape=jax.ShapeDtypeStruct((K, N), BF16),
        scratch_shapes=[pltpu.VMEM((K, tn), F32)],
        compiler_params=_params(("parallel", "arbitrary")),
    )(a, b)


def _rope_swap(t):
    lane = lax.broadcasted_iota(jnp.int32, t.shape, 1)
    first = (lane & (HEAD_DIM - 1)) < (HEAD_DIM // 2)
    return jnp.where(first, pltpu.roll(t, LANES - HEAD_DIM // 2, 1), pltpu.roll(t, HEAD_DIM // 2, 1))


def _rope(t, cos, sin_signed):
    return t * cos + _rope_swap(t) * sin_signed


def _rope_t(d, cos, sin_signed):
    return d * cos + _rope_swap(d * sin_signed)


def _to_kv_lanes(chunk, p, h):
    lane = lax.broadcasted_iota(jnp.int32, chunk.shape, 1)
    src = chunk if p == h else pltpu.roll(chunk, HEAD_DIM, 1)
    return jnp.where((lane >> 6) == h, src, jnp.zeros_like(src))


def _from_kv_lanes(o_a, o_b, h):
    lane = lax.broadcasted_iota(jnp.int32, o_a.shape, 1)
    a = o_a if h == 0 else pltpu.roll(o_a, HEAD_DIM, 1)
    b = o_b if h == 1 else pltpu.roll(o_b, HEAD_DIM, 1)
    return jnp.where(lane < HEAD_DIM, a, b)


def _band_bias(n):
    blk = ATTN_BLOCK
    qi = lax.broadcasted_iota(jnp.int32, (blk, 2 * blk), 0)
    kj = lax.broadcasted_iota(jnp.int32, (blk, 2 * blk), 1)
    seen = (kj > qi) & (kj <= qi + blk) & ((n > 0) | (kj >= blk))
    return jnp.concatenate([jnp.where(seen, 0.0, NEG)] * N_Q_HEADS, axis=0)


def _stack_heads(chunks, h, dtype):
    parts = []
    for g in range(4):
        j = 4 * h + g
        parts.append(_to_kv_lanes(chunks[j // 2], j % 2, h))
    return jnp.concatenate(parts, axis=0).astype(dtype)


def _fused_call(parts, name, n_steps):
    counts = [(len(p["in_specs"]), len(p["out_specs"]), len(p["scratch"])) for p in parts]
    n_in, n_out = sum(c[0] for c in counts), sum(c[1] for c in counts)

    def kernel_fn(*refs):
        i = pl.program_id(0)
        groups, a, b, c = [], 0, n_in, n_in + n_out
        for ci, co, cs in counts:
            groups.append(refs[a:a + ci] + refs[b:b + co] + refs[c:c + cs])
            a, b, c = a + ci, b + co, c + cs
        for p, g in zip(parts, groups):
            p["init"](i, *g)
        for p, g in zip(parts, groups):
            p["body"](i, *g)

    flat = lambda key: [v for p in parts for v in p[key]]
    res = pl.pallas_call(
        kernel_fn, name=name, grid=(n_steps,), in_specs=flat("in_specs"), out_specs=flat("out_specs"),
        out_shape=flat("out_shape"), scratch_shapes=flat("scratch"), compiler_params=_params(("arbitrary",)),
    )(*flat("operands"))
    out, pos = [], 0
    for _, co, _ in counts:
        out.append(res[pos:pos + co])
        pos += co
    return out


def _attn_fwd_part(pa, cos, sin, sinks):
    S = pa.shape[0]
    blk = ATTN_BLOCK
    nb = S // blk

    def body(n, sink_ref, cur_ref, prev_ref, cos_ref, sin_ref, cosp_ref, sinp_ref,
             ya_ref, qr_ref, kr_ref, vb_ref, lse_ref):
        cos_c, sin_c = cos_ref[...], sin_ref[...]
        qch = [_rope(cur_ref[:, c * LANES:(c + 1) * LANES], cos_c, sin_c) * (HEAD_DIM ** -0.5) for c in range(4)]
        for c in range(4):
            qr_ref[:, c * LANES:(c + 1) * LANES] = qch[c].astype(BF16)
        k_cur = _rope(cur_ref[:, 512:640], cos_c, sin_c).astype(BF16)
        k_prev = _rope(prev_ref[:, 0:LANES], cosp_ref[...], sinp_ref[...]).astype(BF16)
        v_cur = cur_ref[:, 640:768].astype(BF16)
        v_prev = prev_ref[:, LANES:2 * LANES].astype(BF16)
        kr_ref[...] = k_cur
        vb_ref[...] = v_cur
        K = jnp.concatenate([k_prev, k_cur], axis=0)
        V = jnp.concatenate([v_prev, v_cur], axis=0)
        lane = lax.broadcasted_iota(jnp.int32, (blk, LANES), 1)
        s = jnp.concatenate([_dot_nt(_stack_heads(qch, h, BF16), K) for h in range(2)], axis=0)
        s = s + _band_bias(n)
        rowmax = jnp.max(s, axis=1, keepdims=True)
        hd = lambda x, j: x[j * blk:(j + 1) * blk]
        m = jnp.concatenate([jnp.maximum(hd(rowmax, j), sink_ref[j]) for j in range(N_Q_HEADS)], axis=0)
        p = jnp.exp(s - m)
        pb = p.astype(BF16)
        den = _rowsum_mxu(pb) + jnp.concatenate([jnp.exp(sink_ref[j] - hd(m, j)) for j in range(N_Q_HEADS)], axis=0)
        o = jnp.concatenate([_dot(pb[4 * h * blk:4 * (h + 1) * blk], V) for h in range(2)], axis=0) * _recip(den)
        lse = m + jnp.log(den)
        outs = [o[j * blk:(j + 1) * blk, :] for j in range(N_Q_HEADS)]
        lse_tile = jnp.zeros((blk, LANES), F32)
        for j in range(N_Q_HEADS):
            lse_tile = jnp.where(lane == j, lse[j * blk:(j + 1) * blk, :], lse_tile)
        for c in range(4):
            ya_ref[:, c * LANES:(c + 1) * LANES] = _from_kv_lanes(outs[2 * c], outs[2 * c + 1], c // 2).astype(BF16)
        lse_ref[...] = lse_tile

    prev = lambda n: jnp.maximum(n - 1, 0)
    sd = lambda w, dt: jax.ShapeDtypeStruct((S, w), dt)
    return dict(
        init=lambda n, *refs: None, body=body, scratch=[], operands=[sinks, pa, pa, cos, sin, cos, sin],
        in_specs=[pl.BlockSpec(memory_space=pltpu.SMEM),
                  pl.BlockSpec((blk, A_W), lambda n: (n, 0)),
                  pl.BlockSpec((blk, 256), lambda n: (prev(n), 2)),
                  pl.BlockSpec((blk, LANES), lambda n: (n, 0)), pl.BlockSpec((blk, LANES), lambda n: (n, 0)),
                  pl.BlockSpec((blk, LANES), lambda n: (prev(n), 0)), pl.BlockSpec((blk, LANES), lambda n: (prev(n), 0))],
        out_specs=[pl.BlockSpec((blk, 512), lambda n: (n, 0)), pl.BlockSpec((blk, 512), lambda n: (n, 0)),
                   pl.BlockSpec((blk, LANES), lambda n: (n, 0)), pl.BlockSpec((blk, LANES), lambda n: (n, 0)),
                   pl.BlockSpec((blk, LANES), lambda n: (n, 0))],
        out_shape=[sd(512, BF16), sd(512, BF16), sd(LANES, BF16), sd(LANES, BF16), sd(LANES, F32)])


def _attn_bwd_part(dya, qr, kr, vb, lse, cos, sin, sinks):
    S = dya.shape[0]
    blk = ATTN_BLOCK
    nb = S // blk

    def init(n, sink_ref, dya_ref, qr_ref, kc_ref, kp_ref, vc_ref, vp_ref, lse_ref, cos_ref, sin_ref, cosp_ref, sinp_ref,
             dq_ref, dkv_ref, last_ref, dsink_ref, ck, cv):
        @pl.when(n == 0)
        def _():
            ck[...] = jnp.zeros_like(ck)
            cv[...] = jnp.zeros_like(cv)
            dsink_ref[...] = jnp.zeros_like(dsink_ref)

    def body(n, sink_ref, dya_ref, qr_ref, kc_ref, kp_ref, vc_ref, vp_ref, lse_ref, cos_ref, sin_ref, cosp_ref, sinp_ref,
             dq_ref, dkv_ref, last_ref, dsink_ref, ck, cv):
        K = jnp.concatenate([kp_ref[...], kc_ref[...]], axis=0)
        V = jnp.concatenate([vp_ref[...], vc_ref[...]], axis=0)
        qch = [qr_ref[:, c * LANES:(c + 1) * LANES] for c in range(4)]
        dch = [dya_ref[:, c * LANES:(c + 1) * LANES] for c in range(4)]
        lse_tile = lse_ref[...]
        lane8 = lax.broadcasted_iota(jnp.int32, (8, LANES), 1)
        grp = lambda x, h: x[4 * h * blk:4 * (h + 1) * blk]
        qs = jnp.concatenate([_stack_heads(qch, h, BF16) for h in range(2)], axis=0)
        dos = jnp.concatenate([_stack_heads(dch, h, BF16) for h in range(2)], axis=0)
        lse_col = jnp.concatenate([lse_tile[:, j:j + 1] for j in range(N_Q_HEADS)], axis=0)
        s = jnp.concatenate([_dot_nt(grp(qs, h), K) for h in range(2)], axis=0)
        p = jnp.exp(s + _band_bias(n) - lse_col)
        dp = jnp.concatenate([_dot_nt(grp(dos, h), V) for h in range(2)], axis=0)
        delta = jnp.sum(p * dp, axis=1, keepdims=True)
        dsb = (p * (dp - delta)).astype(BF16)
        pb = p.astype(BF16)
        dq = jnp.concatenate([_dot(grp(dsb, h), K) for h in range(2)], axis=0)
        dk_acc = _dot_tn(grp(dsb, 0), grp(qs, 0)) + _dot_tn(grp(dsb, 1), grp(qs, 1))
        dv_acc = _dot_tn(grp(pb, 0), grp(dos, 0)) + _dot_tn(grp(pb, 1), grp(dos, 1))
        dqs = [dq[j * blk:(j + 1) * blk, :] for j in range(N_Q_HEADS)]
        dsink = jnp.zeros((8, LANES), F32)
        for j in range(N_Q_HEADS):
            rows = slice(j * blk, (j + 1) * blk)
            ps_delta = jnp.exp(sink_ref[j] - lse_col[rows]) * delta[rows]
            dsink = jnp.where(lane8 == j, dsink - jnp.sum(ps_delta), dsink)
        dsink_ref[...] += dsink
        cos_c, sin_c = cos_ref[...], sin_ref[...]
        for c in range(4):
            dqc = _from_kv_lanes(dqs[2 * c], dqs[2 * c + 1], c // 2) * (HEAD_DIM ** -0.5)
            dq_ref[:, c * LANES:(c + 1) * LANES] = _rope_t(dqc, cos_c, sin_c).astype(BF16)
        dkv_ref[:, 0:LANES] = _rope_t(dk_acc[0:blk, :] + ck[...], cosp_ref[...], sinp_ref[...]).astype(BF16)
        dkv_ref[:, LANES:2 * LANES] = (dv_acc[0:blk, :] + cv[...]).astype(BF16)
        ck[...] = dk_acc[blk:2 * blk, :]
        cv[...] = dv_acc[blk:2 * blk, :]
        last_ref[:, 0:LANES] = _rope_t(dk_acc[blk:2 * blk, :], cos_c, sin_c).astype(BF16)
        last_ref[:, LANES:2 * LANES] = dv_acc[blk:2 * blk, :].astype(BF16)

    prev = lambda n: jnp.maximum(n - 1, 0)
    same = lambda n: n
    bs = lambda w, f: pl.BlockSpec((blk, w), lambda n: (f(n), 0))
    return dict(
        init=init, body=body, operands=[sinks, dya, qr, kr, kr, vb, vb, lse, cos, sin, cos, sin],
        in_specs=[pl.BlockSpec(memory_space=pltpu.SMEM),
                  bs(512, same), bs(512, same), bs(LANES, same), bs(LANES, prev), bs(LANES, same), bs(LANES, prev),
                  bs(LANES, same), bs(LANES, same), bs(LANES, same), bs(LANES, prev), bs(LANES, prev)],
        out_specs=[bs(512, same), bs(256, prev), _full((blk, 256)), _full((8, LANES))],
        out_shape=[jax.ShapeDtypeStruct((S, 512), BF16), jax.ShapeDtypeStruct((S, 256), BF16),
                   jax.ShapeDtypeStruct((blk, 256), BF16), jax.ShapeDtypeStruct((8, LANES), F32)],
        scratch=[pltpu.VMEM((blk, LANES), F32), pltpu.VMEM((blk, LANES), F32)])


def _split3(x):
    hi = x.astype(BF16)
    r1 = x - hi.astype(F32)
    mid = r1.astype(BF16)
    lo = (r1 - mid.astype(F32)).astype(BF16)
    return hi, mid, lo


def _tri_matmul(tri_b, x):
    hi, mid, lo = _split3(x)
    return _dot(tri_b, hi) + _dot(tri_b, mid) + _dot(tri_b, lo)


def _log_sigmoid(x):
    return jnp.minimum(x, 0.0) - jnp.log(1.0 + jnp.exp(-jnp.abs(x)))


def _shift_rows(cur, seam, k, down):
    L = cur.shape[0]
    row8 = lax.broadcasted_iota(jnp.int32, seam.shape, 0)
    if down:
        mixed = jnp.concatenate([cur[:L - 8], jnp.where(row8 >= 8 - k, seam, cur[L - 8:])], axis=0)
        return pltpu.roll(mixed, k, 0)
    mixed = jnp.concatenate([jnp.where(row8 < k, seam, cur[:8]), cur[8:]], axis=0)
    return pltpu.roll(mixed, L - k, 0)


def _conv_fwd(cur, tail, cw_ref):
    z = cw_ref[4:5, :]
    for k in range(3, 0, -1):
        z = z + _shift_rows(cur, tail, k, True) * cw_ref[3 - k:4 - k, :]
    return z + cur * cw_ref[3:4, :]


def _rowsum_mxu(x, two_pass=False):
    ones = jnp.ones((x.shape[1], LANES), BF16)
    hi = x.astype(BF16)
    s = _dot(hi, ones)
    if two_pass:
        s = s + _dot((x - hi.astype(F32)).astype(BF16), ones)
    return s


def _stack(f):
    return jnp.concatenate([f(h) for h in range(MLSTM_HEADS)], axis=0)


def _head(x, h):
    L = x.shape[0] // MLSTM_HEADS
    return x[h * L:(h + 1) * L]


def _mlstm_heads_fwd(qk, cur_ref, gt, b_all, c_prev, nmv, tri, eye):
    L = qk.shape[0]
    HD = MLSTM_HEAD_DIM
    W4 = MLSTM_HEADS * HD
    col2row = lambda x: jnp.sum(jnp.where(eye, x, 0.0), axis=0, keepdims=True)
    b_col = _stack(lambda h: b_all[:, 4 + h:5 + h])
    i_col = _stack(lambda h: gt[:, h:h + 1])
    b_row = _stack(lambda h: jnp.broadcast_to(col2row(b_all[:, 4 + h:5 + h]), (L, L)))
    i_row = _stack(lambda h: jnp.broadcast_to(col2row(gt[:, h:h + 1]), (L, L)))
    bl = _stack(lambda h: jnp.broadcast_to(b_all[L - 1:L, 4 + h:5 + h], (L, 1)))
    m_prev = _stack(lambda h: jnp.broadcast_to(nmv[4 + h:5 + h, 0:1], (L, 1)))
    n_prev = _stack(lambda h: jnp.broadcast_to(nmv[h:h + 1, :], (L, HD)))
    tri4 = jnp.concatenate([tri] * MLSTM_HEADS, axis=0)
    Dm = jnp.where(tri4, b_col - b_row + i_row, NEG)
    inter = b_col + m_prev
    m_t = jnp.maximum(inter, jnp.max(Dm, axis=1, keepdims=True))
    W = jnp.exp(Dm - m_t)
    e_t = jnp.exp(inter - m_t)
    q = _stack(lambda h: qk[:, h * HD:(h + 1) * HD])
    k = _stack(lambda h: qk[:, W4 + h * HD:W4 + (h + 1) * HD]) * (HD ** -0.5)
    v = _stack(lambda h: cur_ref[:, 2 * W4 + h * HD:2 * W4 + (h + 1) * HD])
    qb, kb, vb = q.astype(BF16), k.astype(BF16), v.astype(BF16)
    Sc = _stack(lambda h: _dot_nt(_head(qb, h), _head(kb, h))) * W
    Scb = Sc.astype(BF16)
    cb = [c.astype(BF16) for c in c_prev]
    P1 = _stack(lambda h: _dot(_head(qb, h), cb[h]))
    num = _stack(lambda h: _dot(_head(Scb, h), _head(vb, h))) + e_t * P1
    qn = _rowsum_mxu(q * n_prev)
    den = _rowsum_mxu(Scb) + e_t * qn
    floor = jnp.broadcast_to(jnp.exp(-m_t), den.shape)
    inv_g = _recip(jnp.maximum(jnp.abs(den), floor))
    hv = num * inv_g
    a_col = bl - b_col + i_col
    a_max = _stack(lambda h: jnp.broadcast_to(jnp.max(_head(a_col, h), axis=0, keepdims=True), (L, 1)))
    m_new = jnp.maximum(bl + m_prev, a_max)
    dec = jnp.exp(bl + m_prev - m_new)
    u_col = jnp.exp(a_col - m_new)
    return dict(W=W, e_t=e_t, q=q, k=k, v=v, qb=qb, kb=kb, vb=vb, cb=cb, Sc=Sc, Scb=Scb, P1=P1, qn=qn, den=den,
                floor=floor, inv_g=inv_g, hv=hv, n_prev=n_prev, m_new=m_new, dec=dec, u_col=u_col)


def _mlstm_fwd_part(pm, pif, cw, sv):
    S = pm.shape[0]
    L = MLSTM_CHUNK
    nc = S // L
    HD = MLSTM_HEAD_DIM
    W4 = MLSTM_HEADS * HD

    def init(c, cur_ref, pif_ref, cw_ref, sv_ref, ym_ref, z_ref, cst_ref, nst_ref, C, nm, tail):
        @pl.when(c == 0)
        def _():
            C[...] = jnp.zeros_like(C)
            nm[...] = jnp.zeros_like(nm)
            tail[...] = jnp.zeros_like(tail)

    def body(c, cur_ref, pif_ref, cw_ref, sv_ref, ym_ref, z_ref, cst_ref, nst_ref, C, nm, tail):
        z = _conv_fwd(cur_ref[:, 0:2 * W4], tail[...], cw_ref)
        tail[...] = cur_ref[L - 8:L, 0:2 * W4]
        z_ref[...] = z
        qk = z * _sigmoid(z)
        gt = pif_ref[...] + sv_ref[1:2, 0:LANES]
        r_i = lax.broadcasted_iota(jnp.int32, (L, L), 0)
        c_i = lax.broadcasted_iota(jnp.int32, (L, L), 1)
        tri = c_i <= r_i
        eye = c_i == r_i
        b_all = _tri_matmul(tri.astype(BF16), _log_sigmoid(gt))
        nmv = nm[...]
        nst_ref[0] = nmv
        c_prev = [C[h] for h in range(MLSTM_HEADS)]
        f = _mlstm_heads_fwd(qk, cur_ref, gt, b_all, c_prev, nmv, tri, eye)
        hv = f["hv"]
        xc = hv - _rowsum_mxu(hv, True) * (1.0 / HD)
        hhat = xc * lax.rsqrt(_rowsum_mxu(xc * xc) * (1.0 / HD) + NORM_EPS)
        so = _sigmoid(_stack(lambda h: cur_ref[:, 3 * W4 + h * HD:3 * W4 + (h + 1) * HD]))
        wn = _stack(lambda h: jnp.broadcast_to(sv_ref[0:1, h * HD:(h + 1) * HD], (L, HD)))
        y = (so * hhat * wn).astype(BF16)
        kw = f["k"] * f["u_col"]
        kwb = kw.astype(BF16)
        n_new, m_new = [], []
        for h in range(MLSTM_HEADS):
            cst_ref[0, h] = c_prev[h]
            ym_ref[:, h * HD:(h + 1) * HD] = _head(y, h)
            dec = f["dec"][h * L:h * L + 1, :]
            C[h] = dec * c_prev[h] + _dot_tn(_head(kwb, h), _head(f["vb"], h))
            n_new.append(dec * nmv[h:h + 1, :] + _colsum(_head(kw, h)))
            m_new.append(jnp.broadcast_to(f["m_new"][h * L:h * L + 1, :], (1, LANES)))
        nm[...] = jnp.concatenate(n_new + m_new, axis=0)

    return dict(
        init=init, body=body, operands=[pm, pif, cw, sv],
        in_specs=[pl.BlockSpec((L, M_W), lambda c: (c, 0)),
                  pl.BlockSpec((L, IF_W), lambda c: (c, 0)), _full(cw.shape), _full(sv.shape)],
        out_specs=[pl.BlockSpec((L, W4), lambda c: (c, 0)), pl.BlockSpec((L, 2 * W4), lambda c: (c, 0)),
                   pl.BlockSpec((1, MLSTM_HEADS, HD, HD), lambda c: (c, 0, 0, 0)),
                   pl.BlockSpec((1, 8, LANES), lambda c: (c, 0, 0))],
        out_shape=[jax.ShapeDtypeStruct((S, W4), BF16), jax.ShapeDtypeStruct((S, 2 * W4), F32),
                   jax.ShapeDtypeStruct((nc, MLSTM_HEADS, HD, HD), F32), jax.ShapeDtypeStruct((nc, 8, LANES), F32)],
        scratch=[pltpu.VMEM((MLSTM_HEADS, HD, HD), F32), pltpu.VMEM((8, LANES), F32), pltpu.VMEM((8, 2 * W4), F32)])


def _mlstm_bwd_part(pm, zc, pif, cw, sv, dym, cst, nst):
    S = pm.shape[0]
    L = MLSTM_CHUNK
    nc = S // L
    HD = MLSTM_HEAD_DIM
    W4 = MLSTM_HEADS * HD

    def init(r, cur_ref, z_ref, pif_ref, cw_ref, sv_ref, dym_ref, cst_ref, nst_ref,
             dm_ref, dif_ref, dcw_ref, dsv_ref, dC, dn, dz_next, dqk):
        @pl.when(r == 0)
        def _():
            dC[...] = jnp.zeros_like(dC)
            dn[...] = jnp.zeros_like(dn)
            dz_next[...] = jnp.zeros_like(dz_next)
            dcw_ref[...] = jnp.zeros_like(dcw_ref)
            dsv_ref[...] = jnp.zeros_like(dsv_ref)

    def body(r, cur_ref, z_ref, pif_ref, cw_ref, sv_ref, dym_ref, cst_ref, nst_ref,
             dm_ref, dif_ref, dcw_ref, dsv_ref, dC, dn, dz_next, dqk):
        z = z_ref[...]
        sgz = _sigmoid(z)
        qk = z * sgz
        gt = pif_ref[...] + sv_ref[1:2, 0:LANES]
        r_i = lax.broadcasted_iota(jnp.int32, (L, L), 0)
        c_i = lax.broadcasted_iota(jnp.int32, (L, L), 1)
        tri = c_i <= r_i
        eye = c_i == r_i
        b_all = _tri_matmul(tri.astype(BF16), _log_sigmoid(gt))
        lane = lax.broadcasted_iota(jnp.int32, (L, LANES), 1)
        rowl = lax.broadcasted_iota(jnp.int32, (L, 1), 0)
        nmv = nst_ref[0]
        heads = range(MLSTM_HEADS)
        c_prev = [cst_ref[0, h] for h in heads]
        f = _mlstm_heads_fwd(qk, cur_ref, gt, b_all, c_prev, nmv, tri, eye)
        hv, inv_g, den, e_t, u_col, n_prev = f["hv"], f["inv_g"], f["den"], f["e_t"], f["u_col"], f["n_prev"]
        q, k, v, qb, kb, vb, Sc, Scb, W = f["q"], f["k"], f["v"], f["qb"], f["kb"], f["vb"], f["Sc"], f["Scb"], f["W"]
        xc = hv - _rowsum_mxu(hv, True) * (1.0 / HD)
        rstd = lax.rsqrt(_rowsum_mxu(xc * xc) * (1.0 / HD) + NORM_EPS)
        hhat = xc * rstd
        wn = _stack(lambda h: jnp.broadcast_to(sv_ref[0:1, h * HD:(h + 1) * HD], (L, HD)))
        so = _sigmoid(_stack(lambda h: cur_ref[:, 3 * W4 + h * HD:3 * W4 + (h + 1) * HD]))
        dy = _stack(lambda h: dym_ref[:, h * HD:(h + 1) * HD])
        d_o = (dy * hhat * wn * (so * (1.0 - so))).astype(BF16)
        dln = dy * so
        dwn = dln * hhat
        dhhat = dln * wn
        m2 = _rowsum_mxu(dhhat * hhat) * (1.0 / HD)
        dh = rstd * (dhhat - _rowsum_mxu(dhhat) * (1.0 / HD) - hhat * m2)
        dnum = dh * inv_g
        active = jnp.abs(den) > f["floor"]
        dden = jnp.where(active, -(HD * NORM_EPS) * m2 * rstd * rstd * inv_g * jnp.where(den >= 0.0, 1.0, -1.0), 0.0)
        dnumb = dnum.astype(BF16)
        dSc = _stack(lambda h: _dot_nt(_head(dnumb, h), _head(vb, h))) + dden
        dA = (dSc * W).astype(BF16)
        G = dSc * Sc
        Gb = G.astype(BF16)
        ones = jnp.ones((L, LANES), BF16)
        Gr = _dot(Gb, ones)
        Gc = _stack(lambda h: _dot_tn(_head(Gb, h), ones))
        dCn = [dC[h] for h in heads]
        dCnb = [d.astype(BF16) for d in dCn]
        dnv = dn[...]
        dn_new = _stack(lambda h: jnp.broadcast_to(dnv[h:h + 1, :], (L, HD)))
        kdC = _stack(lambda h: _dot(_head(kb, h), dCnb[h]))
        vdC = _stack(lambda h: _dot_nt(_head(vb, h), dCnb[h]))
        dv = (_stack(lambda h: _dot_tn(_head(Scb, h), _head(dnumb, h))) + u_col * kdC).astype(BF16)
        dq = _stack(lambda h: _dot(_head(dA, h), _head(kb, h))) \
            + e_t * _stack(lambda h: _dot_nt(_head(dnumb, h), f["cb"][h])) + (e_t * dden) * n_prev
        dk = (_stack(lambda h: _dot_tn(_head(dA, h), _head(qb, h))) + u_col * (vdC + dn_new)) * (HD ** -0.5)
        E = (_rowsum_mxu(f["P1"] * dnum) + dden * f["qn"]) * e_t
        U = _rowsum_mxu(kdC * v + k * dn_new) * u_col
        qe = (q * e_t).astype(BF16)
        qd = (e_t * dden) * q
        di = Gc + U
        db = Gr + E - Gc - U
        di_tile = jnp.zeros((L, LANES), F32)
        db_tile = jnp.zeros((L, LANES), F32)
        dn_rows = []
        for h in heads:
            dec = f["dec"][h * L:h * L + 1, :]
            ddec = jnp.sum(dCn[h] * c_prev[h]) + jnp.sum(dnv[h:h + 1, :] * nmv[h:h + 1, :])
            dbl = ddec * dec + jnp.sum(_head(U, h), axis=0, keepdims=True)
            di_tile = jnp.where(lane == h, _head(di, h), di_tile)
            db_tile = jnp.where(lane == 4 + h, _head(db, h) + jnp.where(rowl == L - 1, dbl, 0.0), db_tile)
            dC[h] = dec * dCn[h] + _dot_tn(_head(qe, h), _head(dnumb, h))
            dn_rows.append(dec * dnv[h:h + 1, :] + _colsum(_head(qd, h)))
            dsv_ref[0:1, h * HD:(h + 1) * HD] += _colsum(_head(dwn, h))
            dqk[:, h * HD:(h + 1) * HD] = _head(dq, h)
            dqk[:, W4 + h * HD:W4 + (h + 1) * HD] = _head(dk, h)
            dm_ref[:, 2 * W4 + h * HD:2 * W4 + (h + 1) * HD] = _head(dv, h)
            dm_ref[:, 3 * W4 + h * HD:3 * W4 + (h + 1) * HD] = _head(d_o, h)
        dn[...] = jnp.concatenate(dn_rows + [jnp.zeros((8 - MLSTM_HEADS, LANES), F32)], axis=0)
        dlf = _tri_matmul((r_i <= c_i).astype(BF16), db_tile)
        dif = jnp.where(lane < 4, di_tile, jnp.where(lane < 8, dlf * (1.0 - _sigmoid(gt)), 0.0))
        dif_ref[...] = dif.astype(BF16)
        dsv_ref[1:2, 0:LANES] += _colsum(dif)
        dz = dqk[...] * (sgz * (1.0 + z * (1.0 - sgz)))
        dcw_ref[4:5, :] += _colsum(dz)
        u = cur_ref[:, 0:2 * W4]
        du_in = dz * cw_ref[3:4, :]
        dcw_ref[3:4, :] += _colsum(dz * u)
        for k in range(1, 4):
            up = _shift_rows(dz, dz_next[...], k, False)
            dcw_ref[3 - k:4 - k, :] += _colsum(up * u)
            du_in = du_in + up * cw_ref[3 - k:4 - k, :]
        dz_next[...] = dz[0:8, :]
        dm_ref[:, 0:2 * W4] = du_in.astype(BF16)

    cidx = lambda r: nc - 1 - r
    return dict(
        init=init, body=body, operands=[pm, zc, pif, cw, sv, dym, cst, nst],
        in_specs=[pl.BlockSpec((L, M_W), lambda r: (cidx(r), 0)), pl.BlockSpec((L, 2 * W4), lambda r: (cidx(r), 0)),
                  pl.BlockSpec((L, IF_W), lambda r: (cidx(r), 0)), _full(cw.shape), _full(sv.shape),
                  pl.BlockSpec((L, W4), lambda r: (cidx(r), 0)),
                  pl.BlockSpec((1, MLSTM_HEADS, HD, HD), lambda r: (cidx(r), 0, 0, 0)),
                  pl.BlockSpec((1, 8, LANES), lambda r: (cidx(r), 0, 0))],
        out_specs=[pl.BlockSpec((L, M_W), lambda r: (cidx(r), 0)), pl.BlockSpec((L, IF_W), lambda r: (cidx(r), 0)),
                   _full((8, 2 * W4)), _full((8, W4))],
        out_shape=[jax.ShapeDtypeStruct((S, M_W), BF16), jax.ShapeDtypeStruct((S, IF_W), BF16),
                   jax.ShapeDtypeStruct((8, 2 * W4), F32), jax.ShapeDtypeStruct((8, W4), F32)],
        scratch=[pltpu.VMEM((MLSTM_HEADS, HD, HD), F32), pltpu.VMEM((8, LANES), F32),
                 pltpu.VMEM((8, 2 * W4), F32), pltpu.VMEM((L, 2 * W4), F32)])


def _rope_tables(positions):
    half = HEAD_DIM // 2
    inv_freq = ROPE_THETA ** (-2.0 * jnp.arange(half, dtype=F32) / HEAD_DIM)
    ang = positions.astype(F32)[:, None] * inv_freq
    cos = jnp.tile(jnp.cos(ang), (1, LANES // half))
    sign = jnp.tile(jnp.concatenate([-jnp.ones((half,), F32), jnp.ones((half,), F32)]), LANES // HEAD_DIM)
    sin = jnp.tile(jnp.sin(ang), (1, LANES // half)) * sign
    return cos, sin


def _local_step(x, tgt, positions, mod, gains, w_cat, w_ba, w_bm, w_out, w_gate, w_up, w_down,
                conv_w, conv_b, b_if, sinks, norm_w):
    t = _tables(mod, gains, conv_w, conv_b, b_if, norm_w, positions)
    a = _mixer_fwd(x, t, sinks, w_cat)
    b = _ffn_part(x, tgt, t, a, w_ba, w_bm, w_out, w_gate, w_up, w_down)
    c = _mixer_bwd(b["dx1"], t, a, b, sinks, w_ba, w_bm, w_out)
    grad_x, acc_p = _pre_bwd(c["dproj"], x, b["dx1"], t["vecs"], w_cat)
    big = dict(w_cat=jnp.concatenate(c["g_w_cat"], axis=1), w_ba=c["g_w_ba"], w_bm=c["g_w_bm"], w_out=c["g_w_out"], w_gate=b["g_w_gate"],
               w_up=b["g_w_up"], w_down=b["g_w_down"])
    return b["loss"], grad_x, big, _small_grads(acc_p, b, c)


def _tables(mod, gains, conv_w, conv_b, b_if, norm_w, positions):
    cos, sin = _rope_tables(positions)
    return dict(
        vecs=jnp.concatenate([mod, gains, jnp.zeros((6, D_MODEL), F32)], axis=0),
        cw=jnp.concatenate([conv_w, conv_b.reshape(1, -1), jnp.zeros((3, 2 * 512), F32)], axis=0),
        sv=jnp.zeros((8, 512), F32).at[0].set(norm_w).at[1, 0:8].set(b_if), cos=cos, sin=sin)


def _mixer_fwd(x, t, sinks, w_cat):
    h, pa, pm, pif, pg = _pre_proj(x, t["vecs"], w_cat)
    n_blk = x.shape[0] // ATTN_BLOCK
    (ya, qr, kr, vb, lse), = _fused_call([_attn_fwd_part(pa, t["cos"], t["sin"], sinks)], "attn_fwd", n_blk)
    (ym, zc, cst, nst), = _fused_call([_mlstm_fwd_part(pm, pif, t["cw"], t["sv"])], "mlstm_fwd", n_blk)
    return dict(h=h, pm=pm, pif=pif, pg=pg, ya=ya, qr=qr, kr=kr, vb=vb, lse=lse, ym=ym, zc=zc, cst=cst, nst=nst)


def _ffn_part(x, tgt, t, a, w_ba, w_bm, w_out, w_gate, w_up, w_down):
    x1, merged, mix, pba, pbm = _mix_fwd(x, a["ya"], a["ym"], a["pg"], t["vecs"], w_ba, w_bm, w_out)
    dx1, h2, hid, da, du, dff, acc_f, loss = _ffn_fwd_bwd(x1, tgt, t["vecs"], w_gate, w_up, w_down)
    return dict(merged=merged, mix=mix, pba=pba, pbm=pbm, dx1=dx1, acc_f=acc_f, loss=loss[0, 0],
                g_w_gate=_matmul_tn(da, h2, 1024, "dw_ffn_gate"),
                g_w_up=_matmul_tn(du, h2, 1024, "dw_ffn_up"),
                g_w_down=_matmul_tn(hid, dff, 1024, "dw_ffn_down"))


def _mixer_bwd(dx1, t, a, b, sinks, w_ba, w_bm, w_out):
    dmix, dpa, dpb, dg, dya, dym, acc_m = _mix_bwd(dx1, b["mix"], b["pba"], b["pbm"], a["pg"], t["vecs"], w_ba, w_bm, w_out)
    g_w_out = _matmul_tn(b["merged"], dmix, 1024, "dw_out")
    g_w_ba = _matmul_tn(a["ya"], dpa, 1024, "dw_branch_attn")
    g_w_bm = _matmul_tn(a["ym"], dpb, 1024, "dw_branch_mlstm")
    n_blk = dx1.shape[0] // ATTN_BLOCK
    (dq, dkv, dkv_last, dsink), = _fused_call(
        [_attn_bwd_part(dya, a["qr"], a["kr"], a["vb"], a["lse"], t["cos"], t["sin"], sinks)], "attn_bwd", n_blk)
    (dm, dif, dcw, dsv), = _fused_call(
        [_mlstm_bwd_part(a["pm"], a["zc"], a["pif"], t["cw"], t["sv"], dym, a["cst"], a["nst"])], "mlstm_bwd", n_blk)
    dkv = lax.dynamic_update_slice(dkv, dkv_last, (dkv.shape[0] - ATTN_BLOCK, 0))
    dproj = [dq, dkv, dm, dif, dg]
    g_w_cat = [_matmul_tn(a["h"], p, min(p.shape[1], 1024), "dw_in_" + n)
               for p, n in zip(dproj, ("q", "kv", "mlstm", "gates", "branch"))]
    return dict(dproj=dproj, g_w_cat=g_w_cat, g_w_out=g_w_out, g_w_ba=g_w_ba,
                g_w_bm=g_w_bm, acc_m=acc_m, dsink=dsink, dcw=dcw, dsv=dsv)


def _small_grads(acc_p, b, c):
    acc_f, acc_m = b["acc_f"], c["acc_m"]
    dmod = jnp.stack([acc_p[1], acc_p[0], acc_m[0], acc_f[3], acc_f[2], acc_f[0]])
    dgains = jnp.stack([acc_p[2], acc_m[1], acc_f[4], acc_f[1]])
    return dict(dmod=dmod, dgains=dgains, dconv_w=c["dcw"][0:4], dconv_b=c["dcw"][4], db_if=c["dsv"][1, 0:8],
                dsinks=c["dsink"][0, 0:8], dnorm_w=c["dsv"][0])


MESH_ID = pl.DeviceIdType.MESH


def _mesh_pos():
    return lax.axis_index("x"), lax.axis_index("y"), lax.axis_index("c")


def _flip(v, bit):
    return 1 - v if bit else v


def _relations():
    return [((r >> 2) & 1, (r >> 1) & 1, r & 1) for r in range(1, N_DEV)]


def _small_exchange(p, gather, name):
    R, V = p.shape[-2:]

    def body(p_ref, out_ref, send_sems, recv_sems):
        x, y, c = _mesh_pos()
        me = 4 * x + 2 * y + c
        out_ref[me] = p_ref[...] if gather else p_ref[me]
        peers = []
        for dx, dy, dc in _relations():
            px, py, pc = _flip(x, dx), _flip(y, dy), _flip(c, dc)
            peers.append(((px, py, pc), 4 * px + 2 * py + pc))

        def copy(k, landing):
            peer, pid = peers[k]
            return pltpu.make_async_remote_copy(
                src_ref=p_ref if gather else p_ref.at[pid], dst_ref=out_ref.at[landing],
                send_sem=send_sems.at[k], recv_sem=recv_sems.at[k], device_id=peer, device_id_type=MESH_ID)

        sends = [copy(k, me) for k in range(N_DEV - 1)]
        for cp in sends:
            cp.start()
        for k in range(N_DEV - 1):
            copy(k, peers[k][1]).wait_recv()
        for cp in sends:
            cp.wait_send()

    vm = pl.BlockSpec(memory_space=pltpu.VMEM)
    return pl.pallas_call(
        body, name=name, in_specs=[vm], out_specs=vm,
        out_shape=jax.ShapeDtypeStruct((N_DEV, R, V), F32),
        scratch_shapes=[pltpu.SemaphoreType.DMA((N_DEV - 1,)), pltpu.SemaphoreType.DMA((N_DEV - 1,))],
        compiler_params=pltpu.CompilerParams(vmem_limit_bytes=VMEM_LIMIT),
    )(p)


HBM_SPEC = pl.BlockSpec(memory_space=pltpu.HBM)
SEM_SPEC = pl.BlockSpec(memory_space=pltpu.SEMAPHORE)


def _peers(x, y, c):
    out = []
    for dx, dy, dc in _relations():
        px, py, pc = _flip(x, dx), _flip(y, dy), _flip(c, dc)
        out.append(((px, py, pc), 4 * px + 2 * py + pc))
    return out


def _exchange_start(arrs, gather, after, name):
    n = len(arrs)
    me_out = 4 * lax.axis_index("x") + 2 * lax.axis_index("y") + lax.axis_index("c")
    lands = []
    for a in arrs:
        own = a[None] if gather else lax.dynamic_index_in_dim(a, me_out, 0, keepdims=True)
        empty = lax.empty(((N_DEV,) + a.shape) if gather else a.shape, a.dtype)
        lands.append(lax.dynamic_update_index_in_dim(empty, own, me_out, 0))

    def body(*refs):
        a_refs, l_refs = refs[:n], refs[n:2 * n]
        send_sems, recv_sems = refs[2 * n + 1], refs[2 * n + 2]
        token = refs[4 * n + 3]
        x, y, c = _mesh_pos()
        me = 4 * x + 2 * y + c
        for a in range(n):
            for k, (peer, pid) in enumerate(_peers(x, y, c)):
                pltpu.make_async_remote_copy(
                    src_ref=a_refs[a] if gather else a_refs[a].at[pid], dst_ref=l_refs[a].at[me],
                    send_sem=send_sems.at[a * (N_DEV - 1) + k], recv_sem=recv_sems.at[a * (N_DEV - 1) + k],
                    device_id=peer, device_id_type=MESH_ID).start()
        token[...] = jnp.zeros_like(token)

    sem = pltpu.SemaphoreType.DMA((n * (N_DEV - 1),))
    hbm = lambda a: pltpu.with_memory_space_constraint(a, pltpu.HBM)
    res = pl.pallas_call(
        body, name=name,
        out_shape=(sem, sem, *[pltpu.HBM(a.shape, a.dtype) for a in arrs], *[pltpu.HBM(l.shape, l.dtype) for l in lands],
                   jax.ShapeDtypeStruct((8, LANES), F32)),
        in_specs=[HBM_SPEC] * (2 * n) + [pl.BlockSpec(memory_space=pl.ANY)],
        out_specs=(SEM_SPEC, SEM_SPEC, *[HBM_SPEC] * (2 * n), pl.BlockSpec(memory_space=pltpu.VMEM)),
        input_output_aliases={i: 2 + i for i in range(2 * n)},
        compiler_params=pltpu.CompilerParams(has_side_effects=pltpu.SideEffectType.DATAFLOW_SIDE_EFFECTING),
    )(*[hbm(a) for a in arrs], *[hbm(l) for l in lands], after)
    return dict(sems=res[0:2], arrs=res[2:2 + n], lands=res[2 + n:2 + 2 * n], token=res[2 + 2 * n], gather=gather)


def _exchange_wait(st, after, name):
    n = len(st["arrs"])
    gather = st["gather"]

    def body(*refs):
        a_refs, l_refs = refs[:n], refs[n:2 * n]
        send_sems, recv_sems = refs[2 * n], refs[2 * n + 1]
        x, y, c = _mesh_pos()
        for a in range(n):
            for k, (peer, pid) in enumerate(_peers(x, y, c)):
                cp = pltpu.make_async_remote_copy(
                    src_ref=a_refs[a] if gather else a_refs[a].at[pid], dst_ref=l_refs[a].at[pid],
                    send_sem=send_sems.at[a * (N_DEV - 1) + k], recv_sem=recv_sems.at[a * (N_DEV - 1) + k],
                    device_id=peer, device_id_type=MESH_ID)
                cp.wait_send()
                cp.wait_recv()

    both = list(st["arrs"]) + list(st["lands"])
    res = pl.pallas_call(
        body, name=name, out_shape=[pltpu.HBM(a.shape, a.dtype) for a in both],
        in_specs=[HBM_SPEC] * (2 * n) + [SEM_SPEC, SEM_SPEC, pl.BlockSpec(memory_space=pl.ANY)],
        out_specs=[HBM_SPEC] * (2 * n), input_output_aliases={i: i for i in range(2 * n)},
        compiler_params=pltpu.CompilerParams(has_side_effects=pltpu.SideEffectType.DATAFLOW_SIDE_EFFECTING),
    )(*both, *st["sems"], after)
    return res[n:2 * n]


def _tie(x, token, name):
    def body(x_ref, t_ref, o_ref):
        o_ref[...] = x_ref[...]

    vm = pl.BlockSpec(memory_space=pltpu.VMEM)
    return pl.pallas_call(
        body, name=name, in_specs=[vm, pl.BlockSpec(memory_space=pl.ANY)], out_specs=vm,
        out_shape=jax.ShapeDtypeStruct(x.shape, x.dtype),
    )(x, token)


def _all_gather_hbm(shards):
    n = len(shards)

    def body(*refs):
        p_refs, out_refs = refs[:n], refs[n:2 * n]
        send_sems, recv_sems, local_sems = refs[2 * n:]
        x, y, c = _mesh_pos()
        me, sibling = (x, y, c), (x, y, 1 - c)
        chips = [(1 - x, y), (x, 1 - y), (1 - x, 1 - y)]

        def copy(a, k, block, to, own=False):
            slot = out_refs[a].at[4 * block[0] + 2 * block[1] + block[2]]
            return pltpu.make_async_remote_copy(
                src_ref=p_refs[a] if own else slot, dst_ref=slot,
                send_sem=send_sems.at[a, k], recv_sem=recv_sems.at[a, k], device_id=to, device_id_type=MESH_ID)

        mine = [pltpu.make_async_copy(p_refs[a], out_refs[a].at[4 * x + 2 * y + c], local_sems.at[a]) for a in range(n)]
        for cp in mine:
            cp.start()
        first = []
        for a in range(n):
            first.append(copy(a, 0, me, sibling, own=True))
            first += [copy(a, 1 + j, me, (*chip, c), own=True) for j, chip in enumerate(chips)]
        for cp in first:
            cp.start()
        passed = []
        for j, chip in enumerate(chips):
            for a in range(n):
                copy(a, 1 + j, (*chip, c), me).wait_recv()
                passed.append(copy(a, 4 + j, (*chip, c), sibling))
                passed[-1].start()
        for a in range(n):
            copy(a, 0, sibling, me).wait_recv()
            for j, chip in enumerate(chips):
                copy(a, 4 + j, (*chip, 1 - c), me).wait_recv()
        for cp in first + passed:
            cp.wait_send()
        for cp in mine:
            cp.wait()

    hbm = pl.BlockSpec(memory_space=pl.ANY)
    return pl.pallas_call(
        body, name="gather_weights", in_specs=[hbm] * n, out_specs=[hbm] * n,
        out_shape=[jax.ShapeDtypeStruct((N_DEV,) + s.shape, s.dtype) for s in shards],
        scratch_shapes=[pltpu.SemaphoreType.DMA((n, N_DEV - 1)), pltpu.SemaphoreType.DMA((n, N_DEV - 1)),
                        pltpu.SemaphoreType.DMA((n,))],
    )(*shards)


def _adamw(w, g, m, v):
    m2 = ADAM_B1 * m + (1.0 - ADAM_B1) * g
    v2 = ADAM_B2 * v + (1.0 - ADAM_B2) * (g * g)
    m_hat = m2 / (1.0 - ADAM_B1 ** ADAM_STEP)
    v_hat = v2 / (1.0 - ADAM_B2 ** ADAM_STEP)
    delta = -ADAM_LR * (m_hat / (jnp.sqrt(v_hat) + ADAM_EPS) + ADAM_WD * w)
    return delta, m2, v2


def _mod_partial(cmat, w_shard, b_shard):
    def body(c_ref, w_ref, b_ref, o_ref):
        o_ref[...] = _dot(c_ref[...].astype(BF16), w_ref[...].astype(BF16)) + b_ref[...]

    return pl.pallas_call(
        body, name="mod_partial", out_shape=jax.ShapeDtypeStruct((N_DEV, w_shard.shape[1]), F32),
        compiler_params=_params(),
    )(cmat, w_shard, b_shard)


def _adamw_w_ada(cmat, dmod_cols, w, m, v):
    def body(c_ref, d_ref, w_ref, m_ref, v_ref, g_ref, dl_ref, m2_ref, v2_ref):
        g = _dot_tn(c_ref[...].astype(BF16), d_ref[...].astype(BF16))
        g_ref[...] = g
        dl_ref[...], m2_ref[...], v2_ref[...] = _adamw(w_ref[...], g, m_ref[...], v_ref[...])

    return pl.pallas_call(
        body, name="adamw_w_ada", out_shape=[jax.ShapeDtypeStruct(w.shape, F32)] * 4,
        compiler_params=_params(),
    )(cmat, dmod_cols, w, m, v)


SMALL_ROWS = 16
SMALL_AT = {"b_ada": (0, 6, 0, D_MODEL), "g_pre_mix": (6, 1, 0, D_MODEL), "g_post_mix": (7, 1, 0, D_MODEL),
            "g_pre_ffn": (8, 1, 0, D_MODEL), "g_post_ffn": (9, 1, 0, D_MODEL), "conv_b": (10, 1, 0, D_MODEL),
            "mlstm_norm_w": (11, 1, 0, 512), "b_if": (11, 1, 512, LANES), "attn_sinks": (11, 1, 640, LANES)}


def _small_table(part):
    tail = jnp.concatenate([part["mlstm_norm_w"], jnp.pad(part["b_if"], (0, LANES - 8)),
                            jnp.pad(part["attn_sinks"], (0, LANES - 8)), jnp.zeros((256,), F32)])
    return jnp.concatenate([part["b_ada"], part["gains"], part["conv_b"][None], tail[None],
                            jnp.zeros((SMALL_ROWS - 12, D_MODEL), F32)], axis=0)


def _adamw_small(gathered, wmv):
    names = list(SMALL_AT)

    def body(*refs):
        g_ref, ins, outs = refs[0], refs[1:1 + 3 * len(names)], refs[1 + 3 * len(names):]
        g = g_ref[0]
        for k in range(1, N_DEV):
            g = g + g_ref[k]
        for i, n in enumerate(names):
            r0, rows, l0, lanes = SMALL_AT[n]
            gi = jnp.concatenate([g[r:r + 1, l0:l0 + lanes] for r in range(r0, r0 + rows)], axis=1)
            w_ref, m_ref, v_ref = ins[3 * i:3 * i + 3]
            go, dl, m2, v2 = outs[4 * i:4 * i + 4]
            go[...] = gi
            dl[...], m2[...], v2[...] = _adamw(w_ref[...], gi, m_ref[...], v_ref[...])

    flat = [a for n in names for a in wmv[n]]
    res = pl.pallas_call(
        body, name="adamw_small",
        out_shape=[jax.ShapeDtypeStruct(wmv[n][0].shape, F32) for n in names for _ in range(4)],
        compiler_params=_params(),
    )(gathered, *flat)
    return {n: res[4 * i:4 * i + 4] for i, n in enumerate(names)}


def _row_tile(rows):
    return rows // 4 if rows >= 512 and rows % 64 == 0 else rows


def _sum_partials(r_ref):
    g = r_ref[0].astype(F32)
    for k in range(1, N_DEV):
        g = g + r_ref[k].astype(F32)
    return g


def _adamw_sum(recv, w, m, v, name):
    r, cdim = w.shape
    tr = _row_tile(r)

    def body(r_ref, w_ref, m_ref, v_ref, g_ref, dl_ref, m2_ref, v2_ref):
        g = _sum_partials(r_ref)
        g_ref[...] = g
        dl_ref[...], m2_ref[...], v2_ref[...] = _adamw(w_ref[...], g, m_ref[...], v_ref[...])

    row = pl.BlockSpec((tr, cdim), lambda i: (i, 0))
    return pl.pallas_call(
        body, name=name, grid=(r // tr,),
        in_specs=[pl.BlockSpec((N_DEV, tr, cdim), lambda i: (0, i, 0)), row, row, row],
        out_specs=[row] * 4, out_shape=[jax.ShapeDtypeStruct((r, cdim), F32)] * 4,
        compiler_params=_params(("parallel",)),
    )(recv, w, m, v)


def _sum8(recv, name):
    _, r, cdim = recv.shape
    tr = _row_tile(r)

    def body(r_ref, g_ref):
        g_ref[...] = _sum_partials(r_ref)

    return pl.pallas_call(
        body, name=name, grid=(r // tr,),
        in_specs=[pl.BlockSpec((N_DEV, tr, cdim), lambda i: (0, i, 0))],
        out_specs=pl.BlockSpec((tr, cdim), lambda i: (i, 0)), out_shape=jax.ShapeDtypeStruct((r, cdim), F32),
        compiler_params=_params(("parallel",)),
    )(recv)


def _adamw_plain(g, w, m, v, name):
    r, cdim = w.shape
    tr = _row_tile(r)

    def body(g_ref, w_ref, m_ref, v_ref, dl_ref, m2_ref, v2_ref):
        dl_ref[...], m2_ref[...], v2_ref[...] = _adamw(w_ref[...], g_ref[...], m_ref[...], v_ref[...])

    row = pl.BlockSpec((tr, cdim), lambda i: (i, 0))
    return pl.pallas_call(
        body, name=name, grid=(r // tr,), in_specs=[row] * 4, out_specs=[row] * 3,
        out_shape=[jax.ShapeDtypeStruct((r, cdim), F32)] * 3,
        compiler_params=_params(("parallel",)),
    )(g, w, m, v)


IN_SHARD = 609
IN_SHARD_PAD = 640
IF_AT = A_W + M_W


def _regrouped(u):
    return u if u < IF_AT + 8 else u + (IF_W - 8)


def _selection(k, rows, row0, transpose):
    shape = (rows, IN_SHARD_PAD) if transpose else (IN_SHARD_PAD, rows)
    l = lax.broadcasted_iota(jnp.int32, shape, 1 if transpose else 0)
    r = lax.broadcasted_iota(jnp.int32, shape, 0 if transpose else 1) + row0
    u = l + IN_SHARD * k
    ru = u + jnp.where(u >= IF_AT + 8, IF_W - 8, 0)
    return ((ru == r) & (l < IN_SHARD)).astype(BF16)


def _regroup_w_in(g):
    def body(g_ref, o_ref):
        for cb in range(CAT_W // LANES):
            r0 = cb * LANES
            acc = jnp.zeros((D_MODEL, LANES), F32)
            for k in range(N_DEV):
                lo, hi = _regrouped(IN_SHARD * k), _regrouped(IN_SHARD * k + IN_SHARD - 1)
                if hi >= r0 and lo < r0 + LANES:
                    acc = acc + _dot(g_ref[k], _selection(k, LANES, r0, False))
            o_ref[:, r0:r0 + LANES] = acc.astype(BF16)

    return pl.pallas_call(
        body, name="regroup_w_in", out_shape=jax.ShapeDtypeStruct((D_MODEL, CAT_W), BF16),
        compiler_params=_params(),
    )(g)


def _ungroup_w_in(g_parts):
    n = len(g_parts)

    def body(*refs):
        o_ref, g_ref = refs[n], refs[n + 1]
        at = 0
        for p in refs[:n]:
            g_ref[:, at:at + p.shape[1]] = p[...]
            at += p.shape[1]
        for k in range(N_DEV):
            lo, hi = _regrouped(IN_SHARD * k), _regrouped(IN_SHARD * k + IN_SHARD - 1)
            w0, w1 = lo // LANES * LANES, (hi // LANES + 1) * LANES
            o_ref[k] = _dot(g_ref[:, w0:w1], _selection(k, w1 - w0, w0, True)).astype(BF16)

    return pl.pallas_call(
        body, name="ungroup_w_in", out_shape=jax.ShapeDtypeStruct((N_DEV, D_MODEL, IN_SHARD_PAD), BF16),
        scratch_shapes=[pltpu.VMEM((D_MODEL, CAT_W), BF16)], compiler_params=_params(),
    )(*g_parts)


WEIGHT_NAMES = ("w_ada", "b_ada", "g_pre_mix", "g_post_mix", "w_in", "b_if", "conv_w", "conv_b", "attn_sinks",
                "mlstm_norm_w", "w_branch_attn", "w_branch_mlstm", "w_out", "g_pre_ffn", "g_post_ffn",
                "w_ffn_gate", "w_ffn_up", "w_ffn_down")


def kernel(x, c, positions, w_ada, b_ada, g_pre_mix, g_post_mix, w_in, b_if, conv_w, conv_b, attn_sinks, mlstm_norm_w, w_branch_attn, w_branch_mlstm, w_out, g_pre_ffn, g_post_ffn, w_ffn_gate, w_ffn_up, w_ffn_down, loss_target, m_w_ada, m_b_ada, m_g_pre_mix, m_g_post_mix, m_w_in, m_b_if, m_conv_w, m_conv_b, m_attn_sinks, m_mlstm_norm_w, m_w_branch_attn, m_w_branch_mlstm, m_w_out, m_g_pre_ffn, m_g_post_ffn, m_w_ffn_gate, m_w_ffn_up, m_w_ffn_down, v_w_ada, v_b_ada, v_g_pre_mix, v_g_post_mix, v_w_in, v_b_if, v_conv_w, v_conv_b, v_attn_sinks, v_mlstm_norm_w, v_w_branch_attn, v_w_branch_mlstm, v_w_out, v_g_pre_ffn, v_g_post_ffn, v_w_ffn_gate, v_w_ffn_up, v_w_ffn_down):
    given = dict(locals())
    W = {n: given[n][0] for n in WEIGHT_NAMES}
    M = {n: given["m_" + n][0] for n in WEIGHT_NAMES}
    V = {n: given["v_" + n][0] for n in WEIGHT_NAMES}
    me = 4 * lax.axis_index("x") + 2 * lax.axis_index("y") + lax.axis_index("c")

    ff_sh = D_FF // N_DEV
    g_in, g_conv, cg = _all_gather_hbm([jnp.pad(W["w_in"], ((0, 0), (0, IN_SHARD_PAD - IN_SHARD))).astype(BF16),
                                        jnp.pad(W["conv_w"], ((0, 4), (0, 0))), c.reshape(8, D_MODEL // 8)])

    cmat = cg.reshape(N_DEV, D_MODEL)
    ada_w = D_MODEL * 6 // N_DEV
    b_cols = lax.dynamic_slice(W["b_ada"], (me * ada_w,), (ada_w,)).reshape(1, ada_w)
    mod_part = _mod_partial(cmat, W["w_ada"], b_cols)
    mod_recv = _small_exchange(jnp.broadcast_to(mod_part[:, None, :], (N_DEV, 8, ada_w)), False, "scatter_mod")
    mod = mod_recv[:, 0, :].reshape(6, D_MODEL)

    st_b = _exchange_start([W["w_branch_attn"].astype(BF16), W["w_branch_mlstm"].astype(BF16), W["w_out"].astype(BF16),
                            W["w_ffn_gate"].T.astype(BF16), W["w_ffn_up"].T.astype(BF16), W["w_ffn_down"].astype(BF16)],
                           True, mod_recv, "gather_rest_start")
    cols = lambda g: g.transpose(1, 0, 2).reshape(g.shape[1], N_DEV * g.shape[2])
    gains = jnp.stack([W["g_pre_mix"], W["g_post_mix"], W["g_pre_ffn"], W["g_post_ffn"]])
    xs, tgt = x[0], loss_target[0]
    t = _tables(mod, gains, cols(g_conv)[0:4], W["conv_b"], W["b_if"], W["mlstm_norm_w"], positions[0])
    vecs = t["vecs"]
    t["vecs"] = _tie(vecs, st_b["token"], "tie_fwd")
    w_cat = _regroup_w_in(g_in)
    a = _mixer_fwd(xs, t, W["attn_sinks"], w_cat)
    g_ba, g_bm, g_out, g_gate, g_up, g_down = _exchange_wait(st_b, a["ym"], "gather_rest_wait")
    w_ba, w_bm, w_out = cols(g_ba), cols(g_bm), g_out.reshape(D_MODEL, D_MODEL)
    b = _ffn_part(xs, tgt, t, a, w_ba, w_bm, w_out, g_gate.reshape(D_FF, D_MODEL), g_up.reshape(D_FF, D_MODEL),
                  g_down.reshape(D_FF, D_MODEL))

    st_f = _exchange_start([b["g_w_gate"].reshape(N_DEV, ff_sh, D_MODEL), b["g_w_up"].reshape(N_DEV, ff_sh, D_MODEL),
                            b["g_w_down"].reshape(N_DEV, ff_sh, D_MODEL)], False, b["dx1"], "scatter_ffn_start")
    t["vecs"] = _tie(vecs, st_f["token"], "tie_bwd")
    cm = _mixer_bwd(b["dx1"], t, a, b, W["attn_sinks"], w_ba, w_bm, w_out)
    pieces = lambda g, n: g.reshape(g.shape[0], N_DEV, n).transpose(1, 0, 2)
    st_m = _exchange_start([_ungroup_w_in(cm["g_w_cat"]), pieces(cm["g_w_ba"], 128), pieces(cm["g_w_bm"], 128),
                            cm["g_w_out"].reshape(N_DEV, D_MODEL // N_DEV, D_MODEL),
                            jnp.pad(pieces(cm["dcw"][0:4], 128), ((0, 0), (0, 4), (0, 0)))], False, cm["dcw"],
                           "scatter_mixer_start")
    r_gate, r_up, r_down = _exchange_wait(st_f, st_m["token"], "scatter_ffn_wait")
    grad_x, acc_p = _pre_bwd(cm["dproj"], xs, b["dx1"], _tie(vecs, st_m["token"], "tie_pre_bwd"), w_cat)
    small = _small_grads(acc_p, b, cm)
    loss = b["loss"]

    big_out = [{} for _ in range(4)]

    def put(n, res):
        for k in range(4):
            big_out[k][n] = res[k][None]

    put("w_ffn_down", _adamw_sum(r_down, W["w_ffn_down"], M["w_ffn_down"], V["w_ffn_down"], "adamw_w_ffn_down"))
    for n, r in (("w_ffn_gate", r_gate), ("w_ffn_up", r_up)):
        put(n, [o.T for o in _adamw_sum(r, W[n].T, M[n].T, V[n].T, "adamw_" + n)])

    sg = _small_exchange(_small_table({"b_ada": small["dmod"], "gains": small["dgains"], "conv_b": small["dconv_b"],
                                       "mlstm_norm_w": small["dnorm_w"], "b_if": small["db_if"],
                                       "attn_sinks": small["dsinks"]}), True, "gather_small")
    as_row = lambda a, n: jnp.pad(a, (0, SMALL_AT[n][3] * SMALL_AT[n][1] - a.shape[0]))[None]
    small_res = _adamw_small(sg, {n: [as_row(d[n], n) for d in (W, M, V)] for n in SMALL_AT})
    small_out = [{n: small_res[n][k][:, 0:W[n].shape[0]] for n in SMALL_AT} for k in range(4)]
    dmod_cols = lax.dynamic_slice(sg[:, 0:6, :].reshape(N_DEV, 6 * D_MODEL), (0, me * ada_w), (N_DEV, ada_w))
    ada_out = _adamw_w_ada(cmat, dmod_cols, W["w_ada"], M["w_ada"], V["w_ada"])

    r_in, r_ba, r_bm, r_out, r_conv = _exchange_wait(st_m, grad_x, "scatter_mixer_wait")
    for n, r in (("w_branch_attn", r_ba), ("w_branch_mlstm", r_bm), ("w_out", r_out)):
        put(n, _adamw_sum(r, W[n], M[n], V[n], "adamw_" + n))
    pad4 = lambda v: jnp.pad(v, ((0, 4), (0, 0)))
    put("conv_w", [o[0:4] for o in _adamw_sum(r_conv, pad4(W["conv_w"]), pad4(M["conv_w"]), pad4(V["conv_w"]),
                                                 "adamw_conv_w")])
    g = _sum8(r_in, "sum_w_in")[:, 0:IN_SHARD].T
    put("w_in", [o.T for o in [g] + list(_adamw_plain(g, W["w_in"].T, M["w_in"].T, V["w_in"].T, "adamw_w_in"))])

    total = lax.psum(loss, ("x", "y", "c"))
    outs = [total, grad_x[None]]
    for k in range(4):
        for n in WEIGHT_NAMES:
            if n == "w_ada":
                outs.append(ada_out[k][None])
            elif n in big_out[k]:
                outs.append(big_out[k][n])
            else:
                outs.append(small_out[k][n])
    return tuple(outs)
```

```python
import functools

import jax
import jax.numpy as jnp
import numpy as np
from jax import lax
from jax.experimental import pallas as pl
from jax.experimental.pallas import tpu as pltpu

F32 = jnp.float32
BF16 = jnp.bfloat16

N_DEV = 8
D_MODEL = 1024
D_FF = 2816
N_Q_HEADS = 8
HEAD_DIM = 64
ATTN_BLOCK = 128
ROPE_THETA = 10000.0
MLSTM_HEADS = 4
MLSTM_HEAD_DIM = 128
MLSTM_CHUNK = 128
NORM_EPS = 1e-6
ADAM_LR = 0.001
ADAM_B1 = 0.9
ADAM_B2 = 0.999
ADAM_EPS = 1e-08
ADAM_WD = 0.01
ADAM_STEP = 10

ROW_TILE = 256
WIDE_TILE = 512
LANES = 128
NEG = -1e30
VMEM_LIMIT = 56 * 1024 * 1024

A_W = 768
M_W = 2048
IF_W = 128
G_W = 2048
CAT_W = A_W + M_W + IF_W + G_W

R_SHIFT_M, R_SCALE_M, R_GATE_M, R_SHIFT_F, R_SCALE_F, R_GATE_F = 0, 1, 2, 3, 4, 5
R_G_PRE_MIX, R_G_POST_MIX, R_G_PRE_FFN, R_G_POST_FFN = 6, 7, 8, 9


def _dot(a, b):
    return jnp.dot(a, b, preferred_element_type=F32)


def _dot_nt(a, b):
    return lax.dot_general(a, b, (((1,), (1,)), ((), ())), preferred_element_type=F32)


def _dot_tn(a, b):
    return lax.dot_general(a, b, (((0,), (0,)), ((), ())), preferred_element_type=F32)


def _recip(x):
    return 1.0 / x


def _sigmoid(x):
    return _recip(1.0 + jnp.exp(-x))


def _colsum(x):
    return jnp.sum(x, axis=0, keepdims=True)


def _rowmean(x):
    return jnp.mean(x, axis=-1, keepdims=True)


def _params(sem=None, vmem=VMEM_LIMIT):
    kw = dict(vmem_limit_bytes=vmem)
    if sem is not None:
        kw["dimension_semantics"] = sem
    return pltpu.CompilerParams(**kw)


def _full(shape):
    nd = len(shape)
    return pl.BlockSpec(shape, lambda *_: (0,) * nd)


def _pre_proj(x, vecs, w_cat):
    S = x.shape[0]
    tm = WIDE_TILE

    def body(x_ref, v_ref, w_ref, h_ref, pa_ref, pm_ref, pif_ref, pg_ref):
        xv = x_ref[...]
        r = lax.rsqrt(_rowmean(xv * xv) + NORM_EPS)
        h = (xv * r * v_ref[R_G_PRE_MIX:R_G_PRE_MIX + 1, :]) * (1.0 + v_ref[R_SCALE_M:R_SCALE_M + 1, :]) \
            + v_ref[R_SHIFT_M:R_SHIFT_M + 1, :]
        hb = h.astype(BF16)
        h_ref[...] = hb
        pa_ref[...] = _dot(hb, w_ref[:, 0:A_W])
        pm_ref[...] = _dot(hb, w_ref[:, A_W:A_W + M_W])
        pif_ref[...] = _dot(hb, w_ref[:, A_W + M_W:A_W + M_W + IF_W])
        pg_ref[...] = _dot(hb, w_ref[:, A_W + M_W + IF_W:CAT_W]).astype(BF16)

    row = lambda w: pl.BlockSpec((tm, w), lambda i: (i, 0))
    return pl.pallas_call(
        body, name="pre_proj", grid=(S // tm,),
        in_specs=[row(D_MODEL), _full(vecs.shape), _full(w_cat.shape)],
        out_specs=[row(D_MODEL), row(A_W), row(M_W), row(IF_W), row(G_W)],
        out_shape=[jax.ShapeDtypeStruct((S, D_MODEL), BF16), jax.ShapeDtypeStruct((S, A_W), F32),
                   jax.ShapeDtypeStruct((S, M_W), F32), jax.ShapeDtypeStruct((S, IF_W), F32),
                   jax.ShapeDtypeStruct((S, G_W), BF16)],
        compiler_params=_params(("parallel",)),
    )(x, vecs, w_cat)


def _mix_fwd(x, ya, ym, pg, vecs, w_ba, w_bm, w_out):
    S = x.shape[0]
    tm = WIDE_TILE

    def body(x_ref, ya_ref, ym_ref, pg_ref, v_ref, wba_ref, wbm_ref, wout_ref,
             x1_ref, merged_ref, mix_ref, pa_ref, pb_ref):
        pa = _dot(ya_ref[...], wba_ref[...])
        pb = _dot(ym_ref[...], wbm_ref[...])
        merged = _sigmoid(pg_ref[:, 0:D_MODEL].astype(F32)) * pa + _sigmoid(pg_ref[:, D_MODEL:G_W].astype(F32)) * pb
        mb = merged.astype(BF16)
        mix = _dot(mb, wout_ref[...])
        r = lax.rsqrt(_rowmean(mix * mix) + NORM_EPS)
        x1_ref[...] = x_ref[...] + v_ref[R_GATE_M:R_GATE_M + 1, :] * (mix * r * v_ref[R_G_POST_MIX:R_G_POST_MIX + 1, :])
        merged_ref[...] = mb
        mix_ref[...] = mix
        pa_ref[...] = pa.astype(BF16)
        pb_ref[...] = pb.astype(BF16)

    row = lambda w: pl.BlockSpec((tm, w), lambda i: (i, 0))
    sd = lambda w, dt: jax.ShapeDtypeStruct((S, w), dt)
    return pl.pallas_call(
        body, name="mix_fwd", grid=(S // tm,),
        in_specs=[row(D_MODEL), row(512), row(512), row(G_W), _full(vecs.shape), _full(w_ba.shape),
                  _full(w_bm.shape), _full(w_out.shape)],
        out_specs=[row(D_MODEL)] * 5,
        out_shape=[sd(D_MODEL, F32), sd(D_MODEL, BF16), sd(D_MODEL, F32), sd(D_MODEL, BF16), sd(D_MODEL, BF16)],
        compiler_params=_params(("parallel",)),
    )(x, ya, ym, pg, vecs, w_ba, w_bm, w_out)


def _ffn_fwd_bwd(x1, tgt, vecs, w_gate, w_up, w_down):
    S = x1.shape[0]
    tm = ROW_TILE

    def body(x1_ref, tgt_ref, v_ref, wg_hbm, wu_hbm, wd_hbm,
             dx1_ref, h2_ref, hid_ref, da_ref, du_ref, dff_ref, acc_ref, loss_ref,
             wg, wu, wd, sem):
        i = pl.program_id(0)

        @pl.when(i == 0)
        def _():
            cps = [pltpu.make_async_copy(wg_hbm, wg, sem.at[0]), pltpu.make_async_copy(wu_hbm, wu, sem.at[1]),
                   pltpu.make_async_copy(wd_hbm, wd, sem.at[2])]
            for cp in cps:
                cp.start()
            for cp in cps:
                cp.wait()
            acc_ref[...] = jnp.zeros_like(acc_ref)
            loss_ref[...] = jnp.zeros_like(loss_ref)

        vrow = lambda r: v_ref[r:r + 1, :]
        x1v = x1_ref[...]
        r3 = lax.rsqrt(_rowmean(x1v * x1v) + NORM_EPS)
        x1hat = x1v * r3
        xn3 = x1hat * vrow(R_G_PRE_FFN)
        h2b = (xn3 * (1.0 + vrow(R_SCALE_F)) + vrow(R_SHIFT_F)).astype(BF16)
        h2_ref[...] = h2b
        a = _dot_nt(h2b, wg[...])
        u = _dot_nt(h2b, wu[...])
        sg = _sigmoid(a)
        sil = a * sg
        hidb = (sil * u).astype(BF16)
        hid_ref[...] = hidb
        ff = _dot(hidb, wd[...])
        r4 = lax.rsqrt(_rowmean(ff * ff) + NORM_EPS)
        ffhat = ff * r4
        n4 = ffhat * vrow(R_G_POST_FFN)
        err = x1v + vrow(R_GATE_F) * n4 - tgt_ref[...]
        loss_ref[...] += jnp.sum(err * err) * (0.5 / D_MODEL)
        dy = err * (1.0 / D_MODEL)
        acc_ref[0:1, :] += _colsum(dy * n4)
        dn4 = dy * vrow(R_GATE_F)
        acc_ref[1:2, :] += _colsum(dn4 * ffhat)
        dffhat = dn4 * vrow(R_G_POST_FFN)
        dffb = (r4 * (dffhat - ffhat * _rowmean(dffhat * ffhat))).astype(BF16)
        dff_ref[...] = dffb
        dhid = _dot_nt(dffb, wd[...])
        dub = (dhid * sil).astype(BF16)
        dab = (dhid * u * (sg * (1.0 + a * (1.0 - sg)))).astype(BF16)
        da_ref[...] = dab
        du_ref[...] = dub
        dh2 = _dot(dab, wg[...]) + _dot(dub, wu[...])
        acc_ref[2:3, :] += _colsum(dh2 * xn3)
        acc_ref[3:4, :] += _colsum(dh2)
        dxn3 = dh2 * (1.0 + vrow(R_SCALE_F))
        acc_ref[4:5, :] += _colsum(dxn3 * x1hat)
        dx1hat = dxn3 * vrow(R_G_PRE_FFN)
        dx1_ref[...] = dy + r3 * (dx1hat - x1hat * _rowmean(dx1hat * x1hat))

    row = lambda w: pl.BlockSpec((tm, w), lambda i: (i, 0))
    sd = lambda w, dt: jax.ShapeDtypeStruct((S, w), dt)
    anyspec = pl.BlockSpec(memory_space=pl.ANY)
    return pl.pallas_call(
        body, name="ffn_fwd_bwd", grid=(S // tm,),
        in_specs=[row(D_MODEL), row(D_MODEL), _full(vecs.shape), anyspec, anyspec, anyspec],
        out_specs=[row(D_MODEL), row(D_MODEL), row(D_FF), row(D_FF), row(D_FF), row(D_MODEL),
                   _full((8, D_MODEL)), _full((8, LANES))],
        out_shape=[sd(D_MODEL, F32), sd(D_MODEL, BF16), sd(D_FF, BF16), sd(D_FF, BF16), sd(D_FF, BF16),
                   sd(D_MODEL, BF16), jax.ShapeDtypeStruct((8, D_MODEL), F32), jax.ShapeDtypeStruct((8, LANES), F32)],
        scratch_shapes=[pltpu.VMEM(w_gate.shape, BF16), pltpu.VMEM(w_up.shape, BF16), pltpu.VMEM(w_down.shape, BF16),
                        pltpu.SemaphoreType.DMA((3,))],
        compiler_params=_params(("arbitrary",)),
    )(x1, tgt, vecs, w_gate, w_up, w_down)


def _mix_bwd(dx1, mix, pa, pb, pg, vecs, w_ba, w_bm, w_out):
    S = dx1.shape[0]
    tm = WIDE_TILE

    def body(dx1_ref, mix_ref, pa_ref, pb_ref, pg_ref, v_ref, wba_ref, wbm_ref, wout_ref,
             dmix_ref, dpa_ref, dpb_ref, dg_ref, dya_ref, dym_ref, acc_ref):
        i = pl.program_id(0)

        @pl.when(i == 0)
        def _():
            acc_ref[...] = jnp.zeros_like(acc_ref)

        vrow = lambda r: v_ref[r:r + 1, :]
        dx1v = dx1_ref[...]
        mix = mix_ref[...]
        r2 = lax.rsqrt(_rowmean(mix * mix) + NORM_EPS)
        mixhat = mix * r2
        acc_ref[0:1, :] += _colsum(dx1v * (mixhat * vrow(R_G_POST_MIX)))
        dn2 = dx1v * vrow(R_GATE_M)
        acc_ref[1:2, :] += _colsum(dn2 * mixhat)
        dmixhat = dn2 * vrow(R_G_POST_MIX)
        dmixb = (r2 * (dmixhat - mixhat * _rowmean(dmixhat * mixhat))).astype(BF16)
        dmix_ref[...] = dmixb
        dmerged = _dot_nt(dmixb, wout_ref[...])
        sa = _sigmoid(pg_ref[:, 0:D_MODEL].astype(F32))
        sm = _sigmoid(pg_ref[:, D_MODEL:G_W].astype(F32))
        dpab = (dmerged * sa).astype(BF16)
        dpbb = (dmerged * sm).astype(BF16)
        dpa_ref[...] = dpab
        dpb_ref[...] = dpbb
        dg_ref[:, 0:D_MODEL] = (dmerged * pa_ref[...].astype(F32) * (sa * (1.0 - sa))).astype(BF16)
        dg_ref[:, D_MODEL:G_W] = (dmerged * pb_ref[...].astype(F32) * (sm * (1.0 - sm))).astype(BF16)
        dya_ref[...] = _dot_nt(dpab, wba_ref[...])
        dym_ref[...] = _dot_nt(dpbb, wbm_ref[...])

    row = lambda w: pl.BlockSpec((tm, w), lambda i: (i, 0))
    sd = lambda w, dt: jax.ShapeDtypeStruct((S, w), dt)
    return pl.pallas_call(
        body, name="mix_bwd", grid=(S // tm,),
        in_specs=[row(D_MODEL), row(D_MODEL), row(D_MODEL), row(D_MODEL), row(G_W), _full(vecs.shape),
                  _full(w_ba.shape), _full(w_bm.shape), _full(w_out.shape)],
        out_specs=[row(D_MODEL), row(D_MODEL), row(D_MODEL), row(G_W), row(512), row(512), _full((8, D_MODEL))],
        out_shape=[sd(D_MODEL, BF16), sd(D_MODEL, BF16), sd(D_MODEL, BF16), sd(G_W, BF16), sd(512, F32), sd(512, F32),
                   jax.ShapeDtypeStruct((8, D_MODEL), F32)],
        compiler_params=_params(("arbitrary",)),
    )(dx1, mix, pa, pb, pg, vecs, w_ba, w_bm, w_out)


def _pre_bwd(pieces, x, dx1, vecs, w_cat):
    S = x.shape[0]
    tm = WIDE_TILE
    n = len(pieces)
    starts = [sum(p.shape[1] for p in pieces[:k]) for k in range(n + 1)]

    def body(*refs):
        p_refs = refs[:n]
        x_ref, dx1_ref, v_ref, w_ref, dx_ref, acc_ref = refs[n:]
        i = pl.program_id(0)

        @pl.when(i == 0)
        def _():
            acc_ref[...] = jnp.zeros_like(acc_ref)

        vrow = lambda r: v_ref[r:r + 1, :]
        dh = _dot_nt(p_refs[0][...], w_ref[:, starts[0]:starts[1]])
        for k in range(1, n):
            dh = dh + _dot_nt(p_refs[k][...], w_ref[:, starts[k]:starts[k + 1]])
        xv = x_ref[...]
        r1 = lax.rsqrt(_rowmean(xv * xv) + NORM_EPS)
        xhat = xv * r1
        acc_ref[0:1, :] += _colsum(dh * (xhat * vrow(R_G_PRE_MIX)))
        acc_ref[1:2, :] += _colsum(dh)
        dxn = dh * (1.0 + vrow(R_SCALE_M))
        acc_ref[2:3, :] += _colsum(dxn * xhat)
        dxhat = dxn * vrow(R_G_PRE_MIX)
        dx_ref[...] = dx1_ref[...] + r1 * (dxhat - xhat * _rowmean(dxhat * xhat))

    row = lambda w: pl.BlockSpec((tm, w), lambda i: (i, 0))
    return pl.pallas_call(
        body, name="pre_bwd", grid=(S // tm,),
        in_specs=[row(p.shape[1]) for p in pieces] + [row(D_MODEL), row(D_MODEL), _full(vecs.shape), _full(w_cat.shape)],
        out_specs=[row(D_MODEL), _full((8, D_MODEL))],
        out_shape=[jax.ShapeDtypeStruct((S, D_MODEL), F32), jax.ShapeDtypeStruct((8, D_MODEL), F32)],
        compiler_params=_params(("arbitrary",)),
    )(*pieces, x, dx1, vecs, w_cat)


def _matmul_tn(a, b, tn, name, ts=1024):
    S, K = a.shape
    N = b.shape[1]
    n_s = S // ts

    def body(a_ref, b_ref, o_ref, acc_ref):
        s = pl.program_id(1)

        @pl.when(s == 0)
        def _():
            acc_ref[...] = jnp.zeros_like(acc_ref)

        acc_ref[...] += _dot_tn(a_ref[...], b_ref[...])

        @pl.when(s == n_s - 1)
        def _():
            o_ref[...] = acc_ref[...].astype(BF16)

    return pl.pallas_call(
        body, name=name, grid=(N // tn, n_s),
        in_specs=[pl.BlockSpec((ts, K), lambda j, s: (s, 0)), pl.BlockSpec((ts, tn), lambda j, s: (s, j))],
        out_specs=pl.BlockSpec((K, tn), lambda j, s: (0, j)),
        out_shape=jax.ShapeDtypeStruct((K, N), BF16),
        scratch_shapes=[pltpu.VMEM((K, tn), F32)],
        compiler_params=_params(("parallel", "arbitrary")),
    )(a, b)


def _rope_swap(t):
    lane = lax.broadcasted_iota(jnp.int32, t.shape, 1)
    first = (lane & (HEAD_DIM - 1)) < (HEAD_DIM // 2)
    return jnp.where(first, pltpu.roll(t, LANES - HEAD_DIM // 2, 1), pltpu.roll(t, HEAD_DIM // 2, 1))


def _rope(t, cos, sin_signed):
    return t * cos + _rope_swap(t) * sin_signed


def _rope_t(d, cos, sin_signed):
    return d * cos + _rope_swap(d * sin_signed)


def _to_kv_lanes(chunk, p, h):
    lane = lax.broadcasted_iota(jnp.int32, chunk.shape, 1)
    src = chunk if p == h else pltpu.roll(chunk, HEAD_DIM, 1)
    return jnp.where((lane >> 6) == h, src, jnp.zeros_like(src))


def _from_kv_lanes(o_a, o_b, h):
    lane = lax.broadcasted_iota(jnp.int32, o_a.shape, 1)
    a = o_a if h == 0 else pltpu.roll(o_a, HEAD_DIM, 1)
    b = o_b if h == 1 else pltpu.roll(o_b, HEAD_DIM, 1)
    return jnp.where(lane < HEAD_DIM, a, b)


def _band_bias(n):
    blk = ATTN_BLOCK
    qi = lax.broadcasted_iota(jnp.int32, (blk, 2 * blk), 0)
    kj = lax.broadcasted_iota(jnp.int32, (blk, 2 * blk), 1)
    seen = (kj > qi) & (kj <= qi + blk) & ((n > 0) | (kj >= blk))
    return jnp.concatenate([jnp.where(seen, 0.0, NEG)] * N_Q_HEADS, axis=0)


def _stack_heads(chunks, h, dtype):
    parts = []
    for g in range(4):
        j = 4 * h + g
        parts.append(_to_kv_lanes(chunks[j // 2], j % 2, h))
    return jnp.concatenate(parts, axis=0).astype(dtype)


def _fused_call(parts, name, n_steps):
    counts = [(len(p["in_specs"]), len(p["out_specs"]), len(p["scratch"])) for p in parts]
    n_in, n_out = sum(c[0] for c in counts), sum(c[1] for c in counts)

    def kernel_fn(*refs):
        i = pl.program_id(0)
        groups, a, b, c = [], 0, n_in, n_in + n_out
        for ci, co, cs in counts:
            groups.append(refs[a:a + ci] + refs[b:b + co] + refs[c:c + cs])
            a, b, c = a + ci, b + co, c + cs
        for p, g in zip(parts, groups):
            p["init"](i, *g)
        for p, g in zip(parts, groups):
            p["body"](i, *g)

    flat = lambda key: [v for p in parts for v in p[key]]
    res = pl.pallas_call(
        kernel_fn, name=name, grid=(n_steps,), in_specs=flat("in_specs"), out_specs=flat("out_specs"),
        out_shape=flat("out_shape"), scratch_shapes=flat("scratch"), compiler_params=_params(("arbitrary",)),
    )(*flat("operands"))
    out, pos = [], 0
    for _, co, _ in counts:
        out.append(res[pos:pos + co])
        pos += co
    return out


def _attn_fwd_part(pa, cos, sin, sinks):
    S = pa.shape[0]
    blk = ATTN_BLOCK
    nb = S // blk

    def body(n, sink_ref, cur_ref, prev_ref, cos_ref, sin_ref, cosp_ref, sinp_ref,
             ya_ref, qr_ref, kr_ref, vb_ref, lse_ref):
        cos_c, sin_c = cos_ref[...], sin_ref[...]
        qch = [_rope(cur_ref[:, c * LANES:(c + 1) * LANES], cos_c, sin_c) * (HEAD_DIM ** -0.5) for c in range(4)]
        for c in range(4):
            qr_ref[:, c * LANES:(c + 1) * LANES] = qch[c].astype(BF16)
        k_cur = _rope(cur_ref[:, 512:640], cos_c, sin_c).astype(BF16)
        k_prev = _rope(prev_ref[:, 0:LANES], cosp_ref[...], sinp_ref[...]).astype(BF16)
        v_cur = cur_ref[:, 640:768].astype(BF16)
        v_prev = prev_ref[:, LANES:2 * LANES].astype(BF16)
        kr_ref[...] = k_cur
        vb_ref[...] = v_cur
        K = jnp.concatenate([k_prev, k_cur], axis=0)
        V = jnp.concatenate([v_prev, v_cur], axis=0)
        lane = lax.broadcasted_iota(jnp.int32, (blk, LANES), 1)
        s = jnp.concatenate([_dot_nt(_stack_heads(qch, h, BF16), K) for h in range(2)], axis=0)
        s = s + _band_bias(n)
        rowmax = jnp.max(s, axis=1, keepdims=True)
        hd = lambda x, j: x[j * blk:(j + 1) * blk]
        m = jnp.concatenate([jnp.maximum(hd(rowmax, j), sink_ref[j]) for j in range(N_Q_HEADS)], axis=0)
        p = jnp.exp(s - m)
        pb = p.astype(BF16)
        den = _rowsum_mxu(pb) + jnp.concatenate([jnp.exp(sink_ref[j] - hd(m, j)) for j in range(N_Q_HEADS)], axis=0)
        o = jnp.concatenate([_dot(pb[4 * h * blk:4 * (h + 1) * blk], V) for h in range(2)], axis=0) * _recip(den)
        lse = m + jnp.log(den)
        outs = [o[j * blk:(j + 1) * blk, :] for j in range(N_Q_HEADS)]
        lse_tile = jnp.zeros((blk, LANES), F32)
        for j in range(N_Q_HEADS):
            lse_tile = jnp.where(lane == j, lse[j * blk:(j + 1) * blk, :], lse_tile)
        for c in range(4):
            ya_ref[:, c * LANES:(c + 1) * LANES] = _from_kv_lanes(outs[2 * c], outs[2 * c + 1], c // 2).astype(BF16)
        lse_ref[...] = lse_tile

    prev = lambda n: jnp.maximum(n - 1, 0)
    sd = lambda w, dt: jax.ShapeDtypeStruct((S, w), dt)
    return dict(
        init=lambda n, *refs: None, body=body, scratch=[], operands=[sinks, pa, pa, cos, sin, cos, sin],
        in_specs=[pl.BlockSpec(memory_space=pltpu.SMEM),
                  pl.BlockSpec((blk, A_W), lambda n: (n, 0)),
                  pl.BlockSpec((blk, 256), lambda n: (prev(n), 2)),
                  pl.BlockSpec((blk, LANES), lambda n: (n, 0)), pl.BlockSpec((blk, LANES), lambda n: (n, 0)),
                  pl.BlockSpec((blk, LANES), lambda n: (prev(n), 0)), pl.BlockSpec((blk, LANES), lambda n: (prev(n), 0))],
        out_specs=[pl.BlockSpec((blk, 512), lambda n: (n, 0)), pl.BlockSpec((blk, 512), lambda n: (n, 0)),
                   pl.BlockSpec((blk, LANES), lambda n: (n, 0)), pl.BlockSpec((blk, LANES), lambda n: (n, 0)),
                   pl.BlockSpec((blk, LANES), lambda n: (n, 0))],
        out_shape=[sd(512, BF16), sd(512, BF16), sd(LANES, BF16), sd(LANES, BF16), sd(LANES, F32)])


def _attn_bwd_part(dya, qr, kr, vb, lse, cos, sin, sinks):
    S = dya.shape[0]
    blk = ATTN_BLOCK
    nb = S // blk

    def init(n, sink_ref, dya_ref, qr_ref, kc_ref, kp_ref, vc_ref, vp_ref, lse_ref, cos_ref, sin_ref, cosp_ref, sinp_ref,
             dq_ref, dkv_ref, last_ref, dsink_ref, ck, cv):
        @pl.when(n == 0)
        def _():
            ck[...] = jnp.zeros_like(ck)
            cv[...] = jnp.zeros_like(cv)
            dsink_ref[...] = jnp.zeros_like(dsink_ref)

    def body(n, sink_ref, dya_ref, qr_ref, kc_ref, kp_ref, vc_ref, vp_ref, lse_ref, cos_ref, sin_ref, cosp_ref, sinp_ref,
             dq_ref, dkv_ref, last_ref, dsink_ref, ck, cv):
        K = jnp.concatenate([kp_ref[...], kc_ref[...]], axis=0)
        V = jnp.concatenate([vp_ref[...], vc_ref[...]], axis=0)
        qch = [qr_ref[:, c * LANES:(c + 1) * LANES] for c in range(4)]
        dch = [dya_ref[:, c * LANES:(c + 1) * LANES] for c in range(4)]
        lse_tile = lse_ref[...]
        lane8 = lax.broadcasted_iota(jnp.int32, (8, LANES), 1)
        grp = lambda x, h: x[4 * h * blk:4 * (h + 1) * blk]
        qs = jnp.concatenate([_stack_heads(qch, h, BF16) for h in range(2)], axis=0)
        dos = jnp.concatenate([_stack_heads(dch, h, BF16) for h in range(2)], axis=0)
        lse_col = jnp.concatenate([lse_tile[:, j:j + 1] for j in range(N_Q_HEADS)], axis=0)
        s = jnp.concatenate([_dot_nt(grp(qs, h), K) for h in range(2)], axis=0)
        p = jnp.exp(s + _band_bias(n) - lse_col)
        dp = jnp.concatenate([_dot_nt(grp(dos, h), V) for h in range(2)], axis=0)
        delta = jnp.sum(p * dp, axis=1, keepdims=True)
        dsb = (p * (dp - delta)).astype(BF16)
        pb = p.astype(BF16)
        dq = jnp.concatenate([_dot(grp(dsb, h), K) for h in range(2)], axis=0)
        dk_acc = _dot_tn(grp(dsb, 0), grp(qs, 0)) + _dot_tn(grp(dsb, 1), grp(qs, 1))
        dv_acc = _dot_tn(grp(pb, 0), grp(dos, 0)) + _dot_tn(grp(pb, 1), grp(dos, 1))
        dqs = [dq[j * blk:(j + 1) * blk, :] for j in range(N_Q_HEADS)]
        dsink = jnp.zeros((8, LANES), F32)
        for j in range(N_Q_HEADS):
            rows = slice(j * blk, (j + 1) * blk)
            ps_delta = jnp.exp(sink_ref[j] - lse_col[rows]) * delta[rows]
            dsink = jnp.where(lane8 == j, dsink - jnp.sum(ps_delta), dsink)
        dsink_ref[...] += dsink
        cos_c, sin_c = cos_ref[...], sin_ref[...]
        for c in range(4):
            dqc = _from_kv_lanes(dqs[2 * c], dqs[2 * c + 1], c // 2) * (HEAD_DIM ** -0.5)
            dq_ref[:, c * LANES:(c + 1) * LANES] = _rope_t(dqc, cos_c, sin_c).astype(BF16)
        dkv_ref[:, 0:LANES] = _rope_t(dk_acc[0:blk, :] + ck[...], cosp_ref[...], sinp_ref[...]).astype(BF16)
        dkv_ref[:, LANES:2 * LANES] = (dv_acc[0:blk, :] + cv[...]).astype(BF16)
        ck[...] = dk_acc[blk:2 * blk, :]
        cv[...] = dv_acc[blk:2 * blk, :]
        last_ref[:, 0:LANES] = _rope_t(dk_acc[blk:2 * blk, :], cos_c, sin_c).astype(BF16)
        last_ref[:, LANES:2 * LANES] = dv_acc[blk:2 * blk, :].astype(BF16)

    prev = lambda n: jnp.maximum(n - 1, 0)
    same = lambda n: n
    bs = lambda w, f: pl.BlockSpec((blk, w), lambda n: (f(n), 0))
    return dict(
        init=init, body=body, operands=[sinks, dya, qr, kr, kr, vb, vb, lse, cos, sin, cos, sin],
        in_specs=[pl.BlockSpec(memory_space=pltpu.SMEM),
                  bs(512, same), bs(512, same), bs(LANES, same), bs(LANES, prev), bs(LANES, same), bs(LANES, prev),
                  bs(LANES, same), bs(LANES, same), bs(LANES, same), bs(LANES, prev), bs(LANES, prev)],
        out_specs=[bs(512, same), bs(256, prev), _full((blk, 256)), _full((8, LANES))],
        out_shape=[jax.ShapeDtypeStruct((S, 512), BF16), jax.ShapeDtypeStruct((S, 256), BF16),
                   jax.ShapeDtypeStruct((blk, 256), BF16), jax.ShapeDtypeStruct((8, LANES), F32)],
        scratch=[pltpu.VMEM((blk, LANES), F32), pltpu.VMEM((blk, LANES), F32)])


def _split3(x):
    hi = x.astype(BF16)
    r1 = x - hi.astype(F32)
    mid = r1.astype(BF16)
    lo = (r1 - mid.astype(F32)).astype(BF16)
    return hi, mid, lo


def _tri_matmul(tri_b, x):
    hi, mid, lo = _split3(x)
    return _dot(tri_b, hi) + _dot(tri_b, mid) + _dot(tri_b, lo)


def _log_sigmoid(x):
    return jnp.minimum(x, 0.0) - jnp.log(1.0 + jnp.exp(-jnp.abs(x)))


def _shift_rows(cur, seam, k, down):
    L = cur.shape[0]
    row8 = lax.broadcasted_iota(jnp.int32, seam.shape, 0)
    if down:
        mixed = jnp.concatenate([cur[:L - 8], jnp.where(row8 >= 8 - k, seam, cur[L - 8:])], axis=0)
        return pltpu.roll(mixed, k, 0)
    mixed = jnp.concatenate([jnp.where(row8 < k, seam, cur[:8]), cur[8:]], axis=0)
    return pltpu.roll(mixed, L - k, 0)


def _conv_fwd(cur, tail, cw_ref):
    z = cw_ref[4:5, :]
    for k in range(3, 0, -1):
        z = z + _shift_rows(cur, tail, k, True) * cw_ref[3 - k:4 - k, :]
    return z + cur * cw_ref[3:4, :]


def _rowsum_mxu(x, two_pass=False):
    ones = jnp.ones((x.shape[1], LANES), BF16)
    hi = x.astype(BF16)
    s = _dot(hi, ones)
    if two_pass:
        s = s + _dot((x - hi.astype(F32)).astype(BF16), ones)
    return s


def _stack(f):
    return jnp.concatenate([f(h) for h in range(MLSTM_HEADS)], axis=0)


def _head(x, h):
    L = x.shape[0] // MLSTM_HEADS
    return x[h * L:(h + 1) * L]


def _mlstm_heads_fwd(qk, cur_ref, gt, b_all, c_prev, nmv, tri, eye):
    L = qk.shape[0]
    HD = MLSTM_HEAD_DIM
    W4 = MLSTM_HEADS * HD
    col2row = lambda x: jnp.sum(jnp.where(eye, x, 0.0), axis=0, keepdims=True)
    b_col = _stack(lambda h: b_all[:, 4 + h:5 + h])
    i_col = _stack(lambda h: gt[:, h:h + 1])
    b_row = _stack(lambda h: jnp.broadcast_to(col2row(b_all[:, 4 + h:5 + h]), (L, L)))
    i_row = _stack(lambda h: jnp.broadcast_to(col2row(gt[:, h:h + 1]), (L, L)))
    bl = _stack(lambda h: jnp.broadcast_to(b_all[L - 1:L, 4 + h:5 + h], (L, 1)))
    m_prev = _stack(lambda h: jnp.broadcast_to(nmv[4 + h:5 + h, 0:1], (L, 1)))
    n_prev = _stack(lambda h: jnp.broadcast_to(nmv[h:h + 1, :], (L, HD)))
    tri4 = jnp.concatenate([tri] * MLSTM_HEADS, axis=0)
    Dm = jnp.where(tri4, b_col - b_row + i_row, NEG)
    inter = b_col + m_prev
    m_t = jnp.maximum(inter, jnp.max(Dm, axis=1, keepdims=True))
    W = jnp.exp(Dm - m_t)
    e_t = jnp.exp(inter - m_t)
    q = _stack(lambda h: qk[:, h * HD:(h + 1) * HD])
    k = _stack(lambda h: qk[:, W4 + h * HD:W4 + (h + 1) * HD]) * (HD ** -0.5)
    v = _stack(lambda h: cur_ref[:, 2 * W4 + h * HD:2 * W4 + (h + 1) * HD])
    qb, kb, vb = q.astype(BF16), k.astype(BF16), v.astype(BF16)
    Sc = _stack(lambda h: _dot_nt(_head(qb, h), _head(kb, h))) * W
    Scb = Sc.astype(BF16)
    cb = [c.astype(BF16) for c in c_prev]
    P1 = _stack(lambda h: _dot(_head(qb, h), cb[h]))
    num = _stack(lambda h: _dot(_head(Scb, h), _head(vb, h))) + e_t * P1
    qn = _rowsum_mxu(q * n_prev)
    den = _rowsum_mxu(Scb) + e_t * qn
    floor = jnp.broadcast_to(jnp.exp(-m_t), den.shape)
    inv_g = _recip(jnp.maximum(jnp.abs(den), floor))
    hv = num * inv_g
    a_col = bl - b_col + i_col
    a_max = _stack(lambda h: jnp.broadcast_to(jnp.max(_head(a_col, h), axis=0, keepdims=True), (L, 1)))
    m_new = jnp.maximum(bl + m_prev, a_max)
    dec = jnp.exp(bl + m_prev - m_new)
    u_col = jnp.exp(a_col - m_new)
    return dict(W=W, e_t=e_t, q=q, k=k, v=v, qb=qb, kb=kb, vb=vb, cb=cb, Sc=Sc, Scb=Scb, P1=P1, qn=qn, den=den,
                floor=floor, inv_g=inv_g, hv=hv, n_prev=n_prev, m_new=m_new, dec=dec, u_col=u_col)


def _mlstm_fwd_part(pm, pif, cw, sv):
    S = pm.shape[0]
    L = MLSTM_CHUNK
    nc = S // L
    HD = MLSTM_HEAD_DIM
    W4 = MLSTM_HEADS * HD

    def init(c, cur_ref, pif_ref, cw_ref, sv_ref, ym_ref, z_ref, cst_ref, nst_ref, C, nm, tail):
        @pl.when(c == 0)
        def _():
            C[...] = jnp.zeros_like(C)
            nm[...] = jnp.zeros_like(nm)
            tail[...] = jnp.zeros_like(tail)

    def body(c, cur_ref, pif_ref, cw_ref, sv_ref, ym_ref, z_ref, cst_ref, nst_ref, C, nm, tail):
        z = _conv_fwd(cur_ref[:, 0:2 * W4], tail[...], cw_ref)
        tail[...] = cur_ref[L - 8:L, 0:2 * W4]
        z_ref[...] = z
        qk = z * _sigmoid(z)
        gt = pif_ref[...] + sv_ref[1:2, 0:LANES]
        r_i = lax.broadcasted_iota(jnp.int32, (L, L), 0)
        c_i = lax.broadcasted_iota(jnp.int32, (L, L), 1)
        tri = c_i <= r_i
        eye = c_i == r_i
        b_all = _tri_matmul(tri.astype(BF16), _log_sigmoid(gt))
        nmv = nm[...]
        nst_ref[0] = nmv
        c_prev = [C[h] for h in range(MLSTM_HEADS)]
        f = _mlstm_heads_fwd(qk, cur_ref, gt, b_all, c_prev, nmv, tri, eye)
        hv = f["hv"]
        xc = hv - _rowsum_mxu(hv, True) * (1.0 / HD)
        hhat = xc * lax.rsqrt(_rowsum_mxu(xc * xc) * (1.0 / HD) + NORM_EPS)
        so = _sigmoid(_stack(lambda h: cur_ref[:, 3 * W4 + h * HD:3 * W4 + (h + 1) * HD]))
        wn = _stack(lambda h: jnp.broadcast_to(sv_ref[0:1, h * HD:(h + 1) * HD], (L, HD)))
        y = (so * hhat * wn).astype(BF16)
        kw = f["k"] * f["u_col"]
        kwb = kw.astype(BF16)
        n_new, m_new = [], []
        for h in range(MLSTM_HEADS):
            cst_ref[0, h] = c_prev[h]
            ym_ref[:, h * HD:(h + 1) * HD] = _head(y, h)
            dec = f["dec"][h * L:h * L + 1, :]
            C[h] = dec * c_prev[h] + _dot_tn(_head(kwb, h), _head(f["vb"], h))
            n_new.append(dec * nmv[h:h + 1, :] + _colsum(_head(kw, h)))
            m_new.append(jnp.broadcast_to(f["m_new"][h * L:h * L + 1, :], (1, LANES)))
        nm[...] = jnp.concatenate(n_new + m_new, axis=0)

    return dict(
        init=init, body=body, operands=[pm, pif, cw, sv],
        in_specs=[pl.BlockSpec((L, M_W), lambda c: (c, 0)),
                  pl.BlockSpec((L, IF_W), lambda c: (c, 0)), _full(cw.shape), _full(sv.shape)],
        out_specs=[pl.BlockSpec((L, W4), lambda c: (c, 0)), pl.BlockSpec((L, 2 * W4), lambda c: (c, 0)),
                   pl.BlockSpec((1, MLSTM_HEADS, HD, HD), lambda c: (c, 0, 0, 0)),
                   pl.BlockSpec((1, 8, LANES), lambda c: (c, 0, 0))],
        out_shape=[jax.ShapeDtypeStruct((S, W4), BF16), jax.ShapeDtypeStruct((S, 2 * W4), F32),
                   jax.ShapeDtypeStruct((nc, MLSTM_HEADS, HD, HD), F32), jax.ShapeDtypeStruct((nc, 8, LANES), F32)],
        scratch=[pltpu.VMEM((MLSTM_HEADS, HD, HD), F32), pltpu.VMEM((8, LANES), F32), pltpu.VMEM((8, 2 * W4), F32)])


def _mlstm_bwd_part(pm, zc, pif, cw, sv, dym, cst, nst):
    S = pm.shape[0]
    L = MLSTM_CHUNK
    nc = S // L
    HD = MLSTM_HEAD_DIM
    W4 = MLSTM_HEADS * HD

    def init(r, cur_ref, z_ref, pif_ref, cw_ref, sv_ref, dym_ref, cst_ref, nst_ref,
             dm_ref, dif_ref, dcw_ref, dsv_ref, dC, dn, dz_next, dqk):
        @pl.when(r == 0)
        def _():
            dC[...] = jnp.zeros_like(dC)
            dn[...] = jnp.zeros_like(dn)
            dz_next[...] = jnp.zeros_like(dz_next)
            dcw_ref[...] = jnp.zeros_like(dcw_ref)
            dsv_ref[...] = jnp.zeros_like(dsv_ref)

    def body(r, cur_ref, z_ref, pif_ref, cw_ref, sv_ref, dym_ref, cst_ref, nst_ref,
             dm_ref, dif_ref, dcw_ref, dsv_ref, dC, dn, dz_next, dqk):
        z = z_ref[...]
        sgz = _sigmoid(z)
        qk = z * sgz
        gt = pif_ref[...] + sv_ref[1:2, 0:LANES]
        r_i = lax.broadcasted_iota(jnp.int32, (L, L), 0)
        c_i = lax.broadcasted_iota(jnp.int32, (L, L), 1)
        tri = c_i <= r_i
        eye = c_i == r_i
        b_all = _tri_matmul(tri.astype(BF16), _log_sigmoid(gt))
        lane = lax.broadcasted_iota(jnp.int32, (L, LANES), 1)
        rowl = lax.broadcasted_iota(jnp.int32, (L, 1), 0)
        nmv = nst_ref[0]
        heads = range(MLSTM_HEADS)
        c_prev = [cst_ref[0, h] for h in heads]
        f = _mlstm_heads_fwd(qk, cur_ref, gt, b_all, c_prev, nmv, tri, eye)
        hv, inv_g, den, e_t, u_col, n_prev = f["hv"], f["inv_g"], f["den"], f["e_t"], f["u_col"], f["n_prev"]
        q, k, v, qb, kb, vb, Sc, Scb, W = f["q"], f["k"], f["v"], f["qb"], f["kb"], f["vb"], f["Sc"], f["Scb"], f["W"]
        xc = hv - _rowsum_mxu(hv, True) * (1.0 / HD)
        rstd = lax.rsqrt(_rowsum_mxu(xc * xc) * (1.0 / HD) + NORM_EPS)
        hhat = xc * rstd
        wn = _stack(lambda h: jnp.broadcast_to(sv_ref[0:1, h * HD:(h + 1) * HD], (L, HD)))
        so = _sigmoid(_stack(lambda h: cur_ref[:, 3 * W4 + h * HD:3 * W4 + (h + 1) * HD]))
        dy = _stack(lambda h: dym_ref[:, h * HD:(h + 1) * HD])
        d_o = (dy * hhat * wn * (so * (1.0 - so))).astype(BF16)
        dln = dy * so
        dwn = dln * hhat
        dhhat = dln * wn
        m2 = _rowsum_mxu(dhhat * hhat) * (1.0 / HD)
        dh = rstd * (dhhat - _rowsum_mxu(dhhat) * (1.0 / HD) - hhat * m2)
        dnum = dh * inv_g
        active = jnp.abs(den) > f["floor"]
        dden = jnp.where(active, -(HD * NORM_EPS) * m2 * rstd * rstd * inv_g * jnp.where(den >= 0.0, 1.0, -1.0), 0.0)
        dnumb = dnum.astype(BF16)
        dSc = _stack(lambda h: _dot_nt(_head(dnumb, h), _head(vb, h))) + dden
        dA = (dSc * W).astype(BF16)
        G = dSc * Sc
        Gb = G.astype(BF16)
        Gl = (G - Gb.astype(F32)).astype(BF16)
        ones = jnp.ones((L, LANES), BF16)
        Gr = _dot(Gb, ones) + _dot(Gl, ones)
        Gc = _stack(lambda h: _dot_tn(_head(Gb, h), ones) + _dot_tn(_head(Gl, h), ones))
        dCn = [dC[h] for h in heads]
        dCnb = [d.astype(BF16) for d in dCn]
        dnv = dn[...]
        dn_new = _stack(lambda h: jnp.broadcast_to(dnv[h:h + 1, :], (L, HD)))
        kdC = _stack(lambda h: _dot(_head(kb, h), dCnb[h]))
        vdC = _stack(lambda h: _dot_nt(_head(vb, h), dCnb[h]))
        dv = (_stack(lambda h: _dot_tn(_head(Scb, h), _head(dnumb, h))) + u_col * kdC).astype(BF16)
        dq = _stack(lambda h: _dot(_head(dA, h), _head(kb, h))) \
            + e_t * _stack(lambda h: _dot_nt(_head(dnumb, h), f["cb"][h])) + (e_t * dden) * n_prev
        dk = (_stack(lambda h: _dot_tn(_head(dA, h), _head(qb, h))) + u_col * (vdC + dn_new)) * (HD ** -0.5)
        E = (_rowsum_mxu(f["P1"] * dnum, True) + dden * f["qn"]) * e_t
        U = _rowsum_mxu(kdC * v + k * dn_new, True) * u_col
        qe = (q * e_t).astype(BF16)
        qd = (e_t * dden) * q
        di = Gc + U
        db = Gr + E - Gc - U
        di_tile = jnp.zeros((L, LANES), F32)
        db_tile = jnp.zeros((L, LANES), F32)
        dn_rows = []
        for h in heads:
            dec = f["dec"][h * L:h * L + 1, :]
            ddec = jnp.sum(dCn[h] * c_prev[h]) + jnp.sum(dnv[h:h + 1, :] * nmv[h:h + 1, :])
            dbl = ddec * dec + jnp.sum(_head(U, h), axis=0, keepdims=True)
            di_tile = jnp.where(lane == h, _head(di, h), di_tile)
            db_tile = jnp.where(lane == 4 + h, _head(db, h) + jnp.where(rowl == L - 1, dbl, 0.0), db_tile)
            dC[h] = dec * dCn[h] + _dot_tn(_head(qe, h), _head(dnumb, h))
            dn_rows.append(dec * dnv[h:h + 1, :] + _colsum(_head(qd, h)))
            dsv_ref[0:1, h * HD:(h + 1) * HD] += _colsum(_head(dwn, h))
            dqk[:, h * HD:(h + 1) * HD] = _head(dq, h)
            dqk[:, W4 + h * HD:W4 + (h + 1) * HD] = _head(dk, h)
            dm_ref[:, 2 * W4 + h * HD:2 * W4 + (h + 1) * HD] = _head(dv, h)
            dm_ref[:, 3 * W4 + h * HD:3 * W4 + (h + 1) * HD] = _head(d_o, h)
        dn[...] = jnp.concatenate(dn_rows + [jnp.zeros((8 - MLSTM_HEADS, LANES), F32)], axis=0)
        dlf = _tri_matmul((r_i <= c_i).astype(BF16), db_tile)
        dif = jnp.where(lane < 4, di_tile, jnp.where(lane < 8, dlf * (1.0 - _sigmoid(gt)), 0.0))
        dif_ref[...] = dif.astype(BF16)
        dsv_ref[1:2, 0:LANES] += _colsum(dif)
        dz = dqk[...] * (sgz * (1.0 + z * (1.0 - sgz)))
        dcw_ref[4:5, :] += _colsum(dz)
        u = cur_ref[:, 0:2 * W4]
        du_in = dz * cw_ref[3:4, :]
        dcw_ref[3:4, :] += _colsum(dz * u)
        for k in range(1, 4):
            up = _shift_rows(dz, dz_next[...], k, False)
            dcw_ref[3 - k:4 - k, :] += _colsum(up * u)
            du_in = du_in + up * cw_ref[3 - k:4 - k, :]
        dz_next[...] = dz[0:8, :]
        dm_ref[:, 0:2 * W4] = du_in.astype(BF16)

    cidx = lambda r: nc - 1 - r
    return dict(
        init=init, body=body, operands=[pm, zc, pif, cw, sv, dym, cst, nst],
        in_specs=[pl.BlockSpec((L, M_W), lambda r: (cidx(r), 0)), pl.BlockSpec((L, 2 * W4), lambda r: (cidx(r), 0)),
                  pl.BlockSpec((L, IF_W), lambda r: (cidx(r), 0)), _full(cw.shape), _full(sv.shape),
                  pl.BlockSpec((L, W4), lambda r: (cidx(r), 0)),
                  pl.BlockSpec((1, MLSTM_HEADS, HD, HD), lambda r: (cidx(r), 0, 0, 0)),
                  pl.BlockSpec((1, 8, LANES), lambda r: (cidx(r), 0, 0))],
        out_specs=[pl.BlockSpec((L, M_W), lambda r: (cidx(r), 0)), pl.BlockSpec((L, IF_W), lambda r: (cidx(r), 0)),
                   _full((8, 2 * W4)), _full((8, W4))],
        out_shape=[jax.ShapeDtypeStruct((S, M_W), BF16), jax.ShapeDtypeStruct((S, IF_W), BF16),
                   jax.ShapeDtypeStruct((8, 2 * W4), F32), jax.ShapeDtypeStruct((8, W4), F32)],
        scratch=[pltpu.VMEM((MLSTM_HEADS, HD, HD), F32), pltpu.VMEM((8, LANES), F32),
                 pltpu.VMEM((8, 2 * W4), F32), pltpu.VMEM((L, 2 * W4), F32)])


def _rope_tables(positions):
    half = HEAD_DIM // 2
    inv_freq = ROPE_THETA ** (-2.0 * jnp.arange(half, dtype=F32) / HEAD_DIM)
    ang = positions.astype(F32)[:, None] * inv_freq
    cos = jnp.tile(jnp.cos(ang), (1, LANES // half))
    sign = jnp.tile(jnp.concatenate([-jnp.ones((half,), F32), jnp.ones((half,), F32)]), LANES // HEAD_DIM)
    sin = jnp.tile(jnp.sin(ang), (1, LANES // half)) * sign
    return cos, sin


def _local_step(x, tgt, positions, mod, gains, w_cat, w_ba, w_bm, w_out, w_gate, w_up, w_down,
                conv_w, conv_b, b_if, sinks, norm_w):
    t = _tables(mod, gains, conv_w, conv_b, b_if, norm_w, positions)
    a = _mixer_fwd(x, t, sinks, w_cat)
    b = _ffn_part(x, tgt, t, a, w_ba, w_bm, w_out, w_gate, w_up, w_down)
    c = _mixer_bwd(b["dx1"], t, a, b, sinks, w_ba, w_bm, w_out)
    grad_x, acc_p = _pre_bwd(c["dproj"], x, b["dx1"], t["vecs"], w_cat)
    big = dict(w_cat=jnp.concatenate(c["g_w_cat"], axis=1), w_ba=c["g_w_ba"], w_bm=c["g_w_bm"], w_out=c["g_w_out"], w_gate=b["g_w_gate"],
               w_up=b["g_w_up"], w_down=b["g_w_down"])
    return b["loss"], grad_x, big, _small_grads(acc_p, b, c)


def _tables(mod, gains, conv_w, conv_b, b_if, norm_w, positions):
    cos, sin = _rope_tables(positions)
    return dict(
        vecs=jnp.concatenate([mod, gains, jnp.zeros((6, D_MODEL), F32)], axis=0),
        cw=jnp.concatenate([conv_w, conv_b.reshape(1, -1), jnp.zeros((3, 2 * 512), F32)], axis=0),
        sv=jnp.zeros((8, 512), F32).at[0].set(norm_w).at[1, 0:8].set(b_if), cos=cos, sin=sin)


def _mixer_fwd(x, t, sinks, w_cat):
    h, pa, pm, pif, pg = _pre_proj(x, t["vecs"], w_cat)
    n_blk = x.shape[0] // ATTN_BLOCK
    (ya, qr, kr, vb, lse), = _fused_call([_attn_fwd_part(pa, t["cos"], t["sin"], sinks)], "attn_fwd", n_blk)
    (ym, zc, cst, nst), = _fused_call([_mlstm_fwd_part(pm, pif, t["cw"], t["sv"])], "mlstm_fwd", n_blk)
    return dict(h=h, pm=pm, pif=pif, pg=pg, ya=ya, qr=qr, kr=kr, vb=vb, lse=lse, ym=ym, zc=zc, cst=cst, nst=nst)


def _ffn_part(x, tgt, t, a, w_ba, w_bm, w_out, w_gate, w_up, w_down):
    x1, merged, mix, pba, pbm = _mix_fwd(x, a["ya"], a["ym"], a["pg"], t["vecs"], w_ba, w_bm, w_out)
    dx1, h2, hid, da, du, dff, acc_f, loss = _ffn_fwd_bwd(x1, tgt, t["vecs"], w_gate, w_up, w_down)
    return dict(merged=merged, mix=mix, pba=pba, pbm=pbm, dx1=dx1, acc_f=acc_f, loss=loss[0, 0],
                g_w_gate=_matmul_tn(da, h2, 1024, "dw_ffn_gate"),
                g_w_up=_matmul_tn(du, h2, 1024, "dw_ffn_up"),
                g_w_down=_matmul_tn(hid, dff, 1024, "dw_ffn_down"))


def _mixer_bwd(dx1, t, a, b, sinks, w_ba, w_bm, w_out):
    dmix, dpa, dpb, dg, dya, dym, acc_m = _mix_bwd(dx1, b["mix"], b["pba"], b["pbm"], a["pg"], t["vecs"], w_ba, w_bm, w_out)
    g_w_out = _matmul_tn(b["merged"], dmix, 1024, "dw_out")
    g_w_ba = _matmul_tn(a["ya"], dpa, 1024, "dw_branch_attn")
    g_w_bm = _matmul_tn(a["ym"], dpb, 1024, "dw_branch_mlstm")
    n_blk = dx1.shape[0] // ATTN_BLOCK
    (dq, dkv, dkv_last, dsink), = _fused_call(
        [_attn_bwd_part(dya, a["qr"], a["kr"], a["vb"], a["lse"], t["cos"], t["sin"], sinks)], "attn_bwd", n_blk)
    (dm, dif, dcw, dsv), = _fused_call(
        [_mlstm_bwd_part(a["pm"], a["zc"], a["pif"], t["cw"], t["sv"], dym, a["cst"], a["nst"])], "mlstm_bwd", n_blk)
    dkv = lax.dynamic_update_slice(dkv, dkv_last, (dkv.shape[0] - ATTN_BLOCK, 0))
    dproj = [dq, dkv, dm, dif, dg]
    g_w_cat = [_matmul_tn(a["h"], p, min(p.shape[1], 1024), "dw_in_" + n)
               for p, n in zip(dproj, ("q", "kv", "mlstm", "gates", "branch"))]
    return dict(dproj=dproj, g_w_cat=g_w_cat, g_w_out=g_w_out, g_w_ba=g_w_ba,
                g_w_bm=g_w_bm, acc_m=acc_m, dsink=dsink, dcw=dcw, dsv=dsv)


def _small_grads(acc_p, b, c):
    acc_f, acc_m = b["acc_f"], c["acc_m"]
    dmod = jnp.stack([acc_p[1], acc_p[0], acc_m[0], acc_f[3], acc_f[2], acc_f[0]])
    dgains = jnp.stack([acc_p[2], acc_m[1], acc_f[4], acc_f[1]])
    return dict(dmod=dmod, dgains=dgains, dconv_w=c["dcw"][0:4], dconv_b=c["dcw"][4], db_if=c["dsv"][1, 0:8],
                dsinks=c["dsink"][0, 0:8], dnorm_w=c["dsv"][0])


MESH_ID = pl.DeviceIdType.MESH


def _mesh_pos():
    return lax.axis_index("x"), lax.axis_index("y"), lax.axis_index("c")


def _flip(v, bit):
    return 1 - v if bit else v


def _relations():
    return [((r >> 2) & 1, (r >> 1) & 1, r & 1) for r in range(1, N_DEV)]


def _small_exchange(p, gather, name):
    R, V = p.shape[-2:]

    def body(p_ref, out_ref, send_sems, recv_sems):
        x, y, c = _mesh_pos()
        me = 4 * x + 2 * y + c
        out_ref[me] = p_ref[...] if gather else p_ref[me]
        peers = []
        for dx, dy, dc in _relations():
            px, py, pc = _flip(x, dx), _flip(y, dy), _flip(c, dc)
            peers.append(((px, py, pc), 4 * px + 2 * py + pc))

        def copy(k, landing):
            peer, pid = peers[k]
            return pltpu.make_async_remote_copy(
                src_ref=p_ref if gather else p_ref.at[pid], dst_ref=out_ref.at[landing],
                send_sem=send_sems.at[k], recv_sem=recv_sems.at[k], device_id=peer, device_id_type=MESH_ID)

        sends = [copy(k, me) for k in range(N_DEV - 1)]
        for cp in sends:
            cp.start()
        for k in range(N_DEV - 1):
            copy(k, peers[k][1]).wait_recv()
        for cp in sends:
            cp.wait_send()

    vm = pl.BlockSpec(memory_space=pltpu.VMEM)
    return pl.pallas_call(
        body, name=name, in_specs=[vm], out_specs=vm,
        out_shape=jax.ShapeDtypeStruct((N_DEV, R, V), F32),
        scratch_shapes=[pltpu.SemaphoreType.DMA((N_DEV - 1,)), pltpu.SemaphoreType.DMA((N_DEV - 1,))],
        compiler_params=pltpu.CompilerParams(vmem_limit_bytes=VMEM_LIMIT),
    )(p)


HBM_SPEC = pl.BlockSpec(memory_space=pltpu.HBM)
SEM_SPEC = pl.BlockSpec(memory_space=pltpu.SEMAPHORE)


def _peers(x, y, c):
    out = []
    for dx, dy, dc in _relations():
        px, py, pc = _flip(x, dx), _flip(y, dy), _flip(c, dc)
        out.append(((px, py, pc), 4 * px + 2 * py + pc))
    return out


def _exchange_start(arrs, gather, after, name):
    n = len(arrs)
    me_out = 4 * lax.axis_index("x") + 2 * lax.axis_index("y") + lax.axis_index("c")
    lands = []
    for a in arrs:
        own = a[None] if gather else lax.dynamic_index_in_dim(a, me_out, 0, keepdims=True)
        empty = lax.empty(((N_DEV,) + a.shape) if gather else a.shape, a.dtype)
        lands.append(lax.dynamic_update_index_in_dim(empty, own, me_out, 0))

    def body(*refs):
        a_refs, l_refs = refs[:n], refs[n:2 * n]
        send_sems, recv_sems = refs[2 * n + 1], refs[2 * n + 2]
        token = refs[4 * n + 3]
        x, y, c = _mesh_pos()
        me = 4 * x + 2 * y + c
        for a in range(n):
            for k, (peer, pid) in enumerate(_peers(x, y, c)):
                pltpu.make_async_remote_copy(
                    src_ref=a_refs[a] if gather else a_refs[a].at[pid], dst_ref=l_refs[a].at[me],
                    send_sem=send_sems.at[a * (N_DEV - 1) + k], recv_sem=recv_sems.at[a * (N_DEV - 1) + k],
                    device_id=peer, device_id_type=MESH_ID).start()
        token[...] = jnp.zeros_like(token)

    sem = pltpu.SemaphoreType.DMA((n * (N_DEV - 1),))
    hbm = lambda a: pltpu.with_memory_space_constraint(a, pltpu.HBM)
    res = pl.pallas_call(
        body, name=name,
        out_shape=(sem, sem, *[pltpu.HBM(a.shape, a.dtype) for a in arrs], *[pltpu.HBM(l.shape, l.dtype) for l in lands],
                   jax.ShapeDtypeStruct((8, LANES), F32)),
        in_specs=[HBM_SPEC] * (2 * n) + [pl.BlockSpec(memory_space=pl.ANY)],
        out_specs=(SEM_SPEC, SEM_SPEC, *[HBM_SPEC] * (2 * n), pl.BlockSpec(memory_space=pltpu.VMEM)),
        input_output_aliases={i: 2 + i for i in range(2 * n)},
        compiler_params=pltpu.CompilerParams(has_side_effects=pltpu.SideEffectType.DATAFLOW_SIDE_EFFECTING),
    )(*[hbm(a) for a in arrs], *[hbm(l) for l in lands], after)
    return dict(sems=res[0:2], arrs=res[2:2 + n], lands=res[2 + n:2 + 2 * n], token=res[2 + 2 * n], gather=gather)


def _exchange_wait(st, after, name):
    n = len(st["arrs"])
    gather = st["gather"]

    def body(*refs):
        a_refs, l_refs = refs[:n], refs[n:2 * n]
        send_sems, recv_sems = refs[2 * n], refs[2 * n + 1]
        x, y, c = _mesh_pos()
        for a in range(n):
            for k, (peer, pid) in enumerate(_peers(x, y, c)):
                cp = pltpu.make_async_remote_copy(
                    src_ref=a_refs[a] if gather else a_refs[a].at[pid], dst_ref=l_refs[a].at[pid],
                    send_sem=send_sems.at[a * (N_DEV - 1) + k], recv_sem=recv_sems.at[a * (N_DEV - 1) + k],
                    device_id=peer, device_id_type=MESH_ID)
                cp.wait_send()
                cp.wait_recv()

    both = list(st["arrs"]) + list(st["lands"])
    res = pl.pallas_call(
        body, name=name, out_shape=[pltpu.HBM(a.shape, a.dtype) for a in both],
        in_specs=[HBM_SPEC] * (2 * n) + [SEM_SPEC, SEM_SPEC, pl.BlockSpec(memory_space=pl.ANY)],
        out_specs=[HBM_SPEC] * (2 * n), input_output_aliases={i: i for i in range(2 * n)},
        compiler_params=pltpu.CompilerParams(has_side_effects=pltpu.SideEffectType.DATAFLOW_SIDE_EFFECTING),
    )(*both, *st["sems"], after)
    return res[n:2 * n]


def _tie(x, token, name):
    def body(x_ref, t_ref, o_ref):
        o_ref[...] = x_ref[...]

    vm = pl.BlockSpec(memory_space=pltpu.VMEM)
    return pl.pallas_call(
        body, name=name, in_specs=[vm, pl.BlockSpec(memory_space=pl.ANY)], out_specs=vm,
        out_shape=jax.ShapeDtypeStruct(x.shape, x.dtype),
    )(x, token)


def _all_gather_hbm(shards):
    n = len(shards)

    def body(*refs):
        p_refs, out_refs = refs[:n], refs[n:2 * n]
        send_sems, recv_sems, local_sems = refs[2 * n:]
        x, y, c = _mesh_pos()
        me, sibling = (x, y, c), (x, y, 1 - c)
        chips = [(1 - x, y), (x, 1 - y), (1 - x, 1 - y)]

        def copy(a, k, block, to, own=False):
            slot = out_refs[a].at[4 * block[0] + 2 * block[1] + block[2]]
            return pltpu.make_async_remote_copy(
                src_ref=p_refs[a] if own else slot, dst_ref=slot,
                send_sem=send_sems.at[a, k], recv_sem=recv_sems.at[a, k], device_id=to, device_id_type=MESH_ID)

        mine = [pltpu.make_async_copy(p_refs[a], out_refs[a].at[4 * x + 2 * y + c], local_sems.at[a]) for a in range(n)]
        for cp in mine:
            cp.start()
        first = []
        for a in range(n):
            first.append(copy(a, 0, me, sibling, own=True))
            first += [copy(a, 1 + j, me, (*chip, c), own=True) for j, chip in enumerate(chips)]
        for cp in first:
            cp.start()
        passed = []
        for j, chip in enumerate(chips):
            for a in range(n):
                copy(a, 1 + j, (*chip, c), me).wait_recv()
                passed.append(copy(a, 4 + j, (*chip, c), sibling))
                passed[-1].start()
        for a in range(n):
            copy(a, 0, sibling, me).wait_recv()
            for j, chip in enumerate(chips):
                copy(a, 4 + j, (*chip, 1 - c), me).wait_recv()
        for cp in first + passed:
            cp.wait_send()
        for cp in mine:
            cp.wait()

    hbm = pl.BlockSpec(memory_space=pl.ANY)
    return pl.pallas_call(
        body, name="gather_weights", in_specs=[hbm] * n, out_specs=[hbm] * n,
        out_shape=[jax.ShapeDtypeStruct((N_DEV,) + s.shape, s.dtype) for s in shards],
        scratch_shapes=[pltpu.SemaphoreType.DMA((n, N_DEV - 1)), pltpu.SemaphoreType.DMA((n, N_DEV - 1)),
                        pltpu.SemaphoreType.DMA((n,))],
    )(*shards)


def _adamw(w, g, m, v):
    m2 = ADAM_B1 * m + (1.0 - ADAM_B1) * g
    v2 = ADAM_B2 * v + (1.0 - ADAM_B2) * (g * g)
    m_hat = m2 / (1.0 - ADAM_B1 ** ADAM_STEP)
    v_hat = v2 / (1.0 - ADAM_B2 ** ADAM_STEP)
    delta = -ADAM_LR * (m_hat / (jnp.sqrt(v_hat) + ADAM_EPS) + ADAM_WD * w)
    return delta, m2, v2


def _mod_partial(cmat, w_shard, b_shard):
    def body(c_ref, w_ref, b_ref, o_ref):
        o_ref[...] = _dot(c_ref[...].astype(BF16), w_ref[...].astype(BF16)) + b_ref[...]

    return pl.pallas_call(
        body, name="mod_partial", out_shape=jax.ShapeDtypeStruct((N_DEV, w_shard.shape[1]), F32),
        compiler_params=_params(),
    )(cmat, w_shard, b_shard)


def _adamw_w_ada(cmat, dmod_cols, w, m, v):
    def body(c_ref, d_ref, w_ref, m_ref, v_ref, g_ref, dl_ref, m2_ref, v2_ref):
        g = _dot_tn(c_ref[...].astype(BF16), d_ref[...].astype(BF16))
        g_ref[...] = g
        dl_ref[...], m2_ref[...], v2_ref[...] = _adamw(w_ref[...], g, m_ref[...], v_ref[...])

    return pl.pallas_call(
        body, name="adamw_w_ada", out_shape=[jax.ShapeDtypeStruct(w.shape, F32)] * 4,
        compiler_params=_params(),
    )(cmat, dmod_cols, w, m, v)


SMALL_ROWS = 16
SMALL_AT = {"b_ada": (0, 6, 0, D_MODEL), "g_pre_mix": (6, 1, 0, D_MODEL), "g_post_mix": (7, 1, 0, D_MODEL),
            "g_pre_ffn": (8, 1, 0, D_MODEL), "g_post_ffn": (9, 1, 0, D_MODEL), "conv_b": (10, 1, 0, D_MODEL),
            "mlstm_norm_w": (11, 1, 0, 512), "b_if": (11, 1, 512, LANES), "attn_sinks": (11, 1, 640, LANES)}


def _small_table(part):
    tail = jnp.concatenate([part["mlstm_norm_w"], jnp.pad(part["b_if"], (0, LANES - 8)),
                            jnp.pad(part["attn_sinks"], (0, LANES - 8)), jnp.zeros((256,), F32)])
    return jnp.concatenate([part["b_ada"], part["gains"], part["conv_b"][None], tail[None],
                            jnp.zeros((SMALL_ROWS - 12, D_MODEL), F32)], axis=0)


def _adamw_small(gathered, wmv):
    names = list(SMALL_AT)

    def body(*refs):
        g_ref, ins, outs = refs[0], refs[1:1 + 3 * len(names)], refs[1 + 3 * len(names):]
        g = g_ref[0]
        for k in range(1, N_DEV):
            g = g + g_ref[k]
        for i, n in enumerate(names):
            r0, rows, l0, lanes = SMALL_AT[n]
            gi = jnp.concatenate([g[r:r + 1, l0:l0 + lanes] for r in range(r0, r0 + rows)], axis=1)
            w_ref, m_ref, v_ref = ins[3 * i:3 * i + 3]
            go, dl, m2, v2 = outs[4 * i:4 * i + 4]
            go[...] = gi
            dl[...], m2[...], v2[...] = _adamw(w_ref[...], gi, m_ref[...], v_ref[...])

    flat = [a for n in names for a in wmv[n]]
    res = pl.pallas_call(
        body, name="adamw_small",
        out_shape=[jax.ShapeDtypeStruct(wmv[n][0].shape, F32) for n in names for _ in range(4)],
        compiler_params=_params(),
    )(gathered, *flat)
    return {n: res[4 * i:4 * i + 4] for i, n in enumerate(names)}


def _row_tile(rows):
    return rows // 4 if rows >= 512 and rows % 64 == 0 else rows


def _sum_partials(r_ref):
    g = r_ref[0].astype(F32)
    for k in range(1, N_DEV):
        g = g + r_ref[k].astype(F32)
    return g


def _adamw_sum(recv, w, m, v, name):
    r, cdim = w.shape
    tr = _row_tile(r)

    def body(r_ref, w_ref, m_ref, v_ref, g_ref, dl_ref, m2_ref, v2_ref):
        g = _sum_partials(r_ref)
        g_ref[...] = g
        dl_ref[...], m2_ref[...], v2_ref[...] = _adamw(w_ref[...], g, m_ref[...], v_ref[...])

    row = pl.BlockSpec((tr, cdim), lambda i: (i, 0))
    return pl.pallas_call(
        body, name=name, grid=(r // tr,),
        in_specs=[pl.BlockSpec((N_DEV, tr, cdim), lambda i: (0, i, 0)), row, row, row],
        out_specs=[row] * 4, out_shape=[jax.ShapeDtypeStruct((r, cdim), F32)] * 4,
        compiler_params=_params(("parallel",)),
    )(recv, w, m, v)


def _sum8(recv, name):
    _, r, cdim = recv.shape
    tr = _row_tile(r)

    def body(r_ref, g_ref):
        g_ref[...] = _sum_partials(r_ref)

    return pl.pallas_call(
        body, name=name, grid=(r // tr,),
        in_specs=[pl.BlockSpec((N_DEV, tr, cdim), lambda i: (0, i, 0))],
        out_specs=pl.BlockSpec((tr, cdim), lambda i: (i, 0)), out_shape=jax.ShapeDtypeStruct((r, cdim), F32),
        compiler_params=_params(("parallel",)),
    )(recv)


def _adamw_plain(g, w, m, v, name):
    r, cdim = w.shape
    tr = _row_tile(r)

    def body(g_ref, w_ref, m_ref, v_ref, dl_ref, m2_ref, v2_ref):
        dl_ref[...], m2_ref[...], v2_ref[...] = _adamw(w_ref[...], g_ref[...], m_ref[...], v_ref[...])

    row = pl.BlockSpec((tr, cdim), lambda i: (i, 0))
    return pl.pallas_call(
        body, name=name, grid=(r // tr,), in_specs=[row] * 4, out_specs=[row] * 3,
        out_shape=[jax.ShapeDtypeStruct((r, cdim), F32)] * 3,
        compiler_params=_params(("parallel",)),
    )(g, w, m, v)


IN_SHARD = 609
IN_SHARD_PAD = 640
IF_AT = A_W + M_W


def _regrouped(u):
    return u if u < IF_AT + 8 else u + (IF_W - 8)


def _selection(k, rows, row0, transpose):
    shape = (rows, IN_SHARD_PAD) if transpose else (IN_SHARD_PAD, rows)
    l = lax.broadcasted_iota(jnp.int32, shape, 1 if transpose else 0)
    r = lax.broadcasted_iota(jnp.int32, shape, 0 if transpose else 1) + row0
    u = l + IN_SHARD * k
    ru = u + jnp.where(u >= IF_AT + 8, IF_W - 8, 0)
    return ((ru == r) & (l < IN_SHARD)).astype(BF16)


def _regroup_w_in(g):
    def body(g_ref, o_ref):
        for cb in range(CAT_W // LANES):
            r0 = cb * LANES
            acc = jnp.zeros((D_MODEL, LANES), F32)
            for k in range(N_DEV):
                lo, hi = _regrouped(IN_SHARD * k), _regrouped(IN_SHARD * k + IN_SHARD - 1)
                if hi >= r0 and lo < r0 + LANES:
                    acc = acc + _dot(g_ref[k], _selection(k, LANES, r0, False))
            o_ref[:, r0:r0 + LANES] = acc.astype(BF16)

    return pl.pallas_call(
        body, name="regroup_w_in", out_shape=jax.ShapeDtypeStruct((D_MODEL, CAT_W), BF16),
        compiler_params=_params(),
    )(g)


def _ungroup_w_in(g_parts):
    n = len(g_parts)

    def body(*refs):
        o_ref, g_ref = refs[n], refs[n + 1]
        at = 0
        for p in refs[:n]:
            g_ref[:, at:at + p.shape[1]] = p[...]
            at += p.shape[1]
        for k in range(N_DEV):
            lo, hi = _regrouped(IN_SHARD * k), _regrouped(IN_SHARD * k + IN_SHARD - 1)
            w0, w1 = lo // LANES * LANES, (hi // LANES + 1) * LANES
            o_ref[k] = _dot(g_ref[:, w0:w1], _selection(k, w1 - w0, w0, True)).astype(BF16)

    return pl.pallas_call(
        body, name="ungroup_w_in", out_shape=jax.ShapeDtypeStruct((N_DEV, D_MODEL, IN_SHARD_PAD), BF16),
        scratch_shapes=[pltpu.VMEM((D_MODEL, CAT_W), BF16)], compiler_params=_params(),
    )(*g_parts)


WEIGHT_NAMES = ("w_ada", "b_ada", "g_pre_mix", "g_post_mix", "w_in", "b_if", "conv_w", "conv_b", "attn_sinks",
                "mlstm_norm_w", "w_branch_attn", "w_branch_mlstm", "w_out", "g_pre_ffn", "g_post_ffn",
                "w_ffn_gate", "w_ffn_up", "w_ffn_down")


def kernel(x, c, positions, w_ada, b_ada, g_pre_mix, g_post_mix, w_in, b_if, conv_w, conv_b, attn_sinks, mlstm_norm_w, w_branch_attn, w_branch_mlstm, w_out, g_pre_ffn, g_post_ffn, w_ffn_gate, w_ffn_up, w_ffn_down, loss_target, m_w_ada, m_b_ada, m_g_pre_mix, m_g_post_mix, m_w_in, m_b_if, m_conv_w, m_conv_b, m_attn_sinks, m_mlstm_norm_w, m_w_branch_attn, m_w_branch_mlstm, m_w_out, m_g_pre_ffn, m_g_post_ffn, m_w_ffn_gate, m_w_ffn_up, m_w_ffn_down, v_w_ada, v_b_ada, v_g_pre_mix, v_g_post_mix, v_w_in, v_b_if, v_conv_w, v_conv_b, v_attn_sinks, v_mlstm_norm_w, v_w_branch_attn, v_w_branch_mlstm, v_w_out, v_g_pre_ffn, v_g_post_ffn, v_w_ffn_gate, v_w_ffn_up, v_w_ffn_down):
    given = dict(locals())
    W = {n: given[n][0] for n in WEIGHT_NAMES}
    M = {n: given["m_" + n][0] for n in WEIGHT_NAMES}
    V = {n: given["v_" + n][0] for n in WEIGHT_NAMES}
    me = 4 * lax.axis_index("x") + 2 * lax.axis_index("y") + lax.axis_index("c")

    ff_sh = D_FF // N_DEV
    g_in, g_conv, cg = _all_gather_hbm([jnp.pad(W["w_in"], ((0, 0), (0, IN_SHARD_PAD - IN_SHARD))).astype(BF16),
                                        jnp.pad(W["conv_w"], ((0, 4), (0, 0))), c.reshape(8, D_MODEL // 8)])

    cmat = cg.reshape(N_DEV, D_MODEL)
    ada_w = D_MODEL * 6 // N_DEV
    b_cols = lax.dynamic_slice(W["b_ada"], (me * ada_w,), (ada_w,)).reshape(1, ada_w)
    mod_part = _mod_partial(cmat, W["w_ada"], b_cols)
    mod_recv = _small_exchange(jnp.broadcast_to(mod_part[:, None, :], (N_DEV, 8, ada_w)), False, "scatter_mod")
    mod = mod_recv[:, 0, :].reshape(6, D_MODEL)

    st_b = _exchange_start([W["w_branch_attn"].astype(BF16), W["w_branch_mlstm"].astype(BF16), W["w_out"].astype(BF16),
                            W["w_ffn_gate"].T.astype(BF16), W["w_ffn_up"].T.astype(BF16), W["w_ffn_down"].astype(BF16)],
                           True, mod_recv, "gather_rest_start")
    cols = lambda g: g.transpose(1, 0, 2).reshape(g.shape[1], N_DEV * g.shape[2])
    gains = jnp.stack([W["g_pre_mix"], W["g_post_mix"], W["g_pre_ffn"], W["g_post_ffn"]])
    xs, tgt = x[0], loss_target[0]
    t = _tables(mod, gains, cols(g_conv)[0:4], W["conv_b"], W["b_if"], W["mlstm_norm_w"], positions[0])
    vecs = t["vecs"]
    t["vecs"] = _tie(vecs, st_b["token"], "tie_fwd")
    w_cat = _regroup_w_in(g_in)
    a = _mixer_fwd(xs, t, W["attn_sinks"], w_cat)
    g_ba, g_bm, g_out, g_gate, g_up, g_down = _exchange_wait(st_b, a["ym"], "gather_rest_wait")
    w_ba, w_bm, w_out = cols(g_ba), cols(g_bm), g_out.reshape(D_MODEL, D_MODEL)
    b = _ffn_part(xs, tgt, t, a, w_ba, w_bm, w_out, g_gate.reshape(D_FF, D_MODEL), g_up.reshape(D_FF, D_MODEL),
                  g_down.reshape(D_FF, D_MODEL))

    st_f = _exchange_start([b["g_w_gate"].reshape(N_DEV, ff_sh, D_MODEL), b["g_w_up"].reshape(N_DEV, ff_sh, D_MODEL),
                            b["g_w_down"].reshape(N_DEV, ff_sh, D_MODEL)], False, b["dx1"], "scatter_ffn_start")
    t["vecs"] = _tie(vecs, st_f["token"], "tie_bwd")
    cm = _mixer_bwd(b["dx1"], t, a, b, W["attn_sinks"], w_ba, w_bm, w_out)
    pieces = lambda g, n: g.reshape(g.shape[0], N_DEV, n).transpose(1, 0, 2)
    st_m = _exchange_start([_ungroup_w_in(cm["g_w_cat"]), pieces(cm["g_w_ba"], 128), pieces(cm["g_w_bm"], 128),
                            cm["g_w_out"].reshape(N_DEV, D_MODEL // N_DEV, D_MODEL),
                            jnp.pad(pieces(cm["dcw"][0:4], 128), ((0, 0), (0, 4), (0, 0)))], False, cm["dcw"],
                           "scatter_mixer_start")
    r_gate, r_up, r_down = _exchange_wait(st_f, st_m["token"], "scatter_ffn_wait")
    grad_x, acc_p = _pre_bwd(cm["dproj"], xs, b["dx1"], _tie(vecs, st_m["token"], "tie_pre_bwd"), w_cat)
    small = _small_grads(acc_p, b, cm)
    loss = b["loss"]

    big_out = [{} for _ in range(4)]

    def put(n, res):
        for k in range(4):
            big_out[k][n] = res[k][None]

    put("w_ffn_down", _adamw_sum(r_down, W["w_ffn_down"], M["w_ffn_down"], V["w_ffn_down"], "adamw_w_ffn_down"))
    for n, r in (("w_ffn_gate", r_gate), ("w_ffn_up", r_up)):
        put(n, [o.T for o in _adamw_sum(r, W[n].T, M[n].T, V[n].T, "adamw_" + n)])

    sg = _small_exchange(_small_table({"b_ada": small["dmod"], "gains": small["dgains"], "conv_b": small["dconv_b"],
                                       "mlstm_norm_w": small["dnorm_w"], "b_if": small["db_if"],
                                       "attn_sinks": small["dsinks"]}), True, "gather_small")
    as_row = lambda a, n: jnp.pad(a, (0, SMALL_AT[n][3] * SMALL_AT[n][1] - a.shape[0]))[None]
    small_res = _adamw_small(sg, {n: [as_row(d[n], n) for d in (W, M, V)] for n in SMALL_AT})
    small_out = [{n: small_res[n][k][:, 0:W[n].shape[0]] for n in SMALL_AT} for k in range(4)]
    dmod_cols = lax.dynamic_slice(sg[:, 0:6, :].reshape(N_DEV, 6 * D_MODEL), (0, me * ada_w), (N_DEV, ada_w))
    ada_out = _adamw_w_ada(cmat, dmod_cols, W["w_ada"], M["w_ada"], V["w_ada"])

    r_in, r_ba, r_bm, r_out, r_conv = _exchange_wait(st_m, grad_x, "scatter_mixer_wait")
    for n, r in (("w_branch_attn", r_ba), ("w_branch_mlstm", r_bm), ("w_out", r_out)):
        put(n, _adamw_sum(r, W[n], M[n], V[n], "adamw_" + n))
    pad4 = lambda v: jnp.pad(v, ((0, 4), (0, 0)))
    put("conv_w", [o[0:4] for o in _adamw_sum(r_conv, pad4(W["conv_w"]), pad4(M["conv_w"]), pad4(V["conv_w"]),
                                                 "adamw_conv_w")])
    g = _sum8(r_in, "sum_w_in")[:, 0:IN_SHARD].T
    put("w_in", [o.T for o in [g] + list(_adamw_plain(g, W["w_in"].T, M["w_in"].T, V["w_in"].T, "adamw_w_in"))])

    total = lax.psum(loss, ("x", "y", "c"))
    outs = [total, grad_x[None]]
    for k in range(4):
        for n in WEIGHT_NAMES:
            if n == "w_ada":
                outs.append(ada_out[k][None])
            elif n in big_out[k]:
                outs.append(big_out[k][n])
            else:
                outs.append(small_out[k][n])
    return tuple(outs)
```

```python
import functools

import jax
import jax.numpy as jnp
import numpy as np
from jax import lax
from jax.experimental import pallas as pl
from jax.experimental.pallas import tpu as pltpu

F32 = jnp.float32
BF16 = jnp.bfloat16

N_DEV = 8
D_MODEL = 1024
D_FF = 2816
N_Q_HEADS = 8
HEAD_DIM = 64
ATTN_BLOCK = 128
ROPE_THETA = 10000.0
MLSTM_HEADS = 4
MLSTM_HEAD_DIM = 128
MLSTM_CHUNK = 128
NORM_EPS = 1e-6
ADAM_LR = 0.001
ADAM_B1 = 0.9
ADAM_B2 = 0.999
ADAM_EPS = 1e-08
ADAM_WD = 0.01
ADAM_STEP = 10

ROW_TILE = 256
WIDE_TILE = 512
LANES = 128
NEG = -1e30
VMEM_LIMIT = 56 * 1024 * 1024

A_W = 768
M_W = 2048
IF_W = 128
G_W = 2048
CAT_W = A_W + M_W + IF_W + G_W

R_SHIFT_M, R_SCALE_M, R_GATE_M, R_SHIFT_F, R_SCALE_F, R_GATE_F = 0, 1, 2, 3, 4, 5
R_G_PRE_MIX, R_G_POST_MIX, R_G_PRE_FFN, R_G_POST_FFN = 6, 7, 8, 9


def _dot(a, b):
    return jnp.dot(a, b, preferred_element_type=F32)


def _dot_nt(a, b):
    return lax.dot_general(a, b, (((1,), (1,)), ((), ())), preferred_element_type=F32)


def _dot_tn(a, b):
    return lax.dot_general(a, b, (((0,), (0,)), ((), ())), preferred_element_type=F32)


def _recip(x):
    return 1.0 / x


def _sigmoid(x):
    return _recip(1.0 + jnp.exp(-x))


def _colsum(x):
    return jnp.sum(x, axis=0, keepdims=True)


def _rowmean(x):
    return jnp.mean(x, axis=-1, keepdims=True)


def _params(sem=None, vmem=VMEM_LIMIT):
    kw = dict(vmem_limit_bytes=vmem)
    if sem is not None:
        kw["dimension_semantics"] = sem
    return pltpu.CompilerParams(**kw)


def _full(shape):
    nd = len(shape)
    return pl.BlockSpec(shape, lambda *_: (0,) * nd)


def _pre_proj(x, vecs, w_cat):
    S = x.shape[0]
    tm = WIDE_TILE

    def body(x_ref, v_ref, w_ref, h_ref, pa_ref, pm_ref, pif_ref, pg_ref):
        xv = x_ref[...]
        r = lax.rsqrt(_rowmean(xv * xv) + NORM_EPS)
        h = (xv * r * v_ref[R_G_PRE_MIX:R_G_PRE_MIX + 1, :]) * (1.0 + v_ref[R_SCALE_M:R_SCALE_M + 1, :]) \
            + v_ref[R_SHIFT_M:R_SHIFT_M + 1, :]
        hb = h.astype(BF16)
        h_ref[...] = hb
        pa_ref[...] = _dot(hb, w_ref[:, 0:A_W])
        pm_ref[...] = _dot(hb, w_ref[:, A_W:A_W + M_W])
        pif_ref[...] = _dot(hb, w_ref[:, A_W + M_W:A_W + M_W + IF_W])
        pg_ref[...] = _dot(hb, w_ref[:, A_W + M_W + IF_W:CAT_W]).astype(BF16)

    row = lambda w: pl.BlockSpec((tm, w), lambda i: (i, 0))
    return pl.pallas_call(
        body, name="pre_proj", grid=(S // tm,),
        in_specs=[row(D_MODEL), _full(vecs.shape), _full(w_cat.shape)],
        out_specs=[row(D_MODEL), row(A_W), row(M_W), row(IF_W), row(G_W)],
        out_shape=[jax.ShapeDtypeStruct((S, D_MODEL), BF16), jax.ShapeDtypeStruct((S, A_W), F32),
                   jax.ShapeDtypeStruct((S, M_W), F32), jax.ShapeDtypeStruct((S, IF_W), F32),
                   jax.ShapeDtypeStruct((S, G_W), BF16)],
        compiler_params=_params(("parallel",)),
    )(x, vecs, w_cat)


def _mix_fwd(x, ya, ym, pg, vecs, w_ba, w_bm, w_out):
    S = x.shape[0]
    tm = WIDE_TILE

    def body(x_ref, ya_ref, ym_ref, pg_ref, v_ref, wba_ref, wbm_ref, wout_ref,
             x1_ref, merged_ref, mix_ref, pa_ref, pb_ref):
        pa = _dot(ya_ref[...], wba_ref[...])
        pb = _dot(ym_ref[...], wbm_ref[...])
        merged = _sigmoid(pg_ref[:, 0:D_MODEL].astype(F32)) * pa + _sigmoid(pg_ref[:, D_MODEL:G_W].astype(F32)) * pb
        mb = merged.astype(BF16)
        mix = _dot(mb, wout_ref[...])
        r = lax.rsqrt(_rowmean(mix * mix) + NORM_EPS)
        x1_ref[...] = x_ref[...] + v_ref[R_GATE_M:R_GATE_M + 1, :] * (mix * r * v_ref[R_G_POST_MIX:R_G_POST_MIX + 1, :])
        merged_ref[...] = mb
        mix_ref[...] = mix
        pa_ref[...] = pa.astype(BF16)
        pb_ref[...] = pb.astype(BF16)

    row = lambda w: pl.BlockSpec((tm, w), lambda i: (i, 0))
    sd = lambda w, dt: jax.ShapeDtypeStruct((S, w), dt)
    return pl.pallas_call(
        body, name="mix_fwd", grid=(S // tm,),
        in_specs=[row(D_MODEL), row(512), row(512), row(G_W), _full(vecs.shape), _full(w_ba.shape),
                  _full(w_bm.shape), _full(w_out.shape)],
        out_specs=[row(D_MODEL)] * 5,
        out_shape=[sd(D_MODEL, F32), sd(D_MODEL, BF16), sd(D_MODEL, F32), sd(D_MODEL, BF16), sd(D_MODEL, BF16)],
        compiler_params=_params(("parallel",)),
    )(x, ya, ym, pg, vecs, w_ba, w_bm, w_out)


def _ffn_fwd_bwd(x1, tgt, vecs, w_gate, w_up, w_down):
    S = x1.shape[0]
    tm = ROW_TILE

    def body(x1_ref, tgt_ref, v_ref, wg_hbm, wu_hbm, wd_hbm,
             dx1_ref, h2_ref, hid_ref, da_ref, du_ref, dff_ref, acc_ref, loss_ref,
             wg, wu, wd, sem):
        i = pl.program_id(0)

        @pl.when(i == 0)
        def _():
            cps = [pltpu.make_async_copy(wg_hbm, wg, sem.at[0]), pltpu.make_async_copy(wu_hbm, wu, sem.at[1]),
                   pltpu.make_async_copy(wd_hbm, wd, sem.at[2])]
            for cp in cps:
                cp.start()
            for cp in cps:
                cp.wait()
            acc_ref[...] = jnp.zeros_like(acc_ref)
            loss_ref[...] = jnp.zeros_like(loss_ref)

        vrow = lambda r: v_ref[r:r + 1, :]
        x1v = x1_ref[...]
        r3 = lax.rsqrt(_rowmean(x1v * x1v) + NORM_EPS)
        x1hat = x1v * r3
        xn3 = x1hat * vrow(R_G_PRE_FFN)
        h2b = (xn3 * (1.0 + vrow(R_SCALE_F)) + vrow(R_SHIFT_F)).astype(BF16)
        h2_ref[...] = h2b
        a = _dot_nt(h2b, wg[...])
        u = _dot_nt(h2b, wu[...])
        sg = _sigmoid(a)
        sil = a * sg
        hidb = (sil * u).astype(BF16)
        hid_ref[...] = hidb
        ff = _dot(hidb, wd[...])
        r4 = lax.rsqrt(_rowmean(ff * ff) + NORM_EPS)
        ffhat = ff * r4
        n4 = ffhat * vrow(R_G_POST_FFN)
        err = x1v + vrow(R_GATE_F) * n4 - tgt_ref[...]
        loss_ref[...] += jnp.sum(err * err) * (0.5 / D_MODEL)
        dy = err * (1.0 / D_MODEL)
        acc_ref[0:1, :] += _colsum(dy * n4)
        dn4 = dy * vrow(R_GATE_F)
        acc_ref[1:2, :] += _colsum(dn4 * ffhat)
        dffhat = dn4 * vrow(R_G_POST_FFN)
        dffb = (r4 * (dffhat - ffhat * _rowmean(dffhat * ffhat))).astype(BF16)
        dff_ref[...] = dffb
        dhid = _dot_nt(dffb, wd[...])
        dub = (dhid * sil).astype(BF16)
        dab = (dhid * u * (sg * (1.0 + a * (1.0 - sg)))).astype(BF16)
        da_ref[...] = dab
        du_ref[...] = dub
        dh2 = _dot(dab, wg[...]) + _dot(dub, wu[...])
        acc_ref[2:3, :] += _colsum(dh2 * xn3)
        acc_ref[3:4, :] += _colsum(dh2)
        dxn3 = dh2 * (1.0 + vrow(R_SCALE_F))
        acc_ref[4:5, :] += _colsum(dxn3 * x1hat)
        dx1hat = dxn3 * vrow(R_G_PRE_FFN)
        dx1_ref[...] = dy + r3 * (dx1hat - x1hat * _rowmean(dx1hat * x1hat))

    row = lambda w: pl.BlockSpec((tm, w), lambda i: (i, 0))
    sd = lambda w, dt: jax.ShapeDtypeStruct((S, w), dt)
    anyspec = pl.BlockSpec(memory_space=pl.ANY)
    return pl.pallas_call(
        body, name="ffn_fwd_bwd", grid=(S // tm,),
        in_specs=[row(D_MODEL), row(D_MODEL), _full(vecs.shape), anyspec, anyspec, anyspec],
        out_specs=[row(D_MODEL), row(D_MODEL), row(D_FF), row(D_FF), row(D_FF), row(D_MODEL),
                   _full((8, D_MODEL)), _full((8, LANES))],
        out_shape=[sd(D_MODEL, F32), sd(D_MODEL, BF16), sd(D_FF, BF16), sd(D_FF, BF16), sd(D_FF, BF16),
                   sd(D_MODEL, BF16), jax.ShapeDtypeStruct((8, D_MODEL), F32), jax.ShapeDtypeStruct((8, LANES), F32)],
        scratch_shapes=[pltpu.VMEM(w_gate.shape, BF16), pltpu.VMEM(w_up.shape, BF16), pltpu.VMEM(w_down.shape, BF16),
                        pltpu.SemaphoreType.DMA((3,))],
        compiler_params=_params(("arbitrary",)),
    )(x1, tgt, vecs, w_gate, w_up, w_down)


def _mix_bwd(dx1, mix, pa, pb, pg, vecs, w_ba, w_bm, w_out):
    S = dx1.shape[0]
    tm = WIDE_TILE

    def body(dx1_ref, mix_ref, pa_ref, pb_ref, pg_ref, v_ref, wba_ref, wbm_ref, wout_ref,
             dmix_ref, dpa_ref, dpb_ref, dg_ref, dya_ref, dym_ref, acc_ref):
        i = pl.program_id(0)

        @pl.when(i == 0)
        def _():
            acc_ref[...] = jnp.zeros_like(acc_ref)

        vrow = lambda r: v_ref[r:r + 1, :]
        dx1v = dx1_ref[...]
        mix = mix_ref[...]
        r2 = lax.rsqrt(_rowmean(mix * mix) + NORM_EPS)
        mixhat = mix * r2
        acc_ref[0:1, :] += _colsum(dx1v * (mixhat * vrow(R_G_POST_MIX)))
        dn2 = dx1v * vrow(R_GATE_M)
        acc_ref[1:2, :] += _colsum(dn2 * mixhat)
        dmixhat = dn2 * vrow(R_G_POST_MIX)
        dmixb = (r2 * (dmixhat - mixhat * _rowmean(dmixhat * mixhat))).astype(BF16)
        dmix_ref[...] = dmixb
        dmerged = _dot_nt(dmixb, wout_ref[...])
        sa = _sigmoid(pg_ref[:, 0:D_MODEL].astype(F32))
        sm = _sigmoid(pg_ref[:, D_MODEL:G_W].astype(F32))
        dpab = (dmerged * sa).astype(BF16)
        dpbb = (dmerged * sm).astype(BF16)
        dpa_ref[...] = dpab
        dpb_ref[...] = dpbb
        dg_ref[:, 0:D_MODEL] = (dmerged * pa_ref[...].astype(F32) * (sa * (1.0 - sa))).astype(BF16)
        dg_ref[:, D_MODEL:G_W] = (dmerged * pb_ref[...].astype(F32) * (sm * (1.0 - sm))).astype(BF16)
        dya_ref[...] = _dot_nt(dpab, wba_ref[...])
        dym_ref[...] = _dot_nt(dpbb, wbm_ref[...])

    row = lambda w: pl.BlockSpec((tm, w), lambda i: (i, 0))
    sd = lambda w, dt: jax.ShapeDtypeStruct((S, w), dt)
    return pl.pallas_call(
        body, name="mix_bwd", grid=(S // tm,),
        in_specs=[row(D_MODEL), row(D_MODEL), row(D_MODEL), row(D_MODEL), row(G_W), _full(vecs.shape),
                  _full(w_ba.shape), _full(w_bm.shape), _full(w_out.shape)],
        out_specs=[row(D_MODEL), row(D_MODEL), row(D_MODEL), row(G_W), row(512), row(512), _full((8, D_MODEL))],
        out_shape=[sd(D_MODEL, BF16), sd(D_MODEL, BF16), sd(D_MODEL, BF16), sd(G_W, BF16), sd(512, F32), sd(512, F32),
                   jax.ShapeDtypeStruct((8, D_MODEL), F32)],
        compiler_params=_params(("arbitrary",)),
    )(dx1, mix, pa, pb, pg, vecs, w_ba, w_bm, w_out)


def _pre_bwd(pieces, x, dx1, vecs, w_cat):
    S = x.shape[0]
    tm = WIDE_TILE
    n = len(pieces)
    starts = [sum(p.shape[1] for p in pieces[:k]) for k in range(n + 1)]

    def body(*refs):
        p_refs = refs[:n]
        x_ref, dx1_ref, v_ref, w_ref, dx_ref, acc_ref = refs[n:]
        i = pl.program_id(0)

        @pl.when(i == 0)
        def _():
            acc_ref[...] = jnp.zeros_like(acc_ref)

        vrow = lambda r: v_ref[r:r + 1, :]
        dh = _dot_nt(p_refs[0][...], w_ref[:, starts[0]:starts[1]])
        for k in range(1, n):
            dh = dh + _dot_nt(p_refs[k][...], w_ref[:, starts[k]:starts[k + 1]])
        xv = x_ref[...]
        r1 = lax.rsqrt(_rowmean(xv * xv) + NORM_EPS)
        xhat = xv * r1
        acc_ref[0:1, :] += _colsum(dh * (xhat * vrow(R_G_PRE_MIX)))
        acc_ref[1:2, :] += _colsum(dh)
        dxn = dh * (1.0 + vrow(R_SCALE_M))
        acc_ref[2:3, :] += _colsum(dxn * xhat)
        dxhat = dxn * vrow(R_G_PRE_MIX)
        dx_ref[...] = dx1_ref[...] + r1 * (dxhat - xhat * _rowmean(dxhat * xhat))

    row = lambda w: pl.BlockSpec((tm, w), lambda i: (i, 0))
    return pl.pallas_call(
        body, name="pre_bwd", grid=(S // tm,),
        in_specs=[row(p.shape[1]) for p in pieces] + [row(D_MODEL), row(D_MODEL), _full(vecs.shape), _full(w_cat.shape)],
        out_specs=[row(D_MODEL), _full((8, D_MODEL))],
        out_shape=[jax.ShapeDtypeStruct((S, D_MODEL), F32), jax.ShapeDtypeStruct((8, D_MODEL), F32)],
        compiler_params=_params(("arbitrary",)),
    )(*pieces, x, dx1, vecs, w_cat)


def _matmul_tn(a, b, tn, name, ts=1024):
    S, K = a.shape
    N = b.shape[1]
    n_s = S // ts

    def body(a_ref, b_ref, o_ref, acc_ref):
        s = pl.program_id(1)

        @pl.when(s == 0)
        def _():
            acc_ref[...] = jnp.zeros_like(acc_ref)

        acc_ref[...] += _dot_tn(a_ref[...], b_ref[...])

        @pl.when(s == n_s - 1)
        def _():
            o_ref[...] = acc_ref[...].astype(BF16)

    return pl.pallas_call(
        body, name=name, grid=(N // tn, n_s),
        in_specs=[pl.BlockSpec((ts, K), lambda j, s: (s, 0)), pl.BlockSpec((ts, tn), lambda j, s: (s, j))],
        out_specs=pl.BlockSpec((K, tn), lambda j, s: (0, j)),
        out_shape=jax.ShapeDtypeStruct((K, N), BF16),
        scratch_shapes=[pltpu.VMEM((K, tn), F32)],
        compiler_params=_params(("parallel", "arbitrary")),
    )(a, b)


def _rope_swap(t):
    lane = lax.broadcasted_iota(jnp.int32, t.shape, 1)
    first = (lane & (HEAD_DIM - 1)) < (HEAD_DIM // 2)
    return jnp.where(first, pltpu.roll(t, LANES - HEAD_DIM // 2, 1), pltpu.roll(t, HEAD_DIM // 2, 1))


def _rope(t, cos, sin_signed):
    return t * cos + _rope_swap(t) * sin_signed


def _rope_t(d, cos, sin_signed):
    return d * cos + _rope_swap(d * sin_signed)


def _to_kv_lanes(chunk, p, h):
    lane = lax.broadcasted_iota(jnp.int32, chunk.shape, 1)
    src = chunk if p == h else pltpu.roll(chunk, HEAD_DIM, 1)
    return jnp.where((lane >> 6) == h, src, jnp.zeros_like(src))


def _from_kv_lanes(o_a, o_b, h):
    lane = lax.broadcasted_iota(jnp.int32, o_a.shape, 1)
    a = o_a if h == 0 else pltpu.roll(o_a, HEAD_DIM, 1)
    b = o_b if h == 1 else pltpu.roll(o_b, HEAD_DIM, 1)
    return jnp.where(lane < HEAD_DIM, a, b)


def _band_bias(n):
    blk = ATTN_BLOCK
    qi = lax.broadcasted_iota(jnp.int32, (blk, 2 * blk), 0)
    kj = lax.broadcasted_iota(jnp.int32, (blk, 2 * blk), 1)
    seen = (kj > qi) & (kj <= qi + blk) & ((n > 0) | (kj >= blk))
    return jnp.concatenate([jnp.where(seen, 0.0, NEG)] * N_Q_HEADS, axis=0)


def _stack_heads(chunks, h, dtype):
    parts = []
    for g in range(4):
        j = 4 * h + g
        parts.append(_to_kv_lanes(chunks[j // 2], j % 2, h))
    return jnp.concatenate(parts, axis=0).astype(dtype)


def _fused_call(parts, name, n_steps):
    counts = [(len(p["in_specs"]), len(p["out_specs"]), len(p["scratch"])) for p in parts]
    n_in, n_out = sum(c[0] for c in counts), sum(c[1] for c in counts)

    def kernel_fn(*refs):
        i = pl.program_id(0)
        groups, a, b, c = [], 0, n_in, n_in + n_out
        for ci, co, cs in counts:
            groups.append(refs[a:a + ci] + refs[b:b + co] + refs[c:c + cs])
            a, b, c = a + ci, b + co, c + cs
        for p, g in zip(parts, groups):
            p["init"](i, *g)
        for p, g in zip(parts, groups):
            p["body"](i, *g)

    flat = lambda key: [v for p in parts for v in p[key]]
    res = pl.pallas_call(
        kernel_fn, name=name, grid=(n_steps,), in_specs=flat("in_specs"), out_specs=flat("out_specs"),
        out_shape=flat("out_shape"), scratch_shapes=flat("scratch"), compiler_params=_params(("arbitrary",)),
    )(*flat("operands"))
    out, pos = [], 0
    for _, co, _ in counts:
        out.append(res[pos:pos + co])
        pos += co
    return out


def _attn_fwd_part(pa, cos, sin, sinks):
    S = pa.shape[0]
    blk = ATTN_BLOCK
    nb = S // blk

    def body(n, sink_ref, cur_ref, prev_ref, cos_ref, sin_ref, cosp_ref, sinp_ref,
             ya_ref, qr_ref, kr_ref, vb_ref, lse_ref):
        cos_c, sin_c = cos_ref[...], sin_ref[...]
        qch = [_rope(cur_ref[:, c * LANES:(c + 1) * LANES], cos_c, sin_c) * (HEAD_DIM ** -0.5) for c in range(4)]
        for c in range(4):
            qr_ref[:, c * LANES:(c + 1) * LANES] = qch[c].astype(BF16)
        k_cur = _rope(cur_ref[:, 512:640], cos_c, sin_c).astype(BF16)
        k_prev = _rope(prev_ref[:, 0:LANES], cosp_ref[...], sinp_ref[...]).astype(BF16)
        v_cur = cur_ref[:, 640:768].astype(BF16)
        v_prev = prev_ref[:, LANES:2 * LANES].astype(BF16)
        kr_ref[...] = k_cur
        vb_ref[...] = v_cur
        K = jnp.concatenate([k_prev, k_cur], axis=0)
        V = jnp.concatenate([v_prev, v_cur], axis=0)
        lane = lax.broadcasted_iota(jnp.int32, (blk, LANES), 1)
        s = jnp.concatenate([_dot_nt(_stack_heads(qch, h, BF16), K) for h in range(2)], axis=0)
        s = s + _band_bias(n)
        rowmax = jnp.max(s, axis=1, keepdims=True)
        hd = lambda x, j: x[j * blk:(j + 1) * blk]
        m = jnp.concatenate([jnp.maximum(hd(rowmax, j), sink_ref[j]) for j in range(N_Q_HEADS)], axis=0)
        p = jnp.exp(s - m)
        pb = p.astype(BF16)
        den = _rowsum_mxu(pb) + jnp.concatenate([jnp.exp(sink_ref[j] - hd(m, j)) for j in range(N_Q_HEADS)], axis=0)
        o = jnp.concatenate([_dot(pb[4 * h * blk:4 * (h + 1) * blk], V) for h in range(2)], axis=0) * _recip(den)
        lse = m + jnp.log(den)
        outs = [o[j * blk:(j + 1) * blk, :] for j in range(N_Q_HEADS)]
        lse_tile = jnp.zeros((blk, LANES), F32)
        for j in range(N_Q_HEADS):
            lse_tile = jnp.where(lane == j, lse[j * blk:(j + 1) * blk, :], lse_tile)
        for c in range(4):
            ya_ref[:, c * LANES:(c + 1) * LANES] = _from_kv_lanes(outs[2 * c], outs[2 * c + 1], c // 2).astype(BF16)
        lse_ref[...] = lse_tile

    prev = lambda n: jnp.maximum(n - 1, 0)
    sd = lambda w, dt: jax.ShapeDtypeStruct((S, w), dt)
    return dict(
        init=lambda n, *refs: None, body=body, scratch=[], operands=[sinks, pa, pa, cos, sin, cos, sin],
        in_specs=[pl.BlockSpec(memory_space=pltpu.SMEM),
                  pl.BlockSpec((blk, A_W), lambda n: (n, 0)),
                  pl.BlockSpec((blk, 256), lambda n: (prev(n), 2)),
                  pl.BlockSpec((blk, LANES), lambda n: (n, 0)), pl.BlockSpec((blk, LANES), lambda n: (n, 0)),
                  pl.BlockSpec((blk, LANES), lambda n: (prev(n), 0)), pl.BlockSpec((blk, LANES), lambda n: (prev(n), 0))],
        out_specs=[pl.BlockSpec((blk, 512), lambda n: (n, 0)), pl.BlockSpec((blk, 512), lambda n: (n, 0)),
                   pl.BlockSpec((blk, LANES), lambda n: (n, 0)), pl.BlockSpec((blk, LANES), lambda n: (n, 0)),
                   pl.BlockSpec((blk, LANES), lambda n: (n, 0))],
        out_shape=[sd(512, BF16), sd(512, BF16), sd(LANES, BF16), sd(LANES, BF16), sd(LANES, F32)])


def _attn_bwd_part(dya, qr, kr, vb, lse, cos, sin, sinks):
    S = dya.shape[0]
    blk = ATTN_BLOCK
    nb = S // blk

    def init(n, sink_ref, dya_ref, qr_ref, kc_ref, kp_ref, vc_ref, vp_ref, lse_ref, cos_ref, sin_ref, cosp_ref, sinp_ref,
             dq_ref, dkv_ref, last_ref, dsink_ref, ck, cv):
        @pl.when(n == 0)
        def _():
            ck[...] = jnp.zeros_like(ck)
            cv[...] = jnp.zeros_like(cv)
            dsink_ref[...] = jnp.zeros_like(dsink_ref)

    def body(n, sink_ref, dya_ref, qr_ref, kc_ref, kp_ref, vc_ref, vp_ref, lse_ref, cos_ref, sin_ref, cosp_ref, sinp_ref,
             dq_ref, dkv_ref, last_ref, dsink_ref, ck, cv):
        K = jnp.concatenate([kp_ref[...], kc_ref[...]], axis=0)
        V = jnp.concatenate([vp_ref[...], vc_ref[...]], axis=0)
        qch = [qr_ref[:, c * LANES:(c + 1) * LANES] for c in range(4)]
        dch = [dya_ref[:, c * LANES:(c + 1) * LANES] for c in range(4)]
        lse_tile = lse_ref[...]
        lane8 = lax.broadcasted_iota(jnp.int32, (8, LANES), 1)
        grp = lambda x, h: x[4 * h * blk:4 * (h + 1) * blk]
        qs = jnp.concatenate([_stack_heads(qch, h, BF16) for h in range(2)], axis=0)
        dos = jnp.concatenate([_stack_heads(dch, h, BF16) for h in range(2)], axis=0)
        lse_col = jnp.concatenate([lse_tile[:, j:j + 1] for j in range(N_Q_HEADS)], axis=0)
        s = jnp.concatenate([_dot_nt(grp(qs, h), K) for h in range(2)], axis=0)
        p = jnp.exp(s + _band_bias(n) - lse_col)
        dp = jnp.concatenate([_dot_nt(grp(dos, h), V) for h in range(2)], axis=0)
        delta = jnp.sum(p * dp, axis=1, keepdims=True)
        dsb = (p * (dp - delta)).astype(BF16)
        pb = p.astype(BF16)
        dq = jnp.concatenate([_dot(grp(dsb, h), K) for h in range(2)], axis=0)
        dk_acc = _dot_tn(grp(dsb, 0), grp(qs, 0)) + _dot_tn(grp(dsb, 1), grp(qs, 1))
        dv_acc = _dot_tn(grp(pb, 0), grp(dos, 0)) + _dot_tn(grp(pb, 1), grp(dos, 1))
        dqs = [dq[j * blk:(j + 1) * blk, :] for j in range(N_Q_HEADS)]
        dsink = jnp.zeros((8, LANES), F32)
        for j in range(N_Q_HEADS):
            rows = slice(j * blk, (j + 1) * blk)
            ps_delta = jnp.exp(sink_ref[j] - lse_col[rows]) * delta[rows]
            dsink = jnp.where(lane8 == j, dsink - jnp.sum(ps_delta), dsink)
        dsink_ref[...] += dsink
        cos_c, sin_c = cos_ref[...], sin_ref[...]
        for c in range(4):
            dqc = _from_kv_lanes(dqs[2 * c], dqs[2 * c + 1], c // 2) * (HEAD_DIM ** -0.5)
            dq_ref[:, c * LANES:(c + 1) * LANES] = _rope_t(dqc, cos_c, sin_c).astype(BF16)
        dkv_ref[:, 0:LANES] = _rope_t(dk_acc[0:blk, :] + ck[...], cosp_ref[...], sinp_ref[...]).astype(BF16)
        dkv_ref[:, LANES:2 * LANES] = (dv_acc[0:blk, :] + cv[...]).astype(BF16)
        ck[...] = dk_acc[blk:2 * blk, :]
        cv[...] = dv_acc[blk:2 * blk, :]
        last_ref[:, 0:LANES] = _rope_t(dk_acc[blk:2 * blk, :], cos_c, sin_c).astype(BF16)
        last_ref[:, LANES:2 * LANES] = dv_acc[blk:2 * blk, :].astype(BF16)

    prev = lambda n: jnp.maximum(n - 1, 0)
    same = lambda n: n
    bs = lambda w, f: pl.BlockSpec((blk, w), lambda n: (f(n), 0))
    return dict(
        init=init, body=body, operands=[sinks, dya, qr, kr, kr, vb, vb, lse, cos, sin, cos, sin],
        in_specs=[pl.BlockSpec(memory_space=pltpu.SMEM),
                  bs(512, same), bs(512, same), bs(LANES, same), bs(LANES, prev), bs(LANES, same), bs(LANES, prev),
                  bs(LANES, same), bs(LANES, same), bs(LANES, same), bs(LANES, prev), bs(LANES, prev)],
        out_specs=[bs(512, same), bs(256, prev), _full((blk, 256)), _full((8, LANES))],
        out_shape=[jax.ShapeDtypeStruct((S, 512), BF16), jax.ShapeDtypeStruct((S, 256), BF16),
                   jax.ShapeDtypeStruct((blk, 256), BF16), jax.ShapeDtypeStruct((8, LANES), F32)],
        scratch=[pltpu.VMEM((blk, LANES), F32), pltpu.VMEM((blk, LANES), F32)])


def _split3(x):
    hi = x.astype(BF16)
    r1 = x - hi.astype(F32)
    mid = r1.astype(BF16)
    lo = (r1 - mid.astype(F32)).astype(BF16)
    return hi, mid, lo


def _tri_matmul(tri_b, x):
    hi, mid, lo = _split3(x)
    return _dot(tri_b, hi) + _dot(tri_b, mid) + _dot(tri_b, lo)


def _log_sigmoid(x):
    return jnp.minimum(x, 0.0) - jnp.log(1.0 + jnp.exp(-jnp.abs(x)))


def _shift_rows(cur, seam, k, down):
    L = cur.shape[0]
    row8 = lax.broadcasted_iota(jnp.int32, seam.shape, 0)
    if down:
        mixed = jnp.concatenate([cur[:L - 8], jnp.where(row8 >= 8 - k, seam, cur[L - 8:])], axis=0)
        return pltpu.roll(mixed, k, 0)
    mixed = jnp.concatenate([jnp.where(row8 < k, seam, cur[:8]), cur[8:]], axis=0)
    return pltpu.roll(mixed, L - k, 0)


def _conv_fwd(cur, tail, cw_ref):
    z = cw_ref[4:5, :]
    for k in range(3, 0, -1):
        z = z + _shift_rows(cur, tail, k, True) * cw_ref[3 - k:4 - k, :]
    return z + cur * cw_ref[3:4, :]


def _rowsum_mxu(x, two_pass=False):
    ones = jnp.ones((x.shape[1], LANES), BF16)
    hi = x.astype(BF16)
    s = _dot(hi, ones)
    if two_pass:
        s = s + _dot((x - hi.astype(F32)).astype(BF16), ones)
    return s


def _stack(f):
    return jnp.concatenate([f(h) for h in range(MLSTM_HEADS)], axis=0)


def _head(x, h):
    L = x.shape[0] // MLSTM_HEADS
    return x[h * L:(h + 1) * L]


def _mlstm_heads_fwd(qk, cur_ref, gt, b_all, c_prev, nmv, tri, eye):
    L = qk.shape[0]
    HD = MLSTM_HEAD_DIM
    W4 = MLSTM_HEADS * HD
    col2row = lambda x: jnp.sum(jnp.where(eye, x, 0.0), axis=0, keepdims=True)
    b_col = _stack(lambda h: b_all[:, 4 + h:5 + h])
    i_col = _stack(lambda h: gt[:, h:h + 1])
    b_row = _stack(lambda h: jnp.broadcast_to(col2row(b_all[:, 4 + h:5 + h]), (L, L)))
    i_row = _stack(lambda h: jnp.broadcast_to(col2row(gt[:, h:h + 1]), (L, L)))
    bl = _stack(lambda h: jnp.broadcast_to(b_all[L - 1:L, 4 + h:5 + h], (L, 1)))
    m_prev = _stack(lambda h: jnp.broadcast_to(nmv[4 + h:5 + h, 0:1], (L, 1)))
    n_prev = _stack(lambda h: jnp.broadcast_to(nmv[h:h + 1, :], (L, HD)))
    tri4 = jnp.concatenate([tri] * MLSTM_HEADS, axis=0)
    Dm = jnp.where(tri4, b_col - b_row + i_row, NEG)
    inter = b_col + m_prev
    m_t = jnp.maximum(inter, jnp.max(Dm, axis=1, keepdims=True))
    W = jnp.exp(Dm - m_t)
    e_t = jnp.exp(inter - m_t)
    q = _stack(lambda h: qk[:, h * HD:(h + 1) * HD])
    k = _stack(lambda h: qk[:, W4 + h * HD:W4 + (h + 1) * HD]) * (HD ** -0.5)
    v = _stack(lambda h: cur_ref[:, 2 * W4 + h * HD:2 * W4 + (h + 1) * HD])
    qb, kb, vb = q.astype(BF16), k.astype(BF16), v.astype(BF16)
    Sc = _stack(lambda h: _dot_nt(_head(qb, h), _head(kb, h))) * W
    Scb = Sc.astype(BF16)
    cb = [c.astype(BF16) for c in c_prev]
    P1 = _stack(lambda h: _dot(_head(qb, h), cb[h]))
    num = _stack(lambda h: _dot(_head(Scb, h), _head(vb, h))) + e_t * P1
    qn = _rowsum_mxu(q * n_prev)
    den = _rowsum_mxu(Scb) + e_t * qn
    floor = jnp.broadcast_to(jnp.exp(-m_t), den.shape)
    inv_g = _recip(jnp.maximum(jnp.abs(den), floor))
    hv = num * inv_g
    a_col = bl - b_col + i_col
    a_max = _stack(lambda h: jnp.broadcast_to(jnp.max(_head(a_col, h), axis=0, keepdims=True), (L, 1)))
    m_new = jnp.maximum(bl + m_prev, a_max)
    dec = jnp.exp(bl + m_prev - m_new)
    u_col = jnp.exp(a_col - m_new)
    return dict(W=W, e_t=e_t, q=q, k=k, v=v, qb=qb, kb=kb, vb=vb, cb=cb, Sc=Sc, Scb=Scb, P1=P1, qn=qn, den=den,
                floor=floor, inv_g=inv_g, hv=hv, n_prev=n_prev, m_new=m_new, dec=dec, u_col=u_col)


def _mlstm_fwd_part(pm, pif, cw, sv):
    S = pm.shape[0]
    L = MLSTM_CHUNK
    nc = S // L
    HD = MLSTM_HEAD_DIM
    W4 = MLSTM_HEADS * HD

    def init(c, cur_ref, pif_ref, cw_ref, sv_ref, ym_ref, z_ref, cst_ref, nst_ref, C, nm, tail):
        @pl.when(c == 0)
        def _():
            C[...] = jnp.zeros_like(C)
            nm[...] = jnp.zeros_like(nm)
            tail[...] = jnp.zeros_like(tail)

    def body(c, cur_ref, pif_ref, cw_ref, sv_ref, ym_ref, z_ref, cst_ref, nst_ref, C, nm, tail):
        z = _conv_fwd(cur_ref[:, 0:2 * W4], tail[...], cw_ref)
        tail[...] = cur_ref[L - 8:L, 0:2 * W4]
        z_ref[...] = z
        qk = z * _sigmoid(z)
        gt = pif_ref[...] + sv_ref[1:2, 0:LANES]
        r_i = lax.broadcasted_iota(jnp.int32, (L, L), 0)
        c_i = lax.broadcasted_iota(jnp.int32, (L, L), 1)
        tri = c_i <= r_i
        eye = c_i == r_i
        b_all = _tri_matmul(tri.astype(BF16), _log_sigmoid(gt))
        nmv = nm[...]
        nst_ref[0] = nmv
        c_prev = [C[h] for h in range(MLSTM_HEADS)]
        f = _mlstm_heads_fwd(qk, cur_ref, gt, b_all, c_prev, nmv, tri, eye)
        hv = f["hv"]
        xc = hv - _rowsum_mxu(hv, True) * (1.0 / HD)
        hhat = xc * lax.rsqrt(_rowsum_mxu(xc * xc) * (1.0 / HD) + NORM_EPS)
        so = _sigmoid(_stack(lambda h: cur_ref[:, 3 * W4 + h * HD:3 * W4 + (h + 1) * HD]))
        wn = _stack(lambda h: jnp.broadcast_to(sv_ref[0:1, h * HD:(h + 1) * HD], (L, HD)))
        y = (so * hhat * wn).astype(BF16)
        kw = f["k"] * f["u_col"]
        kwb = kw.astype(BF16)
        n_new, m_new = [], []
        for h in range(MLSTM_HEADS):
            cst_ref[0, h] = c_prev[h]
            ym_ref[:, h * HD:(h + 1) * HD] = _head(y, h)
            dec = f["dec"][h * L:h * L + 1, :]
            C[h] = dec * c_prev[h] + _dot_tn(_head(kwb, h), _head(f["vb"], h))
            n_new.append(dec * nmv[h:h + 1, :] + _colsum(_head(kw, h)))
            m_new.append(jnp.broadcast_to(f["m_new"][h * L:h * L + 1, :], (1, LANES)))
        nm[...] = jnp.concatenate(n_new + m_new, axis=0)

    return dict(
        init=init, body=body, operands=[pm, pif, cw, sv],
        in_specs=[pl.BlockSpec((L, M_W), lambda c: (c, 0)),
                  pl.BlockSpec((L, IF_W), lambda c: (c, 0)), _full(cw.shape), _full(sv.shape)],
        out_specs=[pl.BlockSpec((L, W4), lambda c: (c, 0)), pl.BlockSpec((L, 2 * W4), lambda c: (c, 0)),
                   pl.BlockSpec((1, MLSTM_HEADS, HD, HD), lambda c: (c, 0, 0, 0)),
                   pl.BlockSpec((1, 8, LANES), lambda c: (c, 0, 0))],
        out_shape=[jax.ShapeDtypeStruct((S, W4), BF16), jax.ShapeDtypeStruct((S, 2 * W4), F32),
                   jax.ShapeDtypeStruct((nc, MLSTM_HEADS, HD, HD), F32), jax.ShapeDtypeStruct((nc, 8, LANES), F32)],
        scratch=[pltpu.VMEM((MLSTM_HEADS, HD, HD), F32), pltpu.VMEM((8, LANES), F32), pltpu.VMEM((8, 2 * W4), F32)])


def _mlstm_bwd_part(pm, zc, pif, cw, sv, dym, cst, nst):
    S = pm.shape[0]
    L = MLSTM_CHUNK
    nc = S // L
    HD = MLSTM_HEAD_DIM
    W4 = MLSTM_HEADS * HD

    def init(r, cur_ref, z_ref, pif_ref, cw_ref, sv_ref, dym_ref, cst_ref, nst_ref,
             dm_ref, dif_ref, dcw_ref, dsv_ref, dC, dn, dz_next, dqk):
        @pl.when(r == 0)
        def _():
            dC[...] = jnp.zeros_like(dC)
            dn[...] = jnp.zeros_like(dn)
            dz_next[...] = jnp.zeros_like(dz_next)
            dcw_ref[...] = jnp.zeros_like(dcw_ref)
            dsv_ref[...] = jnp.zeros_like(dsv_ref)

    def body(r, cur_ref, z_ref, pif_ref, cw_ref, sv_ref, dym_ref, cst_ref, nst_ref,
             dm_ref, dif_ref, dcw_ref, dsv_ref, dC, dn, dz_next, dqk):
        z = z_ref[...]
        sgz = _sigmoid(z)
        qk = z * sgz
        gt = pif_ref[...] + sv_ref[1:2, 0:LANES]
        r_i = lax.broadcasted_iota(jnp.int32, (L, L), 0)
        c_i = lax.broadcasted_iota(jnp.int32, (L, L), 1)
        tri = c_i <= r_i
        eye = c_i == r_i
        b_all = _tri_matmul(tri.astype(BF16), _log_sigmoid(gt))
        lane = lax.broadcasted_iota(jnp.int32, (L, LANES), 1)
        rowl = lax.broadcasted_iota(jnp.int32, (L, 1), 0)
        nmv = nst_ref[0]
        heads = range(MLSTM_HEADS)
        c_prev = [cst_ref[0, h] for h in heads]
        f = _mlstm_heads_fwd(qk, cur_ref, gt, b_all, c_prev, nmv, tri, eye)
        hv, inv_g, den, e_t, u_col, n_prev = f["hv"], f["inv_g"], f["den"], f["e_t"], f["u_col"], f["n_prev"]
        q, k, v, qb, kb, vb, Sc, Scb, W = f["q"], f["k"], f["v"], f["qb"], f["kb"], f["vb"], f["Sc"], f["Scb"], f["W"]
        xc = hv - _rowsum_mxu(hv, True) * (1.0 / HD)
        rstd = lax.rsqrt(_rowsum_mxu(xc * xc) * (1.0 / HD) + NORM_EPS)
        hhat = xc * rstd
        wn = _stack(lambda h: jnp.broadcast_to(sv_ref[0:1, h * HD:(h + 1) * HD], (L, HD)))
        so = _sigmoid(_stack(lambda h: cur_ref[:, 3 * W4 + h * HD:3 * W4 + (h + 1) * HD]))
        dy = _stack(lambda h: dym_ref[:, h * HD:(h + 1) * HD])
        d_o = (dy * hhat * wn * (so * (1.0 - so))).astype(BF16)
        dln = dy * so
        dwn = dln * hhat
        dhhat = dln * wn
        m2 = _rowsum_mxu(dhhat * hhat) * (1.0 / HD)
        dh = rstd * (dhhat - _rowsum_mxu(dhhat) * (1.0 / HD) - hhat * m2)
        dnum = dh * inv_g
        active = jnp.abs(den) > f["floor"]
        dden = jnp.where(active, -(HD * NORM_EPS) * m2 * rstd * rstd * inv_g * jnp.where(den >= 0.0, 1.0, -1.0), 0.0)
        dnumb = dnum.astype(BF16)
        dSc = _stack(lambda h: _dot_nt(_head(dnumb, h), _head(vb, h))) + dden
        dA = (dSc * W).astype(BF16)
        G = dSc * Sc
        Gb = G.astype(BF16)
        Gl = (G - Gb.astype(F32)).astype(BF16)
        ones = jnp.ones((L, LANES), BF16)
        Gr = _dot(Gb, ones) + _dot(Gl, ones)
        Gc = _stack(lambda h: _dot_tn(_head(Gb, h), ones) + _dot_tn(_head(Gl, h), ones))
        dCn = [dC[h] for h in heads]
        dCnb = [d.astype(BF16) for d in dCn]
        dnv = dn[...]
        dn_new = _stack(lambda h: jnp.broadcast_to(dnv[h:h + 1, :], (L, HD)))
        kdC = _stack(lambda h: _dot(_head(kb, h), dCnb[h]))
        vdC = _stack(lambda h: _dot_nt(_head(vb, h), dCnb[h]))
        dv = (_stack(lambda h: _dot_tn(_head(Scb, h), _head(dnumb, h))) + u_col * kdC).astype(BF16)
        dq = _stack(lambda h: _dot(_head(dA, h), _head(kb, h))) \
            + e_t * _stack(lambda h: _dot_nt(_head(dnumb, h), f["cb"][h])) + (e_t * dden) * n_prev
        dk = (_stack(lambda h: _dot_tn(_head(dA, h), _head(qb, h))) + u_col * (vdC + dn_new)) * (HD ** -0.5)
        E = (_rowsum_mxu(f["P1"] * dnum, True) + dden * f["qn"]) * e_t
        U = _rowsum_mxu(kdC * v + k * dn_new, True) * u_col
        qe = (q * e_t).astype(BF16)
        qd = (e_t * dden) * q
        di = Gc + U
        db = Gr + E - Gc - U
        di_tile = jnp.zeros((L, LANES), F32)
        db_tile = jnp.zeros((L, LANES), F32)
        dn_rows = []
        for h in heads:
            dec = f["dec"][h * L:h * L + 1, :]
            ddec = jnp.sum(dCn[h] * c_prev[h]) + jnp.sum(dnv[h:h + 1, :] * nmv[h:h + 1, :])
            dbl = ddec * dec + jnp.sum(_head(U, h), axis=0, keepdims=True)
            di_tile = jnp.where(lane == h, _head(di, h), di_tile)
            db_tile = jnp.where(lane == 4 + h, _head(db, h) + jnp.where(rowl == L - 1, dbl, 0.0), db_tile)
            dC[h] = dec * dCn[h] + _dot_tn(_head(qe, h), _head(dnumb, h))
            dn_rows.append(dec * dnv[h:h + 1, :] + _colsum(_head(qd, h)))
            dsv_ref[0:1, h * HD:(h + 1) * HD] += _colsum(_head(dwn, h))
            dqk[:, h * HD:(h + 1) * HD] = _head(dq, h)
            dqk[:, W4 + h * HD:W4 + (h + 1) * HD] = _head(dk, h)
            dm_ref[:, 2 * W4 + h * HD:2 * W4 + (h + 1) * HD] = _head(dv, h)
            dm_ref[:, 3 * W4 + h * HD:3 * W4 + (h + 1) * HD] = _head(d_o, h)
        dn[...] = jnp.concatenate(dn_rows + [jnp.zeros((8 - MLSTM_HEADS, LANES), F32)], axis=0)
        dlf = _tri_matmul((r_i <= c_i).astype(BF16), db_tile)
        dif = jnp.where(lane < 4, di_tile, jnp.where(lane < 8, dlf * (1.0 - _sigmoid(gt)), 0.0))
        dif_ref[...] = dif.astype(BF16)
        dsv_ref[1:2, 0:LANES] += _colsum(dif)
        dz = dqk[...] * (sgz * (1.0 + z * (1.0 - sgz)))
        dcw_ref[4:5, :] += _colsum(dz)
        u = cur_ref[:, 0:2 * W4]
        du_in = dz * cw_ref[3:4, :]
        dcw_ref[3:4, :] += _colsum(dz * u)
        for k in range(1, 4):
            up = _shift_rows(dz, dz_next[...], k, False)
            dcw_ref[3 - k:4 - k, :] += _colsum(up * u)
            du_in = du_in + up * cw_ref[3 - k:4 - k, :]
        dz_next[...] = dz[0:8, :]
        dm_ref[:, 0:2 * W4] = du_in.astype(BF16)

    cidx = lambda r: nc - 1 - r
    return dict(
        init=init, body=body, operands=[pm, zc, pif, cw, sv, dym, cst, nst],
        in_specs=[pl.BlockSpec((L, M_W), lambda r: (cidx(r), 0)), pl.BlockSpec((L, 2 * W4), lambda r: (cidx(r), 0)),
                  pl.BlockSpec((L, IF_W), lambda r: (cidx(r), 0)), _full(cw.shape), _full(sv.shape),
                  pl.BlockSpec((L, W4), lambda r: (cidx(r), 0)),
                  pl.BlockSpec((1, MLSTM_HEADS, HD, HD), lambda r: (cidx(r), 0, 0, 0)),
                  pl.BlockSpec((1, 8, LANES), lambda r: (cidx(r), 0, 0))],
        out_specs=[pl.BlockSpec((L, M_W), lambda r: (cidx(r), 0)), pl.BlockSpec((L, IF_W), lambda r: (cidx(r), 0)),
                   _full((8, 2 * W4)), _full((8, W4))],
        out_shape=[jax.ShapeDtypeStruct((S, M_W), BF16), jax.ShapeDtypeStruct((S, IF_W), BF16),
                   jax.ShapeDtypeStruct((8, 2 * W4), F32), jax.ShapeDtypeStruct((8, W4), F32)],
        scratch=[pltpu.VMEM((MLSTM_HEADS, HD, HD), F32), pltpu.VMEM((8, LANES), F32),
                 pltpu.VMEM((8, 2 * W4), F32), pltpu.VMEM((L, 2 * W4), F32)])


def _rope_tables(positions):
    half = HEAD_DIM // 2
    inv_freq = ROPE_THETA ** (-2.0 * jnp.arange(half, dtype=F32) / HEAD_DIM)
    ang = positions.astype(F32)[:, None] * inv_freq
    cos = jnp.tile(jnp.cos(ang), (1, LANES // half))
    sign = jnp.tile(jnp.concatenate([-jnp.ones((half,), F32), jnp.ones((half,), F32)]), LANES // HEAD_DIM)
    sin = jnp.tile(jnp.sin(ang), (1, LANES // half)) * sign
    return cos, sin


def _local_step(x, tgt, positions, mod, gains, w_cat, w_ba, w_bm, w_out, w_gate, w_up, w_down,
                conv_w, conv_b, b_if, sinks, norm_w):
    t = _tables(mod, gains, conv_w, conv_b, b_if, norm_w, positions)
    a = _mixer_fwd(x, t, sinks, w_cat)
    b = _ffn_part(x, tgt, t, a, w_ba, w_bm, w_out, w_gate, w_up, w_down)
    c = _mixer_bwd(b["dx1"], t, a, b, sinks, w_ba, w_bm, w_out)
    grad_x, acc_p = _pre_bwd(c["dproj"], x, b["dx1"], t["vecs"], w_cat)
    big = dict(w_cat=jnp.concatenate(c["g_w_cat"], axis=1), w_ba=c["g_w_ba"], w_bm=c["g_w_bm"], w_out=c["g_w_out"], w_gate=b["g_w_gate"],
               w_up=b["g_w_up"], w_down=b["g_w_down"])
    return b["loss"], grad_x, big, _small_grads(acc_p, b, c)


def _tables(mod, gains, conv_w, conv_b, b_if, norm_w, positions):
    cos, sin = _rope_tables(positions)
    return dict(
        vecs=jnp.concatenate([mod, gains, jnp.zeros((6, D_MODEL), F32)], axis=0),
        cw=jnp.concatenate([conv_w, conv_b.reshape(1, -1), jnp.zeros((3, 2 * 512), F32)], axis=0),
        sv=jnp.zeros((8, 512), F32).at[0].set(norm_w).at[1, 0:8].set(b_if), cos=cos, sin=sin)


def _mixer_fwd(x, t, sinks, w_cat):
    h, pa, pm, pif, pg = _pre_proj(x, t["vecs"], w_cat)
    n_blk = x.shape[0] // ATTN_BLOCK
    (ya, qr, kr, vb, lse), = _fused_call([_attn_fwd_part(pa, t["cos"], t["sin"], sinks)], "attn_fwd", n_blk)
    (ym, zc, cst, nst), = _fused_call([_mlstm_fwd_part(pm, pif, t["cw"], t["sv"])], "mlstm_fwd", n_blk)
    return dict(h=h, pm=pm, pif=pif, pg=pg, ya=ya, qr=qr, kr=kr, vb=vb, lse=lse, ym=ym, zc=zc, cst=cst, nst=nst)


def _ffn_part(x, tgt, t, a, w_ba, w_bm, w_out, w_gate, w_up, w_down):
    x1, merged, mix, pba, pbm = _mix_fwd(x, a["ya"], a["ym"], a["pg"], t["vecs"], w_ba, w_bm, w_out)
    dx1, h2, hid, da, du, dff, acc_f, loss = _ffn_fwd_bwd(x1, tgt, t["vecs"], w_gate, w_up, w_down)
    return dict(merged=merged, mix=mix, pba=pba, pbm=pbm, dx1=dx1, acc_f=acc_f, loss=loss[0, 0],
                g_w_gate=_matmul_tn(da, h2, 1024, "dw_ffn_gate"),
                g_w_up=_matmul_tn(du, h2, 1024, "dw_ffn_up"),
                g_w_down=_matmul_tn(hid, dff, 1024, "dw_ffn_down"))


def _mixer_bwd(dx1, t, a, b, sinks, w_ba, w_bm, w_out):
    dmix, dpa, dpb, dg, dya, dym, acc_m = _mix_bwd(dx1, b["mix"], b["pba"], b["pbm"], a["pg"], t["vecs"], w_ba, w_bm, w_out)
    g_w_out = _matmul_tn(b["merged"], dmix, 1024, "dw_out")
    g_w_ba = _matmul_tn(a["ya"], dpa, 1024, "dw_branch_attn")
    g_w_bm = _matmul_tn(a["ym"], dpb, 1024, "dw_branch_mlstm")
    n_blk = dx1.shape[0] // ATTN_BLOCK
    (dq, dkv, dkv_last, dsink), = _fused_call(
        [_attn_bwd_part(dya, a["qr"], a["kr"], a["vb"], a["lse"], t["cos"], t["sin"], sinks)], "attn_bwd", n_blk)
    (dm, dif, dcw, dsv), = _fused_call(
        [_mlstm_bwd_part(a["pm"], a["zc"], a["pif"], t["cw"], t["sv"], dym, a["cst"], a["nst"])], "mlstm_bwd", n_blk)
    dkv = lax.dynamic_update_slice(dkv, dkv_last, (dkv.shape[0] - ATTN_BLOCK, 0))
    dproj = [dq, dkv, dm, dif, dg]
    g_w_cat = [_matmul_tn(a["h"], p, min(p.shape[1], 1024), "dw_in_" + n)
               for p, n in zip(dproj, ("q", "kv", "mlstm", "gates", "branch"))]
    return dict(dproj=dproj, g_w_cat=g_w_cat, g_w_out=g_w_out, g_w_ba=g_w_ba,
                g_w_bm=g_w_bm, acc_m=acc_m, dsink=dsink, dcw=dcw, dsv=dsv)


def _small_grads(acc_p, b, c):
    acc_f, acc_m = b["acc_f"], c["acc_m"]
    dmod = jnp.stack([acc_p[1], acc_p[0], acc_m[0], acc_f[3], acc_f[2], acc_f[0]])
    dgains = jnp.stack([acc_p[2], acc_m[1], acc_f[4], acc_f[1]])
    return dict(dmod=dmod, dgains=dgains, dconv_w=c["dcw"][0:4], dconv_b=c["dcw"][4], db_if=c["dsv"][1, 0:8],
                dsinks=c["dsink"][0, 0:8], dnorm_w=c["dsv"][0])


MESH_ID = pl.DeviceIdType.MESH


def _mesh_pos():
    return lax.axis_index("x"), lax.axis_index("y"), lax.axis_index("c")


def _flip(v, bit):
    return 1 - v if bit else v


def _relations():
    return [((r >> 2) & 1, (r >> 1) & 1, r & 1) for r in range(1, N_DEV)]


def _small_exchange(p, gather, name):
    R, V = p.shape[-2:]

    def body(p_ref, out_ref, send_sems, recv_sems):
        x, y, c = _mesh_pos()
        me = 4 * x + 2 * y + c
        out_ref[me] = p_ref[...] if gather else p_ref[me]
        peers = []
        for dx, dy, dc in _relations():
            px, py, pc = _flip(x, dx), _flip(y, dy), _flip(c, dc)
            peers.append(((px, py, pc), 4 * px + 2 * py + pc))

        def copy(k, landing):
            peer, pid = peers[k]
            return pltpu.make_async_remote_copy(
                src_ref=p_ref if gather else p_ref.at[pid], dst_ref=out_ref.at[landing],
                send_sem=send_sems.at[k], recv_sem=recv_sems.at[k], device_id=peer, device_id_type=MESH_ID)

        sends = [copy(k, me) for k in range(N_DEV - 1)]
        for cp in sends:
            cp.start()
        for k in range(N_DEV - 1):
            copy(k, peers[k][1]).wait_recv()
        for cp in sends:
            cp.wait_send()

    vm = pl.BlockSpec(memory_space=pltpu.VMEM)
    return pl.pallas_call(
        body, name=name, in_specs=[vm], out_specs=vm,
        out_shape=jax.ShapeDtypeStruct((N_DEV, R, V), F32),
        scratch_shapes=[pltpu.SemaphoreType.DMA((N_DEV - 1,)), pltpu.SemaphoreType.DMA((N_DEV - 1,))],
        compiler_params=pltpu.CompilerParams(vmem_limit_bytes=VMEM_LIMIT),
    )(p)


HBM_SPEC = pl.BlockSpec(memory_space=pltpu.HBM)
SEM_SPEC = pl.BlockSpec(memory_space=pltpu.SEMAPHORE)


def _peers(x, y, c):
    out = []
    for dx, dy, dc in _relations():
        px, py, pc = _flip(x, dx), _flip(y, dy), _flip(c, dc)
        out.append(((px, py, pc), 4 * px + 2 * py + pc))
    return out


def _exchange_start(arrs, gather, after, name):
    n = len(arrs)
    me_out = 4 * lax.axis_index("x") + 2 * lax.axis_index("y") + lax.axis_index("c")
    lands = []
    for a in arrs:
        own = a[None] if gather else lax.dynamic_index_in_dim(a, me_out, 0, keepdims=True)
        empty = lax.empty(((N_DEV,) + a.shape) if gather else a.shape, a.dtype)
        lands.append(lax.dynamic_update_index_in_dim(empty, own, me_out, 0))

    def body(*refs):
        a_refs, l_refs = refs[:n], refs[n:2 * n]
        send_sems, recv_sems = refs[2 * n + 1], refs[2 * n + 2]
        token = refs[4 * n + 3]
        x, y, c = _mesh_pos()
        me = 4 * x + 2 * y + c
        for a in range(n):
            for k, (peer, pid) in enumerate(_peers(x, y, c)):
                pltpu.make_async_remote_copy(
                    src_ref=a_refs[a] if gather else a_refs[a].at[pid], dst_ref=l_refs[a].at[me],
                    send_sem=send_sems.at[a * (N_DEV - 1) + k], recv_sem=recv_sems.at[a * (N_DEV - 1) + k],
                    device_id=peer, device_id_type=MESH_ID).start()
        token[...] = jnp.zeros_like(token)

    sem = pltpu.SemaphoreType.DMA((n * (N_DEV - 1),))
    hbm = lambda a: pltpu.with_memory_space_constraint(a, pltpu.HBM)
    res = pl.pallas_call(
        body, name=name,
        out_shape=(sem, sem, *[pltpu.HBM(a.shape, a.dtype) for a in arrs], *[pltpu.HBM(l.shape, l.dtype) for l in lands],
                   jax.ShapeDtypeStruct((8, LANES), F32)),
        in_specs=[HBM_SPEC] * (2 * n) + [pl.BlockSpec(memory_space=pl.ANY)],
        out_specs=(SEM_SPEC, SEM_SPEC, *[HBM_SPEC] * (2 * n), pl.BlockSpec(memory_space=pltpu.VMEM)),
        input_output_aliases={i: 2 + i for i in range(2 * n)},
        compiler_params=pltpu.CompilerParams(has_side_effects=pltpu.SideEffectType.DATAFLOW_SIDE_EFFECTING),
    )(*[hbm(a) for a in arrs], *[hbm(l) for l in lands], after)
    return dict(sems=res[0:2], arrs=res[2:2 + n], lands=res[2 + n:2 + 2 * n], token=res[2 + 2 * n], gather=gather)


def _exchange_wait(st, after, name):
    n = len(st["arrs"])
    gather = st["gather"]

    def body(*refs):
        a_refs, l_refs = refs[:n], refs[n:2 * n]
        send_sems, recv_sems = refs[2 * n], refs[2 * n + 1]
        x, y, c = _mesh_pos()
        for a in range(n):
            for k, (peer, pid) in enumerate(_peers(x, y, c)):
                cp = pltpu.make_async_remote_copy(
                    src_ref=a_refs[a] if gather else a_refs[a].at[pid], dst_ref=l_refs[a].at[pid],
                    send_sem=send_sems.at[a * (N_DEV - 1) + k], recv_sem=recv_sems.at[a * (N_DEV - 1) + k],
                    device_id=peer, device_id_type=MESH_ID)
                cp.wait_send()
                cp.wait_recv()

    both = list(st["arrs"]) + list(st["lands"])
    res = pl.pallas_call(
        body, name=name, out_shape=[pltpu.HBM(a.shape, a.dtype) for a in both],
        in_specs=[HBM_SPEC] * (2 * n) + [SEM_SPEC, SEM_SPEC, pl.BlockSpec(memory_space=pl.ANY)],
        out_specs=[HBM_SPEC] * (2 * n), input_output_aliases={i: i for i in range(2 * n)},
        compiler_params=pltpu.CompilerParams(has_side_effects=pltpu.SideEffectType.DATAFLOW_SIDE_EFFECTING),
    )(*both, *st["sems"], after)
    return res[n:2 * n]


def _tie(x, token, name):
    def body(x_ref, t_ref, o_ref):
        o_ref[...] = x_ref[...]

    vm = pl.BlockSpec(memory_space=pltpu.VMEM)
    return pl.pallas_call(
        body, name=name, in_specs=[vm, pl.BlockSpec(memory_space=pl.ANY)], out_specs=vm,
        out_shape=jax.ShapeDtypeStruct(x.shape, x.dtype),
    )(x, token)


def _all_gather_hbm(shards):
    n = len(shards)

    def body(*refs):
        p_refs, out_refs = refs[:n], refs[n:2 * n]
        send_sems, recv_sems, local_sems = refs[2 * n:]
        x, y, c = _mesh_pos()
        me, sibling = (x, y, c), (x, y, 1 - c)
        chips = [(1 - x, y), (x, 1 - y), (1 - x, 1 - y)]

        def copy(a, k, block, to, own=False):
            slot = out_refs[a].at[4 * block[0] + 2 * block[1] + block[2]]
            return pltpu.make_async_remote_copy(
                src_ref=p_refs[a] if own else slot, dst_ref=slot,
                send_sem=send_sems.at[a, k], recv_sem=recv_sems.at[a, k], device_id=to, device_id_type=MESH_ID)

        mine = [pltpu.make_async_copy(p_refs[a], out_refs[a].at[4 * x + 2 * y + c], local_sems.at[a]) for a in range(n)]
        for cp in mine:
            cp.start()
        first = []
        for a in range(n):
            first.append(copy(a, 0, me, sibling, own=True))
            first += [copy(a, 1 + j, me, (*chip, c), own=True) for j, chip in enumerate(chips)]
        for cp in first:
            cp.start()
        passed = []
        for j, chip in enumerate(chips):
            for a in range(n):
                copy(a, 1 + j, (*chip, c), me).wait_recv()
                passed.append(copy(a, 4 + j, (*chip, c), sibling))
                passed[-1].start()
        for a in range(n):
            copy(a, 0, sibling, me).wait_recv()
            for j, chip in enumerate(chips):
                copy(a, 4 + j, (*chip, 1 - c), me).wait_recv()
        for cp in first + passed:
            cp.wait_send()
        for cp in mine:
            cp.wait()

    hbm = pl.BlockSpec(memory_space=pl.ANY)
    return pl.pallas_call(
        body, name="gather_weights", in_specs=[hbm] * n, out_specs=[hbm] * n,
        out_shape=[jax.ShapeDtypeStruct((N_DEV,) + s.shape, s.dtype) for s in shards],
        scratch_shapes=[pltpu.SemaphoreType.DMA((n, N_DEV - 1)), pltpu.SemaphoreType.DMA((n, N_DEV - 1)),
                        pltpu.SemaphoreType.DMA((n,))],
    )(*shards)


def _adamw(w, g, m, v):
    m2 = ADAM_B1 * m + (1.0 - ADAM_B1) * g
    v2 = ADAM_B2 * v + (1.0 - ADAM_B2) * (g * g)
    m_hat = m2 / (1.0 - ADAM_B1 ** ADAM_STEP)
    v_hat = v2 / (1.0 - ADAM_B2 ** ADAM_STEP)
    delta = -ADAM_LR * (m_hat / (jnp.sqrt(v_hat) + ADAM_EPS) + ADAM_WD * w)
    return delta, m2, v2


def _mod_partial(cmat, w_shard, b_shard):
    def body(c_ref, w_ref, b_ref, o_ref):
        o_ref[...] = _dot(c_ref[...].astype(BF16), w_ref[...].astype(BF16)) + b_ref[...]

    return pl.pallas_call(
        body, name="mod_partial", out_shape=jax.ShapeDtypeStruct((N_DEV, w_shard.shape[1]), F32),
        compiler_params=_params(),
    )(cmat, w_shard, b_shard)


def _adamw_w_ada(cmat, dmod_cols, w, m, v):
    r, cdim = w.shape
    tr = _row_tile(r)

    def body(c_ref, d_ref, w_ref, m_ref, v_ref, g_ref, dl_ref, m2_ref, v2_ref):
        g = _dot_tn(c_ref[...].astype(BF16), d_ref[...].astype(BF16))
        g_ref[...] = g
        dl_ref[...], m2_ref[...], v2_ref[...] = _adamw(w_ref[...], g, m_ref[...], v_ref[...])

    row = pl.BlockSpec((tr, cdim), lambda i: (i, 0))
    return pl.pallas_call(
        body, name="adamw_w_ada", grid=(r // tr,),
        in_specs=[pl.BlockSpec((N_DEV, tr), lambda i: (0, i)), _full(dmod_cols.shape), row, row, row],
        out_specs=[row] * 4, out_shape=[jax.ShapeDtypeStruct(w.shape, F32)] * 4,
        compiler_params=_params(("parallel",)),
    )(cmat, dmod_cols, w, m, v)


SMALL_ROWS = 16
SMALL_AT = {"b_ada": (0, 6, 0, D_MODEL), "g_pre_mix": (6, 1, 0, D_MODEL), "g_post_mix": (7, 1, 0, D_MODEL),
            "g_pre_ffn": (8, 1, 0, D_MODEL), "g_post_ffn": (9, 1, 0, D_MODEL), "conv_b": (10, 1, 0, D_MODEL),
            "mlstm_norm_w": (11, 1, 0, 512), "b_if": (11, 1, 512, LANES), "attn_sinks": (11, 1, 640, LANES)}


def _small_table(part):
    tail = jnp.concatenate([part["mlstm_norm_w"], jnp.pad(part["b_if"], (0, LANES - 8)),
                            jnp.pad(part["attn_sinks"], (0, LANES - 8)), jnp.zeros((256,), F32)])
    return jnp.concatenate([part["b_ada"], part["gains"], part["conv_b"][None], tail[None],
                            jnp.zeros((SMALL_ROWS - 12, D_MODEL), F32)], axis=0)


def _adamw_small(gathered, wmv):
    names = list(SMALL_AT)

    def body(*refs):
        g_ref, ins, outs = refs[0], refs[1:1 + 3 * len(names)], refs[1 + 3 * len(names):]
        g = g_ref[0]
        for k in range(1, N_DEV):
            g = g + g_ref[k]
        for i, n in enumerate(names):
            r0, rows, l0, lanes = SMALL_AT[n]
            gi = jnp.concatenate([g[r:r + 1, l0:l0 + lanes] for r in range(r0, r0 + rows)], axis=1)
            w_ref, m_ref, v_ref = ins[3 * i:3 * i + 3]
            go, dl, m2, v2 = outs[4 * i:4 * i + 4]
            go[...] = gi
            dl[...], m2[...], v2[...] = _adamw(w_ref[...], gi, m_ref[...], v_ref[...])

    flat = [a for n in names for a in wmv[n]]
    res = pl.pallas_call(
        body, name="adamw_small",
        out_shape=[jax.ShapeDtypeStruct(wmv[n][0].shape, F32) for n in names for _ in range(4)],
        compiler_params=_params(),
    )(gathered, *flat)
    return {n: res[4 * i:4 * i + 4] for i, n in enumerate(names)}


def _row_tile(rows):
    return rows // 4 if rows >= 512 and rows % 64 == 0 else rows


def _sum_partials(r_ref):
    g = r_ref[0].astype(F32)
    for k in range(1, N_DEV):
        g = g + r_ref[k].astype(F32)
    return g


def _adamw_sum(recv, w, m, v, name):
    r, cdim = w.shape
    tr = _row_tile(r)

    def body(r_ref, w_ref, m_ref, v_ref, g_ref, dl_ref, m2_ref, v2_ref):
        g = _sum_partials(r_ref)
        g_ref[...] = g
        dl_ref[...], m2_ref[...], v2_ref[...] = _adamw(w_ref[...], g, m_ref[...], v_ref[...])

    row = pl.BlockSpec((tr, cdim), lambda i: (i, 0))
    return pl.pallas_call(
        body, name=name, grid=(r // tr,),
        in_specs=[pl.BlockSpec((N_DEV, tr, cdim), lambda i: (0, i, 0)), row, row, row],
        out_specs=[row] * 4, out_shape=[jax.ShapeDtypeStruct((r, cdim), F32)] * 4,
        compiler_params=_params(("parallel",)),
    )(recv, w, m, v)


def _sum8(recv, name):
    _, r, cdim = recv.shape
    tr = _row_tile(r)

    def body(r_ref, g_ref):
        g_ref[...] = _sum_partials(r_ref)

    return pl.pallas_call(
        body, name=name, grid=(r // tr,),
        in_specs=[pl.BlockSpec((N_DEV, tr, cdim), lambda i: (0, i, 0))],
        out_specs=pl.BlockSpec((tr, cdim), lambda i: (i, 0)), out_shape=jax.ShapeDtypeStruct((r, cdim), F32),
        compiler_params=_params(("parallel",)),
    )(recv)


def _adamw_plain(g, w, m, v, name):
    r, cdim = w.shape
    tr = _row_tile(r)

    def body(g_ref, w_ref, m_ref, v_ref, dl_ref, m2_ref, v2_ref):
        dl_ref[...], m2_ref[...], v2_ref[...] = _adamw(w_ref[...], g_ref[...], m_ref[...], v_ref[...])

    row = pl.BlockSpec((tr, cdim), lambda i: (i, 0))
    return pl.pallas_call(
        body, name=name, grid=(r // tr,), in_specs=[row] * 4, out_specs=[row] * 3,
        out_shape=[jax.ShapeDtypeStruct((r, cdim), F32)] * 3,
        compiler_params=_params(("parallel",)),
    )(g, w, m, v)


IN_SHARD = 609
IN_SHARD_PAD = 640
IF_AT = A_W + M_W


def _regrouped(u):
    return u if u < IF_AT + 8 else u + (IF_W - 8)


def _selection(k, rows, row0, transpose):
    shape = (rows, IN_SHARD_PAD) if transpose else (IN_SHARD_PAD, rows)
    l = lax.broadcasted_iota(jnp.int32, shape, 1 if transpose else 0)
    r = lax.broadcasted_iota(jnp.int32, shape, 0 if transpose else 1) + row0
    u = l + IN_SHARD * k
    ru = u + jnp.where(u >= IF_AT + 8, IF_W - 8, 0)
    return ((ru == r) & (l < IN_SHARD)).astype(BF16)


def _regroup_w_in(g):
    def body(g_ref, o_ref):
        for cb in range(CAT_W // LANES):
            r0 = cb * LANES
            acc = jnp.zeros((D_MODEL, LANES), F32)
            for k in range(N_DEV):
                lo, hi = _regrouped(IN_SHARD * k), _regrouped(IN_SHARD * k + IN_SHARD - 1)
                if hi >= r0 and lo < r0 + LANES:
                    acc = acc + _dot(g_ref[k], _selection(k, LANES, r0, False))
            o_ref[:, r0:r0 + LANES] = acc.astype(BF16)

    return pl.pallas_call(
        body, name="regroup_w_in", out_shape=jax.ShapeDtypeStruct((D_MODEL, CAT_W), BF16),
        compiler_params=_params(),
    )(g)


def _ungroup_w_in(g_parts):
    n = len(g_parts)

    def body(*refs):
        o_ref, g_ref = refs[n], refs[n + 1]
        at = 0
        for p in refs[:n]:
            g_ref[:, at:at + p.shape[1]] = p[...]
            at += p.shape[1]
        for k in range(N_DEV):
            lo, hi = _regrouped(IN_SHARD * k), _regrouped(IN_SHARD * k + IN_SHARD - 1)
            w0, w1 = lo // LANES * LANES, (hi // LANES + 1) * LANES
            o_ref[k] = _dot(g_ref[:, w0:w1], _selection(k, w1 - w0, w0, True)).astype(BF16)

    return pl.pallas_call(
        body, name="ungroup_w_in", out_shape=jax.ShapeDtypeStruct((N_DEV, D_MODEL, IN_SHARD_PAD), BF16),
        scratch_shapes=[pltpu.VMEM((D_MODEL, CAT_W), BF16)], compiler_params=_params(),
    )(*g_parts)


WEIGHT_NAMES = ("w_ada", "b_ada", "g_pre_mix", "g_post_mix", "w_in", "b_if", "conv_w", "conv_b", "attn_sinks",
                "mlstm_norm_w", "w_branch_attn", "w_branch_mlstm", "w_out", "g_pre_ffn", "g_post_ffn",
                "w_ffn_gate", "w_ffn_up", "w_ffn_down")


def kernel(x, c, positions, w_ada, b_ada, g_pre_mix, g_post_mix, w_in, b_if, conv_w, conv_b, attn_sinks, mlstm_norm_w, w_branch_attn, w_branch_mlstm, w_out, g_pre_ffn, g_post_ffn, w_ffn_gate, w_ffn_up, w_ffn_down, loss_target, m_w_ada, m_b_ada, m_g_pre_mix, m_g_post_mix, m_w_in, m_b_if, m_conv_w, m_conv_b, m_attn_sinks, m_mlstm_norm_w, m_w_branch_attn, m_w_branch_mlstm, m_w_out, m_g_pre_ffn, m_g_post_ffn, m_w_ffn_gate, m_w_ffn_up, m_w_ffn_down, v_w_ada, v_b_ada, v_g_pre_mix, v_g_post_mix, v_w_in, v_b_if, v_conv_w, v_conv_b, v_attn_sinks, v_mlstm_norm_w, v_w_branch_attn, v_w_branch_mlstm, v_w_out, v_g_pre_ffn, v_g_post_ffn, v_w_ffn_gate, v_w_ffn_up, v_w_ffn_down):
    given = dict(locals())
    W = {n: given[n][0] for n in WEIGHT_NAMES}
    M = {n: given["m_" + n][0] for n in WEIGHT_NAMES}
    V = {n: given["v_" + n][0] for n in WEIGHT_NAMES}
    me = 4 * lax.axis_index("x") + 2 * lax.axis_index("y") + lax.axis_index("c")

    ff_sh = D_FF // N_DEV
    g_in, g_conv, cg = _all_gather_hbm([jnp.pad(W["w_in"], ((0, 0), (0, IN_SHARD_PAD - IN_SHARD))).astype(BF16),
                                        jnp.pad(W["conv_w"], ((0, 4), (0, 0))), c.reshape(8, D_MODEL // 8)])

    cmat = cg.reshape(N_DEV, D_MODEL)
    ada_w = D_MODEL * 6 // N_DEV
    b_cols = lax.dynamic_slice(W["b_ada"], (me * ada_w,), (ada_w,)).reshape(1, ada_w)
    mod_part = _mod_partial(cmat, W["w_ada"], b_cols)
    mod_recv = _small_exchange(jnp.broadcast_to(mod_part[:, None, :], (N_DEV, 8, ada_w)), False, "scatter_mod")
    mod = mod_recv[:, 0, :].reshape(6, D_MODEL)

    st_b = _exchange_start([W["w_branch_attn"].astype(BF16), W["w_branch_mlstm"].astype(BF16), W["w_out"].astype(BF16),
                            W["w_ffn_gate"].T.astype(BF16), W["w_ffn_up"].T.astype(BF16), W["w_ffn_down"].astype(BF16)],
                           True, mod_recv, "gather_rest_start")
    cols = lambda g: g.transpose(1, 0, 2).reshape(g.shape[1], N_DEV * g.shape[2])
    gains = jnp.stack([W["g_pre_mix"], W["g_post_mix"], W["g_pre_ffn"], W["g_post_ffn"]])
    xs, tgt = x[0], loss_target[0]
    t = _tables(mod, gains, cols(g_conv)[0:4], W["conv_b"], W["b_if"], W["mlstm_norm_w"], positions[0])
    vecs = t["vecs"]
    t["vecs"] = _tie(vecs, st_b["token"], "tie_fwd")
    w_cat = _regroup_w_in(g_in)
    a = _mixer_fwd(xs, t, W["attn_sinks"], w_cat)
    g_ba, g_bm, g_out, g_gate, g_up, g_down = _exchange_wait(st_b, a["ym"], "gather_rest_wait")
    w_ba, w_bm, w_out = cols(g_ba), cols(g_bm), g_out.reshape(D_MODEL, D_MODEL)
    b = _ffn_part(xs, tgt, t, a, w_ba, w_bm, w_out, g_gate.reshape(D_FF, D_MODEL), g_up.reshape(D_FF, D_MODEL),
                  g_down.reshape(D_FF, D_MODEL))

    st_f = _exchange_start([b["g_w_gate"].reshape(N_DEV, ff_sh, D_MODEL), b["g_w_up"].reshape(N_DEV, ff_sh, D_MODEL),
                            b["g_w_down"].reshape(N_DEV, ff_sh, D_MODEL)], False, b["dx1"], "scatter_ffn_start")
    t["vecs"] = _tie(vecs, st_f["token"], "tie_bwd")
    cm = _mixer_bwd(b["dx1"], t, a, b, W["attn_sinks"], w_ba, w_bm, w_out)
    pieces = lambda g, n: g.reshape(g.shape[0], N_DEV, n).transpose(1, 0, 2)
    st_m = _exchange_start([_ungroup_w_in(cm["g_w_cat"]), pieces(cm["g_w_ba"], 128), pieces(cm["g_w_bm"], 128),
                            cm["g_w_out"].reshape(N_DEV, D_MODEL // N_DEV, D_MODEL),
                            jnp.pad(pieces(cm["dcw"][0:4], 128), ((0, 0), (0, 4), (0, 0)))], False, cm["dcw"],
                           "scatter_mixer_start")
    r_gate, r_up, r_down = _exchange_wait(st_f, st_m["token"], "scatter_ffn_wait")
    grad_x, acc_p = _pre_bwd(cm["dproj"], xs, b["dx1"], _tie(vecs, st_m["token"], "tie_pre_bwd"), w_cat)
    small = _small_grads(acc_p, b, cm)
    loss = b["loss"]

    big_out = [{} for _ in range(4)]

    def put(n, res):
        for k in range(4):
            big_out[k][n] = res[k][None]

    put("w_ffn_down", _adamw_sum(r_down, W["w_ffn_down"], M["w_ffn_down"], V["w_ffn_down"], "adamw_w_ffn_down"))
    for n, r in (("w_ffn_gate", r_gate), ("w_ffn_up", r_up)):
        put(n, [o.T for o in _adamw_sum(r, W[n].T, M[n].T, V[n].T, "adamw_" + n)])

    sg = _small_exchange(_small_table({"b_ada": small["dmod"], "gains": small["dgains"], "conv_b": small["dconv_b"],
                                       "mlstm_norm_w": small["dnorm_w"], "b_if": small["db_if"],
                                       "attn_sinks": small["dsinks"]}), True, "gather_small")
    as_row = lambda a, n: jnp.pad(a, (0, SMALL_AT[n][3] * SMALL_AT[n][1] - a.shape[0]))[None]
    small_res = _adamw_small(sg, {n: [as_row(d[n], n) for d in (W, M, V)] for n in SMALL_AT})
    small_out = [{n: small_res[n][k][:, 0:W[n].shape[0]] for n in SMALL_AT} for k in range(4)]
    dmod_cols = lax.dynamic_slice(sg[:, 0:6, :].reshape(N_DEV, 6 * D_MODEL), (0, me * ada_w), (N_DEV, ada_w))
    ada_out = _adamw_w_ada(cmat, dmod_cols, W["w_ada"], M["w_ada"], V["w_ada"])

    r_in, r_ba, r_bm, r_out, r_conv = _exchange_wait(st_m, ada_out[1], "scatter_mixer_wait")
    for n, r in (("w_branch_attn", r_ba), ("w_branch_mlstm", r_bm), ("w_out", r_out)):
        put(n, _adamw_sum(r, W[n], M[n], V[n], "adamw_" + n))
    pad4 = lambda v: jnp.pad(v, ((0, 4), (0, 0)))
    put("conv_w", [o[0:4] for o in _adamw_sum(r_conv, pad4(W["conv_w"]), pad4(M["conv_w"]), pad4(V["conv_w"]),
                                                 "adamw_conv_w")])
    g = _sum8(r_in, "sum_w_in")[:, 0:IN_SHARD].T
    put("w_in", [o.T for o in [g] + list(_adamw_plain(g, W["w_in"].T, M["w_in"].T, V["w_in"].T, "adamw_w_in"))])

    total = lax.psum(loss, ("x", "y", "c"))
    outs = [total, grad_x[None]]
    for k in range(4):
        for n in WEIGHT_NAMES:
            if n == "w_ada":
                outs.append(ada_out[k][None])
            elif n in big_out[k]:
                outs.append(big_out[k][n])
            else:
                outs.append(small_out[k][n])
    return tuple(outs)
```

```python
import functools

import jax
import jax.numpy as jnp
import numpy as np
from jax import lax
from jax.experimental import pallas as pl
from jax.experimental.pallas import tpu as pltpu

F32 = jnp.float32
BF16 = jnp.bfloat16

N_DEV = 8
D_MODEL = 1024
D_FF = 2816
N_Q_HEADS = 8
HEAD_DIM = 64
ATTN_BLOCK = 128
ROPE_THETA = 10000.0
MLSTM_HEADS = 4
MLSTM_HEAD_DIM = 128
MLSTM_CHUNK = 128
NORM_EPS = 1e-6
ADAM_LR = 0.001
ADAM_B1 = 0.9
ADAM_B2 = 0.999
ADAM_EPS = 1e-08
ADAM_WD = 0.01
ADAM_STEP = 10

ROW_TILE = 256
WIDE_TILE = 512
LANES = 128
NEG = -1e30
VMEM_LIMIT = 56 * 1024 * 1024

A_W = 768
M_W = 2048
IF_W = 128
G_W = 2048
CAT_W = A_W + M_W + IF_W + G_W

R_SHIFT_M, R_SCALE_M, R_GATE_M, R_SHIFT_F, R_SCALE_F, R_GATE_F = 0, 1, 2, 3, 4, 5
R_G_PRE_MIX, R_G_POST_MIX, R_G_PRE_FFN, R_G_POST_FFN = 6, 7, 8, 9


def _dot(a, b):
    return jnp.dot(a, b, preferred_element_type=F32)


def _dot_nt(a, b):
    return lax.dot_general(a, b, (((1,), (1,)), ((), ())), preferred_element_type=F32)


def _dot_tn(a, b):
    return lax.dot_general(a, b, (((0,), (0,)), ((), ())), preferred_element_type=F32)


def _recip(x):
    return 1.0 / x


def _sigmoid(x):
    return _recip(1.0 + jnp.exp(-x))


def _colsum(x):
    return jnp.sum(x, axis=0, keepdims=True)


def _rowmean(x):
    return jnp.mean(x, axis=-1, keepdims=True)


def _params(sem=None, vmem=VMEM_LIMIT):
    kw = dict(vmem_limit_bytes=vmem)
    if sem is not None:
        kw["dimension_semantics"] = sem
    return pltpu.CompilerParams(**kw)


def _full(shape):
    nd = len(shape)
    return pl.BlockSpec(shape, lambda *_: (0,) * nd)


def _pre_proj(x, vecs, w_cat):
    S = x.shape[0]
    tm = WIDE_TILE

    def body(x_ref, v_ref, w_ref, h_ref, pa_ref, pm_ref, pif_ref, pg_ref):
        xv = x_ref[...]
        r = lax.rsqrt(_rowmean(xv * xv) + NORM_EPS)
        h = (xv * r * v_ref[R_G_PRE_MIX:R_G_PRE_MIX + 1, :]) * (1.0 + v_ref[R_SCALE_M:R_SCALE_M + 1, :]) \
            + v_ref[R_SHIFT_M:R_SHIFT_M + 1, :]
        hb = h.astype(BF16)
        h_ref[...] = hb
        pa_ref[...] = _dot(hb, w_ref[:, 0:A_W])
        pm_ref[...] = _dot(hb, w_ref[:, A_W:A_W + M_W])
        pif_ref[...] = _dot(hb, w_ref[:, A_W + M_W:A_W + M_W + IF_W])
        pg_ref[...] = _dot(hb, w_ref[:, A_W + M_W + IF_W:CAT_W]).astype(BF16)

    row = lambda w: pl.BlockSpec((tm, w), lambda i: (i, 0))
    return pl.pallas_call(
        body, name="pre_proj", grid=(S // tm,),
        in_specs=[row(D_MODEL), _full(vecs.shape), _full(w_cat.shape)],
        out_specs=[row(D_MODEL), row(A_W), row(M_W), row(IF_W), row(G_W)],
        out_shape=[jax.ShapeDtypeStruct((S, D_MODEL), BF16), jax.ShapeDtypeStruct((S, A_W), F32),
                   jax.ShapeDtypeStruct((S, M_W), F32), jax.ShapeDtypeStruct((S, IF_W), F32),
                   jax.ShapeDtypeStruct((S, G_W), BF16)],
        compiler_params=_params(("parallel",)),
    )(x, vecs, w_cat)


def _mix_fwd(x, ya, ym, pg, vecs, w_ba, w_bm, w_out):
    S = x.shape[0]
    tm = WIDE_TILE

    def body(x_ref, ya_ref, ym_ref, pg_ref, v_ref, wba_ref, wbm_ref, wout_ref,
             x1_ref, merged_ref, mix_ref, pa_ref, pb_ref):
        pa = _dot(ya_ref[...], wba_ref[...])
        pb = _dot(ym_ref[...], wbm_ref[...])
        merged = _sigmoid(pg_ref[:, 0:D_MODEL].astype(F32)) * pa + _sigmoid(pg_ref[:, D_MODEL:G_W].astype(F32)) * pb
        mb = merged.astype(BF16)
        mix = _dot(mb, wout_ref[...])
        r = lax.rsqrt(_rowmean(mix * mix) + NORM_EPS)
        x1_ref[...] = x_ref[...] + v_ref[R_GATE_M:R_GATE_M + 1, :] * (mix * r * v_ref[R_G_POST_MIX:R_G_POST_MIX + 1, :])
        merged_ref[...] = mb
        mix_ref[...] = mix
        pa_ref[...] = pa.astype(BF16)
        pb_ref[...] = pb.astype(BF16)

    row = lambda w: pl.BlockSpec((tm, w), lambda i: (i, 0))
    sd = lambda w, dt: jax.ShapeDtypeStruct((S, w), dt)
    return pl.pallas_call(
        body, name="mix_fwd", grid=(S // tm,),
        in_specs=[row(D_MODEL), row(512), row(512), row(G_W), _full(vecs.shape), _full(w_ba.shape),
                  _full(w_bm.shape), _full(w_out.shape)],
        out_specs=[row(D_MODEL)] * 5,
        out_shape=[sd(D_MODEL, F32), sd(D_MODEL, BF16), sd(D_MODEL, F32), sd(D_MODEL, BF16), sd(D_MODEL, BF16)],
        compiler_params=_params(("parallel",)),
    )(x, ya, ym, pg, vecs, w_ba, w_bm, w_out)


def _ffn_fwd_bwd(x1, tgt, vecs, w_gate, w_up, w_down):
    S = x1.shape[0]
    tm = ROW_TILE

    def body(x1_ref, tgt_ref, v_ref, wg_hbm, wu_hbm, wd_hbm,
             dx1_ref, h2_ref, hid_ref, da_ref, du_ref, dff_ref, acc_ref, loss_ref,
             wg, wu, wd, sem):
        i = pl.program_id(0)

        @pl.when(i == 0)
        def _():
            cps = [pltpu.make_async_copy(wg_hbm, wg, sem.at[0]), pltpu.make_async_copy(wu_hbm, wu, sem.at[1]),
                   pltpu.make_async_copy(wd_hbm, wd, sem.at[2])]
            for cp in cps:
                cp.start()
            for cp in cps:
                cp.wait()
            acc_ref[...] = jnp.zeros_like(acc_ref)
            loss_ref[...] = jnp.zeros_like(loss_ref)

        vrow = lambda r: v_ref[r:r + 1, :]
        x1v = x1_ref[...]
        r3 = lax.rsqrt(_rowmean(x1v * x1v) + NORM_EPS)
        x1hat = x1v * r3
        xn3 = x1hat * vrow(R_G_PRE_FFN)
        h2b = (xn3 * (1.0 + vrow(R_SCALE_F)) + vrow(R_SHIFT_F)).astype(BF16)
        h2_ref[...] = h2b
        a = _dot_nt(h2b, wg[...])
        u = _dot_nt(h2b, wu[...])
        sg = _sigmoid(a)
        sil = a * sg
        hidb = (sil * u).astype(BF16)
        hid_ref[...] = hidb
        ff = _dot(hidb, wd[...])
        r4 = lax.rsqrt(_rowmean(ff * ff) + NORM_EPS)
        ffhat = ff * r4
        n4 = ffhat * vrow(R_G_POST_FFN)
        err = x1v + vrow(R_GATE_F) * n4 - tgt_ref[...]
        loss_ref[...] += jnp.sum(err * err) * (0.5 / D_MODEL)
        dy = err * (1.0 / D_MODEL)
        acc_ref[0:1, :] += _colsum(dy * n4)
        dn4 = dy * vrow(R_GATE_F)
        acc_ref[1:2, :] += _colsum(dn4 * ffhat)
        dffhat = dn4 * vrow(R_G_POST_FFN)
        dffb = (r4 * (dffhat - ffhat * _rowmean(dffhat * ffhat))).astype(BF16)
        dff_ref[...] = dffb
        dhid = _dot_nt(dffb, wd[...])
        dub = (dhid * sil).astype(BF16)
        dab = (dhid * u * (sg * (1.0 + a * (1.0 - sg)))).astype(BF16)
        da_ref[...] = dab
        du_ref[...] = dub
        dh2 = _dot(dab, wg[...]) + _dot(dub, wu[...])
        acc_ref[2:3, :] += _colsum(dh2 * xn3)
        acc_ref[3:4, :] += _colsum(dh2)
        dxn3 = dh2 * (1.0 + vrow(R_SCALE_F))
        acc_ref[4:5, :] += _colsum(dxn3 * x1hat)
        dx1hat = dxn3 * vrow(R_G_PRE_FFN)
        dx1_ref[...] = dy + r3 * (dx1hat - x1hat * _rowmean(dx1hat * x1hat))

    row = lambda w: pl.BlockSpec((tm, w), lambda i: (i, 0))
    sd = lambda w, dt: jax.ShapeDtypeStruct((S, w), dt)
    anyspec = pl.BlockSpec(memory_space=pl.ANY)
    return pl.pallas_call(
        body, name="ffn_fwd_bwd", grid=(S // tm,),
        in_specs=[row(D_MODEL), row(D_MODEL), _full(vecs.shape), anyspec, anyspec, anyspec],
        out_specs=[row(D_MODEL), row(D_MODEL), row(D_FF), row(D_FF), row(D_FF), row(D_MODEL),
                   _full((8, D_MODEL)), _full((8, LANES))],
        out_shape=[sd(D_MODEL, F32), sd(D_MODEL, BF16), sd(D_FF, BF16), sd(D_FF, BF16), sd(D_FF, BF16),
                   sd(D_MODEL, BF16), jax.ShapeDtypeStruct((8, D_MODEL), F32), jax.ShapeDtypeStruct((8, LANES), F32)],
        scratch_shapes=[pltpu.VMEM(w_gate.shape, BF16), pltpu.VMEM(w_up.shape, BF16), pltpu.VMEM(w_down.shape, BF16),
                        pltpu.SemaphoreType.DMA((3,))],
        compiler_params=_params(("arbitrary",)),
    )(x1, tgt, vecs, w_gate, w_up, w_down)


def _mix_bwd(dx1, mix, pa, pb, pg, vecs, w_ba, w_bm, w_out):
    S = dx1.shape[0]
    tm = WIDE_TILE

    def body(dx1_ref, mix_ref, pa_ref, pb_ref, pg_ref, v_ref, wba_ref, wbm_ref, wout_ref,
             dmix_ref, dpa_ref, dpb_ref, dg_ref, dya_ref, dym_ref, acc_ref):
        i = pl.program_id(0)

        @pl.when(i == 0)
        def _():
            acc_ref[...] = jnp.zeros_like(acc_ref)

        vrow = lambda r: v_ref[r:r + 1, :]
        dx1v = dx1_ref[...]
        mix = mix_ref[...]
        r2 = lax.rsqrt(_rowmean(mix * mix) + NORM_EPS)
        mixhat = mix * r2
        acc_ref[0:1, :] += _colsum(dx1v * (mixhat * vrow(R_G_POST_MIX)))
        dn2 = dx1v * vrow(R_GATE_M)
        acc_ref[1:2, :] += _colsum(dn2 * mixhat)
        dmixhat = dn2 * vrow(R_G_POST_MIX)
        dmixb = (r2 * (dmixhat - mixhat * _rowmean(dmixhat * mixhat))).astype(BF16)
        dmix_ref[...] = dmixb
        dmerged = _dot_nt(dmixb, wout_ref[...])
        sa = _sigmoid(pg_ref[:, 0:D_MODEL].astype(F32))
        sm = _sigmoid(pg_ref[:, D_MODEL:G_W].astype(F32))
        dpab = (dmerged * sa).astype(BF16)
        dpbb = (dmerged * sm).astype(BF16)
        dpa_ref[...] = dpab
        dpb_ref[...] = dpbb
        dg_ref[:, 0:D_MODEL] = (dmerged * pa_ref[...].astype(F32) * (sa * (1.0 - sa))).astype(BF16)
        dg_ref[:, D_MODEL:G_W] = (dmerged * pb_ref[...].astype(F32) * (sm * (1.0 - sm))).astype(BF16)
        dya_ref[...] = _dot_nt(dpab, wba_ref[...])
        dym_ref[...] = _dot_nt(dpbb, wbm_ref[...])

    row = lambda w: pl.BlockSpec((tm, w), lambda i: (i, 0))
    sd = lambda w, dt: jax.ShapeDtypeStruct((S, w), dt)
    return pl.pallas_call(
        body, name="mix_bwd", grid=(S // tm,),
        in_specs=[row(D_MODEL), row(D_MODEL), row(D_MODEL), row(D_MODEL), row(G_W), _full(vecs.shape),
                  _full(w_ba.shape), _full(w_bm.shape), _full(w_out.shape)],
        out_specs=[row(D_MODEL), row(D_MODEL), row(D_MODEL), row(G_W), row(512), row(512), _full((8, D_MODEL))],
        out_shape=[sd(D_MODEL, BF16), sd(D_MODEL, BF16), sd(D_MODEL, BF16), sd(G_W, BF16), sd(512, F32), sd(512, F32),
                   jax.ShapeDtypeStruct((8, D_MODEL), F32)],
        compiler_params=_params(("arbitrary",)),
    )(dx1, mix, pa, pb, pg, vecs, w_ba, w_bm, w_out)


def _pre_bwd(pieces, x, dx1, vecs, w_cat):
    S = x.shape[0]
    tm = WIDE_TILE
    n = len(pieces)
    starts = [sum(p.shape[1] for p in pieces[:k]) for k in range(n + 1)]

    def body(*refs):
        p_refs = refs[:n]
        x_ref, dx1_ref, v_ref, w_ref, dx_ref, acc_ref = refs[n:]
        i = pl.program_id(0)

        @pl.when(i == 0)
        def _():
            acc_ref[...] = jnp.zeros_like(acc_ref)

        vrow = lambda r: v_ref[r:r + 1, :]
        dh = _dot_nt(p_refs[0][...], w_ref[:, starts[0]:starts[1]])
        for k in range(1, n):
            dh = dh + _dot_nt(p_refs[k][...], w_ref[:, starts[k]:starts[k + 1]])
        xv = x_ref[...]
        r1 = lax.rsqrt(_rowmean(xv * xv) + NORM_EPS)
        xhat = xv * r1
        acc_ref[0:1, :] += _colsum(dh * (xhat * vrow(R_G_PRE_MIX)))
        acc_ref[1:2, :] += _colsum(dh)
        dxn = dh * (1.0 + vrow(R_SCALE_M))
        acc_ref[2:3, :] += _colsum(dxn * xhat)
        dxhat = dxn * vrow(R_G_PRE_MIX)
        dx_ref[...] = dx1_ref[...] + r1 * (dxhat - xhat * _rowmean(dxhat * xhat))

    row = lambda w: pl.BlockSpec((tm, w), lambda i: (i, 0))
    return pl.pallas_call(
        body, name="pre_bwd", grid=(S // tm,),
        in_specs=[row(p.shape[1]) for p in pieces] + [row(D_MODEL), row(D_MODEL), _full(vecs.shape), _full(w_cat.shape)],
        out_specs=[row(D_MODEL), _full((8, D_MODEL))],
        out_shape=[jax.ShapeDtypeStruct((S, D_MODEL), F32), jax.ShapeDtypeStruct((8, D_MODEL), F32)],
        compiler_params=_params(("arbitrary",)),
    )(*pieces, x, dx1, vecs, w_cat)


def _matmul_tn(a, b, tn, name, ts=1024):
    S, K = a.shape
    N = b.shape[1]
    n_s = S // ts

    def body(a_ref, b_ref, o_ref, acc_ref):
        s = pl.program_id(1)

        @pl.when(s == 0)
        def _():
            acc_ref[...] = jnp.zeros_like(acc_ref)

        acc_ref[...] += _dot_tn(a_ref[...], b_ref[...])

        @pl.when(s == n_s - 1)
        def _():
            o_ref[...] = acc_ref[...].astype(BF16)

    return pl.pallas_call(
        body, name=name, grid=(N // tn, n_s),
        in_specs=[pl.BlockSpec((ts, K), lambda j, s: (s, 0)), pl.BlockSpec((ts, tn), lambda j, s: (s, j))],
        out_specs=pl.BlockSpec((K, tn), lambda j, s: (0, j)),
        out_shape=jax.ShapeDtypeStruct((K, N), BF16),
        scratch_shapes=[pltpu.VMEM((K, tn), F32)],
        compiler_params=_params(("parallel", "arbitrary")),
    )(a, b)


def _rope_swap(t):
    lane = lax.broadcasted_iota(jnp.int32, t.shape, 1)
    first = (lane & (HEAD_DIM - 1)) < (HEAD_DIM // 2)
    return jnp.where(first, pltpu.roll(t, LANES - HEAD_DIM // 2, 1), pltpu.roll(t, HEAD_DIM // 2, 1))


def _rope(t, cos, sin_signed):
    return t * cos + _rope_swap(t) * sin_signed


def _rope_t(d, cos, sin_signed):
    return d * cos + _rope_swap(d * sin_signed)


def _to_kv_lanes(chunk, p, h):
    lane = lax.broadcasted_iota(jnp.int32, chunk.shape, 1)
    src = chunk if p == h else pltpu.roll(chunk, HEAD_DIM, 1)
    return jnp.where((lane >> 6) == h, src, jnp.zeros_like(src))


def _from_kv_lanes(o_a, o_b, h):
    lane = lax.broadcasted_iota(jnp.int32, o_a.shape, 1)
    a = o_a if h == 0 else pltpu.roll(o_a, HEAD_DIM, 1)
    b = o_b if h == 1 else pltpu.roll(o_b, HEAD_DIM, 1)
    return jnp.where(lane < HEAD_DIM, a, b)


def _band_bias(n):
    blk = ATTN_BLOCK
    qi = lax.broadcasted_iota(jnp.int32, (blk, 2 * blk), 0)
    kj = lax.broadcasted_iota(jnp.int32, (blk, 2 * blk), 1)
    seen = (kj > qi) & (kj <= qi + blk) & ((n > 0) | (kj >= blk))
    return jnp.concatenate([jnp.where(seen, 0.0, NEG)] * N_Q_HEADS, axis=0)


def _stack_heads(chunks, h, dtype):
    parts = []
    for g in range(4):
        j = 4 * h + g
        parts.append(_to_kv_lanes(chunks[j // 2], j % 2, h))
    return jnp.concatenate(parts, axis=0).astype(dtype)


def _fused_call(parts, name, n_steps):
    counts = [(len(p["in_specs"]), len(p["out_specs"]), len(p["scratch"])) for p in parts]
    n_in, n_out = sum(c[0] for c in counts), sum(c[1] for c in counts)

    def kernel_fn(*refs):
        i = pl.program_id(0)
        groups, a, b, c = [], 0, n_in, n_in + n_out
        for ci, co, cs in counts:
            groups.append(refs[a:a + ci] + refs[b:b + co] + refs[c:c + cs])
            a, b, c = a + ci, b + co, c + cs
        for p, g in zip(parts, groups):
            p["init"](i, *g)
        for p, g in zip(parts, groups):
            p["body"](i, *g)

    flat = lambda key: [v for p in parts for v in p[key]]
    res = pl.pallas_call(
        kernel_fn, name=name, grid=(n_steps,), in_specs=flat("in_specs"), out_specs=flat("out_specs"),
        out_shape=flat("out_shape"), scratch_shapes=flat("scratch"), compiler_params=_params(("arbitrary",)),
    )(*flat("operands"))
    out, pos = [], 0
    for _, co, _ in counts:
        out.append(res[pos:pos + co])
        pos += co
    return out


def _attn_fwd_part(pa, cos, sin, sinks):
    S = pa.shape[0]
    blk = ATTN_BLOCK
    nb = S // blk

    def body(n, sink_ref, cur_ref, prev_ref, cos_ref, sin_ref, cosp_ref, sinp_ref,
             ya_ref, qr_ref, kr_ref, vb_ref, lse_ref):
        cos_c, sin_c = cos_ref[...], sin_ref[...]
        qch = [_rope(cur_ref[:, c * LANES:(c + 1) * LANES], cos_c, sin_c) * (HEAD_DIM ** -0.5) for c in range(4)]
        for c in range(4):
            qr_ref[:, c * LANES:(c + 1) * LANES] = qch[c].astype(BF16)
        k_cur = _rope(cur_ref[:, 512:640], cos_c, sin_c).astype(BF16)
        k_prev = _rope(prev_ref[:, 0:LANES], cosp_ref[...], sinp_ref[...]).astype(BF16)
        v_cur = cur_ref[:, 640:768].astype(BF16)
        v_prev = prev_ref[:, LANES:2 * LANES].astype(BF16)
        kr_ref[...] = k_cur
        vb_ref[...] = v_cur
        K = jnp.concatenate([k_prev, k_cur], axis=0)
        V = jnp.concatenate([v_prev, v_cur], axis=0)
        lane = lax.broadcasted_iota(jnp.int32, (blk, LANES), 1)
        s = jnp.concatenate([_dot_nt(_stack_heads(qch, h, BF16), K) for h in range(2)], axis=0)
        s = s + _band_bias(n)
        rowmax = jnp.max(s, axis=1, keepdims=True)
        hd = lambda x, j: x[j * blk:(j + 1) * blk]
        m = jnp.concatenate([jnp.maximum(hd(rowmax, j), sink_ref[j]) for j in range(N_Q_HEADS)], axis=0)
        p = jnp.exp(s - m)
        pb = p.astype(BF16)
        den = _rowsum_mxu(pb) + jnp.concatenate([jnp.exp(sink_ref[j] - hd(m, j)) for j in range(N_Q_HEADS)], axis=0)
        o = jnp.concatenate([_dot(pb[4 * h * blk:4 * (h + 1) * blk], V) for h in range(2)], axis=0) * _recip(den)
        lse = m + jnp.log(den)
        outs = [o[j * blk:(j + 1) * blk, :] for j in range(N_Q_HEADS)]
        lse_tile = jnp.zeros((blk, LANES), F32)
        for j in range(N_Q_HEADS):
            lse_tile = jnp.where(lane == j, lse[j * blk:(j + 1) * blk, :], lse_tile)
        for c in range(4):
            ya_ref[:, c * LANES:(c + 1) * LANES] = _from_kv_lanes(outs[2 * c], outs[2 * c + 1], c // 2).astype(BF16)
        lse_ref[...] = lse_tile

    prev = lambda n: jnp.maximum(n - 1, 0)
    sd = lambda w, dt: jax.ShapeDtypeStruct((S, w), dt)
    return dict(
        init=lambda n, *refs: None, body=body, scratch=[], operands=[sinks, pa, pa, cos, sin, cos, sin],
        in_specs=[pl.BlockSpec(memory_space=pltpu.SMEM),
                  pl.BlockSpec((blk, A_W), lambda n: (n, 0)),
                  pl.BlockSpec((blk, 256), lambda n: (prev(n), 2)),
                  pl.BlockSpec((blk, LANES), lambda n: (n, 0)), pl.BlockSpec((blk, LANES), lambda n: (n, 0)),
                  pl.BlockSpec((blk, LANES), lambda n: (prev(n), 0)), pl.BlockSpec((blk, LANES), lambda n: (prev(n), 0))],
        out_specs=[pl.BlockSpec((blk, 512), lambda n: (n, 0)), pl.BlockSpec((blk, 512), lambda n: (n, 0)),
                   pl.BlockSpec((blk, LANES), lambda n: (n, 0)), pl.BlockSpec((blk, LANES), lambda n: (n, 0)),
                   pl.BlockSpec((blk, LANES), lambda n: (n, 0))],
        out_shape=[sd(512, BF16), sd(512, BF16), sd(LANES, BF16), sd(LANES, BF16), sd(LANES, F32)])


def _attn_bwd_part(dya, qr, kr, vb, lse, cos, sin, sinks):
    S = dya.shape[0]
    blk = ATTN_BLOCK
    nb = S // blk

    def init(n, sink_ref, dya_ref, qr_ref, kc_ref, kp_ref, vc_ref, vp_ref, lse_ref, cos_ref, sin_ref, cosp_ref, sinp_ref,
             dq_ref, dkv_ref, last_ref, dsink_ref, ck, cv):
        @pl.when(n == 0)
        def _():
            ck[...] = jnp.zeros_like(ck)
            cv[...] = jnp.zeros_like(cv)
            dsink_ref[...] = jnp.zeros_like(dsink_ref)

    def body(n, sink_ref, dya_ref, qr_ref, kc_ref, kp_ref, vc_ref, vp_ref, lse_ref, cos_ref, sin_ref, cosp_ref, sinp_ref,
             dq_ref, dkv_ref, last_ref, dsink_ref, ck, cv):
        K = jnp.concatenate([kp_ref[...], kc_ref[...]], axis=0)
        V = jnp.concatenate([vp_ref[...], vc_ref[...]], axis=0)
        qch = [qr_ref[:, c * LANES:(c + 1) * LANES] for c in range(4)]
        dch = [dya_ref[:, c * LANES:(c + 1) * LANES] for c in range(4)]
        lse_tile = lse_ref[...]
        lane8 = lax.broadcasted_iota(jnp.int32, (8, LANES), 1)
        grp = lambda x, h: x[4 * h * blk:4 * (h + 1) * blk]
        qs = jnp.concatenate([_stack_heads(qch, h, BF16) for h in range(2)], axis=0)
        dos = jnp.concatenate([_stack_heads(dch, h, BF16) for h in range(2)], axis=0)
        lse_col = jnp.concatenate([lse_tile[:, j:j + 1] for j in range(N_Q_HEADS)], axis=0)
        s = jnp.concatenate([_dot_nt(grp(qs, h), K) for h in range(2)], axis=0)
        p = jnp.exp(s + _band_bias(n) - lse_col)
        dp = jnp.concatenate([_dot_nt(grp(dos, h), V) for h in range(2)], axis=0)
        delta = jnp.sum(p * dp, axis=1, keepdims=True)
        dsb = (p * (dp - delta)).astype(BF16)
        pb = p.astype(BF16)
        dq = jnp.concatenate([_dot(grp(dsb, h), K) for h in range(2)], axis=0)
        dk_acc = _dot_tn(grp(dsb, 0), grp(qs, 0)) + _dot_tn(grp(dsb, 1), grp(qs, 1))
        dv_acc = _dot_tn(grp(pb, 0), grp(dos, 0)) + _dot_tn(grp(pb, 1), grp(dos, 1))
        dqs = [dq[j * blk:(j + 1) * blk, :] for j in range(N_Q_HEADS)]
        dsink = jnp.zeros((8, LANES), F32)
        for j in range(N_Q_HEADS):
            rows = slice(j * blk, (j + 1) * blk)
            ps_delta = jnp.exp(sink_ref[j] - lse_col[rows]) * delta[rows]
            dsink = jnp.where(lane8 == j, dsink - jnp.sum(ps_delta), dsink)
        dsink_ref[...] += dsink
        cos_c, sin_c = cos_ref[...], sin_ref[...]
        for c in range(4):
            dqc = _from_kv_lanes(dqs[2 * c], dqs[2 * c + 1], c // 2) * (HEAD_DIM ** -0.5)
            dq_ref[:, c * LANES:(c + 1) * LANES] = _rope_t(dqc, cos_c, sin_c).astype(BF16)
        dkv_ref[:, 0:LANES] = _rope_t(dk_acc[0:blk, :] + ck[...], cosp_ref[...], sinp_ref[...]).astype(BF16)
        dkv_ref[:, LANES:2 * LANES] = (dv_acc[0:blk, :] + cv[...]).astype(BF16)
        ck[...] = dk_acc[blk:2 * blk, :]
        cv[...] = dv_acc[blk:2 * blk, :]
        last_ref[:, 0:LANES] = _rope_t(dk_acc[blk:2 * blk, :], cos_c, sin_c).astype(BF16)
        last_ref[:, LANES:2 * LANES] = dv_acc[blk:2 * blk, :].astype(BF16)

    prev = lambda n: jnp.maximum(n - 1, 0)
    same = lambda n: n
    bs = lambda w, f: pl.BlockSpec((blk, w), lambda n: (f(n), 0))
    return dict(
        init=init, body=body, operands=[sinks, dya, qr, kr, kr, vb, vb, lse, cos, sin, cos, sin],
        in_specs=[pl.BlockSpec(memory_space=pltpu.SMEM),
                  bs(512, same), bs(512, same), bs(LANES, same), bs(LANES, prev), bs(LANES, same), bs(LANES, prev),
                  bs(LANES, same), bs(LANES, same), bs(LANES, same), bs(LANES, prev), bs(LANES, prev)],
        out_specs=[bs(512, same), bs(256, prev), _full((blk, 256)), _full((8, LANES))],
        out_shape=[jax.ShapeDtypeStruct((S, 512), BF16), jax.ShapeDtypeStruct((S, 256), BF16),
                   jax.ShapeDtypeStruct((blk, 256), BF16), jax.ShapeDtypeStruct((8, LANES), F32)],
        scratch=[pltpu.VMEM((blk, LANES), F32), pltpu.VMEM((blk, LANES), F32)])


def _split3(x):
    hi = x.astype(BF16)
    r1 = x - hi.astype(F32)
    mid = r1.astype(BF16)
    lo = (r1 - mid.astype(F32)).astype(BF16)
    return hi, mid, lo


def _tri_matmul(tri_b, x):
    hi, mid, lo = _split3(x)
    return _dot(tri_b, hi) + _dot(tri_b, mid) + _dot(tri_b, lo)


def _log_sigmoid(x):
    return jnp.minimum(x, 0.0) - jnp.log(1.0 + jnp.exp(-jnp.abs(x)))


def _shift_rows(cur, seam, k, down):
    L = cur.shape[0]
    row8 = lax.broadcasted_iota(jnp.int32, seam.shape, 0)
    if down:
        mixed = jnp.concatenate([cur[:L - 8], jnp.where(row8 >= 8 - k, seam, cur[L - 8:])], axis=0)
        return pltpu.roll(mixed, k, 0)
    mixed = jnp.concatenate([jnp.where(row8 < k, seam, cur[:8]), cur[8:]], axis=0)
    return pltpu.roll(mixed, L - k, 0)


def _conv_fwd(cur, tail, cw_ref):
    z = cw_ref[4:5, :]
    for k in range(3, 0, -1):
        z = z + _shift_rows(cur, tail, k, True) * cw_ref[3 - k:4 - k, :]
    return z + cur * cw_ref[3:4, :]


def _rowsum_mxu(x, two_pass=False):
    ones = jnp.ones((x.shape[1], LANES), BF16)
    hi = x.astype(BF16)
    s = _dot(hi, ones)
    if two_pass:
        s = s + _dot((x - hi.astype(F32)).astype(BF16), ones)
    return s


def _stack(f):
    return jnp.concatenate([f(h) for h in range(MLSTM_HEADS)], axis=0)


def _head(x, h):
    L = x.shape[0] // MLSTM_HEADS
    return x[h * L:(h + 1) * L]


def _mlstm_heads_fwd(qk, cur_ref, gt, b_all, c_prev, nmv, tri, eye):
    L = qk.shape[0]
    HD = MLSTM_HEAD_DIM
    W4 = MLSTM_HEADS * HD
    col2row = lambda x: jnp.sum(jnp.where(eye, x, 0.0), axis=0, keepdims=True)
    b_col = _stack(lambda h: b_all[:, 4 + h:5 + h])
    i_col = _stack(lambda h: gt[:, h:h + 1])
    b_row = _stack(lambda h: jnp.broadcast_to(col2row(b_all[:, 4 + h:5 + h]), (L, L)))
    i_row = _stack(lambda h: jnp.broadcast_to(col2row(gt[:, h:h + 1]), (L, L)))
    bl = _stack(lambda h: jnp.broadcast_to(b_all[L - 1:L, 4 + h:5 + h], (L, 1)))
    m_prev = _stack(lambda h: jnp.broadcast_to(nmv[4 + h:5 + h, 0:1], (L, 1)))
    n_prev = _stack(lambda h: jnp.broadcast_to(nmv[h:h + 1, :], (L, HD)))
    tri4 = jnp.concatenate([tri] * MLSTM_HEADS, axis=0)
    Dm = jnp.where(tri4, b_col - b_row + i_row, NEG)
    inter = b_col + m_prev
    m_t = jnp.maximum(inter, jnp.max(Dm, axis=1, keepdims=True))
    W = jnp.exp(Dm - m_t)
    e_t = jnp.exp(inter - m_t)
    q = _stack(lambda h: qk[:, h * HD:(h + 1) * HD])
    k = _stack(lambda h: qk[:, W4 + h * HD:W4 + (h + 1) * HD]) * (HD ** -0.5)
    v = _stack(lambda h: cur_ref[:, 2 * W4 + h * HD:2 * W4 + (h + 1) * HD])
    qb, kb, vb = q.astype(BF16), k.astype(BF16), v.astype(BF16)
    Sc = _stack(lambda h: _dot_nt(_head(qb, h), _head(kb, h))) * W
    Scb = Sc.astype(BF16)
    cb = [c.astype(BF16) for c in c_prev]
    P1 = _stack(lambda h: _dot(_head(qb, h), cb[h]))
    num = _stack(lambda h: _dot(_head(Scb, h), _head(vb, h))) + e_t * P1
    qn = _rowsum_mxu(q * n_prev)
    den = _rowsum_mxu(Scb) + e_t * qn
    floor = jnp.broadcast_to(jnp.exp(-m_t), den.shape)
    inv_g = _recip(jnp.maximum(jnp.abs(den), floor))
    hv = num * inv_g
    a_col = bl - b_col + i_col
    a_max = _stack(lambda h: jnp.broadcast_to(jnp.max(_head(a_col, h), axis=0, keepdims=True), (L, 1)))
    m_new = jnp.maximum(bl + m_prev, a_max)
    dec = jnp.exp(bl + m_prev - m_new)
    u_col = jnp.exp(a_col - m_new)
    return dict(W=W, e_t=e_t, q=q, k=k, v=v, qb=qb, kb=kb, vb=vb, cb=cb, Sc=Sc, Scb=Scb, P1=P1, qn=qn, den=den,
                floor=floor, inv_g=inv_g, hv=hv, n_prev=n_prev, m_new=m_new, dec=dec, u_col=u_col)


def _mlstm_fwd_part(pm, pif, cw, sv):
    S = pm.shape[0]
    L = MLSTM_CHUNK
    nc = S // L
    HD = MLSTM_HEAD_DIM
    W4 = MLSTM_HEADS * HD

    def init(c, cur_ref, pif_ref, cw_ref, sv_ref, ym_ref, z_ref, cst_ref, nst_ref, C, nm, tail):
        @pl.when(c == 0)
        def _():
            C[...] = jnp.zeros_like(C)
            nm[...] = jnp.zeros_like(nm)
            tail[...] = jnp.zeros_like(tail)

    def body(c, cur_ref, pif_ref, cw_ref, sv_ref, ym_ref, z_ref, cst_ref, nst_ref, C, nm, tail):
        z = _conv_fwd(cur_ref[:, 0:2 * W4], tail[...], cw_ref)
        tail[...] = cur_ref[L - 8:L, 0:2 * W4]
        z_ref[...] = z
        qk = z * _sigmoid(z)
        gt = pif_ref[...] + sv_ref[1:2, 0:LANES]
        r_i = lax.broadcasted_iota(jnp.int32, (L, L), 0)
        c_i = lax.broadcasted_iota(jnp.int32, (L, L), 1)
        tri = c_i <= r_i
        eye = c_i == r_i
        b_all = _tri_matmul(tri.astype(BF16), _log_sigmoid(gt))
        nmv = nm[...]
        nst_ref[0] = nmv
        c_prev = [C[h] for h in range(MLSTM_HEADS)]
        f = _mlstm_heads_fwd(qk, cur_ref, gt, b_all, c_prev, nmv, tri, eye)
        hv = f["hv"]
        xc = hv - _rowsum_mxu(hv, True) * (1.0 / HD)
        hhat = xc * lax.rsqrt(_rowsum_mxu(xc * xc) * (1.0 / HD) + NORM_EPS)
        so = _sigmoid(_stack(lambda h: cur_ref[:, 3 * W4 + h * HD:3 * W4 + (h + 1) * HD]))
        wn = _stack(lambda h: jnp.broadcast_to(sv_ref[0:1, h * HD:(h + 1) * HD], (L, HD)))
        y = (so * hhat * wn).astype(BF16)
        kw = f["k"] * f["u_col"]
        kwb = kw.astype(BF16)
        n_new, m_new = [], []
        for h in range(MLSTM_HEADS):
            cst_ref[0, h] = c_prev[h]
            ym_ref[:, h * HD:(h + 1) * HD] = _head(y, h)
            dec = f["dec"][h * L:h * L + 1, :]
            C[h] = dec * c_prev[h] + _dot_tn(_head(kwb, h), _head(f["vb"], h))
            n_new.append(dec * nmv[h:h + 1, :] + _colsum(_head(kw, h)))
            m_new.append(jnp.broadcast_to(f["m_new"][h * L:h * L + 1, :], (1, LANES)))
        nm[...] = jnp.concatenate(n_new + m_new, axis=0)

    return dict(
        init=init, body=body, operands=[pm, pif, cw, sv],
        in_specs=[pl.BlockSpec((L, M_W), lambda c: (c, 0)),
                  pl.BlockSpec((L, IF_W), lambda c: (c, 0)), _full(cw.shape), _full(sv.shape)],
        out_specs=[pl.BlockSpec((L, W4), lambda c: (c, 0)), pl.BlockSpec((L, 2 * W4), lambda c: (c, 0)),
                   pl.BlockSpec((1, MLSTM_HEADS, HD, HD), lambda c: (c, 0, 0, 0)),
                   pl.BlockSpec((1, 8, LANES), lambda c: (c, 0, 0))],
        out_shape=[jax.ShapeDtypeStruct((S, W4), BF16), jax.ShapeDtypeStruct((S, 2 * W4), F32),
                   jax.ShapeDtypeStruct((nc, MLSTM_HEADS, HD, HD), F32), jax.ShapeDtypeStruct((nc, 8, LANES), F32)],
        scratch=[pltpu.VMEM((MLSTM_HEADS, HD, HD), F32), pltpu.VMEM((8, LANES), F32), pltpu.VMEM((8, 2 * W4), F32)])


def _mlstm_bwd_part(pm, zc, pif, cw, sv, dym, cst, nst):
    S = pm.shape[0]
    L = MLSTM_CHUNK
    nc = S // L
    HD = MLSTM_HEAD_DIM
    W4 = MLSTM_HEADS * HD

    def init(r, cur_ref, z_ref, pif_ref, cw_ref, sv_ref, dym_ref, cst_ref, nst_ref,
             dm_ref, dcw_ref, dsv_ref, dC, dn, dz_next, dqk):
        @pl.when(r == 0)
        def _():
            dC[...] = jnp.zeros_like(dC)
            dn[...] = jnp.zeros_like(dn)
            dz_next[...] = jnp.zeros_like(dz_next)
            dcw_ref[...] = jnp.zeros_like(dcw_ref)
            dsv_ref[...] = jnp.zeros_like(dsv_ref)

    def body(r, cur_ref, z_ref, pif_ref, cw_ref, sv_ref, dym_ref, cst_ref, nst_ref,
             dm_ref, dcw_ref, dsv_ref, dC, dn, dz_next, dqk):
        z = z_ref[...]
        sgz = _sigmoid(z)
        qk = z * sgz
        gt = pif_ref[...] + sv_ref[1:2, 0:LANES]
        r_i = lax.broadcasted_iota(jnp.int32, (L, L), 0)
        c_i = lax.broadcasted_iota(jnp.int32, (L, L), 1)
        tri = c_i <= r_i
        eye = c_i == r_i
        b_all = _tri_matmul(tri.astype(BF16), _log_sigmoid(gt))
        lane = lax.broadcasted_iota(jnp.int32, (L, LANES), 1)
        rowl = lax.broadcasted_iota(jnp.int32, (L, 1), 0)
        nmv = nst_ref[0]
        heads = range(MLSTM_HEADS)
        c_prev = [cst_ref[0, h] for h in heads]
        f = _mlstm_heads_fwd(qk, cur_ref, gt, b_all, c_prev, nmv, tri, eye)
        hv, inv_g, den, e_t, u_col, n_prev = f["hv"], f["inv_g"], f["den"], f["e_t"], f["u_col"], f["n_prev"]
        q, k, v, qb, kb, vb, Sc, Scb, W = f["q"], f["k"], f["v"], f["qb"], f["kb"], f["vb"], f["Sc"], f["Scb"], f["W"]
        xc = hv - _rowsum_mxu(hv, True) * (1.0 / HD)
        rstd = lax.rsqrt(_rowsum_mxu(xc * xc) * (1.0 / HD) + NORM_EPS)
        hhat = xc * rstd
        wn = _stack(lambda h: jnp.broadcast_to(sv_ref[0:1, h * HD:(h + 1) * HD], (L, HD)))
        so = _sigmoid(_stack(lambda h: cur_ref[:, 3 * W4 + h * HD:3 * W4 + (h + 1) * HD]))
        dy = _stack(lambda h: dym_ref[:, h * HD:(h + 1) * HD])
        d_o = (dy * hhat * wn * (so * (1.0 - so))).astype(BF16)
        dln = dy * so
        dwn = dln * hhat
        dhhat = dln * wn
        m2 = _rowsum_mxu(dhhat * hhat) * (1.0 / HD)
        dh = rstd * (dhhat - _rowsum_mxu(dhhat) * (1.0 / HD) - hhat * m2)
        dnum = dh * inv_g
        active = jnp.abs(den) > f["floor"]
        dden = jnp.where(active, -(HD * NORM_EPS) * m2 * rstd * rstd * inv_g * jnp.where(den >= 0.0, 1.0, -1.0), 0.0)
        dnumb = dnum.astype(BF16)
        dSc = _stack(lambda h: _dot_nt(_head(dnumb, h), _head(vb, h))) + dden
        dA = (dSc * W).astype(BF16)
        G = dSc * Sc
        Gb = G.astype(BF16)
        Gl = (G - Gb.astype(F32)).astype(BF16)
        ones = jnp.ones((L, LANES), BF16)
        Gr = _dot(Gb, ones) + _dot(Gl, ones)
        Gc = _stack(lambda h: _dot_tn(_head(Gb, h), ones) + _dot_tn(_head(Gl, h), ones))
        dCn = [dC[h] for h in heads]
        dCnb = [d.astype(BF16) for d in dCn]
        dnv = dn[...]
        dn_new = _stack(lambda h: jnp.broadcast_to(dnv[h:h + 1, :], (L, HD)))
        kdC = _stack(lambda h: _dot(_head(kb, h), dCnb[h]))
        vdC = _stack(lambda h: _dot_nt(_head(vb, h), dCnb[h]))
        dv = (_stack(lambda h: _dot_tn(_head(Scb, h), _head(dnumb, h))) + u_col * kdC).astype(BF16)
        dq = _stack(lambda h: _dot(_head(dA, h), _head(kb, h))) \
            + e_t * _stack(lambda h: _dot_nt(_head(dnumb, h), f["cb"][h])) + (e_t * dden) * n_prev
        dk = (_stack(lambda h: _dot_tn(_head(dA, h), _head(qb, h))) + u_col * (vdC + dn_new)) * (HD ** -0.5)
        E = (_rowsum_mxu(f["P1"] * dnum, True) + dden * f["qn"]) * e_t
        U = _rowsum_mxu(kdC * v + k * dn_new, True) * u_col
        qe = (q * e_t).astype(BF16)
        qd = (e_t * dden) * q
        di = Gc + U
        db = Gr + E - Gc - U
        di_tile = jnp.zeros((L, LANES), F32)
        db_tile = jnp.zeros((L, LANES), F32)
        dn_rows = []
        for h in heads:
            dec = f["dec"][h * L:h * L + 1, :]
            ddec = jnp.sum(dCn[h] * c_prev[h]) + jnp.sum(dnv[h:h + 1, :] * nmv[h:h + 1, :])
            dbl = ddec * dec + jnp.sum(_head(U, h), axis=0, keepdims=True)
            di_tile = jnp.where(lane == h, _head(di, h), di_tile)
            db_tile = jnp.where(lane == 4 + h, _head(db, h) + jnp.where(rowl == L - 1, dbl, 0.0), db_tile)
            dC[h] = dec * dCn[h] + _dot_tn(_head(qe, h), _head(dnumb, h))
            dn_rows.append(dec * dnv[h:h + 1, :] + _colsum(_head(qd, h)))
            dsv_ref[0:1, h * HD:(h + 1) * HD] += _colsum(_head(dwn, h))
            dqk[:, h * HD:(h + 1) * HD] = _head(dq, h)
            dqk[:, W4 + h * HD:W4 + (h + 1) * HD] = _head(dk, h)
            dm_ref[:, 2 * W4 + h * HD:2 * W4 + (h + 1) * HD] = _head(dv, h)
            dm_ref[:, 3 * W4 + h * HD:3 * W4 + (h + 1) * HD] = _head(d_o, h)
        dn[...] = jnp.concatenate(dn_rows + [jnp.zeros((8 - MLSTM_HEADS, LANES), F32)], axis=0)
        dlf = _tri_matmul((r_i <= c_i).astype(BF16), db_tile)
        dif = jnp.where(lane < 4, di_tile, jnp.where(lane < 8, dlf * (1.0 - _sigmoid(gt)), 0.0))
        dm_ref[:, M_W:M_W + IF_W] = dif.astype(BF16)
        dsv_ref[1:2, 0:LANES] += _colsum(dif)
        dz = dqk[...] * (sgz * (1.0 + z * (1.0 - sgz)))
        dcw_ref[4:5, :] += _colsum(dz)
        u = cur_ref[:, 0:2 * W4]
        du_in = dz * cw_ref[3:4, :]
        dcw_ref[3:4, :] += _colsum(dz * u)
        for k in range(1, 4):
            up = _shift_rows(dz, dz_next[...], k, False)
            dcw_ref[3 - k:4 - k, :] += _colsum(up * u)
            du_in = du_in + up * cw_ref[3 - k:4 - k, :]
        dz_next[...] = dz[0:8, :]
        dm_ref[:, 0:2 * W4] = du_in.astype(BF16)

    cidx = lambda r: nc - 1 - r
    return dict(
        init=init, body=body, operands=[pm, zc, pif, cw, sv, dym, cst, nst],
        in_specs=[pl.BlockSpec((L, M_W), lambda r: (cidx(r), 0)), pl.BlockSpec((L, 2 * W4), lambda r: (cidx(r), 0)),
                  pl.BlockSpec((L, IF_W), lambda r: (cidx(r), 0)), _full(cw.shape), _full(sv.shape),
                  pl.BlockSpec((L, W4), lambda r: (cidx(r), 0)),
                  pl.BlockSpec((1, MLSTM_HEADS, HD, HD), lambda r: (cidx(r), 0, 0, 0)),
                  pl.BlockSpec((1, 8, LANES), lambda r: (cidx(r), 0, 0))],
        out_specs=[pl.BlockSpec((L, M_W + IF_W), lambda r: (cidx(r), 0)), _full((8, 2 * W4)), _full((8, W4))],
        out_shape=[jax.ShapeDtypeStruct((S, M_W + IF_W), BF16),
                   jax.ShapeDtypeStruct((8, 2 * W4), F32), jax.ShapeDtypeStruct((8, W4), F32)],
        scratch=[pltpu.VMEM((MLSTM_HEADS, HD, HD), F32), pltpu.VMEM((8, LANES), F32),
                 pltpu.VMEM((8, 2 * W4), F32), pltpu.VMEM((L, 2 * W4), F32)])


def _rope_tables(positions):
    half = HEAD_DIM // 2
    inv_freq = ROPE_THETA ** (-2.0 * jnp.arange(half, dtype=F32) / HEAD_DIM)
    ang = positions.astype(F32)[:, None] * inv_freq
    cos = jnp.tile(jnp.cos(ang), (1, LANES // half))
    sign = jnp.tile(jnp.concatenate([-jnp.ones((half,), F32), jnp.ones((half,), F32)]), LANES // HEAD_DIM)
    sin = jnp.tile(jnp.sin(ang), (1, LANES // half)) * sign
    return cos, sin


def _local_step(x, tgt, positions, mod, gains, w_cat, w_ba, w_bm, w_out, w_gate, w_up, w_down,
                conv_w, conv_b, b_if, sinks, norm_w):
    t = _tables(mod, gains, conv_w, conv_b, b_if, norm_w, positions)
    a = _mixer_fwd(x, t, sinks, w_cat)
    b = _ffn_part(x, tgt, t, a, w_ba, w_bm, w_out, w_gate, w_up, w_down)
    c = _mixer_bwd(b["dx1"], t, a, b, sinks, w_ba, w_bm, w_out)
    grad_x, acc_p = _pre_bwd(c["dproj"], x, b["dx1"], t["vecs"], w_cat)
    big = dict(w_cat=jnp.concatenate(c["g_w_cat"], axis=1), w_ba=c["g_w_ba"], w_bm=c["g_w_bm"], w_out=c["g_w_out"], w_gate=b["g_w_gate"],
               w_up=b["g_w_up"], w_down=b["g_w_down"])
    return b["loss"], grad_x, big, _small_grads(acc_p, b, c)


def _tables(mod, gains, conv_w, conv_b, b_if, norm_w, positions):
    cos, sin = _rope_tables(positions)
    return dict(
        vecs=jnp.concatenate([mod, gains, jnp.zeros((6, D_MODEL), F32)], axis=0),
        cw=jnp.concatenate([conv_w, conv_b.reshape(1, -1), jnp.zeros((3, 2 * 512), F32)], axis=0),
        sv=jnp.zeros((8, 512), F32).at[0].set(norm_w).at[1, 0:8].set(b_if), cos=cos, sin=sin)


def _mixer_fwd(x, t, sinks, w_cat):
    h, pa, pm, pif, pg = _pre_proj(x, t["vecs"], w_cat)
    n_blk = x.shape[0] // ATTN_BLOCK
    (ya, qr, kr, vb, lse), = _fused_call([_attn_fwd_part(pa, t["cos"], t["sin"], sinks)], "attn_fwd", n_blk)
    (ym, zc, cst, nst), = _fused_call([_mlstm_fwd_part(pm, pif, t["cw"], t["sv"])], "mlstm_fwd", n_blk)
    return dict(h=h, pm=pm, pif=pif, pg=pg, ya=ya, qr=qr, kr=kr, vb=vb, lse=lse, ym=ym, zc=zc, cst=cst, nst=nst)


def _ffn_part(x, tgt, t, a, w_ba, w_bm, w_out, w_gate, w_up, w_down):
    x1, merged, mix, pba, pbm = _mix_fwd(x, a["ya"], a["ym"], a["pg"], t["vecs"], w_ba, w_bm, w_out)
    dx1, h2, hid, da, du, dff, acc_f, loss = _ffn_fwd_bwd(x1, tgt, t["vecs"], w_gate, w_up, w_down)
    return dict(merged=merged, mix=mix, pba=pba, pbm=pbm, dx1=dx1, acc_f=acc_f, loss=loss[0, 0],
                g_w_gate=_matmul_tn(da, h2, 1024, "dw_ffn_gate"),
                g_w_up=_matmul_tn(du, h2, 1024, "dw_ffn_up"),
                g_w_down=_matmul_tn(hid, dff, 1024, "dw_ffn_down"))


def _mixer_bwd(dx1, t, a, b, sinks, w_ba, w_bm, w_out):
    dmix, dpa, dpb, dg, dya, dym, acc_m = _mix_bwd(dx1, b["mix"], b["pba"], b["pbm"], a["pg"], t["vecs"], w_ba, w_bm, w_out)
    g_w_out = _matmul_tn(b["merged"], dmix, 1024, "dw_out")
    g_w_ba = _matmul_tn(a["ya"], dpa, 1024, "dw_branch_attn")
    g_w_bm = _matmul_tn(a["ym"], dpb, 1024, "dw_branch_mlstm")
    n_blk = dx1.shape[0] // ATTN_BLOCK
    (dq, dkv, dkv_last, dsink), = _fused_call(
        [_attn_bwd_part(dya, a["qr"], a["kr"], a["vb"], a["lse"], t["cos"], t["sin"], sinks)], "attn_bwd", n_blk)
    (dm, dcw, dsv), = _fused_call(
        [_mlstm_bwd_part(a["pm"], a["zc"], a["pif"], t["cw"], t["sv"], dym, a["cst"], a["nst"])], "mlstm_bwd", n_blk)
    dkv = lax.dynamic_update_slice(dkv, dkv_last, (dkv.shape[0] - ATTN_BLOCK, 0))
    dproj = [dq, dkv, dm, dg]
    g_w_cat = [_matmul_tn(a["h"], p, 1024 if p.shape[1] % 1024 == 0 else p.shape[1], "dw_in_" + n)
               for p, n in zip(dproj, ("q", "kv", "mlstm", "branch"))]
    return dict(dproj=dproj, g_w_cat=g_w_cat, g_w_out=g_w_out, g_w_ba=g_w_ba,
                g_w_bm=g_w_bm, acc_m=acc_m, dsink=dsink, dcw=dcw, dsv=dsv)


def _small_grads(acc_p, b, c):
    acc_f, acc_m = b["acc_f"], c["acc_m"]
    dmod = jnp.stack([acc_p[1], acc_p[0], acc_m[0], acc_f[3], acc_f[2], acc_f[0]])
    dgains = jnp.stack([acc_p[2], acc_m[1], acc_f[4], acc_f[1]])
    return dict(dmod=dmod, dgains=dgains, dconv_w=c["dcw"][0:4], dconv_b=c["dcw"][4], db_if=c["dsv"][1, 0:8],
                dsinks=c["dsink"][0, 0:8], dnorm_w=c["dsv"][0])


MESH_ID = pl.DeviceIdType.MESH


def _mesh_pos():
    return lax.axis_index("x"), lax.axis_index("y"), lax.axis_index("c")


def _flip(v, bit):
    return 1 - v if bit else v


def _relations():
    return [((r >> 2) & 1, (r >> 1) & 1, r & 1) for r in range(1, N_DEV)]


def _small_exchange(p, gather, name):
    R, V = p.shape[-2:]

    def body(p_ref, out_ref, send_sems, recv_sems):
        x, y, c = _mesh_pos()
        me = 4 * x + 2 * y + c
        out_ref[me] = p_ref[...] if gather else p_ref[me]
        peers = []
        for dx, dy, dc in _relations():
            px, py, pc = _flip(x, dx), _flip(y, dy), _flip(c, dc)
            peers.append(((px, py, pc), 4 * px + 2 * py + pc))

        def copy(k, landing):
            peer, pid = peers[k]
            return pltpu.make_async_remote_copy(
                src_ref=p_ref if gather else p_ref.at[pid], dst_ref=out_ref.at[landing],
                send_sem=send_sems.at[k], recv_sem=recv_sems.at[k], device_id=peer, device_id_type=MESH_ID)

        sends = [copy(k, me) for k in range(N_DEV - 1)]
        for cp in sends:
            cp.start()
        for k in range(N_DEV - 1):
            copy(k, peers[k][1]).wait_recv()
        for cp in sends:
            cp.wait_send()

    vm = pl.BlockSpec(memory_space=pltpu.VMEM)
    return pl.pallas_call(
        body, name=name, in_specs=[vm], out_specs=vm,
        out_shape=jax.ShapeDtypeStruct((N_DEV, R, V), F32),
        scratch_shapes=[pltpu.SemaphoreType.DMA((N_DEV - 1,)), pltpu.SemaphoreType.DMA((N_DEV - 1,))],
        compiler_params=pltpu.CompilerParams(vmem_limit_bytes=VMEM_LIMIT),
    )(p)


HBM_SPEC = pl.BlockSpec(memory_space=pltpu.HBM)
SEM_SPEC = pl.BlockSpec(memory_space=pltpu.SEMAPHORE)


def _peers(x, y, c):
    out = []
    for dx, dy, dc in _relations():
        px, py, pc = _flip(x, dx), _flip(y, dy), _flip(c, dc)
        out.append(((px, py, pc), 4 * px + 2 * py + pc))
    return out


def _exchange_start(arrs, gather, after, name):
    n = len(arrs)
    me_out = 4 * lax.axis_index("x") + 2 * lax.axis_index("y") + lax.axis_index("c")
    lands = []
    for a in arrs:
        own = a[None] if gather else lax.dynamic_index_in_dim(a, me_out, 0, keepdims=True)
        empty = lax.empty(((N_DEV,) + a.shape) if gather else a.shape, a.dtype)
        lands.append(lax.dynamic_update_index_in_dim(empty, own, me_out, 0))

    def body(*refs):
        a_refs, l_refs = refs[:n], refs[n:2 * n]
        send_sems, recv_sems = refs[2 * n + 1], refs[2 * n + 2]
        token = refs[4 * n + 3]
        x, y, c = _mesh_pos()
        me = 4 * x + 2 * y + c
        for a in range(n):
            for k, (peer, pid) in enumerate(_peers(x, y, c)):
                pltpu.make_async_remote_copy(
                    src_ref=a_refs[a] if gather else a_refs[a].at[pid], dst_ref=l_refs[a].at[me],
                    send_sem=send_sems.at[a * (N_DEV - 1) + k], recv_sem=recv_sems.at[a * (N_DEV - 1) + k],
                    device_id=peer, device_id_type=MESH_ID).start()
        token[...] = jnp.zeros_like(token)

    sem = pltpu.SemaphoreType.DMA((n * (N_DEV - 1),))
    hbm = lambda a: pltpu.with_memory_space_constraint(a, pltpu.HBM)
    res = pl.pallas_call(
        body, name=name,
        out_shape=(sem, sem, *[pltpu.HBM(a.shape, a.dtype) for a in arrs], *[pltpu.HBM(l.shape, l.dtype) for l in lands],
                   jax.ShapeDtypeStruct((8, LANES), F32)),
        in_specs=[HBM_SPEC] * (2 * n) + [pl.BlockSpec(memory_space=pl.ANY)],
        out_specs=(SEM_SPEC, SEM_SPEC, *[HBM_SPEC] * (2 * n), pl.BlockSpec(memory_space=pltpu.VMEM)),
        input_output_aliases={i: 2 + i for i in range(2 * n)},
        compiler_params=pltpu.CompilerParams(has_side_effects=pltpu.SideEffectType.DATAFLOW_SIDE_EFFECTING),
    )(*[hbm(a) for a in arrs], *[hbm(l) for l in lands], after)
    return dict(sems=res[0:2], arrs=res[2:2 + n], lands=res[2 + n:2 + 2 * n], token=res[2 + 2 * n], gather=gather)


def _exchange_wait(st, after, name):
    n = len(st["arrs"])
    gather = st["gather"]

    def body(*refs):
        a_refs, l_refs = refs[:n], refs[n:2 * n]
        send_sems, recv_sems = refs[2 * n], refs[2 * n + 1]
        x, y, c = _mesh_pos()
        for a in range(n):
            for k, (peer, pid) in enumerate(_peers(x, y, c)):
                cp = pltpu.make_async_remote_copy(
                    src_ref=a_refs[a] if gather else a_refs[a].at[pid], dst_ref=l_refs[a].at[pid],
                    send_sem=send_sems.at[a * (N_DEV - 1) + k], recv_sem=recv_sems.at[a * (N_DEV - 1) + k],
                    device_id=peer, device_id_type=MESH_ID)
                cp.wait_send()
                cp.wait_recv()

    both = list(st["arrs"]) + list(st["lands"])
    res = pl.pallas_call(
        body, name=name, out_shape=[pltpu.HBM(a.shape, a.dtype) for a in both],
        in_specs=[HBM_SPEC] * (2 * n) + [SEM_SPEC, SEM_SPEC, pl.BlockSpec(memory_space=pl.ANY)],
        out_specs=[HBM_SPEC] * (2 * n), input_output_aliases={i: i for i in range(2 * n)},
        compiler_params=pltpu.CompilerParams(has_side_effects=pltpu.SideEffectType.DATAFLOW_SIDE_EFFECTING),
    )(*both, *st["sems"], after)
    return res[n:2 * n]


def _tie(x, token, name):
    def body(x_ref, t_ref, o_ref):
        o_ref[...] = x_ref[...]

    vm = pl.BlockSpec(memory_space=pltpu.VMEM)
    return pl.pallas_call(
        body, name=name, in_specs=[vm, pl.BlockSpec(memory_space=pl.ANY)], out_specs=vm,
        out_shape=jax.ShapeDtypeStruct(x.shape, x.dtype),
    )(x, token)


def _all_gather_hbm(shards):
    n = len(shards)

    def body(*refs):
        p_refs, out_refs = refs[:n], refs[n:2 * n]
        send_sems, recv_sems, local_sems = refs[2 * n:]
        x, y, c = _mesh_pos()
        me, sibling = (x, y, c), (x, y, 1 - c)
        chips = [(1 - x, y), (x, 1 - y), (1 - x, 1 - y)]

        def copy(a, k, block, to, own=False):
            slot = out_refs[a].at[4 * block[0] + 2 * block[1] + block[2]]
            return pltpu.make_async_remote_copy(
                src_ref=p_refs[a] if own else slot, dst_ref=slot,
                send_sem=send_sems.at[a, k], recv_sem=recv_sems.at[a, k], device_id=to, device_id_type=MESH_ID)

        mine = [pltpu.make_async_copy(p_refs[a], out_refs[a].at[4 * x + 2 * y + c], local_sems.at[a]) for a in range(n)]
        for cp in mine:
            cp.start()
        first = []
        for a in range(n):
            first.append(copy(a, 0, me, sibling, own=True))
            first += [copy(a, 1 + j, me, (*chip, c), own=True) for j, chip in enumerate(chips)]
        for cp in first:
            cp.start()
        passed = []
        for j, chip in enumerate(chips):
            for a in range(n):
                copy(a, 1 + j, (*chip, c), me).wait_recv()
                passed.append(copy(a, 4 + j, (*chip, c), sibling))
                passed[-1].start()
        for a in range(n):
            copy(a, 0, sibling, me).wait_recv()
            for j, chip in enumerate(chips):
                copy(a, 4 + j, (*chip, 1 - c), me).wait_recv()
        for cp in first + passed:
            cp.wait_send()
        for cp in mine:
            cp.wait()

    hbm = pl.BlockSpec(memory_space=pl.ANY)
    return pl.pallas_call(
        body, name="gather_weights", in_specs=[hbm] * n, out_specs=[hbm] * n,
        out_shape=[jax.ShapeDtypeStruct((N_DEV,) + s.shape, s.dtype) for s in shards],
        scratch_shapes=[pltpu.SemaphoreType.DMA((n, N_DEV - 1)), pltpu.SemaphoreType.DMA((n, N_DEV - 1)),
                        pltpu.SemaphoreType.DMA((n,))],
    )(*shards)


def _adamw(w, g, m, v):
    m2 = ADAM_B1 * m + (1.0 - ADAM_B1) * g
    v2 = ADAM_B2 * v + (1.0 - ADAM_B2) * (g * g)
    m_hat = m2 / (1.0 - ADAM_B1 ** ADAM_STEP)
    v_hat = v2 / (1.0 - ADAM_B2 ** ADAM_STEP)
    delta = -ADAM_LR * (m_hat / (jnp.sqrt(v_hat) + ADAM_EPS) + ADAM_WD * w)
    return delta, m2, v2


def _mod_partial(cmat, w_shard, b_shard):
    def body(c_ref, w_ref, b_ref, o_ref):
        o_ref[...] = _dot(c_ref[...].astype(BF16), w_ref[...].astype(BF16)) + b_ref[...]

    return pl.pallas_call(
        body, name="mod_partial", out_shape=jax.ShapeDtypeStruct((N_DEV, w_shard.shape[1]), F32),
        compiler_params=_params(),
    )(cmat, w_shard, b_shard)


def _adamw_w_ada(cmat, dmod_cols, w, m, v):
    r, cdim = w.shape
    tr = _row_tile(r)

    def body(c_ref, d_ref, w_ref, m_ref, v_ref, g_ref, dl_ref, m2_ref, v2_ref):
        g = _dot_tn(c_ref[...].astype(BF16), d_ref[...].astype(BF16))
        g_ref[...] = g
        dl_ref[...], m2_ref[...], v2_ref[...] = _adamw(w_ref[...], g, m_ref[...], v_ref[...])

    row = pl.BlockSpec((tr, cdim), lambda i: (i, 0))
    return pl.pallas_call(
        body, name="adamw_w_ada", grid=(r // tr,),
        in_specs=[pl.BlockSpec((N_DEV, tr), lambda i: (0, i)), _full(dmod_cols.shape), row, row, row],
        out_specs=[row] * 4, out_shape=[jax.ShapeDtypeStruct(w.shape, F32)] * 4,
        compiler_params=_params(("parallel",)),
    )(cmat, dmod_cols, w, m, v)


SMALL_ROWS = 16
SMALL_AT = {"b_ada": (0, 6, 0, D_MODEL), "g_pre_mix": (6, 1, 0, D_MODEL), "g_post_mix": (7, 1, 0, D_MODEL),
            "g_pre_ffn": (8, 1, 0, D_MODEL), "g_post_ffn": (9, 1, 0, D_MODEL), "conv_b": (10, 1, 0, D_MODEL),
            "mlstm_norm_w": (11, 1, 0, 512), "b_if": (11, 1, 512, LANES), "attn_sinks": (11, 1, 640, LANES)}


def _small_table(part):
    tail = jnp.concatenate([part["mlstm_norm_w"], jnp.pad(part["b_if"], (0, LANES - 8)),
                            jnp.pad(part["attn_sinks"], (0, LANES - 8)), jnp.zeros((256,), F32)])
    return jnp.concatenate([part["b_ada"], part["gains"], part["conv_b"][None], tail[None],
                            jnp.zeros((SMALL_ROWS - 12, D_MODEL), F32)], axis=0)


def _adamw_small(gathered, wmv):
    names = list(SMALL_AT)

    def body(*refs):
        g_ref, ins, outs = refs[0], refs[1:1 + 3 * len(names)], refs[1 + 3 * len(names):]
        g = g_ref[0]
        for k in range(1, N_DEV):
            g = g + g_ref[k]
        for i, n in enumerate(names):
            r0, rows, l0, lanes = SMALL_AT[n]
            gi = jnp.concatenate([g[r:r + 1, l0:l0 + lanes] for r in range(r0, r0 + rows)], axis=1)
            w_ref, m_ref, v_ref = ins[3 * i:3 * i + 3]
            go, dl, m2, v2 = outs[4 * i:4 * i + 4]
            go[...] = gi
            dl[...], m2[...], v2[...] = _adamw(w_ref[...], gi, m_ref[...], v_ref[...])

    flat = [a for n in names for a in wmv[n]]
    res = pl.pallas_call(
        body, name="adamw_small",
        out_shape=[jax.ShapeDtypeStruct(wmv[n][0].shape, F32) for n in names for _ in range(4)],
        compiler_params=_params(),
    )(gathered, *flat)
    return {n: res[4 * i:4 * i + 4] for i, n in enumerate(names)}


def _row_tile(rows):
    return rows // 4 if rows >= 512 and rows % 64 == 0 else rows


def _sum_partials(r_ref):
    g = r_ref[0].astype(F32)
    for k in range(1, N_DEV):
        g = g + r_ref[k].astype(F32)
    return g


def _adamw_sum(recv, w, m, v, name):
    r, cdim = w.shape
    tr = _row_tile(r)

    def body(r_ref, w_ref, m_ref, v_ref, g_ref, dl_ref, m2_ref, v2_ref):
        g = _sum_partials(r_ref)
        g_ref[...] = g
        dl_ref[...], m2_ref[...], v2_ref[...] = _adamw(w_ref[...], g, m_ref[...], v_ref[...])

    row = pl.BlockSpec((tr, cdim), lambda i: (i, 0))
    return pl.pallas_call(
        body, name=name, grid=(r // tr,),
        in_specs=[pl.BlockSpec((N_DEV, tr, cdim), lambda i: (0, i, 0)), row, row, row],
        out_specs=[row] * 4, out_shape=[jax.ShapeDtypeStruct((r, cdim), F32)] * 4,
        compiler_params=_params(("parallel",)),
    )(recv, w, m, v)


def _sum8(recv, name):
    _, r, cdim = recv.shape
    tr = _row_tile(r)

    def body(r_ref, g_ref):
        g_ref[...] = _sum_partials(r_ref)

    return pl.pallas_call(
        body, name=name, grid=(r // tr,),
        in_specs=[pl.BlockSpec((N_DEV, tr, cdim), lambda i: (0, i, 0))],
        out_specs=pl.BlockSpec((tr, cdim), lambda i: (i, 0)), out_shape=jax.ShapeDtypeStruct((r, cdim), F32),
        compiler_params=_params(("parallel",)),
    )(recv)


def _adamw_plain(g, w, m, v, name):
    r, cdim = w.shape
    tr = _row_tile(r)

    def body(g_ref, w_ref, m_ref, v_ref, dl_ref, m2_ref, v2_ref):
        dl_ref[...], m2_ref[...], v2_ref[...] = _adamw(w_ref[...], g_ref[...], m_ref[...], v_ref[...])

    row = pl.BlockSpec((tr, cdim), lambda i: (i, 0))
    return pl.pallas_call(
        body, name=name, grid=(r // tr,), in_specs=[row] * 4, out_specs=[row] * 3,
        out_shape=[jax.ShapeDtypeStruct((r, cdim), F32)] * 3,
        compiler_params=_params(("parallel",)),
    )(g, w, m, v)


IN_SHARD = 609
IN_SHARD_PAD = 640
IF_AT = A_W + M_W


def _regrouped(u):
    return u if u < IF_AT + 8 else u + (IF_W - 8)


def _selection(k, rows, row0, transpose):
    shape = (rows, IN_SHARD_PAD) if transpose else (IN_SHARD_PAD, rows)
    l = lax.broadcasted_iota(jnp.int32, shape, 1 if transpose else 0)
    r = lax.broadcasted_iota(jnp.int32, shape, 0 if transpose else 1) + row0
    u = l + IN_SHARD * k
    ru = u + jnp.where(u >= IF_AT + 8, IF_W - 8, 0)
    return ((ru == r) & (l < IN_SHARD)).astype(BF16)


def _regroup_w_in(g):
    def body(g_ref, o_ref):
        for cb in range(CAT_W // LANES):
            r0 = cb * LANES
            acc = jnp.zeros((D_MODEL, LANES), F32)
            for k in range(N_DEV):
                lo, hi = _regrouped(IN_SHARD * k), _regrouped(IN_SHARD * k + IN_SHARD - 1)
                if hi >= r0 and lo < r0 + LANES:
                    acc = acc + _dot(g_ref[k], _selection(k, LANES, r0, False))
            o_ref[:, r0:r0 + LANES] = acc.astype(BF16)

    return pl.pallas_call(
        body, name="regroup_w_in", out_shape=jax.ShapeDtypeStruct((D_MODEL, CAT_W), BF16),
        compiler_params=_params(),
    )(g)


def _ungroup_w_in(g_parts):
    n = len(g_parts)

    def body(*refs):
        o_ref, g_ref = refs[n], refs[n + 1]
        at = 0
        for p in refs[:n]:
            g_ref[:, at:at + p.shape[1]] = p[...]
            at += p.shape[1]
        for k in range(N_DEV):
            lo, hi = _regrouped(IN_SHARD * k), _regrouped(IN_SHARD * k + IN_SHARD - 1)
            w0, w1 = lo // LANES * LANES, (hi // LANES + 1) * LANES
            o_ref[k] = _dot(g_ref[:, w0:w1], _selection(k, w1 - w0, w0, True)).astype(BF16)

    return pl.pallas_call(
        body, name="ungroup_w_in", out_shape=jax.ShapeDtypeStruct((N_DEV, D_MODEL, IN_SHARD_PAD), BF16),
        scratch_shapes=[pltpu.VMEM((D_MODEL, CAT_W), BF16)], compiler_params=_params(),
    )(*g_parts)


WEIGHT_NAMES = ("w_ada", "b_ada", "g_pre_mix", "g_post_mix", "w_in", "b_if", "conv_w", "conv_b", "attn_sinks",
                "mlstm_norm_w", "w_branch_attn", "w_branch_mlstm", "w_out", "g_pre_ffn", "g_post_ffn",
                "w_ffn_gate", "w_ffn_up", "w_ffn_down")


def kernel(x, c, positions, w_ada, b_ada, g_pre_mix, g_post_mix, w_in, b_if, conv_w, conv_b, attn_sinks, mlstm_norm_w, w_branch_attn, w_branch_mlstm, w_out, g_pre_ffn, g_post_ffn, w_ffn_gate, w_ffn_up, w_ffn_down, loss_target, m_w_ada, m_b_ada, m_g_pre_mix, m_g_post_mix, m_w_in, m_b_if, m_conv_w, m_conv_b, m_attn_sinks, m_mlstm_norm_w, m_w_branch_attn, m_w_branch_mlstm, m_w_out, m_g_pre_ffn, m_g_post_ffn, m_w_ffn_gate, m_w_ffn_up, m_w_ffn_down, v_w_ada, v_b_ada, v_g_pre_mix, v_g_post_mix, v_w_in, v_b_if, v_conv_w, v_conv_b, v_attn_sinks, v_mlstm_norm_w, v_w_branch_attn, v_w_branch_mlstm, v_w_out, v_g_pre_ffn, v_g_post_ffn, v_w_ffn_gate, v_w_ffn_up, v_w_ffn_down):
    given = dict(locals())
    W = {n: given[n][0] for n in WEIGHT_NAMES}
    M = {n: given["m_" + n][0] for n in WEIGHT_NAMES}
    V = {n: given["v_" + n][0] for n in WEIGHT_NAMES}
    me = 4 * lax.axis_index("x") + 2 * lax.axis_index("y") + lax.axis_index("c")

    ff_sh = D_FF // N_DEV
    g_in, g_conv, cg = _all_gather_hbm([jnp.pad(W["w_in"], ((0, 0), (0, IN_SHARD_PAD - IN_SHARD))).astype(BF16),
                                        jnp.pad(W["conv_w"], ((0, 4), (0, 0))), c.reshape(8, D_MODEL // 8)])

    cmat = cg.reshape(N_DEV, D_MODEL)
    ada_w = D_MODEL * 6 // N_DEV
    b_cols = lax.dynamic_slice(W["b_ada"], (me * ada_w,), (ada_w,)).reshape(1, ada_w)
    mod_part = _mod_partial(cmat, W["w_ada"], b_cols)
    mod_recv = _small_exchange(jnp.broadcast_to(mod_part[:, None, :], (N_DEV, 8, ada_w)), False, "scatter_mod")
    mod = mod_recv[:, 0, :].reshape(6, D_MODEL)

    st_b = _exchange_start([W["w_branch_attn"].astype(BF16), W["w_branch_mlstm"].astype(BF16), W["w_out"].astype(BF16),
                            W["w_ffn_gate"].T.astype(BF16), W["w_ffn_up"].T.astype(BF16), W["w_ffn_down"].astype(BF16)],
                           True, mod_recv, "gather_rest_start")
    cols = lambda g: g.transpose(1, 0, 2).reshape(g.shape[1], N_DEV * g.shape[2])
    gains = jnp.stack([W["g_pre_mix"], W["g_post_mix"], W["g_pre_ffn"], W["g_post_ffn"]])
    xs, tgt = x[0], loss_target[0]
    t = _tables(mod, gains, cols(g_conv)[0:4], W["conv_b"], W["b_if"], W["mlstm_norm_w"], positions[0])
    vecs = t["vecs"]
    t["vecs"] = _tie(vecs, st_b["token"], "tie_fwd")
    w_cat = _regroup_w_in(g_in)
    a = _mixer_fwd(xs, t, W["attn_sinks"], w_cat)
    g_ba, g_bm, g_out, g_gate, g_up, g_down = _exchange_wait(st_b, a["ym"], "gather_rest_wait")
    w_ba, w_bm, w_out = cols(g_ba), cols(g_bm), g_out.reshape(D_MODEL, D_MODEL)
    b = _ffn_part(xs, tgt, t, a, w_ba, w_bm, w_out, g_gate.reshape(D_FF, D_MODEL), g_up.reshape(D_FF, D_MODEL),
                  g_down.reshape(D_FF, D_MODEL))

    st_f = _exchange_start([b["g_w_gate"].reshape(N_DEV, ff_sh, D_MODEL), b["g_w_up"].reshape(N_DEV, ff_sh, D_MODEL),
                            b["g_w_down"].reshape(N_DEV, ff_sh, D_MODEL)], False, b["dx1"], "scatter_ffn_start")
    t["vecs"] = _tie(vecs, st_f["token"], "tie_bwd")
    cm = _mixer_bwd(b["dx1"], t, a, b, W["attn_sinks"], w_ba, w_bm, w_out)
    pieces = lambda g, n: g.reshape(g.shape[0], N_DEV, n).transpose(1, 0, 2)
    st_m = _exchange_start([_ungroup_w_in(cm["g_w_cat"]), pieces(cm["g_w_ba"], 128), pieces(cm["g_w_bm"], 128),
                            cm["g_w_out"].reshape(N_DEV, D_MODEL // N_DEV, D_MODEL),
                            jnp.pad(pieces(cm["dcw"][0:4], 128), ((0, 0), (0, 4), (0, 0)))], False, cm["dcw"],
                           "scatter_mixer_start")
    r_gate, r_up, r_down = _exchange_wait(st_f, st_m["token"], "scatter_ffn_wait")
    grad_x, acc_p = _pre_bwd(cm["dproj"], xs, b["dx1"], _tie(vecs, st_m["token"], "tie_pre_bwd"), w_cat)
    small = _small_grads(acc_p, b, cm)
    loss = b["loss"]

    big_out = [{} for _ in range(4)]

    def put(n, res):
        for k in range(4):
            big_out[k][n] = res[k][None]

    put("w_ffn_down", _adamw_sum(r_down, W["w_ffn_down"], M["w_ffn_down"], V["w_ffn_down"], "adamw_w_ffn_down"))
    for n, r in (("w_ffn_gate", r_gate), ("w_ffn_up", r_up)):
        put(n, [o.T for o in _adamw_sum(r, W[n].T, M[n].T, V[n].T, "adamw_" + n)])

    sg = _small_exchange(_small_table({"b_ada": small["dmod"], "gains": small["dgains"], "conv_b": small["dconv_b"],
                                       "mlstm_norm_w": small["dnorm_w"], "b_if": small["db_if"],
                                       "attn_sinks": small["dsinks"]}), True, "gather_small")
    as_row = lambda a, n: jnp.pad(a, (0, SMALL_AT[n][3] * SMALL_AT[n][1] - a.shape[0]))[None]
    small_res = _adamw_small(sg, {n: [as_row(d[n], n) for d in (W, M, V)] for n in SMALL_AT})
    small_out = [{n: small_res[n][k][:, 0:W[n].shape[0]] for n in SMALL_AT} for k in range(4)]
    dmod_cols = lax.dynamic_slice(sg[:, 0:6, :].reshape(N_DEV, 6 * D_MODEL), (0, me * ada_w), (N_DEV, ada_w))
    ada_out = _adamw_w_ada(cmat, dmod_cols, W["w_ada"], M["w_ada"], V["w_ada"])

    r_in, r_ba, r_bm, r_out, r_conv = _exchange_wait(st_m, ada_out[1], "scatter_mixer_wait")
    for n, r in (("w_branch_attn", r_ba), ("w_branch_mlstm", r_bm), ("w_out", r_out)):
        put(n, _adamw_sum(r, W[n], M[n], V[n], "adamw_" + n))
    pad4 = lambda v: jnp.pad(v, ((0, 4), (0, 0)))
    put("conv_w", [o[0:4] for o in _adamw_sum(r_conv, pad4(W["conv_w"]), pad4(M["conv_w"]), pad4(V["conv_w"]),
                                                 "adamw_conv_w")])
    g = _sum8(r_in, "sum_w_in")[:, 0:IN_SHARD].T
    put("w_in", [o.T for o in [g] + list(_adamw_plain(g, W["w_in"].T, M["w_in"].T, V["w_in"].T, "adamw_w_in"))])

    total = lax.psum(loss, ("x", "y", "c"))
    outs = [total, grad_x[None]]
    for k in range(4):
        for n in WEIGHT_NAMES:
            if n == "w_ada":
                outs.append(ada_out[k][None])
            elif n in big_out[k]:
                outs.append(big_out[k][n])
            else:
                outs.append(small_out[k][n])
    return tuple(outs)
```

```python
import functools

import jax
import jax.numpy as jnp
import numpy as np
from jax import lax
from jax.experimental import pallas as pl
from jax.experimental.pallas import tpu as pltpu

F32 = jnp.float32
BF16 = jnp.bfloat16

N_DEV = 8
D_MODEL = 1024
D_FF = 2816
N_Q_HEADS = 8
HEAD_DIM = 64
ATTN_BLOCK = 128
ROPE_THETA = 10000.0
MLSTM_HEADS = 4
MLSTM_HEAD_DIM = 128
MLSTM_CHUNK = 128
NORM_EPS = 1e-6
ADAM_LR = 0.001
ADAM_B1 = 0.9
ADAM_B2 = 0.999
ADAM_EPS = 1e-08
ADAM_WD = 0.01
ADAM_STEP = 10

ROW_TILE = 256
WIDE_TILE = 512
LANES = 128
NEG = -1e30
VMEM_LIMIT = 56 * 1024 * 1024

A_W = 768
M_W = 2048
IF_W = 128
G_W = 2048
CAT_W = A_W + M_W + IF_W + G_W

R_SHIFT_M, R_SCALE_M, R_GATE_M, R_SHIFT_F, R_SCALE_F, R_GATE_F = 0, 1, 2, 3, 4, 5
R_G_PRE_MIX, R_G_POST_MIX, R_G_PRE_FFN, R_G_POST_FFN = 6, 7, 8, 9


def _dot(a, b):
    return jnp.dot(a, b, preferred_element_type=F32)


def _dot_nt(a, b):
    return lax.dot_general(a, b, (((1,), (1,)), ((), ())), preferred_element_type=F32)


def _dot_tn(a, b):
    return lax.dot_general(a, b, (((0,), (0,)), ((), ())), preferred_element_type=F32)


def _recip(x):
    return 1.0 / x


def _sigmoid(x):
    return _recip(1.0 + jnp.exp(-x))


def _colsum(x):
    return jnp.sum(x, axis=0, keepdims=True)


def _rowmean(x):
    return jnp.mean(x, axis=-1, keepdims=True)


def _params(sem=None, vmem=VMEM_LIMIT):
    kw = dict(vmem_limit_bytes=vmem)
    if sem is not None:
        kw["dimension_semantics"] = sem
    return pltpu.CompilerParams(**kw)


def _full(shape):
    nd = len(shape)
    return pl.BlockSpec(shape, lambda *_: (0,) * nd)


def _pre_proj(x, vecs, w_cat):
    S = x.shape[0]
    tm = WIDE_TILE

    def body(x_ref, v_ref, w_ref, h_ref, pa_ref, pm_ref, pif_ref, pg_ref):
        xv = x_ref[...]
        r = lax.rsqrt(_rowmean(xv * xv) + NORM_EPS)
        h = (xv * r * v_ref[R_G_PRE_MIX:R_G_PRE_MIX + 1, :]) * (1.0 + v_ref[R_SCALE_M:R_SCALE_M + 1, :]) \
            + v_ref[R_SHIFT_M:R_SHIFT_M + 1, :]
        hb = h.astype(BF16)
        h_ref[...] = hb
        pa_ref[...] = _dot(hb, w_ref[:, 0:A_W])
        pm_ref[...] = _dot(hb, w_ref[:, A_W:A_W + M_W])
        pif_ref[...] = _dot(hb, w_ref[:, A_W + M_W:A_W + M_W + IF_W])
        pg_ref[...] = _dot(hb, w_ref[:, A_W + M_W + IF_W:CAT_W]).astype(BF16)

    row = lambda w: pl.BlockSpec((tm, w), lambda i: (i, 0))
    return pl.pallas_call(
        body, name="pre_proj", grid=(S // tm,),
        in_specs=[row(D_MODEL), _full(vecs.shape), _full(w_cat.shape)],
        out_specs=[row(D_MODEL), row(A_W), row(M_W), row(IF_W), row(G_W)],
        out_shape=[jax.ShapeDtypeStruct((S, D_MODEL), BF16), jax.ShapeDtypeStruct((S, A_W), F32),
                   jax.ShapeDtypeStruct((S, M_W), F32), jax.ShapeDtypeStruct((S, IF_W), F32),
                   jax.ShapeDtypeStruct((S, G_W), BF16)],
        compiler_params=_params(("parallel",)),
    )(x, vecs, w_cat)


def _mix_fwd(x, ya, ym, pg, vecs, w_ba, w_bm, w_out):
    S = x.shape[0]
    tm = WIDE_TILE

    def body(x_ref, ya_ref, ym_ref, pg_ref, v_ref, wba_ref, wbm_ref, wout_ref,
             x1_ref, merged_ref, mix_ref, pa_ref, pb_ref):
        pa = _dot(ya_ref[...], wba_ref[...])
        pb = _dot(ym_ref[...], wbm_ref[...])
        merged = _sigmoid(pg_ref[:, 0:D_MODEL].astype(F32)) * pa + _sigmoid(pg_ref[:, D_MODEL:G_W].astype(F32)) * pb
        mb = merged.astype(BF16)
        mix = _dot(mb, wout_ref[...])
        r = lax.rsqrt(_rowmean(mix * mix) + NORM_EPS)
        x1_ref[...] = x_ref[...] + v_ref[R_GATE_M:R_GATE_M + 1, :] * (mix * r * v_ref[R_G_POST_MIX:R_G_POST_MIX + 1, :])
        merged_ref[...] = mb
        mix_ref[...] = mix
        pa_ref[...] = pa.astype(BF16)
        pb_ref[...] = pb.astype(BF16)

    row = lambda w: pl.BlockSpec((tm, w), lambda i: (i, 0))
    sd = lambda w, dt: jax.ShapeDtypeStruct((S, w), dt)
    return pl.pallas_call(
        body, name="mix_fwd", grid=(S // tm,),
        in_specs=[row(D_MODEL), row(512), row(512), row(G_W), _full(vecs.shape), _full(w_ba.shape),
                  _full(w_bm.shape), _full(w_out.shape)],
        out_specs=[row(D_MODEL)] * 5,
        out_shape=[sd(D_MODEL, F32), sd(D_MODEL, BF16), sd(D_MODEL, F32), sd(D_MODEL, BF16), sd(D_MODEL, BF16)],
        compiler_params=_params(("parallel",)),
    )(x, ya, ym, pg, vecs, w_ba, w_bm, w_out)


def _ffn_fwd_bwd(x1, tgt, vecs, w_gate, w_up, w_down):
    S = x1.shape[0]
    tm = ROW_TILE

    def body(x1_ref, tgt_ref, v_ref, wg_hbm, wu_hbm, wd_hbm,
             dx1_ref, h2_ref, hid_ref, da_ref, du_ref, dff_ref, acc_ref, loss_ref,
             wg, wu, wd, sem):
        i = pl.program_id(0)

        @pl.when(i == 0)
        def _():
            cps = [pltpu.make_async_copy(wg_hbm, wg, sem.at[0]), pltpu.make_async_copy(wu_hbm, wu, sem.at[1]),
                   pltpu.make_async_copy(wd_hbm, wd, sem.at[2])]
            for cp in cps:
                cp.start()
            for cp in cps:
                cp.wait()
            acc_ref[...] = jnp.zeros_like(acc_ref)
            loss_ref[...] = jnp.zeros_like(loss_ref)

        vrow = lambda r: v_ref[r:r + 1, :]
        x1v = x1_ref[...]
        r3 = lax.rsqrt(_rowmean(x1v * x1v) + NORM_EPS)
        x1hat = x1v * r3
        xn3 = x1hat * vrow(R_G_PRE_FFN)
        h2b = (xn3 * (1.0 + vrow(R_SCALE_F)) + vrow(R_SHIFT_F)).astype(BF16)
        h2_ref[...] = h2b
        a = _dot_nt(h2b, wg[...])
        u = _dot_nt(h2b, wu[...])
        sg = _sigmoid(a)
        sil = a * sg
        hidb = (sil * u).astype(BF16)
        hid_ref[...] = hidb
        ff = _dot(hidb, wd[...])
        r4 = lax.rsqrt(_rowmean(ff * ff) + NORM_EPS)
        ffhat = ff * r4
        n4 = ffhat * vrow(R_G_POST_FFN)
        err = x1v + vrow(R_GATE_F) * n4 - tgt_ref[...]
        loss_ref[...] += jnp.sum(err * err) * (0.5 / D_MODEL)
        dy = err * (1.0 / D_MODEL)
        acc_ref[0:1, :] += _colsum(dy * n4)
        dn4 = dy * vrow(R_GATE_F)
        acc_ref[1:2, :] += _colsum(dn4 * ffhat)
        dffhat = dn4 * vrow(R_G_POST_FFN)
        dffb = (r4 * (dffhat - ffhat * _rowmean(dffhat * ffhat))).astype(BF16)
        dff_ref[...] = dffb
        dhid = _dot_nt(dffb, wd[...])
        dub = (dhid * sil).astype(BF16)
        dab = (dhid * u * (sg * (1.0 + a * (1.0 - sg)))).astype(BF16)
        da_ref[...] = dab
        du_ref[...] = dub
        dh2 = _dot(dab, wg[...]) + _dot(dub, wu[...])
        acc_ref[2:3, :] += _colsum(dh2 * xn3)
        acc_ref[3:4, :] += _colsum(dh2)
        dxn3 = dh2 * (1.0 + vrow(R_SCALE_F))
        acc_ref[4:5, :] += _colsum(dxn3 * x1hat)
        dx1hat = dxn3 * vrow(R_G_PRE_FFN)
        dx1_ref[...] = dy + r3 * (dx1hat - x1hat * _rowmean(dx1hat * x1hat))

    row = lambda w: pl.BlockSpec((tm, w), lambda i: (i, 0))
    sd = lambda w, dt: jax.ShapeDtypeStruct((S, w), dt)
    anyspec = pl.BlockSpec(memory_space=pl.ANY)
    return pl.pallas_call(
        body, name="ffn_fwd_bwd", grid=(S // tm,),
        in_specs=[row(D_MODEL), row(D_MODEL), _full(vecs.shape), anyspec, anyspec, anyspec],
        out_specs=[row(D_MODEL), row(D_MODEL), row(D_FF), row(D_FF), row(D_FF), row(D_MODEL),
                   _full((8, D_MODEL)), _full((8, LANES))],
        out_shape=[sd(D_MODEL, F32), sd(D_MODEL, BF16), sd(D_FF, BF16), sd(D_FF, BF16), sd(D_FF, BF16),
                   sd(D_MODEL, BF16), jax.ShapeDtypeStruct((8, D_MODEL), F32), jax.ShapeDtypeStruct((8, LANES), F32)],
        scratch_shapes=[pltpu.VMEM(w_gate.shape, BF16), pltpu.VMEM(w_up.shape, BF16), pltpu.VMEM(w_down.shape, BF16),
                        pltpu.SemaphoreType.DMA((3,))],
        compiler_params=_params(("arbitrary",)),
    )(x1, tgt, vecs, w_gate, w_up, w_down)


def _mix_bwd(dx1, mix, pa, pb, pg, vecs, w_ba, w_bm, w_out):
    S = dx1.shape[0]
    tm = WIDE_TILE

    def body(dx1_ref, mix_ref, pa_ref, pb_ref, pg_ref, v_ref, wba_ref, wbm_ref, wout_ref,
             dmix_ref, dpa_ref, dpb_ref, dg_ref, dya_ref, dym_ref, acc_ref):
        i = pl.program_id(0)

        @pl.when(i == 0)
        def _():
            acc_ref[...] = jnp.zeros_like(acc_ref)

        vrow = lambda r: v_ref[r:r + 1, :]
        dx1v = dx1_ref[...]
        mix = mix_ref[...]
        r2 = lax.rsqrt(_rowmean(mix * mix) + NORM_EPS)
        mixhat = mix * r2
        acc_ref[0:1, :] += _colsum(dx1v * (mixhat * vrow(R_G_POST_MIX)))
        dn2 = dx1v * vrow(R_GATE_M)
        acc_ref[1:2, :] += _colsum(dn2 * mixhat)
        dmixhat = dn2 * vrow(R_G_POST_MIX)
        dmixb = (r2 * (dmixhat - mixhat * _rowmean(dmixhat * mixhat))).astype(BF16)
        dmix_ref[...] = dmixb
        dmerged = _dot_nt(dmixb, wout_ref[...])
        sa = _sigmoid(pg_ref[:, 0:D_MODEL].astype(F32))
        sm = _sigmoid(pg_ref[:, D_MODEL:G_W].astype(F32))
        dpab = (dmerged * sa).astype(BF16)
        dpbb = (dmerged * sm).astype(BF16)
        dpa_ref[...] = dpab
        dpb_ref[...] = dpbb
        dg_ref[:, 0:D_MODEL] = (dmerged * pa_ref[...].astype(F32) * (sa * (1.0 - sa))).astype(BF16)
        dg_ref[:, D_MODEL:G_W] = (dmerged * pb_ref[...].astype(F32) * (sm * (1.0 - sm))).astype(BF16)
        dya_ref[...] = _dot_nt(dpab, wba_ref[...])
        dym_ref[...] = _dot_nt(dpbb, wbm_ref[...])

    row = lambda w: pl.BlockSpec((tm, w), lambda i: (i, 0))
    sd = lambda w, dt: jax.ShapeDtypeStruct((S, w), dt)
    return pl.pallas_call(
        body, name="mix_bwd", grid=(S // tm,),
        in_specs=[row(D_MODEL), row(D_MODEL), row(D_MODEL), row(D_MODEL), row(G_W), _full(vecs.shape),
                  _full(w_ba.shape), _full(w_bm.shape), _full(w_out.shape)],
        out_specs=[row(D_MODEL), row(D_MODEL), row(D_MODEL), row(G_W), row(512), row(512), _full((8, D_MODEL))],
        out_shape=[sd(D_MODEL, BF16), sd(D_MODEL, BF16), sd(D_MODEL, BF16), sd(G_W, BF16), sd(512, F32), sd(512, F32),
                   jax.ShapeDtypeStruct((8, D_MODEL), F32)],
        compiler_params=_params(("arbitrary",)),
    )(dx1, mix, pa, pb, pg, vecs, w_ba, w_bm, w_out)


def _pre_bwd(pieces, x, dx1, vecs, w_cat):
    S = x.shape[0]
    tm = WIDE_TILE
    n = len(pieces)
    starts = [sum(p.shape[1] for p in pieces[:k]) for k in range(n + 1)]

    def body(*refs):
        p_refs = refs[:n]
        x_ref, dx1_ref, v_ref, w_ref, dx_ref, acc_ref = refs[n:]
        i = pl.program_id(0)

        @pl.when(i == 0)
        def _():
            acc_ref[...] = jnp.zeros_like(acc_ref)

        vrow = lambda r: v_ref[r:r + 1, :]
        dh = _dot_nt(p_refs[0][...], w_ref[:, starts[0]:starts[1]])
        for k in range(1, n):
            dh = dh + _dot_nt(p_refs[k][...], w_ref[:, starts[k]:starts[k + 1]])
        xv = x_ref[...]
        r1 = lax.rsqrt(_rowmean(xv * xv) + NORM_EPS)
        xhat = xv * r1
        acc_ref[0:1, :] += _colsum(dh * (xhat * vrow(R_G_PRE_MIX)))
        acc_ref[1:2, :] += _colsum(dh)
        dxn = dh * (1.0 + vrow(R_SCALE_M))
        acc_ref[2:3, :] += _colsum(dxn * xhat)
        dxhat = dxn * vrow(R_G_PRE_MIX)
        dx_ref[...] = dx1_ref[...] + r1 * (dxhat - xhat * _rowmean(dxhat * xhat))

    row = lambda w: pl.BlockSpec((tm, w), lambda i: (i, 0))
    return pl.pallas_call(
        body, name="pre_bwd", grid=(S // tm,),
        in_specs=[row(p.shape[1]) for p in pieces] + [row(D_MODEL), row(D_MODEL), _full(vecs.shape), _full(w_cat.shape)],
        out_specs=[row(D_MODEL), _full((8, D_MODEL))],
        out_shape=[jax.ShapeDtypeStruct((S, D_MODEL), F32), jax.ShapeDtypeStruct((8, D_MODEL), F32)],
        compiler_params=_params(("arbitrary",)),
    )(*pieces, x, dx1, vecs, w_cat)


def _matmul_tn(a, b, tn, name, ts=1024):
    S, K = a.shape
    N = b.shape[1]
    n_s = S // ts

    def body(a_ref, b_ref, o_ref, acc_ref):
        s = pl.program_id(1)

        @pl.when(s == 0)
        def _():
            acc_ref[...] = jnp.zeros_like(acc_ref)

        acc_ref[...] += _dot_tn(a_ref[...], b_ref[...])

        @pl.when(s == n_s - 1)
        def _():
            o_ref[...] = acc_ref[...].astype(BF16)

    return pl.pallas_call(
        body, name=name, grid=(N // tn, n_s),
        in_specs=[pl.BlockSpec((ts, K), lambda j, s: (s, 0)), pl.BlockSpec((ts, tn), lambda j, s: (s, j))],
        out_specs=pl.BlockSpec((K, tn), lambda j, s: (0, j)),
        out_shape=jax.ShapeDtypeStruct((K, N), BF16),
        scratch_shapes=[pltpu.VMEM((K, tn), F32)],
        compiler_params=_params(("parallel", "arbitrary")),
    )(a, b)


def _rope_swap(t):
    lane = lax.broadcasted_iota(jnp.int32, t.shape, 1)
    first = (lane & (HEAD_DIM - 1)) < (HEAD_DIM // 2)
    return jnp.where(first, pltpu.roll(t, LANES - HEAD_DIM // 2, 1), pltpu.roll(t, HEAD_DIM // 2, 1))


def _rope(t, cos, sin_signed):
    return t * cos + _rope_swap(t) * sin_signed


def _rope_t(d, cos, sin_signed):
    return d * cos + _rope_swap(d * sin_signed)


def _to_kv_lanes(chunk, p, h):
    lane = lax.broadcasted_iota(jnp.int32, chunk.shape, 1)
    src = chunk if p == h else pltpu.roll(chunk, HEAD_DIM, 1)
    return jnp.where((lane >> 6) == h, src, jnp.zeros_like(src))


def _from_kv_lanes(o_a, o_b, h):
    lane = lax.broadcasted_iota(jnp.int32, o_a.shape, 1)
    a = o_a if h == 0 else pltpu.roll(o_a, HEAD_DIM, 1)
    b = o_b if h == 1 else pltpu.roll(o_b, HEAD_DIM, 1)
    return jnp.where(lane < HEAD_DIM, a, b)


def _band_bias(n):
    blk = ATTN_BLOCK
    qi = lax.broadcasted_iota(jnp.int32, (blk, 2 * blk), 0)
    kj = lax.broadcasted_iota(jnp.int32, (blk, 2 * blk), 1)
    seen = (kj > qi) & (kj <= qi + blk) & ((n > 0) | (kj >= blk))
    return jnp.concatenate([jnp.where(seen, 0.0, NEG)] * N_Q_HEADS, axis=0)


def _stack_heads(chunks, h, dtype):
    parts = []
    for g in range(4):
        j = 4 * h + g
        parts.append(_to_kv_lanes(chunks[j // 2], j % 2, h))
    return jnp.concatenate(parts, axis=0).astype(dtype)


def _fused_call(parts, name, n_steps):
    counts = [(len(p["in_specs"]), len(p["out_specs"]), len(p["scratch"])) for p in parts]
    n_in, n_out = sum(c[0] for c in counts), sum(c[1] for c in counts)

    def kernel_fn(*refs):
        i = pl.program_id(0)
        groups, a, b, c = [], 0, n_in, n_in + n_out
        for ci, co, cs in counts:
            groups.append(refs[a:a + ci] + refs[b:b + co] + refs[c:c + cs])
            a, b, c = a + ci, b + co, c + cs
        for p, g in zip(parts, groups):
            p["init"](i, *g)
        for p, g in zip(parts, groups):
            p["body"](i, *g)

    flat = lambda key: [v for p in parts for v in p[key]]
    res = pl.pallas_call(
        kernel_fn, name=name, grid=(n_steps,), in_specs=flat("in_specs"), out_specs=flat("out_specs"),
        out_shape=flat("out_shape"), scratch_shapes=flat("scratch"), compiler_params=_params(("arbitrary",)),
    )(*flat("operands"))
    out, pos = [], 0
    for _, co, _ in counts:
        out.append(res[pos:pos + co])
        pos += co
    return out


def _attn_fwd_part(pa, cos, sin, sinks):
    S = pa.shape[0]
    blk = ATTN_BLOCK
    nb = S // blk

    def body(n, sink_ref, cur_ref, prev_ref, cos_ref, sin_ref, cosp_ref, sinp_ref,
             ya_ref, qr_ref, kr_ref, vb_ref, lse_ref):
        cos_c, sin_c = cos_ref[...], sin_ref[...]
        qch = [_rope(cur_ref[:, c * LANES:(c + 1) * LANES], cos_c, sin_c) * (HEAD_DIM ** -0.5) for c in range(4)]
        for c in range(4):
            qr_ref[:, c * LANES:(c + 1) * LANES] = qch[c].astype(BF16)
        k_cur = _rope(cur_ref[:, 512:640], cos_c, sin_c).astype(BF16)
        k_prev = _rope(prev_ref[:, 0:LANES], cosp_ref[...], sinp_ref[...]).astype(BF16)
        v_cur = cur_ref[:, 640:768].astype(BF16)
        v_prev = prev_ref[:, LANES:2 * LANES].astype(BF16)
        kr_ref[...] = k_cur
        vb_ref[...] = v_cur
        K = jnp.concatenate([k_prev, k_cur], axis=0)
        V = jnp.concatenate([v_prev, v_cur], axis=0)
        lane = lax.broadcasted_iota(jnp.int32, (blk, LANES), 1)
        s = jnp.concatenate([_dot_nt(_stack_heads(qch, h, BF16), K) for h in range(2)], axis=0)
        s = s + _band_bias(n)
        rowmax = jnp.max(s, axis=1, keepdims=True)
        hd = lambda x, j: x[j * blk:(j + 1) * blk]
        m = jnp.concatenate([jnp.maximum(hd(rowmax, j), sink_ref[j]) for j in range(N_Q_HEADS)], axis=0)
        p = jnp.exp(s - m)
        pb = p.astype(BF16)
        den = _rowsum_mxu(pb) + jnp.concatenate([jnp.exp(sink_ref[j] - hd(m, j)) for j in range(N_Q_HEADS)], axis=0)
        o = jnp.concatenate([_dot(pb[4 * h * blk:4 * (h + 1) * blk], V) for h in range(2)], axis=0) * _recip(den)
        lse = m + jnp.log(den)
        outs = [o[j * blk:(j + 1) * blk, :] for j in range(N_Q_HEADS)]
        lse_tile = jnp.zeros((blk, LANES), F32)
        for j in range(N_Q_HEADS):
            lse_tile = jnp.where(lane == j, lse[j * blk:(j + 1) * blk, :], lse_tile)
        for c in range(4):
            ya_ref[:, c * LANES:(c + 1) * LANES] = _from_kv_lanes(outs[2 * c], outs[2 * c + 1], c // 2).astype(BF16)
        lse_ref[...] = lse_tile

    prev = lambda n: jnp.maximum(n - 1, 0)
    sd = lambda w, dt: jax.ShapeDtypeStruct((S, w), dt)
    return dict(
        init=lambda n, *refs: None, body=body, scratch=[], operands=[sinks, pa, pa, cos, sin, cos, sin],
        in_specs=[pl.BlockSpec(memory_space=pltpu.SMEM),
                  pl.BlockSpec((blk, A_W), lambda n: (n, 0)),
                  pl.BlockSpec((blk, 256), lambda n: (prev(n), 2)),
                  pl.BlockSpec((blk, LANES), lambda n: (n, 0)), pl.BlockSpec((blk, LANES), lambda n: (n, 0)),
                  pl.BlockSpec((blk, LANES), lambda n: (prev(n), 0)), pl.BlockSpec((blk, LANES), lambda n: (prev(n), 0))],
        out_specs=[pl.BlockSpec((blk, 512), lambda n: (n, 0)), pl.BlockSpec((blk, 512), lambda n: (n, 0)),
                   pl.BlockSpec((blk, LANES), lambda n: (n, 0)), pl.BlockSpec((blk, LANES), lambda n: (n, 0)),
                   pl.BlockSpec((blk, LANES), lambda n: (n, 0))],
        out_shape=[sd(512, BF16), sd(512, BF16), sd(LANES, BF16), sd(LANES, BF16), sd(LANES, F32)])


def _attn_bwd_part(dya, qr, kr, vb, lse, cos, sin, sinks):
    S = dya.shape[0]
    blk = ATTN_BLOCK
    nb = S // blk

    def init(n, sink_ref, dya_ref, qr_ref, kc_ref, kp_ref, vc_ref, vp_ref, lse_ref, cos_ref, sin_ref, cosp_ref, sinp_ref,
             dqkv_ref, last_ref, dsink_ref, ck, cv, cq):
        @pl.when(n == 0)
        def _():
            ck[...] = jnp.zeros_like(ck)
            cv[...] = jnp.zeros_like(cv)
            cq[...] = jnp.zeros_like(cq)
            dsink_ref[...] = jnp.zeros_like(dsink_ref)

    def body(n, sink_ref, dya_ref, qr_ref, kc_ref, kp_ref, vc_ref, vp_ref, lse_ref, cos_ref, sin_ref, cosp_ref, sinp_ref,
             dqkv_ref, last_ref, dsink_ref, ck, cv, cq):
        K = jnp.concatenate([kp_ref[...], kc_ref[...]], axis=0)
        V = jnp.concatenate([vp_ref[...], vc_ref[...]], axis=0)
        qch = [qr_ref[:, c * LANES:(c + 1) * LANES] for c in range(4)]
        dch = [dya_ref[:, c * LANES:(c + 1) * LANES] for c in range(4)]
        lse_tile = lse_ref[...]
        lane8 = lax.broadcasted_iota(jnp.int32, (8, LANES), 1)
        grp = lambda x, h: x[4 * h * blk:4 * (h + 1) * blk]
        qs = jnp.concatenate([_stack_heads(qch, h, BF16) for h in range(2)], axis=0)
        dos = jnp.concatenate([_stack_heads(dch, h, BF16) for h in range(2)], axis=0)
        lse_col = jnp.concatenate([lse_tile[:, j:j + 1] for j in range(N_Q_HEADS)], axis=0)
        s = jnp.concatenate([_dot_nt(grp(qs, h), K) for h in range(2)], axis=0)
        p = jnp.exp(s + _band_bias(n) - lse_col)
        dp = jnp.concatenate([_dot_nt(grp(dos, h), V) for h in range(2)], axis=0)
        delta = jnp.sum(p * dp, axis=1, keepdims=True)
        dsb = (p * (dp - delta)).astype(BF16)
        pb = p.astype(BF16)
        dq = jnp.concatenate([_dot(grp(dsb, h), K) for h in range(2)], axis=0)
        dk_acc = _dot_tn(grp(dsb, 0), grp(qs, 0)) + _dot_tn(grp(dsb, 1), grp(qs, 1))
        dv_acc = _dot_tn(grp(pb, 0), grp(dos, 0)) + _dot_tn(grp(pb, 1), grp(dos, 1))
        dqs = [dq[j * blk:(j + 1) * blk, :] for j in range(N_Q_HEADS)]
        dsink = jnp.zeros((8, LANES), F32)
        for j in range(N_Q_HEADS):
            rows = slice(j * blk, (j + 1) * blk)
            ps_delta = jnp.exp(sink_ref[j] - lse_col[rows]) * delta[rows]
            dsink = jnp.where(lane8 == j, dsink - jnp.sum(ps_delta), dsink)
        dsink_ref[...] += dsink
        cos_c, sin_c = cos_ref[...], sin_ref[...]
        dqkv_ref[:, 0:512] = cq[...]
        dqkv_ref[:, 512:640] = _rope_t(dk_acc[0:blk, :] + ck[...], cosp_ref[...], sinp_ref[...]).astype(BF16)
        dqkv_ref[:, 640:768] = (dv_acc[0:blk, :] + cv[...]).astype(BF16)
        ck[...] = dk_acc[blk:2 * blk, :]
        cv[...] = dv_acc[blk:2 * blk, :]
        for c in range(4):
            dqc = _from_kv_lanes(dqs[2 * c], dqs[2 * c + 1], c // 2) * (HEAD_DIM ** -0.5)
            dq_c = _rope_t(dqc, cos_c, sin_c).astype(BF16)
            cq[:, c * LANES:(c + 1) * LANES] = dq_c
            last_ref[:, c * LANES:(c + 1) * LANES] = dq_c
        last_ref[:, 512:640] = _rope_t(dk_acc[blk:2 * blk, :], cos_c, sin_c).astype(BF16)
        last_ref[:, 640:768] = dv_acc[blk:2 * blk, :].astype(BF16)

    prev = lambda n: jnp.maximum(n - 1, 0)
    same = lambda n: n
    bs = lambda w, f: pl.BlockSpec((blk, w), lambda n: (f(n), 0))
    return dict(
        init=init, body=body, operands=[sinks, dya, qr, kr, kr, vb, vb, lse, cos, sin, cos, sin],
        in_specs=[pl.BlockSpec(memory_space=pltpu.SMEM),
                  bs(512, same), bs(512, same), bs(LANES, same), bs(LANES, prev), bs(LANES, same), bs(LANES, prev),
                  bs(LANES, same), bs(LANES, same), bs(LANES, same), bs(LANES, prev), bs(LANES, prev)],
        out_specs=[bs(A_W, prev), _full((blk, A_W)), _full((8, LANES))],
        out_shape=[jax.ShapeDtypeStruct((S, A_W), BF16), jax.ShapeDtypeStruct((blk, A_W), BF16),
                   jax.ShapeDtypeStruct((8, LANES), F32)],
        scratch=[pltpu.VMEM((blk, LANES), F32), pltpu.VMEM((blk, LANES), F32), pltpu.VMEM((blk, 512), BF16)])


def _split3(x):
    hi = x.astype(BF16)
    r1 = x - hi.astype(F32)
    mid = r1.astype(BF16)
    lo = (r1 - mid.astype(F32)).astype(BF16)
    return hi, mid, lo


def _tri_matmul(tri_b, x):
    hi, mid, lo = _split3(x)
    return _dot(tri_b, hi) + _dot(tri_b, mid) + _dot(tri_b, lo)


def _log_sigmoid(x):
    return jnp.minimum(x, 0.0) - jnp.log(1.0 + jnp.exp(-jnp.abs(x)))


def _shift_rows(cur, seam, k, down):
    L = cur.shape[0]
    row8 = lax.broadcasted_iota(jnp.int32, seam.shape, 0)
    if down:
        mixed = jnp.concatenate([cur[:L - 8], jnp.where(row8 >= 8 - k, seam, cur[L - 8:])], axis=0)
        return pltpu.roll(mixed, k, 0)
    mixed = jnp.concatenate([jnp.where(row8 < k, seam, cur[:8]), cur[8:]], axis=0)
    return pltpu.roll(mixed, L - k, 0)


def _conv_fwd(cur, tail, cw_ref):
    z = cw_ref[4:5, :]
    for k in range(3, 0, -1):
        z = z + _shift_rows(cur, tail, k, True) * cw_ref[3 - k:4 - k, :]
    return z + cur * cw_ref[3:4, :]


def _rowsum_mxu(x, two_pass=False):
    ones = jnp.ones((x.shape[1], LANES), BF16)
    hi = x.astype(BF16)
    s = _dot(hi, ones)
    if two_pass:
        s = s + _dot((x - hi.astype(F32)).astype(BF16), ones)
    return s


def _stack(f):
    return jnp.concatenate([f(h) for h in range(MLSTM_HEADS)], axis=0)


def _head(x, h):
    L = x.shape[0] // MLSTM_HEADS
    return x[h * L:(h + 1) * L]


def _mlstm_heads_fwd(qk, cur_ref, gt, b_all, c_prev, nmv, tri, eye):
    L = qk.shape[0]
    HD = MLSTM_HEAD_DIM
    W4 = MLSTM_HEADS * HD
    col2row = lambda x: jnp.sum(jnp.where(eye, x, 0.0), axis=0, keepdims=True)
    b_col = _stack(lambda h: b_all[:, 4 + h:5 + h])
    i_col = _stack(lambda h: gt[:, h:h + 1])
    b_row = _stack(lambda h: jnp.broadcast_to(col2row(b_all[:, 4 + h:5 + h]), (L, L)))
    i_row = _stack(lambda h: jnp.broadcast_to(col2row(gt[:, h:h + 1]), (L, L)))
    bl = _stack(lambda h: jnp.broadcast_to(b_all[L - 1:L, 4 + h:5 + h], (L, 1)))
    m_prev = _stack(lambda h: jnp.broadcast_to(nmv[4 + h:5 + h, 0:1], (L, 1)))
    n_prev = _stack(lambda h: jnp.broadcast_to(nmv[h:h + 1, :], (L, HD)))
    tri4 = jnp.concatenate([tri] * MLSTM_HEADS, axis=0)
    Dm = jnp.where(tri4, b_col - b_row + i_row, NEG)
    inter = b_col + m_prev
    m_t = jnp.maximum(inter, jnp.max(Dm, axis=1, keepdims=True))
    W = jnp.exp(Dm - m_t)
    e_t = jnp.exp(inter - m_t)
    q = _stack(lambda h: qk[:, h * HD:(h + 1) * HD])
    k = _stack(lambda h: qk[:, W4 + h * HD:W4 + (h + 1) * HD]) * (HD ** -0.5)
    v = _stack(lambda h: cur_ref[:, 2 * W4 + h * HD:2 * W4 + (h + 1) * HD])
    qb, kb, vb = q.astype(BF16), k.astype(BF16), v.astype(BF16)
    Sc = _stack(lambda h: _dot_nt(_head(qb, h), _head(kb, h))) * W
    Scb = Sc.astype(BF16)
    cb = [c.astype(BF16) for c in c_prev]
    P1 = _stack(lambda h: _dot(_head(qb, h), cb[h]))
    num = _stack(lambda h: _dot(_head(Scb, h), _head(vb, h))) + e_t * P1
    qn = _rowsum_mxu(q * n_prev)
    den = _rowsum_mxu(Scb) + e_t * qn
    floor = jnp.broadcast_to(jnp.exp(-m_t), den.shape)
    inv_g = _recip(jnp.maximum(jnp.abs(den), floor))
    hv = num * inv_g
    a_col = bl - b_col + i_col
    a_max = _stack(lambda h: jnp.broadcast_to(jnp.max(_head(a_col, h), axis=0, keepdims=True), (L, 1)))
    m_new = jnp.maximum(bl + m_prev, a_max)
    dec = jnp.exp(bl + m_prev - m_new)
    u_col = jnp.exp(a_col - m_new)
    return dict(W=W, e_t=e_t, q=q, k=k, v=v, qb=qb, kb=kb, vb=vb, cb=cb, Sc=Sc, Scb=Scb, P1=P1, qn=qn, den=den,
                floor=floor, inv_g=inv_g, hv=hv, n_prev=n_prev, m_new=m_new, dec=dec, u_col=u_col)


def _mlstm_fwd_part(pm, pif, cw, sv):
    S = pm.shape[0]
    L = MLSTM_CHUNK
    nc = S // L
    HD = MLSTM_HEAD_DIM
    W4 = MLSTM_HEADS * HD

    def init(c, cur_ref, pif_ref, cw_ref, sv_ref, ym_ref, z_ref, cst_ref, nst_ref, C, nm, tail):
        @pl.when(c == 0)
        def _():
            C[...] = jnp.zeros_like(C)
            nm[...] = jnp.zeros_like(nm)
            tail[...] = jnp.zeros_like(tail)

    def body(c, cur_ref, pif_ref, cw_ref, sv_ref, ym_ref, z_ref, cst_ref, nst_ref, C, nm, tail):
        z = _conv_fwd(cur_ref[:, 0:2 * W4], tail[...], cw_ref)
        tail[...] = cur_ref[L - 8:L, 0:2 * W4]
        z_ref[...] = z
        qk = z * _sigmoid(z)
        gt = pif_ref[...] + sv_ref[1:2, 0:LANES]
        r_i = lax.broadcasted_iota(jnp.int32, (L, L), 0)
        c_i = lax.broadcasted_iota(jnp.int32, (L, L), 1)
        tri = c_i <= r_i
        eye = c_i == r_i
        b_all = _tri_matmul(tri.astype(BF16), _log_sigmoid(gt))
        nmv = nm[...]
        nst_ref[0] = nmv
        c_prev = [C[h] for h in range(MLSTM_HEADS)]
        f = _mlstm_heads_fwd(qk, cur_ref, gt, b_all, c_prev, nmv, tri, eye)
        hv = f["hv"]
        xc = hv - _rowsum_mxu(hv, True) * (1.0 / HD)
        hhat = xc * lax.rsqrt(_rowsum_mxu(xc * xc) * (1.0 / HD) + NORM_EPS)
        so = _sigmoid(_stack(lambda h: cur_ref[:, 3 * W4 + h * HD:3 * W4 + (h + 1) * HD]))
        wn = _stack(lambda h: jnp.broadcast_to(sv_ref[0:1, h * HD:(h + 1) * HD], (L, HD)))
        y = (so * hhat * wn).astype(BF16)
        kw = f["k"] * f["u_col"]
        kwb = kw.astype(BF16)
        n_new, m_new = [], []
        for h in range(MLSTM_HEADS):
            cst_ref[0, h] = c_prev[h]
            ym_ref[:, h * HD:(h + 1) * HD] = _head(y, h)
            dec = f["dec"][h * L:h * L + 1, :]
            C[h] = dec * c_prev[h] + _dot_tn(_head(kwb, h), _head(f["vb"], h))
            n_new.append(dec * nmv[h:h + 1, :] + _colsum(_head(kw, h)))
            m_new.append(jnp.broadcast_to(f["m_new"][h * L:h * L + 1, :], (1, LANES)))
        nm[...] = jnp.concatenate(n_new + m_new, axis=0)

    return dict(
        init=init, body=body, operands=[pm, pif, cw, sv],
        in_specs=[pl.BlockSpec((L, M_W), lambda c: (c, 0)),
                  pl.BlockSpec((L, IF_W), lambda c: (c, 0)), _full(cw.shape), _full(sv.shape)],
        out_specs=[pl.BlockSpec((L, W4), lambda c: (c, 0)), pl.BlockSpec((L, 2 * W4), lambda c: (c, 0)),
                   pl.BlockSpec((1, MLSTM_HEADS, HD, HD), lambda c: (c, 0, 0, 0)),
                   pl.BlockSpec((1, 8, LANES), lambda c: (c, 0, 0))],
        out_shape=[jax.ShapeDtypeStruct((S, W4), BF16), jax.ShapeDtypeStruct((S, 2 * W4), F32),
                   jax.ShapeDtypeStruct((nc, MLSTM_HEADS, HD, HD), F32), jax.ShapeDtypeStruct((nc, 8, LANES), F32)],
        scratch=[pltpu.VMEM((MLSTM_HEADS, HD, HD), F32), pltpu.VMEM((8, LANES), F32), pltpu.VMEM((8, 2 * W4), F32)])


def _mlstm_bwd_part(pm, zc, pif, cw, sv, dym, cst, nst):
    S = pm.shape[0]
    L = MLSTM_CHUNK
    nc = S // L
    HD = MLSTM_HEAD_DIM
    W4 = MLSTM_HEADS * HD

    def init(r, cur_ref, z_ref, pif_ref, cw_ref, sv_ref, dym_ref, cst_ref, nst_ref,
             dm_ref, dcw_ref, dsv_ref, dC, dn, dz_next, dqk):
        @pl.when(r == 0)
        def _():
            dC[...] = jnp.zeros_like(dC)
            dn[...] = jnp.zeros_like(dn)
            dz_next[...] = jnp.zeros_like(dz_next)
            dcw_ref[...] = jnp.zeros_like(dcw_ref)
            dsv_ref[...] = jnp.zeros_like(dsv_ref)

    def body(r, cur_ref, z_ref, pif_ref, cw_ref, sv_ref, dym_ref, cst_ref, nst_ref,
             dm_ref, dcw_ref, dsv_ref, dC, dn, dz_next, dqk):
        z = z_ref[...]
        sgz = _sigmoid(z)
        qk = z * sgz
        gt = pif_ref[...] + sv_ref[1:2, 0:LANES]
        r_i = lax.broadcasted_iota(jnp.int32, (L, L), 0)
        c_i = lax.broadcasted_iota(jnp.int32, (L, L), 1)
        tri = c_i <= r_i
        eye = c_i == r_i
        b_all = _tri_matmul(tri.astype(BF16), _log_sigmoid(gt))
        lane = lax.broadcasted_iota(jnp.int32, (L, LANES), 1)
        rowl = lax.broadcasted_iota(jnp.int32, (L, 1), 0)
        nmv = nst_ref[0]
        heads = range(MLSTM_HEADS)
        c_prev = [cst_ref[0, h] for h in heads]
        f = _mlstm_heads_fwd(qk, cur_ref, gt, b_all, c_prev, nmv, tri, eye)
        hv, inv_g, den, e_t, u_col, n_prev = f["hv"], f["inv_g"], f["den"], f["e_t"], f["u_col"], f["n_prev"]
        q, k, v, qb, kb, vb, Sc, Scb, W = f["q"], f["k"], f["v"], f["qb"], f["kb"], f["vb"], f["Sc"], f["Scb"], f["W"]
        xc = hv - _rowsum_mxu(hv, True) * (1.0 / HD)
        rstd = lax.rsqrt(_rowsum_mxu(xc * xc) * (1.0 / HD) + NORM_EPS)
        hhat = xc * rstd
        wn = _stack(lambda h: jnp.broadcast_to(sv_ref[0:1, h * HD:(h + 1) * HD], (L, HD)))
        so = _sigmoid(_stack(lambda h: cur_ref[:, 3 * W4 + h * HD:3 * W4 + (h + 1) * HD]))
        dy = _stack(lambda h: dym_ref[:, h * HD:(h + 1) * HD])
        d_o = (dy * hhat * wn * (so * (1.0 - so))).astype(BF16)
        dln = dy * so
        dwn = dln * hhat
        dhhat = dln * wn
        m2 = _rowsum_mxu(dhhat * hhat) * (1.0 / HD)
        dh = rstd * (dhhat - _rowsum_mxu(dhhat) * (1.0 / HD) - hhat * m2)
        dnum = dh * inv_g
        active = jnp.abs(den) > f["floor"]
        dden = jnp.where(active, -(HD * NORM_EPS) * m2 * rstd * rstd * inv_g * jnp.where(den >= 0.0, 1.0, -1.0), 0.0)
        dnumb = dnum.astype(BF16)
        dSc = _stack(lambda h: _dot_nt(_head(dnumb, h), _head(vb, h))) + dden
        dA = (dSc * W).astype(BF16)
        G = dSc * Sc
        Gb = G.astype(BF16)
        Gl = (G - Gb.astype(F32)).astype(BF16)
        ones = jnp.ones((L, LANES), BF16)
        Gr = _dot(Gb, ones) + _dot(Gl, ones)
        Gc = _stack(lambda h: _dot_tn(_head(Gb, h), ones) + _dot_tn(_head(Gl, h), ones))
        dCn = [dC[h] for h in heads]
        dCnb = [d.astype(BF16) for d in dCn]
        dnv = dn[...]
        dn_new = _stack(lambda h: jnp.broadcast_to(dnv[h:h + 1, :], (L, HD)))
        kdC = _stack(lambda h: _dot(_head(kb, h), dCnb[h]))
        vdC = _stack(lambda h: _dot_nt(_head(vb, h), dCnb[h]))
        dv = (_stack(lambda h: _dot_tn(_head(Scb, h), _head(dnumb, h))) + u_col * kdC).astype(BF16)
        dq = _stack(lambda h: _dot(_head(dA, h), _head(kb, h))) \
            + e_t * _stack(lambda h: _dot_nt(_head(dnumb, h), f["cb"][h])) + (e_t * dden) * n_prev
        dk = (_stack(lambda h: _dot_tn(_head(dA, h), _head(qb, h))) + u_col * (vdC + dn_new)) * (HD ** -0.5)
        E = (_rowsum_mxu(f["P1"] * dnum, True) + dden * f["qn"]) * e_t
        U = _rowsum_mxu(kdC * v + k * dn_new, True) * u_col
        qe = (q * e_t).astype(BF16)
        qd = (e_t * dden) * q
        di = Gc + U
        db = Gr + E - Gc - U
        di_tile = jnp.zeros((L, LANES), F32)
        db_tile = jnp.zeros((L, LANES), F32)
        dn_rows = []
        for h in heads:
            dec = f["dec"][h * L:h * L + 1, :]
            ddec = jnp.sum(dCn[h] * c_prev[h]) + jnp.sum(dnv[h:h + 1, :] * nmv[h:h + 1, :])
            dbl = ddec * dec + jnp.sum(_head(U, h), axis=0, keepdims=True)
            di_tile = jnp.where(lane == h, _head(di, h), di_tile)
            db_tile = jnp.where(lane == 4 + h, _head(db, h) + jnp.where(rowl == L - 1, dbl, 0.0), db_tile)
            dC[h] = dec * dCn[h] + _dot_tn(_head(qe, h), _head(dnumb, h))
            dn_rows.append(dec * dnv[h:h + 1, :] + _colsum(_head(qd, h)))
            dsv_ref[0:1, h * HD:(h + 1) * HD] += _colsum(_head(dwn, h))
            dqk[:, h * HD:(h + 1) * HD] = _head(dq, h)
            dqk[:, W4 + h * HD:W4 + (h + 1) * HD] = _head(dk, h)
            dm_ref[:, 2 * W4 + h * HD:2 * W4 + (h + 1) * HD] = _head(dv, h)
            dm_ref[:, 3 * W4 + h * HD:3 * W4 + (h + 1) * HD] = _head(d_o, h)
        dn[...] = jnp.concatenate(dn_rows + [jnp.zeros((8 - MLSTM_HEADS, LANES), F32)], axis=0)
        dlf = _tri_matmul((r_i <= c_i).astype(BF16), db_tile)
        dif = jnp.where(lane < 4, di_tile, jnp.where(lane < 8, dlf * (1.0 - _sigmoid(gt)), 0.0))
        dm_ref[:, M_W:M_W + IF_W] = dif.astype(BF16)
        dsv_ref[1:2, 0:LANES] += _colsum(dif)
        dz = dqk[...] * (sgz * (1.0 + z * (1.0 - sgz)))
        dcw_ref[4:5, :] += _colsum(dz)
        u = cur_ref[:, 0:2 * W4]
        du_in = dz * cw_ref[3:4, :]
        dcw_ref[3:4, :] += _colsum(dz * u)
        for k in range(1, 4):
            up = _shift_rows(dz, dz_next[...], k, False)
            dcw_ref[3 - k:4 - k, :] += _colsum(up * u)
            du_in = du_in + up * cw_ref[3 - k:4 - k, :]
        dz_next[...] = dz[0:8, :]
        dm_ref[:, 0:2 * W4] = du_in.astype(BF16)

    cidx = lambda r: nc - 1 - r
    return dict(
        init=init, body=body, operands=[pm, zc, pif, cw, sv, dym, cst, nst],
        in_specs=[pl.BlockSpec((L, M_W), lambda r: (cidx(r), 0)), pl.BlockSpec((L, 2 * W4), lambda r: (cidx(r), 0)),
                  pl.BlockSpec((L, IF_W), lambda r: (cidx(r), 0)), _full(cw.shape), _full(sv.shape),
                  pl.BlockSpec((L, W4), lambda r: (cidx(r), 0)),
                  pl.BlockSpec((1, MLSTM_HEADS, HD, HD), lambda r: (cidx(r), 0, 0, 0)),
                  pl.BlockSpec((1, 8, LANES), lambda r: (cidx(r), 0, 0))],
        out_specs=[pl.BlockSpec((L, M_W + IF_W), lambda r: (cidx(r), 0)), _full((8, 2 * W4)), _full((8, W4))],
        out_shape=[jax.ShapeDtypeStruct((S, M_W + IF_W), BF16),
                   jax.ShapeDtypeStruct((8, 2 * W4), F32), jax.ShapeDtypeStruct((8, W4), F32)],
        scratch=[pltpu.VMEM((MLSTM_HEADS, HD, HD), F32), pltpu.VMEM((8, LANES), F32),
                 pltpu.VMEM((8, 2 * W4), F32), pltpu.VMEM((L, 2 * W4), F32)])


def _rope_tables(positions):
    half = HEAD_DIM // 2
    inv_freq = ROPE_THETA ** (-2.0 * jnp.arange(half, dtype=F32) / HEAD_DIM)
    ang = positions.astype(F32)[:, None] * inv_freq
    cos = jnp.tile(jnp.cos(ang), (1, LANES // half))
    sign = jnp.tile(jnp.concatenate([-jnp.ones((half,), F32), jnp.ones((half,), F32)]), LANES // HEAD_DIM)
    sin = jnp.tile(jnp.sin(ang), (1, LANES // half)) * sign
    return cos, sin


def _local_step(x, tgt, positions, mod, gains, w_cat, w_ba, w_bm, w_out, w_gate, w_up, w_down,
                conv_w, conv_b, b_if, sinks, norm_w):
    t = _tables(mod, gains, conv_w, conv_b, b_if, norm_w, positions)
    a = _mixer_fwd(x, t, sinks, w_cat)
    b = _ffn_part(x, tgt, t, a, w_ba, w_bm, w_out, w_gate, w_up, w_down)
    c = _mixer_bwd(b["dx1"], t, a, b, sinks, w_ba, w_bm, w_out)
    grad_x, acc_p = _pre_bwd(c["dproj"], x, b["dx1"], t["vecs"], w_cat)
    big = dict(w_cat=jnp.concatenate(c["g_w_cat"], axis=1), w_ba=c["g_w_ba"], w_bm=c["g_w_bm"], w_out=c["g_w_out"], w_gate=b["g_w_gate"],
               w_up=b["g_w_up"], w_down=b["g_w_down"])
    return b["loss"], grad_x, big, _small_grads(acc_p, b, c)


def _tables(mod, gains, conv_w, conv_b, b_if, norm_w, positions):
    cos, sin = _rope_tables(positions)
    return dict(
        vecs=jnp.concatenate([mod, gains, jnp.zeros((6, D_MODEL), F32)], axis=0),
        cw=jnp.concatenate([conv_w, conv_b.reshape(1, -1), jnp.zeros((3, 2 * 512), F32)], axis=0),
        sv=jnp.zeros((8, 512), F32).at[0].set(norm_w).at[1, 0:8].set(b_if), cos=cos, sin=sin)


def _mixer_fwd(x, t, sinks, w_cat):
    h, pa, pm, pif, pg = _pre_proj(x, t["vecs"], w_cat)
    n_blk = x.shape[0] // ATTN_BLOCK
    (ya, qr, kr, vb, lse), = _fused_call([_attn_fwd_part(pa, t["cos"], t["sin"], sinks)], "attn_fwd", n_blk)
    (ym, zc, cst, nst), = _fused_call([_mlstm_fwd_part(pm, pif, t["cw"], t["sv"])], "mlstm_fwd", n_blk)
    return dict(h=h, pm=pm, pif=pif, pg=pg, ya=ya, qr=qr, kr=kr, vb=vb, lse=lse, ym=ym, zc=zc, cst=cst, nst=nst)


def _ffn_part(x, tgt, t, a, w_ba, w_bm, w_out, w_gate, w_up, w_down):
    x1, merged, mix, pba, pbm = _mix_fwd(x, a["ya"], a["ym"], a["pg"], t["vecs"], w_ba, w_bm, w_out)
    dx1, h2, hid, da, du, dff, acc_f, loss = _ffn_fwd_bwd(x1, tgt, t["vecs"], w_gate, w_up, w_down)
    return dict(merged=merged, mix=mix, pba=pba, pbm=pbm, dx1=dx1, acc_f=acc_f, loss=loss[0, 0],
                g_w_gate=_matmul_tn(da, h2, 1024, "dw_ffn_gate"),
                g_w_up=_matmul_tn(du, h2, 1024, "dw_ffn_up"),
                g_w_down=_matmul_tn(hid, dff, 1024, "dw_ffn_down"))


def _mixer_bwd(dx1, t, a, b, sinks, w_ba, w_bm, w_out):
    dmix, dpa, dpb, dg, dya, dym, acc_m = _mix_bwd(dx1, b["mix"], b["pba"], b["pbm"], a["pg"], t["vecs"], w_ba, w_bm, w_out)
    g_w_out = _matmul_tn(b["merged"], dmix, 1024, "dw_out")
    g_w_ba = _matmul_tn(a["ya"], dpa, 1024, "dw_branch_attn")
    g_w_bm = _matmul_tn(a["ym"], dpb, 1024, "dw_branch_mlstm")
    n_blk = dx1.shape[0] // ATTN_BLOCK
    (dqkv, dqkv_last, dsink), = _fused_call(
        [_attn_bwd_part(dya, a["qr"], a["kr"], a["vb"], a["lse"], t["cos"], t["sin"], sinks)], "attn_bwd", n_blk)
    (dm, dcw, dsv), = _fused_call(
        [_mlstm_bwd_part(a["pm"], a["zc"], a["pif"], t["cw"], t["sv"], dym, a["cst"], a["nst"])], "mlstm_bwd", n_blk)
    dqkv = lax.dynamic_update_slice(dqkv, dqkv_last, (dqkv.shape[0] - ATTN_BLOCK, 0))
    dproj = [dqkv, dm, dg]
    g_w_cat = [_matmul_tn(a["h"], p, 1024 if p.shape[1] % 1024 == 0 else p.shape[1], "dw_in_" + n)
               for p, n in zip(dproj, ("attn", "mlstm", "branch"))]
    return dict(dproj=dproj, g_w_cat=g_w_cat, g_w_out=g_w_out, g_w_ba=g_w_ba,
                g_w_bm=g_w_bm, acc_m=acc_m, dsink=dsink, dcw=dcw, dsv=dsv)


def _small_grads(acc_p, b, c):
    acc_f, acc_m = b["acc_f"], c["acc_m"]
    dmod = jnp.stack([acc_p[1], acc_p[0], acc_m[0], acc_f[3], acc_f[2], acc_f[0]])
    dgains = jnp.stack([acc_p[2], acc_m[1], acc_f[4], acc_f[1]])
    return dict(dmod=dmod, dgains=dgains, dconv_w=c["dcw"][0:4], dconv_b=c["dcw"][4], db_if=c["dsv"][1, 0:8],
                dsinks=c["dsink"][0, 0:8], dnorm_w=c["dsv"][0])


MESH_ID = pl.DeviceIdType.MESH


def _mesh_pos():
    return lax.axis_index("x"), lax.axis_index("y"), lax.axis_index("c")


def _flip(v, bit):
    return 1 - v if bit else v


def _relations():
    return [((r >> 2) & 1, (r >> 1) & 1, r & 1) for r in range(1, N_DEV)]


def _small_exchange(p, gather, name):
    R, V = p.shape[-2:]

    def body(p_ref, out_ref, send_sems, recv_sems):
        x, y, c = _mesh_pos()
        me = 4 * x + 2 * y + c
        out_ref[me] = p_ref[...] if gather else p_ref[me]
        peers = []
        for dx, dy, dc in _relations():
            px, py, pc = _flip(x, dx), _flip(y, dy), _flip(c, dc)
            peers.append(((px, py, pc), 4 * px + 2 * py + pc))

        def copy(k, landing):
            peer, pid = peers[k]
            return pltpu.make_async_remote_copy(
                src_ref=p_ref if gather else p_ref.at[pid], dst_ref=out_ref.at[landing],
                send_sem=send_sems.at[k], recv_sem=recv_sems.at[k], device_id=peer, device_id_type=MESH_ID)

        sends = [copy(k, me) for k in range(N_DEV - 1)]
        for cp in sends:
            cp.start()
        for k in range(N_DEV - 1):
            copy(k, peers[k][1]).wait_recv()
        for cp in sends:
            cp.wait_send()

    vm = pl.BlockSpec(memory_space=pltpu.VMEM)
    return pl.pallas_call(
        body, name=name, in_specs=[vm], out_specs=vm,
        out_shape=jax.ShapeDtypeStruct((N_DEV, R, V), F32),
        scratch_shapes=[pltpu.SemaphoreType.DMA((N_DEV - 1,)), pltpu.SemaphoreType.DMA((N_DEV - 1,))],
        compiler_params=pltpu.CompilerParams(vmem_limit_bytes=VMEM_LIMIT),
    )(p)


HBM_SPEC = pl.BlockSpec(memory_space=pltpu.HBM)
SEM_SPEC = pl.BlockSpec(memory_space=pltpu.SEMAPHORE)


def _peers(x, y, c):
    out = []
    for dx, dy, dc in _relations():
        px, py, pc = _flip(x, dx), _flip(y, dy), _flip(c, dc)
        out.append(((px, py, pc), 4 * px + 2 * py + pc))
    return out


def _exchange_start(arrs, gather, after, name):
    n = len(arrs)
    me_out = 4 * lax.axis_index("x") + 2 * lax.axis_index("y") + lax.axis_index("c")
    lands = []
    for a in arrs:
        own = a[None] if gather else lax.dynamic_index_in_dim(a, me_out, 0, keepdims=True)
        empty = lax.empty(((N_DEV,) + a.shape) if gather else a.shape, a.dtype)
        lands.append(lax.dynamic_update_index_in_dim(empty, own, me_out, 0))

    def body(*refs):
        a_refs, l_refs = refs[:n], refs[n:2 * n]
        send_sems, recv_sems = refs[2 * n + 1], refs[2 * n + 2]
        token = refs[4 * n + 3]
        x, y, c = _mesh_pos()
        me = 4 * x + 2 * y + c
        for a in range(n):
            for k, (peer, pid) in enumerate(_peers(x, y, c)):
                pltpu.make_async_remote_copy(
                    src_ref=a_refs[a] if gather else a_refs[a].at[pid], dst_ref=l_refs[a].at[me],
                    send_sem=send_sems.at[a * (N_DEV - 1) + k], recv_sem=recv_sems.at[a * (N_DEV - 1) + k],
                    device_id=peer, device_id_type=MESH_ID).start()
        token[...] = jnp.zeros_like(token)

    sem = pltpu.SemaphoreType.DMA((n * (N_DEV - 1),))
    hbm = lambda a: pltpu.with_memory_space_constraint(a, pltpu.HBM)
    res = pl.pallas_call(
        body, name=name,
        out_shape=(sem, sem, *[pltpu.HBM(a.shape, a.dtype) for a in arrs], *[pltpu.HBM(l.shape, l.dtype) for l in lands],
                   jax.ShapeDtypeStruct((8, LANES), F32)),
        in_specs=[HBM_SPEC] * (2 * n) + [pl.BlockSpec(memory_space=pl.ANY)],
        out_specs=(SEM_SPEC, SEM_SPEC, *[HBM_SPEC] * (2 * n), pl.BlockSpec(memory_space=pltpu.VMEM)),
        input_output_aliases={i: 2 + i for i in range(2 * n)},
        compiler_params=pltpu.CompilerParams(has_side_effects=pltpu.SideEffectType.DATAFLOW_SIDE_EFFECTING),
    )(*[hbm(a) for a in arrs], *[hbm(l) for l in lands], after)
    return dict(sems=res[0:2], arrs=res[2:2 + n], lands=res[2 + n:2 + 2 * n], token=res[2 + 2 * n], gather=gather)


def _exchange_wait(st, after, name):
    n = len(st["arrs"])
    gather = st["gather"]

    def body(*refs):
        a_refs, l_refs = refs[:n], refs[n:2 * n]
        send_sems, recv_sems = refs[2 * n], refs[2 * n + 1]
        x, y, c = _mesh_pos()
        for a in range(n):
            for k, (peer, pid) in enumerate(_peers(x, y, c)):
                cp = pltpu.make_async_remote_copy(
                    src_ref=a_refs[a] if gather else a_refs[a].at[pid], dst_ref=l_refs[a].at[pid],
                    send_sem=send_sems.at[a * (N_DEV - 1) + k], recv_sem=recv_sems.at[a * (N_DEV - 1) + k],
                    device_id=peer, device_id_type=MESH_ID)
                cp.wait_send()
                cp.wait_recv()

    both = list(st["arrs"]) + list(st["lands"])
    res = pl.pallas_call(
        body, name=name, out_shape=[pltpu.HBM(a.shape, a.dtype) for a in both],
        in_specs=[HBM_SPEC] * (2 * n) + [SEM_SPEC, SEM_SPEC, pl.BlockSpec(memory_space=pl.ANY)],
        out_specs=[HBM_SPEC] * (2 * n), input_output_aliases={i: i for i in range(2 * n)},
        compiler_params=pltpu.CompilerParams(has_side_effects=pltpu.SideEffectType.DATAFLOW_SIDE_EFFECTING),
    )(*both, *st["sems"], after)
    return res[n:2 * n]


def _tie(x, token, name):
    def body(x_ref, t_ref, o_ref):
        o_ref[...] = x_ref[...]

    vm = pl.BlockSpec(memory_space=pltpu.VMEM)
    return pl.pallas_call(
        body, name=name, in_specs=[vm, pl.BlockSpec(memory_space=pl.ANY)], out_specs=vm,
        out_shape=jax.ShapeDtypeStruct(x.shape, x.dtype),
    )(x, token)


def _all_gather_hbm(shards):
    n = len(shards)

    def body(*refs):
        p_refs, out_refs = refs[:n], refs[n:2 * n]
        send_sems, recv_sems, local_sems = refs[2 * n:]
        x, y, c = _mesh_pos()
        me, sibling = (x, y, c), (x, y, 1 - c)
        chips = [(1 - x, y), (x, 1 - y), (1 - x, 1 - y)]

        def copy(a, k, block, to, own=False):
            slot = out_refs[a].at[4 * block[0] + 2 * block[1] + block[2]]
            return pltpu.make_async_remote_copy(
                src_ref=p_refs[a] if own else slot, dst_ref=slot,
                send_sem=send_sems.at[a, k], recv_sem=recv_sems.at[a, k], device_id=to, device_id_type=MESH_ID)

        mine = [pltpu.make_async_copy(p_refs[a], out_refs[a].at[4 * x + 2 * y + c], local_sems.at[a]) for a in range(n)]
        for cp in mine:
            cp.start()
        first = []
        for a in range(n):
            first.append(copy(a, 0, me, sibling, own=True))
            first += [copy(a, 1 + j, me, (*chip, c), own=True) for j, chip in enumerate(chips)]
        for cp in first:
            cp.start()
        passed = []
        for j, chip in enumerate(chips):
            for a in range(n):
                copy(a, 1 + j, (*chip, c), me).wait_recv()
                passed.append(copy(a, 4 + j, (*chip, c), sibling))
                passed[-1].start()
        for a in range(n):
            copy(a, 0, sibling, me).wait_recv()
            for j, chip in enumerate(chips):
                copy(a, 4 + j, (*chip, 1 - c), me).wait_recv()
        for cp in first + passed:
            cp.wait_send()
        for cp in mine:
            cp.wait()

    hbm = pl.BlockSpec(memory_space=pl.ANY)
    return pl.pallas_call(
        body, name="gather_weights", in_specs=[hbm] * n, out_specs=[hbm] * n,
        out_shape=[jax.ShapeDtypeStruct((N_DEV,) + s.shape, s.dtype) for s in shards],
        scratch_shapes=[pltpu.SemaphoreType.DMA((n, N_DEV - 1)), pltpu.SemaphoreType.DMA((n, N_DEV - 1)),
                        pltpu.SemaphoreType.DMA((n,))],
    )(*shards)


def _adamw(w, g, m, v):
    m2 = ADAM_B1 * m + (1.0 - ADAM_B1) * g
    v2 = ADAM_B2 * v + (1.0 - ADAM_B2) * (g * g)
    m_hat = m2 / (1.0 - ADAM_B1 ** ADAM_STEP)
    v_hat = v2 / (1.0 - ADAM_B2 ** ADAM_STEP)
    delta = -ADAM_LR * (m_hat / (jnp.sqrt(v_hat) + ADAM_EPS) + ADAM_WD * w)
    return delta, m2, v2


def _mod_partial(cmat, w_shard, b_shard):
    def body(c_ref, w_ref, b_ref, o_ref):
        o_ref[...] = _dot(c_ref[...].astype(BF16), w_ref[...].astype(BF16)) + b_ref[...]

    return pl.pallas_call(
        body, name="mod_partial", out_shape=jax.ShapeDtypeStruct((N_DEV, w_shard.shape[1]), F32),
        compiler_params=_params(),
    )(cmat, w_shard, b_shard)


def _adamw_w_ada(cmat, dmod_cols, w, m, v):
    r, cdim = w.shape
    tr = _row_tile(r)

    def body(c_ref, d_ref, w_ref, m_ref, v_ref, g_ref, dl_ref, m2_ref, v2_ref):
        g = _dot_tn(c_ref[...].astype(BF16), d_ref[...].astype(BF16))
        g_ref[...] = g
        dl_ref[...], m2_ref[...], v2_ref[...] = _adamw(w_ref[...], g, m_ref[...], v_ref[...])

    row = pl.BlockSpec((tr, cdim), lambda i: (i, 0))
    return pl.pallas_call(
        body, name="adamw_w_ada", grid=(r // tr,),
        in_specs=[pl.BlockSpec((N_DEV, tr), lambda i: (0, i)), _full(dmod_cols.shape), row, row, row],
        out_specs=[row] * 4, out_shape=[jax.ShapeDtypeStruct(w.shape, F32)] * 4,
        compiler_params=_params(("parallel",)),
    )(cmat, dmod_cols, w, m, v)


SMALL_ROWS = 16
SMALL_AT = {"b_ada": (0, 6, 0, D_MODEL), "g_pre_mix": (6, 1, 0, D_MODEL), "g_post_mix": (7, 1, 0, D_MODEL),
            "g_pre_ffn": (8, 1, 0, D_MODEL), "g_post_ffn": (9, 1, 0, D_MODEL), "conv_b": (10, 1, 0, D_MODEL),
            "mlstm_norm_w": (11, 1, 0, 512), "b_if": (11, 1, 512, LANES), "attn_sinks": (11, 1, 640, LANES)}


def _small_table(part):
    tail = jnp.concatenate([part["mlstm_norm_w"], jnp.pad(part["b_if"], (0, LANES - 8)),
                            jnp.pad(part["attn_sinks"], (0, LANES - 8)), jnp.zeros((256,), F32)])
    return jnp.concatenate([part["b_ada"], part["gains"], part["conv_b"][None], tail[None],
                            jnp.zeros((SMALL_ROWS - 12, D_MODEL), F32)], axis=0)


def _adamw_small(gathered, wmv):
    names = list(SMALL_AT)

    def body(*refs):
        g_ref, ins, outs = refs[0], refs[1:1 + 3 * len(names)], refs[1 + 3 * len(names):]
        g = g_ref[0]
        for k in range(1, N_DEV):
            g = g + g_ref[k]
        for i, n in enumerate(names):
            r0, rows, l0, lanes = SMALL_AT[n]
            gi = jnp.concatenate([g[r:r + 1, l0:l0 + lanes] for r in range(r0, r0 + rows)], axis=1)
            w_ref, m_ref, v_ref = ins[3 * i:3 * i + 3]
            go, dl, m2, v2 = outs[4 * i:4 * i + 4]
            go[...] = gi
            dl[...], m2[...], v2[...] = _adamw(w_ref[...], gi, m_ref[...], v_ref[...])

    flat = [a for n in names for a in wmv[n]]
    res = pl.pallas_call(
        body, name="adamw_small",
        out_shape=[jax.ShapeDtypeStruct(wmv[n][0].shape, F32) for n in names for _ in range(4)],
        compiler_params=_params(),
    )(gathered, *flat)
    return {n: res[4 * i:4 * i + 4] for i, n in enumerate(names)}


def _row_tile(rows):
    return rows // 4 if rows >= 512 and rows % 64 == 0 else rows


def _sum_partials(r_ref):
    g = r_ref[0].astype(F32)
    for k in range(1, N_DEV):
        g = g + r_ref[k].astype(F32)
    return g


def _adamw_sum(recv, w, m, v, name):
    r, cdim = w.shape
    tr = _row_tile(r)

    def body(r_ref, w_ref, m_ref, v_ref, g_ref, dl_ref, m2_ref, v2_ref):
        g = _sum_partials(r_ref)
        g_ref[...] = g
        dl_ref[...], m2_ref[...], v2_ref[...] = _adamw(w_ref[...], g, m_ref[...], v_ref[...])

    row = pl.BlockSpec((tr, cdim), lambda i: (i, 0))
    return pl.pallas_call(
        body, name=name, grid=(r // tr,),
        in_specs=[pl.BlockSpec((N_DEV, tr, cdim), lambda i: (0, i, 0)), row, row, row],
        out_specs=[row] * 4, out_shape=[jax.ShapeDtypeStruct((r, cdim), F32)] * 4,
        compiler_params=_params(("parallel",)),
    )(recv, w, m, v)


def _sum8(recv, name):
    _, r, cdim = recv.shape
    tr = _row_tile(r)

    def body(r_ref, g_ref):
        g_ref[...] = _sum_partials(r_ref)

    return pl.pallas_call(
        body, name=name, grid=(r // tr,),
        in_specs=[pl.BlockSpec((N_DEV, tr, cdim), lambda i: (0, i, 0))],
        out_specs=pl.BlockSpec((tr, cdim), lambda i: (i, 0)), out_shape=jax.ShapeDtypeStruct((r, cdim), F32),
        compiler_params=_params(("parallel",)),
    )(recv)


def _adamw_plain(g, w, m, v, name):
    r, cdim = w.shape
    tr = _row_tile(r)

    def body(g_ref, w_ref, m_ref, v_ref, dl_ref, m2_ref, v2_ref):
        dl_ref[...], m2_ref[...], v2_ref[...] = _adamw(w_ref[...], g_ref[...], m_ref[...], v_ref[...])

    row = pl.BlockSpec((tr, cdim), lambda i: (i, 0))
    return pl.pallas_call(
        body, name=name, grid=(r // tr,), in_specs=[row] * 4, out_specs=[row] * 3,
        out_shape=[jax.ShapeDtypeStruct((r, cdim), F32)] * 3,
        compiler_params=_params(("parallel",)),
    )(g, w, m, v)


IN_SHARD = 609
IN_SHARD_PAD = 640
IF_AT = A_W + M_W


def _regrouped(u):
    return u if u < IF_AT + 8 else u + (IF_W - 8)


def _selection(k, rows, row0, transpose):
    shape = (rows, IN_SHARD_PAD) if transpose else (IN_SHARD_PAD, rows)
    l = lax.broadcasted_iota(jnp.int32, shape, 1 if transpose else 0)
    r = lax.broadcasted_iota(jnp.int32, shape, 0 if transpose else 1) + row0
    u = l + IN_SHARD * k
    ru = u + jnp.where(u >= IF_AT + 8, IF_W - 8, 0)
    return ((ru == r) & (l < IN_SHARD)).astype(BF16)


def _regroup_w_in(g):
    def body(g_ref, o_ref):
        for cb in range(CAT_W // LANES):
            r0 = cb * LANES
            acc = jnp.zeros((D_MODEL, LANES), F32)
            for k in range(N_DEV):
                lo, hi = _regrouped(IN_SHARD * k), _regrouped(IN_SHARD * k + IN_SHARD - 1)
                if hi >= r0 and lo < r0 + LANES:
                    acc = acc + _dot(g_ref[k], _selection(k, LANES, r0, False))
            o_ref[:, r0:r0 + LANES] = acc.astype(BF16)

    return pl.pallas_call(
        body, name="regroup_w_in", out_shape=jax.ShapeDtypeStruct((D_MODEL, CAT_W), BF16),
        compiler_params=_params(),
    )(g)


def _ungroup_w_in(g_parts):
    n = len(g_parts)

    def body(*refs):
        o_ref, g_ref = refs[n], refs[n + 1]
        at = 0
        for p in refs[:n]:
            g_ref[:, at:at + p.shape[1]] = p[...]
            at += p.shape[1]
        for k in range(N_DEV):
            lo, hi = _regrouped(IN_SHARD * k), _regrouped(IN_SHARD * k + IN_SHARD - 1)
            w0, w1 = lo // LANES * LANES, (hi // LANES + 1) * LANES
            o_ref[k] = _dot(g_ref[:, w0:w1], _selection(k, w1 - w0, w0, True)).astype(BF16)

    return pl.pallas_call(
        body, name="ungroup_w_in", out_shape=jax.ShapeDtypeStruct((N_DEV, D_MODEL, IN_SHARD_PAD), BF16),
        scratch_shapes=[pltpu.VMEM((D_MODEL, CAT_W), BF16)], compiler_params=_params(),
    )(*g_parts)


WEIGHT_NAMES = ("w_ada", "b_ada", "g_pre_mix", "g_post_mix", "w_in", "b_if", "conv_w", "conv_b", "attn_sinks",
                "mlstm_norm_w", "w_branch_attn", "w_branch_mlstm", "w_out", "g_pre_ffn", "g_post_ffn",
                "w_ffn_gate", "w_ffn_up", "w_ffn_down")


def kernel(x, c, positions, w_ada, b_ada, g_pre_mix, g_post_mix, w_in, b_if, conv_w, conv_b, attn_sinks, mlstm_norm_w, w_branch_attn, w_branch_mlstm, w_out, g_pre_ffn, g_post_ffn, w_ffn_gate, w_ffn_up, w_ffn_down, loss_target, m_w_ada, m_b_ada, m_g_pre_mix, m_g_post_mix, m_w_in, m_b_if, m_conv_w, m_conv_b, m_attn_sinks, m_mlstm_norm_w, m_w_branch_attn, m_w_branch_mlstm, m_w_out, m_g_pre_ffn, m_g_post_ffn, m_w_ffn_gate, m_w_ffn_up, m_w_ffn_down, v_w_ada, v_b_ada, v_g_pre_mix, v_g_post_mix, v_w_in, v_b_if, v_conv_w, v_conv_b, v_attn_sinks, v_mlstm_norm_w, v_w_branch_attn, v_w_branch_mlstm, v_w_out, v_g_pre_ffn, v_g_post_ffn, v_w_ffn_gate, v_w_ffn_up, v_w_ffn_down):
    given = dict(locals())
    W = {n: given[n][0] for n in WEIGHT_NAMES}
    M = {n: given["m_" + n][0] for n in WEIGHT_NAMES}
    V = {n: given["v_" + n][0] for n in WEIGHT_NAMES}
    me = 4 * lax.axis_index("x") + 2 * lax.axis_index("y") + lax.axis_index("c")

    ff_sh = D_FF // N_DEV
    g_in, g_conv, cg = _all_gather_hbm([jnp.pad(W["w_in"], ((0, 0), (0, IN_SHARD_PAD - IN_SHARD))).astype(BF16),
                                        jnp.pad(W["conv_w"], ((0, 4), (0, 0))), c.reshape(8, D_MODEL // 8)])

    cmat = cg.reshape(N_DEV, D_MODEL)
    ada_w = D_MODEL * 6 // N_DEV
    b_cols = lax.dynamic_slice(W["b_ada"], (me * ada_w,), (ada_w,)).reshape(1, ada_w)
    mod_part = _mod_partial(cmat, W["w_ada"], b_cols)
    mod_recv = _small_exchange(jnp.broadcast_to(mod_part[:, None, :], (N_DEV, 8, ada_w)), False, "scatter_mod")
    mod = mod_recv[:, 0, :].reshape(6, D_MODEL)

    st_b = _exchange_start([W["w_branch_attn"].astype(BF16), W["w_branch_mlstm"].astype(BF16), W["w_out"].astype(BF16),
                            W["w_ffn_gate"].T.astype(BF16), W["w_ffn_up"].T.astype(BF16), W["w_ffn_down"].astype(BF16)],
                           True, mod_recv, "gather_rest_start")
    cols = lambda g: g.transpose(1, 0, 2).reshape(g.shape[1], N_DEV * g.shape[2])
    gains = jnp.stack([W["g_pre_mix"], W["g_post_mix"], W["g_pre_ffn"], W["g_post_ffn"]])
    xs, tgt = x[0], loss_target[0]
    t = _tables(mod, gains, cols(g_conv)[0:4], W["conv_b"], W["b_if"], W["mlstm_norm_w"], positions[0])
    vecs = t["vecs"]
    t["vecs"] = _tie(vecs, st_b["token"], "tie_fwd")
    w_cat = _regroup_w_in(g_in)
    a = _mixer_fwd(xs, t, W["attn_sinks"], w_cat)
    g_ba, g_bm, g_out, g_gate, g_up, g_down = _exchange_wait(st_b, a["ym"], "gather_rest_wait")
    w_ba, w_bm, w_out = cols(g_ba), cols(g_bm), g_out.reshape(D_MODEL, D_MODEL)
    b = _ffn_part(xs, tgt, t, a, w_ba, w_bm, w_out, g_gate.reshape(D_FF, D_MODEL), g_up.reshape(D_FF, D_MODEL),
                  g_down.reshape(D_FF, D_MODEL))

    st_f = _exchange_start([b["g_w_gate"].reshape(N_DEV, ff_sh, D_MODEL), b["g_w_up"].reshape(N_DEV, ff_sh, D_MODEL),
                            b["g_w_down"].reshape(N_DEV, ff_sh, D_MODEL)], False, b["dx1"], "scatter_ffn_start")
    t["vecs"] = _tie(vecs, st_f["token"], "tie_bwd")
    cm = _mixer_bwd(b["dx1"], t, a, b, W["attn_sinks"], w_ba, w_bm, w_out)
    pieces = lambda g, n: g.reshape(g.shape[0], N_DEV, n).transpose(1, 0, 2)
    st_m = _exchange_start([_ungroup_w_in(cm["g_w_cat"]), pieces(cm["g_w_ba"], 128), pieces(cm["g_w_bm"], 128),
                            cm["g_w_out"].reshape(N_DEV, D_MODEL // N_DEV, D_MODEL),
                            jnp.pad(pieces(cm["dcw"][0:4], 128), ((0, 0), (0, 4), (0, 0)))], False, cm["dcw"],
                           "scatter_mixer_start")
    r_gate, r_up, r_down = _exchange_wait(st_f, st_m["token"], "scatter_ffn_wait")
    grad_x, acc_p = _pre_bwd(cm["dproj"], xs, b["dx1"], _tie(vecs, st_m["token"], "tie_pre_bwd"), w_cat)
    small = _small_grads(acc_p, b, cm)
    loss = b["loss"]

    big_out = [{} for _ in range(4)]

    def put(n, res):
        for k in range(4):
            big_out[k][n] = res[k][None]

    put("w_ffn_down", _adamw_sum(r_down, W["w_ffn_down"], M["w_ffn_down"], V["w_ffn_down"], "adamw_w_ffn_down"))
    for n, r in (("w_ffn_gate", r_gate), ("w_ffn_up", r_up)):
        put(n, [o.T for o in _adamw_sum(r, W[n].T, M[n].T, V[n].T, "adamw_" + n)])

    sg = _small_exchange(_small_table({"b_ada": small["dmod"], "gains": small["dgains"], "conv_b": small["dconv_b"],
                                       "mlstm_norm_w": small["dnorm_w"], "b_if": small["db_if"],
                                       "attn_sinks": small["dsinks"]}), True, "gather_small")
    as_row = lambda a, n: jnp.pad(a, (0, SMALL_AT[n][3] * SMALL_AT[n][1] - a.shape[0]))[None]
    small_res = _adamw_small(sg, {n: [as_row(d[n], n) for d in (W, M, V)] for n in SMALL_AT})
    small_out = [{n: small_res[n][k][:, 0:W[n].shape[0]] for n in SMALL_AT} for k in range(4)]
    dmod_cols = lax.dynamic_slice(sg[:, 0:6, :].reshape(N_DEV, 6 * D_MODEL), (0, me * ada_w), (N_DEV, ada_w))
    ada_out = _adamw_w_ada(cmat, dmod_cols, W["w_ada"], M["w_ada"], V["w_ada"])

    r_in, r_ba, r_bm, r_out, r_conv = _exchange_wait(st_m, ada_out[1], "scatter_mixer_wait")
    for n, r in (("w_branch_attn", r_ba), ("w_branch_mlstm", r_bm), ("w_out", r_out)):
        put(n, _adamw_sum(r, W[n], M[n], V[n], "adamw_" + n))
    pad4 = lambda v: jnp.pad(v, ((0, 4), (0, 0)))
    put("conv_w", [o[0:4] for o in _adamw_sum(r_conv, pad4(W["conv_w"]), pad4(M["conv_w"]), pad4(V["conv_w"]),
                                                 "adamw_conv_w")])
    g = _sum8(r_in, "sum_w_in")[:, 0:IN_SHARD].T
    put("w_in", [o.T for o in [g] + list(_adamw_plain(g, W["w_in"].T, M["w_in"].T, V["w_in"].T, "adamw_w_in"))])

    total = lax.psum(loss, ("x", "y", "c"))
    outs = [total, grad_x[None]]
    for k in range(4):
        for n in WEIGHT_NAMES:
            if n == "w_ada":
                outs.append(ada_out[k][None])
            elif n in big_out[k]:
                outs.append(big_out[k][n])
            else:
                outs.append(small_out[k][n])
    return tuple(outs)
```

```python
import functools

import jax
import jax.numpy as jnp
import numpy as np
from jax import lax
from jax.experimental import pallas as pl
from jax.experimental.pallas import tpu as pltpu

F32 = jnp.float32
BF16 = jnp.bfloat16

N_DEV = 8
D_MODEL = 1024
D_FF = 2816
N_Q_HEADS = 8
HEAD_DIM = 64
ATTN_BLOCK = 128
ROPE_THETA = 10000.0
MLSTM_HEADS = 4
MLSTM_HEAD_DIM = 128
MLSTM_CHUNK = 128
NORM_EPS = 1e-6
ADAM_LR = 0.001
ADAM_B1 = 0.9
ADAM_B2 = 0.999
ADAM_EPS = 1e-08
ADAM_WD = 0.01
ADAM_STEP = 10

ROW_TILE = 256
WIDE_TILE = 512
LANES = 128
NEG = -1e30
VMEM_LIMIT = 56 * 1024 * 1024

A_W = 768
M_W = 2048
IF_W = 128
G_W = 2048
CAT_W = A_W + M_W + IF_W + G_W

R_SHIFT_M, R_SCALE_M, R_GATE_M, R_SHIFT_F, R_SCALE_F, R_GATE_F = 0, 1, 2, 3, 4, 5
R_G_PRE_MIX, R_G_POST_MIX, R_G_PRE_FFN, R_G_POST_FFN = 6, 7, 8, 9


def _dot(a, b):
    return jnp.dot(a, b, preferred_element_type=F32)


def _dot_nt(a, b):
    return lax.dot_general(a, b, (((1,), (1,)), ((), ())), preferred_element_type=F32)


def _dot_tn(a, b):
    return lax.dot_general(a, b, (((0,), (0,)), ((), ())), preferred_element_type=F32)


def _recip(x):
    return 1.0 / x


def _sigmoid(x):
    return _recip(1.0 + jnp.exp(-x))


def _colsum(x):
    return jnp.sum(x, axis=0, keepdims=True)


def _rowmean(x):
    return jnp.mean(x, axis=-1, keepdims=True)


def _params(sem=None, vmem=VMEM_LIMIT):
    kw = dict(vmem_limit_bytes=vmem)
    if sem is not None:
        kw["dimension_semantics"] = sem
    return pltpu.CompilerParams(**kw)


def _full(shape):
    nd = len(shape)
    return pl.BlockSpec(shape, lambda *_: (0,) * nd)


def _pre_proj(x, vecs, w_cat):
    S = x.shape[0]
    tm = WIDE_TILE

    def body(x_ref, v_ref, w_ref, h_ref, pa_ref, pm_ref, pif_ref, pg_ref):
        xv = x_ref[...]
        r = lax.rsqrt(_rowmean(xv * xv) + NORM_EPS)
        h = (xv * r * v_ref[R_G_PRE_MIX:R_G_PRE_MIX + 1, :]) * (1.0 + v_ref[R_SCALE_M:R_SCALE_M + 1, :]) \
            + v_ref[R_SHIFT_M:R_SHIFT_M + 1, :]
        hb = h.astype(BF16)
        h_ref[...] = hb
        pa_ref[...] = _dot(hb, w_ref[:, 0:A_W])
        pm_ref[...] = _dot(hb, w_ref[:, A_W:A_W + M_W])
        pif_ref[...] = _dot(hb, w_ref[:, A_W + M_W:A_W + M_W + IF_W])
        pg_ref[...] = _dot(hb, w_ref[:, A_W + M_W + IF_W:CAT_W]).astype(BF16)

    row = lambda w: pl.BlockSpec((tm, w), lambda i: (i, 0))
    return pl.pallas_call(
        body, name="pre_proj", grid=(S // tm,),
        in_specs=[row(D_MODEL), _full(vecs.shape), _full(w_cat.shape)],
        out_specs=[row(D_MODEL), row(A_W), row(M_W), row(IF_W), row(G_W)],
        out_shape=[jax.ShapeDtypeStruct((S, D_MODEL), BF16), jax.ShapeDtypeStruct((S, A_W), F32),
                   jax.ShapeDtypeStruct((S, M_W), F32), jax.ShapeDtypeStruct((S, IF_W), F32),
                   jax.ShapeDtypeStruct((S, G_W), BF16)],
        compiler_params=_params(("parallel",)),
    )(x, vecs, w_cat)


def _mix_fwd(x, ya, ym, pg, vecs, w_ba, w_bm, w_out):
    S = x.shape[0]
    tm = WIDE_TILE

    def body(x_ref, ya_ref, ym_ref, pg_ref, v_ref, wba_ref, wbm_ref, wout_ref,
             x1_ref, merged_ref, mix_ref, pa_ref, pb_ref):
        pa = _dot(ya_ref[...], wba_ref[...])
        pb = _dot(ym_ref[...], wbm_ref[...])
        merged = _sigmoid(pg_ref[:, 0:D_MODEL].astype(F32)) * pa + _sigmoid(pg_ref[:, D_MODEL:G_W].astype(F32)) * pb
        mb = merged.astype(BF16)
        mix = _dot(mb, wout_ref[...])
        r = lax.rsqrt(_rowmean(mix * mix) + NORM_EPS)
        x1_ref[...] = x_ref[...] + v_ref[R_GATE_M:R_GATE_M + 1, :] * (mix * r * v_ref[R_G_POST_MIX:R_G_POST_MIX + 1, :])
        merged_ref[...] = mb
        mix_ref[...] = mix.astype(BF16)
        pa_ref[...] = pa.astype(BF16)
        pb_ref[...] = pb.astype(BF16)

    row = lambda w: pl.BlockSpec((tm, w), lambda i: (i, 0))
    sd = lambda w, dt: jax.ShapeDtypeStruct((S, w), dt)
    return pl.pallas_call(
        body, name="mix_fwd", grid=(S // tm,),
        in_specs=[row(D_MODEL), row(512), row(512), row(G_W), _full(vecs.shape), _full(w_ba.shape),
                  _full(w_bm.shape), _full(w_out.shape)],
        out_specs=[row(D_MODEL)] * 5,
        out_shape=[sd(D_MODEL, F32), sd(D_MODEL, BF16), sd(D_MODEL, BF16), sd(D_MODEL, BF16), sd(D_MODEL, BF16)],
        compiler_params=_params(("parallel",)),
    )(x, ya, ym, pg, vecs, w_ba, w_bm, w_out)


def _ffn_fwd_bwd(x1, tgt, vecs, w_gate, w_up, w_down):
    S = x1.shape[0]
    tm = ROW_TILE

    def body(x1_ref, tgt_ref, v_ref, wg_hbm, wu_hbm, wd_hbm,
             dx1_ref, h2_ref, hid_ref, da_ref, du_ref, dff_ref, acc_ref, loss_ref,
             wg, wu, wd, sem):
        i = pl.program_id(0)

        @pl.when(i == 0)
        def _():
            cps = [pltpu.make_async_copy(wg_hbm, wg, sem.at[0]), pltpu.make_async_copy(wu_hbm, wu, sem.at[1]),
                   pltpu.make_async_copy(wd_hbm, wd, sem.at[2])]
            for cp in cps:
                cp.start()
            for cp in cps:
                cp.wait()
            acc_ref[...] = jnp.zeros_like(acc_ref)
            loss_ref[...] = jnp.zeros_like(loss_ref)

        vrow = lambda r: v_ref[r:r + 1, :]
        x1v = x1_ref[...]
        r3 = lax.rsqrt(_rowmean(x1v * x1v) + NORM_EPS)
        x1hat = x1v * r3
        xn3 = x1hat * vrow(R_G_PRE_FFN)
        h2b = (xn3 * (1.0 + vrow(R_SCALE_F)) + vrow(R_SHIFT_F)).astype(BF16)
        h2_ref[...] = h2b
        a = _dot_nt(h2b, wg[...])
        u = _dot_nt(h2b, wu[...])
        sg = _sigmoid(a)
        sil = a * sg
        hidb = (sil * u).astype(BF16)
        hid_ref[...] = hidb
        ff = _dot(hidb, wd[...])
        r4 = lax.rsqrt(_rowmean(ff * ff) + NORM_EPS)
        ffhat = ff * r4
        n4 = ffhat * vrow(R_G_POST_FFN)
        err = x1v + vrow(R_GATE_F) * n4 - tgt_ref[...]
        loss_ref[...] += jnp.sum(err * err) * (0.5 / D_MODEL)
        dy = err * (1.0 / D_MODEL)
        acc_ref[0:1, :] += _colsum(dy * n4)
        dn4 = dy * vrow(R_GATE_F)
        acc_ref[1:2, :] += _colsum(dn4 * ffhat)
        dffhat = dn4 * vrow(R_G_POST_FFN)
        dffb = (r4 * (dffhat - ffhat * _rowmean(dffhat * ffhat))).astype(BF16)
        dff_ref[...] = dffb
        dhid = _dot_nt(dffb, wd[...])
        dub = (dhid * sil).astype(BF16)
        dab = (dhid * u * (sg * (1.0 + a * (1.0 - sg)))).astype(BF16)
        da_ref[...] = dab
        du_ref[...] = dub
        dh2 = _dot(dab, wg[...]) + _dot(dub, wu[...])
        acc_ref[2:3, :] += _colsum(dh2 * xn3)
        acc_ref[3:4, :] += _colsum(dh2)
        dxn3 = dh2 * (1.0 + vrow(R_SCALE_F))
        acc_ref[4:5, :] += _colsum(dxn3 * x1hat)
        dx1hat = dxn3 * vrow(R_G_PRE_FFN)
        dx1_ref[...] = dy + r3 * (dx1hat - x1hat * _rowmean(dx1hat * x1hat))

    row = lambda w: pl.BlockSpec((tm, w), lambda i: (i, 0))
    sd = lambda w, dt: jax.ShapeDtypeStruct((S, w), dt)
    anyspec = pl.BlockSpec(memory_space=pl.ANY)
    return pl.pallas_call(
        body, name="ffn_fwd_bwd", grid=(S // tm,),
        in_specs=[row(D_MODEL), row(D_MODEL), _full(vecs.shape), anyspec, anyspec, anyspec],
        out_specs=[row(D_MODEL), row(D_MODEL), row(D_FF), row(D_FF), row(D_FF), row(D_MODEL),
                   _full((8, D_MODEL)), _full((8, LANES))],
        out_shape=[sd(D_MODEL, F32), sd(D_MODEL, BF16), sd(D_FF, BF16), sd(D_FF, BF16), sd(D_FF, BF16),
                   sd(D_MODEL, BF16), jax.ShapeDtypeStruct((8, D_MODEL), F32), jax.ShapeDtypeStruct((8, LANES), F32)],
        scratch_shapes=[pltpu.VMEM(w_gate.shape, BF16), pltpu.VMEM(w_up.shape, BF16), pltpu.VMEM(w_down.shape, BF16),
                        pltpu.SemaphoreType.DMA((3,))],
        compiler_params=_params(("arbitrary",)),
    )(x1, tgt, vecs, w_gate, w_up, w_down)


def _mix_bwd(dx1, mix, pa, pb, pg, vecs, w_ba, w_bm, w_out):
    S = dx1.shape[0]
    tm = WIDE_TILE

    def body(dx1_ref, mix_ref, pa_ref, pb_ref, pg_ref, v_ref, wba_ref, wbm_ref, wout_ref,
             dmix_ref, dpa_ref, dpb_ref, dg_ref, dya_ref, dym_ref, acc_ref):
        i = pl.program_id(0)

        @pl.when(i == 0)
        def _():
            acc_ref[...] = jnp.zeros_like(acc_ref)

        vrow = lambda r: v_ref[r:r + 1, :]
        dx1v = dx1_ref[...]
        mix = mix_ref[...].astype(F32)
        r2 = lax.rsqrt(_rowmean(mix * mix) + NORM_EPS)
        mixhat = mix * r2
        acc_ref[0:1, :] += _colsum(dx1v * (mixhat * vrow(R_G_POST_MIX)))
        dn2 = dx1v * vrow(R_GATE_M)
        acc_ref[1:2, :] += _colsum(dn2 * mixhat)
        dmixhat = dn2 * vrow(R_G_POST_MIX)
        dmixb = (r2 * (dmixhat - mixhat * _rowmean(dmixhat * mixhat))).astype(BF16)
        dmix_ref[...] = dmixb
        dmerged = _dot_nt(dmixb, wout_ref[...])
        sa = _sigmoid(pg_ref[:, 0:D_MODEL].astype(F32))
        sm = _sigmoid(pg_ref[:, D_MODEL:G_W].astype(F32))
        dpab = (dmerged * sa).astype(BF16)
        dpbb = (dmerged * sm).astype(BF16)
        dpa_ref[...] = dpab
        dpb_ref[...] = dpbb
        dg_ref[:, 0:D_MODEL] = (dmerged * pa_ref[...].astype(F32) * (sa * (1.0 - sa))).astype(BF16)
        dg_ref[:, D_MODEL:G_W] = (dmerged * pb_ref[...].astype(F32) * (sm * (1.0 - sm))).astype(BF16)
        dya_ref[...] = _dot_nt(dpab, wba_ref[...]).astype(BF16)
        dym_ref[...] = _dot_nt(dpbb, wbm_ref[...])

    row = lambda w: pl.BlockSpec((tm, w), lambda i: (i, 0))
    sd = lambda w, dt: jax.ShapeDtypeStruct((S, w), dt)
    return pl.pallas_call(
        body, name="mix_bwd", grid=(S // tm,),
        in_specs=[row(D_MODEL), row(D_MODEL), row(D_MODEL), row(D_MODEL), row(G_W), _full(vecs.shape),
                  _full(w_ba.shape), _full(w_bm.shape), _full(w_out.shape)],
        out_specs=[row(D_MODEL), row(D_MODEL), row(D_MODEL), row(G_W), row(512), row(512), _full((8, D_MODEL))],
        out_shape=[sd(D_MODEL, BF16), sd(D_MODEL, BF16), sd(D_MODEL, BF16), sd(G_W, BF16), sd(512, BF16), sd(512, F32),
                   jax.ShapeDtypeStruct((8, D_MODEL), F32)],
        compiler_params=_params(("arbitrary",)),
    )(dx1, mix, pa, pb, pg, vecs, w_ba, w_bm, w_out)


def _pre_bwd(pieces, x, dx1, vecs, w_cat):
    S = x.shape[0]
    tm = WIDE_TILE
    n = len(pieces)
    starts = [sum(p.shape[1] for p in pieces[:k]) for k in range(n + 1)]

    def body(*refs):
        p_refs = refs[:n]
        x_ref, dx1_ref, v_ref, w_ref, dx_ref, acc_ref = refs[n:]
        i = pl.program_id(0)

        @pl.when(i == 0)
        def _():
            acc_ref[...] = jnp.zeros_like(acc_ref)

        vrow = lambda r: v_ref[r:r + 1, :]
        dh = _dot_nt(p_refs[0][...], w_ref[:, starts[0]:starts[1]])
        for k in range(1, n):
            dh = dh + _dot_nt(p_refs[k][...], w_ref[:, starts[k]:starts[k + 1]])
        xv = x_ref[...]
        r1 = lax.rsqrt(_rowmean(xv * xv) + NORM_EPS)
        xhat = xv * r1
        acc_ref[0:1, :] += _colsum(dh * (xhat * vrow(R_G_PRE_MIX)))
        acc_ref[1:2, :] += _colsum(dh)
        dxn = dh * (1.0 + vrow(R_SCALE_M))
        acc_ref[2:3, :] += _colsum(dxn * xhat)
        dxhat = dxn * vrow(R_G_PRE_MIX)
        dx_ref[...] = dx1_ref[...] + r1 * (dxhat - xhat * _rowmean(dxhat * xhat))

    row = lambda w: pl.BlockSpec((tm, w), lambda i: (i, 0))
    return pl.pallas_call(
        body, name="pre_bwd", grid=(S // tm,),
        in_specs=[row(p.shape[1]) for p in pieces] + [row(D_MODEL), row(D_MODEL), _full(vecs.shape), _full(w_cat.shape)],
        out_specs=[row(D_MODEL), _full((8, D_MODEL))],
        out_shape=[jax.ShapeDtypeStruct((S, D_MODEL), F32), jax.ShapeDtypeStruct((8, D_MODEL), F32)],
        compiler_params=_params(("arbitrary",)),
    )(*pieces, x, dx1, vecs, w_cat)


def _matmul_tn(a, b, tn, name, ts=1024):
    S, K = a.shape
    N = b.shape[1]
    n_s = S // ts

    def body(a_ref, b_ref, o_ref, acc_ref):
        s = pl.program_id(1)

        @pl.when(s == 0)
        def _():
            acc_ref[...] = jnp.zeros_like(acc_ref)

        acc_ref[...] += _dot_tn(a_ref[...], b_ref[...])

        @pl.when(s == n_s - 1)
        def _():
            o_ref[...] = acc_ref[...].astype(BF16)

    return pl.pallas_call(
        body, name=name, grid=(N // tn, n_s),
        in_specs=[pl.BlockSpec((ts, K), lambda j, s: (s, 0)), pl.BlockSpec((ts, tn), lambda j, s: (s, j))],
        out_specs=pl.BlockSpec((K, tn), lambda j, s: (0, j)),
        out_shape=jax.ShapeDtypeStruct((K, N), BF16),
        scratch_shapes=[pltpu.VMEM((K, tn), F32)],
        compiler_params=_params(("parallel", "arbitrary")),
    )(a, b)


def _rope_swap(t):
    lane = lax.broadcasted_iota(jnp.int32, t.shape, 1)
    first = (lane & (HEAD_DIM - 1)) < (HEAD_DIM // 2)
    return jnp.where(first, pltpu.roll(t, LANES - HEAD_DIM // 2, 1), pltpu.roll(t, HEAD_DIM // 2, 1))


def _rope(t, cos, sin_signed):
    return t * cos + _rope_swap(t) * sin_signed


def _rope_t(d, cos, sin_signed):
    return d * cos + _rope_swap(d * sin_signed)


def _to_kv_lanes(chunk, p, h):
    lane = lax.broadcasted_iota(jnp.int32, chunk.shape, 1)
    src = chunk if p == h else pltpu.roll(chunk, HEAD_DIM, 1)
    return jnp.where((lane >> 6) == h, src, jnp.zeros_like(src))


def _from_kv_lanes(o_a, o_b, h):
    lane = lax.broadcasted_iota(jnp.int32, o_a.shape, 1)
    a = o_a if h == 0 else pltpu.roll(o_a, HEAD_DIM, 1)
    b = o_b if h == 1 else pltpu.roll(o_b, HEAD_DIM, 1)
    return jnp.where(lane < HEAD_DIM, a, b)


def _band_bias(n):
    blk = ATTN_BLOCK
    qi = lax.broadcasted_iota(jnp.int32, (blk, 2 * blk), 0)
    kj = lax.broadcasted_iota(jnp.int32, (blk, 2 * blk), 1)
    seen = (kj > qi) & (kj <= qi + blk) & ((n > 0) | (kj >= blk))
    return jnp.concatenate([jnp.where(seen, 0.0, NEG)] * N_Q_HEADS, axis=0)


def _stack_heads(chunks, h, dtype):
    parts = []
    for g in range(4):
        j = 4 * h + g
        parts.append(_to_kv_lanes(chunks[j // 2], j % 2, h))
    return jnp.concatenate(parts, axis=0).astype(dtype)


def _fused_call(parts, name, n_steps):
    counts = [(len(p["in_specs"]), len(p["out_specs"]), len(p["scratch"])) for p in parts]
    n_in, n_out = sum(c[0] for c in counts), sum(c[1] for c in counts)

    def kernel_fn(*refs):
        i = pl.program_id(0)
        groups, a, b, c = [], 0, n_in, n_in + n_out
        for ci, co, cs in counts:
            groups.append(refs[a:a + ci] + refs[b:b + co] + refs[c:c + cs])
            a, b, c = a + ci, b + co, c + cs
        for p, g in zip(parts, groups):
            p["init"](i, *g)
        for p, g in zip(parts, groups):
            p["body"](i, *g)

    flat = lambda key: [v for p in parts for v in p[key]]
    res = pl.pallas_call(
        kernel_fn, name=name, grid=(n_steps,), in_specs=flat("in_specs"), out_specs=flat("out_specs"),
        out_shape=flat("out_shape"), scratch_shapes=flat("scratch"), compiler_params=_params(("arbitrary",)),
    )(*flat("operands"))
    out, pos = [], 0
    for _, co, _ in counts:
        out.append(res[pos:pos + co])
        pos += co
    return out


def _attn_fwd_part(pa, cos, sin, sinks):
    S = pa.shape[0]
    blk = ATTN_BLOCK
    nb = S // blk

    def body(n, sink_ref, cur_ref, prev_ref, cos_ref, sin_ref, cosp_ref, sinp_ref,
             ya_ref, qr_ref, kr_ref, vb_ref, lse_ref):
        cos_c, sin_c = cos_ref[...], sin_ref[...]
        qch = [_rope(cur_ref[:, c * LANES:(c + 1) * LANES], cos_c, sin_c) * (HEAD_DIM ** -0.5) for c in range(4)]
        for c in range(4):
            qr_ref[:, c * LANES:(c + 1) * LANES] = qch[c].astype(BF16)
        k_cur = _rope(cur_ref[:, 512:640], cos_c, sin_c).astype(BF16)
        k_prev = _rope(prev_ref[:, 0:LANES], cosp_ref[...], sinp_ref[...]).astype(BF16)
        v_cur = cur_ref[:, 640:768].astype(BF16)
        v_prev = prev_ref[:, LANES:2 * LANES].astype(BF16)
        kr_ref[...] = k_cur
        vb_ref[...] = v_cur
        K = jnp.concatenate([k_prev, k_cur], axis=0)
        V = jnp.concatenate([v_prev, v_cur], axis=0)
        lane = lax.broadcasted_iota(jnp.int32, (blk, LANES), 1)
        s = jnp.concatenate([_dot_nt(_stack_heads(qch, h, BF16), K) for h in range(2)], axis=0)
        s = s + _band_bias(n)
        rowmax = jnp.max(s, axis=1, keepdims=True)
        hd = lambda x, j: x[j * blk:(j + 1) * blk]
        m = jnp.concatenate([jnp.maximum(hd(rowmax, j), sink_ref[j]) for j in range(N_Q_HEADS)], axis=0)
        p = jnp.exp(s - m)
        pb = p.astype(BF16)
        den = _rowsum_mxu(pb) + jnp.concatenate([jnp.exp(sink_ref[j] - hd(m, j)) for j in range(N_Q_HEADS)], axis=0)
        o = jnp.concatenate([_dot(pb[4 * h * blk:4 * (h + 1) * blk], V) for h in range(2)], axis=0) * _recip(den)
        lse = m + jnp.log(den)
        outs = [o[j * blk:(j + 1) * blk, :] for j in range(N_Q_HEADS)]
        lse_tile = jnp.zeros((blk, LANES), F32)
        for j in range(N_Q_HEADS):
            lse_tile = jnp.where(lane == j, lse[j * blk:(j + 1) * blk, :], lse_tile)
        for c in range(4):
            ya_ref[:, c * LANES:(c + 1) * LANES] = _from_kv_lanes(outs[2 * c], outs[2 * c + 1], c // 2).astype(BF16)
        lse_ref[...] = lse_tile

    prev = lambda n: jnp.maximum(n - 1, 0)
    sd = lambda w, dt: jax.ShapeDtypeStruct((S, w), dt)
    return dict(
        init=lambda n, *refs: None, body=body, scratch=[], operands=[sinks, pa, pa, cos, sin, cos, sin],
        in_specs=[pl.BlockSpec(memory_space=pltpu.SMEM),
                  pl.BlockSpec((blk, A_W), lambda n: (n, 0)),
                  pl.BlockSpec((blk, 256), lambda n: (prev(n), 2)),
                  pl.BlockSpec((blk, LANES), lambda n: (n, 0)), pl.BlockSpec((blk, LANES), lambda n: (n, 0)),
                  pl.BlockSpec((blk, LANES), lambda n: (prev(n), 0)), pl.BlockSpec((blk, LANES), lambda n: (prev(n), 0))],
        out_specs=[pl.BlockSpec((blk, 512), lambda n: (n, 0)), pl.BlockSpec((blk, 512), lambda n: (n, 0)),
                   pl.BlockSpec((blk, LANES), lambda n: (n, 0)), pl.BlockSpec((blk, LANES), lambda n: (n, 0)),
                   pl.BlockSpec((blk, LANES), lambda n: (n, 0))],
        out_shape=[sd(512, BF16), sd(512, BF16), sd(LANES, BF16), sd(LANES, BF16), sd(LANES, F32)])


def _attn_bwd_part(dya, qr, kr, vb, lse, cos, sin, sinks):
    S = dya.shape[0]
    blk = ATTN_BLOCK
    nb = S // blk

    def init(n, sink_ref, dya_ref, qr_ref, kc_ref, kp_ref, vc_ref, vp_ref, lse_ref, cos_ref, sin_ref, cosp_ref, sinp_ref,
             dqkv_ref, last_ref, dsink_ref, ck, cv, cq):
        @pl.when(n == 0)
        def _():
            ck[...] = jnp.zeros_like(ck)
            cv[...] = jnp.zeros_like(cv)
            cq[...] = jnp.zeros_like(cq)
            dsink_ref[...] = jnp.zeros_like(dsink_ref)

    def body(n, sink_ref, dya_ref, qr_ref, kc_ref, kp_ref, vc_ref, vp_ref, lse_ref, cos_ref, sin_ref, cosp_ref, sinp_ref,
             dqkv_ref, last_ref, dsink_ref, ck, cv, cq):
        K = jnp.concatenate([kp_ref[...], kc_ref[...]], axis=0)
        V = jnp.concatenate([vp_ref[...], vc_ref[...]], axis=0)
        qch = [qr_ref[:, c * LANES:(c + 1) * LANES] for c in range(4)]
        dch = [dya_ref[:, c * LANES:(c + 1) * LANES].astype(F32) for c in range(4)]
        lse_tile = lse_ref[...]
        lane8 = lax.broadcasted_iota(jnp.int32, (8, LANES), 1)
        grp = lambda x, h: x[4 * h * blk:4 * (h + 1) * blk]
        qs = jnp.concatenate([_stack_heads(qch, h, BF16) for h in range(2)], axis=0)
        dos = jnp.concatenate([_stack_heads(dch, h, BF16) for h in range(2)], axis=0)
        lse_col = jnp.concatenate([lse_tile[:, j:j + 1] for j in range(N_Q_HEADS)], axis=0)
        s = jnp.concatenate([_dot_nt(grp(qs, h), K) for h in range(2)], axis=0)
        p = jnp.exp(s + _band_bias(n) - lse_col)
        dp = jnp.concatenate([_dot_nt(grp(dos, h), V) for h in range(2)], axis=0)
        delta = jnp.sum(p * dp, axis=1, keepdims=True)
        dsb = (p * (dp - delta)).astype(BF16)
        pb = p.astype(BF16)
        dq = jnp.concatenate([_dot(grp(dsb, h), K) for h in range(2)], axis=0)
        dk_acc = _dot_tn(grp(dsb, 0), grp(qs, 0)) + _dot_tn(grp(dsb, 1), grp(qs, 1))
        dv_acc = _dot_tn(grp(pb, 0), grp(dos, 0)) + _dot_tn(grp(pb, 1), grp(dos, 1))
        dqs = [dq[j * blk:(j + 1) * blk, :] for j in range(N_Q_HEADS)]
        dsink = jnp.zeros((8, LANES), F32)
        for j in range(N_Q_HEADS):
            rows = slice(j * blk, (j + 1) * blk)
            ps_delta = jnp.exp(sink_ref[j] - lse_col[rows]) * delta[rows]
            dsink = jnp.where(lane8 == j, dsink - jnp.sum(ps_delta), dsink)
        dsink_ref[...] += dsink
        cos_c, sin_c = cos_ref[...], sin_ref[...]
        dqkv_ref[:, 0:512] = cq[...]
        dqkv_ref[:, 512:640] = _rope_t(dk_acc[0:blk, :] + ck[...], cosp_ref[...], sinp_ref[...]).astype(BF16)
        dqkv_ref[:, 640:768] = (dv_acc[0:blk, :] + cv[...]).astype(BF16)
        ck[...] = dk_acc[blk:2 * blk, :]
        cv[...] = dv_acc[blk:2 * blk, :]
        for c in range(4):
            dqc = _from_kv_lanes(dqs[2 * c], dqs[2 * c + 1], c // 2) * (HEAD_DIM ** -0.5)
            dq_c = _rope_t(dqc, cos_c, sin_c).astype(BF16)
            cq[:, c * LANES:(c + 1) * LANES] = dq_c
            last_ref[:, c * LANES:(c + 1) * LANES] = dq_c
        last_ref[:, 512:640] = _rope_t(dk_acc[blk:2 * blk, :], cos_c, sin_c).astype(BF16)
        last_ref[:, 640:768] = dv_acc[blk:2 * blk, :].astype(BF16)

    prev = lambda n: jnp.maximum(n - 1, 0)
    same = lambda n: n
    bs = lambda w, f: pl.BlockSpec((blk, w), lambda n: (f(n), 0))
    return dict(
        init=init, body=body, operands=[sinks, dya, qr, kr, kr, vb, vb, lse, cos, sin, cos, sin],
        in_specs=[pl.BlockSpec(memory_space=pltpu.SMEM),
                  bs(512, same), bs(512, same), bs(LANES, same), bs(LANES, prev), bs(LANES, same), bs(LANES, prev),
                  bs(LANES, same), bs(LANES, same), bs(LANES, same), bs(LANES, prev), bs(LANES, prev)],
        out_specs=[bs(A_W, prev), _full((blk, A_W)), _full((8, LANES))],
        out_shape=[jax.ShapeDtypeStruct((S, A_W), BF16), jax.ShapeDtypeStruct((blk, A_W), BF16),
                   jax.ShapeDtypeStruct((8, LANES), F32)],
        scratch=[pltpu.VMEM((blk, LANES), F32), pltpu.VMEM((blk, LANES), F32), pltpu.VMEM((blk, 512), BF16)])


def _split3(x):
    hi = x.astype(BF16)
    r1 = x - hi.astype(F32)
    mid = r1.astype(BF16)
    lo = (r1 - mid.astype(F32)).astype(BF16)
    return hi, mid, lo


def _tri_matmul(tri_b, x):
    hi, mid, lo = _split3(x)
    return _dot(tri_b, hi) + _dot(tri_b, mid) + _dot(tri_b, lo)


def _log_sigmoid(x):
    return jnp.minimum(x, 0.0) - jnp.log(1.0 + jnp.exp(-jnp.abs(x)))


def _shift_rows(cur, seam, k, down):
    L = cur.shape[0]
    row8 = lax.broadcasted_iota(jnp.int32, seam.shape, 0)
    if down:
        mixed = jnp.concatenate([cur[:L - 8], jnp.where(row8 >= 8 - k, seam, cur[L - 8:])], axis=0)
        return pltpu.roll(mixed, k, 0)
    mixed = jnp.concatenate([jnp.where(row8 < k, seam, cur[:8]), cur[8:]], axis=0)
    return pltpu.roll(mixed, L - k, 0)


def _conv_fwd(cur, tail, cw_ref):
    z = cw_ref[4:5, :]
    for k in range(3, 0, -1):
        z = z + _shift_rows(cur, tail, k, True) * cw_ref[3 - k:4 - k, :]
    return z + cur * cw_ref[3:4, :]


def _rowsum_mxu(x, two_pass=False):
    ones = jnp.ones((x.shape[1], LANES), BF16)
    hi = x.astype(BF16)
    s = _dot(hi, ones)
    if two_pass:
        s = s + _dot((x - hi.astype(F32)).astype(BF16), ones)
    return s


def _stack(f):
    return jnp.concatenate([f(h) for h in range(MLSTM_HEADS)], axis=0)


def _head(x, h):
    L = x.shape[0] // MLSTM_HEADS
    return x[h * L:(h + 1) * L]


def _mlstm_heads_fwd(qk, cur_ref, gt, b_all, c_prev, nmv, tri, eye):
    L = qk.shape[0]
    HD = MLSTM_HEAD_DIM
    W4 = MLSTM_HEADS * HD
    col2row = lambda x: jnp.sum(jnp.where(eye, x, 0.0), axis=0, keepdims=True)
    b_col = _stack(lambda h: b_all[:, 4 + h:5 + h])
    i_col = _stack(lambda h: gt[:, h:h + 1])
    b_row = _stack(lambda h: jnp.broadcast_to(col2row(b_all[:, 4 + h:5 + h]), (L, L)))
    i_row = _stack(lambda h: jnp.broadcast_to(col2row(gt[:, h:h + 1]), (L, L)))
    bl = _stack(lambda h: jnp.broadcast_to(b_all[L - 1:L, 4 + h:5 + h], (L, 1)))
    m_prev = _stack(lambda h: jnp.broadcast_to(nmv[4 + h:5 + h, 0:1], (L, 1)))
    n_prev = _stack(lambda h: jnp.broadcast_to(nmv[h:h + 1, :], (L, HD)))
    tri4 = jnp.concatenate([tri] * MLSTM_HEADS, axis=0)
    Dm = jnp.where(tri4, b_col - b_row + i_row, NEG)
    inter = b_col + m_prev
    m_t = jnp.maximum(inter, jnp.max(Dm, axis=1, keepdims=True))
    W = jnp.exp(Dm - m_t)
    e_t = jnp.exp(inter - m_t)
    q = _stack(lambda h: qk[:, h * HD:(h + 1) * HD])
    k = _stack(lambda h: qk[:, W4 + h * HD:W4 + (h + 1) * HD]) * (HD ** -0.5)
    v = _stack(lambda h: cur_ref[:, 2 * W4 + h * HD:2 * W4 + (h + 1) * HD])
    qb, kb, vb = q.astype(BF16), k.astype(BF16), v.astype(BF16)
    Sc = _stack(lambda h: _dot_nt(_head(qb, h), _head(kb, h))) * W
    Scb = Sc.astype(BF16)
    cb = [c.astype(BF16) for c in c_prev]
    P1 = _stack(lambda h: _dot(_head(qb, h), cb[h]))
    num = _stack(lambda h: _dot(_head(Scb, h), _head(vb, h))) + e_t * P1
    qn = _rowsum_mxu(q * n_prev)
    den = _rowsum_mxu(Scb) + e_t * qn
    floor = jnp.broadcast_to(jnp.exp(-m_t), den.shape)
    inv_g = _recip(jnp.maximum(jnp.abs(den), floor))
    hv = num * inv_g
    a_col = bl - b_col + i_col
    a_max = _stack(lambda h: jnp.broadcast_to(jnp.max(_head(a_col, h), axis=0, keepdims=True), (L, 1)))
    m_new = jnp.maximum(bl + m_prev, a_max)
    dec = jnp.exp(bl + m_prev - m_new)
    u_col = jnp.exp(a_col - m_new)
    return dict(W=W, e_t=e_t, q=q, k=k, v=v, qb=qb, kb=kb, vb=vb, cb=cb, Sc=Sc, Scb=Scb, P1=P1, qn=qn, den=den,
                floor=floor, inv_g=inv_g, hv=hv, n_prev=n_prev, m_new=m_new, dec=dec, u_col=u_col)


def _mlstm_fwd_part(pm, pif, cw, sv):
    S = pm.shape[0]
    L = MLSTM_CHUNK
    nc = S // L
    HD = MLSTM_HEAD_DIM
    W4 = MLSTM_HEADS * HD

    def init(c, cur_ref, pif_ref, cw_ref, sv_ref, ym_ref, z_ref, cst_ref, nst_ref, C, nm, tail):
        @pl.when(c == 0)
        def _():
            C[...] = jnp.zeros_like(C)
            nm[...] = jnp.zeros_like(nm)
            tail[...] = jnp.zeros_like(tail)

    def body(c, cur_ref, pif_ref, cw_ref, sv_ref, ym_ref, z_ref, cst_ref, nst_ref, C, nm, tail):
        z = _conv_fwd(cur_ref[:, 0:2 * W4], tail[...], cw_ref)
        tail[...] = cur_ref[L - 8:L, 0:2 * W4]
        z_ref[...] = z
        qk = z * _sigmoid(z)
        gt = pif_ref[...] + sv_ref[1:2, 0:LANES]
        r_i = lax.broadcasted_iota(jnp.int32, (L, L), 0)
        c_i = lax.broadcasted_iota(jnp.int32, (L, L), 1)
        tri = c_i <= r_i
        eye = c_i == r_i
        b_all = _tri_matmul(tri.astype(BF16), _log_sigmoid(gt))
        nmv = nm[...]
        nst_ref[0] = nmv
        c_prev = [C[h] for h in range(MLSTM_HEADS)]
        f = _mlstm_heads_fwd(qk, cur_ref, gt, b_all, c_prev, nmv, tri, eye)
        hv = f["hv"]
        xc = hv - _rowsum_mxu(hv, True) * (1.0 / HD)
        hhat = xc * lax.rsqrt(_rowsum_mxu(xc * xc) * (1.0 / HD) + NORM_EPS)
        so = _sigmoid(_stack(lambda h: cur_ref[:, 3 * W4 + h * HD:3 * W4 + (h + 1) * HD]))
        wn = _stack(lambda h: jnp.broadcast_to(sv_ref[0:1, h * HD:(h + 1) * HD], (L, HD)))
        y = (so * hhat * wn).astype(BF16)
        kw = f["k"] * f["u_col"]
        kwb = kw.astype(BF16)
        n_new, m_new = [], []
        for h in range(MLSTM_HEADS):
            cst_ref[0, h] = c_prev[h]
            ym_ref[:, h * HD:(h + 1) * HD] = _head(y, h)
            dec = f["dec"][h * L:h * L + 1, :]
            C[h] = dec * c_prev[h] + _dot_tn(_head(kwb, h), _head(f["vb"], h))
            n_new.append(dec * nmv[h:h + 1, :] + _colsum(_head(kw, h)))
            m_new.append(jnp.broadcast_to(f["m_new"][h * L:h * L + 1, :], (1, LANES)))
        nm[...] = jnp.concatenate(n_new + m_new, axis=0)

    return dict(
        init=init, body=body, operands=[pm, pif, cw, sv],
        in_specs=[pl.BlockSpec((L, M_W), lambda c: (c, 0)),
                  pl.BlockSpec((L, IF_W), lambda c: (c, 0)), _full(cw.shape), _full(sv.shape)],
        out_specs=[pl.BlockSpec((L, W4), lambda c: (c, 0)), pl.BlockSpec((L, 2 * W4), lambda c: (c, 0)),
                   pl.BlockSpec((1, MLSTM_HEADS, HD, HD), lambda c: (c, 0, 0, 0)),
                   pl.BlockSpec((1, 8, LANES), lambda c: (c, 0, 0))],
        out_shape=[jax.ShapeDtypeStruct((S, W4), BF16), jax.ShapeDtypeStruct((S, 2 * W4), F32),
                   jax.ShapeDtypeStruct((nc, MLSTM_HEADS, HD, HD), F32), jax.ShapeDtypeStruct((nc, 8, LANES), F32)],
        scratch=[pltpu.VMEM((MLSTM_HEADS, HD, HD), F32), pltpu.VMEM((8, LANES), F32), pltpu.VMEM((8, 2 * W4), F32)])


def _mlstm_bwd_part(pm, zc, pif, cw, sv, dym, cst, nst):
    S = pm.shape[0]
    L = MLSTM_CHUNK
    nc = S // L
    HD = MLSTM_HEAD_DIM
    W4 = MLSTM_HEADS * HD

    def init(r, cur_ref, z_ref, pif_ref, cw_ref, sv_ref, dym_ref, cst_ref, nst_ref,
             dm_ref, dcw_ref, dsv_ref, dC, dn, dz_next, dqk):
        @pl.when(r == 0)
        def _():
            dC[...] = jnp.zeros_like(dC)
            dn[...] = jnp.zeros_like(dn)
            dz_next[...] = jnp.zeros_like(dz_next)
            dcw_ref[...] = jnp.zeros_like(dcw_ref)
            dsv_ref[...] = jnp.zeros_like(dsv_ref)

    def body(r, cur_ref, z_ref, pif_ref, cw_ref, sv_ref, dym_ref, cst_ref, nst_ref,
             dm_ref, dcw_ref, dsv_ref, dC, dn, dz_next, dqk):
        z = z_ref[...]
        sgz = _sigmoid(z)
        qk = z * sgz
        gt = pif_ref[...] + sv_ref[1:2, 0:LANES]
        r_i = lax.broadcasted_iota(jnp.int32, (L, L), 0)
        c_i = lax.broadcasted_iota(jnp.int32, (L, L), 1)
        tri = c_i <= r_i
        eye = c_i == r_i
        b_all = _tri_matmul(tri.astype(BF16), _log_sigmoid(gt))
        lane = lax.broadcasted_iota(jnp.int32, (L, LANES), 1)
        rowl = lax.broadcasted_iota(jnp.int32, (L, 1), 0)
        nmv = nst_ref[0]
        heads = range(MLSTM_HEADS)
        c_prev = [cst_ref[0, h] for h in heads]
        f = _mlstm_heads_fwd(qk, cur_ref, gt, b_all, c_prev, nmv, tri, eye)
        hv, inv_g, den, e_t, u_col, n_prev = f["hv"], f["inv_g"], f["den"], f["e_t"], f["u_col"], f["n_prev"]
        q, k, v, qb, kb, vb, Sc, Scb, W = f["q"], f["k"], f["v"], f["qb"], f["kb"], f["vb"], f["Sc"], f["Scb"], f["W"]
        xc = hv - _rowsum_mxu(hv, True) * (1.0 / HD)
        rstd = lax.rsqrt(_rowsum_mxu(xc * xc) * (1.0 / HD) + NORM_EPS)
        hhat = xc * rstd
        wn = _stack(lambda h: jnp.broadcast_to(sv_ref[0:1, h * HD:(h + 1) * HD], (L, HD)))
        so = _sigmoid(_stack(lambda h: cur_ref[:, 3 * W4 + h * HD:3 * W4 + (h + 1) * HD]))
        dy = _stack(lambda h: dym_ref[:, h * HD:(h + 1) * HD])
        d_o = (dy * hhat * wn * (so * (1.0 - so))).astype(BF16)
        dln = dy * so
        dwn = dln * hhat
        dhhat = dln * wn
        m2 = _rowsum_mxu(dhhat * hhat) * (1.0 / HD)
        dh = rstd * (dhhat - _rowsum_mxu(dhhat) * (1.0 / HD) - hhat * m2)
        dnum = dh * inv_g
        active = jnp.abs(den) > f["floor"]
        dden = jnp.where(active, -(HD * NORM_EPS) * m2 * rstd * rstd * inv_g * jnp.where(den >= 0.0, 1.0, -1.0), 0.0)
        dnumb = dnum.astype(BF16)
        dSc = _stack(lambda h: _dot_nt(_head(dnumb, h), _head(vb, h))) + dden
        dA = (dSc * W).astype(BF16)
        G = dSc * Sc
        Gb = G.astype(BF16)
        Gl = (G - Gb.astype(F32)).astype(BF16)
        ones = jnp.ones((L, LANES), BF16)
        Gr = _dot(Gb, ones) + _dot(Gl, ones)
        Gc = _stack(lambda h: _dot_tn(_head(Gb, h), ones) + _dot_tn(_head(Gl, h), ones))
        dCn = [dC[h] for h in heads]
        dCnb = [d.astype(BF16) for d in dCn]
        dnv = dn[...]
        dn_new = _stack(lambda h: jnp.broadcast_to(dnv[h:h + 1, :], (L, HD)))
        kdC = _stack(lambda h: _dot(_head(kb, h), dCnb[h]))
        vdC = _stack(lambda h: _dot_nt(_head(vb, h), dCnb[h]))
        dv = (_stack(lambda h: _dot_tn(_head(Scb, h), _head(dnumb, h))) + u_col * kdC).astype(BF16)
        dq = _stack(lambda h: _dot(_head(dA, h), _head(kb, h))) \
            + e_t * _stack(lambda h: _dot_nt(_head(dnumb, h), f["cb"][h])) + (e_t * dden) * n_prev
        dk = (_stack(lambda h: _dot_tn(_head(dA, h), _head(qb, h))) + u_col * (vdC + dn_new)) * (HD ** -0.5)
        E = (_rowsum_mxu(f["P1"] * dnum, True) + dden * f["qn"]) * e_t
        U = _rowsum_mxu(kdC * v + k * dn_new, True) * u_col
        qe = (q * e_t).astype(BF16)
        qd = (e_t * dden) * q
        di = Gc + U
        db = Gr + E - Gc - U
        di_tile = jnp.zeros((L, LANES), F32)
        db_tile = jnp.zeros((L, LANES), F32)
        dn_rows = []
        for h in heads:
            dec = f["dec"][h * L:h * L + 1, :]
            ddec = jnp.sum(dCn[h] * c_prev[h]) + jnp.sum(dnv[h:h + 1, :] * nmv[h:h + 1, :])
            dbl = ddec * dec + jnp.sum(_head(U, h), axis=0, keepdims=True)
            di_tile = jnp.where(lane == h, _head(di, h), di_tile)
            db_tile = jnp.where(lane == 4 + h, _head(db, h) + jnp.where(rowl == L - 1, dbl, 0.0), db_tile)
            dC[h] = dec * dCn[h] + _dot_tn(_head(qe, h), _head(dnumb, h))
            dn_rows.append(dec * dnv[h:h + 1, :] + _colsum(_head(qd, h)))
            dsv_ref[0:1, h * HD:(h + 1) * HD] += _colsum(_head(dwn, h))
            dqk[:, h * HD:(h + 1) * HD] = _head(dq, h)
            dqk[:, W4 + h * HD:W4 + (h + 1) * HD] = _head(dk, h)
            dm_ref[:, 2 * W4 + h * HD:2 * W4 + (h + 1) * HD] = _head(dv, h)
            dm_ref[:, 3 * W4 + h * HD:3 * W4 + (h + 1) * HD] = _head(d_o, h)
        dn[...] = jnp.concatenate(dn_rows + [jnp.zeros((8 - MLSTM_HEADS, LANES), F32)], axis=0)
        dlf = _tri_matmul((r_i <= c_i).astype(BF16), db_tile)
        dif = jnp.where(lane < 4, di_tile, jnp.where(lane < 8, dlf * (1.0 - _sigmoid(gt)), 0.0))
        dm_ref[:, M_W:M_W + IF_W] = dif.astype(BF16)
        dsv_ref[1:2, 0:LANES] += _colsum(dif)
        dz = dqk[...] * (sgz * (1.0 + z * (1.0 - sgz)))
        dcw_ref[4:5, :] += _colsum(dz)
        u = cur_ref[:, 0:2 * W4]
        du_in = dz * cw_ref[3:4, :]
        dcw_ref[3:4, :] += _colsum(dz * u)
        for k in range(1, 4):
            up = _shift_rows(dz, dz_next[...], k, False)
            dcw_ref[3 - k:4 - k, :] += _colsum(up * u)
            du_in = du_in + up * cw_ref[3 - k:4 - k, :]
        dz_next[...] = dz[0:8, :]
        dm_ref[:, 0:2 * W4] = du_in.astype(BF16)

    cidx = lambda r: nc - 1 - r
    return dict(
        init=init, body=body, operands=[pm, zc, pif, cw, sv, dym, cst, nst],
        in_specs=[pl.BlockSpec((L, M_W), lambda r: (cidx(r), 0)), pl.BlockSpec((L, 2 * W4), lambda r: (cidx(r), 0)),
                  pl.BlockSpec((L, IF_W), lambda r: (cidx(r), 0)), _full(cw.shape), _full(sv.shape),
                  pl.BlockSpec((L, W4), lambda r: (cidx(r), 0)),
                  pl.BlockSpec((1, MLSTM_HEADS, HD, HD), lambda r: (cidx(r), 0, 0, 0)),
                  pl.BlockSpec((1, 8, LANES), lambda r: (cidx(r), 0, 0))],
        out_specs=[pl.BlockSpec((L, M_W + IF_W), lambda r: (cidx(r), 0)), _full((8, 2 * W4)), _full((8, W4))],
        out_shape=[jax.ShapeDtypeStruct((S, M_W + IF_W), BF16),
                   jax.ShapeDtypeStruct((8, 2 * W4), F32), jax.ShapeDtypeStruct((8, W4), F32)],
        scratch=[pltpu.VMEM((MLSTM_HEADS, HD, HD), F32), pltpu.VMEM((8, LANES), F32),
                 pltpu.VMEM((8, 2 * W4), F32), pltpu.VMEM((L, 2 * W4), F32)])


def _rope_tables(positions):
    half = HEAD_DIM // 2
    inv_freq = ROPE_THETA ** (-2.0 * jnp.arange(half, dtype=F32) / HEAD_DIM)
    ang = positions.astype(F32)[:, None] * inv_freq
    cos = jnp.tile(jnp.cos(ang), (1, LANES // half))
    sign = jnp.tile(jnp.concatenate([-jnp.ones((half,), F32), jnp.ones((half,), F32)]), LANES // HEAD_DIM)
    sin = jnp.tile(jnp.sin(ang), (1, LANES // half)) * sign
    return cos, sin


def _local_step(x, tgt, positions, mod, gains, w_cat, w_ba, w_bm, w_out, w_gate, w_up, w_down,
                conv_w, conv_b, b_if, sinks, norm_w):
    t = _tables(mod, gains, conv_w, conv_b, b_if, norm_w, positions)
    a = _mixer_fwd(x, t, sinks, w_cat)
    b = _ffn_part(x, tgt, t, a, w_ba, w_bm, w_out, w_gate, w_up, w_down)
    c = _mixer_bwd(b["dx1"], t, a, b, sinks, w_ba, w_bm, w_out)
    grad_x, acc_p = _pre_bwd(c["dproj"], x, b["dx1"], t["vecs"], w_cat)
    big = dict(w_cat=jnp.concatenate(c["g_w_cat"], axis=1), w_ba=c["g_w_ba"], w_bm=c["g_w_bm"], w_out=c["g_w_out"], w_gate=b["g_w_gate"],
               w_up=b["g_w_up"], w_down=b["g_w_down"])
    return b["loss"], grad_x, big, _small_grads(acc_p, b, c)


def _tables(mod, gains, conv_w, conv_b, b_if, norm_w, positions):
    cos, sin = _rope_tables(positions)
    return dict(
        vecs=jnp.concatenate([mod, gains, jnp.zeros((6, D_MODEL), F32)], axis=0),
        cw=jnp.concatenate([conv_w, conv_b.reshape(1, -1), jnp.zeros((3, 2 * 512), F32)], axis=0),
        sv=jnp.zeros((8, 512), F32).at[0].set(norm_w).at[1, 0:8].set(b_if), cos=cos, sin=sin)


def _mixer_fwd(x, t, sinks, w_cat):
    h, pa, pm, pif, pg = _pre_proj(x, t["vecs"], w_cat)
    n_blk = x.shape[0] // ATTN_BLOCK
    (ya, qr, kr, vb, lse), = _fused_call([_attn_fwd_part(pa, t["cos"], t["sin"], sinks)], "attn_fwd", n_blk)
    (ym, zc, cst, nst), = _fused_call([_mlstm_fwd_part(pm, pif, t["cw"], t["sv"])], "mlstm_fwd", n_blk)
    return dict(h=h, pm=pm, pif=pif, pg=pg, ya=ya, qr=qr, kr=kr, vb=vb, lse=lse, ym=ym, zc=zc, cst=cst, nst=nst)


def _ffn_part(x, tgt, t, a, w_ba, w_bm, w_out, w_gate, w_up, w_down):
    x1, merged, mix, pba, pbm = _mix_fwd(x, a["ya"], a["ym"], a["pg"], t["vecs"], w_ba, w_bm, w_out)
    dx1, h2, hid, da, du, dff, acc_f, loss = _ffn_fwd_bwd(x1, tgt, t["vecs"], w_gate, w_up, w_down)
    return dict(merged=merged, mix=mix, pba=pba, pbm=pbm, dx1=dx1, acc_f=acc_f, loss=loss[0, 0],
                g_w_gate=_matmul_tn(da, h2, 1024, "dw_ffn_gate"),
                g_w_up=_matmul_tn(du, h2, 1024, "dw_ffn_up"),
                g_w_down=_matmul_tn(hid, dff, 1024, "dw_ffn_down"))


def _mixer_bwd(dx1, t, a, b, sinks, w_ba, w_bm, w_out):
    dmix, dpa, dpb, dg, dya, dym, acc_m = _mix_bwd(dx1, b["mix"], b["pba"], b["pbm"], a["pg"], t["vecs"], w_ba, w_bm, w_out)
    g_w_out = _matmul_tn(b["merged"], dmix, 1024, "dw_out")
    g_w_ba = _matmul_tn(a["ya"], dpa, 1024, "dw_branch_attn")
    g_w_bm = _matmul_tn(a["ym"], dpb, 1024, "dw_branch_mlstm")
    n_blk = dx1.shape[0] // ATTN_BLOCK
    (dqkv, dqkv_last, dsink), = _fused_call(
        [_attn_bwd_part(dya, a["qr"], a["kr"], a["vb"], a["lse"], t["cos"], t["sin"], sinks)], "attn_bwd", n_blk)
    (dm, dcw, dsv), = _fused_call(
        [_mlstm_bwd_part(a["pm"], a["zc"], a["pif"], t["cw"], t["sv"], dym, a["cst"], a["nst"])], "mlstm_bwd", n_blk)
    dqkv = lax.dynamic_update_slice(dqkv, dqkv_last, (dqkv.shape[0] - ATTN_BLOCK, 0))
    dproj = [dqkv, dm, dg]
    g_w_cat = [_matmul_tn(a["h"], p, 1024 if p.shape[1] % 1024 == 0 else p.shape[1], "dw_in_" + n)
               for p, n in zip(dproj, ("attn", "mlstm", "branch"))]
    return dict(dproj=dproj, g_w_cat=g_w_cat, g_w_out=g_w_out, g_w_ba=g_w_ba,
                g_w_bm=g_w_bm, acc_m=acc_m, dsink=dsink, dcw=dcw, dsv=dsv)


def _small_grads(acc_p, b, c):
    acc_f, acc_m = b["acc_f"], c["acc_m"]
    dmod = jnp.stack([acc_p[1], acc_p[0], acc_m[0], acc_f[3], acc_f[2], acc_f[0]])
    dgains = jnp.stack([acc_p[2], acc_m[1], acc_f[4], acc_f[1]])
    return dict(dmod=dmod, dgains=dgains, dconv_w=c["dcw"][0:4], dconv_b=c["dcw"][4], db_if=c["dsv"][1, 0:8],
                dsinks=c["dsink"][0, 0:8], dnorm_w=c["dsv"][0])


MESH_ID = pl.DeviceIdType.MESH


def _mesh_pos():
    return lax.axis_index("x"), lax.axis_index("y"), lax.axis_index("c")


def _flip(v, bit):
    return 1 - v if bit else v


def _relations():
    return [((r >> 2) & 1, (r >> 1) & 1, r & 1) for r in range(1, N_DEV)]


def _small_exchange(p, gather, name):
    R, V = p.shape[-2:]

    def body(p_ref, out_ref, send_sems, recv_sems):
        x, y, c = _mesh_pos()
        me = 4 * x + 2 * y + c
        out_ref[me] = p_ref[...] if gather else p_ref[me]
        peers = []
        for dx, dy, dc in _relations():
            px, py, pc = _flip(x, dx), _flip(y, dy), _flip(c, dc)
            peers.append(((px, py, pc), 4 * px + 2 * py + pc))

        def copy(k, landing):
            peer, pid = peers[k]
            return pltpu.make_async_remote_copy(
                src_ref=p_ref if gather else p_ref.at[pid], dst_ref=out_ref.at[landing],
                send_sem=send_sems.at[k], recv_sem=recv_sems.at[k], device_id=peer, device_id_type=MESH_ID)

        sends = [copy(k, me) for k in range(N_DEV - 1)]
        for cp in sends:
            cp.start()
        for k in range(N_DEV - 1):
            copy(k, peers[k][1]).wait_recv()
        for cp in sends:
            cp.wait_send()

    vm = pl.BlockSpec(memory_space=pltpu.VMEM)
    return pl.pallas_call(
        body, name=name, in_specs=[vm], out_specs=vm,
        out_shape=jax.ShapeDtypeStruct((N_DEV, R, V), F32),
        scratch_shapes=[pltpu.SemaphoreType.DMA((N_DEV - 1,)), pltpu.SemaphoreType.DMA((N_DEV - 1,))],
        compiler_params=pltpu.CompilerParams(vmem_limit_bytes=VMEM_LIMIT),
    )(p)


HBM_SPEC = pl.BlockSpec(memory_space=pltpu.HBM)
SEM_SPEC = pl.BlockSpec(memory_space=pltpu.SEMAPHORE)


def _peers(x, y, c):
    out = []
    for dx, dy, dc in _relations():
        px, py, pc = _flip(x, dx), _flip(y, dy), _flip(c, dc)
        out.append(((px, py, pc), 4 * px + 2 * py + pc))
    return out


def _exchange_start(arrs, gather, after, name):
    n = len(arrs)
    me_out = 4 * lax.axis_index("x") + 2 * lax.axis_index("y") + lax.axis_index("c")
    lands = []
    for a in arrs:
        own = a[None] if gather else lax.dynamic_index_in_dim(a, me_out, 0, keepdims=True)
        empty = lax.empty(((N_DEV,) + a.shape) if gather else a.shape, a.dtype)
        lands.append(lax.dynamic_update_index_in_dim(empty, own, me_out, 0))

    def body(*refs):
        a_refs, l_refs = refs[:n], refs[n:2 * n]
        send_sems, recv_sems = refs[2 * n + 1], refs[2 * n + 2]
        token = refs[4 * n + 3]
        x, y, c = _mesh_pos()
        me = 4 * x + 2 * y + c
        for a in range(n):
            for k, (peer, pid) in enumerate(_peers(x, y, c)):
                pltpu.make_async_remote_copy(
                    src_ref=a_refs[a] if gather else a_refs[a].at[pid], dst_ref=l_refs[a].at[me],
                    send_sem=send_sems.at[a * (N_DEV - 1) + k], recv_sem=recv_sems.at[a * (N_DEV - 1) + k],
                    device_id=peer, device_id_type=MESH_ID).start()
        token[...] = jnp.zeros_like(token)

    sem = pltpu.SemaphoreType.DMA((n * (N_DEV - 1),))
    hbm = lambda a: pltpu.with_memory_space_constraint(a, pltpu.HBM)
    res = pl.pallas_call(
        body, name=name,
        out_shape=(sem, sem, *[pltpu.HBM(a.shape, a.dtype) for a in arrs], *[pltpu.HBM(l.shape, l.dtype) for l in lands],
                   jax.ShapeDtypeStruct((8, LANES), F32)),
        in_specs=[HBM_SPEC] * (2 * n) + [pl.BlockSpec(memory_space=pl.ANY)],
        out_specs=(SEM_SPEC, SEM_SPEC, *[HBM_SPEC] * (2 * n), pl.BlockSpec(memory_space=pltpu.VMEM)),
        input_output_aliases={i: 2 + i for i in range(2 * n)},
        compiler_params=pltpu.CompilerParams(has_side_effects=pltpu.SideEffectType.DATAFLOW_SIDE_EFFECTING),
    )(*[hbm(a) for a in arrs], *[hbm(l) for l in lands], after)
    return dict(sems=res[0:2], arrs=res[2:2 + n], lands=res[2 + n:2 + 2 * n], token=res[2 + 2 * n], gather=gather)


def _exchange_wait(st, after, name):
    n = len(st["arrs"])
    gather = st["gather"]

    def body(*refs):
        a_refs, l_refs = refs[:n], refs[n:2 * n]
        send_sems, recv_sems = refs[2 * n], refs[2 * n + 1]
        x, y, c = _mesh_pos()
        for a in range(n):
            for k, (peer, pid) in enumerate(_peers(x, y, c)):
                cp = pltpu.make_async_remote_copy(
                    src_ref=a_refs[a] if gather else a_refs[a].at[pid], dst_ref=l_refs[a].at[pid],
                    send_sem=send_sems.at[a * (N_DEV - 1) + k], recv_sem=recv_sems.at[a * (N_DEV - 1) + k],
                    device_id=peer, device_id_type=MESH_ID)
                cp.wait_send()
                cp.wait_recv()

    both = list(st["arrs"]) + list(st["lands"])
    res = pl.pallas_call(
        body, name=name, out_shape=[pltpu.HBM(a.shape, a.dtype) for a in both],
        in_specs=[HBM_SPEC] * (2 * n) + [SEM_SPEC, SEM_SPEC, pl.BlockSpec(memory_space=pl.ANY)],
        out_specs=[HBM_SPEC] * (2 * n), input_output_aliases={i: i for i in range(2 * n)},
        compiler_params=pltpu.CompilerParams(has_side_effects=pltpu.SideEffectType.DATAFLOW_SIDE_EFFECTING),
    )(*both, *st["sems"], after)
    return res[n:2 * n]


def _tie(x, token, name):
    def body(x_ref, t_ref, o_ref):
        o_ref[...] = x_ref[...]

    vm = pl.BlockSpec(memory_space=pltpu.VMEM)
    return pl.pallas_call(
        body, name=name, in_specs=[vm, pl.BlockSpec(memory_space=pl.ANY)], out_specs=vm,
        out_shape=jax.ShapeDtypeStruct(x.shape, x.dtype),
    )(x, token)


def _all_gather_hbm(shards):
    n = len(shards)

    def body(*refs):
        p_refs, out_refs = refs[:n], refs[n:2 * n]
        send_sems, recv_sems, local_sems = refs[2 * n:]
        x, y, c = _mesh_pos()
        me, sibling = (x, y, c), (x, y, 1 - c)
        chips = [(1 - x, y), (x, 1 - y), (1 - x, 1 - y)]

        def copy(a, k, block, to, own=False):
            slot = out_refs[a].at[4 * block[0] + 2 * block[1] + block[2]]
            return pltpu.make_async_remote_copy(
                src_ref=p_refs[a] if own else slot, dst_ref=slot,
                send_sem=send_sems.at[a, k], recv_sem=recv_sems.at[a, k], device_id=to, device_id_type=MESH_ID)

        mine = [pltpu.make_async_copy(p_refs[a], out_refs[a].at[4 * x + 2 * y + c], local_sems.at[a]) for a in range(n)]
        for cp in mine:
            cp.start()
        first = []
        for a in range(n):
            first.append(copy(a, 0, me, sibling, own=True))
            first += [copy(a, 1 + j, me, (*chip, c), own=True) for j, chip in enumerate(chips)]
        for cp in first:
            cp.start()
        passed = []
        for j, chip in enumerate(chips):
            for a in range(n):
                copy(a, 1 + j, (*chip, c), me).wait_recv()
                passed.append(copy(a, 4 + j, (*chip, c), sibling))
                passed[-1].start()
        for a in range(n):
            copy(a, 0, sibling, me).wait_recv()
            for j, chip in enumerate(chips):
                copy(a, 4 + j, (*chip, 1 - c), me).wait_recv()
        for cp in first + passed:
            cp.wait_send()
        for cp in mine:
            cp.wait()

    hbm = pl.BlockSpec(memory_space=pl.ANY)
    return pl.pallas_call(
        body, name="gather_weights", in_specs=[hbm] * n, out_specs=[hbm] * n,
        out_shape=[jax.ShapeDtypeStruct((N_DEV,) + s.shape, s.dtype) for s in shards],
        scratch_shapes=[pltpu.SemaphoreType.DMA((n, N_DEV - 1)), pltpu.SemaphoreType.DMA((n, N_DEV - 1)),
                        pltpu.SemaphoreType.DMA((n,))],
    )(*shards)


def _adamw(w, g, m, v):
    m2 = ADAM_B1 * m + (1.0 - ADAM_B1) * g
    v2 = ADAM_B2 * v + (1.0 - ADAM_B2) * (g * g)
    m_hat = m2 / (1.0 - ADAM_B1 ** ADAM_STEP)
    v_hat = v2 / (1.0 - ADAM_B2 ** ADAM_STEP)
    delta = -ADAM_LR * (m_hat / (jnp.sqrt(v_hat) + ADAM_EPS) + ADAM_WD * w)
    return delta, m2, v2


def _mod_partial(cmat, w_shard, b_shard):
    def body(c_ref, w_ref, b_ref, o_ref):
        o_ref[...] = _dot(c_ref[...].astype(BF16), w_ref[...].astype(BF16)) + b_ref[...]

    return pl.pallas_call(
        body, name="mod_partial", out_shape=jax.ShapeDtypeStruct((N_DEV, w_shard.shape[1]), F32),
        compiler_params=_params(),
    )(cmat, w_shard, b_shard)


def _adamw_w_ada(cmat, dmod_cols, w, m, v):
    r, cdim = w.shape
    tr = _row_tile(r)

    def body(c_ref, d_ref, w_ref, m_ref, v_ref, g_ref, dl_ref, m2_ref, v2_ref):
        g = _dot_tn(c_ref[...].astype(BF16), d_ref[...].astype(BF16))
        g_ref[...] = g
        dl_ref[...], m2_ref[...], v2_ref[...] = _adamw(w_ref[...], g, m_ref[...], v_ref[...])

    row = pl.BlockSpec((tr, cdim), lambda i: (i, 0))
    return pl.pallas_call(
        body, name="adamw_w_ada", grid=(r // tr,),
        in_specs=[pl.BlockSpec((N_DEV, tr), lambda i: (0, i)), _full(dmod_cols.shape), row, row, row],
        out_specs=[row] * 4, out_shape=[jax.ShapeDtypeStruct(w.shape, F32)] * 4,
        compiler_params=_params(("parallel",)),
    )(cmat, dmod_cols, w, m, v)


SMALL_ROWS = 16
SMALL_AT = {"b_ada": (0, 6, 0, D_MODEL), "g_pre_mix": (6, 1, 0, D_MODEL), "g_post_mix": (7, 1, 0, D_MODEL),
            "g_pre_ffn": (8, 1, 0, D_MODEL), "g_post_ffn": (9, 1, 0, D_MODEL), "conv_b": (10, 1, 0, D_MODEL),
            "mlstm_norm_w": (11, 1, 0, 512), "b_if": (11, 1, 512, LANES), "attn_sinks": (11, 1, 640, LANES)}


def _small_table(part):
    tail = jnp.concatenate([part["mlstm_norm_w"], jnp.pad(part["b_if"], (0, LANES - 8)),
                            jnp.pad(part["attn_sinks"], (0, LANES - 8)), jnp.zeros((256,), F32)])
    return jnp.concatenate([part["b_ada"], part["gains"], part["conv_b"][None], tail[None],
                            jnp.zeros((SMALL_ROWS - 12, D_MODEL), F32)], axis=0)


def _adamw_small(gathered, wmv):
    names = list(SMALL_AT)

    def body(*refs):
        g_ref, ins, outs = refs[0], refs[1:1 + 3 * len(names)], refs[1 + 3 * len(names):]
        g = g_ref[0]
        for k in range(1, N_DEV):
            g = g + g_ref[k]
        for i, n in enumerate(names):
            r0, rows, l0, lanes = SMALL_AT[n]
            gi = jnp.concatenate([g[r:r + 1, l0:l0 + lanes] for r in range(r0, r0 + rows)], axis=1)
            w_ref, m_ref, v_ref = ins[3 * i:3 * i + 3]
            go, dl, m2, v2 = outs[4 * i:4 * i + 4]
            go[...] = gi
            dl[...], m2[...], v2[...] = _adamw(w_ref[...], gi, m_ref[...], v_ref[...])

    flat = [a for n in names for a in wmv[n]]
    res = pl.pallas_call(
        body, name="adamw_small",
        out_shape=[jax.ShapeDtypeStruct(wmv[n][0].shape, F32) for n in names for _ in range(4)],
        compiler_params=_params(),
    )(gathered, *flat)
    return {n: res[4 * i:4 * i + 4] for i, n in enumerate(names)}


def _row_tile(rows):
    return rows // 4 if rows >= 512 and rows % 64 == 0 else rows


def _sum_partials(r_ref):
    g = r_ref[0].astype(F32)
    for k in range(1, N_DEV):
        g = g + r_ref[k].astype(F32)
    return g


def _adamw_sum(recv, w, m, v, name):
    r, cdim = w.shape
    tr = _row_tile(r)

    def body(r_ref, w_ref, m_ref, v_ref, g_ref, dl_ref, m2_ref, v2_ref):
        g = _sum_partials(r_ref)
        g_ref[...] = g
        dl_ref[...], m2_ref[...], v2_ref[...] = _adamw(w_ref[...], g, m_ref[...], v_ref[...])

    row = pl.BlockSpec((tr, cdim), lambda i: (i, 0))
    return pl.pallas_call(
        body, name=name, grid=(r // tr,),
        in_specs=[pl.BlockSpec((N_DEV, tr, cdim), lambda i: (0, i, 0)), row, row, row],
        out_specs=[row] * 4, out_shape=[jax.ShapeDtypeStruct((r, cdim), F32)] * 4,
        compiler_params=_params(("parallel",)),
    )(recv, w, m, v)


def _sum8(recv, name):
    _, r, cdim = recv.shape
    tr = _row_tile(r)

    def body(r_ref, g_ref):
        g_ref[...] = _sum_partials(r_ref)

    return pl.pallas_call(
        body, name=name, grid=(r // tr,),
        in_specs=[pl.BlockSpec((N_DEV, tr, cdim), lambda i: (0, i, 0))],
        out_specs=pl.BlockSpec((tr, cdim), lambda i: (i, 0)), out_shape=jax.ShapeDtypeStruct((r, cdim), F32),
        compiler_params=_params(("parallel",)),
    )(recv)


def _adamw_plain(g, w, m, v, name):
    r, cdim = w.shape
    tr = _row_tile(r)

    def body(g_ref, w_ref, m_ref, v_ref, dl_ref, m2_ref, v2_ref):
        dl_ref[...], m2_ref[...], v2_ref[...] = _adamw(w_ref[...], g_ref[...], m_ref[...], v_ref[...])

    row = pl.BlockSpec((tr, cdim), lambda i: (i, 0))
    return pl.pallas_call(
        body, name=name, grid=(r // tr,), in_specs=[row] * 4, out_specs=[row] * 3,
        out_shape=[jax.ShapeDtypeStruct((r, cdim), F32)] * 3,
        compiler_params=_params(("parallel",)),
    )(g, w, m, v)


IN_SHARD = 609
IN_SHARD_PAD = 640
IF_AT = A_W + M_W


def _regrouped(u):
    return u if u < IF_AT + 8 else u + (IF_W - 8)


def _selection(k, rows, row0, transpose):
    shape = (rows, IN_SHARD_PAD) if transpose else (IN_SHARD_PAD, rows)
    l = lax.broadcasted_iota(jnp.int32, shape, 1 if transpose else 0)
    r = lax.broadcasted_iota(jnp.int32, shape, 0 if transpose else 1) + row0
    u = l + IN_SHARD * k
    ru = u + jnp.where(u >= IF_AT + 8, IF_W - 8, 0)
    return ((ru == r) & (l < IN_SHARD)).astype(BF16)


def _regroup_w_in(g):
    def body(g_ref, o_ref):
        for cb in range(CAT_W // LANES):
            r0 = cb * LANES
            acc = jnp.zeros((D_MODEL, LANES), F32)
            for k in range(N_DEV):
                lo, hi = _regrouped(IN_SHARD * k), _regrouped(IN_SHARD * k + IN_SHARD - 1)
                if hi >= r0 and lo < r0 + LANES:
                    acc = acc + _dot(g_ref[k], _selection(k, LANES, r0, False))
            o_ref[:, r0:r0 + LANES] = acc.astype(BF16)

    return pl.pallas_call(
        body, name="regroup_w_in", out_shape=jax.ShapeDtypeStruct((D_MODEL, CAT_W), BF16),
        compiler_params=_params(),
    )(g)


def _ungroup_w_in(g_parts):
    n = len(g_parts)

    def body(*refs):
        o_ref, g_ref = refs[n], refs[n + 1]
        at = 0
        for p in refs[:n]:
            g_ref[:, at:at + p.shape[1]] = p[...]
            at += p.shape[1]
        for k in range(N_DEV):
            lo, hi = _regrouped(IN_SHARD * k), _regrouped(IN_SHARD * k + IN_SHARD - 1)
            w0, w1 = lo // LANES * LANES, (hi // LANES + 1) * LANES
            o_ref[k] = _dot(g_ref[:, w0:w1], _selection(k, w1 - w0, w0, True)).astype(BF16)

    return pl.pallas_call(
        body, name="ungroup_w_in", out_shape=jax.ShapeDtypeStruct((N_DEV, D_MODEL, IN_SHARD_PAD), BF16),
        scratch_shapes=[pltpu.VMEM((D_MODEL, CAT_W), BF16)], compiler_params=_params(),
    )(*g_parts)


WEIGHT_NAMES = ("w_ada", "b_ada", "g_pre_mix", "g_post_mix", "w_in", "b_if", "conv_w", "conv_b", "attn_sinks",
                "mlstm_norm_w", "w_branch_attn", "w_branch_mlstm", "w_out", "g_pre_ffn", "g_post_ffn",
                "w_ffn_gate", "w_ffn_up", "w_ffn_down")


def kernel(x, c, positions, w_ada, b_ada, g_pre_mix, g_post_mix, w_in, b_if, conv_w, conv_b, attn_sinks, mlstm_norm_w, w_branch_attn, w_branch_mlstm, w_out, g_pre_ffn, g_post_ffn, w_ffn_gate, w_ffn_up, w_ffn_down, loss_target, m_w_ada, m_b_ada, m_g_pre_mix, m_g_post_mix, m_w_in, m_b_if, m_conv_w, m_conv_b, m_attn_sinks, m_mlstm_norm_w, m_w_branch_attn, m_w_branch_mlstm, m_w_out, m_g_pre_ffn, m_g_post_ffn, m_w_ffn_gate, m_w_ffn_up, m_w_ffn_down, v_w_ada, v_b_ada, v_g_pre_mix, v_g_post_mix, v_w_in, v_b_if, v_conv_w, v_conv_b, v_attn_sinks, v_mlstm_norm_w, v_w_branch_attn, v_w_branch_mlstm, v_w_out, v_g_pre_ffn, v_g_post_ffn, v_w_ffn_gate, v_w_ffn_up, v_w_ffn_down):
    given = dict(locals())
    W = {n: given[n][0] for n in WEIGHT_NAMES}
    M = {n: given["m_" + n][0] for n in WEIGHT_NAMES}
    V = {n: given["v_" + n][0] for n in WEIGHT_NAMES}
    me = 4 * lax.axis_index("x") + 2 * lax.axis_index("y") + lax.axis_index("c")

    ff_sh = D_FF // N_DEV
    g_in, g_conv, cg = _all_gather_hbm([jnp.pad(W["w_in"], ((0, 0), (0, IN_SHARD_PAD - IN_SHARD))).astype(BF16),
                                        jnp.pad(W["conv_w"], ((0, 4), (0, 0))), c.reshape(8, D_MODEL // 8)])

    cmat = cg.reshape(N_DEV, D_MODEL)
    ada_w = D_MODEL * 6 // N_DEV
    b_cols = lax.dynamic_slice(W["b_ada"], (me * ada_w,), (ada_w,)).reshape(1, ada_w)
    mod_part = _mod_partial(cmat, W["w_ada"], b_cols)
    mod_recv = _small_exchange(jnp.broadcast_to(mod_part[:, None, :], (N_DEV, 8, ada_w)), False, "scatter_mod")
    mod = mod_recv[:, 0, :].reshape(6, D_MODEL)

    st_b = _exchange_start([W["w_branch_attn"].astype(BF16), W["w_branch_mlstm"].astype(BF16), W["w_out"].astype(BF16),
                            W["w_ffn_gate"].T.astype(BF16), W["w_ffn_up"].T.astype(BF16), W["w_ffn_down"].astype(BF16)],
                           True, mod_recv, "gather_rest_start")
    cols = lambda g: g.transpose(1, 0, 2).reshape(g.shape[1], N_DEV * g.shape[2])
    gains = jnp.stack([W["g_pre_mix"], W["g_post_mix"], W["g_pre_ffn"], W["g_post_ffn"]])
    xs, tgt = x[0], loss_target[0]
    t = _tables(mod, gains, cols(g_conv)[0:4], W["conv_b"], W["b_if"], W["mlstm_norm_w"], positions[0])
    vecs = t["vecs"]
    t["vecs"] = _tie(vecs, st_b["token"], "tie_fwd")
    w_cat = _regroup_w_in(g_in)
    a = _mixer_fwd(xs, t, W["attn_sinks"], w_cat)
    g_ba, g_bm, g_out, g_gate, g_up, g_down = _exchange_wait(st_b, a["ym"], "gather_rest_wait")
    w_ba, w_bm, w_out = cols(g_ba), cols(g_bm), g_out.reshape(D_MODEL, D_MODEL)
    b = _ffn_part(xs, tgt, t, a, w_ba, w_bm, w_out, g_gate.reshape(D_FF, D_MODEL), g_up.reshape(D_FF, D_MODEL),
                  g_down.reshape(D_FF, D_MODEL))

    st_f = _exchange_start([b["g_w_gate"].reshape(N_DEV, ff_sh, D_MODEL), b["g_w_up"].reshape(N_DEV, ff_sh, D_MODEL),
                            b["g_w_down"].reshape(N_DEV, ff_sh, D_MODEL)], False, b["dx1"], "scatter_ffn_start")
    t["vecs"] = _tie(vecs, st_f["token"], "tie_bwd")
    cm = _mixer_bwd(b["dx1"], t, a, b, W["attn_sinks"], w_ba, w_bm, w_out)
    pieces = lambda g, n: g.reshape(g.shape[0], N_DEV, n).transpose(1, 0, 2)
    st_m = _exchange_start([_ungroup_w_in(cm["g_w_cat"]), pieces(cm["g_w_ba"], 128), pieces(cm["g_w_bm"], 128),
                            cm["g_w_out"].reshape(N_DEV, D_MODEL // N_DEV, D_MODEL),
                            jnp.pad(pieces(cm["dcw"][0:4], 128), ((0, 0), (0, 4), (0, 0)))], False, cm["dcw"],
                           "scatter_mixer_start")
    r_gate, r_up, r_down = _exchange_wait(st_f, st_m["token"], "scatter_ffn_wait")
    grad_x, acc_p = _pre_bwd(cm["dproj"], xs, b["dx1"], _tie(vecs, st_m["token"], "tie_pre_bwd"), w_cat)
    small = _small_grads(acc_p, b, cm)
    loss = b["loss"]

    big_out = [{} for _ in range(4)]

    def put(n, res):
        for k in range(4):
            big_out[k][n] = res[k][None]

    put("w_ffn_down", _adamw_sum(r_down, W["w_ffn_down"], M["w_ffn_down"], V["w_ffn_down"], "adamw_w_ffn_down"))
    for n, r in (("w_ffn_gate", r_gate), ("w_ffn_up", r_up)):
        put(n, [o.T for o in _adamw_sum(r, W[n].T, M[n].T, V[n].T, "adamw_" + n)])

    sg = _small_exchange(_small_table({"b_ada": small["dmod"], "gains": small["dgains"], "conv_b": small["dconv_b"],
                                       "mlstm_norm_w": small["dnorm_w"], "b_if": small["db_if"],
                                       "attn_sinks": small["dsinks"]}), True, "gather_small")
    as_row = lambda a, n: jnp.pad(a, (0, SMALL_AT[n][3] * SMALL_AT[n][1] - a.shape[0]))[None]
    small_res = _adamw_small(sg, {n: [as_row(d[n], n) for d in (W, M, V)] for n in SMALL_AT})
    small_out = [{n: small_res[n][k][:, 0:W[n].shape[0]] for n in SMALL_AT} for k in range(4)]
    dmod_cols = lax.dynamic_slice(sg[:, 0:6, :].reshape(N_DEV, 6 * D_MODEL), (0, me * ada_w), (N_DEV, ada_w))
    ada_out = _adamw_w_ada(cmat, dmod_cols, W["w_ada"], M["w_ada"], V["w_ada"])

    r_in, r_ba, r_bm, r_out, r_conv = _exchange_wait(st_m, ada_out[1], "scatter_mixer_wait")
    for n, r in (("w_branch_attn", r_ba), ("w_branch_mlstm", r_bm), ("w_out", r_out)):
        put(n, _adamw_sum(r, W[n], M[n], V[n], "adamw_" + n))
    pad4 = lambda v: jnp.pad(v, ((0, 4), (0, 0)))
    put("conv_w", [o[0:4] for o in _adamw_sum(r_conv, pad4(W["conv_w"]), pad4(M["conv_w"]), pad4(V["conv_w"]),
                                                 "adamw_conv_w")])
    g = _sum8(r_in, "sum_w_in")[:, 0:IN_SHARD].T
    put("w_in", [o.T for o in [g] + list(_adamw_plain(g, W["w_in"].T, M["w_in"].T, V["w_in"].T, "adamw_w_in"))])

    total = lax.psum(loss, ("x", "y", "c"))
    outs = [total, grad_x[None]]
    for k in range(4):
        for n in WEIGHT_NAMES:
            if n == "w_ada":
                outs.append(ada_out[k][None])
            elif n in big_out[k]:
                outs.append(big_out[k][n])
            else:
                outs.append(small_out[k][n])
    return tuple(outs)
```

```python
import jax
import jax.numpy as jnp
from jax import lax
from jax.experimental import pallas as pl
from jax.experimental.pallas import tpu as pltpu

F32 = jnp.float32
BF16 = jnp.bfloat16

N_DEV = 8
D_MODEL = 1024
D_FF = 2816
N_Q_HEADS = 8
HEAD_DIM = 64
ATTN_BLOCK = 128
ROPE_THETA = 10000.0
MLSTM_HEADS = 4
MLSTM_HEAD_DIM = 128
MLSTM_CHUNK = 128
NORM_EPS = 1e-6
ADAM_LR = 0.001
ADAM_B1 = 0.9
ADAM_B2 = 0.999
ADAM_EPS = 1e-08
ADAM_WD = 0.01
ADAM_STEP = 10

ROW_TILE = 256
WIDE_TILE = 512
LANES = 128
NEG = -1e30
VMEM_LIMIT = 56 * 1024 * 1024

A_W = 768
M_W = 2048
IF_W = 128
G_W = 2048
CAT_W = A_W + M_W + IF_W + G_W

R_SHIFT_M, R_SCALE_M, R_GATE_M, R_SHIFT_F, R_SCALE_F, R_GATE_F = 0, 1, 2, 3, 4, 5
R_G_PRE_MIX, R_G_POST_MIX, R_G_PRE_FFN, R_G_POST_FFN = 6, 7, 8, 9


def _dot(a, b):
    return jnp.dot(a, b, preferred_element_type=F32)


def _dot_nt(a, b):
    return lax.dot_general(a, b, (((1,), (1,)), ((), ())), preferred_element_type=F32)


def _dot_tn(a, b):
    return lax.dot_general(a, b, (((0,), (0,)), ((), ())), preferred_element_type=F32)


def _recip(x):
    return 1.0 / x


def _sigmoid(x):
    return _recip(1.0 + jnp.exp(-x))


def _colsum(x):
    return jnp.sum(x, axis=0, keepdims=True)


def _rowmean(x):
    return jnp.mean(x, axis=-1, keepdims=True)


def _params(sem=None, vmem=VMEM_LIMIT):
    kw = dict(vmem_limit_bytes=vmem)
    if sem is not None:
        kw["dimension_semantics"] = sem
    return pltpu.CompilerParams(**kw)


def _full(shape):
    nd = len(shape)
    return pl.BlockSpec(shape, lambda *_: (0,) * nd)


def _pre_proj(x, vecs, w_cat):
    S = x.shape[0]
    tm = WIDE_TILE

    def body(x_ref, v_ref, w_ref, h_ref, pa_ref, pm_ref, pif_ref, pg_ref):
        xv = x_ref[...]
        r = lax.rsqrt(_rowmean(xv * xv) + NORM_EPS)
        h = (xv * r * v_ref[R_G_PRE_MIX:R_G_PRE_MIX + 1, :]) * (1.0 + v_ref[R_SCALE_M:R_SCALE_M + 1, :]) \
            + v_ref[R_SHIFT_M:R_SHIFT_M + 1, :]
        hb = h.astype(BF16)
        h_ref[...] = hb
        pa_ref[...] = _dot(hb, w_ref[:, 0:A_W])
        pm_ref[...] = _dot(hb, w_ref[:, A_W:A_W + M_W])
        pif_ref[...] = _dot(hb, w_ref[:, A_W + M_W:A_W + M_W + IF_W])
        pg_ref[...] = _dot(hb, w_ref[:, A_W + M_W + IF_W:CAT_W]).astype(BF16)

    row = lambda w: pl.BlockSpec((tm, w), lambda i: (i, 0))
    return pl.pallas_call(
        body, name="pre_proj", grid=(S // tm,),
        in_specs=[row(D_MODEL), _full(vecs.shape), _full(w_cat.shape)],
        out_specs=[row(D_MODEL), row(A_W), row(M_W), row(IF_W), row(G_W)],
        out_shape=[jax.ShapeDtypeStruct((S, D_MODEL), BF16), jax.ShapeDtypeStruct((S, A_W), F32),
                   jax.ShapeDtypeStruct((S, M_W), F32), jax.ShapeDtypeStruct((S, IF_W), F32),
                   jax.ShapeDtypeStruct((S, G_W), BF16)],
        compiler_params=_params(("parallel",)),
    )(x, vecs, w_cat)


def _mix_fwd(x, ya, ym, pg, vecs, w_ba, w_bm, w_out):
    S = x.shape[0]
    tm = WIDE_TILE

    def body(x_ref, ya_ref, ym_ref, pg_ref, v_ref, wba_ref, wbm_ref, wout_ref,
             x1_ref, merged_ref, mix_ref, pa_ref, pb_ref):
        pa = _dot(ya_ref[...], wba_ref[...])
        pb = _dot(ym_ref[...], wbm_ref[...])
        merged = _sigmoid(pg_ref[:, 0:D_MODEL].astype(F32)) * pa + _sigmoid(pg_ref[:, D_MODEL:G_W].astype(F32)) * pb
        mb = merged.astype(BF16)
        mix = _dot(mb, wout_ref[...])
        r = lax.rsqrt(_rowmean(mix * mix) + NORM_EPS)
        x1_ref[...] = x_ref[...] + v_ref[R_GATE_M:R_GATE_M + 1, :] * (mix * r * v_ref[R_G_POST_MIX:R_G_POST_MIX + 1, :])
        merged_ref[...] = mb
        mix_ref[...] = mix.astype(BF16)
        pa_ref[...] = pa.astype(BF16)
        pb_ref[...] = pb.astype(BF16)

    row = lambda w: pl.BlockSpec((tm, w), lambda i: (i, 0))
    sd = lambda w, dt: jax.ShapeDtypeStruct((S, w), dt)
    return pl.pallas_call(
        body, name="mix_fwd", grid=(S // tm,),
        in_specs=[row(D_MODEL), row(512), row(512), row(G_W), _full(vecs.shape), _full(w_ba.shape),
                  _full(w_bm.shape), _full(w_out.shape)],
        out_specs=[row(D_MODEL)] * 5,
        out_shape=[sd(D_MODEL, F32), sd(D_MODEL, BF16), sd(D_MODEL, BF16), sd(D_MODEL, BF16), sd(D_MODEL, BF16)],
        compiler_params=_params(("parallel",)),
    )(x, ya, ym, pg, vecs, w_ba, w_bm, w_out)


def _ffn_fwd_bwd(x1, tgt, vecs, w_gate, w_up, w_down):
    S = x1.shape[0]
    tm = ROW_TILE

    def body(x1_ref, tgt_ref, v_ref, wg_hbm, wu_hbm, wd_hbm,
             dx1_ref, h2_ref, hid_ref, da_ref, du_ref, dff_ref, acc_ref, loss_ref,
             wg, wu, wd, sem):
        i = pl.program_id(0)

        @pl.when(i == 0)
        def _():
            cps = [pltpu.make_async_copy(wg_hbm, wg, sem.at[0]), pltpu.make_async_copy(wu_hbm, wu, sem.at[1]),
                   pltpu.make_async_copy(wd_hbm, wd, sem.at[2])]
            for cp in cps:
                cp.start()
            for cp in cps:
                cp.wait()
            acc_ref[...] = jnp.zeros_like(acc_ref)
            loss_ref[...] = jnp.zeros_like(loss_ref)

        vrow = lambda r: v_ref[r:r + 1, :]
        x1v = x1_ref[...]
        r3 = lax.rsqrt(_rowmean(x1v * x1v) + NORM_EPS)
        x1hat = x1v * r3
        xn3 = x1hat * vrow(R_G_PRE_FFN)
        h2b = (xn3 * (1.0 + vrow(R_SCALE_F)) + vrow(R_SHIFT_F)).astype(BF16)
        h2_ref[...] = h2b
        a = _dot_nt(h2b, wg[...])
        u = _dot_nt(h2b, wu[...])
        sg = _sigmoid(a)
        sil = a * sg
        hidb = (sil * u).astype(BF16)
        hid_ref[...] = hidb
        ff = _dot(hidb, wd[...])
        r4 = lax.rsqrt(_rowmean(ff * ff) + NORM_EPS)
        ffhat = ff * r4
        n4 = ffhat * vrow(R_G_POST_FFN)
        err = x1v + vrow(R_GATE_F) * n4 - tgt_ref[...]
        loss_ref[...] += jnp.sum(err * err) * (0.5 / D_MODEL)
        dy = err * (1.0 / D_MODEL)
        acc_ref[0:1, :] += _colsum(dy * n4)
        dn4 = dy * vrow(R_GATE_F)
        acc_ref[1:2, :] += _colsum(dn4 * ffhat)
        dffhat = dn4 * vrow(R_G_POST_FFN)
        dffb = (r4 * (dffhat - ffhat * _rowmean(dffhat * ffhat))).astype(BF16)
        dff_ref[...] = dffb
        dhid = _dot_nt(dffb, wd[...])
        dub = (dhid * sil).astype(BF16)
        dab = (dhid * u * (sg * (1.0 + a * (1.0 - sg)))).astype(BF16)
        da_ref[...] = dab
        du_ref[...] = dub
        dh2 = _dot(dab, wg[...]) + _dot(dub, wu[...])
        acc_ref[2:3, :] += _colsum(dh2 * xn3)
        acc_ref[3:4, :] += _colsum(dh2)
        dxn3 = dh2 * (1.0 + vrow(R_SCALE_F))
        acc_ref[4:5, :] += _colsum(dxn3 * x1hat)
        dx1hat = dxn3 * vrow(R_G_PRE_FFN)
        dx1_ref[...] = dy + r3 * (dx1hat - x1hat * _rowmean(dx1hat * x1hat))

    row = lambda w: pl.BlockSpec((tm, w), lambda i: (i, 0))
    sd = lambda w, dt: jax.ShapeDtypeStruct((S, w), dt)
    anyspec = pl.BlockSpec(memory_space=pl.ANY)
    return pl.pallas_call(
        body, name="ffn_fwd_bwd", grid=(S // tm,),
        in_specs=[row(D_MODEL), row(D_MODEL), _full(vecs.shape), anyspec, anyspec, anyspec],
        out_specs=[row(D_MODEL), row(D_MODEL), row(D_FF), row(D_FF), row(D_FF), row(D_MODEL),
                   _full((8, D_MODEL)), _full((8, LANES))],
        out_shape=[sd(D_MODEL, F32), sd(D_MODEL, BF16), sd(D_FF, BF16), sd(D_FF, BF16), sd(D_FF, BF16),
                   sd(D_MODEL, BF16), jax.ShapeDtypeStruct((8, D_MODEL), F32), jax.ShapeDtypeStruct((8, LANES), F32)],
        scratch_shapes=[pltpu.VMEM(w_gate.shape, BF16), pltpu.VMEM(w_up.shape, BF16), pltpu.VMEM(w_down.shape, BF16),
                        pltpu.SemaphoreType.DMA((3,))],
        compiler_params=_params(("arbitrary",)),
    )(x1, tgt, vecs, w_gate, w_up, w_down)


def _mix_bwd(dx1, mix, pa, pb, pg, vecs, w_ba, w_bm, w_out):
    S = dx1.shape[0]
    tm = WIDE_TILE

    def body(dx1_ref, mix_ref, pa_ref, pb_ref, pg_ref, v_ref, wba_ref, wbm_ref, wout_ref,
             dmix_ref, dpa_ref, dpb_ref, dg_ref, dya_ref, dym_ref, acc_ref):
        i = pl.program_id(0)

        @pl.when(i == 0)
        def _():
            acc_ref[...] = jnp.zeros_like(acc_ref)

        vrow = lambda r: v_ref[r:r + 1, :]
        dx1v = dx1_ref[...]
        mix = mix_ref[...].astype(F32)
        r2 = lax.rsqrt(_rowmean(mix * mix) + NORM_EPS)
        mixhat = mix * r2
        acc_ref[0:1, :] += _colsum(dx1v * (mixhat * vrow(R_G_POST_MIX)))
        dn2 = dx1v * vrow(R_GATE_M)
        acc_ref[1:2, :] += _colsum(dn2 * mixhat)
        dmixhat = dn2 * vrow(R_G_POST_MIX)
        dmixb = (r2 * (dmixhat - mixhat * _rowmean(dmixhat * mixhat))).astype(BF16)
        dmix_ref[...] = dmixb
        dmerged = _dot_nt(dmixb, wout_ref[...])
        sa = _sigmoid(pg_ref[:, 0:D_MODEL].astype(F32))
        sm = _sigmoid(pg_ref[:, D_MODEL:G_W].astype(F32))
        dpab = (dmerged * sa).astype(BF16)
        dpbb = (dmerged * sm).astype(BF16)
        dpa_ref[...] = dpab
        dpb_ref[...] = dpbb
        dg_ref[:, 0:D_MODEL] = (dmerged * pa_ref[...].astype(F32) * (sa * (1.0 - sa))).astype(BF16)
        dg_ref[:, D_MODEL:G_W] = (dmerged * pb_ref[...].astype(F32) * (sm * (1.0 - sm))).astype(BF16)
        dya_ref[...] = _dot_nt(dpab, wba_ref[...]).astype(BF16)
        dym_ref[...] = _dot_nt(dpbb, wbm_ref[...])

    row = lambda w: pl.BlockSpec((tm, w), lambda i: (i, 0))
    sd = lambda w, dt: jax.ShapeDtypeStruct((S, w), dt)
    return pl.pallas_call(
        body, name="mix_bwd", grid=(S // tm,),
        in_specs=[row(D_MODEL), row(D_MODEL), row(D_MODEL), row(D_MODEL), row(G_W), _full(vecs.shape),
                  _full(w_ba.shape), _full(w_bm.shape), _full(w_out.shape)],
        out_specs=[row(D_MODEL), row(D_MODEL), row(D_MODEL), row(G_W), row(512), row(512), _full((8, D_MODEL))],
        out_shape=[sd(D_MODEL, BF16), sd(D_MODEL, BF16), sd(D_MODEL, BF16), sd(G_W, BF16), sd(512, BF16), sd(512, F32),
                   jax.ShapeDtypeStruct((8, D_MODEL), F32)],
        compiler_params=_params(("arbitrary",)),
    )(dx1, mix, pa, pb, pg, vecs, w_ba, w_bm, w_out)


def _pre_bwd(pieces, x, dx1, vecs, w_cat):
    S = x.shape[0]
    tm = WIDE_TILE
    n = len(pieces)
    starts = [sum(p.shape[1] for p in pieces[:k]) for k in range(n + 1)]

    def body(*refs):
        p_refs = refs[:n]
        x_ref, dx1_ref, v_ref, w_ref, dx_ref, acc_ref = refs[n:]
        i = pl.program_id(0)

        @pl.when(i == 0)
        def _():
            acc_ref[...] = jnp.zeros_like(acc_ref)

        vrow = lambda r: v_ref[r:r + 1, :]
        dh = _dot_nt(p_refs[0][...], w_ref[:, starts[0]:starts[1]])
        for k in range(1, n):
            dh = dh + _dot_nt(p_refs[k][...], w_ref[:, starts[k]:starts[k + 1]])
        xv = x_ref[...]
        r1 = lax.rsqrt(_rowmean(xv * xv) + NORM_EPS)
        xhat = xv * r1
        acc_ref[0:1, :] += _colsum(dh * (xhat * vrow(R_G_PRE_MIX)))
        acc_ref[1:2, :] += _colsum(dh)
        dxn = dh * (1.0 + vrow(R_SCALE_M))
        acc_ref[2:3, :] += _colsum(dxn * xhat)
        dxhat = dxn * vrow(R_G_PRE_MIX)
        dx_ref[...] = dx1_ref[...] + r1 * (dxhat - xhat * _rowmean(dxhat * xhat))

    row = lambda w: pl.BlockSpec((tm, w), lambda i: (i, 0))
    return pl.pallas_call(
        body, name="pre_bwd", grid=(S // tm,),
        in_specs=[row(p.shape[1]) for p in pieces] + [row(D_MODEL), row(D_MODEL), _full(vecs.shape), _full(w_cat.shape)],
        out_specs=[row(D_MODEL), _full((8, D_MODEL))],
        out_shape=[jax.ShapeDtypeStruct((S, D_MODEL), F32), jax.ShapeDtypeStruct((8, D_MODEL), F32)],
        compiler_params=_params(("arbitrary",)),
    )(*pieces, x, dx1, vecs, w_cat)


def _matmul_tn(a, b, tn, name, ts=1024):
    S, K = a.shape
    N = b.shape[1]
    n_s = S // ts

    def body(a_ref, b_ref, o_ref, acc_ref):
        s = pl.program_id(1)

        @pl.when(s == 0)
        def _():
            acc_ref[...] = jnp.zeros_like(acc_ref)

        acc_ref[...] += _dot_tn(a_ref[...], b_ref[...])

        @pl.when(s == n_s - 1)
        def _():
            o_ref[...] = acc_ref[...].astype(BF16)

    return pl.pallas_call(
        body, name=name, grid=(N // tn, n_s),
        in_specs=[pl.BlockSpec((ts, K), lambda j, s: (s, 0)), pl.BlockSpec((ts, tn), lambda j, s: (s, j))],
        out_specs=pl.BlockSpec((K, tn), lambda j, s: (0, j)),
        out_shape=jax.ShapeDtypeStruct((K, N), BF16),
        scratch_shapes=[pltpu.VMEM((K, tn), F32)],
        compiler_params=_params(("parallel", "arbitrary")),
    )(a, b)


def _rope_swap(t):
    lane = lax.broadcasted_iota(jnp.int32, t.shape, 1)
    first = (lane & (HEAD_DIM - 1)) < (HEAD_DIM // 2)
    return jnp.where(first, pltpu.roll(t, LANES - HEAD_DIM // 2, 1), pltpu.roll(t, HEAD_DIM // 2, 1))


def _rope(t, cos, sin_signed):
    return t * cos + _rope_swap(t) * sin_signed


def _rope_t(d, cos, sin_signed):
    return d * cos + _rope_swap(d * sin_signed)


def _to_kv_lanes(chunk, p, h):
    lane = lax.broadcasted_iota(jnp.int32, chunk.shape, 1)
    src = chunk if p == h else pltpu.roll(chunk, HEAD_DIM, 1)
    return jnp.where((lane >> 6) == h, src, jnp.zeros_like(src))


def _from_kv_lanes(o_a, o_b, h):
    lane = lax.broadcasted_iota(jnp.int32, o_a.shape, 1)
    a = o_a if h == 0 else pltpu.roll(o_a, HEAD_DIM, 1)
    b = o_b if h == 1 else pltpu.roll(o_b, HEAD_DIM, 1)
    return jnp.where(lane < HEAD_DIM, a, b)


def _band_bias(n):
    blk = ATTN_BLOCK
    qi = lax.broadcasted_iota(jnp.int32, (blk, 2 * blk), 0)
    kj = lax.broadcasted_iota(jnp.int32, (blk, 2 * blk), 1)
    seen = (kj > qi) & (kj <= qi + blk) & ((n > 0) | (kj >= blk))
    return jnp.concatenate([jnp.where(seen, 0.0, NEG)] * N_Q_HEADS, axis=0)


def _stack_heads(chunks, h, dtype):
    parts = []
    for g in range(4):
        j = 4 * h + g
        parts.append(_to_kv_lanes(chunks[j // 2], j % 2, h))
    return jnp.concatenate(parts, axis=0).astype(dtype)


def _fused_call(parts, name, n_steps):
    counts = [(len(p["in_specs"]), len(p["out_specs"]), len(p["scratch"])) for p in parts]
    n_in, n_out = sum(c[0] for c in counts), sum(c[1] for c in counts)

    def kernel_fn(*refs):
        i = pl.program_id(0)
        groups, a, b, c = [], 0, n_in, n_in + n_out
        for ci, co, cs in counts:
            groups.append(refs[a:a + ci] + refs[b:b + co] + refs[c:c + cs])
            a, b, c = a + ci, b + co, c + cs
        for p, g in zip(parts, groups):
            p["init"](i, *g)
        for p, g in zip(parts, groups):
            p["body"](i, *g)

    flat = lambda key: [v for p in parts for v in p[key]]
    res = pl.pallas_call(
        kernel_fn, name=name, grid=(n_steps,), in_specs=flat("in_specs"), out_specs=flat("out_specs"),
        out_shape=flat("out_shape"), scratch_shapes=flat("scratch"), compiler_params=_params(("arbitrary",)),
    )(*flat("operands"))
    out, pos = [], 0
    for _, co, _ in counts:
        out.append(res[pos:pos + co])
        pos += co
    return out


def _attn_fwd_part(pa, cos, sin, sinks):
    S = pa.shape[0]
    blk = ATTN_BLOCK
    nb = S // blk

    def body(n, sink_ref, cur_ref, prev_ref, cos_ref, sin_ref, cosp_ref, sinp_ref,
             ya_ref, qr_ref, kr_ref, vb_ref, lse_ref):
        cos_c, sin_c = cos_ref[...], sin_ref[...]
        qch = [_rope(cur_ref[:, c * LANES:(c + 1) * LANES], cos_c, sin_c) * (HEAD_DIM ** -0.5) for c in range(4)]
        for c in range(4):
            qr_ref[:, c * LANES:(c + 1) * LANES] = qch[c].astype(BF16)
        k_cur = _rope(cur_ref[:, 512:640], cos_c, sin_c).astype(BF16)
        k_prev = _rope(prev_ref[:, 0:LANES], cosp_ref[...], sinp_ref[...]).astype(BF16)
        v_cur = cur_ref[:, 640:768].astype(BF16)
        v_prev = prev_ref[:, LANES:2 * LANES].astype(BF16)
        kr_ref[...] = k_cur
        vb_ref[...] = v_cur
        K = jnp.concatenate([k_prev, k_cur], axis=0)
        V = jnp.concatenate([v_prev, v_cur], axis=0)
        lane = lax.broadcasted_iota(jnp.int32, (blk, LANES), 1)
        s = jnp.concatenate([_dot_nt(_stack_heads(qch, h, BF16), K) for h in range(2)], axis=0)
        s = s + _band_bias(n)
        rowmax = jnp.max(s, axis=1, keepdims=True)
        hd = lambda x, j: x[j * blk:(j + 1) * blk]
        m = jnp.concatenate([jnp.maximum(hd(rowmax, j), sink_ref[j]) for j in range(N_Q_HEADS)], axis=0)
        p = jnp.exp(s - m)
        pb = p.astype(BF16)
        den = _rowsum_mxu(pb) + jnp.concatenate([jnp.exp(sink_ref[j] - hd(m, j)) for j in range(N_Q_HEADS)], axis=0)
        o = jnp.concatenate([_dot(pb[4 * h * blk:4 * (h + 1) * blk], V) for h in range(2)], axis=0) * _recip(den)
        lse = m + jnp.log(den)
        outs = [o[j * blk:(j + 1) * blk, :] for j in range(N_Q_HEADS)]
        lse_tile = jnp.zeros((blk, LANES), F32)
        for j in range(N_Q_HEADS):
            lse_tile = jnp.where(lane == j, lse[j * blk:(j + 1) * blk, :], lse_tile)
        for c in range(4):
            ya_ref[:, c * LANES:(c + 1) * LANES] = _from_kv_lanes(outs[2 * c], outs[2 * c + 1], c // 2).astype(BF16)
        lse_ref[...] = lse_tile

    prev = lambda n: jnp.maximum(n - 1, 0)
    sd = lambda w, dt: jax.ShapeDtypeStruct((S, w), dt)
    return dict(
        init=lambda n, *refs: None, body=body, scratch=[], operands=[sinks, pa, pa, cos, sin, cos, sin],
        in_specs=[pl.BlockSpec(memory_space=pltpu.SMEM),
                  pl.BlockSpec((blk, A_W), lambda n: (n, 0)),
                  pl.BlockSpec((blk, 256), lambda n: (prev(n), 2)),
                  pl.BlockSpec((blk, LANES), lambda n: (n, 0)), pl.BlockSpec((blk, LANES), lambda n: (n, 0)),
                  pl.BlockSpec((blk, LANES), lambda n: (prev(n), 0)), pl.BlockSpec((blk, LANES), lambda n: (prev(n), 0))],
        out_specs=[pl.BlockSpec((blk, 512), lambda n: (n, 0)), pl.BlockSpec((blk, 512), lambda n: (n, 0)),
                   pl.BlockSpec((blk, LANES), lambda n: (n, 0)), pl.BlockSpec((blk, LANES), lambda n: (n, 0)),
                   pl.BlockSpec((blk, LANES), lambda n: (n, 0))],
        out_shape=[sd(512, BF16), sd(512, BF16), sd(LANES, BF16), sd(LANES, BF16), sd(LANES, F32)])


def _attn_bwd_part(dya, qr, kr, vb, lse, cos, sin, sinks):
    S = dya.shape[0]
    blk = ATTN_BLOCK
    nb = S // blk

    def init(n, sink_ref, dya_ref, qr_ref, kc_ref, kp_ref, vc_ref, vp_ref, lse_ref, cos_ref, sin_ref, cosp_ref, sinp_ref,
             dqkv_ref, last_ref, dsink_ref, ck, cv, cq):
        @pl.when(n == 0)
        def _():
            ck[...] = jnp.zeros_like(ck)
            cv[...] = jnp.zeros_like(cv)
            cq[...] = jnp.zeros_like(cq)
            dsink_ref[...] = jnp.zeros_like(dsink_ref)

    def body(n, sink_ref, dya_ref, qr_ref, kc_ref, kp_ref, vc_ref, vp_ref, lse_ref, cos_ref, sin_ref, cosp_ref, sinp_ref,
             dqkv_ref, last_ref, dsink_ref, ck, cv, cq):
        K = jnp.concatenate([kp_ref[...], kc_ref[...]], axis=0)
        V = jnp.concatenate([vp_ref[...], vc_ref[...]], axis=0)
        qch = [qr_ref[:, c * LANES:(c + 1) * LANES] for c in range(4)]
        dch = [dya_ref[:, c * LANES:(c + 1) * LANES].astype(F32) for c in range(4)]
        lse_tile = lse_ref[...]
        lane8 = lax.broadcasted_iota(jnp.int32, (8, LANES), 1)
        grp = lambda x, h: x[4 * h * blk:4 * (h + 1) * blk]
        qs = jnp.concatenate([_stack_heads(qch, h, BF16) for h in range(2)], axis=0)
        dos = jnp.concatenate([_stack_heads(dch, h, BF16) for h in range(2)], axis=0)
        lse_col = jnp.concatenate([lse_tile[:, j:j + 1] for j in range(N_Q_HEADS)], axis=0)
        s = jnp.concatenate([_dot_nt(grp(qs, h), K) for h in range(2)], axis=0)
        p = jnp.exp(s + _band_bias(n) - lse_col)
        dp = jnp.concatenate([_dot_nt(grp(dos, h), V) for h in range(2)], axis=0)
        delta = jnp.sum(p * dp, axis=1, keepdims=True)
        dsb = (p * (dp - delta)).astype(BF16)
        pb = p.astype(BF16)
        dq = jnp.concatenate([_dot(grp(dsb, h), K) for h in range(2)], axis=0)
        dk_acc = _dot_tn(grp(dsb, 0), grp(qs, 0)) + _dot_tn(grp(dsb, 1), grp(qs, 1))
        dv_acc = _dot_tn(grp(pb, 0), grp(dos, 0)) + _dot_tn(grp(pb, 1), grp(dos, 1))
        dqs = [dq[j * blk:(j + 1) * blk, :] for j in range(N_Q_HEADS)]
        dsink = jnp.zeros((8, LANES), F32)
        for j in range(N_Q_HEADS):
            rows = slice(j * blk, (j + 1) * blk)
            ps_delta = jnp.exp(sink_ref[j] - lse_col[rows]) * delta[rows]
            dsink = jnp.where(lane8 == j, dsink - jnp.sum(ps_delta), dsink)
        dsink_ref[...] += dsink
        cos_c, sin_c = cos_ref[...], sin_ref[...]
        dqkv_ref[:, 0:512] = cq[...]
        dqkv_ref[:, 512:640] = _rope_t(dk_acc[0:blk, :] + ck[...], cosp_ref[...], sinp_ref[...]).astype(BF16)
        dqkv_ref[:, 640:768] = (dv_acc[0:blk, :] + cv[...]).astype(BF16)
        ck[...] = dk_acc[blk:2 * blk, :]
        cv[...] = dv_acc[blk:2 * blk, :]
        for c in range(4):
            dqc = _from_kv_lanes(dqs[2 * c], dqs[2 * c + 1], c // 2) * (HEAD_DIM ** -0.5)
            dq_c = _rope_t(dqc, cos_c, sin_c).astype(BF16)
            cq[:, c * LANES:(c + 1) * LANES] = dq_c
            last_ref[:, c * LANES:(c + 1) * LANES] = dq_c
        last_ref[:, 512:640] = _rope_t(dk_acc[blk:2 * blk, :], cos_c, sin_c).astype(BF16)
        last_ref[:, 640:768] = dv_acc[blk:2 * blk, :].astype(BF16)

    prev = lambda n: jnp.maximum(n - 1, 0)
    same = lambda n: n
    bs = lambda w, f: pl.BlockSpec((blk, w), lambda n: (f(n), 0))
    return dict(
        init=init, body=body, operands=[sinks, dya, qr, kr, kr, vb, vb, lse, cos, sin, cos, sin],
        in_specs=[pl.BlockSpec(memory_space=pltpu.SMEM),
                  bs(512, same), bs(512, same), bs(LANES, same), bs(LANES, prev), bs(LANES, same), bs(LANES, prev),
                  bs(LANES, same), bs(LANES, same), bs(LANES, same), bs(LANES, prev), bs(LANES, prev)],
        out_specs=[bs(A_W, prev), _full((blk, A_W)), _full((8, LANES))],
        out_shape=[jax.ShapeDtypeStruct((S, A_W), BF16), jax.ShapeDtypeStruct((blk, A_W), BF16),
                   jax.ShapeDtypeStruct((8, LANES), F32)],
        scratch=[pltpu.VMEM((blk, LANES), F32), pltpu.VMEM((blk, LANES), F32), pltpu.VMEM((blk, 512), BF16)])


def _split3(x):
    hi = x.astype(BF16)
    r1 = x - hi.astype(F32)
    mid = r1.astype(BF16)
    lo = (r1 - mid.astype(F32)).astype(BF16)
    return hi, mid, lo


def _tri_matmul(tri_b, x):
    hi, mid, lo = _split3(x)
    return _dot(tri_b, hi) + _dot(tri_b, mid) + _dot(tri_b, lo)


def _log_sigmoid(x):
    return jnp.minimum(x, 0.0) - jnp.log(1.0 + jnp.exp(-jnp.abs(x)))


def _shift_rows(cur, seam, k, down):
    L = cur.shape[0]
    row8 = lax.broadcasted_iota(jnp.int32, seam.shape, 0)
    if down:
        mixed = jnp.concatenate([cur[:L - 8], jnp.where(row8 >= 8 - k, seam, cur[L - 8:])], axis=0)
        return pltpu.roll(mixed, k, 0)
    mixed = jnp.concatenate([jnp.where(row8 < k, seam, cur[:8]), cur[8:]], axis=0)
    return pltpu.roll(mixed, L - k, 0)


def _conv_fwd(cur, tail, cw_ref):
    z = cw_ref[4:5, :]
    for k in range(3, 0, -1):
        z = z + _shift_rows(cur, tail, k, True) * cw_ref[3 - k:4 - k, :]
    return z + cur * cw_ref[3:4, :]


def _rowsum_mxu(x, two_pass=False):
    ones = jnp.ones((x.shape[1], LANES), BF16)
    hi = x.astype(BF16)
    s = _dot(hi, ones)
    if two_pass:
        s = s + _dot((x - hi.astype(F32)).astype(BF16), ones)
    return s


def _stack(f):
    return jnp.concatenate([f(h) for h in range(MLSTM_HEADS)], axis=0)


def _head(x, h):
    L = x.shape[0] // MLSTM_HEADS
    return x[h * L:(h + 1) * L]


def _mlstm_heads_fwd(qk, cur_ref, gt, b_all, c_prev, nmv, tri, eye):
    L = qk.shape[0]
    HD = MLSTM_HEAD_DIM
    W4 = MLSTM_HEADS * HD
    col2row = lambda x: jnp.sum(jnp.where(eye, x, 0.0), axis=0, keepdims=True)
    b_col = _stack(lambda h: b_all[:, 4 + h:5 + h])
    i_col = _stack(lambda h: gt[:, h:h + 1])
    b_row = _stack(lambda h: jnp.broadcast_to(col2row(b_all[:, 4 + h:5 + h]), (L, L)))
    i_row = _stack(lambda h: jnp.broadcast_to(col2row(gt[:, h:h + 1]), (L, L)))
    bl = _stack(lambda h: jnp.broadcast_to(b_all[L - 1:L, 4 + h:5 + h], (L, 1)))
    m_prev = _stack(lambda h: jnp.broadcast_to(nmv[4 + h:5 + h, 0:1], (L, 1)))
    n_prev = _stack(lambda h: jnp.broadcast_to(nmv[h:h + 1, :], (L, HD)))
    tri4 = jnp.concatenate([tri] * MLSTM_HEADS, axis=0)
    Dm = jnp.where(tri4, b_col - b_row + i_row, NEG)
    inter = b_col + m_prev
    m_t = jnp.maximum(inter, jnp.max(Dm, axis=1, keepdims=True))
    W = jnp.exp(Dm - m_t)
    e_t = jnp.exp(inter - m_t)
    q = _stack(lambda h: qk[:, h * HD:(h + 1) * HD])
    k = _stack(lambda h: qk[:, W4 + h * HD:W4 + (h + 1) * HD]) * (HD ** -0.5)
    v = _stack(lambda h: cur_ref[:, 2 * W4 + h * HD:2 * W4 + (h + 1) * HD])
    qb, kb, vb = q.astype(BF16), k.astype(BF16), v.astype(BF16)
    Sc = _stack(lambda h: _dot_nt(_head(qb, h), _head(kb, h))) * W
    Scb = Sc.astype(BF16)
    cb = [c.astype(BF16) for c in c_prev]
    P1 = _stack(lambda h: _dot(_head(qb, h), cb[h]))
    num = _stack(lambda h: _dot(_head(Scb, h), _head(vb, h))) + e_t * P1
    qn = _rowsum_mxu(q * n_prev)
    den = _rowsum_mxu(Scb) + e_t * qn
    floor = jnp.broadcast_to(jnp.exp(-m_t), den.shape)
    inv_g = _recip(jnp.maximum(jnp.abs(den), floor))
    hv = num * inv_g
    a_col = bl - b_col + i_col
    a_max = _stack(lambda h: jnp.broadcast_to(jnp.max(_head(a_col, h), axis=0, keepdims=True), (L, 1)))
    m_new = jnp.maximum(bl + m_prev, a_max)
    dec = jnp.exp(bl + m_prev - m_new)
    u_col = jnp.exp(a_col - m_new)
    return dict(W=W, e_t=e_t, q=q, k=k, v=v, qb=qb, kb=kb, vb=vb, cb=cb, Sc=Sc, Scb=Scb, P1=P1, qn=qn, den=den,
                floor=floor, inv_g=inv_g, hv=hv, n_prev=n_prev, m_new=m_new, dec=dec, u_col=u_col)


def _mlstm_fwd_part(pm, pif, cw, sv):
    S = pm.shape[0]
    L = MLSTM_CHUNK
    nc = S // L
    HD = MLSTM_HEAD_DIM
    W4 = MLSTM_HEADS * HD

    def init(c, cur_ref, pif_ref, cw_ref, sv_ref, ym_ref, z_ref, cst_ref, nst_ref, C, nm, tail):
        @pl.when(c == 0)
        def _():
            C[...] = jnp.zeros_like(C)
            nm[...] = jnp.zeros_like(nm)
            tail[...] = jnp.zeros_like(tail)

    def body(c, cur_ref, pif_ref, cw_ref, sv_ref, ym_ref, z_ref, cst_ref, nst_ref, C, nm, tail):
        z = _conv_fwd(cur_ref[:, 0:2 * W4], tail[...], cw_ref)
        tail[...] = cur_ref[L - 8:L, 0:2 * W4]
        z_ref[...] = z
        qk = z * _sigmoid(z)
        gt = pif_ref[...] + sv_ref[1:2, 0:LANES]
        r_i = lax.broadcasted_iota(jnp.int32, (L, L), 0)
        c_i = lax.broadcasted_iota(jnp.int32, (L, L), 1)
        tri = c_i <= r_i
        eye = c_i == r_i
        b_all = _tri_matmul(tri.astype(BF16), _log_sigmoid(gt))
        nmv = nm[...]
        nst_ref[0] = nmv
        c_prev = [C[h] for h in range(MLSTM_HEADS)]
        f = _mlstm_heads_fwd(qk, cur_ref, gt, b_all, c_prev, nmv, tri, eye)
        hv = f["hv"]
        xc = hv - _rowsum_mxu(hv, True) * (1.0 / HD)
        hhat = xc * lax.rsqrt(_rowsum_mxu(xc * xc) * (1.0 / HD) + NORM_EPS)
        so = _sigmoid(_stack(lambda h: cur_ref[:, 3 * W4 + h * HD:3 * W4 + (h + 1) * HD]))
        wn = _stack(lambda h: jnp.broadcast_to(sv_ref[0:1, h * HD:(h + 1) * HD], (L, HD)))
        y = (so * hhat * wn).astype(BF16)
        kw = f["k"] * f["u_col"]
        kwb = kw.astype(BF16)
        n_new, m_new = [], []
        for h in range(MLSTM_HEADS):
            cst_ref[0, h] = c_prev[h]
            ym_ref[:, h * HD:(h + 1) * HD] = _head(y, h)
            dec = f["dec"][h * L:h * L + 1, :]
            C[h] = dec * c_prev[h] + _dot_tn(_head(kwb, h), _head(f["vb"], h))
            n_new.append(dec * nmv[h:h + 1, :] + _colsum(_head(kw, h)))
            m_new.append(jnp.broadcast_to(f["m_new"][h * L:h * L + 1, :], (1, LANES)))
        nm[...] = jnp.concatenate(n_new + m_new, axis=0)

    return dict(
        init=init, body=body, operands=[pm, pif, cw, sv],
        in_specs=[pl.BlockSpec((L, M_W), lambda c: (c, 0)),
                  pl.BlockSpec((L, IF_W), lambda c: (c, 0)), _full(cw.shape), _full(sv.shape)],
        out_specs=[pl.BlockSpec((L, W4), lambda c: (c, 0)), pl.BlockSpec((L, 2 * W4), lambda c: (c, 0)),
                   pl.BlockSpec((1, MLSTM_HEADS, HD, HD), lambda c: (c, 0, 0, 0)),
                   pl.BlockSpec((1, 8, LANES), lambda c: (c, 0, 0))],
        out_shape=[jax.ShapeDtypeStruct((S, W4), BF16), jax.ShapeDtypeStruct((S, 2 * W4), F32),
                   jax.ShapeDtypeStruct((nc, MLSTM_HEADS, HD, HD), F32), jax.ShapeDtypeStruct((nc, 8, LANES), F32)],
        scratch=[pltpu.VMEM((MLSTM_HEADS, HD, HD), F32), pltpu.VMEM((8, LANES), F32), pltpu.VMEM((8, 2 * W4), F32)])


def _mlstm_bwd_part(pm, zc, pif, cw, sv, dym, cst, nst):
    S = pm.shape[0]
    L = MLSTM_CHUNK
    nc = S // L
    HD = MLSTM_HEAD_DIM
    W4 = MLSTM_HEADS * HD

    def init(r, cur_ref, z_ref, pif_ref, cw_ref, sv_ref, dym_ref, cst_ref, nst_ref,
             dm_ref, dcw_ref, dsv_ref, dC, dn, dz_next, dqk):
        @pl.when(r == 0)
        def _():
            dC[...] = jnp.zeros_like(dC)
            dn[...] = jnp.zeros_like(dn)
            dz_next[...] = jnp.zeros_like(dz_next)
            dcw_ref[...] = jnp.zeros_like(dcw_ref)
            dsv_ref[...] = jnp.zeros_like(dsv_ref)

    def body(r, cur_ref, z_ref, pif_ref, cw_ref, sv_ref, dym_ref, cst_ref, nst_ref,
             dm_ref, dcw_ref, dsv_ref, dC, dn, dz_next, dqk):
        z = z_ref[...]
        sgz = _sigmoid(z)
        qk = z * sgz
        gt = pif_ref[...] + sv_ref[1:2, 0:LANES]
        r_i = lax.broadcasted_iota(jnp.int32, (L, L), 0)
        c_i = lax.broadcasted_iota(jnp.int32, (L, L), 1)
        tri = c_i <= r_i
        eye = c_i == r_i
        b_all = _tri_matmul(tri.astype(BF16), _log_sigmoid(gt))
        lane = lax.broadcasted_iota(jnp.int32, (L, LANES), 1)
        rowl = lax.broadcasted_iota(jnp.int32, (L, 1), 0)
        nmv = nst_ref[0]
        heads = range(MLSTM_HEADS)
        c_prev = [cst_ref[0, h] for h in heads]
        f = _mlstm_heads_fwd(qk, cur_ref, gt, b_all, c_prev, nmv, tri, eye)
        hv, inv_g, den, e_t, u_col, n_prev = f["hv"], f["inv_g"], f["den"], f["e_t"], f["u_col"], f["n_prev"]
        q, k, v, qb, kb, vb, Sc, Scb, W = f["q"], f["k"], f["v"], f["qb"], f["kb"], f["vb"], f["Sc"], f["Scb"], f["W"]
        xc = hv - _rowsum_mxu(hv, True) * (1.0 / HD)
        rstd = lax.rsqrt(_rowsum_mxu(xc * xc) * (1.0 / HD) + NORM_EPS)
        hhat = xc * rstd
        wn = _stack(lambda h: jnp.broadcast_to(sv_ref[0:1, h * HD:(h + 1) * HD], (L, HD)))
        so = _sigmoid(_stack(lambda h: cur_ref[:, 3 * W4 + h * HD:3 * W4 + (h + 1) * HD]))
        dy = _stack(lambda h: dym_ref[:, h * HD:(h + 1) * HD])
        d_o = (dy * hhat * wn * (so * (1.0 - so))).astype(BF16)
        dln = dy * so
        dwn = dln * hhat
        dhhat = dln * wn
        m2 = _rowsum_mxu(dhhat * hhat) * (1.0 / HD)
        dh = rstd * (dhhat - _rowsum_mxu(dhhat) * (1.0 / HD) - hhat * m2)
        dnum = dh * inv_g
        active = jnp.abs(den) > f["floor"]
        dden = jnp.where(active, -(HD * NORM_EPS) * m2 * rstd * rstd * inv_g * jnp.where(den >= 0.0, 1.0, -1.0), 0.0)
        dnumb = dnum.astype(BF16)
        dSc = _stack(lambda h: _dot_nt(_head(dnumb, h), _head(vb, h))) + dden
        dA = (dSc * W).astype(BF16)
        G = dSc * Sc
        Gb = G.astype(BF16)
        Gl = (G - Gb.astype(F32)).astype(BF16)
        ones = jnp.ones((L, LANES), BF16)
        Gr = _dot(Gb, ones) + _dot(Gl, ones)
        Gc = _stack(lambda h: _dot_tn(_head(Gb, h), ones) + _dot_tn(_head(Gl, h), ones))
        dCn = [dC[h] for h in heads]
        dCnb = [d.astype(BF16) for d in dCn]
        dnv = dn[...]
        dn_new = _stack(lambda h: jnp.broadcast_to(dnv[h:h + 1, :], (L, HD)))
        kdC = _stack(lambda h: _dot(_head(kb, h), dCnb[h]))
        vdC = _stack(lambda h: _dot_nt(_head(vb, h), dCnb[h]))
        dv = (_stack(lambda h: _dot_tn(_head(Scb, h), _head(dnumb, h))) + u_col * kdC).astype(BF16)
        dq = _stack(lambda h: _dot(_head(dA, h), _head(kb, h))) \
            + e_t * _stack(lambda h: _dot_nt(_head(dnumb, h), f["cb"][h])) + (e_t * dden) * n_prev
        dk = (_stack(lambda h: _dot_tn(_head(dA, h), _head(qb, h))) + u_col * (vdC + dn_new)) * (HD ** -0.5)
        E = (_rowsum_mxu(f["P1"] * dnum, True) + dden * f["qn"]) * e_t
        U = _rowsum_mxu(kdC * v + k * dn_new, True) * u_col
        qe = (q * e_t).astype(BF16)
        qd = (e_t * dden) * q
        di = Gc + U
        db = Gr + E - Gc - U
        di_tile = jnp.zeros((L, LANES), F32)
        db_tile = jnp.zeros((L, LANES), F32)
        dn_rows = []
        for h in heads:
            dec = f["dec"][h * L:h * L + 1, :]
            ddec = jnp.sum(dCn[h] * c_prev[h]) + jnp.sum(dnv[h:h + 1, :] * nmv[h:h + 1, :])
            dbl = ddec * dec + jnp.sum(_head(U, h), axis=0, keepdims=True)
            di_tile = jnp.where(lane == h, _head(di, h), di_tile)
            db_tile = jnp.where(lane == 4 + h, _head(db, h) + jnp.where(rowl == L - 1, dbl, 0.0), db_tile)
            dC[h] = dec * dCn[h] + _dot_tn(_head(qe, h), _head(dnumb, h))
            dn_rows.append(dec * dnv[h:h + 1, :] + _colsum(_head(qd, h)))
            dsv_ref[0:1, h * HD:(h + 1) * HD] += _colsum(_head(dwn, h))
            dqk[:, h * HD:(h + 1) * HD] = _head(dq, h)
            dqk[:, W4 + h * HD:W4 + (h + 1) * HD] = _head(dk, h)
            dm_ref[:, 2 * W4 + h * HD:2 * W4 + (h + 1) * HD] = _head(dv, h)
            dm_ref[:, 3 * W4 + h * HD:3 * W4 + (h + 1) * HD] = _head(d_o, h)
        dn[...] = jnp.concatenate(dn_rows + [jnp.zeros((8 - MLSTM_HEADS, LANES), F32)], axis=0)
        dlf = _tri_matmul((r_i <= c_i).astype(BF16), db_tile)
        dif = jnp.where(lane < 4, di_tile, jnp.where(lane < 8, dlf * (1.0 - _sigmoid(gt)), 0.0))
        dm_ref[:, M_W:M_W + IF_W] = dif.astype(BF16)
        dsv_ref[1:2, 0:LANES] += _colsum(dif)
        dz = dqk[...] * (sgz * (1.0 + z * (1.0 - sgz)))
        dcw_ref[4:5, :] += _colsum(dz)
        u = cur_ref[:, 0:2 * W4]
        du_in = dz * cw_ref[3:4, :]
        dcw_ref[3:4, :] += _colsum(dz * u)
        for k in range(1, 4):
            up = _shift_rows(dz, dz_next[...], k, False)
            dcw_ref[3 - k:4 - k, :] += _colsum(up * u)
            du_in = du_in + up * cw_ref[3 - k:4 - k, :]
        dz_next[...] = dz[0:8, :]
        dm_ref[:, 0:2 * W4] = du_in.astype(BF16)

    cidx = lambda r: nc - 1 - r
    return dict(
        init=init, body=body, operands=[pm, zc, pif, cw, sv, dym, cst, nst],
        in_specs=[pl.BlockSpec((L, M_W), lambda r: (cidx(r), 0)), pl.BlockSpec((L, 2 * W4), lambda r: (cidx(r), 0)),
                  pl.BlockSpec((L, IF_W), lambda r: (cidx(r), 0)), _full(cw.shape), _full(sv.shape),
                  pl.BlockSpec((L, W4), lambda r: (cidx(r), 0)),
                  pl.BlockSpec((1, MLSTM_HEADS, HD, HD), lambda r: (cidx(r), 0, 0, 0)),
                  pl.BlockSpec((1, 8, LANES), lambda r: (cidx(r), 0, 0))],
        out_specs=[pl.BlockSpec((L, M_W + IF_W), lambda r: (cidx(r), 0)), _full((8, 2 * W4)), _full((8, W4))],
        out_shape=[jax.ShapeDtypeStruct((S, M_W + IF_W), BF16),
                   jax.ShapeDtypeStruct((8, 2 * W4), F32), jax.ShapeDtypeStruct((8, W4), F32)],
        scratch=[pltpu.VMEM((MLSTM_HEADS, HD, HD), F32), pltpu.VMEM((8, LANES), F32),
                 pltpu.VMEM((8, 2 * W4), F32), pltpu.VMEM((L, 2 * W4), F32)])


def _rope_tables(positions):
    half = HEAD_DIM // 2
    inv_freq = ROPE_THETA ** (-2.0 * jnp.arange(half, dtype=F32) / HEAD_DIM)
    ang = positions.astype(F32)[:, None] * inv_freq
    cos = jnp.tile(jnp.cos(ang), (1, LANES // half))
    sign = jnp.tile(jnp.concatenate([-jnp.ones((half,), F32), jnp.ones((half,), F32)]), LANES // HEAD_DIM)
    sin = jnp.tile(jnp.sin(ang), (1, LANES // half)) * sign
    return cos, sin


def _local_step(x, tgt, positions, mod, gains, w_cat, w_ba, w_bm, w_out, w_gate, w_up, w_down,
                conv_w, conv_b, b_if, sinks, norm_w):
    t = _tables(mod, gains, conv_w, conv_b, b_if, norm_w, positions)
    a = _mixer_fwd(x, t, sinks, w_cat)
    b = _ffn_part(x, tgt, t, a, w_ba, w_bm, w_out, w_gate, w_up, w_down)
    c = _mixer_bwd(b["dx1"], t, a, b, sinks, w_ba, w_bm, w_out)
    grad_x, acc_p = _pre_bwd(c["dproj"], x, b["dx1"], t["vecs"], w_cat)
    big = dict(w_cat=jnp.concatenate(c["g_w_cat"], axis=1), w_ba=c["g_w_ba"], w_bm=c["g_w_bm"], w_out=c["g_w_out"], w_gate=b["g_w_gate"],
               w_up=b["g_w_up"], w_down=b["g_w_down"])
    return b["loss"], grad_x, big, _small_grads(acc_p, b, c)


def _tables(mod, gains, conv_w, conv_b, b_if, norm_w, positions):
    cos, sin = _rope_tables(positions)
    return dict(
        vecs=jnp.concatenate([mod, gains, jnp.zeros((6, D_MODEL), F32)], axis=0),
        cw=jnp.concatenate([conv_w, conv_b.reshape(1, -1), jnp.zeros((3, 2 * 512), F32)], axis=0),
        sv=jnp.zeros((8, 512), F32).at[0].set(norm_w).at[1, 0:8].set(b_if), cos=cos, sin=sin)


def _mixer_fwd(x, t, sinks, w_cat):
    h, pa, pm, pif, pg = _pre_proj(x, t["vecs"], w_cat)
    n_blk = x.shape[0] // ATTN_BLOCK
    (ya, qr, kr, vb, lse), = _fused_call([_attn_fwd_part(pa, t["cos"], t["sin"], sinks)], "attn_fwd", n_blk)
    (ym, zc, cst, nst), = _fused_call([_mlstm_fwd_part(pm, pif, t["cw"], t["sv"])], "mlstm_fwd", n_blk)
    return dict(h=h, pm=pm, pif=pif, pg=pg, ya=ya, qr=qr, kr=kr, vb=vb, lse=lse, ym=ym, zc=zc, cst=cst, nst=nst)


def _ffn_part(x, tgt, t, a, w_ba, w_bm, w_out, w_gate, w_up, w_down):
    x1, merged, mix, pba, pbm = _mix_fwd(x, a["ya"], a["ym"], a["pg"], t["vecs"], w_ba, w_bm, w_out)
    dx1, h2, hid, da, du, dff, acc_f, loss = _ffn_fwd_bwd(x1, tgt, t["vecs"], w_gate, w_up, w_down)
    return dict(merged=merged, mix=mix, pba=pba, pbm=pbm, dx1=dx1, acc_f=acc_f, loss=loss[0, 0],
                g_w_gate=_matmul_tn(da, h2, 1024, "dw_ffn_gate"),
                g_w_up=_matmul_tn(du, h2, 1024, "dw_ffn_up"),
                g_w_down=_matmul_tn(hid, dff, 1024, "dw_ffn_down"))


def _mixer_bwd(dx1, t, a, b, sinks, w_ba, w_bm, w_out):
    dmix, dpa, dpb, dg, dya, dym, acc_m = _mix_bwd(dx1, b["mix"], b["pba"], b["pbm"], a["pg"], t["vecs"], w_ba, w_bm, w_out)
    g_w_out = _matmul_tn(b["merged"], dmix, 1024, "dw_out")
    g_w_ba = _matmul_tn(a["ya"], dpa, 1024, "dw_branch_attn")
    g_w_bm = _matmul_tn(a["ym"], dpb, 1024, "dw_branch_mlstm")
    n_blk = dx1.shape[0] // ATTN_BLOCK
    (dqkv, dqkv_last, dsink), = _fused_call(
        [_attn_bwd_part(dya, a["qr"], a["kr"], a["vb"], a["lse"], t["cos"], t["sin"], sinks)], "attn_bwd", n_blk)
    (dm, dcw, dsv), = _fused_call(
        [_mlstm_bwd_part(a["pm"], a["zc"], a["pif"], t["cw"], t["sv"], dym, a["cst"], a["nst"])], "mlstm_bwd", n_blk)
    dqkv = lax.dynamic_update_slice(dqkv, dqkv_last, (dqkv.shape[0] - ATTN_BLOCK, 0))
    dproj = [dqkv, dm, dg]
    g_w_cat = [_matmul_tn(a["h"], p, 1024 if p.shape[1] % 1024 == 0 else p.shape[1], "dw_in_" + n)
               for p, n in zip(dproj, ("attn", "mlstm", "branch"))]
    return dict(dproj=dproj, g_w_cat=g_w_cat, g_w_out=g_w_out, g_w_ba=g_w_ba,
                g_w_bm=g_w_bm, acc_m=acc_m, dsink=dsink, dcw=dcw, dsv=dsv)


def _small_grads(acc_p, b, c):
    acc_f, acc_m = b["acc_f"], c["acc_m"]
    dmod = jnp.stack([acc_p[1], acc_p[0], acc_m[0], acc_f[3], acc_f[2], acc_f[0]])
    dgains = jnp.stack([acc_p[2], acc_m[1], acc_f[4], acc_f[1]])
    return dict(dmod=dmod, dgains=dgains, dconv_w=c["dcw"][0:4], dconv_b=c["dcw"][4], db_if=c["dsv"][1, 0:8],
                dsinks=c["dsink"][0, 0:8], dnorm_w=c["dsv"][0])


MESH_ID = pl.DeviceIdType.MESH


def _mesh_pos():
    return lax.axis_index("x"), lax.axis_index("y"), lax.axis_index("c")


def _flip(v, bit):
    return 1 - v if bit else v


def _relations():
    return [((r >> 2) & 1, (r >> 1) & 1, r & 1) for r in range(1, N_DEV)]


def _small_exchange(p, gather, name):
    R, V = p.shape[-2:]

    def body(p_ref, out_ref, send_sems, recv_sems):
        x, y, c = _mesh_pos()
        me = 4 * x + 2 * y + c
        out_ref[me] = p_ref[...] if gather else p_ref[me]
        peers = []
        for dx, dy, dc in _relations():
            px, py, pc = _flip(x, dx), _flip(y, dy), _flip(c, dc)
            peers.append(((px, py, pc), 4 * px + 2 * py + pc))

        def copy(k, landing):
            peer, pid = peers[k]
            return pltpu.make_async_remote_copy(
                src_ref=p_ref if gather else p_ref.at[pid], dst_ref=out_ref.at[landing],
                send_sem=send_sems.at[k], recv_sem=recv_sems.at[k], device_id=peer, device_id_type=MESH_ID)

        sends = [copy(k, me) for k in range(N_DEV - 1)]
        for cp in sends:
            cp.start()
        for k in range(N_DEV - 1):
            copy(k, peers[k][1]).wait_recv()
        for cp in sends:
            cp.wait_send()

    vm = pl.BlockSpec(memory_space=pltpu.VMEM)
    return pl.pallas_call(
        body, name=name, in_specs=[vm], out_specs=vm,
        out_shape=jax.ShapeDtypeStruct((N_DEV, R, V), F32),
        scratch_shapes=[pltpu.SemaphoreType.DMA((N_DEV - 1,)), pltpu.SemaphoreType.DMA((N_DEV - 1,))],
        compiler_params=pltpu.CompilerParams(vmem_limit_bytes=VMEM_LIMIT),
    )(p)


HBM_SPEC = pl.BlockSpec(memory_space=pltpu.HBM)
SEM_SPEC = pl.BlockSpec(memory_space=pltpu.SEMAPHORE)


def _peers(x, y, c):
    out = []
    for dx, dy, dc in _relations():
        px, py, pc = _flip(x, dx), _flip(y, dy), _flip(c, dc)
        out.append(((px, py, pc), 4 * px + 2 * py + pc))
    return out


def _exchange_start(arrs, gather, after, name):
    n = len(arrs)
    me_out = 4 * lax.axis_index("x") + 2 * lax.axis_index("y") + lax.axis_index("c")
    lands = []
    for a in arrs:
        own = a[None] if gather else lax.dynamic_index_in_dim(a, me_out, 0, keepdims=True)
        empty = lax.empty(((N_DEV,) + a.shape) if gather else a.shape, a.dtype)
        lands.append(lax.dynamic_update_index_in_dim(empty, own, me_out, 0))

    def body(*refs):
        a_refs, l_refs = refs[:n], refs[n:2 * n]
        send_sems, recv_sems = refs[2 * n + 1], refs[2 * n + 2]
        token = refs[4 * n + 3]
        x, y, c = _mesh_pos()
        me = 4 * x + 2 * y + c
        for a in range(n):
            for k, (peer, pid) in enumerate(_peers(x, y, c)):
                pltpu.make_async_remote_copy(
                    src_ref=a_refs[a] if gather else a_refs[a].at[pid], dst_ref=l_refs[a].at[me],
                    send_sem=send_sems.at[a * (N_DEV - 1) + k], recv_sem=recv_sems.at[a * (N_DEV - 1) + k],
                    device_id=peer, device_id_type=MESH_ID).start()
        token[...] = jnp.zeros_like(token)

    sem = pltpu.SemaphoreType.DMA((n * (N_DEV - 1),))
    hbm = lambda a: pltpu.with_memory_space_constraint(a, pltpu.HBM)
    res = pl.pallas_call(
        body, name=name,
        out_shape=(sem, sem, *[pltpu.HBM(a.shape, a.dtype) for a in arrs], *[pltpu.HBM(l.shape, l.dtype) for l in lands],
                   jax.ShapeDtypeStruct((8, LANES), F32)),
        in_specs=[HBM_SPEC] * (2 * n) + [pl.BlockSpec(memory_space=pl.ANY)],
        out_specs=(SEM_SPEC, SEM_SPEC, *[HBM_SPEC] * (2 * n), pl.BlockSpec(memory_space=pltpu.VMEM)),
        input_output_aliases={i: 2 + i for i in range(2 * n)},
        compiler_params=pltpu.CompilerParams(has_side_effects=pltpu.SideEffectType.DATAFLOW_SIDE_EFFECTING),
    )(*[hbm(a) for a in arrs], *[hbm(l) for l in lands], after)
    return dict(sems=res[0:2], arrs=res[2:2 + n], lands=res[2 + n:2 + 2 * n], token=res[2 + 2 * n], gather=gather)


def _exchange_wait(st, after, name):
    n = len(st["arrs"])
    gather = st["gather"]

    def body(*refs):
        a_refs, l_refs = refs[:n], refs[n:2 * n]
        send_sems, recv_sems = refs[2 * n], refs[2 * n + 1]
        x, y, c = _mesh_pos()
        for a in range(n):
            for k, (peer, pid) in enumerate(_peers(x, y, c)):
                cp = pltpu.make_async_remote_copy(
                    src_ref=a_refs[a] if gather else a_refs[a].at[pid], dst_ref=l_refs[a].at[pid],
                    send_sem=send_sems.at[a * (N_DEV - 1) + k], recv_sem=recv_sems.at[a * (N_DEV - 1) + k],
                    device_id=peer, device_id_type=MESH_ID)
                cp.wait_send()
                cp.wait_recv()

    both = list(st["arrs"]) + list(st["lands"])
    res = pl.pallas_call(
        body, name=name, out_shape=[pltpu.HBM(a.shape, a.dtype) for a in both],
        in_specs=[HBM_SPEC] * (2 * n) + [SEM_SPEC, SEM_SPEC, pl.BlockSpec(memory_space=pl.ANY)],
        out_specs=[HBM_SPEC] * (2 * n), input_output_aliases={i: i for i in range(2 * n)},
        compiler_params=pltpu.CompilerParams(has_side_effects=pltpu.SideEffectType.DATAFLOW_SIDE_EFFECTING),
    )(*both, *st["sems"], after)
    return res[n:2 * n]


def _tie(x, token, name):
    def body(x_ref, t_ref, o_ref):
        o_ref[...] = x_ref[...]

    vm = pl.BlockSpec(memory_space=pltpu.VMEM)
    return pl.pallas_call(
        body, name=name, in_specs=[vm, pl.BlockSpec(memory_space=pl.ANY)], out_specs=vm,
        out_shape=jax.ShapeDtypeStruct(x.shape, x.dtype),
    )(x, token)


def _all_gather_hbm(shards):
    n = len(shards)

    def body(*refs):
        p_refs, out_refs = refs[:n], refs[n:2 * n]
        send_sems, recv_sems, local_sems = refs[2 * n:]
        x, y, c = _mesh_pos()
        me, sibling = (x, y, c), (x, y, 1 - c)
        chips = [(1 - x, y), (x, 1 - y), (1 - x, 1 - y)]

        def copy(a, k, block, to, own=False):
            slot = out_refs[a].at[4 * block[0] + 2 * block[1] + block[2]]
            return pltpu.make_async_remote_copy(
                src_ref=p_refs[a] if own else slot, dst_ref=slot,
                send_sem=send_sems.at[a, k], recv_sem=recv_sems.at[a, k], device_id=to, device_id_type=MESH_ID)

        mine = [pltpu.make_async_copy(p_refs[a], out_refs[a].at[4 * x + 2 * y + c], local_sems.at[a]) for a in range(n)]
        for cp in mine:
            cp.start()
        first = []
        for a in range(n):
            first.append(copy(a, 0, me, sibling, own=True))
            first += [copy(a, 1 + j, me, (*chip, c), own=True) for j, chip in enumerate(chips)]
        for cp in first:
            cp.start()
        passed = []
        for j, chip in enumerate(chips):
            for a in range(n):
                copy(a, 1 + j, (*chip, c), me).wait_recv()
                passed.append(copy(a, 4 + j, (*chip, c), sibling))
                passed[-1].start()
        for a in range(n):
            copy(a, 0, sibling, me).wait_recv()
            for j, chip in enumerate(chips):
                copy(a, 4 + j, (*chip, 1 - c), me).wait_recv()
        for cp in first + passed:
            cp.wait_send()
        for cp in mine:
            cp.wait()

    hbm = pl.BlockSpec(memory_space=pl.ANY)
    return pl.pallas_call(
        body, name="gather_weights", in_specs=[hbm] * n, out_specs=[hbm] * n,
        out_shape=[jax.ShapeDtypeStruct((N_DEV,) + s.shape, s.dtype) for s in shards],
        scratch_shapes=[pltpu.SemaphoreType.DMA((n, N_DEV - 1)), pltpu.SemaphoreType.DMA((n, N_DEV - 1)),
                        pltpu.SemaphoreType.DMA((n,))],
    )(*shards)


def _adamw(w, g, m, v):
    m2 = ADAM_B1 * m + (1.0 - ADAM_B1) * g
    v2 = ADAM_B2 * v + (1.0 - ADAM_B2) * (g * g)
    m_hat = m2 / (1.0 - ADAM_B1 ** ADAM_STEP)
    v_hat = v2 / (1.0 - ADAM_B2 ** ADAM_STEP)
    delta = -ADAM_LR * (m_hat / (jnp.sqrt(v_hat) + ADAM_EPS) + ADAM_WD * w)
    return delta, m2, v2


def _mod_partial(cmat, w_shard, b_shard):
    def body(c_ref, w_ref, b_ref, o_ref):
        o_ref[...] = _dot(c_ref[...].astype(BF16), w_ref[...].astype(BF16)) + b_ref[...]

    return pl.pallas_call(
        body, name="mod_partial", out_shape=jax.ShapeDtypeStruct((N_DEV, w_shard.shape[1]), F32),
        compiler_params=_params(),
    )(cmat, w_shard, b_shard)


def _adamw_w_ada(cmat, dmod_cols, w, m, v):
    r, cdim = w.shape
    tr = _row_tile(r)

    def body(c_ref, d_ref, w_ref, m_ref, v_ref, g_ref, dl_ref, m2_ref, v2_ref):
        g = _dot_tn(c_ref[...].astype(BF16), d_ref[...].astype(BF16))
        g_ref[...] = g
        dl_ref[...], m2_ref[...], v2_ref[...] = _adamw(w_ref[...], g, m_ref[...], v_ref[...])

    row = pl.BlockSpec((tr, cdim), lambda i: (i, 0))
    return pl.pallas_call(
        body, name="adamw_w_ada", grid=(r // tr,),
        in_specs=[pl.BlockSpec((N_DEV, tr), lambda i: (0, i)), _full(dmod_cols.shape), row, row, row],
        out_specs=[row] * 4, out_shape=[jax.ShapeDtypeStruct(w.shape, F32)] * 4,
        compiler_params=_params(("parallel",)),
    )(cmat, dmod_cols, w, m, v)


SMALL_ROWS = 16
SMALL_AT = {"b_ada": (0, 6, 0, D_MODEL), "g_pre_mix": (6, 1, 0, D_MODEL), "g_post_mix": (7, 1, 0, D_MODEL),
            "g_pre_ffn": (8, 1, 0, D_MODEL), "g_post_ffn": (9, 1, 0, D_MODEL), "conv_b": (10, 1, 0, D_MODEL),
            "mlstm_norm_w": (11, 1, 0, 512), "b_if": (11, 1, 512, LANES), "attn_sinks": (11, 1, 640, LANES)}


def _small_table(part):
    tail = jnp.concatenate([part["mlstm_norm_w"], jnp.pad(part["b_if"], (0, LANES - 8)),
                            jnp.pad(part["attn_sinks"], (0, LANES - 8)), jnp.zeros((256,), F32)])
    return jnp.concatenate([part["b_ada"], part["gains"], part["conv_b"][None], tail[None],
                            jnp.full((1, D_MODEL), part["loss"], F32),
                            jnp.zeros((SMALL_ROWS - 13, D_MODEL), F32)], axis=0)


LOSS_ROW = 12


def _adamw_small(gathered, wmv):
    names = list(SMALL_AT)

    def body(*refs):
        g_ref, ins, outs = refs[0], refs[1:1 + 3 * len(names)], refs[1 + 3 * len(names):]
        g = g_ref[0]
        for k in range(1, N_DEV):
            g = g + g_ref[k]
        outs[4 * len(names)][...] = g[LOSS_ROW:LOSS_ROW + 1, 0:LANES]
        for i, n in enumerate(names):
            r0, rows, l0, lanes = SMALL_AT[n]
            gi = jnp.concatenate([g[r:r + 1, l0:l0 + lanes] for r in range(r0, r0 + rows)], axis=1)
            w_ref, m_ref, v_ref = ins[3 * i:3 * i + 3]
            go, dl, m2, v2 = outs[4 * i:4 * i + 4]
            go[...] = gi
            dl[...], m2[...], v2[...] = _adamw(w_ref[...], gi, m_ref[...], v_ref[...])

    flat = [a for n in names for a in wmv[n]]
    res = pl.pallas_call(
        body, name="adamw_small",
        out_shape=[jax.ShapeDtypeStruct(wmv[n][0].shape, F32) for n in names for _ in range(4)]
        + [jax.ShapeDtypeStruct((1, LANES), F32)],
        compiler_params=_params(),
    )(gathered, *flat)
    out = {n: res[4 * i:4 * i + 4] for i, n in enumerate(names)}
    out["loss"] = res[4 * len(names)]
    return out


def _adamw_sum_many(items, name):
    n = len(items)

    def body(*refs):
        ins, outs = refs[:4 * n], refs[4 * n:]
        for i in range(n):
            r_ref, w_ref, m_ref, v_ref = ins[4 * i:4 * i + 4]
            go, dl, m2, v2 = outs[4 * i:4 * i + 4]
            g = _sum_partials(r_ref)
            go[...] = g
            dl[...], m2[...], v2[...] = _adamw(w_ref[...], g, m_ref[...], v_ref[...])

    res = pl.pallas_call(
        body, name=name, out_shape=[jax.ShapeDtypeStruct(it[1].shape, F32) for it in items for _ in range(4)],
        compiler_params=_params(),
    )(*[a for it in items for a in it])
    return [res[4 * i:4 * i + 4] for i in range(n)]


def _row_tile(rows):
    return rows // 4 if rows >= 512 and rows % 64 == 0 else rows


def _sum_partials(r_ref):
    g = r_ref[0].astype(F32)
    for k in range(1, N_DEV):
        g = g + r_ref[k].astype(F32)
    return g


def _adamw_sum(recv, w, m, v, name):
    r, cdim = w.shape
    tr = _row_tile(r)

    def body(r_ref, w_ref, m_ref, v_ref, g_ref, dl_ref, m2_ref, v2_ref):
        g = _sum_partials(r_ref)
        g_ref[...] = g
        dl_ref[...], m2_ref[...], v2_ref[...] = _adamw(w_ref[...], g, m_ref[...], v_ref[...])

    row = pl.BlockSpec((tr, cdim), lambda i: (i, 0))
    return pl.pallas_call(
        body, name=name, grid=(r // tr,),
        in_specs=[pl.BlockSpec((N_DEV, tr, cdim), lambda i: (0, i, 0)), row, row, row],
        out_specs=[row] * 4, out_shape=[jax.ShapeDtypeStruct((r, cdim), F32)] * 4,
        compiler_params=_params(("parallel",)),
    )(recv, w, m, v)


def _sum8(recv, name):
    _, r, cdim = recv.shape
    tr = _row_tile(r)

    def body(r_ref, g_ref):
        g_ref[...] = _sum_partials(r_ref)

    return pl.pallas_call(
        body, name=name, grid=(r // tr,),
        in_specs=[pl.BlockSpec((N_DEV, tr, cdim), lambda i: (0, i, 0))],
        out_specs=pl.BlockSpec((tr, cdim), lambda i: (i, 0)), out_shape=jax.ShapeDtypeStruct((r, cdim), F32),
        compiler_params=_params(("parallel",)),
    )(recv)


def _adamw_plain(g, w, m, v, name):
    r, cdim = w.shape
    tr = _row_tile(r)

    def body(g_ref, w_ref, m_ref, v_ref, dl_ref, m2_ref, v2_ref):
        dl_ref[...], m2_ref[...], v2_ref[...] = _adamw(w_ref[...], g_ref[...], m_ref[...], v_ref[...])

    row = pl.BlockSpec((tr, cdim), lambda i: (i, 0))
    return pl.pallas_call(
        body, name=name, grid=(r // tr,), in_specs=[row] * 4, out_specs=[row] * 3,
        out_shape=[jax.ShapeDtypeStruct((r, cdim), F32)] * 3,
        compiler_params=_params(("parallel",)),
    )(g, w, m, v)


IN_SHARD = 609
IN_SHARD_PAD = 640
IF_AT = A_W + M_W


def _regrouped(u):
    return u if u < IF_AT + 8 else u + (IF_W - 8)


def _selection(k, rows, row0, transpose):
    shape = (rows, IN_SHARD_PAD) if transpose else (IN_SHARD_PAD, rows)
    l = lax.broadcasted_iota(jnp.int32, shape, 1 if transpose else 0)
    r = lax.broadcasted_iota(jnp.int32, shape, 0 if transpose else 1) + row0
    u = l + IN_SHARD * k
    ru = u + jnp.where(u >= IF_AT + 8, IF_W - 8, 0)
    return ((ru == r) & (l < IN_SHARD)).astype(BF16)


def _regroup_w_in(g):
    def body(g_ref, o_ref):
        for cb in range(CAT_W // LANES):
            r0 = cb * LANES
            acc = jnp.zeros((D_MODEL, LANES), F32)
            for k in range(N_DEV):
                lo, hi = _regrouped(IN_SHARD * k), _regrouped(IN_SHARD * k + IN_SHARD - 1)
                if hi >= r0 and lo < r0 + LANES:
                    acc = acc + _dot(g_ref[k], _selection(k, LANES, r0, False))
            o_ref[:, r0:r0 + LANES] = acc.astype(BF16)

    return pl.pallas_call(
        body, name="regroup_w_in", out_shape=jax.ShapeDtypeStruct((D_MODEL, CAT_W), BF16),
        compiler_params=_params(),
    )(g)


def _ungroup_w_in(g_parts):
    n = len(g_parts)

    def body(*refs):
        o_ref, g_ref = refs[n], refs[n + 1]
        at = 0
        for p in refs[:n]:
            g_ref[:, at:at + p.shape[1]] = p[...]
            at += p.shape[1]
        for k in range(N_DEV):
            lo, hi = _regrouped(IN_SHARD * k), _regrouped(IN_SHARD * k + IN_SHARD - 1)
            w0, w1 = lo // LANES * LANES, (hi // LANES + 1) * LANES
            o_ref[k] = _dot(g_ref[:, w0:w1], _selection(k, w1 - w0, w0, True)).astype(BF16)

    return pl.pallas_call(
        body, name="ungroup_w_in", out_shape=jax.ShapeDtypeStruct((N_DEV, D_MODEL, IN_SHARD_PAD), BF16),
        scratch_shapes=[pltpu.VMEM((D_MODEL, CAT_W), BF16)], compiler_params=_params(),
    )(*g_parts)


WEIGHT_NAMES = ("w_ada", "b_ada", "g_pre_mix", "g_post_mix", "w_in", "b_if", "conv_w", "conv_b", "attn_sinks",
                "mlstm_norm_w", "w_branch_attn", "w_branch_mlstm", "w_out", "g_pre_ffn", "g_post_ffn",
                "w_ffn_gate", "w_ffn_up", "w_ffn_down")


def kernel(x, c, positions, w_ada, b_ada, g_pre_mix, g_post_mix, w_in, b_if, conv_w, conv_b, attn_sinks, mlstm_norm_w, w_branch_attn, w_branch_mlstm, w_out, g_pre_ffn, g_post_ffn, w_ffn_gate, w_ffn_up, w_ffn_down, loss_target, m_w_ada, m_b_ada, m_g_pre_mix, m_g_post_mix, m_w_in, m_b_if, m_conv_w, m_conv_b, m_attn_sinks, m_mlstm_norm_w, m_w_branch_attn, m_w_branch_mlstm, m_w_out, m_g_pre_ffn, m_g_post_ffn, m_w_ffn_gate, m_w_ffn_up, m_w_ffn_down, v_w_ada, v_b_ada, v_g_pre_mix, v_g_post_mix, v_w_in, v_b_if, v_conv_w, v_conv_b, v_attn_sinks, v_mlstm_norm_w, v_w_branch_attn, v_w_branch_mlstm, v_w_out, v_g_pre_ffn, v_g_post_ffn, v_w_ffn_gate, v_w_ffn_up, v_w_ffn_down):
    given = dict(locals())
    W = {n: given[n][0] for n in WEIGHT_NAMES}
    M = {n: given["m_" + n][0] for n in WEIGHT_NAMES}
    V = {n: given["v_" + n][0] for n in WEIGHT_NAMES}
    me = 4 * lax.axis_index("x") + 2 * lax.axis_index("y") + lax.axis_index("c")

    ff_sh = D_FF // N_DEV
    g_in, g_conv, cg = _all_gather_hbm([jnp.pad(W["w_in"], ((0, 0), (0, IN_SHARD_PAD - IN_SHARD))).astype(BF16),
                                        jnp.pad(W["conv_w"], ((0, 4), (0, 0))), c.reshape(8, D_MODEL // 8)])

    cmat = cg.reshape(N_DEV, D_MODEL)
    ada_w = D_MODEL * 6 // N_DEV
    b_cols = lax.dynamic_slice(W["b_ada"], (me * ada_w,), (ada_w,)).reshape(1, ada_w)
    mod_part = _mod_partial(cmat, W["w_ada"], b_cols)
    mod_recv = _small_exchange(jnp.broadcast_to(mod_part[:, None, :], (N_DEV, 8, ada_w)), False, "scatter_mod")
    mod = mod_recv[:, 0, :].reshape(6, D_MODEL)

    st_b = _exchange_start([W["w_branch_attn"].astype(BF16), W["w_branch_mlstm"].astype(BF16), W["w_out"].astype(BF16),
                            W["w_ffn_gate"].T.astype(BF16), W["w_ffn_up"].T.astype(BF16), W["w_ffn_down"].astype(BF16)],
                           True, mod_recv, "gather_rest_start")
    cols = lambda g: g.transpose(1, 0, 2).reshape(g.shape[1], N_DEV * g.shape[2])
    gains = jnp.stack([W["g_pre_mix"], W["g_post_mix"], W["g_pre_ffn"], W["g_post_ffn"]])
    xs, tgt = x[0], loss_target[0]
    t = _tables(mod, gains, cols(g_conv)[0:4], W["conv_b"], W["b_if"], W["mlstm_norm_w"], positions[0])
    vecs = t["vecs"]
    t["vecs"] = _tie(vecs, st_b["token"], "tie_fwd")
    w_cat = _regroup_w_in(g_in)
    a = _mixer_fwd(xs, t, W["attn_sinks"], w_cat)
    g_ba, g_bm, g_out, g_gate, g_up, g_down = _exchange_wait(st_b, a["ym"], "gather_rest_wait")
    w_ba, w_bm, w_out = cols(g_ba), cols(g_bm), g_out.reshape(D_MODEL, D_MODEL)
    b = _ffn_part(xs, tgt, t, a, w_ba, w_bm, w_out, g_gate.reshape(D_FF, D_MODEL), g_up.reshape(D_FF, D_MODEL),
                  g_down.reshape(D_FF, D_MODEL))

    st_f = _exchange_start([b["g_w_gate"].reshape(N_DEV, ff_sh, D_MODEL), b["g_w_up"].reshape(N_DEV, ff_sh, D_MODEL),
                            b["g_w_down"].reshape(N_DEV, ff_sh, D_MODEL)], False, b["dx1"], "scatter_ffn_start")
    t["vecs"] = _tie(vecs, st_f["token"], "tie_bwd")
    cm = _mixer_bwd(b["dx1"], t, a, b, W["attn_sinks"], w_ba, w_bm, w_out)
    pieces = lambda g, n: g.reshape(g.shape[0], N_DEV, n).transpose(1, 0, 2)
    st_m = _exchange_start([_ungroup_w_in(cm["g_w_cat"]), pieces(cm["g_w_ba"], 128), pieces(cm["g_w_bm"], 128),
                            cm["g_w_out"].reshape(N_DEV, D_MODEL // N_DEV, D_MODEL),
                            jnp.pad(pieces(cm["dcw"][0:4], 128), ((0, 0), (0, 4), (0, 0)))], False, cm["dcw"],
                           "scatter_mixer_start")
    r_gate, r_up, r_down = _exchange_wait(st_f, st_m["token"], "scatter_ffn_wait")
    grad_x, acc_p = _pre_bwd(cm["dproj"], xs, b["dx1"], _tie(vecs, st_m["token"], "tie_pre_bwd"), w_cat)
    small = _small_grads(acc_p, b, cm)
    loss = b["loss"]

    big_out = [{} for _ in range(4)]

    def put(n, res):
        for k in range(4):
            big_out[k][n] = res[k][None]

    put("w_ffn_down", _adamw_sum(r_down, W["w_ffn_down"], M["w_ffn_down"], V["w_ffn_down"], "adamw_w_ffn_down"))
    for n, r in (("w_ffn_gate", r_gate), ("w_ffn_up", r_up)):
        put(n, [o.T for o in _adamw_sum(r, W[n].T, M[n].T, V[n].T, "adamw_" + n)])

    sg = _small_exchange(_small_table({"b_ada": small["dmod"], "gains": small["dgains"], "conv_b": small["dconv_b"],
                                       "mlstm_norm_w": small["dnorm_w"], "b_if": small["db_if"],
                                       "attn_sinks": small["dsinks"], "loss": loss}), True, "gather_small")
    as_row = lambda a, n: jnp.pad(a, (0, SMALL_AT[n][3] * SMALL_AT[n][1] - a.shape[0]))[None]
    small_res = _adamw_small(sg, {n: [as_row(d[n], n) for d in (W, M, V)] for n in SMALL_AT})
    small_out = [{n: small_res[n][k][:, 0:W[n].shape[0]] for n in SMALL_AT} for k in range(4)]
    dmod_cols = lax.dynamic_slice(sg[:, 0:6, :].reshape(N_DEV, 6 * D_MODEL), (0, me * ada_w), (N_DEV, ada_w))
    ada_out = _adamw_w_ada(cmat, dmod_cols, W["w_ada"], M["w_ada"], V["w_ada"])

    r_in, r_ba, r_bm, r_out, r_conv = _exchange_wait(st_m, ada_out[1], "scatter_mixer_wait")
    pad4 = lambda v: jnp.pad(v, ((0, 4), (0, 0)))
    res_ba, res_bm, res_out, res_conv = _adamw_sum_many(
        [(r, W[n], M[n], V[n]) for n, r in (("w_branch_attn", r_ba), ("w_branch_mlstm", r_bm), ("w_out", r_out))]
        + [(r_conv, pad4(W["conv_w"]), pad4(M["conv_w"]), pad4(V["conv_w"]))], "adamw_mixer_small")
    put("w_branch_attn", res_ba)
    put("w_branch_mlstm", res_bm)
    put("w_out", res_out)
    put("conv_w", [o[0:4] for o in res_conv])
    g = _sum8(r_in, "sum_w_in")[:, 0:IN_SHARD].T
    put("w_in", [o.T for o in [g] + list(_adamw_plain(g, W["w_in"].T, M["w_in"].T, V["w_in"].T, "adamw_w_in"))])

    outs = [small_res["loss"][0, 0], grad_x[None]]
    for k in range(4):
        for n in WEIGHT_NAMES:
            if n == "w_ada":
                outs.append(ada_out[k][None])
            elif n in big_out[k]:
                outs.append(big_out[k][n])
            else:
                outs.append(small_out[k][n])
    return tuple(outs)
```

```python
import jax
import jax.numpy as jnp
from jax import lax
from jax.experimental import pallas as pl
from jax.experimental.pallas import tpu as pltpu

F32 = jnp.float32
BF16 = jnp.bfloat16

N_DEV = 8
D_MODEL = 1024
D_FF = 2816
N_Q_HEADS = 8
HEAD_DIM = 64
ATTN_BLOCK = 128
ROPE_THETA = 10000.0
MLSTM_HEADS = 4
MLSTM_HEAD_DIM = 128
MLSTM_CHUNK = 128
NORM_EPS = 1e-6
ADAM_LR = 0.001
ADAM_B1 = 0.9
ADAM_B2 = 0.999
ADAM_EPS = 1e-08
ADAM_WD = 0.01
ADAM_STEP = 10

ROW_TILE = 256
WIDE_TILE = 512
LANES = 128
NEG = -1e30
VMEM_LIMIT = 56 * 1024 * 1024

A_W = 768
M_W = 2048
IF_W = 128
G_W = 2048
CAT_W = A_W + M_W + IF_W + G_W

R_SHIFT_M, R_SCALE_M, R_GATE_M, R_SHIFT_F, R_SCALE_F, R_GATE_F = 0, 1, 2, 3, 4, 5
R_G_PRE_MIX, R_G_POST_MIX, R_G_PRE_FFN, R_G_POST_FFN = 6, 7, 8, 9


def _dot(a, b):
    return jnp.dot(a, b, preferred_element_type=F32)


def _dot_nt(a, b):
    return lax.dot_general(a, b, (((1,), (1,)), ((), ())), preferred_element_type=F32)


def _dot_tn(a, b):
    return lax.dot_general(a, b, (((0,), (0,)), ((), ())), preferred_element_type=F32)


def _recip(x):
    return 1.0 / x


def _sigmoid(x):
    return _recip(1.0 + jnp.exp(-x))


def _colsum(x):
    return jnp.sum(x, axis=0, keepdims=True)


def _rowmean(x):
    return jnp.mean(x, axis=-1, keepdims=True)


def _params(sem=None, vmem=VMEM_LIMIT):
    kw = dict(vmem_limit_bytes=vmem)
    if sem is not None:
        kw["dimension_semantics"] = sem
    return pltpu.CompilerParams(**kw)


def _full(shape):
    nd = len(shape)
    return pl.BlockSpec(shape, lambda *_: (0,) * nd)


def _pre_proj(x, vecs, w_cat):
    S = x.shape[0]
    tm = WIDE_TILE

    def body(x_ref, v_ref, w_ref, h_ref, pa_ref, pm_ref, pif_ref, pg_ref):
        xv = x_ref[...]
        r = lax.rsqrt(_rowmean(xv * xv) + NORM_EPS)
        h = (xv * r * v_ref[R_G_PRE_MIX:R_G_PRE_MIX + 1, :]) * (1.0 + v_ref[R_SCALE_M:R_SCALE_M + 1, :]) \
            + v_ref[R_SHIFT_M:R_SHIFT_M + 1, :]
        hb = h.astype(BF16)
        h_ref[...] = hb
        pa_ref[...] = _dot(hb, w_ref[:, 0:A_W])
        pm_ref[...] = _dot(hb, w_ref[:, A_W:A_W + M_W])
        pif_ref[...] = _dot(hb, w_ref[:, A_W + M_W:A_W + M_W + IF_W])
        pg_ref[...] = _dot(hb, w_ref[:, A_W + M_W + IF_W:CAT_W]).astype(BF16)

    row = lambda w: pl.BlockSpec((tm, w), lambda i: (i, 0))
    return pl.pallas_call(
        body, name="pre_proj", grid=(S // tm,),
        in_specs=[row(D_MODEL), _full(vecs.shape), _full(w_cat.shape)],
        out_specs=[row(D_MODEL), row(A_W), row(M_W), row(IF_W), row(G_W)],
        out_shape=[jax.ShapeDtypeStruct((S, D_MODEL), BF16), jax.ShapeDtypeStruct((S, A_W), F32),
                   jax.ShapeDtypeStruct((S, M_W), F32), jax.ShapeDtypeStruct((S, IF_W), F32),
                   jax.ShapeDtypeStruct((S, G_W), BF16)],
        compiler_params=_params(("parallel",)),
    )(x, vecs, w_cat)


def _mix_fwd(x, ya, ym, pg, vecs, w_ba, w_bm, w_out):
    S = x.shape[0]
    tm = WIDE_TILE

    def body(x_ref, ya_ref, ym_ref, pg_ref, v_ref, wba_ref, wbm_ref, wout_ref,
             x1_ref, merged_ref, mix_ref, pa_ref, pb_ref):
        pa = _dot(ya_ref[...], wba_ref[...])
        pb = _dot(ym_ref[...], wbm_ref[...])
        merged = _sigmoid(pg_ref[:, 0:D_MODEL].astype(F32)) * pa + _sigmoid(pg_ref[:, D_MODEL:G_W].astype(F32)) * pb
        mb = merged.astype(BF16)
        mix = _dot(mb, wout_ref[...])
        r = lax.rsqrt(_rowmean(mix * mix) + NORM_EPS)
        x1_ref[...] = x_ref[...] + v_ref[R_GATE_M:R_GATE_M + 1, :] * (mix * r * v_ref[R_G_POST_MIX:R_G_POST_MIX + 1, :])
        merged_ref[...] = mb
        mix_ref[...] = mix.astype(BF16)
        pa_ref[...] = pa.astype(BF16)
        pb_ref[...] = pb.astype(BF16)

    row = lambda w: pl.BlockSpec((tm, w), lambda i: (i, 0))
    sd = lambda w, dt: jax.ShapeDtypeStruct((S, w), dt)
    return pl.pallas_call(
        body, name="mix_fwd", grid=(S // tm,),
        in_specs=[row(D_MODEL), row(512), row(512), row(G_W), _full(vecs.shape), _full(w_ba.shape),
                  _full(w_bm.shape), _full(w_out.shape)],
        out_specs=[row(D_MODEL)] * 5,
        out_shape=[sd(D_MODEL, F32), sd(D_MODEL, BF16), sd(D_MODEL, BF16), sd(D_MODEL, BF16), sd(D_MODEL, BF16)],
        compiler_params=_params(("parallel",)),
    )(x, ya, ym, pg, vecs, w_ba, w_bm, w_out)


def _ffn_fwd_bwd(x1, tgt, vecs, w_gate, w_up, w_down):
    S = x1.shape[0]
    tm = ROW_TILE

    def body(x1_ref, tgt_ref, v_ref, wg_hbm, wu_hbm, wd_hbm,
             dx1_ref, h2_ref, hid_ref, da_ref, du_ref, dff_ref, acc_ref, loss_ref,
             wg, wu, wd, sem):
        i = pl.program_id(0)

        @pl.when(i == 0)
        def _():
            cps = [pltpu.make_async_copy(wg_hbm, wg, sem.at[0]), pltpu.make_async_copy(wu_hbm, wu, sem.at[1]),
                   pltpu.make_async_copy(wd_hbm, wd, sem.at[2])]
            for cp in cps:
                cp.start()
            for cp in cps:
                cp.wait()
            acc_ref[...] = jnp.zeros_like(acc_ref)
            loss_ref[...] = jnp.zeros_like(loss_ref)

        vrow = lambda r: v_ref[r:r + 1, :]
        x1v = x1_ref[...]
        r3 = lax.rsqrt(_rowmean(x1v * x1v) + NORM_EPS)
        x1hat = x1v * r3
        xn3 = x1hat * vrow(R_G_PRE_FFN)
        h2b = (xn3 * (1.0 + vrow(R_SCALE_F)) + vrow(R_SHIFT_F)).astype(BF16)
        h2_ref[...] = h2b
        a = _dot_nt(h2b, wg[...])
        u = _dot_nt(h2b, wu[...])
        sg = _sigmoid(a)
        sil = a * sg
        hidb = (sil * u).astype(BF16)
        hid_ref[...] = hidb
        ff = _dot(hidb, wd[...])
        r4 = lax.rsqrt(_rowmean(ff * ff) + NORM_EPS)
        ffhat = ff * r4
        n4 = ffhat * vrow(R_G_POST_FFN)
        err = x1v + vrow(R_GATE_F) * n4 - tgt_ref[...]
        loss_ref[...] += jnp.sum(err * err) * (0.5 / D_MODEL)
        dy = err * (1.0 / D_MODEL)
        acc_ref[0:1, :] += _colsum(dy * n4)
        dn4 = dy * vrow(R_GATE_F)
        acc_ref[1:2, :] += _colsum(dn4 * ffhat)
        dffhat = dn4 * vrow(R_G_POST_FFN)
        dffb = (r4 * (dffhat - ffhat * _rowmean(dffhat * ffhat))).astype(BF16)
        dff_ref[...] = dffb
        dhid = _dot_nt(dffb, wd[...])
        dub = (dhid * sil).astype(BF16)
        dab = (dhid * u * (sg * (1.0 + a * (1.0 - sg)))).astype(BF16)
        da_ref[...] = dab
        du_ref[...] = dub
        dh2 = _dot(dab, wg[...]) + _dot(dub, wu[...])
        acc_ref[2:3, :] += _colsum(dh2 * xn3)
        acc_ref[3:4, :] += _colsum(dh2)
        dxn3 = dh2 * (1.0 + vrow(R_SCALE_F))
        acc_ref[4:5, :] += _colsum(dxn3 * x1hat)
        dx1hat = dxn3 * vrow(R_G_PRE_FFN)
        dx1_ref[...] = dy + r3 * (dx1hat - x1hat * _rowmean(dx1hat * x1hat))

    row = lambda w: pl.BlockSpec((tm, w), lambda i: (i, 0))
    sd = lambda w, dt: jax.ShapeDtypeStruct((S, w), dt)
    anyspec = pl.BlockSpec(memory_space=pl.ANY)
    return pl.pallas_call(
        body, name="ffn_fwd_bwd", grid=(S // tm,),
        in_specs=[row(D_MODEL), row(D_MODEL), _full(vecs.shape), anyspec, anyspec, anyspec],
        out_specs=[row(D_MODEL), row(D_MODEL), row(D_FF), row(D_FF), row(D_FF), row(D_MODEL),
                   _full((8, D_MODEL)), _full((8, LANES))],
        out_shape=[sd(D_MODEL, F32), sd(D_MODEL, BF16), sd(D_FF, BF16), sd(D_FF, BF16), sd(D_FF, BF16),
                   sd(D_MODEL, BF16), jax.ShapeDtypeStruct((8, D_MODEL), F32), jax.ShapeDtypeStruct((8, LANES), F32)],
        scratch_shapes=[pltpu.VMEM(w_gate.shape, BF16), pltpu.VMEM(w_up.shape, BF16), pltpu.VMEM(w_down.shape, BF16),
                        pltpu.SemaphoreType.DMA((3,))],
        compiler_params=_params(("arbitrary",)),
    )(x1, tgt, vecs, w_gate, w_up, w_down)


def _mix_bwd(dx1, mix, pa, pb, pg, vecs, w_ba, w_bm, w_out):
    S = dx1.shape[0]
    tm = WIDE_TILE

    def body(dx1_ref, mix_ref, pa_ref, pb_ref, pg_ref, v_ref, wba_ref, wbm_ref, wout_ref,
             dmix_ref, dpa_ref, dpb_ref, dg_ref, dya_ref, dym_ref, acc_ref):
        i = pl.program_id(0)

        @pl.when(i == 0)
        def _():
            acc_ref[...] = jnp.zeros_like(acc_ref)

        vrow = lambda r: v_ref[r:r + 1, :]
        dx1v = dx1_ref[...]
        mix = mix_ref[...].astype(F32)
        r2 = lax.rsqrt(_rowmean(mix * mix) + NORM_EPS)
        mixhat = mix * r2
        acc_ref[0:1, :] += _colsum(dx1v * (mixhat * vrow(R_G_POST_MIX)))
        dn2 = dx1v * vrow(R_GATE_M)
        acc_ref[1:2, :] += _colsum(dn2 * mixhat)
        dmixhat = dn2 * vrow(R_G_POST_MIX)
        dmixb = (r2 * (dmixhat - mixhat * _rowmean(dmixhat * mixhat))).astype(BF16)
        dmix_ref[...] = dmixb
        dmerged = _dot_nt(dmixb, wout_ref[...])
        sa = _sigmoid(pg_ref[:, 0:D_MODEL].astype(F32))
        sm = _sigmoid(pg_ref[:, D_MODEL:G_W].astype(F32))
        dpab = (dmerged * sa).astype(BF16)
        dpbb = (dmerged * sm).astype(BF16)
        dpa_ref[...] = dpab
        dpb_ref[...] = dpbb
        dg_ref[:, 0:D_MODEL] = (dmerged * pa_ref[...].astype(F32) * (sa * (1.0 - sa))).astype(BF16)
        dg_ref[:, D_MODEL:G_W] = (dmerged * pb_ref[...].astype(F32) * (sm * (1.0 - sm))).astype(BF16)
        dya_ref[...] = _dot_nt(dpab, wba_ref[...]).astype(BF16)
        dym_ref[...] = _dot_nt(dpbb, wbm_ref[...])

    row = lambda w: pl.BlockSpec((tm, w), lambda i: (i, 0))
    sd = lambda w, dt: jax.ShapeDtypeStruct((S, w), dt)
    return pl.pallas_call(
        body, name="mix_bwd", grid=(S // tm,),
        in_specs=[row(D_MODEL), row(D_MODEL), row(D_MODEL), row(D_MODEL), row(G_W), _full(vecs.shape),
                  _full(w_ba.shape), _full(w_bm.shape), _full(w_out.shape)],
        out_specs=[row(D_MODEL), row(D_MODEL), row(D_MODEL), row(G_W), row(512), row(512), _full((8, D_MODEL))],
        out_shape=[sd(D_MODEL, BF16), sd(D_MODEL, BF16), sd(D_MODEL, BF16), sd(G_W, BF16), sd(512, BF16), sd(512, F32),
                   jax.ShapeDtypeStruct((8, D_MODEL), F32)],
        compiler_params=_params(("arbitrary",)),
    )(dx1, mix, pa, pb, pg, vecs, w_ba, w_bm, w_out)


def _pre_bwd(pieces, x, dx1, vecs, w_cat):
    S = x.shape[0]
    tm = WIDE_TILE
    n = len(pieces)
    starts = [sum(p.shape[1] for p in pieces[:k]) for k in range(n + 1)]

    def body(*refs):
        p_refs = refs[:n]
        x_ref, dx1_ref, v_ref, w_ref, dx_ref, acc_ref = refs[n:]
        i = pl.program_id(0)

        @pl.when(i == 0)
        def _():
            acc_ref[...] = jnp.zeros_like(acc_ref)

        vrow = lambda r: v_ref[r:r + 1, :]
        dh = _dot_nt(p_refs[0][...], w_ref[:, starts[0]:starts[1]])
        for k in range(1, n):
            dh = dh + _dot_nt(p_refs[k][...], w_ref[:, starts[k]:starts[k + 1]])
        xv = x_ref[...]
        r1 = lax.rsqrt(_rowmean(xv * xv) + NORM_EPS)
        xhat = xv * r1
        acc_ref[0:1, :] += _colsum(dh * (xhat * vrow(R_G_PRE_MIX)))
        acc_ref[1:2, :] += _colsum(dh)
        dxn = dh * (1.0 + vrow(R_SCALE_M))
        acc_ref[2:3, :] += _colsum(dxn * xhat)
        dxhat = dxn * vrow(R_G_PRE_MIX)
        dx_ref[...] = dx1_ref[...] + r1 * (dxhat - xhat * _rowmean(dxhat * xhat))

    row = lambda w: pl.BlockSpec((tm, w), lambda i: (i, 0))
    return pl.pallas_call(
        body, name="pre_bwd", grid=(S // tm,),
        in_specs=[row(p.shape[1]) for p in pieces] + [row(D_MODEL), row(D_MODEL), _full(vecs.shape), _full(w_cat.shape)],
        out_specs=[row(D_MODEL), _full((8, D_MODEL))],
        out_shape=[jax.ShapeDtypeStruct((S, D_MODEL), F32), jax.ShapeDtypeStruct((8, D_MODEL), F32)],
        compiler_params=_params(("arbitrary",)),
    )(*pieces, x, dx1, vecs, w_cat)


def _matmul_tn(a, b, tn, name, ts=1024):
    S, K = a.shape
    N = b.shape[1]
    n_s = S // ts

    def body(a_ref, b_ref, o_ref, acc_ref):
        s = pl.program_id(1)

        @pl.when(s == 0)
        def _():
            acc_ref[...] = jnp.zeros_like(acc_ref)

        acc_ref[...] += _dot_tn(a_ref[...], b_ref[...])

        @pl.when(s == n_s - 1)
        def _():
            o_ref[...] = acc_ref[...].astype(BF16)

    return pl.pallas_call(
        body, name=name, grid=(N // tn, n_s),
        in_specs=[pl.BlockSpec((ts, K), lambda j, s: (s, 0)), pl.BlockSpec((ts, tn), lambda j, s: (s, j))],
        out_specs=pl.BlockSpec((K, tn), lambda j, s: (0, j)),
        out_shape=jax.ShapeDtypeStruct((K, N), BF16),
        scratch_shapes=[pltpu.VMEM((K, tn), F32)],
        compiler_params=_params(("parallel", "arbitrary")),
    )(a, b)


def _rope_swap(t):
    lane = lax.broadcasted_iota(jnp.int32, t.shape, 1)
    first = (lane & (HEAD_DIM - 1)) < (HEAD_DIM // 2)
    return jnp.where(first, pltpu.roll(t, LANES - HEAD_DIM // 2, 1), pltpu.roll(t, HEAD_DIM // 2, 1))


def _rope(t, cos, sin_signed):
    return t * cos + _rope_swap(t) * sin_signed


def _rope_t(d, cos, sin_signed):
    return d * cos + _rope_swap(d * sin_signed)


def _to_kv_lanes(chunk, p, h):
    lane = lax.broadcasted_iota(jnp.int32, chunk.shape, 1)
    src = chunk if p == h else pltpu.roll(chunk, HEAD_DIM, 1)
    return jnp.where((lane >> 6) == h, src, jnp.zeros_like(src))


def _from_kv_lanes(o_a, o_b, h):
    lane = lax.broadcasted_iota(jnp.int32, o_a.shape, 1)
    a = o_a if h == 0 else pltpu.roll(o_a, HEAD_DIM, 1)
    b = o_b if h == 1 else pltpu.roll(o_b, HEAD_DIM, 1)
    return jnp.where(lane < HEAD_DIM, a, b)


def _band_bias(n):
    blk = ATTN_BLOCK
    qi = lax.broadcasted_iota(jnp.int32, (blk, 2 * blk), 0)
    kj = lax.broadcasted_iota(jnp.int32, (blk, 2 * blk), 1)
    seen = (kj > qi) & (kj <= qi + blk) & ((n > 0) | (kj >= blk))
    return jnp.concatenate([jnp.where(seen, 0.0, NEG)] * N_Q_HEADS, axis=0)


def _stack_heads(chunks, h, dtype):
    parts = []
    for g in range(4):
        j = 4 * h + g
        parts.append(_to_kv_lanes(chunks[j // 2], j % 2, h))
    return jnp.concatenate(parts, axis=0).astype(dtype)


def _fused_call(parts, name, n_steps):
    counts = [(len(p["in_specs"]), len(p["out_specs"]), len(p["scratch"])) for p in parts]
    n_in, n_out = sum(c[0] for c in counts), sum(c[1] for c in counts)

    def kernel_fn(*refs):
        i = pl.program_id(0)
        groups, a, b, c = [], 0, n_in, n_in + n_out
        for ci, co, cs in counts:
            groups.append(refs[a:a + ci] + refs[b:b + co] + refs[c:c + cs])
            a, b, c = a + ci, b + co, c + cs
        for p, g in zip(parts, groups):
            p["init"](i, *g)
        for p, g in zip(parts, groups):
            p["body"](i, *g)

    flat = lambda key: [v for p in parts for v in p[key]]
    res = pl.pallas_call(
        kernel_fn, name=name, grid=(n_steps,), in_specs=flat("in_specs"), out_specs=flat("out_specs"),
        out_shape=flat("out_shape"), scratch_shapes=flat("scratch"), compiler_params=_params(("arbitrary",)),
    )(*flat("operands"))
    out, pos = [], 0
    for _, co, _ in counts:
        out.append(res[pos:pos + co])
        pos += co
    return out


def _attn_fwd_part(pa, cos, sin, sinks):
    S = pa.shape[0]
    blk = ATTN_BLOCK
    nb = S // blk

    def body(n, sink_ref, cur_ref, prev_ref, cos_ref, sin_ref, cosp_ref, sinp_ref,
             ya_ref, qr_ref, kr_ref, vb_ref, lse_ref):
        cos_c, sin_c = cos_ref[...], sin_ref[...]
        qch = [_rope(cur_ref[:, c * LANES:(c + 1) * LANES], cos_c, sin_c) * (HEAD_DIM ** -0.5) for c in range(4)]
        for c in range(4):
            qr_ref[:, c * LANES:(c + 1) * LANES] = qch[c].astype(BF16)
        k_cur = _rope(cur_ref[:, 512:640], cos_c, sin_c).astype(BF16)
        k_prev = _rope(prev_ref[:, 0:LANES], cosp_ref[...], sinp_ref[...]).astype(BF16)
        v_cur = cur_ref[:, 640:768].astype(BF16)
        v_prev = prev_ref[:, LANES:2 * LANES].astype(BF16)
        kr_ref[...] = k_cur
        vb_ref[...] = v_cur
        K = jnp.concatenate([k_prev, k_cur], axis=0)
        V = jnp.concatenate([v_prev, v_cur], axis=0)
        lane = lax.broadcasted_iota(jnp.int32, (blk, LANES), 1)
        s = jnp.concatenate([_dot_nt(_stack_heads(qch, h, BF16), K) for h in range(2)], axis=0)
        s = s + _band_bias(n)
        rowmax = jnp.max(s, axis=1, keepdims=True)
        hd = lambda x, j: x[j * blk:(j + 1) * blk]
        m = jnp.concatenate([jnp.maximum(hd(rowmax, j), sink_ref[j]) for j in range(N_Q_HEADS)], axis=0)
        p = jnp.exp(s - m)
        pb = p.astype(BF16)
        den = _rowsum_mxu(pb) + jnp.concatenate([jnp.exp(sink_ref[j] - hd(m, j)) for j in range(N_Q_HEADS)], axis=0)
        o = jnp.concatenate([_dot(pb[4 * h * blk:4 * (h + 1) * blk], V) for h in range(2)], axis=0) * _recip(den)
        lse = m + jnp.log(den)
        outs = [o[j * blk:(j + 1) * blk, :] for j in range(N_Q_HEADS)]
        lse_tile = jnp.zeros((blk, LANES), F32)
        for j in range(N_Q_HEADS):
            lse_tile = jnp.where(lane == j, lse[j * blk:(j + 1) * blk, :], lse_tile)
        for c in range(4):
            ya_ref[:, c * LANES:(c + 1) * LANES] = _from_kv_lanes(outs[2 * c], outs[2 * c + 1], c // 2).astype(BF16)
        lse_ref[...] = lse_tile

    prev = lambda n: jnp.maximum(n - 1, 0)
    sd = lambda w, dt: jax.ShapeDtypeStruct((S, w), dt)
    return dict(
        init=lambda n, *refs: None, body=body, scratch=[], operands=[sinks, pa, pa, cos, sin, cos, sin],
        in_specs=[pl.BlockSpec(memory_space=pltpu.SMEM),
                  pl.BlockSpec((blk, A_W), lambda n: (n, 0)),
                  pl.BlockSpec((blk, 256), lambda n: (prev(n), 2)),
                  pl.BlockSpec((blk, LANES), lambda n: (n, 0)), pl.BlockSpec((blk, LANES), lambda n: (n, 0)),
                  pl.BlockSpec((blk, LANES), lambda n: (prev(n), 0)), pl.BlockSpec((blk, LANES), lambda n: (prev(n), 0))],
        out_specs=[pl.BlockSpec((blk, 512), lambda n: (n, 0)), pl.BlockSpec((blk, 512), lambda n: (n, 0)),
                   pl.BlockSpec((blk, LANES), lambda n: (n, 0)), pl.BlockSpec((blk, LANES), lambda n: (n, 0)),
                   pl.BlockSpec((blk, LANES), lambda n: (n, 0))],
        out_shape=[sd(512, BF16), sd(512, BF16), sd(LANES, BF16), sd(LANES, BF16), sd(LANES, F32)])


def _attn_bwd_part(dya, qr, kr, vb, lse, cos, sin, sinks):
    S = dya.shape[0]
    blk = ATTN_BLOCK
    nb = S // blk

    def init(n, sink_ref, dya_ref, qr_ref, kc_ref, kp_ref, vc_ref, vp_ref, lse_ref, cos_ref, sin_ref, cosp_ref, sinp_ref,
             dqkv_ref, last_ref, dsink_ref, ck, cv, cq):
        @pl.when(n == 0)
        def _():
            ck[...] = jnp.zeros_like(ck)
            cv[...] = jnp.zeros_like(cv)
            cq[...] = jnp.zeros_like(cq)
            dsink_ref[...] = jnp.zeros_like(dsink_ref)

    def body(n, sink_ref, dya_ref, qr_ref, kc_ref, kp_ref, vc_ref, vp_ref, lse_ref, cos_ref, sin_ref, cosp_ref, sinp_ref,
             dqkv_ref, last_ref, dsink_ref, ck, cv, cq):
        K = jnp.concatenate([kp_ref[...], kc_ref[...]], axis=0)
        V = jnp.concatenate([vp_ref[...], vc_ref[...]], axis=0)
        qch = [qr_ref[:, c * LANES:(c + 1) * LANES] for c in range(4)]
        dch = [dya_ref[:, c * LANES:(c + 1) * LANES].astype(F32) for c in range(4)]
        lse_tile = lse_ref[...]
        lane8 = lax.broadcasted_iota(jnp.int32, (8, LANES), 1)
        grp = lambda x, h: x[4 * h * blk:4 * (h + 1) * blk]
        qs = jnp.concatenate([_stack_heads(qch, h, BF16) for h in range(2)], axis=0)
        dos = jnp.concatenate([_stack_heads(dch, h, BF16) for h in range(2)], axis=0)
        lse_col = jnp.concatenate([lse_tile[:, j:j + 1] for j in range(N_Q_HEADS)], axis=0)
        s = jnp.concatenate([_dot_nt(grp(qs, h), K) for h in range(2)], axis=0)
        p = jnp.exp(s + _band_bias(n) - lse_col)
        dp = jnp.concatenate([_dot_nt(grp(dos, h), V) for h in range(2)], axis=0)
        delta = jnp.sum(p * dp, axis=1, keepdims=True)
        dsb = (p * (dp - delta)).astype(BF16)
        pb = p.astype(BF16)
        dq = jnp.concatenate([_dot(grp(dsb, h), K) for h in range(2)], axis=0)
        dk_acc = _dot_tn(grp(dsb, 0), grp(qs, 0)) + _dot_tn(grp(dsb, 1), grp(qs, 1))
        dv_acc = _dot_tn(grp(pb, 0), grp(dos, 0)) + _dot_tn(grp(pb, 1), grp(dos, 1))
        dqs = [dq[j * blk:(j + 1) * blk, :] for j in range(N_Q_HEADS)]
        dsink = jnp.zeros((8, LANES), F32)
        for j in range(N_Q_HEADS):
            rows = slice(j * blk, (j + 1) * blk)
            ps_delta = jnp.exp(sink_ref[j] - lse_col[rows]) * delta[rows]
            dsink = jnp.where(lane8 == j, dsink - jnp.sum(ps_delta), dsink)
        dsink_ref[...] += dsink
        cos_c, sin_c = cos_ref[...], sin_ref[...]
        dqkv_ref[:, 0:512] = cq[...]
        dqkv_ref[:, 512:640] = _rope_t(dk_acc[0:blk, :] + ck[...], cosp_ref[...], sinp_ref[...]).astype(BF16)
        dqkv_ref[:, 640:768] = (dv_acc[0:blk, :] + cv[...]).astype(BF16)
        ck[...] = dk_acc[blk:2 * blk, :]
        cv[...] = dv_acc[blk:2 * blk, :]
        for c in range(4):
            dqc = _from_kv_lanes(dqs[2 * c], dqs[2 * c + 1], c // 2) * (HEAD_DIM ** -0.5)
            dq_c = _rope_t(dqc, cos_c, sin_c).astype(BF16)
            cq[:, c * LANES:(c + 1) * LANES] = dq_c
            last_ref[:, c * LANES:(c + 1) * LANES] = dq_c
        last_ref[:, 512:640] = _rope_t(dk_acc[blk:2 * blk, :], cos_c, sin_c).astype(BF16)
        last_ref[:, 640:768] = dv_acc[blk:2 * blk, :].astype(BF16)

    prev = lambda n: jnp.maximum(n - 1, 0)
    same = lambda n: n
    bs = lambda w, f: pl.BlockSpec((blk, w), lambda n: (f(n), 0))
    return dict(
        init=init, body=body, operands=[sinks, dya, qr, kr, kr, vb, vb, lse, cos, sin, cos, sin],
        in_specs=[pl.BlockSpec(memory_space=pltpu.SMEM),
                  bs(512, same), bs(512, same), bs(LANES, same), bs(LANES, prev), bs(LANES, same), bs(LANES, prev),
                  bs(LANES, same), bs(LANES, same), bs(LANES, same), bs(LANES, prev), bs(LANES, prev)],
        out_specs=[bs(A_W, prev), _full((blk, A_W)), _full((8, LANES))],
        out_shape=[jax.ShapeDtypeStruct((S, A_W), BF16), jax.ShapeDtypeStruct((blk, A_W), BF16),
                   jax.ShapeDtypeStruct((8, LANES), F32)],
        scratch=[pltpu.VMEM((blk, LANES), F32), pltpu.VMEM((blk, LANES), F32), pltpu.VMEM((blk, 512), BF16)])


def _split3(x):
    hi = x.astype(BF16)
    r1 = x - hi.astype(F32)
    mid = r1.astype(BF16)
    lo = (r1 - mid.astype(F32)).astype(BF16)
    return hi, mid, lo


def _tri_matmul(tri_b, x):
    hi, mid, lo = _split3(x)
    return _dot(tri_b, hi) + _dot(tri_b, mid) + _dot(tri_b, lo)


def _log_sigmoid(x):
    return jnp.minimum(x, 0.0) - jnp.log(1.0 + jnp.exp(-jnp.abs(x)))


def _shift_rows(cur, seam, k, down):
    L = cur.shape[0]
    row8 = lax.broadcasted_iota(jnp.int32, seam.shape, 0)
    if down:
        mixed = jnp.concatenate([cur[:L - 8], jnp.where(row8 >= 8 - k, seam, cur[L - 8:])], axis=0)
        return pltpu.roll(mixed, k, 0)
    mixed = jnp.concatenate([jnp.where(row8 < k, seam, cur[:8]), cur[8:]], axis=0)
    return pltpu.roll(mixed, L - k, 0)


def _conv_fwd(cur, tail, cw_ref):
    z = cw_ref[4:5, :]
    for k in range(3, 0, -1):
        z = z + _shift_rows(cur, tail, k, True) * cw_ref[3 - k:4 - k, :]
    return z + cur * cw_ref[3:4, :]


def _rowsum_mxu(x, two_pass=False):
    ones = jnp.ones((x.shape[1], LANES), BF16)
    hi = x.astype(BF16)
    s = _dot(hi, ones)
    if two_pass:
        s = s + _dot((x - hi.astype(F32)).astype(BF16), ones)
    return s


def _stack(f):
    return jnp.concatenate([f(h) for h in range(MLSTM_HEADS)], axis=0)


def _head(x, h):
    L = x.shape[0] // MLSTM_HEADS
    return x[h * L:(h + 1) * L]


def _mlstm_heads_fwd(qk, cur_ref, gt, b_all, c_prev, nmv, tri, eye):
    L = qk.shape[0]
    HD = MLSTM_HEAD_DIM
    W4 = MLSTM_HEADS * HD
    col2row = lambda x: jnp.sum(jnp.where(eye, x, 0.0), axis=0, keepdims=True)
    b_col = _stack(lambda h: b_all[:, 4 + h:5 + h])
    i_col = _stack(lambda h: gt[:, h:h + 1])
    b_row = _stack(lambda h: jnp.broadcast_to(col2row(b_all[:, 4 + h:5 + h]), (L, L)))
    i_row = _stack(lambda h: jnp.broadcast_to(col2row(gt[:, h:h + 1]), (L, L)))
    bl = _stack(lambda h: jnp.broadcast_to(b_all[L - 1:L, 4 + h:5 + h], (L, 1)))
    m_prev = _stack(lambda h: jnp.broadcast_to(nmv[4 + h:5 + h, 0:1], (L, 1)))
    n_prev = _stack(lambda h: jnp.broadcast_to(nmv[h:h + 1, :], (L, HD)))
    tri4 = jnp.concatenate([tri] * MLSTM_HEADS, axis=0)
    Dm = jnp.where(tri4, b_col - b_row + i_row, NEG)
    inter = b_col + m_prev
    m_t = jnp.maximum(inter, jnp.max(Dm, axis=1, keepdims=True))
    W = jnp.exp(Dm - m_t)
    e_t = jnp.exp(inter - m_t)
    q = _stack(lambda h: qk[:, h * HD:(h + 1) * HD])
    k = _stack(lambda h: qk[:, W4 + h * HD:W4 + (h + 1) * HD]) * (HD ** -0.5)
    v = _stack(lambda h: cur_ref[:, 2 * W4 + h * HD:2 * W4 + (h + 1) * HD])
    qb, kb, vb = q.astype(BF16), k.astype(BF16), v.astype(BF16)
    Sc = _stack(lambda h: _dot_nt(_head(qb, h), _head(kb, h))) * W
    Scb = Sc.astype(BF16)
    cb = [c.astype(BF16) for c in c_prev]
    P1 = _stack(lambda h: _dot(_head(qb, h), cb[h]))
    num = _stack(lambda h: _dot(_head(Scb, h), _head(vb, h))) + e_t * P1
    qn = _rowsum_mxu(q * n_prev)
    den = _rowsum_mxu(Scb) + e_t * qn
    floor = jnp.broadcast_to(jnp.exp(-m_t), den.shape)
    inv_g = _recip(jnp.maximum(jnp.abs(den), floor))
    hv = num * inv_g
    a_col = bl - b_col + i_col
    a_max = _stack(lambda h: jnp.broadcast_to(jnp.max(_head(a_col, h), axis=0, keepdims=True), (L, 1)))
    m_new = jnp.maximum(bl + m_prev, a_max)
    dec = jnp.exp(bl + m_prev - m_new)
    u_col = jnp.exp(a_col - m_new)
    return dict(W=W, e_t=e_t, q=q, k=k, v=v, qb=qb, kb=kb, vb=vb, cb=cb, Sc=Sc, Scb=Scb, P1=P1, qn=qn, den=den,
                floor=floor, inv_g=inv_g, hv=hv, n_prev=n_prev, m_new=m_new, dec=dec, u_col=u_col)


def _mlstm_fwd_part(pm, pif, cw, sv):
    S = pm.shape[0]
    L = MLSTM_CHUNK
    nc = S // L
    HD = MLSTM_HEAD_DIM
    W4 = MLSTM_HEADS * HD

    def init(c, cur_ref, pif_ref, cw_ref, sv_ref, ym_ref, z_ref, cst_ref, nst_ref, C, nm, tail):
        @pl.when(c == 0)
        def _():
            C[...] = jnp.zeros_like(C)
            nm[...] = jnp.zeros_like(nm)
            tail[...] = jnp.zeros_like(tail)

    def body(c, cur_ref, pif_ref, cw_ref, sv_ref, ym_ref, z_ref, cst_ref, nst_ref, C, nm, tail):
        z = _conv_fwd(cur_ref[:, 0:2 * W4], tail[...], cw_ref)
        tail[...] = cur_ref[L - 8:L, 0:2 * W4]
        z_ref[...] = z
        qk = z * _sigmoid(z)
        gt = pif_ref[...] + sv_ref[1:2, 0:LANES]
        r_i = lax.broadcasted_iota(jnp.int32, (L, L), 0)
        c_i = lax.broadcasted_iota(jnp.int32, (L, L), 1)
        tri = c_i <= r_i
        eye = c_i == r_i
        b_all = _tri_matmul(tri.astype(BF16), _log_sigmoid(gt))
        nmv = nm[...]
        nst_ref[0] = nmv
        c_prev = [C[h] for h in range(MLSTM_HEADS)]
        f = _mlstm_heads_fwd(qk, cur_ref, gt, b_all, c_prev, nmv, tri, eye)
        hv = f["hv"]
        xc = hv - _rowsum_mxu(hv, True) * (1.0 / HD)
        hhat = xc * lax.rsqrt(_rowsum_mxu(xc * xc) * (1.0 / HD) + NORM_EPS)
        so = _sigmoid(_stack(lambda h: cur_ref[:, 3 * W4 + h * HD:3 * W4 + (h + 1) * HD]))
        wn = _stack(lambda h: jnp.broadcast_to(sv_ref[0:1, h * HD:(h + 1) * HD], (L, HD)))
        y = (so * hhat * wn).astype(BF16)
        kw = f["k"] * f["u_col"]
        kwb = kw.astype(BF16)
        n_new, m_new = [], []
        for h in range(MLSTM_HEADS):
            cst_ref[0, h] = c_prev[h]
            ym_ref[:, h * HD:(h + 1) * HD] = _head(y, h)
            dec = f["dec"][h * L:h * L + 1, :]
            C[h] = dec * c_prev[h] + _dot_tn(_head(kwb, h), _head(f["vb"], h))
            n_new.append(dec * nmv[h:h + 1, :] + _colsum(_head(kw, h)))
            m_new.append(jnp.broadcast_to(f["m_new"][h * L:h * L + 1, :], (1, LANES)))
        nm[...] = jnp.concatenate(n_new + m_new, axis=0)

    return dict(
        init=init, body=body, operands=[pm, pif, cw, sv],
        in_specs=[pl.BlockSpec((L, M_W), lambda c: (c, 0)),
                  pl.BlockSpec((L, IF_W), lambda c: (c, 0)), _full(cw.shape), _full(sv.shape)],
        out_specs=[pl.BlockSpec((L, W4), lambda c: (c, 0)), pl.BlockSpec((L, 2 * W4), lambda c: (c, 0)),
                   pl.BlockSpec((1, MLSTM_HEADS, HD, HD), lambda c: (c, 0, 0, 0)),
                   pl.BlockSpec((1, 8, LANES), lambda c: (c, 0, 0))],
        out_shape=[jax.ShapeDtypeStruct((S, W4), BF16), jax.ShapeDtypeStruct((S, 2 * W4), F32),
                   jax.ShapeDtypeStruct((nc, MLSTM_HEADS, HD, HD), F32), jax.ShapeDtypeStruct((nc, 8, LANES), F32)],
        scratch=[pltpu.VMEM((MLSTM_HEADS, HD, HD), F32), pltpu.VMEM((8, LANES), F32), pltpu.VMEM((8, 2 * W4), F32)])


def _mlstm_bwd_part(pm, zc, pif, cw, sv, dym, cst, nst):
    S = pm.shape[0]
    L = MLSTM_CHUNK
    nc = S // L
    HD = MLSTM_HEAD_DIM
    W4 = MLSTM_HEADS * HD

    def init(r, cur_ref, z_ref, pif_ref, cw_ref, sv_ref, dym_ref, cst_ref, nst_ref,
             dm_ref, dcw_ref, dsv_ref, dC, dn, dz_next, dqk):
        @pl.when(r == 0)
        def _():
            dC[...] = jnp.zeros_like(dC)
            dn[...] = jnp.zeros_like(dn)
            dz_next[...] = jnp.zeros_like(dz_next)
            dcw_ref[...] = jnp.zeros_like(dcw_ref)
            dsv_ref[...] = jnp.zeros_like(dsv_ref)

    def body(r, cur_ref, z_ref, pif_ref, cw_ref, sv_ref, dym_ref, cst_ref, nst_ref,
             dm_ref, dcw_ref, dsv_ref, dC, dn, dz_next, dqk):
        z = z_ref[...]
        sgz = _sigmoid(z)
        qk = z * sgz
        gt = pif_ref[...] + sv_ref[1:2, 0:LANES]
        r_i = lax.broadcasted_iota(jnp.int32, (L, L), 0)
        c_i = lax.broadcasted_iota(jnp.int32, (L, L), 1)
        tri = c_i <= r_i
        eye = c_i == r_i
        b_all = _tri_matmul(tri.astype(BF16), _log_sigmoid(gt))
        lane = lax.broadcasted_iota(jnp.int32, (L, LANES), 1)
        rowl = lax.broadcasted_iota(jnp.int32, (L, 1), 0)
        nmv = nst_ref[0]
        heads = range(MLSTM_HEADS)
        c_prev = [cst_ref[0, h] for h in heads]
        f = _mlstm_heads_fwd(qk, cur_ref, gt, b_all, c_prev, nmv, tri, eye)
        hv, inv_g, den, e_t, u_col, n_prev = f["hv"], f["inv_g"], f["den"], f["e_t"], f["u_col"], f["n_prev"]
        q, k, v, qb, kb, vb, Sc, Scb, W = f["q"], f["k"], f["v"], f["qb"], f["kb"], f["vb"], f["Sc"], f["Scb"], f["W"]
        xc = hv - _rowsum_mxu(hv, True) * (1.0 / HD)
        rstd = lax.rsqrt(_rowsum_mxu(xc * xc) * (1.0 / HD) + NORM_EPS)
        hhat = xc * rstd
        wn = _stack(lambda h: jnp.broadcast_to(sv_ref[0:1, h * HD:(h + 1) * HD], (L, HD)))
        so = _sigmoid(_stack(lambda h: cur_ref[:, 3 * W4 + h * HD:3 * W4 + (h + 1) * HD]))
        dy = _stack(lambda h: dym_ref[:, h * HD:(h + 1) * HD])
        d_o = (dy * hhat * wn * (so * (1.0 - so))).astype(BF16)
        dln = dy * so
        dwn = dln * hhat
        dhhat = dln * wn
        m2 = _rowsum_mxu(dhhat * hhat) * (1.0 / HD)
        dh = rstd * (dhhat - _rowsum_mxu(dhhat) * (1.0 / HD) - hhat * m2)
        dnum = dh * inv_g
        active = jnp.abs(den) > f["floor"]
        dden = jnp.where(active, -(HD * NORM_EPS) * m2 * rstd * rstd * inv_g * jnp.where(den >= 0.0, 1.0, -1.0), 0.0)
        dnumb = dnum.astype(BF16)
        dSc = _stack(lambda h: _dot_nt(_head(dnumb, h), _head(vb, h))) + dden
        dA = (dSc * W).astype(BF16)
        G = dSc * Sc
        Gb = G.astype(BF16)
        Gl = (G - Gb.astype(F32)).astype(BF16)
        ones = jnp.ones((L, LANES), BF16)
        Gr = _dot(Gb, ones) + _dot(Gl, ones)
        Gc = _stack(lambda h: _dot_tn(_head(Gb, h), ones) + _dot_tn(_head(Gl, h), ones))
        dCn = [dC[h] for h in heads]
        dCnb = [d.astype(BF16) for d in dCn]
        dnv = dn[...]
        dn_new = _stack(lambda h: jnp.broadcast_to(dnv[h:h + 1, :], (L, HD)))
        kdC = _stack(lambda h: _dot(_head(kb, h), dCnb[h]))
        vdC = _stack(lambda h: _dot_nt(_head(vb, h), dCnb[h]))
        dv = (_stack(lambda h: _dot_tn(_head(Scb, h), _head(dnumb, h))) + u_col * kdC).astype(BF16)
        dq = _stack(lambda h: _dot(_head(dA, h), _head(kb, h))) \
            + e_t * _stack(lambda h: _dot_nt(_head(dnumb, h), f["cb"][h])) + (e_t * dden) * n_prev
        dk = (_stack(lambda h: _dot_tn(_head(dA, h), _head(qb, h))) + u_col * (vdC + dn_new)) * (HD ** -0.5)
        E = (_rowsum_mxu(f["P1"] * dnum, True) + dden * f["qn"]) * e_t
        U = _rowsum_mxu(kdC * v + k * dn_new, True) * u_col
        qe = (q * e_t).astype(BF16)
        qd = (e_t * dden) * q
        di = Gc + U
        db = Gr + E - Gc - U
        di_tile = jnp.zeros((L, LANES), F32)
        db_tile = jnp.zeros((L, LANES), F32)
        dn_rows = []
        for h in heads:
            dec = f["dec"][h * L:h * L + 1, :]
            ddec = jnp.sum(dCn[h] * c_prev[h]) + jnp.sum(dnv[h:h + 1, :] * nmv[h:h + 1, :])
            dbl = ddec * dec + jnp.sum(_head(U, h), axis=0, keepdims=True)
            di_tile = jnp.where(lane == h, _head(di, h), di_tile)
            db_tile = jnp.where(lane == 4 + h, _head(db, h) + jnp.where(rowl == L - 1, dbl, 0.0), db_tile)
            dC[h] = dec * dCn[h] + _dot_tn(_head(qe, h), _head(dnumb, h))
            dn_rows.append(dec * dnv[h:h + 1, :] + _colsum(_head(qd, h)))
            dsv_ref[0:1, h * HD:(h + 1) * HD] += _colsum(_head(dwn, h))
            dqk[:, h * HD:(h + 1) * HD] = _head(dq, h)
            dqk[:, W4 + h * HD:W4 + (h + 1) * HD] = _head(dk, h)
            dm_ref[:, 2 * W4 + h * HD:2 * W4 + (h + 1) * HD] = _head(dv, h)
            dm_ref[:, 3 * W4 + h * HD:3 * W4 + (h + 1) * HD] = _head(d_o, h)
        dn[...] = jnp.concatenate(dn_rows + [jnp.zeros((8 - MLSTM_HEADS, LANES), F32)], axis=0)
        dlf = _tri_matmul((r_i <= c_i).astype(BF16), db_tile)
        dif = jnp.where(lane < 4, di_tile, jnp.where(lane < 8, dlf * (1.0 - _sigmoid(gt)), 0.0))
        dm_ref[:, M_W:M_W + IF_W] = dif.astype(BF16)
        dsv_ref[1:2, 0:LANES] += _colsum(dif)
        dz = dqk[...] * (sgz * (1.0 + z * (1.0 - sgz)))
        dcw_ref[4:5, :] += _colsum(dz)
        u = cur_ref[:, 0:2 * W4]
        du_in = dz * cw_ref[3:4, :]
        dcw_ref[3:4, :] += _colsum(dz * u)
        for k in range(1, 4):
            up = _shift_rows(dz, dz_next[...], k, False)
            dcw_ref[3 - k:4 - k, :] += _colsum(up * u)
            du_in = du_in + up * cw_ref[3 - k:4 - k, :]
        dz_next[...] = dz[0:8, :]
        dm_ref[:, 0:2 * W4] = du_in.astype(BF16)

    cidx = lambda r: nc - 1 - r
    return dict(
        init=init, body=body, operands=[pm, zc, pif, cw, sv, dym, cst, nst],
        in_specs=[pl.BlockSpec((L, M_W), lambda r: (cidx(r), 0)), pl.BlockSpec((L, 2 * W4), lambda r: (cidx(r), 0)),
                  pl.BlockSpec((L, IF_W), lambda r: (cidx(r), 0)), _full(cw.shape), _full(sv.shape),
                  pl.BlockSpec((L, W4), lambda r: (cidx(r), 0)),
                  pl.BlockSpec((1, MLSTM_HEADS, HD, HD), lambda r: (cidx(r), 0, 0, 0)),
                  pl.BlockSpec((1, 8, LANES), lambda r: (cidx(r), 0, 0))],
        out_specs=[pl.BlockSpec((L, M_W + IF_W), lambda r: (cidx(r), 0)), _full((8, 2 * W4)), _full((8, W4))],
        out_shape=[jax.ShapeDtypeStruct((S, M_W + IF_W), BF16),
                   jax.ShapeDtypeStruct((8, 2 * W4), F32), jax.ShapeDtypeStruct((8, W4), F32)],
        scratch=[pltpu.VMEM((MLSTM_HEADS, HD, HD), F32), pltpu.VMEM((8, LANES), F32),
                 pltpu.VMEM((8, 2 * W4), F32), pltpu.VMEM((L, 2 * W4), F32)])


def _rope_tables(positions):
    half = HEAD_DIM // 2
    inv_freq = ROPE_THETA ** (-2.0 * jnp.arange(half, dtype=F32) / HEAD_DIM)
    ang = positions.astype(F32)[:, None] * inv_freq
    cos = jnp.tile(jnp.cos(ang), (1, LANES // half))
    sign = jnp.tile(jnp.concatenate([-jnp.ones((half,), F32), jnp.ones((half,), F32)]), LANES // HEAD_DIM)
    sin = jnp.tile(jnp.sin(ang), (1, LANES // half)) * sign
    return cos, sin


def _local_step(x, tgt, positions, mod, gains, w_cat, w_ba, w_bm, w_out, w_gate, w_up, w_down,
                conv_w, conv_b, b_if, sinks, norm_w):
    t = _tables(mod, gains, conv_w, conv_b, b_if, norm_w, positions)
    a = _mixer_fwd(x, t, sinks, w_cat)
    b = _ffn_part(x, tgt, t, a, w_ba, w_bm, w_out, w_gate, w_up, w_down)
    c = _mixer_bwd(b["dx1"], t, a, b, sinks, w_ba, w_bm, w_out)
    grad_x, acc_p = _pre_bwd(c["dproj"], x, b["dx1"], t["vecs"], w_cat)
    big = dict(w_cat=jnp.concatenate(c["g_w_cat"], axis=1), w_ba=c["g_w_ba"], w_bm=c["g_w_bm"], w_out=c["g_w_out"], w_gate=b["g_w_gate"],
               w_up=b["g_w_up"], w_down=b["g_w_down"])
    return b["loss"], grad_x, big, _small_grads(acc_p, b, c)


def _tables(mod, gains, conv_w, conv_b, b_if, norm_w, positions):
    cos, sin = _rope_tables(positions)
    return dict(
        vecs=jnp.concatenate([mod, gains, jnp.zeros((6, D_MODEL), F32)], axis=0),
        cw=jnp.concatenate([conv_w, conv_b.reshape(1, -1), jnp.zeros((3, 2 * 512), F32)], axis=0),
        sv=jnp.zeros((8, 512), F32).at[0].set(norm_w).at[1, 0:8].set(b_if), cos=cos, sin=sin)


def _mixer_fwd(x, t, sinks, w_cat):
    h, pa, pm, pif, pg = _pre_proj(x, t["vecs"], w_cat)
    n_blk = x.shape[0] // ATTN_BLOCK
    (ya, qr, kr, vb, lse), = _fused_call([_attn_fwd_part(pa, t["cos"], t["sin"], sinks)], "attn_fwd", n_blk)
    (ym, zc, cst, nst), = _fused_call([_mlstm_fwd_part(pm, pif, t["cw"], t["sv"])], "mlstm_fwd", n_blk)
    return dict(h=h, pm=pm, pif=pif, pg=pg, ya=ya, qr=qr, kr=kr, vb=vb, lse=lse, ym=ym, zc=zc, cst=cst, nst=nst)


def _ffn_part(x, tgt, t, a, w_ba, w_bm, w_out, w_gate, w_up, w_down):
    x1, merged, mix, pba, pbm = _mix_fwd(x, a["ya"], a["ym"], a["pg"], t["vecs"], w_ba, w_bm, w_out)
    dx1, h2, hid, da, du, dff, acc_f, loss = _ffn_fwd_bwd(x1, tgt, t["vecs"], w_gate, w_up, w_down)
    return dict(merged=merged, mix=mix, pba=pba, pbm=pbm, dx1=dx1, acc_f=acc_f, loss=loss[0, 0],
                g_w_gate=_matmul_tn(da, h2, 1024, "dw_ffn_gate"),
                g_w_up=_matmul_tn(du, h2, 1024, "dw_ffn_up"),
                g_w_down=_matmul_tn(hid, dff, 1024, "dw_ffn_down"))


def _mixer_bwd(dx1, t, a, b, sinks, w_ba, w_bm, w_out):
    dmix, dpa, dpb, dg, dya, dym, acc_m = _mix_bwd(dx1, b["mix"], b["pba"], b["pbm"], a["pg"], t["vecs"], w_ba, w_bm, w_out)
    g_w_out = _matmul_tn(b["merged"], dmix, 1024, "dw_out")
    g_w_ba = _matmul_tn(a["ya"], dpa, 1024, "dw_branch_attn")
    g_w_bm = _matmul_tn(a["ym"], dpb, 1024, "dw_branch_mlstm")
    n_blk = dx1.shape[0] // ATTN_BLOCK
    (dqkv, dqkv_last, dsink), = _fused_call(
        [_attn_bwd_part(dya, a["qr"], a["kr"], a["vb"], a["lse"], t["cos"], t["sin"], sinks)], "attn_bwd", n_blk)
    (dm, dcw, dsv), = _fused_call(
        [_mlstm_bwd_part(a["pm"], a["zc"], a["pif"], t["cw"], t["sv"], dym, a["cst"], a["nst"])], "mlstm_bwd", n_blk)
    dqkv = lax.dynamic_update_slice(dqkv, dqkv_last, (dqkv.shape[0] - ATTN_BLOCK, 0))
    dproj = [dqkv, dm, dg]
    g_w_cat = [_matmul_tn(a["h"], p, 1024 if p.shape[1] % 1024 == 0 else p.shape[1], "dw_in_" + n)
               for p, n in zip(dproj, ("attn", "mlstm", "branch"))]
    return dict(dproj=dproj, g_w_cat=g_w_cat, g_w_out=g_w_out, g_w_ba=g_w_ba,
                g_w_bm=g_w_bm, acc_m=acc_m, dsink=dsink, dcw=dcw, dsv=dsv)


def _small_grads(acc_p, b, c):
    acc_f, acc_m = b["acc_f"], c["acc_m"]
    dmod = jnp.stack([acc_p[1], acc_p[0], acc_m[0], acc_f[3], acc_f[2], acc_f[0]])
    dgains = jnp.stack([acc_p[2], acc_m[1], acc_f[4], acc_f[1]])
    return dict(dmod=dmod, dgains=dgains, dconv_w=c["dcw"][0:4], dconv_b=c["dcw"][4], db_if=c["dsv"][1, 0:8],
                dsinks=c["dsink"][0, 0:8], dnorm_w=c["dsv"][0])


MESH_ID = pl.DeviceIdType.MESH


def _mesh_pos():
    return lax.axis_index("x"), lax.axis_index("y"), lax.axis_index("c")


def _flip(v, bit):
    return 1 - v if bit else v


def _relations():
    return [((r >> 2) & 1, (r >> 1) & 1, r & 1) for r in range(1, N_DEV)]


def _small_exchange(p, gather, name):
    R, V = p.shape[-2:]

    def body(p_ref, out_ref, send_sems, recv_sems):
        x, y, c = _mesh_pos()
        me = 4 * x + 2 * y + c
        out_ref[me] = p_ref[...] if gather else p_ref[me]
        peers = []
        for dx, dy, dc in _relations():
            px, py, pc = _flip(x, dx), _flip(y, dy), _flip(c, dc)
            peers.append(((px, py, pc), 4 * px + 2 * py + pc))

        def copy(k, landing):
            peer, pid = peers[k]
            return pltpu.make_async_remote_copy(
                src_ref=p_ref if gather else p_ref.at[pid], dst_ref=out_ref.at[landing],
                send_sem=send_sems.at[k], recv_sem=recv_sems.at[k], device_id=peer, device_id_type=MESH_ID)

        sends = [copy(k, me) for k in range(N_DEV - 1)]
        for cp in sends:
            cp.start()
        for k in range(N_DEV - 1):
            copy(k, peers[k][1]).wait_recv()
        for cp in sends:
            cp.wait_send()

    vm = pl.BlockSpec(memory_space=pltpu.VMEM)
    return pl.pallas_call(
        body, name=name, in_specs=[vm], out_specs=vm,
        out_shape=jax.ShapeDtypeStruct((N_DEV, R, V), F32),
        scratch_shapes=[pltpu.SemaphoreType.DMA((N_DEV - 1,)), pltpu.SemaphoreType.DMA((N_DEV - 1,))],
        compiler_params=pltpu.CompilerParams(vmem_limit_bytes=VMEM_LIMIT),
    )(p)


HBM_SPEC = pl.BlockSpec(memory_space=pltpu.HBM)
SEM_SPEC = pl.BlockSpec(memory_space=pltpu.SEMAPHORE)


def _peers(x, y, c):
    out = []
    for dx, dy, dc in _relations():
        px, py, pc = _flip(x, dx), _flip(y, dy), _flip(c, dc)
        out.append(((px, py, pc), 4 * px + 2 * py + pc))
    return out


def _exchange_start(arrs, gather, after, name):
    n = len(arrs)
    me_out = 4 * lax.axis_index("x") + 2 * lax.axis_index("y") + lax.axis_index("c")
    lands = []
    for a in arrs:
        own = a[None] if gather else lax.dynamic_index_in_dim(a, me_out, 0, keepdims=True)
        empty = lax.empty(((N_DEV,) + a.shape) if gather else a.shape, a.dtype)
        lands.append(lax.dynamic_update_index_in_dim(empty, own, me_out, 0))

    def body(*refs):
        a_refs, l_refs = refs[:n], refs[n:2 * n]
        send_sems, recv_sems = refs[2 * n + 1], refs[2 * n + 2]
        token = refs[4 * n + 3]
        x, y, c = _mesh_pos()
        me = 4 * x + 2 * y + c
        for a in range(n):
            for k, (peer, pid) in enumerate(_peers(x, y, c)):
                pltpu.make_async_remote_copy(
                    src_ref=a_refs[a] if gather else a_refs[a].at[pid], dst_ref=l_refs[a].at[me],
                    send_sem=send_sems.at[a * (N_DEV - 1) + k], recv_sem=recv_sems.at[a * (N_DEV - 1) + k],
                    device_id=peer, device_id_type=MESH_ID).start()
        token[...] = jnp.zeros_like(token)

    sem = pltpu.SemaphoreType.DMA((n * (N_DEV - 1),))
    hbm = lambda a: pltpu.with_memory_space_constraint(a, pltpu.HBM)
    res = pl.pallas_call(
        body, name=name,
        out_shape=(sem, sem, *[pltpu.HBM(a.shape, a.dtype) for a in arrs], *[pltpu.HBM(l.shape, l.dtype) for l in lands],
                   jax.ShapeDtypeStruct((8, LANES), F32)),
        in_specs=[HBM_SPEC] * (2 * n) + [pl.BlockSpec(memory_space=pl.ANY)],
        out_specs=(SEM_SPEC, SEM_SPEC, *[HBM_SPEC] * (2 * n), pl.BlockSpec(memory_space=pltpu.VMEM)),
        input_output_aliases={i: 2 + i for i in range(2 * n)},
        compiler_params=pltpu.CompilerParams(has_side_effects=pltpu.SideEffectType.DATAFLOW_SIDE_EFFECTING),
    )(*[hbm(a) for a in arrs], *[hbm(l) for l in lands], after)
    return dict(sems=res[0:2], arrs=res[2:2 + n], lands=res[2 + n:2 + 2 * n], token=res[2 + 2 * n], gather=gather)


def _exchange_wait(st, after, name):
    n = len(st["arrs"])
    gather = st["gather"]

    def body(*refs):
        a_refs, l_refs = refs[:n], refs[n:2 * n]
        send_sems, recv_sems = refs[2 * n], refs[2 * n + 1]
        x, y, c = _mesh_pos()
        for a in range(n):
            for k, (peer, pid) in enumerate(_peers(x, y, c)):
                cp = pltpu.make_async_remote_copy(
                    src_ref=a_refs[a] if gather else a_refs[a].at[pid], dst_ref=l_refs[a].at[pid],
                    send_sem=send_sems.at[a * (N_DEV - 1) + k], recv_sem=recv_sems.at[a * (N_DEV - 1) + k],
                    device_id=peer, device_id_type=MESH_ID)
                cp.wait_send()
                cp.wait_recv()

    both = list(st["arrs"]) + list(st["lands"])
    res = pl.pallas_call(
        body, name=name, out_shape=[pltpu.HBM(a.shape, a.dtype) for a in both],
        in_specs=[HBM_SPEC] * (2 * n) + [SEM_SPEC, SEM_SPEC, pl.BlockSpec(memory_space=pl.ANY)],
        out_specs=[HBM_SPEC] * (2 * n), input_output_aliases={i: i for i in range(2 * n)},
        compiler_params=pltpu.CompilerParams(has_side_effects=pltpu.SideEffectType.DATAFLOW_SIDE_EFFECTING),
    )(*both, *st["sems"], after)
    return res[n:2 * n]


def _tie(x, token, name):
    def body(x_ref, t_ref, o_ref):
        o_ref[...] = x_ref[...]

    vm = pl.BlockSpec(memory_space=pltpu.VMEM)
    return pl.pallas_call(
        body, name=name, in_specs=[vm, pl.BlockSpec(memory_space=pl.ANY)], out_specs=vm,
        out_shape=jax.ShapeDtypeStruct(x.shape, x.dtype),
    )(x, token)


def _gather_front(shards, c8, w_ada, b_cols):
    n = len(shards)
    ada_w = w_ada.shape[1]
    chunk = c8.shape[1]

    def body(*refs):
        p_refs = refs[:n]
        c_ref, wa_ref, b_ref = refs[n:n + 3]
        out_refs = refs[n + 3:2 * n + 3]
        cg_ref, mod_ref = refs[2 * n + 3], refs[2 * n + 4]
        send_sems, recv_sems, local_sems, c_send, c_recv, m_send, m_recv, pbuf = refs[2 * n + 5:]
        x, y, c = _mesh_pos()
        me_id = 4 * x + 2 * y + c
        me, sibling = (x, y, c), (x, y, 1 - c)
        chips = [(1 - x, y), (x, 1 - y), (1 - x, 1 - y)]
        peers = _peers(x, y, c)

        def copy(a, k, block, to, own=False):
            slot = out_refs[a].at[4 * block[0] + 2 * block[1] + block[2]]
            return pltpu.make_async_remote_copy(
                src_ref=p_refs[a] if own else slot, dst_ref=slot,
                send_sem=send_sems.at[a, k], recv_sem=recv_sems.at[a, k], device_id=to, device_id_type=MESH_ID)

        mine = [pltpu.make_async_copy(p_refs[a], out_refs[a].at[me_id], local_sems.at[a]) for a in range(n)]
        for cp in mine:
            cp.start()
        first = []
        for a in range(n):
            first.append(copy(a, 0, me, sibling, own=True))
            first += [copy(a, 1 + j, me, (*chip, c), own=True) for j, chip in enumerate(chips)]
        for cp in first:
            cp.start()

        def c_copy(k, landing):
            return pltpu.make_async_remote_copy(
                src_ref=c_ref, dst_ref=cg_ref.at[landing], send_sem=c_send.at[k], recv_sem=c_recv.at[k],
                device_id=peers[k][0], device_id_type=MESH_ID)

        cg_ref[me_id] = c_ref[...]
        c_sends = [c_copy(k, me_id) for k in range(N_DEV - 1)]
        for cp in c_sends:
            cp.start()
        for k in range(N_DEV - 1):
            c_copy(k, peers[k][1]).wait_recv()
        prod = b_ref[...] + jnp.zeros((N_DEV, ada_w), F32)
        for j in range(D_MODEL // chunk):
            prod = prod + _dot(cg_ref[:, j, :].astype(BF16), wa_ref[j * chunk:(j + 1) * chunk, :].astype(BF16))
        for k in range(N_DEV):
            pbuf[k] = jnp.broadcast_to(prod[k:k + 1, :], (8, ada_w))

        def m_copy(k, landing):
            return pltpu.make_async_remote_copy(
                src_ref=pbuf.at[peers[k][1]], dst_ref=mod_ref.at[landing], send_sem=m_send.at[k], recv_sem=m_recv.at[k],
                device_id=peers[k][0], device_id_type=MESH_ID)

        mod_ref[me_id] = pbuf[me_id]
        m_sends = [m_copy(k, me_id) for k in range(N_DEV - 1)]
        for cp in m_sends:
            cp.start()
        for k in range(N_DEV - 1):
            m_copy(k, peers[k][1]).wait_recv()

        passed = []
        for j, chip in enumerate(chips):
            for a in range(n):
                copy(a, 1 + j, (*chip, c), me).wait_recv()
                passed.append(copy(a, 4 + j, (*chip, c), sibling))
                passed[-1].start()
        for a in range(n):
            copy(a, 0, sibling, me).wait_recv()
            for j, chip in enumerate(chips):
                copy(a, 4 + j, (*chip, 1 - c), me).wait_recv()
        for cp in first + passed + c_sends + m_sends:
            cp.wait_send()
        for cp in mine:
            cp.wait()

    hbm = pl.BlockSpec(memory_space=pl.ANY)
    vm = pl.BlockSpec(memory_space=pltpu.VMEM)
    sem7 = pltpu.SemaphoreType.DMA((N_DEV - 1,))
    res = pl.pallas_call(
        body, name="gather_front", in_specs=[hbm] * n + [vm, vm, vm], out_specs=[hbm] * n + [vm, vm],
        out_shape=[jax.ShapeDtypeStruct((N_DEV,) + s.shape, s.dtype) for s in shards]
        + [jax.ShapeDtypeStruct((N_DEV,) + c8.shape, F32), jax.ShapeDtypeStruct((N_DEV, 8, ada_w), F32)],
        scratch_shapes=[pltpu.SemaphoreType.DMA((n, N_DEV - 1)), pltpu.SemaphoreType.DMA((n, N_DEV - 1)),
                        pltpu.SemaphoreType.DMA((n,)), sem7, sem7, sem7, sem7, pltpu.VMEM((N_DEV, 8, ada_w), F32)],
        compiler_params=pltpu.CompilerParams(vmem_limit_bytes=VMEM_LIMIT),
    )(*shards, c8, w_ada, b_cols)
    return res[:n], res[n], res[n + 1]


def _adamw(w, g, m, v):
    m2 = ADAM_B1 * m + (1.0 - ADAM_B1) * g
    v2 = ADAM_B2 * v + (1.0 - ADAM_B2) * (g * g)
    m_hat = m2 / (1.0 - ADAM_B1 ** ADAM_STEP)
    v_hat = v2 / (1.0 - ADAM_B2 ** ADAM_STEP)
    delta = -ADAM_LR * (m_hat / (jnp.sqrt(v_hat) + ADAM_EPS) + ADAM_WD * w)
    return delta, m2, v2


def _adamw_w_ada(cmat, dmod_cols, w, m, v):
    r, cdim = w.shape
    tr = _row_tile(r)

    def body(c_ref, d_ref, w_ref, m_ref, v_ref, g_ref, dl_ref, m2_ref, v2_ref):
        g = _dot_tn(c_ref[...].astype(BF16), d_ref[...].astype(BF16))
        g_ref[...] = g
        dl_ref[...], m2_ref[...], v2_ref[...] = _adamw(w_ref[...], g, m_ref[...], v_ref[...])

    row = pl.BlockSpec((tr, cdim), lambda i: (i, 0))
    return pl.pallas_call(
        body, name="adamw_w_ada", grid=(r // tr,),
        in_specs=[pl.BlockSpec((N_DEV, tr), lambda i: (0, i)), _full(dmod_cols.shape), row, row, row],
        out_specs=[row] * 4, out_shape=[jax.ShapeDtypeStruct(w.shape, F32)] * 4,
        compiler_params=_params(("parallel",)),
    )(cmat, dmod_cols, w, m, v)


SMALL_ROWS = 16
SMALL_AT = {"b_ada": (0, 6, 0, D_MODEL), "g_pre_mix": (6, 1, 0, D_MODEL), "g_post_mix": (7, 1, 0, D_MODEL),
            "g_pre_ffn": (8, 1, 0, D_MODEL), "g_post_ffn": (9, 1, 0, D_MODEL), "conv_b": (10, 1, 0, D_MODEL),
            "mlstm_norm_w": (11, 1, 0, 512), "b_if": (11, 1, 512, LANES), "attn_sinks": (11, 1, 640, LANES)}


def _small_table(part):
    tail = jnp.concatenate([part["mlstm_norm_w"], jnp.pad(part["b_if"], (0, LANES - 8)),
                            jnp.pad(part["attn_sinks"], (0, LANES - 8)), jnp.zeros((256,), F32)])
    return jnp.concatenate([part["b_ada"], part["gains"], part["conv_b"][None], tail[None],
                            jnp.full((1, D_MODEL), part["loss"], F32),
                            jnp.zeros((SMALL_ROWS - 13, D_MODEL), F32)], axis=0)


LOSS_ROW = 12


def _adamw_small(gathered, wmv):
    names = list(SMALL_AT)

    def body(*refs):
        g_ref, ins, outs = refs[0], refs[1:1 + 3 * len(names)], refs[1 + 3 * len(names):]
        g = g_ref[0]
        for k in range(1, N_DEV):
            g = g + g_ref[k]
        outs[4 * len(names)][...] = g[LOSS_ROW:LOSS_ROW + 1, 0:LANES]
        for i, n in enumerate(names):
            r0, rows, l0, lanes = SMALL_AT[n]
            gi = jnp.concatenate([g[r:r + 1, l0:l0 + lanes] for r in range(r0, r0 + rows)], axis=1)
            w_ref, m_ref, v_ref = ins[3 * i:3 * i + 3]
            go, dl, m2, v2 = outs[4 * i:4 * i + 4]
            go[...] = gi
            dl[...], m2[...], v2[...] = _adamw(w_ref[...], gi, m_ref[...], v_ref[...])

    flat = [a for n in names for a in wmv[n]]
    res = pl.pallas_call(
        body, name="adamw_small",
        out_shape=[jax.ShapeDtypeStruct(wmv[n][0].shape, F32) for n in names for _ in range(4)]
        + [jax.ShapeDtypeStruct((1, LANES), F32)],
        compiler_params=_params(),
    )(gathered, *flat)
    out = {n: res[4 * i:4 * i + 4] for i, n in enumerate(names)}
    out["loss"] = res[4 * len(names)]
    return out


def _adamw_sum_many(items, name):
    n = len(items)

    def body(*refs):
        ins, outs = refs[:4 * n], refs[4 * n:]
        for i in range(n):
            r_ref, w_ref, m_ref, v_ref = ins[4 * i:4 * i + 4]
            go, dl, m2, v2 = outs[4 * i:4 * i + 4]
            g = _sum_partials(r_ref)
            go[...] = g
            dl[...], m2[...], v2[...] = _adamw(w_ref[...], g, m_ref[...], v_ref[...])

    res = pl.pallas_call(
        body, name=name, out_shape=[jax.ShapeDtypeStruct(it[1].shape, F32) for it in items for _ in range(4)],
        compiler_params=_params(),
    )(*[a for it in items for a in it])
    return [res[4 * i:4 * i + 4] for i in range(n)]


def _row_tile(rows):
    return rows // 4 if rows >= 512 and rows % 64 == 0 else rows


def _sum_partials(r_ref):
    g = r_ref[0].astype(F32)
    for k in range(1, N_DEV):
        g = g + r_ref[k].astype(F32)
    return g


def _adamw_sum(recv, w, m, v, name):
    r, cdim = w.shape
    tr = _row_tile(r)

    def body(r_ref, w_ref, m_ref, v_ref, g_ref, dl_ref, m2_ref, v2_ref):
        g = _sum_partials(r_ref)
        g_ref[...] = g
        dl_ref[...], m2_ref[...], v2_ref[...] = _adamw(w_ref[...], g, m_ref[...], v_ref[...])

    row = pl.BlockSpec((tr, cdim), lambda i: (i, 0))
    return pl.pallas_call(
        body, name=name, grid=(r // tr,),
        in_specs=[pl.BlockSpec((N_DEV, tr, cdim), lambda i: (0, i, 0)), row, row, row],
        out_specs=[row] * 4, out_shape=[jax.ShapeDtypeStruct((r, cdim), F32)] * 4,
        compiler_params=_params(("parallel",)),
    )(recv, w, m, v)


def _sum8(recv, name):
    _, r, cdim = recv.shape
    tr = _row_tile(r)

    def body(r_ref, g_ref):
        g_ref[...] = _sum_partials(r_ref)

    return pl.pallas_call(
        body, name=name, grid=(r // tr,),
        in_specs=[pl.BlockSpec((N_DEV, tr, cdim), lambda i: (0, i, 0))],
        out_specs=pl.BlockSpec((tr, cdim), lambda i: (i, 0)), out_shape=jax.ShapeDtypeStruct((r, cdim), F32),
        compiler_params=_params(("parallel",)),
    )(recv)


def _adamw_plain(g, w, m, v, name):
    r, cdim = w.shape
    tr = _row_tile(r)

    def body(g_ref, w_ref, m_ref, v_ref, dl_ref, m2_ref, v2_ref):
        dl_ref[...], m2_ref[...], v2_ref[...] = _adamw(w_ref[...], g_ref[...], m_ref[...], v_ref[...])

    row = pl.BlockSpec((tr, cdim), lambda i: (i, 0))
    return pl.pallas_call(
        body, name=name, grid=(r // tr,), in_specs=[row] * 4, out_specs=[row] * 3,
        out_shape=[jax.ShapeDtypeStruct((r, cdim), F32)] * 3,
        compiler_params=_params(("parallel",)),
    )(g, w, m, v)


IN_SHARD = 609
IN_SHARD_PAD = 640
IF_AT = A_W + M_W


def _regrouped(u):
    return u if u < IF_AT + 8 else u + (IF_W - 8)


def _selection(k, rows, row0, transpose):
    shape = (rows, IN_SHARD_PAD) if transpose else (IN_SHARD_PAD, rows)
    l = lax.broadcasted_iota(jnp.int32, shape, 1 if transpose else 0)
    r = lax.broadcasted_iota(jnp.int32, shape, 0 if transpose else 1) + row0
    u = l + IN_SHARD * k
    ru = u + jnp.where(u >= IF_AT + 8, IF_W - 8, 0)
    return ((ru == r) & (l < IN_SHARD)).astype(BF16)


def _regroup_w_in(g):
    def body(g_ref, o_ref):
        for cb in range(CAT_W // LANES):
            r0 = cb * LANES
            acc = jnp.zeros((D_MODEL, LANES), F32)
            for k in range(N_DEV):
                lo, hi = _regrouped(IN_SHARD * k), _regrouped(IN_SHARD * k + IN_SHARD - 1)
                if hi >= r0 and lo < r0 + LANES:
                    acc = acc + _dot(g_ref[k], _selection(k, LANES, r0, False))
            o_ref[:, r0:r0 + LANES] = acc.astype(BF16)

    return pl.pallas_call(
        body, name="regroup_w_in", out_shape=jax.ShapeDtypeStruct((D_MODEL, CAT_W), BF16),
        compiler_params=_params(),
    )(g)


def _ungroup_w_in(g_parts):
    n = len(g_parts)

    def body(*refs):
        o_ref, g_ref = refs[n], refs[n + 1]
        at = 0
        for p in refs[:n]:
            g_ref[:, at:at + p.shape[1]] = p[...]
            at += p.shape[1]
        for k in range(N_DEV):
            lo, hi = _regrouped(IN_SHARD * k), _regrouped(IN_SHARD * k + IN_SHARD - 1)
            w0, w1 = lo // LANES * LANES, (hi // LANES + 1) * LANES
            o_ref[k] = _dot(g_ref[:, w0:w1], _selection(k, w1 - w0, w0, True)).astype(BF16)

    return pl.pallas_call(
        body, name="ungroup_w_in", out_shape=jax.ShapeDtypeStruct((N_DEV, D_MODEL, IN_SHARD_PAD), BF16),
        scratch_shapes=[pltpu.VMEM((D_MODEL, CAT_W), BF16)], compiler_params=_params(),
    )(*g_parts)


WEIGHT_NAMES = ("w_ada", "b_ada", "g_pre_mix", "g_post_mix", "w_in", "b_if", "conv_w", "conv_b", "attn_sinks",
                "mlstm_norm_w", "w_branch_attn", "w_branch_mlstm", "w_out", "g_pre_ffn", "g_post_ffn",
                "w_ffn_gate", "w_ffn_up", "w_ffn_down")


def kernel(x, c, positions, w_ada, b_ada, g_pre_mix, g_post_mix, w_in, b_if, conv_w, conv_b, attn_sinks, mlstm_norm_w, w_branch_attn, w_branch_mlstm, w_out, g_pre_ffn, g_post_ffn, w_ffn_gate, w_ffn_up, w_ffn_down, loss_target, m_w_ada, m_b_ada, m_g_pre_mix, m_g_post_mix, m_w_in, m_b_if, m_conv_w, m_conv_b, m_attn_sinks, m_mlstm_norm_w, m_w_branch_attn, m_w_branch_mlstm, m_w_out, m_g_pre_ffn, m_g_post_ffn, m_w_ffn_gate, m_w_ffn_up, m_w_ffn_down, v_w_ada, v_b_ada, v_g_pre_mix, v_g_post_mix, v_w_in, v_b_if, v_conv_w, v_conv_b, v_attn_sinks, v_mlstm_norm_w, v_w_branch_attn, v_w_branch_mlstm, v_w_out, v_g_pre_ffn, v_g_post_ffn, v_w_ffn_gate, v_w_ffn_up, v_w_ffn_down):
    given = dict(locals())
    W = {n: given[n][0] for n in WEIGHT_NAMES}
    M = {n: given["m_" + n][0] for n in WEIGHT_NAMES}
    V = {n: given["v_" + n][0] for n in WEIGHT_NAMES}
    me = 4 * lax.axis_index("x") + 2 * lax.axis_index("y") + lax.axis_index("c")

    ff_sh = D_FF // N_DEV
    ada_w = D_MODEL * 6 // N_DEV
    b_cols = lax.dynamic_slice(W["b_ada"], (me * ada_w,), (ada_w,)).reshape(1, ada_w)
    (g_in, g_conv), cg, mod_recv = _gather_front(
        [jnp.pad(W["w_in"], ((0, 0), (0, IN_SHARD_PAD - IN_SHARD))).astype(BF16), jnp.pad(W["conv_w"], ((0, 4), (0, 0)))],
        c.reshape(8, D_MODEL // 8), W["w_ada"], b_cols)
    cmat = cg.reshape(N_DEV, D_MODEL)
    mod = mod_recv[:, 0, :].reshape(6, D_MODEL)

    st_b = _exchange_start([W["w_branch_attn"].astype(BF16), W["w_branch_mlstm"].astype(BF16), W["w_out"].astype(BF16),
                            W["w_ffn_gate"].T.astype(BF16), W["w_ffn_up"].T.astype(BF16), W["w_ffn_down"].astype(BF16)],
                           True, mod_recv, "gather_rest_start")
    cols = lambda g: g.transpose(1, 0, 2).reshape(g.shape[1], N_DEV * g.shape[2])
    gains = jnp.stack([W["g_pre_mix"], W["g_post_mix"], W["g_pre_ffn"], W["g_post_ffn"]])
    xs, tgt = x[0], loss_target[0]
    t = _tables(mod, gains, cols(g_conv)[0:4], W["conv_b"], W["b_if"], W["mlstm_norm_w"], positions[0])
    vecs = t["vecs"]
    t["vecs"] = _tie(vecs, st_b["token"], "tie_fwd")
    w_cat = _regroup_w_in(g_in)
    a = _mixer_fwd(xs, t, W["attn_sinks"], w_cat)
    g_ba, g_bm, g_out, g_gate, g_up, g_down = _exchange_wait(st_b, a["ym"], "gather_rest_wait")
    w_ba, w_bm, w_out = cols(g_ba), cols(g_bm), g_out.reshape(D_MODEL, D_MODEL)
    b = _ffn_part(xs, tgt, t, a, w_ba, w_bm, w_out, g_gate.reshape(D_FF, D_MODEL), g_up.reshape(D_FF, D_MODEL),
                  g_down.reshape(D_FF, D_MODEL))

    st_f = _exchange_start([b["g_w_gate"].reshape(N_DEV, ff_sh, D_MODEL), b["g_w_up"].reshape(N_DEV, ff_sh, D_MODEL),
                            b["g_w_down"].reshape(N_DEV, ff_sh, D_MODEL)], False, b["dx1"], "scatter_ffn_start")
    t["vecs"] = _tie(vecs, st_f["token"], "tie_bwd")
    cm = _mixer_bwd(b["dx1"], t, a, b, W["attn_sinks"], w_ba, w_bm, w_out)
    pieces = lambda g, n: g.reshape(g.shape[0], N_DEV, n).transpose(1, 0, 2)
    st_m = _exchange_start([_ungroup_w_in(cm["g_w_cat"]), pieces(cm["g_w_ba"], 128), pieces(cm["g_w_bm"], 128),
                            cm["g_w_out"].reshape(N_DEV, D_MODEL // N_DEV, D_MODEL),
                            jnp.pad(pieces(cm["dcw"][0:4], 128), ((0, 0), (0, 4), (0, 0)))], False, cm["dcw"],
                           "scatter_mixer_start")
    r_gate, r_up, r_down = _exchange_wait(st_f, st_m["token"], "scatter_ffn_wait")
    grad_x, acc_p = _pre_bwd(cm["dproj"], xs, b["dx1"], _tie(vecs, st_m["token"], "tie_pre_bwd"), w_cat)
    small = _small_grads(acc_p, b, cm)
    loss = b["loss"]

    big_out = [{} for _ in range(4)]

    def put(n, res):
        for k in range(4):
            big_out[k][n] = res[k][None]

    put("w_ffn_down", _adamw_sum(r_down, W["w_ffn_down"], M["w_ffn_down"], V["w_ffn_down"], "adamw_w_ffn_down"))
    for n, r in (("w_ffn_gate", r_gate), ("w_ffn_up", r_up)):
        put(n, [o.T for o in _adamw_sum(r, W[n].T, M[n].T, V[n].T, "adamw_" + n)])

    sg = _small_exchange(_small_table({"b_ada": small["dmod"], "gains": small["dgains"], "conv_b": small["dconv_b"],
                                       "mlstm_norm_w": small["dnorm_w"], "b_if": small["db_if"],
                                       "attn_sinks": small["dsinks"], "loss": loss}), True, "gather_small")
    as_row = lambda a, n: jnp.pad(a, (0, SMALL_AT[n][3] * SMALL_AT[n][1] - a.shape[0]))[None]
    small_res = _adamw_small(sg, {n: [as_row(d[n], n) for d in (W, M, V)] for n in SMALL_AT})
    small_out = [{n: small_res[n][k][:, 0:W[n].shape[0]] for n in SMALL_AT} for k in range(4)]
    dmod_cols = lax.dynamic_slice(sg[:, 0:6, :].reshape(N_DEV, 6 * D_MODEL), (0, me * ada_w), (N_DEV, ada_w))
    ada_out = _adamw_w_ada(cmat, dmod_cols, W["w_ada"], M["w_ada"], V["w_ada"])

    r_in, r_ba, r_bm, r_out, r_conv = _exchange_wait(st_m, ada_out[1], "scatter_mixer_wait")
    pad4 = lambda v: jnp.pad(v, ((0, 4), (0, 0)))
    res_ba, res_bm, res_out, res_conv = _adamw_sum_many(
        [(r, W[n], M[n], V[n]) for n, r in (("w_branch_attn", r_ba), ("w_branch_mlstm", r_bm), ("w_out", r_out))]
        + [(r_conv, pad4(W["conv_w"]), pad4(M["conv_w"]), pad4(V["conv_w"]))], "adamw_mixer_small")
    put("w_branch_attn", res_ba)
    put("w_branch_mlstm", res_bm)
    put("w_out", res_out)
    put("conv_w", [o[0:4] for o in res_conv])
    g = _sum8(r_in, "sum_w_in")[:, 0:IN_SHARD].T
    put("w_in", [o.T for o in [g] + list(_adamw_plain(g, W["w_in"].T, M["w_in"].T, V["w_in"].T, "adamw_w_in"))])

    outs = [small_res["loss"][0, 0], grad_x[None]]
    for k in range(4):
        for n in WEIGHT_NAMES:
            if n == "w_ada":
                outs.append(ada_out[k][None])
            elif n in big_out[k]:
                outs.append(big_out[k][n])
            else:
                outs.append(small_out[k][n])
    return tuple(outs)
```

```python
import jax
import jax.numpy as jnp
from jax import lax
from jax.experimental import pallas as pl
from jax.experimental.pallas import tpu as pltpu

F32 = jnp.float32
BF16 = jnp.bfloat16

N_DEV = 8
D_MODEL = 1024
D_FF = 2816
N_Q_HEADS = 8
HEAD_DIM = 64
ATTN_BLOCK = 128
ROPE_THETA = 10000.0
MLSTM_HEADS = 4
MLSTM_HEAD_DIM = 128
MLSTM_CHUNK = 128
NORM_EPS = 1e-6
ADAM_LR = 0.001
ADAM_B1 = 0.9
ADAM_B2 = 0.999
ADAM_EPS = 1e-08
ADAM_WD = 0.01
ADAM_STEP = 10

ROW_TILE = 256
WIDE_TILE = 512
LANES = 128
NEG = -1e30
VMEM_LIMIT = 56 * 1024 * 1024

A_W = 768
M_W = 2048
IF_W = 128
G_W = 2048
CAT_W = A_W + M_W + IF_W + G_W

R_SHIFT_M, R_SCALE_M, R_GATE_M, R_SHIFT_F, R_SCALE_F, R_GATE_F = 0, 1, 2, 3, 4, 5
R_G_PRE_MIX, R_G_POST_MIX, R_G_PRE_FFN, R_G_POST_FFN = 6, 7, 8, 9


def _dot(a, b):
    return jnp.dot(a, b, preferred_element_type=F32)


def _dot_nt(a, b):
    return lax.dot_general(a, b, (((1,), (1,)), ((), ())), preferred_element_type=F32)


def _dot_tn(a, b):
    return lax.dot_general(a, b, (((0,), (0,)), ((), ())), preferred_element_type=F32)


def _recip(x):
    return 1.0 / x


def _sigmoid(x):
    return _recip(1.0 + jnp.exp(-x))


def _colsum(x):
    return jnp.sum(x, axis=0, keepdims=True)


def _rowmean(x):
    return jnp.mean(x, axis=-1, keepdims=True)


def _params(sem=None, vmem=VMEM_LIMIT):
    kw = dict(vmem_limit_bytes=vmem)
    if sem is not None:
        kw["dimension_semantics"] = sem
    return pltpu.CompilerParams(**kw)


def _full(shape):
    nd = len(shape)
    return pl.BlockSpec(shape, lambda *_: (0,) * nd)


def _pre_proj(x, vecs, w_cat):
    S = x.shape[0]
    tm = WIDE_TILE

    def body(x_ref, v_ref, w_ref, h_ref, pa_ref, pm_ref, pif_ref, pg_ref):
        xv = x_ref[...]
        r = lax.rsqrt(_rowmean(xv * xv) + NORM_EPS)
        h = (xv * r * v_ref[R_G_PRE_MIX:R_G_PRE_MIX + 1, :]) * (1.0 + v_ref[R_SCALE_M:R_SCALE_M + 1, :]) \
            + v_ref[R_SHIFT_M:R_SHIFT_M + 1, :]
        hb = h.astype(BF16)
        h_ref[...] = hb
        pa_ref[...] = _dot(hb, w_ref[:, 0:A_W])
        pm_ref[...] = _dot(hb, w_ref[:, A_W:A_W + M_W])
        pif_ref[...] = _dot(hb, w_ref[:, A_W + M_W:A_W + M_W + IF_W])
        pg_ref[...] = _dot(hb, w_ref[:, A_W + M_W + IF_W:CAT_W]).astype(BF16)

    row = lambda w: pl.BlockSpec((tm, w), lambda i: (i, 0))
    return pl.pallas_call(
        body, name="pre_proj", grid=(S // tm,),
        in_specs=[row(D_MODEL), _full(vecs.shape), _full(w_cat.shape)],
        out_specs=[row(D_MODEL), row(A_W), row(M_W), row(IF_W), row(G_W)],
        out_shape=[jax.ShapeDtypeStruct((S, D_MODEL), BF16), jax.ShapeDtypeStruct((S, A_W), F32),
                   jax.ShapeDtypeStruct((S, M_W), F32), jax.ShapeDtypeStruct((S, IF_W), F32),
                   jax.ShapeDtypeStruct((S, G_W), BF16)],
        compiler_params=_params(("parallel",)),
    )(x, vecs, w_cat)


def _mix_fwd(x, ya, ym, pg, vecs, w_ba, w_bm, w_out):
    S = x.shape[0]
    tm = WIDE_TILE

    def body(x_ref, ya_ref, ym_ref, pg_ref, v_ref, wba_ref, wbm_ref, wout_ref,
             x1_ref, merged_ref, mix_ref, pa_ref, pb_ref):
        pa = _dot(ya_ref[...], wba_ref[...])
        pb = _dot(ym_ref[...], wbm_ref[...])
        merged = _sigmoid(pg_ref[:, 0:D_MODEL].astype(F32)) * pa + _sigmoid(pg_ref[:, D_MODEL:G_W].astype(F32)) * pb
        mb = merged.astype(BF16)
        mix = _dot(mb, wout_ref[...])
        r = lax.rsqrt(_rowmean(mix * mix) + NORM_EPS)
        x1_ref[...] = x_ref[...] + v_ref[R_GATE_M:R_GATE_M + 1, :] * (mix * r * v_ref[R_G_POST_MIX:R_G_POST_MIX + 1, :])
        merged_ref[...] = mb
        mix_ref[...] = mix.astype(BF16)
        pa_ref[...] = pa.astype(BF16)
        pb_ref[...] = pb.astype(BF16)

    row = lambda w: pl.BlockSpec((tm, w), lambda i: (i, 0))
    sd = lambda w, dt: jax.ShapeDtypeStruct((S, w), dt)
    return pl.pallas_call(
        body, name="mix_fwd", grid=(S // tm,),
        in_specs=[row(D_MODEL), row(512), row(512), row(G_W), _full(vecs.shape), _full(w_ba.shape),
                  _full(w_bm.shape), _full(w_out.shape)],
        out_specs=[row(D_MODEL)] * 5,
        out_shape=[sd(D_MODEL, F32), sd(D_MODEL, BF16), sd(D_MODEL, BF16), sd(D_MODEL, BF16), sd(D_MODEL, BF16)],
        compiler_params=_params(("parallel",)),
    )(x, ya, ym, pg, vecs, w_ba, w_bm, w_out)


def _ffn_fwd_bwd(x1, tgt, vecs, w_gate, w_up, w_down):
    S = x1.shape[0]
    tm = ROW_TILE

    def body(x1_ref, tgt_ref, v_ref, wg_hbm, wu_hbm, wd_hbm,
             dx1_ref, h2_ref, hid_ref, da_ref, du_ref, dff_ref, acc_ref, loss_ref,
             wg, wu, wd, sem):
        i = pl.program_id(0)

        @pl.when(i == 0)
        def _():
            cps = [pltpu.make_async_copy(wg_hbm, wg, sem.at[0]), pltpu.make_async_copy(wu_hbm, wu, sem.at[1]),
                   pltpu.make_async_copy(wd_hbm, wd, sem.at[2])]
            for cp in cps:
                cp.start()
            for cp in cps:
                cp.wait()
            acc_ref[...] = jnp.zeros_like(acc_ref)
            loss_ref[...] = jnp.zeros_like(loss_ref)

        vrow = lambda r: v_ref[r:r + 1, :]
        x1v = x1_ref[...]
        r3 = lax.rsqrt(_rowmean(x1v * x1v) + NORM_EPS)
        x1hat = x1v * r3
        xn3 = x1hat * vrow(R_G_PRE_FFN)
        h2b = (xn3 * (1.0 + vrow(R_SCALE_F)) + vrow(R_SHIFT_F)).astype(BF16)
        h2_ref[...] = h2b
        a = _dot_nt(h2b, wg[...])
        u = _dot_nt(h2b, wu[...])
        sg = _sigmoid(a)
        sil = a * sg
        hidb = (sil * u).astype(BF16)
        hid_ref[...] = hidb
        ff = _dot(hidb, wd[...])
        r4 = lax.rsqrt(_rowmean(ff * ff) + NORM_EPS)
        ffhat = ff * r4
        n4 = ffhat * vrow(R_G_POST_FFN)
        err = x1v + vrow(R_GATE_F) * n4 - tgt_ref[...]
        loss_ref[...] += jnp.sum(err * err) * (0.5 / D_MODEL)
        dy = err * (1.0 / D_MODEL)
        acc_ref[0:1, :] += _colsum(dy * n4)
        dn4 = dy * vrow(R_GATE_F)
        acc_ref[1:2, :] += _colsum(dn4 * ffhat)
        dffhat = dn4 * vrow(R_G_POST_FFN)
        dffb = (r4 * (dffhat - ffhat * _rowmean(dffhat * ffhat))).astype(BF16)
        dff_ref[...] = dffb
        dhid = _dot_nt(dffb, wd[...])
        dub = (dhid * sil).astype(BF16)
        dab = (dhid * u * (sg * (1.0 + a * (1.0 - sg)))).astype(BF16)
        da_ref[...] = dab
        du_ref[...] = dub
        dh2 = _dot(dab, wg[...]) + _dot(dub, wu[...])
        acc_ref[2:3, :] += _colsum(dh2 * xn3)
        acc_ref[3:4, :] += _colsum(dh2)
        dxn3 = dh2 * (1.0 + vrow(R_SCALE_F))
        acc_ref[4:5, :] += _colsum(dxn3 * x1hat)
        dx1hat = dxn3 * vrow(R_G_PRE_FFN)
        dx1_ref[...] = dy + r3 * (dx1hat - x1hat * _rowmean(dx1hat * x1hat))

    row = lambda w: pl.BlockSpec((tm, w), lambda i: (i, 0))
    sd = lambda w, dt: jax.ShapeDtypeStruct((S, w), dt)
    anyspec = pl.BlockSpec(memory_space=pl.ANY)
    return pl.pallas_call(
        body, name="ffn_fwd_bwd", grid=(S // tm,),
        in_specs=[row(D_MODEL), row(D_MODEL), _full(vecs.shape), anyspec, anyspec, anyspec],
        out_specs=[row(D_MODEL), row(D_MODEL), row(D_FF), row(D_FF), row(D_FF), row(D_MODEL),
                   _full((8, D_MODEL)), _full((8, LANES))],
        out_shape=[sd(D_MODEL, F32), sd(D_MODEL, BF16), sd(D_FF, BF16), sd(D_FF, BF16), sd(D_FF, BF16),
                   sd(D_MODEL, BF16), jax.ShapeDtypeStruct((8, D_MODEL), F32), jax.ShapeDtypeStruct((8, LANES), F32)],
        scratch_shapes=[pltpu.VMEM(w_gate.shape, BF16), pltpu.VMEM(w_up.shape, BF16), pltpu.VMEM(w_down.shape, BF16),
                        pltpu.SemaphoreType.DMA((3,))],
        compiler_params=_params(("arbitrary",)),
    )(x1, tgt, vecs, w_gate, w_up, w_down)


def _mix_bwd(dx1, mix, pa, pb, pg, vecs, w_ba, w_bm, w_out):
    S = dx1.shape[0]
    tm = WIDE_TILE

    def body(dx1_ref, mix_ref, pa_ref, pb_ref, pg_ref, v_ref, wba_ref, wbm_ref, wout_ref,
             dmix_ref, dpa_ref, dpb_ref, dg_ref, dya_ref, dym_ref, acc_ref):
        i = pl.program_id(0)

        @pl.when(i == 0)
        def _():
            acc_ref[...] = jnp.zeros_like(acc_ref)

        vrow = lambda r: v_ref[r:r + 1, :]
        dx1v = dx1_ref[...]
        mix = mix_ref[...].astype(F32)
        r2 = lax.rsqrt(_rowmean(mix * mix) + NORM_EPS)
        mixhat = mix * r2
        acc_ref[0:1, :] += _colsum(dx1v * (mixhat * vrow(R_G_POST_MIX)))
        dn2 = dx1v * vrow(R_GATE_M)
        acc_ref[1:2, :] += _colsum(dn2 * mixhat)
        dmixhat = dn2 * vrow(R_G_POST_MIX)
        dmixb = (r2 * (dmixhat - mixhat * _rowmean(dmixhat * mixhat))).astype(BF16)
        dmix_ref[...] = dmixb
        dmerged = _dot_nt(dmixb, wout_ref[...])
        sa = _sigmoid(pg_ref[:, 0:D_MODEL].astype(F32))
        sm = _sigmoid(pg_ref[:, D_MODEL:G_W].astype(F32))
        dpab = (dmerged * sa).astype(BF16)
        dpbb = (dmerged * sm).astype(BF16)
        dpa_ref[...] = dpab
        dpb_ref[...] = dpbb
        dg_ref[:, 0:D_MODEL] = (dmerged * pa_ref[...].astype(F32) * (sa * (1.0 - sa))).astype(BF16)
        dg_ref[:, D_MODEL:G_W] = (dmerged * pb_ref[...].astype(F32) * (sm * (1.0 - sm))).astype(BF16)
        dya_ref[...] = _dot_nt(dpab, wba_ref[...]).astype(BF16)
        dym_ref[...] = _dot_nt(dpbb, wbm_ref[...])

    row = lambda w: pl.BlockSpec((tm, w), lambda i: (i, 0))
    sd = lambda w, dt: jax.ShapeDtypeStruct((S, w), dt)
    return pl.pallas_call(
        body, name="mix_bwd", grid=(S // tm,),
        in_specs=[row(D_MODEL), row(D_MODEL), row(D_MODEL), row(D_MODEL), row(G_W), _full(vecs.shape),
                  _full(w_ba.shape), _full(w_bm.shape), _full(w_out.shape)],
        out_specs=[row(D_MODEL), row(D_MODEL), row(D_MODEL), row(G_W), row(512), row(512), _full((8, D_MODEL))],
        out_shape=[sd(D_MODEL, BF16), sd(D_MODEL, BF16), sd(D_MODEL, BF16), sd(G_W, BF16), sd(512, BF16), sd(512, F32),
                   jax.ShapeDtypeStruct((8, D_MODEL), F32)],
        compiler_params=_params(("arbitrary",)),
    )(dx1, mix, pa, pb, pg, vecs, w_ba, w_bm, w_out)


def _pre_bwd(pieces, x, dx1, vecs, w_cat):
    S = x.shape[0]
    tm = WIDE_TILE
    n = len(pieces)
    starts = [sum(p.shape[1] for p in pieces[:k]) for k in range(n + 1)]

    def body(*refs):
        p_refs = refs[:n]
        x_ref, dx1_ref, v_ref, w_ref, dx_ref, acc_ref = refs[n:]
        i = pl.program_id(0)

        @pl.when(i == 0)
        def _():
            acc_ref[...] = jnp.zeros_like(acc_ref)

        vrow = lambda r: v_ref[r:r + 1, :]
        dh = _dot_nt(p_refs[0][...], w_ref[:, starts[0]:starts[1]])
        for k in range(1, n):
            dh = dh + _dot_nt(p_refs[k][...], w_ref[:, starts[k]:starts[k + 1]])
        xv = x_ref[...]
        r1 = lax.rsqrt(_rowmean(xv * xv) + NORM_EPS)
        xhat = xv * r1
        acc_ref[0:1, :] += _colsum(dh * (xhat * vrow(R_G_PRE_MIX)))
        acc_ref[1:2, :] += _colsum(dh)
        dxn = dh * (1.0 + vrow(R_SCALE_M))
        acc_ref[2:3, :] += _colsum(dxn * xhat)
        dxhat = dxn * vrow(R_G_PRE_MIX)
        dx_ref[...] = dx1_ref[...] + r1 * (dxhat - xhat * _rowmean(dxhat * xhat))

    row = lambda w: pl.BlockSpec((tm, w), lambda i: (i, 0))
    return pl.pallas_call(
        body, name="pre_bwd", grid=(S // tm,),
        in_specs=[row(p.shape[1]) for p in pieces] + [row(D_MODEL), row(D_MODEL), _full(vecs.shape), _full(w_cat.shape)],
        out_specs=[row(D_MODEL), _full((8, D_MODEL))],
        out_shape=[jax.ShapeDtypeStruct((S, D_MODEL), F32), jax.ShapeDtypeStruct((8, D_MODEL), F32)],
        compiler_params=_params(("arbitrary",)),
    )(*pieces, x, dx1, vecs, w_cat)


def _matmul_tn(a, b, tn, name, ts=1024):
    S, K = a.shape
    N = b.shape[1]
    n_s = S // ts

    def body(a_ref, b_ref, o_ref, acc_ref):
        s = pl.program_id(1)

        @pl.when(s == 0)
        def _():
            acc_ref[...] = jnp.zeros_like(acc_ref)

        acc_ref[...] += _dot_tn(a_ref[...], b_ref[...])

        @pl.when(s == n_s - 1)
        def _():
            o_ref[...] = acc_ref[...].astype(BF16)

    return pl.pallas_call(
        body, name=name, grid=(N // tn, n_s),
        in_specs=[pl.BlockSpec((ts, K), lambda j, s: (s, 0)), pl.BlockSpec((ts, tn), lambda j, s: (s, j))],
        out_specs=pl.BlockSpec((K, tn), lambda j, s: (0, j)),
        out_shape=jax.ShapeDtypeStruct((K, N), BF16),
        scratch_shapes=[pltpu.VMEM((K, tn), F32)],
        compiler_params=_params(("parallel", "arbitrary")),
    )(a, b)


def _rope_swap(t):
    lane = lax.broadcasted_iota(jnp.int32, t.shape, 1)
    first = (lane & (HEAD_DIM - 1)) < (HEAD_DIM // 2)
    return jnp.where(first, pltpu.roll(t, LANES - HEAD_DIM // 2, 1), pltpu.roll(t, HEAD_DIM // 2, 1))


def _rope(t, cos, sin_signed):
    return t * cos + _rope_swap(t) * sin_signed


def _rope_t(d, cos, sin_signed):
    return d * cos + _rope_swap(d * sin_signed)


def _to_kv_lanes(chunk, p, h):
    lane = lax.broadcasted_iota(jnp.int32, chunk.shape, 1)
    src = chunk if p == h else pltpu.roll(chunk, HEAD_DIM, 1)
    return jnp.where((lane >> 6) == h, src, jnp.zeros_like(src))


def _from_kv_lanes(o_a, o_b, h):
    lane = lax.broadcasted_iota(jnp.int32, o_a.shape, 1)
    a = o_a if h == 0 else pltpu.roll(o_a, HEAD_DIM, 1)
    b = o_b if h == 1 else pltpu.roll(o_b, HEAD_DIM, 1)
    return jnp.where(lane < HEAD_DIM, a, b)


def _band_bias(n):
    blk = ATTN_BLOCK
    qi = lax.broadcasted_iota(jnp.int32, (blk, 2 * blk), 0)
    kj = lax.broadcasted_iota(jnp.int32, (blk, 2 * blk), 1)
    seen = (kj > qi) & (kj <= qi + blk) & ((n > 0) | (kj >= blk))
    return jnp.concatenate([jnp.where(seen, 0.0, NEG)] * N_Q_HEADS, axis=0)


def _stack_heads(chunks, h, dtype):
    parts = []
    for g in range(4):
        j = 4 * h + g
        parts.append(_to_kv_lanes(chunks[j // 2], j % 2, h))
    return jnp.concatenate(parts, axis=0).astype(dtype)


def _fused_call(parts, name, n_steps):
    counts = [(len(p["in_specs"]), len(p["out_specs"]), len(p["scratch"])) for p in parts]
    n_in, n_out = sum(c[0] for c in counts), sum(c[1] for c in counts)

    def kernel_fn(*refs):
        i = pl.program_id(0)
        groups, a, b, c = [], 0, n_in, n_in + n_out
        for ci, co, cs in counts:
            groups.append(refs[a:a + ci] + refs[b:b + co] + refs[c:c + cs])
            a, b, c = a + ci, b + co, c + cs
        for p, g in zip(parts, groups):
            p["init"](i, *g)
        for p, g in zip(parts, groups):
            p["body"](i, *g)

    flat = lambda key: [v for p in parts for v in p[key]]
    res = pl.pallas_call(
        kernel_fn, name=name, grid=(n_steps,), in_specs=flat("in_specs"), out_specs=flat("out_specs"),
        out_shape=flat("out_shape"), scratch_shapes=flat("scratch"), compiler_params=_params(("arbitrary",)),
    )(*flat("operands"))
    out, pos = [], 0
    for _, co, _ in counts:
        out.append(res[pos:pos + co])
        pos += co
    return out


def _attn_fwd_part(pa, cos, sin, sinks):
    S = pa.shape[0]
    blk = ATTN_BLOCK
    nb = S // blk

    def body(n, sink_ref, cur_ref, prev_ref, cos_ref, sin_ref, cosp_ref, sinp_ref,
             ya_ref, qr_ref, kr_ref, vb_ref, lse_ref):
        cos_c, sin_c = cos_ref[...], sin_ref[...]
        qch = [_rope(cur_ref[:, c * LANES:(c + 1) * LANES], cos_c, sin_c) * (HEAD_DIM ** -0.5) for c in range(4)]
        for c in range(4):
            qr_ref[:, c * LANES:(c + 1) * LANES] = qch[c].astype(BF16)
        k_cur = _rope(cur_ref[:, 512:640], cos_c, sin_c).astype(BF16)
        k_prev = _rope(prev_ref[:, 0:LANES], cosp_ref[...], sinp_ref[...]).astype(BF16)
        v_cur = cur_ref[:, 640:768].astype(BF16)
        v_prev = prev_ref[:, LANES:2 * LANES].astype(BF16)
        kr_ref[...] = k_cur
        vb_ref[...] = v_cur
        K = jnp.concatenate([k_prev, k_cur], axis=0)
        V = jnp.concatenate([v_prev, v_cur], axis=0)
        lane = lax.broadcasted_iota(jnp.int32, (blk, LANES), 1)
        s = jnp.concatenate([_dot_nt(_stack_heads(qch, h, BF16), K) for h in range(2)], axis=0)
        s = s + _band_bias(n)
        rowmax = jnp.max(s, axis=1, keepdims=True)
        hd = lambda x, j: x[j * blk:(j + 1) * blk]
        m = jnp.concatenate([jnp.maximum(hd(rowmax, j), sink_ref[j]) for j in range(N_Q_HEADS)], axis=0)
        p = jnp.exp(s - m)
        pb = p.astype(BF16)
        den = _rowsum_mxu(pb) + jnp.concatenate([jnp.exp(sink_ref[j] - hd(m, j)) for j in range(N_Q_HEADS)], axis=0)
        o = jnp.concatenate([_dot(pb[4 * h * blk:4 * (h + 1) * blk], V) for h in range(2)], axis=0) * _recip(den)
        lse = m + jnp.log(den)
        outs = [o[j * blk:(j + 1) * blk, :] for j in range(N_Q_HEADS)]
        lse_tile = jnp.zeros((blk, LANES), F32)
        for j in range(N_Q_HEADS):
            lse_tile = jnp.where(lane == j, lse[j * blk:(j + 1) * blk, :], lse_tile)
        for c in range(4):
            ya_ref[:, c * LANES:(c + 1) * LANES] = _from_kv_lanes(outs[2 * c], outs[2 * c + 1], c // 2).astype(BF16)
        lse_ref[...] = lse_tile

    prev = lambda n: jnp.maximum(n - 1, 0)
    sd = lambda w, dt: jax.ShapeDtypeStruct((S, w), dt)
    return dict(
        init=lambda n, *refs: None, body=body, scratch=[], operands=[sinks, pa, pa, cos, sin, cos, sin],
        in_specs=[pl.BlockSpec(memory_space=pltpu.SMEM),
                  pl.BlockSpec((blk, A_W), lambda n: (n, 0)),
                  pl.BlockSpec((blk, 256), lambda n: (prev(n), 2)),
                  pl.BlockSpec((blk, LANES), lambda n: (n, 0)), pl.BlockSpec((blk, LANES), lambda n: (n, 0)),
                  pl.BlockSpec((blk, LANES), lambda n: (prev(n), 0)), pl.BlockSpec((blk, LANES), lambda n: (prev(n), 0))],
        out_specs=[pl.BlockSpec((blk, 512), lambda n: (n, 0)), pl.BlockSpec((blk, 512), lambda n: (n, 0)),
                   pl.BlockSpec((blk, LANES), lambda n: (n, 0)), pl.BlockSpec((blk, LANES), lambda n: (n, 0)),
                   pl.BlockSpec((blk, LANES), lambda n: (n, 0))],
        out_shape=[sd(512, BF16), sd(512, BF16), sd(LANES, BF16), sd(LANES, BF16), sd(LANES, F32)])


def _attn_bwd_part(dya, qr, kr, vb, lse, cos, sin, sinks):
    S = dya.shape[0]
    blk = ATTN_BLOCK
    nb = S // blk

    def init(n, sink_ref, dya_ref, qr_ref, kc_ref, kp_ref, vc_ref, vp_ref, lse_ref, cos_ref, sin_ref, cosp_ref, sinp_ref,
             dqkv_ref, last_ref, dsink_ref, ck, cv, cq):
        @pl.when(n == 0)
        def _():
            ck[...] = jnp.zeros_like(ck)
            cv[...] = jnp.zeros_like(cv)
            cq[...] = jnp.zeros_like(cq)
            dsink_ref[...] = jnp.zeros_like(dsink_ref)

    def body(n, sink_ref, dya_ref, qr_ref, kc_ref, kp_ref, vc_ref, vp_ref, lse_ref, cos_ref, sin_ref, cosp_ref, sinp_ref,
             dqkv_ref, last_ref, dsink_ref, ck, cv, cq):
        K = jnp.concatenate([kp_ref[...], kc_ref[...]], axis=0)
        V = jnp.concatenate([vp_ref[...], vc_ref[...]], axis=0)
        qch = [qr_ref[:, c * LANES:(c + 1) * LANES] for c in range(4)]
        dch = [dya_ref[:, c * LANES:(c + 1) * LANES].astype(F32) for c in range(4)]
        lse_tile = lse_ref[...]
        lane8 = lax.broadcasted_iota(jnp.int32, (8, LANES), 1)
        grp = lambda x, h: x[4 * h * blk:4 * (h + 1) * blk]
        qs = jnp.concatenate([_stack_heads(qch, h, BF16) for h in range(2)], axis=0)
        dos = jnp.concatenate([_stack_heads(dch, h, BF16) for h in range(2)], axis=0)
        lse_col = jnp.concatenate([lse_tile[:, j:j + 1] for j in range(N_Q_HEADS)], axis=0)
        s = jnp.concatenate([_dot_nt(grp(qs, h), K) for h in range(2)], axis=0)
        p = jnp.exp(s + _band_bias(n) - lse_col)
        dp = jnp.concatenate([_dot_nt(grp(dos, h), V) for h in range(2)], axis=0)
        delta = jnp.sum(p * dp, axis=1, keepdims=True)
        dsb = (p * (dp - delta)).astype(BF16)
        pb = p.astype(BF16)
        dq = jnp.concatenate([_dot(grp(dsb, h), K) for h in range(2)], axis=0)
        dk_acc = _dot_tn(grp(dsb, 0), grp(qs, 0)) + _dot_tn(grp(dsb, 1), grp(qs, 1))
        dv_acc = _dot_tn(grp(pb, 0), grp(dos, 0)) + _dot_tn(grp(pb, 1), grp(dos, 1))
        dqs = [dq[j * blk:(j + 1) * blk, :] for j in range(N_Q_HEADS)]
        dsink = jnp.zeros((8, LANES), F32)
        for j in range(N_Q_HEADS):
            rows = slice(j * blk, (j + 1) * blk)
            ps_delta = jnp.exp(sink_ref[j] - lse_col[rows]) * delta[rows]
            dsink = jnp.where(lane8 == j, dsink - jnp.sum(ps_delta), dsink)
        dsink_ref[...] += dsink
        cos_c, sin_c = cos_ref[...], sin_ref[...]
        dqkv_ref[:, 0:512] = cq[...]
        dqkv_ref[:, 512:640] = _rope_t(dk_acc[0:blk, :] + ck[...], cosp_ref[...], sinp_ref[...]).astype(BF16)
        dqkv_ref[:, 640:768] = (dv_acc[0:blk, :] + cv[...]).astype(BF16)
        ck[...] = dk_acc[blk:2 * blk, :]
        cv[...] = dv_acc[blk:2 * blk, :]
        for c in range(4):
            dqc = _from_kv_lanes(dqs[2 * c], dqs[2 * c + 1], c // 2) * (HEAD_DIM ** -0.5)
            dq_c = _rope_t(dqc, cos_c, sin_c).astype(BF16)
            cq[:, c * LANES:(c + 1) * LANES] = dq_c
            last_ref[:, c * LANES:(c + 1) * LANES] = dq_c
        last_ref[:, 512:640] = _rope_t(dk_acc[blk:2 * blk, :], cos_c, sin_c).astype(BF16)
        last_ref[:, 640:768] = dv_acc[blk:2 * blk, :].astype(BF16)

    prev = lambda n: jnp.maximum(n - 1, 0)
    same = lambda n: n
    bs = lambda w, f: pl.BlockSpec((blk, w), lambda n: (f(n), 0))
    return dict(
        init=init, body=body, operands=[sinks, dya, qr, kr, kr, vb, vb, lse, cos, sin, cos, sin],
        in_specs=[pl.BlockSpec(memory_space=pltpu.SMEM),
                  bs(512, same), bs(512, same), bs(LANES, same), bs(LANES, prev), bs(LANES, same), bs(LANES, prev),
                  bs(LANES, same), bs(LANES, same), bs(LANES, same), bs(LANES, prev), bs(LANES, prev)],
        out_specs=[bs(A_W, prev), _full((blk, A_W)), _full((8, LANES))],
        out_shape=[jax.ShapeDtypeStruct((S, A_W), BF16), jax.ShapeDtypeStruct((blk, A_W), BF16),
                   jax.ShapeDtypeStruct((8, LANES), F32)],
        scratch=[pltpu.VMEM((blk, LANES), F32), pltpu.VMEM((blk, LANES), F32), pltpu.VMEM((blk, 512), BF16)])


def _split3(x):
    hi = x.astype(BF16)
    r1 = x - hi.astype(F32)
    mid = r1.astype(BF16)
    lo = (r1 - mid.astype(F32)).astype(BF16)
    return hi, mid, lo


def _tri_matmul(tri_b, x):
    hi, mid, lo = _split3(x)
    return _dot(tri_b, hi) + _dot(tri_b, mid) + _dot(tri_b, lo)


def _log_sigmoid(x):
    return jnp.minimum(x, 0.0) - jnp.log(1.0 + jnp.exp(-jnp.abs(x)))


def _shift_rows(cur, seam, k, down):
    L = cur.shape[0]
    row8 = lax.broadcasted_iota(jnp.int32, seam.shape, 0)
    if down:
        mixed = jnp.concatenate([cur[:L - 8], jnp.where(row8 >= 8 - k, seam, cur[L - 8:])], axis=0)
        return pltpu.roll(mixed, k, 0)
    mixed = jnp.concatenate([jnp.where(row8 < k, seam, cur[:8]), cur[8:]], axis=0)
    return pltpu.roll(mixed, L - k, 0)


def _conv_fwd(cur, tail, cw_ref):
    z = cw_ref[4:5, :]
    for k in range(3, 0, -1):
        z = z + _shift_rows(cur, tail, k, True) * cw_ref[3 - k:4 - k, :]
    return z + cur * cw_ref[3:4, :]


def _rowsum_mxu(x, two_pass=False):
    ones = jnp.ones((x.shape[1], LANES), BF16)
    hi = x.astype(BF16)
    s = _dot(hi, ones)
    if two_pass:
        s = s + _dot((x - hi.astype(F32)).astype(BF16), ones)
    return s


def _stack(f):
    return jnp.concatenate([f(h) for h in range(MLSTM_HEADS)], axis=0)


def _head(x, h):
    L = x.shape[0] // MLSTM_HEADS
    return x[h * L:(h + 1) * L]


def _mlstm_heads_fwd(qk, cur_ref, gt, b_all, c_prev, nmv, tri, eye):
    L = qk.shape[0]
    HD = MLSTM_HEAD_DIM
    W4 = MLSTM_HEADS * HD
    col2row = lambda x: jnp.sum(jnp.where(eye, x, 0.0), axis=0, keepdims=True)
    b_col = _stack(lambda h: b_all[:, 4 + h:5 + h])
    i_col = _stack(lambda h: gt[:, h:h + 1])
    b_row = _stack(lambda h: jnp.broadcast_to(col2row(b_all[:, 4 + h:5 + h]), (L, L)))
    i_row = _stack(lambda h: jnp.broadcast_to(col2row(gt[:, h:h + 1]), (L, L)))
    bl = _stack(lambda h: jnp.broadcast_to(b_all[L - 1:L, 4 + h:5 + h], (L, 1)))
    m_prev = _stack(lambda h: jnp.broadcast_to(nmv[4 + h:5 + h, 0:1], (L, 1)))
    n_prev = _stack(lambda h: jnp.broadcast_to(nmv[h:h + 1, :], (L, HD)))
    tri4 = jnp.concatenate([tri] * MLSTM_HEADS, axis=0)
    Dm = jnp.where(tri4, b_col - b_row + i_row, NEG)
    inter = b_col + m_prev
    m_t = jnp.maximum(inter, jnp.max(Dm, axis=1, keepdims=True))
    W = jnp.exp(Dm - m_t)
    e_t = jnp.exp(inter - m_t)
    q = _stack(lambda h: qk[:, h * HD:(h + 1) * HD])
    k = _stack(lambda h: qk[:, W4 + h * HD:W4 + (h + 1) * HD]) * (HD ** -0.5)
    v = _stack(lambda h: cur_ref[:, 2 * W4 + h * HD:2 * W4 + (h + 1) * HD])
    qb, kb, vb = q.astype(BF16), k.astype(BF16), v.astype(BF16)
    Sc = _stack(lambda h: _dot_nt(_head(qb, h), _head(kb, h))) * W
    Scb = Sc.astype(BF16)
    cb = [c.astype(BF16) for c in c_prev]
    P1 = _stack(lambda h: _dot(_head(qb, h), cb[h]))
    num = _stack(lambda h: _dot(_head(Scb, h), _head(vb, h))) + e_t * P1
    qn = _rowsum_mxu(q * n_prev)
    den = _rowsum_mxu(Scb) + e_t * qn
    floor = jnp.broadcast_to(jnp.exp(-m_t), den.shape)
    inv_g = _recip(jnp.maximum(jnp.abs(den), floor))
    hv = num * inv_g
    a_col = bl - b_col + i_col
    a_max = _stack(lambda h: jnp.broadcast_to(jnp.max(_head(a_col, h), axis=0, keepdims=True), (L, 1)))
    m_new = jnp.maximum(bl + m_prev, a_max)
    dec = jnp.exp(bl + m_prev - m_new)
    u_col = jnp.exp(a_col - m_new)
    return dict(W=W, e_t=e_t, q=q, k=k, v=v, qb=qb, kb=kb, vb=vb, cb=cb, Sc=Sc, Scb=Scb, P1=P1, qn=qn, den=den,
                floor=floor, inv_g=inv_g, hv=hv, n_prev=n_prev, m_new=m_new, dec=dec, u_col=u_col)


def _mlstm_fwd_part(pm, pif, cw, sv):
    S = pm.shape[0]
    L = MLSTM_CHUNK
    nc = S // L
    HD = MLSTM_HEAD_DIM
    W4 = MLSTM_HEADS * HD

    def init(c, cur_ref, pif_ref, cw_ref, sv_ref, ym_ref, z_ref, cst_ref, nst_ref, C, nm, tail):
        @pl.when(c == 0)
        def _():
            C[...] = jnp.zeros_like(C)
            nm[...] = jnp.zeros_like(nm)
            tail[...] = jnp.zeros_like(tail)

    def body(c, cur_ref, pif_ref, cw_ref, sv_ref, ym_ref, z_ref, cst_ref, nst_ref, C, nm, tail):
        z = _conv_fwd(cur_ref[:, 0:2 * W4], tail[...], cw_ref)
        tail[...] = cur_ref[L - 8:L, 0:2 * W4]
        z_ref[...] = z
        qk = z * _sigmoid(z)
        gt = pif_ref[...] + sv_ref[1:2, 0:LANES]
        r_i = lax.broadcasted_iota(jnp.int32, (L, L), 0)
        c_i = lax.broadcasted_iota(jnp.int32, (L, L), 1)
        tri = c_i <= r_i
        eye = c_i == r_i
        b_all = _tri_matmul(tri.astype(BF16), _log_sigmoid(gt))
        nmv = nm[...]
        nst_ref[0] = nmv
        c_prev = [C[h] for h in range(MLSTM_HEADS)]
        f = _mlstm_heads_fwd(qk, cur_ref, gt, b_all, c_prev, nmv, tri, eye)
        hv = f["hv"]
        xc = hv - _rowsum_mxu(hv, True) * (1.0 / HD)
        hhat = xc * lax.rsqrt(_rowsum_mxu(xc * xc) * (1.0 / HD) + NORM_EPS)
        so = _sigmoid(_stack(lambda h: cur_ref[:, 3 * W4 + h * HD:3 * W4 + (h + 1) * HD]))
        wn = _stack(lambda h: jnp.broadcast_to(sv_ref[0:1, h * HD:(h + 1) * HD], (L, HD)))
        y = (so * hhat * wn).astype(BF16)
        kw = f["k"] * f["u_col"]
        kwb = kw.astype(BF16)
        n_new, m_new = [], []
        for h in range(MLSTM_HEADS):
            cst_ref[0, h] = c_prev[h]
            ym_ref[:, h * HD:(h + 1) * HD] = _head(y, h)
            dec = f["dec"][h * L:h * L + 1, :]
            C[h] = dec * c_prev[h] + _dot_tn(_head(kwb, h), _head(f["vb"], h))
            n_new.append(dec * nmv[h:h + 1, :] + _colsum(_head(kw, h)))
            m_new.append(jnp.broadcast_to(f["m_new"][h * L:h * L + 1, :], (1, LANES)))
        nm[...] = jnp.concatenate(n_new + m_new, axis=0)

    return dict(
        init=init, body=body, operands=[pm, pif, cw, sv],
        in_specs=[pl.BlockSpec((L, M_W), lambda c: (c, 0)),
                  pl.BlockSpec((L, IF_W), lambda c: (c, 0)), _full(cw.shape), _full(sv.shape)],
        out_specs=[pl.BlockSpec((L, W4), lambda c: (c, 0)), pl.BlockSpec((L, 2 * W4), lambda c: (c, 0)),
                   pl.BlockSpec((1, MLSTM_HEADS, HD, HD), lambda c: (c, 0, 0, 0)),
                   pl.BlockSpec((1, 8, LANES), lambda c: (c, 0, 0))],
        out_shape=[jax.ShapeDtypeStruct((S, W4), BF16), jax.ShapeDtypeStruct((S, 2 * W4), F32),
                   jax.ShapeDtypeStruct((nc, MLSTM_HEADS, HD, HD), F32), jax.ShapeDtypeStruct((nc, 8, LANES), F32)],
        scratch=[pltpu.VMEM((MLSTM_HEADS, HD, HD), F32), pltpu.VMEM((8, LANES), F32), pltpu.VMEM((8, 2 * W4), F32)])


def _mlstm_bwd_part(pm, zc, pif, cw, sv, dym, cst, nst):
    S = pm.shape[0]
    L = MLSTM_CHUNK
    nc = S // L
    HD = MLSTM_HEAD_DIM
    W4 = MLSTM_HEADS * HD

    def init(r, cur_ref, z_ref, pif_ref, cw_ref, sv_ref, dym_ref, cst_ref, nst_ref,
             dm_ref, dcw_ref, dsv_ref, dC, dn, dz_next, dqk):
        @pl.when(r == 0)
        def _():
            dC[...] = jnp.zeros_like(dC)
            dn[...] = jnp.zeros_like(dn)
            dz_next[...] = jnp.zeros_like(dz_next)
            dcw_ref[...] = jnp.zeros_like(dcw_ref)
            dsv_ref[...] = jnp.zeros_like(dsv_ref)

    def body(r, cur_ref, z_ref, pif_ref, cw_ref, sv_ref, dym_ref, cst_ref, nst_ref,
             dm_ref, dcw_ref, dsv_ref, dC, dn, dz_next, dqk):
        z = z_ref[...]
        sgz = _sigmoid(z)
        qk = z * sgz
        gt = pif_ref[...] + sv_ref[1:2, 0:LANES]
        r_i = lax.broadcasted_iota(jnp.int32, (L, L), 0)
        c_i = lax.broadcasted_iota(jnp.int32, (L, L), 1)
        tri = c_i <= r_i
        eye = c_i == r_i
        b_all = _tri_matmul(tri.astype(BF16), _log_sigmoid(gt))
        lane = lax.broadcasted_iota(jnp.int32, (L, LANES), 1)
        rowl = lax.broadcasted_iota(jnp.int32, (L, 1), 0)
        nmv = nst_ref[0]
        heads = range(MLSTM_HEADS)
        c_prev = [cst_ref[0, h] for h in heads]
        f = _mlstm_heads_fwd(qk, cur_ref, gt, b_all, c_prev, nmv, tri, eye)
        hv, inv_g, den, e_t, u_col, n_prev = f["hv"], f["inv_g"], f["den"], f["e_t"], f["u_col"], f["n_prev"]
        q, k, v, qb, kb, vb, Sc, Scb, W = f["q"], f["k"], f["v"], f["qb"], f["kb"], f["vb"], f["Sc"], f["Scb"], f["W"]
        xc = hv - _rowsum_mxu(hv, True) * (1.0 / HD)
        rstd = lax.rsqrt(_rowsum_mxu(xc * xc) * (1.0 / HD) + NORM_EPS)
        hhat = xc * rstd
        wn = _stack(lambda h: jnp.broadcast_to(sv_ref[0:1, h * HD:(h + 1) * HD], (L, HD)))
        so = _sigmoid(_stack(lambda h: cur_ref[:, 3 * W4 + h * HD:3 * W4 + (h + 1) * HD]))
        dy = _stack(lambda h: dym_ref[:, h * HD:(h + 1) * HD])
        d_o = (dy * hhat * wn * (so * (1.0 - so))).astype(BF16)
        dln = dy * so
        dwn = dln * hhat
        dhhat = dln * wn
        m2 = _rowsum_mxu(dhhat * hhat) * (1.0 / HD)
        dh = rstd * (dhhat - _rowsum_mxu(dhhat) * (1.0 / HD) - hhat * m2)
        dnum = dh * inv_g
        active = jnp.abs(den) > f["floor"]
        dden = jnp.where(active, -(HD * NORM_EPS) * m2 * rstd * rstd * inv_g * jnp.where(den >= 0.0, 1.0, -1.0), 0.0)
        dnumb = dnum.astype(BF16)
        dSc = _stack(lambda h: _dot_nt(_head(dnumb, h), _head(vb, h))) + dden
        dA = (dSc * W).astype(BF16)
        G = dSc * Sc
        Gb = G.astype(BF16)
        Gl = (G - Gb.astype(F32)).astype(BF16)
        ones = jnp.ones((L, LANES), BF16)
        Gr = _dot(Gb, ones) + _dot(Gl, ones)
        Gc = _stack(lambda h: _dot_tn(_head(Gb, h), ones) + _dot_tn(_head(Gl, h), ones))
        dCn = [dC[h] for h in heads]
        dCnb = [d.astype(BF16) for d in dCn]
        dnv = dn[...]
        dn_new = _stack(lambda h: jnp.broadcast_to(dnv[h:h + 1, :], (L, HD)))
        kdC = _stack(lambda h: _dot(_head(kb, h), dCnb[h]))
        vdC = _stack(lambda h: _dot_nt(_head(vb, h), dCnb[h]))
        dv = (_stack(lambda h: _dot_tn(_head(Scb, h), _head(dnumb, h))) + u_col * kdC).astype(BF16)
        dq = _stack(lambda h: _dot(_head(dA, h), _head(kb, h))) \
            + e_t * _stack(lambda h: _dot_nt(_head(dnumb, h), f["cb"][h])) + (e_t * dden) * n_prev
        dk = (_stack(lambda h: _dot_tn(_head(dA, h), _head(qb, h))) + u_col * (vdC + dn_new)) * (HD ** -0.5)
        E = (_rowsum_mxu(f["P1"] * dnum, True) + dden * f["qn"]) * e_t
        U = _rowsum_mxu(kdC * v + k * dn_new, True) * u_col
        qe = (q * e_t).astype(BF16)
        qd = (e_t * dden) * q
        di = Gc + U
        db = Gr + E - Gc - U
        di_tile = jnp.zeros((L, LANES), F32)
        db_tile = jnp.zeros((L, LANES), F32)
        dn_rows = []
        for h in heads:
            dec = f["dec"][h * L:h * L + 1, :]
            ddec = jnp.sum(dCn[h] * c_prev[h]) + jnp.sum(dnv[h:h + 1, :] * nmv[h:h + 1, :])
            dbl = ddec * dec + jnp.sum(_head(U, h), axis=0, keepdims=True)
            di_tile = jnp.where(lane == h, _head(di, h), di_tile)
            db_tile = jnp.where(lane == 4 + h, _head(db, h) + jnp.where(rowl == L - 1, dbl, 0.0), db_tile)
            dC[h] = dec * dCn[h] + _dot_tn(_head(qe, h), _head(dnumb, h))
            dn_rows.append(dec * dnv[h:h + 1, :] + _colsum(_head(qd, h)))
            dsv_ref[0:1, h * HD:(h + 1) * HD] += _colsum(_head(dwn, h))
            dqk[:, h * HD:(h + 1) * HD] = _head(dq, h)
            dqk[:, W4 + h * HD:W4 + (h + 1) * HD] = _head(dk, h)
            dm_ref[:, 2 * W4 + h * HD:2 * W4 + (h + 1) * HD] = _head(dv, h)
            dm_ref[:, 3 * W4 + h * HD:3 * W4 + (h + 1) * HD] = _head(d_o, h)
        dn[...] = jnp.concatenate(dn_rows + [jnp.zeros((8 - MLSTM_HEADS, LANES), F32)], axis=0)
        dlf = _tri_matmul((r_i <= c_i).astype(BF16), db_tile)
        dif = jnp.where(lane < 4, di_tile, jnp.where(lane < 8, dlf * (1.0 - _sigmoid(gt)), 0.0))
        dm_ref[:, M_W:M_W + IF_W] = dif.astype(BF16)
        dsv_ref[1:2, 0:LANES] += _colsum(dif)
        dz = dqk[...] * (sgz * (1.0 + z * (1.0 - sgz)))
        dcw_ref[4:5, :] += _colsum(dz)
        u = cur_ref[:, 0:2 * W4]
        du_in = dz * cw_ref[3:4, :]
        dcw_ref[3:4, :] += _colsum(dz * u)
        for k in range(1, 4):
            up = _shift_rows(dz, dz_next[...], k, False)
            dcw_ref[3 - k:4 - k, :] += _colsum(up * u)
            du_in = du_in + up * cw_ref[3 - k:4 - k, :]
        dz_next[...] = dz[0:8, :]
        dm_ref[:, 0:2 * W4] = du_in.astype(BF16)

    cidx = lambda r: nc - 1 - r
    return dict(
        init=init, body=body, operands=[pm, zc, pif, cw, sv, dym, cst, nst],
        in_specs=[pl.BlockSpec((L, M_W), lambda r: (cidx(r), 0)), pl.BlockSpec((L, 2 * W4), lambda r: (cidx(r), 0)),
                  pl.BlockSpec((L, IF_W), lambda r: (cidx(r), 0)), _full(cw.shape), _full(sv.shape),
                  pl.BlockSpec((L, W4), lambda r: (cidx(r), 0)),
                  pl.BlockSpec((1, MLSTM_HEADS, HD, HD), lambda r: (cidx(r), 0, 0, 0)),
                  pl.BlockSpec((1, 8, LANES), lambda r: (cidx(r), 0, 0))],
        out_specs=[pl.BlockSpec((L, M_W + IF_W), lambda r: (cidx(r), 0)), _full((8, 2 * W4)), _full((8, W4))],
        out_shape=[jax.ShapeDtypeStruct((S, M_W + IF_W), BF16),
                   jax.ShapeDtypeStruct((8, 2 * W4), F32), jax.ShapeDtypeStruct((8, W4), F32)],
        scratch=[pltpu.VMEM((MLSTM_HEADS, HD, HD), F32), pltpu.VMEM((8, LANES), F32),
                 pltpu.VMEM((8, 2 * W4), F32), pltpu.VMEM((L, 2 * W4), F32)])


def _rope_tables(positions):
    half = HEAD_DIM // 2
    inv_freq = ROPE_THETA ** (-2.0 * jnp.arange(half, dtype=F32) / HEAD_DIM)
    ang = positions.astype(F32)[:, None] * inv_freq
    cos = jnp.tile(jnp.cos(ang), (1, LANES // half))
    sign = jnp.tile(jnp.concatenate([-jnp.ones((half,), F32), jnp.ones((half,), F32)]), LANES // HEAD_DIM)
    sin = jnp.tile(jnp.sin(ang), (1, LANES // half)) * sign
    return cos, sin


def _local_step(x, tgt, positions, mod, gains, w_cat, w_ba, w_bm, w_out, w_gate, w_up, w_down,
                conv_w, conv_b, b_if, sinks, norm_w):
    t = _tables(mod, gains, conv_w, conv_b, b_if, norm_w, positions)
    a = _mixer_fwd(x, t, sinks, w_cat)
    b = _ffn_part(x, tgt, t, a, w_ba, w_bm, w_out, w_gate, w_up, w_down)
    c = _mixer_bwd(b["dx1"], t, a, b, sinks, w_ba, w_bm, w_out)
    grad_x, acc_p = _pre_bwd(c["dproj"], x, b["dx1"], t["vecs"], w_cat)
    big = dict(w_cat=jnp.concatenate(c["g_w_cat"], axis=1), w_ba=c["g_w_ba"], w_bm=c["g_w_bm"], w_out=c["g_w_out"], w_gate=b["g_w_gate"],
               w_up=b["g_w_up"], w_down=b["g_w_down"])
    return b["loss"], grad_x, big, _small_grads(acc_p, b, c)


def _tables(mod, gains, conv_w, conv_b, b_if, norm_w, positions):
    cos, sin = _rope_tables(positions)
    return dict(
        vecs=jnp.concatenate([mod, gains, jnp.zeros((6, D_MODEL), F32)], axis=0),
        cw=jnp.concatenate([conv_w, conv_b.reshape(1, -1), jnp.zeros((3, 2 * 512), F32)], axis=0),
        sv=jnp.zeros((8, 512), F32).at[0].set(norm_w).at[1, 0:8].set(b_if), cos=cos, sin=sin)


def _mixer_fwd(x, t, sinks, w_cat):
    h, pa, pm, pif, pg = _pre_proj(x, t["vecs"], w_cat)
    n_blk = x.shape[0] // ATTN_BLOCK
    (ya, qr, kr, vb, lse), = _fused_call([_attn_fwd_part(pa, t["cos"], t["sin"], sinks)], "attn_fwd", n_blk)
    (ym, zc, cst, nst), = _fused_call([_mlstm_fwd_part(pm, pif, t["cw"], t["sv"])], "mlstm_fwd", n_blk)
    return dict(h=h, pm=pm, pif=pif, pg=pg, ya=ya, qr=qr, kr=kr, vb=vb, lse=lse, ym=ym, zc=zc, cst=cst, nst=nst)


def _ffn_part(x, tgt, t, a, w_ba, w_bm, w_out, w_gate, w_up, w_down):
    x1, merged, mix, pba, pbm = _mix_fwd(x, a["ya"], a["ym"], a["pg"], t["vecs"], w_ba, w_bm, w_out)
    dx1, h2, hid, da, du, dff, acc_f, loss = _ffn_fwd_bwd(x1, tgt, t["vecs"], w_gate, w_up, w_down)
    return dict(merged=merged, mix=mix, pba=pba, pbm=pbm, dx1=dx1, acc_f=acc_f, loss=loss[0, 0],
                g_w_gate=_matmul_tn(da, h2, 1024, "dw_ffn_gate"),
                g_w_up=_matmul_tn(du, h2, 1024, "dw_ffn_up"),
                g_w_down=_matmul_tn(hid, dff, 1024, "dw_ffn_down"))


def _mixer_bwd(dx1, t, a, b, sinks, w_ba, w_bm, w_out):
    dmix, dpa, dpb, dg, dya, dym, acc_m = _mix_bwd(dx1, b["mix"], b["pba"], b["pbm"], a["pg"], t["vecs"], w_ba, w_bm, w_out)
    g_w_out = _matmul_tn(b["merged"], dmix, 1024, "dw_out")
    g_w_ba = _matmul_tn(a["ya"], dpa, 1024, "dw_branch_attn")
    g_w_bm = _matmul_tn(a["ym"], dpb, 1024, "dw_branch_mlstm")
    n_blk = dx1.shape[0] // ATTN_BLOCK
    (dqkv, dqkv_last, dsink), = _fused_call(
        [_attn_bwd_part(dya, a["qr"], a["kr"], a["vb"], a["lse"], t["cos"], t["sin"], sinks)], "attn_bwd", n_blk)
    (dm, dcw, dsv), = _fused_call(
        [_mlstm_bwd_part(a["pm"], a["zc"], a["pif"], t["cw"], t["sv"], dym, a["cst"], a["nst"])], "mlstm_bwd", n_blk)
    dqkv = lax.dynamic_update_slice(dqkv, dqkv_last, (dqkv.shape[0] - ATTN_BLOCK, 0))
    dproj = [dqkv, dm, dg]
    g_w_cat = [_matmul_tn(a["h"], p, 1024 if p.shape[1] % 1024 == 0 else p.shape[1], "dw_in_" + n)
               for p, n in zip(dproj, ("attn", "mlstm", "branch"))]
    return dict(dproj=dproj, g_w_cat=g_w_cat, g_w_out=g_w_out, g_w_ba=g_w_ba,
                g_w_bm=g_w_bm, acc_m=acc_m, dsink=dsink, dcw=dcw, dsv=dsv)


def _small_grads(acc_p, b, c):
    acc_f, acc_m = b["acc_f"], c["acc_m"]
    dmod = jnp.stack([acc_p[1], acc_p[0], acc_m[0], acc_f[3], acc_f[2], acc_f[0]])
    dgains = jnp.stack([acc_p[2], acc_m[1], acc_f[4], acc_f[1]])
    return dict(dmod=dmod, dgains=dgains, dconv_w=c["dcw"][0:4], dconv_b=c["dcw"][4], db_if=c["dsv"][1, 0:8],
                dsinks=c["dsink"][0, 0:8], dnorm_w=c["dsv"][0])


MESH_ID = pl.DeviceIdType.MESH


def _mesh_pos():
    return lax.axis_index("x"), lax.axis_index("y"), lax.axis_index("c")


def _flip(v, bit):
    return 1 - v if bit else v


def _relations():
    return [((r >> 2) & 1, (r >> 1) & 1, r & 1) for r in range(1, N_DEV)]


def _small_exchange(p, gather, name):
    R, V = p.shape[-2:]

    def body(p_ref, out_ref, send_sems, recv_sems):
        x, y, c = _mesh_pos()
        me = 4 * x + 2 * y + c
        out_ref[me] = p_ref[...] if gather else p_ref[me]
        peers = []
        for dx, dy, dc in _relations():
            px, py, pc = _flip(x, dx), _flip(y, dy), _flip(c, dc)
            peers.append(((px, py, pc), 4 * px + 2 * py + pc))

        def copy(k, landing):
            peer, pid = peers[k]
            return pltpu.make_async_remote_copy(
                src_ref=p_ref if gather else p_ref.at[pid], dst_ref=out_ref.at[landing],
                send_sem=send_sems.at[k], recv_sem=recv_sems.at[k], device_id=peer, device_id_type=MESH_ID)

        sends = [copy(k, me) for k in range(N_DEV - 1)]
        for cp in sends:
            cp.start()
        for k in range(N_DEV - 1):
            copy(k, peers[k][1]).wait_recv()
        for cp in sends:
            cp.wait_send()

    vm = pl.BlockSpec(memory_space=pltpu.VMEM)
    return pl.pallas_call(
        body, name=name, in_specs=[vm], out_specs=vm,
        out_shape=jax.ShapeDtypeStruct((N_DEV, R, V), F32),
        scratch_shapes=[pltpu.SemaphoreType.DMA((N_DEV - 1,)), pltpu.SemaphoreType.DMA((N_DEV - 1,))],
        compiler_params=pltpu.CompilerParams(vmem_limit_bytes=VMEM_LIMIT),
    )(p)


HBM_SPEC = pl.BlockSpec(memory_space=pltpu.HBM)
SEM_SPEC = pl.BlockSpec(memory_space=pltpu.SEMAPHORE)


def _peers(x, y, c):
    out = []
    for dx, dy, dc in _relations():
        px, py, pc = _flip(x, dx), _flip(y, dy), _flip(c, dc)
        out.append(((px, py, pc), 4 * px + 2 * py + pc))
    return out


def _exchange_start(arrs, gather, after, name):
    n = len(arrs)
    me_out = 4 * lax.axis_index("x") + 2 * lax.axis_index("y") + lax.axis_index("c")
    lands = []
    for a in arrs:
        own = a[None] if gather else lax.dynamic_index_in_dim(a, me_out, 0, keepdims=True)
        empty = lax.empty(((N_DEV,) + a.shape) if gather else a.shape, a.dtype)
        lands.append(lax.dynamic_update_index_in_dim(empty, own, me_out, 0))

    def body(*refs):
        a_refs, l_refs = refs[:n], refs[n:2 * n]
        send_sems, recv_sems = refs[2 * n + 1], refs[2 * n + 2]
        token = refs[4 * n + 3]
        x, y, c = _mesh_pos()
        me = 4 * x + 2 * y + c
        for a in range(n):
            for k, (peer, pid) in enumerate(_peers(x, y, c)):
                pltpu.make_async_remote_copy(
                    src_ref=a_refs[a] if gather else a_refs[a].at[pid], dst_ref=l_refs[a].at[me],
                    send_sem=send_sems.at[a * (N_DEV - 1) + k], recv_sem=recv_sems.at[a * (N_DEV - 1) + k],
                    device_id=peer, device_id_type=MESH_ID).start()
        token[...] = jnp.zeros_like(token)

    sem = pltpu.SemaphoreType.DMA((n * (N_DEV - 1),))
    hbm = lambda a: pltpu.with_memory_space_constraint(a, pltpu.HBM)
    res = pl.pallas_call(
        body, name=name,
        out_shape=(sem, sem, *[pltpu.HBM(a.shape, a.dtype) for a in arrs], *[pltpu.HBM(l.shape, l.dtype) for l in lands],
                   jax.ShapeDtypeStruct((8, LANES), F32)),
        in_specs=[HBM_SPEC] * (2 * n) + [pl.BlockSpec(memory_space=pl.ANY)],
        out_specs=(SEM_SPEC, SEM_SPEC, *[HBM_SPEC] * (2 * n), pl.BlockSpec(memory_space=pltpu.VMEM)),
        input_output_aliases={i: 2 + i for i in range(2 * n)},
        compiler_params=pltpu.CompilerParams(has_side_effects=pltpu.SideEffectType.DATAFLOW_SIDE_EFFECTING),
    )(*[hbm(a) for a in arrs], *[hbm(l) for l in lands], after)
    return dict(sems=res[0:2], arrs=res[2:2 + n], lands=res[2 + n:2 + 2 * n], token=res[2 + 2 * n], gather=gather)


def _exchange_wait(st, after, name):
    n = len(st["arrs"])
    gather = st["gather"]

    def body(*refs):
        a_refs, l_refs = refs[:n], refs[n:2 * n]
        send_sems, recv_sems = refs[2 * n], refs[2 * n + 1]
        x, y, c = _mesh_pos()
        for a in range(n):
            for k, (peer, pid) in enumerate(_peers(x, y, c)):
                cp = pltpu.make_async_remote_copy(
                    src_ref=a_refs[a] if gather else a_refs[a].at[pid], dst_ref=l_refs[a].at[pid],
                    send_sem=send_sems.at[a * (N_DEV - 1) + k], recv_sem=recv_sems.at[a * (N_DEV - 1) + k],
                    device_id=peer, device_id_type=MESH_ID)
                cp.wait_send()
                cp.wait_recv()

    both = list(st["arrs"]) + list(st["lands"])
    res = pl.pallas_call(
        body, name=name, out_shape=[pltpu.HBM(a.shape, a.dtype) for a in both],
        in_specs=[HBM_SPEC] * (2 * n) + [SEM_SPEC, SEM_SPEC, pl.BlockSpec(memory_space=pl.ANY)],
        out_specs=[HBM_SPEC] * (2 * n), input_output_aliases={i: i for i in range(2 * n)},
        compiler_params=pltpu.CompilerParams(has_side_effects=pltpu.SideEffectType.DATAFLOW_SIDE_EFFECTING),
    )(*both, *st["sems"], after)
    return res[n:2 * n]


def _tie(x, token, name):
    def body(x_ref, t_ref, o_ref):
        o_ref[...] = x_ref[...]

    vm = pl.BlockSpec(memory_space=pltpu.VMEM)
    return pl.pallas_call(
        body, name=name, in_specs=[vm, pl.BlockSpec(memory_space=pl.ANY)], out_specs=vm,
        out_shape=jax.ShapeDtypeStruct(x.shape, x.dtype),
    )(x, token)


def _gather_front(shards, c8, w_ada, b_cols):
    n = len(shards)
    ada_w = w_ada.shape[1]
    chunk = c8.shape[1]

    def body(*refs):
        p_refs = refs[:n]
        c_ref, wa_ref, b_ref = refs[n:n + 3]
        out_refs = refs[n + 3:2 * n + 3]
        cg_ref, mod_ref = refs[2 * n + 3], refs[2 * n + 4]
        send_sems, recv_sems, local_sems, c_send, c_recv, m_send, m_recv, pbuf = refs[2 * n + 5:]
        x, y, c = _mesh_pos()
        me_id = 4 * x + 2 * y + c
        me, sibling = (x, y, c), (x, y, 1 - c)
        chips = [(1 - x, y), (x, 1 - y), (1 - x, 1 - y)]
        peers = _peers(x, y, c)

        def copy(a, k, block, to, own=False):
            slot = out_refs[a].at[4 * block[0] + 2 * block[1] + block[2]]
            return pltpu.make_async_remote_copy(
                src_ref=p_refs[a] if own else slot, dst_ref=slot,
                send_sem=send_sems.at[a, k], recv_sem=recv_sems.at[a, k], device_id=to, device_id_type=MESH_ID)

        def c_copy(k, landing):
            return pltpu.make_async_remote_copy(
                src_ref=c_ref, dst_ref=cg_ref.at[landing], send_sem=c_send.at[k], recv_sem=c_recv.at[k],
                device_id=peers[k][0], device_id_type=MESH_ID)

        cg_ref[me_id] = c_ref[...]
        c_sends = [c_copy(k, me_id) for k in range(N_DEV - 1)]
        for cp in c_sends:
            cp.start()
        mine = [pltpu.make_async_copy(p_refs[a], out_refs[a].at[me_id], local_sems.at[a]) for a in range(n)]
        for cp in mine:
            cp.start()
        first = []
        for a in range(n):
            first.append(copy(a, 0, me, sibling, own=True))
            first += [copy(a, 1 + j, me, (*chip, c), own=True) for j, chip in enumerate(chips)]
        for cp in first:
            cp.start()
        for k in range(N_DEV - 1):
            c_copy(k, peers[k][1]).wait_recv()
        prod = b_ref[...] + jnp.zeros((N_DEV, ada_w), F32)
        for j in range(D_MODEL // chunk):
            prod = prod + _dot(cg_ref[:, j, :].astype(BF16), wa_ref[j * chunk:(j + 1) * chunk, :].astype(BF16))
        for k in range(N_DEV):
            pbuf[k] = jnp.broadcast_to(prod[k:k + 1, :], (8, ada_w))

        def m_copy(k, landing):
            return pltpu.make_async_remote_copy(
                src_ref=pbuf.at[peers[k][1]], dst_ref=mod_ref.at[landing], send_sem=m_send.at[k], recv_sem=m_recv.at[k],
                device_id=peers[k][0], device_id_type=MESH_ID)

        mod_ref[me_id] = pbuf[me_id]
        m_sends = [m_copy(k, me_id) for k in range(N_DEV - 1)]
        for cp in m_sends:
            cp.start()

        passed = []
        for j, chip in enumerate(chips):
            for a in range(n):
                copy(a, 1 + j, (*chip, c), me).wait_recv()
                passed.append(copy(a, 4 + j, (*chip, c), sibling))
                passed[-1].start()
        for a in range(n):
            copy(a, 0, sibling, me).wait_recv()
            for j, chip in enumerate(chips):
                copy(a, 4 + j, (*chip, 1 - c), me).wait_recv()
        for k in range(N_DEV - 1):
            m_copy(k, peers[k][1]).wait_recv()
        for cp in first + passed + c_sends + m_sends:
            cp.wait_send()
        for cp in mine:
            cp.wait()

    hbm = pl.BlockSpec(memory_space=pl.ANY)
    vm = pl.BlockSpec(memory_space=pltpu.VMEM)
    sem7 = pltpu.SemaphoreType.DMA((N_DEV - 1,))
    res = pl.pallas_call(
        body, name="gather_front", in_specs=[hbm] * n + [vm, vm, vm], out_specs=[hbm] * n + [vm, vm],
        out_shape=[jax.ShapeDtypeStruct((N_DEV,) + s.shape, s.dtype) for s in shards]
        + [jax.ShapeDtypeStruct((N_DEV,) + c8.shape, F32), jax.ShapeDtypeStruct((N_DEV, 8, ada_w), F32)],
        scratch_shapes=[pltpu.SemaphoreType.DMA((n, N_DEV - 1)), pltpu.SemaphoreType.DMA((n, N_DEV - 1)),
                        pltpu.SemaphoreType.DMA((n,)), sem7, sem7, sem7, sem7, pltpu.VMEM((N_DEV, 8, ada_w), F32)],
        compiler_params=pltpu.CompilerParams(vmem_limit_bytes=VMEM_LIMIT),
    )(*shards, c8, w_ada, b_cols)
    return res[:n], res[n], res[n + 1]


def _adamw(w, g, m, v):
    m2 = ADAM_B1 * m + (1.0 - ADAM_B1) * g
    v2 = ADAM_B2 * v + (1.0 - ADAM_B2) * (g * g)
    m_hat = m2 / (1.0 - ADAM_B1 ** ADAM_STEP)
    v_hat = v2 / (1.0 - ADAM_B2 ** ADAM_STEP)
    delta = -ADAM_LR * (m_hat / (jnp.sqrt(v_hat) + ADAM_EPS) + ADAM_WD * w)
    return delta, m2, v2


def _adamw_w_ada(cmat, dmod_cols, w, m, v):
    r, cdim = w.shape
    tr = _row_tile(r)

    def body(c_ref, d_ref, w_ref, m_ref, v_ref, g_ref, dl_ref, m2_ref, v2_ref):
        g = _dot_tn(c_ref[...].astype(BF16), d_ref[...].astype(BF16))
        g_ref[...] = g
        dl_ref[...], m2_ref[...], v2_ref[...] = _adamw(w_ref[...], g, m_ref[...], v_ref[...])

    row = pl.BlockSpec((tr, cdim), lambda i: (i, 0))
    return pl.pallas_call(
        body, name="adamw_w_ada", grid=(r // tr,),
        in_specs=[pl.BlockSpec((N_DEV, tr), lambda i: (0, i)), _full(dmod_cols.shape), row, row, row],
        out_specs=[row] * 4, out_shape=[jax.ShapeDtypeStruct(w.shape, F32)] * 4,
        compiler_params=_params(("parallel",)),
    )(cmat, dmod_cols, w, m, v)


SMALL_ROWS = 16
SMALL_AT = {"b_ada": (0, 6, 0, D_MODEL), "g_pre_mix": (6, 1, 0, D_MODEL), "g_post_mix": (7, 1, 0, D_MODEL),
            "g_pre_ffn": (8, 1, 0, D_MODEL), "g_post_ffn": (9, 1, 0, D_MODEL), "conv_b": (10, 1, 0, D_MODEL),
            "mlstm_norm_w": (11, 1, 0, 512), "b_if": (11, 1, 512, LANES), "attn_sinks": (11, 1, 640, LANES)}


def _small_table(part):
    tail = jnp.concatenate([part["mlstm_norm_w"], jnp.pad(part["b_if"], (0, LANES - 8)),
                            jnp.pad(part["attn_sinks"], (0, LANES - 8)), jnp.zeros((256,), F32)])
    return jnp.concatenate([part["b_ada"], part["gains"], part["conv_b"][None], tail[None],
                            jnp.full((1, D_MODEL), part["loss"], F32),
                            jnp.zeros((SMALL_ROWS - 13, D_MODEL), F32)], axis=0)


LOSS_ROW = 12


def _adamw_small(gathered, wmv):
    names = list(SMALL_AT)

    def body(*refs):
        g_ref, ins, outs = refs[0], refs[1:1 + 3 * len(names)], refs[1 + 3 * len(names):]
        g = g_ref[0]
        for k in range(1, N_DEV):
            g = g + g_ref[k]
        outs[4 * len(names)][...] = g[LOSS_ROW:LOSS_ROW + 1, 0:LANES]
        for i, n in enumerate(names):
            r0, rows, l0, lanes = SMALL_AT[n]
            gi = jnp.concatenate([g[r:r + 1, l0:l0 + lanes] for r in range(r0, r0 + rows)], axis=1)
            w_ref, m_ref, v_ref = ins[3 * i:3 * i + 3]
            go, dl, m2, v2 = outs[4 * i:4 * i + 4]
            go[...] = gi
            dl[...], m2[...], v2[...] = _adamw(w_ref[...], gi, m_ref[...], v_ref[...])

    flat = [a for n in names for a in wmv[n]]
    res = pl.pallas_call(
        body, name="adamw_small",
        out_shape=[jax.ShapeDtypeStruct(wmv[n][0].shape, F32) for n in names for _ in range(4)]
        + [jax.ShapeDtypeStruct((1, LANES), F32)],
        compiler_params=_params(),
    )(gathered, *flat)
    out = {n: res[4 * i:4 * i + 4] for i, n in enumerate(names)}
    out["loss"] = res[4 * len(names)]
    return out


def _adamw_sum_many(items, name):
    n = len(items)

    def body(*refs):
        ins, outs = refs[:4 * n], refs[4 * n:]
        for i in range(n):
            r_ref, w_ref, m_ref, v_ref = ins[4 * i:4 * i + 4]
            go, dl, m2, v2 = outs[4 * i:4 * i + 4]
            g = _sum_partials(r_ref)
            go[...] = g
            dl[...], m2[...], v2[...] = _adamw(w_ref[...], g, m_ref[...], v_ref[...])

    res = pl.pallas_call(
        body, name=name, out_shape=[jax.ShapeDtypeStruct(it[1].shape, F32) for it in items for _ in range(4)],
        compiler_params=_params(),
    )(*[a for it in items for a in it])
    return [res[4 * i:4 * i + 4] for i in range(n)]


def _row_tile(rows):
    return rows // 4 if rows >= 512 and rows % 64 == 0 else rows


def _sum_partials(r_ref):
    g = r_ref[0].astype(F32)
    for k in range(1, N_DEV):
        g = g + r_ref[k].astype(F32)
    return g


def _adamw_sum(recv, w, m, v, name):
    r, cdim = w.shape
    tr = _row_tile(r)

    def body(r_ref, w_ref, m_ref, v_ref, g_ref, dl_ref, m2_ref, v2_ref):
        g = _sum_partials(r_ref)
        g_ref[...] = g
        dl_ref[...], m2_ref[...], v2_ref[...] = _adamw(w_ref[...], g, m_ref[...], v_ref[...])

    row = pl.BlockSpec((tr, cdim), lambda i: (i, 0))
    return pl.pallas_call(
        body, name=name, grid=(r // tr,),
        in_specs=[pl.BlockSpec((N_DEV, tr, cdim), lambda i: (0, i, 0)), row, row, row],
        out_specs=[row] * 4, out_shape=[jax.ShapeDtypeStruct((r, cdim), F32)] * 4,
        compiler_params=_params(("parallel",)),
    )(recv, w, m, v)


def _sum8(recv, name):
    _, r, cdim = recv.shape
    tr = _row_tile(r)

    def body(r_ref, g_ref):
        g_ref[...] = _sum_partials(r_ref)

    return pl.pallas_call(
        body, name=name, grid=(r // tr,),
        in_specs=[pl.BlockSpec((N_DEV, tr, cdim), lambda i: (0, i, 0))],
        out_specs=pl.BlockSpec((tr, cdim), lambda i: (i, 0)), out_shape=jax.ShapeDtypeStruct((r, cdim), F32),
        compiler_params=_params(("parallel",)),
    )(recv)


def _adamw_plain(g, w, m, v, name):
    r, cdim = w.shape
    tr = _row_tile(r)

    def body(g_ref, w_ref, m_ref, v_ref, dl_ref, m2_ref, v2_ref):
        dl_ref[...], m2_ref[...], v2_ref[...] = _adamw(w_ref[...], g_ref[...], m_ref[...], v_ref[...])

    row = pl.BlockSpec((tr, cdim), lambda i: (i, 0))
    return pl.pallas_call(
        body, name=name, grid=(r // tr,), in_specs=[row] * 4, out_specs=[row] * 3,
        out_shape=[jax.ShapeDtypeStruct((r, cdim), F32)] * 3,
        compiler_params=_params(("parallel",)),
    )(g, w, m, v)


IN_SHARD = 609
IN_SHARD_PAD = 640
IF_AT = A_W + M_W


def _regrouped(u):
    return u if u < IF_AT + 8 else u + (IF_W - 8)


def _selection(k, rows, row0, transpose):
    shape = (rows, IN_SHARD_PAD) if transpose else (IN_SHARD_PAD, rows)
    l = lax.broadcasted_iota(jnp.int32, shape, 1 if transpose else 0)
    r = lax.broadcasted_iota(jnp.int32, shape, 0 if transpose else 1) + row0
    u = l + IN_SHARD * k
    ru = u + jnp.where(u >= IF_AT + 8, IF_W - 8, 0)
    return ((ru == r) & (l < IN_SHARD)).astype(BF16)


def _regroup_w_in(g):
    def body(g_ref, o_ref):
        for cb in range(CAT_W // LANES):
            r0 = cb * LANES
            acc = jnp.zeros((D_MODEL, LANES), F32)
            for k in range(N_DEV):
                lo, hi = _regrouped(IN_SHARD * k), _regrouped(IN_SHARD * k + IN_SHARD - 1)
                if hi >= r0 and lo < r0 + LANES:
                    acc = acc + _dot(g_ref[k], _selection(k, LANES, r0, False))
            o_ref[:, r0:r0 + LANES] = acc.astype(BF16)

    return pl.pallas_call(
        body, name="regroup_w_in", out_shape=jax.ShapeDtypeStruct((D_MODEL, CAT_W), BF16),
        compiler_params=_params(),
    )(g)


def _ungroup_w_in(g_parts):
    n = len(g_parts)

    def body(*refs):
        o_ref, g_ref = refs[n], refs[n + 1]
        at = 0
        for p in refs[:n]:
            g_ref[:, at:at + p.shape[1]] = p[...]
            at += p.shape[1]
        for k in range(N_DEV):
            lo, hi = _regrouped(IN_SHARD * k), _regrouped(IN_SHARD * k + IN_SHARD - 1)
            w0, w1 = lo // LANES * LANES, (hi // LANES + 1) * LANES
            o_ref[k] = _dot(g_ref[:, w0:w1], _selection(k, w1 - w0, w0, True)).astype(BF16)

    return pl.pallas_call(
        body, name="ungroup_w_in", out_shape=jax.ShapeDtypeStruct((N_DEV, D_MODEL, IN_SHARD_PAD), BF16),
        scratch_shapes=[pltpu.VMEM((D_MODEL, CAT_W), BF16)], compiler_params=_params(),
    )(*g_parts)


WEIGHT_NAMES = ("w_ada", "b_ada", "g_pre_mix", "g_post_mix", "w_in", "b_if", "conv_w", "conv_b", "attn_sinks",
                "mlstm_norm_w", "w_branch_attn", "w_branch_mlstm", "w_out", "g_pre_ffn", "g_post_ffn",
                "w_ffn_gate", "w_ffn_up", "w_ffn_down")


def kernel(x, c, positions, w_ada, b_ada, g_pre_mix, g_post_mix, w_in, b_if, conv_w, conv_b, attn_sinks, mlstm_norm_w, w_branch_attn, w_branch_mlstm, w_out, g_pre_ffn, g_post_ffn, w_ffn_gate, w_ffn_up, w_ffn_down, loss_target, m_w_ada, m_b_ada, m_g_pre_mix, m_g_post_mix, m_w_in, m_b_if, m_conv_w, m_conv_b, m_attn_sinks, m_mlstm_norm_w, m_w_branch_attn, m_w_branch_mlstm, m_w_out, m_g_pre_ffn, m_g_post_ffn, m_w_ffn_gate, m_w_ffn_up, m_w_ffn_down, v_w_ada, v_b_ada, v_g_pre_mix, v_g_post_mix, v_w_in, v_b_if, v_conv_w, v_conv_b, v_attn_sinks, v_mlstm_norm_w, v_w_branch_attn, v_w_branch_mlstm, v_w_out, v_g_pre_ffn, v_g_post_ffn, v_w_ffn_gate, v_w_ffn_up, v_w_ffn_down):
    given = dict(locals())
    W = {n: given[n][0] for n in WEIGHT_NAMES}
    M = {n: given["m_" + n][0] for n in WEIGHT_NAMES}
    V = {n: given["v_" + n][0] for n in WEIGHT_NAMES}
    me = 4 * lax.axis_index("x") + 2 * lax.axis_index("y") + lax.axis_index("c")

    ff_sh = D_FF // N_DEV
    ada_w = D_MODEL * 6 // N_DEV
    b_cols = lax.dynamic_slice(W["b_ada"], (me * ada_w,), (ada_w,)).reshape(1, ada_w)
    (g_in, g_conv), cg, mod_recv = _gather_front(
        [jnp.pad(W["w_in"], ((0, 0), (0, IN_SHARD_PAD - IN_SHARD))).astype(BF16), jnp.pad(W["conv_w"], ((0, 4), (0, 0)))],
        c.reshape(8, D_MODEL // 8), W["w_ada"], b_cols)
    cmat = cg.reshape(N_DEV, D_MODEL)
    mod = mod_recv[:, 0, :].reshape(6, D_MODEL)

    st_b = _exchange_start([W["w_branch_attn"].astype(BF16), W["w_branch_mlstm"].astype(BF16), W["w_out"].astype(BF16),
                            W["w_ffn_gate"].T.astype(BF16), W["w_ffn_up"].T.astype(BF16), W["w_ffn_down"].astype(BF16)],
                           True, mod_recv, "gather_rest_start")
    cols = lambda g: g.transpose(1, 0, 2).reshape(g.shape[1], N_DEV * g.shape[2])
    gains = jnp.stack([W["g_pre_mix"], W["g_post_mix"], W["g_pre_ffn"], W["g_post_ffn"]])
    xs, tgt = x[0], loss_target[0]
    t = _tables(mod, gains, cols(g_conv)[0:4], W["conv_b"], W["b_if"], W["mlstm_norm_w"], positions[0])
    vecs = t["vecs"]
    t["vecs"] = _tie(vecs, st_b["token"], "tie_fwd")
    w_cat = _regroup_w_in(g_in)
    a = _mixer_fwd(xs, t, W["attn_sinks"], w_cat)
    g_ba, g_bm, g_out, g_gate, g_up, g_down = _exchange_wait(st_b, a["ym"], "gather_rest_wait")
    w_ba, w_bm, w_out = cols(g_ba), cols(g_bm), g_out.reshape(D_MODEL, D_MODEL)
    b = _ffn_part(xs, tgt, t, a, w_ba, w_bm, w_out, g_gate.reshape(D_FF, D_MODEL), g_up.reshape(D_FF, D_MODEL),
                  g_down.reshape(D_FF, D_MODEL))

    st_f = _exchange_start([b["g_w_gate"].reshape(N_DEV, ff_sh, D_MODEL), b["g_w_up"].reshape(N_DEV, ff_sh, D_MODEL),
                            b["g_w_down"].reshape(N_DEV, ff_sh, D_MODEL)], False, b["dx1"], "scatter_ffn_start")
    t["vecs"] = _tie(vecs, st_f["token"], "tie_bwd")
    cm = _mixer_bwd(b["dx1"], t, a, b, W["attn_sinks"], w_ba, w_bm, w_out)
    pieces = lambda g, n: g.reshape(g.shape[0], N_DEV, n).transpose(1, 0, 2)
    st_m = _exchange_start([_ungroup_w_in(cm["g_w_cat"]), pieces(cm["g_w_ba"], 128), pieces(cm["g_w_bm"], 128),
                            cm["g_w_out"].reshape(N_DEV, D_MODEL // N_DEV, D_MODEL),
                            jnp.pad(pieces(cm["dcw"][0:4], 128), ((0, 0), (0, 4), (0, 0)))], False, cm["dcw"],
                           "scatter_mixer_start")
    r_gate, r_up, r_down = _exchange_wait(st_f, st_m["token"], "scatter_ffn_wait")
    grad_x, acc_p = _pre_bwd(cm["dproj"], xs, b["dx1"], _tie(vecs, st_m["token"], "tie_pre_bwd"), w_cat)
    small = _small_grads(acc_p, b, cm)
    loss = b["loss"]

    big_out = [{} for _ in range(4)]

    def put(n, res):
        for k in range(4):
            big_out[k][n] = res[k][None]

    put("w_ffn_down", _adamw_sum(r_down, W["w_ffn_down"], M["w_ffn_down"], V["w_ffn_down"], "adamw_w_ffn_down"))
    for n, r in (("w_ffn_gate", r_gate), ("w_ffn_up", r_up)):
        put(n, [o.T for o in _adamw_sum(r, W[n].T, M[n].T, V[n].T, "adamw_" + n)])

    sg = _small_exchange(_small_table({"b_ada": small["dmod"], "gains": small["dgains"], "conv_b": small["dconv_b"],
                                       "mlstm_norm_w": small["dnorm_w"], "b_if": small["db_if"],
                                       "attn_sinks": small["dsinks"], "loss": loss}), True, "gather_small")
    as_row = lambda a, n: jnp.pad(a, (0, SMALL_AT[n][3] * SMALL_AT[n][1] - a.shape[0]))[None]
    small_res = _adamw_small(sg, {n: [as_row(d[n], n) for d in (W, M, V)] for n in SMALL_AT})
    small_out = [{n: small_res[n][k][:, 0:W[n].shape[0]] for n in SMALL_AT} for k in range(4)]
    dmod_cols = lax.dynamic_slice(sg[:, 0:6, :].reshape(N_DEV, 6 * D_MODEL), (0, me * ada_w), (N_DEV, ada_w))
    ada_out = _adamw_w_ada(cmat, dmod_cols, W["w_ada"], M["w_ada"], V["w_ada"])

    r_in, r_ba, r_bm, r_out, r_conv = _exchange_wait(st_m, ada_out[1], "scatter_mixer_wait")
    pad4 = lambda v: jnp.pad(v, ((0, 4), (0, 0)))
    res_ba, res_bm, res_out, res_conv = _adamw_sum_many(
        [(r, W[n], M[n], V[n]) for n, r in (("w_branch_attn", r_ba), ("w_branch_mlstm", r_bm), ("w_out", r_out))]
        + [(r_conv, pad4(W["conv_w"]), pad4(M["conv_w"]), pad4(V["conv_w"]))], "adamw_mixer_small")
    put("w_branch_attn", res_ba)
    put("w_branch_mlstm", res_bm)
    put("w_out", res_out)
    put("conv_w", [o[0:4] for o in res_conv])
    g = _sum8(r_in, "sum_w_in")[:, 0:IN_SHARD].T
    put("w_in", [o.T for o in [g] + list(_adamw_plain(g, W["w_in"].T, M["w_in"].T, V["w_in"].T, "adamw_w_in"))])

    outs = [small_res["loss"][0, 0], grad_x[None]]
    for k in range(4):
        for n in WEIGHT_NAMES:
            if n == "w_ada":
                outs.append(ada_out[k][None])
            elif n in big_out[k]:
                outs.append(big_out[k][n])
            else:
                outs.append(small_out[k][n])
    return tuple(outs)
```

```python
import jax
import jax.numpy as jnp
from jax import lax
from jax.experimental import pallas as pl
from jax.experimental.pallas import tpu as pltpu

F32 = jnp.float32
BF16 = jnp.bfloat16

N_DEV = 8
D_MODEL = 1024
D_FF = 2816
N_Q_HEADS = 8
HEAD_DIM = 64
ATTN_BLOCK = 128
ROPE_THETA = 10000.0
MLSTM_HEADS = 4
MLSTM_HEAD_DIM = 128
MLSTM_CHUNK = 128
NORM_EPS = 1e-6
ADAM_LR = 0.001
ADAM_B1 = 0.9
ADAM_B2 = 0.999
ADAM_EPS = 1e-08
ADAM_WD = 0.01
ADAM_STEP = 10

ROW_TILE = 256
WIDE_TILE = 512
LANES = 128
NEG = -1e30
VMEM_LIMIT = 56 * 1024 * 1024

A_W = 768
M_W = 2048
IF_W = 128
G_W = 2048
CAT_W = A_W + M_W + IF_W + G_W

R_SHIFT_M, R_SCALE_M, R_GATE_M, R_SHIFT_F, R_SCALE_F, R_GATE_F = 0, 1, 2, 3, 4, 5
R_G_PRE_MIX, R_G_POST_MIX, R_G_PRE_FFN, R_G_POST_FFN = 6, 7, 8, 9


def _dot(a, b):
    return jnp.dot(a, b, preferred_element_type=F32)


def _dot_nt(a, b):
    return lax.dot_general(a, b, (((1,), (1,)), ((), ())), preferred_element_type=F32)


def _dot_tn(a, b):
    return lax.dot_general(a, b, (((0,), (0,)), ((), ())), preferred_element_type=F32)


def _recip(x):
    return 1.0 / x


def _sigmoid(x):
    return _recip(1.0 + jnp.exp(-x))


def _colsum(x):
    return jnp.sum(x, axis=0, keepdims=True)


def _rowmean(x):
    return jnp.mean(x, axis=-1, keepdims=True)


def _params(sem=None, vmem=VMEM_LIMIT):
    kw = dict(vmem_limit_bytes=vmem)
    if sem is not None:
        kw["dimension_semantics"] = sem
    return pltpu.CompilerParams(**kw)


def _full(shape):
    nd = len(shape)
    return pl.BlockSpec(shape, lambda *_: (0,) * nd)


def _pre_proj(x, vecs, w_cat):
    S = x.shape[0]
    tm = WIDE_TILE

    def body(x_ref, v_ref, w_ref, h_ref, pa_ref, pm_ref, pif_ref, pg_ref):
        xv = x_ref[...]
        r = lax.rsqrt(_rowmean(xv * xv) + NORM_EPS)
        h = (xv * r * v_ref[R_G_PRE_MIX:R_G_PRE_MIX + 1, :]) * (1.0 + v_ref[R_SCALE_M:R_SCALE_M + 1, :]) \
            + v_ref[R_SHIFT_M:R_SHIFT_M + 1, :]
        hb = h.astype(BF16)
        h_ref[...] = hb
        pa_ref[...] = _dot(hb, w_ref[:, 0:A_W])
        pm_ref[...] = _dot(hb, w_ref[:, A_W:A_W + M_W])
        pif_ref[...] = _dot(hb, w_ref[:, A_W + M_W:A_W + M_W + IF_W])
        pg_ref[...] = _dot(hb, w_ref[:, A_W + M_W + IF_W:CAT_W]).astype(BF16)

    row = lambda w: pl.BlockSpec((tm, w), lambda i: (i, 0))
    return pl.pallas_call(
        body, name="pre_proj", grid=(S // tm,),
        in_specs=[row(D_MODEL), _full(vecs.shape), _full(w_cat.shape)],
        out_specs=[row(D_MODEL), row(A_W), row(M_W), row(IF_W), row(G_W)],
        out_shape=[jax.ShapeDtypeStruct((S, D_MODEL), BF16), jax.ShapeDtypeStruct((S, A_W), F32),
                   jax.ShapeDtypeStruct((S, M_W), F32), jax.ShapeDtypeStruct((S, IF_W), F32),
                   jax.ShapeDtypeStruct((S, G_W), BF16)],
        compiler_params=_params(("parallel",)),
    )(x, vecs, w_cat)


def _mix_fwd(x, ya, ym, pg, vecs, w_ba, w_bm, w_out):
    S = x.shape[0]
    tm = WIDE_TILE

    def body(x_ref, ya_ref, ym_ref, pg_ref, v_ref, wba_ref, wbm_ref, wout_ref,
             x1_ref, merged_ref, mix_ref, pa_ref, pb_ref):
        pa = _dot(ya_ref[...], wba_ref[...])
        pb = _dot(ym_ref[...], wbm_ref[...])
        merged = _sigmoid(pg_ref[:, 0:D_MODEL].astype(F32)) * pa + _sigmoid(pg_ref[:, D_MODEL:G_W].astype(F32)) * pb
        mb = merged.astype(BF16)
        mix = _dot(mb, wout_ref[...])
        r = lax.rsqrt(_rowmean(mix * mix) + NORM_EPS)
        x1_ref[...] = x_ref[...] + v_ref[R_GATE_M:R_GATE_M + 1, :] * (mix * r * v_ref[R_G_POST_MIX:R_G_POST_MIX + 1, :])
        merged_ref[...] = mb
        mix_ref[...] = mix.astype(BF16)
        pa_ref[...] = pa.astype(BF16)
        pb_ref[...] = pb.astype(BF16)

    row = lambda w: pl.BlockSpec((tm, w), lambda i: (i, 0))
    sd = lambda w, dt: jax.ShapeDtypeStruct((S, w), dt)
    return pl.pallas_call(
        body, name="mix_fwd", grid=(S // tm,),
        in_specs=[row(D_MODEL), row(512), row(512), row(G_W), _full(vecs.shape), _full(w_ba.shape),
                  _full(w_bm.shape), _full(w_out.shape)],
        out_specs=[row(D_MODEL)] * 5,
        out_shape=[sd(D_MODEL, F32), sd(D_MODEL, BF16), sd(D_MODEL, BF16), sd(D_MODEL, BF16), sd(D_MODEL, BF16)],
        compiler_params=_params(("parallel",)),
    )(x, ya, ym, pg, vecs, w_ba, w_bm, w_out)


def _ffn_fwd_bwd(x1, tgt, vecs, w_gate, w_up, w_down):
    S = x1.shape[0]
    tm = ROW_TILE

    def body(x1_ref, tgt_ref, v_ref, wg_hbm, wu_hbm, wd_hbm,
             dx1_ref, h2_ref, hid_ref, da_ref, du_ref, dff_ref, acc_ref, loss_ref,
             wg, wu, wd, sem):
        i = pl.program_id(0)

        @pl.when(i == 0)
        def _():
            cps = [pltpu.make_async_copy(wg_hbm, wg, sem.at[0]), pltpu.make_async_copy(wu_hbm, wu, sem.at[1]),
                   pltpu.make_async_copy(wd_hbm, wd, sem.at[2])]
            for cp in cps:
                cp.start()
            for cp in cps:
                cp.wait()
            acc_ref[...] = jnp.zeros_like(acc_ref)
            loss_ref[...] = jnp.zeros_like(loss_ref)

        vrow = lambda r: v_ref[r:r + 1, :]
        x1v = x1_ref[...]
        r3 = lax.rsqrt(_rowmean(x1v * x1v) + NORM_EPS)
        x1hat = x1v * r3
        xn3 = x1hat * vrow(R_G_PRE_FFN)
        h2b = (xn3 * (1.0 + vrow(R_SCALE_F)) + vrow(R_SHIFT_F)).astype(BF16)
        h2_ref[...] = h2b
        a = _dot_nt(h2b, wg[...])
        u = _dot_nt(h2b, wu[...])
        sg = _sigmoid(a)
        sil = a * sg
        hidb = (sil * u).astype(BF16)
        hid_ref[...] = hidb
        ff = _dot(hidb, wd[...])
        r4 = lax.rsqrt(_rowmean(ff * ff) + NORM_EPS)
        ffhat = ff * r4
        n4 = ffhat * vrow(R_G_POST_FFN)
        err = x1v + vrow(R_GATE_F) * n4 - tgt_ref[...]
        loss_ref[...] += jnp.sum(err * err) * (0.5 / D_MODEL)
        dy = err * (1.0 / D_MODEL)
        acc_ref[0:1, :] += _colsum(dy * n4)
        dn4 = dy * vrow(R_GATE_F)
        acc_ref[1:2, :] += _colsum(dn4 * ffhat)
        dffhat = dn4 * vrow(R_G_POST_FFN)
        dffb = (r4 * (dffhat - ffhat * _rowmean(dffhat * ffhat))).astype(BF16)
        dff_ref[...] = dffb
        dhid = _dot_nt(dffb, wd[...])
        dub = (dhid * sil).astype(BF16)
        dab = (dhid * u * (sg * (1.0 + a * (1.0 - sg)))).astype(BF16)
        da_ref[...] = dab
        du_ref[...] = dub
        dh2 = _dot(dab, wg[...]) + _dot(dub, wu[...])
        acc_ref[2:3, :] += _colsum(dh2 * xn3)
        acc_ref[3:4, :] += _colsum(dh2)
        dxn3 = dh2 * (1.0 + vrow(R_SCALE_F))
        acc_ref[4:5, :] += _colsum(dxn3 * x1hat)
        dx1hat = dxn3 * vrow(R_G_PRE_FFN)
        dx1_ref[...] = dy + r3 * (dx1hat - x1hat * _rowmean(dx1hat * x1hat))

    row = lambda w: pl.BlockSpec((tm, w), lambda i: (i, 0))
    sd = lambda w, dt: jax.ShapeDtypeStruct((S, w), dt)
    anyspec = pl.BlockSpec(memory_space=pl.ANY)
    return pl.pallas_call(
        body, name="ffn_fwd_bwd", grid=(S // tm,),
        in_specs=[row(D_MODEL), row(D_MODEL), _full(vecs.shape), anyspec, anyspec, anyspec],
        out_specs=[row(D_MODEL), row(D_MODEL), row(D_FF), row(D_FF), row(D_FF), row(D_MODEL),
                   _full((8, D_MODEL)), _full((8, LANES))],
        out_shape=[sd(D_MODEL, F32), sd(D_MODEL, BF16), sd(D_FF, BF16), sd(D_FF, BF16), sd(D_FF, BF16),
                   sd(D_MODEL, BF16), jax.ShapeDtypeStruct((8, D_MODEL), F32), jax.ShapeDtypeStruct((8, LANES), F32)],
        scratch_shapes=[pltpu.VMEM(w_gate.shape, BF16), pltpu.VMEM(w_up.shape, BF16), pltpu.VMEM(w_down.shape, BF16),
                        pltpu.SemaphoreType.DMA((3,))],
        compiler_params=_params(("arbitrary",)),
    )(x1, tgt, vecs, w_gate, w_up, w_down)


def _mix_bwd(dx1, mix, pa, pb, pg, vecs, w_ba, w_bm, w_out):
    S = dx1.shape[0]
    tm = WIDE_TILE

    def body(dx1_ref, mix_ref, pa_ref, pb_ref, pg_ref, v_ref, wba_ref, wbm_ref, wout_ref,
             dmix_ref, dpa_ref, dpb_ref, dg_ref, dya_ref, dym_ref, acc_ref):
        i = pl.program_id(0)

        @pl.when(i == 0)
        def _():
            acc_ref[...] = jnp.zeros_like(acc_ref)

        vrow = lambda r: v_ref[r:r + 1, :]
        dx1v = dx1_ref[...]
        mix = mix_ref[...].astype(F32)
        r2 = lax.rsqrt(_rowmean(mix * mix) + NORM_EPS)
        mixhat = mix * r2
        acc_ref[0:1, :] += _colsum(dx1v * (mixhat * vrow(R_G_POST_MIX)))
        dn2 = dx1v * vrow(R_GATE_M)
        acc_ref[1:2, :] += _colsum(dn2 * mixhat)
        dmixhat = dn2 * vrow(R_G_POST_MIX)
        dmixb = (r2 * (dmixhat - mixhat * _rowmean(dmixhat * mixhat))).astype(BF16)
        dmix_ref[...] = dmixb
        dmerged = _dot_nt(dmixb, wout_ref[...])
        sa = _sigmoid(pg_ref[:, 0:D_MODEL].astype(F32))
        sm = _sigmoid(pg_ref[:, D_MODEL:G_W].astype(F32))
        dpab = (dmerged * sa).astype(BF16)
        dpbb = (dmerged * sm).astype(BF16)
        dpa_ref[...] = dpab
        dpb_ref[...] = dpbb
        dg_ref[:, 0:D_MODEL] = (dmerged * pa_ref[...].astype(F32) * (sa * (1.0 - sa))).astype(BF16)
        dg_ref[:, D_MODEL:G_W] = (dmerged * pb_ref[...].astype(F32) * (sm * (1.0 - sm))).astype(BF16)
        dya_ref[...] = _dot_nt(dpab, wba_ref[...]).astype(BF16)
        dym_ref[...] = _dot_nt(dpbb, wbm_ref[...])

    row = lambda w: pl.BlockSpec((tm, w), lambda i: (i, 0))
    sd = lambda w, dt: jax.ShapeDtypeStruct((S, w), dt)
    return pl.pallas_call(
        body, name="mix_bwd", grid=(S // tm,),
        in_specs=[row(D_MODEL), row(D_MODEL), row(D_MODEL), row(D_MODEL), row(G_W), _full(vecs.shape),
                  _full(w_ba.shape), _full(w_bm.shape), _full(w_out.shape)],
        out_specs=[row(D_MODEL), row(D_MODEL), row(D_MODEL), row(G_W), row(512), row(512), _full((8, D_MODEL))],
        out_shape=[sd(D_MODEL, BF16), sd(D_MODEL, BF16), sd(D_MODEL, BF16), sd(G_W, BF16), sd(512, BF16), sd(512, F32),
                   jax.ShapeDtypeStruct((8, D_MODEL), F32)],
        compiler_params=_params(("arbitrary",)),
    )(dx1, mix, pa, pb, pg, vecs, w_ba, w_bm, w_out)


def _pre_bwd(pieces, x, dx1, vecs, w_cat):
    S = x.shape[0]
    tm = WIDE_TILE
    n = len(pieces)
    starts = [sum(p.shape[1] for p in pieces[:k]) for k in range(n + 1)]

    def body(*refs):
        p_refs = refs[:n]
        x_ref, dx1_ref, v_ref, w_ref, dx_ref, acc_ref = refs[n:]
        i = pl.program_id(0)

        @pl.when(i == 0)
        def _():
            acc_ref[...] = jnp.zeros_like(acc_ref)

        vrow = lambda r: v_ref[r:r + 1, :]
        dh = _dot_nt(p_refs[0][...], w_ref[:, starts[0]:starts[1]])
        for k in range(1, n):
            dh = dh + _dot_nt(p_refs[k][...], w_ref[:, starts[k]:starts[k + 1]])
        xv = x_ref[...]
        r1 = lax.rsqrt(_rowmean(xv * xv) + NORM_EPS)
        xhat = xv * r1
        acc_ref[0:1, :] += _colsum(dh * (xhat * vrow(R_G_PRE_MIX)))
        acc_ref[1:2, :] += _colsum(dh)
        dxn = dh * (1.0 + vrow(R_SCALE_M))
        acc_ref[2:3, :] += _colsum(dxn * xhat)
        dxhat = dxn * vrow(R_G_PRE_MIX)
        dx_ref[...] = dx1_ref[...] + r1 * (dxhat - xhat * _rowmean(dxhat * xhat))

    row = lambda w: pl.BlockSpec((tm, w), lambda i: (i, 0))
    return pl.pallas_call(
        body, name="pre_bwd", grid=(S // tm,),
        in_specs=[row(p.shape[1]) for p in pieces] + [row(D_MODEL), row(D_MODEL), _full(vecs.shape), _full(w_cat.shape)],
        out_specs=[row(D_MODEL), _full((8, D_MODEL))],
        out_shape=[jax.ShapeDtypeStruct((S, D_MODEL), F32), jax.ShapeDtypeStruct((8, D_MODEL), F32)],
        compiler_params=_params(("arbitrary",)),
    )(*pieces, x, dx1, vecs, w_cat)


def _matmul_tn(a, b, tn, name, ts=1024):
    S, K = a.shape
    N = b.shape[1]
    n_s = S // ts

    def body(a_ref, b_ref, o_ref, acc_ref):
        s = pl.program_id(1)

        @pl.when(s == 0)
        def _():
            acc_ref[...] = jnp.zeros_like(acc_ref)

        acc_ref[...] += _dot_tn(a_ref[...], b_ref[...])

        @pl.when(s == n_s - 1)
        def _():
            o_ref[...] = acc_ref[...].astype(BF16)

    return pl.pallas_call(
        body, name=name, grid=(N // tn, n_s),
        in_specs=[pl.BlockSpec((ts, K), lambda j, s: (s, 0)), pl.BlockSpec((ts, tn), lambda j, s: (s, j))],
        out_specs=pl.BlockSpec((K, tn), lambda j, s: (0, j)),
        out_shape=jax.ShapeDtypeStruct((K, N), BF16),
        scratch_shapes=[pltpu.VMEM((K, tn), F32)],
        compiler_params=_params(("parallel", "arbitrary")),
    )(a, b)


def _rope_swap(t):
    lane = lax.broadcasted_iota(jnp.int32, t.shape, 1)
    first = (lane & (HEAD_DIM - 1)) < (HEAD_DIM // 2)
    return jnp.where(first, pltpu.roll(t, LANES - HEAD_DIM // 2, 1), pltpu.roll(t, HEAD_DIM // 2, 1))


def _rope(t, cos, sin_signed):
    return t * cos + _rope_swap(t) * sin_signed


def _rope_t(d, cos, sin_signed):
    return d * cos + _rope_swap(d * sin_signed)


def _to_kv_lanes(chunk, p, h):
    lane = lax.broadcasted_iota(jnp.int32, chunk.shape, 1)
    src = chunk if p == h else pltpu.roll(chunk, HEAD_DIM, 1)
    return jnp.where((lane >> 6) == h, src, jnp.zeros_like(src))


def _from_kv_lanes(o_a, o_b, h):
    lane = lax.broadcasted_iota(jnp.int32, o_a.shape, 1)
    a = o_a if h == 0 else pltpu.roll(o_a, HEAD_DIM, 1)
    b = o_b if h == 1 else pltpu.roll(o_b, HEAD_DIM, 1)
    return jnp.where(lane < HEAD_DIM, a, b)


def _band_bias(n):
    blk = ATTN_BLOCK
    qi = lax.broadcasted_iota(jnp.int32, (blk, 2 * blk), 0)
    kj = lax.broadcasted_iota(jnp.int32, (blk, 2 * blk), 1)
    seen = (kj > qi) & (kj <= qi + blk) & ((n > 0) | (kj >= blk))
    return jnp.concatenate([jnp.where(seen, 0.0, NEG)] * N_Q_HEADS, axis=0)


def _stack_heads(chunks, h, dtype):
    parts = []
    for g in range(4):
        j = 4 * h + g
        parts.append(_to_kv_lanes(chunks[j // 2], j % 2, h))
    return jnp.concatenate(parts, axis=0).astype(dtype)


def _fused_call(parts, name, n_steps):
    counts = [(len(p["in_specs"]), len(p["out_specs"]), len(p["scratch"])) for p in parts]
    n_in, n_out = sum(c[0] for c in counts), sum(c[1] for c in counts)

    def kernel_fn(*refs):
        i = pl.program_id(0)
        groups, a, b, c = [], 0, n_in, n_in + n_out
        for ci, co, cs in counts:
            groups.append(refs[a:a + ci] + refs[b:b + co] + refs[c:c + cs])
            a, b, c = a + ci, b + co, c + cs
        for p, g in zip(parts, groups):
            p["init"](i, *g)
        for p, g in zip(parts, groups):
            p["body"](i, *g)

    flat = lambda key: [v for p in parts for v in p[key]]
    res = pl.pallas_call(
        kernel_fn, name=name, grid=(n_steps,), in_specs=flat("in_specs"), out_specs=flat("out_specs"),
        out_shape=flat("out_shape"), scratch_shapes=flat("scratch"), compiler_params=_params(("arbitrary",)),
    )(*flat("operands"))
    out, pos = [], 0
    for _, co, _ in counts:
        out.append(res[pos:pos + co])
        pos += co
    return out


def _attn_fwd_part(pa, cos, sin, sinks):
    S = pa.shape[0]
    blk = ATTN_BLOCK
    nb = S // blk

    def body(n, sink_ref, cur_ref, prev_ref, cos_ref, sin_ref, cosp_ref, sinp_ref,
             ya_ref, qr_ref, kr_ref, vb_ref, lse_ref):
        cos_c, sin_c = cos_ref[...], sin_ref[...]
        qch = [_rope(cur_ref[:, c * LANES:(c + 1) * LANES], cos_c, sin_c) * (HEAD_DIM ** -0.5) for c in range(4)]
        for c in range(4):
            qr_ref[:, c * LANES:(c + 1) * LANES] = qch[c].astype(BF16)
        k_cur = _rope(cur_ref[:, 512:640], cos_c, sin_c).astype(BF16)
        k_prev = _rope(prev_ref[:, 0:LANES], cosp_ref[...], sinp_ref[...]).astype(BF16)
        v_cur = cur_ref[:, 640:768].astype(BF16)
        v_prev = prev_ref[:, LANES:2 * LANES].astype(BF16)
        kr_ref[...] = k_cur
        vb_ref[...] = v_cur
        K = jnp.concatenate([k_prev, k_cur], axis=0)
        V = jnp.concatenate([v_prev, v_cur], axis=0)
        lane = lax.broadcasted_iota(jnp.int32, (blk, LANES), 1)
        s = jnp.concatenate([_dot_nt(_stack_heads(qch, h, BF16), K) for h in range(2)], axis=0)
        s = s + _band_bias(n)
        rowmax = jnp.max(s, axis=1, keepdims=True)
        hd = lambda x, j: x[j * blk:(j + 1) * blk]
        m = jnp.concatenate([jnp.maximum(hd(rowmax, j), sink_ref[j]) for j in range(N_Q_HEADS)], axis=0)
        p = jnp.exp(s - m)
        pb = p.astype(BF16)
        den = _rowsum_mxu(pb) + jnp.concatenate([jnp.exp(sink_ref[j] - hd(m, j)) for j in range(N_Q_HEADS)], axis=0)
        o = jnp.concatenate([_dot(pb[4 * h * blk:4 * (h + 1) * blk], V) for h in range(2)], axis=0) * _recip(den)
        lse = m + jnp.log(den)
        outs = [o[j * blk:(j + 1) * blk, :] for j in range(N_Q_HEADS)]
        lse_tile = jnp.zeros((blk, LANES), F32)
        for j in range(N_Q_HEADS):
            lse_tile = jnp.where(lane == j, lse[j * blk:(j + 1) * blk, :], lse_tile)
        for c in range(4):
            ya_ref[:, c * LANES:(c + 1) * LANES] = _from_kv_lanes(outs[2 * c], outs[2 * c + 1], c // 2).astype(BF16)
        lse_ref[...] = lse_tile

    prev = lambda n: jnp.maximum(n - 1, 0)
    sd = lambda w, dt: jax.ShapeDtypeStruct((S, w), dt)
    return dict(
        init=lambda n, *refs: None, body=body, scratch=[], operands=[sinks, pa, pa, cos, sin, cos, sin],
        in_specs=[pl.BlockSpec(memory_space=pltpu.SMEM),
                  pl.BlockSpec((blk, A_W), lambda n: (n, 0)),
                  pl.BlockSpec((blk, 256), lambda n: (prev(n), 2)),
                  pl.BlockSpec((blk, LANES), lambda n: (n, 0)), pl.BlockSpec((blk, LANES), lambda n: (n, 0)),
                  pl.BlockSpec((blk, LANES), lambda n: (prev(n), 0)), pl.BlockSpec((blk, LANES), lambda n: (prev(n), 0))],
        out_specs=[pl.BlockSpec((blk, 512), lambda n: (n, 0)), pl.BlockSpec((blk, 512), lambda n: (n, 0)),
                   pl.BlockSpec((blk, LANES), lambda n: (n, 0)), pl.BlockSpec((blk, LANES), lambda n: (n, 0)),
                   pl.BlockSpec((blk, LANES), lambda n: (n, 0))],
        out_shape=[sd(512, BF16), sd(512, BF16), sd(LANES, BF16), sd(LANES, BF16), sd(LANES, F32)])


def _attn_bwd_part(dya, qr, kr, vb, lse, cos, sin, sinks):
    S = dya.shape[0]
    blk = ATTN_BLOCK
    nb = S // blk

    def init(n, sink_ref, dya_ref, qr_ref, kc_ref, kp_ref, vc_ref, vp_ref, lse_ref, cos_ref, sin_ref, cosp_ref, sinp_ref,
             dqkv_ref, last_ref, dsink_ref, ck, cv, cq):
        @pl.when(n == 0)
        def _():
            ck[...] = jnp.zeros_like(ck)
            cv[...] = jnp.zeros_like(cv)
            cq[...] = jnp.zeros_like(cq)
            dsink_ref[...] = jnp.zeros_like(dsink_ref)

    def body(n, sink_ref, dya_ref, qr_ref, kc_ref, kp_ref, vc_ref, vp_ref, lse_ref, cos_ref, sin_ref, cosp_ref, sinp_ref,
             dqkv_ref, last_ref, dsink_ref, ck, cv, cq):
        K = jnp.concatenate([kp_ref[...], kc_ref[...]], axis=0)
        V = jnp.concatenate([vp_ref[...], vc_ref[...]], axis=0)
        qch = [qr_ref[:, c * LANES:(c + 1) * LANES] for c in range(4)]
        dch = [dya_ref[:, c * LANES:(c + 1) * LANES].astype(F32) for c in range(4)]
        lse_tile = lse_ref[...]
        lane8 = lax.broadcasted_iota(jnp.int32, (8, LANES), 1)
        grp = lambda x, h: x[4 * h * blk:4 * (h + 1) * blk]
        qs = jnp.concatenate([_stack_heads(qch, h, BF16) for h in range(2)], axis=0)
        dos = jnp.concatenate([_stack_heads(dch, h, BF16) for h in range(2)], axis=0)
        lse_col = jnp.concatenate([lse_tile[:, j:j + 1] for j in range(N_Q_HEADS)], axis=0)
        s = jnp.concatenate([_dot_nt(grp(qs, h), K) for h in range(2)], axis=0)
        p = jnp.exp(s + _band_bias(n) - lse_col)
        dp = jnp.concatenate([_dot_nt(grp(dos, h), V) for h in range(2)], axis=0)
        delta = jnp.sum(p * dp, axis=1, keepdims=True)
        dsb = (p * (dp - delta)).astype(BF16)
        pb = p.astype(BF16)
        dq = jnp.concatenate([_dot(grp(dsb, h), K) for h in range(2)], axis=0)
        dk_acc = _dot_tn(grp(dsb, 0), grp(qs, 0)) + _dot_tn(grp(dsb, 1), grp(qs, 1))
        dv_acc = _dot_tn(grp(pb, 0), grp(dos, 0)) + _dot_tn(grp(pb, 1), grp(dos, 1))
        dqs = [dq[j * blk:(j + 1) * blk, :] for j in range(N_Q_HEADS)]
        dsink = jnp.zeros((8, LANES), F32)
        for j in range(N_Q_HEADS):
            rows = slice(j * blk, (j + 1) * blk)
            ps_delta = jnp.exp(sink_ref[j] - lse_col[rows]) * delta[rows]
            dsink = jnp.where(lane8 == j, dsink - jnp.sum(ps_delta), dsink)
        dsink_ref[...] += dsink
        cos_c, sin_c = cos_ref[...], sin_ref[...]
        dqkv_ref[:, 0:512] = cq[...]
        dqkv_ref[:, 512:640] = _rope_t(dk_acc[0:blk, :] + ck[...], cosp_ref[...], sinp_ref[...]).astype(BF16)
        dqkv_ref[:, 640:768] = (dv_acc[0:blk, :] + cv[...]).astype(BF16)
        ck[...] = dk_acc[blk:2 * blk, :]
        cv[...] = dv_acc[blk:2 * blk, :]
        for c in range(4):
            dqc = _from_kv_lanes(dqs[2 * c], dqs[2 * c + 1], c // 2) * (HEAD_DIM ** -0.5)
            dq_c = _rope_t(dqc, cos_c, sin_c).astype(BF16)
            cq[:, c * LANES:(c + 1) * LANES] = dq_c
            last_ref[:, c * LANES:(c + 1) * LANES] = dq_c
        last_ref[:, 512:640] = _rope_t(dk_acc[blk:2 * blk, :], cos_c, sin_c).astype(BF16)
        last_ref[:, 640:768] = dv_acc[blk:2 * blk, :].astype(BF16)

    prev = lambda n: jnp.maximum(n - 1, 0)
    same = lambda n: n
    bs = lambda w, f: pl.BlockSpec((blk, w), lambda n: (f(n), 0))
    return dict(
        init=init, body=body, operands=[sinks, dya, qr, kr, kr, vb, vb, lse, cos, sin, cos, sin],
        in_specs=[pl.BlockSpec(memory_space=pltpu.SMEM),
                  bs(512, same), bs(512, same), bs(LANES, same), bs(LANES, prev), bs(LANES, same), bs(LANES, prev),
                  bs(LANES, same), bs(LANES, same), bs(LANES, same), bs(LANES, prev), bs(LANES, prev)],
        out_specs=[bs(A_W, prev), _full((blk, A_W)), _full((8, LANES))],
        out_shape=[jax.ShapeDtypeStruct((S, A_W), BF16), jax.ShapeDtypeStruct((blk, A_W), BF16),
                   jax.ShapeDtypeStruct((8, LANES), F32)],
        scratch=[pltpu.VMEM((blk, LANES), F32), pltpu.VMEM((blk, LANES), F32), pltpu.VMEM((blk, 512), BF16)])


def _split3(x):
    hi = x.astype(BF16)
    r1 = x - hi.astype(F32)
    mid = r1.astype(BF16)
    lo = (r1 - mid.astype(F32)).astype(BF16)
    return hi, mid, lo


def _tri_matmul(tri_b, x):
    hi, mid, lo = _split3(x)
    return _dot(tri_b, hi) + _dot(tri_b, mid) + _dot(tri_b, lo)


def _log_sigmoid(x):
    return jnp.minimum(x, 0.0) - jnp.log(1.0 + jnp.exp(-jnp.abs(x)))


def _shift_rows(cur, seam, k, down):
    L = cur.shape[0]
    row8 = lax.broadcasted_iota(jnp.int32, seam.shape, 0)
    if down:
        mixed = jnp.concatenate([cur[:L - 8], jnp.where(row8 >= 8 - k, seam, cur[L - 8:])], axis=0)
        return pltpu.roll(mixed, k, 0)
    mixed = jnp.concatenate([jnp.where(row8 < k, seam, cur[:8]), cur[8:]], axis=0)
    return pltpu.roll(mixed, L - k, 0)


def _conv_fwd(cur, tail, cw_ref):
    z = cw_ref[4:5, :]
    for k in range(3, 0, -1):
        z = z + _shift_rows(cur, tail, k, True) * cw_ref[3 - k:4 - k, :]
    return z + cur * cw_ref[3:4, :]


def _rowsum_mxu(x, two_pass=False):
    ones = jnp.ones((x.shape[1], LANES), BF16)
    hi = x.astype(BF16)
    s = _dot(hi, ones)
    if two_pass:
        s = s + _dot((x - hi.astype(F32)).astype(BF16), ones)
    return s


def _stack(f):
    return jnp.concatenate([f(h) for h in range(MLSTM_HEADS)], axis=0)


def _head(x, h):
    L = x.shape[0] // MLSTM_HEADS
    return x[h * L:(h + 1) * L]


def _mlstm_heads_fwd(qk, cur_ref, gt, b_all, c_prev, nmv, tri, eye):
    L = qk.shape[0]
    HD = MLSTM_HEAD_DIM
    W4 = MLSTM_HEADS * HD
    col2row = lambda x: jnp.sum(jnp.where(eye, x, 0.0), axis=0, keepdims=True)
    b_col = _stack(lambda h: b_all[:, 4 + h:5 + h])
    i_col = _stack(lambda h: gt[:, h:h + 1])
    b_row = _stack(lambda h: jnp.broadcast_to(col2row(b_all[:, 4 + h:5 + h]), (L, L)))
    i_row = _stack(lambda h: jnp.broadcast_to(col2row(gt[:, h:h + 1]), (L, L)))
    bl = _stack(lambda h: jnp.broadcast_to(b_all[L - 1:L, 4 + h:5 + h], (L, 1)))
    m_prev = _stack(lambda h: jnp.broadcast_to(nmv[4 + h:5 + h, 0:1], (L, 1)))
    n_prev = _stack(lambda h: jnp.broadcast_to(nmv[h:h + 1, :], (L, HD)))
    tri4 = jnp.concatenate([tri] * MLSTM_HEADS, axis=0)
    Dm = jnp.where(tri4, b_col - b_row + i_row, NEG)
    inter = b_col + m_prev
    m_t = jnp.maximum(inter, jnp.max(Dm, axis=1, keepdims=True))
    W = jnp.exp(Dm - m_t)
    e_t = jnp.exp(inter - m_t)
    q = _stack(lambda h: qk[:, h * HD:(h + 1) * HD])
    k = _stack(lambda h: qk[:, W4 + h * HD:W4 + (h + 1) * HD]) * (HD ** -0.5)
    v = _stack(lambda h: cur_ref[:, 2 * W4 + h * HD:2 * W4 + (h + 1) * HD])
    qb, kb, vb = q.astype(BF16), k.astype(BF16), v.astype(BF16)
    Sc = _stack(lambda h: _dot_nt(_head(qb, h), _head(kb, h))) * W
    Scb = Sc.astype(BF16)
    cb = [c.astype(BF16) for c in c_prev]
    P1 = _stack(lambda h: _dot(_head(qb, h), cb[h]))
    num = _stack(lambda h: _dot(_head(Scb, h), _head(vb, h))) + e_t * P1
    qn = _rowsum_mxu(q * n_prev)
    den = _rowsum_mxu(Scb) + e_t * qn
    floor = jnp.broadcast_to(jnp.exp(-m_t), den.shape)
    inv_g = _recip(jnp.maximum(jnp.abs(den), floor))
    hv = num * inv_g
    a_col = bl - b_col + i_col
    a_max = _stack(lambda h: jnp.broadcast_to(jnp.max(_head(a_col, h), axis=0, keepdims=True), (L, 1)))
    m_new = jnp.maximum(bl + m_prev, a_max)
    dec = jnp.exp(bl + m_prev - m_new)
    u_col = jnp.exp(a_col - m_new)
    return dict(W=W, e_t=e_t, q=q, k=k, v=v, qb=qb, kb=kb, vb=vb, cb=cb, Sc=Sc, Scb=Scb, P1=P1, qn=qn, den=den,
                floor=floor, inv_g=inv_g, hv=hv, n_prev=n_prev, m_new=m_new, dec=dec, u_col=u_col)


def _mlstm_fwd_part(pm, pif, cw, sv):
    S = pm.shape[0]
    L = MLSTM_CHUNK
    nc = S // L
    HD = MLSTM_HEAD_DIM
    W4 = MLSTM_HEADS * HD

    def init(c, cur_ref, pif_ref, cw_ref, sv_ref, ym_ref, z_ref, cst_ref, nst_ref, C, nm, tail):
        @pl.when(c == 0)
        def _():
            C[...] = jnp.zeros_like(C)
            nm[...] = jnp.zeros_like(nm)
            tail[...] = jnp.zeros_like(tail)

    def body(c, cur_ref, pif_ref, cw_ref, sv_ref, ym_ref, z_ref, cst_ref, nst_ref, C, nm, tail):
        z = _conv_fwd(cur_ref[:, 0:2 * W4], tail[...], cw_ref)
        tail[...] = cur_ref[L - 8:L, 0:2 * W4]
        z_ref[...] = z
        qk = z * _sigmoid(z)
        gt = pif_ref[...] + sv_ref[1:2, 0:LANES]
        r_i = lax.broadcasted_iota(jnp.int32, (L, L), 0)
        c_i = lax.broadcasted_iota(jnp.int32, (L, L), 1)
        tri = c_i <= r_i
        eye = c_i == r_i
        b_all = _tri_matmul(tri.astype(BF16), _log_sigmoid(gt))
        nmv = nm[...]
        nst_ref[0] = nmv
        c_prev = [C[h] for h in range(MLSTM_HEADS)]
        f = _mlstm_heads_fwd(qk, cur_ref, gt, b_all, c_prev, nmv, tri, eye)
        hv = f["hv"]
        xc = hv - _rowsum_mxu(hv, True) * (1.0 / HD)
        hhat = xc * lax.rsqrt(_rowsum_mxu(xc * xc) * (1.0 / HD) + NORM_EPS)
        so = _sigmoid(_stack(lambda h: cur_ref[:, 3 * W4 + h * HD:3 * W4 + (h + 1) * HD]))
        wn = _stack(lambda h: jnp.broadcast_to(sv_ref[0:1, h * HD:(h + 1) * HD], (L, HD)))
        y = (so * hhat * wn).astype(BF16)
        kw = f["k"] * f["u_col"]
        kwb = kw.astype(BF16)
        n_new, m_new = [], []
        for h in range(MLSTM_HEADS):
            cst_ref[0, h] = c_prev[h]
            ym_ref[:, h * HD:(h + 1) * HD] = _head(y, h)
            dec = f["dec"][h * L:h * L + 1, :]
            C[h] = dec * c_prev[h] + _dot_tn(_head(kwb, h), _head(f["vb"], h))
            n_new.append(dec * nmv[h:h + 1, :] + _colsum(_head(kw, h)))
            m_new.append(jnp.broadcast_to(f["m_new"][h * L:h * L + 1, :], (1, LANES)))
        nm[...] = jnp.concatenate(n_new + m_new, axis=0)

    return dict(
        init=init, body=body, operands=[pm, pif, cw, sv],
        in_specs=[pl.BlockSpec((L, M_W), lambda c: (c, 0)),
                  pl.BlockSpec((L, IF_W), lambda c: (c, 0)), _full(cw.shape), _full(sv.shape)],
        out_specs=[pl.BlockSpec((L, W4), lambda c: (c, 0)), pl.BlockSpec((L, 2 * W4), lambda c: (c, 0)),
                   pl.BlockSpec((1, MLSTM_HEADS, HD, HD), lambda c: (c, 0, 0, 0)),
                   pl.BlockSpec((1, 8, LANES), lambda c: (c, 0, 0))],
        out_shape=[jax.ShapeDtypeStruct((S, W4), BF16), jax.ShapeDtypeStruct((S, 2 * W4), F32),
                   jax.ShapeDtypeStruct((nc, MLSTM_HEADS, HD, HD), F32), jax.ShapeDtypeStruct((nc, 8, LANES), F32)],
        scratch=[pltpu.VMEM((MLSTM_HEADS, HD, HD), F32), pltpu.VMEM((8, LANES), F32), pltpu.VMEM((8, 2 * W4), F32)])


def _mlstm_bwd_part(pm, zc, pif, cw, sv, dym, cst, nst):
    S = pm.shape[0]
    L = MLSTM_CHUNK
    nc = S // L
    HD = MLSTM_HEAD_DIM
    W4 = MLSTM_HEADS * HD

    def init(r, cur_ref, z_ref, pif_ref, cw_ref, sv_ref, dym_ref, cst_ref, nst_ref,
             dm_ref, dcw_ref, dsv_ref, dC, dn, dz_next, dqk):
        @pl.when(r == 0)
        def _():
            dC[...] = jnp.zeros_like(dC)
            dn[...] = jnp.zeros_like(dn)
            dz_next[...] = jnp.zeros_like(dz_next)
            dcw_ref[...] = jnp.zeros_like(dcw_ref)
            dsv_ref[...] = jnp.zeros_like(dsv_ref)

    def body(r, cur_ref, z_ref, pif_ref, cw_ref, sv_ref, dym_ref, cst_ref, nst_ref,
             dm_ref, dcw_ref, dsv_ref, dC, dn, dz_next, dqk):
        z = z_ref[...]
        sgz = _sigmoid(z)
        qk = z * sgz
        gt = pif_ref[...] + sv_ref[1:2, 0:LANES]
        r_i = lax.broadcasted_iota(jnp.int32, (L, L), 0)
        c_i = lax.broadcasted_iota(jnp.int32, (L, L), 1)
        tri = c_i <= r_i
        eye = c_i == r_i
        b_all = _tri_matmul(tri.astype(BF16), _log_sigmoid(gt))
        lane = lax.broadcasted_iota(jnp.int32, (L, LANES), 1)
        rowl = lax.broadcasted_iota(jnp.int32, (L, 1), 0)
        nmv = nst_ref[0]
        heads = range(MLSTM_HEADS)
        c_prev = [cst_ref[0, h] for h in heads]
        f = _mlstm_heads_fwd(qk, cur_ref, gt, b_all, c_prev, nmv, tri, eye)
        hv, inv_g, den, e_t, u_col, n_prev = f["hv"], f["inv_g"], f["den"], f["e_t"], f["u_col"], f["n_prev"]
        q, k, v, qb, kb, vb, Sc, Scb, W = f["q"], f["k"], f["v"], f["qb"], f["kb"], f["vb"], f["Sc"], f["Scb"], f["W"]
        xc = hv - _rowsum_mxu(hv, True) * (1.0 / HD)
        rstd = lax.rsqrt(_rowsum_mxu(xc * xc) * (1.0 / HD) + NORM_EPS)
        hhat = xc * rstd
        wn = _stack(lambda h: jnp.broadcast_to(sv_ref[0:1, h * HD:(h + 1) * HD], (L, HD)))
        so = _sigmoid(_stack(lambda h: cur_ref[:, 3 * W4 + h * HD:3 * W4 + (h + 1) * HD]))
        dy = _stack(lambda h: dym_ref[:, h * HD:(h + 1) * HD])
        d_o = (dy * hhat * wn * (so * (1.0 - so))).astype(BF16)
        dln = dy * so
        dwn = dln * hhat
        dhhat = dln * wn
        m2 = _rowsum_mxu(dhhat * hhat) * (1.0 / HD)
        dh = rstd * (dhhat - _rowsum_mxu(dhhat) * (1.0 / HD) - hhat * m2)
        dnum = dh * inv_g
        active = jnp.abs(den) > f["floor"]
        dden = jnp.where(active, -(HD * NORM_EPS) * m2 * rstd * rstd * inv_g * jnp.where(den >= 0.0, 1.0, -1.0), 0.0)
        dnumb = dnum.astype(BF16)
        dSc = _stack(lambda h: _dot_nt(_head(dnumb, h), _head(vb, h))) + dden
        dA = (dSc * W).astype(BF16)
        G = dSc * Sc
        Gb = G.astype(BF16)
        Gl = (G - Gb.astype(F32)).astype(BF16)
        ones = jnp.ones((L, LANES), BF16)
        Gr = _dot(Gb, ones) + _dot(Gl, ones)
        Gc = _stack(lambda h: _dot_tn(_head(Gb, h), ones) + _dot_tn(_head(Gl, h), ones))
        dCn = [dC[h] for h in heads]
        dCnb = [d.astype(BF16) for d in dCn]
        dnv = dn[...]
        dn_new = _stack(lambda h: jnp.broadcast_to(dnv[h:h + 1, :], (L, HD)))
        kdC = _stack(lambda h: _dot(_head(kb, h), dCnb[h]))
        vdC = _stack(lambda h: _dot_nt(_head(vb, h), dCnb[h]))
        dv = (_stack(lambda h: _dot_tn(_head(Scb, h), _head(dnumb, h))) + u_col * kdC).astype(BF16)
        dq = _stack(lambda h: _dot(_head(dA, h), _head(kb, h))) \
            + e_t * _stack(lambda h: _dot_nt(_head(dnumb, h), f["cb"][h])) + (e_t * dden) * n_prev
        dk = (_stack(lambda h: _dot_tn(_head(dA, h), _head(qb, h))) + u_col * (vdC + dn_new)) * (HD ** -0.5)
        E = (_rowsum_mxu(f["P1"] * dnum, True) + dden * f["qn"]) * e_t
        U = _rowsum_mxu(kdC * v + k * dn_new, True) * u_col
        qe = (q * e_t).astype(BF16)
        qd = (e_t * dden) * q
        di = Gc + U
        db = Gr + E - Gc - U
        di_tile = jnp.zeros((L, LANES), F32)
        db_tile = jnp.zeros((L, LANES), F32)
        dn_rows = []
        for h in heads:
            dec = f["dec"][h * L:h * L + 1, :]
            ddec = jnp.sum(dCn[h] * c_prev[h]) + jnp.sum(dnv[h:h + 1, :] * nmv[h:h + 1, :])
            dbl = ddec * dec + jnp.sum(_head(U, h), axis=0, keepdims=True)
            di_tile = jnp.where(lane == h, _head(di, h), di_tile)
            db_tile = jnp.where(lane == 4 + h, _head(db, h) + jnp.where(rowl == L - 1, dbl, 0.0), db_tile)
            dC[h] = dec * dCn[h] + _dot_tn(_head(qe, h), _head(dnumb, h))
            dn_rows.append(dec * dnv[h:h + 1, :] + _colsum(_head(qd, h)))
            dsv_ref[0:1, h * HD:(h + 1) * HD] += _colsum(_head(dwn, h))
            dqk[:, h * HD:(h + 1) * HD] = _head(dq, h)
            dqk[:, W4 + h * HD:W4 + (h + 1) * HD] = _head(dk, h)
            dm_ref[:, 2 * W4 + h * HD:2 * W4 + (h + 1) * HD] = _head(dv, h)
            dm_ref[:, 3 * W4 + h * HD:3 * W4 + (h + 1) * HD] = _head(d_o, h)
        dn[...] = jnp.concatenate(dn_rows + [jnp.zeros((8 - MLSTM_HEADS, LANES), F32)], axis=0)
        dlf = _tri_matmul((r_i <= c_i).astype(BF16), db_tile)
        dif = jnp.where(lane < 4, di_tile, jnp.where(lane < 8, dlf * (1.0 - _sigmoid(gt)), 0.0))
        dm_ref[:, M_W:M_W + IF_W] = dif.astype(BF16)
        dsv_ref[1:2, 0:LANES] += _colsum(dif)
        dz = dqk[...] * (sgz * (1.0 + z * (1.0 - sgz)))
        dcw_ref[4:5, :] += _colsum(dz)
        u = cur_ref[:, 0:2 * W4]
        du_in = dz * cw_ref[3:4, :]
        dcw_ref[3:4, :] += _colsum(dz * u)
        for k in range(1, 4):
            up = _shift_rows(dz, dz_next[...], k, False)
            dcw_ref[3 - k:4 - k, :] += _colsum(up * u)
            du_in = du_in + up * cw_ref[3 - k:4 - k, :]
        dz_next[...] = dz[0:8, :]
        dm_ref[:, 0:2 * W4] = du_in.astype(BF16)

    cidx = lambda r: nc - 1 - r
    return dict(
        init=init, body=body, operands=[pm, zc, pif, cw, sv, dym, cst, nst],
        in_specs=[pl.BlockSpec((L, M_W), lambda r: (cidx(r), 0)), pl.BlockSpec((L, 2 * W4), lambda r: (cidx(r), 0)),
                  pl.BlockSpec((L, IF_W), lambda r: (cidx(r), 0)), _full(cw.shape), _full(sv.shape),
                  pl.BlockSpec((L, W4), lambda r: (cidx(r), 0)),
                  pl.BlockSpec((1, MLSTM_HEADS, HD, HD), lambda r: (cidx(r), 0, 0, 0)),
                  pl.BlockSpec((1, 8, LANES), lambda r: (cidx(r), 0, 0))],
        out_specs=[pl.BlockSpec((L, M_W + IF_W), lambda r: (cidx(r), 0)), _full((8, 2 * W4)), _full((8, W4))],
        out_shape=[jax.ShapeDtypeStruct((S, M_W + IF_W), BF16),
                   jax.ShapeDtypeStruct((8, 2 * W4), F32), jax.ShapeDtypeStruct((8, W4), F32)],
        scratch=[pltpu.VMEM((MLSTM_HEADS, HD, HD), F32), pltpu.VMEM((8, LANES), F32),
                 pltpu.VMEM((8, 2 * W4), F32), pltpu.VMEM((L, 2 * W4), F32)])


def _rope_tables(positions):
    half = HEAD_DIM // 2
    inv_freq = ROPE_THETA ** (-2.0 * jnp.arange(half, dtype=F32) / HEAD_DIM)
    ang = positions.astype(F32)[:, None] * inv_freq
    cos = jnp.tile(jnp.cos(ang), (1, LANES // half))
    sign = jnp.tile(jnp.concatenate([-jnp.ones((half,), F32), jnp.ones((half,), F32)]), LANES // HEAD_DIM)
    sin = jnp.tile(jnp.sin(ang), (1, LANES // half)) * sign
    return cos, sin


def _local_step(x, tgt, positions, mod, gains, w_cat, w_ba, w_bm, w_out, w_gate, w_up, w_down,
                conv_w, conv_b, b_if, sinks, norm_w):
    t = _tables(mod, gains, conv_w, conv_b, b_if, norm_w, positions)
    a = _mixer_fwd(x, t, sinks, w_cat)
    b = _ffn_part(x, tgt, t, a, w_ba, w_bm, w_out, w_gate, w_up, w_down)
    c = _mixer_bwd(b["dx1"], t, a, b, sinks, w_ba, w_bm, w_out)
    grad_x, acc_p = _pre_bwd(c["dproj"], x, b["dx1"], t["vecs"], w_cat)
    big = dict(w_cat=jnp.concatenate(c["g_w_cat"], axis=1), w_ba=c["g_w_ba"], w_bm=c["g_w_bm"], w_out=c["g_w_out"], w_gate=b["g_w_gate"],
               w_up=b["g_w_up"], w_down=b["g_w_down"])
    return b["loss"], grad_x, big, _small_grads(acc_p, b, c)


def _tables(mod, gains, conv_w, conv_b, b_if, norm_w, positions):
    cos, sin = _rope_tables(positions)
    return dict(
        vecs=jnp.concatenate([mod, gains, jnp.zeros((6, D_MODEL), F32)], axis=0),
        cw=jnp.concatenate([conv_w, conv_b.reshape(1, -1), jnp.zeros((3, 2 * 512), F32)], axis=0),
        sv=jnp.zeros((8, 512), F32).at[0].set(norm_w).at[1, 0:8].set(b_if), cos=cos, sin=sin)


def _mixer_fwd(x, t, sinks, w_cat):
    h, pa, pm, pif, pg = _pre_proj(x, t["vecs"], w_cat)
    n_blk = x.shape[0] // ATTN_BLOCK
    (ya, qr, kr, vb, lse), = _fused_call([_attn_fwd_part(pa, t["cos"], t["sin"], sinks)], "attn_fwd", n_blk)
    (ym, zc, cst, nst), = _fused_call([_mlstm_fwd_part(pm, pif, t["cw"], t["sv"])], "mlstm_fwd", n_blk)
    return dict(h=h, pm=pm, pif=pif, pg=pg, ya=ya, qr=qr, kr=kr, vb=vb, lse=lse, ym=ym, zc=zc, cst=cst, nst=nst)


def _ffn_part(x, tgt, t, a, w_ba, w_bm, w_out, w_gate, w_up, w_down):
    x1, merged, mix, pba, pbm = _mix_fwd(x, a["ya"], a["ym"], a["pg"], t["vecs"], w_ba, w_bm, w_out)
    dx1, h2, hid, da, du, dff, acc_f, loss = _ffn_fwd_bwd(x1, tgt, t["vecs"], w_gate, w_up, w_down)
    return dict(merged=merged, mix=mix, pba=pba, pbm=pbm, dx1=dx1, acc_f=acc_f, loss=loss[0, 0],
                g_w_gate=_matmul_tn(da, h2, 1024, "dw_ffn_gate"),
                g_w_up=_matmul_tn(du, h2, 1024, "dw_ffn_up"),
                g_w_down=_matmul_tn(hid, dff, 1024, "dw_ffn_down"))


def _mixer_bwd(dx1, t, a, b, sinks, w_ba, w_bm, w_out):
    dmix, dpa, dpb, dg, dya, dym, acc_m = _mix_bwd(dx1, b["mix"], b["pba"], b["pbm"], a["pg"], t["vecs"], w_ba, w_bm, w_out)
    g_w_out = _matmul_tn(b["merged"], dmix, 1024, "dw_out")
    g_w_ba = _matmul_tn(a["ya"], dpa, 1024, "dw_branch_attn")
    g_w_bm = _matmul_tn(a["ym"], dpb, 1024, "dw_branch_mlstm")
    n_blk = dx1.shape[0] // ATTN_BLOCK
    (dqkv, dqkv_last, dsink), = _fused_call(
        [_attn_bwd_part(dya, a["qr"], a["kr"], a["vb"], a["lse"], t["cos"], t["sin"], sinks)], "attn_bwd", n_blk)
    (dm, dcw, dsv), = _fused_call(
        [_mlstm_bwd_part(a["pm"], a["zc"], a["pif"], t["cw"], t["sv"], dym, a["cst"], a["nst"])], "mlstm_bwd", n_blk)
    dqkv = lax.dynamic_update_slice(dqkv, dqkv_last, (dqkv.shape[0] - ATTN_BLOCK, 0))
    dproj = [dqkv, dm, dg]
    g_w_cat = [_matmul_tn(a["h"], p, 1024 if p.shape[1] % 1024 == 0 else p.shape[1], "dw_in_" + n)
               for p, n in zip(dproj, ("attn", "mlstm", "branch"))]
    return dict(dproj=dproj, g_w_cat=g_w_cat, g_w_out=g_w_out, g_w_ba=g_w_ba,
                g_w_bm=g_w_bm, acc_m=acc_m, dsink=dsink, dcw=dcw, dsv=dsv)


def _small_grads(acc_p, b, c):
    acc_f, acc_m = b["acc_f"], c["acc_m"]
    dmod = jnp.stack([acc_p[1], acc_p[0], acc_m[0], acc_f[3], acc_f[2], acc_f[0]])
    dgains = jnp.stack([acc_p[2], acc_m[1], acc_f[4], acc_f[1]])
    return dict(dmod=dmod, dgains=dgains, dconv_w=c["dcw"][0:4], dconv_b=c["dcw"][4], db_if=c["dsv"][1, 0:8],
                dsinks=c["dsink"][0, 0:8], dnorm_w=c["dsv"][0])


MESH_ID = pl.DeviceIdType.MESH


def _mesh_pos():
    return lax.axis_index("x"), lax.axis_index("y"), lax.axis_index("c")


def _flip(v, bit):
    return 1 - v if bit else v


def _relations():
    return [((r >> 2) & 1, (r >> 1) & 1, r & 1) for r in range(1, N_DEV)]


def _small_exchange(p, gather, name):
    R, V = p.shape[-2:]

    def body(p_ref, out_ref, send_sems, recv_sems):
        x, y, c = _mesh_pos()
        me = 4 * x + 2 * y + c
        out_ref[me] = p_ref[...] if gather else p_ref[me]
        peers = []
        for dx, dy, dc in _relations():
            px, py, pc = _flip(x, dx), _flip(y, dy), _flip(c, dc)
            peers.append(((px, py, pc), 4 * px + 2 * py + pc))

        def copy(k, landing):
            peer, pid = peers[k]
            return pltpu.make_async_remote_copy(
                src_ref=p_ref if gather else p_ref.at[pid], dst_ref=out_ref.at[landing],
                send_sem=send_sems.at[k], recv_sem=recv_sems.at[k], device_id=peer, device_id_type=MESH_ID)

        sends = [copy(k, me) for k in range(N_DEV - 1)]
        for cp in sends:
            cp.start()
        for k in range(N_DEV - 1):
            copy(k, peers[k][1]).wait_recv()
        for cp in sends:
            cp.wait_send()

    vm = pl.BlockSpec(memory_space=pltpu.VMEM)
    return pl.pallas_call(
        body, name=name, in_specs=[vm], out_specs=vm,
        out_shape=jax.ShapeDtypeStruct((N_DEV, R, V), F32),
        scratch_shapes=[pltpu.SemaphoreType.DMA((N_DEV - 1,)), pltpu.SemaphoreType.DMA((N_DEV - 1,))],
        compiler_params=pltpu.CompilerParams(vmem_limit_bytes=VMEM_LIMIT),
    )(p)


HBM_SPEC = pl.BlockSpec(memory_space=pltpu.HBM)
SEM_SPEC = pl.BlockSpec(memory_space=pltpu.SEMAPHORE)


def _peers(x, y, c):
    out = []
    for dx, dy, dc in _relations():
        px, py, pc = _flip(x, dx), _flip(y, dy), _flip(c, dc)
        out.append(((px, py, pc), 4 * px + 2 * py + pc))
    return out


def _exchange_start(arrs, gather, after, name):
    n = len(arrs)
    me_out = 4 * lax.axis_index("x") + 2 * lax.axis_index("y") + lax.axis_index("c")
    lands = []
    for a in arrs:
        own = a[None] if gather else lax.dynamic_index_in_dim(a, me_out, 0, keepdims=True)
        empty = lax.empty(((N_DEV,) + a.shape) if gather else a.shape, a.dtype)
        lands.append(lax.dynamic_update_index_in_dim(empty, own, me_out, 0))

    def body(*refs):
        a_refs, l_refs = refs[:n], refs[n:2 * n]
        send_sems, recv_sems = refs[2 * n + 1], refs[2 * n + 2]
        token = refs[4 * n + 3]
        x, y, c = _mesh_pos()
        me = 4 * x + 2 * y + c
        for a in range(n):
            for k, (peer, pid) in enumerate(_peers(x, y, c)):
                pltpu.make_async_remote_copy(
                    src_ref=a_refs[a] if gather else a_refs[a].at[pid], dst_ref=l_refs[a].at[me],
                    send_sem=send_sems.at[a * (N_DEV - 1) + k], recv_sem=recv_sems.at[a * (N_DEV - 1) + k],
                    device_id=peer, device_id_type=MESH_ID).start()
        token[...] = jnp.zeros_like(token)

    sem = pltpu.SemaphoreType.DMA((n * (N_DEV - 1),))
    hbm = lambda a: pltpu.with_memory_space_constraint(a, pltpu.HBM)
    res = pl.pallas_call(
        body, name=name,
        out_shape=(sem, sem, *[pltpu.HBM(a.shape, a.dtype) for a in arrs], *[pltpu.HBM(l.shape, l.dtype) for l in lands],
                   jax.ShapeDtypeStruct((8, LANES), F32)),
        in_specs=[HBM_SPEC] * (2 * n) + [pl.BlockSpec(memory_space=pl.ANY)],
        out_specs=(SEM_SPEC, SEM_SPEC, *[HBM_SPEC] * (2 * n), pl.BlockSpec(memory_space=pltpu.VMEM)),
        input_output_aliases={i: 2 + i for i in range(2 * n)},
        compiler_params=pltpu.CompilerParams(has_side_effects=pltpu.SideEffectType.DATAFLOW_SIDE_EFFECTING),
    )(*[hbm(a) for a in arrs], *[hbm(l) for l in lands], after)
    return dict(sems=res[0:2], arrs=res[2:2 + n], lands=res[2 + n:2 + 2 * n], token=res[2 + 2 * n], gather=gather)


def _exchange_wait(st, after, name):
    n = len(st["arrs"])
    gather = st["gather"]

    def body(*refs):
        a_refs, l_refs = refs[:n], refs[n:2 * n]
        send_sems, recv_sems = refs[2 * n], refs[2 * n + 1]
        x, y, c = _mesh_pos()
        for a in range(n):
            for k, (peer, pid) in enumerate(_peers(x, y, c)):
                cp = pltpu.make_async_remote_copy(
                    src_ref=a_refs[a] if gather else a_refs[a].at[pid], dst_ref=l_refs[a].at[pid],
                    send_sem=send_sems.at[a * (N_DEV - 1) + k], recv_sem=recv_sems.at[a * (N_DEV - 1) + k],
                    device_id=peer, device_id_type=MESH_ID)
                cp.wait_send()
                cp.wait_recv()

    both = list(st["arrs"]) + list(st["lands"])
    res = pl.pallas_call(
        body, name=name, out_shape=[pltpu.HBM(a.shape, a.dtype) for a in both],
        in_specs=[HBM_SPEC] * (2 * n) + [SEM_SPEC, SEM_SPEC, pl.BlockSpec(memory_space=pl.ANY)],
        out_specs=[HBM_SPEC] * (2 * n), input_output_aliases={i: i for i in range(2 * n)},
        compiler_params=pltpu.CompilerParams(has_side_effects=pltpu.SideEffectType.DATAFLOW_SIDE_EFFECTING),
    )(*both, *st["sems"], after)
    return res[n:2 * n]


def _tie(x, token, name):
    def body(x_ref, t_ref, o_ref):
        o_ref[...] = x_ref[...]

    vm = pl.BlockSpec(memory_space=pltpu.VMEM)
    return pl.pallas_call(
        body, name=name, in_specs=[vm, pl.BlockSpec(memory_space=pl.ANY)], out_specs=vm,
        out_shape=jax.ShapeDtypeStruct(x.shape, x.dtype),
    )(x, token)


def _gather_front(shards, c8, w_ada, b_cols):
    n = len(shards)
    ada_w = w_ada.shape[1]
    chunk = c8.shape[1]

    def body(*refs):
        p_refs = refs[:n]
        c_ref, wa_ref, b_ref = refs[n:n + 3]
        out_refs = refs[n + 3:2 * n + 3]
        cg_ref, mod_ref = refs[2 * n + 3], refs[2 * n + 4]
        send_sems, recv_sems, local_sems, c_send, c_recv, m_send, m_recv, pbuf = refs[2 * n + 5:]
        x, y, c = _mesh_pos()
        me_id = 4 * x + 2 * y + c
        me, sibling = (x, y, c), (x, y, 1 - c)
        chips = [(1 - x, y), (x, 1 - y), (1 - x, 1 - y)]
        peers = _peers(x, y, c)

        def copy(a, k, block, to, own=False):
            slot = out_refs[a].at[4 * block[0] + 2 * block[1] + block[2]]
            return pltpu.make_async_remote_copy(
                src_ref=p_refs[a] if own else slot, dst_ref=slot,
                send_sem=send_sems.at[a, k], recv_sem=recv_sems.at[a, k], device_id=to, device_id_type=MESH_ID)

        def c_copy(k, landing):
            return pltpu.make_async_remote_copy(
                src_ref=c_ref, dst_ref=cg_ref.at[landing], send_sem=c_send.at[k], recv_sem=c_recv.at[k],
                device_id=peers[k][0], device_id_type=MESH_ID)

        cg_ref[me_id] = c_ref[...]
        c_sends = [c_copy(k, me_id) for k in range(N_DEV - 1)]
        for cp in c_sends:
            cp.start()
        mine = [pltpu.make_async_copy(p_refs[a], out_refs[a].at[me_id], local_sems.at[a]) for a in range(n)]
        for cp in mine:
            cp.start()
        first = []
        for a in range(n):
            first.append(copy(a, 0, me, sibling, own=True))
            first += [copy(a, 1 + j, me, (*chip, c), own=True) for j, chip in enumerate(chips)]
        for cp in first:
            cp.start()
        for k in range(N_DEV - 1):
            c_copy(k, peers[k][1]).wait_recv()
        prod = b_ref[...] + jnp.zeros((N_DEV, ada_w), F32)
        for j in range(D_MODEL // chunk):
            prod = prod + _dot(cg_ref[:, j, :].astype(BF16), wa_ref[j * chunk:(j + 1) * chunk, :].astype(BF16))
        for k in range(N_DEV):
            pbuf[k] = jnp.broadcast_to(prod[k:k + 1, :], (8, ada_w))

        def m_copy(k, landing):
            return pltpu.make_async_remote_copy(
                src_ref=pbuf.at[peers[k][1]], dst_ref=mod_ref.at[landing], send_sem=m_send.at[k], recv_sem=m_recv.at[k],
                device_id=peers[k][0], device_id_type=MESH_ID)

        mod_ref[me_id] = pbuf[me_id]
        m_sends = [m_copy(k, me_id) for k in range(N_DEV - 1)]
        for cp in m_sends:
            cp.start()

        passed = []
        for j, chip in enumerate(chips):
            for a in range(n):
                copy(a, 1 + j, (*chip, c), me).wait_recv()
                passed.append(copy(a, 4 + j, (*chip, c), sibling))
                passed[-1].start()
        for a in range(n):
            copy(a, 0, sibling, me).wait_recv()
            for j, chip in enumerate(chips):
                copy(a, 4 + j, (*chip, 1 - c), me).wait_recv()
        for k in range(N_DEV - 1):
            m_copy(k, peers[k][1]).wait_recv()
        for cp in first + passed + c_sends + m_sends:
            cp.wait_send()
        for cp in mine:
            cp.wait()

    hbm = pl.BlockSpec(memory_space=pl.ANY)
    vm = pl.BlockSpec(memory_space=pltpu.VMEM)
    sem7 = pltpu.SemaphoreType.DMA((N_DEV - 1,))
    res = pl.pallas_call(
        body, name="gather_front", in_specs=[hbm] * n + [vm, vm, vm], out_specs=[hbm] * n + [vm, vm],
        out_shape=[jax.ShapeDtypeStruct((N_DEV,) + s.shape, s.dtype) for s in shards]
        + [jax.ShapeDtypeStruct((N_DEV,) + c8.shape, F32), jax.ShapeDtypeStruct((N_DEV, 8, ada_w), F32)],
        scratch_shapes=[pltpu.SemaphoreType.DMA((n, N_DEV - 1)), pltpu.SemaphoreType.DMA((n, N_DEV - 1)),
                        pltpu.SemaphoreType.DMA((n,)), sem7, sem7, sem7, sem7, pltpu.VMEM((N_DEV, 8, ada_w), F32)],
        compiler_params=pltpu.CompilerParams(vmem_limit_bytes=VMEM_LIMIT),
    )(*shards, c8, w_ada, b_cols)
    return res[:n], res[n], res[n + 1]


def _adamw(w, g, m, v):
    m2 = ADAM_B1 * m + (1.0 - ADAM_B1) * g
    v2 = ADAM_B2 * v + (1.0 - ADAM_B2) * (g * g)
    m_hat = m2 / (1.0 - ADAM_B1 ** ADAM_STEP)
    v_hat = v2 / (1.0 - ADAM_B2 ** ADAM_STEP)
    delta = -ADAM_LR * (m_hat / (jnp.sqrt(v_hat) + ADAM_EPS) + ADAM_WD * w)
    return delta, m2, v2


def _adamw_w_ada(cmat, dmod_cols, w, m, v):
    r, cdim = w.shape
    tr = _row_tile(r)

    def body(c_ref, d_ref, w_ref, m_ref, v_ref, g_ref, dl_ref, m2_ref, v2_ref):
        g = _dot_tn(c_ref[...].astype(BF16), d_ref[...].astype(BF16))
        g_ref[...] = g
        dl_ref[...], m2_ref[...], v2_ref[...] = _adamw(w_ref[...], g, m_ref[...], v_ref[...])

    row = pl.BlockSpec((tr, cdim), lambda i: (i, 0))
    return pl.pallas_call(
        body, name="adamw_w_ada", grid=(r // tr,),
        in_specs=[pl.BlockSpec((N_DEV, tr), lambda i: (0, i)), _full(dmod_cols.shape), row, row, row],
        out_specs=[row] * 4, out_shape=[jax.ShapeDtypeStruct(w.shape, F32)] * 4,
        compiler_params=_params(("parallel",)),
    )(cmat, dmod_cols, w, m, v)


SMALL_ROWS = 16
SMALL_AT = {"b_ada": (0, 6, 0, D_MODEL), "g_pre_mix": (6, 1, 0, D_MODEL), "g_post_mix": (7, 1, 0, D_MODEL),
            "g_pre_ffn": (8, 1, 0, D_MODEL), "g_post_ffn": (9, 1, 0, D_MODEL), "conv_b": (10, 1, 0, D_MODEL),
            "mlstm_norm_w": (11, 1, 0, 512), "b_if": (11, 1, 512, LANES), "attn_sinks": (11, 1, 640, LANES)}


def _small_table(part):
    tail = jnp.concatenate([part["mlstm_norm_w"], jnp.pad(part["b_if"], (0, LANES - 8)),
                            jnp.pad(part["attn_sinks"], (0, LANES - 8)), jnp.zeros((256,), F32)])
    return jnp.concatenate([part["b_ada"], part["gains"], part["conv_b"][None], tail[None],
                            jnp.full((1, D_MODEL), part["loss"], F32),
                            jnp.zeros((SMALL_ROWS - 13, D_MODEL), F32)], axis=0)


LOSS_ROW = 12


def _adamw_small(gathered, wmv):
    names = list(SMALL_AT)

    def body(*refs):
        g_ref, ins, outs = refs[0], refs[1:1 + 3 * len(names)], refs[1 + 3 * len(names):]
        g = g_ref[0]
        for k in range(1, N_DEV):
            g = g + g_ref[k]
        outs[4 * len(names)][...] = g[LOSS_ROW:LOSS_ROW + 1, 0:LANES]
        for i, n in enumerate(names):
            r0, rows, l0, lanes = SMALL_AT[n]
            gi = jnp.concatenate([g[r:r + 1, l0:l0 + lanes] for r in range(r0, r0 + rows)], axis=1)
            w_ref, m_ref, v_ref = ins[3 * i:3 * i + 3]
            go, dl, m2, v2 = outs[4 * i:4 * i + 4]
            go[...] = gi
            dl[...], m2[...], v2[...] = _adamw(w_ref[...], gi, m_ref[...], v_ref[...])

    flat = [a for n in names for a in wmv[n]]
    res = pl.pallas_call(
        body, name="adamw_small",
        out_shape=[jax.ShapeDtypeStruct(wmv[n][0].shape, F32) for n in names for _ in range(4)]
        + [jax.ShapeDtypeStruct((1, LANES), F32)],
        compiler_params=_params(),
    )(gathered, *flat)
    out = {n: res[4 * i:4 * i + 4] for i, n in enumerate(names)}
    out["loss"] = res[4 * len(names)]
    return out


def _adamw_sum_many(items, name):
    n = len(items)

    def body(*refs):
        ins, outs = refs[:4 * n], refs[4 * n:]
        for i in range(n):
            r_ref, w_ref, m_ref, v_ref = ins[4 * i:4 * i + 4]
            go, dl, m2, v2 = outs[4 * i:4 * i + 4]
            g = _sum_partials(r_ref)
            go[...] = g
            dl[...], m2[...], v2[...] = _adamw(w_ref[...], g, m_ref[...], v_ref[...])

    res = pl.pallas_call(
        body, name=name, out_shape=[jax.ShapeDtypeStruct(it[1].shape, F32) for it in items for _ in range(4)],
        compiler_params=_params(),
    )(*[a for it in items for a in it])
    return [res[4 * i:4 * i + 4] for i in range(n)]


def _row_tile(rows):
    return rows // 4 if rows >= 512 and rows % 64 == 0 else rows


def _sum_partials(r_ref):
    g = r_ref[0].astype(F32)
    for k in range(1, N_DEV):
        g = g + r_ref[k].astype(F32)
    return g


def _adamw_sum(recv, w, m, v, name):
    r, cdim = w.shape
    tr = _row_tile(r)

    def body(r_ref, w_ref, m_ref, v_ref, g_ref, dl_ref, m2_ref, v2_ref):
        g = _sum_partials(r_ref)
        g_ref[...] = g
        dl_ref[...], m2_ref[...], v2_ref[...] = _adamw(w_ref[...], g, m_ref[...], v_ref[...])

    row = pl.BlockSpec((tr, cdim), lambda i: (i, 0))
    return pl.pallas_call(
        body, name=name, grid=(r // tr,),
        in_specs=[pl.BlockSpec((N_DEV, tr, cdim), lambda i: (0, i, 0)), row, row, row],
        out_specs=[row] * 4, out_shape=[jax.ShapeDtypeStruct((r, cdim), F32)] * 4,
        compiler_params=_params(("parallel",)),
    )(recv, w, m, v)


def _sum8(recv, name):
    _, r, cdim = recv.shape
    tr = _row_tile(r)

    def body(r_ref, g_ref):
        g_ref[...] = _sum_partials(r_ref)

    return pl.pallas_call(
        body, name=name, grid=(r // tr,),
        in_specs=[pl.BlockSpec((N_DEV, tr, cdim), lambda i: (0, i, 0))],
        out_specs=pl.BlockSpec((tr, cdim), lambda i: (i, 0)), out_shape=jax.ShapeDtypeStruct((r, cdim), F32),
        compiler_params=_params(("parallel",)),
    )(recv)


def _adamw_plain(g, w, m, v, name):
    r, cdim = w.shape
    tr = _row_tile(r)

    def body(g_ref, w_ref, m_ref, v_ref, dl_ref, m2_ref, v2_ref):
        dl_ref[...], m2_ref[...], v2_ref[...] = _adamw(w_ref[...], g_ref[...], m_ref[...], v_ref[...])

    row = pl.BlockSpec((tr, cdim), lambda i: (i, 0))
    return pl.pallas_call(
        body, name=name, grid=(r // tr,), in_specs=[row] * 4, out_specs=[row] * 3,
        out_shape=[jax.ShapeDtypeStruct((r, cdim), F32)] * 3,
        compiler_params=_params(("parallel",)),
    )(g, w, m, v)


IN_SHARD = 609
IN_SHARD_PAD = 640
IF_AT = A_W + M_W


def _regrouped(u):
    return u if u < IF_AT + 8 else u + (IF_W - 8)


def _selection(k, rows, row0, transpose):
    shape = (rows, IN_SHARD_PAD) if transpose else (IN_SHARD_PAD, rows)
    l = lax.broadcasted_iota(jnp.int32, shape, 1 if transpose else 0)
    r = lax.broadcasted_iota(jnp.int32, shape, 0 if transpose else 1) + row0
    u = l + IN_SHARD * k
    ru = u + jnp.where(u >= IF_AT + 8, IF_W - 8, 0)
    return ((ru == r) & (l < IN_SHARD)).astype(BF16)


def _regroup_w_in(g):
    def body(g_ref, o_ref):
        for cb in range(CAT_W // LANES):
            r0 = cb * LANES
            acc = jnp.zeros((D_MODEL, LANES), F32)
            for k in range(N_DEV):
                src = [l for l in range(IN_SHARD) if r0 <= _regrouped(IN_SHARD * k + l) < r0 + LANES]
                if src:
                    a0 = src[0] // LANES * LANES
                    a1 = min(a0 + 2 * LANES, IN_SHARD_PAD)
                    acc = acc + _dot(g_ref[k, :, a0:a1], _selection(k, LANES, r0, False)[a0:a1])
            o_ref[:, r0:r0 + LANES] = acc.astype(BF16)

    return pl.pallas_call(
        body, name="regroup_w_in", out_shape=jax.ShapeDtypeStruct((D_MODEL, CAT_W), BF16),
        compiler_params=_params(),
    )(g)


def _ungroup_w_in(g_parts):
    n = len(g_parts)

    def body(*refs):
        o_ref, g_ref = refs[n], refs[n + 1]
        at = 0
        for p in refs[:n]:
            g_ref[:, at:at + p.shape[1]] = p[...]
            at += p.shape[1]
        for k in range(N_DEV):
            lo, hi = _regrouped(IN_SHARD * k), _regrouped(IN_SHARD * k + IN_SHARD - 1)
            w0, w1 = lo // LANES * LANES, (hi // LANES + 1) * LANES
            o_ref[k] = _dot(g_ref[:, w0:w1], _selection(k, w1 - w0, w0, True)).astype(BF16)

    return pl.pallas_call(
        body, name="ungroup_w_in", out_shape=jax.ShapeDtypeStruct((N_DEV, D_MODEL, IN_SHARD_PAD), BF16),
        scratch_shapes=[pltpu.VMEM((D_MODEL, CAT_W), BF16)], compiler_params=_params(),
    )(*g_parts)


WEIGHT_NAMES = ("w_ada", "b_ada", "g_pre_mix", "g_post_mix", "w_in", "b_if", "conv_w", "conv_b", "attn_sinks",
                "mlstm_norm_w", "w_branch_attn", "w_branch_mlstm", "w_out", "g_pre_ffn", "g_post_ffn",
                "w_ffn_gate", "w_ffn_up", "w_ffn_down")


def kernel(x, c, positions, w_ada, b_ada, g_pre_mix, g_post_mix, w_in, b_if, conv_w, conv_b, attn_sinks, mlstm_norm_w, w_branch_attn, w_branch_mlstm, w_out, g_pre_ffn, g_post_ffn, w_ffn_gate, w_ffn_up, w_ffn_down, loss_target, m_w_ada, m_b_ada, m_g_pre_mix, m_g_post_mix, m_w_in, m_b_if, m_conv_w, m_conv_b, m_attn_sinks, m_mlstm_norm_w, m_w_branch_attn, m_w_branch_mlstm, m_w_out, m_g_pre_ffn, m_g_post_ffn, m_w_ffn_gate, m_w_ffn_up, m_w_ffn_down, v_w_ada, v_b_ada, v_g_pre_mix, v_g_post_mix, v_w_in, v_b_if, v_conv_w, v_conv_b, v_attn_sinks, v_mlstm_norm_w, v_w_branch_attn, v_w_branch_mlstm, v_w_out, v_g_pre_ffn, v_g_post_ffn, v_w_ffn_gate, v_w_ffn_up, v_w_ffn_down):
    given = dict(locals())
    W = {n: given[n][0] for n in WEIGHT_NAMES}
    M = {n: given["m_" + n][0] for n in WEIGHT_NAMES}
    V = {n: given["v_" + n][0] for n in WEIGHT_NAMES}
    me = 4 * lax.axis_index("x") + 2 * lax.axis_index("y") + lax.axis_index("c")

    ff_sh = D_FF // N_DEV
    ada_w = D_MODEL * 6 // N_DEV
    b_cols = lax.dynamic_slice(W["b_ada"], (me * ada_w,), (ada_w,)).reshape(1, ada_w)
    (g_in, g_conv), cg, mod_recv = _gather_front(
        [jnp.pad(W["w_in"], ((0, 0), (0, IN_SHARD_PAD - IN_SHARD))).astype(BF16), jnp.pad(W["conv_w"], ((0, 4), (0, 0)))],
        c.reshape(8, D_MODEL // 8), W["w_ada"], b_cols)
    cmat = cg.reshape(N_DEV, D_MODEL)
    mod = mod_recv[:, 0, :].reshape(6, D_MODEL)

    st_b = _exchange_start([W["w_branch_attn"].astype(BF16), W["w_branch_mlstm"].astype(BF16), W["w_out"].astype(BF16),
                            W["w_ffn_gate"].T.astype(BF16), W["w_ffn_up"].T.astype(BF16), W["w_ffn_down"].astype(BF16)],
                           True, mod_recv, "gather_rest_start")
    cols = lambda g: g.transpose(1, 0, 2).reshape(g.shape[1], N_DEV * g.shape[2])
    gains = jnp.stack([W["g_pre_mix"], W["g_post_mix"], W["g_pre_ffn"], W["g_post_ffn"]])
    xs, tgt = x[0], loss_target[0]
    t = _tables(mod, gains, cols(g_conv)[0:4], W["conv_b"], W["b_if"], W["mlstm_norm_w"], positions[0])
    vecs = t["vecs"]
    t["vecs"] = _tie(vecs, st_b["token"], "tie_fwd")
    w_cat = _regroup_w_in(g_in)
    a = _mixer_fwd(xs, t, W["attn_sinks"], w_cat)
    g_ba, g_bm, g_out, g_gate, g_up, g_down = _exchange_wait(st_b, a["ym"], "gather_rest_wait")
    w_ba, w_bm, w_out = cols(g_ba), cols(g_bm), g_out.reshape(D_MODEL, D_MODEL)
    b = _ffn_part(xs, tgt, t, a, w_ba, w_bm, w_out, g_gate.reshape(D_FF, D_MODEL), g_up.reshape(D_FF, D_MODEL),
                  g_down.reshape(D_FF, D_MODEL))

    st_f = _exchange_start([b["g_w_gate"].reshape(N_DEV, ff_sh, D_MODEL), b["g_w_up"].reshape(N_DEV, ff_sh, D_MODEL),
                            b["g_w_down"].reshape(N_DEV, ff_sh, D_MODEL)], False, b["dx1"], "scatter_ffn_start")
    t["vecs"] = _tie(vecs, st_f["token"], "tie_bwd")
    cm = _mixer_bwd(b["dx1"], t, a, b, W["attn_sinks"], w_ba, w_bm, w_out)
    pieces = lambda g, n: g.reshape(g.shape[0], N_DEV, n).transpose(1, 0, 2)
    st_m = _exchange_start([_ungroup_w_in(cm["g_w_cat"]), pieces(cm["g_w_ba"], 128), pieces(cm["g_w_bm"], 128),
                            cm["g_w_out"].reshape(N_DEV, D_MODEL // N_DEV, D_MODEL),
                            jnp.pad(pieces(cm["dcw"][0:4], 128), ((0, 0), (0, 4), (0, 0)))], False, cm["dcw"],
                           "scatter_mixer_start")
    r_gate, r_up, r_down = _exchange_wait(st_f, st_m["token"], "scatter_ffn_wait")
    grad_x, acc_p = _pre_bwd(cm["dproj"], xs, b["dx1"], _tie(vecs, st_m["token"], "tie_pre_bwd"), w_cat)
    small = _small_grads(acc_p, b, cm)
    loss = b["loss"]

    big_out = [{} for _ in range(4)]

    def put(n, res):
        for k in range(4):
            big_out[k][n] = res[k][None]

    put("w_ffn_down", _adamw_sum(r_down, W["w_ffn_down"], M["w_ffn_down"], V["w_ffn_down"], "adamw_w_ffn_down"))
    for n, r in (("w_ffn_gate", r_gate), ("w_ffn_up", r_up)):
        put(n, [o.T for o in _adamw_sum(r, W[n].T, M[n].T, V[n].T, "adamw_" + n)])

    sg = _small_exchange(_small_table({"b_ada": small["dmod"], "gains": small["dgains"], "conv_b": small["dconv_b"],
                                       "mlstm_norm_w": small["dnorm_w"], "b_if": small["db_if"],
                                       "attn_sinks": small["dsinks"], "loss": loss}), True, "gather_small")
    as_row = lambda a, n: jnp.pad(a, (0, SMALL_AT[n][3] * SMALL_AT[n][1] - a.shape[0]))[None]
    small_res = _adamw_small(sg, {n: [as_row(d[n], n) for d in (W, M, V)] for n in SMALL_AT})
    small_out = [{n: small_res[n][k][:, 0:W[n].shape[0]] for n in SMALL_AT} for k in range(4)]
    dmod_cols = lax.dynamic_slice(sg[:, 0:6, :].reshape(N_DEV, 6 * D_MODEL), (0, me * ada_w), (N_DEV, ada_w))
    ada_out = _adamw_w_ada(cmat, dmod_cols, W["w_ada"], M["w_ada"], V["w_ada"])

    r_in, r_ba, r_bm, r_out, r_conv = _exchange_wait(st_m, ada_out[1], "scatter_mixer_wait")
    pad4 = lambda v: jnp.pad(v, ((0, 4), (0, 0)))
    res_ba, res_bm, res_out, res_conv = _adamw_sum_many(
        [(r, W[n], M[n], V[n]) for n, r in (("w_branch_attn", r_ba), ("w_branch_mlstm", r_bm), ("w_out", r_out))]
        + [(r_conv, pad4(W["conv_w"]), pad4(M["conv_w"]), pad4(V["conv_w"]))], "adamw_mixer_small")
    put("w_branch_attn", res_ba)
    put("w_branch_mlstm", res_bm)
    put("w_out", res_out)
    put("conv_w", [o[0:4] for o in res_conv])
    g = _sum8(r_in, "sum_w_in")[:, 0:IN_SHARD].T
    put("w_in", [o.T for o in [g] + list(_adamw_plain(g, W["w_in"].T, M["w_in"].T, V["w_in"].T, "adamw_w_in"))])

    outs = [small_res["loss"][0, 0], grad_x[None]]
    for k in range(4):
        for n in WEIGHT_NAMES:
            if n == "w_ada":
                outs.append(ada_out[k][None])
            elif n in big_out[k]:
                outs.append(big_out[k][n])
            else:
                outs.append(small_out[k][n])
    return tuple(outs)
```

```python
import jax
import jax.numpy as jnp
from jax import lax
from jax.experimental import pallas as pl
from jax.experimental.pallas import tpu as pltpu

F32 = jnp.float32
BF16 = jnp.bfloat16

N_DEV = 8
D_MODEL = 1024
D_FF = 2816
N_Q_HEADS = 8
HEAD_DIM = 64
ATTN_BLOCK = 128
ROPE_THETA = 10000.0
MLSTM_HEADS = 4
MLSTM_HEAD_DIM = 128
MLSTM_CHUNK = 128
NORM_EPS = 1e-6
ADAM_LR = 0.001
ADAM_B1 = 0.9
ADAM_B2 = 0.999
ADAM_EPS = 1e-08
ADAM_WD = 0.01
ADAM_STEP = 10

ROW_TILE = 256
WIDE_TILE = 512
LANES = 128
NEG = -1e30
VMEM_LIMIT = 56 * 1024 * 1024

A_W = 768
M_W = 2048
IF_W = 128
G_W = 2048
CAT_W = A_W + M_W + IF_W + G_W

R_SHIFT_M, R_SCALE_M, R_GATE_M, R_SHIFT_F, R_SCALE_F, R_GATE_F = 0, 1, 2, 3, 4, 5
R_G_PRE_MIX, R_G_POST_MIX, R_G_PRE_FFN, R_G_POST_FFN = 6, 7, 8, 9


def _dot(a, b):
    return jnp.dot(a, b, preferred_element_type=F32)


def _dot_nt(a, b):
    return lax.dot_general(a, b, (((1,), (1,)), ((), ())), preferred_element_type=F32)


def _dot_tn(a, b):
    return lax.dot_general(a, b, (((0,), (0,)), ((), ())), preferred_element_type=F32)


def _recip(x):
    return 1.0 / x


def _sigmoid(x):
    return _recip(1.0 + jnp.exp(-x))


def _colsum(x):
    return jnp.sum(x, axis=0, keepdims=True)


def _rowmean(x):
    return jnp.mean(x, axis=-1, keepdims=True)


def _params(sem=None, vmem=VMEM_LIMIT):
    kw = dict(vmem_limit_bytes=vmem)
    if sem is not None:
        kw["dimension_semantics"] = sem
    return pltpu.CompilerParams(**kw)


def _full(shape):
    nd = len(shape)
    return pl.BlockSpec(shape, lambda *_: (0,) * nd)


def _pre_proj(x, vecs, w_cat):
    S = x.shape[0]
    tm = WIDE_TILE

    def body(x_ref, v_ref, w_ref, h_ref, pa_ref, pm_ref, pif_ref, pg_ref):
        xv = x_ref[...]
        r = lax.rsqrt(_rowmean(xv * xv) + NORM_EPS)
        h = (xv * r * v_ref[R_G_PRE_MIX:R_G_PRE_MIX + 1, :]) * (1.0 + v_ref[R_SCALE_M:R_SCALE_M + 1, :]) \
            + v_ref[R_SHIFT_M:R_SHIFT_M + 1, :]
        hb = h.astype(BF16)
        h_ref[...] = hb
        pa_ref[...] = _dot(hb, w_ref[:, 0:A_W])
        pm_ref[...] = _dot(hb, w_ref[:, A_W:A_W + M_W])
        pif_ref[...] = _dot(hb, w_ref[:, A_W + M_W:A_W + M_W + IF_W])
        pg_ref[...] = _dot(hb, w_ref[:, A_W + M_W + IF_W:CAT_W]).astype(BF16)

    row = lambda w: pl.BlockSpec((tm, w), lambda i: (i, 0))
    return pl.pallas_call(
        body, name="pre_proj", grid=(S // tm,),
        in_specs=[row(D_MODEL), _full(vecs.shape), _full(w_cat.shape)],
        out_specs=[row(D_MODEL), row(A_W), row(M_W), row(IF_W), row(G_W)],
        out_shape=[jax.ShapeDtypeStruct((S, D_MODEL), BF16), jax.ShapeDtypeStruct((S, A_W), F32),
                   jax.ShapeDtypeStruct((S, M_W), F32), jax.ShapeDtypeStruct((S, IF_W), F32),
                   jax.ShapeDtypeStruct((S, G_W), BF16)],
        compiler_params=_params(("parallel",)),
    )(x, vecs, w_cat)


def _mix_fwd(x, ya, ym, pg, vecs, w_ba, w_bm, w_out):
    S = x.shape[0]
    tm = WIDE_TILE

    def body(x_ref, ya_ref, ym_ref, pg_ref, v_ref, wba_ref, wbm_ref, wout_ref,
             x1_ref, merged_ref, mix_ref, pa_ref, pb_ref):
        pa = _dot(ya_ref[...], wba_ref[...])
        pb = _dot(ym_ref[...], wbm_ref[...])
        merged = _sigmoid(pg_ref[:, 0:D_MODEL].astype(F32)) * pa + _sigmoid(pg_ref[:, D_MODEL:G_W].astype(F32)) * pb
        mb = merged.astype(BF16)
        mix = _dot(mb, wout_ref[...])
        r = lax.rsqrt(_rowmean(mix * mix) + NORM_EPS)
        x1_ref[...] = x_ref[...] + v_ref[R_GATE_M:R_GATE_M + 1, :] * (mix * r * v_ref[R_G_POST_MIX:R_G_POST_MIX + 1, :])
        merged_ref[...] = mb
        mix_ref[...] = mix.astype(BF16)
        pa_ref[...] = pa.astype(BF16)
        pb_ref[...] = pb.astype(BF16)

    row = lambda w: pl.BlockSpec((tm, w), lambda i: (i, 0))
    sd = lambda w, dt: jax.ShapeDtypeStruct((S, w), dt)
    return pl.pallas_call(
        body, name="mix_fwd", grid=(S // tm,),
        in_specs=[row(D_MODEL), row(512), row(512), row(G_W), _full(vecs.shape), _full(w_ba.shape),
                  _full(w_bm.shape), _full(w_out.shape)],
        out_specs=[row(D_MODEL)] * 5,
        out_shape=[sd(D_MODEL, F32), sd(D_MODEL, BF16), sd(D_MODEL, BF16), sd(D_MODEL, BF16), sd(D_MODEL, BF16)],
        compiler_params=_params(("parallel",)),
    )(x, ya, ym, pg, vecs, w_ba, w_bm, w_out)


def _ffn_fwd_bwd(x1, tgt, vecs, w_gate, w_up, w_down):
    S = x1.shape[0]
    tm = ROW_TILE

    def body(x1_ref, tgt_ref, v_ref, wg_hbm, wu_hbm, wd_hbm,
             dx1_ref, h2_ref, hid_ref, da_ref, du_ref, dff_ref, acc_ref, loss_ref,
             wg, wu, wd, sem):
        i = pl.program_id(0)

        @pl.when(i == 0)
        def _():
            cps = [pltpu.make_async_copy(wg_hbm, wg, sem.at[0]), pltpu.make_async_copy(wu_hbm, wu, sem.at[1]),
                   pltpu.make_async_copy(wd_hbm, wd, sem.at[2])]
            for cp in cps:
                cp.start()
            for cp in cps:
                cp.wait()
            acc_ref[...] = jnp.zeros_like(acc_ref)
            loss_ref[...] = jnp.zeros_like(loss_ref)

        vrow = lambda r: v_ref[r:r + 1, :]
        x1v = x1_ref[...]
        r3 = lax.rsqrt(_rowmean(x1v * x1v) + NORM_EPS)
        x1hat = x1v * r3
        xn3 = x1hat * vrow(R_G_PRE_FFN)
        h2b = (xn3 * (1.0 + vrow(R_SCALE_F)) + vrow(R_SHIFT_F)).astype(BF16)
        h2_ref[...] = h2b
        a = _dot_nt(h2b, wg[...])
        u = _dot_nt(h2b, wu[...])
        sg = _sigmoid(a)
        sil = a * sg
        hidb = (sil * u).astype(BF16)
        hid_ref[...] = hidb
        ff = _dot(hidb, wd[...])
        r4 = lax.rsqrt(_rowmean(ff * ff) + NORM_EPS)
        ffhat = ff * r4
        n4 = ffhat * vrow(R_G_POST_FFN)
        err = x1v + vrow(R_GATE_F) * n4 - tgt_ref[...]
        loss_ref[...] += jnp.sum(err * err) * (0.5 / D_MODEL)
        dy = err * (1.0 / D_MODEL)
        acc_ref[0:1, :] += _colsum(dy * n4)
        dn4 = dy * vrow(R_GATE_F)
        acc_ref[1:2, :] += _colsum(dn4 * ffhat)
        dffhat = dn4 * vrow(R_G_POST_FFN)
        dffb = (r4 * (dffhat - ffhat * _rowmean(dffhat * ffhat))).astype(BF16)
        dff_ref[...] = dffb
        dhid = _dot_nt(dffb, wd[...])
        dub = (dhid * sil).astype(BF16)
        dab = (dhid * u * (sg * (1.0 + a * (1.0 - sg)))).astype(BF16)
        da_ref[...] = dab
        du_ref[...] = dub
        dh2 = _dot(dab, wg[...]) + _dot(dub, wu[...])
        acc_ref[2:3, :] += _colsum(dh2 * xn3)
        acc_ref[3:4, :] += _colsum(dh2)
        dxn3 = dh2 * (1.0 + vrow(R_SCALE_F))
        acc_ref[4:5, :] += _colsum(dxn3 * x1hat)
        dx1hat = dxn3 * vrow(R_G_PRE_FFN)
        dx1_ref[...] = dy + r3 * (dx1hat - x1hat * _rowmean(dx1hat * x1hat))

    row = lambda w: pl.BlockSpec((tm, w), lambda i: (i, 0))
    sd = lambda w, dt: jax.ShapeDtypeStruct((S, w), dt)
    anyspec = pl.BlockSpec(memory_space=pl.ANY)
    return pl.pallas_call(
        body, name="ffn_fwd_bwd", grid=(S // tm,),
        in_specs=[row(D_MODEL), row(D_MODEL), _full(vecs.shape), anyspec, anyspec, anyspec],
        out_specs=[row(D_MODEL), row(D_MODEL), row(D_FF), row(D_FF), row(D_FF), row(D_MODEL),
                   _full((8, D_MODEL)), _full((8, LANES))],
        out_shape=[sd(D_MODEL, F32), sd(D_MODEL, BF16), sd(D_FF, BF16), sd(D_FF, BF16), sd(D_FF, BF16),
                   sd(D_MODEL, BF16), jax.ShapeDtypeStruct((8, D_MODEL), F32), jax.ShapeDtypeStruct((8, LANES), F32)],
        scratch_shapes=[pltpu.VMEM(w_gate.shape, BF16), pltpu.VMEM(w_up.shape, BF16), pltpu.VMEM(w_down.shape, BF16),
                        pltpu.SemaphoreType.DMA((3,))],
        compiler_params=_params(("arbitrary",)),
    )(x1, tgt, vecs, w_gate, w_up, w_down)


def _mix_bwd(dx1, mix, pa, pb, pg, vecs, w_ba, w_bm, w_out):
    S = dx1.shape[0]
    tm = WIDE_TILE

    def body(dx1_ref, mix_ref, pa_ref, pb_ref, pg_ref, v_ref, wba_ref, wbm_ref, wout_ref,
             dmix_ref, dpa_ref, dpb_ref, dg_ref, dya_ref, dym_ref, acc_ref):
        i = pl.program_id(0)

        @pl.when(i == 0)
        def _():
            acc_ref[...] = jnp.zeros_like(acc_ref)

        vrow = lambda r: v_ref[r:r + 1, :]
        dx1v = dx1_ref[...]
        mix = mix_ref[...].astype(F32)
        r2 = lax.rsqrt(_rowmean(mix * mix) + NORM_EPS)
        mixhat = mix * r2
        acc_ref[0:1, :] += _colsum(dx1v * (mixhat * vrow(R_G_POST_MIX)))
        dn2 = dx1v * vrow(R_GATE_M)
        acc_ref[1:2, :] += _colsum(dn2 * mixhat)
        dmixhat = dn2 * vrow(R_G_POST_MIX)
        dmixb = (r2 * (dmixhat - mixhat * _rowmean(dmixhat * mixhat))).astype(BF16)
        dmix_ref[...] = dmixb
        dmerged = _dot_nt(dmixb, wout_ref[...])
        sa = _sigmoid(pg_ref[:, 0:D_MODEL].astype(F32))
        sm = _sigmoid(pg_ref[:, D_MODEL:G_W].astype(F32))
        dpab = (dmerged * sa).astype(BF16)
        dpbb = (dmerged * sm).astype(BF16)
        dpa_ref[...] = dpab
        dpb_ref[...] = dpbb
        dg_ref[:, 0:D_MODEL] = (dmerged * pa_ref[...].astype(F32) * (sa * (1.0 - sa))).astype(BF16)
        dg_ref[:, D_MODEL:G_W] = (dmerged * pb_ref[...].astype(F32) * (sm * (1.0 - sm))).astype(BF16)
        dya_ref[...] = _dot_nt(dpab, wba_ref[...]).astype(BF16)
        dym_ref[...] = _dot_nt(dpbb, wbm_ref[...])

    row = lambda w: pl.BlockSpec((tm, w), lambda i: (i, 0))
    sd = lambda w, dt: jax.ShapeDtypeStruct((S, w), dt)
    return pl.pallas_call(
        body, name="mix_bwd", grid=(S // tm,),
        in_specs=[row(D_MODEL), row(D_MODEL), row(D_MODEL), row(D_MODEL), row(G_W), _full(vecs.shape),
                  _full(w_ba.shape), _full(w_bm.shape), _full(w_out.shape)],
        out_specs=[row(D_MODEL), row(D_MODEL), row(D_MODEL), row(G_W), row(512), row(512), _full((8, D_MODEL))],
        out_shape=[sd(D_MODEL, BF16), sd(D_MODEL, BF16), sd(D_MODEL, BF16), sd(G_W, BF16), sd(512, BF16), sd(512, F32),
                   jax.ShapeDtypeStruct((8, D_MODEL), F32)],
        compiler_params=_params(("arbitrary",)),
    )(dx1, mix, pa, pb, pg, vecs, w_ba, w_bm, w_out)


def _pre_bwd(pieces, x, dx1, vecs, w_cat):
    S = x.shape[0]
    tm = WIDE_TILE
    n = len(pieces)
    starts = [sum(p.shape[1] for p in pieces[:k]) for k in range(n + 1)]

    def body(*refs):
        p_refs = refs[:n]
        x_ref, dx1_ref, v_ref, w_ref, dx_ref, acc_ref = refs[n:]
        i = pl.program_id(0)

        @pl.when(i == 0)
        def _():
            acc_ref[...] = jnp.zeros_like(acc_ref)

        vrow = lambda r: v_ref[r:r + 1, :]
        dh = _dot_nt(p_refs[0][...], w_ref[:, starts[0]:starts[1]])
        for k in range(1, n):
            dh = dh + _dot_nt(p_refs[k][...], w_ref[:, starts[k]:starts[k + 1]])
        xv = x_ref[...]
        r1 = lax.rsqrt(_rowmean(xv * xv) + NORM_EPS)
        xhat = xv * r1
        acc_ref[0:1, :] += _colsum(dh * (xhat * vrow(R_G_PRE_MIX)))
        acc_ref[1:2, :] += _colsum(dh)
        dxn = dh * (1.0 + vrow(R_SCALE_M))
        acc_ref[2:3, :] += _colsum(dxn * xhat)
        dxhat = dxn * vrow(R_G_PRE_MIX)
        dx_ref[...] = dx1_ref[...] + r1 * (dxhat - xhat * _rowmean(dxhat * xhat))

    row = lambda w: pl.BlockSpec((tm, w), lambda i: (i, 0))
    return pl.pallas_call(
        body, name="pre_bwd", grid=(S // tm,),
        in_specs=[row(p.shape[1]) for p in pieces] + [row(D_MODEL), row(D_MODEL), _full(vecs.shape), _full(w_cat.shape)],
        out_specs=[row(D_MODEL), _full((8, D_MODEL))],
        out_shape=[jax.ShapeDtypeStruct((S, D_MODEL), F32), jax.ShapeDtypeStruct((8, D_MODEL), F32)],
        compiler_params=_params(("arbitrary",)),
    )(*pieces, x, dx1, vecs, w_cat)


def _matmul_tn(a, b, tn, name, ts=1024):
    S, K = a.shape
    N = b.shape[1]
    n_s = S // ts

    def body(a_ref, b_ref, o_ref, acc_ref):
        s = pl.program_id(1)

        @pl.when(s == 0)
        def _():
            acc_ref[...] = jnp.zeros_like(acc_ref)

        acc_ref[...] += _dot_tn(a_ref[...], b_ref[...])

        @pl.when(s == n_s - 1)
        def _():
            o_ref[...] = acc_ref[...].astype(BF16)

    return pl.pallas_call(
        body, name=name, grid=(N // tn, n_s),
        in_specs=[pl.BlockSpec((ts, K), lambda j, s: (s, 0)), pl.BlockSpec((ts, tn), lambda j, s: (s, j))],
        out_specs=pl.BlockSpec((K, tn), lambda j, s: (0, j)),
        out_shape=jax.ShapeDtypeStruct((K, N), BF16),
        scratch_shapes=[pltpu.VMEM((K, tn), F32)],
        compiler_params=_params(("parallel", "arbitrary")),
    )(a, b)


def _rope_swap(t):
    lane = lax.broadcasted_iota(jnp.int32, t.shape, 1)
    first = (lane & (HEAD_DIM - 1)) < (HEAD_DIM // 2)
    return jnp.where(first, pltpu.roll(t, LANES - HEAD_DIM // 2, 1), pltpu.roll(t, HEAD_DIM // 2, 1))


def _rope(t, cos, sin_signed):
    return t * cos + _rope_swap(t) * sin_signed


def _rope_t(d, cos, sin_signed):
    return d * cos + _rope_swap(d * sin_signed)


def _to_kv_lanes(chunk, p, h):
    lane = lax.broadcasted_iota(jnp.int32, chunk.shape, 1)
    src = chunk if p == h else pltpu.roll(chunk, HEAD_DIM, 1)
    return jnp.where((lane >> 6) == h, src, jnp.zeros_like(src))


def _from_kv_lanes(o_a, o_b, h):
    lane = lax.broadcasted_iota(jnp.int32, o_a.shape, 1)
    a = o_a if h == 0 else pltpu.roll(o_a, HEAD_DIM, 1)
    b = o_b if h == 1 else pltpu.roll(o_b, HEAD_DIM, 1)
    return jnp.where(lane < HEAD_DIM, a, b)


def _band_bias(n):
    blk = ATTN_BLOCK
    qi = lax.broadcasted_iota(jnp.int32, (blk, 2 * blk), 0)
    kj = lax.broadcasted_iota(jnp.int32, (blk, 2 * blk), 1)
    seen = (kj > qi) & (kj <= qi + blk) & ((n > 0) | (kj >= blk))
    return jnp.concatenate([jnp.where(seen, 0.0, NEG)] * N_Q_HEADS, axis=0)


def _stack_heads(chunks, h, dtype):
    parts = []
    for g in range(4):
        j = 4 * h + g
        parts.append(_to_kv_lanes(chunks[j // 2], j % 2, h))
    return jnp.concatenate(parts, axis=0).astype(dtype)


def _fused_call(parts, name, n_steps):
    counts = [(len(p["in_specs"]), len(p["out_specs"]), len(p["scratch"])) for p in parts]
    n_in, n_out = sum(c[0] for c in counts), sum(c[1] for c in counts)

    def kernel_fn(*refs):
        i = pl.program_id(0)
        groups, a, b, c = [], 0, n_in, n_in + n_out
        for ci, co, cs in counts:
            groups.append(refs[a:a + ci] + refs[b:b + co] + refs[c:c + cs])
            a, b, c = a + ci, b + co, c + cs
        for p, g in zip(parts, groups):
            p["init"](i, *g)
        for p, g in zip(parts, groups):
            p["body"](i, *g)

    flat = lambda key: [v for p in parts for v in p[key]]
    res = pl.pallas_call(
        kernel_fn, name=name, grid=(n_steps,), in_specs=flat("in_specs"), out_specs=flat("out_specs"),
        out_shape=flat("out_shape"), scratch_shapes=flat("scratch"), compiler_params=_params(("arbitrary",)),
    )(*flat("operands"))
    out, pos = [], 0
    for _, co, _ in counts:
        out.append(res[pos:pos + co])
        pos += co
    return out


def _attn_fwd_part(pa, cos, sin, sinks):
    S = pa.shape[0]
    blk = ATTN_BLOCK
    nb = S // blk

    def body(n, sink_ref, cur_ref, prev_ref, cos_ref, sin_ref, cosp_ref, sinp_ref,
             ya_ref, qr_ref, kr_ref, vb_ref, lse_ref):
        cos_c, sin_c = cos_ref[...], sin_ref[...]
        qch = [_rope(cur_ref[:, c * LANES:(c + 1) * LANES], cos_c, sin_c) * (HEAD_DIM ** -0.5) for c in range(4)]
        for c in range(4):
            qr_ref[:, c * LANES:(c + 1) * LANES] = qch[c].astype(BF16)
        k_cur = _rope(cur_ref[:, 512:640], cos_c, sin_c).astype(BF16)
        k_prev = _rope(prev_ref[:, 0:LANES], cosp_ref[...], sinp_ref[...]).astype(BF16)
        v_cur = cur_ref[:, 640:768].astype(BF16)
        v_prev = prev_ref[:, LANES:2 * LANES].astype(BF16)
        kr_ref[...] = k_cur
        vb_ref[...] = v_cur
        K = jnp.concatenate([k_prev, k_cur], axis=0)
        V = jnp.concatenate([v_prev, v_cur], axis=0)
        lane = lax.broadcasted_iota(jnp.int32, (blk, LANES), 1)
        s = jnp.concatenate([_dot_nt(_stack_heads(qch, h, BF16), K) for h in range(2)], axis=0)
        s = s + _band_bias(n)
        rowmax = jnp.max(s, axis=1, keepdims=True)
        hd = lambda x, j: x[j * blk:(j + 1) * blk]
        m = jnp.concatenate([jnp.maximum(hd(rowmax, j), sink_ref[j]) for j in range(N_Q_HEADS)], axis=0)
        p = jnp.exp(s - m)
        pb = p.astype(BF16)
        den = _rowsum_mxu(pb) + jnp.concatenate([jnp.exp(sink_ref[j] - hd(m, j)) for j in range(N_Q_HEADS)], axis=0)
        o = jnp.concatenate([_dot(pb[4 * h * blk:4 * (h + 1) * blk], V) for h in range(2)], axis=0) * _recip(den)
        lse = m + jnp.log(den)
        outs = [o[j * blk:(j + 1) * blk, :] for j in range(N_Q_HEADS)]
        lse_tile = jnp.zeros((blk, LANES), F32)
        for j in range(N_Q_HEADS):
            lse_tile = jnp.where(lane == j, lse[j * blk:(j + 1) * blk, :], lse_tile)
        for c in range(4):
            ya_ref[:, c * LANES:(c + 1) * LANES] = _from_kv_lanes(outs[2 * c], outs[2 * c + 1], c // 2).astype(BF16)
        lse_ref[...] = lse_tile

    prev = lambda n: jnp.maximum(n - 1, 0)
    sd = lambda w, dt: jax.ShapeDtypeStruct((S, w), dt)
    return dict(
        init=lambda n, *refs: None, body=body, scratch=[], operands=[sinks, pa, pa, cos, sin, cos, sin],
        in_specs=[pl.BlockSpec(memory_space=pltpu.SMEM),
                  pl.BlockSpec((blk, A_W), lambda n: (n, 0)),
                  pl.BlockSpec((blk, 256), lambda n: (prev(n), 2)),
                  pl.BlockSpec((blk, LANES), lambda n: (n, 0)), pl.BlockSpec((blk, LANES), lambda n: (n, 0)),
                  pl.BlockSpec((blk, LANES), lambda n: (prev(n), 0)), pl.BlockSpec((blk, LANES), lambda n: (prev(n), 0))],
        out_specs=[pl.BlockSpec((blk, 512), lambda n: (n, 0)), pl.BlockSpec((blk, 512), lambda n: (n, 0)),
                   pl.BlockSpec((blk, LANES), lambda n: (n, 0)), pl.BlockSpec((blk, LANES), lambda n: (n, 0)),
                   pl.BlockSpec((blk, LANES), lambda n: (n, 0))],
        out_shape=[sd(512, BF16), sd(512, BF16), sd(LANES, BF16), sd(LANES, BF16), sd(LANES, F32)])


def _attn_bwd_part(dya, qr, kr, vb, lse, cos, sin, sinks):
    S = dya.shape[0]
    blk = ATTN_BLOCK
    nb = S // blk

    def init(n, sink_ref, dya_ref, qr_ref, kc_ref, kp_ref, vc_ref, vp_ref, lse_ref, cos_ref, sin_ref, cosp_ref, sinp_ref,
             dqkv_ref, last_ref, dsink_ref, ck, cv, cq):
        @pl.when(n == 0)
        def _():
            ck[...] = jnp.zeros_like(ck)
            cv[...] = jnp.zeros_like(cv)
            cq[...] = jnp.zeros_like(cq)
            dsink_ref[...] = jnp.zeros_like(dsink_ref)

    def body(n, sink_ref, dya_ref, qr_ref, kc_ref, kp_ref, vc_ref, vp_ref, lse_ref, cos_ref, sin_ref, cosp_ref, sinp_ref,
             dqkv_ref, last_ref, dsink_ref, ck, cv, cq):
        K = jnp.concatenate([kp_ref[...], kc_ref[...]], axis=0)
        V = jnp.concatenate([vp_ref[...], vc_ref[...]], axis=0)
        qch = [qr_ref[:, c * LANES:(c + 1) * LANES] for c in range(4)]
        dch = [dya_ref[:, c * LANES:(c + 1) * LANES].astype(F32) for c in range(4)]
        lse_tile = lse_ref[...]
        lane8 = lax.broadcasted_iota(jnp.int32, (8, LANES), 1)
        grp = lambda x, h: x[4 * h * blk:4 * (h + 1) * blk]
        qs = jnp.concatenate([_stack_heads(qch, h, BF16) for h in range(2)], axis=0)
        dos = jnp.concatenate([_stack_heads(dch, h, BF16) for h in range(2)], axis=0)
        lse_col = jnp.concatenate([lse_tile[:, j:j + 1] for j in range(N_Q_HEADS)], axis=0)
        s = jnp.concatenate([_dot_nt(grp(qs, h), K) for h in range(2)], axis=0)
        p = jnp.exp(s + _band_bias(n) - lse_col)
        dp = jnp.concatenate([_dot_nt(grp(dos, h), V) for h in range(2)], axis=0)
        delta = jnp.sum(p * dp, axis=1, keepdims=True)
        dsb = (p * (dp - delta)).astype(BF16)
        pb = p.astype(BF16)
        dq = jnp.concatenate([_dot(grp(dsb, h), K) for h in range(2)], axis=0)
        dk_acc = _dot_tn(grp(dsb, 0), grp(qs, 0)) + _dot_tn(grp(dsb, 1), grp(qs, 1))
        dv_acc = _dot_tn(grp(pb, 0), grp(dos, 0)) + _dot_tn(grp(pb, 1), grp(dos, 1))
        dqs = [dq[j * blk:(j + 1) * blk, :] for j in range(N_Q_HEADS)]
        dsink = jnp.zeros((8, LANES), F32)
        for j in range(N_Q_HEADS):
            rows = slice(j * blk, (j + 1) * blk)
            ps_delta = jnp.exp(sink_ref[j] - lse_col[rows]) * delta[rows]
            dsink = jnp.where(lane8 == j, dsink - jnp.sum(ps_delta), dsink)
        dsink_ref[...] += dsink
        cos_c, sin_c = cos_ref[...], sin_ref[...]
        dqkv_ref[:, 0:512] = cq[...]
        dqkv_ref[:, 512:640] = _rope_t(dk_acc[0:blk, :] + ck[...], cosp_ref[...], sinp_ref[...]).astype(BF16)
        dqkv_ref[:, 640:768] = (dv_acc[0:blk, :] + cv[...]).astype(BF16)
        ck[...] = dk_acc[blk:2 * blk, :]
        cv[...] = dv_acc[blk:2 * blk, :]
        for c in range(4):
            dqc = _from_kv_lanes(dqs[2 * c], dqs[2 * c + 1], c // 2) * (HEAD_DIM ** -0.5)
            dq_c = _rope_t(dqc, cos_c, sin_c).astype(BF16)
            cq[:, c * LANES:(c + 1) * LANES] = dq_c
            last_ref[:, c * LANES:(c + 1) * LANES] = dq_c
        last_ref[:, 512:640] = _rope_t(dk_acc[blk:2 * blk, :], cos_c, sin_c).astype(BF16)
        last_ref[:, 640:768] = dv_acc[blk:2 * blk, :].astype(BF16)

    prev = lambda n: jnp.maximum(n - 1, 0)
    same = lambda n: n
    bs = lambda w, f: pl.BlockSpec((blk, w), lambda n: (f(n), 0))
    return dict(
        init=init, body=body, operands=[sinks, dya, qr, kr, kr, vb, vb, lse, cos, sin, cos, sin],
        in_specs=[pl.BlockSpec(memory_space=pltpu.SMEM),
                  bs(512, same), bs(512, same), bs(LANES, same), bs(LANES, prev), bs(LANES, same), bs(LANES, prev),
                  bs(LANES, same), bs(LANES, same), bs(LANES, same), bs(LANES, prev), bs(LANES, prev)],
        out_specs=[bs(A_W, prev), _full((blk, A_W)), _full((8, LANES))],
        out_shape=[jax.ShapeDtypeStruct((S, A_W), BF16), jax.ShapeDtypeStruct((blk, A_W), BF16),
                   jax.ShapeDtypeStruct((8, LANES), F32)],
        scratch=[pltpu.VMEM((blk, LANES), F32), pltpu.VMEM((blk, LANES), F32), pltpu.VMEM((blk, 512), BF16)])


def _split3(x):
    hi = x.astype(BF16)
    r1 = x - hi.astype(F32)
    mid = r1.astype(BF16)
    lo = (r1 - mid.astype(F32)).astype(BF16)
    return hi, mid, lo


def _tri_matmul(tri_b, x):
    hi, mid, lo = _split3(x)
    return _dot(tri_b, hi) + _dot(tri_b, mid) + _dot(tri_b, lo)


def _log_sigmoid(x):
    return jnp.minimum(x, 0.0) - jnp.log(1.0 + jnp.exp(-jnp.abs(x)))


def _shift_rows(cur, seam, k, down):
    L = cur.shape[0]
    row8 = lax.broadcasted_iota(jnp.int32, seam.shape, 0)
    if down:
        mixed = jnp.concatenate([cur[:L - 8], jnp.where(row8 >= 8 - k, seam, cur[L - 8:])], axis=0)
        return pltpu.roll(mixed, k, 0)
    mixed = jnp.concatenate([jnp.where(row8 < k, seam, cur[:8]), cur[8:]], axis=0)
    return pltpu.roll(mixed, L - k, 0)


def _conv_fwd(cur, tail, cw_ref):
    z = cw_ref[4:5, :]
    for k in range(3, 0, -1):
        z = z + _shift_rows(cur, tail, k, True) * cw_ref[3 - k:4 - k, :]
    return z + cur * cw_ref[3:4, :]


def _rowsum_mxu(x, two_pass=False):
    ones = jnp.ones((x.shape[1], LANES), BF16)
    hi = x.astype(BF16)
    s = _dot(hi, ones)
    if two_pass:
        s = s + _dot((x - hi.astype(F32)).astype(BF16), ones)
    return s


def _stack(f):
    return jnp.concatenate([f(h) for h in range(MLSTM_HEADS)], axis=0)


def _head(x, h):
    L = x.shape[0] // MLSTM_HEADS
    return x[h * L:(h + 1) * L]


def _mlstm_heads_fwd(qk, cur_ref, gt, b_all, c_prev, nmv, tri, eye):
    L = qk.shape[0]
    HD = MLSTM_HEAD_DIM
    W4 = MLSTM_HEADS * HD
    col2row = lambda x: jnp.sum(jnp.where(eye, x, 0.0), axis=0, keepdims=True)
    b_col = _stack(lambda h: b_all[:, 4 + h:5 + h])
    i_col = _stack(lambda h: gt[:, h:h + 1])
    b_row = _stack(lambda h: jnp.broadcast_to(col2row(b_all[:, 4 + h:5 + h]), (L, L)))
    i_row = _stack(lambda h: jnp.broadcast_to(col2row(gt[:, h:h + 1]), (L, L)))
    bl = _stack(lambda h: jnp.broadcast_to(b_all[L - 1:L, 4 + h:5 + h], (L, 1)))
    m_prev = _stack(lambda h: jnp.broadcast_to(nmv[4 + h:5 + h, 0:1], (L, 1)))
    n_prev = _stack(lambda h: jnp.broadcast_to(nmv[h:h + 1, :], (L, HD)))
    tri4 = jnp.concatenate([tri] * MLSTM_HEADS, axis=0)
    Dm = jnp.where(tri4, b_col - b_row + i_row, NEG)
    inter = b_col + m_prev
    m_t = jnp.maximum(inter, jnp.max(Dm, axis=1, keepdims=True))
    W = jnp.exp(Dm - m_t)
    e_t = jnp.exp(inter - m_t)
    q = _stack(lambda h: qk[:, h * HD:(h + 1) * HD])
    k = _stack(lambda h: qk[:, W4 + h * HD:W4 + (h + 1) * HD]) * (HD ** -0.5)
    v = _stack(lambda h: cur_ref[:, 2 * W4 + h * HD:2 * W4 + (h + 1) * HD])
    qb, kb, vb = q.astype(BF16), k.astype(BF16), v.astype(BF16)
    Sc = _stack(lambda h: _dot_nt(_head(qb, h), _head(kb, h))) * W
    Scb = Sc.astype(BF16)
    cb = [c.astype(BF16) for c in c_prev]
    P1 = _stack(lambda h: _dot(_head(qb, h), cb[h]))
    num = _stack(lambda h: _dot(_head(Scb, h), _head(vb, h))) + e_t * P1
    qn = _rowsum_mxu(q * n_prev)
    den = _rowsum_mxu(Scb) + e_t * qn
    floor = jnp.broadcast_to(jnp.exp(-m_t), den.shape)
    inv_g = _recip(jnp.maximum(jnp.abs(den), floor))
    hv = num * inv_g
    a_col = bl - b_col + i_col
    a_max = _stack(lambda h: jnp.broadcast_to(jnp.max(_head(a_col, h), axis=0, keepdims=True), (L, 1)))
    m_new = jnp.maximum(bl + m_prev, a_max)
    dec = jnp.exp(bl + m_prev - m_new)
    u_col = jnp.exp(a_col - m_new)
    return dict(W=W, e_t=e_t, q=q, k=k, v=v, qb=qb, kb=kb, vb=vb, cb=cb, Sc=Sc, Scb=Scb, P1=P1, qn=qn, den=den,
                floor=floor, inv_g=inv_g, hv=hv, n_prev=n_prev, m_new=m_new, dec=dec, u_col=u_col)


def _mlstm_fwd_part(pm, pif, cw, sv):
    S = pm.shape[0]
    L = MLSTM_CHUNK
    nc = S // L
    HD = MLSTM_HEAD_DIM
    W4 = MLSTM_HEADS * HD

    def init(c, cur_ref, pif_ref, cw_ref, sv_ref, ym_ref, z_ref, cst_ref, nst_ref, C, nm, tail):
        @pl.when(c == 0)
        def _():
            C[...] = jnp.zeros_like(C)
            nm[...] = jnp.zeros_like(nm)
            tail[...] = jnp.zeros_like(tail)

    def body(c, cur_ref, pif_ref, cw_ref, sv_ref, ym_ref, z_ref, cst_ref, nst_ref, C, nm, tail):
        z = _conv_fwd(cur_ref[:, 0:2 * W4], tail[...], cw_ref)
        tail[...] = cur_ref[L - 8:L, 0:2 * W4]
        z_ref[...] = z
        qk = z * _sigmoid(z)
        gt = pif_ref[...] + sv_ref[1:2, 0:LANES]
        r_i = lax.broadcasted_iota(jnp.int32, (L, L), 0)
        c_i = lax.broadcasted_iota(jnp.int32, (L, L), 1)
        tri = c_i <= r_i
        eye = c_i == r_i
        b_all = _tri_matmul(tri.astype(BF16), _log_sigmoid(gt))
        nmv = nm[...]
        nst_ref[0] = nmv
        c_prev = [C[h] for h in range(MLSTM_HEADS)]
        f = _mlstm_heads_fwd(qk, cur_ref, gt, b_all, c_prev, nmv, tri, eye)
        hv = f["hv"]
        xc = hv - _rowsum_mxu(hv, True) * (1.0 / HD)
        hhat = xc * lax.rsqrt(_rowsum_mxu(xc * xc) * (1.0 / HD) + NORM_EPS)
        so = _sigmoid(_stack(lambda h: cur_ref[:, 3 * W4 + h * HD:3 * W4 + (h + 1) * HD]))
        wn = _stack(lambda h: jnp.broadcast_to(sv_ref[0:1, h * HD:(h + 1) * HD], (L, HD)))
        y = (so * hhat * wn).astype(BF16)
        kw = f["k"] * f["u_col"]
        kwb = kw.astype(BF16)
        n_new, m_new = [], []
        for h in range(MLSTM_HEADS):
            cst_ref[0, h] = c_prev[h]
            ym_ref[:, h * HD:(h + 1) * HD] = _head(y, h)
            dec = f["dec"][h * L:h * L + 1, :]
            C[h] = dec * c_prev[h] + _dot_tn(_head(kwb, h), _head(f["vb"], h))
            n_new.append(dec * nmv[h:h + 1, :] + _colsum(_head(kw, h)))
            m_new.append(jnp.broadcast_to(f["m_new"][h * L:h * L + 1, :], (1, LANES)))
        nm[...] = jnp.concatenate(n_new + m_new, axis=0)

    return dict(
        init=init, body=body, operands=[pm, pif, cw, sv],
        in_specs=[pl.BlockSpec((L, M_W), lambda c: (c, 0)),
                  pl.BlockSpec((L, IF_W), lambda c: (c, 0)), _full(cw.shape), _full(sv.shape)],
        out_specs=[pl.BlockSpec((L, W4), lambda c: (c, 0)), pl.BlockSpec((L, 2 * W4), lambda c: (c, 0)),
                   pl.BlockSpec((1, MLSTM_HEADS, HD, HD), lambda c: (c, 0, 0, 0)),
                   pl.BlockSpec((1, 8, LANES), lambda c: (c, 0, 0))],
        out_shape=[jax.ShapeDtypeStruct((S, W4), BF16), jax.ShapeDtypeStruct((S, 2 * W4), F32),
                   jax.ShapeDtypeStruct((nc, MLSTM_HEADS, HD, HD), F32), jax.ShapeDtypeStruct((nc, 8, LANES), F32)],
        scratch=[pltpu.VMEM((MLSTM_HEADS, HD, HD), F32), pltpu.VMEM((8, LANES), F32), pltpu.VMEM((8, 2 * W4), F32)])


def _mlstm_bwd_part(pm, zc, pif, cw, sv, dym, cst, nst):
    S = pm.shape[0]
    L = MLSTM_CHUNK
    nc = S // L
    HD = MLSTM_HEAD_DIM
    W4 = MLSTM_HEADS * HD

    def init(r, cur_ref, z_ref, pif_ref, cw_ref, sv_ref, dym_ref, cst_ref, nst_ref,
             dm_ref, dcw_ref, dsv_ref, dC, dn, dz_next, dqk):
        @pl.when(r == 0)
        def _():
            dC[...] = jnp.zeros_like(dC)
            dn[...] = jnp.zeros_like(dn)
            dz_next[...] = jnp.zeros_like(dz_next)
            dcw_ref[...] = jnp.zeros_like(dcw_ref)
            dsv_ref[...] = jnp.zeros_like(dsv_ref)

    def body(r, cur_ref, z_ref, pif_ref, cw_ref, sv_ref, dym_ref, cst_ref, nst_ref,
             dm_ref, dcw_ref, dsv_ref, dC, dn, dz_next, dqk):
        z = z_ref[...]
        sgz = _sigmoid(z)
        qk = z * sgz
        gt = pif_ref[...] + sv_ref[1:2, 0:LANES]
        r_i = lax.broadcasted_iota(jnp.int32, (L, L), 0)
        c_i = lax.broadcasted_iota(jnp.int32, (L, L), 1)
        tri = c_i <= r_i
        eye = c_i == r_i
        b_all = _tri_matmul(tri.astype(BF16), _log_sigmoid(gt))
        lane = lax.broadcasted_iota(jnp.int32, (L, LANES), 1)
        rowl = lax.broadcasted_iota(jnp.int32, (L, 1), 0)
        nmv = nst_ref[0]
        heads = range(MLSTM_HEADS)
        c_prev = [cst_ref[0, h] for h in heads]
        f = _mlstm_heads_fwd(qk, cur_ref, gt, b_all, c_prev, nmv, tri, eye)
        hv, inv_g, den, e_t, u_col, n_prev = f["hv"], f["inv_g"], f["den"], f["e_t"], f["u_col"], f["n_prev"]
        q, k, v, qb, kb, vb, Sc, Scb, W = f["q"], f["k"], f["v"], f["qb"], f["kb"], f["vb"], f["Sc"], f["Scb"], f["W"]
        xc = hv - _rowsum_mxu(hv, True) * (1.0 / HD)
        rstd = lax.rsqrt(_rowsum_mxu(xc * xc) * (1.0 / HD) + NORM_EPS)
        hhat = xc * rstd
        wn = _stack(lambda h: jnp.broadcast_to(sv_ref[0:1, h * HD:(h + 1) * HD], (L, HD)))
        so = _sigmoid(_stack(lambda h: cur_ref[:, 3 * W4 + h * HD:3 * W4 + (h + 1) * HD]))
        dy = _stack(lambda h: dym_ref[:, h * HD:(h + 1) * HD])
        d_o = (dy * hhat * wn * (so * (1.0 - so))).astype(BF16)
        dln = dy * so
        dwn = dln * hhat
        dhhat = dln * wn
        m2 = _rowsum_mxu(dhhat * hhat) * (1.0 / HD)
        dh = rstd * (dhhat - _rowsum_mxu(dhhat) * (1.0 / HD) - hhat * m2)
        dnum = dh * inv_g
        active = jnp.abs(den) > f["floor"]
        dden = jnp.where(active, -(HD * NORM_EPS) * m2 * rstd * rstd * inv_g * jnp.where(den >= 0.0, 1.0, -1.0), 0.0)
        dnumb = dnum.astype(BF16)
        dSc = _stack(lambda h: _dot_nt(_head(dnumb, h), _head(vb, h))) + dden
        dA = (dSc * W).astype(BF16)
        G = dSc * Sc
        Gb = G.astype(BF16)
        Gl = (G - Gb.astype(F32)).astype(BF16)
        ones = jnp.ones((L, LANES), BF16)
        Gr = _dot(Gb, ones) + _dot(Gl, ones)
        Gc = _stack(lambda h: _dot_tn(_head(Gb, h), ones) + _dot_tn(_head(Gl, h), ones))
        dCn = [dC[h] for h in heads]
        dCnb = [d.astype(BF16) for d in dCn]
        dnv = dn[...]
        dn_new = _stack(lambda h: jnp.broadcast_to(dnv[h:h + 1, :], (L, HD)))
        kdC = _stack(lambda h: _dot(_head(kb, h), dCnb[h]))
        vdC = _stack(lambda h: _dot_nt(_head(vb, h), dCnb[h]))
        dv = (_stack(lambda h: _dot_tn(_head(Scb, h), _head(dnumb, h))) + u_col * kdC).astype(BF16)
        dq = _stack(lambda h: _dot(_head(dA, h), _head(kb, h))) \
            + e_t * _stack(lambda h: _dot_nt(_head(dnumb, h), f["cb"][h])) + (e_t * dden) * n_prev
        dk = (_stack(lambda h: _dot_tn(_head(dA, h), _head(qb, h))) + u_col * (vdC + dn_new)) * (HD ** -0.5)
        E = (_rowsum_mxu(f["P1"] * dnum, True) + dden * f["qn"]) * e_t
        U = _rowsum_mxu(kdC * v + k * dn_new, True) * u_col
        qe = (q * e_t).astype(BF16)
        qd = (e_t * dden) * q
        di = Gc + U
        db = Gr + E - Gc - U
        di_tile = jnp.zeros((L, LANES), F32)
        db_tile = jnp.zeros((L, LANES), F32)
        dn_rows = []
        for h in heads:
            dec = f["dec"][h * L:h * L + 1, :]
            ddec = jnp.sum(dCn[h] * c_prev[h]) + jnp.sum(dnv[h:h + 1, :] * nmv[h:h + 1, :])
            dbl = ddec * dec + jnp.sum(_head(U, h), axis=0, keepdims=True)
            di_tile = jnp.where(lane == h, _head(di, h), di_tile)
            db_tile = jnp.where(lane == 4 + h, _head(db, h) + jnp.where(rowl == L - 1, dbl, 0.0), db_tile)
            dC[h] = dec * dCn[h] + _dot_tn(_head(qe, h), _head(dnumb, h))
            dn_rows.append(dec * dnv[h:h + 1, :] + _colsum(_head(qd, h)))
            dsv_ref[0:1, h * HD:(h + 1) * HD] += _colsum(_head(dwn, h))
            dqk[:, h * HD:(h + 1) * HD] = _head(dq, h)
            dqk[:, W4 + h * HD:W4 + (h + 1) * HD] = _head(dk, h)
            dm_ref[:, 2 * W4 + h * HD:2 * W4 + (h + 1) * HD] = _head(dv, h)
            dm_ref[:, 3 * W4 + h * HD:3 * W4 + (h + 1) * HD] = _head(d_o, h)
        dn[...] = jnp.concatenate(dn_rows + [jnp.zeros((8 - MLSTM_HEADS, LANES), F32)], axis=0)
        dlf = _tri_matmul((r_i <= c_i).astype(BF16), db_tile)
        dif = jnp.where(lane < 4, di_tile, jnp.where(lane < 8, dlf * (1.0 - _sigmoid(gt)), 0.0))
        dm_ref[:, M_W:M_W + IF_W] = dif.astype(BF16)
        dsv_ref[1:2, 0:LANES] += _colsum(dif)
        dz = dqk[...] * (sgz * (1.0 + z * (1.0 - sgz)))
        dcw_ref[4:5, :] += _colsum(dz)
        u = cur_ref[:, 0:2 * W4]
        du_in = dz * cw_ref[3:4, :]
        dcw_ref[3:4, :] += _colsum(dz * u)
        for k in range(1, 4):
            up = _shift_rows(dz, dz_next[...], k, False)
            dcw_ref[3 - k:4 - k, :] += _colsum(up * u)
            du_in = du_in + up * cw_ref[3 - k:4 - k, :]
        dz_next[...] = dz[0:8, :]
        dm_ref[:, 0:2 * W4] = du_in.astype(BF16)

    cidx = lambda r: nc - 1 - r
    return dict(
        init=init, body=body, operands=[pm, zc, pif, cw, sv, dym, cst, nst],
        in_specs=[pl.BlockSpec((L, M_W), lambda r: (cidx(r), 0)), pl.BlockSpec((L, 2 * W4), lambda r: (cidx(r), 0)),
                  pl.BlockSpec((L, IF_W), lambda r: (cidx(r), 0)), _full(cw.shape), _full(sv.shape),
                  pl.BlockSpec((L, W4), lambda r: (cidx(r), 0)),
                  pl.BlockSpec((1, MLSTM_HEADS, HD, HD), lambda r: (cidx(r), 0, 0, 0)),
                  pl.BlockSpec((1, 8, LANES), lambda r: (cidx(r), 0, 0))],
        out_specs=[pl.BlockSpec((L, M_W + IF_W), lambda r: (cidx(r), 0)), _full((8, 2 * W4)), _full((8, W4))],
        out_shape=[jax.ShapeDtypeStruct((S, M_W + IF_W), BF16),
                   jax.ShapeDtypeStruct((8, 2 * W4), F32), jax.ShapeDtypeStruct((8, W4), F32)],
        scratch=[pltpu.VMEM((MLSTM_HEADS, HD, HD), F32), pltpu.VMEM((8, LANES), F32),
                 pltpu.VMEM((8, 2 * W4), F32), pltpu.VMEM((L, 2 * W4), F32)])


def _rope_tables(positions):
    half = HEAD_DIM // 2
    inv_freq = ROPE_THETA ** (-2.0 * jnp.arange(half, dtype=F32) / HEAD_DIM)
    ang = positions.astype(F32)[:, None] * inv_freq
    cos = jnp.tile(jnp.cos(ang), (1, LANES // half))
    sign = jnp.tile(jnp.concatenate([-jnp.ones((half,), F32), jnp.ones((half,), F32)]), LANES // HEAD_DIM)
    sin = jnp.tile(jnp.sin(ang), (1, LANES // half)) * sign
    return cos, sin


def _local_step(x, tgt, positions, mod, gains, w_cat, w_ba, w_bm, w_out, w_gate, w_up, w_down,
                conv_w, conv_b, b_if, sinks, norm_w):
    t = _tables(mod, gains, conv_w, conv_b, b_if, norm_w, positions)
    a = _mixer_fwd(x, t, sinks, w_cat)
    b = _ffn_part(x, tgt, t, a, w_ba, w_bm, w_out, w_gate, w_up, w_down)
    c = _mixer_bwd(b["dx1"], t, a, b, sinks, w_ba, w_bm, w_out)
    grad_x, acc_p = _pre_bwd(c["dproj"], x, b["dx1"], t["vecs"], w_cat)
    big = dict(w_cat=jnp.concatenate(c["g_w_cat"], axis=1), w_ba=c["g_w_ba"], w_bm=c["g_w_bm"], w_out=c["g_w_out"], w_gate=b["g_w_gate"],
               w_up=b["g_w_up"], w_down=b["g_w_down"])
    return b["loss"], grad_x, big, _small_grads(acc_p, b, c)


def _tables(mod, gains, conv_w, conv_b, b_if, norm_w, positions):
    cos, sin = _rope_tables(positions)
    return dict(
        vecs=jnp.concatenate([mod, gains, jnp.zeros((6, D_MODEL), F32)], axis=0),
        cw=jnp.concatenate([conv_w, conv_b.reshape(1, -1), jnp.zeros((3, 2 * 512), F32)], axis=0),
        sv=jnp.zeros((8, 512), F32).at[0].set(norm_w).at[1, 0:8].set(b_if), cos=cos, sin=sin)


def _mixer_fwd(x, t, sinks, w_cat):
    h, pa, pm, pif, pg = _pre_proj(x, t["vecs"], w_cat)
    n_blk = x.shape[0] // ATTN_BLOCK
    (ya, qr, kr, vb, lse), = _fused_call([_attn_fwd_part(pa, t["cos"], t["sin"], sinks)], "attn_fwd", n_blk)
    (ym, zc, cst, nst), = _fused_call([_mlstm_fwd_part(pm, pif, t["cw"], t["sv"])], "mlstm_fwd", n_blk)
    return dict(h=h, pm=pm, pif=pif, pg=pg, ya=ya, qr=qr, kr=kr, vb=vb, lse=lse, ym=ym, zc=zc, cst=cst, nst=nst)


def _ffn_part(x, tgt, t, a, w_ba, w_bm, w_out, w_gate, w_up, w_down):
    x1, merged, mix, pba, pbm = _mix_fwd(x, a["ya"], a["ym"], a["pg"], t["vecs"], w_ba, w_bm, w_out)
    dx1, h2, hid, da, du, dff, acc_f, loss = _ffn_fwd_bwd(x1, tgt, t["vecs"], w_gate, w_up, w_down)
    return dict(merged=merged, mix=mix, pba=pba, pbm=pbm, dx1=dx1, acc_f=acc_f, loss=loss[0, 0],
                g_w_gate=_matmul_tn(da, h2, 1024, "dw_ffn_gate"),
                g_w_up=_matmul_tn(du, h2, 1024, "dw_ffn_up"),
                g_w_down=_matmul_tn(hid, dff, 1024, "dw_ffn_down"))


def _mixer_bwd(dx1, t, a, b, sinks, w_ba, w_bm, w_out):
    dmix, dpa, dpb, dg, dya, dym, acc_m = _mix_bwd(dx1, b["mix"], b["pba"], b["pbm"], a["pg"], t["vecs"], w_ba, w_bm, w_out)
    g_w_out = _matmul_tn(b["merged"], dmix, 1024, "dw_out")
    g_w_ba = _matmul_tn(a["ya"], dpa, 1024, "dw_branch_attn")
    g_w_bm = _matmul_tn(a["ym"], dpb, 1024, "dw_branch_mlstm")
    n_blk = dx1.shape[0] // ATTN_BLOCK
    (dqkv, dqkv_last, dsink), = _fused_call(
        [_attn_bwd_part(dya, a["qr"], a["kr"], a["vb"], a["lse"], t["cos"], t["sin"], sinks)], "attn_bwd", n_blk)
    (dm, dcw, dsv), = _fused_call(
        [_mlstm_bwd_part(a["pm"], a["zc"], a["pif"], t["cw"], t["sv"], dym, a["cst"], a["nst"])], "mlstm_bwd", n_blk)
    dqkv = lax.dynamic_update_slice(dqkv, dqkv_last, (dqkv.shape[0] - ATTN_BLOCK, 0))
    dproj = [dqkv, dm, dg]
    g_w_cat = [_matmul_tn(a["h"], p, 1024 if p.shape[1] % 1024 == 0 else p.shape[1], "dw_in_" + n)
               for p, n in zip(dproj, ("attn", "mlstm", "branch"))]
    return dict(dproj=dproj, g_w_cat=g_w_cat, g_w_out=g_w_out, g_w_ba=g_w_ba,
                g_w_bm=g_w_bm, acc_m=acc_m, dsink=dsink, dcw=dcw, dsv=dsv)


def _small_grads(acc_p, b, c):
    acc_f, acc_m = b["acc_f"], c["acc_m"]
    dmod = jnp.stack([acc_p[1], acc_p[0], acc_m[0], acc_f[3], acc_f[2], acc_f[0]])
    dgains = jnp.stack([acc_p[2], acc_m[1], acc_f[4], acc_f[1]])
    return dict(dmod=dmod, dgains=dgains, dconv_w=c["dcw"][0:4], dconv_b=c["dcw"][4], db_if=c["dsv"][1, 0:8],
                dsinks=c["dsink"][0, 0:8], dnorm_w=c["dsv"][0])


MESH_ID = pl.DeviceIdType.MESH


def _mesh_pos():
    return lax.axis_index("x"), lax.axis_index("y"), lax.axis_index("c")


def _flip(v, bit):
    return 1 - v if bit else v


def _relations():
    return [((r >> 2) & 1, (r >> 1) & 1, r & 1) for r in range(1, N_DEV)]


def _small_exchange(p, gather, name):
    R, V = p.shape[-2:]

    def body(p_ref, out_ref, send_sems, recv_sems):
        x, y, c = _mesh_pos()
        me = 4 * x + 2 * y + c
        out_ref[me] = p_ref[...] if gather else p_ref[me]
        peers = []
        for dx, dy, dc in _relations():
            px, py, pc = _flip(x, dx), _flip(y, dy), _flip(c, dc)
            peers.append(((px, py, pc), 4 * px + 2 * py + pc))

        def copy(k, landing):
            peer, pid = peers[k]
            return pltpu.make_async_remote_copy(
                src_ref=p_ref if gather else p_ref.at[pid], dst_ref=out_ref.at[landing],
                send_sem=send_sems.at[k], recv_sem=recv_sems.at[k], device_id=peer, device_id_type=MESH_ID)

        sends = [copy(k, me) for k in range(N_DEV - 1)]
        for cp in sends:
            cp.start()
        for k in range(N_DEV - 1):
            copy(k, peers[k][1]).wait_recv()
        for cp in sends:
            cp.wait_send()

    vm = pl.BlockSpec(memory_space=pltpu.VMEM)
    return pl.pallas_call(
        body, name=name, in_specs=[vm], out_specs=vm,
        out_shape=jax.ShapeDtypeStruct((N_DEV, R, V), F32),
        scratch_shapes=[pltpu.SemaphoreType.DMA((N_DEV - 1,)), pltpu.SemaphoreType.DMA((N_DEV - 1,))],
        compiler_params=pltpu.CompilerParams(vmem_limit_bytes=VMEM_LIMIT),
    )(p)


HBM_SPEC = pl.BlockSpec(memory_space=pltpu.HBM)
SEM_SPEC = pl.BlockSpec(memory_space=pltpu.SEMAPHORE)


def _peers(x, y, c):
    out = []
    for dx, dy, dc in _relations():
        px, py, pc = _flip(x, dx), _flip(y, dy), _flip(c, dc)
        out.append(((px, py, pc), 4 * px + 2 * py + pc))
    return out


def _exchange_start(arrs, gather, after, name):
    n = len(arrs)
    me_out = 4 * lax.axis_index("x") + 2 * lax.axis_index("y") + lax.axis_index("c")
    lands = []
    for a in arrs:
        own = a[None] if gather else lax.dynamic_index_in_dim(a, me_out, 0, keepdims=True)
        empty = lax.empty(((N_DEV,) + a.shape) if gather else a.shape, a.dtype)
        lands.append(lax.dynamic_update_index_in_dim(empty, own, me_out, 0))

    def body(*refs):
        a_refs, l_refs = refs[:n], refs[n:2 * n]
        send_sems, recv_sems = refs[2 * n + 1], refs[2 * n + 2]
        token = refs[4 * n + 3]
        x, y, c = _mesh_pos()
        me = 4 * x + 2 * y + c
        for a in range(n):
            for k, (peer, pid) in enumerate(_peers(x, y, c)):
                pltpu.make_async_remote_copy(
                    src_ref=a_refs[a] if gather else a_refs[a].at[pid], dst_ref=l_refs[a].at[me],
                    send_sem=send_sems.at[a * (N_DEV - 1) + k], recv_sem=recv_sems.at[a * (N_DEV - 1) + k],
                    device_id=peer, device_id_type=MESH_ID).start()
        token[...] = jnp.zeros_like(token)

    sem = pltpu.SemaphoreType.DMA((n * (N_DEV - 1),))
    hbm = lambda a: pltpu.with_memory_space_constraint(a, pltpu.HBM)
    res = pl.pallas_call(
        body, name=name,
        out_shape=(sem, sem, *[pltpu.HBM(a.shape, a.dtype) for a in arrs], *[pltpu.HBM(l.shape, l.dtype) for l in lands],
                   jax.ShapeDtypeStruct((8, LANES), F32)),
        in_specs=[HBM_SPEC] * (2 * n) + [pl.BlockSpec(memory_space=pl.ANY)],
        out_specs=(SEM_SPEC, SEM_SPEC, *[HBM_SPEC] * (2 * n), pl.BlockSpec(memory_space=pltpu.VMEM)),
        input_output_aliases={i: 2 + i for i in range(2 * n)},
        compiler_params=pltpu.CompilerParams(has_side_effects=pltpu.SideEffectType.DATAFLOW_SIDE_EFFECTING),
    )(*[hbm(a) for a in arrs], *[hbm(l) for l in lands], after)
    return dict(sems=res[0:2], arrs=res[2:2 + n], lands=res[2 + n:2 + 2 * n], token=res[2 + 2 * n], gather=gather)


def _exchange_wait(st, after, name):
    n = len(st["arrs"])
    gather = st["gather"]

    def body(*refs):
        a_refs, l_refs = refs[:n], refs[n:2 * n]
        send_sems, recv_sems = refs[2 * n], refs[2 * n + 1]
        x, y, c = _mesh_pos()
        for a in range(n):
            for k, (peer, pid) in enumerate(_peers(x, y, c)):
                cp = pltpu.make_async_remote_copy(
                    src_ref=a_refs[a] if gather else a_refs[a].at[pid], dst_ref=l_refs[a].at[pid],
                    send_sem=send_sems.at[a * (N_DEV - 1) + k], recv_sem=recv_sems.at[a * (N_DEV - 1) + k],
                    device_id=peer, device_id_type=MESH_ID)
                cp.wait_send()
                cp.wait_recv()

    both = list(st["arrs"]) + list(st["lands"])
    res = pl.pallas_call(
        body, name=name, out_shape=[pltpu.HBM(a.shape, a.dtype) for a in both],
        in_specs=[HBM_SPEC] * (2 * n) + [SEM_SPEC, SEM_SPEC, pl.BlockSpec(memory_space=pl.ANY)],
        out_specs=[HBM_SPEC] * (2 * n), input_output_aliases={i: i for i in range(2 * n)},
        compiler_params=pltpu.CompilerParams(has_side_effects=pltpu.SideEffectType.DATAFLOW_SIDE_EFFECTING),
    )(*both, *st["sems"], after)
    return res[n:2 * n]


def _tie(x, token, name):
    def body(x_ref, t_ref, o_ref):
        o_ref[...] = x_ref[...]

    vm = pl.BlockSpec(memory_space=pltpu.VMEM)
    return pl.pallas_call(
        body, name=name, in_specs=[vm, pl.BlockSpec(memory_space=pl.ANY)], out_specs=vm,
        out_shape=jax.ShapeDtypeStruct(x.shape, x.dtype),
    )(x, token)


def _gather_front(shards, c8, w_ada, b_cols):
    n = len(shards)
    ada_w = w_ada.shape[1]
    chunk = c8.shape[1]

    def body(*refs):
        p_refs = refs[:n]
        c_ref, wa_ref, b_ref = refs[n:n + 3]
        out_refs = refs[n + 3:2 * n + 3]
        cg_ref, mod_ref = refs[2 * n + 3], refs[2 * n + 4]
        send_sems, recv_sems, local_sems, c_send, c_recv, m_send, m_recv, pbuf = refs[2 * n + 5:]
        x, y, c = _mesh_pos()
        me_id = 4 * x + 2 * y + c
        me, sibling = (x, y, c), (x, y, 1 - c)
        chips = [(1 - x, y), (x, 1 - y), (1 - x, 1 - y)]
        peers = _peers(x, y, c)

        def copy(a, k, block, to, own=False):
            slot = out_refs[a].at[4 * block[0] + 2 * block[1] + block[2]]
            return pltpu.make_async_remote_copy(
                src_ref=p_refs[a] if own else slot, dst_ref=slot,
                send_sem=send_sems.at[a, k], recv_sem=recv_sems.at[a, k], device_id=to, device_id_type=MESH_ID)

        def c_copy(k, landing):
            return pltpu.make_async_remote_copy(
                src_ref=c_ref, dst_ref=cg_ref.at[landing], send_sem=c_send.at[k], recv_sem=c_recv.at[k],
                device_id=peers[k][0], device_id_type=MESH_ID)

        cg_ref[me_id] = c_ref[...]
        c_sends = [c_copy(k, me_id) for k in range(N_DEV - 1)]
        for cp in c_sends:
            cp.start()
        mine = [pltpu.make_async_copy(p_refs[a], out_refs[a].at[me_id], local_sems.at[a]) for a in range(n)]
        for cp in mine:
            cp.start()
        first = []
        for a in range(n):
            first.append(copy(a, 0, me, sibling, own=True))
            first += [copy(a, 1 + j, me, (*chip, c), own=True) for j, chip in enumerate(chips)]
        for cp in first:
            cp.start()
        for k in range(N_DEV - 1):
            c_copy(k, peers[k][1]).wait_recv()
        prod = b_ref[...] + jnp.zeros((N_DEV, ada_w), F32)
        for j in range(D_MODEL // chunk):
            prod = prod + _dot(cg_ref[:, j, :].astype(BF16), wa_ref[j * chunk:(j + 1) * chunk, :].astype(BF16))
        for k in range(N_DEV):
            pbuf[k] = jnp.broadcast_to(prod[k:k + 1, :], (8, ada_w))

        def m_copy(k, landing):
            return pltpu.make_async_remote_copy(
                src_ref=pbuf.at[peers[k][1]], dst_ref=mod_ref.at[landing], send_sem=m_send.at[k], recv_sem=m_recv.at[k],
                device_id=peers[k][0], device_id_type=MESH_ID)

        mod_ref[me_id] = pbuf[me_id]
        m_sends = [m_copy(k, me_id) for k in range(N_DEV - 1)]
        for cp in m_sends:
            cp.start()

        passed = []
        for j, chip in enumerate(chips):
            for a in range(n):
                copy(a, 1 + j, (*chip, c), me).wait_recv()
                passed.append(copy(a, 4 + j, (*chip, c), sibling))
                passed[-1].start()
        for a in range(n):
            copy(a, 0, sibling, me).wait_recv()
            for j, chip in enumerate(chips):
                copy(a, 4 + j, (*chip, 1 - c), me).wait_recv()
        for k in range(N_DEV - 1):
            m_copy(k, peers[k][1]).wait_recv()
        for cp in first + passed + c_sends + m_sends:
            cp.wait_send()
        for cp in mine:
            cp.wait()

    hbm = pl.BlockSpec(memory_space=pl.ANY)
    vm = pl.BlockSpec(memory_space=pltpu.VMEM)
    sem7 = pltpu.SemaphoreType.DMA((N_DEV - 1,))
    res = pl.pallas_call(
        body, name="gather_front", in_specs=[hbm] * n + [vm, vm, vm], out_specs=[hbm] * n + [vm, vm],
        out_shape=[jax.ShapeDtypeStruct((N_DEV,) + s.shape, s.dtype) for s in shards]
        + [jax.ShapeDtypeStruct((N_DEV,) + c8.shape, F32), jax.ShapeDtypeStruct((N_DEV, 8, ada_w), F32)],
        scratch_shapes=[pltpu.SemaphoreType.DMA((n, N_DEV - 1)), pltpu.SemaphoreType.DMA((n, N_DEV - 1)),
                        pltpu.SemaphoreType.DMA((n,)), sem7, sem7, sem7, sem7, pltpu.VMEM((N_DEV, 8, ada_w), F32)],
        compiler_params=pltpu.CompilerParams(vmem_limit_bytes=VMEM_LIMIT),
    )(*shards, c8, w_ada, b_cols)
    return res[:n], res[n], res[n + 1]


def _adamw(w, g, m, v):
    m2 = ADAM_B1 * m + (1.0 - ADAM_B1) * g
    v2 = ADAM_B2 * v + (1.0 - ADAM_B2) * (g * g)
    m_hat = m2 / (1.0 - ADAM_B1 ** ADAM_STEP)
    v_hat = v2 / (1.0 - ADAM_B2 ** ADAM_STEP)
    delta = -ADAM_LR * (m_hat / (jnp.sqrt(v_hat) + ADAM_EPS) + ADAM_WD * w)
    return delta, m2, v2


def _adamw_w_ada(cmat, dmod_cols, w, m, v):
    r, cdim = w.shape
    tr = _row_tile(r)

    def body(c_ref, d_ref, w_ref, m_ref, v_ref, g_ref, dl_ref, m2_ref, v2_ref):
        g = _dot_tn(c_ref[...].astype(BF16), d_ref[...].astype(BF16))
        g_ref[...] = g
        dl_ref[...], m2_ref[...], v2_ref[...] = _adamw(w_ref[...], g, m_ref[...], v_ref[...])

    row = pl.BlockSpec((tr, cdim), lambda i: (i, 0))
    return pl.pallas_call(
        body, name="adamw_w_ada", grid=(r // tr,),
        in_specs=[pl.BlockSpec((N_DEV, tr), lambda i: (0, i)), _full(dmod_cols.shape), row, row, row],
        out_specs=[row] * 4, out_shape=[jax.ShapeDtypeStruct(w.shape, F32)] * 4,
        compiler_params=_params(("parallel",)),
    )(cmat, dmod_cols, w, m, v)


SMALL_ROWS = 16
SMALL_AT = {"b_ada": (0, 6, 0, D_MODEL), "g_pre_mix": (6, 1, 0, D_MODEL), "g_post_mix": (7, 1, 0, D_MODEL),
            "g_pre_ffn": (8, 1, 0, D_MODEL), "g_post_ffn": (9, 1, 0, D_MODEL), "conv_b": (10, 1, 0, D_MODEL),
            "mlstm_norm_w": (11, 1, 0, 512), "b_if": (11, 1, 512, LANES), "attn_sinks": (11, 1, 640, LANES)}


def _small_table(part):
    tail = jnp.concatenate([part["mlstm_norm_w"], jnp.pad(part["b_if"], (0, LANES - 8)),
                            jnp.pad(part["attn_sinks"], (0, LANES - 8)), jnp.zeros((256,), F32)])
    return jnp.concatenate([part["b_ada"], part["gains"], part["conv_b"][None], tail[None],
                            jnp.full((1, D_MODEL), part["loss"], F32),
                            jnp.zeros((SMALL_ROWS - 13, D_MODEL), F32)], axis=0)


LOSS_ROW = 12


def _adamw_small(gathered, wmv):
    names = list(SMALL_AT)

    def body(*refs):
        g_ref, ins, outs = refs[0], refs[1:1 + 3 * len(names)], refs[1 + 3 * len(names):]
        g = g_ref[0]
        for k in range(1, N_DEV):
            g = g + g_ref[k]
        outs[4 * len(names)][...] = g[LOSS_ROW:LOSS_ROW + 1, 0:LANES]
        for i, n in enumerate(names):
            r0, rows, l0, lanes = SMALL_AT[n]
            gi = jnp.concatenate([g[r:r + 1, l0:l0 + lanes] for r in range(r0, r0 + rows)], axis=1)
            w_ref, m_ref, v_ref = ins[3 * i:3 * i + 3]
            go, dl, m2, v2 = outs[4 * i:4 * i + 4]
            go[...] = gi
            dl[...], m2[...], v2[...] = _adamw(w_ref[...], gi, m_ref[...], v_ref[...])

    flat = [a for n in names for a in wmv[n]]
    res = pl.pallas_call(
        body, name="adamw_small",
        out_shape=[jax.ShapeDtypeStruct(wmv[n][0].shape, F32) for n in names for _ in range(4)]
        + [jax.ShapeDtypeStruct((1, LANES), F32)],
        compiler_params=_params(),
    )(gathered, *flat)
    out = {n: res[4 * i:4 * i + 4] for i, n in enumerate(names)}
    out["loss"] = res[4 * len(names)]
    return out


def _adamw_sum_many(items, name):
    n = len(items)

    def body(*refs):
        ins, outs = refs[:4 * n], refs[4 * n:]
        for i in range(n):
            r_ref, w_ref, m_ref, v_ref = ins[4 * i:4 * i + 4]
            go, dl, m2, v2 = outs[4 * i:4 * i + 4]
            g = _sum_partials(r_ref)
            go[...] = g
            dl[...], m2[...], v2[...] = _adamw(w_ref[...], g, m_ref[...], v_ref[...])

    res = pl.pallas_call(
        body, name=name, out_shape=[jax.ShapeDtypeStruct(it[1].shape, F32) for it in items for _ in range(4)],
        compiler_params=_params(),
    )(*[a for it in items for a in it])
    return [res[4 * i:4 * i + 4] for i in range(n)]


def _row_tile(rows):
    return rows // 4 if rows >= 512 and rows % 64 == 0 else rows


def _sum_partials(r_ref):
    g = r_ref[0].astype(F32)
    for k in range(1, N_DEV):
        g = g + r_ref[k].astype(F32)
    return g


def _adamw_sum(recv, w, m, v, name):
    r, cdim = w.shape
    tr = _row_tile(r)

    def body(r_ref, w_ref, m_ref, v_ref, g_ref, dl_ref, m2_ref, v2_ref):
        g = _sum_partials(r_ref)
        g_ref[...] = g
        dl_ref[...], m2_ref[...], v2_ref[...] = _adamw(w_ref[...], g, m_ref[...], v_ref[...])

    row = pl.BlockSpec((tr, cdim), lambda i: (i, 0))
    return pl.pallas_call(
        body, name=name, grid=(r // tr,),
        in_specs=[pl.BlockSpec((N_DEV, tr, cdim), lambda i: (0, i, 0)), row, row, row],
        out_specs=[row] * 4, out_shape=[jax.ShapeDtypeStruct((r, cdim), F32)] * 4,
        compiler_params=_params(("parallel",)),
    )(recv, w, m, v)


def _sum8(recv, name):
    _, r, cdim = recv.shape
    tr = _row_tile(r)

    def body(r_ref, g_ref):
        g_ref[...] = _sum_partials(r_ref)

    return pl.pallas_call(
        body, name=name, grid=(r // tr,),
        in_specs=[pl.BlockSpec((N_DEV, tr, cdim), lambda i: (0, i, 0))],
        out_specs=pl.BlockSpec((tr, cdim), lambda i: (i, 0)), out_shape=jax.ShapeDtypeStruct((r, cdim), F32),
        compiler_params=_params(("parallel",)),
    )(recv)


def _adamw_plain(g, w, m, v, name):
    r, cdim = w.shape
    tr = _row_tile(r)

    def body(g_ref, w_ref, m_ref, v_ref, dl_ref, m2_ref, v2_ref):
        dl_ref[...], m2_ref[...], v2_ref[...] = _adamw(w_ref[...], g_ref[...], m_ref[...], v_ref[...])

    row = pl.BlockSpec((tr, cdim), lambda i: (i, 0))
    return pl.pallas_call(
        body, name=name, grid=(r // tr,), in_specs=[row] * 4, out_specs=[row] * 3,
        out_shape=[jax.ShapeDtypeStruct((r, cdim), F32)] * 3,
        compiler_params=_params(("parallel",)),
    )(g, w, m, v)


IN_SHARD = 609
IN_SHARD_PAD = 640
IF_AT = A_W + M_W


def _regrouped(u):
    return u if u < IF_AT + 8 else u + (IF_W - 8)


def _selection(k, rows, row0, transpose):
    shape = (rows, IN_SHARD_PAD) if transpose else (IN_SHARD_PAD, rows)
    l = lax.broadcasted_iota(jnp.int32, shape, 1 if transpose else 0)
    r = lax.broadcasted_iota(jnp.int32, shape, 0 if transpose else 1) + row0
    u = l + IN_SHARD * k
    ru = u + jnp.where(u >= IF_AT + 8, IF_W - 8, 0)
    return ((ru == r) & (l < IN_SHARD)).astype(BF16)


def _regroup_w_in(g):
    def body(g_ref, o_ref):
        for cb in range(CAT_W // LANES):
            r0 = cb * LANES
            acc = jnp.zeros((D_MODEL, LANES), F32)
            for k in range(N_DEV):
                src = [l for l in range(IN_SHARD) if r0 <= _regrouped(IN_SHARD * k + l) < r0 + LANES]
                if src:
                    a0 = src[0] // LANES * LANES
                    a1 = min(a0 + 2 * LANES, IN_SHARD_PAD)
                    acc = acc + _dot(g_ref[k, :, a0:a1], _selection(k, LANES, r0, False)[a0:a1])
            o_ref[:, r0:r0 + LANES] = acc.astype(BF16)

    return pl.pallas_call(
        body, name="regroup_w_in", out_shape=jax.ShapeDtypeStruct((D_MODEL, CAT_W), BF16),
        compiler_params=_params(),
    )(g)


def _ungroup_w_in(g_parts):
    n = len(g_parts)

    def body(*refs):
        o_ref, g_ref = refs[n], refs[n + 1]
        at = 0
        for p in refs[:n]:
            g_ref[:, at:at + p.shape[1]] = p[...]
            at += p.shape[1]
        for k in range(N_DEV):
            for cb in range(IN_SHARD_PAD // LANES):
                cols = slice(cb * LANES, (cb + 1) * LANES)
                lo = _regrouped(IN_SHARD * k + cb * LANES)
                hi = _regrouped(IN_SHARD * k + min((cb + 1) * LANES, IN_SHARD) - 1)
                w0, w1 = lo // LANES * LANES, (hi // LANES + 1) * LANES
                o_ref[k, :, cols] = _dot(g_ref[:, w0:w1], _selection(k, w1 - w0, w0, True)[:, cols]).astype(BF16)

    return pl.pallas_call(
        body, name="ungroup_w_in", out_shape=jax.ShapeDtypeStruct((N_DEV, D_MODEL, IN_SHARD_PAD), BF16),
        scratch_shapes=[pltpu.VMEM((D_MODEL, CAT_W), BF16)], compiler_params=_params(),
    )(*g_parts)


WEIGHT_NAMES = ("w_ada", "b_ada", "g_pre_mix", "g_post_mix", "w_in", "b_if", "conv_w", "conv_b", "attn_sinks",
                "mlstm_norm_w", "w_branch_attn", "w_branch_mlstm", "w_out", "g_pre_ffn", "g_post_ffn",
                "w_ffn_gate", "w_ffn_up", "w_ffn_down")


def kernel(x, c, positions, w_ada, b_ada, g_pre_mix, g_post_mix, w_in, b_if, conv_w, conv_b, attn_sinks, mlstm_norm_w, w_branch_attn, w_branch_mlstm, w_out, g_pre_ffn, g_post_ffn, w_ffn_gate, w_ffn_up, w_ffn_down, loss_target, m_w_ada, m_b_ada, m_g_pre_mix, m_g_post_mix, m_w_in, m_b_if, m_conv_w, m_conv_b, m_attn_sinks, m_mlstm_norm_w, m_w_branch_attn, m_w_branch_mlstm, m_w_out, m_g_pre_ffn, m_g_post_ffn, m_w_ffn_gate, m_w_ffn_up, m_w_ffn_down, v_w_ada, v_b_ada, v_g_pre_mix, v_g_post_mix, v_w_in, v_b_if, v_conv_w, v_conv_b, v_attn_sinks, v_mlstm_norm_w, v_w_branch_attn, v_w_branch_mlstm, v_w_out, v_g_pre_ffn, v_g_post_ffn, v_w_ffn_gate, v_w_ffn_up, v_w_ffn_down):
    given = dict(locals())
    W = {n: given[n][0] for n in WEIGHT_NAMES}
    M = {n: given["m_" + n][0] for n in WEIGHT_NAMES}
    V = {n: given["v_" + n][0] for n in WEIGHT_NAMES}
    me = 4 * lax.axis_index("x") + 2 * lax.axis_index("y") + lax.axis_index("c")

    ff_sh = D_FF // N_DEV
    ada_w = D_MODEL * 6 // N_DEV
    b_cols = lax.dynamic_slice(W["b_ada"], (me * ada_w,), (ada_w,)).reshape(1, ada_w)
    (g_in, g_conv), cg, mod_recv = _gather_front(
        [jnp.pad(W["w_in"], ((0, 0), (0, IN_SHARD_PAD - IN_SHARD))).astype(BF16), jnp.pad(W["conv_w"], ((0, 4), (0, 0)))],
        c.reshape(8, D_MODEL // 8), W["w_ada"], b_cols)
    cmat = cg.reshape(N_DEV, D_MODEL)
    mod = mod_recv[:, 0, :].reshape(6, D_MODEL)

    st_b = _exchange_start([W["w_branch_attn"].astype(BF16), W["w_branch_mlstm"].astype(BF16), W["w_out"].astype(BF16),
                            W["w_ffn_gate"].T.astype(BF16), W["w_ffn_up"].T.astype(BF16), W["w_ffn_down"].astype(BF16)],
                           True, mod_recv, "gather_rest_start")
    cols = lambda g: g.transpose(1, 0, 2).reshape(g.shape[1], N_DEV * g.shape[2])
    gains = jnp.stack([W["g_pre_mix"], W["g_post_mix"], W["g_pre_ffn"], W["g_post_ffn"]])
    xs, tgt = x[0], loss_target[0]
    t = _tables(mod, gains, cols(g_conv)[0:4], W["conv_b"], W["b_if"], W["mlstm_norm_w"], positions[0])
    vecs = t["vecs"]
    t["vecs"] = _tie(vecs, st_b["token"], "tie_fwd")
    w_cat = _regroup_w_in(g_in)
    a = _mixer_fwd(xs, t, W["attn_sinks"], w_cat)
    g_ba, g_bm, g_out, g_gate, g_up, g_down = _exchange_wait(st_b, a["ym"], "gather_rest_wait")
    w_ba, w_bm, w_out = cols(g_ba), cols(g_bm), g_out.reshape(D_MODEL, D_MODEL)
    b = _ffn_part(xs, tgt, t, a, w_ba, w_bm, w_out, g_gate.reshape(D_FF, D_MODEL), g_up.reshape(D_FF, D_MODEL),
                  g_down.reshape(D_FF, D_MODEL))

    st_f = _exchange_start([b["g_w_gate"].reshape(N_DEV, ff_sh, D_MODEL), b["g_w_up"].reshape(N_DEV, ff_sh, D_MODEL),
                            b["g_w_down"].reshape(N_DEV, ff_sh, D_MODEL)], False, b["dx1"], "scatter_ffn_start")
    t["vecs"] = _tie(vecs, st_f["token"], "tie_bwd")
    cm = _mixer_bwd(b["dx1"], t, a, b, W["attn_sinks"], w_ba, w_bm, w_out)
    pieces = lambda g, n: g.reshape(g.shape[0], N_DEV, n).transpose(1, 0, 2)
    st_m = _exchange_start([_ungroup_w_in(cm["g_w_cat"]), pieces(cm["g_w_ba"], 128), pieces(cm["g_w_bm"], 128),
                            cm["g_w_out"].reshape(N_DEV, D_MODEL // N_DEV, D_MODEL),
                            jnp.pad(pieces(cm["dcw"][0:4], 128), ((0, 0), (0, 4), (0, 0)))], False, cm["dcw"],
                           "scatter_mixer_start")
    r_gate, r_up, r_down = _exchange_wait(st_f, st_m["token"], "scatter_ffn_wait")
    grad_x, acc_p = _pre_bwd(cm["dproj"], xs, b["dx1"], _tie(vecs, st_m["token"], "tie_pre_bwd"), w_cat)
    small = _small_grads(acc_p, b, cm)
    loss = b["loss"]

    big_out = [{} for _ in range(4)]

    def put(n, res):
        for k in range(4):
            big_out[k][n] = res[k][None]

    put("w_ffn_down", _adamw_sum(r_down, W["w_ffn_down"], M["w_ffn_down"], V["w_ffn_down"], "adamw_w_ffn_down"))
    for n, r in (("w_ffn_gate", r_gate), ("w_ffn_up", r_up)):
        put(n, [o.T for o in _adamw_sum(r, W[n].T, M[n].T, V[n].T, "adamw_" + n)])

    sg = _small_exchange(_small_table({"b_ada": small["dmod"], "gains": small["dgains"], "conv_b": small["dconv_b"],
                                       "mlstm_norm_w": small["dnorm_w"], "b_if": small["db_if"],
                                       "attn_sinks": small["dsinks"], "loss": loss}), True, "gather_small")
    as_row = lambda a, n: jnp.pad(a, (0, SMALL_AT[n][3] * SMALL_AT[n][1] - a.shape[0]))[None]
    small_res = _adamw_small(sg, {n: [as_row(d[n], n) for d in (W, M, V)] for n in SMALL_AT})
    small_out = [{n: small_res[n][k][:, 0:W[n].shape[0]] for n in SMALL_AT} for k in range(4)]
    dmod_cols = lax.dynamic_slice(sg[:, 0:6, :].reshape(N_DEV, 6 * D_MODEL), (0, me * ada_w), (N_DEV, ada_w))
    ada_out = _adamw_w_ada(cmat, dmod_cols, W["w_ada"], M["w_ada"], V["w_ada"])

    r_in, r_ba, r_bm, r_out, r_conv = _exchange_wait(st_m, ada_out[1], "scatter_mixer_wait")
    pad4 = lambda v: jnp.pad(v, ((0, 4), (0, 0)))
    res_ba, res_bm, res_out, res_conv = _adamw_sum_many(
        [(r, W[n], M[n], V[n]) for n, r in (("w_branch_attn", r_ba), ("w_branch_mlstm", r_bm), ("w_out", r_out))]
        + [(r_conv, pad4(W["conv_w"]), pad4(M["conv_w"]), pad4(V["conv_w"]))], "adamw_mixer_small")
    put("w_branch_attn", res_ba)
    put("w_branch_mlstm", res_bm)
    put("w_out", res_out)
    put("conv_w", [o[0:4] for o in res_conv])
    g = _sum8(r_in, "sum_w_in")[:, 0:IN_SHARD].T
    put("w_in", [o.T for o in [g] + list(_adamw_plain(g, W["w_in"].T, M["w_in"].T, V["w_in"].T, "adamw_w_in"))])

    outs = [small_res["loss"][0, 0], grad_x[None]]
    for k in range(4):
        for n in WEIGHT_NAMES:
            if n == "w_ada":
                outs.append(ada_out[k][None])
            elif n in big_out[k]:
                outs.append(big_out[k][n])
            else:
                outs.append(small_out[k][n])
    return tuple(outs)
```

```python
import jax
import jax.numpy as jnp
from jax import lax
from jax.experimental import pallas as pl
from jax.experimental.pallas import tpu as pltpu

F32 = jnp.float32
BF16 = jnp.bfloat16

N_DEV = 8
D_MODEL = 1024
D_FF = 2816
N_Q_HEADS = 8
HEAD_DIM = 64
ATTN_BLOCK = 128
ROPE_THETA = 10000.0
MLSTM_HEADS = 4
MLSTM_HEAD_DIM = 128
MLSTM_CHUNK = 128
NORM_EPS = 1e-6
ADAM_LR = 0.001
ADAM_B1 = 0.9
ADAM_B2 = 0.999
ADAM_EPS = 1e-08
ADAM_WD = 0.01
ADAM_STEP = 10

ROW_TILE = 256
WIDE_TILE = 512
LANES = 128
NEG = -1e30
VMEM_LIMIT = 56 * 1024 * 1024

A_W = 768
M_W = 2048
IF_W = 128
G_W = 2048
CAT_W = A_W + M_W + IF_W + G_W

R_SHIFT_M, R_SCALE_M, R_GATE_M, R_SHIFT_F, R_SCALE_F, R_GATE_F = 0, 1, 2, 3, 4, 5
R_G_PRE_MIX, R_G_POST_MIX, R_G_PRE_FFN, R_G_POST_FFN = 6, 7, 8, 9


def _dot(a, b):
    return jnp.dot(a, b, preferred_element_type=F32)


def _dot_nt(a, b):
    return lax.dot_general(a, b, (((1,), (1,)), ((), ())), preferred_element_type=F32)


def _dot_tn(a, b):
    return lax.dot_general(a, b, (((0,), (0,)), ((), ())), preferred_element_type=F32)


def _recip(x):
    return 1.0 / x


def _sigmoid(x):
    return _recip(1.0 + jnp.exp(-x))


def _colsum(x):
    return jnp.sum(x, axis=0, keepdims=True)


def _rowmean(x):
    return jnp.mean(x, axis=-1, keepdims=True)


def _params(sem=None, vmem=VMEM_LIMIT):
    kw = dict(vmem_limit_bytes=vmem)
    if sem is not None:
        kw["dimension_semantics"] = sem
    return pltpu.CompilerParams(**kw)


def _full(shape):
    nd = len(shape)
    return pl.BlockSpec(shape, lambda *_: (0,) * nd)


def _pre_proj(x, vecs, w_cat, after):
    S = x.shape[0]
    tm = WIDE_TILE

    def body(x_ref, v_ref, w_ref, after_ref, h_ref, pa_ref, pm_ref, pif_ref, pg_ref):
        xv = x_ref[...]
        r = lax.rsqrt(_rowmean(xv * xv) + NORM_EPS)
        h = (xv * r * v_ref[R_G_PRE_MIX:R_G_PRE_MIX + 1, :]) * (1.0 + v_ref[R_SCALE_M:R_SCALE_M + 1, :]) \
            + v_ref[R_SHIFT_M:R_SHIFT_M + 1, :]
        hb = h.astype(BF16)
        h_ref[...] = hb
        pa_ref[...] = _dot(hb, w_ref[:, 0:A_W])
        pm_ref[...] = _dot(hb, w_ref[:, A_W:A_W + M_W])
        pif_ref[...] = _dot(hb, w_ref[:, A_W + M_W:A_W + M_W + IF_W])
        pg_ref[...] = _dot(hb, w_ref[:, A_W + M_W + IF_W:CAT_W]).astype(BF16)

    row = lambda w: pl.BlockSpec((tm, w), lambda i: (i, 0))
    return pl.pallas_call(
        body, name="pre_proj", grid=(S // tm,),
        in_specs=[row(D_MODEL), _full(vecs.shape), _full(w_cat.shape), pl.BlockSpec(memory_space=pl.ANY)],
        out_specs=[row(D_MODEL), row(A_W), row(M_W), row(IF_W), row(G_W)],
        out_shape=[jax.ShapeDtypeStruct((S, D_MODEL), BF16), jax.ShapeDtypeStruct((S, A_W), F32),
                   jax.ShapeDtypeStruct((S, M_W), F32), jax.ShapeDtypeStruct((S, IF_W), F32),
                   jax.ShapeDtypeStruct((S, G_W), BF16)],
        compiler_params=_params(("parallel",)),
    )(x, vecs, w_cat, after)


def _mix_fwd(x, ya, ym, pg, vecs, w_ba, w_bm, w_out):
    S = x.shape[0]
    tm = WIDE_TILE

    def body(x_ref, ya_ref, ym_ref, pg_ref, v_ref, wba_ref, wbm_ref, wout_ref,
             x1_ref, merged_ref, mix_ref, pa_ref, pb_ref):
        pa = _dot(ya_ref[...], wba_ref[...])
        pb = _dot(ym_ref[...], wbm_ref[...])
        merged = _sigmoid(pg_ref[:, 0:D_MODEL].astype(F32)) * pa + _sigmoid(pg_ref[:, D_MODEL:G_W].astype(F32)) * pb
        mb = merged.astype(BF16)
        mix = _dot(mb, wout_ref[...])
        r = lax.rsqrt(_rowmean(mix * mix) + NORM_EPS)
        x1_ref[...] = x_ref[...] + v_ref[R_GATE_M:R_GATE_M + 1, :] * (mix * r * v_ref[R_G_POST_MIX:R_G_POST_MIX + 1, :])
        merged_ref[...] = mb
        mix_ref[...] = mix.astype(BF16)
        pa_ref[...] = pa.astype(BF16)
        pb_ref[...] = pb.astype(BF16)

    row = lambda w: pl.BlockSpec((tm, w), lambda i: (i, 0))
    sd = lambda w, dt: jax.ShapeDtypeStruct((S, w), dt)
    return pl.pallas_call(
        body, name="mix_fwd", grid=(S // tm,),
        in_specs=[row(D_MODEL), row(512), row(512), row(G_W), _full(vecs.shape), _full(w_ba.shape),
                  _full(w_bm.shape), _full(w_out.shape)],
        out_specs=[row(D_MODEL)] * 5,
        out_shape=[sd(D_MODEL, F32), sd(D_MODEL, BF16), sd(D_MODEL, BF16), sd(D_MODEL, BF16), sd(D_MODEL, BF16)],
        compiler_params=_params(("parallel",)),
    )(x, ya, ym, pg, vecs, w_ba, w_bm, w_out)


def _ffn_fwd_bwd(x1, tgt, vecs, w_gate, w_up, w_down):
    S = x1.shape[0]
    tm = ROW_TILE

    def body(x1_ref, tgt_ref, v_ref, wg_hbm, wu_hbm, wd_hbm,
             dx1_ref, h2_ref, hid_ref, da_ref, du_ref, dff_ref, acc_ref, loss_ref,
             wg, wu, wd, sem):
        i = pl.program_id(0)

        @pl.when(i == 0)
        def _():
            cps = [pltpu.make_async_copy(wg_hbm, wg, sem.at[0]), pltpu.make_async_copy(wu_hbm, wu, sem.at[1]),
                   pltpu.make_async_copy(wd_hbm, wd, sem.at[2])]
            for cp in cps:
                cp.start()
            for cp in cps:
                cp.wait()
            acc_ref[...] = jnp.zeros_like(acc_ref)
            loss_ref[...] = jnp.zeros_like(loss_ref)

        vrow = lambda r: v_ref[r:r + 1, :]
        x1v = x1_ref[...]
        r3 = lax.rsqrt(_rowmean(x1v * x1v) + NORM_EPS)
        x1hat = x1v * r3
        xn3 = x1hat * vrow(R_G_PRE_FFN)
        h2b = (xn3 * (1.0 + vrow(R_SCALE_F)) + vrow(R_SHIFT_F)).astype(BF16)
        h2_ref[...] = h2b
        a = _dot_nt(h2b, wg[...])
        u = _dot_nt(h2b, wu[...])
        sg = _sigmoid(a)
        sil = a * sg
        hidb = (sil * u).astype(BF16)
        hid_ref[...] = hidb
        ff = _dot(hidb, wd[...])
        r4 = lax.rsqrt(_rowmean(ff * ff) + NORM_EPS)
        ffhat = ff * r4
        n4 = ffhat * vrow(R_G_POST_FFN)
        err = x1v + vrow(R_GATE_F) * n4 - tgt_ref[...]
        loss_ref[...] += jnp.sum(err * err) * (0.5 / D_MODEL)
        dy = err * (1.0 / D_MODEL)
        acc_ref[0:1, :] += _colsum(dy * n4)
        dn4 = dy * vrow(R_GATE_F)
        acc_ref[1:2, :] += _colsum(dn4 * ffhat)
        dffhat = dn4 * vrow(R_G_POST_FFN)
        dffb = (r4 * (dffhat - ffhat * _rowmean(dffhat * ffhat))).astype(BF16)
        dff_ref[...] = dffb
        dhid = _dot_nt(dffb, wd[...])
        dub = (dhid * sil).astype(BF16)
        dab = (dhid * u * (sg * (1.0 + a * (1.0 - sg)))).astype(BF16)
        da_ref[...] = dab
        du_ref[...] = dub
        dh2 = _dot(dab, wg[...]) + _dot(dub, wu[...])
        acc_ref[2:3, :] += _colsum(dh2 * xn3)
        acc_ref[3:4, :] += _colsum(dh2)
        dxn3 = dh2 * (1.0 + vrow(R_SCALE_F))
        acc_ref[4:5, :] += _colsum(dxn3 * x1hat)
        dx1hat = dxn3 * vrow(R_G_PRE_FFN)
        dx1_ref[...] = dy + r3 * (dx1hat - x1hat * _rowmean(dx1hat * x1hat))

    row = lambda w: pl.BlockSpec((tm, w), lambda i: (i, 0))
    sd = lambda w, dt: jax.ShapeDtypeStruct((S, w), dt)
    anyspec = pl.BlockSpec(memory_space=pl.ANY)
    return pl.pallas_call(
        body, name="ffn_fwd_bwd", grid=(S // tm,),
        in_specs=[row(D_MODEL), row(D_MODEL), _full(vecs.shape), anyspec, anyspec, anyspec],
        out_specs=[row(D_MODEL), row(D_MODEL), row(D_FF), row(D_FF), row(D_FF), row(D_MODEL),
                   _full((8, D_MODEL)), _full((8, LANES))],
        out_shape=[sd(D_MODEL, F32), sd(D_MODEL, BF16), sd(D_FF, BF16), sd(D_FF, BF16), sd(D_FF, BF16),
                   sd(D_MODEL, BF16), jax.ShapeDtypeStruct((8, D_MODEL), F32), jax.ShapeDtypeStruct((8, LANES), F32)],
        scratch_shapes=[pltpu.VMEM(w_gate.shape, BF16), pltpu.VMEM(w_up.shape, BF16), pltpu.VMEM(w_down.shape, BF16),
                        pltpu.SemaphoreType.DMA((3,))],
        compiler_params=_params(("arbitrary",)),
    )(x1, tgt, vecs, w_gate, w_up, w_down)


def _mix_bwd(dx1, mix, pa, pb, pg, vecs, w_ba, w_bm, w_out, after):
    S = dx1.shape[0]
    tm = WIDE_TILE

    def body(dx1_ref, mix_ref, pa_ref, pb_ref, pg_ref, v_ref, wba_ref, wbm_ref, wout_ref, after_ref,
             dmix_ref, dpa_ref, dpb_ref, dg_ref, dya_ref, dym_ref, acc_ref):
        i = pl.program_id(0)

        @pl.when(i == 0)
        def _():
            acc_ref[...] = jnp.zeros_like(acc_ref)

        vrow = lambda r: v_ref[r:r + 1, :]
        dx1v = dx1_ref[...]
        mix = mix_ref[...].astype(F32)
        r2 = lax.rsqrt(_rowmean(mix * mix) + NORM_EPS)
        mixhat = mix * r2
        acc_ref[0:1, :] += _colsum(dx1v * (mixhat * vrow(R_G_POST_MIX)))
        dn2 = dx1v * vrow(R_GATE_M)
        acc_ref[1:2, :] += _colsum(dn2 * mixhat)
        dmixhat = dn2 * vrow(R_G_POST_MIX)
        dmixb = (r2 * (dmixhat - mixhat * _rowmean(dmixhat * mixhat))).astype(BF16)
        dmix_ref[...] = dmixb
        dmerged = _dot_nt(dmixb, wout_ref[...])
        sa = _sigmoid(pg_ref[:, 0:D_MODEL].astype(F32))
        sm = _sigmoid(pg_ref[:, D_MODEL:G_W].astype(F32))
        dpab = (dmerged * sa).astype(BF16)
        dpbb = (dmerged * sm).astype(BF16)
        dpa_ref[...] = dpab
        dpb_ref[...] = dpbb
        dg_ref[:, 0:D_MODEL] = (dmerged * pa_ref[...].astype(F32) * (sa * (1.0 - sa))).astype(BF16)
        dg_ref[:, D_MODEL:G_W] = (dmerged * pb_ref[...].astype(F32) * (sm * (1.0 - sm))).astype(BF16)
        dya_ref[...] = _dot_nt(dpab, wba_ref[...]).astype(BF16)
        dym_ref[...] = _dot_nt(dpbb, wbm_ref[...])

    row = lambda w: pl.BlockSpec((tm, w), lambda i: (i, 0))
    sd = lambda w, dt: jax.ShapeDtypeStruct((S, w), dt)
    return pl.pallas_call(
        body, name="mix_bwd", grid=(S // tm,),
        in_specs=[row(D_MODEL), row(D_MODEL), row(D_MODEL), row(D_MODEL), row(G_W), _full(vecs.shape),
                  _full(w_ba.shape), _full(w_bm.shape), _full(w_out.shape), pl.BlockSpec(memory_space=pl.ANY)],
        out_specs=[row(D_MODEL), row(D_MODEL), row(D_MODEL), row(G_W), row(512), row(512), _full((8, D_MODEL))],
        out_shape=[sd(D_MODEL, BF16), sd(D_MODEL, BF16), sd(D_MODEL, BF16), sd(G_W, BF16), sd(512, BF16), sd(512, F32),
                   jax.ShapeDtypeStruct((8, D_MODEL), F32)],
        compiler_params=_params(("arbitrary",)),
    )(dx1, mix, pa, pb, pg, vecs, w_ba, w_bm, w_out, after)


def _pre_bwd(pieces, x, dx1, vecs, w_cat, after):
    S = x.shape[0]
    tm = WIDE_TILE
    n = len(pieces)
    starts = [sum(p.shape[1] for p in pieces[:k]) for k in range(n + 1)]

    def body(*refs):
        p_refs = refs[:n]
        x_ref, dx1_ref, v_ref, w_ref, after_ref, dx_ref, acc_ref = refs[n:]
        i = pl.program_id(0)

        @pl.when(i == 0)
        def _():
            acc_ref[...] = jnp.zeros_like(acc_ref)

        vrow = lambda r: v_ref[r:r + 1, :]
        dh = _dot_nt(p_refs[0][...], w_ref[:, starts[0]:starts[1]])
        for k in range(1, n):
            dh = dh + _dot_nt(p_refs[k][...], w_ref[:, starts[k]:starts[k + 1]])
        xv = x_ref[...]
        r1 = lax.rsqrt(_rowmean(xv * xv) + NORM_EPS)
        xhat = xv * r1
        acc_ref[0:1, :] += _colsum(dh * (xhat * vrow(R_G_PRE_MIX)))
        acc_ref[1:2, :] += _colsum(dh)
        dxn = dh * (1.0 + vrow(R_SCALE_M))
        acc_ref[2:3, :] += _colsum(dxn * xhat)
        dxhat = dxn * vrow(R_G_PRE_MIX)
        dx_ref[...] = dx1_ref[...] + r1 * (dxhat - xhat * _rowmean(dxhat * xhat))

    row = lambda w: pl.BlockSpec((tm, w), lambda i: (i, 0))
    return pl.pallas_call(
        body, name="pre_bwd", grid=(S // tm,),
        in_specs=[row(p.shape[1]) for p in pieces] + [row(D_MODEL), row(D_MODEL), _full(vecs.shape), _full(w_cat.shape),
                                                         pl.BlockSpec(memory_space=pl.ANY)],
        out_specs=[row(D_MODEL), _full((8, D_MODEL))],
        out_shape=[jax.ShapeDtypeStruct((S, D_MODEL), F32), jax.ShapeDtypeStruct((8, D_MODEL), F32)],
        compiler_params=_params(("arbitrary",)),
    )(*pieces, x, dx1, vecs, w_cat, after)


def _matmul_tn(a, b, tn, name, ts=1024):
    S, K = a.shape
    N = b.shape[1]
    n_s = S // ts

    def body(a_ref, b_ref, o_ref, acc_ref):
        s = pl.program_id(1)

        @pl.when(s == 0)
        def _():
            acc_ref[...] = jnp.zeros_like(acc_ref)

        acc_ref[...] += _dot_tn(a_ref[...], b_ref[...])

        @pl.when(s == n_s - 1)
        def _():
            o_ref[...] = acc_ref[...].astype(BF16)

    return pl.pallas_call(
        body, name=name, grid=(N // tn, n_s),
        in_specs=[pl.BlockSpec((ts, K), lambda j, s: (s, 0)), pl.BlockSpec((ts, tn), lambda j, s: (s, j))],
        out_specs=pl.BlockSpec((K, tn), lambda j, s: (0, j)),
        out_shape=jax.ShapeDtypeStruct((K, N), BF16),
        scratch_shapes=[pltpu.VMEM((K, tn), F32)],
        compiler_params=_params(("parallel", "arbitrary")),
    )(a, b)


def _rope_swap(t):
    lane = lax.broadcasted_iota(jnp.int32, t.shape, 1)
    first = (lane & (HEAD_DIM - 1)) < (HEAD_DIM // 2)
    return jnp.where(first, pltpu.roll(t, LANES - HEAD_DIM // 2, 1), pltpu.roll(t, HEAD_DIM // 2, 1))


def _rope(t, cos, sin_signed):
    return t * cos + _rope_swap(t) * sin_signed


def _rope_t(d, cos, sin_signed):
    return d * cos + _rope_swap(d * sin_signed)


def _to_kv_lanes(chunk, p, h):
    lane = lax.broadcasted_iota(jnp.int32, chunk.shape, 1)
    src = chunk if p == h else pltpu.roll(chunk, HEAD_DIM, 1)
    return jnp.where((lane >> 6) == h, src, jnp.zeros_like(src))


def _from_kv_lanes(o_a, o_b, h):
    lane = lax.broadcasted_iota(jnp.int32, o_a.shape, 1)
    a = o_a if h == 0 else pltpu.roll(o_a, HEAD_DIM, 1)
    b = o_b if h == 1 else pltpu.roll(o_b, HEAD_DIM, 1)
    return jnp.where(lane < HEAD_DIM, a, b)


def _band_bias(n):
    blk = ATTN_BLOCK
    qi = lax.broadcasted_iota(jnp.int32, (blk, 2 * blk), 0)
    kj = lax.broadcasted_iota(jnp.int32, (blk, 2 * blk), 1)
    seen = (kj > qi) & (kj <= qi + blk) & ((n > 0) | (kj >= blk))
    return jnp.concatenate([jnp.where(seen, 0.0, NEG)] * N_Q_HEADS, axis=0)


def _stack_heads(chunks, h, dtype):
    parts = []
    for g in range(4):
        j = 4 * h + g
        parts.append(_to_kv_lanes(chunks[j // 2], j % 2, h))
    return jnp.concatenate(parts, axis=0).astype(dtype)


def _fused_call(parts, name, n_steps):
    counts = [(len(p["in_specs"]), len(p["out_specs"]), len(p["scratch"])) for p in parts]
    n_in, n_out = sum(c[0] for c in counts), sum(c[1] for c in counts)

    def kernel_fn(*refs):
        i = pl.program_id(0)
        groups, a, b, c = [], 0, n_in, n_in + n_out
        for ci, co, cs in counts:
            groups.append(refs[a:a + ci] + refs[b:b + co] + refs[c:c + cs])
            a, b, c = a + ci, b + co, c + cs
        for p, g in zip(parts, groups):
            p["init"](i, *g)
        for p, g in zip(parts, groups):
            p["body"](i, *g)

    flat = lambda key: [v for p in parts for v in p[key]]
    res = pl.pallas_call(
        kernel_fn, name=name, grid=(n_steps,), in_specs=flat("in_specs"), out_specs=flat("out_specs"),
        out_shape=flat("out_shape"), scratch_shapes=flat("scratch"), compiler_params=_params(("arbitrary",)),
    )(*flat("operands"))
    out, pos = [], 0
    for _, co, _ in counts:
        out.append(res[pos:pos + co])
        pos += co
    return out


def _attn_fwd_part(pa, cos, sin, sinks):
    S = pa.shape[0]
    blk = ATTN_BLOCK
    nb = S // blk

    def body(n, sink_ref, cur_ref, prev_ref, cos_ref, sin_ref, cosp_ref, sinp_ref,
             ya_ref, qr_ref, kr_ref, vb_ref, lse_ref):
        cos_c, sin_c = cos_ref[...], sin_ref[...]
        qch = [_rope(cur_ref[:, c * LANES:(c + 1) * LANES], cos_c, sin_c) * (HEAD_DIM ** -0.5) for c in range(4)]
        for c in range(4):
            qr_ref[:, c * LANES:(c + 1) * LANES] = qch[c].astype(BF16)
        k_cur = _rope(cur_ref[:, 512:640], cos_c, sin_c).astype(BF16)
        k_prev = _rope(prev_ref[:, 0:LANES], cosp_ref[...], sinp_ref[...]).astype(BF16)
        v_cur = cur_ref[:, 640:768].astype(BF16)
        v_prev = prev_ref[:, LANES:2 * LANES].astype(BF16)
        kr_ref[...] = k_cur
        vb_ref[...] = v_cur
        K = jnp.concatenate([k_prev, k_cur], axis=0)
        V = jnp.concatenate([v_prev, v_cur], axis=0)
        lane = lax.broadcasted_iota(jnp.int32, (blk, LANES), 1)
        s = jnp.concatenate([_dot_nt(_stack_heads(qch, h, BF16), K) for h in range(2)], axis=0)
        s = s + _band_bias(n)
        rowmax = jnp.max(s, axis=1, keepdims=True)
        hd = lambda x, j: x[j * blk:(j + 1) * blk]
        m = jnp.concatenate([jnp.maximum(hd(rowmax, j), sink_ref[j]) for j in range(N_Q_HEADS)], axis=0)
        p = jnp.exp(s - m)
        pb = p.astype(BF16)
        den = _rowsum_mxu(pb) + jnp.concatenate([jnp.exp(sink_ref[j] - hd(m, j)) for j in range(N_Q_HEADS)], axis=0)
        o = jnp.concatenate([_dot(pb[4 * h * blk:4 * (h + 1) * blk], V) for h in range(2)], axis=0) * _recip(den)
        lse = m + jnp.log(den)
        outs = [o[j * blk:(j + 1) * blk, :] for j in range(N_Q_HEADS)]
        lse_tile = jnp.zeros((blk, LANES), F32)
        for j in range(N_Q_HEADS):
            lse_tile = jnp.where(lane == j, lse[j * blk:(j + 1) * blk, :], lse_tile)
        for c in range(4):
            ya_ref[:, c * LANES:(c + 1) * LANES] = _from_kv_lanes(outs[2 * c], outs[2 * c + 1], c // 2).astype(BF16)
        lse_ref[...] = lse_tile

    prev = lambda n: jnp.maximum(n - 1, 0)
    sd = lambda w, dt: jax.ShapeDtypeStruct((S, w), dt)
    return dict(
        init=lambda n, *refs: None, body=body, scratch=[], operands=[sinks, pa, pa, cos, sin, cos, sin],
        in_specs=[pl.BlockSpec(memory_space=pltpu.SMEM),
                  pl.BlockSpec((blk, A_W), lambda n: (n, 0)),
                  pl.BlockSpec((blk, 256), lambda n: (prev(n), 2)),
                  pl.BlockSpec((blk, LANES), lambda n: (n, 0)), pl.BlockSpec((blk, LANES), lambda n: (n, 0)),
                  pl.BlockSpec((blk, LANES), lambda n: (prev(n), 0)), pl.BlockSpec((blk, LANES), lambda n: (prev(n), 0))],
        out_specs=[pl.BlockSpec((blk, 512), lambda n: (n, 0)), pl.BlockSpec((blk, 512), lambda n: (n, 0)),
                   pl.BlockSpec((blk, LANES), lambda n: (n, 0)), pl.BlockSpec((blk, LANES), lambda n: (n, 0)),
                   pl.BlockSpec((blk, LANES), lambda n: (n, 0))],
        out_shape=[sd(512, BF16), sd(512, BF16), sd(LANES, BF16), sd(LANES, BF16), sd(LANES, F32)])


def _attn_bwd_part(dya, qr, kr, vb, lse, cos, sin, sinks):
    S = dya.shape[0]
    blk = ATTN_BLOCK
    nb = S // blk

    def init(n, sink_ref, dya_ref, qr_ref, kc_ref, kp_ref, vc_ref, vp_ref, lse_ref, cos_ref, sin_ref, cosp_ref, sinp_ref,
             dqkv_ref, last_ref, dsink_ref, ck, cv, cq):
        @pl.when(n == 0)
        def _():
            ck[...] = jnp.zeros_like(ck)
            cv[...] = jnp.zeros_like(cv)
            cq[...] = jnp.zeros_like(cq)
            dsink_ref[...] = jnp.zeros_like(dsink_ref)

    def body(n, sink_ref, dya_ref, qr_ref, kc_ref, kp_ref, vc_ref, vp_ref, lse_ref, cos_ref, sin_ref, cosp_ref, sinp_ref,
             dqkv_ref, last_ref, dsink_ref, ck, cv, cq):
        K = jnp.concatenate([kp_ref[...], kc_ref[...]], axis=0)
        V = jnp.concatenate([vp_ref[...], vc_ref[...]], axis=0)
        qch = [qr_ref[:, c * LANES:(c + 1) * LANES] for c in range(4)]
        dch = [dya_ref[:, c * LANES:(c + 1) * LANES].astype(F32) for c in range(4)]
        lse_tile = lse_ref[...]
        lane8 = lax.broadcasted_iota(jnp.int32, (8, LANES), 1)
        grp = lambda x, h: x[4 * h * blk:4 * (h + 1) * blk]
        qs = jnp.concatenate([_stack_heads(qch, h, BF16) for h in range(2)], axis=0)
        dos = jnp.concatenate([_stack_heads(dch, h, BF16) for h in range(2)], axis=0)
        lse_col = jnp.concatenate([lse_tile[:, j:j + 1] for j in range(N_Q_HEADS)], axis=0)
        s = jnp.concatenate([_dot_nt(grp(qs, h), K) for h in range(2)], axis=0)
        p = jnp.exp(s + _band_bias(n) - lse_col)
        dp = jnp.concatenate([_dot_nt(grp(dos, h), V) for h in range(2)], axis=0)
        delta = jnp.sum(p * dp, axis=1, keepdims=True)
        dsb = (p * (dp - delta)).astype(BF16)
        pb = p.astype(BF16)
        dq = jnp.concatenate([_dot(grp(dsb, h), K) for h in range(2)], axis=0)
        dk_acc = _dot_tn(grp(dsb, 0), grp(qs, 0)) + _dot_tn(grp(dsb, 1), grp(qs, 1))
        dv_acc = _dot_tn(grp(pb, 0), grp(dos, 0)) + _dot_tn(grp(pb, 1), grp(dos, 1))
        dqs = [dq[j * blk:(j + 1) * blk, :] for j in range(N_Q_HEADS)]
        dsink = jnp.zeros((8, LANES), F32)
        for j in range(N_Q_HEADS):
            rows = slice(j * blk, (j + 1) * blk)
            ps_delta = jnp.exp(sink_ref[j] - lse_col[rows]) * delta[rows]
            dsink = jnp.where(lane8 == j, dsink - jnp.sum(ps_delta), dsink)
        dsink_ref[...] += dsink
        cos_c, sin_c = cos_ref[...], sin_ref[...]
        dqkv_ref[:, 0:512] = cq[...]
        dqkv_ref[:, 512:640] = _rope_t(dk_acc[0:blk, :] + ck[...], cosp_ref[...], sinp_ref[...]).astype(BF16)
        dqkv_ref[:, 640:768] = (dv_acc[0:blk, :] + cv[...]).astype(BF16)
        ck[...] = dk_acc[blk:2 * blk, :]
        cv[...] = dv_acc[blk:2 * blk, :]
        for c in range(4):
            dqc = _from_kv_lanes(dqs[2 * c], dqs[2 * c + 1], c // 2) * (HEAD_DIM ** -0.5)
            dq_c = _rope_t(dqc, cos_c, sin_c).astype(BF16)
            cq[:, c * LANES:(c + 1) * LANES] = dq_c
            last_ref[:, c * LANES:(c + 1) * LANES] = dq_c
        last_ref[:, 512:640] = _rope_t(dk_acc[blk:2 * blk, :], cos_c, sin_c).astype(BF16)
        last_ref[:, 640:768] = dv_acc[blk:2 * blk, :].astype(BF16)

    prev = lambda n: jnp.maximum(n - 1, 0)
    same = lambda n: n
    bs = lambda w, f: pl.BlockSpec((blk, w), lambda n: (f(n), 0))
    return dict(
        init=init, body=body, operands=[sinks, dya, qr, kr, kr, vb, vb, lse, cos, sin, cos, sin],
        in_specs=[pl.BlockSpec(memory_space=pltpu.SMEM),
                  bs(512, same), bs(512, same), bs(LANES, same), bs(LANES, prev), bs(LANES, same), bs(LANES, prev),
                  bs(LANES, same), bs(LANES, same), bs(LANES, same), bs(LANES, prev), bs(LANES, prev)],
        out_specs=[bs(A_W, prev), _full((blk, A_W)), _full((8, LANES))],
        out_shape=[jax.ShapeDtypeStruct((S, A_W), BF16), jax.ShapeDtypeStruct((blk, A_W), BF16),
                   jax.ShapeDtypeStruct((8, LANES), F32)],
        scratch=[pltpu.VMEM((blk, LANES), F32), pltpu.VMEM((blk, LANES), F32), pltpu.VMEM((blk, 512), BF16)])


def _split3(x):
    hi = x.astype(BF16)
    r1 = x - hi.astype(F32)
    mid = r1.astype(BF16)
    lo = (r1 - mid.astype(F32)).astype(BF16)
    return hi, mid, lo


def _tri_matmul(tri_b, x):
    hi, mid, lo = _split3(x)
    return _dot(tri_b, hi) + _dot(tri_b, mid) + _dot(tri_b, lo)


def _log_sigmoid(x):
    return jnp.minimum(x, 0.0) - jnp.log(1.0 + jnp.exp(-jnp.abs(x)))


def _shift_rows(cur, seam, k, down):
    L = cur.shape[0]
    row8 = lax.broadcasted_iota(jnp.int32, seam.shape, 0)
    if down:
        mixed = jnp.concatenate([cur[:L - 8], jnp.where(row8 >= 8 - k, seam, cur[L - 8:])], axis=0)
        return pltpu.roll(mixed, k, 0)
    mixed = jnp.concatenate([jnp.where(row8 < k, seam, cur[:8]), cur[8:]], axis=0)
    return pltpu.roll(mixed, L - k, 0)


def _conv_fwd(cur, tail, cw_ref):
    z = cw_ref[4:5, :]
    for k in range(3, 0, -1):
        z = z + _shift_rows(cur, tail, k, True) * cw_ref[3 - k:4 - k, :]
    return z + cur * cw_ref[3:4, :]


def _rowsum_mxu(x, two_pass=False):
    ones = jnp.ones((x.shape[1], LANES), BF16)
    hi = x.astype(BF16)
    s = _dot(hi, ones)
    if two_pass:
        s = s + _dot((x - hi.astype(F32)).astype(BF16), ones)
    return s


def _stack(f):
    return jnp.concatenate([f(h) for h in range(MLSTM_HEADS)], axis=0)


def _head(x, h):
    L = x.shape[0] // MLSTM_HEADS
    return x[h * L:(h + 1) * L]


def _mlstm_heads_fwd(qk, cur_ref, gt, b_all, c_prev, nmv, tri, eye):
    L = qk.shape[0]
    HD = MLSTM_HEAD_DIM
    W4 = MLSTM_HEADS * HD
    col2row = lambda x: jnp.sum(jnp.where(eye, x, 0.0), axis=0, keepdims=True)
    b_col = _stack(lambda h: b_all[:, 4 + h:5 + h])
    i_col = _stack(lambda h: gt[:, h:h + 1])
    b_row = _stack(lambda h: jnp.broadcast_to(col2row(b_all[:, 4 + h:5 + h]), (L, L)))
    i_row = _stack(lambda h: jnp.broadcast_to(col2row(gt[:, h:h + 1]), (L, L)))
    bl = _stack(lambda h: jnp.broadcast_to(b_all[L - 1:L, 4 + h:5 + h], (L, 1)))
    m_prev = _stack(lambda h: jnp.broadcast_to(nmv[4 + h:5 + h, 0:1], (L, 1)))
    n_prev = _stack(lambda h: jnp.broadcast_to(nmv[h:h + 1, :], (L, HD)))
    tri4 = jnp.concatenate([tri] * MLSTM_HEADS, axis=0)
    Dm = jnp.where(tri4, b_col - b_row + i_row, NEG)
    inter = b_col + m_prev
    m_t = jnp.maximum(inter, jnp.max(Dm, axis=1, keepdims=True))
    W = jnp.exp(Dm - m_t)
    e_t = jnp.exp(inter - m_t)
    q = _stack(lambda h: qk[:, h * HD:(h + 1) * HD])
    k = _stack(lambda h: qk[:, W4 + h * HD:W4 + (h + 1) * HD]) * (HD ** -0.5)
    v = _stack(lambda h: cur_ref[:, 2 * W4 + h * HD:2 * W4 + (h + 1) * HD])
    qb, kb, vb = q.astype(BF16), k.astype(BF16), v.astype(BF16)
    Sc = _stack(lambda h: _dot_nt(_head(qb, h), _head(kb, h))) * W
    Scb = Sc.astype(BF16)
    cb = [c.astype(BF16) for c in c_prev]
    P1 = _stack(lambda h: _dot(_head(qb, h), cb[h]))
    num = _stack(lambda h: _dot(_head(Scb, h), _head(vb, h))) + e_t * P1
    qn = _rowsum_mxu(q * n_prev)
    den = _rowsum_mxu(Scb) + e_t * qn
    floor = jnp.broadcast_to(jnp.exp(-m_t), den.shape)
    inv_g = _recip(jnp.maximum(jnp.abs(den), floor))
    hv = num * inv_g
    a_col = bl - b_col + i_col
    a_max = _stack(lambda h: jnp.broadcast_to(jnp.max(_head(a_col, h), axis=0, keepdims=True), (L, 1)))
    m_new = jnp.maximum(bl + m_prev, a_max)
    dec = jnp.exp(bl + m_prev - m_new)
    u_col = jnp.exp(a_col - m_new)
    return dict(W=W, e_t=e_t, q=q, k=k, v=v, qb=qb, kb=kb, vb=vb, cb=cb, Sc=Sc, Scb=Scb, P1=P1, qn=qn, den=den,
                floor=floor, inv_g=inv_g, hv=hv, n_prev=n_prev, m_new=m_new, dec=dec, u_col=u_col)


def _mlstm_fwd_part(pm, pif, cw, sv):
    S = pm.shape[0]
    L = MLSTM_CHUNK
    nc = S // L
    HD = MLSTM_HEAD_DIM
    W4 = MLSTM_HEADS * HD

    def init(c, cur_ref, pif_ref, cw_ref, sv_ref, ym_ref, z_ref, cst_ref, nst_ref, C, nm, tail):
        @pl.when(c == 0)
        def _():
            C[...] = jnp.zeros_like(C)
            nm[...] = jnp.zeros_like(nm)
            tail[...] = jnp.zeros_like(tail)

    def body(c, cur_ref, pif_ref, cw_ref, sv_ref, ym_ref, z_ref, cst_ref, nst_ref, C, nm, tail):
        z = _conv_fwd(cur_ref[:, 0:2 * W4], tail[...], cw_ref)
        tail[...] = cur_ref[L - 8:L, 0:2 * W4]
        z_ref[...] = z
        qk = z * _sigmoid(z)
        gt = pif_ref[...] + sv_ref[1:2, 0:LANES]
        r_i = lax.broadcasted_iota(jnp.int32, (L, L), 0)
        c_i = lax.broadcasted_iota(jnp.int32, (L, L), 1)
        tri = c_i <= r_i
        eye = c_i == r_i
        b_all = _tri_matmul(tri.astype(BF16), _log_sigmoid(gt))
        nmv = nm[...]
        nst_ref[0] = nmv
        c_prev = [C[h] for h in range(MLSTM_HEADS)]
        f = _mlstm_heads_fwd(qk, cur_ref, gt, b_all, c_prev, nmv, tri, eye)
        hv = f["hv"]
        xc = hv - _rowsum_mxu(hv, True) * (1.0 / HD)
        hhat = xc * lax.rsqrt(_rowsum_mxu(xc * xc) * (1.0 / HD) + NORM_EPS)
        so = _sigmoid(_stack(lambda h: cur_ref[:, 3 * W4 + h * HD:3 * W4 + (h + 1) * HD]))
        wn = _stack(lambda h: jnp.broadcast_to(sv_ref[0:1, h * HD:(h + 1) * HD], (L, HD)))
        y = (so * hhat * wn).astype(BF16)
        kw = f["k"] * f["u_col"]
        kwb = kw.astype(BF16)
        n_new, m_new = [], []
        for h in range(MLSTM_HEADS):
            cst_ref[0, h] = c_prev[h]
            ym_ref[:, h * HD:(h + 1) * HD] = _head(y, h)
            dec = f["dec"][h * L:h * L + 1, :]
            C[h] = dec * c_prev[h] + _dot_tn(_head(kwb, h), _head(f["vb"], h))
            n_new.append(dec * nmv[h:h + 1, :] + _colsum(_head(kw, h)))
            m_new.append(jnp.broadcast_to(f["m_new"][h * L:h * L + 1, :], (1, LANES)))
        nm[...] = jnp.concatenate(n_new + m_new, axis=0)

    return dict(
        init=init, body=body, operands=[pm, pif, cw, sv],
        in_specs=[pl.BlockSpec((L, M_W), lambda c: (c, 0)),
                  pl.BlockSpec((L, IF_W), lambda c: (c, 0)), _full(cw.shape), _full(sv.shape)],
        out_specs=[pl.BlockSpec((L, W4), lambda c: (c, 0)), pl.BlockSpec((L, 2 * W4), lambda c: (c, 0)),
                   pl.BlockSpec((1, MLSTM_HEADS, HD, HD), lambda c: (c, 0, 0, 0)),
                   pl.BlockSpec((1, 8, LANES), lambda c: (c, 0, 0))],
        out_shape=[jax.ShapeDtypeStruct((S, W4), BF16), jax.ShapeDtypeStruct((S, 2 * W4), F32),
                   jax.ShapeDtypeStruct((nc, MLSTM_HEADS, HD, HD), F32), jax.ShapeDtypeStruct((nc, 8, LANES), F32)],
        scratch=[pltpu.VMEM((MLSTM_HEADS, HD, HD), F32), pltpu.VMEM((8, LANES), F32), pltpu.VMEM((8, 2 * W4), F32)])


def _mlstm_bwd_part(pm, zc, pif, cw, sv, dym, cst, nst):
    S = pm.shape[0]
    L = MLSTM_CHUNK
    nc = S // L
    HD = MLSTM_HEAD_DIM
    W4 = MLSTM_HEADS * HD

    def init(r, cur_ref, z_ref, pif_ref, cw_ref, sv_ref, dym_ref, cst_ref, nst_ref,
             dm_ref, dcw_ref, dsv_ref, dC, dn, dz_next, dqk):
        @pl.when(r == 0)
        def _():
            dC[...] = jnp.zeros_like(dC)
            dn[...] = jnp.zeros_like(dn)
            dz_next[...] = jnp.zeros_like(dz_next)
            dcw_ref[...] = jnp.zeros_like(dcw_ref)
            dsv_ref[...] = jnp.zeros_like(dsv_ref)

    def body(r, cur_ref, z_ref, pif_ref, cw_ref, sv_ref, dym_ref, cst_ref, nst_ref,
             dm_ref, dcw_ref, dsv_ref, dC, dn, dz_next, dqk):
        z = z_ref[...]
        sgz = _sigmoid(z)
        qk = z * sgz
        gt = pif_ref[...] + sv_ref[1:2, 0:LANES]
        r_i = lax.broadcasted_iota(jnp.int32, (L, L), 0)
        c_i = lax.broadcasted_iota(jnp.int32, (L, L), 1)
        tri = c_i <= r_i
        eye = c_i == r_i
        b_all = _tri_matmul(tri.astype(BF16), _log_sigmoid(gt))
        lane = lax.broadcasted_iota(jnp.int32, (L, LANES), 1)
        rowl = lax.broadcasted_iota(jnp.int32, (L, 1), 0)
        nmv = nst_ref[0]
        heads = range(MLSTM_HEADS)
        c_prev = [cst_ref[0, h] for h in heads]
        f = _mlstm_heads_fwd(qk, cur_ref, gt, b_all, c_prev, nmv, tri, eye)
        hv, inv_g, den, e_t, u_col, n_prev = f["hv"], f["inv_g"], f["den"], f["e_t"], f["u_col"], f["n_prev"]
        q, k, v, qb, kb, vb, Sc, Scb, W = f["q"], f["k"], f["v"], f["qb"], f["kb"], f["vb"], f["Sc"], f["Scb"], f["W"]
        xc = hv - _rowsum_mxu(hv, True) * (1.0 / HD)
        rstd = lax.rsqrt(_rowsum_mxu(xc * xc) * (1.0 / HD) + NORM_EPS)
        hhat = xc * rstd
        wn = _stack(lambda h: jnp.broadcast_to(sv_ref[0:1, h * HD:(h + 1) * HD], (L, HD)))
        so = _sigmoid(_stack(lambda h: cur_ref[:, 3 * W4 + h * HD:3 * W4 + (h + 1) * HD]))
        dy = _stack(lambda h: dym_ref[:, h * HD:(h + 1) * HD])
        d_o = (dy * hhat * wn * (so * (1.0 - so))).astype(BF16)
        dln = dy * so
        dwn = dln * hhat
        dhhat = dln * wn
        m2 = _rowsum_mxu(dhhat * hhat) * (1.0 / HD)
        dh = rstd * (dhhat - _rowsum_mxu(dhhat) * (1.0 / HD) - hhat * m2)
        dnum = dh * inv_g
        active = jnp.abs(den) > f["floor"]
        dden = jnp.where(active, -(HD * NORM_EPS) * m2 * rstd * rstd * inv_g * jnp.where(den >= 0.0, 1.0, -1.0), 0.0)
        dnumb = dnum.astype(BF16)
        dSc = _stack(lambda h: _dot_nt(_head(dnumb, h), _head(vb, h))) + dden
        dA = (dSc * W).astype(BF16)
        G = dSc * Sc
        Gb = G.astype(BF16)
        Gl = (G - Gb.astype(F32)).astype(BF16)
        ones = jnp.ones((L, LANES), BF16)
        Gr = _dot(Gb, ones) + _dot(Gl, ones)
        Gc = _stack(lambda h: _dot_tn(_head(Gb, h), ones) + _dot_tn(_head(Gl, h), ones))
        dCn = [dC[h] for h in heads]
        dCnb = [d.astype(BF16) for d in dCn]
        dnv = dn[...]
        dn_new = _stack(lambda h: jnp.broadcast_to(dnv[h:h + 1, :], (L, HD)))
        kdC = _stack(lambda h: _dot(_head(kb, h), dCnb[h]))
        vdC = _stack(lambda h: _dot_nt(_head(vb, h), dCnb[h]))
        dv = (_stack(lambda h: _dot_tn(_head(Scb, h), _head(dnumb, h))) + u_col * kdC).astype(BF16)
        dq = _stack(lambda h: _dot(_head(dA, h), _head(kb, h))) \
            + e_t * _stack(lambda h: _dot_nt(_head(dnumb, h), f["cb"][h])) + (e_t * dden) * n_prev
        dk = (_stack(lambda h: _dot_tn(_head(dA, h), _head(qb, h))) + u_col * (vdC + dn_new)) * (HD ** -0.5)
        E = (_rowsum_mxu(f["P1"] * dnum, True) + dden * f["qn"]) * e_t
        U = _rowsum_mxu(kdC * v + k * dn_new, True) * u_col
        qe = (q * e_t).astype(BF16)
        qd = (e_t * dden) * q
        di = Gc + U
        db = Gr + E - Gc - U
        di_tile = jnp.zeros((L, LANES), F32)
        db_tile = jnp.zeros((L, LANES), F32)
        dn_rows = []
        for h in heads:
            dec = f["dec"][h * L:h * L + 1, :]
            ddec = jnp.sum(dCn[h] * c_prev[h]) + jnp.sum(dnv[h:h + 1, :] * nmv[h:h + 1, :])
            dbl = ddec * dec + jnp.sum(_head(U, h), axis=0, keepdims=True)
            di_tile = jnp.where(lane == h, _head(di, h), di_tile)
            db_tile = jnp.where(lane == 4 + h, _head(db, h) + jnp.where(rowl == L - 1, dbl, 0.0), db_tile)
            dC[h] = dec * dCn[h] + _dot_tn(_head(qe, h), _head(dnumb, h))
            dn_rows.append(dec * dnv[h:h + 1, :] + _colsum(_head(qd, h)))
            dsv_ref[0:1, h * HD:(h + 1) * HD] += _colsum(_head(dwn, h))
            dqk[:, h * HD:(h + 1) * HD] = _head(dq, h)
            dqk[:, W4 + h * HD:W4 + (h + 1) * HD] = _head(dk, h)
            dm_ref[:, 2 * W4 + h * HD:2 * W4 + (h + 1) * HD] = _head(dv, h)
            dm_ref[:, 3 * W4 + h * HD:3 * W4 + (h + 1) * HD] = _head(d_o, h)
        dn[...] = jnp.concatenate(dn_rows + [jnp.zeros((8 - MLSTM_HEADS, LANES), F32)], axis=0)
        dlf = _tri_matmul((r_i <= c_i).astype(BF16), db_tile)
        dif = jnp.where(lane < 4, di_tile, jnp.where(lane < 8, dlf * (1.0 - _sigmoid(gt)), 0.0))
        dm_ref[:, M_W:M_W + IF_W] = dif.astype(BF16)
        dsv_ref[1:2, 0:LANES] += _colsum(dif)
        dz = dqk[...] * (sgz * (1.0 + z * (1.0 - sgz)))
        dcw_ref[4:5, :] += _colsum(dz)
        u = cur_ref[:, 0:2 * W4]
        du_in = dz * cw_ref[3:4, :]
        dcw_ref[3:4, :] += _colsum(dz * u)
        for k in range(1, 4):
            up = _shift_rows(dz, dz_next[...], k, False)
            dcw_ref[3 - k:4 - k, :] += _colsum(up * u)
            du_in = du_in + up * cw_ref[3 - k:4 - k, :]
        dz_next[...] = dz[0:8, :]
        dm_ref[:, 0:2 * W4] = du_in.astype(BF16)

    cidx = lambda r: nc - 1 - r
    return dict(
        init=init, body=body, operands=[pm, zc, pif, cw, sv, dym, cst, nst],
        in_specs=[pl.BlockSpec((L, M_W), lambda r: (cidx(r), 0)), pl.BlockSpec((L, 2 * W4), lambda r: (cidx(r), 0)),
                  pl.BlockSpec((L, IF_W), lambda r: (cidx(r), 0)), _full(cw.shape), _full(sv.shape),
                  pl.BlockSpec((L, W4), lambda r: (cidx(r), 0)),
                  pl.BlockSpec((1, MLSTM_HEADS, HD, HD), lambda r: (cidx(r), 0, 0, 0)),
                  pl.BlockSpec((1, 8, LANES), lambda r: (cidx(r), 0, 0))],
        out_specs=[pl.BlockSpec((L, M_W + IF_W), lambda r: (cidx(r), 0)), _full((8, 2 * W4)), _full((8, W4))],
        out_shape=[jax.ShapeDtypeStruct((S, M_W + IF_W), BF16),
                   jax.ShapeDtypeStruct((8, 2 * W4), F32), jax.ShapeDtypeStruct((8, W4), F32)],
        scratch=[pltpu.VMEM((MLSTM_HEADS, HD, HD), F32), pltpu.VMEM((8, LANES), F32),
                 pltpu.VMEM((8, 2 * W4), F32), pltpu.VMEM((L, 2 * W4), F32)])


def _rope_tables(positions):
    half = HEAD_DIM // 2
    inv_freq = ROPE_THETA ** (-2.0 * jnp.arange(half, dtype=F32) / HEAD_DIM)
    ang = positions.astype(F32)[:, None] * inv_freq
    cos = jnp.tile(jnp.cos(ang), (1, LANES // half))
    sign = jnp.tile(jnp.concatenate([-jnp.ones((half,), F32), jnp.ones((half,), F32)]), LANES // HEAD_DIM)
    sin = jnp.tile(jnp.sin(ang), (1, LANES // half)) * sign
    return cos, sin


def _local_step(x, tgt, positions, mod, gains, w_cat, w_ba, w_bm, w_out, w_gate, w_up, w_down,
                conv_w, conv_b, b_if, sinks, norm_w):
    t = _tables(mod, gains, conv_w, conv_b, b_if, norm_w, positions)
    a = _mixer_fwd(x, t, sinks, w_cat)
    b = _ffn_part(x, tgt, t, a, w_ba, w_bm, w_out, w_gate, w_up, w_down)
    c = _mixer_bwd(b["dx1"], t, a, b, sinks, w_ba, w_bm, w_out)
    grad_x, acc_p = _pre_bwd(c["dproj"], x, b["dx1"], t["vecs"], w_cat, t["after"])
    big = dict(w_cat=jnp.concatenate(c["g_w_cat"], axis=1), w_ba=c["g_w_ba"], w_bm=c["g_w_bm"], w_out=c["g_w_out"], w_gate=b["g_w_gate"],
               w_up=b["g_w_up"], w_down=b["g_w_down"])
    return b["loss"], grad_x, big, _small_grads(acc_p, b, c)


def _tables(mod, gains, conv_w, conv_b, b_if, norm_w, positions):
    cos, sin = _rope_tables(positions)
    return dict(
        vecs=jnp.concatenate([mod, gains, jnp.zeros((6, D_MODEL), F32)], axis=0),
        cw=jnp.concatenate([conv_w, conv_b.reshape(1, -1), jnp.zeros((3, 2 * 512), F32)], axis=0),
        sv=jnp.zeros((8, 512), F32).at[0].set(norm_w).at[1, 0:8].set(b_if), cos=cos, sin=sin,
        after=jnp.zeros((8, LANES), F32))


def _mixer_fwd(x, t, sinks, w_cat):
    h, pa, pm, pif, pg = _pre_proj(x, t["vecs"], w_cat, t["after"])
    n_blk = x.shape[0] // ATTN_BLOCK
    (ya, qr, kr, vb, lse), = _fused_call([_attn_fwd_part(pa, t["cos"], t["sin"], sinks)], "attn_fwd", n_blk)
    (ym, zc, cst, nst), = _fused_call([_mlstm_fwd_part(pm, pif, t["cw"], t["sv"])], "mlstm_fwd", n_blk)
    return dict(h=h, pm=pm, pif=pif, pg=pg, ya=ya, qr=qr, kr=kr, vb=vb, lse=lse, ym=ym, zc=zc, cst=cst, nst=nst)


def _ffn_part(x, tgt, t, a, w_ba, w_bm, w_out, w_gate, w_up, w_down):
    x1, merged, mix, pba, pbm = _mix_fwd(x, a["ya"], a["ym"], a["pg"], t["vecs"], w_ba, w_bm, w_out)
    dx1, h2, hid, da, du, dff, acc_f, loss = _ffn_fwd_bwd(x1, tgt, t["vecs"], w_gate, w_up, w_down)
    return dict(merged=merged, mix=mix, pba=pba, pbm=pbm, dx1=dx1, acc_f=acc_f, loss=loss[0, 0],
                g_w_gate=_matmul_tn(da, h2, 1024, "dw_ffn_gate"),
                g_w_up=_matmul_tn(du, h2, 1024, "dw_ffn_up"),
                g_w_down=_matmul_tn(hid, dff, 1024, "dw_ffn_down"))


def _mixer_bwd(dx1, t, a, b, sinks, w_ba, w_bm, w_out):
    dmix, dpa, dpb, dg, dya, dym, acc_m = _mix_bwd(dx1, b["mix"], b["pba"], b["pbm"], a["pg"], t["vecs"], w_ba, w_bm, w_out,
                                                   t["after"])
    g_w_out = _matmul_tn(b["merged"], dmix, 1024, "dw_out")
    g_w_ba = _matmul_tn(a["ya"], dpa, 1024, "dw_branch_attn")
    g_w_bm = _matmul_tn(a["ym"], dpb, 1024, "dw_branch_mlstm")
    n_blk = dx1.shape[0] // ATTN_BLOCK
    (dqkv, dqkv_last, dsink), = _fused_call(
        [_attn_bwd_part(dya, a["qr"], a["kr"], a["vb"], a["lse"], t["cos"], t["sin"], sinks)], "attn_bwd", n_blk)
    (dm, dcw, dsv), = _fused_call(
        [_mlstm_bwd_part(a["pm"], a["zc"], a["pif"], t["cw"], t["sv"], dym, a["cst"], a["nst"])], "mlstm_bwd", n_blk)
    dqkv = lax.dynamic_update_slice(dqkv, dqkv_last, (dqkv.shape[0] - ATTN_BLOCK, 0))
    dproj = [dqkv, dm, dg]
    g_w_cat = [_matmul_tn(a["h"], p, 1024 if p.shape[1] % 1024 == 0 else p.shape[1], "dw_in_" + n)
               for p, n in zip(dproj, ("attn", "mlstm", "branch"))]
    return dict(dproj=dproj, g_w_cat=g_w_cat, g_w_out=g_w_out, g_w_ba=g_w_ba,
                g_w_bm=g_w_bm, acc_m=acc_m, dsink=dsink, dcw=dcw, dsv=dsv)


def _small_grads(acc_p, b, c):
    acc_f, acc_m = b["acc_f"], c["acc_m"]
    dmod = jnp.stack([acc_p[1], acc_p[0], acc_m[0], acc_f[3], acc_f[2], acc_f[0]])
    dgains = jnp.stack([acc_p[2], acc_m[1], acc_f[4], acc_f[1]])
    return dict(dmod=dmod, dgains=dgains, dconv_w=c["dcw"][0:4], dconv_b=c["dcw"][4], db_if=c["dsv"][1, 0:8],
                dsinks=c["dsink"][0, 0:8], dnorm_w=c["dsv"][0])


MESH_ID = pl.DeviceIdType.MESH


def _mesh_pos():
    return lax.axis_index("x"), lax.axis_index("y"), lax.axis_index("c")


def _flip(v, bit):
    return 1 - v if bit else v


def _relations():
    return [((r >> 2) & 1, (r >> 1) & 1, r & 1) for r in range(1, N_DEV)]


def _small_exchange(p, gather, name):
    R, V = p.shape[-2:]

    def body(p_ref, out_ref, send_sems, recv_sems):
        x, y, c = _mesh_pos()
        me = 4 * x + 2 * y + c
        out_ref[me] = p_ref[...] if gather else p_ref[me]
        peers = []
        for dx, dy, dc in _relations():
            px, py, pc = _flip(x, dx), _flip(y, dy), _flip(c, dc)
            peers.append(((px, py, pc), 4 * px + 2 * py + pc))

        def copy(k, landing):
            peer, pid = peers[k]
            return pltpu.make_async_remote_copy(
                src_ref=p_ref if gather else p_ref.at[pid], dst_ref=out_ref.at[landing],
                send_sem=send_sems.at[k], recv_sem=recv_sems.at[k], device_id=peer, device_id_type=MESH_ID)

        sends = [copy(k, me) for k in range(N_DEV - 1)]
        for cp in sends:
            cp.start()
        for k in range(N_DEV - 1):
            copy(k, peers[k][1]).wait_recv()
        for cp in sends:
            cp.wait_send()

    vm = pl.BlockSpec(memory_space=pltpu.VMEM)
    return pl.pallas_call(
        body, name=name, in_specs=[vm], out_specs=vm,
        out_shape=jax.ShapeDtypeStruct((N_DEV, R, V), F32),
        scratch_shapes=[pltpu.SemaphoreType.DMA((N_DEV - 1,)), pltpu.SemaphoreType.DMA((N_DEV - 1,))],
        compiler_params=pltpu.CompilerParams(vmem_limit_bytes=VMEM_LIMIT),
    )(p)


HBM_SPEC = pl.BlockSpec(memory_space=pltpu.HBM)
SEM_SPEC = pl.BlockSpec(memory_space=pltpu.SEMAPHORE)


def _peers(x, y, c):
    out = []
    for dx, dy, dc in _relations():
        px, py, pc = _flip(x, dx), _flip(y, dy), _flip(c, dc)
        out.append(((px, py, pc), 4 * px + 2 * py + pc))
    return out


def _exchange_start(arrs, gather, after, name):
    n = len(arrs)
    me_out = 4 * lax.axis_index("x") + 2 * lax.axis_index("y") + lax.axis_index("c")
    lands = []
    for a in arrs:
        own = a[None] if gather else lax.dynamic_index_in_dim(a, me_out, 0, keepdims=True)
        empty = lax.empty(((N_DEV,) + a.shape) if gather else a.shape, a.dtype)
        lands.append(lax.dynamic_update_index_in_dim(empty, own, me_out, 0))

    def body(*refs):
        a_refs, l_refs = refs[:n], refs[n:2 * n]
        send_sems, recv_sems = refs[2 * n + 1], refs[2 * n + 2]
        token = refs[4 * n + 3]
        x, y, c = _mesh_pos()
        me = 4 * x + 2 * y + c
        for a in range(n):
            for k, (peer, pid) in enumerate(_peers(x, y, c)):
                pltpu.make_async_remote_copy(
                    src_ref=a_refs[a] if gather else a_refs[a].at[pid], dst_ref=l_refs[a].at[me],
                    send_sem=send_sems.at[a * (N_DEV - 1) + k], recv_sem=recv_sems.at[a * (N_DEV - 1) + k],
                    device_id=peer, device_id_type=MESH_ID).start()
        token[...] = jnp.zeros_like(token)

    sem = pltpu.SemaphoreType.DMA((n * (N_DEV - 1),))
    hbm = lambda a: pltpu.with_memory_space_constraint(a, pltpu.HBM)
    res = pl.pallas_call(
        body, name=name,
        out_shape=(sem, sem, *[pltpu.HBM(a.shape, a.dtype) for a in arrs], *[pltpu.HBM(l.shape, l.dtype) for l in lands],
                   jax.ShapeDtypeStruct((8, LANES), F32)),
        in_specs=[HBM_SPEC] * (2 * n) + [pl.BlockSpec(memory_space=pl.ANY)],
        out_specs=(SEM_SPEC, SEM_SPEC, *[HBM_SPEC] * (2 * n), pl.BlockSpec(memory_space=pltpu.VMEM)),
        input_output_aliases={i: 2 + i for i in range(2 * n)},
        compiler_params=pltpu.CompilerParams(has_side_effects=pltpu.SideEffectType.DATAFLOW_SIDE_EFFECTING),
    )(*[hbm(a) for a in arrs], *[hbm(l) for l in lands], after)
    return dict(sems=res[0:2], arrs=res[2:2 + n], lands=res[2 + n:2 + 2 * n], token=res[2 + 2 * n], gather=gather)


def _exchange_wait(st, after, name):
    n = len(st["arrs"])
    gather = st["gather"]

    def body(*refs):
        a_refs, l_refs = refs[:n], refs[n:2 * n]
        send_sems, recv_sems = refs[2 * n], refs[2 * n + 1]
        x, y, c = _mesh_pos()
        for a in range(n):
            for k, (peer, pid) in enumerate(_peers(x, y, c)):
                cp = pltpu.make_async_remote_copy(
                    src_ref=a_refs[a] if gather else a_refs[a].at[pid], dst_ref=l_refs[a].at[pid],
                    send_sem=send_sems.at[a * (N_DEV - 1) + k], recv_sem=recv_sems.at[a * (N_DEV - 1) + k],
                    device_id=peer, device_id_type=MESH_ID)
                cp.wait_send()
                cp.wait_recv()

    both = list(st["arrs"]) + list(st["lands"])
    res = pl.pallas_call(
        body, name=name, out_shape=[pltpu.HBM(a.shape, a.dtype) for a in both],
        in_specs=[HBM_SPEC] * (2 * n) + [SEM_SPEC, SEM_SPEC, pl.BlockSpec(memory_space=pl.ANY)],
        out_specs=[HBM_SPEC] * (2 * n), input_output_aliases={i: i for i in range(2 * n)},
        compiler_params=pltpu.CompilerParams(has_side_effects=pltpu.SideEffectType.DATAFLOW_SIDE_EFFECTING),
    )(*both, *st["sems"], after)
    return res[n:2 * n]


def _gather_front(shards, c8, w_ada, b_cols):
    n = len(shards)
    ada_w = w_ada.shape[1]
    chunk = c8.shape[1]

    def body(*refs):
        p_refs = refs[:n]
        c_ref, wa_ref, b_ref = refs[n:n + 3]
        out_refs = refs[n + 3:2 * n + 3]
        cg_ref, mod_ref = refs[2 * n + 3], refs[2 * n + 4]
        send_sems, recv_sems, local_sems, c_send, c_recv, m_send, m_recv, pbuf = refs[2 * n + 5:]
        x, y, c = _mesh_pos()
        me_id = 4 * x + 2 * y + c
        me, sibling = (x, y, c), (x, y, 1 - c)
        chips = [(1 - x, y), (x, 1 - y), (1 - x, 1 - y)]
        peers = _peers(x, y, c)

        def copy(a, k, block, to, own=False):
            slot = out_refs[a].at[4 * block[0] + 2 * block[1] + block[2]]
            return pltpu.make_async_remote_copy(
                src_ref=p_refs[a] if own else slot, dst_ref=slot,
                send_sem=send_sems.at[a, k], recv_sem=recv_sems.at[a, k], device_id=to, device_id_type=MESH_ID)

        def c_copy(k, landing):
            return pltpu.make_async_remote_copy(
                src_ref=c_ref, dst_ref=cg_ref.at[landing], send_sem=c_send.at[k], recv_sem=c_recv.at[k],
                device_id=peers[k][0], device_id_type=MESH_ID)

        cg_ref[me_id] = c_ref[...]
        c_sends = [c_copy(k, me_id) for k in range(N_DEV - 1)]
        for cp in c_sends:
            cp.start()
        mine = [pltpu.make_async_copy(p_refs[a], out_refs[a].at[me_id], local_sems.at[a]) for a in range(n)]
        for cp in mine:
            cp.start()
        first = []
        for a in range(n):
            first.append(copy(a, 0, me, sibling, own=True))
            first += [copy(a, 1 + j, me, (*chip, c), own=True) for j, chip in enumerate(chips)]
        for cp in first:
            cp.start()
        for k in range(N_DEV - 1):
            c_copy(k, peers[k][1]).wait_recv()
        prod = b_ref[...] + jnp.zeros((N_DEV, ada_w), F32)
        for j in range(D_MODEL // chunk):
            prod = prod + _dot(cg_ref[:, j, :].astype(BF16), wa_ref[j * chunk:(j + 1) * chunk, :].astype(BF16))
        for k in range(N_DEV):
            pbuf[k] = jnp.broadcast_to(prod[k:k + 1, :], (8, ada_w))

        def m_copy(k, landing):
            return pltpu.make_async_remote_copy(
                src_ref=pbuf.at[peers[k][1]], dst_ref=mod_ref.at[landing], send_sem=m_send.at[k], recv_sem=m_recv.at[k],
                device_id=peers[k][0], device_id_type=MESH_ID)

        mod_ref[me_id] = pbuf[me_id]
        m_sends = [m_copy(k, me_id) for k in range(N_DEV - 1)]
        for cp in m_sends:
            cp.start()

        passed = []
        for j, chip in enumerate(chips):
            for a in range(n):
                copy(a, 1 + j, (*chip, c), me).wait_recv()
                passed.append(copy(a, 4 + j, (*chip, c), sibling))
                passed[-1].start()
        for a in range(n):
            copy(a, 0, sibling, me).wait_recv()
            for j, chip in enumerate(chips):
                copy(a, 4 + j, (*chip, 1 - c), me).wait_recv()
        for k in range(N_DEV - 1):
            m_copy(k, peers[k][1]).wait_recv()
        for cp in first + passed + c_sends + m_sends:
            cp.wait_send()
        for cp in mine:
            cp.wait()

    hbm = pl.BlockSpec(memory_space=pl.ANY)
    vm = pl.BlockSpec(memory_space=pltpu.VMEM)
    sem7 = pltpu.SemaphoreType.DMA((N_DEV - 1,))
    res = pl.pallas_call(
        body, name="gather_front", in_specs=[hbm] * n + [vm, vm, vm], out_specs=[hbm] * n + [vm, vm],
        out_shape=[jax.ShapeDtypeStruct((N_DEV,) + s.shape, s.dtype) for s in shards]
        + [jax.ShapeDtypeStruct((N_DEV,) + c8.shape, F32), jax.ShapeDtypeStruct((N_DEV, 8, ada_w), F32)],
        scratch_shapes=[pltpu.SemaphoreType.DMA((n, N_DEV - 1)), pltpu.SemaphoreType.DMA((n, N_DEV - 1)),
                        pltpu.SemaphoreType.DMA((n,)), sem7, sem7, sem7, sem7, pltpu.VMEM((N_DEV, 8, ada_w), F32)],
        compiler_params=pltpu.CompilerParams(vmem_limit_bytes=VMEM_LIMIT),
    )(*shards, c8, w_ada, b_cols)
    return res[:n], res[n], res[n + 1]


def _adamw(w, g, m, v):
    m2 = ADAM_B1 * m + (1.0 - ADAM_B1) * g
    v2 = ADAM_B2 * v + (1.0 - ADAM_B2) * (g * g)
    m_hat = m2 / (1.0 - ADAM_B1 ** ADAM_STEP)
    v_hat = v2 / (1.0 - ADAM_B2 ** ADAM_STEP)
    delta = -ADAM_LR * (m_hat / (jnp.sqrt(v_hat) + ADAM_EPS) + ADAM_WD * w)
    return delta, m2, v2


def _adamw_w_ada(cmat, dmod_cols, w, m, v):
    r, cdim = w.shape
    tr = _row_tile(r)

    def body(c_ref, d_ref, w_ref, m_ref, v_ref, g_ref, dl_ref, m2_ref, v2_ref):
        g = _dot_tn(c_ref[...].astype(BF16), d_ref[...].astype(BF16))
        g_ref[...] = g
        dl_ref[...], m2_ref[...], v2_ref[...] = _adamw(w_ref[...], g, m_ref[...], v_ref[...])

    row = pl.BlockSpec((tr, cdim), lambda i: (i, 0))
    return pl.pallas_call(
        body, name="adamw_w_ada", grid=(r // tr,),
        in_specs=[pl.BlockSpec((N_DEV, tr), lambda i: (0, i)), _full(dmod_cols.shape), row, row, row],
        out_specs=[row] * 4, out_shape=[jax.ShapeDtypeStruct(w.shape, F32)] * 4,
        compiler_params=_params(("parallel",)),
    )(cmat, dmod_cols, w, m, v)


SMALL_ROWS = 16
SMALL_AT = {"b_ada": (0, 6, 0, D_MODEL), "g_pre_mix": (6, 1, 0, D_MODEL), "g_post_mix": (7, 1, 0, D_MODEL),
            "g_pre_ffn": (8, 1, 0, D_MODEL), "g_post_ffn": (9, 1, 0, D_MODEL), "conv_b": (10, 1, 0, D_MODEL),
            "mlstm_norm_w": (11, 1, 0, 512), "b_if": (11, 1, 512, LANES), "attn_sinks": (11, 1, 640, LANES)}


def _small_table(part):
    tail = jnp.concatenate([part["mlstm_norm_w"], jnp.pad(part["b_if"], (0, LANES - 8)),
                            jnp.pad(part["attn_sinks"], (0, LANES - 8)), jnp.zeros((256,), F32)])
    return jnp.concatenate([part["b_ada"], part["gains"], part["conv_b"][None], tail[None],
                            jnp.full((1, D_MODEL), part["loss"], F32),
                            jnp.zeros((SMALL_ROWS - 13, D_MODEL), F32)], axis=0)


LOSS_ROW = 12


def _adamw_small(gathered, wmv):
    names = list(SMALL_AT)

    def body(*refs):
        g_ref, ins, outs = refs[0], refs[1:1 + 3 * len(names)], refs[1 + 3 * len(names):]
        g = g_ref[0]
        for k in range(1, N_DEV):
            g = g + g_ref[k]
        outs[4 * len(names)][...] = g[LOSS_ROW:LOSS_ROW + 1, 0:LANES]
        for i, n in enumerate(names):
            r0, rows, l0, lanes = SMALL_AT[n]
            gi = jnp.concatenate([g[r:r + 1, l0:l0 + lanes] for r in range(r0, r0 + rows)], axis=1)
            w_ref, m_ref, v_ref = ins[3 * i:3 * i + 3]
            go, dl, m2, v2 = outs[4 * i:4 * i + 4]
            go[...] = gi
            dl[...], m2[...], v2[...] = _adamw(w_ref[...], gi, m_ref[...], v_ref[...])

    flat = [a for n in names for a in wmv[n]]
    res = pl.pallas_call(
        body, name="adamw_small",
        out_shape=[jax.ShapeDtypeStruct(wmv[n][0].shape, F32) for n in names for _ in range(4)]
        + [jax.ShapeDtypeStruct((1, LANES), F32)],
        compiler_params=_params(),
    )(gathered, *flat)
    out = {n: res[4 * i:4 * i + 4] for i, n in enumerate(names)}
    out["loss"] = res[4 * len(names)]
    return out


def _adamw_sum_many(items, name):
    n = len(items)

    def body(*refs):
        ins, outs = refs[:4 * n], refs[4 * n:]
        for i in range(n):
            r_ref, w_ref, m_ref, v_ref = ins[4 * i:4 * i + 4]
            go, dl, m2, v2 = outs[4 * i:4 * i + 4]
            g = _sum_partials(r_ref)
            go[...] = g
            dl[...], m2[...], v2[...] = _adamw(w_ref[...], g, m_ref[...], v_ref[...])

    res = pl.pallas_call(
        body, name=name, out_shape=[jax.ShapeDtypeStruct(it[1].shape, F32) for it in items for _ in range(4)],
        compiler_params=_params(),
    )(*[a for it in items for a in it])
    return [res[4 * i:4 * i + 4] for i in range(n)]


def _row_tile(rows):
    return rows // 4 if rows >= 512 and rows % 64 == 0 else rows


def _sum_partials(r_ref):
    g = r_ref[0].astype(F32)
    for k in range(1, N_DEV):
        g = g + r_ref[k].astype(F32)
    return g


def _adamw_sum(recv, w, m, v, name):
    r, cdim = w.shape
    tr = _row_tile(r)

    def body(r_ref, w_ref, m_ref, v_ref, g_ref, dl_ref, m2_ref, v2_ref):
        g = _sum_partials(r_ref)
        g_ref[...] = g
        dl_ref[...], m2_ref[...], v2_ref[...] = _adamw(w_ref[...], g, m_ref[...], v_ref[...])

    row = pl.BlockSpec((tr, cdim), lambda i: (i, 0))
    return pl.pallas_call(
        body, name=name, grid=(r // tr,),
        in_specs=[pl.BlockSpec((N_DEV, tr, cdim), lambda i: (0, i, 0)), row, row, row],
        out_specs=[row] * 4, out_shape=[jax.ShapeDtypeStruct((r, cdim), F32)] * 4,
        compiler_params=_params(("parallel",)),
    )(recv, w, m, v)


def _sum8(recv, name):
    _, r, cdim = recv.shape
    tr = _row_tile(r)

    def body(r_ref, g_ref):
        g_ref[...] = _sum_partials(r_ref)

    return pl.pallas_call(
        body, name=name, grid=(r // tr,),
        in_specs=[pl.BlockSpec((N_DEV, tr, cdim), lambda i: (0, i, 0))],
        out_specs=pl.BlockSpec((tr, cdim), lambda i: (i, 0)), out_shape=jax.ShapeDtypeStruct((r, cdim), F32),
        compiler_params=_params(("parallel",)),
    )(recv)


def _adamw_plain(g, w, m, v, name):
    r, cdim = w.shape
    tr = _row_tile(r)

    def body(g_ref, w_ref, m_ref, v_ref, dl_ref, m2_ref, v2_ref):
        dl_ref[...], m2_ref[...], v2_ref[...] = _adamw(w_ref[...], g_ref[...], m_ref[...], v_ref[...])

    row = pl.BlockSpec((tr, cdim), lambda i: (i, 0))
    return pl.pallas_call(
        body, name=name, grid=(r // tr,), in_specs=[row] * 4, out_specs=[row] * 3,
        out_shape=[jax.ShapeDtypeStruct((r, cdim), F32)] * 3,
        compiler_params=_params(("parallel",)),
    )(g, w, m, v)


IN_SHARD = 609
IN_SHARD_PAD = 640
IF_AT = A_W + M_W


def _regrouped(u):
    return u if u < IF_AT + 8 else u + (IF_W - 8)


def _selection(k, rows, row0, transpose):
    shape = (rows, IN_SHARD_PAD) if transpose else (IN_SHARD_PAD, rows)
    l = lax.broadcasted_iota(jnp.int32, shape, 1 if transpose else 0)
    r = lax.broadcasted_iota(jnp.int32, shape, 0 if transpose else 1) + row0
    u = l + IN_SHARD * k
    ru = u + jnp.where(u >= IF_AT + 8, IF_W - 8, 0)
    return ((ru == r) & (l < IN_SHARD)).astype(BF16)


def _regroup_w_in(g):
    def body(g_ref, o_ref):
        for cb in range(CAT_W // LANES):
            r0 = cb * LANES
            acc = jnp.zeros((D_MODEL, LANES), F32)
            for k in range(N_DEV):
                src = [l for l in range(IN_SHARD) if r0 <= _regrouped(IN_SHARD * k + l) < r0 + LANES]
                if src:
                    a0 = src[0] // LANES * LANES
                    a1 = min(a0 + 2 * LANES, IN_SHARD_PAD)
                    acc = acc + _dot(g_ref[k, :, a0:a1], _selection(k, LANES, r0, False)[a0:a1])
            o_ref[:, r0:r0 + LANES] = acc.astype(BF16)

    return pl.pallas_call(
        body, name="regroup_w_in", out_shape=jax.ShapeDtypeStruct((D_MODEL, CAT_W), BF16),
        compiler_params=_params(),
    )(g)


def _ungroup_w_in(g_parts):
    n = len(g_parts)

    def body(*refs):
        o_ref, g_ref = refs[n], refs[n + 1]
        at = 0
        for p in refs[:n]:
            g_ref[:, at:at + p.shape[1]] = p[...]
            at += p.shape[1]
        for k in range(N_DEV):
            for cb in range(IN_SHARD_PAD // LANES):
                cols = slice(cb * LANES, (cb + 1) * LANES)
                lo = _regrouped(IN_SHARD * k + cb * LANES)
                hi = _regrouped(IN_SHARD * k + min((cb + 1) * LANES, IN_SHARD) - 1)
                w0, w1 = lo // LANES * LANES, (hi // LANES + 1) * LANES
                o_ref[k, :, cols] = _dot(g_ref[:, w0:w1], _selection(k, w1 - w0, w0, True)[:, cols]).astype(BF16)

    return pl.pallas_call(
        body, name="ungroup_w_in", out_shape=jax.ShapeDtypeStruct((N_DEV, D_MODEL, IN_SHARD_PAD), BF16),
        scratch_shapes=[pltpu.VMEM((D_MODEL, CAT_W), BF16)], compiler_params=_params(),
    )(*g_parts)


WEIGHT_NAMES = ("w_ada", "b_ada", "g_pre_mix", "g_post_mix", "w_in", "b_if", "conv_w", "conv_b", "attn_sinks",
                "mlstm_norm_w", "w_branch_attn", "w_branch_mlstm", "w_out", "g_pre_ffn", "g_post_ffn",
                "w_ffn_gate", "w_ffn_up", "w_ffn_down")


def kernel(x, c, positions, w_ada, b_ada, g_pre_mix, g_post_mix, w_in, b_if, conv_w, conv_b, attn_sinks, mlstm_norm_w, w_branch_attn, w_branch_mlstm, w_out, g_pre_ffn, g_post_ffn, w_ffn_gate, w_ffn_up, w_ffn_down, loss_target, m_w_ada, m_b_ada, m_g_pre_mix, m_g_post_mix, m_w_in, m_b_if, m_conv_w, m_conv_b, m_attn_sinks, m_mlstm_norm_w, m_w_branch_attn, m_w_branch_mlstm, m_w_out, m_g_pre_ffn, m_g_post_ffn, m_w_ffn_gate, m_w_ffn_up, m_w_ffn_down, v_w_ada, v_b_ada, v_g_pre_mix, v_g_post_mix, v_w_in, v_b_if, v_conv_w, v_conv_b, v_attn_sinks, v_mlstm_norm_w, v_w_branch_attn, v_w_branch_mlstm, v_w_out, v_g_pre_ffn, v_g_post_ffn, v_w_ffn_gate, v_w_ffn_up, v_w_ffn_down):
    given = dict(locals())
    W = {n: given[n][0] for n in WEIGHT_NAMES}
    M = {n: given["m_" + n][0] for n in WEIGHT_NAMES}
    V = {n: given["v_" + n][0] for n in WEIGHT_NAMES}
    me = 4 * lax.axis_index("x") + 2 * lax.axis_index("y") + lax.axis_index("c")

    ff_sh = D_FF // N_DEV
    ada_w = D_MODEL * 6 // N_DEV
    b_cols = lax.dynamic_slice(W["b_ada"], (me * ada_w,), (ada_w,)).reshape(1, ada_w)
    (g_in, g_conv), cg, mod_recv = _gather_front(
        [jnp.pad(W["w_in"], ((0, 0), (0, IN_SHARD_PAD - IN_SHARD))).astype(BF16), jnp.pad(W["conv_w"], ((0, 4), (0, 0)))],
        c.reshape(8, D_MODEL // 8), W["w_ada"], b_cols)
    cmat = cg.reshape(N_DEV, D_MODEL)
    mod = mod_recv[:, 0, :].reshape(6, D_MODEL)

    st_b = _exchange_start([W["w_branch_attn"].astype(BF16), W["w_branch_mlstm"].astype(BF16), W["w_out"].astype(BF16),
                            W["w_ffn_gate"].T.astype(BF16), W["w_ffn_up"].T.astype(BF16), W["w_ffn_down"].astype(BF16)],
                           True, mod_recv, "gather_rest_start")
    cols = lambda g: g.transpose(1, 0, 2).reshape(g.shape[1], N_DEV * g.shape[2])
    gains = jnp.stack([W["g_pre_mix"], W["g_post_mix"], W["g_pre_ffn"], W["g_post_ffn"]])
    xs, tgt = x[0], loss_target[0]
    t = _tables(mod, gains, cols(g_conv)[0:4], W["conv_b"], W["b_if"], W["mlstm_norm_w"], positions[0])
    t["after"] = st_b["token"]
    w_cat = _regroup_w_in(g_in)
    a = _mixer_fwd(xs, t, W["attn_sinks"], w_cat)
    g_ba, g_bm, g_out, g_gate, g_up, g_down = _exchange_wait(st_b, a["ym"], "gather_rest_wait")
    w_ba, w_bm, w_out = cols(g_ba), cols(g_bm), g_out.reshape(D_MODEL, D_MODEL)
    b = _ffn_part(xs, tgt, t, a, w_ba, w_bm, w_out, g_gate.reshape(D_FF, D_MODEL), g_up.reshape(D_FF, D_MODEL),
                  g_down.reshape(D_FF, D_MODEL))

    st_f = _exchange_start([b["g_w_gate"].reshape(N_DEV, ff_sh, D_MODEL), b["g_w_up"].reshape(N_DEV, ff_sh, D_MODEL),
                            b["g_w_down"].reshape(N_DEV, ff_sh, D_MODEL)], False, b["dx1"], "scatter_ffn_start")
    t["after"] = st_f["token"]
    cm = _mixer_bwd(b["dx1"], t, a, b, W["attn_sinks"], w_ba, w_bm, w_out)
    pieces = lambda g, n: g.reshape(g.shape[0], N_DEV, n).transpose(1, 0, 2)
    st_m = _exchange_start([_ungroup_w_in(cm["g_w_cat"]), pieces(cm["g_w_ba"], 128), pieces(cm["g_w_bm"], 128),
                            cm["g_w_out"].reshape(N_DEV, D_MODEL // N_DEV, D_MODEL),
                            jnp.pad(pieces(cm["dcw"][0:4], 128), ((0, 0), (0, 4), (0, 0)))], False, cm["dcw"],
                           "scatter_mixer_start")
    r_gate, r_up, r_down = _exchange_wait(st_f, st_m["token"], "scatter_ffn_wait")
    grad_x, acc_p = _pre_bwd(cm["dproj"], xs, b["dx1"], t["vecs"], w_cat, st_m["token"])
    small = _small_grads(acc_p, b, cm)
    loss = b["loss"]

    big_out = [{} for _ in range(4)]

    def put(n, res):
        for k in range(4):
            big_out[k][n] = res[k][None]

    put("w_ffn_down", _adamw_sum(r_down, W["w_ffn_down"], M["w_ffn_down"], V["w_ffn_down"], "adamw_w_ffn_down"))
    for n, r in (("w_ffn_gate", r_gate), ("w_ffn_up", r_up)):
        put(n, [o.T for o in _adamw_sum(r, W[n].T, M[n].T, V[n].T, "adamw_" + n)])

    sg = _small_exchange(_small_table({"b_ada": small["dmod"], "gains": small["dgains"], "conv_b": small["dconv_b"],
                                       "mlstm_norm_w": small["dnorm_w"], "b_if": small["db_if"],
                                       "attn_sinks": small["dsinks"], "loss": loss}), True, "gather_small")
    as_row = lambda a, n: jnp.pad(a, (0, SMALL_AT[n][3] * SMALL_AT[n][1] - a.shape[0]))[None]
    small_res = _adamw_small(sg, {n: [as_row(d[n], n) for d in (W, M, V)] for n in SMALL_AT})
    small_out = [{n: small_res[n][k][:, 0:W[n].shape[0]] for n in SMALL_AT} for k in range(4)]
    dmod_cols = lax.dynamic_slice(sg[:, 0:6, :].reshape(N_DEV, 6 * D_MODEL), (0, me * ada_w), (N_DEV, ada_w))
    ada_out = _adamw_w_ada(cmat, dmod_cols, W["w_ada"], M["w_ada"], V["w_ada"])

    r_in, r_ba, r_bm, r_out, r_conv = _exchange_wait(st_m, ada_out[1], "scatter_mixer_wait")
    pad4 = lambda v: jnp.pad(v, ((0, 4), (0, 0)))
    res_ba, res_bm, res_out, res_conv = _adamw_sum_many(
        [(r, W[n], M[n], V[n]) for n, r in (("w_branch_attn", r_ba), ("w_branch_mlstm", r_bm), ("w_out", r_out))]
        + [(r_conv, pad4(W["conv_w"]), pad4(M["conv_w"]), pad4(V["conv_w"]))], "adamw_mixer_small")
    put("w_branch_attn", res_ba)
    put("w_branch_mlstm", res_bm)
    put("w_out", res_out)
    put("conv_w", [o[0:4] for o in res_conv])
    g = _sum8(r_in, "sum_w_in")[:, 0:IN_SHARD].T
    put("w_in", [o.T for o in [g] + list(_adamw_plain(g, W["w_in"].T, M["w_in"].T, V["w_in"].T, "adamw_w_in"))])

    outs = [small_res["loss"][0, 0], grad_x[None]]
    for k in range(4):
        for n in WEIGHT_NAMES:
            if n == "w_ada":
                outs.append(ada_out[k][None])
            elif n in big_out[k]:
                outs.append(big_out[k][n])
            else:
                outs.append(small_out[k][n])
    return tuple(outs)
```

```python
import jax
import jax.numpy as jnp
from jax import lax
from jax.experimental import pallas as pl
from jax.experimental.pallas import tpu as pltpu

F32 = jnp.float32
BF16 = jnp.bfloat16

N_DEV = 8
D_MODEL = 1024
D_FF = 2816
N_Q_HEADS = 8
HEAD_DIM = 64
ATTN_BLOCK = 128
ROPE_THETA = 10000.0
MLSTM_HEADS = 4
MLSTM_HEAD_DIM = 128
MLSTM_CHUNK = 128
NORM_EPS = 1e-6
ADAM_LR = 0.001
ADAM_B1 = 0.9
ADAM_B2 = 0.999
ADAM_EPS = 1e-08
ADAM_WD = 0.01
ADAM_STEP = 10

ROW_TILE = 256
WIDE_TILE = 512
LANES = 128
NEG = -1e30
VMEM_LIMIT = 56 * 1024 * 1024

A_W = 768
M_W = 2048
IF_W = 128
G_W = 2048
CAT_W = A_W + M_W + IF_W + G_W

R_SHIFT_M, R_SCALE_M, R_GATE_M, R_SHIFT_F, R_SCALE_F, R_GATE_F = 0, 1, 2, 3, 4, 5
R_G_PRE_MIX, R_G_POST_MIX, R_G_PRE_FFN, R_G_POST_FFN = 6, 7, 8, 9


def _dot(a, b):
    return jnp.dot(a, b, preferred_element_type=F32)


def _dot_nt(a, b):
    return lax.dot_general(a, b, (((1,), (1,)), ((), ())), preferred_element_type=F32)


def _dot_tn(a, b):
    return lax.dot_general(a, b, (((0,), (0,)), ((), ())), preferred_element_type=F32)


def _recip(x):
    return 1.0 / x


def _sigmoid(x):
    return _recip(1.0 + jnp.exp(-x))


def _colsum(x):
    return jnp.sum(x, axis=0, keepdims=True)


def _rowmean(x):
    return jnp.mean(x, axis=-1, keepdims=True)


def _params(sem=None, vmem=VMEM_LIMIT):
    kw = dict(vmem_limit_bytes=vmem)
    if sem is not None:
        kw["dimension_semantics"] = sem
    return pltpu.CompilerParams(**kw)


def _full(shape):
    nd = len(shape)
    return pl.BlockSpec(shape, lambda *_: (0,) * nd)


def _pre_proj(x, vecs, w_cat, after):
    S = x.shape[0]
    tm = WIDE_TILE

    def body(x_ref, v_ref, w_ref, after_ref, h_ref, pa_ref, pm_ref, pif_ref, pg_ref):
        xv = x_ref[...]
        r = lax.rsqrt(_rowmean(xv * xv) + NORM_EPS)
        h = (xv * r * v_ref[R_G_PRE_MIX:R_G_PRE_MIX + 1, :]) * (1.0 + v_ref[R_SCALE_M:R_SCALE_M + 1, :]) \
            + v_ref[R_SHIFT_M:R_SHIFT_M + 1, :]
        hb = h.astype(BF16)
        h_ref[...] = hb
        pa_ref[...] = _dot(hb, w_ref[:, 0:A_W])
        pm_ref[...] = _dot(hb, w_ref[:, A_W:A_W + M_W])
        pif_ref[...] = _dot(hb, w_ref[:, A_W + M_W:A_W + M_W + IF_W])
        pg_ref[...] = _dot(hb, w_ref[:, A_W + M_W + IF_W:CAT_W]).astype(BF16)

    row = lambda w: pl.BlockSpec((tm, w), lambda i: (i, 0))
    return pl.pallas_call(
        body, name="pre_proj", grid=(S // tm,),
        in_specs=[row(D_MODEL), _full(vecs.shape), _full(w_cat.shape), pl.BlockSpec(memory_space=pl.ANY)],
        out_specs=[row(D_MODEL), row(A_W), row(M_W), row(IF_W), row(G_W)],
        out_shape=[jax.ShapeDtypeStruct((S, D_MODEL), BF16), jax.ShapeDtypeStruct((S, A_W), F32),
                   jax.ShapeDtypeStruct((S, M_W), F32), jax.ShapeDtypeStruct((S, IF_W), F32),
                   jax.ShapeDtypeStruct((S, G_W), BF16)],
        compiler_params=_params(("parallel",)),
    )(x, vecs, w_cat, after)


def _mix_fwd(x, ya, ym, pg, vecs, w_ba, w_bm, w_out):
    S = x.shape[0]
    tm = WIDE_TILE

    def body(x_ref, ya_ref, ym_ref, pg_ref, v_ref, wba_ref, wbm_ref, wout_ref,
             x1_ref, merged_ref, mix_ref, pa_ref, pb_ref):
        pa = _dot(ya_ref[...], wba_ref[...])
        pb = _dot(ym_ref[...], wbm_ref[...])
        merged = _sigmoid(pg_ref[:, 0:D_MODEL].astype(F32)) * pa + _sigmoid(pg_ref[:, D_MODEL:G_W].astype(F32)) * pb
        mb = merged.astype(BF16)
        mix = _dot(mb, wout_ref[...])
        r = lax.rsqrt(_rowmean(mix * mix) + NORM_EPS)
        x1_ref[...] = x_ref[...] + v_ref[R_GATE_M:R_GATE_M + 1, :] * (mix * r * v_ref[R_G_POST_MIX:R_G_POST_MIX + 1, :])
        merged_ref[...] = mb
        mix_ref[...] = mix.astype(BF16)
        pa_ref[...] = pa.astype(BF16)
        pb_ref[...] = pb.astype(BF16)

    row = lambda w: pl.BlockSpec((tm, w), lambda i: (i, 0))
    sd = lambda w, dt: jax.ShapeDtypeStruct((S, w), dt)
    return pl.pallas_call(
        body, name="mix_fwd", grid=(S // tm,),
        in_specs=[row(D_MODEL), row(512), row(512), row(G_W), _full(vecs.shape), _full(w_ba.shape),
                  _full(w_bm.shape), _full(w_out.shape)],
        out_specs=[row(D_MODEL)] * 5,
        out_shape=[sd(D_MODEL, F32), sd(D_MODEL, BF16), sd(D_MODEL, BF16), sd(D_MODEL, BF16), sd(D_MODEL, BF16)],
        compiler_params=_params(("parallel",)),
    )(x, ya, ym, pg, vecs, w_ba, w_bm, w_out)


def _ffn_fwd_bwd(x1, tgt, vecs, w_gate, w_up, w_down):
    S = x1.shape[0]
    tm = ROW_TILE

    def body(x1_ref, tgt_ref, v_ref, wg_hbm, wu_hbm, wd_hbm,
             dx1_ref, h2_ref, hid_ref, da_ref, du_ref, dff_ref, acc_ref, loss_ref,
             wg, wu, wd, sem):
        i = pl.program_id(0)

        @pl.when(i == 0)
        def _():
            cps = [pltpu.make_async_copy(wg_hbm, wg, sem.at[0]), pltpu.make_async_copy(wu_hbm, wu, sem.at[1]),
                   pltpu.make_async_copy(wd_hbm, wd, sem.at[2])]
            for cp in cps:
                cp.start()
            for cp in cps:
                cp.wait()
            acc_ref[...] = jnp.zeros_like(acc_ref)
            loss_ref[...] = jnp.zeros_like(loss_ref)

        vrow = lambda r: v_ref[r:r + 1, :]
        x1v = x1_ref[...]
        r3 = lax.rsqrt(_rowmean(x1v * x1v) + NORM_EPS)
        x1hat = x1v * r3
        xn3 = x1hat * vrow(R_G_PRE_FFN)
        h2b = (xn3 * (1.0 + vrow(R_SCALE_F)) + vrow(R_SHIFT_F)).astype(BF16)
        h2_ref[...] = h2b
        a = _dot_nt(h2b, wg[...])
        u = _dot_nt(h2b, wu[...])
        sg = _sigmoid(a)
        sil = a * sg
        hidb = (sil * u).astype(BF16)
        hid_ref[...] = hidb
        ff = _dot(hidb, wd[...])
        r4 = lax.rsqrt(_rowmean(ff * ff) + NORM_EPS)
        ffhat = ff * r4
        n4 = ffhat * vrow(R_G_POST_FFN)
        err = x1v + vrow(R_GATE_F) * n4 - tgt_ref[...]
        loss_ref[...] += jnp.sum(err * err) * (0.5 / D_MODEL)
        dy = err * (1.0 / D_MODEL)
        acc_ref[0:1, :] += _colsum(dy * n4)
        dn4 = dy * vrow(R_GATE_F)
        acc_ref[1:2, :] += _colsum(dn4 * ffhat)
        dffhat = dn4 * vrow(R_G_POST_FFN)
        dffb = (r4 * (dffhat - ffhat * _rowmean(dffhat * ffhat))).astype(BF16)
        dff_ref[...] = dffb
        dhid = _dot_nt(dffb, wd[...])
        dub = (dhid * sil).astype(BF16)
        dab = (dhid * u * (sg * (1.0 + a * (1.0 - sg)))).astype(BF16)
        da_ref[...] = dab
        du_ref[...] = dub
        dh2 = _dot(dab, wg[...]) + _dot(dub, wu[...])
        acc_ref[2:3, :] += _colsum(dh2 * xn3)
        acc_ref[3:4, :] += _colsum(dh2)
        dxn3 = dh2 * (1.0 + vrow(R_SCALE_F))
        acc_ref[4:5, :] += _colsum(dxn3 * x1hat)
        dx1hat = dxn3 * vrow(R_G_PRE_FFN)
        dx1_ref[...] = dy + r3 * (dx1hat - x1hat * _rowmean(dx1hat * x1hat))

    row = lambda w: pl.BlockSpec((tm, w), lambda i: (i, 0))
    sd = lambda w, dt: jax.ShapeDtypeStruct((S, w), dt)
    anyspec = pl.BlockSpec(memory_space=pl.ANY)
    return pl.pallas_call(
        body, name="ffn_fwd_bwd", grid=(S // tm,),
        in_specs=[row(D_MODEL), row(D_MODEL), _full(vecs.shape), anyspec, anyspec, anyspec],
        out_specs=[row(D_MODEL), row(D_MODEL), row(D_FF), row(D_FF), row(D_FF), row(D_MODEL),
                   _full((8, D_MODEL)), _full((8, LANES))],
        out_shape=[sd(D_MODEL, F32), sd(D_MODEL, BF16), sd(D_FF, BF16), sd(D_FF, BF16), sd(D_FF, BF16),
                   sd(D_MODEL, BF16), jax.ShapeDtypeStruct((8, D_MODEL), F32), jax.ShapeDtypeStruct((8, LANES), F32)],
        scratch_shapes=[pltpu.VMEM(w_gate.shape, BF16), pltpu.VMEM(w_up.shape, BF16), pltpu.VMEM(w_down.shape, BF16),
                        pltpu.SemaphoreType.DMA((3,))],
        compiler_params=_params(("arbitrary",)),
    )(x1, tgt, vecs, w_gate, w_up, w_down)


def _mix_bwd(dx1, mix, pa, pb, pg, vecs, w_ba, w_bm, w_out, after):
    S = dx1.shape[0]
    tm = WIDE_TILE

    def body(dx1_ref, mix_ref, pa_ref, pb_ref, pg_ref, v_ref, wba_ref, wbm_ref, wout_ref, after_ref,
             dmix_ref, dpa_ref, dpb_ref, dg_ref, dya_ref, dym_ref, acc_ref):
        i = pl.program_id(0)

        @pl.when(i == 0)
        def _():
            acc_ref[...] = jnp.zeros_like(acc_ref)

        vrow = lambda r: v_ref[r:r + 1, :]
        dx1v = dx1_ref[...]
        mix = mix_ref[...].astype(F32)
        r2 = lax.rsqrt(_rowmean(mix * mix) + NORM_EPS)
        mixhat = mix * r2
        acc_ref[0:1, :] += _colsum(dx1v * (mixhat * vrow(R_G_POST_MIX)))
        dn2 = dx1v * vrow(R_GATE_M)
        acc_ref[1:2, :] += _colsum(dn2 * mixhat)
        dmixhat = dn2 * vrow(R_G_POST_MIX)
        dmixb = (r2 * (dmixhat - mixhat * _rowmean(dmixhat * mixhat))).astype(BF16)
        dmix_ref[...] = dmixb
        dmerged = _dot_nt(dmixb, wout_ref[...])
        sa = _sigmoid(pg_ref[:, 0:D_MODEL].astype(F32))
        sm = _sigmoid(pg_ref[:, D_MODEL:G_W].astype(F32))
        dpab = (dmerged * sa).astype(BF16)
        dpbb = (dmerged * sm).astype(BF16)
        dpa_ref[...] = dpab
        dpb_ref[...] = dpbb
        dg_ref[:, 0:D_MODEL] = (dmerged * pa_ref[...].astype(F32) * (sa * (1.0 - sa))).astype(BF16)
        dg_ref[:, D_MODEL:G_W] = (dmerged * pb_ref[...].astype(F32) * (sm * (1.0 - sm))).astype(BF16)
        dya_ref[...] = _dot_nt(dpab, wba_ref[...]).astype(BF16)
        dym_ref[...] = _dot_nt(dpbb, wbm_ref[...])

    row = lambda w: pl.BlockSpec((tm, w), lambda i: (i, 0))
    sd = lambda w, dt: jax.ShapeDtypeStruct((S, w), dt)
    return pl.pallas_call(
        body, name="mix_bwd", grid=(S // tm,),
        in_specs=[row(D_MODEL), row(D_MODEL), row(D_MODEL), row(D_MODEL), row(G_W), _full(vecs.shape),
                  _full(w_ba.shape), _full(w_bm.shape), _full(w_out.shape), pl.BlockSpec(memory_space=pl.ANY)],
        out_specs=[row(D_MODEL), row(D_MODEL), row(D_MODEL), row(G_W), row(512), row(512), _full((8, D_MODEL))],
        out_shape=[sd(D_MODEL, BF16), sd(D_MODEL, BF16), sd(D_MODEL, BF16), sd(G_W, BF16), sd(512, BF16), sd(512, F32),
                   jax.ShapeDtypeStruct((8, D_MODEL), F32)],
        compiler_params=_params(("arbitrary",)),
    )(dx1, mix, pa, pb, pg, vecs, w_ba, w_bm, w_out, after)


def _pre_bwd(pieces, x, dx1, vecs, w_cat, after):
    S = x.shape[0]
    tm = WIDE_TILE
    n = len(pieces)
    starts = [sum(p.shape[1] for p in pieces[:k]) for k in range(n + 1)]

    def body(*refs):
        p_refs = refs[:n]
        x_ref, dx1_ref, v_ref, w_ref, after_ref, dx_ref, acc_ref = refs[n:]
        i = pl.program_id(0)

        @pl.when(i == 0)
        def _():
            acc_ref[...] = jnp.zeros_like(acc_ref)

        vrow = lambda r: v_ref[r:r + 1, :]
        dh = _dot_nt(p_refs[0][...], w_ref[:, starts[0]:starts[1]])
        for k in range(1, n):
            dh = dh + _dot_nt(p_refs[k][...], w_ref[:, starts[k]:starts[k + 1]])
        xv = x_ref[...]
        r1 = lax.rsqrt(_rowmean(xv * xv) + NORM_EPS)
        xhat = xv * r1
        acc_ref[0:1, :] += _colsum(dh * (xhat * vrow(R_G_PRE_MIX)))
        acc_ref[1:2, :] += _colsum(dh)
        dxn = dh * (1.0 + vrow(R_SCALE_M))
        acc_ref[2:3, :] += _colsum(dxn * xhat)
        dxhat = dxn * vrow(R_G_PRE_MIX)
        dx_ref[...] = dx1_ref[...] + r1 * (dxhat - xhat * _rowmean(dxhat * xhat))

    row = lambda w: pl.BlockSpec((tm, w), lambda i: (i, 0))
    return pl.pallas_call(
        body, name="pre_bwd", grid=(S // tm,),
        in_specs=[row(p.shape[1]) for p in pieces] + [row(D_MODEL), row(D_MODEL), _full(vecs.shape), _full(w_cat.shape),
                                                         pl.BlockSpec(memory_space=pl.ANY)],
        out_specs=[row(D_MODEL), _full((8, D_MODEL))],
        out_shape=[jax.ShapeDtypeStruct((S, D_MODEL), F32), jax.ShapeDtypeStruct((8, D_MODEL), F32)],
        compiler_params=_params(("arbitrary",)),
    )(*pieces, x, dx1, vecs, w_cat, after)


def _matmul_tn(a, b, tn, name, ts=1024):
    S, K = a.shape
    N = b.shape[1]
    n_s = S // ts

    def body(a_ref, b_ref, o_ref, acc_ref):
        s = pl.program_id(1)

        @pl.when(s == 0)
        def _():
            acc_ref[...] = jnp.zeros_like(acc_ref)

        acc_ref[...] += _dot_tn(a_ref[...], b_ref[...])

        @pl.when(s == n_s - 1)
        def _():
            o_ref[...] = acc_ref[...].astype(BF16)

    return pl.pallas_call(
        body, name=name, grid=(N // tn, n_s),
        in_specs=[pl.BlockSpec((ts, K), lambda j, s: (s, 0)), pl.BlockSpec((ts, tn), lambda j, s: (s, j))],
        out_specs=pl.BlockSpec((K, tn), lambda j, s: (0, j)),
        out_shape=jax.ShapeDtypeStruct((K, N), BF16),
        scratch_shapes=[pltpu.VMEM((K, tn), F32)],
        compiler_params=_params(("parallel", "arbitrary")),
    )(a, b)


def _rope_swap(t):
    lane = lax.broadcasted_iota(jnp.int32, t.shape, 1)
    first = (lane & (HEAD_DIM - 1)) < (HEAD_DIM // 2)
    return jnp.where(first, pltpu.roll(t, LANES - HEAD_DIM // 2, 1), pltpu.roll(t, HEAD_DIM // 2, 1))


def _rope(t, cos, sin_signed):
    return t * cos + _rope_swap(t) * sin_signed


def _rope_t(d, cos, sin_signed):
    return d * cos + _rope_swap(d * sin_signed)


def _to_kv_lanes(chunk, p, h):
    lane = lax.broadcasted_iota(jnp.int32, chunk.shape, 1)
    src = chunk if p == h else pltpu.roll(chunk, HEAD_DIM, 1)
    return jnp.where((lane >> 6) == h, src, jnp.zeros_like(src))


def _from_kv_lanes(o_a, o_b, h):
    lane = lax.broadcasted_iota(jnp.int32, o_a.shape, 1)
    a = o_a if h == 0 else pltpu.roll(o_a, HEAD_DIM, 1)
    b = o_b if h == 1 else pltpu.roll(o_b, HEAD_DIM, 1)
    return jnp.where(lane < HEAD_DIM, a, b)


def _band_bias(n):
    blk = ATTN_BLOCK
    qi = lax.broadcasted_iota(jnp.int32, (blk, 2 * blk), 0)
    kj = lax.broadcasted_iota(jnp.int32, (blk, 2 * blk), 1)
    seen = (kj > qi) & (kj <= qi + blk) & ((n > 0) | (kj >= blk))
    return jnp.concatenate([jnp.where(seen, 0.0, NEG)] * N_Q_HEADS, axis=0)


def _stack_heads(chunks, h, dtype):
    parts = []
    for g in range(4):
        j = 4 * h + g
        parts.append(_to_kv_lanes(chunks[j // 2], j % 2, h))
    return jnp.concatenate(parts, axis=0).astype(dtype)


def _fused_call(parts, name, n_steps):
    counts = [(len(p["in_specs"]), len(p["out_specs"]), len(p["scratch"])) for p in parts]
    n_in, n_out = sum(c[0] for c in counts), sum(c[1] for c in counts)

    def kernel_fn(*refs):
        i = pl.program_id(0)
        groups, a, b, c = [], 0, n_in, n_in + n_out
        for ci, co, cs in counts:
            groups.append(refs[a:a + ci] + refs[b:b + co] + refs[c:c + cs])
            a, b, c = a + ci, b + co, c + cs
        for p, g in zip(parts, groups):
            p["init"](i, *g)
        for p, g in zip(parts, groups):
            p["body"](i, *g)

    flat = lambda key: [v for p in parts for v in p[key]]
    res = pl.pallas_call(
        kernel_fn, name=name, grid=(n_steps,), in_specs=flat("in_specs"), out_specs=flat("out_specs"),
        out_shape=flat("out_shape"), scratch_shapes=flat("scratch"), compiler_params=_params(("arbitrary",)),
    )(*flat("operands"))
    out, pos = [], 0
    for _, co, _ in counts:
        out.append(res[pos:pos + co])
        pos += co
    return out


def _attn_fwd_part(pa, cos, sin, sinks):
    S = pa.shape[0]
    blk = ATTN_BLOCK
    nb = S // blk

    def body(n, sink_ref, cur_ref, prev_ref, cos_ref, sin_ref, cosp_ref, sinp_ref,
             ya_ref, qr_ref, kr_ref, vb_ref, lse_ref):
        cos_c, sin_c = cos_ref[...], sin_ref[...]
        qch = [_rope(cur_ref[:, c * LANES:(c + 1) * LANES], cos_c, sin_c) * (HEAD_DIM ** -0.5) for c in range(4)]
        for c in range(4):
            qr_ref[:, c * LANES:(c + 1) * LANES] = qch[c].astype(BF16)
        k_cur = _rope(cur_ref[:, 512:640], cos_c, sin_c).astype(BF16)
        k_prev = _rope(prev_ref[:, 0:LANES], cosp_ref[...], sinp_ref[...]).astype(BF16)
        v_cur = cur_ref[:, 640:768].astype(BF16)
        v_prev = prev_ref[:, LANES:2 * LANES].astype(BF16)
        kr_ref[...] = k_cur
        vb_ref[...] = v_cur
        K = jnp.concatenate([k_prev, k_cur], axis=0)
        V = jnp.concatenate([v_prev, v_cur], axis=0)
        lane = lax.broadcasted_iota(jnp.int32, (blk, LANES), 1)
        s = jnp.concatenate([_dot_nt(_stack_heads(qch, h, BF16), K) for h in range(2)], axis=0)
        s = s + _band_bias(n)
        rowmax = jnp.max(s, axis=1, keepdims=True)
        hd = lambda x, j: x[j * blk:(j + 1) * blk]
        m = jnp.concatenate([jnp.maximum(hd(rowmax, j), sink_ref[j]) for j in range(N_Q_HEADS)], axis=0)
        p = jnp.exp(s - m)
        pb = p.astype(BF16)
        den = _rowsum_mxu(pb) + jnp.concatenate([jnp.exp(sink_ref[j] - hd(m, j)) for j in range(N_Q_HEADS)], axis=0)
        o = jnp.concatenate([_dot(pb[4 * h * blk:4 * (h + 1) * blk], V) for h in range(2)], axis=0) * _recip(den)
        lse = m + jnp.log(den)
        outs = [o[j * blk:(j + 1) * blk, :] for j in range(N_Q_HEADS)]
        lse_tile = jnp.zeros((blk, LANES), F32)
        for j in range(N_Q_HEADS):
            lse_tile = jnp.where(lane == j, lse[j * blk:(j + 1) * blk, :], lse_tile)
        for c in range(4):
            ya_ref[:, c * LANES:(c + 1) * LANES] = _from_kv_lanes(outs[2 * c], outs[2 * c + 1], c // 2).astype(BF16)
        lse_ref[...] = lse_tile

    prev = lambda n: jnp.maximum(n - 1, 0)
    sd = lambda w, dt: jax.ShapeDtypeStruct((S, w), dt)
    return dict(
        init=lambda n, *refs: None, body=body, scratch=[], operands=[sinks, pa, pa, cos, sin, cos, sin],
        in_specs=[pl.BlockSpec(memory_space=pltpu.SMEM),
                  pl.BlockSpec((blk, A_W), lambda n: (n, 0)),
                  pl.BlockSpec((blk, 256), lambda n: (prev(n), 2)),
                  pl.BlockSpec((blk, LANES), lambda n: (n, 0)), pl.BlockSpec((blk, LANES), lambda n: (n, 0)),
                  pl.BlockSpec((blk, LANES), lambda n: (prev(n), 0)), pl.BlockSpec((blk, LANES), lambda n: (prev(n), 0))],
        out_specs=[pl.BlockSpec((blk, 512), lambda n: (n, 0)), pl.BlockSpec((blk, 512), lambda n: (n, 0)),
                   pl.BlockSpec((blk, LANES), lambda n: (n, 0)), pl.BlockSpec((blk, LANES), lambda n: (n, 0)),
                   pl.BlockSpec((blk, LANES), lambda n: (n, 0))],
        out_shape=[sd(512, BF16), sd(512, BF16), sd(LANES, BF16), sd(LANES, BF16), sd(LANES, F32)])


def _attn_bwd_part(dya, qr, kr, vb, lse, cos, sin, sinks):
    S = dya.shape[0]
    blk = ATTN_BLOCK
    nb = S // blk

    def init(n, sink_ref, dya_ref, qr_ref, kc_ref, kp_ref, vc_ref, vp_ref, lse_ref, cos_ref, sin_ref, cosp_ref, sinp_ref,
             dqkv_ref, last_ref, dsink_ref, ck, cv, cq):
        @pl.when(n == 0)
        def _():
            ck[...] = jnp.zeros_like(ck)
            cv[...] = jnp.zeros_like(cv)
            cq[...] = jnp.zeros_like(cq)
            dsink_ref[...] = jnp.zeros_like(dsink_ref)

    def body(n, sink_ref, dya_ref, qr_ref, kc_ref, kp_ref, vc_ref, vp_ref, lse_ref, cos_ref, sin_ref, cosp_ref, sinp_ref,
             dqkv_ref, last_ref, dsink_ref, ck, cv, cq):
        K = jnp.concatenate([kp_ref[...], kc_ref[...]], axis=0)
        V = jnp.concatenate([vp_ref[...], vc_ref[...]], axis=0)
        qch = [qr_ref[:, c * LANES:(c + 1) * LANES] for c in range(4)]
        dch = [dya_ref[:, c * LANES:(c + 1) * LANES].astype(F32) for c in range(4)]
        lse_tile = lse_ref[...]
        lane8 = lax.broadcasted_iota(jnp.int32, (8, LANES), 1)
        grp = lambda x, h: x[4 * h * blk:4 * (h + 1) * blk]
        qs = jnp.concatenate([_stack_heads(qch, h, BF16) for h in range(2)], axis=0)
        dos = jnp.concatenate([_stack_heads(dch, h, BF16) for h in range(2)], axis=0)
        lse_col = jnp.concatenate([lse_tile[:, j:j + 1] for j in range(N_Q_HEADS)], axis=0)
        s = jnp.concatenate([_dot_nt(grp(qs, h), K) for h in range(2)], axis=0)
        p = jnp.exp(s + _band_bias(n) - lse_col)
        dp = jnp.concatenate([_dot_nt(grp(dos, h), V) for h in range(2)], axis=0)
        delta = jnp.sum(p * dp, axis=1, keepdims=True)
        dsb = (p * (dp - delta)).astype(BF16)
        pb = p.astype(BF16)
        dq = jnp.concatenate([_dot(grp(dsb, h), K) for h in range(2)], axis=0)
        dk_acc = _dot_tn(grp(dsb, 0), grp(qs, 0)) + _dot_tn(grp(dsb, 1), grp(qs, 1))
        dv_acc = _dot_tn(grp(pb, 0), grp(dos, 0)) + _dot_tn(grp(pb, 1), grp(dos, 1))
        dqs = [dq[j * blk:(j + 1) * blk, :] for j in range(N_Q_HEADS)]
        dsink = jnp.zeros((8, LANES), F32)
        for j in range(N_Q_HEADS):
            rows = slice(j * blk, (j + 1) * blk)
            ps_delta = jnp.exp(sink_ref[j] - lse_col[rows]) * delta[rows]
            dsink = jnp.where(lane8 == j, dsink - jnp.sum(ps_delta), dsink)
        dsink_ref[...] += dsink
        cos_c, sin_c = cos_ref[...], sin_ref[...]
        dqkv_ref[:, 0:512] = cq[...]
        dqkv_ref[:, 512:640] = _rope_t(dk_acc[0:blk, :] + ck[...], cosp_ref[...], sinp_ref[...]).astype(BF16)
        dqkv_ref[:, 640:768] = (dv_acc[0:blk, :] + cv[...]).astype(BF16)
        ck[...] = dk_acc[blk:2 * blk, :]
        cv[...] = dv_acc[blk:2 * blk, :]
        for c in range(4):
            dqc = _from_kv_lanes(dqs[2 * c], dqs[2 * c + 1], c // 2) * (HEAD_DIM ** -0.5)
            dq_c = _rope_t(dqc, cos_c, sin_c).astype(BF16)
            cq[:, c * LANES:(c + 1) * LANES] = dq_c
            last_ref[:, c * LANES:(c + 1) * LANES] = dq_c
        last_ref[:, 512:640] = _rope_t(dk_acc[blk:2 * blk, :], cos_c, sin_c).astype(BF16)
        last_ref[:, 640:768] = dv_acc[blk:2 * blk, :].astype(BF16)

    prev = lambda n: jnp.maximum(n - 1, 0)
    same = lambda n: n
    bs = lambda w, f: pl.BlockSpec((blk, w), lambda n: (f(n), 0))
    return dict(
        init=init, body=body, operands=[sinks, dya, qr, kr, kr, vb, vb, lse, cos, sin, cos, sin],
        in_specs=[pl.BlockSpec(memory_space=pltpu.SMEM),
                  bs(512, same), bs(512, same), bs(LANES, same), bs(LANES, prev), bs(LANES, same), bs(LANES, prev),
                  bs(LANES, same), bs(LANES, same), bs(LANES, same), bs(LANES, prev), bs(LANES, prev)],
        out_specs=[bs(A_W, prev), _full((blk, A_W)), _full((8, LANES))],
        out_shape=[jax.ShapeDtypeStruct((S, A_W), BF16), jax.ShapeDtypeStruct((blk, A_W), BF16),
                   jax.ShapeDtypeStruct((8, LANES), F32)],
        scratch=[pltpu.VMEM((blk, LANES), F32), pltpu.VMEM((blk, LANES), F32), pltpu.VMEM((blk, 512), BF16)])


def _split3(x):
    hi = x.astype(BF16)
    r1 = x - hi.astype(F32)
    mid = r1.astype(BF16)
    lo = (r1 - mid.astype(F32)).astype(BF16)
    return hi, mid, lo


def _tri_matmul(tri_b, x):
    hi, mid, lo = _split3(x)
    return _dot(tri_b, hi) + _dot(tri_b, mid) + _dot(tri_b, lo)


def _log_sigmoid(x):
    return jnp.minimum(x, 0.0) - jnp.log(1.0 + jnp.exp(-jnp.abs(x)))


def _shift_rows(cur, seam, k, down):
    L = cur.shape[0]
    row8 = lax.broadcasted_iota(jnp.int32, seam.shape, 0)
    if down:
        mixed = jnp.concatenate([cur[:L - 8], jnp.where(row8 >= 8 - k, seam, cur[L - 8:])], axis=0)
        return pltpu.roll(mixed, k, 0)
    mixed = jnp.concatenate([jnp.where(row8 < k, seam, cur[:8]), cur[8:]], axis=0)
    return pltpu.roll(mixed, L - k, 0)


def _conv_fwd(cur, tail, cw_ref):
    z = cw_ref[4:5, :]
    for k in range(3, 0, -1):
        z = z + _shift_rows(cur, tail, k, True) * cw_ref[3 - k:4 - k, :]
    return z + cur * cw_ref[3:4, :]


def _rowsum_mxu(x, two_pass=False):
    ones = jnp.ones((x.shape[1], LANES), BF16)
    hi = x.astype(BF16)
    s = _dot(hi, ones)
    if two_pass:
        s = s + _dot((x - hi.astype(F32)).astype(BF16), ones)
    return s


def _stack(f):
    return jnp.concatenate([f(h) for h in range(MLSTM_HEADS)], axis=0)


def _head(x, h):
    L = x.shape[0] // MLSTM_HEADS
    return x[h * L:(h + 1) * L]


def _mlstm_heads_fwd(qk, cur_ref, gt, b_all, c_prev, nmv, tri, eye):
    L = qk.shape[0]
    HD = MLSTM_HEAD_DIM
    W4 = MLSTM_HEADS * HD
    col2row = lambda x: jnp.sum(jnp.where(eye, x, 0.0), axis=0, keepdims=True)
    b_col = _stack(lambda h: b_all[:, 4 + h:5 + h])
    i_col = _stack(lambda h: gt[:, h:h + 1])
    b_row = _stack(lambda h: jnp.broadcast_to(col2row(b_all[:, 4 + h:5 + h]), (L, L)))
    i_row = _stack(lambda h: jnp.broadcast_to(col2row(gt[:, h:h + 1]), (L, L)))
    bl = _stack(lambda h: jnp.broadcast_to(b_all[L - 1:L, 4 + h:5 + h], (L, 1)))
    m_prev = _stack(lambda h: jnp.broadcast_to(nmv[4 + h:5 + h, 0:1], (L, 1)))
    n_prev = _stack(lambda h: jnp.broadcast_to(nmv[h:h + 1, :], (L, HD)))
    tri4 = jnp.concatenate([tri] * MLSTM_HEADS, axis=0)
    Dm = jnp.where(tri4, b_col - b_row + i_row, NEG)
    inter = b_col + m_prev
    m_t = jnp.maximum(inter, jnp.max(Dm, axis=1, keepdims=True))
    W = jnp.exp(Dm - m_t)
    e_t = jnp.exp(inter - m_t)
    q = _stack(lambda h: qk[:, h * HD:(h + 1) * HD])
    k = _stack(lambda h: qk[:, W4 + h * HD:W4 + (h + 1) * HD]) * (HD ** -0.5)
    v = _stack(lambda h: cur_ref[:, 2 * W4 + h * HD:2 * W4 + (h + 1) * HD])
    qb, kb, vb = q.astype(BF16), k.astype(BF16), v.astype(BF16)
    Sc = _stack(lambda h: _dot_nt(_head(qb, h), _head(kb, h))) * W
    Scb = Sc.astype(BF16)
    cb = [c.astype(BF16) for c in c_prev]
    P1 = _stack(lambda h: _dot(_head(qb, h), cb[h]))
    num = _stack(lambda h: _dot(_head(Scb, h), _head(vb, h))) + e_t * P1
    qn = _rowsum_mxu(q * n_prev)
    den = _rowsum_mxu(Scb) + e_t * qn
    floor = jnp.broadcast_to(jnp.exp(-m_t), den.shape)
    inv_g = _recip(jnp.maximum(jnp.abs(den), floor))
    hv = num * inv_g
    a_col = bl - b_col + i_col
    a_max = _stack(lambda h: jnp.broadcast_to(jnp.max(_head(a_col, h), axis=0, keepdims=True), (L, 1)))
    m_new = jnp.maximum(bl + m_prev, a_max)
    dec = jnp.exp(bl + m_prev - m_new)
    u_col = jnp.exp(a_col - m_new)
    return dict(W=W, e_t=e_t, q=q, k=k, v=v, qb=qb, kb=kb, vb=vb, cb=cb, Sc=Sc, Scb=Scb, P1=P1, qn=qn, den=den,
                floor=floor, inv_g=inv_g, hv=hv, n_prev=n_prev, m_new=m_new, dec=dec, u_col=u_col)


def _mlstm_fwd_part(pm, pif, cw, sv):
    S = pm.shape[0]
    L = MLSTM_CHUNK
    nc = S // L
    HD = MLSTM_HEAD_DIM
    W4 = MLSTM_HEADS * HD

    def init(c, cur_ref, pif_ref, cw_ref, sv_ref, ym_ref, z_ref, cst_ref, nst_ref, C, nm, tail):
        @pl.when(c == 0)
        def _():
            C[...] = jnp.zeros_like(C)
            nm[...] = jnp.zeros_like(nm)
            tail[...] = jnp.zeros_like(tail)

    def body(c, cur_ref, pif_ref, cw_ref, sv_ref, ym_ref, z_ref, cst_ref, nst_ref, C, nm, tail):
        z = _conv_fwd(cur_ref[:, 0:2 * W4], tail[...], cw_ref)
        tail[...] = cur_ref[L - 8:L, 0:2 * W4]
        z_ref[...] = z
        qk = z * _sigmoid(z)
        gt = pif_ref[...] + sv_ref[1:2, 0:LANES]
        r_i = lax.broadcasted_iota(jnp.int32, (L, L), 0)
        c_i = lax.broadcasted_iota(jnp.int32, (L, L), 1)
        tri = c_i <= r_i
        eye = c_i == r_i
        b_all = _tri_matmul(tri.astype(BF16), _log_sigmoid(gt))
        nmv = nm[...]
        nst_ref[0] = nmv
        c_prev = [C[h] for h in range(MLSTM_HEADS)]
        f = _mlstm_heads_fwd(qk, cur_ref, gt, b_all, c_prev, nmv, tri, eye)
        hv = f["hv"]
        xc = hv - _rowsum_mxu(hv, True) * (1.0 / HD)
        hhat = xc * lax.rsqrt(_rowsum_mxu(xc * xc) * (1.0 / HD) + NORM_EPS)
        so = _sigmoid(_stack(lambda h: cur_ref[:, 3 * W4 + h * HD:3 * W4 + (h + 1) * HD]))
        wn = _stack(lambda h: jnp.broadcast_to(sv_ref[0:1, h * HD:(h + 1) * HD], (L, HD)))
        y = (so * hhat * wn).astype(BF16)
        kw = f["k"] * f["u_col"]
        kwb = kw.astype(BF16)
        n_new, m_new = [], []
        for h in range(MLSTM_HEADS):
            cst_ref[0, h] = c_prev[h]
            ym_ref[:, h * HD:(h + 1) * HD] = _head(y, h)
            dec = f["dec"][h * L:h * L + 1, :]
            C[h] = dec * c_prev[h] + _dot_tn(_head(kwb, h), _head(f["vb"], h))
            n_new.append(dec * nmv[h:h + 1, :] + _colsum(_head(kw, h)))
            m_new.append(jnp.broadcast_to(f["m_new"][h * L:h * L + 1, :], (1, LANES)))
        nm[...] = jnp.concatenate(n_new + m_new, axis=0)

    return dict(
        init=init, body=body, operands=[pm, pif, cw, sv],
        in_specs=[pl.BlockSpec((L, M_W), lambda c: (c, 0)),
                  pl.BlockSpec((L, IF_W), lambda c: (c, 0)), _full(cw.shape), _full(sv.shape)],
        out_specs=[pl.BlockSpec((L, W4), lambda c: (c, 0)), pl.BlockSpec((L, 2 * W4), lambda c: (c, 0)),
                   pl.BlockSpec((1, MLSTM_HEADS, HD, HD), lambda c: (c, 0, 0, 0)),
                   pl.BlockSpec((1, 8, LANES), lambda c: (c, 0, 0))],
        out_shape=[jax.ShapeDtypeStruct((S, W4), BF16), jax.ShapeDtypeStruct((S, 2 * W4), F32),
                   jax.ShapeDtypeStruct((nc, MLSTM_HEADS, HD, HD), F32), jax.ShapeDtypeStruct((nc, 8, LANES), F32)],
        scratch=[pltpu.VMEM((MLSTM_HEADS, HD, HD), F32), pltpu.VMEM((8, LANES), F32), pltpu.VMEM((8, 2 * W4), F32)])


def _mlstm_bwd_part(pm, zc, pif, cw, sv, dym, cst, nst):
    S = pm.shape[0]
    L = MLSTM_CHUNK
    nc = S // L
    HD = MLSTM_HEAD_DIM
    W4 = MLSTM_HEADS * HD

    def init(r, cur_ref, z_ref, pif_ref, cw_ref, sv_ref, dym_ref, cst_ref, nst_ref,
             dm_ref, dcw_ref, dsv_ref, dC, dn, dz_next, dqk):
        @pl.when(r == 0)
        def _():
            dC[...] = jnp.zeros_like(dC)
            dn[...] = jnp.zeros_like(dn)
            dz_next[...] = jnp.zeros_like(dz_next)
            dcw_ref[...] = jnp.zeros_like(dcw_ref)
            dsv_ref[...] = jnp.zeros_like(dsv_ref)

    def body(r, cur_ref, z_ref, pif_ref, cw_ref, sv_ref, dym_ref, cst_ref, nst_ref,
             dm_ref, dcw_ref, dsv_ref, dC, dn, dz_next, dqk):
        z = z_ref[...]
        sgz = _sigmoid(z)
        qk = z * sgz
        gt = pif_ref[...] + sv_ref[1:2, 0:LANES]
        r_i = lax.broadcasted_iota(jnp.int32, (L, L), 0)
        c_i = lax.broadcasted_iota(jnp.int32, (L, L), 1)
        tri = c_i <= r_i
        eye = c_i == r_i
        b_all = _tri_matmul(tri.astype(BF16), _log_sigmoid(gt))
        lane = lax.broadcasted_iota(jnp.int32, (L, LANES), 1)
        rowl = lax.broadcasted_iota(jnp.int32, (L, 1), 0)
        nmv = nst_ref[0]
        heads = range(MLSTM_HEADS)
        c_prev = [cst_ref[0, h] for h in heads]
        f = _mlstm_heads_fwd(qk, cur_ref, gt, b_all, c_prev, nmv, tri, eye)
        hv, inv_g, den, e_t, u_col, n_prev = f["hv"], f["inv_g"], f["den"], f["e_t"], f["u_col"], f["n_prev"]
        q, k, v, qb, kb, vb, Sc, Scb, W = f["q"], f["k"], f["v"], f["qb"], f["kb"], f["vb"], f["Sc"], f["Scb"], f["W"]
        xc = hv - _rowsum_mxu(hv, True) * (1.0 / HD)
        rstd = lax.rsqrt(_rowsum_mxu(xc * xc) * (1.0 / HD) + NORM_EPS)
        hhat = xc * rstd
        wn = _stack(lambda h: jnp.broadcast_to(sv_ref[0:1, h * HD:(h + 1) * HD], (L, HD)))
        so = _sigmoid(_stack(lambda h: cur_ref[:, 3 * W4 + h * HD:3 * W4 + (h + 1) * HD]))
        dy = _stack(lambda h: dym_ref[:, h * HD:(h + 1) * HD])
        d_o = (dy * hhat * wn * (so * (1.0 - so))).astype(BF16)
        dln = dy * so
        dwn = dln * hhat
        dhhat = dln * wn
        m2 = _rowsum_mxu(dhhat * hhat) * (1.0 / HD)
        dh = rstd * (dhhat - _rowsum_mxu(dhhat) * (1.0 / HD) - hhat * m2)
        dnum = dh * inv_g
        active = jnp.abs(den) > f["floor"]
        dden = jnp.where(active, -(HD * NORM_EPS) * m2 * rstd * rstd * inv_g * jnp.where(den >= 0.0, 1.0, -1.0), 0.0)
        dnumb = dnum.astype(BF16)
        dSc = _stack(lambda h: _dot_nt(_head(dnumb, h), _head(vb, h))) + dden
        dA = (dSc * W).astype(BF16)
        G = dSc * Sc
        Gb = G.astype(BF16)
        Gl = (G - Gb.astype(F32)).astype(BF16)
        ones = jnp.ones((L, LANES), BF16)
        Gr = _dot(Gb, ones) + _dot(Gl, ones)
        Gc = _stack(lambda h: _dot_tn(_head(Gb, h), ones) + _dot_tn(_head(Gl, h), ones))
        dCn = [dC[h] for h in heads]
        dCnb = [d.astype(BF16) for d in dCn]
        dnv = dn[...]
        dn_new = _stack(lambda h: jnp.broadcast_to(dnv[h:h + 1, :], (L, HD)))
        kdC = _stack(lambda h: _dot(_head(kb, h), dCnb[h]))
        vdC = _stack(lambda h: _dot_nt(_head(vb, h), dCnb[h]))
        dv = (_stack(lambda h: _dot_tn(_head(Scb, h), _head(dnumb, h))) + u_col * kdC).astype(BF16)
        dq = _stack(lambda h: _dot(_head(dA, h), _head(kb, h))) \
            + e_t * _stack(lambda h: _dot_nt(_head(dnumb, h), f["cb"][h])) + (e_t * dden) * n_prev
        dk = (_stack(lambda h: _dot_tn(_head(dA, h), _head(qb, h))) + u_col * (vdC + dn_new)) * (HD ** -0.5)
        E = (_rowsum_mxu(f["P1"] * dnum, True) + dden * f["qn"]) * e_t
        U = _rowsum_mxu(kdC * v + k * dn_new, True) * u_col
        qe = (q * e_t).astype(BF16)
        qd = (e_t * dden) * q
        di = Gc + U
        db = Gr + E - Gc - U
        di_tile = jnp.zeros((L, LANES), F32)
        db_tile = jnp.zeros((L, LANES), F32)
        dn_rows = []
        for h in heads:
            dec = f["dec"][h * L:h * L + 1, :]
            ddec = jnp.sum(dCn[h] * c_prev[h]) + jnp.sum(dnv[h:h + 1, :] * nmv[h:h + 1, :])
            dbl = ddec * dec + jnp.sum(_head(U, h), axis=0, keepdims=True)
            di_tile = jnp.where(lane == h, _head(di, h), di_tile)
            db_tile = jnp.where(lane == 4 + h, _head(db, h) + jnp.where(rowl == L - 1, dbl, 0.0), db_tile)
            dC[h] = dec * dCn[h] + _dot_tn(_head(qe, h), _head(dnumb, h))
            dn_rows.append(dec * dnv[h:h + 1, :] + _colsum(_head(qd, h)))
            dsv_ref[0:1, h * HD:(h + 1) * HD] += _colsum(_head(dwn, h))
            dqk[:, h * HD:(h + 1) * HD] = _head(dq, h)
            dqk[:, W4 + h * HD:W4 + (h + 1) * HD] = _head(dk, h)
            dm_ref[:, 2 * W4 + h * HD:2 * W4 + (h + 1) * HD] = _head(dv, h)
            dm_ref[:, 3 * W4 + h * HD:3 * W4 + (h + 1) * HD] = _head(d_o, h)
        dn[...] = jnp.concatenate(dn_rows + [jnp.zeros((8 - MLSTM_HEADS, LANES), F32)], axis=0)
        dlf = _tri_matmul((r_i <= c_i).astype(BF16), db_tile)
        dif = jnp.where(lane < 4, di_tile, jnp.where(lane < 8, dlf * (1.0 - _sigmoid(gt)), 0.0))
        dm_ref[:, M_W:M_W + IF_W] = dif.astype(BF16)
        dsv_ref[1:2, 0:LANES] += _colsum(dif)
        for half in range(2):
            cols = slice(half * W4, (half + 1) * W4)
            zh = z_ref[:, cols]
            sh = _sigmoid(zh)
            dz = dqk[:, cols] * (sh * (1.0 + zh * (1.0 - sh)))
            dcw_ref[4:5, cols] += _colsum(dz)
            u = cur_ref[:, cols]
            du_in = dz * cw_ref[3:4, cols]
            dcw_ref[3:4, cols] += _colsum(dz * u)
            for k in range(1, 4):
                up = _shift_rows(dz, dz_next[:, cols], k, False)
                dcw_ref[3 - k:4 - k, cols] += _colsum(up * u)
                du_in = du_in + up * cw_ref[3 - k:4 - k, cols]
            dz_next[:, cols] = dz[0:8, :]
            dm_ref[:, cols] = du_in.astype(BF16)

    cidx = lambda r: nc - 1 - r
    return dict(
        init=init, body=body, operands=[pm, zc, pif, cw, sv, dym, cst, nst],
        in_specs=[pl.BlockSpec((L, M_W), lambda r: (cidx(r), 0)), pl.BlockSpec((L, 2 * W4), lambda r: (cidx(r), 0)),
                  pl.BlockSpec((L, IF_W), lambda r: (cidx(r), 0)), _full(cw.shape), _full(sv.shape),
                  pl.BlockSpec((L, W4), lambda r: (cidx(r), 0)),
                  pl.BlockSpec((1, MLSTM_HEADS, HD, HD), lambda r: (cidx(r), 0, 0, 0)),
                  pl.BlockSpec((1, 8, LANES), lambda r: (cidx(r), 0, 0))],
        out_specs=[pl.BlockSpec((L, M_W + IF_W), lambda r: (cidx(r), 0)), _full((8, 2 * W4)), _full((8, W4))],
        out_shape=[jax.ShapeDtypeStruct((S, M_W + IF_W), BF16),
                   jax.ShapeDtypeStruct((8, 2 * W4), F32), jax.ShapeDtypeStruct((8, W4), F32)],
        scratch=[pltpu.VMEM((MLSTM_HEADS, HD, HD), F32), pltpu.VMEM((8, LANES), F32),
                 pltpu.VMEM((8, 2 * W4), F32), pltpu.VMEM((L, 2 * W4), F32)])


def _rope_tables(positions):
    half = HEAD_DIM // 2
    inv_freq = ROPE_THETA ** (-2.0 * jnp.arange(half, dtype=F32) / HEAD_DIM)
    ang = positions.astype(F32)[:, None] * inv_freq
    cos = jnp.tile(jnp.cos(ang), (1, LANES // half))
    sign = jnp.tile(jnp.concatenate([-jnp.ones((half,), F32), jnp.ones((half,), F32)]), LANES // HEAD_DIM)
    sin = jnp.tile(jnp.sin(ang), (1, LANES // half)) * sign
    return cos, sin


def _local_step(x, tgt, positions, mod, gains, w_cat, w_ba, w_bm, w_out, w_gate, w_up, w_down,
                conv_w, conv_b, b_if, sinks, norm_w):
    t = _tables(mod, gains, conv_w, conv_b, b_if, norm_w, positions)
    a = _mixer_fwd(x, t, sinks, w_cat)
    b = _ffn_part(x, tgt, t, a, w_ba, w_bm, w_out, w_gate, w_up, w_down)
    c = _mixer_bwd(b["dx1"], t, a, b, sinks, w_ba, w_bm, w_out)
    grad_x, acc_p = _pre_bwd(c["dproj"], x, b["dx1"], t["vecs"], w_cat, t["after"])
    big = dict(w_cat=jnp.concatenate(c["g_w_cat"], axis=1), w_ba=c["g_w_ba"], w_bm=c["g_w_bm"], w_out=c["g_w_out"], w_gate=b["g_w_gate"],
               w_up=b["g_w_up"], w_down=b["g_w_down"])
    return b["loss"], grad_x, big, _small_grads(acc_p, b, c)


def _tables(mod, gains, conv_w, conv_b, b_if, norm_w, positions):
    cos, sin = _rope_tables(positions)
    return dict(
        vecs=jnp.concatenate([mod, gains, jnp.zeros((6, D_MODEL), F32)], axis=0),
        cw=jnp.concatenate([conv_w, conv_b.reshape(1, -1), jnp.zeros((3, 2 * 512), F32)], axis=0),
        sv=jnp.zeros((8, 512), F32).at[0].set(norm_w).at[1, 0:8].set(b_if), cos=cos, sin=sin,
        after=jnp.zeros((8, LANES), F32))


def _mixer_fwd(x, t, sinks, w_cat):
    h, pa, pm, pif, pg = _pre_proj(x, t["vecs"], w_cat, t["after"])
    n_blk = x.shape[0] // ATTN_BLOCK
    (ya, qr, kr, vb, lse), = _fused_call([_attn_fwd_part(pa, t["cos"], t["sin"], sinks)], "attn_fwd", n_blk)
    (ym, zc, cst, nst), = _fused_call([_mlstm_fwd_part(pm, pif, t["cw"], t["sv"])], "mlstm_fwd", n_blk)
    return dict(h=h, pm=pm, pif=pif, pg=pg, ya=ya, qr=qr, kr=kr, vb=vb, lse=lse, ym=ym, zc=zc, cst=cst, nst=nst)


def _ffn_part(x, tgt, t, a, w_ba, w_bm, w_out, w_gate, w_up, w_down):
    x1, merged, mix, pba, pbm = _mix_fwd(x, a["ya"], a["ym"], a["pg"], t["vecs"], w_ba, w_bm, w_out)
    dx1, h2, hid, da, du, dff, acc_f, loss = _ffn_fwd_bwd(x1, tgt, t["vecs"], w_gate, w_up, w_down)
    return dict(merged=merged, mix=mix, pba=pba, pbm=pbm, dx1=dx1, acc_f=acc_f, loss=loss[0, 0],
                g_w_gate=_matmul_tn(da, h2, 1024, "dw_ffn_gate"),
                g_w_up=_matmul_tn(du, h2, 1024, "dw_ffn_up"),
                g_w_down=_matmul_tn(hid, dff, 1024, "dw_ffn_down"))


def _mixer_bwd(dx1, t, a, b, sinks, w_ba, w_bm, w_out):
    dmix, dpa, dpb, dg, dya, dym, acc_m = _mix_bwd(dx1, b["mix"], b["pba"], b["pbm"], a["pg"], t["vecs"], w_ba, w_bm, w_out,
                                                   t["after"])
    g_w_out = _matmul_tn(b["merged"], dmix, 1024, "dw_out")
    g_w_ba = _matmul_tn(a["ya"], dpa, 1024, "dw_branch_attn")
    g_w_bm = _matmul_tn(a["ym"], dpb, 1024, "dw_branch_mlstm")
    n_blk = dx1.shape[0] // ATTN_BLOCK
    (dqkv, dqkv_last, dsink), = _fused_call(
        [_attn_bwd_part(dya, a["qr"], a["kr"], a["vb"], a["lse"], t["cos"], t["sin"], sinks)], "attn_bwd", n_blk)
    (dm, dcw, dsv), = _fused_call(
        [_mlstm_bwd_part(a["pm"], a["zc"], a["pif"], t["cw"], t["sv"], dym, a["cst"], a["nst"])], "mlstm_bwd", n_blk)
    dqkv = lax.dynamic_update_slice(dqkv, dqkv_last, (dqkv.shape[0] - ATTN_BLOCK, 0))
    dproj = [dqkv, dm, dg]
    g_w_cat = [_matmul_tn(a["h"], p, 1024 if p.shape[1] % 1024 == 0 else p.shape[1], "dw_in_" + n)
               for p, n in zip(dproj, ("attn", "mlstm", "branch"))]
    return dict(dproj=dproj, g_w_cat=g_w_cat, g_w_out=g_w_out, g_w_ba=g_w_ba,
                g_w_bm=g_w_bm, acc_m=acc_m, dsink=dsink, dcw=dcw, dsv=dsv)


def _small_grads(acc_p, b, c):
    acc_f, acc_m = b["acc_f"], c["acc_m"]
    dmod = jnp.stack([acc_p[1], acc_p[0], acc_m[0], acc_f[3], acc_f[2], acc_f[0]])
    dgains = jnp.stack([acc_p[2], acc_m[1], acc_f[4], acc_f[1]])
    return dict(dmod=dmod, dgains=dgains, dconv_w=c["dcw"][0:4], dconv_b=c["dcw"][4], db_if=c["dsv"][1, 0:8],
                dsinks=c["dsink"][0, 0:8], dnorm_w=c["dsv"][0])


MESH_ID = pl.DeviceIdType.MESH


def _mesh_pos():
    return lax.axis_index("x"), lax.axis_index("y"), lax.axis_index("c")


def _flip(v, bit):
    return 1 - v if bit else v


def _relations():
    return [((r >> 2) & 1, (r >> 1) & 1, r & 1) for r in range(1, N_DEV)]


def _small_exchange(p, gather, name):
    R, V = p.shape[-2:]

    def body(p_ref, out_ref, send_sems, recv_sems):
        x, y, c = _mesh_pos()
        me = 4 * x + 2 * y + c
        out_ref[me] = p_ref[...] if gather else p_ref[me]
        peers = []
        for dx, dy, dc in _relations():
            px, py, pc = _flip(x, dx), _flip(y, dy), _flip(c, dc)
            peers.append(((px, py, pc), 4 * px + 2 * py + pc))

        def copy(k, landing):
            peer, pid = peers[k]
            return pltpu.make_async_remote_copy(
                src_ref=p_ref if gather else p_ref.at[pid], dst_ref=out_ref.at[landing],
                send_sem=send_sems.at[k], recv_sem=recv_sems.at[k], device_id=peer, device_id_type=MESH_ID)

        sends = [copy(k, me) for k in range(N_DEV - 1)]
        for cp in sends:
            cp.start()
        for k in range(N_DEV - 1):
            copy(k, peers[k][1]).wait_recv()
        for cp in sends:
            cp.wait_send()

    vm = pl.BlockSpec(memory_space=pltpu.VMEM)
    return pl.pallas_call(
        body, name=name, in_specs=[vm], out_specs=vm,
        out_shape=jax.ShapeDtypeStruct((N_DEV, R, V), F32),
        scratch_shapes=[pltpu.SemaphoreType.DMA((N_DEV - 1,)), pltpu.SemaphoreType.DMA((N_DEV - 1,))],
        compiler_params=pltpu.CompilerParams(vmem_limit_bytes=VMEM_LIMIT),
    )(p)


HBM_SPEC = pl.BlockSpec(memory_space=pltpu.HBM)
SEM_SPEC = pl.BlockSpec(memory_space=pltpu.SEMAPHORE)


def _peers(x, y, c):
    out = []
    for dx, dy, dc in _relations():
        px, py, pc = _flip(x, dx), _flip(y, dy), _flip(c, dc)
        out.append(((px, py, pc), 4 * px + 2 * py + pc))
    return out


def _exchange_start(arrs, gather, after, name):
    n = len(arrs)
    me_out = 4 * lax.axis_index("x") + 2 * lax.axis_index("y") + lax.axis_index("c")
    lands = []
    for a in arrs:
        own = a[None] if gather else lax.dynamic_index_in_dim(a, me_out, 0, keepdims=True)
        empty = lax.empty(((N_DEV,) + a.shape) if gather else a.shape, a.dtype)
        lands.append(lax.dynamic_update_index_in_dim(empty, own, me_out, 0))

    def body(*refs):
        a_refs, l_refs = refs[:n], refs[n:2 * n]
        send_sems, recv_sems = refs[2 * n + 1], refs[2 * n + 2]
        token = refs[4 * n + 3]
        x, y, c = _mesh_pos()
        me = 4 * x + 2 * y + c
        for a in range(n):
            for k, (peer, pid) in enumerate(_peers(x, y, c)):
                pltpu.make_async_remote_copy(
                    src_ref=a_refs[a] if gather else a_refs[a].at[pid], dst_ref=l_refs[a].at[me],
                    send_sem=send_sems.at[a * (N_DEV - 1) + k], recv_sem=recv_sems.at[a * (N_DEV - 1) + k],
                    device_id=peer, device_id_type=MESH_ID).start()
        token[...] = jnp.zeros_like(token)

    sem = pltpu.SemaphoreType.DMA((n * (N_DEV - 1),))
    hbm = lambda a: pltpu.with_memory_space_constraint(a, pltpu.HBM)
    res = pl.pallas_call(
        body, name=name,
        out_shape=(sem, sem, *[pltpu.HBM(a.shape, a.dtype) for a in arrs], *[pltpu.HBM(l.shape, l.dtype) for l in lands],
                   jax.ShapeDtypeStruct((8, LANES), F32)),
        in_specs=[HBM_SPEC] * (2 * n) + [pl.BlockSpec(memory_space=pl.ANY)],
        out_specs=(SEM_SPEC, SEM_SPEC, *[HBM_SPEC] * (2 * n), pl.BlockSpec(memory_space=pltpu.VMEM)),
        input_output_aliases={i: 2 + i for i in range(2 * n)},
        compiler_params=pltpu.CompilerParams(has_side_effects=pltpu.SideEffectType.DATAFLOW_SIDE_EFFECTING),
    )(*[hbm(a) for a in arrs], *[hbm(l) for l in lands], after)
    return dict(sems=res[0:2], arrs=res[2:2 + n], lands=res[2 + n:2 + 2 * n], token=res[2 + 2 * n], gather=gather)


def _exchange_wait(st, after, name):
    n = len(st["arrs"])
    gather = st["gather"]

    def body(*refs):
        a_refs, l_refs = refs[:n], refs[n:2 * n]
        send_sems, recv_sems = refs[2 * n], refs[2 * n + 1]
        x, y, c = _mesh_pos()
        for a in range(n):
            for k, (peer, pid) in enumerate(_peers(x, y, c)):
                cp = pltpu.make_async_remote_copy(
                    src_ref=a_refs[a] if gather else a_refs[a].at[pid], dst_ref=l_refs[a].at[pid],
                    send_sem=send_sems.at[a * (N_DEV - 1) + k], recv_sem=recv_sems.at[a * (N_DEV - 1) + k],
                    device_id=peer, device_id_type=MESH_ID)
                cp.wait_send()
                cp.wait_recv()

    both = list(st["arrs"]) + list(st["lands"])
    res = pl.pallas_call(
        body, name=name, out_shape=[pltpu.HBM(a.shape, a.dtype) for a in both],
        in_specs=[HBM_SPEC] * (2 * n) + [SEM_SPEC, SEM_SPEC, pl.BlockSpec(memory_space=pl.ANY)],
        out_specs=[HBM_SPEC] * (2 * n), input_output_aliases={i: i for i in range(2 * n)},
        compiler_params=pltpu.CompilerParams(has_side_effects=pltpu.SideEffectType.DATAFLOW_SIDE_EFFECTING),
    )(*both, *st["sems"], after)
    return res[n:2 * n]


def _gather_front(shards, c8, w_ada, b_cols):
    n = len(shards)
    ada_w = w_ada.shape[1]
    chunk = c8.shape[1]

    def body(*refs):
        p_refs = refs[:n]
        c_ref, wa_ref, b_ref = refs[n:n + 3]
        out_refs = refs[n + 3:2 * n + 3]
        cg_ref, mod_ref = refs[2 * n + 3], refs[2 * n + 4]
        send_sems, recv_sems, local_sems, c_send, c_recv, m_send, m_recv, pbuf = refs[2 * n + 5:]
        x, y, c = _mesh_pos()
        me_id = 4 * x + 2 * y + c
        me, sibling = (x, y, c), (x, y, 1 - c)
        chips = [(1 - x, y), (x, 1 - y), (1 - x, 1 - y)]
        peers = _peers(x, y, c)

        def copy(a, k, block, to, own=False):
            slot = out_refs[a].at[4 * block[0] + 2 * block[1] + block[2]]
            return pltpu.make_async_remote_copy(
                src_ref=p_refs[a] if own else slot, dst_ref=slot,
                send_sem=send_sems.at[a, k], recv_sem=recv_sems.at[a, k], device_id=to, device_id_type=MESH_ID)

        def c_copy(k, landing):
            return pltpu.make_async_remote_copy(
                src_ref=c_ref, dst_ref=cg_ref.at[landing], send_sem=c_send.at[k], recv_sem=c_recv.at[k],
                device_id=peers[k][0], device_id_type=MESH_ID)

        cg_ref[me_id] = c_ref[...]
        c_sends = [c_copy(k, me_id) for k in range(N_DEV - 1)]
        for cp in c_sends:
            cp.start()
        mine = [pltpu.make_async_copy(p_refs[a], out_refs[a].at[me_id], local_sems.at[a]) for a in range(n)]
        for cp in mine:
            cp.start()
        first = []
        for a in range(n):
            first.append(copy(a, 0, me, sibling, own=True))
            first += [copy(a, 1 + j, me, (*chip, c), own=True) for j, chip in enumerate(chips)]
        for cp in first:
            cp.start()
        for k in range(N_DEV - 1):
            c_copy(k, peers[k][1]).wait_recv()
        prod = b_ref[...] + jnp.zeros((N_DEV, ada_w), F32)
        for j in range(D_MODEL // chunk):
            prod = prod + _dot(cg_ref[:, j, :].astype(BF16), wa_ref[j * chunk:(j + 1) * chunk, :].astype(BF16))
        for k in range(N_DEV):
            pbuf[k] = jnp.broadcast_to(prod[k:k + 1, :], (8, ada_w))

        def m_copy(k, landing):
            return pltpu.make_async_remote_copy(
                src_ref=pbuf.at[peers[k][1]], dst_ref=mod_ref.at[landing], send_sem=m_send.at[k], recv_sem=m_recv.at[k],
                device_id=peers[k][0], device_id_type=MESH_ID)

        mod_ref[me_id] = pbuf[me_id]
        m_sends = [m_copy(k, me_id) for k in range(N_DEV - 1)]
        for cp in m_sends:
            cp.start()

        passed = []
        for j, chip in enumerate(chips):
            for a in range(n):
                copy(a, 1 + j, (*chip, c), me).wait_recv()
                passed.append(copy(a, 4 + j, (*chip, c), sibling))
                passed[-1].start()
        for a in range(n):
            copy(a, 0, sibling, me).wait_recv()
            for j, chip in enumerate(chips):
                copy(a, 4 + j, (*chip, 1 - c), me).wait_recv()
        for k in range(N_DEV - 1):
            m_copy(k, peers[k][1]).wait_recv()
        for cp in first + passed + c_sends + m_sends:
            cp.wait_send()
        for cp in mine:
            cp.wait()

    hbm = pl.BlockSpec(memory_space=pl.ANY)
    vm = pl.BlockSpec(memory_space=pltpu.VMEM)
    sem7 = pltpu.SemaphoreType.DMA((N_DEV - 1,))
    res = pl.pallas_call(
        body, name="gather_front", in_specs=[hbm] * n + [vm, vm, vm], out_specs=[hbm] * n + [vm, vm],
        out_shape=[jax.ShapeDtypeStruct((N_DEV,) + s.shape, s.dtype) for s in shards]
        + [jax.ShapeDtypeStruct((N_DEV,) + c8.shape, F32), jax.ShapeDtypeStruct((N_DEV, 8, ada_w), F32)],
        scratch_shapes=[pltpu.SemaphoreType.DMA((n, N_DEV - 1)), pltpu.SemaphoreType.DMA((n, N_DEV - 1)),
                        pltpu.SemaphoreType.DMA((n,)), sem7, sem7, sem7, sem7, pltpu.VMEM((N_DEV, 8, ada_w), F32)],
        compiler_params=pltpu.CompilerParams(vmem_limit_bytes=VMEM_LIMIT),
    )(*shards, c8, w_ada, b_cols)
    return res[:n], res[n], res[n + 1]


def _adamw(w, g, m, v):
    m2 = ADAM_B1 * m + (1.0 - ADAM_B1) * g
    v2 = ADAM_B2 * v + (1.0 - ADAM_B2) * (g * g)
    m_hat = m2 / (1.0 - ADAM_B1 ** ADAM_STEP)
    v_hat = v2 / (1.0 - ADAM_B2 ** ADAM_STEP)
    delta = -ADAM_LR * (m_hat / (jnp.sqrt(v_hat) + ADAM_EPS) + ADAM_WD * w)
    return delta, m2, v2


def _adamw_w_ada(cmat, dmod_cols, w, m, v):
    r, cdim = w.shape
    tr = _row_tile(r)

    def body(c_ref, d_ref, w_ref, m_ref, v_ref, g_ref, dl_ref, m2_ref, v2_ref):
        g = _dot_tn(c_ref[...].astype(BF16), d_ref[...].astype(BF16))
        g_ref[...] = g
        dl_ref[...], m2_ref[...], v2_ref[...] = _adamw(w_ref[...], g, m_ref[...], v_ref[...])

    row = pl.BlockSpec((tr, cdim), lambda i: (i, 0))
    return pl.pallas_call(
        body, name="adamw_w_ada", grid=(r // tr,),
        in_specs=[pl.BlockSpec((N_DEV, tr), lambda i: (0, i)), _full(dmod_cols.shape), row, row, row],
        out_specs=[row] * 4, out_shape=[jax.ShapeDtypeStruct(w.shape, F32)] * 4,
        compiler_params=_params(("parallel",)),
    )(cmat, dmod_cols, w, m, v)


SMALL_ROWS = 16
SMALL_AT = {"b_ada": (0, 6, 0, D_MODEL), "g_pre_mix": (6, 1, 0, D_MODEL), "g_post_mix": (7, 1, 0, D_MODEL),
            "g_pre_ffn": (8, 1, 0, D_MODEL), "g_post_ffn": (9, 1, 0, D_MODEL), "conv_b": (10, 1, 0, D_MODEL),
            "mlstm_norm_w": (11, 1, 0, 512), "b_if": (11, 1, 512, LANES), "attn_sinks": (11, 1, 640, LANES)}


def _small_table(part):
    tail = jnp.concatenate([part["mlstm_norm_w"], jnp.pad(part["b_if"], (0, LANES - 8)),
                            jnp.pad(part["attn_sinks"], (0, LANES - 8)), jnp.zeros((256,), F32)])
    return jnp.concatenate([part["b_ada"], part["gains"], part["conv_b"][None], tail[None],
                            jnp.full((1, D_MODEL), part["loss"], F32),
                            jnp.zeros((SMALL_ROWS - 13, D_MODEL), F32)], axis=0)


LOSS_ROW = 12


def _adamw_small(gathered, wmv):
    names = list(SMALL_AT)

    def body(*refs):
        g_ref, ins, outs = refs[0], refs[1:1 + 3 * len(names)], refs[1 + 3 * len(names):]
        g = g_ref[0]
        for k in range(1, N_DEV):
            g = g + g_ref[k]
        outs[4 * len(names)][...] = g[LOSS_ROW:LOSS_ROW + 1, 0:LANES]
        for i, n in enumerate(names):
            r0, rows, l0, lanes = SMALL_AT[n]
            gi = jnp.concatenate([g[r:r + 1, l0:l0 + lanes] for r in range(r0, r0 + rows)], axis=1)
            w_ref, m_ref, v_ref = ins[3 * i:3 * i + 3]
            go, dl, m2, v2 = outs[4 * i:4 * i + 4]
            go[...] = gi
            dl[...], m2[...], v2[...] = _adamw(w_ref[...], gi, m_ref[...], v_ref[...])

    flat = [a for n in names for a in wmv[n]]
    res = pl.pallas_call(
        body, name="adamw_small",
        out_shape=[jax.ShapeDtypeStruct(wmv[n][0].shape, F32) for n in names for _ in range(4)]
        + [jax.ShapeDtypeStruct((1, LANES), F32)],
        compiler_params=_params(),
    )(gathered, *flat)
    out = {n: res[4 * i:4 * i + 4] for i, n in enumerate(names)}
    out["loss"] = res[4 * len(names)]
    return out


def _adamw_sum_many(items, name):
    n = len(items)

    def body(*refs):
        ins, outs = refs[:4 * n], refs[4 * n:]
        for i in range(n):
            r_ref, w_ref, m_ref, v_ref = ins[4 * i:4 * i + 4]
            go, dl, m2, v2 = outs[4 * i:4 * i + 4]
            g = _sum_partials(r_ref)
            go[...] = g
            dl[...], m2[...], v2[...] = _adamw(w_ref[...], g, m_ref[...], v_ref[...])

    res = pl.pallas_call(
        body, name=name, out_shape=[jax.ShapeDtypeStruct(it[1].shape, F32) for it in items for _ in range(4)],
        compiler_params=_params(),
    )(*[a for it in items for a in it])
    return [res[4 * i:4 * i + 4] for i in range(n)]


def _row_tile(rows):
    return rows // 4 if rows >= 512 and rows % 64 == 0 else rows


def _sum_partials(r_ref):
    g = r_ref[0].astype(F32)
    for k in range(1, N_DEV):
        g = g + r_ref[k].astype(F32)
    return g


def _adamw_sum(recv, w, m, v, name):
    r, cdim = w.shape
    tr = _row_tile(r)

    def body(r_ref, w_ref, m_ref, v_ref, g_ref, dl_ref, m2_ref, v2_ref):
        g = _sum_partials(r_ref)
        g_ref[...] = g
        dl_ref[...], m2_ref[...], v2_ref[...] = _adamw(w_ref[...], g, m_ref[...], v_ref[...])

    row = pl.BlockSpec((tr, cdim), lambda i: (i, 0))
    return pl.pallas_call(
        body, name=name, grid=(r // tr,),
        in_specs=[pl.BlockSpec((N_DEV, tr, cdim), lambda i: (0, i, 0)), row, row, row],
        out_specs=[row] * 4, out_shape=[jax.ShapeDtypeStruct((r, cdim), F32)] * 4,
        compiler_params=_params(("parallel",)),
    )(recv, w, m, v)


def _sum8(recv, name):
    _, r, cdim = recv.shape
    tr = _row_tile(r)

    def body(r_ref, g_ref):
        g_ref[...] = _sum_partials(r_ref)

    return pl.pallas_call(
        body, name=name, grid=(r // tr,),
        in_specs=[pl.BlockSpec((N_DEV, tr, cdim), lambda i: (0, i, 0))],
        out_specs=pl.BlockSpec((tr, cdim), lambda i: (i, 0)), out_shape=jax.ShapeDtypeStruct((r, cdim), F32),
        compiler_params=_params(("parallel",)),
    )(recv)


def _adamw_plain(g, w, m, v, name):
    r, cdim = w.shape
    tr = _row_tile(r)

    def body(g_ref, w_ref, m_ref, v_ref, dl_ref, m2_ref, v2_ref):
        dl_ref[...], m2_ref[...], v2_ref[...] = _adamw(w_ref[...], g_ref[...], m_ref[...], v_ref[...])

    row = pl.BlockSpec((tr, cdim), lambda i: (i, 0))
    return pl.pallas_call(
        body, name=name, grid=(r // tr,), in_specs=[row] * 4, out_specs=[row] * 3,
        out_shape=[jax.ShapeDtypeStruct((r, cdim), F32)] * 3,
        compiler_params=_params(("parallel",)),
    )(g, w, m, v)


IN_SHARD = 609
IN_SHARD_PAD = 640
IF_AT = A_W + M_W


def _regrouped(u):
    return u if u < IF_AT + 8 else u + (IF_W - 8)


def _selection(k, rows, row0, transpose):
    shape = (rows, IN_SHARD_PAD) if transpose else (IN_SHARD_PAD, rows)
    l = lax.broadcasted_iota(jnp.int32, shape, 1 if transpose else 0)
    r = lax.broadcasted_iota(jnp.int32, shape, 0 if transpose else 1) + row0
    u = l + IN_SHARD * k
    ru = u + jnp.where(u >= IF_AT + 8, IF_W - 8, 0)
    return ((ru == r) & (l < IN_SHARD)).astype(BF16)


def _regroup_w_in(g):
    def body(g_ref, o_ref):
        for cb in range(CAT_W // LANES):
            r0 = cb * LANES
            acc = jnp.zeros((D_MODEL, LANES), F32)
            for k in range(N_DEV):
                src = [l for l in range(IN_SHARD) if r0 <= _regrouped(IN_SHARD * k + l) < r0 + LANES]
                if src:
                    a0 = src[0] // LANES * LANES
                    a1 = min(a0 + 2 * LANES, IN_SHARD_PAD)
                    acc = acc + _dot(g_ref[k, :, a0:a1], _selection(k, LANES, r0, False)[a0:a1])
            o_ref[:, r0:r0 + LANES] = acc.astype(BF16)

    return pl.pallas_call(
        body, name="regroup_w_in", out_shape=jax.ShapeDtypeStruct((D_MODEL, CAT_W), BF16),
        compiler_params=_params(),
    )(g)


def _ungroup_w_in(g_parts):
    n = len(g_parts)

    def body(*refs):
        o_ref, g_ref = refs[n], refs[n + 1]
        at = 0
        for p in refs[:n]:
            g_ref[:, at:at + p.shape[1]] = p[...]
            at += p.shape[1]
        for k in range(N_DEV):
            for cb in range(IN_SHARD_PAD // LANES):
                cols = slice(cb * LANES, (cb + 1) * LANES)
                lo = _regrouped(IN_SHARD * k + cb * LANES)
                hi = _regrouped(IN_SHARD * k + min((cb + 1) * LANES, IN_SHARD) - 1)
                w0, w1 = lo // LANES * LANES, (hi // LANES + 1) * LANES
                o_ref[k, :, cols] = _dot(g_ref[:, w0:w1], _selection(k, w1 - w0, w0, True)[:, cols]).astype(BF16)

    return pl.pallas_call(
        body, name="ungroup_w_in", out_shape=jax.ShapeDtypeStruct((N_DEV, D_MODEL, IN_SHARD_PAD), BF16),
        scratch_shapes=[pltpu.VMEM((D_MODEL, CAT_W), BF16)], compiler_params=_params(),
    )(*g_parts)


WEIGHT_NAMES = ("w_ada", "b_ada", "g_pre_mix", "g_post_mix", "w_in", "b_if", "conv_w", "conv_b", "attn_sinks",
                "mlstm_norm_w", "w_branch_attn", "w_branch_mlstm", "w_out", "g_pre_ffn", "g_post_ffn",
                "w_ffn_gate", "w_ffn_up", "w_ffn_down")


def kernel(x, c, positions, w_ada, b_ada, g_pre_mix, g_post_mix, w_in, b_if, conv_w, conv_b, attn_sinks, mlstm_norm_w, w_branch_attn, w_branch_mlstm, w_out, g_pre_ffn, g_post_ffn, w_ffn_gate, w_ffn_up, w_ffn_down, loss_target, m_w_ada, m_b_ada, m_g_pre_mix, m_g_post_mix, m_w_in, m_b_if, m_conv_w, m_conv_b, m_attn_sinks, m_mlstm_norm_w, m_w_branch_attn, m_w_branch_mlstm, m_w_out, m_g_pre_ffn, m_g_post_ffn, m_w_ffn_gate, m_w_ffn_up, m_w_ffn_down, v_w_ada, v_b_ada, v_g_pre_mix, v_g_post_mix, v_w_in, v_b_if, v_conv_w, v_conv_b, v_attn_sinks, v_mlstm_norm_w, v_w_branch_attn, v_w_branch_mlstm, v_w_out, v_g_pre_ffn, v_g_post_ffn, v_w_ffn_gate, v_w_ffn_up, v_w_ffn_down):
    given = dict(locals())
    W = {n: given[n][0] for n in WEIGHT_NAMES}
    M = {n: given["m_" + n][0] for n in WEIGHT_NAMES}
    V = {n: given["v_" + n][0] for n in WEIGHT_NAMES}
    me = 4 * lax.axis_index("x") + 2 * lax.axis_index("y") + lax.axis_index("c")

    ff_sh = D_FF // N_DEV
    ada_w = D_MODEL * 6 // N_DEV
    b_cols = lax.dynamic_slice(W["b_ada"], (me * ada_w,), (ada_w,)).reshape(1, ada_w)
    (g_in, g_conv), cg, mod_recv = _gather_front(
        [jnp.pad(W["w_in"], ((0, 0), (0, IN_SHARD_PAD - IN_SHARD))).astype(BF16), jnp.pad(W["conv_w"], ((0, 4), (0, 0)))],
        c.reshape(8, D_MODEL // 8), W["w_ada"], b_cols)
    cmat = cg.reshape(N_DEV, D_MODEL)
    mod = mod_recv[:, 0, :].reshape(6, D_MODEL)

    st_b = _exchange_start([W["w_branch_attn"].astype(BF16), W["w_branch_mlstm"].astype(BF16), W["w_out"].astype(BF16),
                            W["w_ffn_gate"].T.astype(BF16), W["w_ffn_up"].T.astype(BF16), W["w_ffn_down"].astype(BF16)],
                           True, mod_recv, "gather_rest_start")
    cols = lambda g: g.transpose(1, 0, 2).reshape(g.shape[1], N_DEV * g.shape[2])
    gains = jnp.stack([W["g_pre_mix"], W["g_post_mix"], W["g_pre_ffn"], W["g_post_ffn"]])
    xs, tgt = x[0], loss_target[0]
    t = _tables(mod, gains, cols(g_conv)[0:4], W["conv_b"], W["b_if"], W["mlstm_norm_w"], positions[0])
    t["after"] = st_b["token"]
    w_cat = _regroup_w_in(g_in)
    a = _mixer_fwd(xs, t, W["attn_sinks"], w_cat)
    g_ba, g_bm, g_out, g_gate, g_up, g_down = _exchange_wait(st_b, a["ym"], "gather_rest_wait")
    w_ba, w_bm, w_out = cols(g_ba), cols(g_bm), g_out.reshape(D_MODEL, D_MODEL)
    b = _ffn_part(xs, tgt, t, a, w_ba, w_bm, w_out, g_gate.reshape(D_FF, D_MODEL), g_up.reshape(D_FF, D_MODEL),
                  g_down.reshape(D_FF, D_MODEL))

    st_f = _exchange_start([b["g_w_gate"].reshape(N_DEV, ff_sh, D_MODEL), b["g_w_up"].reshape(N_DEV, ff_sh, D_MODEL),
                            b["g_w_down"].reshape(N_DEV, ff_sh, D_MODEL)], False, b["dx1"], "scatter_ffn_start")
    t["after"] = st_f["token"]
    cm = _mixer_bwd(b["dx1"], t, a, b, W["attn_sinks"], w_ba, w_bm, w_out)
    pieces = lambda g, n: g.reshape(g.shape[0], N_DEV, n).transpose(1, 0, 2)
    st_m = _exchange_start([_ungroup_w_in(cm["g_w_cat"]), pieces(cm["g_w_ba"], 128), pieces(cm["g_w_bm"], 128),
                            cm["g_w_out"].reshape(N_DEV, D_MODEL // N_DEV, D_MODEL),
                            jnp.pad(pieces(cm["dcw"][0:4], 128), ((0, 0), (0, 4), (0, 0)))], False, cm["dcw"],
                           "scatter_mixer_start")
    r_gate, r_up, r_down = _exchange_wait(st_f, st_m["token"], "scatter_ffn_wait")
    grad_x, acc_p = _pre_bwd(cm["dproj"], xs, b["dx1"], t["vecs"], w_cat, st_m["token"])
    small = _small_grads(acc_p, b, cm)
    loss = b["loss"]

    big_out = [{} for _ in range(4)]

    def put(n, res):
        for k in range(4):
            big_out[k][n] = res[k][None]

    put("w_ffn_down", _adamw_sum(r_down, W["w_ffn_down"], M["w_ffn_down"], V["w_ffn_down"], "adamw_w_ffn_down"))
    for n, r in (("w_ffn_gate", r_gate), ("w_ffn_up", r_up)):
        put(n, [o.T for o in _adamw_sum(r, W[n].T, M[n].T, V[n].T, "adamw_" + n)])

    sg = _small_exchange(_small_table({"b_ada": small["dmod"], "gains": small["dgains"], "conv_b": small["dconv_b"],
                                       "mlstm_norm_w": small["dnorm_w"], "b_if": small["db_if"],
                                       "attn_sinks": small["dsinks"], "loss": loss}), True, "gather_small")
    as_row = lambda a, n: jnp.pad(a, (0, SMALL_AT[n][3] * SMALL_AT[n][1] - a.shape[0]))[None]
    small_res = _adamw_small(sg, {n: [as_row(d[n], n) for d in (W, M, V)] for n in SMALL_AT})
    small_out = [{n: small_res[n][k][:, 0:W[n].shape[0]] for n in SMALL_AT} for k in range(4)]
    dmod_cols = lax.dynamic_slice(sg[:, 0:6, :].reshape(N_DEV, 6 * D_MODEL), (0, me * ada_w), (N_DEV, ada_w))
    ada_out = _adamw_w_ada(cmat, dmod_cols, W["w_ada"], M["w_ada"], V["w_ada"])

    r_in, r_ba, r_bm, r_out, r_conv = _exchange_wait(st_m, ada_out[1], "scatter_mixer_wait")
    pad4 = lambda v: jnp.pad(v, ((0, 4), (0, 0)))
    res_ba, res_bm, res_out, res_conv = _adamw_sum_many(
        [(r, W[n], M[n], V[n]) for n, r in (("w_branch_attn", r_ba), ("w_branch_mlstm", r_bm), ("w_out", r_out))]
        + [(r_conv, pad4(W["conv_w"]), pad4(M["conv_w"]), pad4(V["conv_w"]))], "adamw_mixer_small")
    put("w_branch_attn", res_ba)
    put("w_branch_mlstm", res_bm)
    put("w_out", res_out)
    put("conv_w", [o[0:4] for o in res_conv])
    g = _sum8(r_in, "sum_w_in")[:, 0:IN_SHARD].T
    put("w_in", [o.T for o in [g] + list(_adamw_plain(g, W["w_in"].T, M["w_in"].T, V["w_in"].T, "adamw_w_in"))])

    outs = [small_res["loss"][0, 0], grad_x[None]]
    for k in range(4):
        for n in WEIGHT_NAMES:
            if n == "w_ada":
                outs.append(ada_out[k][None])
            elif n in big_out[k]:
                outs.append(big_out[k][n])
            else:
                outs.append(small_out[k][n])
    return tuple(outs)
```
